```python
import jax, jax.numpy as jnp
from jax import lax
import numpy as np

D_MODEL = 1024
BATCH = 8
SEQ = 2048
DEPTH = 2

EPS = 1e-6
N_EVEN = (DEPTH + 1) // 2
N_ODD = DEPTH // 2

A_HEADS = 8
A_HEAD_DIM = 64
A_WIDTH = A_HEADS * A_HEAD_DIM
CONV_WIDTH = 3
POOL_WINDOWS = (2, 4, 8, 16)
B_GROUPS = len(POOL_WINDOWS)
B_GROUP_DIM = 128
B_WIDTH = B_GROUPS * B_GROUP_DIM
AB_IN = 3 * A_WIDTH + B_WIDTH
AB_OUT = A_WIDTH + B_WIDTH

C_HEADS = 8
C_NOPE = 64
C_ROPE = 32
C_V = 64
C_Q_RANK = 256
C_KV_RANK = 128
C_WIDTH = C_HEADS * C_V
ROPE_THETA = 10000.0
ATTN_BLOCK = 128
D_GROUPS = 4
D_GROUP_DIM = 128
D_WIDTH = D_GROUPS * D_GROUP_DIM
D_CHUNK = 128
CD_IN = C_Q_RANK + C_KV_RANK + C_ROPE + 2 * D_WIDTH
CD_OUT = C_WIDTH + D_WIDTH

D_FF = 2816
N_MOD = 6

kernel_name = "hybrid_conv_pool_mla_gmlp_block"


def rms_norm(x, g):
    xf = x.astype(jnp.float32)
    y = xf * lax.rsqrt(jnp.mean(xf * xf, axis=-1, keepdims=True) + EPS)
    return (y * g.astype(jnp.float32)).astype(x.dtype)


def layer_norm(x, g, b):
    xf = x.astype(jnp.float32)
    mu = jnp.mean(xf, axis=-1, keepdims=True)
    xc = xf - mu
    y = xc * lax.rsqrt(jnp.mean(xc * xc, axis=-1, keepdims=True) + EPS)
    return (y * g.astype(jnp.float32) + b.astype(jnp.float32)).astype(x.dtype)


def causal_dwconv(x, w):
    K, C = w.shape
    return lax.conv_general_dilated(
        x, w[:, None, :].astype(x.dtype), window_strides=(1,),
        padding=((K - 1, 0),), dimension_numbers=("NWC", "WIO", "NWC"),
        feature_group_count=C)


def rope_tables(positions):
    half = C_ROPE // 2
    inv_freq = ROPE_THETA ** (-jnp.arange(half, dtype=jnp.float32) / half)
    ang = positions.astype(jnp.float32)[..., None] * inv_freq
    return jnp.cos(ang), jnp.sin(ang)


def apply_rope(x, cos, sin):
    if x.ndim == 4:
        cos, sin = cos[:, :, None, :], sin[:, :, None, :]
    xf = x.astype(jnp.float32)
    x1, x2 = jnp.split(xf, 2, axis=-1)
    return jnp.concatenate([x1 * cos - x2 * sin, x2 * cos + x1 * sin], axis=-1).astype(x.dtype)


def short_gated_conv(b_gate, c_gate, h, conv_w):
    return b_gate * causal_dwconv(c_gate * h, conv_w)


def multiscale_pool(p, mix_w, scale):
    Bn, S, _ = p.shape
    pg = p.reshape(Bn, S, B_GROUPS, B_GROUP_DIM)
    cs = jnp.cumsum(pg.astype(jnp.float32), axis=1)
    t = jnp.arange(S)
    pooled = []
    for g, w in enumerate(POOL_WINDOWS):
        csg = cs[:, :, g]
        lag = jnp.pad(csg[:, :S - w], ((0, 0), (w, 0), (0, 0)))
        cnt = jnp.minimum(t + 1, w).astype(jnp.float32)[None, :, None]
        pooled.append((csg - lag) / cnt)
    pooled = jnp.stack(pooled, axis=2).astype(p.dtype) - pg
    y = jnp.einsum("bsgc,gcd->bsgd", pooled, mix_w)
    return y.reshape(Bn, S, B_WIDTH) * scale


def latent_attention(q_lat, kv_lat, k_pe, q_norm_g, w_uq, kv_norm_g, w_ukv, cos, sin):
    Bn, S, _ = q_lat.shape
    q = (rms_norm(q_lat, q_norm_g) @ w_uq).reshape(Bn, S, C_HEADS, C_NOPE + C_ROPE)
    q_nope, q_pe = q[..., :C_NOPE], apply_rope(q[..., C_NOPE:], cos, sin)
    kv = (rms_norm(kv_lat, kv_norm_g) @ w_ukv).reshape(Bn, S, C_HEADS, C_NOPE + C_V)
    k_nope, v = kv[..., :C_NOPE], kv[..., C_NOPE:]
    k_pe = apply_rope(k_pe, cos, sin)
    scale = (C_NOPE + C_ROPE) ** -0.5
    nb = S // ATTN_BLOCK
    qn_blocks = q_nope.reshape(Bn, nb, ATTN_BLOCK, C_HEADS, C_NOPE).transpose(1, 0, 2, 3, 4)
    qp_blocks = q_pe.reshape(Bn, nb, ATTN_BLOCK, C_HEADS, C_ROPE).transpose(1, 0, 2, 3, 4)
    key_pos = jnp.arange(S)

    def one_block(args):
        i, qn, qp = args
        s = (jnp.einsum("bqhd,bkhd->bhqk", qn, k_nope, preferred_element_type=jnp.float32)
             + jnp.einsum("bqhr,bkr->bhqk", qp, k_pe, preferred_element_type=jnp.float32)) * scale
        q_pos = i * ATTN_BLOCK + jnp.arange(ATTN_BLOCK)
        s = jnp.where(key_pos[None, :] <= q_pos[:, None], s, -jnp.inf)
        prob = jax.nn.softmax(s, axis=-1).astype(v.dtype)
        return jnp.einsum("bhqk,bkhd->bqhd", prob, v)

    out = lax.map(one_block, (jnp.arange(nb), qn_blocks, qp_blocks))
    return out.transpose(1, 0, 2, 3, 4).reshape(Bn, S, C_WIDTH)


def spatial_gating(u, v, ln_g, ln_b, w_s, b_s):
    Bn, S, _ = v.shape
    v = layer_norm(v, ln_g, ln_b)
    vc = v.reshape(Bn, S // D_CHUNK, D_CHUNK, D_GROUPS, D_GROUP_DIM)
    mask = jnp.tril(jnp.ones((D_CHUNK, D_CHUNK), dtype=bool))
    w = jnp.where(mask[None], w_s, 0)
    mixed = jnp.einsum("gts,bnsgc->bntgc", w, vc) + b_s.T[None, None, :, :, None]
    return u * mixed.reshape(Bn, S, D_WIDTH)


def conv_ffn(h, w_up, conv_w, w_down):
    z = causal_dwconv(h @ w_up, conv_w)
    g, u = jnp.split(z, 2, axis=-1)
    return (jax.nn.silu(g) * u) @ w_down


def _fwd_setup_inputs(seed: int = 0) -> dict:
    key = jax.random.key(seed)
    ks = iter(jax.random.split(key, 32))
    nrm = lambda shape, s: jax.random.normal(next(ks), shape, jnp.float32) * s
    gain = lambda shape: 1.0 + nrm(shape, 0.05)
    D = D_MODEL
    offsets = jax.random.randint(next(ks), (BATCH, 1), 0, 4096, dtype=jnp.int32)
    positions = (jnp.arange(SEQ, dtype=jnp.int32)[None, :] + offsets).astype(jnp.int32)
    return {
        "x": nrm((BATCH, SEQ, D), 1.0),
        "c": nrm((BATCH, D), 1.0),
        "positions": positions,
        "ada_w": nrm((DEPTH, D, N_MOD * D), 0.5 * D ** -0.5),
        "ada_b": nrm((DEPTH, N_MOD * D), 0.02),
        "norm1_g": gain((DEPTH, D)),
        "norm2_g": gain((DEPTH, D)),
        "ab_w_in": nrm((N_EVEN, D, AB_IN), D ** -0.5),
        "a_conv_w": nrm((N_EVEN, CONV_WIDTH, A_WIDTH), CONV_WIDTH ** -0.5),
        "b_mix_w": nrm((N_EVEN, B_GROUPS, B_GROUP_DIM, B_GROUP_DIM), B_GROUP_DIM ** -0.5),
        "b_scale": 1.0 + nrm((N_EVEN, B_WIDTH), 0.1),
        "ab_w_out": nrm((N_EVEN, AB_OUT, D), AB_OUT ** -0.5),
        "cd_w_in": nrm((N_ODD, D, CD_IN), D ** -0.5),
        "c_q_norm_g": gain((N_ODD, C_Q_RANK)),
        "c_w_uq": nrm((N_ODD, C_Q_RANK, C_HEADS * (C_NOPE + C_ROPE)), C_Q_RANK ** -0.5),
        "c_kv_norm_g": gain((N_ODD, C_KV_RANK)),
        "c_w_ukv": nrm((N_ODD, C_KV_RANK, C_HEADS * (C_NOPE + C_V)), C_KV_RANK ** -0.5),
        "d_ln_g": gain((N_ODD, D_WIDTH)),
        "d_ln_b": nrm((N_ODD, D_WIDTH), 0.02),
        "d_w_s": nrm((N_ODD, D_GROUPS, D_CHUNK, D_CHUNK), 0.5 * D_CHUNK ** -0.5),
        "d_b_s": 1.0 + nrm((N_ODD, D_GROUPS, D_CHUNK), 0.02),
        "cd_w_out": nrm((N_ODD, CD_OUT, D), CD_OUT ** -0.5),
        "ffn_w_up": nrm((DEPTH, D, 2 * D_FF), D ** -0.5),
        "ffn_conv_w": nrm((DEPTH, CONV_WIDTH, 2 * D_FF), CONV_WIDTH ** -0.5),
        "ffn_w_down": nrm((DEPTH, D_FF, D), D_FF ** -0.5),
        "final_norm_g": gain((D,)),
    }


def _fwd_reference(x, c, positions, ada_w, ada_b, norm1_g, norm2_g,
              ab_w_in, a_conv_w, b_mix_w, b_scale, ab_w_out,
              cd_w_in, c_q_norm_g, c_w_uq, c_kv_norm_g, c_w_ukv,
              d_ln_g, d_ln_b, d_w_s, d_b_s, cd_w_out,
              ffn_w_up, ffn_conv_w, ffn_w_down, final_norm_g):
    cos, sin = rope_tables(positions)
    c_act = jax.nn.silu(c)
    for l in range(DEPTH):
        mod = c_act @ ada_w[l] + ada_b[l]
        sh1, sc1, g1, sh2, sc2, g2 = [m[:, None, :] for m in jnp.split(mod, N_MOD, axis=-1)]
        h = rms_norm(x, norm1_g[l]) * (1 + sc1) + sh1
        i = l // 2
        if l % 2 == 0:
            z = h @ ab_w_in[i]
            b_gate, c_gate, a_in, p = jnp.split(z, [A_WIDTH, 2 * A_WIDTH, 3 * A_WIDTH], axis=-1)
            y_a = short_gated_conv(b_gate, c_gate, a_in, a_conv_w[i])
            y_b = multiscale_pool(p, b_mix_w[i], b_scale[i])
            y = jnp.concatenate([y_a, y_b], axis=-1) @ ab_w_out[i]
        else:
            z = h @ cd_w_in[i]
            q_lat, kv_lat, k_pe, uv = jnp.split(
                z, [C_Q_RANK, C_Q_RANK + C_KV_RANK, C_Q_RANK + C_KV_RANK + C_ROPE], axis=-1)
            y_c = latent_attention(q_lat, kv_lat, k_pe, c_q_norm_g[i], c_w_uq[i],
                                   c_kv_norm_g[i], c_w_ukv[i], cos, sin)
            u, v = jnp.split(jax.nn.gelu(uv), 2, axis=-1)
            y_d = spatial_gating(u, v, d_ln_g[i], d_ln_b[i], d_w_s[i], d_b_s[i])
            y = jnp.concatenate([y_c, y_d], axis=-1) @ cd_w_out[i]
        x = x + g1 * y
        h = rms_norm(x, norm2_g[l]) * (1 + sc2) + sh2
        x = x + g2 * conv_ffn(h, ffn_w_up[l], ffn_conv_w[l], ffn_w_down[l])
    return rms_norm(x, final_norm_g)


import jax as _jax
import jax.numpy as _jnp

TWIN_FORMAT = 'train_step'
FWD_PARAMS = ['x', 'c', 'positions', 'ada_w', 'ada_b', 'norm1_g', 'norm2_g', 'ab_w_in', 'a_conv_w', 'b_mix_w', 'b_scale', 'ab_w_out', 'cd_w_in', 'c_q_norm_g', 'c_w_uq', 'c_kv_norm_g', 'c_w_ukv', 'd_ln_g', 'd_ln_b', 'd_w_s', 'd_b_s', 'cd_w_out', 'ffn_w_up', 'ffn_conv_w', 'ffn_w_down', 'final_norm_g']
TWIN_WEIGHTS = ['ada_w', 'ada_b', 'norm1_g', 'norm2_g', 'ab_w_in', 'a_conv_w', 'b_mix_w', 'b_scale', 'ab_w_out', 'cd_w_in', 'c_q_norm_g', 'c_w_uq', 'c_kv_norm_g', 'c_w_ukv', 'd_ln_g', 'd_ln_b', 'd_w_s', 'd_b_s', 'cd_w_out', 'ffn_w_up', 'ffn_conv_w', 'ffn_w_down', 'final_norm_g']
TWIN_DIFF_INPUT = 'x'
TWIN_INPUTS = ['x', 'c', 'positions', 'ada_w', 'ada_b', 'norm1_g', 'norm2_g', 'ab_w_in', 'a_conv_w', 'b_mix_w', 'b_scale', 'ab_w_out', 'cd_w_in', 'c_q_norm_g', 'c_w_uq', 'c_kv_norm_g', 'c_w_ukv', 'd_ln_g', 'd_ln_b', 'd_w_s', 'd_b_s', 'cd_w_out', 'ffn_w_up', 'ffn_conv_w', 'ffn_w_down', 'final_norm_g', 'loss_target', 'm_ada_w', 'm_ada_b', 'm_norm1_g', 'm_norm2_g', 'm_ab_w_in', 'm_a_conv_w', 'm_b_mix_w', 'm_b_scale', 'm_ab_w_out', 'm_cd_w_in', 'm_c_q_norm_g', 'm_c_w_uq', 'm_c_kv_norm_g', 'm_c_w_ukv', 'm_d_ln_g', 'm_d_ln_b', 'm_d_w_s', 'm_d_b_s', 'm_cd_w_out', 'm_ffn_w_up', 'm_ffn_conv_w', 'm_ffn_w_down', 'm_final_norm_g', 'v_ada_w', 'v_ada_b', 'v_norm1_g', 'v_norm2_g', 'v_ab_w_in', 'v_a_conv_w', 'v_b_mix_w', 'v_b_scale', 'v_ab_w_out', 'v_cd_w_in', 'v_c_q_norm_g', 'v_c_w_uq', 'v_c_kv_norm_g', 'v_c_w_ukv', 'v_d_ln_g', 'v_d_ln_b', 'v_d_w_s', 'v_d_b_s', 'v_cd_w_out', 'v_ffn_w_up', 'v_ffn_conv_w', 'v_ffn_w_down', 'v_final_norm_g']
TWIN_OUTPUTS = ['loss', 'grad_x', 'grad_ada_w', 'grad_ada_b', 'grad_norm1_g', 'grad_norm2_g', 'grad_ab_w_in', 'grad_a_conv_w', 'grad_b_mix_w', 'grad_b_scale', 'grad_ab_w_out', 'grad_cd_w_in', 'grad_c_q_norm_g', 'grad_c_w_uq', 'grad_c_kv_norm_g', 'grad_c_w_ukv', 'grad_d_ln_g', 'grad_d_ln_b', 'grad_d_w_s', 'grad_d_b_s', 'grad_cd_w_out', 'grad_ffn_w_up', 'grad_ffn_conv_w', 'grad_ffn_w_down', 'grad_final_norm_g', 'delta_ada_w', 'delta_ada_b', 'delta_norm1_g', 'delta_norm2_g', 'delta_ab_w_in', 'delta_a_conv_w', 'delta_b_mix_w', 'delta_b_scale', 'delta_ab_w_out', 'delta_cd_w_in', 'delta_c_q_norm_g', 'delta_c_w_uq', 'delta_c_kv_norm_g', 'delta_c_w_ukv', 'delta_d_ln_g', 'delta_d_ln_b', 'delta_d_w_s', 'delta_d_b_s', 'delta_cd_w_out', 'delta_ffn_w_up', 'delta_ffn_conv_w', 'delta_ffn_w_down', 'delta_final_norm_g', 'new_m_ada_w', 'new_m_ada_b', 'new_m_norm1_g', 'new_m_norm2_g', 'new_m_ab_w_in', 'new_m_a_conv_w', 'new_m_b_mix_w', 'new_m_b_scale', 'new_m_ab_w_out', 'new_m_cd_w_in', 'new_m_c_q_norm_g', 'new_m_c_w_uq', 'new_m_c_kv_norm_g', 'new_m_c_w_ukv', 'new_m_d_ln_g', 'new_m_d_ln_b', 'new_m_d_w_s', 'new_m_d_b_s', 'new_m_cd_w_out', 'new_m_ffn_w_up', 'new_m_ffn_conv_w', 'new_m_ffn_w_down', 'new_m_final_norm_g', 'new_v_ada_w', 'new_v_ada_b', 'new_v_norm1_g', 'new_v_norm2_g', 'new_v_ab_w_in', 'new_v_a_conv_w', 'new_v_b_mix_w', 'new_v_b_scale', 'new_v_ab_w_out', 'new_v_cd_w_in', 'new_v_c_q_norm_g', 'new_v_c_w_uq', 'new_v_c_kv_norm_g', 'new_v_c_w_ukv', 'new_v_d_ln_g', 'new_v_d_ln_b', 'new_v_d_w_s', 'new_v_d_b_s', 'new_v_cd_w_out', 'new_v_ffn_w_up', 'new_v_ffn_conv_w', 'new_v_ffn_w_down', 'new_v_final_norm_g']
TWIN_LEAF_KINDS = {'loss': 'loss', 'grad_x': 'grad_x', 'grad_ada_w': 'grad_w', 'grad_ada_b': 'grad_w', 'grad_norm1_g': 'grad_w', 'grad_norm2_g': 'grad_w', 'grad_ab_w_in': 'grad_w', 'grad_a_conv_w': 'grad_w', 'grad_b_mix_w': 'grad_w', 'grad_b_scale': 'grad_w', 'grad_ab_w_out': 'grad_w', 'grad_cd_w_in': 'grad_w', 'grad_c_q_norm_g': 'grad_w', 'grad_c_w_uq': 'grad_w', 'grad_c_kv_norm_g': 'grad_w', 'grad_c_w_ukv': 'grad_w', 'grad_d_ln_g': 'grad_w', 'grad_d_ln_b': 'grad_w', 'grad_d_w_s': 'grad_w', 'grad_d_b_s': 'grad_w', 'grad_cd_w_out': 'grad_w', 'grad_ffn_w_up': 'grad_w', 'grad_ffn_conv_w': 'grad_w', 'grad_ffn_w_down': 'grad_w', 'grad_final_norm_g': 'grad_w', 'delta_ada_w': 'delta_w', 'delta_ada_b': 'delta_w', 'delta_norm1_g': 'delta_w', 'delta_norm2_g': 'delta_w', 'delta_ab_w_in': 'delta_w', 'delta_a_conv_w': 'delta_w', 'delta_b_mix_w': 'delta_w', 'delta_b_scale': 'delta_w', 'delta_ab_w_out': 'delta_w', 'delta_cd_w_in': 'delta_w', 'delta_c_q_norm_g': 'delta_w', 'delta_c_w_uq': 'delta_w', 'delta_c_kv_norm_g': 'delta_w', 'delta_c_w_ukv': 'delta_w', 'delta_d_ln_g': 'delta_w', 'delta_d_ln_b': 'delta_w', 'delta_d_w_s': 'delta_w', 'delta_d_b_s': 'delta_w', 'delta_cd_w_out': 'delta_w', 'delta_ffn_w_up': 'delta_w', 'delta_ffn_conv_w': 'delta_w', 'delta_ffn_w_down': 'delta_w', 'delta_final_norm_g': 'delta_w', 'new_m_ada_w': 'new_m', 'new_m_ada_b': 'new_m', 'new_m_norm1_g': 'new_m', 'new_m_norm2_g': 'new_m', 'new_m_ab_w_in': 'new_m', 'new_m_a_conv_w': 'new_m', 'new_m_b_mix_w': 'new_m', 'new_m_b_scale': 'new_m', 'new_m_ab_w_out': 'new_m', 'new_m_cd_w_in': 'new_m', 'new_m_c_q_norm_g': 'new_m', 'new_m_c_w_uq': 'new_m', 'new_m_c_kv_norm_g': 'new_m', 'new_m_c_w_ukv': 'new_m', 'new_m_d_ln_g': 'new_m', 'new_m_d_ln_b': 'new_m', 'new_m_d_w_s': 'new_m', 'new_m_d_b_s': 'new_m', 'new_m_cd_w_out': 'new_m', 'new_m_ffn_w_up': 'new_m', 'new_m_ffn_conv_w': 'new_m', 'new_m_ffn_w_down': 'new_m', 'new_m_final_norm_g': 'new_m', 'new_v_ada_w': 'new_v', 'new_v_ada_b': 'new_v', 'new_v_norm1_g': 'new_v', 'new_v_norm2_g': 'new_v', 'new_v_ab_w_in': 'new_v', 'new_v_a_conv_w': 'new_v', 'new_v_b_mix_w': 'new_v', 'new_v_b_scale': 'new_v', 'new_v_ab_w_out': 'new_v', 'new_v_cd_w_in': 'new_v', 'new_v_c_q_norm_g': 'new_v', 'new_v_c_w_uq': 'new_v', 'new_v_c_kv_norm_g': 'new_v', 'new_v_c_w_ukv': 'new_v', 'new_v_d_ln_g': 'new_v', 'new_v_d_ln_b': 'new_v', 'new_v_d_w_s': 'new_v', 'new_v_d_b_s': 'new_v', 'new_v_cd_w_out': 'new_v', 'new_v_ffn_w_up': 'new_v', 'new_v_ffn_conv_w': 'new_v', 'new_v_ffn_w_down': 'new_v', 'new_v_final_norm_g': 'new_v'}


def _forward(args):
    return _fwd_reference(*[args[k] for k in FWD_PARAMS])


def _output_shape():
    out = _jax.eval_shape(lambda: _forward(_fwd_setup_inputs(0)))
    return out.shape, out.dtype

N_MICROBATCH = 1
ADAM_LR = 0.001
ADAM_B1 = 0.9
ADAM_B2 = 0.999
ADAM_EPS = 1e-08
ADAM_WD = 0.01
ADAM_STEP = 10
PER_EXAMPLE_BATCH_AXIS = {'x': 0, 'c': 0, 'positions': 0, 'loss_target': 0}
SHARED_INPUTS = []
_WEIGHT_DTYPES = {'ada_w': _jnp.float32, 'ada_b': _jnp.float32, 'norm1_g': _jnp.float32, 'norm2_g': _jnp.float32, 'ab_w_in': _jnp.float32, 'a_conv_w': _jnp.float32, 'b_mix_w': _jnp.float32, 'b_scale': _jnp.float32, 'ab_w_out': _jnp.float32, 'cd_w_in': _jnp.float32, 'c_q_norm_g': _jnp.float32, 'c_w_uq': _jnp.float32, 'c_kv_norm_g': _jnp.float32, 'c_w_ukv': _jnp.float32, 'd_ln_g': _jnp.float32, 'd_ln_b': _jnp.float32, 'd_w_s': _jnp.float32, 'd_b_s': _jnp.float32, 'cd_w_out': _jnp.float32, 'ffn_w_up': _jnp.float32, 'ffn_conv_w': _jnp.float32, 'ffn_w_down': _jnp.float32, 'final_norm_g': _jnp.float32}
MOMENT_SCALE = {'ada_w': 5.644376e-02, 'ada_b': 9.796634e-02, 'norm1_g': 4.914881e-02, 'norm2_g': 3.620544e-02, 'ab_w_in': 4.546434e-02, 'a_conv_w': 5.064079e-02, 'b_mix_w': 3.583415e-02, 'b_scale': 3.821590e-02, 'ab_w_out': 4.255923e-02, 'cd_w_in': 2.042105e-02, 'c_q_norm_g': 8.904977e-03, 'c_w_uq': 5.252290e-03, 'c_kv_norm_g': 2.887230e-02, 'c_w_ukv': 9.782261e-03, 'd_ln_g': 9.016989e-03, 'd_ln_b': 9.359236e-03, 'd_w_s': 1.860670e-02, 'd_b_s': 2.556277e-02, 'cd_w_out': 2.221157e-02, 'ffn_w_up': 1.617312e-02, 'ffn_conv_w': 1.610389e-02, 'ffn_w_down': 2.642997e-02, 'final_norm_g': 1.606754e+01}


def _to_microbatches(a, axis):
    t = _jnp.moveaxis(a, axis, 0)
    t = t.reshape((N_MICROBATCH, t.shape[0] // N_MICROBATCH) + t.shape[1:])
    return _jnp.moveaxis(t, 1, axis + 1)


def setup_inputs(seed: int = 0) -> dict:
    inp = _fwd_setup_inputs(seed)
    key = _jax.random.fold_in(_jax.random.key(seed), 7919)
    shape, _ = _output_shape()
    out = dict(inp)
    out["loss_target"] = _jax.random.normal(_jax.random.fold_in(key, 0), shape, _jnp.float32)
    for i, name in enumerate(TWIN_WEIGHTS):
        w = inp[name].astype(_jnp.float32)
        if MOMENT_SCALE is None:
            s = _jnp.sqrt(_jnp.mean(_jnp.square(w)) + 1e-30)
        else:
            s = MOMENT_SCALE[name]
        km, kv = _jax.random.split(_jax.random.fold_in(key, i + 1))
        out[name] = w
        out["m_" + name] = s * _jax.random.normal(km, w.shape, _jnp.float32)
        out["v_" + name] = (s * s) * _jax.random.uniform(kv, w.shape, _jnp.float32, 0.5, 1.5)
    if N_MICROBATCH > 1:
        for name, axis in PER_EXAMPLE_BATCH_AXIS.items():
            out[name] = _to_microbatches(out[name], axis)
    return {'x': out['x'], 'c': out['c'], 'positions': out['positions'], 'ada_w': out['ada_w'], 'ada_b': out['ada_b'], 'norm1_g': out['norm1_g'], 'norm2_g': out['norm2_g'], 'ab_w_in': out['ab_w_in'], 'a_conv_w': out['a_conv_w'], 'b_mix_w': out['b_mix_w'], 'b_scale': out['b_scale'], 'ab_w_out': out['ab_w_out'], 'cd_w_in': out['cd_w_in'], 'c_q_norm_g': out['c_q_norm_g'], 'c_w_uq': out['c_w_uq'], 'c_kv_norm_g': out['c_kv_norm_g'], 'c_w_ukv': out['c_w_ukv'], 'd_ln_g': out['d_ln_g'], 'd_ln_b': out['d_ln_b'], 'd_w_s': out['d_w_s'], 'd_b_s': out['d_b_s'], 'cd_w_out': out['cd_w_out'], 'ffn_w_up': out['ffn_w_up'], 'ffn_conv_w': out['ffn_conv_w'], 'ffn_w_down': out['ffn_w_down'], 'final_norm_g': out['final_norm_g'], 'loss_target': out['loss_target'], 'm_ada_w': out['m_ada_w'], 'm_ada_b': out['m_ada_b'], 'm_norm1_g': out['m_norm1_g'], 'm_norm2_g': out['m_norm2_g'], 'm_ab_w_in': out['m_ab_w_in'], 'm_a_conv_w': out['m_a_conv_w'], 'm_b_mix_w': out['m_b_mix_w'], 'm_b_scale': out['m_b_scale'], 'm_ab_w_out': out['m_ab_w_out'], 'm_cd_w_in': out['m_cd_w_in'], 'm_c_q_norm_g': out['m_c_q_norm_g'], 'm_c_w_uq': out['m_c_w_uq'], 'm_c_kv_norm_g': out['m_c_kv_norm_g'], 'm_c_w_ukv': out['m_c_w_ukv'], 'm_d_ln_g': out['m_d_ln_g'], 'm_d_ln_b': out['m_d_ln_b'], 'm_d_w_s': out['m_d_w_s'], 'm_d_b_s': out['m_d_b_s'], 'm_cd_w_out': out['m_cd_w_out'], 'm_ffn_w_up': out['m_ffn_w_up'], 'm_ffn_conv_w': out['m_ffn_conv_w'], 'm_ffn_w_down': out['m_ffn_w_down'], 'm_final_norm_g': out['m_final_norm_g'], 'v_ada_w': out['v_ada_w'], 'v_ada_b': out['v_ada_b'], 'v_norm1_g': out['v_norm1_g'], 'v_norm2_g': out['v_norm2_g'], 'v_ab_w_in': out['v_ab_w_in'], 'v_a_conv_w': out['v_a_conv_w'], 'v_b_mix_w': out['v_b_mix_w'], 'v_b_scale': out['v_b_scale'], 'v_ab_w_out': out['v_ab_w_out'], 'v_cd_w_in': out['v_cd_w_in'], 'v_c_q_norm_g': out['v_c_q_norm_g'], 'v_c_w_uq': out['v_c_w_uq'], 'v_c_kv_norm_g': out['v_c_kv_norm_g'], 'v_c_w_ukv': out['v_c_w_ukv'], 'v_d_ln_g': out['v_d_ln_g'], 'v_d_ln_b': out['v_d_ln_b'], 'v_d_w_s': out['v_d_w_s'], 'v_d_b_s': out['v_d_b_s'], 'v_cd_w_out': out['v_cd_w_out'], 'v_ffn_w_up': out['v_ffn_w_up'], 'v_ffn_conv_w': out['v_ffn_conv_w'], 'v_ffn_w_down': out['v_ffn_w_down'], 'v_final_norm_g': out['v_final_norm_g']}


def _loss(weights, diff, rest, loss_target):
    with _jax.named_scope("forward"):
        args = {**rest, TWIN_DIFF_INPUT: diff, **{k: w.astype(_WEIGHT_DTYPES[k]) for k, w in weights.items()}}
        y = _forward(args)
    with _jax.named_scope("loss_head"):
        err = _jnp.square(y.astype(_jnp.float32) - loss_target)
        return 0.5 * _jnp.sum(_jnp.mean(err, axis=-1)) if err.ndim else 0.5 * err


def _adamw(w, g, m, v):
    m = ADAM_B1 * m + (1.0 - ADAM_B1) * g
    v = ADAM_B2 * v + (1.0 - ADAM_B2) * _jnp.square(g)
    m_hat = m / (1.0 - ADAM_B1 ** ADAM_STEP)
    v_hat = v / (1.0 - ADAM_B2 ** ADAM_STEP)
    delta = -ADAM_LR * (m_hat / (_jnp.sqrt(v_hat) + ADAM_EPS) + ADAM_WD * w)
    return delta, m, v


def reference(x, c, positions, ada_w, ada_b, norm1_g, norm2_g, ab_w_in, a_conv_w, b_mix_w, b_scale, ab_w_out, cd_w_in, c_q_norm_g, c_w_uq, c_kv_norm_g, c_w_ukv, d_ln_g, d_ln_b, d_w_s, d_b_s, cd_w_out, ffn_w_up, ffn_conv_w, ffn_w_down, final_norm_g, loss_target, m_ada_w, m_ada_b, m_norm1_g, m_norm2_g, m_ab_w_in, m_a_conv_w, m_b_mix_w, m_b_scale, m_ab_w_out, m_cd_w_in, m_c_q_norm_g, m_c_w_uq, m_c_kv_norm_g, m_c_w_ukv, m_d_ln_g, m_d_ln_b, m_d_w_s, m_d_b_s, m_cd_w_out, m_ffn_w_up, m_ffn_conv_w, m_ffn_w_down, m_final_norm_g, v_ada_w, v_ada_b, v_norm1_g, v_norm2_g, v_ab_w_in, v_a_conv_w, v_b_mix_w, v_b_scale, v_ab_w_out, v_cd_w_in, v_c_q_norm_g, v_c_w_uq, v_c_kv_norm_g, v_c_w_ukv, v_d_ln_g, v_d_ln_b, v_d_w_s, v_d_b_s, v_cd_w_out, v_ffn_w_up, v_ffn_conv_w, v_ffn_w_down, v_final_norm_g):
    given = dict(x=x, c=c, positions=positions, ada_w=ada_w, ada_b=ada_b, norm1_g=norm1_g, norm2_g=norm2_g, ab_w_in=ab_w_in, a_conv_w=a_conv_w, b_mix_w=b_mix_w, b_scale=b_scale, ab_w_out=ab_w_out, cd_w_in=cd_w_in, c_q_norm_g=c_q_norm_g, c_w_uq=c_w_uq, c_kv_norm_g=c_kv_norm_g, c_w_ukv=c_w_ukv, d_ln_g=d_ln_g, d_ln_b=d_ln_b, d_w_s=d_w_s, d_b_s=d_b_s, cd_w_out=cd_w_out, ffn_w_up=ffn_w_up, ffn_conv_w=ffn_conv_w, ffn_w_down=ffn_w_down, final_norm_g=final_norm_g, loss_target=loss_target, m_ada_w=m_ada_w, m_ada_b=m_ada_b, m_norm1_g=m_norm1_g, m_norm2_g=m_norm2_g, m_ab_w_in=m_ab_w_in, m_a_conv_w=m_a_conv_w, m_b_mix_w=m_b_mix_w, m_b_scale=m_b_scale, m_ab_w_out=m_ab_w_out, m_cd_w_in=m_cd_w_in, m_c_q_norm_g=m_c_q_norm_g, m_c_w_uq=m_c_w_uq, m_c_kv_norm_g=m_c_kv_norm_g, m_c_w_ukv=m_c_w_ukv, m_d_ln_g=m_d_ln_g, m_d_ln_b=m_d_ln_b, m_d_w_s=m_d_w_s, m_d_b_s=m_d_b_s, m_cd_w_out=m_cd_w_out, m_ffn_w_up=m_ffn_w_up, m_ffn_conv_w=m_ffn_conv_w, m_ffn_w_down=m_ffn_w_down, m_final_norm_g=m_final_norm_g, v_ada_w=v_ada_w, v_ada_b=v_ada_b, v_norm1_g=v_norm1_g, v_norm2_g=v_norm2_g, v_ab_w_in=v_ab_w_in, v_a_conv_w=v_a_conv_w, v_b_mix_w=v_b_mix_w, v_b_scale=v_b_scale, v_ab_w_out=v_ab_w_out, v_cd_w_in=v_cd_w_in, v_c_q_norm_g=v_c_q_norm_g, v_c_w_uq=v_c_w_uq, v_c_kv_norm_g=v_c_kv_norm_g, v_c_w_ukv=v_c_w_ukv, v_d_ln_g=v_d_ln_g, v_d_ln_b=v_d_ln_b, v_d_w_s=v_d_w_s, v_d_b_s=v_d_b_s, v_cd_w_out=v_cd_w_out, v_ffn_w_up=v_ffn_w_up, v_ffn_conv_w=v_ffn_conv_w, v_ffn_w_down=v_ffn_w_down, v_final_norm_g=v_final_norm_g)
    weights = {n: given[n] for n in TWIN_WEIGHTS}
    shared = {n: given[n] for n in SHARED_INPUTS}
    per_example = {n: given[n] for n in ['x', 'c', 'positions']}
    grad_fn = _jax.value_and_grad(_loss, argnums=(0, 1))

    def one_microbatch(ex, loss_target):
        ex = dict(ex)
        diff = ex.pop(TWIN_DIFF_INPUT)
        return grad_fn(weights, diff, {**shared, **ex}, loss_target)

    if N_MICROBATCH == 1:
        loss, (grad_w, grad_x) = one_microbatch(per_example, given["loss_target"])
    else:
        def body(carry, xs):
            loss_sum, grad_sum = carry
            l_k, (gw_k, gx_k) = one_microbatch(xs[0], xs[1])
            with _jax.named_scope("update"):
                return (loss_sum + l_k, _jax.tree.map(_jnp.add, grad_sum, gw_k)), gx_k

        init = (_jnp.zeros((), _jnp.float32), _jax.tree.map(_jnp.zeros_like, weights))
        (loss, grad_w), grad_x = _jax.lax.scan(body, init, (per_example, given["loss_target"]))
    with _jax.named_scope("update"):
        delta_w, new_m, new_v = {}, {}, {}
        for n in TWIN_WEIGHTS:
            delta_w[n], new_m[n], new_v[n] = _adamw(weights[n], grad_w[n], given["m_" + n], given["v_" + n])
    return (loss, grad_x, *[grad_w[n] for n in TWIN_WEIGHTS], *[delta_w[n] for n in TWIN_WEIGHTS],
            *[new_m[n] for n in TWIN_WEIGHTS], *[new_v[n] for n in TWIN_WEIGHTS])
```

```python
import functools
import math

import jax
import jax.numpy as jnp
from jax import lax
from jax.experimental import pallas as pl
from jax.experimental.pallas import tpu as pltpu

F32 = jnp.float32
BF16 = jnp.bfloat16
_MXU_DTYPE = BF16
_VMEM_LIMIT = 56 * 2 ** 20
N_DEV = 8
EPS = 1e-6
POOL_WINDOWS = (2, 4, 8, 16)
ATTN_SCALE = (64 + 32) ** -0.5
ADAM_LR, ADAM_B1, ADAM_B2, ADAM_EPS, ADAM_WD, ADAM_STEP = 0.001, 0.9, 0.999, 1e-08, 0.01, 10
MESH = pl.DeviceIdType.MESH
ANY = pl.BlockSpec(memory_space=pl.ANY)


def _cp(*sem):
    return pltpu.CompilerParams(dimension_semantics=sem, vmem_limit_bytes=_VMEM_LIMIT)


def _dot(a, b, contract):
    dn = {"nn": (((1,), (0,)), ((), ())), "nt": (((1,), (1,)), ((), ())), "tn": (((0,), (0,)), ((), ()))}[contract]
    return lax.dot_general(a.astype(_MXU_DTYPE), b.astype(_MXU_DTYPE), dn, preferred_element_type=F32)


def _my_position():
    x, y, c = lax.axis_index("x"), lax.axis_index("y"), lax.axis_index("c")
    return x, y, c, 4 * x + 2 * y + c


def _exchange(name, groups, scatter):
    flat = [a for g in groups for a in g]
    n_in, n_grp = len(flat), len(groups)
    out_shapes = []
    for g in groups:
        slab = g[0].shape[1:] if scatter else g[0].shape
        out_shapes.append(jax.ShapeDtypeStruct((N_DEV, len(g)) + tuple(slab), g[0].dtype))

    def body(*refs):
        ins, outs = refs[:n_in], refs[n_in:n_in + n_grp]
        send_sems, recv_sems, local_sems = refs[n_in + n_grp:]
        x, y, c, me = _my_position()
        started = []
        i = 0
        for gi, g in enumerate(groups):
            for l in range(len(g)):
                src = ins[i]
                i += 1
                local = pltpu.make_async_copy(src.at[me] if scatter else src, outs[gi].at[me, l], local_sems.at[gi])
                local.start()
                started.append(local)
                for k in range(1, N_DEV):
                    px = 1 - x if k & 4 else x
                    py = 1 - y if k & 2 else y
                    pc = 1 - c if k & 1 else c
                    peer = 4 * px + 2 * py + pc
                    pltpu.make_async_remote_copy(
                        src_ref=src.at[peer] if scatter else src, dst_ref=outs[gi].at[me, l],
                        send_sem=send_sems.at[gi], recv_sem=recv_sems.at[gi],
                        device_id=(px, py, pc), device_id_type=MESH).start()
        for local in started:
            local.wait()
        for gi in range(n_grp):
            seven = outs[gi].at[pl.ds(0, N_DEV - 1)]
            w = pltpu.make_async_remote_copy(src_ref=seven, dst_ref=seven, send_sem=send_sems.at[gi],
                                             recv_sem=recv_sems.at[gi], device_id=(x, y, c), device_id_type=MESH)
            w.wait_send()
            w.wait_recv()

    return pl.pallas_call(
        body, name=name, out_shape=tuple(out_shapes),
        in_specs=[ANY] * n_in, out_specs=tuple([ANY] * n_grp),
        scratch_shapes=[pltpu.SemaphoreType.DMA((n_grp,)), pltpu.SemaphoreType.DMA((n_grp,)),
                        pltpu.SemaphoreType.DMA((n_grp,))],
        compiler_params=pltpu.CompilerParams(has_side_effects=True),
    )(*flat)


def _pack(arrs):
    flat = jnp.concatenate([a.reshape(-1).astype(F32) for a in arrs])
    n = flat.shape[0]
    rows = -(-n // 1024) * 8
    return jnp.pad(flat, (0, rows * 128 - n)).reshape(rows, 128)


def _unpack(buf, shapes, lead=()):
    flat = buf.reshape(lead + (-1,))
    out, off = [], 0
    for s in shapes:
        n = math.prod(s)
        out.append(flat[..., off:off + n].reshape(lead + tuple(s)))
        off += n
    return out


def _mm(name, a, a_spec, b, b_spec, out_sds, o_spec, grid, contract, nk=1):
    o_blk = tuple(d for d in o_spec.block_shape if d is not None)

    def body(a_ref, b_ref, o_ref, *acc):
        r = _dot(a_ref[...], b_ref[...], contract)
        if nk == 1:
            o_ref[...] = r.astype(o_ref.dtype)
        else:
            k = pl.program_id(len(grid) - 1)

            @pl.when(k == 0)
            def _():
                acc[0][...] = r

            @pl.when(k > 0)
            def _():
                acc[0][...] += r

            @pl.when(k == nk - 1)
            def _():
                o_ref[...] = acc[0][...].astype(o_ref.dtype)

    sem = ("parallel",) * (len(grid) - 1) + (("arbitrary",) if nk > 1 else ("parallel",))
    return pl.pallas_call(
        body, name=name, out_shape=out_sds, grid=grid, in_specs=[a_spec, b_spec], out_specs=o_spec,
        scratch_shapes=[pltpu.VMEM(o_blk, F32)] if nk > 1 else [], compiler_params=_cp(*sem))(a, b)


def _tile(n, want):
    t = min(n, want)
    assert n % t == 0, (n, t)
    return t


def _mm_nn(name, a, b, out_dtype=F32, tm=512, tn=512):
    (M, K), N = a.shape, b.shape[1]
    tm, tn = _tile(M, tm), _tile(N, tn)
    return _mm(name, a, pl.BlockSpec((tm, K), lambda i, j: (i, 0)), b, pl.BlockSpec((K, tn), lambda i, j: (0, j)),
               jax.ShapeDtypeStruct((M, N), out_dtype), pl.BlockSpec((tm, tn), lambda i, j: (i, j)),
               (M // tm, N // tn), "nn")


def _mm_nt(name, a, b, out_dtype=F32, tm=512, tn=512):
    (M, K), N = a.shape, b.shape[0]
    tm, tn = _tile(M, tm), _tile(N, tn)
    return _mm(name, a, pl.BlockSpec((tm, K), lambda i, j: (i, 0)), b, pl.BlockSpec((tn, K), lambda i, j: (j, 0)),
               jax.ShapeDtypeStruct((M, N), out_dtype), pl.BlockSpec((tm, tn), lambda i, j: (i, j)),
               (M // tm, N // tn), "nt")


def _mm_tn(name, a, b, out_dtype=F32, tm=512, tn=512):
    (K, M), N = a.shape, b.shape[1]
    tm, tn = _tile(M, tm), _tile(N, tn)
    return _mm(name, a, pl.BlockSpec((K, tm), lambda i, j: (0, i)), b, pl.BlockSpec((K, tn), lambda i, j: (0, j)),
               jax.ShapeDtypeStruct((M, N), out_dtype), pl.BlockSpec((tm, tn), lambda i, j: (i, j)),
               (M // tm, N // tn), "tn")


def _mm_cols(name, a, w, out_dtype=F32, tm=512):
    (M, K), (J, _, n) = a.shape, w.shape
    tm = _tile(M, tm)
    return _mm(name, a, pl.BlockSpec((tm, K), lambda j, i: (i, 0)), w, pl.BlockSpec((None, K, n), lambda j, i: (j, 0, 0)),
               jax.ShapeDtypeStruct((J, M, n), out_dtype), pl.BlockSpec((None, tm, n), lambda j, i: (j, i, 0)),
               (J, M // tm), "nn")


def _mm_cols_dx(name, d, w, out_dtype=F32, tm=512):
    (J, M, n), K = d.shape, w.shape[1]
    tm = _tile(M, tm)
    return _mm(name, d, pl.BlockSpec((None, tm, n), lambda i, j: (j, i, 0)), w, pl.BlockSpec((None, K, n), lambda i, j: (j, 0, 0)),
               jax.ShapeDtypeStruct((M, K), out_dtype), pl.BlockSpec((tm, K), lambda i, j: (i, 0)),
               (M // tm, J), "nt", nk=J)


def _mm_cols_dw(name, a, d, tk=512):
    (M, K), (J, _, n) = a.shape, d.shape
    tk = _tile(K, tk)
    return _mm(name, a, pl.BlockSpec((M, tk), lambda j, i: (0, i)), d, pl.BlockSpec((None, M, n), lambda j, i: (j, 0, 0)),
               jax.ShapeDtypeStruct((J, K, n), F32), pl.BlockSpec((None, tk, n), lambda j, i: (j, i, 0)),
               (J, K // tk), "tn")


def _mm_rows_resid(name, a, w, resid, gate, tm=512, tn=512):
    (Q, M, k), N = a.shape, w.shape[2]
    tm, tn = _tile(M, tm), _tile(N, tn)

    def body(a_ref, w_ref, r_ref, g_ref, y_ref, x_ref, acc):
        q = pl.program_id(2)
        r = _dot(a_ref[...], w_ref[...], "nn")

        @pl.when(q == 0)
        def _():
            acc[...] = r

        @pl.when(q > 0)
        def _():
            acc[...] += r

        @pl.when(q == Q - 1)
        def _():
            y = acc[...]
            y_ref[...] = y
            x_ref[...] = r_ref[...] + g_ref[...] * y

    return pl.pallas_call(
        body, name=name, grid=(M // tm, N // tn, Q),
        out_shape=(jax.ShapeDtypeStruct((M, N), F32), jax.ShapeDtypeStruct((M, N), F32)),
        in_specs=[pl.BlockSpec((None, tm, k), lambda i, j, q: (q, i, 0)), pl.BlockSpec((None, k, tn), lambda i, j, q: (q, 0, j)),
                  pl.BlockSpec((tm, tn), lambda i, j, q: (i, j)), pl.BlockSpec((1, tn), lambda i, j, q: (0, j))],
        out_specs=(pl.BlockSpec((tm, tn), lambda i, j, q: (i, j)), pl.BlockSpec((tm, tn), lambda i, j, q: (i, j))),
        scratch_shapes=[pltpu.VMEM((tm, tn), F32)], compiler_params=_cp("parallel", "parallel", "arbitrary"))(a, w, resid, gate)


def _mm_rows_dx(name, d, w, out_dtype=F32, tm=512):
    (M, N), (Q, k, _) = d.shape, w.shape
    tm = _tile(M, tm)
    return _mm(name, d, pl.BlockSpec((tm, N), lambda q, i: (i, 0)), w, pl.BlockSpec((None, k, N), lambda q, i: (q, 0, 0)),
               jax.ShapeDtypeStruct((Q, M, k), out_dtype), pl.BlockSpec((None, tm, k), lambda q, i: (q, i, 0)),
               (Q, M // tm), "nt")


def _mm_rows_dw(name, a, d, tn=512):
    (Q, M, k), N = a.shape, d.shape[1]
    tn = _tile(N, tn)
    return _mm(name, a, pl.BlockSpec((None, M, k), lambda q, j: (q, 0, 0)), d, pl.BlockSpec((M, tn), lambda q, j: (0, j)),
               jax.ShapeDtypeStruct((Q, k, N), F32), pl.BlockSpec((None, k, tn), lambda q, j: (q, 0, j)),
               (Q, N // tn), "tn")


def _silu(v):
    return v * jax.nn.sigmoid(v)


def _ada_fwd(c16, ada_w):
    L, D, n = ada_w.shape

    def body(c_ref, w_ref, o_ref):
        o_ref[...] = _dot(_silu(c_ref[...]), w_ref[...], "nn")

    return pl.pallas_call(
        body, name="ada_fwd", grid=(L,), out_shape=jax.ShapeDtypeStruct((L, 16, n), F32),
        in_specs=[pl.BlockSpec((16, D), lambda l: (0, 0)), pl.BlockSpec((None, D, n), lambda l: (l, 0, 0))],
        out_specs=pl.BlockSpec((None, 16, n), lambda l: (l, 0, 0)), compiler_params=_cp("parallel"))(c16, ada_w)


def _ada_bwd(c16, dmod16):
    L, _, n = dmod16.shape
    D = c16.shape[1]

    def body(c_ref, d_ref, o_ref):
        o_ref[...] = _dot(_silu(c_ref[...]), d_ref[...], "tn")

    return pl.pallas_call(
        body, name="ada_bwd", grid=(L,), out_shape=jax.ShapeDtypeStruct((L, D, n), F32),
        in_specs=[pl.BlockSpec((16, D), lambda l: (0, 0)), pl.BlockSpec((None, 16, n), lambda l: (l, 0, 0))],
        out_specs=pl.BlockSpec((None, D, n), lambda l: (l, 0, 0)), compiler_params=_cp("parallel"))(c16, dmod16)


def _row_spec(tr, n):
    return pl.BlockSpec((tr, n), lambda i: (i, 0))


def _vec_spec(n):
    return pl.BlockSpec((1, n), lambda i: (0, 0))


def _rmsmod_fwd(name, x, g, sc, sh, tr=256):
    S, D = x.shape

    def body(x_ref, g_ref, sc_ref, sh_ref, h_ref):
        xv = x_ref[...]
        rstd = lax.rsqrt(jnp.mean(xv * xv, axis=-1, keepdims=True) + EPS)
        y = xv * rstd * g_ref[...]
        h_ref[...] = (y * (1.0 + sc_ref[...]) + sh_ref[...]).astype(h_ref.dtype)

    return pl.pallas_call(
        body, name=name, grid=(S // tr,), out_shape=jax.ShapeDtypeStruct((S, D), _MXU_DTYPE),
        in_specs=[_row_spec(tr, D), _vec_spec(D), _vec_spec(D), _vec_spec(D)], out_specs=_row_spec(tr, D),
        compiler_params=_cp("parallel"))(x, g, sc, sh)


def _acc_rows(ref, val, first):
    s = jnp.sum(val, axis=0, keepdims=True)

    @pl.when(first)
    def _():
        ref[...] = s

    @pl.when(jnp.logical_not(first))
    def _():
        ref[...] += s


def _rmsmod_bwd(name, x, g, sc, dh, dres, tr=256):
    S, D = x.shape

    def body(x_ref, g_ref, sc_ref, dh_ref, dres_ref, dx_ref, dg_ref, dsc_ref, dsh_ref):
        first = pl.program_id(0) == 0
        xv, dh_v, gv = x_ref[...], dh_ref[...], g_ref[...]
        rstd = lax.rsqrt(jnp.mean(xv * xv, axis=-1, keepdims=True) + EPS)
        xhat = xv * rstd
        _acc_rows(dsh_ref, dh_v, first)
        _acc_rows(dsc_ref, dh_v * (xhat * gv), first)
        dyg = dh_v * (1.0 + sc_ref[...])
        _acc_rows(dg_ref, dyg * xhat, first)
        dxhat = dyg * gv
        dx_ref[...] = dres_ref[...] + rstd * (dxhat - xhat * jnp.mean(dxhat * xhat, axis=-1, keepdims=True))

    vec = jax.ShapeDtypeStruct((1, D), F32)
    return pl.pallas_call(
        body, name=name, grid=(S // tr,), out_shape=(jax.ShapeDtypeStruct((S, D), F32), vec, vec, vec),
        in_specs=[_row_spec(tr, D), _vec_spec(D), _vec_spec(D), _row_spec(tr, D), _row_spec(tr, D)],
        out_specs=(_row_spec(tr, D), _vec_spec(D), _vec_spec(D), _vec_spec(D)),
        compiler_params=_cp("arbitrary"))(x, g, sc, dh, dres)


def _loss_head(x, g, target, tr=256):
    S, D = x.shape

    def body(x_ref, g_ref, t_ref, loss_ref, dx_ref, dg_ref):
        first = pl.program_id(0) == 0
        xv, gv = x_ref[...], g_ref[...]
        rstd = lax.rsqrt(jnp.mean(xv * xv, axis=-1, keepdims=True) + EPS)
        xhat = xv * rstd
        err = xhat * gv - t_ref[...]
        part = 0.5 * jnp.sum(jnp.mean(err * err, axis=-1, keepdims=True), axis=0, keepdims=True)

        @pl.when(first)
        def _():
            loss_ref[...] = part

        @pl.when(jnp.logical_not(first))
        def _():
            loss_ref[...] += part

        dout = err * (1.0 / D)
        _acc_rows(dg_ref, dout * xhat, first)
        dxhat = dout * gv
        dx_ref[...] = rstd * (dxhat - xhat * jnp.mean(dxhat * xhat, axis=-1, keepdims=True))

    return pl.pallas_call(
        body, name="loss_head", grid=(S // tr,),
        out_shape=(jax.ShapeDtypeStruct((1, 1), F32), jax.ShapeDtypeStruct((S, D), F32), jax.ShapeDtypeStruct((1, D), F32)),
        in_specs=[_row_spec(tr, D), _vec_spec(D), _row_spec(tr, D)],
        out_specs=(pl.BlockSpec((1, 1), lambda i: (0, 0)), _row_spec(tr, D), _vec_spec(D)),
        compiler_params=_cp("arbitrary"))(x, g, target)


def _gate_bwd(name, dx, y, gate, tr=256):
    S, D = dx.shape

    def body(dx_ref, y_ref, g_ref, dy_ref, dg_ref):
        dxv = dx_ref[...]
        dy_ref[...] = (g_ref[...] * dxv).astype(dy_ref.dtype)
        _acc_rows(dg_ref, dxv * y_ref[...], pl.program_id(0) == 0)

    return pl.pallas_call(
        body, name=name, grid=(S // tr,),
        out_shape=(jax.ShapeDtypeStruct((S, D), _MXU_DTYPE), jax.ShapeDtypeStruct((1, D), F32)),
        in_specs=[_row_spec(tr, D), _row_spec(tr, D), _vec_spec(D)], out_specs=(_row_spec(tr, D), _vec_spec(D)),
        compiler_params=_cp("arbitrary"))(dx, y, gate)


def _shift_down(v, k):
    t = lax.broadcasted_iota(jnp.int32, v.shape, 0)
    return jnp.where(t >= k, pltpu.roll(v, k, axis=0), 0.0)


def _shift_up(v, k):
    n = v.shape[0]
    t = lax.broadcasted_iota(jnp.int32, v.shape, 0)
    return jnp.where(t < n - k, pltpu.roll(v, n - k, axis=0), 0.0)


def _window_sum(p, w, shift):
    s, k = p, 1
    while k < w:
        s = s + shift(s, k)
        k *= 2
    return s


def _pool_count(shape, w):
    t = lax.broadcasted_iota(jnp.int32, shape, 0)
    return jnp.minimum(t + 1, w).astype(F32)


def _ab_specs(S):
    zs = [pl.BlockSpec((None, S, 128), functools.partial(lambda g, q: (2 * q + g // 2, 0, g % 2), q=q)) for q in range(4)]
    return zs


def _ab_mix_fwd(z8, conv_w, mix_w, scale):
    S = z8.shape[1]

    def body(b_ref, c_ref, a_ref, p_ref, w_ref, mix_ref, sc_ref, y_ref):
        g = pl.program_id(0)
        cg = c_ref[...] * a_ref[...]
        w = w_ref[...]
        conv = w[0:1] * _shift_down(cg, 2) + w[1:2] * _shift_down(cg, 1) + w[2:3] * cg
        y_ref[0] = (b_ref[...] * conv).astype(y_ref.dtype)
        for gg, win in enumerate(POOL_WINDOWS):
            @pl.when(g == gg)
            def _(win=win):
                p = p_ref[...]
                pooled = _window_sum(p, win, _shift_down) / _pool_count(p.shape, win) - p
                y_ref[1] = (_dot(pooled, mix_ref[...], "nn") * sc_ref[...]).astype(y_ref.dtype)

    return pl.pallas_call(
        body, name="ab_mix_fwd", grid=(4,), out_shape=jax.ShapeDtypeStruct((2, S, 512), _MXU_DTYPE),
        in_specs=_ab_specs(S) + [pl.BlockSpec((3, 128), lambda g: (0, g)), pl.BlockSpec((None, 128, 128), lambda g: (g, 0, 0)),
                                 pl.BlockSpec((1, 128), lambda g: (0, g))],
        out_specs=pl.BlockSpec((2, S, 128), lambda g: (0, 0, g)), compiler_params=_cp("parallel"))(z8, z8, z8, z8, conv_w, mix_w, scale)


def _ab_mix_bwd(z8, dycat2, conv_w, mix_w, scale):
    S = z8.shape[1]

    def body(b_ref, c_ref, a_ref, p_ref, dy_ref, w_ref, mix_ref, sc_ref, dz_ref, dw_ref, dmix_ref, dsc_ref):
        g = pl.program_id(0)
        bv, cv, av, w = b_ref[...], c_ref[...], a_ref[...], w_ref[...]
        dya = dy_ref[0]
        cg = cv * av
        cg1, cg2 = _shift_down(cg, 1), _shift_down(cg, 2)
        conv = w[0:1] * cg2 + w[1:2] * cg1 + w[2:3] * cg
        dz_ref[0] = (dya * conv).astype(dz_ref.dtype)
        dconv = dya * bv
        dcg = w[2:3] * dconv + w[1:2] * _shift_up(dconv, 1) + w[0:1] * _shift_up(dconv, 2)
        dz_ref[1] = (dcg * av).astype(dz_ref.dtype)
        dz_ref[2] = (dcg * cv).astype(dz_ref.dtype)
        dw_ref[0:1, :] = jnp.sum(dconv * cg2, axis=0, keepdims=True)
        dw_ref[1:2, :] = jnp.sum(dconv * cg1, axis=0, keepdims=True)
        dw_ref[2:3, :] = jnp.sum(dconv * cg, axis=0, keepdims=True)
        for gg, win in enumerate(POOL_WINDOWS):
            @pl.when(g == gg)
            def _(win=win):
                p, dyb, mix = p_ref[...], dy_ref[1], mix_ref[...]
                cnt = _pool_count(p.shape, win)
                pooled = _window_sum(p, win, _shift_down) / cnt - p
                dsc_ref[...] = jnp.sum(dyb * _dot(pooled, mix, "nn"), axis=0, keepdims=True)
                dmixed = dyb * sc_ref[...]
                dmix_ref[...] = _dot(pooled, dmixed, "tn")
                dpooled = _dot(dmixed, mix, "nt")
                dz_ref[3] = (_window_sum(dpooled / cnt, win, _shift_up) - dpooled).astype(dz_ref.dtype)

    return pl.pallas_call(
        body, name="ab_mix_bwd", grid=(4,),
        out_shape=(jax.ShapeDtypeStruct((4, 2, S, 256), _MXU_DTYPE), jax.ShapeDtypeStruct((3, 512), F32),
                   jax.ShapeDtypeStruct((4, 128, 128), F32), jax.ShapeDtypeStruct((1, 512), F32)),
        in_specs=_ab_specs(S) + [pl.BlockSpec((2, S, 128), lambda g: (0, 0, g)), pl.BlockSpec((3, 128), lambda g: (0, g)),
                                 pl.BlockSpec((None, 128, 128), lambda g: (g, 0, 0)), pl.BlockSpec((1, 128), lambda g: (0, g))],
        out_specs=(pl.BlockSpec((4, None, S, 128), lambda g: (0, g // 2, 0, g % 2)), pl.BlockSpec((3, 128), lambda g: (0, g)),
                   pl.BlockSpec((None, 128, 128), lambda g: (g, 0, 0)), pl.BlockSpec((1, 128), lambda g: (0, g))),
        compiler_params=_cp("parallel"))(z8, z8, z8, z8, dycat2, conv_w, mix_w, scale)


HALO = 8


def _ffn_specs(S, n, tr):
    nb = S // HALO
    tile = pl.BlockSpec((2, None, tr, n), lambda j, i: (0, j, i, 0))
    prev = pl.BlockSpec((2, None, HALO, n), lambda j, i: (0, j, jnp.maximum(i * (tr // HALO) - 1, 0), 0))
    nxt = pl.BlockSpec((2, None, HALO, n), lambda j, i: (0, j, jnp.minimum((i + 1) * (tr // HALO), nb - 1), 0))
    cw = pl.BlockSpec((2, None, 3, n), lambda j, i: (0, j, 0, 0))
    return tile, prev, nxt, cw


def _conv_rows(ext, w, lo, tr):
    n = ext.shape[0]
    return (w[0:1] * pltpu.roll(ext, 2, axis=0)[lo:lo + tr] + w[1:2] * pltpu.roll(ext, 1, axis=0)[lo:lo + tr]
            + w[2:3] * ext[lo:lo + tr])


def _ffn_gate_fwd(name, u24, cw24, tr=256):
    _, J, S, n = u24.shape
    tile, prev, _, cw = _ffn_specs(S, n, tr)

    def body(u_ref, up_ref, w_ref, a_ref):
        keep = (pl.program_id(1) > 0).astype(F32)
        z = []
        for h in range(2):
            ext = jnp.concatenate([up_ref[h] * keep, u_ref[h]], axis=0)
            z.append(_conv_rows(ext, w_ref[h], HALO, tr))
        a_ref[...] = (_silu(z[0]) * z[1]).astype(a_ref.dtype)

    return pl.pallas_call(
        body, name=name, grid=(J, S // tr), out_shape=jax.ShapeDtypeStruct((J, S, n), _MXU_DTYPE),
        in_specs=[tile, prev, cw], out_specs=pl.BlockSpec((None, tr, n), lambda j, i: (j, i, 0)),
        compiler_params=_cp("parallel", "parallel"))(u24, u24, cw24)


def _ffn_gate_bwd(name, u24, cw24, da4, tr=256):
    _, J, S, n = u24.shape
    tile, prev, nxt, cw = _ffn_specs(S, n, tr)
    nb = S // HALO
    ext_rows = tr + 2 * HALO

    def body(u_ref, up_ref, un_ref, w_ref, da_ref, dan_ref, du_ref, dcw_ref):
        i = pl.program_id(1)
        first = i == 0
        keep_prev = (i > 0).astype(F32)
        keep_next = (i < S // tr - 1).astype(F32)
        ext = [jnp.concatenate([up_ref[h] * keep_prev, u_ref[h], un_ref[h]], axis=0) for h in range(2)]
        w = [w_ref[h] for h in range(2)]
        zg = _conv_rows(ext[0], w[0], HALO, tr + HALO)
        zu = _conv_rows(ext[1], w[1], HALO, tr + HALO)
        da = jnp.concatenate([da_ref[...], dan_ref[...] * keep_next], axis=0)
        sg = jax.nn.sigmoid(zg)
        dz = [da * zu * (sg * (1.0 + zg * (1.0 - sg))), da * (zg * sg)]
        m = tr + HALO
        for h in range(2):
            d = dz[h]
            du = w[h][2:3] * d[:tr] + w[h][1:2] * pltpu.roll(d, m - 1, axis=0)[:tr] + w[h][0:1] * pltpu.roll(d, m - 2, axis=0)[:tr]
            du_ref[h] = du.astype(du_ref.dtype)
            dt = d[:tr]
            e = ext[h]
            parts = [jnp.sum(dt * pltpu.roll(e, 2, axis=0)[HALO:HALO + tr], axis=0, keepdims=True),
                     jnp.sum(dt * pltpu.roll(e, 1, axis=0)[HALO:HALO + tr], axis=0, keepdims=True),
                     jnp.sum(dt * e[HALO:HALO + tr], axis=0, keepdims=True)]
            for k in range(3):
                @pl.when(first)
                def _(k=k, h=h):
                    dcw_ref[h, k:k + 1, :] = parts[k]

                @pl.when(jnp.logical_not(first))
                def _(k=k, h=h):
                    dcw_ref[h, k:k + 1, :] += parts[k]

    da_tile = pl.BlockSpec((None, tr, n), lambda j, i: (j, i, 0))
    da_next = pl.BlockSpec((None, HALO, n), lambda j, i: (j, jnp.minimum((i + 1) * (tr // HALO), nb - 1), 0))
    return pl.pallas_call(
        body, name=name, grid=(J, S // tr),
        out_shape=(jax.ShapeDtypeStruct((2, J, S, n), _MXU_DTYPE), jax.ShapeDtypeStruct((2, J, 3, n), F32)),
        in_specs=[tile, prev, nxt, cw, da_tile, da_next], out_specs=(tile, cw),
        compiler_params=_cp("parallel", "arbitrary"))(u24, u24, u24, cw24, da4, da4)


def _rms_rows(v, g):
    rstd = lax.rsqrt(jnp.mean(v * v, axis=-1, keepdims=True) + EPS)
    return v * rstd * g


def _rms_rows_bwd(v, g, dy):
    rstd = lax.rsqrt(jnp.mean(v * v, axis=-1, keepdims=True) + EPS)
    vhat = v * rstd
    dvhat = dy * g
    return rstd * (dvhat - vhat * jnp.mean(dvhat * vhat, axis=-1, keepdims=True)), dy * vhat


def _mla_prep_fwd(z, qg, kvg, tr=256):
    S = z.shape[0]

    def body(q_ref, kv_ref, qg_ref, kvg_ref, qn_ref, kvn_ref):
        qn_ref[...] = _rms_rows(q_ref[...], qg_ref[...]).astype(qn_ref.dtype)
        kvn_ref[...] = _rms_rows(kv_ref[...], kvg_ref[...]).astype(kvn_ref.dtype)

    return pl.pallas_call(
        body, name="mla_prep_fwd", grid=(S // tr,),
        out_shape=(jax.ShapeDtypeStruct((S, 256), _MXU_DTYPE), jax.ShapeDtypeStruct((S, 128), _MXU_DTYPE)),
        in_specs=[pl.BlockSpec((tr, 256), lambda i: (i, 0)), pl.BlockSpec((tr, 128), lambda i: (i, 2)), _vec_spec(256), _vec_spec(128)],
        out_specs=(_row_spec(tr, 256), _row_spec(tr, 128)), compiler_params=_cp("parallel"))(z, z, qg, kvg)


def _mla_prep_bwd(z, qg, kvg, dqn, dkvn, dkpe, duv, tr=256):
    S = z.shape[0]

    def body(q_ref, kv_ref, qg_ref, kvg_ref, dqn_ref, dkvn_ref, dkpe_ref, duv_ref, dz_ref, dqg_ref, dkvg_ref):
        first = pl.program_id(0) == 0
        dq, dqg = _rms_rows_bwd(q_ref[...], qg_ref[...], dqn_ref[...])
        dkv, dkvg = _rms_rows_bwd(kv_ref[...], kvg_ref[...], dkvn_ref[...])
        _acc_rows(dqg_ref, dqg, first)
        _acc_rows(dkvg_ref, dkvg, first)
        dz_ref[:, 0:256] = dq.astype(dz_ref.dtype)
        dz_ref[:, 256:384] = dkv.astype(dz_ref.dtype)
        dz_ref[:, 384:512] = dkpe_ref[...].astype(dz_ref.dtype)
        dz_ref[:, 512:1536] = duv_ref[...].astype(dz_ref.dtype)

    return pl.pallas_call(
        body, name="mla_prep_bwd", grid=(S // tr,),
        out_shape=(jax.ShapeDtypeStruct((S, 1536), _MXU_DTYPE), jax.ShapeDtypeStruct((1, 256), F32), jax.ShapeDtypeStruct((1, 128), F32)),
        in_specs=[pl.BlockSpec((tr, 256), lambda i: (i, 0)), pl.BlockSpec((tr, 128), lambda i: (i, 2)), _vec_spec(256), _vec_spec(128),
                  _row_spec(tr, 256), _row_spec(tr, 128), _row_spec(tr, 128), _row_spec(tr, 1024)],
        out_specs=(_row_spec(tr, 1536), _vec_spec(256), _vec_spec(128)),
        compiler_params=_cp("arbitrary"))(z, z, qg, kvg, dqn, dkvn, dkpe, duv)


def _rope(v, cos, sa, sb):
    return v * cos + pltpu.roll(v, 112, axis=1) * sa + pltpu.roll(v, 16, axis=1) * sb


def _rope_t(d, cos, sa, sb):
    return d * cos + pltpu.roll(d * sa, 16, axis=1) + pltpu.roll(d * sb, 112, axis=1)


def _rope_fwd(qraw, kvall, z, cosq, cosk, sa, sb, tr=256):
    S = qraw.shape[0]

    def body(q_ref, k_ref, v_ref, kpe_ref, cq_ref, ck_ref, sa_ref, sb_ref, qo_ref, ko_ref, vo_ref):
        cq, ck, sa_v, sb_v = cq_ref[...], ck_ref[...], sa_ref[...], sb_ref[...]
        kpe = _rope(kpe_ref[...], ck, sa_v, sb_v)
        for h in range(8):
            cols = slice(128 * h, 128 * h + 128)
            qo_ref[:, cols] = _rope(q_ref[:, cols], cq, sa_v, sb_v).astype(qo_ref.dtype)
            ko_ref[:, cols] = (k_ref[:, cols] + kpe).astype(ko_ref.dtype)
        vo_ref[...] = v_ref[...].astype(vo_ref.dtype)

    tab = _row_spec(tr, 128)
    return pl.pallas_call(
        body, name="rope_fwd", grid=(S // tr,),
        out_shape=(jax.ShapeDtypeStruct((S, 1024), _MXU_DTYPE), jax.ShapeDtypeStruct((S, 1024), _MXU_DTYPE),
                   jax.ShapeDtypeStruct((S, 512), _MXU_DTYPE)),
        in_specs=[_row_spec(tr, 1024), pl.BlockSpec((tr, 1024), lambda i: (i, 0)), pl.BlockSpec((tr, 512), lambda i: (i, 2)),
                  pl.BlockSpec((tr, 128), lambda i: (i, 3)), tab, tab, tab, tab],
        out_specs=(_row_spec(tr, 1024), _row_spec(tr, 1024), _row_spec(tr, 512)),
        compiler_params=_cp("parallel"))(qraw, kvall, kvall, z, cosq, cosk, sa, sb)


def _rope_bwd(dq, dk, dv, cosq, cosk, sa, sb, tr=256):
    S = dq.shape[0]

    def body(dq_ref, dk_ref, dv_ref, cq_ref, ck_ref, sa_ref, sb_ref, dqo_ref, dkv_ref, dkpe_ref):
        cq, ck, sa_v, sb_v = cq_ref[...], ck_ref[...], sa_ref[...], sb_ref[...]
        tot = jnp.zeros((tr, 128), F32)
        for h in range(8):
            cols = slice(128 * h, 128 * h + 128)
            dqo_ref[:, cols] = _rope_t(dq_ref[:, cols], cq, sa_v, sb_v).astype(dqo_ref.dtype)
            dkh = dk_ref[:, cols]
            tot = tot + dkh
            dkv_ref[:, cols] = dkh.astype(dkv_ref.dtype)
        dkv_ref[:, 1024:1536] = dv_ref[...].astype(dkv_ref.dtype)
        dkpe_ref[...] = _rope_t(tot, ck, sa_v, sb_v)

    tab = _row_spec(tr, 128)
    return pl.pallas_call(
        body, name="rope_bwd", grid=(S // tr,),
        out_shape=(jax.ShapeDtypeStruct((S, 1024), _MXU_DTYPE), jax.ShapeDtypeStruct((S, 1536), _MXU_DTYPE),
                   jax.ShapeDtypeStruct((S, 128), F32)),
        in_specs=[_row_spec(tr, 1024), _row_spec(tr, 1024), _row_spec(tr, 512), tab, tab, tab, tab],
        out_specs=(_row_spec(tr, 1024), _row_spec(tr, 1536), _row_spec(tr, 128)),
        compiler_params=_cp("parallel"))(dq, dk, dv, cosq, cosk, sa, sb)


NEG = -1e30


def _attn_fwd(q, k, v, tq=256, tk=256):
    S = q.shape[0]

    def body(q_ref, k_ref, v_ref, o_ref, lse_ref):
        i = pl.program_id(1)
        row = i * tq + lax.broadcasted_iota(jnp.int32, (tq, tk), 0)
        qs = [q_ref[:, 0:128], q_ref[:, 128:256]]

        def step(kb, carry):
            start = pl.multiple_of(kb * tk, tk)
            col = start + lax.broadcasted_iota(jnp.int32, (tq, tk), 1)
            vv = v_ref[pl.ds(start, tk), :]
            out = []
            for h in range(2):
                m, l, acc = carry[3 * h:3 * h + 3]
                s = _dot(qs[h], k_ref[pl.ds(start, tk), 128 * h:128 * h + 128], "nt") * ATTN_SCALE
                s = jnp.where(col <= row, s, NEG)
                m_new = jnp.maximum(m, jnp.max(s, axis=-1, keepdims=True))
                alpha = jnp.exp(m - m_new)
                p = jnp.exp(s - m_new)
                out += [m_new, alpha * l + jnp.sum(p, axis=-1, keepdims=True), alpha * acc + _dot(p, vv, "nn")]
            return tuple(out)

        init = (jnp.full((tq, 1), NEG, F32), jnp.zeros((tq, 1), F32), jnp.zeros((tq, 128), F32)) * 2
        ma, la, acca, mb, lb, accb = lax.fori_loop(0, (i * tq + tq) // tk, step, init)
        lane = lax.broadcasted_iota(jnp.int32, (tq, 128), 1)
        o_ref[...] = jnp.where(lane < 64, acca / la, accb / lb)
        lse_ref[...] = jnp.where(lane < 64, ma + jnp.log(la), mb + jnp.log(lb))

    return pl.pallas_call(
        body, name="attn_fwd", grid=(4, S // tq),
        out_shape=(jax.ShapeDtypeStruct((S, 512), F32), jax.ShapeDtypeStruct((4, S, 128), F32)),
        in_specs=[pl.BlockSpec((tq, 256), lambda p, i: (i, p)), pl.BlockSpec((S, 256), lambda p, i: (0, p)),
                  pl.BlockSpec((S, 128), lambda p, i: (0, p))],
        out_specs=(pl.BlockSpec((tq, 128), lambda p, i: (i, p)), pl.BlockSpec((None, tq, 128), lambda p, i: (p, i, 0))),
        compiler_params=_cp("parallel", "parallel"))(q, k, v)


def _attn_bwd(q, k, v, o, lse, dycat2, tq=256, tk=256):
    S = q.shape[0]

    def body(q_ref, k_ref, v_ref, o_ref, lse_ref, do_ref, dq_ref, dk_ref, dv_ref):
        j = pl.program_id(1)

        @pl.when(j == 0)
        def _():
            dq_ref[...] = jnp.zeros_like(dq_ref)

        col = j * tk + lax.broadcasted_iota(jnp.int32, (tq, tk), 1)
        lane = lax.broadcasted_iota(jnp.int32, (tq, 128), 1)
        ks = [k_ref[:, 0:128], k_ref[:, 128:256]]
        vv = v_ref[...]

        def step(qb, carry):
            dka, dkb, dvp = carry
            start = pl.multiple_of(qb * tq, tq)
            rows = pl.ds(start, tq)
            row = start + lax.broadcasted_iota(jnp.int32, (tq, tk), 0)
            do, lse_v = do_ref[rows, :], lse_ref[rows, :]
            prod = do * o_ref[rows, :]
            dks = [dka, dkb]
            for h in range(2):
                mine = (lane < 64) if h == 0 else (lane >= 64)
                delta = jnp.sum(jnp.where(mine, prod, 0.0), axis=-1, keepdims=True)
                do_h = jnp.where(mine, do, 0.0)
                qh = q_ref[rows, 128 * h:128 * h + 128]
                s = _dot(qh, ks[h], "nt") * ATTN_SCALE
                p = jnp.where(col <= row, jnp.exp(s - lse_v[:, 64 * h:64 * h + 1]), 0.0)
                dvp = dvp + _dot(p, do_h, "tn")
                ds = p * (_dot(do_h, vv, "nt") - delta) * ATTN_SCALE
                dq_ref[rows, 128 * h:128 * h + 128] += _dot(ds, ks[h], "nn")
                dks[h] = dks[h] + _dot(ds, qh, "tn")
            return dks[0], dks[1], dvp

        zero = jnp.zeros((tk, 128), F32)
        dka, dkb, dvp = lax.fori_loop((j * tk) // tq, S // tq, step, (zero, zero, zero))
        dk_ref[:, 0:128] = dka
        dk_ref[:, 128:256] = dkb
        dv_ref[...] = dvp

    return pl.pallas_call(
        body, name="attn_bwd", grid=(4, S // tk),
        out_shape=(jax.ShapeDtypeStruct((S, 1024), F32), jax.ShapeDtypeStruct((S, 1024), F32), jax.ShapeDtypeStruct((S, 512), F32)),
        in_specs=[pl.BlockSpec((S, 256), lambda p, j: (0, p)), pl.BlockSpec((tk, 256), lambda p, j: (j, p)),
                  pl.BlockSpec((tk, 128), lambda p, j: (j, p)), pl.BlockSpec((S, 128), lambda p, j: (0, p)),
                  pl.BlockSpec((None, S, 128), lambda p, j: (p, 0, 0)), pl.BlockSpec((None, S, 128), lambda p, j: (0, 0, p))],
        out_specs=(pl.BlockSpec((S, 256), lambda p, j: (0, p)), pl.BlockSpec((tk, 256), lambda p, j: (j, p)),
                   pl.BlockSpec((tk, 128), lambda p, j: (j, p))),
        compiler_params=_cp("parallel", "arbitrary"))(q, k, v, o, lse, dycat2)


CHUNK = 128
GELU_C = math.sqrt(2.0 / math.pi)


def _gelu(v):
    t = jnp.tanh(GELU_C * (v + 0.044715 * (v * v * v)))
    return v * (0.5 * (1.0 + t)), t


def _gelu_grad(v, t):
    return 0.5 * (1.0 + t) + v * (0.5 * (1.0 - t * t) * GELU_C * (1.0 + 3.0 * 0.044715 * v * v))


def _tril(w):
    r = lax.broadcasted_iota(jnp.int32, w.shape, 0)
    c = lax.broadcasted_iota(jnp.int32, w.shape, 1)
    return jnp.where(c <= r, w, 0.0)


def _layer_norm(v, g, b):
    xc = v - jnp.mean(v, axis=-1, keepdims=True)
    rstd = lax.rsqrt(jnp.mean(xc * xc, axis=-1, keepdims=True) + EPS)
    xhat = xc * rstd
    return xhat * g + b, xhat, rstd


def _sgu_fwd(z, o, ln_g, ln_b, w_s, b_st, tr=256):
    S = z.shape[0]

    def body(u_ref, v_ref, o_ref, g_ref, b_ref, ws_ref, bs_ref, y_ref):
        gu, _ = _gelu(u_ref[...])
        gv, _ = _gelu(v_ref[...])
        vln, _, _ = _layer_norm(gv, g_ref[...], b_ref[...])
        y_ref[0] = o_ref[...].astype(y_ref.dtype)
        for g in range(4):
            wt = _tril(ws_ref[g])
            cols = slice(128 * g, 128 * g + 128)
            for ch in range(tr // CHUNK):
                rows = slice(CHUNK * ch, CHUNK * ch + CHUNK)
                mixed = _dot(wt, vln[rows, cols], "nn") + bs_ref[:, g:g + 1]
                y_ref[1, rows, cols] = (gu[rows, cols] * mixed).astype(y_ref.dtype)

    return pl.pallas_call(
        body, name="sgu_fwd", grid=(S // tr,), out_shape=jax.ShapeDtypeStruct((2, S, 512), _MXU_DTYPE),
        in_specs=[pl.BlockSpec((tr, 512), lambda i: (i, 1)), pl.BlockSpec((tr, 512), lambda i: (i, 2)), _row_spec(tr, 512),
                  _vec_spec(512), _vec_spec(512), pl.BlockSpec((4, 128, 128), lambda i: (0, 0, 0)), pl.BlockSpec((128, 4), lambda i: (0, 0))],
        out_specs=pl.BlockSpec((2, tr, 512), lambda i: (0, i, 0)), compiler_params=_cp("parallel"))(z, z, o, ln_g, ln_b, w_s, b_st)


def _sgu_bwd(z, dycat2, ln_g, ln_b, w_s, b_st, tr=256):
    S = z.shape[0]

    def body(u_ref, v_ref, dy_ref, g_ref, b_ref, ws_ref, bs_ref, duv_ref, dg_ref, db_ref, dws_ref, dbs_ref):
        first = pl.program_id(0) == 0
        u_pre, v_pre = u_ref[...], v_ref[...]
        gu, tu = _gelu(u_pre)
        gv, tv = _gelu(v_pre)
        gain = g_ref[...]
        vln, xhat, rstd = _layer_norm(gv, gain, b_ref[...])

        @pl.when(first)
        def _():
            dws_ref[...] = jnp.zeros_like(dws_ref)
            dbs_ref[...] = jnp.zeros_like(dbs_ref)

        dvln_cols = []
        for g in range(4):
            wt = _tril(ws_ref[g])
            cols = slice(128 * g, 128 * g + 128)
            dmixed_sum = jnp.zeros((CHUNK, 128), F32)
            dw = jnp.zeros((CHUNK, CHUNK), F32)
            dvln_rows = []
            for ch in range(tr // CHUNK):
                rows = slice(CHUNK * ch, CHUNK * ch + CHUNK)
                vt = vln[rows, cols]
                mixed = _dot(wt, vt, "nn") + bs_ref[:, g:g + 1]
                dyd = dy_ref[rows, cols]
                duv_ref[rows, cols] = (dyd * mixed * _gelu_grad(u_pre[rows, cols], tu[rows, cols])).astype(duv_ref.dtype)
                dmixed = dyd * gu[rows, cols]
                dmixed_sum = dmixed_sum + dmixed
                dw = dw + _dot(dmixed, vt, "nt")
                dvln_rows.append(_dot(wt, dmixed, "tn"))
            dws_ref[g] += _tril(dw)
            dbs_ref[g:g + 1, :] += jnp.sum(dmixed_sum.T, axis=0, keepdims=True)
            dvln_cols.append(jnp.concatenate(dvln_rows, axis=0))
        dvln = jnp.concatenate(dvln_cols, axis=1)
        _acc_rows(dg_ref, dvln * xhat, first)
        _acc_rows(db_ref, dvln, first)
        dxhat = dvln * gain
        dgv = rstd * (dxhat - jnp.mean(dxhat, axis=-1, keepdims=True) - xhat * jnp.mean(dxhat * xhat, axis=-1, keepdims=True))
        duv_ref[:, 512:1024] = (dgv * _gelu_grad(v_pre, tv)).astype(duv_ref.dtype)

    return pl.pallas_call(
        body, name="sgu_bwd", grid=(S // tr,),
        out_shape=(jax.ShapeDtypeStruct((S, 1024), _MXU_DTYPE), jax.ShapeDtypeStruct((1, 512), F32), jax.ShapeDtypeStruct((1, 512), F32),
                   jax.ShapeDtypeStruct((4, 128, 128), F32), jax.ShapeDtypeStruct((4, 128), F32)),
        in_specs=[pl.BlockSpec((tr, 512), lambda i: (i, 1)), pl.BlockSpec((tr, 512), lambda i: (i, 2)),
                  pl.BlockSpec((None, tr, 512), lambda i: (1, i, 0)), _vec_spec(512), _vec_spec(512),
                  pl.BlockSpec((4, 128, 128), lambda i: (0, 0, 0)), pl.BlockSpec((128, 4), lambda i: (0, 0))],
        out_specs=(_row_spec(tr, 1024), _vec_spec(512), _vec_spec(512), pl.BlockSpec((4, 128, 128), lambda i: (0, 0, 0)),
                   pl.BlockSpec((4, 128), lambda i: (0, 0))),
        compiler_params=_cp("arbitrary"))(z, z, dycat2, ln_g, ln_b, w_s, b_st)


def _sum_parts(name, parts, tr=512):
    P, R, C = parts.shape
    tr = _tile(R, tr) if R % 8 == 0 else R

    def body(p_ref, o_ref):
        g = p_ref[0]
        for k in range(1, P):
            g = g + p_ref[k]
        o_ref[...] = g

    return pl.pallas_call(
        body, name=name, grid=(R // tr,), out_shape=jax.ShapeDtypeStruct((R, C), F32),
        in_specs=[pl.BlockSpec((P, tr, C), lambda i: (0, i, 0))], out_specs=_row_spec(tr, C),
        compiler_params=_cp("parallel"))(parts)


def _adamw(name, w, m, v, parts):
    P, R, C = parts.shape
    tr = R
    for cand in (512, 256, 128, 64, 32, 16, 8):
        if R % cand == 0 and cand * C * 4 * (P + 7) * 2 <= 40 * 2 ** 20:
            tr = cand
            break
    c1 = 1.0 / (1.0 - ADAM_B1 ** ADAM_STEP)
    c2 = 1.0 / (1.0 - ADAM_B2 ** ADAM_STEP)

    def body(w_ref, m_ref, v_ref, p_ref, g_ref, d_ref, mo_ref, vo_ref):
        g = p_ref[0]
        for k in range(1, P):
            g = g + p_ref[k]
        m2 = ADAM_B1 * m_ref[...] + (1.0 - ADAM_B1) * g
        v2 = ADAM_B2 * v_ref[...] + (1.0 - ADAM_B2) * (g * g)
        g_ref[...] = g
        mo_ref[...] = m2
        vo_ref[...] = v2
        d_ref[...] = -ADAM_LR * ((m2 * c1) / (jnp.sqrt(v2 * c2) + ADAM_EPS) + ADAM_WD * w_ref[...])

    sds = jax.ShapeDtypeStruct((R, C), F32)
    return pl.pallas_call(
        body, name=name, grid=(R // tr,), out_shape=(sds, sds, sds, sds),
        in_specs=[_row_spec(tr, C)] * 3 + [pl.BlockSpec((P, tr, C), lambda i: (0, i, 0))],
        out_specs=(_row_spec(tr, C),) * 4, compiler_params=_cp("parallel"))(w, m, v, parts)


def _rope_tables(positions):
    half = 16
    inv_freq = 10000.0 ** (-jnp.arange(half, dtype=F32) / half)
    ang = positions.astype(F32)[:, None] * inv_freq
    cos, sin = jnp.cos(ang), jnp.sin(ang)
    S = positions.shape[0]
    z16, z32, z64 = jnp.zeros((S, 16), F32), jnp.zeros((S, 32), F32), jnp.zeros((S, 64), F32)
    cosk = jnp.concatenate([z64, cos, cos, z32], axis=1)
    cosq = jnp.concatenate([jnp.ones((S, 64), F32), cos, cos, z32], axis=1)
    sa = jnp.concatenate([z64, -sin, z16, z32], axis=1)
    sb = jnp.concatenate([z64, z16, sin, z32], axis=1)
    return cosq, cosk, sa, sb


def _ffn_fwd(l, x, mod, n2g, w_up8, cw24, w_down4):
    sh, sc, gate = mod
    h = _rmsmod_fwd(f"ffn{l}_norm", x, n2g, sc, sh)
    u8 = _mm_cols(f"ffn{l}_up", h, w_up8)
    S, n = u8.shape[1], u8.shape[2]
    u24 = u8.reshape(2, 4, S, n)
    a4 = _ffn_gate_fwd(f"ffn{l}_gate", u24, cw24)
    f, x_new = _mm_rows_resid(f"ffn{l}_down", a4, w_down4, x, gate)
    return x_new, (x, h, u24, a4, f)


def _ffn_bwd(l, dx, saved, mod, n2g, w_up8, cw24, w_down4):
    sh, sc, gate = mod
    x, h, u24, a4, f = saved
    df, dgate = _gate_bwd(f"ffn{l}_gate_bwd", dx, f, gate)
    da4 = _mm_rows_dx(f"ffn{l}_down_dx", df, w_down4)
    dw_down4 = _mm_rows_dw(f"ffn{l}_down_dw", a4, df)
    du24, dcw24 = _ffn_gate_bwd(f"ffn{l}_act_bwd", u24, cw24, da4)
    du8 = du24.reshape((8,) + du24.shape[2:])
    dh = _mm_cols_dx(f"ffn{l}_up_dx", du8, w_up8)
    dw_up8 = _mm_cols_dw(f"ffn{l}_up_dw", h, du8)
    dx_new, dn2g, dsc, dsh = _rmsmod_bwd(f"ffn{l}_norm_bwd", x, n2g, sc, dh, dx)
    return dx_new, dict(w_up8=dw_up8, cw24=dcw24, w_down4=dw_down4, n2g=dn2g, mod=(dsh, dsc, dgate))


def kernel(x, c, positions, ada_w, ada_b, norm1_g, norm2_g, ab_w_in, a_conv_w, b_mix_w, b_scale, ab_w_out, cd_w_in, c_q_norm_g, c_w_uq, c_kv_norm_g, c_w_ukv, d_ln_g, d_ln_b, d_w_s, d_b_s, cd_w_out, ffn_w_up, ffn_conv_w, ffn_w_down, final_norm_g, loss_target, m_ada_w, m_ada_b, m_norm1_g, m_norm2_g, m_ab_w_in, m_a_conv_w, m_b_mix_w, m_b_scale, m_ab_w_out, m_cd_w_in, m_c_q_norm_g, m_c_w_uq, m_c_kv_norm_g, m_c_w_ukv, m_d_ln_g, m_d_ln_b, m_d_w_s, m_d_b_s, m_cd_w_out, m_ffn_w_up, m_ffn_conv_w, m_ffn_w_down, m_final_norm_g, v_ada_w, v_ada_b, v_norm1_g, v_norm2_g, v_ab_w_in, v_a_conv_w, v_b_mix_w, v_b_scale, v_ab_w_out, v_cd_w_in, v_c_q_norm_g, v_c_w_uq, v_c_kv_norm_g, v_c_w_ukv, v_d_ln_g, v_d_ln_b, v_d_w_s, v_d_b_s, v_cd_w_out, v_ffn_w_up, v_ffn_conv_w, v_ffn_w_down, v_final_norm_g):
    S, D = x.shape[1], x.shape[2]
    me = 4 * lax.axis_index("x") + 2 * lax.axis_index("y") + lax.axis_index("c")
    x0, target = x[0], loss_target[0]
    W = _MXU_DTYPE

    small_shapes = [(1024,), (3, 64), (32,), (64,), (64,), (2, 3, 704)]
    (g0,) = _exchange("gather_small", [[_pack([c, a_conv_w, c_q_norm_g, d_ln_g, d_ln_b, ffn_conv_w])]], scatter=False)
    c_all, aconv_s, qg_s, lng_s, lnb_s, fcw_s = _unpack(g0[:, 0], small_shapes, lead=(N_DEV,))
    conv_w = aconv_s.transpose(1, 0, 2).reshape(3, 512)
    qg, ln_g, ln_b = qg_s.reshape(1, 256), lng_s.reshape(1, 512), lnb_s.reshape(1, 512)
    cw24 = [fcw_s[:, l].reshape(2, 4, 3, 704) for l in range(2)]
    c16 = jnp.pad(c_all, ((0, 16 - N_DEV), (0, 0)))

    mod_cols = _ada_fwd(c16, ada_w)
    (g1,) = _exchange("gather_mod", [[_pack([mod_cols])]], scatter=False)
    mod_all = _unpack(g1[:, 0], [(2, 16, 768)], lead=(N_DEV,))[0]
    mod_mine = lax.dynamic_index_in_dim(mod_all, me, axis=2, keepdims=False)
    mod = mod_mine.transpose(1, 0, 2).reshape(2, 6 * D) + ada_b
    mods = [[mod[l, k * D:(k + 1) * D].reshape(1, D) for k in range(6)] for l in range(2)]

    gathered = _exchange("gather_weights", [
        [ab_w_in[0].astype(W)], [ab_w_out[0].astype(W)], [cd_w_in[0].astype(W).reshape(1440, 128)],
        [c_w_uq[0].astype(W).reshape(192, 128)], [c_w_ukv[0].astype(W)], [cd_w_out[0].astype(W)],
        [ffn_w_up[0].astype(W)], [ffn_w_up[1].astype(W)], [ffn_w_down[0].astype(W)], [ffn_w_down[1].astype(W)]], scatter=False)
    w_abin8, w_about, w_cdin, w_uq, w_ukv, w_cdout, w_up0, w_up1, w_dn0, w_dn1 = [g[:, 0] for g in gathered]
    w_about2 = w_about.reshape(2, 512, D)
    w_cdout2 = w_cdout.reshape(2, 512, D)
    w_up8 = [w_up0, w_up1]
    w_down4 = [w_dn0.reshape(4, 704, D), w_dn1.reshape(4, 704, D)]
    w_cd = w_cdin.reshape(8, D, 180).transpose(1, 0, 2).reshape(D, 1440)
    zc = lambda n: jnp.zeros((D, n), W)
    w_cd_pad = jnp.concatenate([w_cd[:, :384], zc(64), w_cd[:, 384:416], zc(32), w_cd[:, 416:]], axis=1)
    w_uq_pad = jnp.pad(w_uq.reshape(8, 256, 96).transpose(1, 0, 2), ((0, 0), (0, 0), (0, 32))).reshape(256, 1024)
    w_ukv_h = w_ukv.transpose(1, 0, 2)
    w_k_pad = jnp.pad(w_ukv_h[:, :, :64], ((0, 0), (0, 0), (0, 64))).reshape(128, 1024)
    w_kv_pad = jnp.concatenate([w_k_pad, w_ukv_h[:, :, 64:].reshape(128, 512)], axis=1)

    cosq, cosk, sa, sb = _rope_tables(positions[0])
    n1g = [norm1_g[l].reshape(1, D) for l in range(2)]
    n2g = [norm2_g[l].reshape(1, D) for l in range(2)]
    mix_w, scale = b_mix_w[0], b_scale
    kvg = c_kv_norm_g
    w_s, b_st = d_w_s[0], d_b_s[0].T

    sh1, sc1, g1m = mods[0][:3]
    h_ab = _rmsmod_fwd("ab_norm", x0, n1g[0], sc1, sh1)
    z8 = _mm_cols("ab_in", h_ab, w_abin8)
    ycat_ab = _ab_mix_fwd(z8, conv_w, mix_w, scale)
    y_ab, x1 = _mm_rows_resid("ab_out", ycat_ab, w_about2, x0, g1m)
    x2, ffn0_saved = _ffn_fwd(0, x1, mods[0][3:], n2g[0], w_up8[0], cw24[0], w_down4[0])

    sh1, sc1, g1c = mods[1][:3]
    h_cd = _rmsmod_fwd("cd_norm", x2, n1g[1], sc1, sh1)
    z_cd = _mm_nn("cd_in", h_cd, w_cd_pad)
    qn, kvn = _mla_prep_fwd(z_cd, qg, kvg)
    qraw = _mm_nn("cd_uq", qn, w_uq_pad)
    kvall = _mm_nn("cd_ukv", kvn, w_kv_pad)
    q_r, k_r, v_r = _rope_fwd(qraw, kvall, z_cd, cosq, cosk, sa, sb)
    o, lse = _attn_fwd(q_r, k_r, v_r)
    ycat_cd = _sgu_fwd(z_cd, o, ln_g, ln_b, w_s, b_st)
    y_cd, x3 = _mm_rows_resid("cd_out", ycat_cd, w_cdout2, x2, g1c)
    x4, ffn1_saved = _ffn_fwd(1, x3, mods[1][3:], n2g[1], w_up8[1], cw24[1], w_down4[1])

    loss_local, dx4, dfg = _loss_head(x4, final_norm_g.reshape(1, D), target)

    dx3, gf1 = _ffn_bwd(1, dx4, ffn1_saved, mods[1][3:], n2g[1], w_up8[1], cw24[1], w_down4[1])

    dy, dg1c = _gate_bwd("cd_gate_bwd", dx3, y_cd, g1c)
    dycat = _mm_rows_dx("cd_out_dx", dy, w_cdout2)
    dw_cdout = _mm_rows_dw("cd_out_dw", ycat_cd, dy)
    duv, dln_g, dln_b, dws, dbs = _sgu_bwd(z_cd, dycat, ln_g, ln_b, w_s, b_st)
    dq_r, dk_r, dv_r = _attn_bwd(q_r, k_r, v_r, o, lse, dycat)
    dqraw, dkvall, dkpe = _rope_bwd(dq_r, dk_r, dv_r, cosq, cosk, sa, sb)
    dqn = _mm_nt("cd_uq_dx", dqraw, w_uq_pad, tn=256)
    dkvn = _mm_nt("cd_ukv_dx", dkvall, w_kv_pad, tn=128)
    dw_uq_pad = _mm_tn("cd_uq_dw", qn, dqraw, tm=256)
    dw_kv_pad = _mm_tn("cd_ukv_dw", kvn, dkvall, tm=128)
    dz_cd, dqg, dkvg = _mla_prep_bwd(z_cd, qg, kvg, dqn, dkvn, dkpe, duv)
    dh_cd = _mm_nt("cd_in_dx", dz_cd, w_cd_pad)
    dw_cd_pad = _mm_tn("cd_in_dw", h_cd, dz_cd)
    dx2, dn1g_cd, dsc1_cd, dsh1_cd = _rmsmod_bwd("cd_norm_bwd", x2, n1g[1], sc1, dh_cd, dx3)

    dx1, gf0 = _ffn_bwd(0, dx2, ffn0_saved, mods[0][3:], n2g[0], w_up8[0], cw24[0], w_down4[0])

    dy, dg1m = _gate_bwd("ab_gate_bwd", dx1, y_ab, g1m)
    dycat = _mm_rows_dx("ab_out_dx", dy, w_about2)
    dw_about = _mm_rows_dw("ab_out_dw", ycat_ab, dy)
    dz8, dconv_w, dmix_w, dscale = _ab_mix_bwd(z8, dycat, conv_w, mix_w, scale)
    dz8 = dz8.reshape(8, S, 256)
    dh_ab = _mm_cols_dx("ab_in_dx", dz8, w_abin8)
    dw_abin8 = _mm_cols_dw("ab_in_dw", h_ab, dz8)
    dx0, dn1g_ab, dsc1_ab, dsh1_ab = _rmsmod_bwd("ab_norm_bwd", x0, n1g[0], mods[0][1], dh_ab, dx1)

    dmod = jnp.stack([jnp.concatenate([dsh1_ab, dsc1_ab, dg1m, *gf0["mod"]], axis=1)[0],
                      jnp.concatenate([dsh1_cd, dsc1_cd, dg1c, *gf1["mod"]], axis=1)[0]])
    rep_grads = [dmod, jnp.concatenate([dn1g_ab, dn1g_cd]), jnp.concatenate([gf0["n2g"], gf1["n2g"]]),
                 dmix_w, dscale, dkvg, dws, dbs, dfg]
    rep_names = ["ada_b", "norm1_g", "norm2_g", "b_mix_w", "b_scale", "c_kv_norm_g", "d_w_s", "d_b_s", "final_norm_g"]
    dfcw = jnp.stack([gf0["cw24"].reshape(8, 3, 704), gf1["cw24"].reshape(8, 3, 704)])
    shard_grads = [dconv_w, dqg, dln_g, dln_b, dfcw]
    rep_buf, shard_buf = _pack(rep_grads), _pack(shard_grads)
    r_rep = rep_buf.shape[0]
    (g2,) = _exchange("gather_small_grads", [[jnp.concatenate([rep_buf, shard_buf])]], scatter=False)
    g2 = g2[:, 0]

    dmod_all = g2[:, :2 * 6 * D // 128].reshape(N_DEV, 2, N_DEV, 768)
    dmod_cols = lax.dynamic_index_in_dim(dmod_all, me, axis=2, keepdims=False).transpose(1, 0, 2)
    g_ada_w = _ada_bwd(c16, jnp.pad(dmod_cols, ((0, 0), (0, 16 - N_DEV), (0, 0))))

    dw_cd = jnp.concatenate([dw_cd_pad[:, :384], dw_cd_pad[:, 448:480], dw_cd_pad[:, 512:]], axis=1)
    dw_cd8 = dw_cd.reshape(D, 8, 180).transpose(1, 0, 2).reshape(8, 1440, 128)
    dw_uq8 = dw_uq_pad.reshape(256, 8, 128)[:, :, :96].transpose(1, 0, 2).reshape(8, 192, 128)
    dw_ukv8 = jnp.concatenate([dw_kv_pad[:, :1024].reshape(128, 8, 128)[:, :, :64], dw_kv_pad[:, 1024:].reshape(128, 8, 64)],
                              axis=2).transpose(1, 0, 2)
    parts = _exchange("scatter_grads", [
        [dw_abin8], [dw_about.reshape(8, 128, D)], [dw_cd8], [dw_uq8], [dw_ukv8], [dw_cdout.reshape(8, 128, D)],
        [gf0["w_up8"], gf1["w_up8"]], [gf0["w_down4"].reshape(8, 352, D), gf1["w_down4"].reshape(8, 352, D)]], scatter=True)
    p_abin, p_about, p_cdin, p_uq, p_ukv, p_cdout, p_up, p_down = parts

    res = {}

    def update(name, w, m, v, p, shape2d):
        outs = _adamw("adamw_" + name, w.reshape(shape2d), m.reshape(shape2d), v.reshape(shape2d), p.reshape((p.shape[0],) + shape2d))
        res[name] = [o_.reshape(w.shape) for o_ in outs]

    update("ada_w", ada_w, m_ada_w, v_ada_w, g_ada_w[None], (2 * D, 768))
    update("ab_w_in", ab_w_in, m_ab_w_in, v_ab_w_in, p_abin, (D, 256))
    update("ab_w_out", ab_w_out, m_ab_w_out, v_ab_w_out, p_about, (128, D))
    update("cd_w_in", cd_w_in, m_cd_w_in, v_cd_w_in, p_cdin, (1440, 128))
    update("c_w_uq", c_w_uq, m_c_w_uq, v_c_w_uq, p_uq, (192, 128))
    update("c_w_ukv", c_w_ukv, m_c_w_ukv, v_c_w_ukv, p_ukv, (128, 128))
    update("cd_w_out", cd_w_out, m_cd_w_out, v_cd_w_out, p_cdout, (128, D))
    update("ffn_w_up", ffn_w_up, m_ffn_w_up, v_ffn_w_up, p_up, (2 * D, 704))
    update("ffn_w_down", ffn_w_down, m_ffn_w_down, v_ffn_w_down, p_down, (2 * 352, D))

    rep_w = dict(ada_b=(ada_b, m_ada_b, v_ada_b), norm1_g=(norm1_g, m_norm1_g, v_norm1_g), norm2_g=(norm2_g, m_norm2_g, v_norm2_g),
                 b_mix_w=(b_mix_w, m_b_mix_w, v_b_mix_w), b_scale=(b_scale, m_b_scale, v_b_scale),
                 c_kv_norm_g=(c_kv_norm_g, m_c_kv_norm_g, v_c_kv_norm_g), d_w_s=(d_w_s, m_d_w_s, v_d_w_s),
                 d_b_s=(d_b_s, m_d_b_s, v_d_b_s), final_norm_g=(final_norm_g, m_final_norm_g, v_final_norm_g))
    rep_packed = [_pack([rep_w[n][k] for n in rep_names]) for k in range(3)]
    rep_out = _adamw("adamw_replicated", *rep_packed, g2[:, :r_rep])
    rep_shapes = [rep_w[n][0].shape for n in rep_names]
    for k, n in enumerate(rep_names):
        res[n] = [_unpack(o_, rep_shapes)[k] for o_ in rep_out]

    shard_sum = _sum_parts("sum_small_grads", g2[:, r_rep:])
    g_conv, g_qg, g_lng, g_lnb, g_fcw = _unpack(shard_sum, [(3, 512), (256,), (512,), (512,), (2, 8, 3, 704)])
    mine = lambda a, n, axis: lax.dynamic_slice_in_dim(a, me * n, n, axis=axis)
    sh_names = ["a_conv_w", "c_q_norm_g", "d_ln_g", "d_ln_b", "ffn_conv_w"]
    sh_grads = [mine(g_conv, 64, 1), mine(g_qg, 32, 0), mine(g_lng, 64, 0), mine(g_lnb, 64, 0),
                lax.dynamic_index_in_dim(g_fcw, me, axis=1, keepdims=False)]
    sh_w = dict(a_conv_w=(a_conv_w, m_a_conv_w, v_a_conv_w), c_q_norm_g=(c_q_norm_g, m_c_q_norm_g, v_c_q_norm_g),
                d_ln_g=(d_ln_g, m_d_ln_g, v_d_ln_g), d_ln_b=(d_ln_b, m_d_ln_b, v_d_ln_b),
                ffn_conv_w=(ffn_conv_w, m_ffn_conv_w, v_ffn_conv_w))
    sh_packed = [_pack([sh_w[n][k] for n in sh_names]) for k in range(3)]
    sh_out = _adamw("adamw_small_shards", *sh_packed, _pack(sh_grads)[None])
    sh_shapes = [sh_w[n][0].shape for n in sh_names]
    for k, n in enumerate(sh_names):
        res[n] = [_unpack(o_, sh_shapes)[k] for o_ in sh_out]

    loss = lax.psum(loss_local[0, 0], ("x", "y", "c"))
    order = ["ada_w", "ada_b", "norm1_g", "norm2_g", "ab_w_in", "a_conv_w", "b_mix_w", "b_scale", "ab_w_out", "cd_w_in", "c_q_norm_g",
             "c_w_uq", "c_kv_norm_g", "c_w_ukv", "d_ln_g", "d_ln_b", "d_w_s", "d_b_s", "cd_w_out", "ffn_w_up", "ffn_conv_w",
             "ffn_w_down", "final_norm_g"]
    return (loss, dx0[None], *[res[n][0] for n in order], *[res[n][1] for n in order], *[res[n][2] for n in order],
            *[res[n][3] for n in order])
```

```python
import functools
import math

import jax
import jax.numpy as jnp
from jax import lax
from jax.experimental import pallas as pl
from jax.experimental.pallas import tpu as pltpu

F32 = jnp.float32
BF16 = jnp.bfloat16
_MXU_DTYPE = BF16
WIRE_DTYPE = BF16
_VMEM_LIMIT = 56 * 2 ** 20
N_DEV = 8
EPS = 1e-6
POOL_WINDOWS = (2, 4, 8, 16)
ATTN_SCALE = (64 + 32) ** -0.5
ADAM_LR, ADAM_B1, ADAM_B2, ADAM_EPS, ADAM_WD, ADAM_STEP = 0.001, 0.9, 0.999, 1e-08, 0.01, 10
MESH = pl.DeviceIdType.MESH
ANY = pl.BlockSpec(memory_space=pl.ANY)


def _cp(*sem):
    return pltpu.CompilerParams(dimension_semantics=sem, vmem_limit_bytes=_VMEM_LIMIT)


def _dot(a, b, contract):
    dn = {"nn": (((1,), (0,)), ((), ())), "nt": (((1,), (1,)), ((), ())), "tn": (((0,), (0,)), ((), ()))}[contract]
    return lax.dot_general(a.astype(_MXU_DTYPE), b.astype(_MXU_DTYPE), dn, preferred_element_type=F32)


def _my_position():
    x, y, c = lax.axis_index("x"), lax.axis_index("y"), lax.axis_index("c")
    return x, y, c, 4 * x + 2 * y + c


def _exchange(name, groups, scatter):
    flat = [a for g in groups for a in g]
    n_in, n_grp = len(flat), len(groups)
    out_shapes = []
    for g in groups:
        slab = g[0].shape[1:] if scatter else g[0].shape
        out_shapes.append(jax.ShapeDtypeStruct((N_DEV, len(g)) + tuple(slab), g[0].dtype))

    def body(*refs):
        ins, outs = refs[:n_in], refs[n_in:n_in + n_grp]
        send_sems, recv_sems, local_sems = refs[n_in + n_grp:]
        x, y, c, me = _my_position()
        i = 0
        for gi, g in enumerate(groups):
            for l in range(len(g)):
                src = ins[i]
                i += 1
                pltpu.make_async_copy(src.at[me] if scatter else src, outs[gi].at[me, l], local_sems.at[gi]).start()
                for k in range(1, N_DEV):
                    px = 1 - x if k & 4 else x
                    py = 1 - y if k & 2 else y
                    pc = 1 - c if k & 1 else c
                    peer = 4 * px + 2 * py + pc
                    pltpu.make_async_remote_copy(
                        src_ref=src.at[peer] if scatter else src, dst_ref=outs[gi].at[me, l],
                        send_sem=send_sems.at[gi], recv_sem=recv_sems.at[gi],
                        device_id=(px, py, pc), device_id_type=MESH).start()
        for gi in range(n_grp):
            mine = outs[gi].at[me]
            pltpu.make_async_copy(mine, mine, local_sems.at[gi]).wait()
            seven = outs[gi].at[pl.ds(0, N_DEV - 1)]
            w = pltpu.make_async_remote_copy(src_ref=seven, dst_ref=seven, send_sem=send_sems.at[gi],
                                             recv_sem=recv_sems.at[gi], device_id=(x, y, c), device_id_type=MESH)
            w.wait_send()
            w.wait_recv()

    return pl.pallas_call(
        body, name=name, out_shape=tuple(out_shapes),
        in_specs=[ANY] * n_in, out_specs=tuple([ANY] * n_grp),
        scratch_shapes=[pltpu.SemaphoreType.DMA((n_grp,)), pltpu.SemaphoreType.DMA((n_grp,)),
                        pltpu.SemaphoreType.DMA((n_grp,))],
        compiler_params=pltpu.CompilerParams(has_side_effects=True),
    )(*flat)


HBM_SPEC = pl.BlockSpec(memory_space=pltpu.HBM)
SEM_SPEC = pl.BlockSpec(memory_space=pltpu.SEMAPHORE)
EFFECT = pltpu.SideEffectType.DATAFLOW_SIDE_EFFECTING


def _group_shapes(groups, scatter):
    return [((N_DEV, len(g)) + tuple(g[0].shape[1:] if scatter else g[0].shape), g[0].dtype) for g in groups]


def _exchange_start(name, groups, scatter):
    flat = [pltpu.with_memory_space_constraint(a, pltpu.HBM) for g in groups for a in g]
    shapes = _group_shapes(groups, scatter)
    lands = [pltpu.with_memory_space_constraint(lax.empty(s, d), pltpu.HBM) for s, d in shapes]
    n_in, n_grp = len(flat), len(groups)

    def body(*refs):
        ins, land = refs[:n_in], refs[n_in:n_in + n_grp]
        send_sems, recv_sems = refs[n_in + n_grp], refs[n_in + n_grp + 1]
        token, local_sems = refs[-2], refs[-1]
        x, y, c, me = _my_position()
        i = 0
        for gi, g in enumerate(groups):
            for l in range(len(g)):
                src = ins[i]
                i += 1
                pltpu.make_async_copy(src.at[me] if scatter else src, land[gi].at[me, l], local_sems.at[gi]).start()
                for k in range(1, N_DEV):
                    px = 1 - x if k & 4 else x
                    py = 1 - y if k & 2 else y
                    pc = 1 - c if k & 1 else c
                    peer = 4 * px + 2 * py + pc
                    pltpu.make_async_remote_copy(
                        src_ref=src.at[peer] if scatter else src, dst_ref=land[gi].at[me, l],
                        send_sem=send_sems.at[gi], recv_sem=recv_sems.at[gi],
                        device_id=(px, py, pc), device_id_type=MESH).start()
        for gi in range(n_grp):
            mine = land[gi].at[me]
            pltpu.make_async_copy(mine, mine, local_sems.at[gi]).wait()
        token[...] = jnp.zeros_like(token)

    outs = pl.pallas_call(
        body, name=name,
        out_shape=(pltpu.SemaphoreType.DMA((n_grp,)), pltpu.SemaphoreType.DMA((n_grp,)),
                   *[pltpu.HBM(a.shape, a.dtype) for a in flat], *[pltpu.HBM(s, d) for s, d in shapes],
                   jax.ShapeDtypeStruct((8, 128), F32)),
        in_specs=[HBM_SPEC] * (n_in + n_grp),
        out_specs=(SEM_SPEC, SEM_SPEC, *[HBM_SPEC] * (n_in + n_grp), pl.BlockSpec(memory_space=pltpu.VMEM)),
        input_output_aliases={i: 2 + i for i in range(n_in + n_grp)},
        scratch_shapes=[pltpu.SemaphoreType.DMA((n_grp,))],
        compiler_params=pltpu.CompilerParams(has_side_effects=EFFECT),
    )(*flat, *lands)
    handle = (outs[0], outs[1], outs[2:2 + n_in], outs[2 + n_in:2 + n_in + n_grp])
    return handle, outs[-1]


def _exchange_wait(name, handle, after):
    send_sems, recv_sems, srcs, lands = handle
    n_in, n_grp = len(srcs), len(lands)

    def body(*refs):
        land = refs[n_in:n_in + n_grp]
        send_ref, recv_ref = refs[n_in + n_grp], refs[n_in + n_grp + 1]
        x, y, c, _ = _my_position()
        for gi in range(n_grp):
            seven = land[gi].at[pl.ds(0, N_DEV - 1)]
            w = pltpu.make_async_remote_copy(src_ref=seven, dst_ref=seven, send_sem=send_ref.at[gi], recv_sem=recv_ref.at[gi],
                                             device_id=(x, y, c), device_id_type=MESH)
            w.wait_send()
            w.wait_recv()

    outs = pl.pallas_call(
        body, name=name,
        out_shape=(*[pltpu.HBM(a.shape, a.dtype) for a in srcs], *[pltpu.HBM(a.shape, a.dtype) for a in lands]),
        in_specs=[HBM_SPEC] * (n_in + n_grp) + [SEM_SPEC, SEM_SPEC, ANY],
        out_specs=tuple([HBM_SPEC] * (n_in + n_grp)),
        input_output_aliases={i: i for i in range(n_in + n_grp)},
        compiler_params=pltpu.CompilerParams(has_side_effects=EFFECT),
    )(*srcs, *lands, send_sems, recv_sems, after)
    return outs[n_in:]


def _tie(tree, token):
    tree, _ = lax.optimization_barrier((tree, token))
    return tree


def _pack(arrs):
    flat = jnp.concatenate([a.reshape(-1).astype(F32) for a in arrs])
    n = flat.shape[0]
    rows = -(-n // 1024) * 8
    return jnp.pad(flat, (0, rows * 128 - n)).reshape(rows, 128)


def _unpack(buf, shapes, lead=()):
    flat = buf.reshape(lead + (-1,))
    out, off = [], 0
    for s in shapes:
        n = math.prod(s)
        out.append(flat[..., off:off + n].reshape(lead + tuple(s)))
        off += n
    return out


def _mm(name, a, a_spec, b, b_spec, out_sds, o_spec, grid, contract, nk=1):
    o_blk = tuple(d for d in o_spec.block_shape if d is not None)

    def body(a_ref, b_ref, o_ref, *acc):
        r = _dot(a_ref[...], b_ref[...], contract)
        if nk == 1:
            o_ref[...] = r.astype(o_ref.dtype)
        else:
            k = pl.program_id(len(grid) - 1)

            @pl.when(k == 0)
            def _():
                acc[0][...] = r

            @pl.when(k > 0)
            def _():
                acc[0][...] += r

            @pl.when(k == nk - 1)
            def _():
                o_ref[...] = acc[0][...].astype(o_ref.dtype)

    sem = ("parallel",) * (len(grid) - 1) + (("arbitrary",) if nk > 1 else ("parallel",))
    return pl.pallas_call(
        body, name=name, out_shape=out_sds, grid=grid, in_specs=[a_spec, b_spec], out_specs=o_spec,
        scratch_shapes=[pltpu.VMEM(o_blk, F32)] if nk > 1 else [], compiler_params=_cp(*sem))(a, b)


def _tile(n, want):
    t = min(n, want)
    assert n % t == 0, (n, t)
    return t


def _mm_nn(name, a, b, out_dtype=F32, tm=512, tn=512):
    (M, K), N = a.shape, b.shape[1]
    tm, tn = _tile(M, tm), _tile(N, tn)
    return _mm(name, a, pl.BlockSpec((tm, K), lambda i, j: (i, 0)), b, pl.BlockSpec((K, tn), lambda i, j: (0, j)),
               jax.ShapeDtypeStruct((M, N), out_dtype), pl.BlockSpec((tm, tn), lambda i, j: (i, j)),
               (M // tm, N // tn), "nn")


def _mm_nt(name, a, b, out_dtype=F32, tm=512, tn=512):
    (M, K), N = a.shape, b.shape[0]
    tm, tn = _tile(M, tm), _tile(N, tn)
    return _mm(name, a, pl.BlockSpec((tm, K), lambda i, j: (i, 0)), b, pl.BlockSpec((tn, K), lambda i, j: (j, 0)),
               jax.ShapeDtypeStruct((M, N), out_dtype), pl.BlockSpec((tm, tn), lambda i, j: (i, j)),
               (M // tm, N // tn), "nt")


def _mm_tn(name, a, b, out_dtype=F32, tm=512, tn=512):
    (K, M), N = a.shape, b.shape[1]
    tm, tn = _tile(M, tm), _tile(N, tn)
    return _mm(name, a, pl.BlockSpec((K, tm), lambda i, j: (0, i)), b, pl.BlockSpec((K, tn), lambda i, j: (0, j)),
               jax.ShapeDtypeStruct((M, N), out_dtype), pl.BlockSpec((tm, tn), lambda i, j: (i, j)),
               (M // tm, N // tn), "tn")


def _mm_cols(name, a, w, out_dtype=F32, tm=512):
    (M, K), (J, _, n) = a.shape, w.shape
    tm = _tile(M, tm)
    return _mm(name, a, pl.BlockSpec((tm, K), lambda j, i: (i, 0)), w, pl.BlockSpec((None, K, n), lambda j, i: (j, 0, 0)),
               jax.ShapeDtypeStruct((J, M, n), out_dtype), pl.BlockSpec((None, tm, n), lambda j, i: (j, i, 0)),
               (J, M // tm), "nn")


def _mm_cols_dx(name, d, w, out_dtype=F32, tm=512):
    (J, M, n), K = d.shape, w.shape[1]
    tm = _tile(M, tm)
    return _mm(name, d, pl.BlockSpec((None, tm, n), lambda i, j: (j, i, 0)), w, pl.BlockSpec((None, K, n), lambda i, j: (j, 0, 0)),
               jax.ShapeDtypeStruct((M, K), out_dtype), pl.BlockSpec((tm, K), lambda i, j: (i, 0)),
               (M // tm, J), "nt", nk=J)


def _mm_cols_dw(name, a, d, out_dtype=F32, tk=512):
    (M, K), (J, _, n) = a.shape, d.shape
    tk = _tile(K, tk)
    return _mm(name, a, pl.BlockSpec((M, tk), lambda j, i: (0, i)), d, pl.BlockSpec((None, M, n), lambda j, i: (j, 0, 0)),
               jax.ShapeDtypeStruct((J, K, n), out_dtype), pl.BlockSpec((None, tk, n), lambda j, i: (j, i, 0)),
               (J, K // tk), "tn")


def _mm_rows_resid(name, a, w, resid, gate, tm=512, tn=512):
    (Q, M, k), N = a.shape, w.shape[2]
    tm, tn = _tile(M, tm), _tile(N, tn)

    def body(a_ref, w_ref, r_ref, g_ref, y_ref, x_ref, acc):
        q = pl.program_id(2)
        r = _dot(a_ref[...], w_ref[...], "nn")

        @pl.when(q == 0)
        def _():
            acc[...] = r

        @pl.when(q > 0)
        def _():
            acc[...] += r

        @pl.when(q == Q - 1)
        def _():
            y = acc[...]
            y_ref[...] = y
            x_ref[...] = r_ref[...] + g_ref[...] * y

    return pl.pallas_call(
        body, name=name, grid=(M // tm, N // tn, Q),
        out_shape=(jax.ShapeDtypeStruct((M, N), F32), jax.ShapeDtypeStruct((M, N), F32)),
        in_specs=[pl.BlockSpec((None, tm, k), lambda i, j, q: (q, i, 0)), pl.BlockSpec((None, k, tn), lambda i, j, q: (q, 0, j)),
                  pl.BlockSpec((tm, tn), lambda i, j, q: (i, j)), pl.BlockSpec((1, tn), lambda i, j, q: (0, j))],
        out_specs=(pl.BlockSpec((tm, tn), lambda i, j, q: (i, j)), pl.BlockSpec((tm, tn), lambda i, j, q: (i, j))),
        scratch_shapes=[pltpu.VMEM((tm, tn), F32)], compiler_params=_cp("parallel", "parallel", "arbitrary"))(a, w, resid, gate)


def _mm_rows_dx(name, d, w, out_dtype=F32, tm=512):
    (M, N), (Q, k, _) = d.shape, w.shape
    tm = _tile(M, tm)
    return _mm(name, d, pl.BlockSpec((tm, N), lambda q, i: (i, 0)), w, pl.BlockSpec((None, k, N), lambda q, i: (q, 0, 0)),
               jax.ShapeDtypeStruct((Q, M, k), out_dtype), pl.BlockSpec((None, tm, k), lambda q, i: (q, i, 0)),
               (Q, M // tm), "nt")


def _mm_rows_dw(name, a, d, out_dtype=F32, tn=512):
    (Q, M, k), N = a.shape, d.shape[1]
    tn = _tile(N, tn)
    return _mm(name, a, pl.BlockSpec((None, M, k), lambda q, j: (q, 0, 0)), d, pl.BlockSpec((M, tn), lambda q, j: (0, j)),
               jax.ShapeDtypeStruct((Q, k, N), out_dtype), pl.BlockSpec((None, k, tn), lambda q, j: (q, 0, j)),
               (Q, N // tn), "tn")


def _silu(v):
    return v * jax.nn.sigmoid(v)


def _ada_fwd(c16, ada_w):
    L, D, n = ada_w.shape

    def body(c_ref, w_ref, o_ref):
        o_ref[...] = _dot(_silu(c_ref[...]), w_ref[...], "nn")

    return pl.pallas_call(
        body, name="ada_fwd", grid=(L,), out_shape=jax.ShapeDtypeStruct((L, 16, n), F32),
        in_specs=[pl.BlockSpec((16, D), lambda l: (0, 0)), pl.BlockSpec((None, D, n), lambda l: (l, 0, 0))],
        out_specs=pl.BlockSpec((None, 16, n), lambda l: (l, 0, 0)), compiler_params=_cp("parallel"))(c16, ada_w)


def _ada_bwd(c16, dmod16):
    L, _, n = dmod16.shape
    D = c16.shape[1]

    def body(c_ref, d_ref, o_ref):
        o_ref[...] = _dot(_silu(c_ref[...]), d_ref[...], "tn")

    return pl.pallas_call(
        body, name="ada_bwd", grid=(L,), out_shape=jax.ShapeDtypeStruct((L, D, n), F32),
        in_specs=[pl.BlockSpec((16, D), lambda l: (0, 0)), pl.BlockSpec((None, 16, n), lambda l: (l, 0, 0))],
        out_specs=pl.BlockSpec((None, D, n), lambda l: (l, 0, 0)), compiler_params=_cp("parallel"))(c16, dmod16)


def _row_spec(tr, n):
    return pl.BlockSpec((tr, n), lambda i: (i, 0))


def _vec_spec(n):
    return pl.BlockSpec((1, n), lambda i: (0, 0))


def _rmsmod_fwd(name, x, g, sc, sh, tr=256):
    S, D = x.shape

    def body(x_ref, g_ref, sc_ref, sh_ref, h_ref):
        xv = x_ref[...]
        rstd = lax.rsqrt(jnp.mean(xv * xv, axis=-1, keepdims=True) + EPS)
        y = xv * rstd * g_ref[...]
        h_ref[...] = (y * (1.0 + sc_ref[...]) + sh_ref[...]).astype(h_ref.dtype)

    return pl.pallas_call(
        body, name=name, grid=(S // tr,), out_shape=jax.ShapeDtypeStruct((S, D), _MXU_DTYPE),
        in_specs=[_row_spec(tr, D), _vec_spec(D), _vec_spec(D), _vec_spec(D)], out_specs=_row_spec(tr, D),
        compiler_params=_cp("parallel"))(x, g, sc, sh)


def _acc_rows(ref, val, first):
    s = jnp.sum(val, axis=0, keepdims=True)

    @pl.when(first)
    def _():
        ref[...] = s

    @pl.when(jnp.logical_not(first))
    def _():
        ref[...] += s


def _rmsmod_bwd(name, x, g, sc, dh, dres, tr=256):
    S, D = x.shape

    def body(x_ref, g_ref, sc_ref, dh_ref, dres_ref, dx_ref, dg_ref, dsc_ref, dsh_ref):
        first = pl.program_id(0) == 0
        xv, dh_v, gv = x_ref[...], dh_ref[...], g_ref[...]
        rstd = lax.rsqrt(jnp.mean(xv * xv, axis=-1, keepdims=True) + EPS)
        xhat = xv * rstd
        _acc_rows(dsh_ref, dh_v, first)
        _acc_rows(dsc_ref, dh_v * (xhat * gv), first)
        dyg = dh_v * (1.0 + sc_ref[...])
        _acc_rows(dg_ref, dyg * xhat, first)
        dxhat = dyg * gv
        dx_ref[...] = dres_ref[...] + rstd * (dxhat - xhat * jnp.mean(dxhat * xhat, axis=-1, keepdims=True))

    vec = jax.ShapeDtypeStruct((1, D), F32)
    return pl.pallas_call(
        body, name=name, grid=(S // tr,), out_shape=(jax.ShapeDtypeStruct((S, D), F32), vec, vec, vec),
        in_specs=[_row_spec(tr, D), _vec_spec(D), _vec_spec(D), _row_spec(tr, D), _row_spec(tr, D)],
        out_specs=(_row_spec(tr, D), _vec_spec(D), _vec_spec(D), _vec_spec(D)),
        compiler_params=_cp("arbitrary"))(x, g, sc, dh, dres)


def _loss_head(x, g, target, tr=256):
    S, D = x.shape

    def body(x_ref, g_ref, t_ref, loss_ref, dx_ref, dg_ref):
        first = pl.program_id(0) == 0
        xv, gv = x_ref[...], g_ref[...]
        rstd = lax.rsqrt(jnp.mean(xv * xv, axis=-1, keepdims=True) + EPS)
        xhat = xv * rstd
        err = xhat * gv - t_ref[...]
        part = 0.5 * jnp.sum(jnp.mean(err * err, axis=-1, keepdims=True), axis=0, keepdims=True)

        @pl.when(first)
        def _():
            loss_ref[...] = part

        @pl.when(jnp.logical_not(first))
        def _():
            loss_ref[...] += part

        dout = err * (1.0 / D)
        _acc_rows(dg_ref, dout * xhat, first)
        dxhat = dout * gv
        dx_ref[...] = rstd * (dxhat - xhat * jnp.mean(dxhat * xhat, axis=-1, keepdims=True))

    return pl.pallas_call(
        body, name="loss_head", grid=(S // tr,),
        out_shape=(jax.ShapeDtypeStruct((1, 1), F32), jax.ShapeDtypeStruct((S, D), F32), jax.ShapeDtypeStruct((1, D), F32)),
        in_specs=[_row_spec(tr, D), _vec_spec(D), _row_spec(tr, D)],
        out_specs=(pl.BlockSpec((1, 1), lambda i: (0, 0)), _row_spec(tr, D), _vec_spec(D)),
        compiler_params=_cp("arbitrary"))(x, g, target)


def _gate_bwd(name, dx, y, gate, tr=256):
    S, D = dx.shape

    def body(dx_ref, y_ref, g_ref, dy_ref, dg_ref):
        dxv = dx_ref[...]
        dy_ref[...] = (g_ref[...] * dxv).astype(dy_ref.dtype)
        _acc_rows(dg_ref, dxv * y_ref[...], pl.program_id(0) == 0)

    return pl.pallas_call(
        body, name=name, grid=(S // tr,),
        out_shape=(jax.ShapeDtypeStruct((S, D), _MXU_DTYPE), jax.ShapeDtypeStruct((1, D), F32)),
        in_specs=[_row_spec(tr, D), _row_spec(tr, D), _vec_spec(D)], out_specs=(_row_spec(tr, D), _vec_spec(D)),
        compiler_params=_cp("arbitrary"))(dx, y, gate)


def _shift_down(v, k):
    t = lax.broadcasted_iota(jnp.int32, v.shape, 0)
    return jnp.where(t >= k, pltpu.roll(v, k, axis=0), 0.0)


def _shift_up(v, k):
    n = v.shape[0]
    t = lax.broadcasted_iota(jnp.int32, v.shape, 0)
    return jnp.where(t < n - k, pltpu.roll(v, n - k, axis=0), 0.0)


def _window_sum(p, w, shift):
    s, k = p, 1
    while k < w:
        s = s + shift(s, k)
        k *= 2
    return s


def _pool_count(shape, w):
    t = lax.broadcasted_iota(jnp.int32, shape, 0)
    return jnp.minimum(t + 1, w).astype(F32)


def _ab_specs(S):
    zs = [pl.BlockSpec((None, S, 128), functools.partial(lambda g, q: (2 * q + g // 2, 0, g % 2), q=q)) for q in range(4)]
    return zs


def _ab_mix_fwd(z8, conv_w, mix_w, scale):
    S = z8.shape[1]

    def body(b_ref, c_ref, a_ref, p_ref, w_ref, mix_ref, sc_ref, y_ref):
        g = pl.program_id(0)
        cg = c_ref[...] * a_ref[...]
        w = w_ref[...]
        conv = w[0:1] * _shift_down(cg, 2) + w[1:2] * _shift_down(cg, 1) + w[2:3] * cg
        y_ref[0] = (b_ref[...] * conv).astype(y_ref.dtype)
        for gg, win in enumerate(POOL_WINDOWS):
            @pl.when(g == gg)
            def _(win=win):
                p = p_ref[...]
                pooled = _window_sum(p, win, _shift_down) / _pool_count(p.shape, win) - p
                y_ref[1] = (_dot(pooled, mix_ref[...], "nn") * sc_ref[...]).astype(y_ref.dtype)

    return pl.pallas_call(
        body, name="ab_mix_fwd", grid=(4,), out_shape=jax.ShapeDtypeStruct((2, S, 512), _MXU_DTYPE),
        in_specs=_ab_specs(S) + [pl.BlockSpec((3, 128), lambda g: (0, g)), pl.BlockSpec((None, 128, 128), lambda g: (g, 0, 0)),
                                 pl.BlockSpec((1, 128), lambda g: (0, g))],
        out_specs=pl.BlockSpec((2, S, 128), lambda g: (0, 0, g)), compiler_params=_cp("parallel"))(z8, z8, z8, z8, conv_w, mix_w, scale)


def _ab_mix_bwd(z8, dycat2, conv_w, mix_w, scale):
    S = z8.shape[1]

    def body(b_ref, c_ref, a_ref, p_ref, dy_ref, w_ref, mix_ref, sc_ref, dz_ref, dw_ref, dmix_ref, dsc_ref):
        g = pl.program_id(0)
        bv, cv, av, w = b_ref[...], c_ref[...], a_ref[...], w_ref[...]
        dya = dy_ref[0]
        cg = cv * av
        cg1, cg2 = _shift_down(cg, 1), _shift_down(cg, 2)
        conv = w[0:1] * cg2 + w[1:2] * cg1 + w[2:3] * cg
        dz_ref[0] = (dya * conv).astype(dz_ref.dtype)
        dconv = dya * bv
        dcg = w[2:3] * dconv + w[1:2] * _shift_up(dconv, 1) + w[0:1] * _shift_up(dconv, 2)
        dz_ref[1] = (dcg * av).astype(dz_ref.dtype)
        dz_ref[2] = (dcg * cv).astype(dz_ref.dtype)
        dw_ref[0:1, :] = jnp.sum(dconv * cg2, axis=0, keepdims=True)
        dw_ref[1:2, :] = jnp.sum(dconv * cg1, axis=0, keepdims=True)
        dw_ref[2:3, :] = jnp.sum(dconv * cg, axis=0, keepdims=True)
        for gg, win in enumerate(POOL_WINDOWS):
            @pl.when(g == gg)
            def _(win=win):
                p, dyb, mix = p_ref[...], dy_ref[1], mix_ref[...]
                cnt = _pool_count(p.shape, win)
                pooled = _window_sum(p, win, _shift_down) / cnt - p
                dsc_ref[...] = jnp.sum(dyb * _dot(pooled, mix, "nn"), axis=0, keepdims=True)
                dmixed = dyb * sc_ref[...]
                dmix_ref[...] = _dot(pooled, dmixed, "tn")
                dpooled = _dot(dmixed, mix, "nt")
                dz_ref[3] = (_window_sum(dpooled / cnt, win, _shift_up) - dpooled).astype(dz_ref.dtype)

    return pl.pallas_call(
        body, name="ab_mix_bwd", grid=(4,),
        out_shape=(jax.ShapeDtypeStruct((4, 2, S, 256), _MXU_DTYPE), jax.ShapeDtypeStruct((3, 512), F32),
                   jax.ShapeDtypeStruct((4, 128, 128), F32), jax.ShapeDtypeStruct((1, 512), F32)),
        in_specs=_ab_specs(S) + [pl.BlockSpec((2, S, 128), lambda g: (0, 0, g)), pl.BlockSpec((3, 128), lambda g: (0, g)),
                                 pl.BlockSpec((None, 128, 128), lambda g: (g, 0, 0)), pl.BlockSpec((1, 128), lambda g: (0, g))],
        out_specs=(pl.BlockSpec((4, None, S, 128), lambda g: (0, g // 2, 0, g % 2)), pl.BlockSpec((3, 128), lambda g: (0, g)),
                   pl.BlockSpec((None, 128, 128), lambda g: (g, 0, 0)), pl.BlockSpec((1, 128), lambda g: (0, g))),
        compiler_params=_cp("parallel"))(z8, z8, z8, z8, dycat2, conv_w, mix_w, scale)


HALO = 8


def _ffn_specs(S, n, tr):
    nb = S // HALO
    tile = pl.BlockSpec((2, None, tr, n), lambda j, i: (0, j, i, 0))
    prev = pl.BlockSpec((2, None, HALO, n), lambda j, i: (0, j, jnp.maximum(i * (tr // HALO) - 1, 0), 0))
    nxt = pl.BlockSpec((2, None, HALO, n), lambda j, i: (0, j, jnp.minimum((i + 1) * (tr // HALO), nb - 1), 0))
    cw = pl.BlockSpec((2, None, 3, n), lambda j, i: (0, j, 0, 0))
    return tile, prev, nxt, cw


def _conv_rows(ext, w, lo, tr):
    n = ext.shape[0]
    return (w[0:1] * pltpu.roll(ext, 2, axis=0)[lo:lo + tr] + w[1:2] * pltpu.roll(ext, 1, axis=0)[lo:lo + tr]
            + w[2:3] * ext[lo:lo + tr])


def _ffn_gate_fwd(name, u24, cw24, tr=256):
    _, J, S, n = u24.shape
    tile, prev, _, cw = _ffn_specs(S, n, tr)

    def body(u_ref, up_ref, w_ref, a_ref):
        keep = (pl.program_id(1) > 0).astype(F32)
        z = []
        for h in range(2):
            ext = jnp.concatenate([up_ref[h] * keep, u_ref[h]], axis=0)
            z.append(_conv_rows(ext, w_ref[h], HALO, tr))
        a_ref[...] = (_silu(z[0]) * z[1]).astype(a_ref.dtype)

    return pl.pallas_call(
        body, name=name, grid=(J, S // tr), out_shape=jax.ShapeDtypeStruct((J, S, n), _MXU_DTYPE),
        in_specs=[tile, prev, cw], out_specs=pl.BlockSpec((None, tr, n), lambda j, i: (j, i, 0)),
        compiler_params=_cp("parallel", "parallel"))(u24, u24, cw24)


def _ffn_gate_bwd(name, u24, cw24, da4, tr=256):
    _, J, S, n = u24.shape
    tile, prev, nxt, cw = _ffn_specs(S, n, tr)
    nb = S // HALO
    ext_rows = tr + 2 * HALO

    def body(u_ref, up_ref, un_ref, w_ref, da_ref, dan_ref, du_ref, dcw_ref):
        i = pl.program_id(1)
        first = i == 0
        keep_prev = (i > 0).astype(F32)
        keep_next = (i < S // tr - 1).astype(F32)
        ext = [jnp.concatenate([up_ref[h] * keep_prev, u_ref[h], un_ref[h]], axis=0) for h in range(2)]
        w = [w_ref[h] for h in range(2)]
        zg = _conv_rows(ext[0], w[0], HALO, tr + HALO)
        zu = _conv_rows(ext[1], w[1], HALO, tr + HALO)
        da = jnp.concatenate([da_ref[...], dan_ref[...] * keep_next], axis=0)
        sg = jax.nn.sigmoid(zg)
        dz = [da * zu * (sg * (1.0 + zg * (1.0 - sg))), da * (zg * sg)]
        m = tr + HALO
        for h in range(2):
            d = dz[h]
            du = w[h][2:3] * d[:tr] + w[h][1:2] * pltpu.roll(d, m - 1, axis=0)[:tr] + w[h][0:1] * pltpu.roll(d, m - 2, axis=0)[:tr]
            du_ref[h] = du.astype(du_ref.dtype)
            dt = d[:tr]
            e = ext[h]
            parts = [jnp.sum(dt * pltpu.roll(e, 2, axis=0)[HALO:HALO + tr], axis=0, keepdims=True),
                     jnp.sum(dt * pltpu.roll(e, 1, axis=0)[HALO:HALO + tr], axis=0, keepdims=True),
                     jnp.sum(dt * e[HALO:HALO + tr], axis=0, keepdims=True)]
            for k in range(3):
                @pl.when(first)
                def _(k=k, h=h):
                    dcw_ref[h, k:k + 1, :] = parts[k]

                @pl.when(jnp.logical_not(first))
                def _(k=k, h=h):
                    dcw_ref[h, k:k + 1, :] += parts[k]

    da_tile = pl.BlockSpec((None, tr, n), lambda j, i: (j, i, 0))
    da_next = pl.BlockSpec((None, HALO, n), lambda j, i: (j, jnp.minimum((i + 1) * (tr // HALO), nb - 1), 0))
    return pl.pallas_call(
        body, name=name, grid=(J, S // tr),
        out_shape=(jax.ShapeDtypeStruct((2, J, S, n), _MXU_DTYPE), jax.ShapeDtypeStruct((2, J, 3, n), F32)),
        in_specs=[tile, prev, nxt, cw, da_tile, da_next], out_specs=(tile, cw),
        compiler_params=_cp("parallel", "arbitrary"))(u24, u24, u24, cw24, da4, da4)


def _rms_rows(v, g):
    rstd = lax.rsqrt(jnp.mean(v * v, axis=-1, keepdims=True) + EPS)
    return v * rstd * g


def _rms_rows_bwd(v, g, dy):
    rstd = lax.rsqrt(jnp.mean(v * v, axis=-1, keepdims=True) + EPS)
    vhat = v * rstd
    dvhat = dy * g
    return rstd * (dvhat - vhat * jnp.mean(dvhat * vhat, axis=-1, keepdims=True)), dy * vhat


def _mla_prep_fwd(z, qg, kvg, tr=256):
    S = z.shape[0]

    def body(q_ref, kv_ref, qg_ref, kvg_ref, qn_ref, kvn_ref):
        qn_ref[...] = _rms_rows(q_ref[...], qg_ref[...]).astype(qn_ref.dtype)
        kvn_ref[...] = _rms_rows(kv_ref[...], kvg_ref[...]).astype(kvn_ref.dtype)

    return pl.pallas_call(
        body, name="mla_prep_fwd", grid=(S // tr,),
        out_shape=(jax.ShapeDtypeStruct((S, 256), _MXU_DTYPE), jax.ShapeDtypeStruct((S, 128), _MXU_DTYPE)),
        in_specs=[pl.BlockSpec((tr, 256), lambda i: (i, 0)), pl.BlockSpec((tr, 128), lambda i: (i, 2)), _vec_spec(256), _vec_spec(128)],
        out_specs=(_row_spec(tr, 256), _row_spec(tr, 128)), compiler_params=_cp("parallel"))(z, z, qg, kvg)


def _mla_prep_bwd(z, qg, kvg, dqn, dkvn, dkpe, duv, tr=256):
    S = z.shape[0]

    def body(q_ref, kv_ref, qg_ref, kvg_ref, dqn_ref, dkvn_ref, dkpe_ref, duv_ref, dz_ref, dqg_ref, dkvg_ref):
        first = pl.program_id(0) == 0
        dq, dqg = _rms_rows_bwd(q_ref[...], qg_ref[...], dqn_ref[...])
        dkv, dkvg = _rms_rows_bwd(kv_ref[...], kvg_ref[...], dkvn_ref[...])
        _acc_rows(dqg_ref, dqg, first)
        _acc_rows(dkvg_ref, dkvg, first)
        dz_ref[:, 0:256] = dq.astype(dz_ref.dtype)
        dz_ref[:, 256:384] = dkv.astype(dz_ref.dtype)
        dz_ref[:, 384:512] = dkpe_ref[...].astype(dz_ref.dtype)
        dz_ref[:, 512:1536] = duv_ref[...].astype(dz_ref.dtype)

    return pl.pallas_call(
        body, name="mla_prep_bwd", grid=(S // tr,),
        out_shape=(jax.ShapeDtypeStruct((S, 1536), _MXU_DTYPE), jax.ShapeDtypeStruct((1, 256), F32), jax.ShapeDtypeStruct((1, 128), F32)),
        in_specs=[pl.BlockSpec((tr, 256), lambda i: (i, 0)), pl.BlockSpec((tr, 128), lambda i: (i, 2)), _vec_spec(256), _vec_spec(128),
                  _row_spec(tr, 256), _row_spec(tr, 128), _row_spec(tr, 128), _row_spec(tr, 1024)],
        out_specs=(_row_spec(tr, 1536), _vec_spec(256), _vec_spec(128)),
        compiler_params=_cp("arbitrary"))(z, z, qg, kvg, dqn, dkvn, dkpe, duv)


def _rope(v, cos, sa, sb):
    return v * cos + pltpu.roll(v, 112, axis=1) * sa + pltpu.roll(v, 16, axis=1) * sb


def _rope_t(d, cos, sa, sb):
    return d * cos + pltpu.roll(d * sa, 16, axis=1) + pltpu.roll(d * sb, 112, axis=1)


def _rope_fwd(qraw, kvall, z, cosq, cosk, sa, sb, tr=256):
    S = qraw.shape[0]

    def body(q_ref, k_ref, v_ref, kpe_ref, cq_ref, ck_ref, sa_ref, sb_ref, qo_ref, ko_ref, vo_ref):
        cq, ck, sa_v, sb_v = cq_ref[...], ck_ref[...], sa_ref[...], sb_ref[...]
        kpe = _rope(kpe_ref[...], ck, sa_v, sb_v)
        for h in range(8):
            cols = slice(128 * h, 128 * h + 128)
            qo_ref[:, cols] = _rope(q_ref[:, cols], cq, sa_v, sb_v).astype(qo_ref.dtype)
            ko_ref[:, cols] = (k_ref[:, cols] + kpe).astype(ko_ref.dtype)
        vo_ref[...] = v_ref[...].astype(vo_ref.dtype)

    tab = _row_spec(tr, 128)
    return pl.pallas_call(
        body, name="rope_fwd", grid=(S // tr,),
        out_shape=(jax.ShapeDtypeStruct((S, 1024), _MXU_DTYPE), jax.ShapeDtypeStruct((S, 1024), _MXU_DTYPE),
                   jax.ShapeDtypeStruct((S, 512), _MXU_DTYPE)),
        in_specs=[_row_spec(tr, 1024), pl.BlockSpec((tr, 1024), lambda i: (i, 0)), pl.BlockSpec((tr, 512), lambda i: (i, 2)),
                  pl.BlockSpec((tr, 128), lambda i: (i, 3)), tab, tab, tab, tab],
        out_specs=(_row_spec(tr, 1024), _row_spec(tr, 1024), _row_spec(tr, 512)),
        compiler_params=_cp("parallel"))(qraw, kvall, kvall, z, cosq, cosk, sa, sb)


def _rope_bwd(dq, dk, dv, cosq, cosk, sa, sb, tr=256):
    S = dq.shape[0]

    def body(dq_ref, dk_ref, dv_ref, cq_ref, ck_ref, sa_ref, sb_ref, dqo_ref, dkv_ref, dkpe_ref):
        cq, ck, sa_v, sb_v = cq_ref[...], ck_ref[...], sa_ref[...], sb_ref[...]
        tot = jnp.zeros((tr, 128), F32)
        for h in range(8):
            cols = slice(128 * h, 128 * h + 128)
            dqo_ref[:, cols] = _rope_t(dq_ref[:, cols], cq, sa_v, sb_v).astype(dqo_ref.dtype)
            dkh = dk_ref[:, cols]
            tot = tot + dkh
            dkv_ref[:, cols] = dkh.astype(dkv_ref.dtype)
        dkv_ref[:, 1024:1536] = dv_ref[...].astype(dkv_ref.dtype)
        dkpe_ref[...] = _rope_t(tot, ck, sa_v, sb_v)

    tab = _row_spec(tr, 128)
    return pl.pallas_call(
        body, name="rope_bwd", grid=(S // tr,),
        out_shape=(jax.ShapeDtypeStruct((S, 1024), _MXU_DTYPE), jax.ShapeDtypeStruct((S, 1536), _MXU_DTYPE),
                   jax.ShapeDtypeStruct((S, 128), F32)),
        in_specs=[_row_spec(tr, 1024), _row_spec(tr, 1024), _row_spec(tr, 512), tab, tab, tab, tab],
        out_specs=(_row_spec(tr, 1024), _row_spec(tr, 1536), _row_spec(tr, 128)),
        compiler_params=_cp("parallel"))(dq, dk, dv, cosq, cosk, sa, sb)


NEG = -1e30


def _attn_fwd(q, k, v, tq=256, tk=256):
    S = q.shape[0]

    def body(q_ref, k_ref, v_ref, o_ref, lse_ref):
        i = pl.program_id(1)
        row = i * tq + lax.broadcasted_iota(jnp.int32, (tq, tk), 0)
        qs = [q_ref[:, 0:128], q_ref[:, 128:256]]

        def step(kb, carry):
            start = pl.multiple_of(kb * tk, tk)
            col = start + lax.broadcasted_iota(jnp.int32, (tq, tk), 1)
            vv = v_ref[pl.ds(start, tk), :]
            out = []
            for h in range(2):
                m, l, acc = carry[3 * h:3 * h + 3]
                s = _dot(qs[h], k_ref[pl.ds(start, tk), 128 * h:128 * h + 128], "nt") * ATTN_SCALE
                s = jnp.where(col <= row, s, NEG)
                m_new = jnp.maximum(m, jnp.max(s, axis=-1, keepdims=True))
                alpha = jnp.exp(m - m_new)
                p = jnp.exp(s - m_new)
                out += [m_new, alpha * l + jnp.sum(p, axis=-1, keepdims=True), alpha * acc + _dot(p, vv, "nn")]
            return tuple(out)

        init = (jnp.full((tq, 1), NEG, F32), jnp.zeros((tq, 1), F32), jnp.zeros((tq, 128), F32)) * 2
        ma, la, acca, mb, lb, accb = lax.fori_loop(0, (i * tq + tq) // tk, step, init)
        lane = lax.broadcasted_iota(jnp.int32, (tq, 128), 1)
        o_ref[...] = jnp.where(lane < 64, acca / la, accb / lb)
        lse_ref[...] = jnp.where(lane < 64, ma + jnp.log(la), mb + jnp.log(lb))

    return pl.pallas_call(
        body, name="attn_fwd", grid=(4, S // tq),
        out_shape=(jax.ShapeDtypeStruct((S, 512), F32), jax.ShapeDtypeStruct((4, S, 128), F32)),
        in_specs=[pl.BlockSpec((tq, 256), lambda p, i: (i, p)), pl.BlockSpec((S, 256), lambda p, i: (0, p)),
                  pl.BlockSpec((S, 128), lambda p, i: (0, p))],
        out_specs=(pl.BlockSpec((tq, 128), lambda p, i: (i, p)), pl.BlockSpec((None, tq, 128), lambda p, i: (p, i, 0))),
        compiler_params=_cp("parallel", "parallel"))(q, k, v)


def _attn_bwd(q, k, v, o, lse, dycat2, tq=256, tk=256):
    S = q.shape[0]

    def body(q_ref, k_ref, v_ref, o_ref, lse_ref, do_ref, dq_ref, dk_ref, dv_ref):
        j = pl.program_id(1)

        @pl.when(j == 0)
        def _():
            dq_ref[...] = jnp.zeros_like(dq_ref)

        col = j * tk + lax.broadcasted_iota(jnp.int32, (tq, tk), 1)
        lane = lax.broadcasted_iota(jnp.int32, (tq, 128), 1)
        ks = [k_ref[:, 0:128], k_ref[:, 128:256]]
        vv = v_ref[...]

        def step(qb, carry):
            dka, dkb, dvp = carry
            start = pl.multiple_of(qb * tq, tq)
            rows = pl.ds(start, tq)
            row = start + lax.broadcasted_iota(jnp.int32, (tq, tk), 0)
            do, lse_v = do_ref[rows, :], lse_ref[rows, :]
            prod = do * o_ref[rows, :]
            dks = [dka, dkb]
            for h in range(2):
                mine = (lane < 64) if h == 0 else (lane >= 64)
                delta = jnp.sum(jnp.where(mine, prod, 0.0), axis=-1, keepdims=True)
                do_h = jnp.where(mine, do, 0.0)
                qh = q_ref[rows, 128 * h:128 * h + 128]
                s = _dot(qh, ks[h], "nt") * ATTN_SCALE
                p = jnp.where(col <= row, jnp.exp(s - lse_v[:, 64 * h:64 * h + 1]), 0.0)
                dvp = dvp + _dot(p, do_h, "tn")
                ds = p * (_dot(do_h, vv, "nt") - delta) * ATTN_SCALE
                dq_ref[rows, 128 * h:128 * h + 128] += _dot(ds, ks[h], "nn")
                dks[h] = dks[h] + _dot(ds, qh, "tn")
            return dks[0], dks[1], dvp

        zero = jnp.zeros((tk, 128), F32)
        dka, dkb, dvp = lax.fori_loop((j * tk) // tq, S // tq, step, (zero, zero, zero))
        dk_ref[:, 0:128] = dka
        dk_ref[:, 128:256] = dkb
        dv_ref[...] = dvp

    return pl.pallas_call(
        body, name="attn_bwd", grid=(4, S // tk),
        out_shape=(jax.ShapeDtypeStruct((S, 1024), F32), jax.ShapeDtypeStruct((S, 1024), F32), jax.ShapeDtypeStruct((S, 512), F32)),
        in_specs=[pl.BlockSpec((S, 256), lambda p, j: (0, p)), pl.BlockSpec((tk, 256), lambda p, j: (j, p)),
                  pl.BlockSpec((tk, 128), lambda p, j: (j, p)), pl.BlockSpec((S, 128), lambda p, j: (0, p)),
                  pl.BlockSpec((None, S, 128), lambda p, j: (p, 0, 0)), pl.BlockSpec((None, S, 128), lambda p, j: (0, 0, p))],
        out_specs=(pl.BlockSpec((S, 256), lambda p, j: (0, p)), pl.BlockSpec((tk, 256), lambda p, j: (j, p)),
                   pl.BlockSpec((tk, 128), lambda p, j: (j, p))),
        compiler_params=_cp("parallel", "arbitrary"))(q, k, v, o, lse, dycat2)


CHUNK = 128
GELU_C = math.sqrt(2.0 / math.pi)


def _gelu(v):
    t = jnp.tanh(GELU_C * (v + 0.044715 * (v * v * v)))
    return v * (0.5 * (1.0 + t)), t


def _gelu_grad(v, t):
    return 0.5 * (1.0 + t) + v * (0.5 * (1.0 - t * t) * GELU_C * (1.0 + 3.0 * 0.044715 * v * v))


def _tril(w):
    r = lax.broadcasted_iota(jnp.int32, w.shape, 0)
    c = lax.broadcasted_iota(jnp.int32, w.shape, 1)
    return jnp.where(c <= r, w, 0.0)


def _layer_norm(v, g, b):
    xc = v - jnp.mean(v, axis=-1, keepdims=True)
    rstd = lax.rsqrt(jnp.mean(xc * xc, axis=-1, keepdims=True) + EPS)
    xhat = xc * rstd
    return xhat * g + b, xhat, rstd


def _sgu_fwd(z, o, ln_g, ln_b, w_s, b_st, tr=256):
    S = z.shape[0]

    def body(u_ref, v_ref, o_ref, g_ref, b_ref, ws_ref, bs_ref, y_ref):
        gu, _ = _gelu(u_ref[...])
        gv, _ = _gelu(v_ref[...])
        vln, _, _ = _layer_norm(gv, g_ref[...], b_ref[...])
        y_ref[0] = o_ref[...].astype(y_ref.dtype)
        for g in range(4):
            wt = _tril(ws_ref[g])
            cols = slice(128 * g, 128 * g + 128)
            for ch in range(tr // CHUNK):
                rows = slice(CHUNK * ch, CHUNK * ch + CHUNK)
                mixed = _dot(wt, vln[rows, cols], "nn") + bs_ref[:, g:g + 1]
                y_ref[1, rows, cols] = (gu[rows, cols] * mixed).astype(y_ref.dtype)

    return pl.pallas_call(
        body, name="sgu_fwd", grid=(S // tr,), out_shape=jax.ShapeDtypeStruct((2, S, 512), _MXU_DTYPE),
        in_specs=[pl.BlockSpec((tr, 512), lambda i: (i, 1)), pl.BlockSpec((tr, 512), lambda i: (i, 2)), _row_spec(tr, 512),
                  _vec_spec(512), _vec_spec(512), pl.BlockSpec((4, 128, 128), lambda i: (0, 0, 0)), pl.BlockSpec((128, 4), lambda i: (0, 0))],
        out_specs=pl.BlockSpec((2, tr, 512), lambda i: (0, i, 0)), compiler_params=_cp("parallel"))(z, z, o, ln_g, ln_b, w_s, b_st)


def _sgu_bwd(z, dycat2, ln_g, ln_b, w_s, b_st, tr=256):
    S = z.shape[0]

    def body(u_ref, v_ref, dy_ref, g_ref, b_ref, ws_ref, bs_ref, duv_ref, dg_ref, db_ref, dws_ref, dbs_ref):
        first = pl.program_id(0) == 0
        u_pre, v_pre = u_ref[...], v_ref[...]
        gu, tu = _gelu(u_pre)
        gv, tv = _gelu(v_pre)
        gain = g_ref[...]
        vln, xhat, rstd = _layer_norm(gv, gain, b_ref[...])

        @pl.when(first)
        def _():
            dws_ref[...] = jnp.zeros_like(dws_ref)
            dbs_ref[...] = jnp.zeros_like(dbs_ref)

        dvln_cols = []
        for g in range(4):
            wt = _tril(ws_ref[g])
            cols = slice(128 * g, 128 * g + 128)
            dmixed_sum = jnp.zeros((CHUNK, 128), F32)
            dw = jnp.zeros((CHUNK, CHUNK), F32)
            dvln_rows = []
            for ch in range(tr // CHUNK):
                rows = slice(CHUNK * ch, CHUNK * ch + CHUNK)
                vt = vln[rows, cols]
                mixed = _dot(wt, vt, "nn") + bs_ref[:, g:g + 1]
                dyd = dy_ref[rows, cols]
                duv_ref[rows, cols] = (dyd * mixed * _gelu_grad(u_pre[rows, cols], tu[rows, cols])).astype(duv_ref.dtype)
                dmixed = dyd * gu[rows, cols]
                dmixed_sum = dmixed_sum + dmixed
                dw = dw + _dot(dmixed, vt, "nt")
                dvln_rows.append(_dot(wt, dmixed, "tn"))
            dws_ref[g] += _tril(dw)
            dbs_ref[g:g + 1, :] += jnp.sum(dmixed_sum.T, axis=0, keepdims=True)
            dvln_cols.append(jnp.concatenate(dvln_rows, axis=0))
        dvln = jnp.concatenate(dvln_cols, axis=1)
        _acc_rows(dg_ref, dvln * xhat, first)
        _acc_rows(db_ref, dvln, first)
        dxhat = dvln * gain
        dgv = rstd * (dxhat - jnp.mean(dxhat, axis=-1, keepdims=True) - xhat * jnp.mean(dxhat * xhat, axis=-1, keepdims=True))
        duv_ref[:, 512:1024] = (dgv * _gelu_grad(v_pre, tv)).astype(duv_ref.dtype)

    return pl.pallas_call(
        body, name="sgu_bwd", grid=(S // tr,),
        out_shape=(jax.ShapeDtypeStruct((S, 1024), _MXU_DTYPE), jax.ShapeDtypeStruct((1, 512), F32), jax.ShapeDtypeStruct((1, 512), F32),
                   jax.ShapeDtypeStruct((4, 128, 128), F32), jax.ShapeDtypeStruct((4, 128), F32)),
        in_specs=[pl.BlockSpec((tr, 512), lambda i: (i, 1)), pl.BlockSpec((tr, 512), lambda i: (i, 2)),
                  pl.BlockSpec((None, tr, 512), lambda i: (1, i, 0)), _vec_spec(512), _vec_spec(512),
                  pl.BlockSpec((4, 128, 128), lambda i: (0, 0, 0)), pl.BlockSpec((128, 4), lambda i: (0, 0))],
        out_specs=(_row_spec(tr, 1024), _vec_spec(512), _vec_spec(512), pl.BlockSpec((4, 128, 128), lambda i: (0, 0, 0)),
                   pl.BlockSpec((4, 128), lambda i: (0, 0))),
        compiler_params=_cp("arbitrary"))(z, z, dycat2, ln_g, ln_b, w_s, b_st)


def _sum_parts(name, parts, tr=512):
    P, R, C = parts.shape
    tr = _tile(R, tr) if R % 8 == 0 else R

    def body(p_ref, o_ref):
        g = p_ref[0]
        for k in range(1, P):
            g = g + p_ref[k]
        o_ref[...] = g

    return pl.pallas_call(
        body, name=name, grid=(R // tr,), out_shape=jax.ShapeDtypeStruct((R, C), F32),
        in_specs=[pl.BlockSpec((P, tr, C), lambda i: (0, i, 0))], out_specs=_row_spec(tr, C),
        compiler_params=_cp("parallel"))(parts)


ADAMW_BLOCK_BYTES = 36 * 2 ** 20


def _adamw(name, w, m, v, parts):
    L, R, C = w.shape
    P = parts[0].shape[0]
    row_bytes = 2 * C * (7 * 4 + P * parts[0].dtype.itemsize)
    tr = R
    if R * row_bytes > ADAMW_BLOCK_BYTES:
        tr = next(t for t in (1024, 512, 256, 128, 64, 32, 16) if R % t == 0 and t * row_bytes <= ADAMW_BLOCK_BYTES)
    nr = R // tr
    c1 = 1.0 / (1.0 - ADAM_B1 ** ADAM_STEP)
    c2 = 1.0 / (1.0 - ADAM_B2 ** ADAM_STEP)

    def body(w_ref, m_ref, v_ref, *rest):
        p_refs, (g_ref, d_ref, mo_ref, vo_ref) = rest[:L], rest[L:]
        for ll in range(L):
            @pl.when(pl.program_id(0) == ll)
            def _(p_ref=p_refs[ll]):
                g = p_ref[0].astype(F32)
                for k in range(1, P):
                    g = g + p_ref[k].astype(F32)
                m2 = ADAM_B1 * m_ref[...] + (1.0 - ADAM_B1) * g
                v2 = ADAM_B2 * v_ref[...] + (1.0 - ADAM_B2) * (g * g)
                g_ref[...] = g
                mo_ref[...] = m2
                vo_ref[...] = v2
                d_ref[...] = -ADAM_LR * ((m2 * c1) / (jnp.sqrt(v2 * c2) + ADAM_EPS) + ADAM_WD * w_ref[...])

    def part_spec(ll):
        return pl.BlockSpec((P, tr, C), lambda l, i: (0, jnp.where(l == ll, i, jnp.where(l < ll, 0, nr - 1)), 0))

    full = pl.BlockSpec((None, tr, C), lambda l, i: (l, i, 0))
    sds = jax.ShapeDtypeStruct((L, R, C), F32)
    return pl.pallas_call(
        body, name=name, grid=(L, nr), out_shape=(sds, sds, sds, sds),
        in_specs=[full] * 3 + [part_spec(ll) for ll in range(L)],
        out_specs=(full,) * 4, compiler_params=_cp("arbitrary", "arbitrary"))(w, m, v, *parts)


def _rope_tables(positions):
    half = 16
    inv_freq = 10000.0 ** (-jnp.arange(half, dtype=F32) / half)
    ang = positions.astype(F32)[:, None] * inv_freq
    cos, sin = jnp.cos(ang), jnp.sin(ang)
    S = positions.shape[0]
    z16, z32, z64 = jnp.zeros((S, 16), F32), jnp.zeros((S, 32), F32), jnp.zeros((S, 64), F32)
    cosk = jnp.concatenate([z64, cos, cos, z32], axis=1)
    cosq = jnp.concatenate([jnp.ones((S, 64), F32), cos, cos, z32], axis=1)
    sa = jnp.concatenate([z64, -sin, z16, z32], axis=1)
    sb = jnp.concatenate([z64, z16, sin, z32], axis=1)
    return cosq, cosk, sa, sb


def _ffn_fwd(l, x, mod, n2g, w_up8, cw24, w_down4):
    sh, sc, gate = mod
    h = _rmsmod_fwd(f"ffn{l}_norm", x, n2g, sc, sh)
    u8 = _mm_cols(f"ffn{l}_up", h, w_up8)
    S, n = u8.shape[1], u8.shape[2]
    u24 = u8.reshape(2, 4, S, n)
    a4 = _ffn_gate_fwd(f"ffn{l}_gate", u24, cw24)
    f, x_new = _mm_rows_resid(f"ffn{l}_down", a4, w_down4, x, gate)
    return x_new, (x, h, u24, a4, f)


def _ffn_bwd(l, dx, saved, mod, n2g, w_up8, cw24, w_down4):
    sh, sc, gate = mod
    x, h, u24, a4, f = saved
    df, dgate = _gate_bwd(f"ffn{l}_gate_bwd", dx, f, gate)
    da4 = _mm_rows_dx(f"ffn{l}_down_dx", df, w_down4)
    dw_down4 = _mm_rows_dw(f"ffn{l}_down_dw", a4, df, out_dtype=WIRE_DTYPE)
    du24, dcw24 = _ffn_gate_bwd(f"ffn{l}_act_bwd", u24, cw24, da4)
    du8 = du24.reshape((8,) + du24.shape[2:])
    dw_up8 = _mm_cols_dw(f"ffn{l}_up_dw", h, du8, out_dtype=WIRE_DTYPE)
    sent, token = _exchange_start(f"scatter_ffn{l}", [[dw_up8], [dw_down4.reshape(8, 352, dw_down4.shape[2])]], scatter=True)
    dh = _mm_cols_dx(f"ffn{l}_up_dx", _tie(du8, token), w_up8)
    dx_new, dn2g, dsc, dsh = _rmsmod_bwd(f"ffn{l}_norm_bwd", x, n2g, sc, dh, dx)
    return dx_new, dict(sent=sent, cw24=dcw24, n2g=dn2g, mod=(dsh, dsc, dgate))


def kernel(x, c, positions, ada_w, ada_b, norm1_g, norm2_g, ab_w_in, a_conv_w, b_mix_w, b_scale, ab_w_out, cd_w_in, c_q_norm_g, c_w_uq, c_kv_norm_g, c_w_ukv, d_ln_g, d_ln_b, d_w_s, d_b_s, cd_w_out, ffn_w_up, ffn_conv_w, ffn_w_down, final_norm_g, loss_target, m_ada_w, m_ada_b, m_norm1_g, m_norm2_g, m_ab_w_in, m_a_conv_w, m_b_mix_w, m_b_scale, m_ab_w_out, m_cd_w_in, m_c_q_norm_g, m_c_w_uq, m_c_kv_norm_g, m_c_w_ukv, m_d_ln_g, m_d_ln_b, m_d_w_s, m_d_b_s, m_cd_w_out, m_ffn_w_up, m_ffn_conv_w, m_ffn_w_down, m_final_norm_g, v_ada_w, v_ada_b, v_norm1_g, v_norm2_g, v_ab_w_in, v_a_conv_w, v_b_mix_w, v_b_scale, v_ab_w_out, v_cd_w_in, v_c_q_norm_g, v_c_w_uq, v_c_kv_norm_g, v_c_w_ukv, v_d_ln_g, v_d_ln_b, v_d_w_s, v_d_b_s, v_cd_w_out, v_ffn_w_up, v_ffn_conv_w, v_ffn_w_down, v_final_norm_g):
    S, D = x.shape[1], x.shape[2]
    me = 4 * lax.axis_index("x") + 2 * lax.axis_index("y") + lax.axis_index("c")
    x0, target = x[0], loss_target[0]
    W = _MXU_DTYPE

    small_shapes = [(1024,), (3, 64), (32,), (64,), (64,), (2, 3, 704)]
    (g0,) = _exchange("gather_small", [[_pack([c, a_conv_w, c_q_norm_g, d_ln_g, d_ln_b, ffn_conv_w])]], scatter=False)
    c_all, aconv_s, qg_s, lng_s, lnb_s, fcw_s = _unpack(g0[:, 0], small_shapes, lead=(N_DEV,))
    conv_w = aconv_s.transpose(1, 0, 2).reshape(3, 512)
    qg, ln_g, ln_b = qg_s.reshape(1, 256), lng_s.reshape(1, 512), lnb_s.reshape(1, 512)
    cw24 = [fcw_s[:, l].reshape(2, 4, 3, 704) for l in range(2)]
    c16 = jnp.pad(c_all, ((0, 16 - N_DEV), (0, 0)))

    mod_cols = _ada_fwd(c16, ada_w)
    (g1,) = _exchange("gather_mod", [[_pack([mod_cols])]], scatter=False)
    mod_all = _unpack(g1[:, 0], [(2, 16, 768)], lead=(N_DEV,))[0]
    mod_mine = lax.dynamic_index_in_dim(mod_all, me, axis=2, keepdims=False)
    mod = mod_mine.transpose(1, 0, 2).reshape(2, 6 * D) + ada_b
    mods = [[mod[l, k * D:(k + 1) * D].reshape(1, D) for k in range(6)] for l in range(2)]

    sq = lambda g: g.reshape(g.shape[:1] + g.shape[2:])
    gw_ab, token = _exchange_start("gather_w_ab", _tie([[ab_w_in[0].astype(W)], [ab_w_out[0].astype(W)]], mod), scatter=False)
    gw_f0, token = _exchange_start("gather_w_ffn0", _tie([[ffn_w_up[0].astype(W)], [ffn_w_down[0].astype(W)]], token), scatter=False)
    gw_cd, token = _exchange_start("gather_w_cd", _tie([
        [cd_w_in[0].astype(W).reshape(1440, 128)], [c_w_uq[0].astype(W).reshape(192, 128)], [c_w_ukv[0].astype(W)],
        [cd_w_out[0].astype(W)]], token), scatter=False)
    gw_f1, token = _exchange_start("gather_w_ffn1", _tie([[ffn_w_up[1].astype(W)], [ffn_w_down[1].astype(W)]], token), scatter=False)
    x0 = _tie(x0, token)

    cosq, cosk, sa, sb = _rope_tables(positions[0])
    n1g = [norm1_g[l].reshape(1, D) for l in range(2)]
    n2g = [norm2_g[l].reshape(1, D) for l in range(2)]
    mix_w, scale = b_mix_w[0], b_scale
    kvg = c_kv_norm_g
    w_s, b_st = d_w_s[0], d_b_s[0].T

    sh1, sc1, g1m = mods[0][:3]
    h_ab = _rmsmod_fwd("ab_norm", x0, n1g[0], sc1, sh1)
    w_abin8, w_about = [sq(g) for g in _exchange_wait("wait_w_ab", gw_ab, h_ab)]
    w_about2 = w_about.reshape(2, 512, D)
    z8 = _mm_cols("ab_in", h_ab, w_abin8)
    ycat_ab = _ab_mix_fwd(z8, conv_w, mix_w, scale)
    y_ab, x1 = _mm_rows_resid("ab_out", ycat_ab, w_about2, x0, g1m)
    w_up0, w_dn0 = [sq(g) for g in _exchange_wait("wait_w_ffn0", gw_f0, x1)]
    w_up8, w_down4 = [w_up0, None], [w_dn0.reshape(4, 704, D), None]
    x2, ffn0_saved = _ffn_fwd(0, x1, mods[0][3:], n2g[0], w_up8[0], cw24[0], w_down4[0])

    w_cdin, w_uq, w_ukv, w_cdout = [sq(g) for g in _exchange_wait("wait_w_cd", gw_cd, x2)]
    w_cdout2 = w_cdout.reshape(2, 512, D)
    w_cd = w_cdin.reshape(8, D, 180).transpose(1, 0, 2).reshape(D, 1440)
    zc = lambda n: jnp.zeros((D, n), W)
    w_cd_pad = jnp.concatenate([w_cd[:, :384], zc(64), w_cd[:, 384:416], zc(32), w_cd[:, 416:]], axis=1)
    w_uq_pad = jnp.pad(w_uq.reshape(8, 256, 96).transpose(1, 0, 2), ((0, 0), (0, 0), (0, 32))).reshape(256, 1024)
    w_ukv_h = w_ukv.transpose(1, 0, 2)
    w_k_pad = jnp.pad(w_ukv_h[:, :, :64], ((0, 0), (0, 0), (0, 64))).reshape(128, 1024)
    w_kv_pad = jnp.concatenate([w_k_pad, w_ukv_h[:, :, 64:].reshape(128, 512)], axis=1)

    sh1, sc1, g1c = mods[1][:3]
    h_cd = _rmsmod_fwd("cd_norm", x2, n1g[1], sc1, sh1)
    z_cd = _mm_nn("cd_in", h_cd, w_cd_pad)
    qn, kvn = _mla_prep_fwd(z_cd, qg, kvg)
    qraw = _mm_nn("cd_uq", qn, w_uq_pad)
    kvall = _mm_nn("cd_ukv", kvn, w_kv_pad)
    q_r, k_r, v_r = _rope_fwd(qraw, kvall, z_cd, cosq, cosk, sa, sb)
    o, lse = _attn_fwd(q_r, k_r, v_r)
    ycat_cd = _sgu_fwd(z_cd, o, ln_g, ln_b, w_s, b_st)
    y_cd, x3 = _mm_rows_resid("cd_out", ycat_cd, w_cdout2, x2, g1c)
    w_up1, w_dn1 = [sq(g) for g in _exchange_wait("wait_w_ffn1", gw_f1, x3)]
    w_up8[1], w_down4[1] = w_up1, w_dn1.reshape(4, 704, D)
    x4, ffn1_saved = _ffn_fwd(1, x3, mods[1][3:], n2g[1], w_up8[1], cw24[1], w_down4[1])

    loss_local, dx4, dfg = _loss_head(x4, final_norm_g.reshape(1, D), target)

    dx3, gf1 = _ffn_bwd(1, dx4, ffn1_saved, mods[1][3:], n2g[1], w_up8[1], cw24[1], w_down4[1])

    dy, dg1c = _gate_bwd("cd_gate_bwd", dx3, y_cd, g1c)
    dycat = _mm_rows_dx("cd_out_dx", dy, w_cdout2)
    dw_cdout = _mm_rows_dw("cd_out_dw", ycat_cd, dy, out_dtype=WIRE_DTYPE)
    duv, dln_g, dln_b, dws, dbs = _sgu_bwd(z_cd, dycat, ln_g, ln_b, w_s, b_st)
    dq_r, dk_r, dv_r = _attn_bwd(q_r, k_r, v_r, o, lse, dycat)
    dqraw, dkvall, dkpe = _rope_bwd(dq_r, dk_r, dv_r, cosq, cosk, sa, sb)
    dqn = _mm_nt("cd_uq_dx", dqraw, w_uq_pad, tn=256)
    dkvn = _mm_nt("cd_ukv_dx", dkvall, w_kv_pad, tn=128)
    dw_uq_pad = _mm_tn("cd_uq_dw", qn, dqraw, tm=256)
    dw_kv_pad = _mm_tn("cd_ukv_dw", kvn, dkvall, tm=128)
    dz_cd, dqg, dkvg = _mla_prep_bwd(z_cd, qg, kvg, dqn, dkvn, dkpe, duv)
    dh_cd = _mm_nt("cd_in_dx", dz_cd, w_cd_pad)
    dw_cd_pad = _mm_tn("cd_in_dw", h_cd, dz_cd)
    dx2, dn1g_cd, dsc1_cd, dsh1_cd = _rmsmod_bwd("cd_norm_bwd", x2, n1g[1], sc1, dh_cd, dx3)
    dw_cd = jnp.concatenate([dw_cd_pad[:, :384], dw_cd_pad[:, 448:480], dw_cd_pad[:, 512:]], axis=1)
    dw_cd8 = dw_cd.reshape(D, 8, 180).transpose(1, 0, 2).reshape(8, 1440, 128).astype(WIRE_DTYPE)
    dw_uq8 = dw_uq_pad.reshape(256, 8, 128)[:, :, :96].transpose(1, 0, 2).reshape(8, 192, 128).astype(WIRE_DTYPE)
    dw_ukv8 = jnp.concatenate([dw_kv_pad[:, :1024].reshape(128, 8, 128)[:, :, :64], dw_kv_pad[:, 1024:].reshape(128, 8, 64)],
                              axis=2).transpose(1, 0, 2).astype(WIRE_DTYPE)
    sent_cd, token = _exchange_start("scatter_cd", [[dw_cd8], [dw_uq8], [dw_ukv8], [dw_cdout.reshape(8, 128, D)]], scatter=True)

    dx1, gf0 = _ffn_bwd(0, _tie(dx2, token), ffn0_saved, mods[0][3:], n2g[0], w_up8[0], cw24[0], w_down4[0])

    dy, dg1m = _gate_bwd("ab_gate_bwd", dx1, y_ab, g1m)
    dycat = _mm_rows_dx("ab_out_dx", dy, w_about2)
    dw_about = _mm_rows_dw("ab_out_dw", ycat_ab, dy, out_dtype=WIRE_DTYPE)
    dz8, dconv_w, dmix_w, dscale = _ab_mix_bwd(z8, dycat, conv_w, mix_w, scale)
    dz8 = dz8.reshape(8, S, 256)
    dw_abin8 = _mm_cols_dw("ab_in_dw", h_ab, dz8, out_dtype=WIRE_DTYPE)
    sent_ab, token = _exchange_start("scatter_ab", [[dw_abin8], [dw_about.reshape(8, 128, D)]], scatter=True)
    dh_ab = _mm_cols_dx("ab_in_dx", _tie(dz8, token), w_abin8)
    dx0, dn1g_ab, dsc1_ab, dsh1_ab = _rmsmod_bwd("ab_norm_bwd", x0, n1g[0], mods[0][1], dh_ab, dx1)

    dmod = jnp.stack([jnp.concatenate([dsh1_ab, dsc1_ab, dg1m, *gf0["mod"]], axis=1)[0],
                      jnp.concatenate([dsh1_cd, dsc1_cd, dg1c, *gf1["mod"]], axis=1)[0]])
    rep_grads = [dmod, jnp.concatenate([dn1g_ab, dn1g_cd]), jnp.concatenate([gf0["n2g"], gf1["n2g"]]),
                 dmix_w, dscale, dkvg, dws, dbs, dfg]
    rep_names = ["ada_b", "norm1_g", "norm2_g", "b_mix_w", "b_scale", "c_kv_norm_g", "d_w_s", "d_b_s", "final_norm_g"]
    dfcw = jnp.stack([gf0["cw24"].reshape(8, 3, 704), gf1["cw24"].reshape(8, 3, 704)])
    shard_grads = [dconv_w, dqg, dln_g, dln_b, dfcw]
    rep_buf, shard_buf = _pack(rep_grads), _pack(shard_grads)
    r_rep = rep_buf.shape[0]
    (g2,) = _exchange("gather_small_grads", [[jnp.concatenate([rep_buf, shard_buf])]], scatter=False)
    g2 = g2[:, 0]

    dmod_all = g2[:, :2 * 6 * D // 128].reshape(N_DEV, 2, N_DEV, 768)
    dmod_cols = lax.dynamic_index_in_dim(dmod_all, me, axis=2, keepdims=False).transpose(1, 0, 2)
    g_ada_w = _ada_bwd(c16, jnp.pad(dmod_cols, ((0, 0), (0, 16 - N_DEV), (0, 0))))

    p_up1, p_dn1 = [sq(p) for p in _exchange_wait("wait_scatter_ffn1", gf1["sent"], dx0)]
    p_cdin, p_uq, p_ukv, p_cdout = [sq(p) for p in _exchange_wait("wait_scatter_cd", sent_cd, dx0)]
    p_up0, p_dn0 = [sq(p) for p in _exchange_wait("wait_scatter_ffn0", gf0["sent"], dx0)]
    p_abin, p_about = [sq(p) for p in _exchange_wait("wait_scatter_ab", sent_ab, dx0)]

    res = {}

    def update(name, w, m, v, parts, shape3d):
        outs = _adamw("adamw_" + name, w.reshape(shape3d), m.reshape(shape3d), v.reshape(shape3d),
                      [p.reshape((p.shape[0],) + shape3d[1:]) for p in parts])
        res[name] = [o_.reshape(w.shape) for o_ in outs]

    update("ada_w", ada_w, m_ada_w, v_ada_w, [g_ada_w[l][None] for l in range(2)], (2, D, 768))
    update("ab_w_in", ab_w_in, m_ab_w_in, v_ab_w_in, [p_abin], (1, D, 256))
    update("ab_w_out", ab_w_out, m_ab_w_out, v_ab_w_out, [p_about], (1, 128, D))
    update("cd_w_in", cd_w_in, m_cd_w_in, v_cd_w_in, [p_cdin], (1, 1440, 128))
    update("c_w_uq", c_w_uq, m_c_w_uq, v_c_w_uq, [p_uq], (1, 192, 128))
    update("c_w_ukv", c_w_ukv, m_c_w_ukv, v_c_w_ukv, [p_ukv], (1, 128, 128))
    update("cd_w_out", cd_w_out, m_cd_w_out, v_cd_w_out, [p_cdout], (1, 128, D))
    update("ffn_w_up", ffn_w_up, m_ffn_w_up, v_ffn_w_up, [p_up0, p_up1], (2, D, 704))
    update("ffn_w_down", ffn_w_down, m_ffn_w_down, v_ffn_w_down, [p_dn0, p_dn1], (2, 352, D))

    rep_w = dict(ada_b=(ada_b, m_ada_b, v_ada_b), norm1_g=(norm1_g, m_norm1_g, v_norm1_g), norm2_g=(norm2_g, m_norm2_g, v_norm2_g),
                 b_mix_w=(b_mix_w, m_b_mix_w, v_b_mix_w), b_scale=(b_scale, m_b_scale, v_b_scale),
                 c_kv_norm_g=(c_kv_norm_g, m_c_kv_norm_g, v_c_kv_norm_g), d_w_s=(d_w_s, m_d_w_s, v_d_w_s),
                 d_b_s=(d_b_s, m_d_b_s, v_d_b_s), final_norm_g=(final_norm_g, m_final_norm_g, v_final_norm_g))
    rep_packed = [_pack([rep_w[n][k] for n in rep_names]) for k in range(3)]
    rep_out = _adamw("adamw_replicated", *[a[None] for a in rep_packed], [g2[:, :r_rep]])
    rep_shapes = [rep_w[n][0].shape for n in rep_names]
    for k, n in enumerate(rep_names):
        res[n] = [_unpack(o_, rep_shapes)[k] for o_ in rep_out]

    shard_sum = _sum_parts("sum_small_grads", g2[:, r_rep:])
    g_conv, g_qg, g_lng, g_lnb, g_fcw = _unpack(shard_sum, [(3, 512), (256,), (512,), (512,), (2, 8, 3, 704)])
    mine = lambda a, n, axis: lax.dynamic_slice_in_dim(a, me * n, n, axis=axis)
    sh_names = ["a_conv_w", "c_q_norm_g", "d_ln_g", "d_ln_b", "ffn_conv_w"]
    sh_grads = [mine(g_conv, 64, 1), mine(g_qg, 32, 0), mine(g_lng, 64, 0), mine(g_lnb, 64, 0),
                lax.dynamic_index_in_dim(g_fcw, me, axis=1, keepdims=False)]
    sh_w = dict(a_conv_w=(a_conv_w, m_a_conv_w, v_a_conv_w), c_q_norm_g=(c_q_norm_g, m_c_q_norm_g, v_c_q_norm_g),
                d_ln_g=(d_ln_g, m_d_ln_g, v_d_ln_g), d_ln_b=(d_ln_b, m_d_ln_b, v_d_ln_b),
                ffn_conv_w=(ffn_conv_w, m_ffn_conv_w, v_ffn_conv_w))
    sh_packed = [_pack([sh_w[n][k] for n in sh_names]) for k in range(3)]
    sh_out = _adamw("adamw_small_shards", *[a[None] for a in sh_packed], [_pack(sh_grads)[None]])
    sh_shapes = [sh_w[n][0].shape for n in sh_names]
    for k, n in enumerate(sh_names):
        res[n] = [_unpack(o_, sh_shapes)[k] for o_ in sh_out]

    loss = lax.psum(loss_local[0, 0], ("x", "y", "c"))
    order = ["ada_w", "ada_b", "norm1_g", "norm2_g", "ab_w_in", "a_conv_w", "b_mix_w", "b_scale", "ab_w_out", "cd_w_in", "c_q_norm_g",
             "c_w_uq", "c_kv_norm_g", "c_w_ukv", "d_ln_g", "d_ln_b", "d_w_s", "d_b_s", "cd_w_out", "ffn_w_up", "ffn_conv_w",
             "ffn_w_down", "final_norm_g"]
    return (loss, dx0[None], *[res[n][0] for n in order], *[res[n][1] for n in order], *[res[n][2] for n in order],
            *[res[n][3] for n in order])
```

```python
import functools
import math

import jax
import jax.numpy as jnp
from jax import lax
from jax.experimental import pallas as pl
from jax.experimental.pallas import tpu as pltpu

F32 = jnp.float32
BF16 = jnp.bfloat16
_MXU_DTYPE = BF16
WIRE_DTYPE = BF16
_VMEM_LIMIT = 56 * 2 ** 20
N_DEV = 8
EPS = 1e-6
POOL_WINDOWS = (2, 4, 8, 16)
ATTN_SCALE = (64 + 32) ** -0.5
ADAM_LR, ADAM_B1, ADAM_B2, ADAM_EPS, ADAM_WD, ADAM_STEP = 0.001, 0.9, 0.999, 1e-08, 0.01, 10
MESH = pl.DeviceIdType.MESH
ANY = pl.BlockSpec(memory_space=pl.ANY)


def _cp(*sem):
    return pltpu.CompilerParams(dimension_semantics=sem, vmem_limit_bytes=_VMEM_LIMIT)


def _dot(a, b, contract):
    dn = {"nn": (((1,), (0,)), ((), ())), "nt": (((1,), (1,)), ((), ())), "tn": (((0,), (0,)), ((), ()))}[contract]
    return lax.dot_general(a.astype(_MXU_DTYPE), b.astype(_MXU_DTYPE), dn, preferred_element_type=F32)


def _my_position():
    x, y, c = lax.axis_index("x"), lax.axis_index("y"), lax.axis_index("c")
    return x, y, c, 4 * x + 2 * y + c


def _exchange(name, groups, scatter):
    flat = [a for g in groups for a in g]
    n_in, n_grp = len(flat), len(groups)
    out_shapes = []
    for g in groups:
        slab = g[0].shape[1:] if scatter else g[0].shape
        out_shapes.append(jax.ShapeDtypeStruct((N_DEV, len(g)) + tuple(slab), g[0].dtype))

    def body(*refs):
        ins, outs = refs[:n_in], refs[n_in:n_in + n_grp]
        send_sems, recv_sems, local_sems = refs[n_in + n_grp:]
        x, y, c, me = _my_position()
        i = 0
        for gi, g in enumerate(groups):
            for l in range(len(g)):
                src = ins[i]
                i += 1
                pltpu.make_async_copy(src.at[me] if scatter else src, outs[gi].at[me, l], local_sems.at[gi]).start()
                for k in range(1, N_DEV):
                    px = 1 - x if k & 4 else x
                    py = 1 - y if k & 2 else y
                    pc = 1 - c if k & 1 else c
                    peer = 4 * px + 2 * py + pc
                    pltpu.make_async_remote_copy(
                        src_ref=src.at[peer] if scatter else src, dst_ref=outs[gi].at[me, l],
                        send_sem=send_sems.at[gi], recv_sem=recv_sems.at[gi],
                        device_id=(px, py, pc), device_id_type=MESH).start()
        for gi in range(n_grp):
            mine = outs[gi].at[me]
            pltpu.make_async_copy(mine, mine, local_sems.at[gi]).wait()
            seven = outs[gi].at[pl.ds(0, N_DEV - 1)]
            w = pltpu.make_async_remote_copy(src_ref=seven, dst_ref=seven, send_sem=send_sems.at[gi],
                                             recv_sem=recv_sems.at[gi], device_id=(x, y, c), device_id_type=MESH)
            w.wait_send()
            w.wait_recv()

    return pl.pallas_call(
        body, name=name, out_shape=tuple(out_shapes),
        in_specs=[ANY] * n_in, out_specs=tuple([ANY] * n_grp),
        scratch_shapes=[pltpu.SemaphoreType.DMA((n_grp,)), pltpu.SemaphoreType.DMA((n_grp,)),
                        pltpu.SemaphoreType.DMA((n_grp,))],
        compiler_params=pltpu.CompilerParams(has_side_effects=True),
    )(*flat)


HBM_SPEC = pl.BlockSpec(memory_space=pltpu.HBM)
SEM_SPEC = pl.BlockSpec(memory_space=pltpu.SEMAPHORE)
EFFECT = pltpu.SideEffectType.DATAFLOW_SIDE_EFFECTING


def _group_shapes(groups, scatter):
    return [((N_DEV, len(g)) + tuple(g[0].shape[1:] if scatter else g[0].shape), g[0].dtype) for g in groups]


def _exchange_start(name, groups, scatter, after):
    flat = [pltpu.with_memory_space_constraint(a, pltpu.HBM) for g in groups for a in g]
    shapes = _group_shapes(groups, scatter)
    lands = [pltpu.with_memory_space_constraint(lax.empty(s, d), pltpu.HBM) for s, d in shapes]
    n_in, n_grp = len(flat), len(groups)

    def body(*refs):
        ins, land = refs[:n_in], refs[n_in:n_in + n_grp]
        send_sems, recv_sems = refs[n_in + n_grp + 1], refs[n_in + n_grp + 2]
        token = refs[-1]
        x, y, c, me = _my_position()
        i = 0
        for gi, g in enumerate(groups):
            for l in range(len(g)):
                src = ins[i]
                i += 1
                for k in range(1, N_DEV):
                    px = 1 - x if k & 4 else x
                    py = 1 - y if k & 2 else y
                    pc = 1 - c if k & 1 else c
                    peer = 4 * px + 2 * py + pc
                    pltpu.make_async_remote_copy(
                        src_ref=src.at[peer] if scatter else src, dst_ref=land[gi].at[me, l],
                        send_sem=send_sems.at[gi], recv_sem=recv_sems.at[gi],
                        device_id=(px, py, pc), device_id_type=MESH).start()
        token[...] = jnp.zeros_like(token)

    outs = pl.pallas_call(
        body, name=name,
        out_shape=(pltpu.SemaphoreType.DMA((n_grp,)), pltpu.SemaphoreType.DMA((n_grp,)),
                   *[pltpu.HBM(a.shape, a.dtype) for a in flat], *[pltpu.HBM(s, d) for s, d in shapes],
                   jax.ShapeDtypeStruct((8, 128), F32)),
        in_specs=[HBM_SPEC] * (n_in + n_grp) + [ANY],
        out_specs=(SEM_SPEC, SEM_SPEC, *[HBM_SPEC] * (n_in + n_grp), pl.BlockSpec(memory_space=pltpu.VMEM)),
        input_output_aliases={i: 2 + i for i in range(n_in + n_grp)},
        compiler_params=pltpu.CompilerParams(has_side_effects=EFFECT),
    )(*flat, *lands, after)
    handle = (outs[0], outs[1], outs[2:2 + n_in], outs[2 + n_in:2 + n_in + n_grp], [len(g) for g in groups], scatter)
    return handle, outs[-1]


def _exchange_wait(name, handle, after):
    send_sems, recv_sems, srcs, lands, sizes, scatter = handle
    n_in, n_grp = len(srcs), len(lands)

    def body(*refs):
        ins, land = refs[:n_in], refs[n_in:n_in + n_grp]
        send_ref, recv_ref, local_sems = refs[n_in + n_grp], refs[n_in + n_grp + 1], refs[-1]
        x, y, c, me = _my_position()
        i = 0
        for gi in range(n_grp):
            for l in range(sizes[gi]):
                pltpu.make_async_copy(ins[i].at[me] if scatter else ins[i], land[gi].at[me, l], local_sems.at[gi]).start()
                i += 1
        for gi in range(n_grp):
            seven = land[gi].at[pl.ds(0, N_DEV - 1)]
            w = pltpu.make_async_remote_copy(src_ref=seven, dst_ref=seven, send_sem=send_ref.at[gi], recv_sem=recv_ref.at[gi],
                                             device_id=(x, y, c), device_id_type=MESH)
            w.wait_send()
            w.wait_recv()
            mine = land[gi].at[me]
            pltpu.make_async_copy(mine, mine, local_sems.at[gi]).wait()

    outs = pl.pallas_call(
        body, name=name,
        out_shape=(*[pltpu.HBM(a.shape, a.dtype) for a in srcs], *[pltpu.HBM(a.shape, a.dtype) for a in lands]),
        in_specs=[HBM_SPEC] * (n_in + n_grp) + [SEM_SPEC, SEM_SPEC, ANY],
        out_specs=tuple([HBM_SPEC] * (n_in + n_grp)),
        input_output_aliases={i: i for i in range(n_in + n_grp)},
        scratch_shapes=[pltpu.SemaphoreType.DMA((n_grp,))],
        compiler_params=pltpu.CompilerParams(has_side_effects=EFFECT),
    )(*srcs, *lands, send_sems, recv_sems, after)
    return outs[n_in:]


def _pack(arrs):
    flat = jnp.concatenate([a.reshape(-1).astype(F32) for a in arrs])
    n = flat.shape[0]
    rows = -(-n // 1024) * 8
    return jnp.pad(flat, (0, rows * 128 - n)).reshape(rows, 128)


def _unpack(buf, shapes, lead=()):
    flat = buf.reshape(lead + (-1,))
    out, off = [], 0
    for s in shapes:
        n = math.prod(s)
        out.append(flat[..., off:off + n].reshape(lead + tuple(s)))
        off += n
    return out


def _mm(name, a, a_spec, b, b_spec, out_sds, o_spec, grid, contract, nk=1):
    o_blk = tuple(d for d in o_spec.block_shape if d is not None)

    def body(a_ref, b_ref, o_ref, *acc):
        r = _dot(a_ref[...], b_ref[...], contract)
        if nk == 1:
            o_ref[...] = r.astype(o_ref.dtype)
        else:
            k = pl.program_id(len(grid) - 1)

            @pl.when(k == 0)
            def _():
                acc[0][...] = r

            @pl.when(k > 0)
            def _():
                acc[0][...] += r

            @pl.when(k == nk - 1)
            def _():
                o_ref[...] = acc[0][...].astype(o_ref.dtype)

    sem = ("parallel",) * (len(grid) - 1) + (("arbitrary",) if nk > 1 else ("parallel",))
    return pl.pallas_call(
        body, name=name, out_shape=out_sds, grid=grid, in_specs=[a_spec, b_spec], out_specs=o_spec,
        scratch_shapes=[pltpu.VMEM(o_blk, F32)] if nk > 1 else [], compiler_params=_cp(*sem))(a, b)


def _tile(n, want):
    t = min(n, want)
    assert n % t == 0, (n, t)
    return t


def _mm_nn(name, a, b, out_dtype=F32, tm=512, tn=512):
    (M, K), N = a.shape, b.shape[1]
    tm, tn = _tile(M, tm), _tile(N, tn)
    return _mm(name, a, pl.BlockSpec((tm, K), lambda i, j: (i, 0)), b, pl.BlockSpec((K, tn), lambda i, j: (0, j)),
               jax.ShapeDtypeStruct((M, N), out_dtype), pl.BlockSpec((tm, tn), lambda i, j: (i, j)),
               (M // tm, N // tn), "nn")


def _mm_nt(name, a, b, out_dtype=F32, tm=512, tn=512):
    (M, K), N = a.shape, b.shape[0]
    tm, tn = _tile(M, tm), _tile(N, tn)
    return _mm(name, a, pl.BlockSpec((tm, K), lambda i, j: (i, 0)), b, pl.BlockSpec((tn, K), lambda i, j: (j, 0)),
               jax.ShapeDtypeStruct((M, N), out_dtype), pl.BlockSpec((tm, tn), lambda i, j: (i, j)),
               (M // tm, N // tn), "nt")


def _mm_tn(name, a, b, out_dtype=F32, tm=512, tn=512):
    (K, M), N = a.shape, b.shape[1]
    tm, tn = _tile(M, tm), _tile(N, tn)
    return _mm(name, a, pl.BlockSpec((K, tm), lambda i, j: (0, i)), b, pl.BlockSpec((K, tn), lambda i, j: (0, j)),
               jax.ShapeDtypeStruct((M, N), out_dtype), pl.BlockSpec((tm, tn), lambda i, j: (i, j)),
               (M // tm, N // tn), "tn")


def _mm_cols(name, a, w, out_dtype=F32, tm=512):
    (M, K), (J, _, n) = a.shape, w.shape
    tm = _tile(M, tm)
    return _mm(name, a, pl.BlockSpec((tm, K), lambda j, i: (i, 0)), w, pl.BlockSpec((None, K, n), lambda j, i: (j, 0, 0)),
               jax.ShapeDtypeStruct((J, M, n), out_dtype), pl.BlockSpec((None, tm, n), lambda j, i: (j, i, 0)),
               (J, M // tm), "nn")


def _mm_cols_dx(name, d, w, out_dtype=F32, tm=512):
    (J, M, n), K = d.shape, w.shape[1]
    tm = _tile(M, tm)
    return _mm(name, d, pl.BlockSpec((None, tm, n), lambda i, j: (j, i, 0)), w, pl.BlockSpec((None, K, n), lambda i, j: (j, 0, 0)),
               jax.ShapeDtypeStruct((M, K), out_dtype), pl.BlockSpec((tm, K), lambda i, j: (i, 0)),
               (M // tm, J), "nt", nk=J)


def _mm_cols_dw(name, a, d, out_dtype=F32, tk=512):
    (M, K), (J, _, n) = a.shape, d.shape
    tk = _tile(K, tk)
    return _mm(name, a, pl.BlockSpec((M, tk), lambda j, i: (0, i)), d, pl.BlockSpec((None, M, n), lambda j, i: (j, 0, 0)),
               jax.ShapeDtypeStruct((J, K, n), out_dtype), pl.BlockSpec((None, tk, n), lambda j, i: (j, i, 0)),
               (J, K // tk), "tn")


def _mm_rows_resid(name, a, w, resid, gate, tm=512, tn=512):
    (Q, M, k), N = a.shape, w.shape[2]
    tm, tn = _tile(M, tm), _tile(N, tn)

    def body(a_ref, w_ref, r_ref, g_ref, y_ref, x_ref, acc):
        q = pl.program_id(2)
        r = _dot(a_ref[...], w_ref[...], "nn")

        @pl.when(q == 0)
        def _():
            acc[...] = r

        @pl.when(q > 0)
        def _():
            acc[...] += r

        @pl.when(q == Q - 1)
        def _():
            y = acc[...]
            y_ref[...] = y
            x_ref[...] = r_ref[...] + g_ref[...] * y

    return pl.pallas_call(
        body, name=name, grid=(M // tm, N // tn, Q),
        out_shape=(jax.ShapeDtypeStruct((M, N), F32), jax.ShapeDtypeStruct((M, N), F32)),
        in_specs=[pl.BlockSpec((None, tm, k), lambda i, j, q: (q, i, 0)), pl.BlockSpec((None, k, tn), lambda i, j, q: (q, 0, j)),
                  pl.BlockSpec((tm, tn), lambda i, j, q: (i, j)), pl.BlockSpec((1, tn), lambda i, j, q: (0, j))],
        out_specs=(pl.BlockSpec((tm, tn), lambda i, j, q: (i, j)), pl.BlockSpec((tm, tn), lambda i, j, q: (i, j))),
        scratch_shapes=[pltpu.VMEM((tm, tn), F32)], compiler_params=_cp("parallel", "parallel", "arbitrary"))(a, w, resid, gate)


def _mm_rows_dx(name, d, w, out_dtype=F32, tm=512):
    (M, N), (Q, k, _) = d.shape, w.shape
    tm = _tile(M, tm)
    return _mm(name, d, pl.BlockSpec((tm, N), lambda q, i: (i, 0)), w, pl.BlockSpec((None, k, N), lambda q, i: (q, 0, 0)),
               jax.ShapeDtypeStruct((Q, M, k), out_dtype), pl.BlockSpec((None, tm, k), lambda q, i: (q, i, 0)),
               (Q, M // tm), "nt")


def _mm_rows_dw(name, a, d, out_dtype=F32, tn=512):
    (Q, M, k), N = a.shape, d.shape[1]
    tn = _tile(N, tn)
    return _mm(name, a, pl.BlockSpec((None, M, k), lambda q, j: (q, 0, 0)), d, pl.BlockSpec((M, tn), lambda q, j: (0, j)),
               jax.ShapeDtypeStruct((Q, k, N), out_dtype), pl.BlockSpec((None, k, tn), lambda q, j: (q, 0, j)),
               (Q, N // tn), "tn")


def _silu(v):
    return v * jax.nn.sigmoid(v)


def _ada_fwd(c16, ada_w):
    L, D, n = ada_w.shape

    def body(c_ref, w_ref, o_ref):
        o_ref[...] = _dot(_silu(c_ref[...]), w_ref[...], "nn")

    return pl.pallas_call(
        body, name="ada_fwd", grid=(L,), out_shape=jax.ShapeDtypeStruct((L, 16, n), F32),
        in_specs=[pl.BlockSpec((16, D), lambda l: (0, 0)), pl.BlockSpec((None, D, n), lambda l: (l, 0, 0))],
        out_specs=pl.BlockSpec((None, 16, n), lambda l: (l, 0, 0)), compiler_params=_cp("parallel"))(c16, ada_w)


def _ada_bwd(c16, dmod16):
    L, _, n = dmod16.shape
    D = c16.shape[1]

    def body(c_ref, d_ref, o_ref):
        o_ref[...] = _dot(_silu(c_ref[...]), d_ref[...], "tn")

    return pl.pallas_call(
        body, name="ada_bwd", grid=(L,), out_shape=jax.ShapeDtypeStruct((L, D, n), F32),
        in_specs=[pl.BlockSpec((16, D), lambda l: (0, 0)), pl.BlockSpec((None, 16, n), lambda l: (l, 0, 0))],
        out_specs=pl.BlockSpec((None, D, n), lambda l: (l, 0, 0)), compiler_params=_cp("parallel"))(c16, dmod16)


def _row_spec(tr, n):
    return pl.BlockSpec((tr, n), lambda i: (i, 0))


def _vec_spec(n):
    return pl.BlockSpec((1, n), lambda i: (0, 0))


def _rmsmod_fwd(name, x, g, sc, sh, after, tr=256):
    S, D = x.shape

    def body(x_ref, g_ref, sc_ref, sh_ref, after_ref, h_ref):
        xv = x_ref[...]
        rstd = lax.rsqrt(jnp.mean(xv * xv, axis=-1, keepdims=True) + EPS)
        y = xv * rstd * g_ref[...]
        h_ref[...] = (y * (1.0 + sc_ref[...]) + sh_ref[...]).astype(h_ref.dtype)

    return pl.pallas_call(
        body, name=name, grid=(S // tr,), out_shape=jax.ShapeDtypeStruct((S, D), _MXU_DTYPE),
        in_specs=[_row_spec(tr, D), _vec_spec(D), _vec_spec(D), _vec_spec(D), ANY], out_specs=_row_spec(tr, D),
        compiler_params=_cp("parallel"))(x, g, sc, sh, after)


def _acc_rows(ref, val, first):
    s = jnp.sum(val, axis=0, keepdims=True)

    @pl.when(first)
    def _():
        ref[...] = s

    @pl.when(jnp.logical_not(first))
    def _():
        ref[...] += s


def _rmsmod_bwd(name, x, g, sc, dh, dres, after, tr=256):
    S, D = x.shape

    def body(x_ref, g_ref, sc_ref, dh_ref, dres_ref, after_ref, dx_ref, dg_ref, dsc_ref, dsh_ref):
        first = pl.program_id(0) == 0
        xv, dh_v, gv = x_ref[...], dh_ref[...], g_ref[...]
        rstd = lax.rsqrt(jnp.mean(xv * xv, axis=-1, keepdims=True) + EPS)
        xhat = xv * rstd
        _acc_rows(dsh_ref, dh_v, first)
        _acc_rows(dsc_ref, dh_v * (xhat * gv), first)
        dyg = dh_v * (1.0 + sc_ref[...])
        _acc_rows(dg_ref, dyg * xhat, first)
        dxhat = dyg * gv
        dx_ref[...] = dres_ref[...] + rstd * (dxhat - xhat * jnp.mean(dxhat * xhat, axis=-1, keepdims=True))

    vec = jax.ShapeDtypeStruct((1, D), F32)
    return pl.pallas_call(
        body, name=name, grid=(S // tr,), out_shape=(jax.ShapeDtypeStruct((S, D), F32), vec, vec, vec),
        in_specs=[_row_spec(tr, D), _vec_spec(D), _vec_spec(D), _row_spec(tr, D), _row_spec(tr, D), ANY],
        out_specs=(_row_spec(tr, D), _vec_spec(D), _vec_spec(D), _vec_spec(D)),
        compiler_params=_cp("arbitrary"))(x, g, sc, dh, dres, after)


def _loss_head(x, g, target, tr=256):
    S, D = x.shape

    def body(x_ref, g_ref, t_ref, loss_ref, dx_ref, dg_ref):
        first = pl.program_id(0) == 0
        xv, gv = x_ref[...], g_ref[...]
        rstd = lax.rsqrt(jnp.mean(xv * xv, axis=-1, keepdims=True) + EPS)
        xhat = xv * rstd
        err = xhat * gv - t_ref[...]
        part = 0.5 * jnp.sum(jnp.mean(err * err, axis=-1, keepdims=True), axis=0, keepdims=True)

        @pl.when(first)
        def _():
            loss_ref[...] = part

        @pl.when(jnp.logical_not(first))
        def _():
            loss_ref[...] += part

        dout = err * (1.0 / D)
        _acc_rows(dg_ref, dout * xhat, first)
        dxhat = dout * gv
        dx_ref[...] = rstd * (dxhat - xhat * jnp.mean(dxhat * xhat, axis=-1, keepdims=True))

    return pl.pallas_call(
        body, name="loss_head", grid=(S // tr,),
        out_shape=(jax.ShapeDtypeStruct((1, 1), F32), jax.ShapeDtypeStruct((S, D), F32), jax.ShapeDtypeStruct((1, D), F32)),
        in_specs=[_row_spec(tr, D), _vec_spec(D), _row_spec(tr, D)],
        out_specs=(pl.BlockSpec((1, 1), lambda i: (0, 0)), _row_spec(tr, D), _vec_spec(D)),
        compiler_params=_cp("arbitrary"))(x, g, target)


def _gate_bwd(name, dx, y, gate, tr=256):
    S, D = dx.shape

    def body(dx_ref, y_ref, g_ref, dy_ref, dg_ref):
        dxv = dx_ref[...]
        dy_ref[...] = (g_ref[...] * dxv).astype(dy_ref.dtype)
        _acc_rows(dg_ref, dxv * y_ref[...], pl.program_id(0) == 0)

    return pl.pallas_call(
        body, name=name, grid=(S // tr,),
        out_shape=(jax.ShapeDtypeStruct((S, D), _MXU_DTYPE), jax.ShapeDtypeStruct((1, D), F32)),
        in_specs=[_row_spec(tr, D), _row_spec(tr, D), _vec_spec(D)], out_specs=(_row_spec(tr, D), _vec_spec(D)),
        compiler_params=_cp("arbitrary"))(dx, y, gate)


def _shift_down(v, k):
    t = lax.broadcasted_iota(jnp.int32, v.shape, 0)
    return jnp.where(t >= k, pltpu.roll(v, k, axis=0), 0.0)


def _shift_up(v, k):
    n = v.shape[0]
    t = lax.broadcasted_iota(jnp.int32, v.shape, 0)
    return jnp.where(t < n - k, pltpu.roll(v, n - k, axis=0), 0.0)


def _window_sum(p, w, shift):
    s, k = p, 1
    while k < w:
        s = s + shift(s, k)
        k *= 2
    return s


def _pool_count(shape, w):
    t = lax.broadcasted_iota(jnp.int32, shape, 0)
    return jnp.minimum(t + 1, w).astype(F32)


def _ab_specs(S):
    zs = [pl.BlockSpec((None, S, 128), functools.partial(lambda g, q: (2 * q + g // 2, 0, g % 2), q=q)) for q in range(4)]
    return zs


def _ab_mix_fwd(z8, conv_w, mix_w, scale):
    S = z8.shape[1]

    def body(b_ref, c_ref, a_ref, p_ref, w_ref, mix_ref, sc_ref, y_ref):
        g = pl.program_id(0)
        cg = c_ref[...] * a_ref[...]
        w = w_ref[...]
        conv = w[0:1] * _shift_down(cg, 2) + w[1:2] * _shift_down(cg, 1) + w[2:3] * cg
        y_ref[0] = (b_ref[...] * conv).astype(y_ref.dtype)
        for gg, win in enumerate(POOL_WINDOWS):
            @pl.when(g == gg)
            def _(win=win):
                p = p_ref[...]
                pooled = _window_sum(p, win, _shift_down) / _pool_count(p.shape, win) - p
                y_ref[1] = (_dot(pooled, mix_ref[...], "nn") * sc_ref[...]).astype(y_ref.dtype)

    return pl.pallas_call(
        body, name="ab_mix_fwd", grid=(4,), out_shape=jax.ShapeDtypeStruct((2, S, 512), _MXU_DTYPE),
        in_specs=_ab_specs(S) + [pl.BlockSpec((3, 128), lambda g: (0, g)), pl.BlockSpec((None, 128, 128), lambda g: (g, 0, 0)),
                                 pl.BlockSpec((1, 128), lambda g: (0, g))],
        out_specs=pl.BlockSpec((2, S, 128), lambda g: (0, 0, g)), compiler_params=_cp("parallel"))(z8, z8, z8, z8, conv_w, mix_w, scale)


def _ab_mix_bwd(z8, dycat2, conv_w, mix_w, scale):
    S = z8.shape[1]

    def body(b_ref, c_ref, a_ref, p_ref, dy_ref, w_ref, mix_ref, sc_ref, dz_ref, dw_ref, dmix_ref, dsc_ref):
        g = pl.program_id(0)
        bv, cv, av, w = b_ref[...], c_ref[...], a_ref[...], w_ref[...]
        dya = dy_ref[0]
        cg = cv * av
        cg1, cg2 = _shift_down(cg, 1), _shift_down(cg, 2)
        conv = w[0:1] * cg2 + w[1:2] * cg1 + w[2:3] * cg
        dz_ref[0] = (dya * conv).astype(dz_ref.dtype)
        dconv = dya * bv
        dcg = w[2:3] * dconv + w[1:2] * _shift_up(dconv, 1) + w[0:1] * _shift_up(dconv, 2)
        dz_ref[1] = (dcg * av).astype(dz_ref.dtype)
        dz_ref[2] = (dcg * cv).astype(dz_ref.dtype)
        dw_ref[0:1, :] = jnp.sum(dconv * cg2, axis=0, keepdims=True)
        dw_ref[1:2, :] = jnp.sum(dconv * cg1, axis=0, keepdims=True)
        dw_ref[2:3, :] = jnp.sum(dconv * cg, axis=0, keepdims=True)
        for gg, win in enumerate(POOL_WINDOWS):
            @pl.when(g == gg)
            def _(win=win):
                p, dyb, mix = p_ref[...], dy_ref[1], mix_ref[...]
                cnt = _pool_count(p.shape, win)
                pooled = _window_sum(p, win, _shift_down) / cnt - p
                dsc_ref[...] = jnp.sum(dyb * _dot(pooled, mix, "nn"), axis=0, keepdims=True)
                dmixed = dyb * sc_ref[...]
                dmix_ref[...] = _dot(pooled, dmixed, "tn")
                dpooled = _dot(dmixed, mix, "nt")
                dz_ref[3] = (_window_sum(dpooled / cnt, win, _shift_up) - dpooled).astype(dz_ref.dtype)

    return pl.pallas_call(
        body, name="ab_mix_bwd", grid=(4,),
        out_shape=(jax.ShapeDtypeStruct((4, 2, S, 256), _MXU_DTYPE), jax.ShapeDtypeStruct((3, 512), F32),
                   jax.ShapeDtypeStruct((4, 128, 128), F32), jax.ShapeDtypeStruct((1, 512), F32)),
        in_specs=_ab_specs(S) + [pl.BlockSpec((2, S, 128), lambda g: (0, 0, g)), pl.BlockSpec((3, 128), lambda g: (0, g)),
                                 pl.BlockSpec((None, 128, 128), lambda g: (g, 0, 0)), pl.BlockSpec((1, 128), lambda g: (0, g))],
        out_specs=(pl.BlockSpec((4, None, S, 128), lambda g: (0, g // 2, 0, g % 2)), pl.BlockSpec((3, 128), lambda g: (0, g)),
                   pl.BlockSpec((None, 128, 128), lambda g: (g, 0, 0)), pl.BlockSpec((1, 128), lambda g: (0, g))),
        compiler_params=_cp("parallel"))(z8, z8, z8, z8, dycat2, conv_w, mix_w, scale)


HALO = 8


def _ffn_specs(S, n, tr):
    nb = S // HALO
    tile = pl.BlockSpec((2, None, tr, n), lambda j, i: (0, j, i, 0))
    prev = pl.BlockSpec((2, None, HALO, n), lambda j, i: (0, j, jnp.maximum(i * (tr // HALO) - 1, 0), 0))
    nxt = pl.BlockSpec((2, None, HALO, n), lambda j, i: (0, j, jnp.minimum((i + 1) * (tr // HALO), nb - 1), 0))
    cw = pl.BlockSpec((2, None, 3, n), lambda j, i: (0, j, 0, 0))
    return tile, prev, nxt, cw


def _conv_rows(ext, w, lo, tr):
    n = ext.shape[0]
    return (w[0:1] * pltpu.roll(ext, 2, axis=0)[lo:lo + tr] + w[1:2] * pltpu.roll(ext, 1, axis=0)[lo:lo + tr]
            + w[2:3] * ext[lo:lo + tr])


def _ffn_gate_fwd(name, u24, cw24, tr=256):
    _, J, S, n = u24.shape
    tile, prev, _, cw = _ffn_specs(S, n, tr)

    def body(u_ref, up_ref, w_ref, a_ref):
        keep = (pl.program_id(1) > 0).astype(F32)
        z = []
        for h in range(2):
            ext = jnp.concatenate([up_ref[h] * keep, u_ref[h]], axis=0)
            z.append(_conv_rows(ext, w_ref[h], HALO, tr))
        a_ref[...] = (_silu(z[0]) * z[1]).astype(a_ref.dtype)

    return pl.pallas_call(
        body, name=name, grid=(J, S // tr), out_shape=jax.ShapeDtypeStruct((J, S, n), _MXU_DTYPE),
        in_specs=[tile, prev, cw], out_specs=pl.BlockSpec((None, tr, n), lambda j, i: (j, i, 0)),
        compiler_params=_cp("parallel", "parallel"))(u24, u24, cw24)


def _ffn_gate_bwd(name, u24, cw24, da4, tr=256):
    _, J, S, n = u24.shape
    tile, prev, nxt, cw = _ffn_specs(S, n, tr)
    nb = S // HALO
    ext_rows = tr + 2 * HALO

    def body(u_ref, up_ref, un_ref, w_ref, da_ref, dan_ref, du_ref, dcw_ref):
        i = pl.program_id(1)
        first = i == 0
        keep_prev = (i > 0).astype(F32)
        keep_next = (i < S // tr - 1).astype(F32)
        ext = [jnp.concatenate([up_ref[h] * keep_prev, u_ref[h], un_ref[h]], axis=0) for h in range(2)]
        w = [w_ref[h] for h in range(2)]
        zg = _conv_rows(ext[0], w[0], HALO, tr + HALO)
        zu = _conv_rows(ext[1], w[1], HALO, tr + HALO)
        da = jnp.concatenate([da_ref[...], dan_ref[...] * keep_next], axis=0)
        sg = jax.nn.sigmoid(zg)
        dz = [da * zu * (sg * (1.0 + zg * (1.0 - sg))), da * (zg * sg)]
        m = tr + HALO
        for h in range(2):
            d = dz[h]
            du = w[h][2:3] * d[:tr] + w[h][1:2] * pltpu.roll(d, m - 1, axis=0)[:tr] + w[h][0:1] * pltpu.roll(d, m - 2, axis=0)[:tr]
            du_ref[h] = du.astype(du_ref.dtype)
            dt = d[:tr]
            e = ext[h]
            parts = [jnp.sum(dt * pltpu.roll(e, 2, axis=0)[HALO:HALO + tr], axis=0, keepdims=True),
                     jnp.sum(dt * pltpu.roll(e, 1, axis=0)[HALO:HALO + tr], axis=0, keepdims=True),
                     jnp.sum(dt * e[HALO:HALO + tr], axis=0, keepdims=True)]
            for k in range(3):
                @pl.when(first)
                def _(k=k, h=h):
                    dcw_ref[h, k:k + 1, :] = parts[k]

                @pl.when(jnp.logical_not(first))
                def _(k=k, h=h):
                    dcw_ref[h, k:k + 1, :] += parts[k]

    da_tile = pl.BlockSpec((None, tr, n), lambda j, i: (j, i, 0))
    da_next = pl.BlockSpec((None, HALO, n), lambda j, i: (j, jnp.minimum((i + 1) * (tr // HALO), nb - 1), 0))
    return pl.pallas_call(
        body, name=name, grid=(J, S // tr),
        out_shape=(jax.ShapeDtypeStruct((2, J, S, n), _MXU_DTYPE), jax.ShapeDtypeStruct((2, J, 3, n), F32)),
        in_specs=[tile, prev, nxt, cw, da_tile, da_next], out_specs=(tile, cw),
        compiler_params=_cp("parallel", "arbitrary"))(u24, u24, u24, cw24, da4, da4)


def _rms_rows(v, g):
    rstd = lax.rsqrt(jnp.mean(v * v, axis=-1, keepdims=True) + EPS)
    return v * rstd * g


def _rms_rows_bwd(v, g, dy):
    rstd = lax.rsqrt(jnp.mean(v * v, axis=-1, keepdims=True) + EPS)
    vhat = v * rstd
    dvhat = dy * g
    return rstd * (dvhat - vhat * jnp.mean(dvhat * vhat, axis=-1, keepdims=True)), dy * vhat


def _mla_prep_fwd(z, qg, kvg, tr=256):
    S = z.shape[0]

    def body(q_ref, kv_ref, qg_ref, kvg_ref, qn_ref, kvn_ref):
        qn_ref[...] = _rms_rows(q_ref[...], qg_ref[...]).astype(qn_ref.dtype)
        kvn_ref[...] = _rms_rows(kv_ref[...], kvg_ref[...]).astype(kvn_ref.dtype)

    return pl.pallas_call(
        body, name="mla_prep_fwd", grid=(S // tr,),
        out_shape=(jax.ShapeDtypeStruct((S, 256), _MXU_DTYPE), jax.ShapeDtypeStruct((S, 128), _MXU_DTYPE)),
        in_specs=[pl.BlockSpec((tr, 256), lambda i: (i, 0)), pl.BlockSpec((tr, 128), lambda i: (i, 2)), _vec_spec(256), _vec_spec(128)],
        out_specs=(_row_spec(tr, 256), _row_spec(tr, 128)), compiler_params=_cp("parallel"))(z, z, qg, kvg)


def _mla_prep_bwd(z, qg, kvg, dqn, dkvn, dkpe, duv, tr=256):
    S = z.shape[0]

    def body(q_ref, kv_ref, qg_ref, kvg_ref, dqn_ref, dkvn_ref, dkpe_ref, duv_ref, dz_ref, dqg_ref, dkvg_ref):
        first = pl.program_id(0) == 0
        dq, dqg = _rms_rows_bwd(q_ref[...], qg_ref[...], dqn_ref[...])
        dkv, dkvg = _rms_rows_bwd(kv_ref[...], kvg_ref[...], dkvn_ref[...])
        _acc_rows(dqg_ref, dqg, first)
        _acc_rows(dkvg_ref, dkvg, first)
        dz_ref[:, 0:256] = dq.astype(dz_ref.dtype)
        dz_ref[:, 256:384] = dkv.astype(dz_ref.dtype)
        dz_ref[:, 384:512] = dkpe_ref[...].astype(dz_ref.dtype)
        dz_ref[:, 512:1536] = duv_ref[...].astype(dz_ref.dtype)

    return pl.pallas_call(
        body, name="mla_prep_bwd", grid=(S // tr,),
        out_shape=(jax.ShapeDtypeStruct((S, 1536), _MXU_DTYPE), jax.ShapeDtypeStruct((1, 256), F32), jax.ShapeDtypeStruct((1, 128), F32)),
        in_specs=[pl.BlockSpec((tr, 256), lambda i: (i, 0)), pl.BlockSpec((tr, 128), lambda i: (i, 2)), _vec_spec(256), _vec_spec(128),
                  _row_spec(tr, 256), _row_spec(tr, 128), _row_spec(tr, 128), _row_spec(tr, 1024)],
        out_specs=(_row_spec(tr, 1536), _vec_spec(256), _vec_spec(128)),
        compiler_params=_cp("arbitrary"))(z, z, qg, kvg, dqn, dkvn, dkpe, duv)


def _rope(v, cos, sa, sb):
    return v * cos + pltpu.roll(v, 112, axis=1) * sa + pltpu.roll(v, 16, axis=1) * sb


def _rope_t(d, cos, sa, sb):
    return d * cos + pltpu.roll(d * sa, 16, axis=1) + pltpu.roll(d * sb, 112, axis=1)


def _rope_fwd(qraw, kvall, z, cosq, cosk, sa, sb, tr=256):
    S = qraw.shape[0]

    def body(q_ref, k_ref, v_ref, kpe_ref, cq_ref, ck_ref, sa_ref, sb_ref, qo_ref, ko_ref, vo_ref):
        cq, ck, sa_v, sb_v = cq_ref[...], ck_ref[...], sa_ref[...], sb_ref[...]
        kpe = _rope(kpe_ref[...], ck, sa_v, sb_v)
        for h in range(8):
            cols = slice(128 * h, 128 * h + 128)
            qo_ref[:, cols] = _rope(q_ref[:, cols], cq, sa_v, sb_v).astype(qo_ref.dtype)
            ko_ref[:, cols] = (k_ref[:, cols] + kpe).astype(ko_ref.dtype)
        vo_ref[...] = v_ref[...].astype(vo_ref.dtype)

    tab = _row_spec(tr, 128)
    return pl.pallas_call(
        body, name="rope_fwd", grid=(S // tr,),
        out_shape=(jax.ShapeDtypeStruct((S, 1024), _MXU_DTYPE), jax.ShapeDtypeStruct((S, 1024), _MXU_DTYPE),
                   jax.ShapeDtypeStruct((S, 512), _MXU_DTYPE)),
        in_specs=[_row_spec(tr, 1024), pl.BlockSpec((tr, 1024), lambda i: (i, 0)), pl.BlockSpec((tr, 512), lambda i: (i, 2)),
                  pl.BlockSpec((tr, 128), lambda i: (i, 3)), tab, tab, tab, tab],
        out_specs=(_row_spec(tr, 1024), _row_spec(tr, 1024), _row_spec(tr, 512)),
        compiler_params=_cp("parallel"))(qraw, kvall, kvall, z, cosq, cosk, sa, sb)


def _rope_bwd(dq, dk, dv, cosq, cosk, sa, sb, tr=256):
    S = dq.shape[0]

    def body(dq_ref, dk_ref, dv_ref, cq_ref, ck_ref, sa_ref, sb_ref, dqo_ref, dkv_ref, dkpe_ref):
        cq, ck, sa_v, sb_v = cq_ref[...], ck_ref[...], sa_ref[...], sb_ref[...]
        tot = jnp.zeros((tr, 128), F32)
        for h in range(8):
            cols = slice(128 * h, 128 * h + 128)
            dqo_ref[:, cols] = _rope_t(dq_ref[:, cols], cq, sa_v, sb_v).astype(dqo_ref.dtype)
            dkh = dk_ref[:, cols]
            tot = tot + dkh
            dkv_ref[:, cols] = dkh.astype(dkv_ref.dtype)
        dkv_ref[:, 1024:1536] = dv_ref[...].astype(dkv_ref.dtype)
        dkpe_ref[...] = _rope_t(tot, ck, sa_v, sb_v)

    tab = _row_spec(tr, 128)
    return pl.pallas_call(
        body, name="rope_bwd", grid=(S // tr,),
        out_shape=(jax.ShapeDtypeStruct((S, 1024), _MXU_DTYPE), jax.ShapeDtypeStruct((S, 1536), _MXU_DTYPE),
                   jax.ShapeDtypeStruct((S, 128), F32)),
        in_specs=[_row_spec(tr, 1024), _row_spec(tr, 1024), _row_spec(tr, 512), tab, tab, tab, tab],
        out_specs=(_row_spec(tr, 1024), _row_spec(tr, 1536), _row_spec(tr, 128)),
        compiler_params=_cp("parallel"))(dq, dk, dv, cosq, cosk, sa, sb)


NEG = -1e30


def _attn_fwd(q, k, v, tq=256, tk=256):
    S = q.shape[0]

    def body(q_ref, k_ref, v_ref, o_ref, lse_ref):
        i = pl.program_id(1)
        row = i * tq + lax.broadcasted_iota(jnp.int32, (tq, tk), 0)
        qs = [q_ref[:, 0:128], q_ref[:, 128:256]]

        def step(kb, carry):
            start = pl.multiple_of(kb * tk, tk)
            col = start + lax.broadcasted_iota(jnp.int32, (tq, tk), 1)
            vv = v_ref[pl.ds(start, tk), :]
            out = []
            for h in range(2):
                m, l, acc = carry[3 * h:3 * h + 3]
                s = _dot(qs[h], k_ref[pl.ds(start, tk), 128 * h:128 * h + 128], "nt") * ATTN_SCALE
                s = jnp.where(col <= row, s, NEG)
                m_new = jnp.maximum(m, jnp.max(s, axis=-1, keepdims=True))
                alpha = jnp.exp(m - m_new)
                p = jnp.exp(s - m_new)
                out += [m_new, alpha * l + jnp.sum(p, axis=-1, keepdims=True), alpha * acc + _dot(p, vv, "nn")]
            return tuple(out)

        init = (jnp.full((tq, 1), NEG, F32), jnp.zeros((tq, 1), F32), jnp.zeros((tq, 128), F32)) * 2
        ma, la, acca, mb, lb, accb = lax.fori_loop(0, (i * tq + tq) // tk, step, init)
        lane = lax.broadcasted_iota(jnp.int32, (tq, 128), 1)
        o_ref[...] = jnp.where(lane < 64, acca / la, accb / lb)
        lse_ref[...] = jnp.where(lane < 64, ma + jnp.log(la), mb + jnp.log(lb))

    return pl.pallas_call(
        body, name="attn_fwd", grid=(4, S // tq),
        out_shape=(jax.ShapeDtypeStruct((S, 512), F32), jax.ShapeDtypeStruct((4, S, 128), F32)),
        in_specs=[pl.BlockSpec((tq, 256), lambda p, i: (i, p)), pl.BlockSpec((S, 256), lambda p, i: (0, p)),
                  pl.BlockSpec((S, 128), lambda p, i: (0, p))],
        out_specs=(pl.BlockSpec((tq, 128), lambda p, i: (i, p)), pl.BlockSpec((None, tq, 128), lambda p, i: (p, i, 0))),
        compiler_params=_cp("parallel", "parallel"))(q, k, v)


def _attn_bwd(q, k, v, o, lse, dycat2, tq=256, tk=256):
    S = q.shape[0]

    def body(q_ref, k_ref, v_ref, o_ref, lse_ref, do_ref, dq_ref, dk_ref, dv_ref):
        j = pl.program_id(1)

        @pl.when(j == 0)
        def _():
            dq_ref[...] = jnp.zeros_like(dq_ref)

        col = j * tk + lax.broadcasted_iota(jnp.int32, (tq, tk), 1)
        lane = lax.broadcasted_iota(jnp.int32, (tq, 128), 1)
        ks = [k_ref[:, 0:128], k_ref[:, 128:256]]
        vv = v_ref[...]

        def step(qb, carry):
            dka, dkb, dvp = carry
            start = pl.multiple_of(qb * tq, tq)
            rows = pl.ds(start, tq)
            row = start + lax.broadcasted_iota(jnp.int32, (tq, tk), 0)
            do, lse_v = do_ref[rows, :], lse_ref[rows, :]
            prod = do * o_ref[rows, :]
            dks = [dka, dkb]
            for h in range(2):
                mine = (lane < 64) if h == 0 else (lane >= 64)
                delta = jnp.sum(jnp.where(mine, prod, 0.0), axis=-1, keepdims=True)
                do_h = jnp.where(mine, do, 0.0)
                qh = q_ref[rows, 128 * h:128 * h + 128]
                s = _dot(qh, ks[h], "nt") * ATTN_SCALE
                p = jnp.where(col <= row, jnp.exp(s - lse_v[:, 64 * h:64 * h + 1]), 0.0)
                dvp = dvp + _dot(p, do_h, "tn")
                ds = p * (_dot(do_h, vv, "nt") - delta) * ATTN_SCALE
                dq_ref[rows, 128 * h:128 * h + 128] += _dot(ds, ks[h], "nn")
                dks[h] = dks[h] + _dot(ds, qh, "tn")
            return dks[0], dks[1], dvp

        zero = jnp.zeros((tk, 128), F32)
        dka, dkb, dvp = lax.fori_loop((j * tk) // tq, S // tq, step, (zero, zero, zero))
        dk_ref[:, 0:128] = dka
        dk_ref[:, 128:256] = dkb
        dv_ref[...] = dvp

    return pl.pallas_call(
        body, name="attn_bwd", grid=(4, S // tk),
        out_shape=(jax.ShapeDtypeStruct((S, 1024), F32), jax.ShapeDtypeStruct((S, 1024), F32), jax.ShapeDtypeStruct((S, 512), F32)),
        in_specs=[pl.BlockSpec((S, 256), lambda p, j: (0, p)), pl.BlockSpec((tk, 256), lambda p, j: (j, p)),
                  pl.BlockSpec((tk, 128), lambda p, j: (j, p)), pl.BlockSpec((S, 128), lambda p, j: (0, p)),
                  pl.BlockSpec((None, S, 128), lambda p, j: (p, 0, 0)), pl.BlockSpec((None, S, 128), lambda p, j: (0, 0, p))],
        out_specs=(pl.BlockSpec((S, 256), lambda p, j: (0, p)), pl.BlockSpec((tk, 256), lambda p, j: (j, p)),
                   pl.BlockSpec((tk, 128), lambda p, j: (j, p))),
        compiler_params=_cp("parallel", "arbitrary"))(q, k, v, o, lse, dycat2)


CHUNK = 128
GELU_C = math.sqrt(2.0 / math.pi)


def _gelu(v):
    t = jnp.tanh(GELU_C * (v + 0.044715 * (v * v * v)))
    return v * (0.5 * (1.0 + t)), t


def _gelu_grad(v, t):
    return 0.5 * (1.0 + t) + v * (0.5 * (1.0 - t * t) * GELU_C * (1.0 + 3.0 * 0.044715 * v * v))


def _tril(w):
    r = lax.broadcasted_iota(jnp.int32, w.shape, 0)
    c = lax.broadcasted_iota(jnp.int32, w.shape, 1)
    return jnp.where(c <= r, w, 0.0)


def _layer_norm(v, g, b):
    xc = v - jnp.mean(v, axis=-1, keepdims=True)
    rstd = lax.rsqrt(jnp.mean(xc * xc, axis=-1, keepdims=True) + EPS)
    xhat = xc * rstd
    return xhat * g + b, xhat, rstd


def _sgu_fwd(z, o, ln_g, ln_b, w_s, b_st, tr=256):
    S = z.shape[0]

    def body(u_ref, v_ref, o_ref, g_ref, b_ref, ws_ref, bs_ref, y_ref):
        gu, _ = _gelu(u_ref[...])
        gv, _ = _gelu(v_ref[...])
        vln, _, _ = _layer_norm(gv, g_ref[...], b_ref[...])
        y_ref[0] = o_ref[...].astype(y_ref.dtype)
        for g in range(4):
            wt = _tril(ws_ref[g])
            cols = slice(128 * g, 128 * g + 128)
            for ch in range(tr // CHUNK):
                rows = slice(CHUNK * ch, CHUNK * ch + CHUNK)
                mixed = _dot(wt, vln[rows, cols], "nn") + bs_ref[:, g:g + 1]
                y_ref[1, rows, cols] = (gu[rows, cols] * mixed).astype(y_ref.dtype)

    return pl.pallas_call(
        body, name="sgu_fwd", grid=(S // tr,), out_shape=jax.ShapeDtypeStruct((2, S, 512), _MXU_DTYPE),
        in_specs=[pl.BlockSpec((tr, 512), lambda i: (i, 1)), pl.BlockSpec((tr, 512), lambda i: (i, 2)), _row_spec(tr, 512),
                  _vec_spec(512), _vec_spec(512), pl.BlockSpec((4, 128, 128), lambda i: (0, 0, 0)), pl.BlockSpec((128, 4), lambda i: (0, 0))],
        out_specs=pl.BlockSpec((2, tr, 512), lambda i: (0, i, 0)), compiler_params=_cp("parallel"))(z, z, o, ln_g, ln_b, w_s, b_st)


def _sgu_bwd(z, dycat2, ln_g, ln_b, w_s, b_st, tr=256):
    S = z.shape[0]

    def body(u_ref, v_ref, dy_ref, g_ref, b_ref, ws_ref, bs_ref, duv_ref, dg_ref, db_ref, dws_ref, dbs_ref):
        first = pl.program_id(0) == 0
        u_pre, v_pre = u_ref[...], v_ref[...]
        gu, tu = _gelu(u_pre)
        gv, tv = _gelu(v_pre)
        gain = g_ref[...]
        vln, xhat, rstd = _layer_norm(gv, gain, b_ref[...])

        @pl.when(first)
        def _():
            dws_ref[...] = jnp.zeros_like(dws_ref)
            dbs_ref[...] = jnp.zeros_like(dbs_ref)

        dvln_cols = []
        for g in range(4):
            wt = _tril(ws_ref[g])
            cols = slice(128 * g, 128 * g + 128)
            dmixed_sum = jnp.zeros((CHUNK, 128), F32)
            dw = jnp.zeros((CHUNK, CHUNK), F32)
            dvln_rows = []
            for ch in range(tr // CHUNK):
                rows = slice(CHUNK * ch, CHUNK * ch + CHUNK)
                vt = vln[rows, cols]
                mixed = _dot(wt, vt, "nn") + bs_ref[:, g:g + 1]
                dyd = dy_ref[rows, cols]
                duv_ref[rows, cols] = (dyd * mixed * _gelu_grad(u_pre[rows, cols], tu[rows, cols])).astype(duv_ref.dtype)
                dmixed = dyd * gu[rows, cols]
                dmixed_sum = dmixed_sum + dmixed
                dw = dw + _dot(dmixed, vt, "nt")
                dvln_rows.append(_dot(wt, dmixed, "tn"))
            dws_ref[g] += _tril(dw)
            dbs_ref[g:g + 1, :] += jnp.sum(dmixed_sum.T, axis=0, keepdims=True)
            dvln_cols.append(jnp.concatenate(dvln_rows, axis=0))
        dvln = jnp.concatenate(dvln_cols, axis=1)
        _acc_rows(dg_ref, dvln * xhat, first)
        _acc_rows(db_ref, dvln, first)
        dxhat = dvln * gain
        dgv = rstd * (dxhat - jnp.mean(dxhat, axis=-1, keepdims=True) - xhat * jnp.mean(dxhat * xhat, axis=-1, keepdims=True))
        duv_ref[:, 512:1024] = (dgv * _gelu_grad(v_pre, tv)).astype(duv_ref.dtype)

    return pl.pallas_call(
        body, name="sgu_bwd", grid=(S // tr,),
        out_shape=(jax.ShapeDtypeStruct((S, 1024), _MXU_DTYPE), jax.ShapeDtypeStruct((1, 512), F32), jax.ShapeDtypeStruct((1, 512), F32),
                   jax.ShapeDtypeStruct((4, 128, 128), F32), jax.ShapeDtypeStruct((4, 128), F32)),
        in_specs=[pl.BlockSpec((tr, 512), lambda i: (i, 1)), pl.BlockSpec((tr, 512), lambda i: (i, 2)),
                  pl.BlockSpec((None, tr, 512), lambda i: (1, i, 0)), _vec_spec(512), _vec_spec(512),
                  pl.BlockSpec((4, 128, 128), lambda i: (0, 0, 0)), pl.BlockSpec((128, 4), lambda i: (0, 0))],
        out_specs=(_row_spec(tr, 1024), _vec_spec(512), _vec_spec(512), pl.BlockSpec((4, 128, 128), lambda i: (0, 0, 0)),
                   pl.BlockSpec((4, 128), lambda i: (0, 0))),
        compiler_params=_cp("arbitrary"))(z, z, dycat2, ln_g, ln_b, w_s, b_st)


def _sum_parts(name, parts, tr=512):
    P, R, C = parts.shape
    tr = _tile(R, tr) if R % 8 == 0 else R

    def body(p_ref, o_ref):
        g = p_ref[0]
        for k in range(1, P):
            g = g + p_ref[k]
        o_ref[...] = g

    return pl.pallas_call(
        body, name=name, grid=(R // tr,), out_shape=jax.ShapeDtypeStruct((R, C), F32),
        in_specs=[pl.BlockSpec((P, tr, C), lambda i: (0, i, 0))], out_specs=_row_spec(tr, C),
        compiler_params=_cp("parallel"))(parts)


ADAMW_BLOCK_BYTES = 36 * 2 ** 20


def _adamw(name, w, m, v, parts):
    L, R, C = w.shape
    P = parts[0].shape[0]
    row_bytes = 2 * C * (7 * 4 + P * parts[0].dtype.itemsize)
    tr = R
    if R * row_bytes > ADAMW_BLOCK_BYTES:
        tr = next(t for t in (1024, 512, 256, 128, 64, 32, 16) if R % t == 0 and t * row_bytes <= ADAMW_BLOCK_BYTES)
    nr = R // tr
    c1 = 1.0 / (1.0 - ADAM_B1 ** ADAM_STEP)
    c2 = 1.0 / (1.0 - ADAM_B2 ** ADAM_STEP)

    def body(w_ref, m_ref, v_ref, *rest):
        p_refs, (g_ref, d_ref, mo_ref, vo_ref) = rest[:L], rest[L:]
        for ll in range(L):
            @pl.when(pl.program_id(0) == ll)
            def _(p_ref=p_refs[ll]):
                g = p_ref[0].astype(F32)
                for k in range(1, P):
                    g = g + p_ref[k].astype(F32)
                m2 = ADAM_B1 * m_ref[...] + (1.0 - ADAM_B1) * g
                v2 = ADAM_B2 * v_ref[...] + (1.0 - ADAM_B2) * (g * g)
                g_ref[...] = g
                mo_ref[...] = m2
                vo_ref[...] = v2
                d_ref[...] = -ADAM_LR * ((m2 * c1) / (jnp.sqrt(v2 * c2) + ADAM_EPS) + ADAM_WD * w_ref[...])

    def part_spec(ll):
        return pl.BlockSpec((P, tr, C), lambda l, i: (0, jnp.where(l == ll, i, jnp.where(l < ll, 0, nr - 1)), 0))

    full = pl.BlockSpec((None, tr, C), lambda l, i: (l, i, 0))
    sds = jax.ShapeDtypeStruct((L, R, C), F32)
    return pl.pallas_call(
        body, name=name, grid=(L, nr), out_shape=(sds, sds, sds, sds),
        in_specs=[full] * 3 + [part_spec(ll) for ll in range(L)],
        out_specs=(full,) * 4, compiler_params=_cp("arbitrary", "arbitrary"))(w, m, v, *parts)


def _rope_tables(positions):
    half = 16
    inv_freq = 10000.0 ** (-jnp.arange(half, dtype=F32) / half)
    ang = positions.astype(F32)[:, None] * inv_freq
    cos, sin = jnp.cos(ang), jnp.sin(ang)
    S = positions.shape[0]
    z16, z32, z64 = jnp.zeros((S, 16), F32), jnp.zeros((S, 32), F32), jnp.zeros((S, 64), F32)
    cosk = jnp.concatenate([z64, cos, cos, z32], axis=1)
    cosq = jnp.concatenate([jnp.ones((S, 64), F32), cos, cos, z32], axis=1)
    sa = jnp.concatenate([z64, -sin, z16, z32], axis=1)
    sb = jnp.concatenate([z64, z16, sin, z32], axis=1)
    return cosq, cosk, sa, sb


def _ffn_fwd(l, x, mod, n2g, w_up8, cw24, w_down4):
    sh, sc, gate = mod
    h = _rmsmod_fwd(f"ffn{l}_norm", x, n2g, sc, sh, n2g)
    u8 = _mm_cols(f"ffn{l}_up", h, w_up8)
    S, n = u8.shape[1], u8.shape[2]
    u24 = u8.reshape(2, 4, S, n)
    a4 = _ffn_gate_fwd(f"ffn{l}_gate", u24, cw24)
    f, x_new = _mm_rows_resid(f"ffn{l}_down", a4, w_down4, x, gate)
    return x_new, (x, h, u24, a4, f)


def _ffn_bwd(l, dx, saved, mod, n2g, w_up8, cw24, w_down4):
    sh, sc, gate = mod
    x, h, u24, a4, f = saved
    df, dgate = _gate_bwd(f"ffn{l}_gate_bwd", dx, f, gate)
    da4 = _mm_rows_dx(f"ffn{l}_down_dx", df, w_down4)
    dw_down4 = _mm_rows_dw(f"ffn{l}_down_dw", a4, df, out_dtype=WIRE_DTYPE)
    du24, dcw24 = _ffn_gate_bwd(f"ffn{l}_act_bwd", u24, cw24, da4)
    du8 = du24.reshape((8,) + du24.shape[2:])
    dw_up8 = _mm_cols_dw(f"ffn{l}_up_dw", h, du8, out_dtype=WIRE_DTYPE)
    sent, token = _exchange_start(f"scatter_ffn{l}", [[dw_up8], [dw_down4.reshape(8, 352, dw_down4.shape[2])]], True, dw_up8)
    dh = _mm_cols_dx(f"ffn{l}_up_dx", du8, w_up8)
    dx_new, dn2g, dsc, dsh = _rmsmod_bwd(f"ffn{l}_norm_bwd", x, n2g, sc, dh, dx, token)
    return dx_new, dict(sent=sent, cw24=dcw24, n2g=dn2g, mod=(dsh, dsc, dgate))


def kernel(x, c, positions, ada_w, ada_b, norm1_g, norm2_g, ab_w_in, a_conv_w, b_mix_w, b_scale, ab_w_out, cd_w_in, c_q_norm_g, c_w_uq, c_kv_norm_g, c_w_ukv, d_ln_g, d_ln_b, d_w_s, d_b_s, cd_w_out, ffn_w_up, ffn_conv_w, ffn_w_down, final_norm_g, loss_target, m_ada_w, m_ada_b, m_norm1_g, m_norm2_g, m_ab_w_in, m_a_conv_w, m_b_mix_w, m_b_scale, m_ab_w_out, m_cd_w_in, m_c_q_norm_g, m_c_w_uq, m_c_kv_norm_g, m_c_w_ukv, m_d_ln_g, m_d_ln_b, m_d_w_s, m_d_b_s, m_cd_w_out, m_ffn_w_up, m_ffn_conv_w, m_ffn_w_down, m_final_norm_g, v_ada_w, v_ada_b, v_norm1_g, v_norm2_g, v_ab_w_in, v_a_conv_w, v_b_mix_w, v_b_scale, v_ab_w_out, v_cd_w_in, v_c_q_norm_g, v_c_w_uq, v_c_kv_norm_g, v_c_w_ukv, v_d_ln_g, v_d_ln_b, v_d_w_s, v_d_b_s, v_cd_w_out, v_ffn_w_up, v_ffn_conv_w, v_ffn_w_down, v_final_norm_g):
    S, D = x.shape[1], x.shape[2]
    me = 4 * lax.axis_index("x") + 2 * lax.axis_index("y") + lax.axis_index("c")
    x0, target = x[0], loss_target[0]
    W = _MXU_DTYPE

    small_shapes = [(1024,), (3, 64), (32,), (64,), (64,), (2, 3, 704)]
    (g0,) = _exchange("gather_small", [[_pack([c, a_conv_w, c_q_norm_g, d_ln_g, d_ln_b, ffn_conv_w])]], scatter=False)
    c_all, aconv_s, qg_s, lng_s, lnb_s, fcw_s = _unpack(g0[:, 0], small_shapes, lead=(N_DEV,))
    conv_w = aconv_s.transpose(1, 0, 2).reshape(3, 512)
    qg, ln_g, ln_b = qg_s.reshape(1, 256), lng_s.reshape(1, 512), lnb_s.reshape(1, 512)
    cw24 = [fcw_s[:, l].reshape(2, 4, 3, 704) for l in range(2)]
    c16 = jnp.pad(c_all, ((0, 16 - N_DEV), (0, 0)))

    mod_cols = _ada_fwd(c16, ada_w)
    (g1,) = _exchange("gather_mod", [[_pack([mod_cols])]], scatter=False)
    mod_all = _unpack(g1[:, 0], [(2, 16, 768)], lead=(N_DEV,))[0]
    mod_mine = lax.dynamic_index_in_dim(mod_all, me, axis=2, keepdims=False)
    mod = mod_mine.transpose(1, 0, 2).reshape(2, 6 * D) + ada_b
    mods = [[mod[l, k * D:(k + 1) * D].reshape(1, D) for k in range(6)] for l in range(2)]

    sq = lambda g: g.reshape(g.shape[:1] + g.shape[2:])
    gw_ab, token = _exchange_start("gather_w_ab", [[ab_w_in[0].astype(W)], [ab_w_out[0].astype(W)]], False, mod)
    gw_f0, token = _exchange_start("gather_w_ffn0", [[ffn_w_up[0].astype(W)], [ffn_w_down[0].astype(W)]], False, token)
    gw_cd, token = _exchange_start("gather_w_cd", [
        [cd_w_in[0].astype(W).reshape(1440, 128)], [c_w_uq[0].astype(W).reshape(192, 128)], [c_w_ukv[0].astype(W)],
        [cd_w_out[0].astype(W)]], False, token)
    gw_f1, started = _exchange_start("gather_w_ffn1", [[ffn_w_up[1].astype(W)], [ffn_w_down[1].astype(W)]], False, token)

    cosq, cosk, sa, sb = _rope_tables(positions[0])
    n1g = [norm1_g[l].reshape(1, D) for l in range(2)]
    n2g = [norm2_g[l].reshape(1, D) for l in range(2)]
    mix_w, scale = b_mix_w[0], b_scale
    kvg = c_kv_norm_g
    w_s, b_st = d_w_s[0], d_b_s[0].T

    sh1, sc1, g1m = mods[0][:3]
    h_ab = _rmsmod_fwd("ab_norm", x0, n1g[0], sc1, sh1, started)
    w_abin8, w_about = [sq(g) for g in _exchange_wait("wait_w_ab", gw_ab, h_ab)]
    w_about2 = w_about.reshape(2, 512, D)
    z8 = _mm_cols("ab_in", h_ab, w_abin8)
    ycat_ab = _ab_mix_fwd(z8, conv_w, mix_w, scale)
    y_ab, x1 = _mm_rows_resid("ab_out", ycat_ab, w_about2, x0, g1m)
    w_up0, w_dn0 = [sq(g) for g in _exchange_wait("wait_w_ffn0", gw_f0, x1)]
    w_up8, w_down4 = [w_up0, None], [w_dn0.reshape(4, 704, D), None]
    x2, ffn0_saved = _ffn_fwd(0, x1, mods[0][3:], n2g[0], w_up8[0], cw24[0], w_down4[0])

    w_cdin, w_uq, w_ukv, w_cdout = [sq(g) for g in _exchange_wait("wait_w_cd", gw_cd, x2)]
    w_cdout2 = w_cdout.reshape(2, 512, D)
    w_cd = w_cdin.reshape(8, D, 180).transpose(1, 0, 2).reshape(D, 1440)
    zc = lambda n: jnp.zeros((D, n), W)
    w_cd_pad = jnp.concatenate([w_cd[:, :384], zc(64), w_cd[:, 384:416], zc(32), w_cd[:, 416:]], axis=1)
    w_uq_pad = jnp.pad(w_uq.reshape(8, 256, 96).transpose(1, 0, 2), ((0, 0), (0, 0), (0, 32))).reshape(256, 1024)
    w_ukv_h = w_ukv.transpose(1, 0, 2)
    w_k_pad = jnp.pad(w_ukv_h[:, :, :64], ((0, 0), (0, 0), (0, 64))).reshape(128, 1024)
    w_kv_pad = jnp.concatenate([w_k_pad, w_ukv_h[:, :, 64:].reshape(128, 512)], axis=1)

    sh1, sc1, g1c = mods[1][:3]
    h_cd = _rmsmod_fwd("cd_norm", x2, n1g[1], sc1, sh1, n1g[1])
    z_cd = _mm_nn("cd_in", h_cd, w_cd_pad)
    qn, kvn = _mla_prep_fwd(z_cd, qg, kvg)
    qraw = _mm_nn("cd_uq", qn, w_uq_pad)
    kvall = _mm_nn("cd_ukv", kvn, w_kv_pad)
    q_r, k_r, v_r = _rope_fwd(qraw, kvall, z_cd, cosq, cosk, sa, sb)
    o, lse = _attn_fwd(q_r, k_r, v_r)
    ycat_cd = _sgu_fwd(z_cd, o, ln_g, ln_b, w_s, b_st)
    y_cd, x3 = _mm_rows_resid("cd_out", ycat_cd, w_cdout2, x2, g1c)
    w_up1, w_dn1 = [sq(g) for g in _exchange_wait("wait_w_ffn1", gw_f1, x3)]
    w_up8[1], w_down4[1] = w_up1, w_dn1.reshape(4, 704, D)
    x4, ffn1_saved = _ffn_fwd(1, x3, mods[1][3:], n2g[1], w_up8[1], cw24[1], w_down4[1])

    loss_local, dx4, dfg = _loss_head(x4, final_norm_g.reshape(1, D), target)

    dx3, gf1 = _ffn_bwd(1, dx4, ffn1_saved, mods[1][3:], n2g[1], w_up8[1], cw24[1], w_down4[1])

    dy, dg1c = _gate_bwd("cd_gate_bwd", dx3, y_cd, g1c)
    dycat = _mm_rows_dx("cd_out_dx", dy, w_cdout2)
    dw_cdout = _mm_rows_dw("cd_out_dw", ycat_cd, dy, out_dtype=WIRE_DTYPE)
    duv, dln_g, dln_b, dws, dbs = _sgu_bwd(z_cd, dycat, ln_g, ln_b, w_s, b_st)
    dq_r, dk_r, dv_r = _attn_bwd(q_r, k_r, v_r, o, lse, dycat)
    dqraw, dkvall, dkpe = _rope_bwd(dq_r, dk_r, dv_r, cosq, cosk, sa, sb)
    dqn = _mm_nt("cd_uq_dx", dqraw, w_uq_pad, tn=256)
    dkvn = _mm_nt("cd_ukv_dx", dkvall, w_kv_pad, tn=128)
    dw_uq_pad = _mm_tn("cd_uq_dw", qn, dqraw, tm=256)
    dw_kv_pad = _mm_tn("cd_ukv_dw", kvn, dkvall, tm=128)
    dz_cd, dqg, dkvg = _mla_prep_bwd(z_cd, qg, kvg, dqn, dkvn, dkpe, duv)
    dh_cd = _mm_nt("cd_in_dx", dz_cd, w_cd_pad)
    dw_cd_pad = _mm_tn("cd_in_dw", h_cd, dz_cd)
    dw_cd =jnp.concatenate([dw_cd_pad[:, :384], dw_cd_pad[:, 448:480], dw_cd_pad[:, 512:]], axis=1)
    dw_cd8 = dw_cd.reshape(D, 8, 180).transpose(1, 0, 2).reshape(8, 1440, 128).astype(WIRE_DTYPE)
    dw_uq8 = dw_uq_pad.reshape(256, 8, 128)[:, :, :96].transpose(1, 0, 2).reshape(8, 192, 128).astype(WIRE_DTYPE)
    dw_ukv8 = jnp.concatenate([dw_kv_pad[:, :1024].reshape(128, 8, 128)[:, :, :64], dw_kv_pad[:, 1024:].reshape(128, 8, 64)],
                              axis=2).transpose(1, 0, 2).astype(WIRE_DTYPE)
    sent_cd, token = _exchange_start("scatter_cd", [[dw_cd8], [dw_uq8], [dw_ukv8], [dw_cdout.reshape(8, 128, D)]], True, dw_cd8)
    dx2, dn1g_cd, dsc1_cd, dsh1_cd = _rmsmod_bwd("cd_norm_bwd", x2, n1g[1], sc1, dh_cd, dx3, token)

    dx1, gf0 = _ffn_bwd(0, dx2, ffn0_saved, mods[0][3:], n2g[0], w_up8[0], cw24[0], w_down4[0])

    dy, dg1m = _gate_bwd("ab_gate_bwd", dx1, y_ab, g1m)
    dycat = _mm_rows_dx("ab_out_dx", dy, w_about2)
    dw_about = _mm_rows_dw("ab_out_dw", ycat_ab, dy, out_dtype=WIRE_DTYPE)
    dz8, dconv_w, dmix_w, dscale = _ab_mix_bwd(z8, dycat, conv_w, mix_w, scale)
    dz8 = dz8.reshape(8, S, 256)
    dw_abin8 = _mm_cols_dw("ab_in_dw", h_ab, dz8, out_dtype=WIRE_DTYPE)
    sent_ab, token = _exchange_start("scatter_ab", [[dw_abin8], [dw_about.reshape(8, 128, D)]], True, dw_abin8)
    dh_ab = _mm_cols_dx("ab_in_dx", dz8, w_abin8)
    dx0, dn1g_ab, dsc1_ab, dsh1_ab = _rmsmod_bwd("ab_norm_bwd", x0, n1g[0], mods[0][1], dh_ab, dx1, token)

    dmod = jnp.stack([jnp.concatenate([dsh1_ab, dsc1_ab, dg1m, *gf0["mod"]], axis=1)[0],
                      jnp.concatenate([dsh1_cd, dsc1_cd, dg1c, *gf1["mod"]], axis=1)[0]])
    rep_grads = [dmod, jnp.concatenate([dn1g_ab, dn1g_cd]), jnp.concatenate([gf0["n2g"], gf1["n2g"]]),
                 dmix_w, dscale, dkvg, dws, dbs, dfg]
    rep_names = ["ada_b", "norm1_g", "norm2_g", "b_mix_w", "b_scale", "c_kv_norm_g", "d_w_s", "d_b_s", "final_norm_g"]
    dfcw = jnp.stack([gf0["cw24"].reshape(8, 3, 704), gf1["cw24"].reshape(8, 3, 704)])
    shard_grads = [dconv_w, dqg, dln_g, dln_b, dfcw]
    rep_buf, shard_buf = _pack(rep_grads), _pack(shard_grads)
    r_rep = rep_buf.shape[0]
    (g2,) = _exchange("gather_small_grads", [[jnp.concatenate([rep_buf, shard_buf])]], scatter=False)
    g2 = g2[:, 0]

    dmod_all = g2[:, :2 * 6 * D // 128].reshape(N_DEV, 2, N_DEV, 768)
    dmod_cols = lax.dynamic_index_in_dim(dmod_all, me, axis=2, keepdims=False).transpose(1, 0, 2)
    g_ada_w = _ada_bwd(c16, jnp.pad(dmod_cols, ((0, 0), (0, 16 - N_DEV), (0, 0))))

    p_up1, p_dn1 = [sq(p) for p in _exchange_wait("wait_scatter_ffn1", gf1["sent"], dx0)]
    p_cdin, p_uq, p_ukv, p_cdout = [sq(p) for p in _exchange_wait("wait_scatter_cd", sent_cd, dx0)]
    p_up0, p_dn0 = [sq(p) for p in _exchange_wait("wait_scatter_ffn0", gf0["sent"], dx0)]
    p_abin, p_about = [sq(p) for p in _exchange_wait("wait_scatter_ab", sent_ab, dx0)]

    res = {}

    def update(name, w, m, v, parts, shape3d):
        outs = _adamw("adamw_" + name, w.reshape(shape3d), m.reshape(shape3d), v.reshape(shape3d),
                      [p.reshape((p.shape[0],) + shape3d[1:]) for p in parts])
        res[name] = [o_.reshape(w.shape) for o_ in outs]

    update("ada_w", ada_w, m_ada_w, v_ada_w, [g_ada_w[l][None] for l in range(2)], (2, D, 768))
    update("ab_w_in", ab_w_in, m_ab_w_in, v_ab_w_in, [p_abin], (1, D, 256))
    update("ab_w_out", ab_w_out, m_ab_w_out, v_ab_w_out, [p_about], (1, 128, D))
    update("cd_w_in", cd_w_in, m_cd_w_in, v_cd_w_in, [p_cdin], (1, 1440, 128))
    update("c_w_uq", c_w_uq, m_c_w_uq, v_c_w_uq, [p_uq], (1, 192, 128))
    update("c_w_ukv", c_w_ukv, m_c_w_ukv, v_c_w_ukv, [p_ukv], (1, 128, 128))
    update("cd_w_out", cd_w_out, m_cd_w_out, v_cd_w_out, [p_cdout], (1, 128, D))
    update("ffn_w_up", ffn_w_up, m_ffn_w_up, v_ffn_w_up, [p_up0, p_up1], (2, D, 704))
    update("ffn_w_down", ffn_w_down, m_ffn_w_down, v_ffn_w_down, [p_dn0, p_dn1], (2, 352, D))

    rep_w = dict(ada_b=(ada_b, m_ada_b, v_ada_b), norm1_g=(norm1_g, m_norm1_g, v_norm1_g), norm2_g=(norm2_g, m_norm2_g, v_norm2_g),
                 b_mix_w=(b_mix_w, m_b_mix_w, v_b_mix_w), b_scale=(b_scale, m_b_scale, v_b_scale),
                 c_kv_norm_g=(c_kv_norm_g, m_c_kv_norm_g, v_c_kv_norm_g), d_w_s=(d_w_s, m_d_w_s, v_d_w_s),
                 d_b_s=(d_b_s, m_d_b_s, v_d_b_s), final_norm_g=(final_norm_g, m_final_norm_g, v_final_norm_g))
    rep_packed = [_pack([rep_w[n][k] for n in rep_names]) for k in range(3)]
    rep_out = _adamw("adamw_replicated", *[a[None] for a in rep_packed], [g2[:, :r_rep]])
    rep_shapes = [rep_w[n][0].shape for n in rep_names]
    for k, n in enumerate(rep_names):
        res[n] = [_unpack(o_, rep_shapes)[k] for o_ in rep_out]

    shard_sum = _sum_parts("sum_small_grads", g2[:, r_rep:])
    g_conv, g_qg, g_lng, g_lnb, g_fcw = _unpack(shard_sum, [(3, 512), (256,), (512,), (512,), (2, 8, 3, 704)])
    mine = lambda a, n, axis: lax.dynamic_slice_in_dim(a, me * n, n, axis=axis)
    sh_names = ["a_conv_w", "c_q_norm_g", "d_ln_g", "d_ln_b", "ffn_conv_w"]
    sh_grads = [mine(g_conv, 64, 1), mine(g_qg, 32, 0), mine(g_lng, 64, 0), mine(g_lnb, 64, 0),
                lax.dynamic_index_in_dim(g_fcw, me, axis=1, keepdims=False)]
    sh_w = dict(a_conv_w=(a_conv_w, m_a_conv_w, v_a_conv_w), c_q_norm_g=(c_q_norm_g, m_c_q_norm_g, v_c_q_norm_g),
                d_ln_g=(d_ln_g, m_d_ln_g, v_d_ln_g), d_ln_b=(d_ln_b, m_d_ln_b, v_d_ln_b),
                ffn_conv_w=(ffn_conv_w, m_ffn_conv_w, v_ffn_conv_w))
    sh_packed = [_pack([sh_w[n][k] for n in sh_names]) for k in range(3)]
    sh_out = _adamw("adamw_small_shards", *[a[None] for a in sh_packed], [_pack(sh_grads)[None]])
    sh_shapes = [sh_w[n][0].shape for n in sh_names]
    for k, n in enumerate(sh_names):
        res[n] = [_unpack(o_, sh_shapes)[k] for o_ in sh_out]

    loss = lax.psum(loss_local[0, 0], ("x", "y", "c"))
    order = ["ada_w", "ada_b", "norm1_g", "norm2_g", "ab_w_in", "a_conv_w", "b_mix_w", "b_scale", "ab_w_out", "cd_w_in", "c_q_norm_g",
             "c_w_uq", "c_kv_norm_g", "c_w_ukv", "d_ln_g", "d_ln_b", "d_w_s", "d_b_s", "cd_w_out", "ffn_w_up", "ffn_conv_w",
             "ffn_w_down", "final_norm_g"]
    return (loss, dx0[None], *[res[n][0] for n in order], *[res[n][1] for n in order], *[res[n][2] for n in order],
            *[res[n][3] for n in order])
```

```python
import functools
import math

import jax
import jax.numpy as jnp
from jax import lax
from jax.experimental import pallas as pl
from jax.experimental.pallas import tpu as pltpu

F32 = jnp.float32
BF16 = jnp.bfloat16
_MXU_DTYPE = BF16
WIRE_DTYPE = BF16
_VMEM_LIMIT = 56 * 2 ** 20
N_DEV = 8
EPS = 1e-6
POOL_WINDOWS = (2, 4, 8, 16)
ATTN_SCALE = (64 + 32) ** -0.5
ADAM_LR, ADAM_B1, ADAM_B2, ADAM_EPS, ADAM_WD, ADAM_STEP = 0.001, 0.9, 0.999, 1e-08, 0.01, 10
MESH = pl.DeviceIdType.MESH
ANY = pl.BlockSpec(memory_space=pl.ANY)


def _cp(*sem):
    return pltpu.CompilerParams(dimension_semantics=sem, vmem_limit_bytes=_VMEM_LIMIT)


def _dot(a, b, contract):
    dn = {"nn": (((1,), (0,)), ((), ())), "nt": (((1,), (1,)), ((), ())), "tn": (((0,), (0,)), ((), ()))}[contract]
    return lax.dot_general(a.astype(_MXU_DTYPE), b.astype(_MXU_DTYPE), dn, preferred_element_type=F32)


def _my_position():
    x, y, c = lax.axis_index("x"), lax.axis_index("y"), lax.axis_index("c")
    return x, y, c, 4 * x + 2 * y + c


def _exchange(name, groups, scatter):
    flat = [a for g in groups for a in g]
    n_in, n_grp = len(flat), len(groups)
    out_shapes = []
    for g in groups:
        slab = g[0].shape[1:] if scatter else g[0].shape
        out_shapes.append(jax.ShapeDtypeStruct((N_DEV, len(g)) + tuple(slab), g[0].dtype))

    def body(*refs):
        ins, outs = refs[:n_in], refs[n_in:n_in + n_grp]
        send_sems, recv_sems, local_sems = refs[n_in + n_grp:]
        x, y, c, me = _my_position()
        i = 0
        for gi, g in enumerate(groups):
            for l in range(len(g)):
                src = ins[i]
                i += 1
                pltpu.make_async_copy(src.at[me] if scatter else src, outs[gi].at[me, l], local_sems.at[gi]).start()
                for k in range(1, N_DEV):
                    px = 1 - x if k & 4 else x
                    py = 1 - y if k & 2 else y
                    pc = 1 - c if k & 1 else c
                    peer = 4 * px + 2 * py + pc
                    pltpu.make_async_remote_copy(
                        src_ref=src.at[peer] if scatter else src, dst_ref=outs[gi].at[me, l],
                        send_sem=send_sems.at[gi], recv_sem=recv_sems.at[gi],
                        device_id=(px, py, pc), device_id_type=MESH).start()
        for gi in range(n_grp):
            mine = outs[gi].at[me]
            pltpu.make_async_copy(mine, mine, local_sems.at[gi]).wait()
            seven = outs[gi].at[pl.ds(0, N_DEV - 1)]
            w = pltpu.make_async_remote_copy(src_ref=seven, dst_ref=seven, send_sem=send_sems.at[gi],
                                             recv_sem=recv_sems.at[gi], device_id=(x, y, c), device_id_type=MESH)
            w.wait_send()
            w.wait_recv()

    return pl.pallas_call(
        body, name=name, out_shape=tuple(out_shapes),
        in_specs=[ANY] * n_in, out_specs=tuple([ANY] * n_grp),
        scratch_shapes=[pltpu.SemaphoreType.DMA((n_grp,)), pltpu.SemaphoreType.DMA((n_grp,)),
                        pltpu.SemaphoreType.DMA((n_grp,))],
        compiler_params=pltpu.CompilerParams(has_side_effects=True),
    )(*flat)


HBM_SPEC = pl.BlockSpec(memory_space=pltpu.HBM)
SEM_SPEC = pl.BlockSpec(memory_space=pltpu.SEMAPHORE)
EFFECT = pltpu.SideEffectType.DATAFLOW_SIDE_EFFECTING


def _put_mine(name, srcs, scatter, me):
    n = len(srcs)
    slabs = [tuple(s.shape[1:] if scatter else s.shape) for s in srcs]

    def body(me_ref, *refs):
        for i in range(n):
            refs[n + i][...] = refs[i][...]

    def at_me(slab):
        return pl.BlockSpec((None,) + slab, lambda g, me_ref, nd=len(slab): (me_ref[0],) + (0,) * nd)

    def whole(slab):
        return pl.BlockSpec(slab, lambda g, me_ref, nd=len(slab): (0,) * nd)

    return pl.pallas_call(
        body, name=name,
        grid_spec=pltpu.PrefetchScalarGridSpec(
            num_scalar_prefetch=1, grid=(1,),
            in_specs=[at_me(slab) if scatter else whole(slab) for slab in slabs], out_specs=[at_me(slab) for slab in slabs]),
        out_shape=[jax.ShapeDtypeStruct((N_DEV,) + slab, s.dtype) for slab, s in zip(slabs, srcs)],
        compiler_params=_cp("arbitrary"))(me.reshape(1), *srcs)


def _exchange_start(name, srcs, scatter, after, me):
    n = len(srcs)
    lands = _put_mine(name + "_mine", srcs, scatter, me)
    srcs = [pltpu.with_memory_space_constraint(a, pltpu.HBM) for a in srcs]
    lands = [pltpu.with_memory_space_constraint(a, pltpu.HBM) for a in lands]

    def body(*refs):
        ins, land = refs[:n], refs[n:2 * n]
        send_sems, recv_sems, token = refs[2 * n + 1], refs[2 * n + 2], refs[-1]
        x, y, c, me_in = _my_position()
        for i in range(n):
            for k in range(1, N_DEV):
                px = 1 - x if k & 4 else x
                py = 1 - y if k & 2 else y
                pc = 1 - c if k & 1 else c
                pltpu.make_async_remote_copy(
                    src_ref=ins[i].at[4 * px + 2 * py + pc] if scatter else ins[i], dst_ref=land[i].at[me_in],
                    send_sem=send_sems.at[i], recv_sem=recv_sems.at[i],
                    device_id=(px, py, pc), device_id_type=MESH).start()
        token[...] = jnp.zeros_like(token)

    outs = pl.pallas_call(
        body, name=name,
        out_shape=(pltpu.SemaphoreType.DMA((n,)), pltpu.SemaphoreType.DMA((n,)),
                   *[pltpu.HBM(a.shape, a.dtype) for a in srcs], *[pltpu.HBM(a.shape, a.dtype) for a in lands],
                   jax.ShapeDtypeStruct((8, 128), F32)),
        in_specs=[HBM_SPEC] * (2 * n) + [ANY],
        out_specs=(SEM_SPEC, SEM_SPEC, *[HBM_SPEC] * (2 * n), pl.BlockSpec(memory_space=pltpu.VMEM)),
        input_output_aliases={i: 2 + i for i in range(2 * n)},
        compiler_params=pltpu.CompilerParams(has_side_effects=EFFECT),
    )(*srcs, *lands, after)
    return (outs[0], outs[1], outs[2:2 + n], outs[2 + n:2 + 2 * n]), outs[-1]


def _exchange_wait(name, handle, after):
    send_sems, recv_sems, srcs, lands = handle
    n = len(srcs)

    def body(*refs):
        land, send_ref, recv_ref = refs[n:2 * n], refs[2 * n], refs[2 * n + 1]
        x, y, c, _ = _my_position()
        for i in range(n):
            seven = land[i].at[pl.ds(0, N_DEV - 1)]
            w = pltpu.make_async_remote_copy(src_ref=seven, dst_ref=seven, send_sem=send_ref.at[i], recv_sem=recv_ref.at[i],
                                             device_id=(x, y, c), device_id_type=MESH)
            w.wait_send()
            w.wait_recv()

    outs = pl.pallas_call(
        body, name=name,
        out_shape=(*[pltpu.HBM(a.shape, a.dtype) for a in srcs], *[pltpu.HBM(a.shape, a.dtype) for a in lands]),
        in_specs=[HBM_SPEC] * (2 * n) + [SEM_SPEC, SEM_SPEC, ANY],
        out_specs=tuple([HBM_SPEC] * (2 * n)),
        input_output_aliases={i: i for i in range(2 * n)},
        compiler_params=pltpu.CompilerParams(has_side_effects=EFFECT),
    )(*srcs, *lands, send_sems, recv_sems, after)
    return outs[n:]


def _pack(arrs):
    flat = jnp.concatenate([a.reshape(-1).astype(F32) for a in arrs])
    n = flat.shape[0]
    rows = -(-n // 1024) * 8
    return jnp.pad(flat, (0, rows * 128 - n)).reshape(rows, 128)


def _unpack(buf, shapes, lead=()):
    flat = buf.reshape(lead + (-1,))
    out, off = [], 0
    for s in shapes:
        n = math.prod(s)
        out.append(flat[..., off:off + n].reshape(lead + tuple(s)))
        off += n
    return out


def _mm(name, a, a_spec, b, b_spec, out_sds, o_spec, grid, contract, nk=1):
    o_blk = tuple(d for d in o_spec.block_shape if d is not None)

    def body(a_ref, b_ref, o_ref, *acc):
        r = _dot(a_ref[...], b_ref[...], contract)
        if nk == 1:
            o_ref[...] = r.astype(o_ref.dtype)
        else:
            k = pl.program_id(len(grid) - 1)

            @pl.when(k == 0)
            def _():
                acc[0][...] = r

            @pl.when(k > 0)
            def _():
                acc[0][...] += r

            @pl.when(k == nk - 1)
            def _():
                o_ref[...] = acc[0][...].astype(o_ref.dtype)

    sem = ("parallel",) * (len(grid) - 1) + (("arbitrary",) if nk > 1 else ("parallel",))
    return pl.pallas_call(
        body, name=name, out_shape=out_sds, grid=grid, in_specs=[a_spec, b_spec], out_specs=o_spec,
        scratch_shapes=[pltpu.VMEM(o_blk, F32)] if nk > 1 else [], compiler_params=_cp(*sem))(a, b)


def _tile(n, want):
    t = min(n, want)
    assert n % t == 0, (n, t)
    return t


def _mm_nn(name, a, b, out_dtype=F32, tm=512, tn=512):
    (M, K), N = a.shape, b.shape[1]
    tm, tn = _tile(M, tm), _tile(N, tn)
    return _mm(name, a, pl.BlockSpec((tm, K), lambda i, j: (i, 0)), b, pl.BlockSpec((K, tn), lambda i, j: (0, j)),
               jax.ShapeDtypeStruct((M, N), out_dtype), pl.BlockSpec((tm, tn), lambda i, j: (i, j)),
               (M // tm, N // tn), "nn")


def _mm_nt(name, a, b, out_dtype=F32, tm=512, tn=512):
    (M, K), N = a.shape, b.shape[0]
    tm, tn = _tile(M, tm), _tile(N, tn)
    return _mm(name, a, pl.BlockSpec((tm, K), lambda i, j: (i, 0)), b, pl.BlockSpec((tn, K), lambda i, j: (j, 0)),
               jax.ShapeDtypeStruct((M, N), out_dtype), pl.BlockSpec((tm, tn), lambda i, j: (i, j)),
               (M // tm, N // tn), "nt")


def _mm_tn(name, a, b, out_dtype=F32, tm=512, tn=512):
    (K, M), N = a.shape, b.shape[1]
    tm, tn = _tile(M, tm), _tile(N, tn)
    return _mm(name, a, pl.BlockSpec((K, tm), lambda i, j: (0, i)), b, pl.BlockSpec((K, tn), lambda i, j: (0, j)),
               jax.ShapeDtypeStruct((M, N), out_dtype), pl.BlockSpec((tm, tn), lambda i, j: (i, j)),
               (M // tm, N // tn), "tn")


def _mm_cols(name, a, w, out_dtype=F32, tm=512):
    (M, K), (J, _, n) = a.shape, w.shape
    tm = _tile(M, tm)
    return _mm(name, a, pl.BlockSpec((tm, K), lambda j, i: (i, 0)), w, pl.BlockSpec((None, K, n), lambda j, i: (j, 0, 0)),
               jax.ShapeDtypeStruct((J, M, n), out_dtype), pl.BlockSpec((None, tm, n), lambda j, i: (j, i, 0)),
               (J, M // tm), "nn")


def _mm_cols_dx(name, d, w, out_dtype=F32, tm=512):
    (J, M, n), K = d.shape, w.shape[1]
    tm = _tile(M, tm)
    return _mm(name, d, pl.BlockSpec((None, tm, n), lambda i, j: (j, i, 0)), w, pl.BlockSpec((None, K, n), lambda i, j: (j, 0, 0)),
               jax.ShapeDtypeStruct((M, K), out_dtype), pl.BlockSpec((tm, K), lambda i, j: (i, 0)),
               (M // tm, J), "nt", nk=J)


def _mm_cols_dw(name, a, d, out_dtype=F32, tk=512):
    (M, K), (J, _, n) = a.shape, d.shape
    tk = _tile(K, tk)
    return _mm(name, a, pl.BlockSpec((M, tk), lambda j, i: (0, i)), d, pl.BlockSpec((None, M, n), lambda j, i: (j, 0, 0)),
               jax.ShapeDtypeStruct((J, K, n), out_dtype), pl.BlockSpec((None, tk, n), lambda j, i: (j, i, 0)),
               (J, K // tk), "tn")


def _mm_rows_resid(name, a, w, resid, gate, tm=512, tn=512):
    (Q, M, k), N = a.shape, w.shape[2]
    tm, tn = _tile(M, tm), _tile(N, tn)

    def body(a_ref, w_ref, r_ref, g_ref, y_ref, x_ref, acc):
        q = pl.program_id(2)
        r = _dot(a_ref[...], w_ref[...], "nn")

        @pl.when(q == 0)
        def _():
            acc[...] = r

        @pl.when(q > 0)
        def _():
            acc[...] += r

        @pl.when(q == Q - 1)
        def _():
            y = acc[...]
            y_ref[...] = y
            x_ref[...] = r_ref[...] + g_ref[...] * y

    return pl.pallas_call(
        body, name=name, grid=(M // tm, N // tn, Q),
        out_shape=(jax.ShapeDtypeStruct((M, N), F32), jax.ShapeDtypeStruct((M, N), F32)),
        in_specs=[pl.BlockSpec((None, tm, k), lambda i, j, q: (q, i, 0)), pl.BlockSpec((None, k, tn), lambda i, j, q: (q, 0, j)),
                  pl.BlockSpec((tm, tn), lambda i, j, q: (i, j)), pl.BlockSpec((1, tn), lambda i, j, q: (0, j))],
        out_specs=(pl.BlockSpec((tm, tn), lambda i, j, q: (i, j)), pl.BlockSpec((tm, tn), lambda i, j, q: (i, j))),
        scratch_shapes=[pltpu.VMEM((tm, tn), F32)], compiler_params=_cp("parallel", "parallel", "arbitrary"))(a, w, resid, gate)


def _mm_rows_dx(name, d, w, out_dtype=F32, tm=512):
    (M, N), (Q, k, _) = d.shape, w.shape
    tm = _tile(M, tm)
    return _mm(name, d, pl.BlockSpec((tm, N), lambda q, i: (i, 0)), w, pl.BlockSpec((None, k, N), lambda q, i: (q, 0, 0)),
               jax.ShapeDtypeStruct((Q, M, k), out_dtype), pl.BlockSpec((None, tm, k), lambda q, i: (q, i, 0)),
               (Q, M // tm), "nt")


def _mm_rows_dw(name, a, d, out_dtype=F32, tn=512):
    (Q, M, k), N = a.shape, d.shape[1]
    tn = _tile(N, tn)
    return _mm(name, a, pl.BlockSpec((None, M, k), lambda q, j: (q, 0, 0)), d, pl.BlockSpec((M, tn), lambda q, j: (0, j)),
               jax.ShapeDtypeStruct((Q, k, N), out_dtype), pl.BlockSpec((None, k, tn), lambda q, j: (q, 0, j)),
               (Q, N // tn), "tn")


def _silu(v):
    return v * jax.nn.sigmoid(v)


def _ada_fwd(c16, ada_w):
    L, D, n = ada_w.shape

    def body(c_ref, w_ref, o_ref):
        o_ref[...] = _dot(_silu(c_ref[...]), w_ref[...], "nn")

    return pl.pallas_call(
        body, name="ada_fwd", grid=(L,), out_shape=jax.ShapeDtypeStruct((L, 16, n), F32),
        in_specs=[pl.BlockSpec((16, D), lambda l: (0, 0)), pl.BlockSpec((None, D, n), lambda l: (l, 0, 0))],
        out_specs=pl.BlockSpec((None, 16, n), lambda l: (l, 0, 0)), compiler_params=_cp("parallel"))(c16, ada_w)


def _ada_bwd(c16, dmod16):
    L, _, n = dmod16.shape
    D = c16.shape[1]

    def body(c_ref, d_ref, o_ref):
        o_ref[...] = _dot(_silu(c_ref[...]), d_ref[...], "tn")

    return pl.pallas_call(
        body, name="ada_bwd", grid=(L,), out_shape=jax.ShapeDtypeStruct((L, D, n), F32),
        in_specs=[pl.BlockSpec((16, D), lambda l: (0, 0)), pl.BlockSpec((None, 16, n), lambda l: (l, 0, 0))],
        out_specs=pl.BlockSpec((None, D, n), lambda l: (l, 0, 0)), compiler_params=_cp("parallel"))(c16, dmod16)


def _row_spec(tr, n):
    return pl.BlockSpec((tr, n), lambda i: (i, 0))


def _vec_spec(n):
    return pl.BlockSpec((1, n), lambda i: (0, 0))


def _rmsmod_fwd(name, x, g, sc, sh, after, tr=256):
    S, D = x.shape

    def body(x_ref, g_ref, sc_ref, sh_ref, after_ref, h_ref):
        xv = x_ref[...]
        rstd = lax.rsqrt(jnp.mean(xv * xv, axis=-1, keepdims=True) + EPS)
        y = xv * rstd * g_ref[...]
        h_ref[...] = (y * (1.0 + sc_ref[...]) + sh_ref[...]).astype(h_ref.dtype)

    return pl.pallas_call(
        body, name=name, grid=(S // tr,), out_shape=jax.ShapeDtypeStruct((S, D), _MXU_DTYPE),
        in_specs=[_row_spec(tr, D), _vec_spec(D), _vec_spec(D), _vec_spec(D), ANY], out_specs=_row_spec(tr, D),
        compiler_params=_cp("parallel"))(x, g, sc, sh, after)


def _acc_rows(ref, val, first):
    s = jnp.sum(val, axis=0, keepdims=True)

    @pl.when(first)
    def _():
        ref[...] = s

    @pl.when(jnp.logical_not(first))
    def _():
        ref[...] += s


def _rmsmod_bwd(name, x, g, sc, dh, dres, after, tr=256):
    S, D = x.shape

    def body(x_ref, g_ref, sc_ref, dh_ref, dres_ref, after_ref, dx_ref, dg_ref, dsc_ref, dsh_ref):
        first = pl.program_id(0) == 0
        xv, dh_v, gv = x_ref[...], dh_ref[...], g_ref[...]
        rstd = lax.rsqrt(jnp.mean(xv * xv, axis=-1, keepdims=True) + EPS)
        xhat = xv * rstd
        _acc_rows(dsh_ref, dh_v, first)
        _acc_rows(dsc_ref, dh_v * (xhat * gv), first)
        dyg = dh_v * (1.0 + sc_ref[...])
        _acc_rows(dg_ref, dyg * xhat, first)
        dxhat = dyg * gv
        dx_ref[...] = dres_ref[...] + rstd * (dxhat - xhat * jnp.mean(dxhat * xhat, axis=-1, keepdims=True))

    vec = jax.ShapeDtypeStruct((1, D), F32)
    return pl.pallas_call(
        body, name=name, grid=(S // tr,), out_shape=(jax.ShapeDtypeStruct((S, D), F32), vec, vec, vec),
        in_specs=[_row_spec(tr, D), _vec_spec(D), _vec_spec(D), _row_spec(tr, D), _row_spec(tr, D), ANY],
        out_specs=(_row_spec(tr, D), _vec_spec(D), _vec_spec(D), _vec_spec(D)),
        compiler_params=_cp("arbitrary"))(x, g, sc, dh, dres, after)


def _loss_head(x, g, target, tr=256):
    S, D = x.shape

    def body(x_ref, g_ref, t_ref, loss_ref, dx_ref, dg_ref):
        first = pl.program_id(0) == 0
        xv, gv = x_ref[...], g_ref[...]
        rstd = lax.rsqrt(jnp.mean(xv * xv, axis=-1, keepdims=True) + EPS)
        xhat = xv * rstd
        err = xhat * gv - t_ref[...]
        part = 0.5 * jnp.sum(jnp.mean(err * err, axis=-1, keepdims=True), axis=0, keepdims=True)

        @pl.when(first)
        def _():
            loss_ref[...] = part

        @pl.when(jnp.logical_not(first))
        def _():
            loss_ref[...] += part

        dout = err * (1.0 / D)
        _acc_rows(dg_ref, dout * xhat, first)
        dxhat = dout * gv
        dx_ref[...] = rstd * (dxhat - xhat * jnp.mean(dxhat * xhat, axis=-1, keepdims=True))

    return pl.pallas_call(
        body, name="loss_head", grid=(S // tr,),
        out_shape=(jax.ShapeDtypeStruct((1, 1), F32), jax.ShapeDtypeStruct((S, D), F32), jax.ShapeDtypeStruct((1, D), F32)),
        in_specs=[_row_spec(tr, D), _vec_spec(D), _row_spec(tr, D)],
        out_specs=(pl.BlockSpec((1, 1), lambda i: (0, 0)), _row_spec(tr, D), _vec_spec(D)),
        compiler_params=_cp("arbitrary"))(x, g, target)


def _gate_bwd(name, dx, y, gate, tr=256):
    S, D = dx.shape

    def body(dx_ref, y_ref, g_ref, dy_ref, dg_ref):
        dxv = dx_ref[...]
        dy_ref[...] = (g_ref[...] * dxv).astype(dy_ref.dtype)
        _acc_rows(dg_ref, dxv * y_ref[...], pl.program_id(0) == 0)

    return pl.pallas_call(
        body, name=name, grid=(S // tr,),
        out_shape=(jax.ShapeDtypeStruct((S, D), _MXU_DTYPE), jax.ShapeDtypeStruct((1, D), F32)),
        in_specs=[_row_spec(tr, D), _row_spec(tr, D), _vec_spec(D)], out_specs=(_row_spec(tr, D), _vec_spec(D)),
        compiler_params=_cp("arbitrary"))(dx, y, gate)


def _shift_down(v, k):
    t = lax.broadcasted_iota(jnp.int32, v.shape, 0)
    return jnp.where(t >= k, pltpu.roll(v, k, axis=0), 0.0)


def _shift_up(v, k):
    n = v.shape[0]
    t = lax.broadcasted_iota(jnp.int32, v.shape, 0)
    return jnp.where(t < n - k, pltpu.roll(v, n - k, axis=0), 0.0)


def _window_sum(p, w, shift):
    s, k = p, 1
    while k < w:
        s = s + shift(s, k)
        k *= 2
    return s


def _pool_count(shape, w):
    t = lax.broadcasted_iota(jnp.int32, shape, 0)
    return jnp.minimum(t + 1, w).astype(F32)


def _ab_specs(S):
    zs = [pl.BlockSpec((None, S, 128), functools.partial(lambda g, q: (2 * q + g // 2, 0, g % 2), q=q)) for q in range(4)]
    return zs


def _ab_mix_fwd(z8, conv_w, mix_w, scale):
    S = z8.shape[1]

    def body(b_ref, c_ref, a_ref, p_ref, w_ref, mix_ref, sc_ref, y_ref):
        g = pl.program_id(0)
        cg = c_ref[...] * a_ref[...]
        w = w_ref[...]
        conv = w[0:1] * _shift_down(cg, 2) + w[1:2] * _shift_down(cg, 1) + w[2:3] * cg
        y_ref[0] = (b_ref[...] * conv).astype(y_ref.dtype)
        for gg, win in enumerate(POOL_WINDOWS):
            @pl.when(g == gg)
            def _(win=win):
                p = p_ref[...]
                pooled = _window_sum(p, win, _shift_down) / _pool_count(p.shape, win) - p
                y_ref[1] = (_dot(pooled, mix_ref[...], "nn") * sc_ref[...]).astype(y_ref.dtype)

    return pl.pallas_call(
        body, name="ab_mix_fwd", grid=(4,), out_shape=jax.ShapeDtypeStruct((2, S, 512), _MXU_DTYPE),
        in_specs=_ab_specs(S) + [pl.BlockSpec((3, 128), lambda g: (0, g)), pl.BlockSpec((None, 128, 128), lambda g: (g, 0, 0)),
                                 pl.BlockSpec((1, 128), lambda g: (0, g))],
        out_specs=pl.BlockSpec((2, S, 128), lambda g: (0, 0, g)), compiler_params=_cp("parallel"))(z8, z8, z8, z8, conv_w, mix_w, scale)


def _ab_mix_bwd(z8, dycat2, conv_w, mix_w, scale):
    S = z8.shape[1]

    def body(b_ref, c_ref, a_ref, p_ref, dy_ref, w_ref, mix_ref, sc_ref, dz_ref, dw_ref, dmix_ref, dsc_ref):
        g = pl.program_id(0)
        bv, cv, av, w = b_ref[...], c_ref[...], a_ref[...], w_ref[...]
        dya = dy_ref[0]
        cg = cv * av
        cg1, cg2 = _shift_down(cg, 1), _shift_down(cg, 2)
        conv = w[0:1] * cg2 + w[1:2] * cg1 + w[2:3] * cg
        dz_ref[0] = (dya * conv).astype(dz_ref.dtype)
        dconv = dya * bv
        dcg = w[2:3] * dconv + w[1:2] * _shift_up(dconv, 1) + w[0:1] * _shift_up(dconv, 2)
        dz_ref[1] = (dcg * av).astype(dz_ref.dtype)
        dz_ref[2] = (dcg * cv).astype(dz_ref.dtype)
        dw_ref[0:1, :] = jnp.sum(dconv * cg2, axis=0, keepdims=True)
        dw_ref[1:2, :] = jnp.sum(dconv * cg1, axis=0, keepdims=True)
        dw_ref[2:3, :] = jnp.sum(dconv * cg, axis=0, keepdims=True)
        for gg, win in enumerate(POOL_WINDOWS):
            @pl.when(g == gg)
            def _(win=win):
                p, dyb, mix = p_ref[...], dy_ref[1], mix_ref[...]
                cnt = _pool_count(p.shape, win)
                pooled = _window_sum(p, win, _shift_down) / cnt - p
                dsc_ref[...] = jnp.sum(dyb * _dot(pooled, mix, "nn"), axis=0, keepdims=True)
                dmixed = dyb * sc_ref[...]
                dmix_ref[...] = _dot(pooled, dmixed, "tn")
                dpooled = _dot(dmixed, mix, "nt")
                dz_ref[3] = (_window_sum(dpooled / cnt, win, _shift_up) - dpooled).astype(dz_ref.dtype)

    return pl.pallas_call(
        body, name="ab_mix_bwd", grid=(4,),
        out_shape=(jax.ShapeDtypeStruct((4, 2, S, 256), _MXU_DTYPE), jax.ShapeDtypeStruct((3, 512), F32),
                   jax.ShapeDtypeStruct((4, 128, 128), F32), jax.ShapeDtypeStruct((1, 512), F32)),
        in_specs=_ab_specs(S) + [pl.BlockSpec((2, S, 128), lambda g: (0, 0, g)), pl.BlockSpec((3, 128), lambda g: (0, g)),
                                 pl.BlockSpec((None, 128, 128), lambda g: (g, 0, 0)), pl.BlockSpec((1, 128), lambda g: (0, g))],
        out_specs=(pl.BlockSpec((4, None, S, 128), lambda g: (0, g // 2, 0, g % 2)), pl.BlockSpec((3, 128), lambda g: (0, g)),
                   pl.BlockSpec((None, 128, 128), lambda g: (g, 0, 0)), pl.BlockSpec((1, 128), lambda g: (0, g))),
        compiler_params=_cp("parallel"))(z8, z8, z8, z8, dycat2, conv_w, mix_w, scale)


HALO = 8


def _ffn_specs(S, n, tr):
    nb = S // HALO
    tile = pl.BlockSpec((2, None, tr, n), lambda j, i: (0, j, i, 0))
    prev = pl.BlockSpec((2, None, HALO, n), lambda j, i: (0, j, jnp.maximum(i * (tr // HALO) - 1, 0), 0))
    nxt = pl.BlockSpec((2, None, HALO, n), lambda j, i: (0, j, jnp.minimum((i + 1) * (tr // HALO), nb - 1), 0))
    cw = pl.BlockSpec((2, None, 3, n), lambda j, i: (0, j, 0, 0))
    return tile, prev, nxt, cw


def _conv_rows(ext, w, lo, tr):
    n = ext.shape[0]
    return (w[0:1] * pltpu.roll(ext, 2, axis=0)[lo:lo + tr] + w[1:2] * pltpu.roll(ext, 1, axis=0)[lo:lo + tr]
            + w[2:3] * ext[lo:lo + tr])


def _ffn_gate_fwd(name, u24, cw24, tr=256):
    _, J, S, n = u24.shape
    tile, prev, _, cw = _ffn_specs(S, n, tr)

    def body(u_ref, up_ref, w_ref, a_ref):
        keep = (pl.program_id(1) > 0).astype(F32)
        z = []
        for h in range(2):
            ext = jnp.concatenate([up_ref[h] * keep, u_ref[h]], axis=0)
            z.append(_conv_rows(ext, w_ref[h], HALO, tr))
        a_ref[...] = (_silu(z[0]) * z[1]).astype(a_ref.dtype)

    return pl.pallas_call(
        body, name=name, grid=(J, S // tr), out_shape=jax.ShapeDtypeStruct((J, S, n), _MXU_DTYPE),
        in_specs=[tile, prev, cw], out_specs=pl.BlockSpec((None, tr, n), lambda j, i: (j, i, 0)),
        compiler_params=_cp("parallel", "parallel"))(u24, u24, cw24)


def _ffn_gate_bwd(name, u24, cw24, da4, tr=256):
    _, J, S, n = u24.shape
    tile, prev, nxt, cw = _ffn_specs(S, n, tr)
    nb = S // HALO
    ext_rows = tr + 2 * HALO

    def body(u_ref, up_ref, un_ref, w_ref, da_ref, dan_ref, du_ref, dcw_ref):
        i = pl.program_id(1)
        first = i == 0
        keep_prev = (i > 0).astype(F32)
        keep_next = (i < S // tr - 1).astype(F32)
        ext = [jnp.concatenate([up_ref[h] * keep_prev, u_ref[h], un_ref[h]], axis=0) for h in range(2)]
        w = [w_ref[h] for h in range(2)]
        zg = _conv_rows(ext[0], w[0], HALO, tr + HALO)
        zu = _conv_rows(ext[1], w[1], HALO, tr + HALO)
        da = jnp.concatenate([da_ref[...], dan_ref[...] * keep_next], axis=0)
        sg = jax.nn.sigmoid(zg)
        dz = [da * zu * (sg * (1.0 + zg * (1.0 - sg))), da * (zg * sg)]
        m = tr + HALO
        for h in range(2):
            d = dz[h]
            du = w[h][2:3] * d[:tr] + w[h][1:2] * pltpu.roll(d, m - 1, axis=0)[:tr] + w[h][0:1] * pltpu.roll(d, m - 2, axis=0)[:tr]
            du_ref[h] = du.astype(du_ref.dtype)
            dt = d[:tr]
            e = ext[h]
            parts = [jnp.sum(dt * pltpu.roll(e, 2, axis=0)[HALO:HALO + tr], axis=0, keepdims=True),
                     jnp.sum(dt * pltpu.roll(e, 1, axis=0)[HALO:HALO + tr], axis=0, keepdims=True),
                     jnp.sum(dt * e[HALO:HALO + tr], axis=0, keepdims=True)]
            for k in range(3):
                @pl.when(first)
                def _(k=k, h=h):
                    dcw_ref[h, k:k + 1, :] = parts[k]

                @pl.when(jnp.logical_not(first))
                def _(k=k, h=h):
                    dcw_ref[h, k:k + 1, :] += parts[k]

    da_tile = pl.BlockSpec((None, tr, n), lambda j, i: (j, i, 0))
    da_next = pl.BlockSpec((None, HALO, n), lambda j, i: (j, jnp.minimum((i + 1) * (tr // HALO), nb - 1), 0))
    return pl.pallas_call(
        body, name=name, grid=(J, S // tr),
        out_shape=(jax.ShapeDtypeStruct((2, J, S, n), _MXU_DTYPE), jax.ShapeDtypeStruct((2, J, 3, n), F32)),
        in_specs=[tile, prev, nxt, cw, da_tile, da_next], out_specs=(tile, cw),
        compiler_params=_cp("parallel", "arbitrary"))(u24, u24, u24, cw24, da4, da4)


def _rms_rows(v, g):
    rstd = lax.rsqrt(jnp.mean(v * v, axis=-1, keepdims=True) + EPS)
    return v * rstd * g


def _rms_rows_bwd(v, g, dy):
    rstd = lax.rsqrt(jnp.mean(v * v, axis=-1, keepdims=True) + EPS)
    vhat = v * rstd
    dvhat = dy * g
    return rstd * (dvhat - vhat * jnp.mean(dvhat * vhat, axis=-1, keepdims=True)), dy * vhat


def _mla_prep_fwd(z, qg, kvg, tr=256):
    S = z.shape[0]

    def body(q_ref, kv_ref, qg_ref, kvg_ref, qn_ref, kvn_ref):
        qn_ref[...] = _rms_rows(q_ref[...], qg_ref[...]).astype(qn_ref.dtype)
        kvn_ref[...] = _rms_rows(kv_ref[...], kvg_ref[...]).astype(kvn_ref.dtype)

    return pl.pallas_call(
        body, name="mla_prep_fwd", grid=(S // tr,),
        out_shape=(jax.ShapeDtypeStruct((S, 256), _MXU_DTYPE), jax.ShapeDtypeStruct((S, 128), _MXU_DTYPE)),
        in_specs=[pl.BlockSpec((tr, 256), lambda i: (i, 0)), pl.BlockSpec((tr, 128), lambda i: (i, 2)), _vec_spec(256), _vec_spec(128)],
        out_specs=(_row_spec(tr, 256), _row_spec(tr, 128)), compiler_params=_cp("parallel"))(z, z, qg, kvg)


def _mla_prep_bwd(z, qg, kvg, dqn, dkvn, dkpe, duv, tr=256):
    S = z.shape[0]

    def body(q_ref, kv_ref, qg_ref, kvg_ref, dqn_ref, dkvn_ref, dkpe_ref, duv_ref, dz_ref, dqg_ref, dkvg_ref):
        first = pl.program_id(0) == 0
        dq, dqg = _rms_rows_bwd(q_ref[...], qg_ref[...], dqn_ref[...])
        dkv, dkvg = _rms_rows_bwd(kv_ref[...], kvg_ref[...], dkvn_ref[...])
        _acc_rows(dqg_ref, dqg, first)
        _acc_rows(dkvg_ref, dkvg, first)
        dz_ref[:, 0:256] = dq.astype(dz_ref.dtype)
        dz_ref[:, 256:384] = dkv.astype(dz_ref.dtype)
        dz_ref[:, 384:512] = dkpe_ref[...].astype(dz_ref.dtype)
        dz_ref[:, 512:1536] = duv_ref[...].astype(dz_ref.dtype)

    return pl.pallas_call(
        body, name="mla_prep_bwd", grid=(S // tr,),
        out_shape=(jax.ShapeDtypeStruct((S, 1536), _MXU_DTYPE), jax.ShapeDtypeStruct((1, 256), F32), jax.ShapeDtypeStruct((1, 128), F32)),
        in_specs=[pl.BlockSpec((tr, 256), lambda i: (i, 0)), pl.BlockSpec((tr, 128), lambda i: (i, 2)), _vec_spec(256), _vec_spec(128),
                  _row_spec(tr, 256), _row_spec(tr, 128), _row_spec(tr, 128), _row_spec(tr, 1024)],
        out_specs=(_row_spec(tr, 1536), _vec_spec(256), _vec_spec(128)),
        compiler_params=_cp("arbitrary"))(z, z, qg, kvg, dqn, dkvn, dkpe, duv)


def _rope(v, cos, sa, sb):
    return v * cos + pltpu.roll(v, 112, axis=1) * sa + pltpu.roll(v, 16, axis=1) * sb


def _rope_t(d, cos, sa, sb):
    return d * cos + pltpu.roll(d * sa, 16, axis=1) + pltpu.roll(d * sb, 112, axis=1)


def _rope_fwd(qraw, kvall, z, cosq, cosk, sa, sb, tr=256):
    S = qraw.shape[0]

    def body(q_ref, k_ref, v_ref, kpe_ref, cq_ref, ck_ref, sa_ref, sb_ref, qo_ref, ko_ref, vo_ref):
        cq, ck, sa_v, sb_v = cq_ref[...], ck_ref[...], sa_ref[...], sb_ref[...]
        kpe = _rope(kpe_ref[...], ck, sa_v, sb_v)
        for h in range(8):
            cols = slice(128 * h, 128 * h + 128)
            qo_ref[:, cols] = _rope(q_ref[:, cols], cq, sa_v, sb_v).astype(qo_ref.dtype)
            ko_ref[:, cols] = (k_ref[:, cols] + kpe).astype(ko_ref.dtype)
        vo_ref[...] = v_ref[...].astype(vo_ref.dtype)

    tab = _row_spec(tr, 128)
    return pl.pallas_call(
        body, name="rope_fwd", grid=(S // tr,),
        out_shape=(jax.ShapeDtypeStruct((S, 1024), _MXU_DTYPE), jax.ShapeDtypeStruct((S, 1024), _MXU_DTYPE),
                   jax.ShapeDtypeStruct((S, 512), _MXU_DTYPE)),
        in_specs=[_row_spec(tr, 1024), pl.BlockSpec((tr, 1024), lambda i: (i, 0)), pl.BlockSpec((tr, 512), lambda i: (i, 2)),
                  pl.BlockSpec((tr, 128), lambda i: (i, 3)), tab, tab, tab, tab],
        out_specs=(_row_spec(tr, 1024), _row_spec(tr, 1024), _row_spec(tr, 512)),
        compiler_params=_cp("parallel"))(qraw, kvall, kvall, z, cosq, cosk, sa, sb)


def _rope_bwd(dq, dk, dv, cosq, cosk, sa, sb, tr=256):
    S = dq.shape[0]

    def body(dq_ref, dk_ref, dv_ref, cq_ref, ck_ref, sa_ref, sb_ref, dqo_ref, dkv_ref, dkpe_ref):
        cq, ck, sa_v, sb_v = cq_ref[...], ck_ref[...], sa_ref[...], sb_ref[...]
        tot = jnp.zeros((tr, 128), F32)
        for h in range(8):
            cols = slice(128 * h, 128 * h + 128)
            dqo_ref[:, cols] = _rope_t(dq_ref[:, cols], cq, sa_v, sb_v).astype(dqo_ref.dtype)
            dkh = dk_ref[:, cols]
            tot = tot + dkh
            dkv_ref[:, cols] = dkh.astype(dkv_ref.dtype)
        dkv_ref[:, 1024:1536] = dv_ref[...].astype(dkv_ref.dtype)
        dkpe_ref[...] = _rope_t(tot, ck, sa_v, sb_v)

    tab = _row_spec(tr, 128)
    return pl.pallas_call(
        body, name="rope_bwd", grid=(S // tr,),
        out_shape=(jax.ShapeDtypeStruct((S, 1024), _MXU_DTYPE), jax.ShapeDtypeStruct((S, 1536), _MXU_DTYPE),
                   jax.ShapeDtypeStruct((S, 128), F32)),
        in_specs=[_row_spec(tr, 1024), _row_spec(tr, 1024), _row_spec(tr, 512), tab, tab, tab, tab],
        out_specs=(_row_spec(tr, 1024), _row_spec(tr, 1536), _row_spec(tr, 128)),
        compiler_params=_cp("parallel"))(dq, dk, dv, cosq, cosk, sa, sb)


NEG = -1e30


def _attn_fwd(q, k, v, tq=256, tk=256):
    S = q.shape[0]

    def body(q_ref, k_ref, v_ref, o_ref, lse_ref):
        i = pl.program_id(1)
        row = i * tq + lax.broadcasted_iota(jnp.int32, (tq, tk), 0)
        qs = [q_ref[:, 0:128], q_ref[:, 128:256]]

        def step(kb, carry):
            start = pl.multiple_of(kb * tk, tk)
            col = start + lax.broadcasted_iota(jnp.int32, (tq, tk), 1)
            vv = v_ref[pl.ds(start, tk), :]
            out = []
            for h in range(2):
                m, l, acc = carry[3 * h:3 * h + 3]
                s = _dot(qs[h], k_ref[pl.ds(start, tk), 128 * h:128 * h + 128], "nt") * ATTN_SCALE
                s = jnp.where(col <= row, s, NEG)
                m_new = jnp.maximum(m, jnp.max(s, axis=-1, keepdims=True))
                alpha = jnp.exp(m - m_new)
                p = jnp.exp(s - m_new)
                out += [m_new, alpha * l + jnp.sum(p, axis=-1, keepdims=True), alpha * acc + _dot(p, vv, "nn")]
            return tuple(out)

        init = (jnp.full((tq, 1), NEG, F32), jnp.zeros((tq, 1), F32), jnp.zeros((tq, 128), F32)) * 2
        ma, la, acca, mb, lb, accb = lax.fori_loop(0, (i * tq + tq) // tk, step, init)
        lane = lax.broadcasted_iota(jnp.int32, (tq, 128), 1)
        o_ref[...] = jnp.where(lane < 64, acca / la, accb / lb)
        lse_ref[...] = jnp.where(lane < 64, ma + jnp.log(la), mb + jnp.log(lb))

    return pl.pallas_call(
        body, name="attn_fwd", grid=(4, S // tq),
        out_shape=(jax.ShapeDtypeStruct((S, 512), F32), jax.ShapeDtypeStruct((4, S, 128), F32)),
        in_specs=[pl.BlockSpec((tq, 256), lambda p, i: (i, p)), pl.BlockSpec((S, 256), lambda p, i: (0, p)),
                  pl.BlockSpec((S, 128), lambda p, i: (0, p))],
        out_specs=(pl.BlockSpec((tq, 128), lambda p, i: (i, p)), pl.BlockSpec((None, tq, 128), lambda p, i: (p, i, 0))),
        compiler_params=_cp("parallel", "parallel"))(q, k, v)


def _attn_bwd(q, k, v, o, lse, dycat2, tq=256, tk=256):
    S = q.shape[0]

    def body(q_ref, k_ref, v_ref, o_ref, lse_ref, do_ref, dq_ref, dk_ref, dv_ref):
        j = pl.program_id(1)

        @pl.when(j == 0)
        def _():
            dq_ref[...] = jnp.zeros_like(dq_ref)

        col = j * tk + lax.broadcasted_iota(jnp.int32, (tq, tk), 1)
        lane = lax.broadcasted_iota(jnp.int32, (tq, 128), 1)
        ks = [k_ref[:, 0:128], k_ref[:, 128:256]]
        vv = v_ref[...]

        def step(qb, carry):
            dka, dkb, dvp = carry
            start = pl.multiple_of(qb * tq, tq)
            rows = pl.ds(start, tq)
            row = start + lax.broadcasted_iota(jnp.int32, (tq, tk), 0)
            do, lse_v = do_ref[rows, :], lse_ref[rows, :]
            prod = do * o_ref[rows, :]
            dks = [dka, dkb]
            for h in range(2):
                mine = (lane < 64) if h == 0 else (lane >= 64)
                delta = jnp.sum(jnp.where(mine, prod, 0.0), axis=-1, keepdims=True)
                do_h = jnp.where(mine, do, 0.0)
                qh = q_ref[rows, 128 * h:128 * h + 128]
                s = _dot(qh, ks[h], "nt") * ATTN_SCALE
                p = jnp.where(col <= row, jnp.exp(s - lse_v[:, 64 * h:64 * h + 1]), 0.0)
                dvp = dvp + _dot(p, do_h, "tn")
                ds = p * (_dot(do_h, vv, "nt") - delta) * ATTN_SCALE
                dq_ref[rows, 128 * h:128 * h + 128] += _dot(ds, ks[h], "nn")
                dks[h] = dks[h] + _dot(ds, qh, "tn")
            return dks[0], dks[1], dvp

        zero = jnp.zeros((tk, 128), F32)
        dka, dkb, dvp = lax.fori_loop((j * tk) // tq, S // tq, step, (zero, zero, zero))
        dk_ref[:, 0:128] = dka
        dk_ref[:, 128:256] = dkb
        dv_ref[...] = dvp

    return pl.pallas_call(
        body, name="attn_bwd", grid=(4, S // tk),
        out_shape=(jax.ShapeDtypeStruct((S, 1024), F32), jax.ShapeDtypeStruct((S, 1024), F32), jax.ShapeDtypeStruct((S, 512), F32)),
        in_specs=[pl.BlockSpec((S, 256), lambda p, j: (0, p)), pl.BlockSpec((tk, 256), lambda p, j: (j, p)),
                  pl.BlockSpec((tk, 128), lambda p, j: (j, p)), pl.BlockSpec((S, 128), lambda p, j: (0, p)),
                  pl.BlockSpec((None, S, 128), lambda p, j: (p, 0, 0)), pl.BlockSpec((None, S, 128), lambda p, j: (0, 0, p))],
        out_specs=(pl.BlockSpec((S, 256), lambda p, j: (0, p)), pl.BlockSpec((tk, 256), lambda p, j: (j, p)),
                   pl.BlockSpec((tk, 128), lambda p, j: (j, p))),
        compiler_params=_cp("parallel", "arbitrary"))(q, k, v, o, lse, dycat2)


CHUNK = 128
GELU_C = math.sqrt(2.0 / math.pi)


def _gelu(v):
    t = jnp.tanh(GELU_C * (v + 0.044715 * (v * v * v)))
    return v * (0.5 * (1.0 + t)), t


def _gelu_grad(v, t):
    return 0.5 * (1.0 + t) + v * (0.5 * (1.0 - t * t) * GELU_C * (1.0 + 3.0 * 0.044715 * v * v))


def _tril(w):
    r = lax.broadcasted_iota(jnp.int32, w.shape, 0)
    c = lax.broadcasted_iota(jnp.int32, w.shape, 1)
    return jnp.where(c <= r, w, 0.0)


def _layer_norm(v, g, b):
    xc = v - jnp.mean(v, axis=-1, keepdims=True)
    rstd = lax.rsqrt(jnp.mean(xc * xc, axis=-1, keepdims=True) + EPS)
    xhat = xc * rstd
    return xhat * g + b, xhat, rstd


def _sgu_fwd(z, o, ln_g, ln_b, w_s, b_st, tr=256):
    S = z.shape[0]

    def body(u_ref, v_ref, o_ref, g_ref, b_ref, ws_ref, bs_ref, y_ref):
        gu, _ = _gelu(u_ref[...])
        gv, _ = _gelu(v_ref[...])
        vln, _, _ = _layer_norm(gv, g_ref[...], b_ref[...])
        y_ref[0] = o_ref[...].astype(y_ref.dtype)
        for g in range(4):
            wt = _tril(ws_ref[g])
            cols = slice(128 * g, 128 * g + 128)
            for ch in range(tr // CHUNK):
                rows = slice(CHUNK * ch, CHUNK * ch + CHUNK)
                mixed = _dot(wt, vln[rows, cols], "nn") + bs_ref[:, g:g + 1]
                y_ref[1, rows, cols] = (gu[rows, cols] * mixed).astype(y_ref.dtype)

    return pl.pallas_call(
        body, name="sgu_fwd", grid=(S // tr,), out_shape=jax.ShapeDtypeStruct((2, S, 512), _MXU_DTYPE),
        in_specs=[pl.BlockSpec((tr, 512), lambda i: (i, 1)), pl.BlockSpec((tr, 512), lambda i: (i, 2)), _row_spec(tr, 512),
                  _vec_spec(512), _vec_spec(512), pl.BlockSpec((4, 128, 128), lambda i: (0, 0, 0)), pl.BlockSpec((128, 4), lambda i: (0, 0))],
        out_specs=pl.BlockSpec((2, tr, 512), lambda i: (0, i, 0)), compiler_params=_cp("parallel"))(z, z, o, ln_g, ln_b, w_s, b_st)


def _sgu_bwd(z, dycat2, ln_g, ln_b, w_s, b_st, tr=256):
    S = z.shape[0]

    def body(u_ref, v_ref, dy_ref, g_ref, b_ref, ws_ref, bs_ref, duv_ref, dg_ref, db_ref, dws_ref, dbs_ref):
        first = pl.program_id(0) == 0
        u_pre, v_pre = u_ref[...], v_ref[...]
        gu, tu = _gelu(u_pre)
        gv, tv = _gelu(v_pre)
        gain = g_ref[...]
        vln, xhat, rstd = _layer_norm(gv, gain, b_ref[...])

        @pl.when(first)
        def _():
            dws_ref[...] = jnp.zeros_like(dws_ref)
            dbs_ref[...] = jnp.zeros_like(dbs_ref)

        dvln_cols = []
        for g in range(4):
            wt = _tril(ws_ref[g])
            cols = slice(128 * g, 128 * g + 128)
            dmixed_sum = jnp.zeros((CHUNK, 128), F32)
            dw = jnp.zeros((CHUNK, CHUNK), F32)
            dvln_rows = []
            for ch in range(tr // CHUNK):
                rows = slice(CHUNK * ch, CHUNK * ch + CHUNK)
                vt = vln[rows, cols]
                mixed = _dot(wt, vt, "nn") + bs_ref[:, g:g + 1]
                dyd = dy_ref[rows, cols]
                duv_ref[rows, cols] = (dyd * mixed * _gelu_grad(u_pre[rows, cols], tu[rows, cols])).astype(duv_ref.dtype)
                dmixed = dyd * gu[rows, cols]
                dmixed_sum = dmixed_sum + dmixed
                dw = dw + _dot(dmixed, vt, "nt")
                dvln_rows.append(_dot(wt, dmixed, "tn"))
            dws_ref[g] += _tril(dw)
            dbs_ref[g:g + 1, :] += jnp.sum(dmixed_sum.T, axis=0, keepdims=True)
            dvln_cols.append(jnp.concatenate(dvln_rows, axis=0))
        dvln = jnp.concatenate(dvln_cols, axis=1)
        _acc_rows(dg_ref, dvln * xhat, first)
        _acc_rows(db_ref, dvln, first)
        dxhat = dvln * gain
        dgv = rstd * (dxhat - jnp.mean(dxhat, axis=-1, keepdims=True) - xhat * jnp.mean(dxhat * xhat, axis=-1, keepdims=True))
        duv_ref[:, 512:1024] = (dgv * _gelu_grad(v_pre, tv)).astype(duv_ref.dtype)

    return pl.pallas_call(
        body, name="sgu_bwd", grid=(S // tr,),
        out_shape=(jax.ShapeDtypeStruct((S, 1024), _MXU_DTYPE), jax.ShapeDtypeStruct((1, 512), F32), jax.ShapeDtypeStruct((1, 512), F32),
                   jax.ShapeDtypeStruct((4, 128, 128), F32), jax.ShapeDtypeStruct((4, 128), F32)),
        in_specs=[pl.BlockSpec((tr, 512), lambda i: (i, 1)), pl.BlockSpec((tr, 512), lambda i: (i, 2)),
                  pl.BlockSpec((None, tr, 512), lambda i: (1, i, 0)), _vec_spec(512), _vec_spec(512),
                  pl.BlockSpec((4, 128, 128), lambda i: (0, 0, 0)), pl.BlockSpec((128, 4), lambda i: (0, 0))],
        out_specs=(_row_spec(tr, 1024), _vec_spec(512), _vec_spec(512), pl.BlockSpec((4, 128, 128), lambda i: (0, 0, 0)),
                   pl.BlockSpec((4, 128), lambda i: (0, 0))),
        compiler_params=_cp("arbitrary"))(z, z, dycat2, ln_g, ln_b, w_s, b_st)


def _sum_parts(name, parts, tr=512):
    P, R, C = parts.shape
    tr = _tile(R, tr) if R % 8 == 0 else R

    def body(p_ref, o_ref):
        g = p_ref[0]
        for k in range(1, P):
            g = g + p_ref[k]
        o_ref[...] = g

    return pl.pallas_call(
        body, name=name, grid=(R // tr,), out_shape=jax.ShapeDtypeStruct((R, C), F32),
        in_specs=[pl.BlockSpec((P, tr, C), lambda i: (0, i, 0))], out_specs=_row_spec(tr, C),
        compiler_params=_cp("parallel"))(parts)


ADAMW_BLOCK_BYTES = 36 * 2 ** 20


def _adamw(name, w, m, v, parts):
    L, R, C = w.shape
    P = parts[0].shape[0]
    row_bytes = 2 * C * (7 * 4 + P * parts[0].dtype.itemsize)
    tr = R
    if R * row_bytes > ADAMW_BLOCK_BYTES:
        tr = next(t for t in (1024, 512, 256, 128, 64, 32, 16) if R % t == 0 and t * row_bytes <= ADAMW_BLOCK_BYTES)
    nr = R // tr
    c1 = 1.0 / (1.0 - ADAM_B1 ** ADAM_STEP)
    c2 = 1.0 / (1.0 - ADAM_B2 ** ADAM_STEP)

    def body(w_ref, m_ref, v_ref, *rest):
        p_refs, (g_ref, d_ref, mo_ref, vo_ref) = rest[:L], rest[L:]
        for ll in range(L):
            @pl.when(pl.program_id(0) == ll)
            def _(p_ref=p_refs[ll]):
                g = p_ref[0].astype(F32)
                for k in range(1, P):
                    g = g + p_ref[k].astype(F32)
                m2 = ADAM_B1 * m_ref[...] + (1.0 - ADAM_B1) * g
                v2 = ADAM_B2 * v_ref[...] + (1.0 - ADAM_B2) * (g * g)
                g_ref[...] = g
                mo_ref[...] = m2
                vo_ref[...] = v2
                d_ref[...] = -ADAM_LR * ((m2 * c1) / (jnp.sqrt(v2 * c2) + ADAM_EPS) + ADAM_WD * w_ref[...])

    def part_spec(ll):
        return pl.BlockSpec((P, tr, C), lambda l, i: (0, jnp.where(l == ll, i, jnp.where(l < ll, 0, nr - 1)), 0))

    full = pl.BlockSpec((None, tr, C), lambda l, i: (l, i, 0))
    sds = jax.ShapeDtypeStruct((L, R, C), F32)
    return pl.pallas_call(
        body, name=name, grid=(L, nr), out_shape=(sds, sds, sds, sds),
        in_specs=[full] * 3 + [part_spec(ll) for ll in range(L)],
        out_specs=(full,) * 4, compiler_params=_cp("arbitrary", "arbitrary"))(w, m, v, *parts)


def _rope_tables(positions):
    half = 16
    inv_freq = 10000.0 ** (-jnp.arange(half, dtype=F32) / half)
    ang = positions.astype(F32)[:, None] * inv_freq
    cos, sin = jnp.cos(ang), jnp.sin(ang)
    S = positions.shape[0]
    z16, z32, z64 = jnp.zeros((S, 16), F32), jnp.zeros((S, 32), F32), jnp.zeros((S, 64), F32)
    cosk = jnp.concatenate([z64, cos, cos, z32], axis=1)
    cosq = jnp.concatenate([jnp.ones((S, 64), F32), cos, cos, z32], axis=1)
    sa = jnp.concatenate([z64, -sin, z16, z32], axis=1)
    sb = jnp.concatenate([z64, z16, sin, z32], axis=1)
    return cosq, cosk, sa, sb


def _ffn_fwd(l, x, mod, n2g, w_up8, cw24, w_down4):
    sh, sc, gate = mod
    h = _rmsmod_fwd(f"ffn{l}_norm", x, n2g, sc, sh, n2g)
    u8 = _mm_cols(f"ffn{l}_up", h, w_up8)
    S, n = u8.shape[1], u8.shape[2]
    u24 = u8.reshape(2, 4, S, n)
    a4 = _ffn_gate_fwd(f"ffn{l}_gate", u24, cw24)
    f, x_new = _mm_rows_resid(f"ffn{l}_down", a4, w_down4, x, gate)
    return x_new, (x, h, u24, a4, f)


def _ffn_bwd(l, dx, saved, mod, n2g, w_up8, cw24, w_down4, me):
    sh, sc, gate = mod
    x, h, u24, a4, f = saved
    df, dgate = _gate_bwd(f"ffn{l}_gate_bwd", dx, f, gate)
    da4 = _mm_rows_dx(f"ffn{l}_down_dx", df, w_down4)
    dw_down4 = _mm_rows_dw(f"ffn{l}_down_dw", a4, df, out_dtype=WIRE_DTYPE)
    du24, dcw24 = _ffn_gate_bwd(f"ffn{l}_act_bwd", u24, cw24, da4)
    du8 = du24.reshape((8,) + du24.shape[2:])
    dw_up8 = _mm_cols_dw(f"ffn{l}_up_dw", h, du8, out_dtype=WIRE_DTYPE)
    sent, token = _exchange_start(f"scatter_ffn{l}", [dw_up8, dw_down4.reshape(8, 352, dw_down4.shape[2])], True, dw_up8, me)
    dh = _mm_cols_dx(f"ffn{l}_up_dx", du8, w_up8)
    dx_new, dn2g, dsc, dsh = _rmsmod_bwd(f"ffn{l}_norm_bwd", x, n2g, sc, dh, dx, token)
    return dx_new, dict(sent=sent, cw24=dcw24, n2g=dn2g, mod=(dsh, dsc, dgate))


def kernel(x, c, positions, ada_w, ada_b, norm1_g, norm2_g, ab_w_in, a_conv_w, b_mix_w, b_scale, ab_w_out, cd_w_in, c_q_norm_g, c_w_uq, c_kv_norm_g, c_w_ukv, d_ln_g, d_ln_b, d_w_s, d_b_s, cd_w_out, ffn_w_up, ffn_conv_w, ffn_w_down, final_norm_g, loss_target, m_ada_w, m_ada_b, m_norm1_g, m_norm2_g, m_ab_w_in, m_a_conv_w, m_b_mix_w, m_b_scale, m_ab_w_out, m_cd_w_in, m_c_q_norm_g, m_c_w_uq, m_c_kv_norm_g, m_c_w_ukv, m_d_ln_g, m_d_ln_b, m_d_w_s, m_d_b_s, m_cd_w_out, m_ffn_w_up, m_ffn_conv_w, m_ffn_w_down, m_final_norm_g, v_ada_w, v_ada_b, v_norm1_g, v_norm2_g, v_ab_w_in, v_a_conv_w, v_b_mix_w, v_b_scale, v_ab_w_out, v_cd_w_in, v_c_q_norm_g, v_c_w_uq, v_c_kv_norm_g, v_c_w_ukv, v_d_ln_g, v_d_ln_b, v_d_w_s, v_d_b_s, v_cd_w_out, v_ffn_w_up, v_ffn_conv_w, v_ffn_w_down, v_final_norm_g):
    S, D = x.shape[1], x.shape[2]
    me = 4 * lax.axis_index("x") + 2 * lax.axis_index("y") + lax.axis_index("c")
    x0, target = x[0], loss_target[0]
    W = _MXU_DTYPE

    small_shapes = [(1024,), (3, 64), (32,), (64,), (64,), (2, 3, 704)]
    (g0,) = _exchange("gather_small", [[_pack([c, a_conv_w, c_q_norm_g, d_ln_g, d_ln_b, ffn_conv_w])]], scatter=False)
    c_all, aconv_s, qg_s, lng_s, lnb_s, fcw_s = _unpack(g0[:, 0], small_shapes, lead=(N_DEV,))
    conv_w = aconv_s.transpose(1, 0, 2).reshape(3, 512)
    qg, ln_g, ln_b = qg_s.reshape(1, 256), lng_s.reshape(1, 512), lnb_s.reshape(1, 512)
    cw24 = [fcw_s[:, l].reshape(2, 4, 3, 704) for l in range(2)]
    c16 = jnp.pad(c_all, ((0, 16 - N_DEV), (0, 0)))

    mod_cols = _ada_fwd(c16, ada_w)
    (g1,) = _exchange("gather_mod", [[_pack([mod_cols])]], scatter=False)
    mod_all = _unpack(g1[:, 0], [(2, 16, 768)], lead=(N_DEV,))[0]
    mod_mine = lax.dynamic_index_in_dim(mod_all, me, axis=2, keepdims=False)
    mod = mod_mine.transpose(1, 0, 2).reshape(2, 6 * D) + ada_b
    mods = [[mod[l, k * D:(k + 1) * D].reshape(1, D) for k in range(6)] for l in range(2)]

    gw_ab, token = _exchange_start("gather_w_ab", [ab_w_in[0].astype(W), ab_w_out[0].astype(W)], False, mod, me)
    gw_f0, token = _exchange_start("gather_w_ffn0", [ffn_w_up[0].astype(W), ffn_w_down[0].astype(W)], False, token, me)
    gw_cd, token = _exchange_start("gather_w_cd", [
        cd_w_in[0].astype(W).reshape(1440, 128), c_w_uq[0].astype(W).reshape(192, 128), c_w_ukv[0].astype(W),
        cd_w_out[0].astype(W)], False, token, me)
    gw_f1, started = _exchange_start("gather_w_ffn1", [ffn_w_up[1].astype(W), ffn_w_down[1].astype(W)], False, token, me)

    cosq, cosk, sa, sb = _rope_tables(positions[0])
    n1g = [norm1_g[l].reshape(1, D) for l in range(2)]
    n2g = [norm2_g[l].reshape(1, D) for l in range(2)]
    mix_w, scale = b_mix_w[0], b_scale
    kvg = c_kv_norm_g
    w_s, b_st = d_w_s[0], d_b_s[0].T

    sh1, sc1, g1m = mods[0][:3]
    h_ab = _rmsmod_fwd("ab_norm", x0, n1g[0], sc1, sh1, started)
    w_abin8, w_about = _exchange_wait("wait_w_ab", gw_ab, h_ab)
    w_about2 = w_about.reshape(2, 512, D)
    z8 = _mm_cols("ab_in", h_ab, w_abin8)
    ycat_ab = _ab_mix_fwd(z8, conv_w, mix_w, scale)
    y_ab, x1 = _mm_rows_resid("ab_out", ycat_ab, w_about2, x0, g1m)
    w_up0, w_dn0 = _exchange_wait("wait_w_ffn0", gw_f0, x1)
    w_up8, w_down4 = [w_up0, None], [w_dn0.reshape(4, 704, D), None]
    x2, ffn0_saved = _ffn_fwd(0, x1, mods[0][3:], n2g[0], w_up8[0], cw24[0], w_down4[0])

    w_cdin, w_uq, w_ukv, w_cdout = _exchange_wait("wait_w_cd", gw_cd, x2)
    w_cdout2 = w_cdout.reshape(2, 512, D)
    w_cd = w_cdin.reshape(8, D, 180).transpose(1, 0, 2).reshape(D, 1440)
    zc = lambda n: jnp.zeros((D, n), W)
    w_cd_pad = jnp.concatenate([w_cd[:, :384], zc(64), w_cd[:, 384:416], zc(32), w_cd[:, 416:]], axis=1)
    w_uq_pad = jnp.pad(w_uq.reshape(8, 256, 96).transpose(1, 0, 2), ((0, 0), (0, 0), (0, 32))).reshape(256, 1024)
    w_ukv_h = w_ukv.transpose(1, 0, 2)
    w_k_pad = jnp.pad(w_ukv_h[:, :, :64], ((0, 0), (0, 0), (0, 64))).reshape(128, 1024)
    w_kv_pad = jnp.concatenate([w_k_pad, w_ukv_h[:, :, 64:].reshape(128, 512)], axis=1)

    sh1, sc1, g1c = mods[1][:3]
    h_cd = _rmsmod_fwd("cd_norm", x2, n1g[1], sc1, sh1, n1g[1])
    z_cd = _mm_nn("cd_in", h_cd, w_cd_pad)
    qn, kvn = _mla_prep_fwd(z_cd, qg, kvg)
    qraw = _mm_nn("cd_uq", qn, w_uq_pad)
    kvall = _mm_nn("cd_ukv", kvn, w_kv_pad)
    q_r, k_r, v_r = _rope_fwd(qraw, kvall, z_cd, cosq, cosk, sa, sb)
    o, lse = _attn_fwd(q_r, k_r, v_r)
    ycat_cd = _sgu_fwd(z_cd, o, ln_g, ln_b, w_s, b_st)
    y_cd, x3 = _mm_rows_resid("cd_out", ycat_cd, w_cdout2, x2, g1c)
    w_up1, w_dn1 = _exchange_wait("wait_w_ffn1", gw_f1, x3)
    w_up8[1], w_down4[1] = w_up1, w_dn1.reshape(4, 704, D)
    x4, ffn1_saved = _ffn_fwd(1, x3, mods[1][3:], n2g[1], w_up8[1], cw24[1], w_down4[1])

    loss_local, dx4, dfg = _loss_head(x4, final_norm_g.reshape(1, D), target)

    dx3, gf1 = _ffn_bwd(1, dx4, ffn1_saved, mods[1][3:], n2g[1], w_up8[1], cw24[1], w_down4[1], me)

    dy, dg1c = _gate_bwd("cd_gate_bwd", dx3, y_cd, g1c)
    dycat = _mm_rows_dx("cd_out_dx", dy, w_cdout2)
    dw_cdout = _mm_rows_dw("cd_out_dw", ycat_cd, dy, out_dtype=WIRE_DTYPE)
    duv, dln_g, dln_b, dws, dbs = _sgu_bwd(z_cd, dycat, ln_g, ln_b, w_s, b_st)
    dq_r, dk_r, dv_r = _attn_bwd(q_r, k_r, v_r, o, lse, dycat)
    dqraw, dkvall, dkpe = _rope_bwd(dq_r, dk_r, dv_r, cosq, cosk, sa, sb)
    dqn = _mm_nt("cd_uq_dx", dqraw, w_uq_pad, tn=256)
    dkvn = _mm_nt("cd_ukv_dx", dkvall, w_kv_pad, tn=128)
    dw_uq_pad = _mm_tn("cd_uq_dw", qn, dqraw, tm=256)
    dw_kv_pad = _mm_tn("cd_ukv_dw", kvn, dkvall, tm=128)
    dz_cd, dqg, dkvg = _mla_prep_bwd(z_cd, qg, kvg, dqn, dkvn, dkpe, duv)
    dh_cd = _mm_nt("cd_in_dx", dz_cd, w_cd_pad)
    dw_cd_pad = _mm_tn("cd_in_dw", h_cd, dz_cd)
    dw_cd =jnp.concatenate([dw_cd_pad[:, :384], dw_cd_pad[:, 448:480], dw_cd_pad[:, 512:]], axis=1)
    dw_cd8 = dw_cd.reshape(D, 8, 180).transpose(1, 0, 2).reshape(8, 1440, 128).astype(WIRE_DTYPE)
    dw_uq8 = dw_uq_pad.reshape(256, 8, 128)[:, :, :96].transpose(1, 0, 2).reshape(8, 192, 128).astype(WIRE_DTYPE)
    dw_ukv8 = jnp.concatenate([dw_kv_pad[:, :1024].reshape(128, 8, 128)[:, :, :64], dw_kv_pad[:, 1024:].reshape(128, 8, 64)],
                              axis=2).transpose(1, 0, 2).astype(WIRE_DTYPE)
    sent_cd, token = _exchange_start("scatter_cd", [dw_cd8, dw_uq8, dw_ukv8, dw_cdout.reshape(8, 128, D)], True, dw_cd8, me)
    dx2, dn1g_cd, dsc1_cd, dsh1_cd = _rmsmod_bwd("cd_norm_bwd", x2, n1g[1], sc1, dh_cd, dx3, token)

    dx1, gf0 = _ffn_bwd(0, dx2, ffn0_saved, mods[0][3:], n2g[0], w_up8[0], cw24[0], w_down4[0], me)

    dy, dg1m = _gate_bwd("ab_gate_bwd", dx1, y_ab, g1m)
    dycat = _mm_rows_dx("ab_out_dx", dy, w_about2)
    dw_about = _mm_rows_dw("ab_out_dw", ycat_ab, dy, out_dtype=WIRE_DTYPE)
    dz8, dconv_w, dmix_w, dscale = _ab_mix_bwd(z8, dycat, conv_w, mix_w, scale)
    dz8 = dz8.reshape(8, S, 256)
    dw_abin8 = _mm_cols_dw("ab_in_dw", h_ab, dz8, out_dtype=WIRE_DTYPE)
    sent_ab, token = _exchange_start("scatter_ab", [dw_abin8, dw_about.reshape(8, 128, D)], True, dw_abin8, me)
    dh_ab = _mm_cols_dx("ab_in_dx", dz8, w_abin8)
    dx0, dn1g_ab, dsc1_ab, dsh1_ab = _rmsmod_bwd("ab_norm_bwd", x0, n1g[0], mods[0][1], dh_ab, dx1, token)

    dmod = jnp.stack([jnp.concatenate([dsh1_ab, dsc1_ab, dg1m, *gf0["mod"]], axis=1)[0],
                      jnp.concatenate([dsh1_cd, dsc1_cd, dg1c, *gf1["mod"]], axis=1)[0]])
    rep_grads = [dmod, jnp.concatenate([dn1g_ab, dn1g_cd]), jnp.concatenate([gf0["n2g"], gf1["n2g"]]),
                 dmix_w, dscale, dkvg, dws, dbs, dfg]
    rep_names = ["ada_b", "norm1_g", "norm2_g", "b_mix_w", "b_scale", "c_kv_norm_g", "d_w_s", "d_b_s", "final_norm_g"]
    dfcw = jnp.stack([gf0["cw24"].reshape(8, 3, 704), gf1["cw24"].reshape(8, 3, 704)])
    shard_grads = [dconv_w, dqg, dln_g, dln_b, dfcw]
    rep_buf, shard_buf = _pack(rep_grads), _pack(shard_grads)
    r_rep = rep_buf.shape[0]
    (g2,) = _exchange("gather_small_grads", [[jnp.concatenate([rep_buf, shard_buf])]], scatter=False)
    g2 = g2[:, 0]

    dmod_all = g2[:, :2 * 6 * D // 128].reshape(N_DEV, 2, N_DEV, 768)
    dmod_cols = lax.dynamic_index_in_dim(dmod_all, me, axis=2, keepdims=False).transpose(1, 0, 2)
    g_ada_w = _ada_bwd(c16, jnp.pad(dmod_cols, ((0, 0), (0, 16 - N_DEV), (0, 0))))

    p_up1, p_dn1 = _exchange_wait("wait_scatter_ffn1", gf1["sent"], dx0)
    p_cdin, p_uq, p_ukv, p_cdout = _exchange_wait("wait_scatter_cd", sent_cd, dx0)
    p_up0, p_dn0 = _exchange_wait("wait_scatter_ffn0", gf0["sent"], dx0)
    p_abin, p_about = _exchange_wait("wait_scatter_ab", sent_ab, dx0)

    res = {}

    def update(name, w, m, v, parts, shape3d):
        outs = _adamw("adamw_" + name, w.reshape(shape3d), m.reshape(shape3d), v.reshape(shape3d),
                      [p.reshape((p.shape[0],) + shape3d[1:]) for p in parts])
        res[name] = [o_.reshape(w.shape) for o_ in outs]

    update("ada_w", ada_w, m_ada_w, v_ada_w, [g_ada_w[l][None] for l in range(2)], (2, D, 768))
    update("ab_w_in", ab_w_in, m_ab_w_in, v_ab_w_in, [p_abin], (1, D, 256))
    update("ab_w_out", ab_w_out, m_ab_w_out, v_ab_w_out, [p_about], (1, 128, D))
    update("cd_w_in", cd_w_in, m_cd_w_in, v_cd_w_in, [p_cdin], (1, 1440, 128))
    update("c_w_uq", c_w_uq, m_c_w_uq, v_c_w_uq, [p_uq], (1, 192, 128))
    update("c_w_ukv", c_w_ukv, m_c_w_ukv, v_c_w_ukv, [p_ukv], (1, 128, 128))
    update("cd_w_out", cd_w_out, m_cd_w_out, v_cd_w_out, [p_cdout], (1, 128, D))
    update("ffn_w_up", ffn_w_up, m_ffn_w_up, v_ffn_w_up, [p_up0, p_up1], (2, D, 704))
    update("ffn_w_down", ffn_w_down, m_ffn_w_down, v_ffn_w_down, [p_dn0, p_dn1], (2, 352, D))

    rep_w = dict(ada_b=(ada_b, m_ada_b, v_ada_b), norm1_g=(norm1_g, m_norm1_g, v_norm1_g), norm2_g=(norm2_g, m_norm2_g, v_norm2_g),
                 b_mix_w=(b_mix_w, m_b_mix_w, v_b_mix_w), b_scale=(b_scale, m_b_scale, v_b_scale),
                 c_kv_norm_g=(c_kv_norm_g, m_c_kv_norm_g, v_c_kv_norm_g), d_w_s=(d_w_s, m_d_w_s, v_d_w_s),
                 d_b_s=(d_b_s, m_d_b_s, v_d_b_s), final_norm_g=(final_norm_g, m_final_norm_g, v_final_norm_g))
    rep_packed = [_pack([rep_w[n][k] for n in rep_names]) for k in range(3)]
    rep_out = _adamw("adamw_replicated", *[a[None] for a in rep_packed], [g2[:, :r_rep]])
    rep_shapes = [rep_w[n][0].shape for n in rep_names]
    for k, n in enumerate(rep_names):
        res[n] = [_unpack(o_, rep_shapes)[k] for o_ in rep_out]

    shard_sum = _sum_parts("sum_small_grads", g2[:, r_rep:])
    g_conv, g_qg, g_lng, g_lnb, g_fcw = _unpack(shard_sum, [(3, 512), (256,), (512,), (512,), (2, 8, 3, 704)])
    mine = lambda a, n, axis: lax.dynamic_slice_in_dim(a, me * n, n, axis=axis)
    sh_names = ["a_conv_w", "c_q_norm_g", "d_ln_g", "d_ln_b", "ffn_conv_w"]
    sh_grads = [mine(g_conv, 64, 1), mine(g_qg, 32, 0), mine(g_lng, 64, 0), mine(g_lnb, 64, 0),
                lax.dynamic_index_in_dim(g_fcw, me, axis=1, keepdims=False)]
    sh_w = dict(a_conv_w=(a_conv_w, m_a_conv_w, v_a_conv_w), c_q_norm_g=(c_q_norm_g, m_c_q_norm_g, v_c_q_norm_g),
                d_ln_g=(d_ln_g, m_d_ln_g, v_d_ln_g), d_ln_b=(d_ln_b, m_d_ln_b, v_d_ln_b),
                ffn_conv_w=(ffn_conv_w, m_ffn_conv_w, v_ffn_conv_w))
    sh_packed = [_pack([sh_w[n][k] for n in sh_names]) for k in range(3)]
    sh_out = _adamw("adamw_small_shards", *[a[None] for a in sh_packed], [_pack(sh_grads)[None]])
    sh_shapes = [sh_w[n][0].shape for n in sh_names]
    for k, n in enumerate(sh_names):
        res[n] = [_unpack(o_, sh_shapes)[k] for o_ in sh_out]

    loss = lax.psum(loss_local[0, 0], ("x", "y", "c"))
    order = ["ada_w", "ada_b", "norm1_g", "norm2_g", "ab_w_in", "a_conv_w", "b_mix_w", "b_scale", "ab_w_out", "cd_w_in", "c_q_norm_g",
             "c_w_uq", "c_kv_norm_g", "c_w_ukv", "d_ln_g", "d_ln_b", "d_w_s", "d_b_s", "cd_w_out", "ffn_w_up", "ffn_conv_w",
             "ffn_w_down", "final_norm_g"]
    return (loss, dx0[None], *[res[n][0] for n in order], *[res[n][1] for n in order], *[res[n][2] for n in order],
            *[res[n][3] for n in order])
```

```python
import functools
import math

import jax
import jax.numpy as jnp
from jax import lax
from jax.experimental import pallas as pl
from jax.experimental.pallas import tpu as pltpu

F32 = jnp.float32
BF16 = jnp.bfloat16
_MXU_DTYPE = BF16
WIRE_DTYPE = BF16
ACT_DTYPE = BF16
_VMEM_LIMIT = 56 * 2 ** 20
N_DEV = 8
EPS = 1e-6
POOL_WINDOWS = (2, 4, 8, 16)
ATTN_SCALE = (64 + 32) ** -0.5
ADAM_LR, ADAM_B1, ADAM_B2, ADAM_EPS, ADAM_WD, ADAM_STEP = 0.001, 0.9, 0.999, 1e-08, 0.01, 10
MESH = pl.DeviceIdType.MESH
ANY = pl.BlockSpec(memory_space=pl.ANY)


def _cp(*sem):
    return pltpu.CompilerParams(dimension_semantics=sem, vmem_limit_bytes=_VMEM_LIMIT)


def _dot(a, b, contract):
    dn = {"nn": (((1,), (0,)), ((), ())), "nt": (((1,), (1,)), ((), ())), "tn": (((0,), (0,)), ((), ()))}[contract]
    return lax.dot_general(a.astype(_MXU_DTYPE), b.astype(_MXU_DTYPE), dn, preferred_element_type=F32)


def _my_position():
    x, y, c = lax.axis_index("x"), lax.axis_index("y"), lax.axis_index("c")
    return x, y, c, 4 * x + 2 * y + c


def _exchange(name, groups, scatter):
    flat = [a for g in groups for a in g]
    n_in, n_grp = len(flat), len(groups)
    out_shapes = []
    for g in groups:
        slab = g[0].shape[1:] if scatter else g[0].shape
        out_shapes.append(jax.ShapeDtypeStruct((N_DEV, len(g)) + tuple(slab), g[0].dtype))

    def body(*refs):
        ins, outs = refs[:n_in], refs[n_in:n_in + n_grp]
        send_sems, recv_sems, local_sems = refs[n_in + n_grp:]
        x, y, c, me = _my_position()
        i = 0
        for gi, g in enumerate(groups):
            for l in range(len(g)):
                src = ins[i]
                i += 1
                pltpu.make_async_copy(src.at[me] if scatter else src, outs[gi].at[me, l], local_sems.at[gi]).start()
                for k in range(1, N_DEV):
                    px = 1 - x if k & 4 else x
                    py = 1 - y if k & 2 else y
                    pc = 1 - c if k & 1 else c
                    peer = 4 * px + 2 * py + pc
                    pltpu.make_async_remote_copy(
                        src_ref=src.at[peer] if scatter else src, dst_ref=outs[gi].at[me, l],
                        send_sem=send_sems.at[gi], recv_sem=recv_sems.at[gi],
                        device_id=(px, py, pc), device_id_type=MESH).start()
        for gi in range(n_grp):
            mine = outs[gi].at[me]
            pltpu.make_async_copy(mine, mine, local_sems.at[gi]).wait()
            seven = outs[gi].at[pl.ds(0, N_DEV - 1)]
            w = pltpu.make_async_remote_copy(src_ref=seven, dst_ref=seven, send_sem=send_sems.at[gi],
                                             recv_sem=recv_sems.at[gi], device_id=(x, y, c), device_id_type=MESH)
            w.wait_send()
            w.wait_recv()

    return pl.pallas_call(
        body, name=name, out_shape=tuple(out_shapes),
        in_specs=[ANY] * n_in, out_specs=tuple([ANY] * n_grp),
        scratch_shapes=[pltpu.SemaphoreType.DMA((n_grp,)), pltpu.SemaphoreType.DMA((n_grp,)),
                        pltpu.SemaphoreType.DMA((n_grp,))],
        compiler_params=pltpu.CompilerParams(has_side_effects=True),
    )(*flat)


HBM_SPEC = pl.BlockSpec(memory_space=pltpu.HBM)
SEM_SPEC = pl.BlockSpec(memory_space=pltpu.SEMAPHORE)
EFFECT = pltpu.SideEffectType.DATAFLOW_SIDE_EFFECTING


def _put_mine(name, srcs, scatter, me):
    n = len(srcs)
    slabs = [tuple(s.shape[1:] if scatter else s.shape) for s in srcs]

    def body(me_ref, *refs):
        for i in range(n):
            refs[n + i][...] = refs[i][...]

    def at_me(slab):
        return pl.BlockSpec((None,) + slab, lambda g, me_ref, nd=len(slab): (me_ref[0],) + (0,) * nd)

    def whole(slab):
        return pl.BlockSpec(slab, lambda g, me_ref, nd=len(slab): (0,) * nd)

    return pl.pallas_call(
        body, name=name,
        grid_spec=pltpu.PrefetchScalarGridSpec(
            num_scalar_prefetch=1, grid=(1,),
            in_specs=[at_me(slab) if scatter else whole(slab) for slab in slabs], out_specs=[at_me(slab) for slab in slabs]),
        out_shape=[jax.ShapeDtypeStruct((N_DEV,) + slab, s.dtype) for slab, s in zip(slabs, srcs)],
        compiler_params=_cp("arbitrary"))(me.reshape(1), *srcs)


def _exchange_start(name, srcs, scatter, after, me):
    n = len(srcs)
    lands = _put_mine(name + "_mine", srcs, scatter, me)
    srcs = [pltpu.with_memory_space_constraint(a, pltpu.HBM) for a in srcs]
    lands = [pltpu.with_memory_space_constraint(a, pltpu.HBM) for a in lands]

    def body(*refs):
        ins, land = refs[:n], refs[n:2 * n]
        send_sems, recv_sems, token = refs[2 * n + 1], refs[2 * n + 2], refs[-1]
        x, y, c, me_in = _my_position()
        for i in range(n):
            for k in range(1, N_DEV):
                px = 1 - x if k & 4 else x
                py = 1 - y if k & 2 else y
                pc = 1 - c if k & 1 else c
                pltpu.make_async_remote_copy(
                    src_ref=ins[i].at[4 * px + 2 * py + pc] if scatter else ins[i], dst_ref=land[i].at[me_in],
                    send_sem=send_sems.at[i], recv_sem=recv_sems.at[i],
                    device_id=(px, py, pc), device_id_type=MESH).start()
        token[...] = jnp.zeros_like(token)

    outs = pl.pallas_call(
        body, name=name,
        out_shape=(pltpu.SemaphoreType.DMA((n,)), pltpu.SemaphoreType.DMA((n,)),
                   *[pltpu.HBM(a.shape, a.dtype) for a in srcs], *[pltpu.HBM(a.shape, a.dtype) for a in lands],
                   jax.ShapeDtypeStruct((8, 128), F32)),
        in_specs=[HBM_SPEC] * (2 * n) + [ANY],
        out_specs=(SEM_SPEC, SEM_SPEC, *[HBM_SPEC] * (2 * n), pl.BlockSpec(memory_space=pltpu.VMEM)),
        input_output_aliases={i: 2 + i for i in range(2 * n)},
        compiler_params=pltpu.CompilerParams(has_side_effects=EFFECT),
    )(*srcs, *lands, after)
    return (outs[0], outs[1], outs[2:2 + n], outs[2 + n:2 + 2 * n]), outs[-1]


def _exchange_wait(name, handle, after):
    send_sems, recv_sems, srcs, lands = handle
    n = len(srcs)

    def body(*refs):
        land, send_ref, recv_ref = refs[n:2 * n], refs[2 * n], refs[2 * n + 1]
        x, y, c, _ = _my_position()
        for i in range(n):
            seven = land[i].at[pl.ds(0, N_DEV - 1)]
            w = pltpu.make_async_remote_copy(src_ref=seven, dst_ref=seven, send_sem=send_ref.at[i], recv_sem=recv_ref.at[i],
                                             device_id=(x, y, c), device_id_type=MESH)
            w.wait_send()
            w.wait_recv()

    outs = pl.pallas_call(
        body, name=name,
        out_shape=(*[pltpu.HBM(a.shape, a.dtype) for a in srcs], *[pltpu.HBM(a.shape, a.dtype) for a in lands]),
        in_specs=[HBM_SPEC] * (2 * n) + [SEM_SPEC, SEM_SPEC, ANY],
        out_specs=tuple([HBM_SPEC] * (2 * n)),
        input_output_aliases={i: i for i in range(2 * n)},
        compiler_params=pltpu.CompilerParams(has_side_effects=EFFECT),
    )(*srcs, *lands, send_sems, recv_sems, after)
    return outs[n:]


def _pack(arrs):
    flat = jnp.concatenate([a.reshape(-1).astype(F32) for a in arrs])
    n = flat.shape[0]
    rows = -(-n // 1024) * 8
    return jnp.pad(flat, (0, rows * 128 - n)).reshape(rows, 128)


def _unpack(buf, shapes, lead=()):
    flat = buf.reshape(lead + (-1,))
    out, off = [], 0
    for s in shapes:
        n = math.prod(s)
        out.append(flat[..., off:off + n].reshape(lead + tuple(s)))
        off += n
    return out


def _mm(name, a, a_spec, b, b_spec, out_sds, o_spec, grid, contract, nk=1):
    o_blk = tuple(d for d in o_spec.block_shape if d is not None)

    def body(a_ref, b_ref, o_ref, *acc):
        r = _dot(a_ref[...], b_ref[...], contract)
        if nk == 1:
            o_ref[...] = r.astype(o_ref.dtype)
        else:
            k = pl.program_id(len(grid) - 1)

            @pl.when(k == 0)
            def _():
                acc[0][...] = r

            @pl.when(k > 0)
            def _():
                acc[0][...] += r

            @pl.when(k == nk - 1)
            def _():
                o_ref[...] = acc[0][...].astype(o_ref.dtype)

    sem = ("parallel",) * (len(grid) - 1) + (("arbitrary",) if nk > 1 else ("parallel",))
    return pl.pallas_call(
        body, name=name, out_shape=out_sds, grid=grid, in_specs=[a_spec, b_spec], out_specs=o_spec,
        scratch_shapes=[pltpu.VMEM(o_blk, F32)] if nk > 1 else [], compiler_params=_cp(*sem))(a, b)


def _tile(n, want):
    t = min(n, want)
    assert n % t == 0, (n, t)
    return t


def _mm_nn(name, a, b, out_dtype=F32, tm=512, tn=512):
    (M, K), N = a.shape, b.shape[1]
    tm, tn = _tile(M, tm), _tile(N, tn)
    return _mm(name, a, pl.BlockSpec((tm, K), lambda i, j: (i, 0)), b, pl.BlockSpec((K, tn), lambda i, j: (0, j)),
               jax.ShapeDtypeStruct((M, N), out_dtype), pl.BlockSpec((tm, tn), lambda i, j: (i, j)),
               (M // tm, N // tn), "nn")


def _mm_nt(name, a, b, out_dtype=F32, tm=512, tn=512):
    (M, K), N = a.shape, b.shape[0]
    tm, tn = _tile(M, tm), _tile(N, tn)
    return _mm(name, a, pl.BlockSpec((tm, K), lambda i, j: (i, 0)), b, pl.BlockSpec((tn, K), lambda i, j: (j, 0)),
               jax.ShapeDtypeStruct((M, N), out_dtype), pl.BlockSpec((tm, tn), lambda i, j: (i, j)),
               (M // tm, N // tn), "nt")


def _mm_tn(name, a, b, out_dtype=F32, tm=512, tn=512):
    (K, M), N = a.shape, b.shape[1]
    tm, tn = _tile(M, tm), _tile(N, tn)
    return _mm(name, a, pl.BlockSpec((K, tm), lambda i, j: (0, i)), b, pl.BlockSpec((K, tn), lambda i, j: (0, j)),
               jax.ShapeDtypeStruct((M, N), out_dtype), pl.BlockSpec((tm, tn), lambda i, j: (i, j)),
               (M // tm, N // tn), "tn")


def _mm_cols(name, a, w, out_dtype=F32, tm=512):
    (M, K), (J, _, n) = a.shape, w.shape
    tm = _tile(M, tm)
    return _mm(name, a, pl.BlockSpec((tm, K), lambda j, i: (i, 0)), w, pl.BlockSpec((None, K, n), lambda j, i: (j, 0, 0)),
               jax.ShapeDtypeStruct((J, M, n), out_dtype), pl.BlockSpec((None, tm, n), lambda j, i: (j, i, 0)),
               (J, M // tm), "nn")


def _mm_cols_dx(name, d, w, out_dtype=F32, tm=512):
    (J, M, n), K = d.shape, w.shape[1]
    tm = _tile(M, tm)
    return _mm(name, d, pl.BlockSpec((None, tm, n), lambda i, j: (j, i, 0)), w, pl.BlockSpec((None, K, n), lambda i, j: (j, 0, 0)),
               jax.ShapeDtypeStruct((M, K), out_dtype), pl.BlockSpec((tm, K), lambda i, j: (i, 0)),
               (M // tm, J), "nt", nk=J)


def _mm_cols_dw(name, a, d, out_dtype=F32, tk=512):
    (M, K), (J, _, n) = a.shape, d.shape
    tk = _tile(K, tk)
    return _mm(name, a, pl.BlockSpec((M, tk), lambda j, i: (0, i)), d, pl.BlockSpec((None, M, n), lambda j, i: (j, 0, 0)),
               jax.ShapeDtypeStruct((J, K, n), out_dtype), pl.BlockSpec((None, tk, n), lambda j, i: (j, i, 0)),
               (J, K // tk), "tn")


def _mm_cols_dwt(name, a, d, out_dtype=F32, tk=512):
    (M, K), (J, _, n) = a.shape, d.shape
    tk = _tile(K, tk)
    return _mm(name, d, pl.BlockSpec((None, M, n), lambda j, i: (j, 0, 0)), a, pl.BlockSpec((M, tk), lambda j, i: (0, i)),
               jax.ShapeDtypeStruct((J, n, K), out_dtype), pl.BlockSpec((None, n, tk), lambda j, i: (j, 0, i)),
               (J, K // tk), "tn")


def _mm_rows_resid(name, a, w, resid, gate, tm=512, tn=512):
    (Q, M, k), N = a.shape, w.shape[2]
    tm, tn = _tile(M, tm), _tile(N, tn)

    def body(a_ref, w_ref, r_ref, g_ref, y_ref, x_ref, acc):
        q = pl.program_id(2)
        r = _dot(a_ref[...], w_ref[...], "nn")

        @pl.when(q == 0)
        def _():
            acc[...] = r

        @pl.when(q > 0)
        def _():
            acc[...] += r

        @pl.when(q == Q - 1)
        def _():
            y = acc[...]
            y_ref[...] = y
            x_ref[...] = r_ref[...] + g_ref[...] * y

    return pl.pallas_call(
        body, name=name, grid=(M // tm, N // tn, Q),
        out_shape=(jax.ShapeDtypeStruct((M, N), F32), jax.ShapeDtypeStruct((M, N), F32)),
        in_specs=[pl.BlockSpec((None, tm, k), lambda i, j, q: (q, i, 0)), pl.BlockSpec((None, k, tn), lambda i, j, q: (q, 0, j)),
                  pl.BlockSpec((tm, tn), lambda i, j, q: (i, j)), pl.BlockSpec((1, tn), lambda i, j, q: (0, j))],
        out_specs=(pl.BlockSpec((tm, tn), lambda i, j, q: (i, j)), pl.BlockSpec((tm, tn), lambda i, j, q: (i, j))),
        scratch_shapes=[pltpu.VMEM((tm, tn), F32)], compiler_params=_cp("parallel", "parallel", "arbitrary"))(a, w, resid, gate)


def _mm_rows_dx(name, d, w, out_dtype=F32, tm=512):
    (M, N), (Q, k, _) = d.shape, w.shape
    tm = _tile(M, tm)
    return _mm(name, d, pl.BlockSpec((tm, N), lambda q, i: (i, 0)), w, pl.BlockSpec((None, k, N), lambda q, i: (q, 0, 0)),
               jax.ShapeDtypeStruct((Q, M, k), out_dtype), pl.BlockSpec((None, tm, k), lambda q, i: (q, i, 0)),
               (Q, M // tm), "nt")


def _mm_rows_dw(name, a, d, out_dtype=F32, tn=512):
    (Q, M, k), N = a.shape, d.shape[1]
    tn = _tile(N, tn)
    return _mm(name, a, pl.BlockSpec((None, M, k), lambda q, j: (q, 0, 0)), d, pl.BlockSpec((M, tn), lambda q, j: (0, j)),
               jax.ShapeDtypeStruct((Q, k, N), out_dtype), pl.BlockSpec((None, k, tn), lambda q, j: (q, 0, j)),
               (Q, N // tn), "tn")


def _silu(v):
    return v * jax.nn.sigmoid(v)


def _ada_fwd(c16, ada_w):
    L, D, n = ada_w.shape

    def body(c_ref, w_ref, o_ref):
        o_ref[...] = _dot(_silu(c_ref[...]), w_ref[...], "nn")

    return pl.pallas_call(
        body, name="ada_fwd", grid=(L,), out_shape=jax.ShapeDtypeStruct((L, 16, n), F32),
        in_specs=[pl.BlockSpec((16, D), lambda l: (0, 0)), pl.BlockSpec((None, D, n), lambda l: (l, 0, 0))],
        out_specs=pl.BlockSpec((None, 16, n), lambda l: (l, 0, 0)), compiler_params=_cp("parallel"))(c16, ada_w)


def _ada_bwd(c16, dmod16):
    L, _, n = dmod16.shape
    D = c16.shape[1]

    def body(c_ref, d_ref, o_ref):
        o_ref[...] = _dot(_silu(c_ref[...]), d_ref[...], "tn")

    return pl.pallas_call(
        body, name="ada_bwd", grid=(L,), out_shape=jax.ShapeDtypeStruct((L, D, n), F32),
        in_specs=[pl.BlockSpec((16, D), lambda l: (0, 0)), pl.BlockSpec((None, 16, n), lambda l: (l, 0, 0))],
        out_specs=pl.BlockSpec((None, D, n), lambda l: (l, 0, 0)), compiler_params=_cp("parallel"))(c16, dmod16)


def _row_spec(tr, n):
    return pl.BlockSpec((tr, n), lambda i: (i, 0))


def _vec_spec(n):
    return pl.BlockSpec((1, n), lambda i: (0, 0))


def _rmsmod_fwd(name, x, g, sc, sh, after, tr=256):
    S, D = x.shape

    def body(x_ref, g_ref, sc_ref, sh_ref, after_ref, h_ref):
        xv = x_ref[...]
        rstd = lax.rsqrt(jnp.mean(xv * xv, axis=-1, keepdims=True) + EPS)
        y = xv * rstd * g_ref[...]
        h_ref[...] = (y * (1.0 + sc_ref[...]) + sh_ref[...]).astype(h_ref.dtype)

    return pl.pallas_call(
        body, name=name, grid=(S // tr,), out_shape=jax.ShapeDtypeStruct((S, D), _MXU_DTYPE),
        in_specs=[_row_spec(tr, D), _vec_spec(D), _vec_spec(D), _vec_spec(D), ANY], out_specs=_row_spec(tr, D),
        compiler_params=_cp("parallel"))(x, g, sc, sh, after)


def _acc_rows(ref, val, first):
    s = jnp.sum(val, axis=0, keepdims=True)

    @pl.when(first)
    def _():
        ref[...] = s

    @pl.when(jnp.logical_not(first))
    def _():
        ref[...] += s


def _rmsmod_bwd(name, x, g, sc, dh, dres, after, tr=256):
    S, D = x.shape

    def body(x_ref, g_ref, sc_ref, dh_ref, dres_ref, after_ref, dx_ref, dg_ref, dsc_ref, dsh_ref):
        first = pl.program_id(0) == 0
        xv, dh_v, gv = x_ref[...], dh_ref[...], g_ref[...]
        rstd = lax.rsqrt(jnp.mean(xv * xv, axis=-1, keepdims=True) + EPS)
        xhat = xv * rstd
        _acc_rows(dsh_ref, dh_v, first)
        _acc_rows(dsc_ref, dh_v * (xhat * gv), first)
        dyg = dh_v * (1.0 + sc_ref[...])
        _acc_rows(dg_ref, dyg * xhat, first)
        dxhat = dyg * gv
        dx_ref[...] = dres_ref[...] + rstd * (dxhat - xhat * jnp.mean(dxhat * xhat, axis=-1, keepdims=True))

    vec = jax.ShapeDtypeStruct((1, D), F32)
    return pl.pallas_call(
        body, name=name, grid=(S // tr,), out_shape=(jax.ShapeDtypeStruct((S, D), F32), vec, vec, vec),
        in_specs=[_row_spec(tr, D), _vec_spec(D), _vec_spec(D), _row_spec(tr, D), _row_spec(tr, D), ANY],
        out_specs=(_row_spec(tr, D), _vec_spec(D), _vec_spec(D), _vec_spec(D)),
        compiler_params=_cp("arbitrary"))(x, g, sc, dh, dres, after)


def _loss_head(x, g, target, tr=256):
    S, D = x.shape

    def body(x_ref, g_ref, t_ref, loss_ref, dx_ref, dg_ref):
        first = pl.program_id(0) == 0
        xv, gv = x_ref[...], g_ref[...]
        rstd = lax.rsqrt(jnp.mean(xv * xv, axis=-1, keepdims=True) + EPS)
        xhat = xv * rstd
        err = xhat * gv - t_ref[...]
        part = 0.5 * jnp.sum(jnp.mean(err * err, axis=-1, keepdims=True), axis=0, keepdims=True)

        @pl.when(first)
        def _():
            loss_ref[...] = part

        @pl.when(jnp.logical_not(first))
        def _():
            loss_ref[...] += part

        dout = err * (1.0 / D)
        _acc_rows(dg_ref, dout * xhat, first)
        dxhat = dout * gv
        dx_ref[...] = rstd * (dxhat - xhat * jnp.mean(dxhat * xhat, axis=-1, keepdims=True))

    return pl.pallas_call(
        body, name="loss_head", grid=(S // tr,),
        out_shape=(jax.ShapeDtypeStruct((1, 1), F32), jax.ShapeDtypeStruct((S, D), F32), jax.ShapeDtypeStruct((1, D), F32)),
        in_specs=[_row_spec(tr, D), _vec_spec(D), _row_spec(tr, D)],
        out_specs=(pl.BlockSpec((1, 1), lambda i: (0, 0)), _row_spec(tr, D), _vec_spec(D)),
        compiler_params=_cp("arbitrary"))(x, g, target)


def _gate_bwd(name, dx, y, gate, tr=256):
    S, D = dx.shape

    def body(dx_ref, y_ref, g_ref, dy_ref, dg_ref):
        dxv = dx_ref[...]
        dy_ref[...] = (g_ref[...] * dxv).astype(dy_ref.dtype)
        _acc_rows(dg_ref, dxv * y_ref[...], pl.program_id(0) == 0)

    return pl.pallas_call(
        body, name=name, grid=(S // tr,),
        out_shape=(jax.ShapeDtypeStruct((S, D), _MXU_DTYPE), jax.ShapeDtypeStruct((1, D), F32)),
        in_specs=[_row_spec(tr, D), _row_spec(tr, D), _vec_spec(D)], out_specs=(_row_spec(tr, D), _vec_spec(D)),
        compiler_params=_cp("arbitrary"))(dx, y, gate)


def _shift_down(v, k):
    t = lax.broadcasted_iota(jnp.int32, v.shape, 0)
    return jnp.where(t >= k, pltpu.roll(v, k, axis=0), 0.0)


def _shift_up(v, k):
    n = v.shape[0]
    t = lax.broadcasted_iota(jnp.int32, v.shape, 0)
    return jnp.where(t < n - k, pltpu.roll(v, n - k, axis=0), 0.0)


def _window_sum(p, w, shift):
    s, k = p, 1
    while k < w:
        s = s + shift(s, k)
        k *= 2
    return s


def _pool_count(shape, w):
    t = lax.broadcasted_iota(jnp.int32, shape, 0)
    return jnp.minimum(t + 1, w).astype(F32)


def _ab_specs(S):
    zs = [pl.BlockSpec((None, S, 128), functools.partial(lambda g, q: (2 * q + g // 2, 0, g % 2), q=q)) for q in range(4)]
    return zs


def _ab_mix_fwd(z8, conv_w, mix_w, scale):
    S = z8.shape[1]

    def body(b_ref, c_ref, a_ref, p_ref, w_ref, mix_ref, sc_ref, y_ref):
        g = pl.program_id(0)
        cg = c_ref[...] * a_ref[...]
        w = w_ref[...]
        conv = w[0:1] * _shift_down(cg, 2) + w[1:2] * _shift_down(cg, 1) + w[2:3] * cg
        y_ref[0] = (b_ref[...] * conv).astype(y_ref.dtype)
        for gg, win in enumerate(POOL_WINDOWS):
            @pl.when(g == gg)
            def _(win=win):
                p = p_ref[...]
                pooled = _window_sum(p, win, _shift_down) / _pool_count(p.shape, win) - p
                y_ref[1] = (_dot(pooled, mix_ref[...], "nn") * sc_ref[...]).astype(y_ref.dtype)

    return pl.pallas_call(
        body, name="ab_mix_fwd", grid=(4,), out_shape=jax.ShapeDtypeStruct((2, S, 512), _MXU_DTYPE),
        in_specs=_ab_specs(S) + [pl.BlockSpec((3, 128), lambda g: (0, g)), pl.BlockSpec((None, 128, 128), lambda g: (g, 0, 0)),
                                 pl.BlockSpec((1, 128), lambda g: (0, g))],
        out_specs=pl.BlockSpec((2, S, 128), lambda g: (0, 0, g)), compiler_params=_cp("parallel"))(z8, z8, z8, z8, conv_w, mix_w, scale)


def _ab_mix_bwd(z8, dycat2, conv_w, mix_w, scale):
    S = z8.shape[1]

    def body(b_ref, c_ref, a_ref, p_ref, dy_ref, w_ref, mix_ref, sc_ref, dz_ref, dw_ref, dmix_ref, dsc_ref):
        g = pl.program_id(0)
        bv, cv, av, w = b_ref[...], c_ref[...], a_ref[...], w_ref[...]
        dya = dy_ref[0]
        cg = cv * av
        cg1, cg2 = _shift_down(cg, 1), _shift_down(cg, 2)
        conv = w[0:1] * cg2 + w[1:2] * cg1 + w[2:3] * cg
        dz_ref[0] = (dya * conv).astype(dz_ref.dtype)
        dconv = dya * bv
        dcg = w[2:3] * dconv + w[1:2] * _shift_up(dconv, 1) + w[0:1] * _shift_up(dconv, 2)
        dz_ref[1] = (dcg * av).astype(dz_ref.dtype)
        dz_ref[2] = (dcg * cv).astype(dz_ref.dtype)
        dw_ref[0:1, :] = jnp.sum(dconv * cg2, axis=0, keepdims=True)
        dw_ref[1:2, :] = jnp.sum(dconv * cg1, axis=0, keepdims=True)
        dw_ref[2:3, :] = jnp.sum(dconv * cg, axis=0, keepdims=True)
        for gg, win in enumerate(POOL_WINDOWS):
            @pl.when(g == gg)
            def _(win=win):
                p, dyb, mix = p_ref[...], dy_ref[1], mix_ref[...]
                cnt = _pool_count(p.shape, win)
                pooled = _window_sum(p, win, _shift_down) / cnt - p
                dsc_ref[...] = jnp.sum(dyb * _dot(pooled, mix, "nn"), axis=0, keepdims=True)
                dmixed = dyb * sc_ref[...]
                dmix_ref[...] = _dot(pooled, dmixed, "tn")
                dpooled = _dot(dmixed, mix, "nt")
                dz_ref[3] = (_window_sum(dpooled / cnt, win, _shift_up) - dpooled).astype(dz_ref.dtype)

    return pl.pallas_call(
        body, name="ab_mix_bwd", grid=(4,),
        out_shape=(jax.ShapeDtypeStruct((4, 2, S, 256), _MXU_DTYPE), jax.ShapeDtypeStruct((3, 512), F32),
                   jax.ShapeDtypeStruct((4, 128, 128), F32), jax.ShapeDtypeStruct((1, 512), F32)),
        in_specs=_ab_specs(S) + [pl.BlockSpec((2, S, 128), lambda g: (0, 0, g)), pl.BlockSpec((3, 128), lambda g: (0, g)),
                                 pl.BlockSpec((None, 128, 128), lambda g: (g, 0, 0)), pl.BlockSpec((1, 128), lambda g: (0, g))],
        out_specs=(pl.BlockSpec((4, None, S, 128), lambda g: (0, g // 2, 0, g % 2)), pl.BlockSpec((3, 128), lambda g: (0, g)),
                   pl.BlockSpec((None, 128, 128), lambda g: (g, 0, 0)), pl.BlockSpec((1, 128), lambda g: (0, g))),
        compiler_params=_cp("parallel"))(z8, z8, z8, z8, dycat2, conv_w, mix_w, scale)


HALO = 16


def _ffn_specs(S, n, tr):
    nb = S // HALO
    tile = pl.BlockSpec((2, None, tr, n), lambda j, i: (0, j, i, 0))
    prev = pl.BlockSpec((2, None, HALO, n), lambda j, i: (0, j, jnp.maximum(i * (tr // HALO) - 1, 0), 0))
    nxt = pl.BlockSpec((2, None, HALO, n), lambda j, i: (0, j, jnp.minimum((i + 1) * (tr // HALO), nb - 1), 0))
    cw = pl.BlockSpec((2, None, 3, n), lambda j, i: (0, j, 0, 0))
    return tile, prev, nxt, cw


def _conv_rows(ext, w, lo, tr):
    n = ext.shape[0]
    return (w[0:1] * pltpu.roll(ext, 2, axis=0)[lo:lo + tr] + w[1:2] * pltpu.roll(ext, 1, axis=0)[lo:lo + tr]
            + w[2:3] * ext[lo:lo + tr])


def _ffn_gate_fwd(name, u24, cw24, tr=256):
    _, J, S, n = u24.shape
    tile, prev, _, cw = _ffn_specs(S, n, tr)

    def body(u_ref, up_ref, w_ref, a_ref):
        keep = (pl.program_id(1) > 0).astype(F32)
        z = []
        for h in range(2):
            ext = jnp.concatenate([up_ref[h].astype(F32) * keep, u_ref[h].astype(F32)], axis=0)
            z.append(_conv_rows(ext, w_ref[h], HALO, tr))
        a_ref[...] = (_silu(z[0]) * z[1]).astype(a_ref.dtype)

    return pl.pallas_call(
        body, name=name, grid=(J, S // tr), out_shape=jax.ShapeDtypeStruct((J, S, n), _MXU_DTYPE),
        in_specs=[tile, prev, cw], out_specs=pl.BlockSpec((None, tr, n), lambda j, i: (j, i, 0)),
        compiler_params=_cp("parallel", "parallel"))(u24, u24, cw24)


def _ffn_gate_bwd(name, u24, cw24, da4, tr=256):
    _, J, S, n = u24.shape
    tile, prev, nxt, cw = _ffn_specs(S, n, tr)
    nb = S // HALO
    ext_rows = tr + 2 * HALO

    def body(u_ref, up_ref, un_ref, w_ref, da_ref, dan_ref, du_ref, dcw_ref):
        i = pl.program_id(1)
        first = i == 0
        keep_prev = (i > 0).astype(F32)
        keep_next = (i < S // tr - 1).astype(F32)
        ext = [jnp.concatenate([up_ref[h].astype(F32) * keep_prev, u_ref[h].astype(F32), un_ref[h].astype(F32)], axis=0)
               for h in range(2)]
        w = [w_ref[h] for h in range(2)]
        zg = _conv_rows(ext[0], w[0], HALO, tr + HALO)
        zu = _conv_rows(ext[1], w[1], HALO, tr + HALO)
        da = jnp.concatenate([da_ref[...].astype(F32), dan_ref[...].astype(F32) * keep_next], axis=0)
        sg = jax.nn.sigmoid(zg)
        dz = [da * zu * (sg * (1.0 + zg * (1.0 - sg))), da * (zg * sg)]
        m = tr + HALO
        for h in range(2):
            d = dz[h]
            du = w[h][2:3] * d[:tr] + w[h][1:2] * pltpu.roll(d, m - 1, axis=0)[:tr] + w[h][0:1] * pltpu.roll(d, m - 2, axis=0)[:tr]
            du_ref[h] = du.astype(du_ref.dtype)
            dt = d[:tr]
            e = ext[h]
            parts = [jnp.sum(dt * pltpu.roll(e, 2, axis=0)[HALO:HALO + tr], axis=0, keepdims=True),
                     jnp.sum(dt * pltpu.roll(e, 1, axis=0)[HALO:HALO + tr], axis=0, keepdims=True),
                     jnp.sum(dt * e[HALO:HALO + tr], axis=0, keepdims=True)]
            for k in range(3):
                @pl.when(first)
                def _(k=k, h=h):
                    dcw_ref[h, k:k + 1, :] = parts[k]

                @pl.when(jnp.logical_not(first))
                def _(k=k, h=h):
                    dcw_ref[h, k:k + 1, :] += parts[k]

    da_tile = pl.BlockSpec((None, tr, n), lambda j, i: (j, i, 0))
    da_next = pl.BlockSpec((None, HALO, n), lambda j, i: (j, jnp.minimum((i + 1) * (tr // HALO), nb - 1), 0))
    return pl.pallas_call(
        body, name=name, grid=(J, S // tr),
        out_shape=(jax.ShapeDtypeStruct((2, J, S, n), _MXU_DTYPE), jax.ShapeDtypeStruct((2, J, 3, n), F32)),
        in_specs=[tile, prev, nxt, cw, da_tile, da_next], out_specs=(tile, cw),
        compiler_params=_cp("parallel", "arbitrary"))(u24, u24, u24, cw24, da4, da4)


def _rms_rows(v, g):
    rstd = lax.rsqrt(jnp.mean(v * v, axis=-1, keepdims=True) + EPS)
    return v * rstd * g


def _rms_rows_bwd(v, g, dy):
    rstd = lax.rsqrt(jnp.mean(v * v, axis=-1, keepdims=True) + EPS)
    vhat = v * rstd
    dvhat = dy * g
    return rstd * (dvhat - vhat * jnp.mean(dvhat * vhat, axis=-1, keepdims=True)), dy * vhat


def _mla_prep_fwd(z, qg, kvg, tr=256):
    S = z.shape[0]

    def body(q_ref, kv_ref, qg_ref, kvg_ref, qn_ref, kvn_ref):
        qn_ref[...] = _rms_rows(q_ref[...], qg_ref[...]).astype(qn_ref.dtype)
        kvn_ref[...] = _rms_rows(kv_ref[...], kvg_ref[...]).astype(kvn_ref.dtype)

    return pl.pallas_call(
        body, name="mla_prep_fwd", grid=(S // tr,),
        out_shape=(jax.ShapeDtypeStruct((S, 256), _MXU_DTYPE), jax.ShapeDtypeStruct((S, 128), _MXU_DTYPE)),
        in_specs=[pl.BlockSpec((tr, 256), lambda i: (i, 0)), pl.BlockSpec((tr, 128), lambda i: (i, 2)), _vec_spec(256), _vec_spec(128)],
        out_specs=(_row_spec(tr, 256), _row_spec(tr, 128)), compiler_params=_cp("parallel"))(z, z, qg, kvg)


def _mla_prep_bwd(z, qg, kvg, dqn, dkvn, dkpe, duv, tr=256):
    S = z.shape[0]

    def body(q_ref, kv_ref, qg_ref, kvg_ref, dqn_ref, dkvn_ref, dkpe_ref, duv_ref, dz_ref, dqg_ref, dkvg_ref):
        first = pl.program_id(0) == 0
        dq, dqg = _rms_rows_bwd(q_ref[...], qg_ref[...], dqn_ref[...])
        dkv, dkvg = _rms_rows_bwd(kv_ref[...], kvg_ref[...], dkvn_ref[...])
        _acc_rows(dqg_ref, dqg, first)
        _acc_rows(dkvg_ref, dkvg, first)
        dz_ref[:, 0:256] = dq.astype(dz_ref.dtype)
        dz_ref[:, 256:384] = dkv.astype(dz_ref.dtype)
        dz_ref[:, 384:512] = dkpe_ref[...].astype(dz_ref.dtype)
        dz_ref[:, 512:1536] = duv_ref[...].astype(dz_ref.dtype)

    return pl.pallas_call(
        body, name="mla_prep_bwd", grid=(S // tr,),
        out_shape=(jax.ShapeDtypeStruct((S, 1536), _MXU_DTYPE), jax.ShapeDtypeStruct((1, 256), F32), jax.ShapeDtypeStruct((1, 128), F32)),
        in_specs=[pl.BlockSpec((tr, 256), lambda i: (i, 0)), pl.BlockSpec((tr, 128), lambda i: (i, 2)), _vec_spec(256), _vec_spec(128),
                  _row_spec(tr, 256), _row_spec(tr, 128), _row_spec(tr, 128), _row_spec(tr, 1024)],
        out_specs=(_row_spec(tr, 1536), _vec_spec(256), _vec_spec(128)),
        compiler_params=_cp("arbitrary"))(z, z, qg, kvg, dqn, dkvn, dkpe, duv)


def _rope(v, cos, sa, sb):
    return v * cos + pltpu.roll(v, 112, axis=1) * sa + pltpu.roll(v, 16, axis=1) * sb


def _rope_t(d, cos, sa, sb):
    return d * cos + pltpu.roll(d * sa, 16, axis=1) + pltpu.roll(d * sb, 112, axis=1)


def _rope_fwd(qraw, kvall, z, cosq, cosk, sa, sb, tr=256):
    S = qraw.shape[0]

    def body(q_ref, k_ref, v_ref, kpe_ref, cq_ref, ck_ref, sa_ref, sb_ref, qo_ref, ko_ref, vo_ref):
        cq, ck, sa_v, sb_v = cq_ref[...], ck_ref[...], sa_ref[...], sb_ref[...]
        kpe = _rope(kpe_ref[...], ck, sa_v, sb_v)
        for h in range(8):
            cols = slice(128 * h, 128 * h + 128)
            qo_ref[:, cols] = _rope(q_ref[:, cols], cq, sa_v, sb_v).astype(qo_ref.dtype)
            ko_ref[:, cols] = (k_ref[:, cols] + kpe).astype(ko_ref.dtype)
        vo_ref[...] = v_ref[...].astype(vo_ref.dtype)

    tab = _row_spec(tr, 128)
    return pl.pallas_call(
        body, name="rope_fwd", grid=(S // tr,),
        out_shape=(jax.ShapeDtypeStruct((S, 1024), _MXU_DTYPE), jax.ShapeDtypeStruct((S, 1024), _MXU_DTYPE),
                   jax.ShapeDtypeStruct((S, 512), _MXU_DTYPE)),
        in_specs=[_row_spec(tr, 1024), pl.BlockSpec((tr, 1024), lambda i: (i, 0)), pl.BlockSpec((tr, 512), lambda i: (i, 2)),
                  pl.BlockSpec((tr, 128), lambda i: (i, 3)), tab, tab, tab, tab],
        out_specs=(_row_spec(tr, 1024), _row_spec(tr, 1024), _row_spec(tr, 512)),
        compiler_params=_cp("parallel"))(qraw, kvall, kvall, z, cosq, cosk, sa, sb)


def _rope_bwd(dq, dk, dv, cosq, cosk, sa, sb, tr=256):
    S = dq.shape[0]

    def body(dq_ref, dk_ref, dv_ref, cq_ref, ck_ref, sa_ref, sb_ref, dqo_ref, dkv_ref, dkpe_ref):
        cq, ck, sa_v, sb_v = cq_ref[...], ck_ref[...], sa_ref[...], sb_ref[...]
        tot = jnp.zeros((tr, 128), F32)
        for h in range(8):
            cols = slice(128 * h, 128 * h + 128)
            dqo_ref[:, cols] = _rope_t(dq_ref[:, cols], cq, sa_v, sb_v).astype(dqo_ref.dtype)
            dkh = dk_ref[:, cols]
            tot = tot + dkh
            dkv_ref[:, cols] = dkh.astype(dkv_ref.dtype)
        dkv_ref[:, 1024:1536] = dv_ref[...].astype(dkv_ref.dtype)
        dkpe_ref[...] = _rope_t(tot, ck, sa_v, sb_v)

    tab = _row_spec(tr, 128)
    return pl.pallas_call(
        body, name="rope_bwd", grid=(S // tr,),
        out_shape=(jax.ShapeDtypeStruct((S, 1024), _MXU_DTYPE), jax.ShapeDtypeStruct((S, 1536), _MXU_DTYPE),
                   jax.ShapeDtypeStruct((S, 128), F32)),
        in_specs=[_row_spec(tr, 1024), _row_spec(tr, 1024), _row_spec(tr, 512), tab, tab, tab, tab],
        out_specs=(_row_spec(tr, 1024), _row_spec(tr, 1536), _row_spec(tr, 128)),
        compiler_params=_cp("parallel"))(dq, dk, dv, cosq, cosk, sa, sb)


NEG = -1e30


def _attn_fwd(q, k, v, tq=256, tk=256):
    S = q.shape[0]

    def body(q_ref, k_ref, v_ref, o_ref, lse_ref):
        i = pl.program_id(1)
        row = i * tq + lax.broadcasted_iota(jnp.int32, (tq, tk), 0)
        qs = [q_ref[:, 0:128], q_ref[:, 128:256]]

        def step(kb, carry):
            start = pl.multiple_of(kb * tk, tk)
            col = start + lax.broadcasted_iota(jnp.int32, (tq, tk), 1)
            vv = v_ref[pl.ds(start, tk), :]
            out = []
            for h in range(2):
                m, l, acc = carry[3 * h:3 * h + 3]
                s = _dot(qs[h], k_ref[pl.ds(start, tk), 128 * h:128 * h + 128], "nt") * ATTN_SCALE
                s = jnp.where(col <= row, s, NEG)
                m_new = jnp.maximum(m, jnp.max(s, axis=-1, keepdims=True))
                alpha = jnp.exp(m - m_new)
                p = jnp.exp(s - m_new)
                out += [m_new, alpha * l + jnp.sum(p, axis=-1, keepdims=True), alpha * acc + _dot(p, vv, "nn")]
            return tuple(out)

        init = (jnp.full((tq, 1), NEG, F32), jnp.zeros((tq, 1), F32), jnp.zeros((tq, 128), F32)) * 2
        ma, la, acca, mb, lb, accb = lax.fori_loop(0, (i * tq + tq) // tk, step, init)
        lane = lax.broadcasted_iota(jnp.int32, (tq, 128), 1)
        o_ref[...] = jnp.where(lane < 64, acca / la, accb / lb)
        lse_ref[...] = jnp.where(lane < 64, ma + jnp.log(la), mb + jnp.log(lb))

    return pl.pallas_call(
        body, name="attn_fwd", grid=(4, S // tq),
        out_shape=(jax.ShapeDtypeStruct((S, 512), F32), jax.ShapeDtypeStruct((4, S, 128), F32)),
        in_specs=[pl.BlockSpec((tq, 256), lambda p, i: (i, p)), pl.BlockSpec((S, 256), lambda p, i: (0, p)),
                  pl.BlockSpec((S, 128), lambda p, i: (0, p))],
        out_specs=(pl.BlockSpec((tq, 128), lambda p, i: (i, p)), pl.BlockSpec((None, tq, 128), lambda p, i: (p, i, 0))),
        compiler_params=_cp("parallel", "parallel"))(q, k, v)


def _attn_bwd(q, k, v, o, lse, dycat2, tq=256, tk=256):
    S = q.shape[0]

    def body(q_ref, k_ref, v_ref, o_ref, lse_ref, do_ref, dq_ref, dk_ref, dv_ref):
        j = pl.program_id(1)

        @pl.when(j == 0)
        def _():
            dq_ref[...] = jnp.zeros_like(dq_ref)

        col = j * tk + lax.broadcasted_iota(jnp.int32, (tq, tk), 1)
        lane = lax.broadcasted_iota(jnp.int32, (tq, 128), 1)
        ks = [k_ref[:, 0:128], k_ref[:, 128:256]]
        vv = v_ref[...]

        def step(qb, carry):
            dka, dkb, dvp = carry
            start = pl.multiple_of(qb * tq, tq)
            rows = pl.ds(start, tq)
            row = start + lax.broadcasted_iota(jnp.int32, (tq, tk), 0)
            do, lse_v = do_ref[rows, :], lse_ref[rows, :]
            prod = do * o_ref[rows, :]
            dks = [dka, dkb]
            for h in range(2):
                mine = (lane < 64) if h == 0 else (lane >= 64)
                delta = jnp.sum(jnp.where(mine, prod, 0.0), axis=-1, keepdims=True)
                do_h = jnp.where(mine, do, 0.0)
                qh = q_ref[rows, 128 * h:128 * h + 128]
                s = _dot(qh, ks[h], "nt") * ATTN_SCALE
                p = jnp.where(col <= row, jnp.exp(s - lse_v[:, 64 * h:64 * h + 1]), 0.0)
                dvp = dvp + _dot(p, do_h, "tn")
                ds = p * (_dot(do_h, vv, "nt") - delta) * ATTN_SCALE
                dq_ref[rows, 128 * h:128 * h + 128] += _dot(ds, ks[h], "nn")
                dks[h] = dks[h] + _dot(ds, qh, "tn")
            return dks[0], dks[1], dvp

        zero = jnp.zeros((tk, 128), F32)
        dka, dkb, dvp = lax.fori_loop((j * tk) // tq, S // tq, step, (zero, zero, zero))
        dk_ref[:, 0:128] = dka
        dk_ref[:, 128:256] = dkb
        dv_ref[...] = dvp

    return pl.pallas_call(
        body, name="attn_bwd", grid=(4, S // tk),
        out_shape=(jax.ShapeDtypeStruct((S, 1024), F32), jax.ShapeDtypeStruct((S, 1024), F32), jax.ShapeDtypeStruct((S, 512), F32)),
        in_specs=[pl.BlockSpec((S, 256), lambda p, j: (0, p)), pl.BlockSpec((tk, 256), lambda p, j: (j, p)),
                  pl.BlockSpec((tk, 128), lambda p, j: (j, p)), pl.BlockSpec((S, 128), lambda p, j: (0, p)),
                  pl.BlockSpec((None, S, 128), lambda p, j: (p, 0, 0)), pl.BlockSpec((None, S, 128), lambda p, j: (0, 0, p))],
        out_specs=(pl.BlockSpec((S, 256), lambda p, j: (0, p)), pl.BlockSpec((tk, 256), lambda p, j: (j, p)),
                   pl.BlockSpec((tk, 128), lambda p, j: (j, p))),
        compiler_params=_cp("parallel", "arbitrary"))(q, k, v, o, lse, dycat2)


CHUNK = 128
GELU_C = math.sqrt(2.0 / math.pi)


def _gelu(v):
    t = jnp.tanh(GELU_C * (v + 0.044715 * (v * v * v)))
    return v * (0.5 * (1.0 + t)), t


def _gelu_grad(v, t):
    return 0.5 * (1.0 + t) + v * (0.5 * (1.0 - t * t) * GELU_C * (1.0 + 3.0 * 0.044715 * v * v))


def _tril(w):
    r = lax.broadcasted_iota(jnp.int32, w.shape, 0)
    c = lax.broadcasted_iota(jnp.int32, w.shape, 1)
    return jnp.where(c <= r, w, 0.0)


def _layer_norm(v, g, b):
    xc = v - jnp.mean(v, axis=-1, keepdims=True)
    rstd = lax.rsqrt(jnp.mean(xc * xc, axis=-1, keepdims=True) + EPS)
    xhat = xc * rstd
    return xhat * g + b, xhat, rstd


def _sgu_fwd(z, o, ln_g, ln_b, w_s, b_st, tr=256):
    S = z.shape[0]

    def body(u_ref, v_ref, o_ref, g_ref, b_ref, ws_ref, bs_ref, y_ref):
        gu, _ = _gelu(u_ref[...])
        gv, _ = _gelu(v_ref[...])
        vln, _, _ = _layer_norm(gv, g_ref[...], b_ref[...])
        y_ref[0] = o_ref[...].astype(y_ref.dtype)
        for g in range(4):
            wt = _tril(ws_ref[g])
            cols = slice(128 * g, 128 * g + 128)
            for ch in range(tr // CHUNK):
                rows = slice(CHUNK * ch, CHUNK * ch + CHUNK)
                mixed = _dot(wt, vln[rows, cols], "nn") + bs_ref[:, g:g + 1]
                y_ref[1, rows, cols] = (gu[rows, cols] * mixed).astype(y_ref.dtype)

    return pl.pallas_call(
        body, name="sgu_fwd", grid=(S // tr,), out_shape=jax.ShapeDtypeStruct((2, S, 512), _MXU_DTYPE),
        in_specs=[pl.BlockSpec((tr, 512), lambda i: (i, 1)), pl.BlockSpec((tr, 512), lambda i: (i, 2)), _row_spec(tr, 512),
                  _vec_spec(512), _vec_spec(512), pl.BlockSpec((4, 128, 128), lambda i: (0, 0, 0)), pl.BlockSpec((128, 4), lambda i: (0, 0))],
        out_specs=pl.BlockSpec((2, tr, 512), lambda i: (0, i, 0)), compiler_params=_cp("parallel"))(z, z, o, ln_g, ln_b, w_s, b_st)


def _sgu_bwd(z, dycat2, ln_g, ln_b, w_s, b_st, tr=256):
    S = z.shape[0]

    def body(u_ref, v_ref, dy_ref, g_ref, b_ref, ws_ref, bs_ref, duv_ref, dg_ref, db_ref, dws_ref, dbs_ref):
        first = pl.program_id(0) == 0
        u_pre, v_pre = u_ref[...], v_ref[...]
        gu, tu = _gelu(u_pre)
        gv, tv = _gelu(v_pre)
        gain = g_ref[...]
        vln, xhat, rstd = _layer_norm(gv, gain, b_ref[...])

        @pl.when(first)
        def _():
            dws_ref[...] = jnp.zeros_like(dws_ref)
            dbs_ref[...] = jnp.zeros_like(dbs_ref)

        dvln_cols = []
        for g in range(4):
            wt = _tril(ws_ref[g])
            cols = slice(128 * g, 128 * g + 128)
            dmixed_sum = jnp.zeros((CHUNK, 128), F32)
            dw = jnp.zeros((CHUNK, CHUNK), F32)
            dvln_rows = []
            for ch in range(tr // CHUNK):
                rows = slice(CHUNK * ch, CHUNK * ch + CHUNK)
                vt = vln[rows, cols]
                mixed = _dot(wt, vt, "nn") + bs_ref[:, g:g + 1]
                dyd = dy_ref[rows, cols]
                duv_ref[rows, cols] = (dyd * mixed * _gelu_grad(u_pre[rows, cols], tu[rows, cols])).astype(duv_ref.dtype)
                dmixed = dyd * gu[rows, cols]
                dmixed_sum = dmixed_sum + dmixed
                dw = dw + _dot(dmixed, vt, "nt")
                dvln_rows.append(_dot(wt, dmixed, "tn"))
            dws_ref[g] += _tril(dw)
            dbs_ref[g:g + 1, :] += jnp.sum(dmixed_sum.T, axis=0, keepdims=True)
            dvln_cols.append(jnp.concatenate(dvln_rows, axis=0))
        dvln = jnp.concatenate(dvln_cols, axis=1)
        _acc_rows(dg_ref, dvln * xhat, first)
        _acc_rows(db_ref, dvln, first)
        dxhat = dvln * gain
        dgv = rstd * (dxhat - jnp.mean(dxhat, axis=-1, keepdims=True) - xhat * jnp.mean(dxhat * xhat, axis=-1, keepdims=True))
        duv_ref[:, 512:1024] = (dgv * _gelu_grad(v_pre, tv)).astype(duv_ref.dtype)

    return pl.pallas_call(
        body, name="sgu_bwd", grid=(S // tr,),
        out_shape=(jax.ShapeDtypeStruct((S, 1024), _MXU_DTYPE), jax.ShapeDtypeStruct((1, 512), F32), jax.ShapeDtypeStruct((1, 512), F32),
                   jax.ShapeDtypeStruct((4, 128, 128), F32), jax.ShapeDtypeStruct((4, 128), F32)),
        in_specs=[pl.BlockSpec((tr, 512), lambda i: (i, 1)), pl.BlockSpec((tr, 512), lambda i: (i, 2)),
                  pl.BlockSpec((None, tr, 512), lambda i: (1, i, 0)), _vec_spec(512), _vec_spec(512),
                  pl.BlockSpec((4, 128, 128), lambda i: (0, 0, 0)), pl.BlockSpec((128, 4), lambda i: (0, 0))],
        out_specs=(_row_spec(tr, 1024), _vec_spec(512), _vec_spec(512), pl.BlockSpec((4, 128, 128), lambda i: (0, 0, 0)),
                   pl.BlockSpec((4, 128), lambda i: (0, 0))),
        compiler_params=_cp("arbitrary"))(z, z, dycat2, ln_g, ln_b, w_s, b_st)


def _sum_parts(name, parts, tr=512):
    P, R, C = parts.shape
    tr = _tile(R, tr) if R % 8 == 0 else R

    def body(p_ref, o_ref):
        g = p_ref[0]
        for k in range(1, P):
            g = g + p_ref[k]
        o_ref[...] = g

    return pl.pallas_call(
        body, name=name, grid=(R // tr,), out_shape=jax.ShapeDtypeStruct((R, C), F32),
        in_specs=[pl.BlockSpec((P, tr, C), lambda i: (0, i, 0))], out_specs=_row_spec(tr, C),
        compiler_params=_cp("parallel"))(parts)


ADAMW_BLOCK_BYTES = 36 * 2 ** 20


def _adamw(name, w, m, v, parts):
    L, R, C = w.shape
    P = parts[0].shape[0]
    row_bytes = 2 * C * (7 * 4 + P * parts[0].dtype.itemsize)
    tr = R
    if R * row_bytes > ADAMW_BLOCK_BYTES:
        tr = next(t for t in (1024, 512, 256, 128, 64, 32, 16) if R % t == 0 and t * row_bytes <= ADAMW_BLOCK_BYTES)
    nr = R // tr
    c1 = 1.0 / (1.0 - ADAM_B1 ** ADAM_STEP)
    c2 = 1.0 / (1.0 - ADAM_B2 ** ADAM_STEP)

    def body(w_ref, m_ref, v_ref, *rest):
        p_refs, (g_ref, d_ref, mo_ref, vo_ref) = rest[:L], rest[L:]
        for ll in range(L):
            @pl.when(pl.program_id(0) == ll)
            def _(p_ref=p_refs[ll]):
                g = p_ref[0].astype(F32)
                for k in range(1, P):
                    g = g + p_ref[k].astype(F32)
                m2 = ADAM_B1 * m_ref[...] + (1.0 - ADAM_B1) * g
                v2 = ADAM_B2 * v_ref[...] + (1.0 - ADAM_B2) * (g * g)
                g_ref[...] = g
                mo_ref[...] = m2
                vo_ref[...] = v2
                d_ref[...] = -ADAM_LR * ((m2 * c1) / (jnp.sqrt(v2 * c2) + ADAM_EPS) + ADAM_WD * w_ref[...])

    def part_spec(ll):
        return pl.BlockSpec((P, tr, C), lambda l, i: (0, jnp.where(l == ll, i, jnp.where(l < ll, 0, nr - 1)), 0))

    full = pl.BlockSpec((None, tr, C), lambda l, i: (l, i, 0))
    sds = jax.ShapeDtypeStruct((L, R, C), F32)
    return pl.pallas_call(
        body, name=name, grid=(L, nr), out_shape=(sds, sds, sds, sds),
        in_specs=[full] * 3 + [part_spec(ll) for ll in range(L)],
        out_specs=(full,) * 4, compiler_params=_cp("arbitrary", "arbitrary"))(w, m, v, *parts)


def _rope_tables(positions):
    half = 16
    inv_freq = 10000.0 ** (-jnp.arange(half, dtype=F32) / half)
    ang = positions.astype(F32)[:, None] * inv_freq
    cos, sin = jnp.cos(ang), jnp.sin(ang)
    S = positions.shape[0]
    z16, z32, z64 = jnp.zeros((S, 16), F32), jnp.zeros((S, 32), F32), jnp.zeros((S, 64), F32)
    cosk = jnp.concatenate([z64, cos, cos, z32], axis=1)
    cosq = jnp.concatenate([jnp.ones((S, 64), F32), cos, cos, z32], axis=1)
    sa = jnp.concatenate([z64, -sin, z16, z32], axis=1)
    sb = jnp.concatenate([z64, z16, sin, z32], axis=1)
    return cosq, cosk, sa, sb


def _ffn_fwd(l, x, mod, n2g, w_up8, cw24, w_down4):
    sh, sc, gate = mod
    h = _rmsmod_fwd(f"ffn{l}_norm", x, n2g, sc, sh, n2g)
    u8 = _mm_cols(f"ffn{l}_up", h, w_up8, out_dtype=ACT_DTYPE, tm=1024)
    S, n = u8.shape[1], u8.shape[2]
    u24 = u8.reshape(2, 4, S, n)
    a4 = _ffn_gate_fwd(f"ffn{l}_gate", u24, cw24)
    f, x_new = _mm_rows_resid(f"ffn{l}_down", a4, w_down4, x, gate)
    return x_new, (x, h, u24, a4, f)


def _ffn_bwd(l, dx, saved, mod, n2g, w_up8, cw24, w_down4, me):
    sh, sc, gate = mod
    x, h, u24, a4, f = saved
    df, dgate = _gate_bwd(f"ffn{l}_gate_bwd", dx, f, gate)
    da4 = _mm_rows_dx(f"ffn{l}_down_dx", df, w_down4, out_dtype=ACT_DTYPE)
    dw_down4 = _mm_rows_dw(f"ffn{l}_down_dw", a4, df, out_dtype=WIRE_DTYPE)
    du24, dcw24 = _ffn_gate_bwd(f"ffn{l}_act_bwd", u24, cw24, da4)
    du8 = du24.reshape((8,) + du24.shape[2:])
    dw_up8t = _mm_cols_dwt(f"ffn{l}_up_dw", h, du8, out_dtype=WIRE_DTYPE)
    sent, token = _exchange_start(f"scatter_ffn{l}", [dw_up8t, dw_down4.reshape(8, 352, dw_down4.shape[2])], True, dcw24, me)
    dh = _mm_cols_dx(f"ffn{l}_up_dx", du8, w_up8)
    dx_new, dn2g, dsc, dsh = _rmsmod_bwd(f"ffn{l}_norm_bwd", x, n2g, sc, dh, dx, token)
    return dx_new, dict(sent=sent, cw24=dcw24, n2g=dn2g, mod=(dsh, dsc, dgate))


def kernel(x, c, positions, ada_w, ada_b, norm1_g, norm2_g, ab_w_in, a_conv_w, b_mix_w, b_scale, ab_w_out, cd_w_in, c_q_norm_g, c_w_uq, c_kv_norm_g, c_w_ukv, d_ln_g, d_ln_b, d_w_s, d_b_s, cd_w_out, ffn_w_up, ffn_conv_w, ffn_w_down, final_norm_g, loss_target, m_ada_w, m_ada_b, m_norm1_g, m_norm2_g, m_ab_w_in, m_a_conv_w, m_b_mix_w, m_b_scale, m_ab_w_out, m_cd_w_in, m_c_q_norm_g, m_c_w_uq, m_c_kv_norm_g, m_c_w_ukv, m_d_ln_g, m_d_ln_b, m_d_w_s, m_d_b_s, m_cd_w_out, m_ffn_w_up, m_ffn_conv_w, m_ffn_w_down, m_final_norm_g, v_ada_w, v_ada_b, v_norm1_g, v_norm2_g, v_ab_w_in, v_a_conv_w, v_b_mix_w, v_b_scale, v_ab_w_out, v_cd_w_in, v_c_q_norm_g, v_c_w_uq, v_c_kv_norm_g, v_c_w_ukv, v_d_ln_g, v_d_ln_b, v_d_w_s, v_d_b_s, v_cd_w_out, v_ffn_w_up, v_ffn_conv_w, v_ffn_w_down, v_final_norm_g):
    S, D = x.shape[1], x.shape[2]
    me = 4 * lax.axis_index("x") + 2 * lax.axis_index("y") + lax.axis_index("c")
    x0, target = x[0], loss_target[0]
    W = _MXU_DTYPE

    small_shapes = [(1024,), (3, 64), (32,), (64,), (64,), (2, 3, 704)]
    (g0,) = _exchange("gather_small", [[_pack([c, a_conv_w, c_q_norm_g, d_ln_g, d_ln_b, ffn_conv_w])]], scatter=False)
    c_all, aconv_s, qg_s, lng_s, lnb_s, fcw_s = _unpack(g0[:, 0], small_shapes, lead=(N_DEV,))
    conv_w = aconv_s.transpose(1, 0, 2).reshape(3, 512)
    qg, ln_g, ln_b = qg_s.reshape(1, 256), lng_s.reshape(1, 512), lnb_s.reshape(1, 512)
    cw24 = [fcw_s[:, l].reshape(2, 4, 3, 704) for l in range(2)]
    c16 = jnp.pad(c_all, ((0, 16 - N_DEV), (0, 0)))

    mod_cols = _ada_fwd(c16, ada_w)
    (g1,) = _exchange("gather_mod", [[_pack([mod_cols])]], scatter=False)
    mod_all = _unpack(g1[:, 0], [(2, 16, 768)], lead=(N_DEV,))[0]
    mod_mine = lax.dynamic_index_in_dim(mod_all, me, axis=2, keepdims=False)
    mod = mod_mine.transpose(1, 0, 2).reshape(2, 6 * D) + ada_b
    mods = [[mod[l, k * D:(k + 1) * D].reshape(1, D) for k in range(6)] for l in range(2)]

    gw_ab, token = _exchange_start("gather_w_ab", [ab_w_in[0].astype(W), ab_w_out[0].astype(W)], False, mod, me)
    gw_f0, token = _exchange_start("gather_w_ffn0", [ffn_w_up[0].astype(W), ffn_w_down[0].astype(W)], False, token, me)
    gw_cd, token = _exchange_start("gather_w_cd", [
        cd_w_in[0].astype(W).reshape(1440, 128), c_w_uq[0].astype(W).reshape(192, 128), c_w_ukv[0].astype(W),
        cd_w_out[0].astype(W)], False, token, me)
    gw_f1, started = _exchange_start("gather_w_ffn1", [ffn_w_up[1].astype(W), ffn_w_down[1].astype(W)], False, token, me)

    cosq, cosk, sa, sb = _rope_tables(positions[0])
    n1g = [norm1_g[l].reshape(1, D) for l in range(2)]
    n2g = [norm2_g[l].reshape(1, D) for l in range(2)]
    mix_w, scale = b_mix_w[0], b_scale
    kvg = c_kv_norm_g
    w_s, b_st = d_w_s[0], d_b_s[0].T

    sh1, sc1, g1m = mods[0][:3]
    h_ab = _rmsmod_fwd("ab_norm", x0, n1g[0], sc1, sh1, started)
    w_abin8, w_about = _exchange_wait("wait_w_ab", gw_ab, h_ab)
    w_about2 = w_about.reshape(2, 512, D)
    z8 = _mm_cols("ab_in", h_ab, w_abin8)
    ycat_ab = _ab_mix_fwd(z8, conv_w, mix_w, scale)
    y_ab, x1 = _mm_rows_resid("ab_out", ycat_ab, w_about2, x0, g1m)
    w_up0, w_dn0 = _exchange_wait("wait_w_ffn0", gw_f0, x1)
    w_up8, w_down4 = [w_up0, None], [w_dn0.reshape(4, 704, D), None]
    x2, ffn0_saved = _ffn_fwd(0, x1, mods[0][3:], n2g[0], w_up8[0], cw24[0], w_down4[0])

    w_cdin, w_uq, w_ukv, w_cdout = _exchange_wait("wait_w_cd", gw_cd, x2)
    w_cdout2 = w_cdout.reshape(2, 512, D)
    w_cd = w_cdin.reshape(8, D, 180).transpose(1, 0, 2).reshape(D, 1440)
    zc = lambda n: jnp.zeros((D, n), W)
    w_cd_pad = jnp.concatenate([w_cd[:, :384], zc(64), w_cd[:, 384:416], zc(32), w_cd[:, 416:]], axis=1)
    w_uq_pad = jnp.pad(w_uq.reshape(8, 256, 96).transpose(1, 0, 2), ((0, 0), (0, 0), (0, 32))).reshape(256, 1024)
    w_ukv_h = w_ukv.transpose(1, 0, 2)
    w_k_pad = jnp.pad(w_ukv_h[:, :, :64], ((0, 0), (0, 0), (0, 64))).reshape(128, 1024)
    w_kv_pad = jnp.concatenate([w_k_pad, w_ukv_h[:, :, 64:].reshape(128, 512)], axis=1)

    sh1, sc1, g1c = mods[1][:3]
    h_cd = _rmsmod_fwd("cd_norm", x2, n1g[1], sc1, sh1, n1g[1])
    z_cd = _mm_nn("cd_in", h_cd, w_cd_pad)
    qn, kvn = _mla_prep_fwd(z_cd, qg, kvg)
    qraw = _mm_nn("cd_uq", qn, w_uq_pad)
    kvall = _mm_nn("cd_ukv", kvn, w_kv_pad)
    q_r, k_r, v_r = _rope_fwd(qraw, kvall, z_cd, cosq, cosk, sa, sb)
    o, lse = _attn_fwd(q_r, k_r, v_r)
    ycat_cd = _sgu_fwd(z_cd, o, ln_g, ln_b, w_s, b_st)
    y_cd, x3 = _mm_rows_resid("cd_out", ycat_cd, w_cdout2, x2, g1c)
    w_up1, w_dn1 = _exchange_wait("wait_w_ffn1", gw_f1, x3)
    w_up8[1], w_down4[1] = w_up1, w_dn1.reshape(4, 704, D)
    x4, ffn1_saved = _ffn_fwd(1, x3, mods[1][3:], n2g[1], w_up8[1], cw24[1], w_down4[1])

    loss_local, dx4, dfg = _loss_head(x4, final_norm_g.reshape(1, D), target)

    dx3, gf1 = _ffn_bwd(1, dx4, ffn1_saved, mods[1][3:], n2g[1], w_up8[1], cw24[1], w_down4[1], me)

    dy, dg1c = _gate_bwd("cd_gate_bwd", dx3, y_cd, g1c)
    dycat = _mm_rows_dx("cd_out_dx", dy, w_cdout2)
    dw_cdout = _mm_rows_dw("cd_out_dw", ycat_cd, dy, out_dtype=WIRE_DTYPE)
    duv, dln_g, dln_b, dws, dbs = _sgu_bwd(z_cd, dycat, ln_g, ln_b, w_s, b_st)
    dq_r, dk_r, dv_r = _attn_bwd(q_r, k_r, v_r, o, lse, dycat)
    dqraw, dkvall, dkpe = _rope_bwd(dq_r, dk_r, dv_r, cosq, cosk, sa, sb)
    dqn = _mm_nt("cd_uq_dx", dqraw, w_uq_pad, tn=256)
    dkvn = _mm_nt("cd_ukv_dx", dkvall, w_kv_pad, tn=128)
    dw_uq_pad = _mm_tn("cd_uq_dw", qn, dqraw, tm=256)
    dw_kv_pad = _mm_tn("cd_ukv_dw", kvn, dkvall, tm=128)
    dz_cd, dqg, dkvg = _mla_prep_bwd(z_cd, qg, kvg, dqn, dkvn, dkpe, duv)
    dh_cd = _mm_nt("cd_in_dx", dz_cd, w_cd_pad)
    dw_cd_pad = _mm_tn("cd_in_dw", h_cd, dz_cd)
    dw_cd =jnp.concatenate([dw_cd_pad[:, :384], dw_cd_pad[:, 448:480], dw_cd_pad[:, 512:]], axis=1)
    dw_cd8 = dw_cd.reshape(D, 8, 180).transpose(1, 0, 2).reshape(8, 1440, 128).astype(WIRE_DTYPE)
    dw_uq8 = dw_uq_pad.reshape(256, 8, 128)[:, :, :96].transpose(1, 0, 2).reshape(8, 192, 128).astype(WIRE_DTYPE)
    dw_ukv8 = jnp.concatenate([dw_kv_pad[:, :1024].reshape(128, 8, 128)[:, :, :64], dw_kv_pad[:, 1024:].reshape(128, 8, 64)],
                              axis=2).transpose(1, 0, 2).astype(WIRE_DTYPE)
    sent_cd, token = _exchange_start("scatter_cd", [dw_cd8, dw_uq8, dw_ukv8, dw_cdout.reshape(8, 128, D)], True, dqg, me)
    dx2, dn1g_cd, dsc1_cd, dsh1_cd = _rmsmod_bwd("cd_norm_bwd", x2, n1g[1], sc1, dh_cd, dx3, token)

    dx1, gf0 = _ffn_bwd(0, dx2, ffn0_saved, mods[0][3:], n2g[0], w_up8[0], cw24[0], w_down4[0], me)

    dy, dg1m = _gate_bwd("ab_gate_bwd", dx1, y_ab, g1m)
    dycat = _mm_rows_dx("ab_out_dx", dy, w_about2)
    dw_about = _mm_rows_dw("ab_out_dw", ycat_ab, dy, out_dtype=WIRE_DTYPE)
    dz8, dconv_w, dmix_w, dscale = _ab_mix_bwd(z8, dycat, conv_w, mix_w, scale)
    dz8 = dz8.reshape(8, S, 256)
    dw_abin8 = _mm_cols_dw("ab_in_dw", h_ab, dz8, out_dtype=WIRE_DTYPE)
    sent_ab, token = _exchange_start("scatter_ab", [dw_abin8, dw_about.reshape(8, 128, D)], True, dscale, me)
    dh_ab = _mm_cols_dx("ab_in_dx", dz8, w_abin8)
    dx0, dn1g_ab, dsc1_ab, dsh1_ab = _rmsmod_bwd("ab_norm_bwd", x0, n1g[0], mods[0][1], dh_ab, dx1, token)

    dmod = jnp.stack([jnp.concatenate([dsh1_ab, dsc1_ab, dg1m, *gf0["mod"]], axis=1)[0],
                      jnp.concatenate([dsh1_cd, dsc1_cd, dg1c, *gf1["mod"]], axis=1)[0]])
    rep_grads = [dmod, jnp.concatenate([dn1g_ab, dn1g_cd]), jnp.concatenate([gf0["n2g"], gf1["n2g"]]),
                 dmix_w, dscale, dkvg, dws, dbs, dfg]
    rep_names = ["ada_b", "norm1_g", "norm2_g", "b_mix_w", "b_scale", "c_kv_norm_g", "d_w_s", "d_b_s", "final_norm_g"]
    dfcw = jnp.stack([gf0["cw24"].reshape(8, 3, 704), gf1["cw24"].reshape(8, 3, 704)])
    shard_grads = [dconv_w, dqg, dln_g, dln_b, dfcw]
    rep_buf, shard_buf = _pack(rep_grads), _pack(shard_grads)
    r_rep = rep_buf.shape[0]
    small_sent, token = _exchange_start("gather_small_grads", [jnp.concatenate([rep_buf, shard_buf])], False, dx0, me)

    res = {}

    def update(name, w, m, v, parts, shape3d):
        outs = _adamw("adamw_" + name, w.reshape(shape3d), m.reshape(shape3d), v.reshape(shape3d),
                      [p.reshape((p.shape[0],) + shape3d[1:]) for p in parts])
        res[name] = [o_.reshape(w.shape) for o_ in outs]

    p_up1, p_dn1 = _exchange_wait("wait_scatter_ffn1", gf1["sent"], token)
    p_up0, p_dn0 = _exchange_wait("wait_scatter_ffn0", gf0["sent"], token)
    swap = lambda a: jnp.swapaxes(a, 1, 2)
    update("ffn_w_up", swap(ffn_w_up), swap(m_ffn_w_up), swap(v_ffn_w_up), [p_up0, p_up1], (2, 704, D))
    res["ffn_w_up"] = [swap(o_) for o_ in res["ffn_w_up"]]
    update("ffn_w_down", ffn_w_down, m_ffn_w_down, v_ffn_w_down, [p_dn0, p_dn1], (2, 352, D))
    p_cdin, p_uq, p_ukv, p_cdout = _exchange_wait("wait_scatter_cd", sent_cd, res["ffn_w_down"][0])
    update("cd_w_in", cd_w_in, m_cd_w_in, v_cd_w_in, [p_cdin], (1, 1440, 128))
    update("c_w_uq", c_w_uq, m_c_w_uq, v_c_w_uq, [p_uq], (1, 192, 128))
    update("c_w_ukv", c_w_ukv, m_c_w_ukv, v_c_w_ukv, [p_ukv], (1, 128, 128))
    update("cd_w_out", cd_w_out, m_cd_w_out, v_cd_w_out, [p_cdout], (1, 128, D))
    p_abin, p_about = _exchange_wait("wait_scatter_ab", sent_ab, res["cd_w_out"][0])
    update("ab_w_in", ab_w_in, m_ab_w_in, v_ab_w_in, [p_abin], (1, D, 256))
    update("ab_w_out", ab_w_out, m_ab_w_out, v_ab_w_out, [p_about], (1, 128, D))

    (g2,) = _exchange_wait("wait_small_grads", small_sent, res["ab_w_out"][0])
    dmod_all = g2[:, :2 * 6 * D // 128].reshape(N_DEV, 2, N_DEV, 768)
    dmod_cols = lax.dynamic_index_in_dim(dmod_all, me, axis=2, keepdims=False).transpose(1, 0, 2)
    g_ada_w = _ada_bwd(c16, jnp.pad(dmod_cols, ((0, 0), (0, 16 - N_DEV), (0, 0))))
    update("ada_w", ada_w, m_ada_w, v_ada_w, [g_ada_w[l][None] for l in range(2)], (2, D, 768))

    rep_w = dict(ada_b=(ada_b, m_ada_b, v_ada_b), norm1_g=(norm1_g, m_norm1_g, v_norm1_g), norm2_g=(norm2_g, m_norm2_g, v_norm2_g),
                 b_mix_w=(b_mix_w, m_b_mix_w, v_b_mix_w), b_scale=(b_scale, m_b_scale, v_b_scale),
                 c_kv_norm_g=(c_kv_norm_g, m_c_kv_norm_g, v_c_kv_norm_g), d_w_s=(d_w_s, m_d_w_s, v_d_w_s),
                 d_b_s=(d_b_s, m_d_b_s, v_d_b_s), final_norm_g=(final_norm_g, m_final_norm_g, v_final_norm_g))
    rep_packed = [_pack([rep_w[n][k] for n in rep_names]) for k in range(3)]
    rep_out = _adamw("adamw_replicated", *[a[None] for a in rep_packed], [g2[:, :r_rep]])
    rep_shapes = [rep_w[n][0].shape for n in rep_names]
    for k, n in enumerate(rep_names):
        res[n] = [_unpack(o_, rep_shapes)[k] for o_ in rep_out]

    shard_sum = _sum_parts("sum_small_grads", g2[:, r_rep:])
    g_conv, g_qg, g_lng, g_lnb, g_fcw = _unpack(shard_sum, [(3, 512), (256,), (512,), (512,), (2, 8, 3, 704)])
    mine = lambda a, n, axis: lax.dynamic_slice_in_dim(a, me * n, n, axis=axis)
    sh_names = ["a_conv_w", "c_q_norm_g", "d_ln_g", "d_ln_b", "ffn_conv_w"]
    sh_grads = [mine(g_conv, 64, 1), mine(g_qg, 32, 0), mine(g_lng, 64, 0), mine(g_lnb, 64, 0),
                lax.dynamic_index_in_dim(g_fcw, me, axis=1, keepdims=False)]
    sh_w = dict(a_conv_w=(a_conv_w, m_a_conv_w, v_a_conv_w), c_q_norm_g=(c_q_norm_g, m_c_q_norm_g, v_c_q_norm_g),
                d_ln_g=(d_ln_g, m_d_ln_g, v_d_ln_g), d_ln_b=(d_ln_b, m_d_ln_b, v_d_ln_b),
                ffn_conv_w=(ffn_conv_w, m_ffn_conv_w, v_ffn_conv_w))
    sh_packed = [_pack([sh_w[n][k] for n in sh_names]) for k in range(3)]
    sh_out = _adamw("adamw_small_shards", *[a[None] for a in sh_packed], [_pack(sh_grads)[None]])
    sh_shapes = [sh_w[n][0].shape for n in sh_names]
    for k, n in enumerate(sh_names):
        res[n] = [_unpack(o_, sh_shapes)[k] for o_ in sh_out]

    loss = lax.psum(loss_local[0, 0], ("x", "y", "c"))
    order = ["ada_w", "ada_b", "norm1_g", "norm2_g", "ab_w_in", "a_conv_w", "b_mix_w", "b_scale", "ab_w_out", "cd_w_in", "c_q_norm_g",
             "c_w_uq", "c_kv_norm_g", "c_w_ukv", "d_ln_g", "d_ln_b", "d_w_s", "d_b_s", "cd_w_out", "ffn_w_up", "ffn_conv_w",
             "ffn_w_down", "final_norm_g"]
    return (loss, dx0[None], *[res[n][0] for n in order], *[res[n][1] for n in order], *[res[n][2] for n in order],
            *[res[n][3] for n in order])
```

```python
import functools
import math

import jax
import jax.numpy as jnp
from jax import lax
from jax.experimental import pallas as pl
from jax.experimental.pallas import tpu as pltpu

F32 = jnp.float32
BF16 = jnp.bfloat16
_MXU_DTYPE = BF16
WIRE_DTYPE = BF16
ACT_DTYPE = BF16
_VMEM_LIMIT = 56 * 2 ** 20
N_DEV = 8
EPS = 1e-6
POOL_WINDOWS = (2, 4, 8, 16)
ATTN_SCALE = (64 + 32) ** -0.5
ADAM_LR, ADAM_B1, ADAM_B2, ADAM_EPS, ADAM_WD, ADAM_STEP = 0.001, 0.9, 0.999, 1e-08, 0.01, 10
MESH = pl.DeviceIdType.MESH
ANY = pl.BlockSpec(memory_space=pl.ANY)


def _cp(*sem):
    return pltpu.CompilerParams(dimension_semantics=sem, vmem_limit_bytes=_VMEM_LIMIT)


def _dot(a, b, contract):
    dn = {"nn": (((1,), (0,)), ((), ())), "nt": (((1,), (1,)), ((), ())), "tn": (((0,), (0,)), ((), ()))}[contract]
    return lax.dot_general(a.astype(_MXU_DTYPE), b.astype(_MXU_DTYPE), dn, preferred_element_type=F32)


def _my_position():
    x, y, c = lax.axis_index("x"), lax.axis_index("y"), lax.axis_index("c")
    return x, y, c, 4 * x + 2 * y + c


def _exchange(name, groups, scatter):
    flat = [a for g in groups for a in g]
    n_in, n_grp = len(flat), len(groups)
    out_shapes = []
    for g in groups:
        slab = g[0].shape[1:] if scatter else g[0].shape
        out_shapes.append(jax.ShapeDtypeStruct((N_DEV, len(g)) + tuple(slab), g[0].dtype))

    def body(*refs):
        ins, outs = refs[:n_in], refs[n_in:n_in + n_grp]
        send_sems, recv_sems, local_sems = refs[n_in + n_grp:]
        x, y, c, me = _my_position()
        i = 0
        for gi, g in enumerate(groups):
            for l in range(len(g)):
                src = ins[i]
                i += 1
                pltpu.make_async_copy(src.at[me] if scatter else src, outs[gi].at[me, l], local_sems.at[gi]).start()
                for k in range(1, N_DEV):
                    px = 1 - x if k & 4 else x
                    py = 1 - y if k & 2 else y
                    pc = 1 - c if k & 1 else c
                    peer = 4 * px + 2 * py + pc
                    pltpu.make_async_remote_copy(
                        src_ref=src.at[peer] if scatter else src, dst_ref=outs[gi].at[me, l],
                        send_sem=send_sems.at[gi], recv_sem=recv_sems.at[gi],
                        device_id=(px, py, pc), device_id_type=MESH).start()
        for gi in range(n_grp):
            mine = outs[gi].at[me]
            pltpu.make_async_copy(mine, mine, local_sems.at[gi]).wait()
            seven = outs[gi].at[pl.ds(0, N_DEV - 1)]
            w = pltpu.make_async_remote_copy(src_ref=seven, dst_ref=seven, send_sem=send_sems.at[gi],
                                             recv_sem=recv_sems.at[gi], device_id=(x, y, c), device_id_type=MESH)
            w.wait_send()
            w.wait_recv()

    return pl.pallas_call(
        body, name=name, out_shape=tuple(out_shapes),
        in_specs=[ANY] * n_in, out_specs=tuple([ANY] * n_grp),
        scratch_shapes=[pltpu.SemaphoreType.DMA((n_grp,)), pltpu.SemaphoreType.DMA((n_grp,)),
                        pltpu.SemaphoreType.DMA((n_grp,))],
        compiler_params=pltpu.CompilerParams(has_side_effects=True),
    )(*flat)


HBM_SPEC = pl.BlockSpec(memory_space=pltpu.HBM)
SEM_SPEC = pl.BlockSpec(memory_space=pltpu.SEMAPHORE)
EFFECT = pltpu.SideEffectType.DATAFLOW_SIDE_EFFECTING


def _put_mine(name, srcs, scatter, me):
    n = len(srcs)
    slabs = [tuple(s.shape[1:] if sc else s.shape) for s, sc in zip(srcs, scatter)]

    def body(me_ref, *refs):
        for i in range(n):
            refs[n + i][...] = refs[i][...]

    def at_me(slab):
        return pl.BlockSpec((None,) + slab, lambda g, me_ref, nd=len(slab): (me_ref[0],) + (0,) * nd)

    def whole(slab):
        return pl.BlockSpec(slab, lambda g, me_ref, nd=len(slab): (0,) * nd)

    return pl.pallas_call(
        body, name=name,
        grid_spec=pltpu.PrefetchScalarGridSpec(
            num_scalar_prefetch=1, grid=(1,),
            in_specs=[at_me(slab) if sc else whole(slab) for slab, sc in zip(slabs, scatter)],
            out_specs=[at_me(slab) for slab in slabs]),
        out_shape=[jax.ShapeDtypeStruct((N_DEV,) + slab, s.dtype) for slab, s in zip(slabs, srcs)],
        compiler_params=_cp("arbitrary"))(me.reshape(1), *srcs)


def _exchange_start(name, srcs, scatter, after, me):
    n = len(srcs)
    scatter = list(scatter) if isinstance(scatter, (list, tuple)) else [scatter] * n
    lands = _put_mine(name + "_mine", srcs, scatter, me)
    srcs = [pltpu.with_memory_space_constraint(a, pltpu.HBM) for a in srcs]
    lands = [pltpu.with_memory_space_constraint(a, pltpu.HBM) for a in lands]

    def body(*refs):
        ins, land = refs[:n], refs[n:2 * n]
        send_sems, recv_sems, token = refs[2 * n + 1], refs[2 * n + 2], refs[-1]
        x, y, c, me_in = _my_position()
        for i in range(n):
            for k in range(1, N_DEV):
                px = 1 - x if k & 4 else x
                py = 1 - y if k & 2 else y
                pc = 1 - c if k & 1 else c
                pltpu.make_async_remote_copy(
                    src_ref=ins[i].at[4 * px + 2 * py + pc] if scatter[i] else ins[i], dst_ref=land[i].at[me_in],
                    send_sem=send_sems.at[i], recv_sem=recv_sems.at[i],
                    device_id=(px, py, pc), device_id_type=MESH).start()
        token[...] = jnp.zeros_like(token)

    outs = pl.pallas_call(
        body, name=name,
        out_shape=(pltpu.SemaphoreType.DMA((n,)), pltpu.SemaphoreType.DMA((n,)),
                   *[pltpu.HBM(a.shape, a.dtype) for a in srcs], *[pltpu.HBM(a.shape, a.dtype) for a in lands],
                   jax.ShapeDtypeStruct((8, 128), F32)),
        in_specs=[HBM_SPEC] * (2 * n) + [ANY],
        out_specs=(SEM_SPEC, SEM_SPEC, *[HBM_SPEC] * (2 * n), pl.BlockSpec(memory_space=pltpu.VMEM)),
        input_output_aliases={i: 2 + i for i in range(2 * n)},
        compiler_params=pltpu.CompilerParams(has_side_effects=EFFECT),
    )(*srcs, *lands, after)
    return (outs[0], outs[1], outs[2:2 + n], outs[2 + n:2 + 2 * n]), outs[-1]


def _exchange_wait(name, handle, after):
    send_sems, recv_sems, srcs, lands = handle
    n = len(srcs)

    def body(*refs):
        land, send_ref, recv_ref = refs[n:2 * n], refs[2 * n], refs[2 * n + 1]
        x, y, c, _ = _my_position()
        for i in range(n):
            seven = land[i].at[pl.ds(0, N_DEV - 1)]
            w = pltpu.make_async_remote_copy(src_ref=seven, dst_ref=seven, send_sem=send_ref.at[i], recv_sem=recv_ref.at[i],
                                             device_id=(x, y, c), device_id_type=MESH)
            w.wait_send()
            w.wait_recv()

    outs = pl.pallas_call(
        body, name=name,
        out_shape=(*[pltpu.HBM(a.shape, a.dtype) for a in srcs], *[pltpu.HBM(a.shape, a.dtype) for a in lands]),
        in_specs=[HBM_SPEC] * (2 * n) + [SEM_SPEC, SEM_SPEC, ANY],
        out_specs=tuple([HBM_SPEC] * (2 * n)),
        input_output_aliases={i: i for i in range(2 * n)},
        compiler_params=pltpu.CompilerParams(has_side_effects=EFFECT),
    )(*srcs, *lands, send_sems, recv_sems, after)
    return outs[n:]


def _pack(arrs):
    flat = jnp.concatenate([a.reshape(-1).astype(F32) for a in arrs])
    n = flat.shape[0]
    rows = -(-n // 1024) * 8
    return jnp.pad(flat, (0, rows * 128 - n)).reshape(rows, 128)


def _unpack(buf, shapes, lead=()):
    flat = buf.reshape(lead + (-1,))
    out, off = [], 0
    for s in shapes:
        n = math.prod(s)
        out.append(flat[..., off:off + n].reshape(lead + tuple(s)))
        off += n
    return out


def _mm(name, a, a_spec, b, b_spec, out_sds, o_spec, grid, contract, nk=1):
    o_blk = tuple(d for d in o_spec.block_shape if d is not None)

    def body(a_ref, b_ref, o_ref, *acc):
        r = _dot(a_ref[...], b_ref[...], contract)
        if nk == 1:
            o_ref[...] = r.astype(o_ref.dtype)
        else:
            k = pl.program_id(len(grid) - 1)

            @pl.when(k == 0)
            def _():
                acc[0][...] = r

            @pl.when(k > 0)
            def _():
                acc[0][...] += r

            @pl.when(k == nk - 1)
            def _():
                o_ref[...] = acc[0][...].astype(o_ref.dtype)

    sem = ("parallel",) * (len(grid) - 1) + (("arbitrary",) if nk > 1 else ("parallel",))
    return pl.pallas_call(
        body, name=name, out_shape=out_sds, grid=grid, in_specs=[a_spec, b_spec], out_specs=o_spec,
        scratch_shapes=[pltpu.VMEM(o_blk, F32)] if nk > 1 else [], compiler_params=_cp(*sem))(a, b)


def _tile(n, want):
    t = min(n, want)
    assert n % t == 0, (n, t)
    return t


def _mm_nn(name, a, b, out_dtype=F32, tm=512, tn=512):
    (M, K), N = a.shape, b.shape[1]
    tm, tn = _tile(M, tm), _tile(N, tn)
    return _mm(name, a, pl.BlockSpec((tm, K), lambda i, j: (i, 0)), b, pl.BlockSpec((K, tn), lambda i, j: (0, j)),
               jax.ShapeDtypeStruct((M, N), out_dtype), pl.BlockSpec((tm, tn), lambda i, j: (i, j)),
               (M // tm, N // tn), "nn")


def _mm_nt(name, a, b, out_dtype=F32, tm=512, tn=512):
    (M, K), N = a.shape, b.shape[0]
    tm, tn = _tile(M, tm), _tile(N, tn)
    return _mm(name, a, pl.BlockSpec((tm, K), lambda i, j: (i, 0)), b, pl.BlockSpec((tn, K), lambda i, j: (j, 0)),
               jax.ShapeDtypeStruct((M, N), out_dtype), pl.BlockSpec((tm, tn), lambda i, j: (i, j)),
               (M // tm, N // tn), "nt")


def _mm_tn(name, a, b, out_dtype=F32, tm=512, tn=512):
    (K, M), N = a.shape, b.shape[1]
    tm, tn = _tile(M, tm), _tile(N, tn)
    return _mm(name, a, pl.BlockSpec((K, tm), lambda i, j: (0, i)), b, pl.BlockSpec((K, tn), lambda i, j: (0, j)),
               jax.ShapeDtypeStruct((M, N), out_dtype), pl.BlockSpec((tm, tn), lambda i, j: (i, j)),
               (M // tm, N // tn), "tn")


def _mm_cols(name, a, w, out_dtype=F32, tm=512):
    (M, K), (J, _, n) = a.shape, w.shape
    tm = _tile(M, tm)
    return _mm(name, a, pl.BlockSpec((tm, K), lambda j, i: (i, 0)), w, pl.BlockSpec((None, K, n), lambda j, i: (j, 0, 0)),
               jax.ShapeDtypeStruct((J, M, n), out_dtype), pl.BlockSpec((None, tm, n), lambda j, i: (j, i, 0)),
               (J, M // tm), "nn")


def _mm_cols_dx(name, d, w, out_dtype=F32, tm=512):
    (J, M, n), K = d.shape, w.shape[1]
    tm = _tile(M, tm)
    return _mm(name, d, pl.BlockSpec((None, tm, n), lambda i, j: (j, i, 0)), w, pl.BlockSpec((None, K, n), lambda i, j: (j, 0, 0)),
               jax.ShapeDtypeStruct((M, K), out_dtype), pl.BlockSpec((tm, K), lambda i, j: (i, 0)),
               (M // tm, J), "nt", nk=J)


def _mm_cols_dw(name, a, d, out_dtype=F32, tk=512):
    (M, K), (J, _, n) = a.shape, d.shape
    tk = _tile(K, tk)
    return _mm(name, a, pl.BlockSpec((M, tk), lambda j, i: (0, i)), d, pl.BlockSpec((None, M, n), lambda j, i: (j, 0, 0)),
               jax.ShapeDtypeStruct((J, K, n), out_dtype), pl.BlockSpec((None, tk, n), lambda j, i: (j, i, 0)),
               (J, K // tk), "tn")


def _mm_cols_dwt(name, a, d, out_dtype=F32, tk=512):
    (M, K), (J, _, n) = a.shape, d.shape
    tk = _tile(K, tk)
    return _mm(name, d, pl.BlockSpec((None, M, n), lambda j, i: (j, 0, 0)), a, pl.BlockSpec((M, tk), lambda j, i: (0, i)),
               jax.ShapeDtypeStruct((J, n, K), out_dtype), pl.BlockSpec((None, n, tk), lambda j, i: (j, 0, i)),
               (J, K // tk), "tn")


def _mm_rows_resid(name, a, w, resid, gate, tm=512, tn=512):
    (Q, M, k), N = a.shape, w.shape[2]
    tm, tn = _tile(M, tm), _tile(N, tn)

    def body(a_ref, w_ref, r_ref, g_ref, y_ref, x_ref, acc):
        q = pl.program_id(2)
        r = _dot(a_ref[...], w_ref[...], "nn")

        @pl.when(q == 0)
        def _():
            acc[...] = r

        @pl.when(q > 0)
        def _():
            acc[...] += r

        @pl.when(q == Q - 1)
        def _():
            y = acc[...]
            y_ref[...] = y
            x_ref[...] = r_ref[...] + g_ref[...] * y

    return pl.pallas_call(
        body, name=name, grid=(M // tm, N // tn, Q),
        out_shape=(jax.ShapeDtypeStruct((M, N), F32), jax.ShapeDtypeStruct((M, N), F32)),
        in_specs=[pl.BlockSpec((None, tm, k), lambda i, j, q: (q, i, 0)), pl.BlockSpec((None, k, tn), lambda i, j, q: (q, 0, j)),
                  pl.BlockSpec((tm, tn), lambda i, j, q: (i, j)), pl.BlockSpec((1, tn), lambda i, j, q: (0, j))],
        out_specs=(pl.BlockSpec((tm, tn), lambda i, j, q: (i, j)), pl.BlockSpec((tm, tn), lambda i, j, q: (i, j))),
        scratch_shapes=[pltpu.VMEM((tm, tn), F32)], compiler_params=_cp("parallel", "parallel", "arbitrary"))(a, w, resid, gate)


def _mm_rows_dx(name, d, w, out_dtype=F32, tm=512):
    (M, N), (Q, k, _) = d.shape, w.shape
    tm = _tile(M, tm)
    return _mm(name, d, pl.BlockSpec((tm, N), lambda q, i: (i, 0)), w, pl.BlockSpec((None, k, N), lambda q, i: (q, 0, 0)),
               jax.ShapeDtypeStruct((Q, M, k), out_dtype), pl.BlockSpec((None, tm, k), lambda q, i: (q, i, 0)),
               (Q, M // tm), "nt")


def _mm_rows_dw(name, a, d, out_dtype=F32, tn=512):
    (Q, M, k), N = a.shape, d.shape[1]
    tn = _tile(N, tn)
    return _mm(name, a, pl.BlockSpec((None, M, k), lambda q, j: (q, 0, 0)), d, pl.BlockSpec((M, tn), lambda q, j: (0, j)),
               jax.ShapeDtypeStruct((Q, k, N), out_dtype), pl.BlockSpec((None, k, tn), lambda q, j: (q, 0, j)),
               (Q, N // tn), "tn")


def _silu(v):
    return v * jax.nn.sigmoid(v)


def _ada_fwd(c16, ada_w):
    L, D, n = ada_w.shape

    def body(c_ref, w_ref, o_ref):
        o_ref[...] = _dot(_silu(c_ref[...]), w_ref[...], "nn")

    return pl.pallas_call(
        body, name="ada_fwd", grid=(L,), out_shape=jax.ShapeDtypeStruct((L, 16, n), F32),
        in_specs=[pl.BlockSpec((16, D), lambda l: (0, 0)), pl.BlockSpec((None, D, n), lambda l: (l, 0, 0))],
        out_specs=pl.BlockSpec((None, 16, n), lambda l: (l, 0, 0)), compiler_params=_cp("parallel"))(c16, ada_w)


def _ada_bwd(c16, dmod16):
    L, _, n = dmod16.shape
    D = c16.shape[1]

    def body(c_ref, d_ref, o_ref):
        o_ref[...] = _dot(_silu(c_ref[...]), d_ref[...], "tn")

    return pl.pallas_call(
        body, name="ada_bwd", grid=(L,), out_shape=jax.ShapeDtypeStruct((L, D, n), F32),
        in_specs=[pl.BlockSpec((16, D), lambda l: (0, 0)), pl.BlockSpec((None, 16, n), lambda l: (l, 0, 0))],
        out_specs=pl.BlockSpec((None, D, n), lambda l: (l, 0, 0)), compiler_params=_cp("parallel"))(c16, dmod16)


def _row_spec(tr, n):
    return pl.BlockSpec((tr, n), lambda i: (i, 0))


def _vec_spec(n):
    return pl.BlockSpec((1, n), lambda i: (0, 0))


def _rmsmod_fwd(name, x, g, sc, sh, after, tr=256):
    S, D = x.shape

    def body(x_ref, g_ref, sc_ref, sh_ref, after_ref, h_ref):
        xv = x_ref[...]
        rstd = lax.rsqrt(jnp.mean(xv * xv, axis=-1, keepdims=True) + EPS)
        y = xv * rstd * g_ref[...]
        h_ref[...] = (y * (1.0 + sc_ref[...]) + sh_ref[...]).astype(h_ref.dtype)

    return pl.pallas_call(
        body, name=name, grid=(S // tr,), out_shape=jax.ShapeDtypeStruct((S, D), _MXU_DTYPE),
        in_specs=[_row_spec(tr, D), _vec_spec(D), _vec_spec(D), _vec_spec(D), ANY], out_specs=_row_spec(tr, D),
        compiler_params=_cp("parallel"))(x, g, sc, sh, after)


def _acc_rows(ref, val, first):
    s = jnp.sum(val, axis=0, keepdims=True)

    @pl.when(first)
    def _():
        ref[...] = s

    @pl.when(jnp.logical_not(first))
    def _():
        ref[...] += s


def _rmsmod_bwd(name, x, g, sc, dh, dres, after, tr=256):
    S, D = x.shape

    def body(x_ref, g_ref, sc_ref, dh_ref, dres_ref, after_ref, dx_ref, dg_ref, dsc_ref, dsh_ref):
        first = pl.program_id(0) == 0
        xv, dh_v, gv = x_ref[...], dh_ref[...], g_ref[...]
        rstd = lax.rsqrt(jnp.mean(xv * xv, axis=-1, keepdims=True) + EPS)
        xhat = xv * rstd
        _acc_rows(dsh_ref, dh_v, first)
        _acc_rows(dsc_ref, dh_v * (xhat * gv), first)
        dyg = dh_v * (1.0 + sc_ref[...])
        _acc_rows(dg_ref, dyg * xhat, first)
        dxhat = dyg * gv
        dx_ref[...] = dres_ref[...] + rstd * (dxhat - xhat * jnp.mean(dxhat * xhat, axis=-1, keepdims=True))

    vec = jax.ShapeDtypeStruct((1, D), F32)
    return pl.pallas_call(
        body, name=name, grid=(S // tr,), out_shape=(jax.ShapeDtypeStruct((S, D), F32), vec, vec, vec),
        in_specs=[_row_spec(tr, D), _vec_spec(D), _vec_spec(D), _row_spec(tr, D), _row_spec(tr, D), ANY],
        out_specs=(_row_spec(tr, D), _vec_spec(D), _vec_spec(D), _vec_spec(D)),
        compiler_params=_cp("arbitrary"))(x, g, sc, dh, dres, after)


def _loss_head(x, g, target, tr=256):
    S, D = x.shape

    def body(x_ref, g_ref, t_ref, loss_ref, dx_ref, dg_ref):
        first = pl.program_id(0) == 0
        xv, gv = x_ref[...], g_ref[...]
        rstd = lax.rsqrt(jnp.mean(xv * xv, axis=-1, keepdims=True) + EPS)
        xhat = xv * rstd
        err = xhat * gv - t_ref[...]
        part = 0.5 * jnp.sum(jnp.mean(err * err, axis=-1, keepdims=True), axis=0, keepdims=True)

        @pl.when(first)
        def _():
            loss_ref[...] = part

        @pl.when(jnp.logical_not(first))
        def _():
            loss_ref[...] += part

        dout = err * (1.0 / D)
        _acc_rows(dg_ref, dout * xhat, first)
        dxhat = dout * gv
        dx_ref[...] = rstd * (dxhat - xhat * jnp.mean(dxhat * xhat, axis=-1, keepdims=True))

    return pl.pallas_call(
        body, name="loss_head", grid=(S // tr,),
        out_shape=(jax.ShapeDtypeStruct((1, 1), F32), jax.ShapeDtypeStruct((S, D), F32), jax.ShapeDtypeStruct((1, D), F32)),
        in_specs=[_row_spec(tr, D), _vec_spec(D), _row_spec(tr, D)],
        out_specs=(pl.BlockSpec((1, 1), lambda i: (0, 0)), _row_spec(tr, D), _vec_spec(D)),
        compiler_params=_cp("arbitrary"))(x, g, target)


def _gate_bwd(name, dx, y, gate, tr=256):
    S, D = dx.shape

    def body(dx_ref, y_ref, g_ref, dy_ref, dg_ref):
        dxv = dx_ref[...]
        dy_ref[...] = (g_ref[...] * dxv).astype(dy_ref.dtype)
        _acc_rows(dg_ref, dxv * y_ref[...], pl.program_id(0) == 0)

    return pl.pallas_call(
        body, name=name, grid=(S // tr,),
        out_shape=(jax.ShapeDtypeStruct((S, D), _MXU_DTYPE), jax.ShapeDtypeStruct((1, D), F32)),
        in_specs=[_row_spec(tr, D), _row_spec(tr, D), _vec_spec(D)], out_specs=(_row_spec(tr, D), _vec_spec(D)),
        compiler_params=_cp("arbitrary"))(dx, y, gate)


def _shift_down(v, k):
    t = lax.broadcasted_iota(jnp.int32, v.shape, 0)
    return jnp.where(t >= k, pltpu.roll(v, k, axis=0), 0.0)


def _shift_up(v, k):
    n = v.shape[0]
    t = lax.broadcasted_iota(jnp.int32, v.shape, 0)
    return jnp.where(t < n - k, pltpu.roll(v, n - k, axis=0), 0.0)


def _window_sum(p, w, shift):
    s, k = p, 1
    while k < w:
        s = s + shift(s, k)
        k *= 2
    return s


def _pool_count(shape, w):
    t = lax.broadcasted_iota(jnp.int32, shape, 0)
    return jnp.minimum(t + 1, w).astype(F32)


def _ab_specs(S):
    zs = [pl.BlockSpec((None, S, 128), functools.partial(lambda g, q: (2 * q + g // 2, 0, g % 2), q=q)) for q in range(4)]
    return zs


def _ab_mix_fwd(z8, conv_w, mix_w, scale):
    S = z8.shape[1]

    def body(b_ref, c_ref, a_ref, p_ref, w_ref, mix_ref, sc_ref, y_ref):
        g = pl.program_id(0)
        cg = c_ref[...] * a_ref[...]
        w = w_ref[...]
        conv = w[0:1] * _shift_down(cg, 2) + w[1:2] * _shift_down(cg, 1) + w[2:3] * cg
        y_ref[0] = (b_ref[...] * conv).astype(y_ref.dtype)
        for gg, win in enumerate(POOL_WINDOWS):
            @pl.when(g == gg)
            def _(win=win):
                p = p_ref[...]
                pooled = _window_sum(p, win, _shift_down) / _pool_count(p.shape, win) - p
                y_ref[1] = (_dot(pooled, mix_ref[...], "nn") * sc_ref[...]).astype(y_ref.dtype)

    return pl.pallas_call(
        body, name="ab_mix_fwd", grid=(4,), out_shape=jax.ShapeDtypeStruct((2, S, 512), _MXU_DTYPE),
        in_specs=_ab_specs(S) + [pl.BlockSpec((3, 128), lambda g: (0, g)), pl.BlockSpec((None, 128, 128), lambda g: (g, 0, 0)),
                                 pl.BlockSpec((1, 128), lambda g: (0, g))],
        out_specs=pl.BlockSpec((2, S, 128), lambda g: (0, 0, g)), compiler_params=_cp("parallel"))(z8, z8, z8, z8, conv_w, mix_w, scale)


def _ab_mix_bwd(z8, dycat2, conv_w, mix_w, scale):
    S = z8.shape[1]

    def body(b_ref, c_ref, a_ref, p_ref, dy_ref, w_ref, mix_ref, sc_ref, dz_ref, dw_ref, dmix_ref, dsc_ref):
        g = pl.program_id(0)
        bv, cv, av, w = b_ref[...], c_ref[...], a_ref[...], w_ref[...]
        dya = dy_ref[0]
        cg = cv * av
        cg1, cg2 = _shift_down(cg, 1), _shift_down(cg, 2)
        conv = w[0:1] * cg2 + w[1:2] * cg1 + w[2:3] * cg
        dz_ref[0] = (dya * conv).astype(dz_ref.dtype)
        dconv = dya * bv
        dcg = w[2:3] * dconv + w[1:2] * _shift_up(dconv, 1) + w[0:1] * _shift_up(dconv, 2)
        dz_ref[1] = (dcg * av).astype(dz_ref.dtype)
        dz_ref[2] = (dcg * cv).astype(dz_ref.dtype)
        dw_ref[0:1, :] = jnp.sum(dconv * cg2, axis=0, keepdims=True)
        dw_ref[1:2, :] = jnp.sum(dconv * cg1, axis=0, keepdims=True)
        dw_ref[2:3, :] = jnp.sum(dconv * cg, axis=0, keepdims=True)
        for gg, win in enumerate(POOL_WINDOWS):
            @pl.when(g == gg)
            def _(win=win):
                p, dyb, mix = p_ref[...], dy_ref[1], mix_ref[...]
                cnt = _pool_count(p.shape, win)
                pooled = _window_sum(p, win, _shift_down) / cnt - p
                dsc_ref[...] = jnp.sum(dyb * _dot(pooled, mix, "nn"), axis=0, keepdims=True)
                dmixed = dyb * sc_ref[...]
                dmix_ref[...] = _dot(pooled, dmixed, "tn")
                dpooled = _dot(dmixed, mix, "nt")
                dz_ref[3] = (_window_sum(dpooled / cnt, win, _shift_up) - dpooled).astype(dz_ref.dtype)

    return pl.pallas_call(
        body, name="ab_mix_bwd", grid=(4,),
        out_shape=(jax.ShapeDtypeStruct((4, 2, S, 256), _MXU_DTYPE), jax.ShapeDtypeStruct((3, 512), F32),
                   jax.ShapeDtypeStruct((4, 128, 128), F32), jax.ShapeDtypeStruct((1, 512), F32)),
        in_specs=_ab_specs(S) + [pl.BlockSpec((2, S, 128), lambda g: (0, 0, g)), pl.BlockSpec((3, 128), lambda g: (0, g)),
                                 pl.BlockSpec((None, 128, 128), lambda g: (g, 0, 0)), pl.BlockSpec((1, 128), lambda g: (0, g))],
        out_specs=(pl.BlockSpec((4, None, S, 128), lambda g: (0, g // 2, 0, g % 2)), pl.BlockSpec((3, 128), lambda g: (0, g)),
                   pl.BlockSpec((None, 128, 128), lambda g: (g, 0, 0)), pl.BlockSpec((1, 128), lambda g: (0, g))),
        compiler_params=_cp("parallel"))(z8, z8, z8, z8, dycat2, conv_w, mix_w, scale)


HALO = 16


def _ffn_specs(S, n, tr):
    nb = S // HALO
    tile = pl.BlockSpec((2, None, tr, n), lambda j, i: (0, j, i, 0))
    prev = pl.BlockSpec((2, None, HALO, n), lambda j, i: (0, j, jnp.maximum(i * (tr // HALO) - 1, 0), 0))
    nxt = pl.BlockSpec((2, None, HALO, n), lambda j, i: (0, j, jnp.minimum((i + 1) * (tr // HALO), nb - 1), 0))
    cw = pl.BlockSpec((2, None, 3, n), lambda j, i: (0, j, 0, 0))
    return tile, prev, nxt, cw


def _conv_rows(ext, w, lo, tr):
    n = ext.shape[0]
    return (w[0:1] * pltpu.roll(ext, 2, axis=0)[lo:lo + tr] + w[1:2] * pltpu.roll(ext, 1, axis=0)[lo:lo + tr]
            + w[2:3] * ext[lo:lo + tr])


def _ffn_gate_fwd(name, u24, cw24, tr=256):
    _, J, S, n = u24.shape
    tile, prev, _, cw = _ffn_specs(S, n, tr)

    def body(u_ref, up_ref, w_ref, a_ref):
        keep = (pl.program_id(1) > 0).astype(F32)
        z = []
        for h in range(2):
            ext = jnp.concatenate([up_ref[h].astype(F32) * keep, u_ref[h].astype(F32)], axis=0)
            z.append(_conv_rows(ext, w_ref[h], HALO, tr))
        a_ref[...] = (_silu(z[0]) * z[1]).astype(a_ref.dtype)

    return pl.pallas_call(
        body, name=name, grid=(J, S // tr), out_shape=jax.ShapeDtypeStruct((J, S, n), _MXU_DTYPE),
        in_specs=[tile, prev, cw], out_specs=pl.BlockSpec((None, tr, n), lambda j, i: (j, i, 0)),
        compiler_params=_cp("parallel", "parallel"))(u24, u24, cw24)


def _ffn_gate_bwd(name, u24, cw24, da4, tr=256):
    _, J, S, n = u24.shape
    tile, prev, nxt, cw = _ffn_specs(S, n, tr)
    nb = S // HALO
    ext_rows = tr + 2 * HALO

    def body(u_ref, up_ref, un_ref, w_ref, da_ref, dan_ref, du_ref, dcw_ref):
        i = pl.program_id(1)
        first = i == 0
        keep_prev = (i > 0).astype(F32)
        keep_next = (i < S // tr - 1).astype(F32)
        ext = [jnp.concatenate([up_ref[h].astype(F32) * keep_prev, u_ref[h].astype(F32), un_ref[h].astype(F32)], axis=0)
               for h in range(2)]
        w = [w_ref[h] for h in range(2)]
        zg = _conv_rows(ext[0], w[0], HALO, tr + HALO)
        zu = _conv_rows(ext[1], w[1], HALO, tr + HALO)
        da = jnp.concatenate([da_ref[...].astype(F32), dan_ref[...].astype(F32) * keep_next], axis=0)
        sg = jax.nn.sigmoid(zg)
        dz = [da * zu * (sg * (1.0 + zg * (1.0 - sg))), da * (zg * sg)]
        m = tr + HALO
        for h in range(2):
            d = dz[h]
            du = w[h][2:3] * d[:tr] + w[h][1:2] * pltpu.roll(d, m - 1, axis=0)[:tr] + w[h][0:1] * pltpu.roll(d, m - 2, axis=0)[:tr]
            du_ref[h] = du.astype(du_ref.dtype)
            dt = d[:tr]
            e = ext[h]
            parts = [jnp.sum(dt * pltpu.roll(e, 2, axis=0)[HALO:HALO + tr], axis=0, keepdims=True),
                     jnp.sum(dt * pltpu.roll(e, 1, axis=0)[HALO:HALO + tr], axis=0, keepdims=True),
                     jnp.sum(dt * e[HALO:HALO + tr], axis=0, keepdims=True)]
            for k in range(3):
                @pl.when(first)
                def _(k=k, h=h):
                    dcw_ref[h, k:k + 1, :] = parts[k]

                @pl.when(jnp.logical_not(first))
                def _(k=k, h=h):
                    dcw_ref[h, k:k + 1, :] += parts[k]

    da_tile = pl.BlockSpec((None, tr, n), lambda j, i: (j, i, 0))
    da_next = pl.BlockSpec((None, HALO, n), lambda j, i: (j, jnp.minimum((i + 1) * (tr // HALO), nb - 1), 0))
    return pl.pallas_call(
        body, name=name, grid=(J, S // tr),
        out_shape=(jax.ShapeDtypeStruct((2, J, S, n), _MXU_DTYPE), jax.ShapeDtypeStruct((2, J, 3, n), F32)),
        in_specs=[tile, prev, nxt, cw, da_tile, da_next], out_specs=(tile, cw),
        compiler_params=_cp("parallel", "arbitrary"))(u24, u24, u24, cw24, da4, da4)


def _rms_rows(v, g):
    rstd = lax.rsqrt(jnp.mean(v * v, axis=-1, keepdims=True) + EPS)
    return v * rstd * g


def _rms_rows_bwd(v, g, dy):
    rstd = lax.rsqrt(jnp.mean(v * v, axis=-1, keepdims=True) + EPS)
    vhat = v * rstd
    dvhat = dy * g
    return rstd * (dvhat - vhat * jnp.mean(dvhat * vhat, axis=-1, keepdims=True)), dy * vhat


def _mla_prep_fwd(z, qg, kvg, tr=256):
    S = z.shape[0]

    def body(q_ref, kv_ref, qg_ref, kvg_ref, qn_ref, kvn_ref):
        qn_ref[...] = _rms_rows(q_ref[...], qg_ref[...]).astype(qn_ref.dtype)
        kvn_ref[...] = _rms_rows(kv_ref[...], kvg_ref[...]).astype(kvn_ref.dtype)

    return pl.pallas_call(
        body, name="mla_prep_fwd", grid=(S // tr,),
        out_shape=(jax.ShapeDtypeStruct((S, 256), _MXU_DTYPE), jax.ShapeDtypeStruct((S, 128), _MXU_DTYPE)),
        in_specs=[pl.BlockSpec((tr, 256), lambda i: (i, 0)), pl.BlockSpec((tr, 128), lambda i: (i, 2)), _vec_spec(256), _vec_spec(128)],
        out_specs=(_row_spec(tr, 256), _row_spec(tr, 128)), compiler_params=_cp("parallel"))(z, z, qg, kvg)


def _mla_prep_bwd(z, qg, kvg, dqn, dkvn, dkpe, duv, tr=256):
    S = z.shape[0]

    def body(q_ref, kv_ref, qg_ref, kvg_ref, dqn_ref, dkvn_ref, dkpe_ref, duv_ref, dz_ref, dqg_ref, dkvg_ref):
        first = pl.program_id(0) == 0
        dq, dqg = _rms_rows_bwd(q_ref[...], qg_ref[...], dqn_ref[...])
        dkv, dkvg = _rms_rows_bwd(kv_ref[...], kvg_ref[...], dkvn_ref[...])
        _acc_rows(dqg_ref, dqg, first)
        _acc_rows(dkvg_ref, dkvg, first)
        dz_ref[:, 0:256] = dq.astype(dz_ref.dtype)
        dz_ref[:, 256:384] = dkv.astype(dz_ref.dtype)
        dz_ref[:, 384:512] = dkpe_ref[...].astype(dz_ref.dtype)
        dz_ref[:, 512:1536] = duv_ref[...].astype(dz_ref.dtype)

    return pl.pallas_call(
        body, name="mla_prep_bwd", grid=(S // tr,),
        out_shape=(jax.ShapeDtypeStruct((S, 1536), _MXU_DTYPE), jax.ShapeDtypeStruct((1, 256), F32), jax.ShapeDtypeStruct((1, 128), F32)),
        in_specs=[pl.BlockSpec((tr, 256), lambda i: (i, 0)), pl.BlockSpec((tr, 128), lambda i: (i, 2)), _vec_spec(256), _vec_spec(128),
                  _row_spec(tr, 256), _row_spec(tr, 128), _row_spec(tr, 128), _row_spec(tr, 1024)],
        out_specs=(_row_spec(tr, 1536), _vec_spec(256), _vec_spec(128)),
        compiler_params=_cp("arbitrary"))(z, z, qg, kvg, dqn, dkvn, dkpe, duv)


def _rope(v, cos, sa, sb):
    return v * cos + pltpu.roll(v, 112, axis=1) * sa + pltpu.roll(v, 16, axis=1) * sb


def _rope_t(d, cos, sa, sb):
    return d * cos + pltpu.roll(d * sa, 16, axis=1) + pltpu.roll(d * sb, 112, axis=1)


def _rope_fwd(qraw, kvall, z, cosq, cosk, sa, sb, tr=256):
    S = qraw.shape[0]

    def body(q_ref, k_ref, v_ref, kpe_ref, cq_ref, ck_ref, sa_ref, sb_ref, qo_ref, ko_ref, vo_ref):
        cq, ck, sa_v, sb_v = cq_ref[...], ck_ref[...], sa_ref[...], sb_ref[...]
        kpe = _rope(kpe_ref[...], ck, sa_v, sb_v)
        for h in range(8):
            cols = slice(128 * h, 128 * h + 128)
            qo_ref[:, cols] = _rope(q_ref[:, cols], cq, sa_v, sb_v).astype(qo_ref.dtype)
            ko_ref[:, cols] = (k_ref[:, cols] + kpe).astype(ko_ref.dtype)
        vo_ref[...] = v_ref[...].astype(vo_ref.dtype)

    tab = _row_spec(tr, 128)
    return pl.pallas_call(
        body, name="rope_fwd", grid=(S // tr,),
        out_shape=(jax.ShapeDtypeStruct((S, 1024), _MXU_DTYPE), jax.ShapeDtypeStruct((S, 1024), _MXU_DTYPE),
                   jax.ShapeDtypeStruct((S, 512), _MXU_DTYPE)),
        in_specs=[_row_spec(tr, 1024), pl.BlockSpec((tr, 1024), lambda i: (i, 0)), pl.BlockSpec((tr, 512), lambda i: (i, 2)),
                  pl.BlockSpec((tr, 128), lambda i: (i, 3)), tab, tab, tab, tab],
        out_specs=(_row_spec(tr, 1024), _row_spec(tr, 1024), _row_spec(tr, 512)),
        compiler_params=_cp("parallel"))(qraw, kvall, kvall, z, cosq, cosk, sa, sb)


def _rope_bwd(dq, dk, dv, cosq, cosk, sa, sb, tr=256):
    S = dq.shape[0]

    def body(dq_ref, dk_ref, dv_ref, cq_ref, ck_ref, sa_ref, sb_ref, dqo_ref, dkv_ref, dkpe_ref):
        cq, ck, sa_v, sb_v = cq_ref[...], ck_ref[...], sa_ref[...], sb_ref[...]
        tot = jnp.zeros((tr, 128), F32)
        for h in range(8):
            cols = slice(128 * h, 128 * h + 128)
            dqo_ref[:, cols] = _rope_t(dq_ref[:, cols], cq, sa_v, sb_v).astype(dqo_ref.dtype)
            dkh = dk_ref[:, cols]
            tot = tot + dkh
            dkv_ref[:, cols] = dkh.astype(dkv_ref.dtype)
        dkv_ref[:, 1024:1536] = dv_ref[...].astype(dkv_ref.dtype)
        dkpe_ref[...] = _rope_t(tot, ck, sa_v, sb_v)

    tab = _row_spec(tr, 128)
    return pl.pallas_call(
        body, name="rope_bwd", grid=(S // tr,),
        out_shape=(jax.ShapeDtypeStruct((S, 1024), _MXU_DTYPE), jax.ShapeDtypeStruct((S, 1536), _MXU_DTYPE),
                   jax.ShapeDtypeStruct((S, 128), F32)),
        in_specs=[_row_spec(tr, 1024), _row_spec(tr, 1024), _row_spec(tr, 512), tab, tab, tab, tab],
        out_specs=(_row_spec(tr, 1024), _row_spec(tr, 1536), _row_spec(tr, 128)),
        compiler_params=_cp("parallel"))(dq, dk, dv, cosq, cosk, sa, sb)


NEG = -1e30


def _attn_fwd(q, k, v, tq=256, tk=256):
    S = q.shape[0]

    def body(q_ref, k_ref, v_ref, o_ref, lse_ref):
        i = pl.program_id(1)
        row = i * tq + lax.broadcasted_iota(jnp.int32, (tq, tk), 0)
        qs = [q_ref[:, 0:128], q_ref[:, 128:256]]

        def step(kb, carry):
            start = pl.multiple_of(kb * tk, tk)
            col = start + lax.broadcasted_iota(jnp.int32, (tq, tk), 1)
            vv = v_ref[pl.ds(start, tk), :]
            out = []
            for h in range(2):
                m, l, acc = carry[3 * h:3 * h + 3]
                s = _dot(qs[h], k_ref[pl.ds(start, tk), 128 * h:128 * h + 128], "nt") * ATTN_SCALE
                s = jnp.where(col <= row, s, NEG)
                m_new = jnp.maximum(m, jnp.max(s, axis=-1, keepdims=True))
                alpha = jnp.exp(m - m_new)
                p = jnp.exp(s - m_new)
                out += [m_new, alpha * l + jnp.sum(p, axis=-1, keepdims=True), alpha * acc + _dot(p, vv, "nn")]
            return tuple(out)

        init = (jnp.full((tq, 1), NEG, F32), jnp.zeros((tq, 1), F32), jnp.zeros((tq, 128), F32)) * 2
        ma, la, acca, mb, lb, accb = lax.fori_loop(0, (i * tq + tq) // tk, step, init)
        lane = lax.broadcasted_iota(jnp.int32, (tq, 128), 1)
        o_ref[...] = jnp.where(lane < 64, acca / la, accb / lb)
        lse_ref[...] = jnp.where(lane < 64, ma + jnp.log(la), mb + jnp.log(lb))

    return pl.pallas_call(
        body, name="attn_fwd", grid=(4, S // tq),
        out_shape=(jax.ShapeDtypeStruct((S, 512), F32), jax.ShapeDtypeStruct((4, S, 128), F32)),
        in_specs=[pl.BlockSpec((tq, 256), lambda p, i: (i, p)), pl.BlockSpec((S, 256), lambda p, i: (0, p)),
                  pl.BlockSpec((S, 128), lambda p, i: (0, p))],
        out_specs=(pl.BlockSpec((tq, 128), lambda p, i: (i, p)), pl.BlockSpec((None, tq, 128), lambda p, i: (p, i, 0))),
        compiler_params=_cp("parallel", "parallel"))(q, k, v)


def _attn_bwd(q, k, v, o, lse, dycat2, tq=256, tk=256):
    S = q.shape[0]

    def body(q_ref, k_ref, v_ref, o_ref, lse_ref, do_ref, dq_ref, dk_ref, dv_ref):
        j = pl.program_id(1)

        @pl.when(j == 0)
        def _():
            dq_ref[...] = jnp.zeros_like(dq_ref)

        col = j * tk + lax.broadcasted_iota(jnp.int32, (tq, tk), 1)
        lane = lax.broadcasted_iota(jnp.int32, (tq, 128), 1)
        ks = [k_ref[:, 0:128], k_ref[:, 128:256]]
        vv = v_ref[...]

        def step(qb, carry):
            dka, dkb, dvp = carry
            start = pl.multiple_of(qb * tq, tq)
            rows = pl.ds(start, tq)
            row = start + lax.broadcasted_iota(jnp.int32, (tq, tk), 0)
            do, lse_v = do_ref[rows, :], lse_ref[rows, :]
            prod = do * o_ref[rows, :]
            dks = [dka, dkb]
            for h in range(2):
                mine = (lane < 64) if h == 0 else (lane >= 64)
                delta = jnp.sum(jnp.where(mine, prod, 0.0), axis=-1, keepdims=True)
                do_h = jnp.where(mine, do, 0.0)
                qh = q_ref[rows, 128 * h:128 * h + 128]
                s = _dot(qh, ks[h], "nt") * ATTN_SCALE
                p = jnp.where(col <= row, jnp.exp(s - lse_v[:, 64 * h:64 * h + 1]), 0.0)
                dvp = dvp + _dot(p, do_h, "tn")
                ds = p * (_dot(do_h, vv, "nt") - delta) * ATTN_SCALE
                dq_ref[rows, 128 * h:128 * h + 128] += _dot(ds, ks[h], "nn")
                dks[h] = dks[h] + _dot(ds, qh, "tn")
            return dks[0], dks[1], dvp

        zero = jnp.zeros((tk, 128), F32)
        dka, dkb, dvp = lax.fori_loop((j * tk) // tq, S // tq, step, (zero, zero, zero))
        dk_ref[:, 0:128] = dka
        dk_ref[:, 128:256] = dkb
        dv_ref[...] = dvp

    return pl.pallas_call(
        body, name="attn_bwd", grid=(4, S // tk),
        out_shape=(jax.ShapeDtypeStruct((S, 1024), F32), jax.ShapeDtypeStruct((S, 1024), F32), jax.ShapeDtypeStruct((S, 512), F32)),
        in_specs=[pl.BlockSpec((S, 256), lambda p, j: (0, p)), pl.BlockSpec((tk, 256), lambda p, j: (j, p)),
                  pl.BlockSpec((tk, 128), lambda p, j: (j, p)), pl.BlockSpec((S, 128), lambda p, j: (0, p)),
                  pl.BlockSpec((None, S, 128), lambda p, j: (p, 0, 0)), pl.BlockSpec((None, S, 128), lambda p, j: (0, 0, p))],
        out_specs=(pl.BlockSpec((S, 256), lambda p, j: (0, p)), pl.BlockSpec((tk, 256), lambda p, j: (j, p)),
                   pl.BlockSpec((tk, 128), lambda p, j: (j, p))),
        compiler_params=_cp("parallel", "arbitrary"))(q, k, v, o, lse, dycat2)


CHUNK = 128
GELU_C = math.sqrt(2.0 / math.pi)


def _gelu(v):
    t = jnp.tanh(GELU_C * (v + 0.044715 * (v * v * v)))
    return v * (0.5 * (1.0 + t)), t


def _gelu_grad(v, t):
    return 0.5 * (1.0 + t) + v * (0.5 * (1.0 - t * t) * GELU_C * (1.0 + 3.0 * 0.044715 * v * v))


def _tril(w):
    r = lax.broadcasted_iota(jnp.int32, w.shape, 0)
    c = lax.broadcasted_iota(jnp.int32, w.shape, 1)
    return jnp.where(c <= r, w, 0.0)


def _layer_norm(v, g, b):
    xc = v - jnp.mean(v, axis=-1, keepdims=True)
    rstd = lax.rsqrt(jnp.mean(xc * xc, axis=-1, keepdims=True) + EPS)
    xhat = xc * rstd
    return xhat * g + b, xhat, rstd


def _sgu_fwd(z, o, ln_g, ln_b, w_s, b_st, tr=256):
    S = z.shape[0]

    def body(u_ref, v_ref, o_ref, g_ref, b_ref, ws_ref, bs_ref, y_ref):
        gu, _ = _gelu(u_ref[...])
        gv, _ = _gelu(v_ref[...])
        vln, _, _ = _layer_norm(gv, g_ref[...], b_ref[...])
        y_ref[0] = o_ref[...].astype(y_ref.dtype)
        for g in range(4):
            wt = _tril(ws_ref[g])
            cols = slice(128 * g, 128 * g + 128)
            for ch in range(tr // CHUNK):
                rows = slice(CHUNK * ch, CHUNK * ch + CHUNK)
                mixed = _dot(wt, vln[rows, cols], "nn") + bs_ref[:, g:g + 1]
                y_ref[1, rows, cols] = (gu[rows, cols] * mixed).astype(y_ref.dtype)

    return pl.pallas_call(
        body, name="sgu_fwd", grid=(S // tr,), out_shape=jax.ShapeDtypeStruct((2, S, 512), _MXU_DTYPE),
        in_specs=[pl.BlockSpec((tr, 512), lambda i: (i, 1)), pl.BlockSpec((tr, 512), lambda i: (i, 2)), _row_spec(tr, 512),
                  _vec_spec(512), _vec_spec(512), pl.BlockSpec((4, 128, 128), lambda i: (0, 0, 0)), pl.BlockSpec((128, 4), lambda i: (0, 0))],
        out_specs=pl.BlockSpec((2, tr, 512), lambda i: (0, i, 0)), compiler_params=_cp("parallel"))(z, z, o, ln_g, ln_b, w_s, b_st)


def _sgu_bwd(z, dycat2, ln_g, ln_b, w_s, b_st, tr=256):
    S = z.shape[0]

    def body(u_ref, v_ref, dy_ref, g_ref, b_ref, ws_ref, bs_ref, duv_ref, dg_ref, db_ref, dws_ref, dbs_ref):
        first = pl.program_id(0) == 0
        u_pre, v_pre = u_ref[...], v_ref[...]
        gu, tu = _gelu(u_pre)
        gv, tv = _gelu(v_pre)
        gain = g_ref[...]
        vln, xhat, rstd = _layer_norm(gv, gain, b_ref[...])

        @pl.when(first)
        def _():
            dws_ref[...] = jnp.zeros_like(dws_ref)
            dbs_ref[...] = jnp.zeros_like(dbs_ref)

        dvln_cols = []
        for g in range(4):
            wt = _tril(ws_ref[g])
            cols = slice(128 * g, 128 * g + 128)
            dmixed_sum = jnp.zeros((CHUNK, 128), F32)
            dw = jnp.zeros((CHUNK, CHUNK), F32)
            dvln_rows = []
            for ch in range(tr // CHUNK):
                rows = slice(CHUNK * ch, CHUNK * ch + CHUNK)
                vt = vln[rows, cols]
                mixed = _dot(wt, vt, "nn") + bs_ref[:, g:g + 1]
                dyd = dy_ref[rows, cols]
                duv_ref[rows, cols] = (dyd * mixed * _gelu_grad(u_pre[rows, cols], tu[rows, cols])).astype(duv_ref.dtype)
                dmixed = dyd * gu[rows, cols]
                dmixed_sum = dmixed_sum + dmixed
                dw = dw + _dot(dmixed, vt, "nt")
                dvln_rows.append(_dot(wt, dmixed, "tn"))
            dws_ref[g] += _tril(dw)
            dbs_ref[g:g + 1, :] += jnp.sum(dmixed_sum.T, axis=0, keepdims=True)
            dvln_cols.append(jnp.concatenate(dvln_rows, axis=0))
        dvln = jnp.concatenate(dvln_cols, axis=1)
        _acc_rows(dg_ref, dvln * xhat, first)
        _acc_rows(db_ref, dvln, first)
        dxhat = dvln * gain
        dgv = rstd * (dxhat - jnp.mean(dxhat, axis=-1, keepdims=True) - xhat * jnp.mean(dxhat * xhat, axis=-1, keepdims=True))
        duv_ref[:, 512:1024] = (dgv * _gelu_grad(v_pre, tv)).astype(duv_ref.dtype)

    return pl.pallas_call(
        body, name="sgu_bwd", grid=(S // tr,),
        out_shape=(jax.ShapeDtypeStruct((S, 1024), _MXU_DTYPE), jax.ShapeDtypeStruct((1, 512), F32), jax.ShapeDtypeStruct((1, 512), F32),
                   jax.ShapeDtypeStruct((4, 128, 128), F32), jax.ShapeDtypeStruct((4, 128), F32)),
        in_specs=[pl.BlockSpec((tr, 512), lambda i: (i, 1)), pl.BlockSpec((tr, 512), lambda i: (i, 2)),
                  pl.BlockSpec((None, tr, 512), lambda i: (1, i, 0)), _vec_spec(512), _vec_spec(512),
                  pl.BlockSpec((4, 128, 128), lambda i: (0, 0, 0)), pl.BlockSpec((128, 4), lambda i: (0, 0))],
        out_specs=(_row_spec(tr, 1024), _vec_spec(512), _vec_spec(512), pl.BlockSpec((4, 128, 128), lambda i: (0, 0, 0)),
                   pl.BlockSpec((4, 128), lambda i: (0, 0))),
        compiler_params=_cp("arbitrary"))(z, z, dycat2, ln_g, ln_b, w_s, b_st)


def _sum_parts(name, parts, tr=512):
    P, R, C = parts.shape
    tr = _tile(R, tr) if R % 8 == 0 else R

    def body(p_ref, o_ref):
        g = p_ref[0]
        for k in range(1, P):
            g = g + p_ref[k]
        o_ref[...] = g

    return pl.pallas_call(
        body, name=name, grid=(R // tr,), out_shape=jax.ShapeDtypeStruct((R, C), F32),
        in_specs=[pl.BlockSpec((P, tr, C), lambda i: (0, i, 0))], out_specs=_row_spec(tr, C),
        compiler_params=_cp("parallel"))(parts)


def _adamw_math(w, m, v, g):
    c1 = 1.0 / (1.0 - ADAM_B1 ** ADAM_STEP)
    c2 = 1.0 / (1.0 - ADAM_B2 ** ADAM_STEP)
    m2 = ADAM_B1 * m + (1.0 - ADAM_B1) * g
    v2 = ADAM_B2 * v + (1.0 - ADAM_B2) * (g * g)
    return -ADAM_LR * ((m2 * c1) / (jnp.sqrt(v2 * c2) + ADAM_EPS) + ADAM_WD * w), m2, v2


def _adamw_small(name, params, parts):
    n = len(params)

    def body(*refs):
        ins, outs = refs[:4 * n], refs[4 * n:]
        for i in range(n):
            w_ref, m_ref, v_ref, p_ref = ins[4 * i:4 * i + 4]
            g = p_ref[0]
            for k in range(1, N_DEV):
                g = g + p_ref[k]
            delta, m2, v2 = _adamw_math(w_ref[...], m_ref[...], v_ref[...], g)
            outs[4 * i][...] = g
            outs[4 * i + 1][...] = delta
            outs[4 * i + 2][...] = m2
            outs[4 * i + 3][...] = v2

    flat = [a for (w, m, v), p in zip(params, parts) for a in (w, m, v, p)]
    out = pl.pallas_call(
        body, name=name, out_shape=[jax.ShapeDtypeStruct(w.shape, F32) for (w, _, _) in params for _ in range(4)],
        compiler_params=pltpu.CompilerParams(vmem_limit_bytes=_VMEM_LIMIT))(*flat)
    return [out[4 * i:4 * i + 4] for i in range(n)]


ADAMW_BLOCK_BYTES = 36 * 2 ** 20


def _adamw(name, w, m, v, parts):
    L, R, C = w.shape
    P = parts[0].shape[0]
    row_bytes = 2 * C * (7 * 4 + P * parts[0].dtype.itemsize)
    tr = R
    if R * row_bytes > ADAMW_BLOCK_BYTES:
        tr = next(t for t in (1024, 512, 256, 128, 64, 32, 16) if R % t == 0 and t * row_bytes <= ADAMW_BLOCK_BYTES)
    nr = R // tr
    c1 = 1.0 / (1.0 - ADAM_B1 ** ADAM_STEP)
    c2 = 1.0 / (1.0 - ADAM_B2 ** ADAM_STEP)

    def body(w_ref, m_ref, v_ref, *rest):
        p_refs, (g_ref, d_ref, mo_ref, vo_ref) = rest[:L], rest[L:]
        for ll in range(L):
            @pl.when(pl.program_id(0) == ll)
            def _(p_ref=p_refs[ll]):
                g = p_ref[0].astype(F32)
                for k in range(1, P):
                    g = g + p_ref[k].astype(F32)
                m2 = ADAM_B1 * m_ref[...] + (1.0 - ADAM_B1) * g
                v2 = ADAM_B2 * v_ref[...] + (1.0 - ADAM_B2) * (g * g)
                g_ref[...] = g
                mo_ref[...] = m2
                vo_ref[...] = v2
                d_ref[...] = -ADAM_LR * ((m2 * c1) / (jnp.sqrt(v2 * c2) + ADAM_EPS) + ADAM_WD * w_ref[...])

    def part_spec(ll):
        return pl.BlockSpec((P, tr, C), lambda l, i: (0, jnp.where(l == ll, i, jnp.where(l < ll, 0, nr - 1)), 0))

    full = pl.BlockSpec((None, tr, C), lambda l, i: (l, i, 0))
    sds = jax.ShapeDtypeStruct((L, R, C), F32)
    return pl.pallas_call(
        body, name=name, grid=(L, nr), out_shape=(sds, sds, sds, sds),
        in_specs=[full] * 3 + [part_spec(ll) for ll in range(L)],
        out_specs=(full,) * 4, compiler_params=_cp("arbitrary", "arbitrary"))(w, m, v, *parts)


def _rope_tables(positions):
    half = 16
    inv_freq = 10000.0 ** (-jnp.arange(half, dtype=F32) / half)
    ang = positions.astype(F32)[:, None] * inv_freq
    cos, sin = jnp.cos(ang), jnp.sin(ang)
    S = positions.shape[0]
    z16, z32, z64 = jnp.zeros((S, 16), F32), jnp.zeros((S, 32), F32), jnp.zeros((S, 64), F32)
    cosk = jnp.concatenate([z64, cos, cos, z32], axis=1)
    cosq = jnp.concatenate([jnp.ones((S, 64), F32), cos, cos, z32], axis=1)
    sa = jnp.concatenate([z64, -sin, z16, z32], axis=1)
    sb = jnp.concatenate([z64, z16, sin, z32], axis=1)
    return cosq, cosk, sa, sb


def _ffn_fwd(l, x, mod, n2g, w_up8, cw24, get_w_down4):
    sh, sc, gate = mod
    h = _rmsmod_fwd(f"ffn{l}_norm", x, n2g, sc, sh, n2g)
    u8 = _mm_cols(f"ffn{l}_up", h, w_up8, out_dtype=ACT_DTYPE, tm=1024)
    S, n = u8.shape[1], u8.shape[2]
    u24 = u8.reshape(2, 4, S, n)
    a4 = _ffn_gate_fwd(f"ffn{l}_gate", u24, cw24)
    w_down4 = get_w_down4(a4)
    f, x_new = _mm_rows_resid(f"ffn{l}_down", a4, w_down4, x, gate)
    return x_new, (x, h, u24, a4, f), w_down4


def _ffn_bwd(l, dx, saved, mod, n2g, w_up8, cw24, w_down4, me):
    sh, sc, gate = mod
    x, h, u24, a4, f = saved
    df, dgate = _gate_bwd(f"ffn{l}_gate_bwd", dx, f, gate)
    da4 = _mm_rows_dx(f"ffn{l}_down_dx", df, w_down4, out_dtype=ACT_DTYPE)
    dw_down4 = _mm_rows_dw(f"ffn{l}_down_dw", a4, df, out_dtype=WIRE_DTYPE)
    du24, dcw24 = _ffn_gate_bwd(f"ffn{l}_act_bwd", u24, cw24, da4)
    du8 = du24.reshape((8,) + du24.shape[2:])
    dw_up8t = _mm_cols_dwt(f"ffn{l}_up_dw", h, du8, out_dtype=WIRE_DTYPE)
    sent, token = _exchange_start(f"scatter_ffn{l}", [dw_up8t, dw_down4.reshape(8, 352, dw_down4.shape[2])], True, dcw24, me)
    dh = _mm_cols_dx(f"ffn{l}_up_dx", du8, w_up8)
    dx_new, dn2g, dsc, dsh = _rmsmod_bwd(f"ffn{l}_norm_bwd", x, n2g, sc, dh, dx, token)
    return dx_new, dict(sent=sent, cw24=dcw24, n2g=dn2g, mod=(dsh, dsc, dgate))


def kernel(x, c, positions, ada_w, ada_b, norm1_g, norm2_g, ab_w_in, a_conv_w, b_mix_w, b_scale, ab_w_out, cd_w_in, c_q_norm_g, c_w_uq, c_kv_norm_g, c_w_ukv, d_ln_g, d_ln_b, d_w_s, d_b_s, cd_w_out, ffn_w_up, ffn_conv_w, ffn_w_down, final_norm_g, loss_target, m_ada_w, m_ada_b, m_norm1_g, m_norm2_g, m_ab_w_in, m_a_conv_w, m_b_mix_w, m_b_scale, m_ab_w_out, m_cd_w_in, m_c_q_norm_g, m_c_w_uq, m_c_kv_norm_g, m_c_w_ukv, m_d_ln_g, m_d_ln_b, m_d_w_s, m_d_b_s, m_cd_w_out, m_ffn_w_up, m_ffn_conv_w, m_ffn_w_down, m_final_norm_g, v_ada_w, v_ada_b, v_norm1_g, v_norm2_g, v_ab_w_in, v_a_conv_w, v_b_mix_w, v_b_scale, v_ab_w_out, v_cd_w_in, v_c_q_norm_g, v_c_w_uq, v_c_kv_norm_g, v_c_w_ukv, v_d_ln_g, v_d_ln_b, v_d_w_s, v_d_b_s, v_cd_w_out, v_ffn_w_up, v_ffn_conv_w, v_ffn_w_down, v_final_norm_g):
    S, D = x.shape[1], x.shape[2]
    me = 4 * lax.axis_index("x") + 2 * lax.axis_index("y") + lax.axis_index("c")
    x0, target = x[0], loss_target[0]
    W = _MXU_DTYPE

    small_shapes = [(1024,), (3, 64), (32,), (64,), (64,), (2, 3, 704)]
    (g0,) = _exchange("gather_small", [[_pack([c, a_conv_w, c_q_norm_g, d_ln_g, d_ln_b, ffn_conv_w])]], scatter=False)
    c_all, aconv_s, qg_s, lng_s, lnb_s, fcw_s = _unpack(g0[:, 0], small_shapes, lead=(N_DEV,))
    conv_w = aconv_s.transpose(1, 0, 2).reshape(3, 512)
    qg, ln_g, ln_b = qg_s.reshape(1, 256), lng_s.reshape(1, 512), lnb_s.reshape(1, 512)
    cw24 = [fcw_s[:, l].reshape(2, 4, 3, 704) for l in range(2)]
    c16 = jnp.pad(c_all, ((0, 16 - N_DEV), (0, 0)))

    mod_cols = _ada_fwd(c16, ada_w)
    (g1,) = _exchange("gather_mod", [[_pack([mod_cols])]], scatter=False)
    mod_all = _unpack(g1[:, 0], [(2, 16, 768)], lead=(N_DEV,))[0]
    mod_mine = lax.dynamic_index_in_dim(mod_all, me, axis=2, keepdims=False)
    mod = mod_mine.transpose(1, 0, 2).reshape(2, 6 * D) + ada_b
    mods = [[mod[l, k * D:(k + 1) * D].reshape(1, D) for k in range(6)] for l in range(2)]

    gw_ab, token = _exchange_start("gather_w_ab", [ab_w_in[0].astype(W), ab_w_out[0].astype(W)], False, mod, me)
    gw_up0, token = _exchange_start("gather_w_ffn0_up", [ffn_w_up[0].astype(W)], False, token, me)
    gw_dn0, token = _exchange_start("gather_w_ffn0_down", [ffn_w_down[0].astype(W)], False, token, me)
    gw_cd, token = _exchange_start("gather_w_cd", [
        cd_w_in[0].astype(W).reshape(1440, 128), c_w_uq[0].astype(W).reshape(192, 128), c_w_ukv[0].astype(W),
        cd_w_out[0].astype(W)], False, token, me)
    gw_up1, token = _exchange_start("gather_w_ffn1_up", [ffn_w_up[1].astype(W)], False, token, me)
    gw_dn1, started = _exchange_start("gather_w_ffn1_down", [ffn_w_down[1].astype(W)], False, token, me)

    cosq, cosk, sa, sb = _rope_tables(positions[0])
    n1g = [norm1_g[l].reshape(1, D) for l in range(2)]
    n2g = [norm2_g[l].reshape(1, D) for l in range(2)]
    mix_w, scale = b_mix_w[0], b_scale
    kvg = c_kv_norm_g
    w_s, b_st = d_w_s[0], d_b_s[0].T

    sh1, sc1, g1m = mods[0][:3]
    h_ab = _rmsmod_fwd("ab_norm", x0, n1g[0], sc1, sh1, started)
    w_abin8, w_about = _exchange_wait("wait_w_ab", gw_ab, h_ab)
    w_about2 = w_about.reshape(2, 512, D)
    z8 = _mm_cols("ab_in", h_ab, w_abin8)
    ycat_ab = _ab_mix_fwd(z8, conv_w, mix_w, scale)
    y_ab, x1 = _mm_rows_resid("ab_out", ycat_ab, w_about2, x0, g1m)
    w_up8, w_down4 = [None, None], [None, None]
    (w_up8[0],) = _exchange_wait("wait_w_ffn0_up", gw_up0, x1)
    x2, ffn0_saved, w_down4[0] = _ffn_fwd(0, x1, mods[0][3:], n2g[0], w_up8[0], cw24[0],
                                          lambda after: _exchange_wait("wait_w_ffn0_down", gw_dn0, after)[0].reshape(4, 704, D))

    w_cdin, w_uq, w_ukv, w_cdout = _exchange_wait("wait_w_cd", gw_cd, x2)
    w_cdout2 = w_cdout.reshape(2, 512, D)
    w_cd = w_cdin.reshape(8, D, 180).transpose(1, 0, 2).reshape(D, 1440)
    zc = lambda n: jnp.zeros((D, n), W)
    w_cd_pad = jnp.concatenate([w_cd[:, :384], zc(64), w_cd[:, 384:416], zc(32), w_cd[:, 416:]], axis=1)
    w_uq_pad = jnp.pad(w_uq.reshape(8, 256, 96).transpose(1, 0, 2), ((0, 0), (0, 0), (0, 32))).reshape(256, 1024)
    w_ukv_h = w_ukv.transpose(1, 0, 2)
    w_k_pad = jnp.pad(w_ukv_h[:, :, :64], ((0, 0), (0, 0), (0, 64))).reshape(128, 1024)
    w_kv_pad = jnp.concatenate([w_k_pad, w_ukv_h[:, :, 64:].reshape(128, 512)], axis=1)

    sh1, sc1, g1c = mods[1][:3]
    h_cd = _rmsmod_fwd("cd_norm", x2, n1g[1], sc1, sh1, n1g[1])
    z_cd = _mm_nn("cd_in", h_cd, w_cd_pad)
    qn, kvn = _mla_prep_fwd(z_cd, qg, kvg)
    qraw = _mm_nn("cd_uq", qn, w_uq_pad)
    kvall = _mm_nn("cd_ukv", kvn, w_kv_pad)
    q_r, k_r, v_r = _rope_fwd(qraw, kvall, z_cd, cosq, cosk, sa, sb)
    o, lse = _attn_fwd(q_r, k_r, v_r)
    ycat_cd = _sgu_fwd(z_cd, o, ln_g, ln_b, w_s, b_st)
    y_cd, x3 = _mm_rows_resid("cd_out", ycat_cd, w_cdout2, x2, g1c)
    (w_up8[1],) = _exchange_wait("wait_w_ffn1_up", gw_up1, x3)
    x4, ffn1_saved, w_down4[1] = _ffn_fwd(1, x3, mods[1][3:], n2g[1], w_up8[1], cw24[1],
                                          lambda after: _exchange_wait("wait_w_ffn1_down", gw_dn1, after)[0].reshape(4, 704, D))

    loss_local, dx4, dfg = _loss_head(x4, final_norm_g.reshape(1, D), target)

    dx3, gf1 = _ffn_bwd(1, dx4, ffn1_saved, mods[1][3:], n2g[1], w_up8[1], cw24[1], w_down4[1], me)

    dy, dg1c = _gate_bwd("cd_gate_bwd", dx3, y_cd, g1c)
    dycat = _mm_rows_dx("cd_out_dx", dy, w_cdout2)
    dw_cdout = _mm_rows_dw("cd_out_dw", ycat_cd, dy, out_dtype=WIRE_DTYPE)
    duv, dln_g, dln_b, dws, dbs = _sgu_bwd(z_cd, dycat, ln_g, ln_b, w_s, b_st)
    dq_r, dk_r, dv_r = _attn_bwd(q_r, k_r, v_r, o, lse, dycat)
    dqraw, dkvall, dkpe = _rope_bwd(dq_r, dk_r, dv_r, cosq, cosk, sa, sb)
    dqn = _mm_nt("cd_uq_dx", dqraw, w_uq_pad, tn=256)
    dkvn = _mm_nt("cd_ukv_dx", dkvall, w_kv_pad, tn=128)
    dw_uq_pad = _mm_tn("cd_uq_dw", qn, dqraw, tm=256)
    dw_kv_pad = _mm_tn("cd_ukv_dw", kvn, dkvall, tm=128)
    dz_cd, dqg, dkvg = _mla_prep_bwd(z_cd, qg, kvg, dqn, dkvn, dkpe, duv)
    dh_cd = _mm_nt("cd_in_dx", dz_cd, w_cd_pad)
    dw_cd_pad = _mm_tn("cd_in_dw", h_cd, dz_cd)
    dw_cd =jnp.concatenate([dw_cd_pad[:, :384], dw_cd_pad[:, 448:480], dw_cd_pad[:, 512:]], axis=1)
    dw_cd8 = dw_cd.reshape(D, 8, 180).transpose(1, 0, 2).reshape(8, 1440, 128).astype(WIRE_DTYPE)
    dw_uq8 = dw_uq_pad.reshape(256, 8, 128)[:, :, :96].transpose(1, 0, 2).reshape(8, 192, 128).astype(WIRE_DTYPE)
    dw_ukv8 = jnp.concatenate([dw_kv_pad[:, :1024].reshape(128, 8, 128)[:, :, :64], dw_kv_pad[:, 1024:].reshape(128, 8, 64)],
                              axis=2).transpose(1, 0, 2).astype(WIRE_DTYPE)
    sent_cd, token = _exchange_start("scatter_cd", [dw_cd8, dw_uq8, dw_ukv8, dw_cdout.reshape(8, 128, D)], True, dqg, me)
    dx2, dn1g_cd, dsc1_cd, dsh1_cd = _rmsmod_bwd("cd_norm_bwd", x2, n1g[1], sc1, dh_cd, dx3, token)

    dx1, gf0 = _ffn_bwd(0, dx2, ffn0_saved, mods[0][3:], n2g[0], w_up8[0], cw24[0], w_down4[0], me)

    dy, dg1m = _gate_bwd("ab_gate_bwd", dx1, y_ab, g1m)
    dycat = _mm_rows_dx("ab_out_dx", dy, w_about2)
    dw_about = _mm_rows_dw("ab_out_dw", ycat_ab, dy, out_dtype=WIRE_DTYPE)
    dz8, dconv_w, dmix_w, dscale = _ab_mix_bwd(z8, dycat, conv_w, mix_w, scale)
    dz8 = dz8.reshape(8, S, 256)
    dw_abin8 = _mm_cols_dw("ab_in_dw", h_ab, dz8, out_dtype=WIRE_DTYPE)
    sent_ab, token = _exchange_start("scatter_ab", [dw_abin8, dw_about.reshape(8, 128, D)], True, dscale, me)
    dh_ab = _mm_cols_dx("ab_in_dx", dz8, w_abin8)
    dx0, dn1g_ab, dsc1_ab, dsh1_ab = _rmsmod_bwd("ab_norm_bwd", x0, n1g[0], mods[0][1], dh_ab, dx1, token)

    dmod = jnp.stack([jnp.concatenate([dsh1_ab, dsc1_ab, dg1m, *gf0["mod"]], axis=1)[0],
                      jnp.concatenate([dsh1_cd, dsc1_cd, dg1c, *gf1["mod"]], axis=1)[0]])
    small_view = dict(ada_b=(2, 6 * D), norm1_g=(2, D), norm2_g=(2, D), b_mix_w=(512, 128), b_scale=(1, 512), c_kv_norm_g=(1, 128),
                      d_w_s=(512, 128), d_b_s=(4, 128), final_norm_g=(1, D),
                      a_conv_w=(3, 64), c_q_norm_g=(1, 32), d_ln_g=(1, 64), d_ln_b=(1, 64), ffn_conv_w=(2, 3, 704))
    small_names = list(small_view)
    small_grads = [dmod, jnp.concatenate([dn1g_ab, dn1g_cd]), jnp.concatenate([gf0["n2g"], gf1["n2g"]]),
                   dmix_w.reshape(512, 128), dscale, dkvg, dws.reshape(512, 128), dbs, dfg,
                   dconv_w.reshape(3, 8, 64).transpose(1, 0, 2), dqg.reshape(8, 1, 32), dln_g.reshape(8, 1, 64), dln_b.reshape(8, 1, 64),
                   jnp.stack([gf0["cw24"].reshape(8, 3, 704), gf1["cw24"].reshape(8, 3, 704)], axis=1)]
    small_sent, token = _exchange_start("gather_small_grads", small_grads, [False] * 9 + [True] * 5, dx0, me)

    res = {}

    def update(name, w, m, v, parts, shape3d):
        outs = _adamw("adamw_" + name, w.reshape(shape3d), m.reshape(shape3d), v.reshape(shape3d),
                      [p.reshape((p.shape[0],) + shape3d[1:]) for p in parts])
        res[name] = [o_.reshape(w.shape) for o_ in outs]

    p_up1, p_dn1 = _exchange_wait("wait_scatter_ffn1", gf1["sent"], token)
    p_up0, p_dn0 = _exchange_wait("wait_scatter_ffn0", gf0["sent"], token)
    swap = lambda a: jnp.swapaxes(a, 1, 2)
    update("ffn_w_up", swap(ffn_w_up), swap(m_ffn_w_up), swap(v_ffn_w_up), [p_up0, p_up1], (2, 704, D))
    res["ffn_w_up"] = [swap(o_) for o_ in res["ffn_w_up"]]
    update("ffn_w_down", ffn_w_down, m_ffn_w_down, v_ffn_w_down, [p_dn0, p_dn1], (2, 352, D))
    p_cdin, p_uq, p_ukv, p_cdout = _exchange_wait("wait_scatter_cd", sent_cd, res["ffn_w_down"][0])
    update("cd_w_in", cd_w_in, m_cd_w_in, v_cd_w_in, [p_cdin], (1, 1440, 128))
    update("c_w_uq", c_w_uq, m_c_w_uq, v_c_w_uq, [p_uq], (1, 192, 128))
    update("c_w_ukv", c_w_ukv, m_c_w_ukv, v_c_w_ukv, [p_ukv], (1, 128, 128))
    update("cd_w_out", cd_w_out, m_cd_w_out, v_cd_w_out, [p_cdout], (1, 128, D))
    p_abin, p_about = _exchange_wait("wait_scatter_ab", sent_ab, res["cd_w_out"][0])
    update("ab_w_in", ab_w_in, m_ab_w_in, v_ab_w_in, [p_abin], (1, D, 256))
    update("ab_w_out", ab_w_out, m_ab_w_out, v_ab_w_out, [p_about], (1, 128, D))

    small_parts = _exchange_wait("wait_small_grads", small_sent, res["ab_w_out"][0])
    dmod_all = small_parts[0]
    dmod_cols = lax.dynamic_slice_in_dim(dmod_all, me * 768, 768, axis=2).transpose(1, 0, 2)
    g_ada_w = _ada_bwd(c16, jnp.pad(dmod_cols, ((0, 0), (0, 16 - N_DEV), (0, 0))))
    update("ada_w", ada_w, m_ada_w, v_ada_w, [g_ada_w[l][None] for l in range(2)], (2, D, 768))

    small_w = dict(ada_b=(ada_b, m_ada_b, v_ada_b), norm1_g=(norm1_g, m_norm1_g, v_norm1_g), norm2_g=(norm2_g, m_norm2_g, v_norm2_g),
                   b_mix_w=(b_mix_w, m_b_mix_w, v_b_mix_w), b_scale=(b_scale, m_b_scale, v_b_scale),
                   c_kv_norm_g=(c_kv_norm_g, m_c_kv_norm_g, v_c_kv_norm_g), d_w_s=(d_w_s, m_d_w_s, v_d_w_s),
                   d_b_s=(d_b_s, m_d_b_s, v_d_b_s), final_norm_g=(final_norm_g, m_final_norm_g, v_final_norm_g),
                   a_conv_w=(a_conv_w, m_a_conv_w, v_a_conv_w), c_q_norm_g=(c_q_norm_g, m_c_q_norm_g, v_c_q_norm_g),
                   d_ln_g=(d_ln_g, m_d_ln_g, v_d_ln_g), d_ln_b=(d_ln_b, m_d_ln_b, v_d_ln_b),
                   ffn_conv_w=(ffn_conv_w, m_ffn_conv_w, v_ffn_conv_w))
    small_out = _adamw_small("adamw_small", [tuple(a.reshape(small_view[n]) for a in small_w[n]) for n in small_names],
                             list(small_parts))
    for n, outs in zip(small_names, small_out):
        res[n] = [o_.reshape(small_w[n][0].shape) for o_ in outs]

    loss = lax.psum(loss_local[0, 0], ("x", "y", "c"))
    order = ["ada_w", "ada_b", "norm1_g", "norm2_g", "ab_w_in", "a_conv_w", "b_mix_w", "b_scale", "ab_w_out", "cd_w_in", "c_q_norm_g",
             "c_w_uq", "c_kv_norm_g", "c_w_ukv", "d_ln_g", "d_ln_b", "d_w_s", "d_b_s", "cd_w_out", "ffn_w_up", "ffn_conv_w",
             "ffn_w_down", "final_norm_g"]
    return (loss, dx0[None], *[res[n][0] for n in order], *[res[n][1] for n in order], *[res[n][2] for n in order],
            *[res[n][3] for n in order])
```

```python
import functools
import math

import jax
import jax.numpy as jnp
from jax import lax
from jax.experimental import pallas as pl
from jax.experimental.pallas import tpu as pltpu

F32 = jnp.float32
BF16 = jnp.bfloat16
_MXU_DTYPE = BF16
WIRE_DTYPE = BF16
ACT_DTYPE = BF16
_VMEM_LIMIT = 56 * 2 ** 20
N_DEV = 8
EPS = 1e-6
POOL_WINDOWS = (2, 4, 8, 16)
ATTN_SCALE = (64 + 32) ** -0.5
ADAM_LR, ADAM_B1, ADAM_B2, ADAM_EPS, ADAM_WD, ADAM_STEP = 0.001, 0.9, 0.999, 1e-08, 0.01, 10
MESH = pl.DeviceIdType.MESH
ANY = pl.BlockSpec(memory_space=pl.ANY)


def _cp(*sem):
    return pltpu.CompilerParams(dimension_semantics=sem, vmem_limit_bytes=_VMEM_LIMIT)


def _dot(a, b, contract):
    dn = {"nn": (((1,), (0,)), ((), ())), "nt": (((1,), (1,)), ((), ())), "tn": (((0,), (0,)), ((), ()))}[contract]
    return lax.dot_general(a.astype(_MXU_DTYPE), b.astype(_MXU_DTYPE), dn, preferred_element_type=F32)


def _my_position():
    x, y, c = lax.axis_index("x"), lax.axis_index("y"), lax.axis_index("c")
    return x, y, c, 4 * x + 2 * y + c


def _exchange(name, groups, scatter):
    flat = [a for g in groups for a in g]
    n_in, n_grp = len(flat), len(groups)
    out_shapes = []
    for g in groups:
        slab = g[0].shape[1:] if scatter else g[0].shape
        out_shapes.append(jax.ShapeDtypeStruct((N_DEV, len(g)) + tuple(slab), g[0].dtype))

    def body(*refs):
        ins, outs = refs[:n_in], refs[n_in:n_in + n_grp]
        send_sems, recv_sems, local_sems = refs[n_in + n_grp:]
        x, y, c, me = _my_position()
        i = 0
        for gi, g in enumerate(groups):
            for l in range(len(g)):
                src = ins[i]
                i += 1
                pltpu.make_async_copy(src.at[me] if scatter else src, outs[gi].at[me, l], local_sems.at[gi]).start()
                for k in range(1, N_DEV):
                    px = 1 - x if k & 4 else x
                    py = 1 - y if k & 2 else y
                    pc = 1 - c if k & 1 else c
                    peer = 4 * px + 2 * py + pc
                    pltpu.make_async_remote_copy(
                        src_ref=src.at[peer] if scatter else src, dst_ref=outs[gi].at[me, l],
                        send_sem=send_sems.at[gi], recv_sem=recv_sems.at[gi],
                        device_id=(px, py, pc), device_id_type=MESH).start()
        for gi in range(n_grp):
            mine = outs[gi].at[me]
            pltpu.make_async_copy(mine, mine, local_sems.at[gi]).wait()
            seven = outs[gi].at[pl.ds(0, N_DEV - 1)]
            w = pltpu.make_async_remote_copy(src_ref=seven, dst_ref=seven, send_sem=send_sems.at[gi],
                                             recv_sem=recv_sems.at[gi], device_id=(x, y, c), device_id_type=MESH)
            w.wait_send()
            w.wait_recv()

    return pl.pallas_call(
        body, name=name, out_shape=tuple(out_shapes),
        in_specs=[ANY] * n_in, out_specs=tuple([ANY] * n_grp),
        scratch_shapes=[pltpu.SemaphoreType.DMA((n_grp,)), pltpu.SemaphoreType.DMA((n_grp,)),
                        pltpu.SemaphoreType.DMA((n_grp,))],
        compiler_params=pltpu.CompilerParams(has_side_effects=True),
    )(*flat)


HBM_SPEC = pl.BlockSpec(memory_space=pltpu.HBM)
SEM_SPEC = pl.BlockSpec(memory_space=pltpu.SEMAPHORE)
EFFECT = pltpu.SideEffectType.DATAFLOW_SIDE_EFFECTING


def _put_mine(name, srcs, scatter, me):
    n = len(srcs)
    slabs = [tuple(s.shape[1:] if sc else s.shape) for s, sc in zip(srcs, scatter)]

    def body(me_ref, *refs):
        for i in range(n):
            refs[n + i][...] = refs[i][...]

    def at_me(slab):
        return pl.BlockSpec((None,) + slab, lambda g, me_ref, nd=len(slab): (me_ref[0],) + (0,) * nd)

    def whole(slab):
        return pl.BlockSpec(slab, lambda g, me_ref, nd=len(slab): (0,) * nd)

    return pl.pallas_call(
        body, name=name,
        grid_spec=pltpu.PrefetchScalarGridSpec(
            num_scalar_prefetch=1, grid=(1,),
            in_specs=[at_me(slab) if sc else whole(slab) for slab, sc in zip(slabs, scatter)],
            out_specs=[at_me(slab) for slab in slabs]),
        out_shape=[jax.ShapeDtypeStruct((N_DEV,) + slab, s.dtype) for slab, s in zip(slabs, srcs)],
        compiler_params=_cp("arbitrary"))(me.reshape(1), *srcs)


def _exchange_start(name, srcs, scatter, after, me):
    n = len(srcs)
    scatter = list(scatter) if isinstance(scatter, (list, tuple)) else [scatter] * n
    lands = _put_mine(name + "_mine", srcs, scatter, me)
    srcs = [pltpu.with_memory_space_constraint(a, pltpu.HBM) for a in srcs]
    lands = [pltpu.with_memory_space_constraint(a, pltpu.HBM) for a in lands]

    def body(*refs):
        ins, land = refs[:n], refs[n:2 * n]
        send_sems, recv_sems, token = refs[2 * n + 1], refs[2 * n + 2], refs[-1]
        x, y, c, me_in = _my_position()
        for i in range(n):
            for k in range(1, N_DEV):
                px = 1 - x if k & 4 else x
                py = 1 - y if k & 2 else y
                pc = 1 - c if k & 1 else c
                pltpu.make_async_remote_copy(
                    src_ref=ins[i].at[4 * px + 2 * py + pc] if scatter[i] else ins[i], dst_ref=land[i].at[me_in],
                    send_sem=send_sems.at[i], recv_sem=recv_sems.at[i],
                    device_id=(px, py, pc), device_id_type=MESH).start()
        token[...] = jnp.zeros_like(token)

    outs = pl.pallas_call(
        body, name=name,
        out_shape=(pltpu.SemaphoreType.DMA((n,)), pltpu.SemaphoreType.DMA((n,)),
                   *[pltpu.HBM(a.shape, a.dtype) for a in srcs], *[pltpu.HBM(a.shape, a.dtype) for a in lands],
                   jax.ShapeDtypeStruct((8, 128), F32)),
        in_specs=[HBM_SPEC] * (2 * n) + [ANY],
        out_specs=(SEM_SPEC, SEM_SPEC, *[HBM_SPEC] * (2 * n), pl.BlockSpec(memory_space=pltpu.VMEM)),
        input_output_aliases={i: 2 + i for i in range(2 * n)},
        compiler_params=pltpu.CompilerParams(has_side_effects=EFFECT),
    )(*srcs, *lands, after)
    return (outs[0], outs[1], outs[2:2 + n], outs[2 + n:2 + 2 * n]), outs[-1]


def _exchange_wait(name, handle, after):
    send_sems, recv_sems, srcs, lands = handle
    n = len(srcs)

    def body(*refs):
        land, send_ref, recv_ref = refs[n:2 * n], refs[2 * n], refs[2 * n + 1]
        x, y, c, _ = _my_position()
        for i in range(n):
            seven = land[i].at[pl.ds(0, N_DEV - 1)]
            w = pltpu.make_async_remote_copy(src_ref=seven, dst_ref=seven, send_sem=send_ref.at[i], recv_sem=recv_ref.at[i],
                                             device_id=(x, y, c), device_id_type=MESH)
            w.wait_send()
            w.wait_recv()

    outs = pl.pallas_call(
        body, name=name,
        out_shape=(*[pltpu.HBM(a.shape, a.dtype) for a in srcs], *[pltpu.HBM(a.shape, a.dtype) for a in lands]),
        in_specs=[HBM_SPEC] * (2 * n) + [SEM_SPEC, SEM_SPEC, ANY],
        out_specs=tuple([HBM_SPEC] * (2 * n)),
        input_output_aliases={i: i for i in range(2 * n)},
        compiler_params=pltpu.CompilerParams(has_side_effects=EFFECT),
    )(*srcs, *lands, send_sems, recv_sems, after)
    return outs[n:]


def _pack(arrs):
    flat = jnp.concatenate([a.reshape(-1).astype(F32) for a in arrs])
    n = flat.shape[0]
    rows = -(-n // 1024) * 8
    return jnp.pad(flat, (0, rows * 128 - n)).reshape(rows, 128)


def _unpack(buf, shapes, lead=()):
    flat = buf.reshape(lead + (-1,))
    out, off = [], 0
    for s in shapes:
        n = math.prod(s)
        out.append(flat[..., off:off + n].reshape(lead + tuple(s)))
        off += n
    return out


def _mm(name, a, a_spec, b, b_spec, out_sds, o_spec, grid, contract, nk=1, stacked=0):
    o_blk = tuple(d for d in o_spec.block_shape if d is not None)

    def body(a_ref, b_ref, o_ref, *acc):
        if stacked:
            r = _dot(a_ref[0], b_ref[0], contract)
            for q in range(1, stacked):
                r = r + _dot(a_ref[q], b_ref[q], contract)
        else:
            r = _dot(a_ref[...], b_ref[...], contract)
        if nk == 1:
            o_ref[...] = r.astype(o_ref.dtype)
        else:
            k = pl.program_id(len(grid) - 1)

            @pl.when(k == 0)
            def _():
                acc[0][...] = r

            @pl.when(k > 0)
            def _():
                acc[0][...] += r

            @pl.when(k == nk - 1)
            def _():
                o_ref[...] = acc[0][...].astype(o_ref.dtype)

    sem = ("parallel",) * (len(grid) - 1) + (("arbitrary",) if nk > 1 else ("parallel",))
    return pl.pallas_call(
        body, name=name, out_shape=out_sds, grid=grid, in_specs=[a_spec, b_spec], out_specs=o_spec,
        scratch_shapes=[pltpu.VMEM(o_blk, F32)] if nk > 1 else [], compiler_params=_cp(*sem))(a, b)


def _tile(n, want):
    t = min(n, want)
    assert n % t == 0, (n, t)
    return t


def _mm_nn(name, a, b, out_dtype=F32, tm=512, tn=512):
    (M, K), N = a.shape, b.shape[1]
    tm, tn = _tile(M, tm), _tile(N, tn)
    return _mm(name, a, pl.BlockSpec((tm, K), lambda i, j: (i, 0)), b, pl.BlockSpec((K, tn), lambda i, j: (0, j)),
               jax.ShapeDtypeStruct((M, N), out_dtype), pl.BlockSpec((tm, tn), lambda i, j: (i, j)),
               (M // tm, N // tn), "nn")


def _mm_nt(name, a, b, out_dtype=F32, tm=512, tn=512):
    (M, K), N = a.shape, b.shape[0]
    tm, tn = _tile(M, tm), _tile(N, tn)
    return _mm(name, a, pl.BlockSpec((tm, K), lambda i, j: (i, 0)), b, pl.BlockSpec((tn, K), lambda i, j: (j, 0)),
               jax.ShapeDtypeStruct((M, N), out_dtype), pl.BlockSpec((tm, tn), lambda i, j: (i, j)),
               (M // tm, N // tn), "nt")


def _mm_tn(name, a, b, out_dtype=F32, tm=512, tn=512):
    (K, M), N = a.shape, b.shape[1]
    tm, tn = _tile(M, tm), _tile(N, tn)
    return _mm(name, a, pl.BlockSpec((K, tm), lambda i, j: (0, i)), b, pl.BlockSpec((K, tn), lambda i, j: (0, j)),
               jax.ShapeDtypeStruct((M, N), out_dtype), pl.BlockSpec((tm, tn), lambda i, j: (i, j)),
               (M // tm, N // tn), "tn")


def _mm_cols(name, a, w, out_dtype=F32, tm=512):
    (M, K), (J, _, n) = a.shape, w.shape
    tm = _tile(M, tm)
    return _mm(name, a, pl.BlockSpec((tm, K), lambda j, i: (i, 0)), w, pl.BlockSpec((None, K, n), lambda j, i: (j, 0, 0)),
               jax.ShapeDtypeStruct((J, M, n), out_dtype), pl.BlockSpec((None, tm, n), lambda j, i: (j, i, 0)),
               (J, M // tm), "nn")


def _mm_cols_dx(name, d, w, out_dtype=F32, tm=512, jb=None):
    (J, M, n), K = d.shape, w.shape[1]
    tm, jb = _tile(M, tm), J if jb is None else jb
    return _mm(name, d, pl.BlockSpec((jb, tm, n), lambda i, j: (j, i, 0)), w, pl.BlockSpec((jb, K, n), lambda i, j: (j, 0, 0)),
               jax.ShapeDtypeStruct((M, K), out_dtype), pl.BlockSpec((tm, K), lambda i, j: (i, 0)),
               (M // tm, J // jb), "nt", nk=J // jb, stacked=jb)


def _mm_cols_dw(name, a, d, out_dtype=F32, tk=512):
    (M, K), (J, _, n) = a.shape, d.shape
    tk = _tile(K, tk)
    return _mm(name, a, pl.BlockSpec((M, tk), lambda j, i: (0, i)), d, pl.BlockSpec((None, M, n), lambda j, i: (j, 0, 0)),
               jax.ShapeDtypeStruct((J, K, n), out_dtype), pl.BlockSpec((None, tk, n), lambda j, i: (j, i, 0)),
               (J, K // tk), "tn")


def _mm_cols_dwt(name, a, d, out_dtype=F32, tk=512):
    (M, K), (J, _, n) = a.shape, d.shape
    tk = _tile(K, tk)
    return _mm(name, d, pl.BlockSpec((None, M, n), lambda j, i: (j, 0, 0)), a, pl.BlockSpec((M, tk), lambda j, i: (0, i)),
               jax.ShapeDtypeStruct((J, n, K), out_dtype), pl.BlockSpec((None, n, tk), lambda j, i: (j, 0, i)),
               (J, K // tk), "tn")


def _mm_rows_resid(name, a, w, resid, gate, tm=512):
    (Q, M, k), N = a.shape, w.shape[2]
    tm = _tile(M, tm)

    def body(a_ref, w_ref, r_ref, g_ref, y_ref, x_ref):
        y = _dot(a_ref[0], w_ref[0], "nn")
        for q in range(1, Q):
            y = y + _dot(a_ref[q], w_ref[q], "nn")
        y_ref[...] = y
        x_ref[...] = r_ref[...] + g_ref[...] * y

    return pl.pallas_call(
        body, name=name, grid=(M // tm,),
        out_shape=(jax.ShapeDtypeStruct((M, N), F32), jax.ShapeDtypeStruct((M, N), F32)),
        in_specs=[pl.BlockSpec((Q, tm, k), lambda i: (0, i, 0)), pl.BlockSpec((Q, k, N), lambda i: (0, 0, 0)),
                  pl.BlockSpec((tm, N), lambda i: (i, 0)), pl.BlockSpec((1, N), lambda i: (0, 0))],
        out_specs=(pl.BlockSpec((tm, N), lambda i: (i, 0)), pl.BlockSpec((tm, N), lambda i: (i, 0))),
        compiler_params=_cp("parallel"))(a, w, resid, gate)


def _mm_rows_dx(name, d, w, out_dtype=F32, tm=512):
    (M, N), (Q, k, _) = d.shape, w.shape
    tm = _tile(M, tm)
    return _mm(name, d, pl.BlockSpec((tm, N), lambda q, i: (i, 0)), w, pl.BlockSpec((None, k, N), lambda q, i: (q, 0, 0)),
               jax.ShapeDtypeStruct((Q, M, k), out_dtype), pl.BlockSpec((None, tm, k), lambda q, i: (q, i, 0)),
               (Q, M // tm), "nt")


def _mm_rows_dw(name, a, d, out_dtype=F32, tn=512):
    (Q, M, k), N = a.shape, d.shape[1]
    tn = _tile(N, tn)
    return _mm(name, a, pl.BlockSpec((None, M, k), lambda q, j: (q, 0, 0)), d, pl.BlockSpec((M, tn), lambda q, j: (0, j)),
               jax.ShapeDtypeStruct((Q, k, N), out_dtype), pl.BlockSpec((None, k, tn), lambda q, j: (q, 0, j)),
               (Q, N // tn), "tn")


def _silu(v):
    return v * jax.nn.sigmoid(v)


def _ada_fwd(c16, ada_w):
    L, D, n = ada_w.shape

    def body(c_ref, w_ref, o_ref):
        o_ref[...] = _dot(_silu(c_ref[...]), w_ref[...], "nn")

    return pl.pallas_call(
        body, name="ada_fwd", grid=(L,), out_shape=jax.ShapeDtypeStruct((L, 16, n), F32),
        in_specs=[pl.BlockSpec((16, D), lambda l: (0, 0)), pl.BlockSpec((None, D, n), lambda l: (l, 0, 0))],
        out_specs=pl.BlockSpec((None, 16, n), lambda l: (l, 0, 0)), compiler_params=_cp("parallel"))(c16, ada_w)


def _ada_bwd(c16, dmod16):
    L, _, n = dmod16.shape
    D = c16.shape[1]

    def body(c_ref, d_ref, o_ref):
        o_ref[...] = _dot(_silu(c_ref[...]), d_ref[...], "tn")

    return pl.pallas_call(
        body, name="ada_bwd", grid=(L,), out_shape=jax.ShapeDtypeStruct((L, D, n), F32),
        in_specs=[pl.BlockSpec((16, D), lambda l: (0, 0)), pl.BlockSpec((None, 16, n), lambda l: (l, 0, 0))],
        out_specs=pl.BlockSpec((None, D, n), lambda l: (l, 0, 0)), compiler_params=_cp("parallel"))(c16, dmod16)


def _row_spec(tr, n):
    return pl.BlockSpec((tr, n), lambda i: (i, 0))


def _vec_spec(n):
    return pl.BlockSpec((1, n), lambda i: (0, 0))


def _rmsmod_fwd(name, x, g, sc, sh, after, tr=256):
    S, D = x.shape

    def body(x_ref, g_ref, sc_ref, sh_ref, after_ref, h_ref):
        xv = x_ref[...]
        rstd = lax.rsqrt(jnp.mean(xv * xv, axis=-1, keepdims=True) + EPS)
        y = xv * rstd * g_ref[...]
        h_ref[...] = (y * (1.0 + sc_ref[...]) + sh_ref[...]).astype(h_ref.dtype)

    return pl.pallas_call(
        body, name=name, grid=(S // tr,), out_shape=jax.ShapeDtypeStruct((S, D), _MXU_DTYPE),
        in_specs=[_row_spec(tr, D), _vec_spec(D), _vec_spec(D), _vec_spec(D), ANY], out_specs=_row_spec(tr, D),
        compiler_params=_cp("parallel"))(x, g, sc, sh, after)


def _acc_rows(ref, val, first):
    s = jnp.sum(val, axis=0, keepdims=True)

    @pl.when(first)
    def _():
        ref[...] = s

    @pl.when(jnp.logical_not(first))
    def _():
        ref[...] += s


def _rmsmod_bwd(name, x, g, sc, dh, dres, after, tr=256):
    S, D = x.shape

    def body(x_ref, g_ref, sc_ref, dh_ref, dres_ref, after_ref, dx_ref, dg_ref, dsc_ref, dsh_ref):
        first = pl.program_id(0) == 0
        xv, dh_v, gv = x_ref[...], dh_ref[...], g_ref[...]
        rstd = lax.rsqrt(jnp.mean(xv * xv, axis=-1, keepdims=True) + EPS)
        xhat = xv * rstd
        _acc_rows(dsh_ref, dh_v, first)
        _acc_rows(dsc_ref, dh_v * (xhat * gv), first)
        dyg = dh_v * (1.0 + sc_ref[...])
        _acc_rows(dg_ref, dyg * xhat, first)
        dxhat = dyg * gv
        dx_ref[...] = dres_ref[...] + rstd * (dxhat - xhat * jnp.mean(dxhat * xhat, axis=-1, keepdims=True))

    vec = jax.ShapeDtypeStruct((1, D), F32)
    return pl.pallas_call(
        body, name=name, grid=(S // tr,), out_shape=(jax.ShapeDtypeStruct((S, D), F32), vec, vec, vec),
        in_specs=[_row_spec(tr, D), _vec_spec(D), _vec_spec(D), _row_spec(tr, D), _row_spec(tr, D), ANY],
        out_specs=(_row_spec(tr, D), _vec_spec(D), _vec_spec(D), _vec_spec(D)),
        compiler_params=_cp("arbitrary"))(x, g, sc, dh, dres, after)


def _loss_head(x, g, target, tr=256):
    S, D = x.shape

    def body(x_ref, g_ref, t_ref, loss_ref, dx_ref, dg_ref):
        first = pl.program_id(0) == 0
        xv, gv = x_ref[...], g_ref[...]
        rstd = lax.rsqrt(jnp.mean(xv * xv, axis=-1, keepdims=True) + EPS)
        xhat = xv * rstd
        err = xhat * gv - t_ref[...]
        part = 0.5 * jnp.sum(jnp.mean(err * err, axis=-1, keepdims=True), axis=0, keepdims=True)

        @pl.when(first)
        def _():
            loss_ref[...] = part

        @pl.when(jnp.logical_not(first))
        def _():
            loss_ref[...] += part

        dout = err * (1.0 / D)
        _acc_rows(dg_ref, dout * xhat, first)
        dxhat = dout * gv
        dx_ref[...] = rstd * (dxhat - xhat * jnp.mean(dxhat * xhat, axis=-1, keepdims=True))

    return pl.pallas_call(
        body, name="loss_head", grid=(S // tr,),
        out_shape=(jax.ShapeDtypeStruct((1, 1), F32), jax.ShapeDtypeStruct((S, D), F32), jax.ShapeDtypeStruct((1, D), F32)),
        in_specs=[_row_spec(tr, D), _vec_spec(D), _row_spec(tr, D)],
        out_specs=(pl.BlockSpec((1, 1), lambda i: (0, 0)), _row_spec(tr, D), _vec_spec(D)),
        compiler_params=_cp("arbitrary"))(x, g, target)


def _gate_bwd(name, dx, y, gate, tr=256):
    S, D = dx.shape

    def body(dx_ref, y_ref, g_ref, dy_ref, dg_ref):
        dxv = dx_ref[...]
        dy_ref[...] = (g_ref[...] * dxv).astype(dy_ref.dtype)
        _acc_rows(dg_ref, dxv * y_ref[...], pl.program_id(0) == 0)

    return pl.pallas_call(
        body, name=name, grid=(S // tr,),
        out_shape=(jax.ShapeDtypeStruct((S, D), _MXU_DTYPE), jax.ShapeDtypeStruct((1, D), F32)),
        in_specs=[_row_spec(tr, D), _row_spec(tr, D), _vec_spec(D)], out_specs=(_row_spec(tr, D), _vec_spec(D)),
        compiler_params=_cp("arbitrary"))(dx, y, gate)


def _shift_down(v, k):
    t = lax.broadcasted_iota(jnp.int32, v.shape, 0)
    return jnp.where(t >= k, pltpu.roll(v, k, axis=0), 0.0)


def _shift_up(v, k):
    n = v.shape[0]
    t = lax.broadcasted_iota(jnp.int32, v.shape, 0)
    return jnp.where(t < n - k, pltpu.roll(v, n - k, axis=0), 0.0)


def _window_sum(p, w, shift):
    s, k = p, 1
    while k < w:
        s = s + shift(s, k)
        k *= 2
    return s


def _pool_count(shape, w):
    t = lax.broadcasted_iota(jnp.int32, shape, 0)
    return jnp.minimum(t + 1, w).astype(F32)


def _ab_specs(S):
    zs = [pl.BlockSpec((None, S, 128), functools.partial(lambda g, q: (2 * q + g // 2, 0, g % 2), q=q)) for q in range(4)]
    return zs


def _ab_mix_fwd(z8, conv_w, mix_w, scale):
    S = z8.shape[1]

    def body(b_ref, c_ref, a_ref, p_ref, w_ref, mix_ref, sc_ref, y_ref):
        g = pl.program_id(0)
        cg = c_ref[...] * a_ref[...]
        w = w_ref[...]
        conv = w[0:1] * _shift_down(cg, 2) + w[1:2] * _shift_down(cg, 1) + w[2:3] * cg
        y_ref[0] = (b_ref[...] * conv).astype(y_ref.dtype)
        for gg, win in enumerate(POOL_WINDOWS):
            @pl.when(g == gg)
            def _(win=win):
                p = p_ref[...]
                pooled = _window_sum(p, win, _shift_down) / _pool_count(p.shape, win) - p
                y_ref[1] = (_dot(pooled, mix_ref[...], "nn") * sc_ref[...]).astype(y_ref.dtype)

    return pl.pallas_call(
        body, name="ab_mix_fwd", grid=(4,), out_shape=jax.ShapeDtypeStruct((2, S, 512), _MXU_DTYPE),
        in_specs=_ab_specs(S) + [pl.BlockSpec((3, 128), lambda g: (0, g)), pl.BlockSpec((None, 128, 128), lambda g: (g, 0, 0)),
                                 pl.BlockSpec((1, 128), lambda g: (0, g))],
        out_specs=pl.BlockSpec((2, S, 128), lambda g: (0, 0, g)), compiler_params=_cp("parallel"))(z8, z8, z8, z8, conv_w, mix_w, scale)


def _ab_mix_bwd(z8, dycat2, conv_w, mix_w, scale):
    S = z8.shape[1]

    def body(b_ref, c_ref, a_ref, p_ref, dy_ref, w_ref, mix_ref, sc_ref, dz_ref, dw_ref, dmix_ref, dsc_ref):
        g = pl.program_id(0)
        bv, cv, av, w = b_ref[...], c_ref[...], a_ref[...], w_ref[...]
        dya = dy_ref[0]
        cg = cv * av
        cg1, cg2 = _shift_down(cg, 1), _shift_down(cg, 2)
        conv = w[0:1] * cg2 + w[1:2] * cg1 + w[2:3] * cg
        dz_ref[0] = (dya * conv).astype(dz_ref.dtype)
        dconv = dya * bv
        dcg = w[2:3] * dconv + w[1:2] * _shift_up(dconv, 1) + w[0:1] * _shift_up(dconv, 2)
        dz_ref[1] = (dcg * av).astype(dz_ref.dtype)
        dz_ref[2] = (dcg * cv).astype(dz_ref.dtype)
        dw_ref[0:1, :] = jnp.sum(dconv * cg2, axis=0, keepdims=True)
        dw_ref[1:2, :] = jnp.sum(dconv * cg1, axis=0, keepdims=True)
        dw_ref[2:3, :] = jnp.sum(dconv * cg, axis=0, keepdims=True)
        for gg, win in enumerate(POOL_WINDOWS):
            @pl.when(g == gg)
            def _(win=win):
                p, dyb, mix = p_ref[...], dy_ref[1], mix_ref[...]
                cnt = _pool_count(p.shape, win)
                pooled = _window_sum(p, win, _shift_down) / cnt - p
                dsc_ref[...] = jnp.sum(dyb * _dot(pooled, mix, "nn"), axis=0, keepdims=True)
                dmixed = dyb * sc_ref[...]
                dmix_ref[...] = _dot(pooled, dmixed, "tn")
                dpooled = _dot(dmixed, mix, "nt")
                dz_ref[3] = (_window_sum(dpooled / cnt, win, _shift_up) - dpooled).astype(dz_ref.dtype)

    return pl.pallas_call(
        body, name="ab_mix_bwd", grid=(4,),
        out_shape=(jax.ShapeDtypeStruct((4, 2, S, 256), _MXU_DTYPE), jax.ShapeDtypeStruct((3, 512), F32),
                   jax.ShapeDtypeStruct((4, 128, 128), F32), jax.ShapeDtypeStruct((1, 512), F32)),
        in_specs=_ab_specs(S) + [pl.BlockSpec((2, S, 128), lambda g: (0, 0, g)), pl.BlockSpec((3, 128), lambda g: (0, g)),
                                 pl.BlockSpec((None, 128, 128), lambda g: (g, 0, 0)), pl.BlockSpec((1, 128), lambda g: (0, g))],
        out_specs=(pl.BlockSpec((4, None, S, 128), lambda g: (0, g // 2, 0, g % 2)), pl.BlockSpec((3, 128), lambda g: (0, g)),
                   pl.BlockSpec((None, 128, 128), lambda g: (g, 0, 0)), pl.BlockSpec((1, 128), lambda g: (0, g))),
        compiler_params=_cp("parallel"))(z8, z8, z8, z8, dycat2, conv_w, mix_w, scale)


HALO = 16


def _ffn_specs(S, n, tr):
    nb = S // HALO
    tile = pl.BlockSpec((2, None, tr, n), lambda j, i: (0, j, i, 0))
    prev = pl.BlockSpec((2, None, HALO, n), lambda j, i: (0, j, jnp.maximum(i * (tr // HALO) - 1, 0), 0))
    nxt = pl.BlockSpec((2, None, HALO, n), lambda j, i: (0, j, jnp.minimum((i + 1) * (tr // HALO), nb - 1), 0))
    cw = pl.BlockSpec((2, None, 3, n), lambda j, i: (0, j, 0, 0))
    return tile, prev, nxt, cw


def _conv_rows(ext, w, lo, tr):
    n = ext.shape[0]
    return (w[0:1] * pltpu.roll(ext, 2, axis=0)[lo:lo + tr] + w[1:2] * pltpu.roll(ext, 1, axis=0)[lo:lo + tr]
            + w[2:3] * ext[lo:lo + tr])


def _ffn_gate_fwd(name, u24, cw24, tr=256):
    _, J, S, n = u24.shape
    tile, prev, _, cw = _ffn_specs(S, n, tr)

    def body(u_ref, up_ref, w_ref, a_ref):
        keep = (pl.program_id(1) > 0).astype(F32)
        z = []
        for h in range(2):
            ext = jnp.concatenate([up_ref[h].astype(F32) * keep, u_ref[h].astype(F32)], axis=0)
            z.append(_conv_rows(ext, w_ref[h], HALO, tr))
        a_ref[...] = (_silu(z[0]) * z[1]).astype(a_ref.dtype)

    return pl.pallas_call(
        body, name=name, grid=(J, S // tr), out_shape=jax.ShapeDtypeStruct((J, S, n), _MXU_DTYPE),
        in_specs=[tile, prev, cw], out_specs=pl.BlockSpec((None, tr, n), lambda j, i: (j, i, 0)),
        compiler_params=_cp("parallel", "parallel"))(u24, u24, cw24)


def _ffn_gate_bwd(name, u24, cw24, da4, tr=256):
    _, J, S, n = u24.shape
    tile, prev, nxt, cw = _ffn_specs(S, n, tr)
    nb = S // HALO
    ext_rows = tr + 2 * HALO

    def body(u_ref, up_ref, un_ref, w_ref, da_ref, dan_ref, du_ref, dcw_ref):
        i = pl.program_id(1)
        first = i == 0
        keep_prev = (i > 0).astype(F32)
        keep_next = (i < S // tr - 1).astype(F32)
        ext = [jnp.concatenate([up_ref[h].astype(F32) * keep_prev, u_ref[h].astype(F32), un_ref[h].astype(F32)], axis=0)
               for h in range(2)]
        w = [w_ref[h] for h in range(2)]
        zg = _conv_rows(ext[0], w[0], HALO, tr + HALO)
        zu = _conv_rows(ext[1], w[1], HALO, tr + HALO)
        da = jnp.concatenate([da_ref[...].astype(F32), dan_ref[...].astype(F32) * keep_next], axis=0)
        sg = jax.nn.sigmoid(zg)
        dz = [da * zu * (sg * (1.0 + zg * (1.0 - sg))), da * (zg * sg)]
        m = tr + HALO
        for h in range(2):
            d = dz[h]
            du = w[h][2:3] * d[:tr] + w[h][1:2] * pltpu.roll(d, m - 1, axis=0)[:tr] + w[h][0:1] * pltpu.roll(d, m - 2, axis=0)[:tr]
            du_ref[h] = du.astype(du_ref.dtype)
            dt = d[:tr]
            e = ext[h]
            parts = [jnp.sum(dt * pltpu.roll(e, 2, axis=0)[HALO:HALO + tr], axis=0, keepdims=True),
                     jnp.sum(dt * pltpu.roll(e, 1, axis=0)[HALO:HALO + tr], axis=0, keepdims=True),
                     jnp.sum(dt * e[HALO:HALO + tr], axis=0, keepdims=True)]
            for k in range(3):
                @pl.when(first)
                def _(k=k, h=h):
                    dcw_ref[h, k:k + 1, :] = parts[k]

                @pl.when(jnp.logical_not(first))
                def _(k=k, h=h):
                    dcw_ref[h, k:k + 1, :] += parts[k]

    da_tile = pl.BlockSpec((None, tr, n), lambda j, i: (j, i, 0))
    da_next = pl.BlockSpec((None, HALO, n), lambda j, i: (j, jnp.minimum((i + 1) * (tr // HALO), nb - 1), 0))
    return pl.pallas_call(
        body, name=name, grid=(J, S // tr),
        out_shape=(jax.ShapeDtypeStruct((2, J, S, n), _MXU_DTYPE), jax.ShapeDtypeStruct((2, J, 3, n), F32)),
        in_specs=[tile, prev, nxt, cw, da_tile, da_next], out_specs=(tile, cw),
        compiler_params=_cp("parallel", "arbitrary"))(u24, u24, u24, cw24, da4, da4)


def _rms_rows(v, g):
    rstd = lax.rsqrt(jnp.mean(v * v, axis=-1, keepdims=True) + EPS)
    return v * rstd * g


def _rms_rows_bwd(v, g, dy):
    rstd = lax.rsqrt(jnp.mean(v * v, axis=-1, keepdims=True) + EPS)
    vhat = v * rstd
    dvhat = dy * g
    return rstd * (dvhat - vhat * jnp.mean(dvhat * vhat, axis=-1, keepdims=True)), dy * vhat


def _mla_prep_fwd(z, qg, kvg, tr=256):
    S = z.shape[0]

    def body(q_ref, kv_ref, qg_ref, kvg_ref, qn_ref, kvn_ref):
        qn_ref[...] = _rms_rows(q_ref[...], qg_ref[...]).astype(qn_ref.dtype)
        kvn_ref[...] = _rms_rows(kv_ref[...], kvg_ref[...]).astype(kvn_ref.dtype)

    return pl.pallas_call(
        body, name="mla_prep_fwd", grid=(S // tr,),
        out_shape=(jax.ShapeDtypeStruct((S, 256), _MXU_DTYPE), jax.ShapeDtypeStruct((S, 128), _MXU_DTYPE)),
        in_specs=[pl.BlockSpec((tr, 256), lambda i: (i, 0)), pl.BlockSpec((tr, 128), lambda i: (i, 2)), _vec_spec(256), _vec_spec(128)],
        out_specs=(_row_spec(tr, 256), _row_spec(tr, 128)), compiler_params=_cp("parallel"))(z, z, qg, kvg)


def _mla_prep_bwd(z, qg, kvg, dqn, dkvn, dkpe, duv, tr=256):
    S = z.shape[0]

    def body(q_ref, kv_ref, qg_ref, kvg_ref, dqn_ref, dkvn_ref, dkpe_ref, duv_ref, dz_ref, dqg_ref, dkvg_ref):
        first = pl.program_id(0) == 0
        dq, dqg = _rms_rows_bwd(q_ref[...], qg_ref[...], dqn_ref[...])
        dkv, dkvg = _rms_rows_bwd(kv_ref[...], kvg_ref[...], dkvn_ref[...])
        _acc_rows(dqg_ref, dqg, first)
        _acc_rows(dkvg_ref, dkvg, first)
        dz_ref[:, 0:256] = dq.astype(dz_ref.dtype)
        dz_ref[:, 256:384] = dkv.astype(dz_ref.dtype)
        dz_ref[:, 384:512] = dkpe_ref[...].astype(dz_ref.dtype)
        dz_ref[:, 512:1536] = duv_ref[...].astype(dz_ref.dtype)

    return pl.pallas_call(
        body, name="mla_prep_bwd", grid=(S // tr,),
        out_shape=(jax.ShapeDtypeStruct((S, 1536), _MXU_DTYPE), jax.ShapeDtypeStruct((1, 256), F32), jax.ShapeDtypeStruct((1, 128), F32)),
        in_specs=[pl.BlockSpec((tr, 256), lambda i: (i, 0)), pl.BlockSpec((tr, 128), lambda i: (i, 2)), _vec_spec(256), _vec_spec(128),
                  _row_spec(tr, 256), _row_spec(tr, 128), _row_spec(tr, 128), _row_spec(tr, 1024)],
        out_specs=(_row_spec(tr, 1536), _vec_spec(256), _vec_spec(128)),
        compiler_params=_cp("arbitrary"))(z, z, qg, kvg, dqn, dkvn, dkpe, duv)


def _rope(v, cos, sa, sb):
    return v * cos + pltpu.roll(v, 112, axis=1) * sa + pltpu.roll(v, 16, axis=1) * sb


def _rope_t(d, cos, sa, sb):
    return d * cos + pltpu.roll(d * sa, 16, axis=1) + pltpu.roll(d * sb, 112, axis=1)


def _rope_fwd(qraw, kvall, z, cosq, cosk, sa, sb, tr=256):
    S = qraw.shape[0]

    def body(q_ref, k_ref, v_ref, kpe_ref, cq_ref, ck_ref, sa_ref, sb_ref, qo_ref, ko_ref, vo_ref):
        cq, ck, sa_v, sb_v = cq_ref[...], ck_ref[...], sa_ref[...], sb_ref[...]
        kpe = _rope(kpe_ref[...], ck, sa_v, sb_v)
        for h in range(8):
            cols = slice(128 * h, 128 * h + 128)
            qo_ref[:, cols] = _rope(q_ref[:, cols], cq, sa_v, sb_v).astype(qo_ref.dtype)
            ko_ref[:, cols] = (k_ref[:, cols] + kpe).astype(ko_ref.dtype)
        vo_ref[...] = v_ref[...].astype(vo_ref.dtype)

    tab = _row_spec(tr, 128)
    return pl.pallas_call(
        body, name="rope_fwd", grid=(S // tr,),
        out_shape=(jax.ShapeDtypeStruct((S, 1024), _MXU_DTYPE), jax.ShapeDtypeStruct((S, 1024), _MXU_DTYPE),
                   jax.ShapeDtypeStruct((S, 512), _MXU_DTYPE)),
        in_specs=[_row_spec(tr, 1024), pl.BlockSpec((tr, 1024), lambda i: (i, 0)), pl.BlockSpec((tr, 512), lambda i: (i, 2)),
                  pl.BlockSpec((tr, 128), lambda i: (i, 3)), tab, tab, tab, tab],
        out_specs=(_row_spec(tr, 1024), _row_spec(tr, 1024), _row_spec(tr, 512)),
        compiler_params=_cp("parallel"))(qraw, kvall, kvall, z, cosq, cosk, sa, sb)


def _rope_bwd(dq, dk, dv, cosq, cosk, sa, sb, tr=256):
    S = dq.shape[0]

    def body(dq_ref, dk_ref, dv_ref, cq_ref, ck_ref, sa_ref, sb_ref, dqo_ref, dkv_ref, dkpe_ref):
        cq, ck, sa_v, sb_v = cq_ref[...], ck_ref[...], sa_ref[...], sb_ref[...]
        tot = jnp.zeros((tr, 128), F32)
        for h in range(8):
            cols = slice(128 * h, 128 * h + 128)
            dqo_ref[:, cols] = _rope_t(dq_ref[:, cols], cq, sa_v, sb_v).astype(dqo_ref.dtype)
            dkh = dk_ref[:, cols]
            tot = tot + dkh
            dkv_ref[:, cols] = dkh.astype(dkv_ref.dtype)
        dkv_ref[:, 1024:1536] = dv_ref[...].astype(dkv_ref.dtype)
        dkpe_ref[...] = _rope_t(tot, ck, sa_v, sb_v)

    tab = _row_spec(tr, 128)
    return pl.pallas_call(
        body, name="rope_bwd", grid=(S // tr,),
        out_shape=(jax.ShapeDtypeStruct((S, 1024), _MXU_DTYPE), jax.ShapeDtypeStruct((S, 1536), _MXU_DTYPE),
                   jax.ShapeDtypeStruct((S, 128), F32)),
        in_specs=[_row_spec(tr, 1024), _row_spec(tr, 1024), _row_spec(tr, 512), tab, tab, tab, tab],
        out_specs=(_row_spec(tr, 1024), _row_spec(tr, 1536), _row_spec(tr, 128)),
        compiler_params=_cp("parallel"))(dq, dk, dv, cosq, cosk, sa, sb)


NEG = -1e30


def _attn_fwd(q, k, v, tq=256, tk=256):
    S = q.shape[0]
    assert tq == tk

    def body(q_ref, k_ref, v_ref, o_ref, lse_ref):
        i = pl.program_id(1)
        qs = [q_ref[:, 0:128], q_ref[:, 128:256]]

        def step(kb, carry, diagonal=False):
            start = pl.multiple_of(kb * tk, tk)
            vv = v_ref[pl.ds(start, tk), :]
            out = []
            for h in range(2):
                m, l, acc = carry[3 * h:3 * h + 3]
                s = _dot(qs[h], k_ref[pl.ds(start, tk), 128 * h:128 * h + 128], "nt") * ATTN_SCALE
                if diagonal:
                    s = jnp.where(below, s, NEG)
                m_new = jnp.maximum(m, jnp.max(s, axis=-1, keepdims=True))
                alpha = jnp.exp(m - m_new)
                p = jnp.exp(s - m_new)
                out += [m_new, alpha * l + jnp.sum(p, axis=-1, keepdims=True), alpha * acc + _dot(p, vv, "nn")]
            return tuple(out)

        below = lax.broadcasted_iota(jnp.int32, (tq, tk), 1) <= lax.broadcasted_iota(jnp.int32, (tq, tk), 0)
        init = (jnp.full((tq, 1), NEG, F32), jnp.zeros((tq, 1), F32), jnp.zeros((tq, 128), F32)) * 2
        ma, la, acca, mb, lb, accb = step(i, lax.fori_loop(0, i, step, init), diagonal=True)
        lane = lax.broadcasted_iota(jnp.int32, (tq, 128), 1)
        o_ref[...] = jnp.where(lane < 64, acca / la, accb / lb)
        lse_ref[...] = jnp.where(lane < 64, ma + jnp.log(la), mb + jnp.log(lb))

    return pl.pallas_call(
        body, name="attn_fwd", grid=(4, S // tq),
        out_shape=(jax.ShapeDtypeStruct((S, 512), F32), jax.ShapeDtypeStruct((4, S, 128), F32)),
        in_specs=[pl.BlockSpec((tq, 256), lambda p, i: (i, p)), pl.BlockSpec((S, 256), lambda p, i: (0, p)),
                  pl.BlockSpec((S, 128), lambda p, i: (0, p))],
        out_specs=(pl.BlockSpec((tq, 128), lambda p, i: (i, p)), pl.BlockSpec((None, tq, 128), lambda p, i: (p, i, 0))),
        compiler_params=_cp("parallel", "parallel"))(q, k, v)


def _attn_bwd(q, k, v, o, lse, dycat2, tq=256, tk=256):
    S = q.shape[0]
    assert tq == tk

    def body(q_ref, k_ref, v_ref, o_ref, lse_ref, do_ref, dq_ref, dk_ref, dv_ref):
        j = pl.program_id(1)

        @pl.when(j == 0)
        def _():
            dq_ref[...] = jnp.zeros_like(dq_ref)

        below = lax.broadcasted_iota(jnp.int32, (tq, tk), 1) <= lax.broadcasted_iota(jnp.int32, (tq, tk), 0)
        lane = lax.broadcasted_iota(jnp.int32, (tq, 128), 1)
        ks = [k_ref[:, 0:128], k_ref[:, 128:256]]
        vv = v_ref[...]

        def step(qb, carry, diagonal=False):
            dka, dkb, dvp = carry
            start = pl.multiple_of(qb * tq, tq)
            rows = pl.ds(start, tq)
            do, lse_v = do_ref[rows, :], lse_ref[rows, :]
            prod = do * o_ref[rows, :]
            dks = [dka, dkb]
            for h in range(2):
                mine = (lane < 64) if h == 0 else (lane >= 64)
                delta = jnp.sum(jnp.where(mine, prod, 0.0), axis=-1, keepdims=True)
                do_h = jnp.where(mine, do, 0.0)
                qh = q_ref[rows, 128 * h:128 * h + 128]
                s = _dot(qh, ks[h], "nt") * ATTN_SCALE
                p = jnp.exp(s - lse_v[:, 64 * h:64 * h + 1])
                if diagonal:
                    p = jnp.where(below, p, 0.0)
                dvp = dvp + _dot(p, do_h, "tn")
                ds = p * (_dot(do_h, vv, "nt") - delta) * ATTN_SCALE
                dq_ref[rows, 128 * h:128 * h + 128] += _dot(ds, ks[h], "nn")
                dks[h] = dks[h] + _dot(ds, qh, "tn")
            return dks[0], dks[1], dvp

        zero = jnp.zeros((tk, 128), F32)
        dka, dkb, dvp = lax.fori_loop(j + 1, S // tq, step, step(j, (zero, zero, zero), diagonal=True))
        dk_ref[:, 0:128] = dka
        dk_ref[:, 128:256] = dkb
        dv_ref[...] = dvp

    return pl.pallas_call(
        body, name="attn_bwd", grid=(4, S // tk),
        out_shape=(jax.ShapeDtypeStruct((S, 1024), F32), jax.ShapeDtypeStruct((S, 1024), F32), jax.ShapeDtypeStruct((S, 512), F32)),
        in_specs=[pl.BlockSpec((S, 256), lambda p, j: (0, p)), pl.BlockSpec((tk, 256), lambda p, j: (j, p)),
                  pl.BlockSpec((tk, 128), lambda p, j: (j, p)), pl.BlockSpec((S, 128), lambda p, j: (0, p)),
                  pl.BlockSpec((None, S, 128), lambda p, j: (p, 0, 0)), pl.BlockSpec((None, S, 128), lambda p, j: (0, 0, p))],
        out_specs=(pl.BlockSpec((S, 256), lambda p, j: (0, p)), pl.BlockSpec((tk, 256), lambda p, j: (j, p)),
                   pl.BlockSpec((tk, 128), lambda p, j: (j, p))),
        compiler_params=_cp("parallel", "arbitrary"))(q, k, v, o, lse, dycat2)


CHUNK = 128
GELU_C = math.sqrt(2.0 / math.pi)


def _gelu(v):
    t = jnp.tanh(GELU_C * (v + 0.044715 * (v * v * v)))
    return v * (0.5 * (1.0 + t)), t


def _gelu_grad(v, t):
    return 0.5 * (1.0 + t) + v * (0.5 * (1.0 - t * t) * GELU_C * (1.0 + 3.0 * 0.044715 * v * v))


def _tril(w):
    r = lax.broadcasted_iota(jnp.int32, w.shape, 0)
    c = lax.broadcasted_iota(jnp.int32, w.shape, 1)
    return jnp.where(c <= r, w, 0.0)


def _layer_norm(v, g, b):
    xc = v - jnp.mean(v, axis=-1, keepdims=True)
    rstd = lax.rsqrt(jnp.mean(xc * xc, axis=-1, keepdims=True) + EPS)
    xhat = xc * rstd
    return xhat * g + b, xhat, rstd


def _sgu_fwd(z, o, ln_g, ln_b, w_s, b_st, tr=256):
    S = z.shape[0]

    def body(u_ref, v_ref, o_ref, g_ref, b_ref, ws_ref, bs_ref, y_ref):
        gu, _ = _gelu(u_ref[...])
        gv, _ = _gelu(v_ref[...])
        vln, _, _ = _layer_norm(gv, g_ref[...], b_ref[...])
        y_ref[0] = o_ref[...].astype(y_ref.dtype)
        for g in range(4):
            wt = _tril(ws_ref[g])
            cols = slice(128 * g, 128 * g + 128)
            for ch in range(tr // CHUNK):
                rows = slice(CHUNK * ch, CHUNK * ch + CHUNK)
                mixed = _dot(wt, vln[rows, cols], "nn") + bs_ref[:, g:g + 1]
                y_ref[1, rows, cols] = (gu[rows, cols] * mixed).astype(y_ref.dtype)

    return pl.pallas_call(
        body, name="sgu_fwd", grid=(S // tr,), out_shape=jax.ShapeDtypeStruct((2, S, 512), _MXU_DTYPE),
        in_specs=[pl.BlockSpec((tr, 512), lambda i: (i, 1)), pl.BlockSpec((tr, 512), lambda i: (i, 2)), _row_spec(tr, 512),
                  _vec_spec(512), _vec_spec(512), pl.BlockSpec((4, 128, 128), lambda i: (0, 0, 0)), pl.BlockSpec((128, 4), lambda i: (0, 0))],
        out_specs=pl.BlockSpec((2, tr, 512), lambda i: (0, i, 0)), compiler_params=_cp("parallel"))(z, z, o, ln_g, ln_b, w_s, b_st)


def _sgu_bwd(z, dycat2, ln_g, ln_b, w_s, b_st, tr=256):
    S = z.shape[0]

    def body(u_ref, v_ref, dy_ref, g_ref, b_ref, ws_ref, bs_ref, duv_ref, dg_ref, db_ref, dws_ref, dbs_ref):
        first = pl.program_id(0) == 0
        u_pre, v_pre = u_ref[...], v_ref[...]
        gu, tu = _gelu(u_pre)
        gv, tv = _gelu(v_pre)
        gain = g_ref[...]
        vln, xhat, rstd = _layer_norm(gv, gain, b_ref[...])

        @pl.when(first)
        def _():
            dws_ref[...] = jnp.zeros_like(dws_ref)
            dbs_ref[...] = jnp.zeros_like(dbs_ref)

        dvln_cols = []
        for g in range(4):
            wt = _tril(ws_ref[g])
            cols = slice(128 * g, 128 * g + 128)
            dmixed_sum = jnp.zeros((CHUNK, 128), F32)
            dw = jnp.zeros((CHUNK, CHUNK), F32)
            dvln_rows = []
            for ch in range(tr // CHUNK):
                rows = slice(CHUNK * ch, CHUNK * ch + CHUNK)
                vt = vln[rows, cols]
                mixed = _dot(wt, vt, "nn") + bs_ref[:, g:g + 1]
                dyd = dy_ref[rows, cols]
                duv_ref[rows, cols] = (dyd * mixed * _gelu_grad(u_pre[rows, cols], tu[rows, cols])).astype(duv_ref.dtype)
                dmixed = dyd * gu[rows, cols]
                dmixed_sum = dmixed_sum + dmixed
                dw = dw + _dot(dmixed, vt, "nt")
                dvln_rows.append(_dot(wt, dmixed, "tn"))
            dws_ref[g] += _tril(dw)
            dbs_ref[g:g + 1, :] += jnp.sum(dmixed_sum.T, axis=0, keepdims=True)
            dvln_cols.append(jnp.concatenate(dvln_rows, axis=0))
        dvln = jnp.concatenate(dvln_cols, axis=1)
        _acc_rows(dg_ref, dvln * xhat, first)
        _acc_rows(db_ref, dvln, first)
        dxhat = dvln * gain
        dgv = rstd * (dxhat - jnp.mean(dxhat, axis=-1, keepdims=True) - xhat * jnp.mean(dxhat * xhat, axis=-1, keepdims=True))
        duv_ref[:, 512:1024] = (dgv * _gelu_grad(v_pre, tv)).astype(duv_ref.dtype)

    return pl.pallas_call(
        body, name="sgu_bwd", grid=(S // tr,),
        out_shape=(jax.ShapeDtypeStruct((S, 1024), _MXU_DTYPE), jax.ShapeDtypeStruct((1, 512), F32), jax.ShapeDtypeStruct((1, 512), F32),
                   jax.ShapeDtypeStruct((4, 128, 128), F32), jax.ShapeDtypeStruct((4, 128), F32)),
        in_specs=[pl.BlockSpec((tr, 512), lambda i: (i, 1)), pl.BlockSpec((tr, 512), lambda i: (i, 2)),
                  pl.BlockSpec((None, tr, 512), lambda i: (1, i, 0)), _vec_spec(512), _vec_spec(512),
                  pl.BlockSpec((4, 128, 128), lambda i: (0, 0, 0)), pl.BlockSpec((128, 4), lambda i: (0, 0))],
        out_specs=(_row_spec(tr, 1024), _vec_spec(512), _vec_spec(512), pl.BlockSpec((4, 128, 128), lambda i: (0, 0, 0)),
                   pl.BlockSpec((4, 128), lambda i: (0, 0))),
        compiler_params=_cp("arbitrary"))(z, z, dycat2, ln_g, ln_b, w_s, b_st)


def _sum_parts(name, parts, tr=512):
    P, R, C = parts.shape
    tr = _tile(R, tr) if R % 8 == 0 else R

    def body(p_ref, o_ref):
        g = p_ref[0]
        for k in range(1, P):
            g = g + p_ref[k]
        o_ref[...] = g

    return pl.pallas_call(
        body, name=name, grid=(R // tr,), out_shape=jax.ShapeDtypeStruct((R, C), F32),
        in_specs=[pl.BlockSpec((P, tr, C), lambda i: (0, i, 0))], out_specs=_row_spec(tr, C),
        compiler_params=_cp("parallel"))(parts)


def _adamw_math(w, m, v, g):
    c1 = 1.0 / (1.0 - ADAM_B1 ** ADAM_STEP)
    c2 = 1.0 / (1.0 - ADAM_B2 ** ADAM_STEP)
    m2 = ADAM_B1 * m + (1.0 - ADAM_B1) * g
    v2 = ADAM_B2 * v + (1.0 - ADAM_B2) * (g * g)
    return -ADAM_LR * ((m2 * c1) / (jnp.sqrt(v2 * c2) + ADAM_EPS) + ADAM_WD * w), m2, v2


def _adamw_small(name, params, parts):
    n = len(params)

    def body(*refs):
        ins, outs = refs[:4 * n], refs[4 * n:]
        for i in range(n):
            w_ref, m_ref, v_ref, p_ref = ins[4 * i:4 * i + 4]
            g = p_ref[0]
            for k in range(1, N_DEV):
                g = g + p_ref[k]
            delta, m2, v2 = _adamw_math(w_ref[...], m_ref[...], v_ref[...], g)
            outs[4 * i][...] = g
            outs[4 * i + 1][...] = delta
            outs[4 * i + 2][...] = m2
            outs[4 * i + 3][...] = v2

    flat = [a for (w, m, v), p in zip(params, parts) for a in (w, m, v, p)]
    out = pl.pallas_call(
        body, name=name, out_shape=[jax.ShapeDtypeStruct(w.shape, F32) for (w, _, _) in params for _ in range(4)],
        compiler_params=pltpu.CompilerParams(vmem_limit_bytes=_VMEM_LIMIT))(*flat)
    return [out[4 * i:4 * i + 4] for i in range(n)]


ADAMW_BLOCK_BYTES = 36 * 2 ** 20


def _adamw(name, w, m, v, parts):
    L, R, C = w.shape
    P = parts[0].shape[0]
    row_bytes = 2 * C * (7 * 4 + P * parts[0].dtype.itemsize)
    tr = R
    if R * row_bytes > ADAMW_BLOCK_BYTES:
        tr = next(t for t in (1024, 512, 256, 128, 64, 32, 16) if R % t == 0 and t * row_bytes <= ADAMW_BLOCK_BYTES)
    nr = R // tr
    c1 = 1.0 / (1.0 - ADAM_B1 ** ADAM_STEP)
    c2 = 1.0 / (1.0 - ADAM_B2 ** ADAM_STEP)

    def body(w_ref, m_ref, v_ref, *rest):
        p_refs, (g_ref, d_ref, mo_ref, vo_ref) = rest[:L], rest[L:]
        for ll in range(L):
            @pl.when(pl.program_id(0) == ll)
            def _(p_ref=p_refs[ll]):
                g = p_ref[0].astype(F32)
                for k in range(1, P):
                    g = g + p_ref[k].astype(F32)
                m2 = ADAM_B1 * m_ref[...] + (1.0 - ADAM_B1) * g
                v2 = ADAM_B2 * v_ref[...] + (1.0 - ADAM_B2) * (g * g)
                g_ref[...] = g
                mo_ref[...] = m2
                vo_ref[...] = v2
                d_ref[...] = -ADAM_LR * ((m2 * c1) / (jnp.sqrt(v2 * c2) + ADAM_EPS) + ADAM_WD * w_ref[...])

    def part_spec(ll):
        return pl.BlockSpec((P, tr, C), lambda l, i: (0, jnp.where(l == ll, i, jnp.where(l < ll, 0, nr - 1)), 0))

    full = pl.BlockSpec((None, tr, C), lambda l, i: (l, i, 0))
    sds = jax.ShapeDtypeStruct((L, R, C), F32)
    return pl.pallas_call(
        body, name=name, grid=(L, nr), out_shape=(sds, sds, sds, sds),
        in_specs=[full] * 3 + [part_spec(ll) for ll in range(L)],
        out_specs=(full,) * 4, compiler_params=_cp("arbitrary", "arbitrary"))(w, m, v, *parts)


def _rope_tables(positions):
    half = 16
    inv_freq = 10000.0 ** (-jnp.arange(half, dtype=F32) / half)
    ang = positions.astype(F32)[:, None] * inv_freq
    cos, sin = jnp.cos(ang), jnp.sin(ang)
    S = positions.shape[0]
    z16, z32, z64 = jnp.zeros((S, 16), F32), jnp.zeros((S, 32), F32), jnp.zeros((S, 64), F32)
    cosk = jnp.concatenate([z64, cos, cos, z32], axis=1)
    cosq = jnp.concatenate([jnp.ones((S, 64), F32), cos, cos, z32], axis=1)
    sa = jnp.concatenate([z64, -sin, z16, z32], axis=1)
    sb = jnp.concatenate([z64, z16, sin, z32], axis=1)
    return cosq, cosk, sa, sb


def _ffn_fwd(l, x, mod, n2g, w_up8, cw24, get_w_down4):
    sh, sc, gate = mod
    h = _rmsmod_fwd(f"ffn{l}_norm", x, n2g, sc, sh, n2g)
    u8 = _mm_cols(f"ffn{l}_up", h, w_up8, out_dtype=ACT_DTYPE, tm=1024)
    S, n = u8.shape[1], u8.shape[2]
    u24 = u8.reshape(2, 4, S, n)
    a4 = _ffn_gate_fwd(f"ffn{l}_gate", u24, cw24)
    w_down4 = get_w_down4(a4)
    f, x_new = _mm_rows_resid(f"ffn{l}_down", a4, w_down4, x, gate)
    return x_new, (x, h, u24, a4, f), w_down4


def _ffn_bwd(l, dx, saved, mod, n2g, w_up8, cw24, w_down4, me):
    sh, sc, gate = mod
    x, h, u24, a4, f = saved
    df, dgate = _gate_bwd(f"ffn{l}_gate_bwd", dx, f, gate)
    da4 = _mm_rows_dx(f"ffn{l}_down_dx", df, w_down4, out_dtype=ACT_DTYPE, tm=2048)
    dw_down4 = _mm_rows_dw(f"ffn{l}_down_dw", a4, df, out_dtype=WIRE_DTYPE)
    du24, dcw24 = _ffn_gate_bwd(f"ffn{l}_act_bwd", u24, cw24, da4)
    du8 = du24.reshape((8,) + du24.shape[2:])
    dw_up8t = _mm_cols_dwt(f"ffn{l}_up_dw", h, du8, out_dtype=WIRE_DTYPE, tk=1024)
    sent, token = _exchange_start(f"scatter_ffn{l}", [dw_up8t, dw_down4.reshape(8, 352, dw_down4.shape[2])], True, dcw24, me)
    dh = _mm_cols_dx(f"ffn{l}_up_dx", du8, w_up8, tm=1024, jb=4)
    dx_new, dn2g, dsc, dsh = _rmsmod_bwd(f"ffn{l}_norm_bwd", x, n2g, sc, dh, dx, token)
    return dx_new, dict(sent=sent, cw24=dcw24, n2g=dn2g, mod=(dsh, dsc, dgate))


def kernel(x, c, positions, ada_w, ada_b, norm1_g, norm2_g, ab_w_in, a_conv_w, b_mix_w, b_scale, ab_w_out, cd_w_in, c_q_norm_g, c_w_uq, c_kv_norm_g, c_w_ukv, d_ln_g, d_ln_b, d_w_s, d_b_s, cd_w_out, ffn_w_up, ffn_conv_w, ffn_w_down, final_norm_g, loss_target, m_ada_w, m_ada_b, m_norm1_g, m_norm2_g, m_ab_w_in, m_a_conv_w, m_b_mix_w, m_b_scale, m_ab_w_out, m_cd_w_in, m_c_q_norm_g, m_c_w_uq, m_c_kv_norm_g, m_c_w_ukv, m_d_ln_g, m_d_ln_b, m_d_w_s, m_d_b_s, m_cd_w_out, m_ffn_w_up, m_ffn_conv_w, m_ffn_w_down, m_final_norm_g, v_ada_w, v_ada_b, v_norm1_g, v_norm2_g, v_ab_w_in, v_a_conv_w, v_b_mix_w, v_b_scale, v_ab_w_out, v_cd_w_in, v_c_q_norm_g, v_c_w_uq, v_c_kv_norm_g, v_c_w_ukv, v_d_ln_g, v_d_ln_b, v_d_w_s, v_d_b_s, v_cd_w_out, v_ffn_w_up, v_ffn_conv_w, v_ffn_w_down, v_final_norm_g):
    S, D = x.shape[1], x.shape[2]
    me = 4 * lax.axis_index("x") + 2 * lax.axis_index("y") + lax.axis_index("c")
    x0, target = x[0], loss_target[0]
    W = _MXU_DTYPE

    small_shapes = [(1024,), (3, 64), (32,), (64,), (64,), (2, 3, 704)]
    (g0,) = _exchange("gather_small", [[_pack([c, a_conv_w, c_q_norm_g, d_ln_g, d_ln_b, ffn_conv_w])]], scatter=False)
    c_all, aconv_s, qg_s, lng_s, lnb_s, fcw_s = _unpack(g0[:, 0], small_shapes, lead=(N_DEV,))
    conv_w = aconv_s.transpose(1, 0, 2).reshape(3, 512)
    qg, ln_g, ln_b = qg_s.reshape(1, 256), lng_s.reshape(1, 512), lnb_s.reshape(1, 512)
    cw24 = [fcw_s[:, l].reshape(2, 4, 3, 704) for l in range(2)]
    c16 = jnp.pad(c_all, ((0, 16 - N_DEV), (0, 0)))

    mod_cols = _ada_fwd(c16, ada_w)
    (g1,) = _exchange("gather_mod", [[_pack([mod_cols])]], scatter=False)
    mod_all = _unpack(g1[:, 0], [(2, 16, 768)], lead=(N_DEV,))[0]
    mod_mine = lax.dynamic_index_in_dim(mod_all, me, axis=2, keepdims=False)
    mod = mod_mine.transpose(1, 0, 2).reshape(2, 6 * D) + ada_b
    mods = [[mod[l, k * D:(k + 1) * D].reshape(1, D) for k in range(6)] for l in range(2)]

    gw_ab, token = _exchange_start("gather_w_ab", [ab_w_in[0].astype(W), ab_w_out[0].astype(W)], False, mod, me)
    gw_up0, token = _exchange_start("gather_w_ffn0_up", [ffn_w_up[0].astype(W)], False, token, me)
    gw_dn0, token = _exchange_start("gather_w_ffn0_down", [ffn_w_down[0].astype(W)], False, token, me)
    gw_cd, token = _exchange_start("gather_w_cd", [
        cd_w_in[0].astype(W).reshape(1440, 128), c_w_uq[0].astype(W).reshape(192, 128), c_w_ukv[0].astype(W),
        cd_w_out[0].astype(W)], False, token, me)
    gw_up1, token = _exchange_start("gather_w_ffn1_up", [ffn_w_up[1].astype(W)], False, token, me)
    gw_dn1, started = _exchange_start("gather_w_ffn1_down", [ffn_w_down[1].astype(W)], False, token, me)

    cosq, cosk, sa, sb = _rope_tables(positions[0])
    n1g = [norm1_g[l].reshape(1, D) for l in range(2)]
    n2g = [norm2_g[l].reshape(1, D) for l in range(2)]
    mix_w, scale = b_mix_w[0], b_scale
    kvg = c_kv_norm_g
    w_s, b_st = d_w_s[0], d_b_s[0].T

    sh1, sc1, g1m = mods[0][:3]
    h_ab = _rmsmod_fwd("ab_norm", x0, n1g[0], sc1, sh1, started)
    w_abin8, w_about = _exchange_wait("wait_w_ab", gw_ab, h_ab)
    w_about2 = w_about.reshape(2, 512, D)
    z8 = _mm_cols("ab_in", h_ab, w_abin8, tm=2048)
    ycat_ab = _ab_mix_fwd(z8, conv_w, mix_w, scale)
    y_ab, x1 = _mm_rows_resid("ab_out", ycat_ab, w_about2, x0, g1m)
    w_up8, w_down4 = [None, None], [None, None]
    (w_up8[0],) = _exchange_wait("wait_w_ffn0_up", gw_up0, x1)
    x2, ffn0_saved, w_down4[0] = _ffn_fwd(0, x1, mods[0][3:], n2g[0], w_up8[0], cw24[0],
                                          lambda after: _exchange_wait("wait_w_ffn0_down", gw_dn0, after)[0].reshape(4, 704, D))

    w_cdin, w_uq, w_ukv, w_cdout = _exchange_wait("wait_w_cd", gw_cd, x2)
    w_cdout2 = w_cdout.reshape(2, 512, D)
    w_cd = w_cdin.reshape(8, D, 180).transpose(1, 0, 2).reshape(D, 1440)
    zc = lambda n: jnp.zeros((D, n), W)
    w_cd_pad = jnp.concatenate([w_cd[:, :384], zc(64), w_cd[:, 384:416], zc(32), w_cd[:, 416:]], axis=1)
    w_uq_pad = jnp.pad(w_uq.reshape(8, 256, 96).transpose(1, 0, 2), ((0, 0), (0, 0), (0, 32))).reshape(256, 1024)
    w_ukv_h = w_ukv.transpose(1, 0, 2)
    w_k_pad = jnp.pad(w_ukv_h[:, :, :64], ((0, 0), (0, 0), (0, 64))).reshape(128, 1024)
    w_kv_pad = jnp.concatenate([w_k_pad, w_ukv_h[:, :, 64:].reshape(128, 512)], axis=1)

    sh1, sc1, g1c = mods[1][:3]
    h_cd = _rmsmod_fwd("cd_norm", x2, n1g[1], sc1, sh1, n1g[1])
    z_cd = _mm_nn("cd_in", h_cd, w_cd_pad)
    qn, kvn = _mla_prep_fwd(z_cd, qg, kvg)
    qraw = _mm_nn("cd_uq", qn, w_uq_pad)
    kvall = _mm_nn("cd_ukv", kvn, w_kv_pad)
    q_r, k_r, v_r = _rope_fwd(qraw, kvall, z_cd, cosq, cosk, sa, sb)
    o, lse = _attn_fwd(q_r, k_r, v_r)
    ycat_cd = _sgu_fwd(z_cd, o, ln_g, ln_b, w_s, b_st)
    y_cd, x3 = _mm_rows_resid("cd_out", ycat_cd, w_cdout2, x2, g1c)
    (w_up8[1],) = _exchange_wait("wait_w_ffn1_up", gw_up1, x3)
    x4, ffn1_saved, w_down4[1] = _ffn_fwd(1, x3, mods[1][3:], n2g[1], w_up8[1], cw24[1],
                                          lambda after: _exchange_wait("wait_w_ffn1_down", gw_dn1, after)[0].reshape(4, 704, D))

    loss_local, dx4, dfg = _loss_head(x4, final_norm_g.reshape(1, D), target)

    dx3, gf1 = _ffn_bwd(1, dx4, ffn1_saved, mods[1][3:], n2g[1], w_up8[1], cw24[1], w_down4[1], me)

    dy, dg1c = _gate_bwd("cd_gate_bwd", dx3, y_cd, g1c)
    dycat = _mm_rows_dx("cd_out_dx", dy, w_cdout2)
    dw_cdout = _mm_rows_dw("cd_out_dw", ycat_cd, dy, out_dtype=WIRE_DTYPE)
    duv, dln_g, dln_b, dws, dbs = _sgu_bwd(z_cd, dycat, ln_g, ln_b, w_s, b_st)
    dq_r, dk_r, dv_r = _attn_bwd(q_r, k_r, v_r, o, lse, dycat)
    dqraw, dkvall, dkpe = _rope_bwd(dq_r, dk_r, dv_r, cosq, cosk, sa, sb)
    dqn = _mm_nt("cd_uq_dx", dqraw, w_uq_pad, tn=256)
    dkvn = _mm_nt("cd_ukv_dx", dkvall, w_kv_pad, tn=128)
    dw_uq_pad = _mm_tn("cd_uq_dw", qn, dqraw, tm=256)
    dw_kv_pad = _mm_tn("cd_ukv_dw", kvn, dkvall, tm=128)
    dz_cd, dqg, dkvg = _mla_prep_bwd(z_cd, qg, kvg, dqn, dkvn, dkpe, duv)
    dh_cd = _mm_nt("cd_in_dx", dz_cd, w_cd_pad)
    dw_cd_pad = _mm_tn("cd_in_dw", h_cd, dz_cd)
    dw_cd =jnp.concatenate([dw_cd_pad[:, :384], dw_cd_pad[:, 448:480], dw_cd_pad[:, 512:]], axis=1)
    dw_cd8 = dw_cd.reshape(D, 8, 180).transpose(1, 0, 2).reshape(8, 1440, 128).astype(WIRE_DTYPE)
    dw_uq8 = dw_uq_pad.reshape(256, 8, 128)[:, :, :96].transpose(1, 0, 2).reshape(8, 192, 128).astype(WIRE_DTYPE)
    dw_ukv8 = jnp.concatenate([dw_kv_pad[:, :1024].reshape(128, 8, 128)[:, :, :64], dw_kv_pad[:, 1024:].reshape(128, 8, 64)],
                              axis=2).transpose(1, 0, 2).astype(WIRE_DTYPE)
    sent_cd, token = _exchange_start("scatter_cd", [dw_cd8, dw_uq8, dw_ukv8, dw_cdout.reshape(8, 128, D)], True, dqg, me)
    dx2, dn1g_cd, dsc1_cd, dsh1_cd = _rmsmod_bwd("cd_norm_bwd", x2, n1g[1], sc1, dh_cd, dx3, token)

    dx1, gf0 = _ffn_bwd(0, dx2, ffn0_saved, mods[0][3:], n2g[0], w_up8[0], cw24[0], w_down4[0], me)

    dy, dg1m = _gate_bwd("ab_gate_bwd", dx1, y_ab, g1m)
    dycat = _mm_rows_dx("ab_out_dx", dy, w_about2)
    dw_about = _mm_rows_dw("ab_out_dw", ycat_ab, dy, out_dtype=WIRE_DTYPE)
    dz8, dconv_w, dmix_w, dscale = _ab_mix_bwd(z8, dycat, conv_w, mix_w, scale)
    dz8 = dz8.reshape(8, S, 256)
    dw_abin8 = _mm_cols_dw("ab_in_dw", h_ab, dz8, out_dtype=WIRE_DTYPE, tk=1024)
    sent_ab, token = _exchange_start("scatter_ab", [dw_abin8, dw_about.reshape(8, 128, D)], True, dscale, me)
    dh_ab = _mm_cols_dx("ab_in_dx", dz8, w_abin8)
    dx0, dn1g_ab, dsc1_ab, dsh1_ab = _rmsmod_bwd("ab_norm_bwd", x0, n1g[0], mods[0][1], dh_ab, dx1, token)

    dmod = jnp.stack([jnp.concatenate([dsh1_ab, dsc1_ab, dg1m, *gf0["mod"]], axis=1)[0],
                      jnp.concatenate([dsh1_cd, dsc1_cd, dg1c, *gf1["mod"]], axis=1)[0]])
    small_view = dict(ada_b=(2, 6 * D), norm1_g=(2, D), norm2_g=(2, D), b_mix_w=(512, 128), b_scale=(1, 512), c_kv_norm_g=(1, 128),
                      d_w_s=(512, 128), d_b_s=(4, 128), final_norm_g=(1, D),
                      a_conv_w=(3, 64), c_q_norm_g=(1, 32), d_ln_g=(1, 64), d_ln_b=(1, 64), ffn_conv_w=(2, 3, 704))
    small_names = list(small_view)
    small_grads = [dmod, jnp.concatenate([dn1g_ab, dn1g_cd]), jnp.concatenate([gf0["n2g"], gf1["n2g"]]),
                   dmix_w.reshape(512, 128), dscale, dkvg, dws.reshape(512, 128), dbs, dfg,
                   dconv_w.reshape(3, 8, 64).transpose(1, 0, 2), dqg.reshape(8, 1, 32), dln_g.reshape(8, 1, 64), dln_b.reshape(8, 1, 64),
                   jnp.stack([gf0["cw24"].reshape(8, 3, 704), gf1["cw24"].reshape(8, 3, 704)], axis=1)]
    small_sent, token = _exchange_start("gather_small_grads", small_grads, [False] * 9 + [True] * 5, dx0, me)

    res = {}

    def update(name, w, m, v, parts, shape3d):
        outs = _adamw("adamw_" + name, w.reshape(shape3d), m.reshape(shape3d), v.reshape(shape3d),
                      [p.reshape((p.shape[0],) + shape3d[1:]) for p in parts])
        res[name] = [o_.reshape(w.shape) for o_ in outs]

    p_up1, p_dn1 = _exchange_wait("wait_scatter_ffn1", gf1["sent"], token)
    p_up0, p_dn0 = _exchange_wait("wait_scatter_ffn0", gf0["sent"], token)
    swap = lambda a: jnp.swapaxes(a, 1, 2)
    update("ffn_w_up", swap(ffn_w_up), swap(m_ffn_w_up), swap(v_ffn_w_up), [p_up0, p_up1], (2, 704, D))
    res["ffn_w_up"] = [swap(o_) for o_ in res["ffn_w_up"]]
    update("ffn_w_down", ffn_w_down, m_ffn_w_down, v_ffn_w_down, [p_dn0, p_dn1], (2, 352, D))
    p_cdin, p_uq, p_ukv, p_cdout = _exchange_wait("wait_scatter_cd", sent_cd, res["ffn_w_down"][0])
    update("cd_w_in", cd_w_in, m_cd_w_in, v_cd_w_in, [p_cdin], (1, 1440, 128))
    update("c_w_uq", c_w_uq, m_c_w_uq, v_c_w_uq, [p_uq], (1, 192, 128))
    update("c_w_ukv", c_w_ukv, m_c_w_ukv, v_c_w_ukv, [p_ukv], (1, 128, 128))
    update("cd_w_out", cd_w_out, m_cd_w_out, v_cd_w_out, [p_cdout], (1, 128, D))
    p_abin, p_about = _exchange_wait("wait_scatter_ab", sent_ab, res["cd_w_out"][0])
    update("ab_w_in", ab_w_in, m_ab_w_in, v_ab_w_in, [p_abin], (1, D, 256))
    update("ab_w_out", ab_w_out, m_ab_w_out, v_ab_w_out, [p_about], (1, 128, D))

    small_parts = _exchange_wait("wait_small_grads", small_sent, res["ab_w_out"][0])
    dmod_all = small_parts[0]
    dmod_cols = lax.dynamic_slice_in_dim(dmod_all, me * 768, 768, axis=2).transpose(1, 0, 2)
    g_ada_w = _ada_bwd(c16, jnp.pad(dmod_cols, ((0, 0), (0, 16 - N_DEV), (0, 0))))
    update("ada_w", ada_w, m_ada_w, v_ada_w, [g_ada_w[l][None] for l in range(2)], (2, D, 768))

    small_w = dict(ada_b=(ada_b, m_ada_b, v_ada_b), norm1_g=(norm1_g, m_norm1_g, v_norm1_g), norm2_g=(norm2_g, m_norm2_g, v_norm2_g),
                   b_mix_w=(b_mix_w, m_b_mix_w, v_b_mix_w), b_scale=(b_scale, m_b_scale, v_b_scale),
                   c_kv_norm_g=(c_kv_norm_g, m_c_kv_norm_g, v_c_kv_norm_g), d_w_s=(d_w_s, m_d_w_s, v_d_w_s),
                   d_b_s=(d_b_s, m_d_b_s, v_d_b_s), final_norm_g=(final_norm_g, m_final_norm_g, v_final_norm_g),
                   a_conv_w=(a_conv_w, m_a_conv_w, v_a_conv_w), c_q_norm_g=(c_q_norm_g, m_c_q_norm_g, v_c_q_norm_g),
                   d_ln_g=(d_ln_g, m_d_ln_g, v_d_ln_g), d_ln_b=(d_ln_b, m_d_ln_b, v_d_ln_b),
                   ffn_conv_w=(ffn_conv_w, m_ffn_conv_w, v_ffn_conv_w))
    small_out = _adamw_small("adamw_small", [tuple(a.reshape(small_view[n]) for a in small_w[n]) for n in small_names],
                             list(small_parts))
    for n, outs in zip(small_names, small_out):
        res[n] = [o_.reshape(small_w[n][0].shape) for o_ in outs]

    loss = lax.psum(loss_local[0, 0], ("x", "y", "c"))
    order = ["ada_w", "ada_b", "norm1_g", "norm2_g", "ab_w_in", "a_conv_w", "b_mix_w", "b_scale", "ab_w_out", "cd_w_in", "c_q_norm_g",
             "c_w_uq", "c_kv_norm_g", "c_w_ukv", "d_ln_g", "d_ln_b", "d_w_s", "d_b_s", "cd_w_out", "ffn_w_up", "ffn_conv_w",
             "ffn_w_down", "final_norm_g"]
    return (loss, dx0[None], *[res[n][0] for n in order], *[res[n][1] for n in order], *[res[n][2] for n in order],
            *[res[n][3] for n in order])
```

```python
import functools
import math

import jax
import jax.numpy as jnp
from jax import lax
from jax.experimental import pallas as pl
from jax.experimental.pallas import tpu as pltpu

F32 = jnp.float32
BF16 = jnp.bfloat16
_MXU_DTYPE = BF16
WIRE_DTYPE = BF16
ACT_DTYPE = BF16
_VMEM_LIMIT = 56 * 2 ** 20
N_DEV = 8
EPS = 1e-6
POOL_WINDOWS = (2, 4, 8, 16)
ATTN_SCALE = (64 + 32) ** -0.5
ADAM_LR, ADAM_B1, ADAM_B2, ADAM_EPS, ADAM_WD, ADAM_STEP = 0.001, 0.9, 0.999, 1e-08, 0.01, 10
MESH = pl.DeviceIdType.MESH
ANY = pl.BlockSpec(memory_space=pl.ANY)


def _cp(*sem):
    return pltpu.CompilerParams(dimension_semantics=sem, vmem_limit_bytes=_VMEM_LIMIT)


def _dot(a, b, contract):
    dn = {"nn": (((1,), (0,)), ((), ())), "nt": (((1,), (1,)), ((), ())), "tn": (((0,), (0,)), ((), ()))}[contract]
    return lax.dot_general(a.astype(_MXU_DTYPE), b.astype(_MXU_DTYPE), dn, preferred_element_type=F32)


def _my_position():
    x, y, c = lax.axis_index("x"), lax.axis_index("y"), lax.axis_index("c")
    return x, y, c, 4 * x + 2 * y + c


def _exchange(name, groups, scatter):
    flat = [a for g in groups for a in g]
    n_in, n_grp = len(flat), len(groups)
    out_shapes = []
    for g in groups:
        slab = g[0].shape[1:] if scatter else g[0].shape
        out_shapes.append(jax.ShapeDtypeStruct((N_DEV, len(g)) + tuple(slab), g[0].dtype))

    def body(*refs):
        ins, outs = refs[:n_in], refs[n_in:n_in + n_grp]
        send_sems, recv_sems, local_sems = refs[n_in + n_grp:]
        x, y, c, me = _my_position()
        i = 0
        for gi, g in enumerate(groups):
            for l in range(len(g)):
                src = ins[i]
                i += 1
                pltpu.make_async_copy(src.at[me] if scatter else src, outs[gi].at[me, l], local_sems.at[gi]).start()
                for k in range(1, N_DEV):
                    px = 1 - x if k & 4 else x
                    py = 1 - y if k & 2 else y
                    pc = 1 - c if k & 1 else c
                    peer = 4 * px + 2 * py + pc
                    pltpu.make_async_remote_copy(
                        src_ref=src.at[peer] if scatter else src, dst_ref=outs[gi].at[me, l],
                        send_sem=send_sems.at[gi], recv_sem=recv_sems.at[gi],
                        device_id=(px, py, pc), device_id_type=MESH).start()
        for gi in range(n_grp):
            mine = outs[gi].at[me]
            pltpu.make_async_copy(mine, mine, local_sems.at[gi]).wait()
            seven = outs[gi].at[pl.ds(0, N_DEV - 1)]
            w = pltpu.make_async_remote_copy(src_ref=seven, dst_ref=seven, send_sem=send_sems.at[gi],
                                             recv_sem=recv_sems.at[gi], device_id=(x, y, c), device_id_type=MESH)
            w.wait_send()
            w.wait_recv()

    return pl.pallas_call(
        body, name=name, out_shape=tuple(out_shapes),
        in_specs=[ANY] * n_in, out_specs=tuple([ANY] * n_grp),
        scratch_shapes=[pltpu.SemaphoreType.DMA((n_grp,)), pltpu.SemaphoreType.DMA((n_grp,)),
                        pltpu.SemaphoreType.DMA((n_grp,))],
        compiler_params=pltpu.CompilerParams(has_side_effects=True),
    )(*flat)


HBM_SPEC = pl.BlockSpec(memory_space=pltpu.HBM)
SEM_SPEC = pl.BlockSpec(memory_space=pltpu.SEMAPHORE)
EFFECT = pltpu.SideEffectType.DATAFLOW_SIDE_EFFECTING


def _put_mine(name, srcs, scatter, me):
    n = len(srcs)
    slabs = [tuple(s.shape[1:] if sc else s.shape) for s, sc in zip(srcs, scatter)]

    def body(me_ref, *refs):
        for i in range(n):
            refs[n + i][...] = refs[i][...]

    def at_me(slab):
        return pl.BlockSpec((None,) + slab, lambda g, me_ref, nd=len(slab): (me_ref[0],) + (0,) * nd)

    def whole(slab):
        return pl.BlockSpec(slab, lambda g, me_ref, nd=len(slab): (0,) * nd)

    return pl.pallas_call(
        body, name=name,
        grid_spec=pltpu.PrefetchScalarGridSpec(
            num_scalar_prefetch=1, grid=(1,),
            in_specs=[at_me(slab) if sc else whole(slab) for slab, sc in zip(slabs, scatter)],
            out_specs=[at_me(slab) for slab in slabs]),
        out_shape=[jax.ShapeDtypeStruct((N_DEV,) + slab, s.dtype) for slab, s in zip(slabs, srcs)],
        compiler_params=_cp("arbitrary"))(me.reshape(1), *srcs)


def _exchange_start(name, srcs, scatter, after, me):
    n = len(srcs)
    scatter = list(scatter) if isinstance(scatter, (list, tuple)) else [scatter] * n
    lands = _put_mine(name + "_mine", srcs, scatter, me)
    srcs = [pltpu.with_memory_space_constraint(a, pltpu.HBM) for a in srcs]
    lands = [pltpu.with_memory_space_constraint(a, pltpu.HBM) for a in lands]

    def body(*refs):
        ins, land = refs[:n], refs[n:2 * n]
        send_sems, recv_sems, token = refs[2 * n + 1], refs[2 * n + 2], refs[-1]
        x, y, c, me_in = _my_position()
        for i in range(n):
            for k in range(1, N_DEV):
                px = 1 - x if k & 4 else x
                py = 1 - y if k & 2 else y
                pc = 1 - c if k & 1 else c
                pltpu.make_async_remote_copy(
                    src_ref=ins[i].at[4 * px + 2 * py + pc] if scatter[i] else ins[i], dst_ref=land[i].at[me_in],
                    send_sem=send_sems.at[i], recv_sem=recv_sems.at[i],
                    device_id=(px, py, pc), device_id_type=MESH).start()
        token[...] = jnp.zeros_like(token)

    outs = pl.pallas_call(
        body, name=name,
        out_shape=(pltpu.SemaphoreType.DMA((n,)), pltpu.SemaphoreType.DMA((n,)),
                   *[pltpu.HBM(a.shape, a.dtype) for a in srcs], *[pltpu.HBM(a.shape, a.dtype) for a in lands],
                   jax.ShapeDtypeStruct((8, 128), F32)),
        in_specs=[HBM_SPEC] * (2 * n) + [ANY],
        out_specs=(SEM_SPEC, SEM_SPEC, *[HBM_SPEC] * (2 * n), pl.BlockSpec(memory_space=pltpu.VMEM)),
        input_output_aliases={i: 2 + i for i in range(2 * n)},
        compiler_params=pltpu.CompilerParams(has_side_effects=EFFECT),
    )(*srcs, *lands, after)
    return (outs[0], outs[1], outs[2:2 + n], outs[2 + n:2 + 2 * n]), outs[-1]


def _exchange_wait(name, handle, after):
    send_sems, recv_sems, srcs, lands = handle
    n = len(srcs)

    def body(*refs):
        land, send_ref, recv_ref = refs[n:2 * n], refs[2 * n], refs[2 * n + 1]
        x, y, c, _ = _my_position()
        for i in range(n):
            seven = land[i].at[pl.ds(0, N_DEV - 1)]
            w = pltpu.make_async_remote_copy(src_ref=seven, dst_ref=seven, send_sem=send_ref.at[i], recv_sem=recv_ref.at[i],
                                             device_id=(x, y, c), device_id_type=MESH)
            w.wait_send()
            w.wait_recv()

    outs = pl.pallas_call(
        body, name=name,
        out_shape=(*[pltpu.HBM(a.shape, a.dtype) for a in srcs], *[pltpu.HBM(a.shape, a.dtype) for a in lands]),
        in_specs=[HBM_SPEC] * (2 * n) + [SEM_SPEC, SEM_SPEC, ANY],
        out_specs=tuple([HBM_SPEC] * (2 * n)),
        input_output_aliases={i: i for i in range(2 * n)},
        compiler_params=pltpu.CompilerParams(has_side_effects=EFFECT),
    )(*srcs, *lands, send_sems, recv_sems, after)
    return outs[n:]


def _pack(arrs):
    flat = jnp.concatenate([a.reshape(-1).astype(F32) for a in arrs])
    n = flat.shape[0]
    rows = -(-n // 1024) * 8
    return jnp.pad(flat, (0, rows * 128 - n)).reshape(rows, 128)


def _unpack(buf, shapes, lead=()):
    flat = buf.reshape(lead + (-1,))
    out, off = [], 0
    for s in shapes:
        n = math.prod(s)
        out.append(flat[..., off:off + n].reshape(lead + tuple(s)))
        off += n
    return out


def _mm(name, a, a_spec, b, b_spec, out_sds, o_spec, grid, contract, nk=1, stacked=0):
    o_blk = tuple(d for d in o_spec.block_shape if d is not None)

    def body(a_ref, b_ref, o_ref, *acc):
        if stacked:
            r = _dot(a_ref[0], b_ref[0], contract)
            for q in range(1, stacked):
                r = r + _dot(a_ref[q], b_ref[q], contract)
        else:
            r = _dot(a_ref[...], b_ref[...], contract)
        if nk == 1:
            o_ref[...] = r.astype(o_ref.dtype)
        else:
            k = pl.program_id(len(grid) - 1)

            @pl.when(k == 0)
            def _():
                acc[0][...] = r

            @pl.when(k > 0)
            def _():
                acc[0][...] += r

            @pl.when(k == nk - 1)
            def _():
                o_ref[...] = acc[0][...].astype(o_ref.dtype)

    sem = ("parallel",) * (len(grid) - 1) + (("arbitrary",) if nk > 1 else ("parallel",))
    return pl.pallas_call(
        body, name=name, out_shape=out_sds, grid=grid, in_specs=[a_spec, b_spec], out_specs=o_spec,
        scratch_shapes=[pltpu.VMEM(o_blk, F32)] if nk > 1 else [], compiler_params=_cp(*sem))(a, b)


def _tile(n, want):
    t = min(n, want)
    assert n % t == 0, (n, t)
    return t


def _mm_nn(name, a, b, out_dtype=F32, tm=512, tn=512):
    (M, K), N = a.shape, b.shape[1]
    tm, tn = _tile(M, tm), _tile(N, tn)
    return _mm(name, a, pl.BlockSpec((tm, K), lambda i, j: (i, 0)), b, pl.BlockSpec((K, tn), lambda i, j: (0, j)),
               jax.ShapeDtypeStruct((M, N), out_dtype), pl.BlockSpec((tm, tn), lambda i, j: (i, j)),
               (M // tm, N // tn), "nn")


def _mm_nt(name, a, b, out_dtype=F32, tm=512, tn=512):
    (M, K), N = a.shape, b.shape[0]
    tm, tn = _tile(M, tm), _tile(N, tn)
    return _mm(name, a, pl.BlockSpec((tm, K), lambda i, j: (i, 0)), b, pl.BlockSpec((tn, K), lambda i, j: (j, 0)),
               jax.ShapeDtypeStruct((M, N), out_dtype), pl.BlockSpec((tm, tn), lambda i, j: (i, j)),
               (M // tm, N // tn), "nt")


def _mm_tn(name, a, b, out_dtype=F32, tm=512, tn=512):
    (K, M), N = a.shape, b.shape[1]
    tm, tn = _tile(M, tm), _tile(N, tn)
    return _mm(name, a, pl.BlockSpec((K, tm), lambda i, j: (0, i)), b, pl.BlockSpec((K, tn), lambda i, j: (0, j)),
               jax.ShapeDtypeStruct((M, N), out_dtype), pl.BlockSpec((tm, tn), lambda i, j: (i, j)),
               (M // tm, N // tn), "tn")


def _mm_cols(name, a, w, out_dtype=F32, tm=512):
    (M, K), (J, _, n) = a.shape, w.shape
    tm = _tile(M, tm)
    return _mm(name, a, pl.BlockSpec((tm, K), lambda j, i: (i, 0)), w, pl.BlockSpec((None, K, n), lambda j, i: (j, 0, 0)),
               jax.ShapeDtypeStruct((J, M, n), out_dtype), pl.BlockSpec((None, tm, n), lambda j, i: (j, i, 0)),
               (J, M // tm), "nn")


def _mm_cols_dx(name, d, w, out_dtype=F32, tm=512, jb=None):
    (J, M, n), K = d.shape, w.shape[1]
    tm, jb = _tile(M, tm), J if jb is None else jb
    return _mm(name, d, pl.BlockSpec((jb, tm, n), lambda i, j: (j, i, 0)), w, pl.BlockSpec((jb, K, n), lambda i, j: (j, 0, 0)),
               jax.ShapeDtypeStruct((M, K), out_dtype), pl.BlockSpec((tm, K), lambda i, j: (i, 0)),
               (M // tm, J // jb), "nt", nk=J // jb, stacked=jb)


def _mm_cols_dw(name, a, d, out_dtype=F32, tk=512):
    (M, K), (J, _, n) = a.shape, d.shape
    tk = _tile(K, tk)
    return _mm(name, a, pl.BlockSpec((M, tk), lambda j, i: (0, i)), d, pl.BlockSpec((None, M, n), lambda j, i: (j, 0, 0)),
               jax.ShapeDtypeStruct((J, K, n), out_dtype), pl.BlockSpec((None, tk, n), lambda j, i: (j, i, 0)),
               (J, K // tk), "tn")


def _mm_cols_dwt(name, a, d, out_dtype=F32, tk=512):
    (M, K), (J, _, n) = a.shape, d.shape
    tk = _tile(K, tk)
    return _mm(name, d, pl.BlockSpec((None, M, n), lambda j, i: (j, 0, 0)), a, pl.BlockSpec((M, tk), lambda j, i: (0, i)),
               jax.ShapeDtypeStruct((J, n, K), out_dtype), pl.BlockSpec((None, n, tk), lambda j, i: (j, 0, i)),
               (J, K // tk), "tn")


def _mm_rows_resid(name, a, w, resid, gate, tm=512):
    (Q, M, k), N = a.shape, w.shape[2]
    tm = _tile(M, tm)

    def body(a_ref, w_ref, r_ref, g_ref, y_ref, x_ref):
        y = _dot(a_ref[0], w_ref[0], "nn")
        for q in range(1, Q):
            y = y + _dot(a_ref[q], w_ref[q], "nn")
        y_ref[...] = y
        x_ref[...] = r_ref[...] + g_ref[...] * y

    return pl.pallas_call(
        body, name=name, grid=(M // tm,),
        out_shape=(jax.ShapeDtypeStruct((M, N), F32), jax.ShapeDtypeStruct((M, N), F32)),
        in_specs=[pl.BlockSpec((Q, tm, k), lambda i: (0, i, 0)), pl.BlockSpec((Q, k, N), lambda i: (0, 0, 0)),
                  pl.BlockSpec((tm, N), lambda i: (i, 0)), pl.BlockSpec((1, N), lambda i: (0, 0))],
        out_specs=(pl.BlockSpec((tm, N), lambda i: (i, 0)), pl.BlockSpec((tm, N), lambda i: (i, 0))),
        compiler_params=_cp("parallel"))(a, w, resid, gate)


def _mm_rows_dx(name, d, w, out_dtype=F32, tm=512):
    (M, N), (Q, k, _) = d.shape, w.shape
    tm = _tile(M, tm)
    return _mm(name, d, pl.BlockSpec((tm, N), lambda q, i: (i, 0)), w, pl.BlockSpec((None, k, N), lambda q, i: (q, 0, 0)),
               jax.ShapeDtypeStruct((Q, M, k), out_dtype), pl.BlockSpec((None, tm, k), lambda q, i: (q, i, 0)),
               (Q, M // tm), "nt")


def _mm_rows_dw(name, a, d, out_dtype=F32, tn=512):
    (Q, M, k), N = a.shape, d.shape[1]
    tn = _tile(N, tn)
    return _mm(name, a, pl.BlockSpec((None, M, k), lambda q, j: (q, 0, 0)), d, pl.BlockSpec((M, tn), lambda q, j: (0, j)),
               jax.ShapeDtypeStruct((Q, k, N), out_dtype), pl.BlockSpec((None, k, tn), lambda q, j: (q, 0, j)),
               (Q, N // tn), "tn")


def _silu(v):
    return v * jax.nn.sigmoid(v)


def _ada_fwd(c16, ada_w):
    L, D, n = ada_w.shape

    def body(c_ref, w_ref, o_ref):
        o_ref[...] = _dot(_silu(c_ref[...]), w_ref[...], "nn")

    return pl.pallas_call(
        body, name="ada_fwd", grid=(L,), out_shape=jax.ShapeDtypeStruct((L, 16, n), F32),
        in_specs=[pl.BlockSpec((16, D), lambda l: (0, 0)), pl.BlockSpec((None, D, n), lambda l: (l, 0, 0))],
        out_specs=pl.BlockSpec((None, 16, n), lambda l: (l, 0, 0)), compiler_params=_cp("parallel"))(c16, ada_w)


def _ada_bwd(c16, dmod16):
    L, _, n = dmod16.shape
    D = c16.shape[1]

    def body(c_ref, d_ref, o_ref):
        o_ref[...] = _dot(_silu(c_ref[...]), d_ref[...], "tn")

    return pl.pallas_call(
        body, name="ada_bwd", grid=(L,), out_shape=jax.ShapeDtypeStruct((L, D, n), F32),
        in_specs=[pl.BlockSpec((16, D), lambda l: (0, 0)), pl.BlockSpec((None, 16, n), lambda l: (l, 0, 0))],
        out_specs=pl.BlockSpec((None, D, n), lambda l: (l, 0, 0)), compiler_params=_cp("parallel"))(c16, dmod16)


def _row_spec(tr, n):
    return pl.BlockSpec((tr, n), lambda i: (i, 0))


def _vec_spec(n):
    return pl.BlockSpec((1, n), lambda i: (0, 0))


def _rmsmod_fwd(name, x, g, sc, sh, after, tr=256):
    S, D = x.shape

    def body(x_ref, g_ref, sc_ref, sh_ref, after_ref, h_ref):
        xv = x_ref[...]
        rstd = lax.rsqrt(jnp.mean(xv * xv, axis=-1, keepdims=True) + EPS)
        y = xv * rstd * g_ref[...]
        h_ref[...] = (y * (1.0 + sc_ref[...]) + sh_ref[...]).astype(h_ref.dtype)

    return pl.pallas_call(
        body, name=name, grid=(S // tr,), out_shape=jax.ShapeDtypeStruct((S, D), _MXU_DTYPE),
        in_specs=[_row_spec(tr, D), _vec_spec(D), _vec_spec(D), _vec_spec(D), ANY], out_specs=_row_spec(tr, D),
        compiler_params=_cp("parallel"))(x, g, sc, sh, after)


def _acc_rows(ref, val, first):
    s = jnp.sum(val, axis=0, keepdims=True)

    @pl.when(first)
    def _():
        ref[...] = s

    @pl.when(jnp.logical_not(first))
    def _():
        ref[...] += s


def _rmsmod_bwd(name, x, g, sc, dh, dres, after, tr=256):
    S, D = x.shape

    def body(x_ref, g_ref, sc_ref, dh_ref, dres_ref, after_ref, dx_ref, dg_ref, dsc_ref, dsh_ref):
        first = pl.program_id(0) == 0
        xv, dh_v, gv = x_ref[...], dh_ref[...], g_ref[...]
        rstd = lax.rsqrt(jnp.mean(xv * xv, axis=-1, keepdims=True) + EPS)
        xhat = xv * rstd
        _acc_rows(dsh_ref, dh_v, first)
        _acc_rows(dsc_ref, dh_v * (xhat * gv), first)
        dyg = dh_v * (1.0 + sc_ref[...])
        _acc_rows(dg_ref, dyg * xhat, first)
        dxhat = dyg * gv
        dx_ref[...] = dres_ref[...] + rstd * (dxhat - xhat * jnp.mean(dxhat * xhat, axis=-1, keepdims=True))

    vec = jax.ShapeDtypeStruct((1, D), F32)
    return pl.pallas_call(
        body, name=name, grid=(S // tr,), out_shape=(jax.ShapeDtypeStruct((S, D), F32), vec, vec, vec),
        in_specs=[_row_spec(tr, D), _vec_spec(D), _vec_spec(D), _row_spec(tr, D), _row_spec(tr, D), ANY],
        out_specs=(_row_spec(tr, D), _vec_spec(D), _vec_spec(D), _vec_spec(D)),
        compiler_params=_cp("arbitrary"))(x, g, sc, dh, dres, after)


def _loss_head(x, g, target, tr=256):
    S, D = x.shape

    def body(x_ref, g_ref, t_ref, loss_ref, dx_ref, dg_ref):
        first = pl.program_id(0) == 0
        xv, gv = x_ref[...], g_ref[...]
        rstd = lax.rsqrt(jnp.mean(xv * xv, axis=-1, keepdims=True) + EPS)
        xhat = xv * rstd
        err = xhat * gv - t_ref[...]
        part = 0.5 * jnp.sum(jnp.mean(err * err, axis=-1, keepdims=True), axis=0, keepdims=True)

        @pl.when(first)
        def _():
            loss_ref[...] = part

        @pl.when(jnp.logical_not(first))
        def _():
            loss_ref[...] += part

        dout = err * (1.0 / D)
        _acc_rows(dg_ref, dout * xhat, first)
        dxhat = dout * gv
        dx_ref[...] = rstd * (dxhat - xhat * jnp.mean(dxhat * xhat, axis=-1, keepdims=True))

    return pl.pallas_call(
        body, name="loss_head", grid=(S // tr,),
        out_shape=(jax.ShapeDtypeStruct((1, 1), F32), jax.ShapeDtypeStruct((S, D), F32), jax.ShapeDtypeStruct((1, D), F32)),
        in_specs=[_row_spec(tr, D), _vec_spec(D), _row_spec(tr, D)],
        out_specs=(pl.BlockSpec((1, 1), lambda i: (0, 0)), _row_spec(tr, D), _vec_spec(D)),
        compiler_params=_cp("arbitrary"))(x, g, target)


def _gate_bwd(name, dx, y, gate, tr=256):
    S, D = dx.shape

    def body(dx_ref, y_ref, g_ref, dy_ref, dg_ref):
        dxv = dx_ref[...]
        dy_ref[...] = (g_ref[...] * dxv).astype(dy_ref.dtype)
        _acc_rows(dg_ref, dxv * y_ref[...], pl.program_id(0) == 0)

    return pl.pallas_call(
        body, name=name, grid=(S // tr,),
        out_shape=(jax.ShapeDtypeStruct((S, D), _MXU_DTYPE), jax.ShapeDtypeStruct((1, D), F32)),
        in_specs=[_row_spec(tr, D), _row_spec(tr, D), _vec_spec(D)], out_specs=(_row_spec(tr, D), _vec_spec(D)),
        compiler_params=_cp("arbitrary"))(dx, y, gate)


def _shift_down(v, k):
    t = lax.broadcasted_iota(jnp.int32, v.shape, 0)
    return jnp.where(t >= k, pltpu.roll(v, k, axis=0), 0.0)


def _shift_up(v, k):
    n = v.shape[0]
    t = lax.broadcasted_iota(jnp.int32, v.shape, 0)
    return jnp.where(t < n - k, pltpu.roll(v, n - k, axis=0), 0.0)


def _window_sum(p, w, shift):
    s, k = p, 1
    while k < w:
        s = s + shift(s, k)
        k *= 2
    return s


def _pool_count(shape, w):
    t = lax.broadcasted_iota(jnp.int32, shape, 0)
    return jnp.minimum(t + 1, w).astype(F32)


def _ab_specs(S):
    zs = [pl.BlockSpec((None, S, 128), functools.partial(lambda g, q: (2 * q + g // 2, 0, g % 2), q=q)) for q in range(4)]
    return zs


def _ab_mix_fwd(z8, conv_w, mix_w, scale):
    S = z8.shape[1]

    def body(b_ref, c_ref, a_ref, p_ref, w_ref, mix_ref, sc_ref, y_ref):
        g = pl.program_id(0)
        cg = c_ref[...] * a_ref[...]
        w = w_ref[...]
        conv = w[0:1] * _shift_down(cg, 2) + w[1:2] * _shift_down(cg, 1) + w[2:3] * cg
        y_ref[0] = (b_ref[...] * conv).astype(y_ref.dtype)
        for gg, win in enumerate(POOL_WINDOWS):
            @pl.when(g == gg)
            def _(win=win):
                p = p_ref[...]
                pooled = _window_sum(p, win, _shift_down) / _pool_count(p.shape, win) - p
                y_ref[1] = (_dot(pooled, mix_ref[...], "nn") * sc_ref[...]).astype(y_ref.dtype)

    return pl.pallas_call(
        body, name="ab_mix_fwd", grid=(4,), out_shape=jax.ShapeDtypeStruct((2, S, 512), _MXU_DTYPE),
        in_specs=_ab_specs(S) + [pl.BlockSpec((3, 128), lambda g: (0, g)), pl.BlockSpec((None, 128, 128), lambda g: (g, 0, 0)),
                                 pl.BlockSpec((1, 128), lambda g: (0, g))],
        out_specs=pl.BlockSpec((2, S, 128), lambda g: (0, 0, g)), compiler_params=_cp("parallel"))(z8, z8, z8, z8, conv_w, mix_w, scale)


def _ab_mix_bwd(z8, dycat2, conv_w, mix_w, scale, after):
    S = z8.shape[1]

    def body(b_ref, c_ref, a_ref, p_ref, dy_ref, w_ref, mix_ref, sc_ref, after_ref, dz_ref, dw_ref, dmix_ref, dsc_ref):
        g = pl.program_id(0)
        bv, cv, av, w = b_ref[...], c_ref[...], a_ref[...], w_ref[...]
        dya = dy_ref[0]
        cg = cv * av
        cg1, cg2 = _shift_down(cg, 1), _shift_down(cg, 2)
        conv = w[0:1] * cg2 + w[1:2] * cg1 + w[2:3] * cg
        dz_ref[0] = (dya * conv).astype(dz_ref.dtype)
        dconv = dya * bv
        dcg = w[2:3] * dconv + w[1:2] * _shift_up(dconv, 1) + w[0:1] * _shift_up(dconv, 2)
        dz_ref[1] = (dcg * av).astype(dz_ref.dtype)
        dz_ref[2] = (dcg * cv).astype(dz_ref.dtype)
        dw_ref[0:1, :] = jnp.sum(dconv * cg2, axis=0, keepdims=True)
        dw_ref[1:2, :] = jnp.sum(dconv * cg1, axis=0, keepdims=True)
        dw_ref[2:3, :] = jnp.sum(dconv * cg, axis=0, keepdims=True)
        for gg, win in enumerate(POOL_WINDOWS):
            @pl.when(g == gg)
            def _(win=win):
                p, dyb, mix = p_ref[...], dy_ref[1], mix_ref[...]
                cnt = _pool_count(p.shape, win)
                pooled = _window_sum(p, win, _shift_down) / cnt - p
                dsc_ref[...] = jnp.sum(dyb * _dot(pooled, mix, "nn"), axis=0, keepdims=True)
                dmixed = dyb * sc_ref[...]
                dmix_ref[...] = _dot(pooled, dmixed, "tn")
                dpooled = _dot(dmixed, mix, "nt")
                dz_ref[3] = (_window_sum(dpooled / cnt, win, _shift_up) - dpooled).astype(dz_ref.dtype)

    return pl.pallas_call(
        body, name="ab_mix_bwd", grid=(4,),
        out_shape=(jax.ShapeDtypeStruct((4, 2, S, 256), _MXU_DTYPE), jax.ShapeDtypeStruct((3, 512), F32),
                   jax.ShapeDtypeStruct((4, 128, 128), F32), jax.ShapeDtypeStruct((1, 512), F32)),
        in_specs=_ab_specs(S) + [pl.BlockSpec((2, S, 128), lambda g: (0, 0, g)), pl.BlockSpec((3, 128), lambda g: (0, g)),
                                 pl.BlockSpec((None, 128, 128), lambda g: (g, 0, 0)), pl.BlockSpec((1, 128), lambda g: (0, g)), ANY],
        out_specs=(pl.BlockSpec((4, None, S, 128), lambda g: (0, g // 2, 0, g % 2)), pl.BlockSpec((3, 128), lambda g: (0, g)),
                   pl.BlockSpec((None, 128, 128), lambda g: (g, 0, 0)), pl.BlockSpec((1, 128), lambda g: (0, g))),
        compiler_params=_cp("parallel"))(z8, z8, z8, z8, dycat2, conv_w, mix_w, scale, after)


HALO = 16


def _ffn_specs(S, n, tr):
    nb = S // HALO
    tile = pl.BlockSpec((2, None, tr, n), lambda j, i: (0, j, i, 0))
    prev = pl.BlockSpec((2, None, HALO, n), lambda j, i: (0, j, jnp.maximum(i * (tr // HALO) - 1, 0), 0))
    nxt = pl.BlockSpec((2, None, HALO, n), lambda j, i: (0, j, jnp.minimum((i + 1) * (tr // HALO), nb - 1), 0))
    cw = pl.BlockSpec((2, None, 3, n), lambda j, i: (0, j, 0, 0))
    return tile, prev, nxt, cw


def _conv_rows(ext, w, lo, tr):
    n = ext.shape[0]
    return (w[0:1] * pltpu.roll(ext, 2, axis=0)[lo:lo + tr] + w[1:2] * pltpu.roll(ext, 1, axis=0)[lo:lo + tr]
            + w[2:3] * ext[lo:lo + tr])


def _ffn_gate_fwd(name, u24, cw24, tr=256):
    _, J, S, n = u24.shape
    tile, prev, _, cw = _ffn_specs(S, n, tr)

    def body(u_ref, up_ref, w_ref, a_ref):
        keep = (pl.program_id(1) > 0).astype(F32)
        z = []
        for h in range(2):
            ext = jnp.concatenate([up_ref[h].astype(F32) * keep, u_ref[h].astype(F32)], axis=0)
            z.append(_conv_rows(ext, w_ref[h], HALO, tr))
        a_ref[...] = (_silu(z[0]) * z[1]).astype(a_ref.dtype)

    return pl.pallas_call(
        body, name=name, grid=(J, S // tr), out_shape=jax.ShapeDtypeStruct((J, S, n), _MXU_DTYPE),
        in_specs=[tile, prev, cw], out_specs=pl.BlockSpec((None, tr, n), lambda j, i: (j, i, 0)),
        compiler_params=_cp("parallel", "parallel"))(u24, u24, cw24)


def _ffn_gate_bwd(name, u24, cw24, da4, after, tr=256):
    _, J, S, n = u24.shape
    tile, prev, nxt, cw = _ffn_specs(S, n, tr)
    nb = S // HALO
    ext_rows = tr + 2 * HALO

    def body(u_ref, up_ref, un_ref, w_ref, da_ref, dan_ref, after_ref, du_ref, dcw_ref):
        i = pl.program_id(1)
        first = i == 0
        keep_prev = (i > 0).astype(F32)
        keep_next = (i < S // tr - 1).astype(F32)
        ext = [jnp.concatenate([up_ref[h].astype(F32) * keep_prev, u_ref[h].astype(F32), un_ref[h].astype(F32)], axis=0)
               for h in range(2)]
        w = [w_ref[h] for h in range(2)]
        zg = _conv_rows(ext[0], w[0], HALO, tr + HALO)
        zu = _conv_rows(ext[1], w[1], HALO, tr + HALO)
        da = jnp.concatenate([da_ref[...].astype(F32), dan_ref[...].astype(F32) * keep_next], axis=0)
        sg = jax.nn.sigmoid(zg)
        dz = [da * zu * (sg * (1.0 + zg * (1.0 - sg))), da * (zg * sg)]
        m = tr + HALO
        for h in range(2):
            d = dz[h]
            du = w[h][2:3] * d[:tr] + w[h][1:2] * pltpu.roll(d, m - 1, axis=0)[:tr] + w[h][0:1] * pltpu.roll(d, m - 2, axis=0)[:tr]
            du_ref[h] = du.astype(du_ref.dtype)
            dt = d[:tr]
            e = ext[h]
            parts = [jnp.sum(dt * pltpu.roll(e, 2, axis=0)[HALO:HALO + tr], axis=0, keepdims=True),
                     jnp.sum(dt * pltpu.roll(e, 1, axis=0)[HALO:HALO + tr], axis=0, keepdims=True),
                     jnp.sum(dt * e[HALO:HALO + tr], axis=0, keepdims=True)]
            for k in range(3):
                @pl.when(first)
                def _(k=k, h=h):
                    dcw_ref[h, k:k + 1, :] = parts[k]

                @pl.when(jnp.logical_not(first))
                def _(k=k, h=h):
                    dcw_ref[h, k:k + 1, :] += parts[k]

    da_tile = pl.BlockSpec((None, tr, n), lambda j, i: (j, i, 0))
    da_next = pl.BlockSpec((None, HALO, n), lambda j, i: (j, jnp.minimum((i + 1) * (tr // HALO), nb - 1), 0))
    return pl.pallas_call(
        body, name=name, grid=(J, S // tr),
        out_shape=(jax.ShapeDtypeStruct((2, J, S, n), _MXU_DTYPE), jax.ShapeDtypeStruct((2, J, 3, n), F32)),
        in_specs=[tile, prev, nxt, cw, da_tile, da_next, ANY], out_specs=(tile, cw),
        compiler_params=_cp("parallel", "arbitrary"))(u24, u24, u24, cw24, da4, da4, after)


def _rms_rows(v, g):
    rstd = lax.rsqrt(jnp.mean(v * v, axis=-1, keepdims=True) + EPS)
    return v * rstd * g


def _rms_rows_bwd(v, g, dy):
    rstd = lax.rsqrt(jnp.mean(v * v, axis=-1, keepdims=True) + EPS)
    vhat = v * rstd
    dvhat = dy * g
    return rstd * (dvhat - vhat * jnp.mean(dvhat * vhat, axis=-1, keepdims=True)), dy * vhat


def _mla_prep_fwd(z, qg, kvg, tr=256):
    S = z.shape[0]

    def body(q_ref, kv_ref, qg_ref, kvg_ref, qn_ref, kvn_ref):
        qn_ref[...] = _rms_rows(q_ref[...], qg_ref[...]).astype(qn_ref.dtype)
        kvn_ref[...] = _rms_rows(kv_ref[...], kvg_ref[...]).astype(kvn_ref.dtype)

    return pl.pallas_call(
        body, name="mla_prep_fwd", grid=(S // tr,),
        out_shape=(jax.ShapeDtypeStruct((S, 256), _MXU_DTYPE), jax.ShapeDtypeStruct((S, 128), _MXU_DTYPE)),
        in_specs=[pl.BlockSpec((tr, 256), lambda i: (i, 0)), pl.BlockSpec((tr, 128), lambda i: (i, 2)), _vec_spec(256), _vec_spec(128)],
        out_specs=(_row_spec(tr, 256), _row_spec(tr, 128)), compiler_params=_cp("parallel"))(z, z, qg, kvg)


def _mla_prep_bwd(z, qg, kvg, dqn, dkvn, dkpe, duv, tr=256):
    S = z.shape[0]

    def body(q_ref, kv_ref, qg_ref, kvg_ref, dqn_ref, dkvn_ref, dkpe_ref, duv_ref, dz_ref, dqg_ref, dkvg_ref):
        first = pl.program_id(0) == 0
        dq, dqg = _rms_rows_bwd(q_ref[...], qg_ref[...], dqn_ref[...])
        dkv, dkvg = _rms_rows_bwd(kv_ref[...], kvg_ref[...], dkvn_ref[...])
        _acc_rows(dqg_ref, dqg, first)
        _acc_rows(dkvg_ref, dkvg, first)
        dz_ref[:, 0:256] = dq.astype(dz_ref.dtype)
        dz_ref[:, 256:384] = dkv.astype(dz_ref.dtype)
        dz_ref[:, 384:512] = dkpe_ref[...].astype(dz_ref.dtype)
        dz_ref[:, 512:1536] = duv_ref[...].astype(dz_ref.dtype)

    return pl.pallas_call(
        body, name="mla_prep_bwd", grid=(S // tr,),
        out_shape=(jax.ShapeDtypeStruct((S, 1536), _MXU_DTYPE), jax.ShapeDtypeStruct((1, 256), F32), jax.ShapeDtypeStruct((1, 128), F32)),
        in_specs=[pl.BlockSpec((tr, 256), lambda i: (i, 0)), pl.BlockSpec((tr, 128), lambda i: (i, 2)), _vec_spec(256), _vec_spec(128),
                  _row_spec(tr, 256), _row_spec(tr, 128), _row_spec(tr, 128), _row_spec(tr, 1024)],
        out_specs=(_row_spec(tr, 1536), _vec_spec(256), _vec_spec(128)),
        compiler_params=_cp("arbitrary"))(z, z, qg, kvg, dqn, dkvn, dkpe, duv)


def _rope(v, cos, sa, sb):
    return v * cos + pltpu.roll(v, 112, axis=1) * sa + pltpu.roll(v, 16, axis=1) * sb


def _rope_t(d, cos, sa, sb):
    return d * cos + pltpu.roll(d * sa, 16, axis=1) + pltpu.roll(d * sb, 112, axis=1)


def _rope_fwd(qraw, kvall, z, cosq, cosk, sa, sb, tr=256):
    S = qraw.shape[0]

    def body(q_ref, k_ref, v_ref, kpe_ref, cq_ref, ck_ref, sa_ref, sb_ref, qo_ref, ko_ref, vo_ref):
        cq, ck, sa_v, sb_v = cq_ref[...], ck_ref[...], sa_ref[...], sb_ref[...]
        kpe = _rope(kpe_ref[...], ck, sa_v, sb_v)
        for h in range(8):
            cols = slice(128 * h, 128 * h + 128)
            qo_ref[:, cols] = _rope(q_ref[:, cols], cq, sa_v, sb_v).astype(qo_ref.dtype)
            ko_ref[:, cols] = (k_ref[:, cols] + kpe).astype(ko_ref.dtype)
        vo_ref[...] = v_ref[...].astype(vo_ref.dtype)

    tab = _row_spec(tr, 128)
    return pl.pallas_call(
        body, name="rope_fwd", grid=(S // tr,),
        out_shape=(jax.ShapeDtypeStruct((S, 1024), _MXU_DTYPE), jax.ShapeDtypeStruct((S, 1024), _MXU_DTYPE),
                   jax.ShapeDtypeStruct((S, 512), _MXU_DTYPE)),
        in_specs=[_row_spec(tr, 1024), pl.BlockSpec((tr, 1024), lambda i: (i, 0)), pl.BlockSpec((tr, 512), lambda i: (i, 2)),
                  pl.BlockSpec((tr, 128), lambda i: (i, 3)), tab, tab, tab, tab],
        out_specs=(_row_spec(tr, 1024), _row_spec(tr, 1024), _row_spec(tr, 512)),
        compiler_params=_cp("parallel"))(qraw, kvall, kvall, z, cosq, cosk, sa, sb)


def _rope_bwd(dq, dk, dv, cosq, cosk, sa, sb, tr=256):
    S = dq.shape[0]

    def body(dq_ref, dk_ref, dv_ref, cq_ref, ck_ref, sa_ref, sb_ref, dqo_ref, dkv_ref, dkpe_ref):
        cq, ck, sa_v, sb_v = cq_ref[...], ck_ref[...], sa_ref[...], sb_ref[...]
        tot = jnp.zeros((tr, 128), F32)
        for h in range(8):
            cols = slice(128 * h, 128 * h + 128)
            dqo_ref[:, cols] = _rope_t(dq_ref[:, cols], cq, sa_v, sb_v).astype(dqo_ref.dtype)
            dkh = dk_ref[:, cols]
            tot = tot + dkh
            dkv_ref[:, cols] = dkh.astype(dkv_ref.dtype)
        dkv_ref[:, 1024:1536] = dv_ref[...].astype(dkv_ref.dtype)
        dkpe_ref[...] = _rope_t(tot, ck, sa_v, sb_v)

    tab = _row_spec(tr, 128)
    return pl.pallas_call(
        body, name="rope_bwd", grid=(S // tr,),
        out_shape=(jax.ShapeDtypeStruct((S, 1024), _MXU_DTYPE), jax.ShapeDtypeStruct((S, 1536), _MXU_DTYPE),
                   jax.ShapeDtypeStruct((S, 128), F32)),
        in_specs=[_row_spec(tr, 1024), _row_spec(tr, 1024), _row_spec(tr, 512), tab, tab, tab, tab],
        out_specs=(_row_spec(tr, 1024), _row_spec(tr, 1536), _row_spec(tr, 128)),
        compiler_params=_cp("parallel"))(dq, dk, dv, cosq, cosk, sa, sb)


NEG = -1e30


def _attn_fwd(q, k, v, tq=256, tk=256):
    S = q.shape[0]
    assert tq == tk

    def body(q_ref, k_ref, v_ref, o_ref, lse_ref):
        i = pl.program_id(1)
        qs = [q_ref[:, 0:128], q_ref[:, 128:256]]

        def step(kb, carry, diagonal=False):
            start = pl.multiple_of(kb * tk, tk)
            vv = v_ref[pl.ds(start, tk), :]
            out = []
            for h in range(2):
                m, l, acc = carry[3 * h:3 * h + 3]
                s = _dot(qs[h], k_ref[pl.ds(start, tk), 128 * h:128 * h + 128], "nt") * ATTN_SCALE
                if diagonal:
                    s = jnp.where(below, s, NEG)
                m_new = jnp.maximum(m, jnp.max(s, axis=-1, keepdims=True))
                alpha = jnp.exp(m - m_new)
                p = jnp.exp(s - m_new)
                out += [m_new, alpha * l + jnp.sum(p, axis=-1, keepdims=True), alpha * acc + _dot(p, vv, "nn")]
            return tuple(out)

        below = lax.broadcasted_iota(jnp.int32, (tq, tk), 1) <= lax.broadcasted_iota(jnp.int32, (tq, tk), 0)
        init = (jnp.full((tq, 1), NEG, F32), jnp.zeros((tq, 1), F32), jnp.zeros((tq, 128), F32)) * 2
        ma, la, acca, mb, lb, accb = step(i, lax.fori_loop(0, i, step, init), diagonal=True)
        lane = lax.broadcasted_iota(jnp.int32, (tq, 128), 1)
        o_ref[...] = jnp.where(lane < 64, acca / la, accb / lb)
        lse_ref[...] = jnp.where(lane < 64, ma + jnp.log(la), mb + jnp.log(lb))

    return pl.pallas_call(
        body, name="attn_fwd", grid=(4, S // tq),
        out_shape=(jax.ShapeDtypeStruct((S, 512), F32), jax.ShapeDtypeStruct((4, S, 128), F32)),
        in_specs=[pl.BlockSpec((tq, 256), lambda p, i: (i, p)), pl.BlockSpec((S, 256), lambda p, i: (0, p)),
                  pl.BlockSpec((S, 128), lambda p, i: (0, p))],
        out_specs=(pl.BlockSpec((tq, 128), lambda p, i: (i, p)), pl.BlockSpec((None, tq, 128), lambda p, i: (p, i, 0))),
        compiler_params=_cp("parallel", "parallel"))(q, k, v)


def _attn_bwd(q, k, v, o, lse, dycat2, tq=256, tk=256):
    S = q.shape[0]
    assert tq == tk

    def body(q_ref, k_ref, v_ref, o_ref, lse_ref, do_ref, dq_ref, dk_ref, dv_ref):
        j = pl.program_id(1)

        @pl.when(j == 0)
        def _():
            dq_ref[...] = jnp.zeros_like(dq_ref)

        below = lax.broadcasted_iota(jnp.int32, (tq, tk), 1) <= lax.broadcasted_iota(jnp.int32, (tq, tk), 0)
        lane = lax.broadcasted_iota(jnp.int32, (tq, 128), 1)
        ks = [k_ref[:, 0:128], k_ref[:, 128:256]]
        vv = v_ref[...]

        def step(qb, carry, diagonal=False):
            dka, dkb, dvp = carry
            start = pl.multiple_of(qb * tq, tq)
            rows = pl.ds(start, tq)
            do, lse_v = do_ref[rows, :], lse_ref[rows, :]
            prod = do * o_ref[rows, :]
            dks = [dka, dkb]
            for h in range(2):
                mine = (lane < 64) if h == 0 else (lane >= 64)
                delta = jnp.sum(jnp.where(mine, prod, 0.0), axis=-1, keepdims=True)
                do_h = jnp.where(mine, do, 0.0)
                qh = q_ref[rows, 128 * h:128 * h + 128]
                s = _dot(qh, ks[h], "nt") * ATTN_SCALE
                p = jnp.exp(s - lse_v[:, 64 * h:64 * h + 1])
                if diagonal:
                    p = jnp.where(below, p, 0.0)
                dvp = dvp + _dot(p, do_h, "tn")
                ds = p * (_dot(do_h, vv, "nt") - delta) * ATTN_SCALE
                dq_ref[rows, 128 * h:128 * h + 128] += _dot(ds, ks[h], "nn")
                dks[h] = dks[h] + _dot(ds, qh, "tn")
            return dks[0], dks[1], dvp

        zero = jnp.zeros((tk, 128), F32)
        dka, dkb, dvp = lax.fori_loop(j + 1, S // tq, step, step(j, (zero, zero, zero), diagonal=True))
        dk_ref[:, 0:128] = dka
        dk_ref[:, 128:256] = dkb
        dv_ref[...] = dvp

    return pl.pallas_call(
        body, name="attn_bwd", grid=(4, S // tk),
        out_shape=(jax.ShapeDtypeStruct((S, 1024), F32), jax.ShapeDtypeStruct((S, 1024), F32), jax.ShapeDtypeStruct((S, 512), F32)),
        in_specs=[pl.BlockSpec((S, 256), lambda p, j: (0, p)), pl.BlockSpec((tk, 256), lambda p, j: (j, p)),
                  pl.BlockSpec((tk, 128), lambda p, j: (j, p)), pl.BlockSpec((S, 128), lambda p, j: (0, p)),
                  pl.BlockSpec((None, S, 128), lambda p, j: (p, 0, 0)), pl.BlockSpec((None, S, 128), lambda p, j: (0, 0, p))],
        out_specs=(pl.BlockSpec((S, 256), lambda p, j: (0, p)), pl.BlockSpec((tk, 256), lambda p, j: (j, p)),
                   pl.BlockSpec((tk, 128), lambda p, j: (j, p))),
        compiler_params=_cp("parallel", "arbitrary"))(q, k, v, o, lse, dycat2)


CHUNK = 128
GELU_C = math.sqrt(2.0 / math.pi)


def _gelu(v):
    t = jnp.tanh(GELU_C * (v + 0.044715 * (v * v * v)))
    return v * (0.5 * (1.0 + t)), t


def _gelu_grad(v, t):
    return 0.5 * (1.0 + t) + v * (0.5 * (1.0 - t * t) * GELU_C * (1.0 + 3.0 * 0.044715 * v * v))


def _tril(w):
    r = lax.broadcasted_iota(jnp.int32, w.shape, 0)
    c = lax.broadcasted_iota(jnp.int32, w.shape, 1)
    return jnp.where(c <= r, w, 0.0)


def _layer_norm(v, g, b):
    xc = v - jnp.mean(v, axis=-1, keepdims=True)
    rstd = lax.rsqrt(jnp.mean(xc * xc, axis=-1, keepdims=True) + EPS)
    xhat = xc * rstd
    return xhat * g + b, xhat, rstd


def _sgu_fwd(z, o, ln_g, ln_b, w_s, b_st, tr=256):
    S = z.shape[0]

    def body(u_ref, v_ref, o_ref, g_ref, b_ref, ws_ref, bs_ref, y_ref):
        gu, _ = _gelu(u_ref[...])
        gv, _ = _gelu(v_ref[...])
        vln, _, _ = _layer_norm(gv, g_ref[...], b_ref[...])
        y_ref[0] = o_ref[...].astype(y_ref.dtype)
        for g in range(4):
            wt = _tril(ws_ref[g])
            cols = slice(128 * g, 128 * g + 128)
            for ch in range(tr // CHUNK):
                rows = slice(CHUNK * ch, CHUNK * ch + CHUNK)
                mixed = _dot(wt, vln[rows, cols], "nn") + bs_ref[:, g:g + 1]
                y_ref[1, rows, cols] = (gu[rows, cols] * mixed).astype(y_ref.dtype)

    return pl.pallas_call(
        body, name="sgu_fwd", grid=(S // tr,), out_shape=jax.ShapeDtypeStruct((2, S, 512), _MXU_DTYPE),
        in_specs=[pl.BlockSpec((tr, 512), lambda i: (i, 1)), pl.BlockSpec((tr, 512), lambda i: (i, 2)), _row_spec(tr, 512),
                  _vec_spec(512), _vec_spec(512), pl.BlockSpec((4, 128, 128), lambda i: (0, 0, 0)), pl.BlockSpec((128, 4), lambda i: (0, 0))],
        out_specs=pl.BlockSpec((2, tr, 512), lambda i: (0, i, 0)), compiler_params=_cp("parallel"))(z, z, o, ln_g, ln_b, w_s, b_st)


def _sgu_bwd(z, dycat2, ln_g, ln_b, w_s, b_st, tr=256):
    S = z.shape[0]

    def body(u_ref, v_ref, dy_ref, g_ref, b_ref, ws_ref, bs_ref, duv_ref, dg_ref, db_ref, dws_ref, dbs_ref):
        first = pl.program_id(0) == 0
        u_pre, v_pre = u_ref[...], v_ref[...]
        gu, tu = _gelu(u_pre)
        gv, tv = _gelu(v_pre)
        gain = g_ref[...]
        vln, xhat, rstd = _layer_norm(gv, gain, b_ref[...])

        @pl.when(first)
        def _():
            dws_ref[...] = jnp.zeros_like(dws_ref)
            dbs_ref[...] = jnp.zeros_like(dbs_ref)

        dvln_cols = []
        for g in range(4):
            wt = _tril(ws_ref[g])
            cols = slice(128 * g, 128 * g + 128)
            dmixed_sum = jnp.zeros((CHUNK, 128), F32)
            dw = jnp.zeros((CHUNK, CHUNK), F32)
            dvln_rows = []
            for ch in range(tr // CHUNK):
                rows = slice(CHUNK * ch, CHUNK * ch + CHUNK)
                vt = vln[rows, cols]
                mixed = _dot(wt, vt, "nn") + bs_ref[:, g:g + 1]
                dyd = dy_ref[rows, cols]
                duv_ref[rows, cols] = (dyd * mixed * _gelu_grad(u_pre[rows, cols], tu[rows, cols])).astype(duv_ref.dtype)
                dmixed = dyd * gu[rows, cols]
                dmixed_sum = dmixed_sum + dmixed
                dw = dw + _dot(dmixed, vt, "nt")
                dvln_rows.append(_dot(wt, dmixed, "tn"))
            dws_ref[g] += _tril(dw)
            dbs_ref[g:g + 1, :] += jnp.sum(dmixed_sum.T, axis=0, keepdims=True)
            dvln_cols.append(jnp.concatenate(dvln_rows, axis=0))
        dvln = jnp.concatenate(dvln_cols, axis=1)
        _acc_rows(dg_ref, dvln * xhat, first)
        _acc_rows(db_ref, dvln, first)
        dxhat = dvln * gain
        dgv = rstd * (dxhat - jnp.mean(dxhat, axis=-1, keepdims=True) - xhat * jnp.mean(dxhat * xhat, axis=-1, keepdims=True))
        duv_ref[:, 512:1024] = (dgv * _gelu_grad(v_pre, tv)).astype(duv_ref.dtype)

    return pl.pallas_call(
        body, name="sgu_bwd", grid=(S // tr,),
        out_shape=(jax.ShapeDtypeStruct((S, 1024), _MXU_DTYPE), jax.ShapeDtypeStruct((1, 512), F32), jax.ShapeDtypeStruct((1, 512), F32),
                   jax.ShapeDtypeStruct((4, 128, 128), F32), jax.ShapeDtypeStruct((4, 128), F32)),
        in_specs=[pl.BlockSpec((tr, 512), lambda i: (i, 1)), pl.BlockSpec((tr, 512), lambda i: (i, 2)),
                  pl.BlockSpec((None, tr, 512), lambda i: (1, i, 0)), _vec_spec(512), _vec_spec(512),
                  pl.BlockSpec((4, 128, 128), lambda i: (0, 0, 0)), pl.BlockSpec((128, 4), lambda i: (0, 0))],
        out_specs=(_row_spec(tr, 1024), _vec_spec(512), _vec_spec(512), pl.BlockSpec((4, 128, 128), lambda i: (0, 0, 0)),
                   pl.BlockSpec((4, 128), lambda i: (0, 0))),
        compiler_params=_cp("arbitrary"))(z, z, dycat2, ln_g, ln_b, w_s, b_st)


def _sum_parts(name, parts, tr=512):
    P, R, C = parts.shape
    tr = _tile(R, tr) if R % 8 == 0 else R

    def body(p_ref, o_ref):
        g = p_ref[0]
        for k in range(1, P):
            g = g + p_ref[k]
        o_ref[...] = g

    return pl.pallas_call(
        body, name=name, grid=(R // tr,), out_shape=jax.ShapeDtypeStruct((R, C), F32),
        in_specs=[pl.BlockSpec((P, tr, C), lambda i: (0, i, 0))], out_specs=_row_spec(tr, C),
        compiler_params=_cp("parallel"))(parts)


def _adamw_math(w, m, v, g):
    c1 = 1.0 / (1.0 - ADAM_B1 ** ADAM_STEP)
    c2 = 1.0 / (1.0 - ADAM_B2 ** ADAM_STEP)
    m2 = ADAM_B1 * m + (1.0 - ADAM_B1) * g
    v2 = ADAM_B2 * v + (1.0 - ADAM_B2) * (g * g)
    return -ADAM_LR * ((m2 * c1) / (jnp.sqrt(v2 * c2) + ADAM_EPS) + ADAM_WD * w), m2, v2


def _adamw_small(name, params, parts):
    n = len(params)

    def body(*refs):
        ins, outs = refs[:4 * n], refs[4 * n:]
        for i in range(n):
            w_ref, m_ref, v_ref, p_ref = ins[4 * i:4 * i + 4]
            g = p_ref[0]
            for k in range(1, N_DEV):
                g = g + p_ref[k]
            delta, m2, v2 = _adamw_math(w_ref[...], m_ref[...], v_ref[...], g)
            outs[4 * i][...] = g
            outs[4 * i + 1][...] = delta
            outs[4 * i + 2][...] = m2
            outs[4 * i + 3][...] = v2

    flat = [a for (w, m, v), p in zip(params, parts) for a in (w, m, v, p)]
    out = pl.pallas_call(
        body, name=name, out_shape=[jax.ShapeDtypeStruct(w.shape, F32) for (w, _, _) in params for _ in range(4)],
        compiler_params=pltpu.CompilerParams(vmem_limit_bytes=_VMEM_LIMIT))(*flat)
    return [out[4 * i:4 * i + 4] for i in range(n)]


ADAMW_BLOCK_BYTES = 36 * 2 ** 20


def _adamw(name, w, m, v, parts):
    L, R, C = w.shape
    P = parts[0].shape[0]
    row_bytes = 2 * C * (7 * 4 + P * parts[0].dtype.itemsize)
    tr = R
    if R * row_bytes > ADAMW_BLOCK_BYTES:
        tr = next(t for t in (1024, 512, 256, 128, 64, 32, 16) if R % t == 0 and t * row_bytes <= ADAMW_BLOCK_BYTES)
    nr = R // tr
    c1 = 1.0 / (1.0 - ADAM_B1 ** ADAM_STEP)
    c2 = 1.0 / (1.0 - ADAM_B2 ** ADAM_STEP)

    def body(w_ref, m_ref, v_ref, *rest):
        p_refs, (g_ref, d_ref, mo_ref, vo_ref) = rest[:L], rest[L:]
        for ll in range(L):
            @pl.when(pl.program_id(0) == ll)
            def _(p_ref=p_refs[ll]):
                g = p_ref[0].astype(F32)
                for k in range(1, P):
                    g = g + p_ref[k].astype(F32)
                m2 = ADAM_B1 * m_ref[...] + (1.0 - ADAM_B1) * g
                v2 = ADAM_B2 * v_ref[...] + (1.0 - ADAM_B2) * (g * g)
                g_ref[...] = g
                mo_ref[...] = m2
                vo_ref[...] = v2
                d_ref[...] = -ADAM_LR * ((m2 * c1) / (jnp.sqrt(v2 * c2) + ADAM_EPS) + ADAM_WD * w_ref[...])

    def part_spec(ll):
        return pl.BlockSpec((P, tr, C), lambda l, i: (0, jnp.where(l == ll, i, jnp.where(l < ll, 0, nr - 1)), 0))

    full = pl.BlockSpec((None, tr, C), lambda l, i: (l, i, 0))
    sds = jax.ShapeDtypeStruct((L, R, C), F32)
    return pl.pallas_call(
        body, name=name, grid=(L, nr), out_shape=(sds, sds, sds, sds),
        in_specs=[full] * 3 + [part_spec(ll) for ll in range(L)],
        out_specs=(full,) * 4, compiler_params=_cp("arbitrary", "arbitrary"))(w, m, v, *parts)


def _rope_tables(positions):
    half = 16
    inv_freq = 10000.0 ** (-jnp.arange(half, dtype=F32) / half)
    ang = positions.astype(F32)[:, None] * inv_freq
    cos, sin = jnp.cos(ang), jnp.sin(ang)
    S = positions.shape[0]
    z16, z32, z64 = jnp.zeros((S, 16), F32), jnp.zeros((S, 32), F32), jnp.zeros((S, 64), F32)
    cosk = jnp.concatenate([z64, cos, cos, z32], axis=1)
    cosq = jnp.concatenate([jnp.ones((S, 64), F32), cos, cos, z32], axis=1)
    sa = jnp.concatenate([z64, -sin, z16, z32], axis=1)
    sb = jnp.concatenate([z64, z16, sin, z32], axis=1)
    return cosq, cosk, sa, sb


def _ffn_fwd(l, x, mod, n2g, w_up8, cw24, get_w_down4):
    sh, sc, gate = mod
    h = _rmsmod_fwd(f"ffn{l}_norm", x, n2g, sc, sh, n2g)
    u8 = _mm_cols(f"ffn{l}_up", h, w_up8, out_dtype=ACT_DTYPE, tm=1024)
    S, n = u8.shape[1], u8.shape[2]
    u24 = u8.reshape(2, 4, S, n)
    a4 = _ffn_gate_fwd(f"ffn{l}_gate", u24, cw24)
    w_down4 = get_w_down4(a4)
    f, x_new = _mm_rows_resid(f"ffn{l}_down", a4, w_down4, x, gate)
    return x_new, (x, h, u24, a4, f), w_down4


def _ffn_bwd(l, dx, saved, mod, n2g, w_up8, cw24, w_down4, me):
    sh, sc, gate = mod
    x, h, u24, a4, f = saved
    df, dgate = _gate_bwd(f"ffn{l}_gate_bwd", dx, f, gate)
    da4 = _mm_rows_dx(f"ffn{l}_down_dx", df, w_down4, out_dtype=ACT_DTYPE, tm=2048)
    dw_down4 = _mm_rows_dw(f"ffn{l}_down_dw", a4, df, out_dtype=WIRE_DTYPE)
    sent_down, token = _exchange_start(f"scatter_ffn{l}_down", [dw_down4.reshape(8, 352, dw_down4.shape[2])], True, dgate, me)
    du24, dcw24 = _ffn_gate_bwd(f"ffn{l}_act_bwd", u24, cw24, da4, token)
    du8 = du24.reshape((8,) + du24.shape[2:])
    dw_up8t = _mm_cols_dwt(f"ffn{l}_up_dw", h, du8, out_dtype=WIRE_DTYPE, tk=1024)
    sent_up, token = _exchange_start(f"scatter_ffn{l}_up", [dw_up8t], True, dcw24, me)
    dh = _mm_cols_dx(f"ffn{l}_up_dx", du8, w_up8, tm=1024, jb=4)
    dx_new, dn2g, dsc, dsh = _rmsmod_bwd(f"ffn{l}_norm_bwd", x, n2g, sc, dh, dx, token)
    return dx_new, dict(sent_up=sent_up, sent_down=sent_down, cw24=dcw24, n2g=dn2g, mod=(dsh, dsc, dgate))


def kernel(x, c, positions, ada_w, ada_b, norm1_g, norm2_g, ab_w_in, a_conv_w, b_mix_w, b_scale, ab_w_out, cd_w_in, c_q_norm_g, c_w_uq, c_kv_norm_g, c_w_ukv, d_ln_g, d_ln_b, d_w_s, d_b_s, cd_w_out, ffn_w_up, ffn_conv_w, ffn_w_down, final_norm_g, loss_target, m_ada_w, m_ada_b, m_norm1_g, m_norm2_g, m_ab_w_in, m_a_conv_w, m_b_mix_w, m_b_scale, m_ab_w_out, m_cd_w_in, m_c_q_norm_g, m_c_w_uq, m_c_kv_norm_g, m_c_w_ukv, m_d_ln_g, m_d_ln_b, m_d_w_s, m_d_b_s, m_cd_w_out, m_ffn_w_up, m_ffn_conv_w, m_ffn_w_down, m_final_norm_g, v_ada_w, v_ada_b, v_norm1_g, v_norm2_g, v_ab_w_in, v_a_conv_w, v_b_mix_w, v_b_scale, v_ab_w_out, v_cd_w_in, v_c_q_norm_g, v_c_w_uq, v_c_kv_norm_g, v_c_w_ukv, v_d_ln_g, v_d_ln_b, v_d_w_s, v_d_b_s, v_cd_w_out, v_ffn_w_up, v_ffn_conv_w, v_ffn_w_down, v_final_norm_g):
    S, D = x.shape[1], x.shape[2]
    me = 4 * lax.axis_index("x") + 2 * lax.axis_index("y") + lax.axis_index("c")
    x0, target = x[0], loss_target[0]
    W = _MXU_DTYPE

    small_shapes = [(1024,), (3, 64), (32,), (64,), (64,), (2, 3, 704)]
    (g0,) = _exchange("gather_small", [[_pack([c, a_conv_w, c_q_norm_g, d_ln_g, d_ln_b, ffn_conv_w])]], scatter=False)
    c_all, aconv_s, qg_s, lng_s, lnb_s, fcw_s = _unpack(g0[:, 0], small_shapes, lead=(N_DEV,))
    conv_w = aconv_s.transpose(1, 0, 2).reshape(3, 512)
    qg, ln_g, ln_b = qg_s.reshape(1, 256), lng_s.reshape(1, 512), lnb_s.reshape(1, 512)
    cw24 = [fcw_s[:, l].reshape(2, 4, 3, 704) for l in range(2)]
    c16 = jnp.pad(c_all, ((0, 16 - N_DEV), (0, 0)))

    mod_cols = _ada_fwd(c16, ada_w)
    (g1,) = _exchange("gather_mod", [[_pack([mod_cols])]], scatter=False)
    mod_all = _unpack(g1[:, 0], [(2, 16, 768)], lead=(N_DEV,))[0]
    mod_mine = lax.dynamic_index_in_dim(mod_all, me, axis=2, keepdims=False)
    mod = mod_mine.transpose(1, 0, 2).reshape(2, 6 * D) + ada_b
    mods = [[mod[l, k * D:(k + 1) * D].reshape(1, D) for k in range(6)] for l in range(2)]

    gw_ab, token = _exchange_start("gather_w_ab", [ab_w_in[0].astype(W), ab_w_out[0].astype(W)], False, mod, me)
    gw_up0, token = _exchange_start("gather_w_ffn0_up", [ffn_w_up[0].astype(W)], False, token, me)
    gw_dn0, token = _exchange_start("gather_w_ffn0_down", [ffn_w_down[0].astype(W)], False, token, me)
    gw_cd, token = _exchange_start("gather_w_cd", [
        cd_w_in[0].astype(W).reshape(1440, 128), c_w_uq[0].astype(W).reshape(192, 128), c_w_ukv[0].astype(W),
        cd_w_out[0].astype(W)], False, token, me)
    gw_up1, token = _exchange_start("gather_w_ffn1_up", [ffn_w_up[1].astype(W)], False, token, me)
    gw_dn1, started = _exchange_start("gather_w_ffn1_down", [ffn_w_down[1].astype(W)], False, token, me)

    cosq, cosk, sa, sb = _rope_tables(positions[0])
    n1g = [norm1_g[l].reshape(1, D) for l in range(2)]
    n2g = [norm2_g[l].reshape(1, D) for l in range(2)]
    mix_w, scale = b_mix_w[0], b_scale
    kvg = c_kv_norm_g
    w_s, b_st = d_w_s[0], d_b_s[0].T

    sh1, sc1, g1m = mods[0][:3]
    h_ab = _rmsmod_fwd("ab_norm", x0, n1g[0], sc1, sh1, started)
    w_abin8, w_about = _exchange_wait("wait_w_ab", gw_ab, h_ab)
    w_about2 = w_about.reshape(2, 512, D)
    z8 = _mm_cols("ab_in", h_ab, w_abin8, tm=2048)
    ycat_ab = _ab_mix_fwd(z8, conv_w, mix_w, scale)
    y_ab, x1 = _mm_rows_resid("ab_out", ycat_ab, w_about2, x0, g1m)
    w_up8, w_down4 = [None, None], [None, None]
    (w_up8[0],) = _exchange_wait("wait_w_ffn0_up", gw_up0, x1)
    x2, ffn0_saved, w_down4[0] = _ffn_fwd(0, x1, mods[0][3:], n2g[0], w_up8[0], cw24[0],
                                          lambda after: _exchange_wait("wait_w_ffn0_down", gw_dn0, after)[0].reshape(4, 704, D))

    w_cdin, w_uq, w_ukv, w_cdout = _exchange_wait("wait_w_cd", gw_cd, x2)
    w_cdout2 = w_cdout.reshape(2, 512, D)
    w_cd = w_cdin.reshape(8, D, 180).transpose(1, 0, 2).reshape(D, 1440)
    zc = lambda n: jnp.zeros((D, n), W)
    w_cd_pad = jnp.concatenate([w_cd[:, :384], zc(64), w_cd[:, 384:416], zc(32), w_cd[:, 416:]], axis=1)
    w_uq_pad = jnp.pad(w_uq.reshape(8, 256, 96).transpose(1, 0, 2), ((0, 0), (0, 0), (0, 32))).reshape(256, 1024)
    w_ukv_h = w_ukv.transpose(1, 0, 2)
    w_k_pad = jnp.pad(w_ukv_h[:, :, :64], ((0, 0), (0, 0), (0, 64))).reshape(128, 1024)
    w_kv_pad = jnp.concatenate([w_k_pad, w_ukv_h[:, :, 64:].reshape(128, 512)], axis=1)

    sh1, sc1, g1c = mods[1][:3]
    h_cd = _rmsmod_fwd("cd_norm", x2, n1g[1], sc1, sh1, n1g[1])
    z_cd = _mm_nn("cd_in", h_cd, w_cd_pad)
    qn, kvn = _mla_prep_fwd(z_cd, qg, kvg)
    qraw = _mm_nn("cd_uq", qn, w_uq_pad)
    kvall = _mm_nn("cd_ukv", kvn, w_kv_pad)
    q_r, k_r, v_r = _rope_fwd(qraw, kvall, z_cd, cosq, cosk, sa, sb)
    o, lse = _attn_fwd(q_r, k_r, v_r)
    ycat_cd = _sgu_fwd(z_cd, o, ln_g, ln_b, w_s, b_st)
    y_cd, x3 = _mm_rows_resid("cd_out", ycat_cd, w_cdout2, x2, g1c)
    (w_up8[1],) = _exchange_wait("wait_w_ffn1_up", gw_up1, x3)
    x4, ffn1_saved, w_down4[1] = _ffn_fwd(1, x3, mods[1][3:], n2g[1], w_up8[1], cw24[1],
                                          lambda after: _exchange_wait("wait_w_ffn1_down", gw_dn1, after)[0].reshape(4, 704, D))

    loss_local, dx4, dfg = _loss_head(x4, final_norm_g.reshape(1, D), target)

    dx3, gf1 = _ffn_bwd(1, dx4, ffn1_saved, mods[1][3:], n2g[1], w_up8[1], cw24[1], w_down4[1], me)

    dy, dg1c = _gate_bwd("cd_gate_bwd", dx3, y_cd, g1c)
    dycat = _mm_rows_dx("cd_out_dx", dy, w_cdout2)
    dw_cdout = _mm_rows_dw("cd_out_dw", ycat_cd, dy, out_dtype=WIRE_DTYPE)
    duv, dln_g, dln_b, dws, dbs = _sgu_bwd(z_cd, dycat, ln_g, ln_b, w_s, b_st)
    dq_r, dk_r, dv_r = _attn_bwd(q_r, k_r, v_r, o, lse, dycat)
    dqraw, dkvall, dkpe = _rope_bwd(dq_r, dk_r, dv_r, cosq, cosk, sa, sb)
    dqn = _mm_nt("cd_uq_dx", dqraw, w_uq_pad, tn=256)
    dkvn = _mm_nt("cd_ukv_dx", dkvall, w_kv_pad, tn=128)
    dw_uq_pad = _mm_tn("cd_uq_dw", qn, dqraw, tm=256)
    dw_kv_pad = _mm_tn("cd_ukv_dw", kvn, dkvall, tm=128)
    dz_cd, dqg, dkvg = _mla_prep_bwd(z_cd, qg, kvg, dqn, dkvn, dkpe, duv)
    dh_cd = _mm_nt("cd_in_dx", dz_cd, w_cd_pad)
    dw_cd_pad = _mm_tn("cd_in_dw", h_cd, dz_cd)
    dw_cd =jnp.concatenate([dw_cd_pad[:, :384], dw_cd_pad[:, 448:480], dw_cd_pad[:, 512:]], axis=1)
    dw_cd8 = dw_cd.reshape(D, 8, 180).transpose(1, 0, 2).reshape(8, 1440, 128).astype(WIRE_DTYPE)
    dw_uq8 = dw_uq_pad.reshape(256, 8, 128)[:, :, :96].transpose(1, 0, 2).reshape(8, 192, 128).astype(WIRE_DTYPE)
    dw_ukv8 = jnp.concatenate([dw_kv_pad[:, :1024].reshape(128, 8, 128)[:, :, :64], dw_kv_pad[:, 1024:].reshape(128, 8, 64)],
                              axis=2).transpose(1, 0, 2).astype(WIRE_DTYPE)
    sent_cd, token = _exchange_start("scatter_cd", [dw_cd8, dw_uq8, dw_ukv8, dw_cdout.reshape(8, 128, D)], True, dqg, me)
    early_names = ["c_kv_norm_g", "d_w_s", "d_b_s", "final_norm_g", "c_q_norm_g", "d_ln_g", "d_ln_b"]
    early_grads = [dkvg, dws.reshape(512, 128), dbs, dfg, dqg.reshape(8, 1, 32), dln_g.reshape(8, 1, 64), dln_b.reshape(8, 1, 64)]
    early_sent, token = _exchange_start("gather_small_grads_early", early_grads, [False] * 4 + [True] * 3, token, me)
    dx2, dn1g_cd, dsc1_cd, dsh1_cd = _rmsmod_bwd("cd_norm_bwd", x2, n1g[1], sc1, dh_cd, dx3, token)

    dx1, gf0 = _ffn_bwd(0, dx2, ffn0_saved, mods[0][3:], n2g[0], w_up8[0], cw24[0], w_down4[0], me)

    dy, dg1m = _gate_bwd("ab_gate_bwd", dx1, y_ab, g1m)
    dw_about = _mm_rows_dw("ab_out_dw", ycat_ab, dy, out_dtype=WIRE_DTYPE)
    sent_about, token = _exchange_start("scatter_ab_out", [dw_about.reshape(8, 128, D)], True, dg1m, me)
    dycat = _mm_rows_dx("ab_out_dx", dy, w_about2)
    dz8, dconv_w, dmix_w, dscale = _ab_mix_bwd(z8, dycat, conv_w, mix_w, scale, token)
    dz8 = dz8.reshape(8, S, 256)
    dw_abin8 = _mm_cols_dw("ab_in_dw", h_ab, dz8, out_dtype=WIRE_DTYPE, tk=1024)
    sent_abin, token = _exchange_start("scatter_ab_in", [dw_abin8], True, dscale, me)
    dh_ab = _mm_cols_dx("ab_in_dx", dz8, w_abin8)
    dx0, dn1g_ab, dsc1_ab, dsh1_ab = _rmsmod_bwd("ab_norm_bwd", x0, n1g[0], mods[0][1], dh_ab, dx1, token)

    dmod = jnp.stack([jnp.concatenate([dsh1_ab, dsc1_ab, dg1m, *gf0["mod"]], axis=1)[0],
                      jnp.concatenate([dsh1_cd, dsc1_cd, dg1c, *gf1["mod"]], axis=1)[0]])
    late_names = ["ada_b", "norm1_g", "norm2_g", "b_mix_w", "b_scale", "a_conv_w", "ffn_conv_w"]
    late_grads = [dmod, jnp.concatenate([dn1g_ab, dn1g_cd]), jnp.concatenate([gf0["n2g"], gf1["n2g"]]),
                  dmix_w.reshape(512, 128), dscale, dconv_w.reshape(3, 8, 64).transpose(1, 0, 2),
                  jnp.stack([gf0["cw24"].reshape(8, 3, 704), gf1["cw24"].reshape(8, 3, 704)], axis=1)]
    small_view = dict(ada_b=(2, 6 * D), norm1_g=(2, D), norm2_g=(2, D), b_mix_w=(512, 128), b_scale=(1, 512), c_kv_norm_g=(1, 128),
                      d_w_s=(512, 128), d_b_s=(4, 128), final_norm_g=(1, D),
                      a_conv_w=(3, 64), c_q_norm_g=(1, 32), d_ln_g=(1, 64), d_ln_b=(1, 64), ffn_conv_w=(2, 3, 704))
    late_sent, token = _exchange_start("gather_small_grads_late", late_grads, [False] * 5 + [True] * 2, dx0, me)

    res = {}

    def update(name, w, m, v, parts, shape3d):
        outs = _adamw("adamw_" + name, w.reshape(shape3d), m.reshape(shape3d), v.reshape(shape3d),
                      [p.reshape((p.shape[0],) + shape3d[1:]) for p in parts])
        res[name] = [o_.reshape(w.shape) for o_ in outs]

    p_cdin, p_uq, p_ukv, p_cdout = _exchange_wait("wait_scatter_cd", sent_cd, token)
    update("cd_w_in", cd_w_in, m_cd_w_in, v_cd_w_in, [p_cdin], (1, 1440, 128))
    update("c_w_uq", c_w_uq, m_c_w_uq, v_c_w_uq, [p_uq], (1, 192, 128))
    update("c_w_ukv", c_w_ukv, m_c_w_ukv, v_c_w_ukv, [p_ukv], (1, 128, 128))
    update("cd_w_out", cd_w_out, m_cd_w_out, v_cd_w_out, [p_cdout], (1, 128, D))
    (p_dn1,) = _exchange_wait("wait_scatter_ffn1_down", gf1["sent_down"], token)
    (p_dn0,) = _exchange_wait("wait_scatter_ffn0_down", gf0["sent_down"], res["cd_w_out"][0])
    update("ffn_w_down", ffn_w_down, m_ffn_w_down, v_ffn_w_down, [p_dn0, p_dn1], (2, 352, D))
    (p_up1,) = _exchange_wait("wait_scatter_ffn1_up", gf1["sent_up"], token)
    (p_up0,) = _exchange_wait("wait_scatter_ffn0_up", gf0["sent_up"], res["ffn_w_down"][0])
    swap = lambda a: jnp.swapaxes(a, 1, 2)
    update("ffn_w_up", swap(ffn_w_up), swap(m_ffn_w_up), swap(v_ffn_w_up), [p_up0, p_up1], (2, 704, D))
    up_done = res["ffn_w_up"][0]
    res["ffn_w_up"] = [swap(o_) for o_ in res["ffn_w_up"]]
    (p_about,) = _exchange_wait("wait_scatter_ab_out", sent_about, up_done)
    update("ab_w_out", ab_w_out, m_ab_w_out, v_ab_w_out, [p_about], (1, 128, D))
    (p_abin,) = _exchange_wait("wait_scatter_ab_in", sent_abin, res["ab_w_out"][0])
    update("ab_w_in", ab_w_in, m_ab_w_in, v_ab_w_in, [p_abin], (1, D, 256))

    early_parts = _exchange_wait("wait_small_grads_early", early_sent, res["ab_w_in"][0])
    late_parts = _exchange_wait("wait_small_grads_late", late_sent, res["ab_w_in"][0])
    small_names = early_names + late_names
    small_parts = list(early_parts) + list(late_parts)
    dmod_all = late_parts[0]
    dmod_cols = lax.dynamic_slice_in_dim(dmod_all, me * 768, 768, axis=2).transpose(1, 0, 2)
    g_ada_w = _ada_bwd(c16, jnp.pad(dmod_cols, ((0, 0), (0, 16 - N_DEV), (0, 0))))
    update("ada_w", ada_w, m_ada_w, v_ada_w, [g_ada_w[l][None] for l in range(2)], (2, D, 768))

    small_w = dict(ada_b=(ada_b, m_ada_b, v_ada_b), norm1_g=(norm1_g, m_norm1_g, v_norm1_g), norm2_g=(norm2_g, m_norm2_g, v_norm2_g),
                   b_mix_w=(b_mix_w, m_b_mix_w, v_b_mix_w), b_scale=(b_scale, m_b_scale, v_b_scale),
                   c_kv_norm_g=(c_kv_norm_g, m_c_kv_norm_g, v_c_kv_norm_g), d_w_s=(d_w_s, m_d_w_s, v_d_w_s),
                   d_b_s=(d_b_s, m_d_b_s, v_d_b_s), final_norm_g=(final_norm_g, m_final_norm_g, v_final_norm_g),
                   a_conv_w=(a_conv_w, m_a_conv_w, v_a_conv_w), c_q_norm_g=(c_q_norm_g, m_c_q_norm_g, v_c_q_norm_g),
                   d_ln_g=(d_ln_g, m_d_ln_g, v_d_ln_g), d_ln_b=(d_ln_b, m_d_ln_b, v_d_ln_b),
                   ffn_conv_w=(ffn_conv_w, m_ffn_conv_w, v_ffn_conv_w))
    small_out = _adamw_small("adamw_small", [tuple(a.reshape(small_view[n]) for a in small_w[n]) for n in small_names],
                             list(small_parts))
    for n, outs in zip(small_names, small_out):
        res[n] = [o_.reshape(small_w[n][0].shape) for o_ in outs]

    loss = lax.psum(loss_local[0, 0], ("x", "y", "c"))
    order = ["ada_w", "ada_b", "norm1_g", "norm2_g", "ab_w_in", "a_conv_w", "b_mix_w", "b_scale", "ab_w_out", "cd_w_in", "c_q_norm_g",
             "c_w_uq", "c_kv_norm_g", "c_w_ukv", "d_ln_g", "d_ln_b", "d_w_s", "d_b_s", "cd_w_out", "ffn_w_up", "ffn_conv_w",
             "ffn_w_down", "final_norm_g"]
    return (loss, dx0[None], *[res[n][0] for n in order], *[res[n][1] for n in order], *[res[n][2] for n in order],
            *[res[n][3] for n in order])
```

```python
import functools
import math

import jax
import jax.numpy as jnp
from jax import lax
from jax.experimental import pallas as pl
from jax.experimental.pallas import tpu as pltpu

F32 = jnp.float32
BF16 = jnp.bfloat16
_MXU_DTYPE = BF16
WIRE_DTYPE = BF16
ACT_DTYPE = BF16
_VMEM_LIMIT = 56 * 2 ** 20
N_DEV = 8
EPS = 1e-6
POOL_WINDOWS = (2, 4, 8, 16)
ATTN_SCALE = (64 + 32) ** -0.5
ADAM_LR, ADAM_B1, ADAM_B2, ADAM_EPS, ADAM_WD, ADAM_STEP = 0.001, 0.9, 0.999, 1e-08, 0.01, 10
MESH = pl.DeviceIdType.MESH
ANY = pl.BlockSpec(memory_space=pl.ANY)


def _cp(*sem):
    return pltpu.CompilerParams(dimension_semantics=sem, vmem_limit_bytes=_VMEM_LIMIT)


def _dot(a, b, contract):
    dn = {"nn": (((1,), (0,)), ((), ())), "nt": (((1,), (1,)), ((), ())), "tn": (((0,), (0,)), ((), ()))}[contract]
    return lax.dot_general(a.astype(_MXU_DTYPE), b.astype(_MXU_DTYPE), dn, preferred_element_type=F32)


def _my_position():
    x, y, c = lax.axis_index("x"), lax.axis_index("y"), lax.axis_index("c")
    return x, y, c, 4 * x + 2 * y + c


def _exchange(name, groups, scatter):
    flat = [a for g in groups for a in g]
    n_in, n_grp = len(flat), len(groups)
    out_shapes = []
    for g in groups:
        slab = g[0].shape[1:] if scatter else g[0].shape
        out_shapes.append(jax.ShapeDtypeStruct((N_DEV, len(g)) + tuple(slab), g[0].dtype))

    def body(*refs):
        ins, outs = refs[:n_in], refs[n_in:n_in + n_grp]
        send_sems, recv_sems, local_sems = refs[n_in + n_grp:]
        x, y, c, me = _my_position()
        i = 0
        for gi, g in enumerate(groups):
            for l in range(len(g)):
                src = ins[i]
                i += 1
                pltpu.make_async_copy(src.at[me] if scatter else src, outs[gi].at[me, l], local_sems.at[gi]).start()
                for k in range(1, N_DEV):
                    px = 1 - x if k & 4 else x
                    py = 1 - y if k & 2 else y
                    pc = 1 - c if k & 1 else c
                    peer = 4 * px + 2 * py + pc
                    pltpu.make_async_remote_copy(
                        src_ref=src.at[peer] if scatter else src, dst_ref=outs[gi].at[me, l],
                        send_sem=send_sems.at[gi], recv_sem=recv_sems.at[gi],
                        device_id=(px, py, pc), device_id_type=MESH).start()
        for gi in range(n_grp):
            mine = outs[gi].at[me]
            pltpu.make_async_copy(mine, mine, local_sems.at[gi]).wait()
            seven = outs[gi].at[pl.ds(0, N_DEV - 1)]
            w = pltpu.make_async_remote_copy(src_ref=seven, dst_ref=seven, send_sem=send_sems.at[gi],
                                             recv_sem=recv_sems.at[gi], device_id=(x, y, c), device_id_type=MESH)
            w.wait_send()
            w.wait_recv()

    return pl.pallas_call(
        body, name=name, out_shape=tuple(out_shapes),
        in_specs=[ANY] * n_in, out_specs=tuple([ANY] * n_grp),
        scratch_shapes=[pltpu.SemaphoreType.DMA((n_grp,)), pltpu.SemaphoreType.DMA((n_grp,)),
                        pltpu.SemaphoreType.DMA((n_grp,))],
        compiler_params=pltpu.CompilerParams(has_side_effects=True),
    )(*flat)


HBM_SPEC = pl.BlockSpec(memory_space=pltpu.HBM)
SEM_SPEC = pl.BlockSpec(memory_space=pltpu.SEMAPHORE)
EFFECT = pltpu.SideEffectType.DATAFLOW_SIDE_EFFECTING


def _put_mine(name, srcs, scatter, me):
    n = len(srcs)
    slabs = [tuple(s.shape[1:] if sc else s.shape) for s, sc in zip(srcs, scatter)]

    def body(me_ref, *refs):
        for i in range(n):
            refs[n + i][...] = refs[i][...]

    def at_me(slab):
        return pl.BlockSpec((None,) + slab, lambda g, me_ref, nd=len(slab): (me_ref[0],) + (0,) * nd)

    def whole(slab):
        return pl.BlockSpec(slab, lambda g, me_ref, nd=len(slab): (0,) * nd)

    return pl.pallas_call(
        body, name=name,
        grid_spec=pltpu.PrefetchScalarGridSpec(
            num_scalar_prefetch=1, grid=(1,),
            in_specs=[at_me(slab) if sc else whole(slab) for slab, sc in zip(slabs, scatter)],
            out_specs=[at_me(slab) for slab in slabs]),
        out_shape=[jax.ShapeDtypeStruct((N_DEV,) + slab, s.dtype) for slab, s in zip(slabs, srcs)],
        compiler_params=_cp("arbitrary"))(me.reshape(1), *srcs)


def _exchange_start(name, srcs, scatter, after, me):
    n = len(srcs)
    scatter = list(scatter) if isinstance(scatter, (list, tuple)) else [scatter] * n
    lands = _put_mine(name + "_mine", srcs, scatter, me)
    srcs = [pltpu.with_memory_space_constraint(a, pltpu.HBM) for a in srcs]
    lands = [pltpu.with_memory_space_constraint(a, pltpu.HBM) for a in lands]

    def body(*refs):
        ins, land = refs[:n], refs[n:2 * n]
        send_sems, recv_sems, token = refs[2 * n + 1], refs[2 * n + 2], refs[-1]
        x, y, c, me_in = _my_position()
        for i in range(n):
            for k in range(1, N_DEV):
                px = 1 - x if k & 4 else x
                py = 1 - y if k & 2 else y
                pc = 1 - c if k & 1 else c
                pltpu.make_async_remote_copy(
                    src_ref=ins[i].at[4 * px + 2 * py + pc] if scatter[i] else ins[i], dst_ref=land[i].at[me_in],
                    send_sem=send_sems.at[i], recv_sem=recv_sems.at[i],
                    device_id=(px, py, pc), device_id_type=MESH).start()
        token[...] = jnp.zeros_like(token)

    outs = pl.pallas_call(
        body, name=name,
        out_shape=(pltpu.SemaphoreType.DMA((n,)), pltpu.SemaphoreType.DMA((n,)),
                   *[pltpu.HBM(a.shape, a.dtype) for a in srcs], *[pltpu.HBM(a.shape, a.dtype) for a in lands],
                   jax.ShapeDtypeStruct((8, 128), F32)),
        in_specs=[HBM_SPEC] * (2 * n) + [ANY],
        out_specs=(SEM_SPEC, SEM_SPEC, *[HBM_SPEC] * (2 * n), pl.BlockSpec(memory_space=pltpu.VMEM)),
        input_output_aliases={i: 2 + i for i in range(2 * n)},
        compiler_params=pltpu.CompilerParams(has_side_effects=EFFECT),
    )(*srcs, *lands, after)
    return (outs[0], outs[1], outs[2:2 + n], outs[2 + n:2 + 2 * n]), outs[-1]


def _exchange_wait(name, handle, after):
    send_sems, recv_sems, srcs, lands = handle
    n = len(srcs)

    def body(*refs):
        land, send_ref, recv_ref = refs[n:2 * n], refs[2 * n], refs[2 * n + 1]
        x, y, c, _ = _my_position()
        for i in range(n):
            seven = land[i].at[pl.ds(0, N_DEV - 1)]
            w = pltpu.make_async_remote_copy(src_ref=seven, dst_ref=seven, send_sem=send_ref.at[i], recv_sem=recv_ref.at[i],
                                             device_id=(x, y, c), device_id_type=MESH)
            w.wait_send()
            w.wait_recv()

    outs = pl.pallas_call(
        body, name=name,
        out_shape=(*[pltpu.HBM(a.shape, a.dtype) for a in srcs], *[pltpu.HBM(a.shape, a.dtype) for a in lands]),
        in_specs=[HBM_SPEC] * (2 * n) + [SEM_SPEC, SEM_SPEC, ANY],
        out_specs=tuple([HBM_SPEC] * (2 * n)),
        input_output_aliases={i: i for i in range(2 * n)},
        compiler_params=pltpu.CompilerParams(has_side_effects=EFFECT),
    )(*srcs, *lands, send_sems, recv_sems, after)
    return outs[n:]


def _other_chips(x, y):
    return [(1 - x, y), (x, 1 - y), (1 - x, 1 - y)]


def _hier_gather_start(name, srcs, after, me):
    n = len(srcs)
    lands = _put_mine(name + "_mine", srcs, [False] * n, me)
    srcs = [pltpu.with_memory_space_constraint(a, pltpu.HBM) for a in srcs]
    lands = [pltpu.with_memory_space_constraint(a, pltpu.HBM) for a in lands]

    def body(*refs):
        ins, land = refs[:n], refs[n:2 * n]
        ici_send, ici_recv, d2d_send, d2d_recv = refs[2 * n + 1:2 * n + 5]
        token = refs[-1]
        x, y, c, me_in = _my_position()
        for i in range(n):
            pltpu.make_async_remote_copy(src_ref=ins[i], dst_ref=land[i].at[me_in], send_sem=d2d_send.at[i], recv_sem=d2d_recv.at[i],
                                         device_id=(x, y, 1 - c), device_id_type=MESH).start()
            for px, py in _other_chips(x, y):
                pltpu.make_async_remote_copy(src_ref=ins[i], dst_ref=land[i].at[me_in], send_sem=ici_send.at[i],
                                             recv_sem=ici_recv.at[i], device_id=(px, py, c), device_id_type=MESH).start()
        token[...] = jnp.zeros_like(token)

    sem = pltpu.SemaphoreType.DMA((n,))
    outs = pl.pallas_call(
        body, name=name,
        out_shape=(sem, sem, sem, sem, *[pltpu.HBM(a.shape, a.dtype) for a in srcs], *[pltpu.HBM(a.shape, a.dtype) for a in lands],
                   jax.ShapeDtypeStruct((8, 128), F32)),
        in_specs=[HBM_SPEC] * (2 * n) + [ANY],
        out_specs=(SEM_SPEC,) * 4 + (HBM_SPEC,) * (2 * n) + (pl.BlockSpec(memory_space=pltpu.VMEM),),
        input_output_aliases={i: 4 + i for i in range(2 * n)},
        compiler_params=pltpu.CompilerParams(has_side_effects=EFFECT),
    )(*srcs, *lands, after)
    return (outs[:4], outs[4:4 + n], outs[4 + n:4 + 2 * n]), outs[-1]


def _hier_gather_forward(name, handle, after):
    sems, srcs, lands = handle
    n = len(srcs)

    def body(*refs):
        land = refs[n:2 * n]
        ici_send, ici_recv, d2d_send, d2d_recv = refs[2 * n:2 * n + 4]
        x, y, c, _ = _my_position()
        for i in range(n):
            three = land[i].at[pl.ds(0, 3)]
            pltpu.make_async_remote_copy(src_ref=three, dst_ref=three, send_sem=ici_send.at[i], recv_sem=ici_recv.at[i],
                                         device_id=(x, y, c), device_id_type=MESH).wait_recv()
            for px, py in _other_chips(x, y):
                slab = land[i].at[4 * px + 2 * py + c]
                pltpu.make_async_remote_copy(src_ref=slab, dst_ref=slab, send_sem=d2d_send.at[i], recv_sem=d2d_recv.at[i],
                                             device_id=(x, y, 1 - c), device_id_type=MESH).start()

    outs = pl.pallas_call(
        body, name=name,
        out_shape=(*[pltpu.HBM(a.shape, a.dtype) for a in srcs], *[pltpu.HBM(a.shape, a.dtype) for a in lands]),
        in_specs=[HBM_SPEC] * (2 * n) + [SEM_SPEC] * 4 + [ANY],
        out_specs=tuple([HBM_SPEC] * (2 * n)),
        input_output_aliases={i: i for i in range(2 * n)},
        compiler_params=pltpu.CompilerParams(has_side_effects=EFFECT),
    )(*srcs, *lands, *sems, after)
    return (sems, outs[:n], outs[n:])


def _hier_gather_wait(name, handle, after):
    sems, srcs, lands = handle
    n = len(srcs)

    def body(*refs):
        land = refs[n:2 * n]
        ici_send, ici_recv, d2d_send, d2d_recv = refs[2 * n:2 * n + 4]
        x, y, c, _ = _my_position()
        for i in range(n):
            three, four = land[i].at[pl.ds(0, 3)], land[i].at[pl.ds(0, 4)]
            pltpu.make_async_remote_copy(src_ref=three, dst_ref=three, send_sem=ici_send.at[i], recv_sem=ici_recv.at[i],
                                         device_id=(x, y, c), device_id_type=MESH).wait_send()
            w = pltpu.make_async_remote_copy(src_ref=four, dst_ref=four, send_sem=d2d_send.at[i], recv_sem=d2d_recv.at[i],
                                             device_id=(x, y, c), device_id_type=MESH)
            w.wait_send()
            w.wait_recv()

    outs = pl.pallas_call(
        body, name=name,
        out_shape=(*[pltpu.HBM(a.shape, a.dtype) for a in srcs], *[pltpu.HBM(a.shape, a.dtype) for a in lands]),
        in_specs=[HBM_SPEC] * (2 * n) + [SEM_SPEC] * 4 + [ANY],
        out_specs=tuple([HBM_SPEC] * (2 * n)),
        input_output_aliases={i: i for i in range(2 * n)},
        compiler_params=pltpu.CompilerParams(has_side_effects=EFFECT),
    )(*srcs, *lands, *sems, after)
    return outs[n:]


def _pack(arrs):
    flat = jnp.concatenate([a.reshape(-1).astype(F32) for a in arrs])
    n = flat.shape[0]
    rows = -(-n // 1024) * 8
    return jnp.pad(flat, (0, rows * 128 - n)).reshape(rows, 128)


def _unpack(buf, shapes, lead=()):
    flat = buf.reshape(lead + (-1,))
    out, off = [], 0
    for s in shapes:
        n = math.prod(s)
        out.append(flat[..., off:off + n].reshape(lead + tuple(s)))
        off += n
    return out


def _mm(name, a, a_spec, b, b_spec, out_sds, o_spec, grid, contract, nk=1, stacked=0):
    o_blk = tuple(d for d in o_spec.block_shape if d is not None)

    def body(a_ref, b_ref, o_ref, *acc):
        if stacked:
            r = _dot(a_ref[0], b_ref[0], contract)
            for q in range(1, stacked):
                r = r + _dot(a_ref[q], b_ref[q], contract)
        else:
            r = _dot(a_ref[...], b_ref[...], contract)
        if nk == 1:
            o_ref[...] = r.astype(o_ref.dtype)
        else:
            k = pl.program_id(len(grid) - 1)

            @pl.when(k == 0)
            def _():
                acc[0][...] = r

            @pl.when(k > 0)
            def _():
                acc[0][...] += r

            @pl.when(k == nk - 1)
            def _():
                o_ref[...] = acc[0][...].astype(o_ref.dtype)

    sem = ("parallel",) * (len(grid) - 1) + (("arbitrary",) if nk > 1 else ("parallel",))
    return pl.pallas_call(
        body, name=name, out_shape=out_sds, grid=grid, in_specs=[a_spec, b_spec], out_specs=o_spec,
        scratch_shapes=[pltpu.VMEM(o_blk, F32)] if nk > 1 else [], compiler_params=_cp(*sem))(a, b)


def _tile(n, want):
    t = min(n, want)
    assert n % t == 0, (n, t)
    return t


def _mm_nn(name, a, b, out_dtype=F32, tm=512, tn=512):
    (M, K), N = a.shape, b.shape[1]
    tm, tn = _tile(M, tm), _tile(N, tn)
    return _mm(name, a, pl.BlockSpec((tm, K), lambda i, j: (i, 0)), b, pl.BlockSpec((K, tn), lambda i, j: (0, j)),
               jax.ShapeDtypeStruct((M, N), out_dtype), pl.BlockSpec((tm, tn), lambda i, j: (i, j)),
               (M // tm, N // tn), "nn")


def _mm_nt(name, a, b, out_dtype=F32, tm=512, tn=512):
    (M, K), N = a.shape, b.shape[0]
    tm, tn = _tile(M, tm), _tile(N, tn)
    return _mm(name, a, pl.BlockSpec((tm, K), lambda i, j: (i, 0)), b, pl.BlockSpec((tn, K), lambda i, j: (j, 0)),
               jax.ShapeDtypeStruct((M, N), out_dtype), pl.BlockSpec((tm, tn), lambda i, j: (i, j)),
               (M // tm, N // tn), "nt")


def _mm_tn(name, a, b, out_dtype=F32, tm=512, tn=512):
    (K, M), N = a.shape, b.shape[1]
    tm, tn = _tile(M, tm), _tile(N, tn)
    return _mm(name, a, pl.BlockSpec((K, tm), lambda i, j: (0, i)), b, pl.BlockSpec((K, tn), lambda i, j: (0, j)),
               jax.ShapeDtypeStruct((M, N), out_dtype), pl.BlockSpec((tm, tn), lambda i, j: (i, j)),
               (M // tm, N // tn), "tn")


def _mm_cols(name, a, w, out_dtype=F32, tm=512):
    (M, K), (J, _, n) = a.shape, w.shape
    tm = _tile(M, tm)
    return _mm(name, a, pl.BlockSpec((tm, K), lambda j, i: (i, 0)), w, pl.BlockSpec((None, K, n), lambda j, i: (j, 0, 0)),
               jax.ShapeDtypeStruct((J, M, n), out_dtype), pl.BlockSpec((None, tm, n), lambda j, i: (j, i, 0)),
               (J, M // tm), "nn")


def _mm_cols_dx(name, d, w, out_dtype=F32, tm=512, jb=None):
    (J, M, n), K = d.shape, w.shape[1]
    tm, jb = _tile(M, tm), J if jb is None else jb
    return _mm(name, d, pl.BlockSpec((jb, tm, n), lambda i, j: (j, i, 0)), w, pl.BlockSpec((jb, K, n), lambda i, j: (j, 0, 0)),
               jax.ShapeDtypeStruct((M, K), out_dtype), pl.BlockSpec((tm, K), lambda i, j: (i, 0)),
               (M // tm, J // jb), "nt", nk=J // jb, stacked=jb)


def _mm_cols_dw(name, a, d, out_dtype=F32, tk=512):
    (M, K), (J, _, n) = a.shape, d.shape
    tk = _tile(K, tk)
    return _mm(name, a, pl.BlockSpec((M, tk), lambda j, i: (0, i)), d, pl.BlockSpec((None, M, n), lambda j, i: (j, 0, 0)),
               jax.ShapeDtypeStruct((J, K, n), out_dtype), pl.BlockSpec((None, tk, n), lambda j, i: (j, i, 0)),
               (J, K // tk), "tn")


def _mm_cols_dwt(name, a, d, out_dtype=F32, tk=512):
    (M, K), (J, _, n) = a.shape, d.shape
    tk = _tile(K, tk)
    return _mm(name, d, pl.BlockSpec((None, M, n), lambda j, i: (j, 0, 0)), a, pl.BlockSpec((M, tk), lambda j, i: (0, i)),
               jax.ShapeDtypeStruct((J, n, K), out_dtype), pl.BlockSpec((None, n, tk), lambda j, i: (j, 0, i)),
               (J, K // tk), "tn")


def _mm_rows_resid(name, a, w, resid, gate, tm=512):
    (Q, M, k), N = a.shape, w.shape[2]
    tm = _tile(M, tm)

    def body(a_ref, w_ref, r_ref, g_ref, y_ref, x_ref):
        y = _dot(a_ref[0], w_ref[0], "nn")
        for q in range(1, Q):
            y = y + _dot(a_ref[q], w_ref[q], "nn")
        y_ref[...] = y
        x_ref[...] = r_ref[...] + g_ref[...] * y

    return pl.pallas_call(
        body, name=name, grid=(M // tm,),
        out_shape=(jax.ShapeDtypeStruct((M, N), F32), jax.ShapeDtypeStruct((M, N), F32)),
        in_specs=[pl.BlockSpec((Q, tm, k), lambda i: (0, i, 0)), pl.BlockSpec((Q, k, N), lambda i: (0, 0, 0)),
                  pl.BlockSpec((tm, N), lambda i: (i, 0)), pl.BlockSpec((1, N), lambda i: (0, 0))],
        out_specs=(pl.BlockSpec((tm, N), lambda i: (i, 0)), pl.BlockSpec((tm, N), lambda i: (i, 0))),
        compiler_params=_cp("parallel"))(a, w, resid, gate)


def _mm_rows_dx(name, d, w, out_dtype=F32, tm=512):
    (M, N), (Q, k, _) = d.shape, w.shape
    tm = _tile(M, tm)
    return _mm(name, d, pl.BlockSpec((tm, N), lambda q, i: (i, 0)), w, pl.BlockSpec((None, k, N), lambda q, i: (q, 0, 0)),
               jax.ShapeDtypeStruct((Q, M, k), out_dtype), pl.BlockSpec((None, tm, k), lambda q, i: (q, i, 0)),
               (Q, M // tm), "nt")


def _mm_rows_dw(name, a, d, out_dtype=F32, tn=512):
    (Q, M, k), N = a.shape, d.shape[1]
    tn = _tile(N, tn)
    return _mm(name, a, pl.BlockSpec((None, M, k), lambda q, j: (q, 0, 0)), d, pl.BlockSpec((M, tn), lambda q, j: (0, j)),
               jax.ShapeDtypeStruct((Q, k, N), out_dtype), pl.BlockSpec((None, k, tn), lambda q, j: (q, 0, j)),
               (Q, N // tn), "tn")


def _silu(v):
    return v * jax.nn.sigmoid(v)


def _ada_fwd(c16, ada_w):
    L, D, n = ada_w.shape

    def body(c_ref, w_ref, o_ref):
        o_ref[...] = _dot(_silu(c_ref[...]), w_ref[...], "nn")

    return pl.pallas_call(
        body, name="ada_fwd", grid=(L,), out_shape=jax.ShapeDtypeStruct((L, 16, n), F32),
        in_specs=[pl.BlockSpec((16, D), lambda l: (0, 0)), pl.BlockSpec((None, D, n), lambda l: (l, 0, 0))],
        out_specs=pl.BlockSpec((None, 16, n), lambda l: (l, 0, 0)), compiler_params=_cp("parallel"))(c16, ada_w)


def _ada_bwd(c16, dmod16):
    L, _, n = dmod16.shape
    D = c16.shape[1]

    def body(c_ref, d_ref, o_ref):
        o_ref[...] = _dot(_silu(c_ref[...]), d_ref[...], "tn")

    return pl.pallas_call(
        body, name="ada_bwd", grid=(L,), out_shape=jax.ShapeDtypeStruct((L, D, n), F32),
        in_specs=[pl.BlockSpec((16, D), lambda l: (0, 0)), pl.BlockSpec((None, 16, n), lambda l: (l, 0, 0))],
        out_specs=pl.BlockSpec((None, D, n), lambda l: (l, 0, 0)), compiler_params=_cp("parallel"))(c16, dmod16)


def _row_spec(tr, n):
    return pl.BlockSpec((tr, n), lambda i: (i, 0))


def _vec_spec(n):
    return pl.BlockSpec((1, n), lambda i: (0, 0))


def _rmsmod_fwd(name, x, g, sc, sh, after, tr=256):
    S, D = x.shape

    def body(x_ref, g_ref, sc_ref, sh_ref, after_ref, h_ref):
        xv = x_ref[...]
        rstd = lax.rsqrt(jnp.mean(xv * xv, axis=-1, keepdims=True) + EPS)
        y = xv * rstd * g_ref[...]
        h_ref[...] = (y * (1.0 + sc_ref[...]) + sh_ref[...]).astype(h_ref.dtype)

    return pl.pallas_call(
        body, name=name, grid=(S // tr,), out_shape=jax.ShapeDtypeStruct((S, D), _MXU_DTYPE),
        in_specs=[_row_spec(tr, D), _vec_spec(D), _vec_spec(D), _vec_spec(D), ANY], out_specs=_row_spec(tr, D),
        compiler_params=_cp("parallel"))(x, g, sc, sh, after)


def _acc_rows(ref, val, first):
    s = jnp.sum(val, axis=0, keepdims=True)

    @pl.when(first)
    def _():
        ref[...] = s

    @pl.when(jnp.logical_not(first))
    def _():
        ref[...] += s


def _rmsmod_bwd(name, x, g, sc, dh, dres, after, tr=256):
    S, D = x.shape

    def body(x_ref, g_ref, sc_ref, dh_ref, dres_ref, after_ref, dx_ref, dg_ref, dsc_ref, dsh_ref):
        first = pl.program_id(0) == 0
        xv, dh_v, gv = x_ref[...], dh_ref[...], g_ref[...]
        rstd = lax.rsqrt(jnp.mean(xv * xv, axis=-1, keepdims=True) + EPS)
        xhat = xv * rstd
        _acc_rows(dsh_ref, dh_v, first)
        _acc_rows(dsc_ref, dh_v * (xhat * gv), first)
        dyg = dh_v * (1.0 + sc_ref[...])
        _acc_rows(dg_ref, dyg * xhat, first)
        dxhat = dyg * gv
        dx_ref[...] = dres_ref[...] + rstd * (dxhat - xhat * jnp.mean(dxhat * xhat, axis=-1, keepdims=True))

    vec = jax.ShapeDtypeStruct((1, D), F32)
    return pl.pallas_call(
        body, name=name, grid=(S // tr,), out_shape=(jax.ShapeDtypeStruct((S, D), F32), vec, vec, vec),
        in_specs=[_row_spec(tr, D), _vec_spec(D), _vec_spec(D), _row_spec(tr, D), _row_spec(tr, D), ANY],
        out_specs=(_row_spec(tr, D), _vec_spec(D), _vec_spec(D), _vec_spec(D)),
        compiler_params=_cp("arbitrary"))(x, g, sc, dh, dres, after)


def _loss_head(x, g, target, tr=256):
    S, D = x.shape

    def body(x_ref, g_ref, t_ref, loss_ref, dx_ref, dg_ref):
        first = pl.program_id(0) == 0
        xv, gv = x_ref[...], g_ref[...]
        rstd = lax.rsqrt(jnp.mean(xv * xv, axis=-1, keepdims=True) + EPS)
        xhat = xv * rstd
        err = xhat * gv - t_ref[...]
        part = 0.5 * jnp.sum(jnp.mean(err * err, axis=-1, keepdims=True), axis=0, keepdims=True)

        @pl.when(first)
        def _():
            loss_ref[...] = part

        @pl.when(jnp.logical_not(first))
        def _():
            loss_ref[...] += part

        dout = err * (1.0 / D)
        _acc_rows(dg_ref, dout * xhat, first)
        dxhat = dout * gv
        dx_ref[...] = rstd * (dxhat - xhat * jnp.mean(dxhat * xhat, axis=-1, keepdims=True))

    return pl.pallas_call(
        body, name="loss_head", grid=(S // tr,),
        out_shape=(jax.ShapeDtypeStruct((1, 1), F32), jax.ShapeDtypeStruct((S, D), F32), jax.ShapeDtypeStruct((1, D), F32)),
        in_specs=[_row_spec(tr, D), _vec_spec(D), _row_spec(tr, D)],
        out_specs=(pl.BlockSpec((1, 1), lambda i: (0, 0)), _row_spec(tr, D), _vec_spec(D)),
        compiler_params=_cp("arbitrary"))(x, g, target)


def _gate_bwd(name, dx, y, gate, tr=256):
    S, D = dx.shape

    def body(dx_ref, y_ref, g_ref, dy_ref, dg_ref):
        dxv = dx_ref[...]
        dy_ref[...] = (g_ref[...] * dxv).astype(dy_ref.dtype)
        _acc_rows(dg_ref, dxv * y_ref[...], pl.program_id(0) == 0)

    return pl.pallas_call(
        body, name=name, grid=(S // tr,),
        out_shape=(jax.ShapeDtypeStruct((S, D), _MXU_DTYPE), jax.ShapeDtypeStruct((1, D), F32)),
        in_specs=[_row_spec(tr, D), _row_spec(tr, D), _vec_spec(D)], out_specs=(_row_spec(tr, D), _vec_spec(D)),
        compiler_params=_cp("arbitrary"))(dx, y, gate)


def _shift_down(v, k):
    t = lax.broadcasted_iota(jnp.int32, v.shape, 0)
    return jnp.where(t >= k, pltpu.roll(v, k, axis=0), 0.0)


def _shift_up(v, k):
    n = v.shape[0]
    t = lax.broadcasted_iota(jnp.int32, v.shape, 0)
    return jnp.where(t < n - k, pltpu.roll(v, n - k, axis=0), 0.0)


def _window_sum(p, w, shift):
    s, k = p, 1
    while k < w:
        s = s + shift(s, k)
        k *= 2
    return s


def _pool_count(shape, w):
    t = lax.broadcasted_iota(jnp.int32, shape, 0)
    return jnp.minimum(t + 1, w).astype(F32)


def _ab_specs(S):
    zs = [pl.BlockSpec((None, S, 128), functools.partial(lambda g, q: (2 * q + g // 2, 0, g % 2), q=q)) for q in range(4)]
    return zs


def _ab_mix_fwd(z8, conv_w, mix_w, scale):
    S = z8.shape[1]

    def body(b_ref, c_ref, a_ref, p_ref, w_ref, mix_ref, sc_ref, y_ref):
        g = pl.program_id(0)
        cg = c_ref[...] * a_ref[...]
        w = w_ref[...]
        conv = w[0:1] * _shift_down(cg, 2) + w[1:2] * _shift_down(cg, 1) + w[2:3] * cg
        y_ref[0] = (b_ref[...] * conv).astype(y_ref.dtype)
        for gg, win in enumerate(POOL_WINDOWS):
            @pl.when(g == gg)
            def _(win=win):
                p = p_ref[...]
                pooled = _window_sum(p, win, _shift_down) / _pool_count(p.shape, win) - p
                y_ref[1] = (_dot(pooled, mix_ref[...], "nn") * sc_ref[...]).astype(y_ref.dtype)

    return pl.pallas_call(
        body, name="ab_mix_fwd", grid=(4,), out_shape=jax.ShapeDtypeStruct((2, S, 512), _MXU_DTYPE),
        in_specs=_ab_specs(S) + [pl.BlockSpec((3, 128), lambda g: (0, g)), pl.BlockSpec((None, 128, 128), lambda g: (g, 0, 0)),
                                 pl.BlockSpec((1, 128), lambda g: (0, g))],
        out_specs=pl.BlockSpec((2, S, 128), lambda g: (0, 0, g)), compiler_params=_cp("parallel"))(z8, z8, z8, z8, conv_w, mix_w, scale)


def _ab_mix_bwd(z8, dycat2, conv_w, mix_w, scale, after):
    S = z8.shape[1]

    def body(b_ref, c_ref, a_ref, p_ref, dy_ref, w_ref, mix_ref, sc_ref, after_ref, dz_ref, dw_ref, dmix_ref, dsc_ref):
        g = pl.program_id(0)
        bv, cv, av, w = b_ref[...], c_ref[...], a_ref[...], w_ref[...]
        dya = dy_ref[0]
        cg = cv * av
        cg1, cg2 = _shift_down(cg, 1), _shift_down(cg, 2)
        conv = w[0:1] * cg2 + w[1:2] * cg1 + w[2:3] * cg
        dz_ref[0] = (dya * conv).astype(dz_ref.dtype)
        dconv = dya * bv
        dcg = w[2:3] * dconv + w[1:2] * _shift_up(dconv, 1) + w[0:1] * _shift_up(dconv, 2)
        dz_ref[1] = (dcg * av).astype(dz_ref.dtype)
        dz_ref[2] = (dcg * cv).astype(dz_ref.dtype)
        dw_ref[0:1, :] = jnp.sum(dconv * cg2, axis=0, keepdims=True)
        dw_ref[1:2, :] = jnp.sum(dconv * cg1, axis=0, keepdims=True)
        dw_ref[2:3, :] = jnp.sum(dconv * cg, axis=0, keepdims=True)
        for gg, win in enumerate(POOL_WINDOWS):
            @pl.when(g == gg)
            def _(win=win):
                p, dyb, mix = p_ref[...], dy_ref[1], mix_ref[...]
                cnt = _pool_count(p.shape, win)
                pooled = _window_sum(p, win, _shift_down) / cnt - p
                dsc_ref[...] = jnp.sum(dyb * _dot(pooled, mix, "nn"), axis=0, keepdims=True)
                dmixed = dyb * sc_ref[...]
                dmix_ref[...] = _dot(pooled, dmixed, "tn")
                dpooled = _dot(dmixed, mix, "nt")
                dz_ref[3] = (_window_sum(dpooled / cnt, win, _shift_up) - dpooled).astype(dz_ref.dtype)

    return pl.pallas_call(
        body, name="ab_mix_bwd", grid=(4,),
        out_shape=(jax.ShapeDtypeStruct((4, 2, S, 256), _MXU_DTYPE), jax.ShapeDtypeStruct((3, 512), F32),
                   jax.ShapeDtypeStruct((4, 128, 128), F32), jax.ShapeDtypeStruct((1, 512), F32)),
        in_specs=_ab_specs(S) + [pl.BlockSpec((2, S, 128), lambda g: (0, 0, g)), pl.BlockSpec((3, 128), lambda g: (0, g)),
                                 pl.BlockSpec((None, 128, 128), lambda g: (g, 0, 0)), pl.BlockSpec((1, 128), lambda g: (0, g)), ANY],
        out_specs=(pl.BlockSpec((4, None, S, 128), lambda g: (0, g // 2, 0, g % 2)), pl.BlockSpec((3, 128), lambda g: (0, g)),
                   pl.BlockSpec((None, 128, 128), lambda g: (g, 0, 0)), pl.BlockSpec((1, 128), lambda g: (0, g))),
        compiler_params=_cp("parallel"))(z8, z8, z8, z8, dycat2, conv_w, mix_w, scale, after)


HALO = 16


def _ffn_specs(S, n, tr):
    nb = S // HALO
    tile = pl.BlockSpec((2, None, tr, n), lambda j, i: (0, j, i, 0))
    prev = pl.BlockSpec((2, None, HALO, n), lambda j, i: (0, j, jnp.maximum(i * (tr // HALO) - 1, 0), 0))
    nxt = pl.BlockSpec((2, None, HALO, n), lambda j, i: (0, j, jnp.minimum((i + 1) * (tr // HALO), nb - 1), 0))
    cw = pl.BlockSpec((2, None, 3, n), lambda j, i: (0, j, 0, 0))
    return tile, prev, nxt, cw


def _conv_rows(ext, w, lo, tr):
    n = ext.shape[0]
    return (w[0:1] * pltpu.roll(ext, 2, axis=0)[lo:lo + tr] + w[1:2] * pltpu.roll(ext, 1, axis=0)[lo:lo + tr]
            + w[2:3] * ext[lo:lo + tr])


def _ffn_gate_fwd(name, u24, cw24, tr=256):
    _, J, S, n = u24.shape
    tile, prev, _, cw = _ffn_specs(S, n, tr)

    def body(u_ref, up_ref, w_ref, a_ref):
        keep = (pl.program_id(1) > 0).astype(F32)
        z = []
        for h in range(2):
            ext = jnp.concatenate([up_ref[h].astype(F32) * keep, u_ref[h].astype(F32)], axis=0)
            z.append(_conv_rows(ext, w_ref[h], HALO, tr))
        a_ref[...] = (_silu(z[0]) * z[1]).astype(a_ref.dtype)

    return pl.pallas_call(
        body, name=name, grid=(J, S // tr), out_shape=jax.ShapeDtypeStruct((J, S, n), _MXU_DTYPE),
        in_specs=[tile, prev, cw], out_specs=pl.BlockSpec((None, tr, n), lambda j, i: (j, i, 0)),
        compiler_params=_cp("parallel", "parallel"))(u24, u24, cw24)


def _ffn_gate_bwd(name, u24, cw24, da4, after, tr=256):
    _, J, S, n = u24.shape
    tile, prev, nxt, cw = _ffn_specs(S, n, tr)
    nb = S // HALO
    ext_rows = tr + 2 * HALO

    def body(u_ref, up_ref, un_ref, w_ref, da_ref, dan_ref, after_ref, du_ref, dcw_ref):
        i = pl.program_id(1)
        first = i == 0
        keep_prev = (i > 0).astype(F32)
        keep_next = (i < S // tr - 1).astype(F32)
        ext = [jnp.concatenate([up_ref[h].astype(F32) * keep_prev, u_ref[h].astype(F32), un_ref[h].astype(F32)], axis=0)
               for h in range(2)]
        w = [w_ref[h] for h in range(2)]
        zg = _conv_rows(ext[0], w[0], HALO, tr + HALO)
        zu = _conv_rows(ext[1], w[1], HALO, tr + HALO)
        da = jnp.concatenate([da_ref[...].astype(F32), dan_ref[...].astype(F32) * keep_next], axis=0)
        sg = jax.nn.sigmoid(zg)
        dz = [da * zu * (sg * (1.0 + zg * (1.0 - sg))), da * (zg * sg)]
        m = tr + HALO
        for h in range(2):
            d = dz[h]
            du = w[h][2:3] * d[:tr] + w[h][1:2] * pltpu.roll(d, m - 1, axis=0)[:tr] + w[h][0:1] * pltpu.roll(d, m - 2, axis=0)[:tr]
            du_ref[h] = du.astype(du_ref.dtype)
            dt = d[:tr]
            e = ext[h]
            parts = [jnp.sum(dt * pltpu.roll(e, 2, axis=0)[HALO:HALO + tr], axis=0, keepdims=True),
                     jnp.sum(dt * pltpu.roll(e, 1, axis=0)[HALO:HALO + tr], axis=0, keepdims=True),
                     jnp.sum(dt * e[HALO:HALO + tr], axis=0, keepdims=True)]
            for k in range(3):
                @pl.when(first)
                def _(k=k, h=h):
                    dcw_ref[h, k:k + 1, :] = parts[k]

                @pl.when(jnp.logical_not(first))
                def _(k=k, h=h):
                    dcw_ref[h, k:k + 1, :] += parts[k]

    da_tile = pl.BlockSpec((None, tr, n), lambda j, i: (j, i, 0))
    da_next = pl.BlockSpec((None, HALO, n), lambda j, i: (j, jnp.minimum((i + 1) * (tr // HALO), nb - 1), 0))
    return pl.pallas_call(
        body, name=name, grid=(J, S // tr),
        out_shape=(jax.ShapeDtypeStruct((2, J, S, n), _MXU_DTYPE), jax.ShapeDtypeStruct((2, J, 3, n), F32)),
        in_specs=[tile, prev, nxt, cw, da_tile, da_next, ANY], out_specs=(tile, cw),
        compiler_params=_cp("parallel", "arbitrary"))(u24, u24, u24, cw24, da4, da4, after)


def _rms_rows(v, g):
    rstd = lax.rsqrt(jnp.mean(v * v, axis=-1, keepdims=True) + EPS)
    return v * rstd * g


def _rms_rows_bwd(v, g, dy):
    rstd = lax.rsqrt(jnp.mean(v * v, axis=-1, keepdims=True) + EPS)
    vhat = v * rstd
    dvhat = dy * g
    return rstd * (dvhat - vhat * jnp.mean(dvhat * vhat, axis=-1, keepdims=True)), dy * vhat


def _mla_prep_fwd(z, qg, kvg, tr=256):
    S = z.shape[0]

    def body(q_ref, kv_ref, qg_ref, kvg_ref, qn_ref, kvn_ref):
        qn_ref[...] = _rms_rows(q_ref[...], qg_ref[...]).astype(qn_ref.dtype)
        kvn_ref[...] = _rms_rows(kv_ref[...], kvg_ref[...]).astype(kvn_ref.dtype)

    return pl.pallas_call(
        body, name="mla_prep_fwd", grid=(S // tr,),
        out_shape=(jax.ShapeDtypeStruct((S, 256), _MXU_DTYPE), jax.ShapeDtypeStruct((S, 128), _MXU_DTYPE)),
        in_specs=[pl.BlockSpec((tr, 256), lambda i: (i, 0)), pl.BlockSpec((tr, 128), lambda i: (i, 2)), _vec_spec(256), _vec_spec(128)],
        out_specs=(_row_spec(tr, 256), _row_spec(tr, 128)), compiler_params=_cp("parallel"))(z, z, qg, kvg)


def _mla_prep_bwd(z, qg, kvg, dqn, dkvn, dkpe, duv, tr=256):
    S = z.shape[0]

    def body(q_ref, kv_ref, qg_ref, kvg_ref, dqn_ref, dkvn_ref, dkpe_ref, duv_ref, dz_ref, dqg_ref, dkvg_ref):
        first = pl.program_id(0) == 0
        dq, dqg = _rms_rows_bwd(q_ref[...], qg_ref[...], dqn_ref[...])
        dkv, dkvg = _rms_rows_bwd(kv_ref[...], kvg_ref[...], dkvn_ref[...])
        _acc_rows(dqg_ref, dqg, first)
        _acc_rows(dkvg_ref, dkvg, first)
        dz_ref[:, 0:256] = dq.astype(dz_ref.dtype)
        dz_ref[:, 256:384] = dkv.astype(dz_ref.dtype)
        dz_ref[:, 384:512] = dkpe_ref[...].astype(dz_ref.dtype)
        dz_ref[:, 512:1536] = duv_ref[...].astype(dz_ref.dtype)

    return pl.pallas_call(
        body, name="mla_prep_bwd", grid=(S // tr,),
        out_shape=(jax.ShapeDtypeStruct((S, 1536), _MXU_DTYPE), jax.ShapeDtypeStruct((1, 256), F32), jax.ShapeDtypeStruct((1, 128), F32)),
        in_specs=[pl.BlockSpec((tr, 256), lambda i: (i, 0)), pl.BlockSpec((tr, 128), lambda i: (i, 2)), _vec_spec(256), _vec_spec(128),
                  _row_spec(tr, 256), _row_spec(tr, 128), _row_spec(tr, 128), _row_spec(tr, 1024)],
        out_specs=(_row_spec(tr, 1536), _vec_spec(256), _vec_spec(128)),
        compiler_params=_cp("arbitrary"))(z, z, qg, kvg, dqn, dkvn, dkpe, duv)


def _rope(v, cos, sa, sb):
    return v * cos + pltpu.roll(v, 112, axis=1) * sa + pltpu.roll(v, 16, axis=1) * sb


def _rope_t(d, cos, sa, sb):
    return d * cos + pltpu.roll(d * sa, 16, axis=1) + pltpu.roll(d * sb, 112, axis=1)


def _rope_fwd(qraw, kvall, z, cosq, cosk, sa, sb, tr=256):
    S = qraw.shape[0]

    def body(q_ref, k_ref, v_ref, kpe_ref, cq_ref, ck_ref, sa_ref, sb_ref, qo_ref, ko_ref, vo_ref):
        cq, ck, sa_v, sb_v = cq_ref[...], ck_ref[...], sa_ref[...], sb_ref[...]
        kpe = _rope(kpe_ref[...], ck, sa_v, sb_v)
        for h in range(8):
            cols = slice(128 * h, 128 * h + 128)
            qo_ref[:, cols] = _rope(q_ref[:, cols], cq, sa_v, sb_v).astype(qo_ref.dtype)
            ko_ref[:, cols] = (k_ref[:, cols] + kpe).astype(ko_ref.dtype)
        vo_ref[...] = v_ref[...].astype(vo_ref.dtype)

    tab = _row_spec(tr, 128)
    return pl.pallas_call(
        body, name="rope_fwd", grid=(S // tr,),
        out_shape=(jax.ShapeDtypeStruct((S, 1024), _MXU_DTYPE), jax.ShapeDtypeStruct((S, 1024), _MXU_DTYPE),
                   jax.ShapeDtypeStruct((S, 512), _MXU_DTYPE)),
        in_specs=[_row_spec(tr, 1024), pl.BlockSpec((tr, 1024), lambda i: (i, 0)), pl.BlockSpec((tr, 512), lambda i: (i, 2)),
                  pl.BlockSpec((tr, 128), lambda i: (i, 3)), tab, tab, tab, tab],
        out_specs=(_row_spec(tr, 1024), _row_spec(tr, 1024), _row_spec(tr, 512)),
        compiler_params=_cp("parallel"))(qraw, kvall, kvall, z, cosq, cosk, sa, sb)


def _rope_bwd(dq, dk, dv, cosq, cosk, sa, sb, tr=256):
    S = dq.shape[0]

    def body(dq_ref, dk_ref, dv_ref, cq_ref, ck_ref, sa_ref, sb_ref, dqo_ref, dkv_ref, dkpe_ref):
        cq, ck, sa_v, sb_v = cq_ref[...], ck_ref[...], sa_ref[...], sb_ref[...]
        tot = jnp.zeros((tr, 128), F32)
        for h in range(8):
            cols = slice(128 * h, 128 * h + 128)
            dqo_ref[:, cols] = _rope_t(dq_ref[:, cols], cq, sa_v, sb_v).astype(dqo_ref.dtype)
            dkh = dk_ref[:, cols]
            tot = tot + dkh
            dkv_ref[:, cols] = dkh.astype(dkv_ref.dtype)
        dkv_ref[:, 1024:1536] = dv_ref[...].astype(dkv_ref.dtype)
        dkpe_ref[...] = _rope_t(tot, ck, sa_v, sb_v)

    tab = _row_spec(tr, 128)
    return pl.pallas_call(
        body, name="rope_bwd", grid=(S // tr,),
        out_shape=(jax.ShapeDtypeStruct((S, 1024), _MXU_DTYPE), jax.ShapeDtypeStruct((S, 1536), _MXU_DTYPE),
                   jax.ShapeDtypeStruct((S, 128), F32)),
        in_specs=[_row_spec(tr, 1024), _row_spec(tr, 1024), _row_spec(tr, 512), tab, tab, tab, tab],
        out_specs=(_row_spec(tr, 1024), _row_spec(tr, 1536), _row_spec(tr, 128)),
        compiler_params=_cp("parallel"))(dq, dk, dv, cosq, cosk, sa, sb)


NEG = -1e30


def _attn_fwd(q, k, v, tq=256, tk=256):
    S = q.shape[0]
    assert tq == tk

    def body(q_ref, k_ref, v_ref, o_ref, lse_ref):
        i = pl.program_id(1)
        qs = [q_ref[:, 0:128], q_ref[:, 128:256]]

        def step(kb, carry, diagonal=False):
            start = pl.multiple_of(kb * tk, tk)
            vv = v_ref[pl.ds(start, tk), :]
            out = []
            for h in range(2):
                m, l, acc = carry[3 * h:3 * h + 3]
                s = _dot(qs[h], k_ref[pl.ds(start, tk), 128 * h:128 * h + 128], "nt") * ATTN_SCALE
                if diagonal:
                    s = jnp.where(below, s, NEG)
                m_new = jnp.maximum(m, jnp.max(s, axis=-1, keepdims=True))
                alpha = jnp.exp(m - m_new)
                p = jnp.exp(s - m_new)
                out += [m_new, alpha * l + jnp.sum(p, axis=-1, keepdims=True), alpha * acc + _dot(p, vv, "nn")]
            return tuple(out)

        below = lax.broadcasted_iota(jnp.int32, (tq, tk), 1) <= lax.broadcasted_iota(jnp.int32, (tq, tk), 0)
        init = (jnp.full((tq, 1), NEG, F32), jnp.zeros((tq, 1), F32), jnp.zeros((tq, 128), F32)) * 2
        ma, la, acca, mb, lb, accb = step(i, lax.fori_loop(0, i, step, init), diagonal=True)
        lane = lax.broadcasted_iota(jnp.int32, (tq, 128), 1)
        o_ref[...] = jnp.where(lane < 64, acca / la, accb / lb)
        lse_ref[...] = jnp.where(lane < 64, ma + jnp.log(la), mb + jnp.log(lb))

    return pl.pallas_call(
        body, name="attn_fwd", grid=(4, S // tq),
        out_shape=(jax.ShapeDtypeStruct((S, 512), F32), jax.ShapeDtypeStruct((4, S, 128), F32)),
        in_specs=[pl.BlockSpec((tq, 256), lambda p, i: (i, p)), pl.BlockSpec((S, 256), lambda p, i: (0, p)),
                  pl.BlockSpec((S, 128), lambda p, i: (0, p))],
        out_specs=(pl.BlockSpec((tq, 128), lambda p, i: (i, p)), pl.BlockSpec((None, tq, 128), lambda p, i: (p, i, 0))),
        compiler_params=_cp("parallel", "parallel"))(q, k, v)


def _attn_bwd(q, k, v, o, lse, dycat2, tq=256, tk=256):
    S = q.shape[0]
    assert tq == tk

    def body(q_ref, k_ref, v_ref, o_ref, lse_ref, do_ref, dq_ref, dk_ref, dv_ref):
        j = pl.program_id(1)

        @pl.when(j == 0)
        def _():
            dq_ref[...] = jnp.zeros_like(dq_ref)

        below = lax.broadcasted_iota(jnp.int32, (tq, tk), 1) <= lax.broadcasted_iota(jnp.int32, (tq, tk), 0)
        lane = lax.broadcasted_iota(jnp.int32, (tq, 128), 1)
        ks = [k_ref[:, 0:128], k_ref[:, 128:256]]
        vv = v_ref[...]

        def step(qb, carry, diagonal=False):
            dka, dkb, dvp = carry
            start = pl.multiple_of(qb * tq, tq)
            rows = pl.ds(start, tq)
            do, lse_v = do_ref[rows, :], lse_ref[rows, :]
            prod = do * o_ref[rows, :]
            dks = [dka, dkb]
            for h in range(2):
                mine = (lane < 64) if h == 0 else (lane >= 64)
                delta = jnp.sum(jnp.where(mine, prod, 0.0), axis=-1, keepdims=True)
                do_h = jnp.where(mine, do, 0.0)
                qh = q_ref[rows, 128 * h:128 * h + 128]
                s = _dot(qh, ks[h], "nt") * ATTN_SCALE
                p = jnp.exp(s - lse_v[:, 64 * h:64 * h + 1])
                if diagonal:
                    p = jnp.where(below, p, 0.0)
                dvp = dvp + _dot(p, do_h, "tn")
                ds = p * (_dot(do_h, vv, "nt") - delta) * ATTN_SCALE
                dq_ref[rows, 128 * h:128 * h + 128] += _dot(ds, ks[h], "nn")
                dks[h] = dks[h] + _dot(ds, qh, "tn")
            return dks[0], dks[1], dvp

        zero = jnp.zeros((tk, 128), F32)
        dka, dkb, dvp = lax.fori_loop(j + 1, S // tq, step, step(j, (zero, zero, zero), diagonal=True))
        dk_ref[:, 0:128] = dka
        dk_ref[:, 128:256] = dkb
        dv_ref[...] = dvp

    return pl.pallas_call(
        body, name="attn_bwd", grid=(4, S // tk),
        out_shape=(jax.ShapeDtypeStruct((S, 1024), F32), jax.ShapeDtypeStruct((S, 1024), F32), jax.ShapeDtypeStruct((S, 512), F32)),
        in_specs=[pl.BlockSpec((S, 256), lambda p, j: (0, p)), pl.BlockSpec((tk, 256), lambda p, j: (j, p)),
                  pl.BlockSpec((tk, 128), lambda p, j: (j, p)), pl.BlockSpec((S, 128), lambda p, j: (0, p)),
                  pl.BlockSpec((None, S, 128), lambda p, j: (p, 0, 0)), pl.BlockSpec((None, S, 128), lambda p, j: (0, 0, p))],
        out_specs=(pl.BlockSpec((S, 256), lambda p, j: (0, p)), pl.BlockSpec((tk, 256), lambda p, j: (j, p)),
                   pl.BlockSpec((tk, 128), lambda p, j: (j, p))),
        compiler_params=_cp("parallel", "arbitrary"))(q, k, v, o, lse, dycat2)


CHUNK = 128
GELU_C = math.sqrt(2.0 / math.pi)


def _gelu(v):
    t = jnp.tanh(GELU_C * (v + 0.044715 * (v * v * v)))
    return v * (0.5 * (1.0 + t)), t


def _gelu_grad(v, t):
    return 0.5 * (1.0 + t) + v * (0.5 * (1.0 - t * t) * GELU_C * (1.0 + 3.0 * 0.044715 * v * v))


def _tril(w):
    r = lax.broadcasted_iota(jnp.int32, w.shape, 0)
    c = lax.broadcasted_iota(jnp.int32, w.shape, 1)
    return jnp.where(c <= r, w, 0.0)


def _layer_norm(v, g, b):
    xc = v - jnp.mean(v, axis=-1, keepdims=True)
    rstd = lax.rsqrt(jnp.mean(xc * xc, axis=-1, keepdims=True) + EPS)
    xhat = xc * rstd
    return xhat * g + b, xhat, rstd


def _sgu_fwd(z, o, ln_g, ln_b, w_s, b_st, tr=256):
    S = z.shape[0]

    def body(u_ref, v_ref, o_ref, g_ref, b_ref, ws_ref, bs_ref, y_ref):
        gu, _ = _gelu(u_ref[...])
        gv, _ = _gelu(v_ref[...])
        vln, _, _ = _layer_norm(gv, g_ref[...], b_ref[...])
        y_ref[0] = o_ref[...].astype(y_ref.dtype)
        for g in range(4):
            wt = _tril(ws_ref[g])
            cols = slice(128 * g, 128 * g + 128)
            for ch in range(tr // CHUNK):
                rows = slice(CHUNK * ch, CHUNK * ch + CHUNK)
                mixed = _dot(wt, vln[rows, cols], "nn") + bs_ref[:, g:g + 1]
                y_ref[1, rows, cols] = (gu[rows, cols] * mixed).astype(y_ref.dtype)

    return pl.pallas_call(
        body, name="sgu_fwd", grid=(S // tr,), out_shape=jax.ShapeDtypeStruct((2, S, 512), _MXU_DTYPE),
        in_specs=[pl.BlockSpec((tr, 512), lambda i: (i, 1)), pl.BlockSpec((tr, 512), lambda i: (i, 2)), _row_spec(tr, 512),
                  _vec_spec(512), _vec_spec(512), pl.BlockSpec((4, 128, 128), lambda i: (0, 0, 0)), pl.BlockSpec((128, 4), lambda i: (0, 0))],
        out_specs=pl.BlockSpec((2, tr, 512), lambda i: (0, i, 0)), compiler_params=_cp("parallel"))(z, z, o, ln_g, ln_b, w_s, b_st)


def _sgu_bwd(z, dycat2, ln_g, ln_b, w_s, b_st, tr=256):
    S = z.shape[0]

    def body(u_ref, v_ref, dy_ref, g_ref, b_ref, ws_ref, bs_ref, duv_ref, dg_ref, db_ref, dws_ref, dbs_ref):
        first = pl.program_id(0) == 0
        u_pre, v_pre = u_ref[...], v_ref[...]
        gu, tu = _gelu(u_pre)
        gv, tv = _gelu(v_pre)
        gain = g_ref[...]
        vln, xhat, rstd = _layer_norm(gv, gain, b_ref[...])

        @pl.when(first)
        def _():
            dws_ref[...] = jnp.zeros_like(dws_ref)
            dbs_ref[...] = jnp.zeros_like(dbs_ref)

        dvln_cols = []
        for g in range(4):
            wt = _tril(ws_ref[g])
            cols = slice(128 * g, 128 * g + 128)
            dmixed_sum = jnp.zeros((CHUNK, 128), F32)
            dw = jnp.zeros((CHUNK, CHUNK), F32)
            dvln_rows = []
            for ch in range(tr // CHUNK):
                rows = slice(CHUNK * ch, CHUNK * ch + CHUNK)
                vt = vln[rows, cols]
                mixed = _dot(wt, vt, "nn") + bs_ref[:, g:g + 1]
                dyd = dy_ref[rows, cols]
                duv_ref[rows, cols] = (dyd * mixed * _gelu_grad(u_pre[rows, cols], tu[rows, cols])).astype(duv_ref.dtype)
                dmixed = dyd * gu[rows, cols]
                dmixed_sum = dmixed_sum + dmixed
                dw = dw + _dot(dmixed, vt, "nt")
                dvln_rows.append(_dot(wt, dmixed, "tn"))
            dws_ref[g] += _tril(dw)
            dbs_ref[g:g + 1, :] += jnp.sum(dmixed_sum.T, axis=0, keepdims=True)
            dvln_cols.append(jnp.concatenate(dvln_rows, axis=0))
        dvln = jnp.concatenate(dvln_cols, axis=1)
        _acc_rows(dg_ref, dvln * xhat, first)
        _acc_rows(db_ref, dvln, first)
        dxhat = dvln * gain
        dgv = rstd * (dxhat - jnp.mean(dxhat, axis=-1, keepdims=True) - xhat * jnp.mean(dxhat * xhat, axis=-1, keepdims=True))
        duv_ref[:, 512:1024] = (dgv * _gelu_grad(v_pre, tv)).astype(duv_ref.dtype)

    return pl.pallas_call(
        body, name="sgu_bwd", grid=(S // tr,),
        out_shape=(jax.ShapeDtypeStruct((S, 1024), _MXU_DTYPE), jax.ShapeDtypeStruct((1, 512), F32), jax.ShapeDtypeStruct((1, 512), F32),
                   jax.ShapeDtypeStruct((4, 128, 128), F32), jax.ShapeDtypeStruct((4, 128), F32)),
        in_specs=[pl.BlockSpec((tr, 512), lambda i: (i, 1)), pl.BlockSpec((tr, 512), lambda i: (i, 2)),
                  pl.BlockSpec((None, tr, 512), lambda i: (1, i, 0)), _vec_spec(512), _vec_spec(512),
                  pl.BlockSpec((4, 128, 128), lambda i: (0, 0, 0)), pl.BlockSpec((128, 4), lambda i: (0, 0))],
        out_specs=(_row_spec(tr, 1024), _vec_spec(512), _vec_spec(512), pl.BlockSpec((4, 128, 128), lambda i: (0, 0, 0)),
                   pl.BlockSpec((4, 128), lambda i: (0, 0))),
        compiler_params=_cp("arbitrary"))(z, z, dycat2, ln_g, ln_b, w_s, b_st)


def _sum_parts(name, parts, tr=512):
    P, R, C = parts.shape
    tr = _tile(R, tr) if R % 8 == 0 else R

    def body(p_ref, o_ref):
        g = p_ref[0]
        for k in range(1, P):
            g = g + p_ref[k]
        o_ref[...] = g

    return pl.pallas_call(
        body, name=name, grid=(R // tr,), out_shape=jax.ShapeDtypeStruct((R, C), F32),
        in_specs=[pl.BlockSpec((P, tr, C), lambda i: (0, i, 0))], out_specs=_row_spec(tr, C),
        compiler_params=_cp("parallel"))(parts)


def _adamw_math(w, m, v, g):
    c1 = 1.0 / (1.0 - ADAM_B1 ** ADAM_STEP)
    c2 = 1.0 / (1.0 - ADAM_B2 ** ADAM_STEP)
    m2 = ADAM_B1 * m + (1.0 - ADAM_B1) * g
    v2 = ADAM_B2 * v + (1.0 - ADAM_B2) * (g * g)
    return -ADAM_LR * ((m2 * c1) / (jnp.sqrt(v2 * c2) + ADAM_EPS) + ADAM_WD * w), m2, v2


def _adamw_small(name, params, parts):
    n = len(params)

    def body(*refs):
        ins, outs = refs[:4 * n], refs[4 * n:]
        for i in range(n):
            w_ref, m_ref, v_ref, p_ref = ins[4 * i:4 * i + 4]
            g = p_ref[0]
            for k in range(1, N_DEV):
                g = g + p_ref[k]
            delta, m2, v2 = _adamw_math(w_ref[...], m_ref[...], v_ref[...], g)
            outs[4 * i][...] = g
            outs[4 * i + 1][...] = delta
            outs[4 * i + 2][...] = m2
            outs[4 * i + 3][...] = v2

    flat = [a for (w, m, v), p in zip(params, parts) for a in (w, m, v, p)]
    out = pl.pallas_call(
        body, name=name, out_shape=[jax.ShapeDtypeStruct(w.shape, F32) for (w, _, _) in params for _ in range(4)],
        compiler_params=pltpu.CompilerParams(vmem_limit_bytes=_VMEM_LIMIT))(*flat)
    return [out[4 * i:4 * i + 4] for i in range(n)]


ADAMW_BLOCK_BYTES = 36 * 2 ** 20


def _adamw(name, w, m, v, parts):
    L, R, C = w.shape
    P = parts[0].shape[0]
    row_bytes = 2 * C * (7 * 4 + P * parts[0].dtype.itemsize)
    tr = R
    if R * row_bytes > ADAMW_BLOCK_BYTES:
        tr = next(t for t in (1024, 512, 256, 128, 64, 32, 16) if R % t == 0 and t * row_bytes <= ADAMW_BLOCK_BYTES)
    nr = R // tr
    c1 = 1.0 / (1.0 - ADAM_B1 ** ADAM_STEP)
    c2 = 1.0 / (1.0 - ADAM_B2 ** ADAM_STEP)

    def body(w_ref, m_ref, v_ref, *rest):
        p_refs, (g_ref, d_ref, mo_ref, vo_ref) = rest[:L], rest[L:]
        for ll in range(L):
            @pl.when(pl.program_id(0) == ll)
            def _(p_ref=p_refs[ll]):
                g = p_ref[0].astype(F32)
                for k in range(1, P):
                    g = g + p_ref[k].astype(F32)
                m2 = ADAM_B1 * m_ref[...] + (1.0 - ADAM_B1) * g
                v2 = ADAM_B2 * v_ref[...] + (1.0 - ADAM_B2) * (g * g)
                g_ref[...] = g
                mo_ref[...] = m2
                vo_ref[...] = v2
                d_ref[...] = -ADAM_LR * ((m2 * c1) / (jnp.sqrt(v2 * c2) + ADAM_EPS) + ADAM_WD * w_ref[...])

    def part_spec(ll):
        return pl.BlockSpec((P, tr, C), lambda l, i: (0, jnp.where(l == ll, i, jnp.where(l < ll, 0, nr - 1)), 0))

    full = pl.BlockSpec((None, tr, C), lambda l, i: (l, i, 0))
    sds = jax.ShapeDtypeStruct((L, R, C), F32)
    return pl.pallas_call(
        body, name=name, grid=(L, nr), out_shape=(sds, sds, sds, sds),
        in_specs=[full] * 3 + [part_spec(ll) for ll in range(L)],
        out_specs=(full,) * 4, compiler_params=_cp("arbitrary", "arbitrary"))(w, m, v, *parts)


def _rope_tables(positions):
    half = 16
    inv_freq = 10000.0 ** (-jnp.arange(half, dtype=F32) / half)
    ang = positions.astype(F32)[:, None] * inv_freq
    cos, sin = jnp.cos(ang), jnp.sin(ang)
    S = positions.shape[0]
    z16, z32, z64 = jnp.zeros((S, 16), F32), jnp.zeros((S, 32), F32), jnp.zeros((S, 64), F32)
    cosk = jnp.concatenate([z64, cos, cos, z32], axis=1)
    cosq = jnp.concatenate([jnp.ones((S, 64), F32), cos, cos, z32], axis=1)
    sa = jnp.concatenate([z64, -sin, z16, z32], axis=1)
    sb = jnp.concatenate([z64, z16, sin, z32], axis=1)
    return cosq, cosk, sa, sb


def _ffn_fwd(l, x, mod, n2g, get_w_up8, cw24, get_w_down4):
    sh, sc, gate = mod
    h = _rmsmod_fwd(f"ffn{l}_norm", x, n2g, sc, sh, n2g)
    w_up8 = get_w_up8(h)
    u8 = _mm_cols(f"ffn{l}_up", h, w_up8, out_dtype=ACT_DTYPE, tm=1024)
    S, n = u8.shape[1], u8.shape[2]
    u24 = u8.reshape(2, 4, S, n)
    a4 = _ffn_gate_fwd(f"ffn{l}_gate", u24, cw24)
    w_down4 = get_w_down4(a4)
    f, x_new = _mm_rows_resid(f"ffn{l}_down", a4, w_down4, x, gate)
    return x_new, (x, h, u24, a4, f), w_up8, w_down4


def _ffn_bwd(l, dx, saved, mod, n2g, w_up8, cw24, w_down4, me):
    sh, sc, gate = mod
    x, h, u24, a4, f = saved
    df, dgate = _gate_bwd(f"ffn{l}_gate_bwd", dx, f, gate)
    da4 = _mm_rows_dx(f"ffn{l}_down_dx", df, w_down4, out_dtype=ACT_DTYPE, tm=2048)
    dw_down4 = _mm_rows_dw(f"ffn{l}_down_dw", a4, df, out_dtype=WIRE_DTYPE)
    sent_down, token = _exchange_start(f"scatter_ffn{l}_down", [dw_down4.reshape(8, 352, dw_down4.shape[2])], True, dgate, me)
    du24, dcw24 = _ffn_gate_bwd(f"ffn{l}_act_bwd", u24, cw24, da4, token)
    du8 = du24.reshape((8,) + du24.shape[2:])
    dw_up8t = _mm_cols_dwt(f"ffn{l}_up_dw", h, du8, out_dtype=WIRE_DTYPE, tk=1024)
    sent_up, token = _exchange_start(f"scatter_ffn{l}_up", [dw_up8t], True, dcw24, me)
    dh = _mm_cols_dx(f"ffn{l}_up_dx", du8, w_up8, tm=1024, jb=4)
    dx_new, dn2g, dsc, dsh = _rmsmod_bwd(f"ffn{l}_norm_bwd", x, n2g, sc, dh, dx, token)
    return dx_new, dict(sent_up=sent_up, sent_down=sent_down, cw24=dcw24, n2g=dn2g, mod=(dsh, dsc, dgate))


def kernel(x, c, positions, ada_w, ada_b, norm1_g, norm2_g, ab_w_in, a_conv_w, b_mix_w, b_scale, ab_w_out, cd_w_in, c_q_norm_g, c_w_uq, c_kv_norm_g, c_w_ukv, d_ln_g, d_ln_b, d_w_s, d_b_s, cd_w_out, ffn_w_up, ffn_conv_w, ffn_w_down, final_norm_g, loss_target, m_ada_w, m_ada_b, m_norm1_g, m_norm2_g, m_ab_w_in, m_a_conv_w, m_b_mix_w, m_b_scale, m_ab_w_out, m_cd_w_in, m_c_q_norm_g, m_c_w_uq, m_c_kv_norm_g, m_c_w_ukv, m_d_ln_g, m_d_ln_b, m_d_w_s, m_d_b_s, m_cd_w_out, m_ffn_w_up, m_ffn_conv_w, m_ffn_w_down, m_final_norm_g, v_ada_w, v_ada_b, v_norm1_g, v_norm2_g, v_ab_w_in, v_a_conv_w, v_b_mix_w, v_b_scale, v_ab_w_out, v_cd_w_in, v_c_q_norm_g, v_c_w_uq, v_c_kv_norm_g, v_c_w_ukv, v_d_ln_g, v_d_ln_b, v_d_w_s, v_d_b_s, v_cd_w_out, v_ffn_w_up, v_ffn_conv_w, v_ffn_w_down, v_final_norm_g):
    S, D = x.shape[1], x.shape[2]
    me = 4 * lax.axis_index("x") + 2 * lax.axis_index("y") + lax.axis_index("c")
    x0, target = x[0], loss_target[0]
    W = _MXU_DTYPE

    small_shapes = [(1024,), (3, 64), (32,), (64,), (64,), (2, 3, 704)]
    (g0,) = _exchange("gather_small", [[_pack([c, a_conv_w, c_q_norm_g, d_ln_g, d_ln_b, ffn_conv_w])]], scatter=False)
    c_all, aconv_s, qg_s, lng_s, lnb_s, fcw_s = _unpack(g0[:, 0], small_shapes, lead=(N_DEV,))
    conv_w = aconv_s.transpose(1, 0, 2).reshape(3, 512)
    qg, ln_g, ln_b = qg_s.reshape(1, 256), lng_s.reshape(1, 512), lnb_s.reshape(1, 512)
    cw24 = [fcw_s[:, l].reshape(2, 4, 3, 704) for l in range(2)]
    c16 = jnp.pad(c_all, ((0, 16 - N_DEV), (0, 0)))

    mod_cols = _ada_fwd(c16, ada_w)
    (g1,) = _exchange("gather_mod", [[_pack([mod_cols])]], scatter=False)
    mod_all = _unpack(g1[:, 0], [(2, 16, 768)], lead=(N_DEV,))[0]
    mod_mine = lax.dynamic_index_in_dim(mod_all, me, axis=2, keepdims=False)
    mod = mod_mine.transpose(1, 0, 2).reshape(2, 6 * D) + ada_b
    mods = [[mod[l, k * D:(k + 1) * D].reshape(1, D) for k in range(6)] for l in range(2)]

    gw_ab, token = _hier_gather_start("gather_w_ab", [ab_w_in[0].astype(W), ab_w_out[0].astype(W)], mod, me)
    gw_up0, token = _hier_gather_start("gather_w_ffn0_up", [ffn_w_up[0].astype(W)], token, me)
    gw_dn0, token = _exchange_start("gather_w_ffn0_down", [ffn_w_down[0].astype(W)], False, token, me)
    gw_cd, token = _exchange_start("gather_w_cd", [
        cd_w_in[0].astype(W).reshape(1440, 128), c_w_uq[0].astype(W).reshape(192, 128), c_w_ukv[0].astype(W),
        cd_w_out[0].astype(W)], False, token, me)
    gw_up1, token = _exchange_start("gather_w_ffn1_up", [ffn_w_up[1].astype(W)], False, token, me)
    gw_dn1, started = _exchange_start("gather_w_ffn1_down", [ffn_w_down[1].astype(W)], False, token, me)

    cosq, cosk, sa, sb = _rope_tables(positions[0])
    n1g = [norm1_g[l].reshape(1, D) for l in range(2)]
    n2g = [norm2_g[l].reshape(1, D) for l in range(2)]
    mix_w, scale = b_mix_w[0], b_scale
    kvg = c_kv_norm_g
    w_s, b_st = d_w_s[0], d_b_s[0].T

    sh1, sc1, g1m = mods[0][:3]
    h_ab = _rmsmod_fwd("ab_norm", x0, n1g[0], sc1, sh1, started)
    w_abin8, w_about = _hier_gather_wait("wait_w_ab", _hier_gather_forward("forward_w_ab", gw_ab, h_ab), h_ab)
    w_about2 = w_about.reshape(2, 512, D)
    z8 = _mm_cols("ab_in", h_ab, w_abin8, tm=2048)
    ycat_ab = _ab_mix_fwd(z8, conv_w, mix_w, scale)
    y_ab, x1 = _mm_rows_resid("ab_out", ycat_ab, w_about2, x0, g1m)
    w_up8, w_down4 = [None, None], [None, None]
    gw_up0 = _hier_gather_forward("forward_w_ffn0_up", gw_up0, x1)
    x2, ffn0_saved, w_up8[0], w_down4[0] = _ffn_fwd(
        0, x1, mods[0][3:], n2g[0], lambda after: _hier_gather_wait("wait_w_ffn0_up", gw_up0, after)[0], cw24[0],
        lambda after: _exchange_wait("wait_w_ffn0_down", gw_dn0, after)[0].reshape(4, 704, D))

    w_cdin, w_uq, w_ukv, w_cdout = _exchange_wait("wait_w_cd", gw_cd, x2)
    w_cdout2 = w_cdout.reshape(2, 512, D)
    w_cd = w_cdin.reshape(8, D, 180).transpose(1, 0, 2).reshape(D, 1440)
    zc = lambda n: jnp.zeros((D, n), W)
    w_cd_pad = jnp.concatenate([w_cd[:, :384], zc(64), w_cd[:, 384:416], zc(32), w_cd[:, 416:]], axis=1)
    w_uq_pad = jnp.pad(w_uq.reshape(8, 256, 96).transpose(1, 0, 2), ((0, 0), (0, 0), (0, 32))).reshape(256, 1024)
    w_ukv_h = w_ukv.transpose(1, 0, 2)
    w_k_pad = jnp.pad(w_ukv_h[:, :, :64], ((0, 0), (0, 0), (0, 64))).reshape(128, 1024)
    w_kv_pad = jnp.concatenate([w_k_pad, w_ukv_h[:, :, 64:].reshape(128, 512)], axis=1)

    sh1, sc1, g1c = mods[1][:3]
    h_cd = _rmsmod_fwd("cd_norm", x2, n1g[1], sc1, sh1, n1g[1])
    z_cd = _mm_nn("cd_in", h_cd, w_cd_pad)
    qn, kvn = _mla_prep_fwd(z_cd, qg, kvg)
    qraw = _mm_nn("cd_uq", qn, w_uq_pad)
    kvall = _mm_nn("cd_ukv", kvn, w_kv_pad)
    q_r, k_r, v_r = _rope_fwd(qraw, kvall, z_cd, cosq, cosk, sa, sb)
    o, lse = _attn_fwd(q_r, k_r, v_r)
    ycat_cd = _sgu_fwd(z_cd, o, ln_g, ln_b, w_s, b_st)
    y_cd, x3 = _mm_rows_resid("cd_out", ycat_cd, w_cdout2, x2, g1c)
    x4, ffn1_saved, w_up8[1], w_down4[1] = _ffn_fwd(
        1, x3, mods[1][3:], n2g[1], lambda after: _exchange_wait("wait_w_ffn1_up", gw_up1, after)[0], cw24[1],
        lambda after: _exchange_wait("wait_w_ffn1_down", gw_dn1, after)[0].reshape(4, 704, D))

    loss_local, dx4, dfg = _loss_head(x4, final_norm_g.reshape(1, D), target)

    dx3, gf1 = _ffn_bwd(1, dx4, ffn1_saved, mods[1][3:], n2g[1], w_up8[1], cw24[1], w_down4[1], me)

    dy, dg1c = _gate_bwd("cd_gate_bwd", dx3, y_cd, g1c)
    dycat = _mm_rows_dx("cd_out_dx", dy, w_cdout2)
    dw_cdout = _mm_rows_dw("cd_out_dw", ycat_cd, dy, out_dtype=WIRE_DTYPE)
    duv, dln_g, dln_b, dws, dbs = _sgu_bwd(z_cd, dycat, ln_g, ln_b, w_s, b_st)
    dq_r, dk_r, dv_r = _attn_bwd(q_r, k_r, v_r, o, lse, dycat)
    dqraw, dkvall, dkpe = _rope_bwd(dq_r, dk_r, dv_r, cosq, cosk, sa, sb)
    dqn = _mm_nt("cd_uq_dx", dqraw, w_uq_pad, tn=256)
    dkvn = _mm_nt("cd_ukv_dx", dkvall, w_kv_pad, tn=128)
    dw_uq_pad = _mm_tn("cd_uq_dw", qn, dqraw, tm=256)
    dw_kv_pad = _mm_tn("cd_ukv_dw", kvn, dkvall, tm=128)
    dz_cd, dqg, dkvg = _mla_prep_bwd(z_cd, qg, kvg, dqn, dkvn, dkpe, duv)
    dh_cd = _mm_nt("cd_in_dx", dz_cd, w_cd_pad)
    dw_cd_pad = _mm_tn("cd_in_dw", h_cd, dz_cd)
    dw_cd =jnp.concatenate([dw_cd_pad[:, :384], dw_cd_pad[:, 448:480], dw_cd_pad[:, 512:]], axis=1)
    dw_cd8 = dw_cd.reshape(D, 8, 180).transpose(1, 0, 2).reshape(8, 1440, 128).astype(WIRE_DTYPE)
    dw_uq8 = dw_uq_pad.reshape(256, 8, 128)[:, :, :96].transpose(1, 0, 2).reshape(8, 192, 128).astype(WIRE_DTYPE)
    dw_ukv8 = jnp.concatenate([dw_kv_pad[:, :1024].reshape(128, 8, 128)[:, :, :64], dw_kv_pad[:, 1024:].reshape(128, 8, 64)],
                              axis=2).transpose(1, 0, 2).astype(WIRE_DTYPE)
    sent_cd, token = _exchange_start("scatter_cd", [dw_cd8, dw_uq8, dw_ukv8, dw_cdout.reshape(8, 128, D)], True, dqg, me)
    early_names = ["c_kv_norm_g", "d_w_s", "d_b_s", "final_norm_g", "c_q_norm_g", "d_ln_g", "d_ln_b"]
    early_grads = [dkvg, dws.reshape(512, 128), dbs, dfg, dqg.reshape(8, 1, 32), dln_g.reshape(8, 1, 64), dln_b.reshape(8, 1, 64)]
    early_sent, token = _exchange_start("gather_small_grads_early", early_grads, [False] * 4 + [True] * 3, token, me)
    dx2, dn1g_cd, dsc1_cd, dsh1_cd = _rmsmod_bwd("cd_norm_bwd", x2, n1g[1], sc1, dh_cd, dx3, token)

    dx1, gf0 = _ffn_bwd(0, dx2, ffn0_saved, mods[0][3:], n2g[0], w_up8[0], cw24[0], w_down4[0], me)

    dy, dg1m = _gate_bwd("ab_gate_bwd", dx1, y_ab, g1m)
    dw_about = _mm_rows_dw("ab_out_dw", ycat_ab, dy, out_dtype=WIRE_DTYPE)
    sent_about, token = _exchange_start("scatter_ab_out", [dw_about.reshape(8, 128, D)], True, dg1m, me)
    dycat = _mm_rows_dx("ab_out_dx", dy, w_about2)
    dz8, dconv_w, dmix_w, dscale = _ab_mix_bwd(z8, dycat, conv_w, mix_w, scale, token)
    dz8 = dz8.reshape(8, S, 256)
    dw_abin8 = _mm_cols_dw("ab_in_dw", h_ab, dz8, out_dtype=WIRE_DTYPE, tk=1024)
    sent_abin, token = _exchange_start("scatter_ab_in", [dw_abin8], True, dscale, me)
    dh_ab = _mm_cols_dx("ab_in_dx", dz8, w_abin8)
    dx0, dn1g_ab, dsc1_ab, dsh1_ab = _rmsmod_bwd("ab_norm_bwd", x0, n1g[0], mods[0][1], dh_ab, dx1, token)

    dmod = jnp.stack([jnp.concatenate([dsh1_ab, dsc1_ab, dg1m, *gf0["mod"]], axis=1)[0],
                      jnp.concatenate([dsh1_cd, dsc1_cd, dg1c, *gf1["mod"]], axis=1)[0]])
    late_names = ["ada_b", "norm1_g", "norm2_g", "b_mix_w", "b_scale", "a_conv_w", "ffn_conv_w"]
    late_grads = [dmod, jnp.concatenate([dn1g_ab, dn1g_cd]), jnp.concatenate([gf0["n2g"], gf1["n2g"]]),
                  dmix_w.reshape(512, 128), dscale, dconv_w.reshape(3, 8, 64).transpose(1, 0, 2),
                  jnp.stack([gf0["cw24"].reshape(8, 3, 704), gf1["cw24"].reshape(8, 3, 704)], axis=1)]
    small_view = dict(ada_b=(2, 6 * D), norm1_g=(2, D), norm2_g=(2, D), b_mix_w=(512, 128), b_scale=(1, 512), c_kv_norm_g=(1, 128),
                      d_w_s=(512, 128), d_b_s=(4, 128), final_norm_g=(1, D),
                      a_conv_w=(3, 64), c_q_norm_g=(1, 32), d_ln_g=(1, 64), d_ln_b=(1, 64), ffn_conv_w=(2, 3, 704))
    late_sent, token = _exchange_start("gather_small_grads_late", late_grads, [False] * 5 + [True] * 2, dx0, me)

    res = {}

    def update(name, w, m, v, parts, shape3d):
        outs = _adamw("adamw_" + name, w.reshape(shape3d), m.reshape(shape3d), v.reshape(shape3d),
                      [p.reshape((p.shape[0],) + shape3d[1:]) for p in parts])
        res[name] = [o_.reshape(w.shape) for o_ in outs]

    p_cdin, p_uq, p_ukv, p_cdout = _exchange_wait("wait_scatter_cd", sent_cd, token)
    update("cd_w_in", cd_w_in, m_cd_w_in, v_cd_w_in, [p_cdin], (1, 1440, 128))
    update("c_w_uq", c_w_uq, m_c_w_uq, v_c_w_uq, [p_uq], (1, 192, 128))
    update("c_w_ukv", c_w_ukv, m_c_w_ukv, v_c_w_ukv, [p_ukv], (1, 128, 128))
    update("cd_w_out", cd_w_out, m_cd_w_out, v_cd_w_out, [p_cdout], (1, 128, D))
    (p_dn1,) = _exchange_wait("wait_scatter_ffn1_down", gf1["sent_down"], token)
    (p_dn0,) = _exchange_wait("wait_scatter_ffn0_down", gf0["sent_down"], res["cd_w_out"][0])
    update("ffn_w_down", ffn_w_down, m_ffn_w_down, v_ffn_w_down, [p_dn0, p_dn1], (2, 352, D))
    (p_up1,) = _exchange_wait("wait_scatter_ffn1_up", gf1["sent_up"], token)
    (p_up0,) = _exchange_wait("wait_scatter_ffn0_up", gf0["sent_up"], res["ffn_w_down"][0])
    swap = lambda a: jnp.swapaxes(a, 1, 2)
    update("ffn_w_up", swap(ffn_w_up), swap(m_ffn_w_up), swap(v_ffn_w_up), [p_up0, p_up1], (2, 704, D))
    up_done = res["ffn_w_up"][0]
    res["ffn_w_up"] = [swap(o_) for o_ in res["ffn_w_up"]]
    (p_about,) = _exchange_wait("wait_scatter_ab_out", sent_about, up_done)
    update("ab_w_out", ab_w_out, m_ab_w_out, v_ab_w_out, [p_about], (1, 128, D))
    (p_abin,) = _exchange_wait("wait_scatter_ab_in", sent_abin, res["ab_w_out"][0])
    update("ab_w_in", ab_w_in, m_ab_w_in, v_ab_w_in, [p_abin], (1, D, 256))

    early_parts = _exchange_wait("wait_small_grads_early", early_sent, res["ab_w_in"][0])
    late_parts = _exchange_wait("wait_small_grads_late", late_sent, res["ab_w_in"][0])
    small_names = early_names + late_names
    small_parts = list(early_parts) + list(late_parts)
    dmod_all = late_parts[0]
    dmod_cols = lax.dynamic_slice_in_dim(dmod_all, me * 768, 768, axis=2).transpose(1, 0, 2)
    g_ada_w = _ada_bwd(c16, jnp.pad(dmod_cols, ((0, 0), (0, 16 - N_DEV), (0, 0))))
    update("ada_w", ada_w, m_ada_w, v_ada_w, [g_ada_w[l][None] for l in range(2)], (2, D, 768))

    small_w = dict(ada_b=(ada_b, m_ada_b, v_ada_b), norm1_g=(norm1_g, m_norm1_g, v_norm1_g), norm2_g=(norm2_g, m_norm2_g, v_norm2_g),
                   b_mix_w=(b_mix_w, m_b_mix_w, v_b_mix_w), b_scale=(b_scale, m_b_scale, v_b_scale),
                   c_kv_norm_g=(c_kv_norm_g, m_c_kv_norm_g, v_c_kv_norm_g), d_w_s=(d_w_s, m_d_w_s, v_d_w_s),
                   d_b_s=(d_b_s, m_d_b_s, v_d_b_s), final_norm_g=(final_norm_g, m_final_norm_g, v_final_norm_g),
                   a_conv_w=(a_conv_w, m_a_conv_w, v_a_conv_w), c_q_norm_g=(c_q_norm_g, m_c_q_norm_g, v_c_q_norm_g),
                   d_ln_g=(d_ln_g, m_d_ln_g, v_d_ln_g), d_ln_b=(d_ln_b, m_d_ln_b, v_d_ln_b),
                   ffn_conv_w=(ffn_conv_w, m_ffn_conv_w, v_ffn_conv_w))
    small_out = _adamw_small("adamw_small", [tuple(a.reshape(small_view[n]) for a in small_w[n]) for n in small_names],
                             list(small_parts))
    for n, outs in zip(small_names, small_out):
        res[n] = [o_.reshape(small_w[n][0].shape) for o_ in outs]

    loss = lax.psum(loss_local[0, 0], ("x", "y", "c"))
    order = ["ada_w", "ada_b", "norm1_g", "norm2_g", "ab_w_in", "a_conv_w", "b_mix_w", "b_scale", "ab_w_out", "cd_w_in", "c_q_norm_g",
             "c_w_uq", "c_kv_norm_g", "c_w_ukv", "d_ln_g", "d_ln_b", "d_w_s", "d_b_s", "cd_w_out", "ffn_w_up", "ffn_conv_w",
             "ffn_w_down", "final_norm_g"]
    return (loss, dx0[None], *[res[n][0] for n in order], *[res[n][1] for n in order], *[res[n][2] for n in order],
            *[res[n][3] for n in order])
```

```python
import functools
import math

import jax
import jax.numpy as jnp
from jax import lax
from jax.experimental import pallas as pl
from jax.experimental.pallas import tpu as pltpu

F32 = jnp.float32
BF16 = jnp.bfloat16
_MXU_DTYPE = BF16
WIRE_DTYPE = BF16
ACT_DTYPE = BF16
_VMEM_LIMIT = 56 * 2 ** 20
N_DEV = 8
EPS = 1e-6
POOL_WINDOWS = (2, 4, 8, 16)
ATTN_SCALE = (64 + 32) ** -0.5
ADAM_LR, ADAM_B1, ADAM_B2, ADAM_EPS, ADAM_WD, ADAM_STEP = 0.001, 0.9, 0.999, 1e-08, 0.01, 10
MESH = pl.DeviceIdType.MESH
ANY = pl.BlockSpec(memory_space=pl.ANY)


def _cp(*sem):
    return pltpu.CompilerParams(dimension_semantics=sem, vmem_limit_bytes=_VMEM_LIMIT)


def _dot(a, b, contract):
    dn = {"nn": (((1,), (0,)), ((), ())), "nt": (((1,), (1,)), ((), ())), "tn": (((0,), (0,)), ((), ()))}[contract]
    return lax.dot_general(a.astype(_MXU_DTYPE), b.astype(_MXU_DTYPE), dn, preferred_element_type=F32)


def _my_position():
    x, y, c = lax.axis_index("x"), lax.axis_index("y"), lax.axis_index("c")
    return x, y, c, 4 * x + 2 * y + c


def _exchange(name, groups, scatter):
    flat = [a for g in groups for a in g]
    n_in, n_grp = len(flat), len(groups)
    out_shapes = []
    for g in groups:
        slab = g[0].shape[1:] if scatter else g[0].shape
        out_shapes.append(jax.ShapeDtypeStruct((N_DEV, len(g)) + tuple(slab), g[0].dtype))

    def body(*refs):
        ins, outs = refs[:n_in], refs[n_in:n_in + n_grp]
        send_sems, recv_sems, local_sems = refs[n_in + n_grp:]
        x, y, c, me = _my_position()
        i = 0
        for gi, g in enumerate(groups):
            for l in range(len(g)):
                src = ins[i]
                i += 1
                pltpu.make_async_copy(src.at[me] if scatter else src, outs[gi].at[me, l], local_sems.at[gi]).start()
                for k in range(1, N_DEV):
                    px = 1 - x if k & 4 else x
                    py = 1 - y if k & 2 else y
                    pc = 1 - c if k & 1 else c
                    peer = 4 * px + 2 * py + pc
                    pltpu.make_async_remote_copy(
                        src_ref=src.at[peer] if scatter else src, dst_ref=outs[gi].at[me, l],
                        send_sem=send_sems.at[gi], recv_sem=recv_sems.at[gi],
                        device_id=(px, py, pc), device_id_type=MESH).start()
        for gi in range(n_grp):
            mine = outs[gi].at[me]
            pltpu.make_async_copy(mine, mine, local_sems.at[gi]).wait()
            seven = outs[gi].at[pl.ds(0, N_DEV - 1)]
            w = pltpu.make_async_remote_copy(src_ref=seven, dst_ref=seven, send_sem=send_sems.at[gi],
                                             recv_sem=recv_sems.at[gi], device_id=(x, y, c), device_id_type=MESH)
            w.wait_send()
            w.wait_recv()

    return pl.pallas_call(
        body, name=name, out_shape=tuple(out_shapes),
        in_specs=[ANY] * n_in, out_specs=tuple([ANY] * n_grp),
        scratch_shapes=[pltpu.SemaphoreType.DMA((n_grp,)), pltpu.SemaphoreType.DMA((n_grp,)),
                        pltpu.SemaphoreType.DMA((n_grp,))],
        compiler_params=pltpu.CompilerParams(has_side_effects=True),
    )(*flat)


HBM_SPEC = pl.BlockSpec(memory_space=pltpu.HBM)
SEM_SPEC = pl.BlockSpec(memory_space=pltpu.SEMAPHORE)
EFFECT = pltpu.SideEffectType.DATAFLOW_SIDE_EFFECTING


def _put_mine(name, srcs, scatter, me):
    n = len(srcs)
    slabs = [tuple(s.shape[1:] if sc else s.shape) for s, sc in zip(srcs, scatter)]

    def body(me_ref, *refs):
        for i in range(n):
            refs[n + i][...] = refs[i][...]

    def at_me(slab):
        return pl.BlockSpec((None,) + slab, lambda g, me_ref, nd=len(slab): (me_ref[0],) + (0,) * nd)

    def whole(slab):
        return pl.BlockSpec(slab, lambda g, me_ref, nd=len(slab): (0,) * nd)

    return pl.pallas_call(
        body, name=name,
        grid_spec=pltpu.PrefetchScalarGridSpec(
            num_scalar_prefetch=1, grid=(1,),
            in_specs=[at_me(slab) if sc else whole(slab) for slab, sc in zip(slabs, scatter)],
            out_specs=[at_me(slab) for slab in slabs]),
        out_shape=[jax.ShapeDtypeStruct((N_DEV,) + slab, s.dtype) for slab, s in zip(slabs, srcs)],
        compiler_params=_cp("arbitrary"))(me.reshape(1), *srcs)


def _exchange_start(name, srcs, scatter, after, me):
    n = len(srcs)
    scatter = list(scatter) if isinstance(scatter, (list, tuple)) else [scatter] * n
    lands = _put_mine(name + "_mine", srcs, scatter, me)
    srcs = [pltpu.with_memory_space_constraint(a, pltpu.HBM) for a in srcs]
    lands = [pltpu.with_memory_space_constraint(a, pltpu.HBM) for a in lands]

    def body(*refs):
        ins, land = refs[:n], refs[n:2 * n]
        send_sems, recv_sems, token = refs[2 * n + 1], refs[2 * n + 2], refs[-1]
        x, y, c, me_in = _my_position()
        for i in range(n):
            for k in range(1, N_DEV):
                px = 1 - x if k & 4 else x
                py = 1 - y if k & 2 else y
                pc = 1 - c if k & 1 else c
                pltpu.make_async_remote_copy(
                    src_ref=ins[i].at[4 * px + 2 * py + pc] if scatter[i] else ins[i], dst_ref=land[i].at[me_in],
                    send_sem=send_sems.at[i], recv_sem=recv_sems.at[i],
                    device_id=(px, py, pc), device_id_type=MESH).start()
        token[...] = jnp.zeros_like(token)

    outs = pl.pallas_call(
        body, name=name,
        out_shape=(pltpu.SemaphoreType.DMA((n,)), pltpu.SemaphoreType.DMA((n,)),
                   *[pltpu.HBM(a.shape, a.dtype) for a in srcs], *[pltpu.HBM(a.shape, a.dtype) for a in lands],
                   jax.ShapeDtypeStruct((8, 128), F32)),
        in_specs=[HBM_SPEC] * (2 * n) + [ANY],
        out_specs=(SEM_SPEC, SEM_SPEC, *[HBM_SPEC] * (2 * n), pl.BlockSpec(memory_space=pltpu.VMEM)),
        input_output_aliases={i: 2 + i for i in range(2 * n)},
        compiler_params=pltpu.CompilerParams(has_side_effects=EFFECT),
    )(*srcs, *lands, after)
    return (outs[0], outs[1], outs[2:2 + n], outs[2 + n:2 + 2 * n]), outs[-1]


def _exchange_wait(name, handle, after):
    send_sems, recv_sems, srcs, lands = handle
    n = len(srcs)

    def body(*refs):
        land, send_ref, recv_ref = refs[n:2 * n], refs[2 * n], refs[2 * n + 1]
        x, y, c, _ = _my_position()
        for i in range(n):
            seven = land[i].at[pl.ds(0, N_DEV - 1)]
            w = pltpu.make_async_remote_copy(src_ref=seven, dst_ref=seven, send_sem=send_ref.at[i], recv_sem=recv_ref.at[i],
                                             device_id=(x, y, c), device_id_type=MESH)
            w.wait_send()
            w.wait_recv()

    outs = pl.pallas_call(
        body, name=name,
        out_shape=(*[pltpu.HBM(a.shape, a.dtype) for a in srcs], *[pltpu.HBM(a.shape, a.dtype) for a in lands]),
        in_specs=[HBM_SPEC] * (2 * n) + [SEM_SPEC, SEM_SPEC, ANY],
        out_specs=tuple([HBM_SPEC] * (2 * n)),
        input_output_aliases={i: i for i in range(2 * n)},
        compiler_params=pltpu.CompilerParams(has_side_effects=EFFECT),
    )(*srcs, *lands, send_sems, recv_sems, after)
    return outs[n:]


def _other_chips(x, y):
    return [(1 - x, y), (x, 1 - y), (1 - x, 1 - y)]


def _hier_gather_start(name, srcs, after, me):
    n = len(srcs)
    lands = _put_mine(name + "_mine", srcs, [False] * n, me)
    srcs = [pltpu.with_memory_space_constraint(a, pltpu.HBM) for a in srcs]
    lands = [pltpu.with_memory_space_constraint(a, pltpu.HBM) for a in lands]

    def body(*refs):
        ins, land = refs[:n], refs[n:2 * n]
        ici_send, ici_recv, d2d_send, d2d_recv = refs[2 * n + 1:2 * n + 5]
        token = refs[-1]
        x, y, c, me_in = _my_position()
        for i in range(n):
            pltpu.make_async_remote_copy(src_ref=ins[i], dst_ref=land[i].at[me_in], send_sem=d2d_send.at[i], recv_sem=d2d_recv.at[i],
                                         device_id=(x, y, 1 - c), device_id_type=MESH).start()
            for px, py in _other_chips(x, y):
                pltpu.make_async_remote_copy(src_ref=ins[i], dst_ref=land[i].at[me_in], send_sem=ici_send.at[i],
                                             recv_sem=ici_recv.at[i], device_id=(px, py, c), device_id_type=MESH).start()
        token[...] = jnp.zeros_like(token)

    sem = pltpu.SemaphoreType.DMA((n,))
    outs = pl.pallas_call(
        body, name=name,
        out_shape=(sem, sem, sem, sem, *[pltpu.HBM(a.shape, a.dtype) for a in srcs], *[pltpu.HBM(a.shape, a.dtype) for a in lands],
                   jax.ShapeDtypeStruct((8, 128), F32)),
        in_specs=[HBM_SPEC] * (2 * n) + [ANY],
        out_specs=(SEM_SPEC,) * 4 + (HBM_SPEC,) * (2 * n) + (pl.BlockSpec(memory_space=pltpu.VMEM),),
        input_output_aliases={i: 4 + i for i in range(2 * n)},
        compiler_params=pltpu.CompilerParams(has_side_effects=EFFECT),
    )(*srcs, *lands, after)
    return (outs[:4], outs[4:4 + n], outs[4 + n:4 + 2 * n]), outs[-1]


def _hier_gather_forward(name, handle, after):
    sems, srcs, lands = handle
    n = len(srcs)

    def body(*refs):
        land = refs[n:2 * n]
        ici_send, ici_recv, d2d_send, d2d_recv = refs[2 * n:2 * n + 4]
        x, y, c, _ = _my_position()
        for i in range(n):
            three = land[i].at[pl.ds(0, 3)]
            pltpu.make_async_remote_copy(src_ref=three, dst_ref=three, send_sem=ici_send.at[i], recv_sem=ici_recv.at[i],
                                         device_id=(x, y, c), device_id_type=MESH).wait_recv()
            for px, py in _other_chips(x, y):
                slab = land[i].at[4 * px + 2 * py + c]
                pltpu.make_async_remote_copy(src_ref=slab, dst_ref=slab, send_sem=d2d_send.at[i], recv_sem=d2d_recv.at[i],
                                             device_id=(x, y, 1 - c), device_id_type=MESH).start()

    outs = pl.pallas_call(
        body, name=name,
        out_shape=(*[pltpu.HBM(a.shape, a.dtype) for a in srcs], *[pltpu.HBM(a.shape, a.dtype) for a in lands]),
        in_specs=[HBM_SPEC] * (2 * n) + [SEM_SPEC] * 4 + [ANY],
        out_specs=tuple([HBM_SPEC] * (2 * n)),
        input_output_aliases={i: i for i in range(2 * n)},
        compiler_params=pltpu.CompilerParams(has_side_effects=EFFECT),
    )(*srcs, *lands, *sems, after)
    return (sems, outs[:n], outs[n:])


def _hier_gather_wait(name, handle, after):
    sems, srcs, lands = handle
    n = len(srcs)

    def body(*refs):
        land = refs[n:2 * n]
        ici_send, ici_recv, d2d_send, d2d_recv = refs[2 * n:2 * n + 4]
        x, y, c, _ = _my_position()
        for i in range(n):
            three, four = land[i].at[pl.ds(0, 3)], land[i].at[pl.ds(0, 4)]
            pltpu.make_async_remote_copy(src_ref=three, dst_ref=three, send_sem=ici_send.at[i], recv_sem=ici_recv.at[i],
                                         device_id=(x, y, c), device_id_type=MESH).wait_send()
            w = pltpu.make_async_remote_copy(src_ref=four, dst_ref=four, send_sem=d2d_send.at[i], recv_sem=d2d_recv.at[i],
                                             device_id=(x, y, c), device_id_type=MESH)
            w.wait_send()
            w.wait_recv()

    outs = pl.pallas_call(
        body, name=name,
        out_shape=(*[pltpu.HBM(a.shape, a.dtype) for a in srcs], *[pltpu.HBM(a.shape, a.dtype) for a in lands]),
        in_specs=[HBM_SPEC] * (2 * n) + [SEM_SPEC] * 4 + [ANY],
        out_specs=tuple([HBM_SPEC] * (2 * n)),
        input_output_aliases={i: i for i in range(2 * n)},
        compiler_params=pltpu.CompilerParams(has_side_effects=EFFECT),
    )(*srcs, *lands, *sems, after)
    return outs[n:]


def _pack(arrs):
    flat = jnp.concatenate([a.reshape(-1).astype(F32) for a in arrs])
    n = flat.shape[0]
    rows = -(-n // 1024) * 8
    return jnp.pad(flat, (0, rows * 128 - n)).reshape(rows, 128)


def _unpack(buf, shapes, lead=()):
    flat = buf.reshape(lead + (-1,))
    out, off = [], 0
    for s in shapes:
        n = math.prod(s)
        out.append(flat[..., off:off + n].reshape(lead + tuple(s)))
        off += n
    return out


def _mm(name, a, a_spec, b, b_spec, out_sds, o_spec, grid, contract, nk=1, stacked=0):
    o_blk = tuple(d for d in o_spec.block_shape if d is not None)

    def body(a_ref, b_ref, o_ref, *acc):
        if stacked:
            r = _dot(a_ref[0], b_ref[0], contract)
            for q in range(1, stacked):
                r = r + _dot(a_ref[q], b_ref[q], contract)
        else:
            r = _dot(a_ref[...], b_ref[...], contract)
        if nk == 1:
            o_ref[...] = r.astype(o_ref.dtype)
        else:
            k = pl.program_id(len(grid) - 1)

            @pl.when(k == 0)
            def _():
                acc[0][...] = r

            @pl.when(k > 0)
            def _():
                acc[0][...] += r

            @pl.when(k == nk - 1)
            def _():
                o_ref[...] = acc[0][...].astype(o_ref.dtype)

    sem = ("parallel",) * (len(grid) - 1) + (("arbitrary",) if nk > 1 else ("parallel",))
    return pl.pallas_call(
        body, name=name, out_shape=out_sds, grid=grid, in_specs=[a_spec, b_spec], out_specs=o_spec,
        scratch_shapes=[pltpu.VMEM(o_blk, F32)] if nk > 1 else [], compiler_params=_cp(*sem))(a, b)


def _tile(n, want):
    t = min(n, want)
    assert n % t == 0, (n, t)
    return t


def _mm_nn(name, a, b, out_dtype=F32, tm=512, tn=512):
    (M, K), N = a.shape, b.shape[1]
    tm, tn = _tile(M, tm), _tile(N, tn)
    return _mm(name, a, pl.BlockSpec((tm, K), lambda i, j: (i, 0)), b, pl.BlockSpec((K, tn), lambda i, j: (0, j)),
               jax.ShapeDtypeStruct((M, N), out_dtype), pl.BlockSpec((tm, tn), lambda i, j: (i, j)),
               (M // tm, N // tn), "nn")


def _mm_nt(name, a, b, out_dtype=F32, tm=512, tn=512):
    (M, K), N = a.shape, b.shape[0]
    tm, tn = _tile(M, tm), _tile(N, tn)
    return _mm(name, a, pl.BlockSpec((tm, K), lambda i, j: (i, 0)), b, pl.BlockSpec((tn, K), lambda i, j: (j, 0)),
               jax.ShapeDtypeStruct((M, N), out_dtype), pl.BlockSpec((tm, tn), lambda i, j: (i, j)),
               (M // tm, N // tn), "nt")


def _mm_tn(name, a, b, out_dtype=F32, tm=512, tn=512):
    (K, M), N = a.shape, b.shape[1]
    tm, tn = _tile(M, tm), _tile(N, tn)
    return _mm(name, a, pl.BlockSpec((K, tm), lambda i, j: (0, i)), b, pl.BlockSpec((K, tn), lambda i, j: (0, j)),
               jax.ShapeDtypeStruct((M, N), out_dtype), pl.BlockSpec((tm, tn), lambda i, j: (i, j)),
               (M // tm, N // tn), "tn")


def _mm_cols(name, a, w, out_dtype=F32, tm=512):
    (M, K), (J, _, n) = a.shape, w.shape
    tm = _tile(M, tm)
    return _mm(name, a, pl.BlockSpec((tm, K), lambda j, i: (i, 0)), w, pl.BlockSpec((None, K, n), lambda j, i: (j, 0, 0)),
               jax.ShapeDtypeStruct((J, M, n), out_dtype), pl.BlockSpec((None, tm, n), lambda j, i: (j, i, 0)),
               (J, M // tm), "nn")


def _mm_cols_dx(name, d, w, out_dtype=F32, tm=512, jb=None):
    (J, M, n), K = d.shape, w.shape[1]
    tm, jb = _tile(M, tm), J if jb is None else jb
    return _mm(name, d, pl.BlockSpec((jb, tm, n), lambda i, j: (j, i, 0)), w, pl.BlockSpec((jb, K, n), lambda i, j: (j, 0, 0)),
               jax.ShapeDtypeStruct((M, K), out_dtype), pl.BlockSpec((tm, K), lambda i, j: (i, 0)),
               (M // tm, J // jb), "nt", nk=J // jb, stacked=jb)


def _mm_cols_dw(name, a, d, out_dtype=F32, tk=512):
    (M, K), (J, _, n) = a.shape, d.shape
    tk = _tile(K, tk)
    return _mm(name, a, pl.BlockSpec((M, tk), lambda j, i: (0, i)), d, pl.BlockSpec((None, M, n), lambda j, i: (j, 0, 0)),
               jax.ShapeDtypeStruct((J, K, n), out_dtype), pl.BlockSpec((None, tk, n), lambda j, i: (j, i, 0)),
               (J, K // tk), "tn")


def _mm_cols_dwt(name, a, d, out_dtype=F32, tk=512):
    (M, K), (J, _, n) = a.shape, d.shape
    tk = _tile(K, tk)
    return _mm(name, d, pl.BlockSpec((None, M, n), lambda j, i: (j, 0, 0)), a, pl.BlockSpec((M, tk), lambda j, i: (0, i)),
               jax.ShapeDtypeStruct((J, n, K), out_dtype), pl.BlockSpec((None, n, tk), lambda j, i: (j, 0, i)),
               (J, K // tk), "tn")


def _mm_rows_resid(name, a, w, resid, gate, tm=512):
    (Q, M, k), N = a.shape, w.shape[2]
    tm = _tile(M, tm)

    def body(a_ref, w_ref, r_ref, g_ref, y_ref, x_ref):
        y = _dot(a_ref[0], w_ref[0], "nn")
        for q in range(1, Q):
            y = y + _dot(a_ref[q], w_ref[q], "nn")
        y_ref[...] = y
        x_ref[...] = r_ref[...] + g_ref[...] * y

    return pl.pallas_call(
        body, name=name, grid=(M // tm,),
        out_shape=(jax.ShapeDtypeStruct((M, N), F32), jax.ShapeDtypeStruct((M, N), F32)),
        in_specs=[pl.BlockSpec((Q, tm, k), lambda i: (0, i, 0)), pl.BlockSpec((Q, k, N), lambda i: (0, 0, 0)),
                  pl.BlockSpec((tm, N), lambda i: (i, 0)), pl.BlockSpec((1, N), lambda i: (0, 0))],
        out_specs=(pl.BlockSpec((tm, N), lambda i: (i, 0)), pl.BlockSpec((tm, N), lambda i: (i, 0))),
        compiler_params=_cp("parallel"))(a, w, resid, gate)


def _mm_rows_dx(name, d, w, out_dtype=F32, tm=512):
    (M, N), (Q, k, _) = d.shape, w.shape
    tm = _tile(M, tm)
    return _mm(name, d, pl.BlockSpec((tm, N), lambda q, i: (i, 0)), w, pl.BlockSpec((None, k, N), lambda q, i: (q, 0, 0)),
               jax.ShapeDtypeStruct((Q, M, k), out_dtype), pl.BlockSpec((None, tm, k), lambda q, i: (q, i, 0)),
               (Q, M // tm), "nt")


def _mm_rows_dw(name, a, d, out_dtype=F32, tn=512):
    (Q, M, k), N = a.shape, d.shape[1]
    tn = _tile(N, tn)
    return _mm(name, a, pl.BlockSpec((None, M, k), lambda q, j: (q, 0, 0)), d, pl.BlockSpec((M, tn), lambda q, j: (0, j)),
               jax.ShapeDtypeStruct((Q, k, N), out_dtype), pl.BlockSpec((None, k, tn), lambda q, j: (q, 0, j)),
               (Q, N // tn), "tn")


def _silu(v):
    return v * jax.nn.sigmoid(v)


def _ada_fwd(c16, ada_w):
    L, D, n = ada_w.shape

    def body(c_ref, w_ref, o_ref):
        o_ref[...] = _dot(_silu(c_ref[...]), w_ref[...], "nn")

    return pl.pallas_call(
        body, name="ada_fwd", grid=(L,), out_shape=jax.ShapeDtypeStruct((L, 16, n), F32),
        in_specs=[pl.BlockSpec((16, D), lambda l: (0, 0)), pl.BlockSpec((None, D, n), lambda l: (l, 0, 0))],
        out_specs=pl.BlockSpec((None, 16, n), lambda l: (l, 0, 0)), compiler_params=_cp("parallel"))(c16, ada_w)


def _ada_bwd(c16, dmod16):
    L, _, n = dmod16.shape
    D = c16.shape[1]

    def body(c_ref, d_ref, o_ref):
        o_ref[...] = _dot(_silu(c_ref[...]), d_ref[...], "tn")

    return pl.pallas_call(
        body, name="ada_bwd", grid=(L,), out_shape=jax.ShapeDtypeStruct((L, D, n), F32),
        in_specs=[pl.BlockSpec((16, D), lambda l: (0, 0)), pl.BlockSpec((None, 16, n), lambda l: (l, 0, 0))],
        out_specs=pl.BlockSpec((None, D, n), lambda l: (l, 0, 0)), compiler_params=_cp("parallel"))(c16, dmod16)


def _row_spec(tr, n):
    return pl.BlockSpec((tr, n), lambda i: (i, 0))


def _vec_spec(n):
    return pl.BlockSpec((1, n), lambda i: (0, 0))


def _rmsmod_fwd(name, x, g, sc, sh, after, tr=256):
    S, D = x.shape

    def body(x_ref, g_ref, sc_ref, sh_ref, after_ref, h_ref):
        xv = x_ref[...]
        rstd = lax.rsqrt(jnp.mean(xv * xv, axis=-1, keepdims=True) + EPS)
        y = xv * rstd * g_ref[...]
        h_ref[...] = (y * (1.0 + sc_ref[...]) + sh_ref[...]).astype(h_ref.dtype)

    return pl.pallas_call(
        body, name=name, grid=(S // tr,), out_shape=jax.ShapeDtypeStruct((S, D), _MXU_DTYPE),
        in_specs=[_row_spec(tr, D), _vec_spec(D), _vec_spec(D), _vec_spec(D), ANY], out_specs=_row_spec(tr, D),
        compiler_params=_cp("parallel"))(x, g, sc, sh, after)


def _acc_rows(ref, val, first):
    s = jnp.sum(val, axis=0, keepdims=True)

    @pl.when(first)
    def _():
        ref[...] = s

    @pl.when(jnp.logical_not(first))
    def _():
        ref[...] += s


def _rmsmod_bwd(name, x, g, sc, dh, dres, after, tr=256):
    S, D = x.shape

    def body(x_ref, g_ref, sc_ref, dh_ref, dres_ref, after_ref, dx_ref, dg_ref, dsc_ref, dsh_ref):
        first = pl.program_id(0) == 0
        xv, dh_v, gv = x_ref[...], dh_ref[...], g_ref[...]
        rstd = lax.rsqrt(jnp.mean(xv * xv, axis=-1, keepdims=True) + EPS)
        xhat = xv * rstd
        _acc_rows(dsh_ref, dh_v, first)
        _acc_rows(dsc_ref, dh_v * (xhat * gv), first)
        dyg = dh_v * (1.0 + sc_ref[...])
        _acc_rows(dg_ref, dyg * xhat, first)
        dxhat = dyg * gv
        dx_ref[...] = dres_ref[...] + rstd * (dxhat - xhat * jnp.mean(dxhat * xhat, axis=-1, keepdims=True))

    vec = jax.ShapeDtypeStruct((1, D), F32)
    return pl.pallas_call(
        body, name=name, grid=(S // tr,), out_shape=(jax.ShapeDtypeStruct((S, D), F32), vec, vec, vec),
        in_specs=[_row_spec(tr, D), _vec_spec(D), _vec_spec(D), _row_spec(tr, D), _row_spec(tr, D), ANY],
        out_specs=(_row_spec(tr, D), _vec_spec(D), _vec_spec(D), _vec_spec(D)),
        compiler_params=_cp("arbitrary"))(x, g, sc, dh, dres, after)


def _loss_head(x, g, target, tr=256):
    S, D = x.shape

    def body(x_ref, g_ref, t_ref, loss_ref, dx_ref, dg_ref):
        first = pl.program_id(0) == 0
        xv, gv = x_ref[...], g_ref[...]
        rstd = lax.rsqrt(jnp.mean(xv * xv, axis=-1, keepdims=True) + EPS)
        xhat = xv * rstd
        err = xhat * gv - t_ref[...]
        part = 0.5 * jnp.sum(jnp.mean(err * err, axis=-1, keepdims=True), axis=0, keepdims=True)

        @pl.when(first)
        def _():
            loss_ref[...] = part

        @pl.when(jnp.logical_not(first))
        def _():
            loss_ref[...] += part

        dout = err * (1.0 / D)
        _acc_rows(dg_ref, dout * xhat, first)
        dxhat = dout * gv
        dx_ref[...] = rstd * (dxhat - xhat * jnp.mean(dxhat * xhat, axis=-1, keepdims=True))

    return pl.pallas_call(
        body, name="loss_head", grid=(S // tr,),
        out_shape=(jax.ShapeDtypeStruct((1, 1), F32), jax.ShapeDtypeStruct((S, D), F32), jax.ShapeDtypeStruct((1, D), F32)),
        in_specs=[_row_spec(tr, D), _vec_spec(D), _row_spec(tr, D)],
        out_specs=(pl.BlockSpec((1, 1), lambda i: (0, 0)), _row_spec(tr, D), _vec_spec(D)),
        compiler_params=_cp("arbitrary"))(x, g, target)


def _gate_bwd(name, dx, y, gate, tr=256):
    S, D = dx.shape

    def body(dx_ref, y_ref, g_ref, dy_ref, dg_ref):
        dxv = dx_ref[...]
        dy_ref[...] = (g_ref[...] * dxv).astype(dy_ref.dtype)
        _acc_rows(dg_ref, dxv * y_ref[...], pl.program_id(0) == 0)

    return pl.pallas_call(
        body, name=name, grid=(S // tr,),
        out_shape=(jax.ShapeDtypeStruct((S, D), _MXU_DTYPE), jax.ShapeDtypeStruct((1, D), F32)),
        in_specs=[_row_spec(tr, D), _row_spec(tr, D), _vec_spec(D)], out_specs=(_row_spec(tr, D), _vec_spec(D)),
        compiler_params=_cp("arbitrary"))(dx, y, gate)


def _shift_down(v, k):
    t = lax.broadcasted_iota(jnp.int32, v.shape, 0)
    return jnp.where(t >= k, pltpu.roll(v, k, axis=0), 0.0)


def _shift_up(v, k):
    n = v.shape[0]
    t = lax.broadcasted_iota(jnp.int32, v.shape, 0)
    return jnp.where(t < n - k, pltpu.roll(v, n - k, axis=0), 0.0)


def _window_sum(p, w, shift):
    s, k = p, 1
    while k < w:
        s = s + shift(s, k)
        k *= 2
    return s


def _pool_count(shape, w):
    t = lax.broadcasted_iota(jnp.int32, shape, 0)
    return jnp.minimum(t + 1, w).astype(F32)


def _ab_specs(S):
    zs = [pl.BlockSpec((None, S, 128), functools.partial(lambda g, q: (2 * q + g // 2, 0, g % 2), q=q)) for q in range(4)]
    return zs


def _ab_mix_fwd(z8, conv_w, mix_w, scale):
    S = z8.shape[1]

    def body(b_ref, c_ref, a_ref, p_ref, w_ref, mix_ref, sc_ref, y_ref):
        g = pl.program_id(0)
        cg = c_ref[...] * a_ref[...]
        w = w_ref[...]
        conv = w[0:1] * _shift_down(cg, 2) + w[1:2] * _shift_down(cg, 1) + w[2:3] * cg
        y_ref[0] = (b_ref[...] * conv).astype(y_ref.dtype)
        for gg, win in enumerate(POOL_WINDOWS):
            @pl.when(g == gg)
            def _(win=win):
                p = p_ref[...]
                pooled = _window_sum(p, win, _shift_down) / _pool_count(p.shape, win) - p
                y_ref[1] = (_dot(pooled, mix_ref[...], "nn") * sc_ref[...]).astype(y_ref.dtype)

    return pl.pallas_call(
        body, name="ab_mix_fwd", grid=(4,), out_shape=jax.ShapeDtypeStruct((2, S, 512), _MXU_DTYPE),
        in_specs=_ab_specs(S) + [pl.BlockSpec((3, 128), lambda g: (0, g)), pl.BlockSpec((None, 128, 128), lambda g: (g, 0, 0)),
                                 pl.BlockSpec((1, 128), lambda g: (0, g))],
        out_specs=pl.BlockSpec((2, S, 128), lambda g: (0, 0, g)), compiler_params=_cp("parallel"))(z8, z8, z8, z8, conv_w, mix_w, scale)


def _ab_mix_bwd(z8, dycat2, conv_w, mix_w, scale, after):
    S = z8.shape[1]

    def body(b_ref, c_ref, a_ref, p_ref, dy_ref, w_ref, mix_ref, sc_ref, after_ref, dz_ref, dw_ref, dmix_ref, dsc_ref):
        g = pl.program_id(0)
        bv, cv, av, w = b_ref[...], c_ref[...], a_ref[...], w_ref[...]
        dya = dy_ref[0]
        cg = cv * av
        cg1, cg2 = _shift_down(cg, 1), _shift_down(cg, 2)
        conv = w[0:1] * cg2 + w[1:2] * cg1 + w[2:3] * cg
        dz_ref[0] = (dya * conv).astype(dz_ref.dtype)
        dconv = dya * bv
        dcg = w[2:3] * dconv + w[1:2] * _shift_up(dconv, 1) + w[0:1] * _shift_up(dconv, 2)
        dz_ref[1] = (dcg * av).astype(dz_ref.dtype)
        dz_ref[2] = (dcg * cv).astype(dz_ref.dtype)
        dw_ref[0:1, :] = jnp.sum(dconv * cg2, axis=0, keepdims=True)
        dw_ref[1:2, :] = jnp.sum(dconv * cg1, axis=0, keepdims=True)
        dw_ref[2:3, :] = jnp.sum(dconv * cg, axis=0, keepdims=True)
        for gg, win in enumerate(POOL_WINDOWS):
            @pl.when(g == gg)
            def _(win=win):
                p, dyb, mix = p_ref[...], dy_ref[1], mix_ref[...]
                cnt = _pool_count(p.shape, win)
                pooled = _window_sum(p, win, _shift_down) / cnt - p
                dsc_ref[...] = jnp.sum(dyb * _dot(pooled, mix, "nn"), axis=0, keepdims=True)
                dmixed = dyb * sc_ref[...]
                dmix_ref[...] = _dot(pooled, dmixed, "tn")
                dpooled = _dot(dmixed, mix, "nt")
                dz_ref[3] = (_window_sum(dpooled / cnt, win, _shift_up) - dpooled).astype(dz_ref.dtype)

    return pl.pallas_call(
        body, name="ab_mix_bwd", grid=(4,),
        out_shape=(jax.ShapeDtypeStruct((4, 2, S, 256), _MXU_DTYPE), jax.ShapeDtypeStruct((3, 512), F32),
                   jax.ShapeDtypeStruct((4, 128, 128), F32), jax.ShapeDtypeStruct((1, 512), F32)),
        in_specs=_ab_specs(S) + [pl.BlockSpec((2, S, 128), lambda g: (0, 0, g)), pl.BlockSpec((3, 128), lambda g: (0, g)),
                                 pl.BlockSpec((None, 128, 128), lambda g: (g, 0, 0)), pl.BlockSpec((1, 128), lambda g: (0, g)), ANY],
        out_specs=(pl.BlockSpec((4, None, S, 128), lambda g: (0, g // 2, 0, g % 2)), pl.BlockSpec((3, 128), lambda g: (0, g)),
                   pl.BlockSpec((None, 128, 128), lambda g: (g, 0, 0)), pl.BlockSpec((1, 128), lambda g: (0, g))),
        compiler_params=_cp("parallel"))(z8, z8, z8, z8, dycat2, conv_w, mix_w, scale, after)


HALO = 16


def _ffn_specs(S, n, tr):
    nb = S // HALO
    tile = pl.BlockSpec((2, None, tr, n), lambda j, i: (0, j, i, 0))
    prev = pl.BlockSpec((2, None, HALO, n), lambda j, i: (0, j, jnp.maximum(i * (tr // HALO) - 1, 0), 0))
    nxt = pl.BlockSpec((2, None, HALO, n), lambda j, i: (0, j, jnp.minimum((i + 1) * (tr // HALO), nb - 1), 0))
    cw = pl.BlockSpec((2, None, 3, n), lambda j, i: (0, j, 0, 0))
    return tile, prev, nxt, cw


def _shifted_rows(ext, lo, rows):
    shape = (2 * rows, ext.shape[0])
    r, c = lax.broadcasted_iota(jnp.int32, shape, 0), lax.broadcasted_iota(jnp.int32, shape, 1)
    source = jnp.where(r < rows, r + (lo - 1), r - rows + (lo - 2))
    pick = jnp.where(c == source, 1.0, 0.0)
    sh = _dot(pick, ext, "nn")
    return sh[:rows], sh[rows:]


def _ffn_gate_fwd(name, u24, cw24, tr=256):
    _, J, S, n = u24.shape
    tile, prev, _, cw = _ffn_specs(S, n, tr)

    def body(u_ref, up_ref, w_ref, a_ref):
        keep = (pl.program_id(1) > 0).astype(u_ref.dtype)
        z = []
        for h in range(2):
            ext = jnp.concatenate([up_ref[h] * keep, u_ref[h]], axis=0)
            x1, x2 = _shifted_rows(ext, HALO, tr)
            w = w_ref[h]
            z.append(w[0:1] * x2 + w[1:2] * x1 + w[2:3] * u_ref[h].astype(F32))
        a_ref[...] = (_silu(z[0]) * z[1]).astype(a_ref.dtype)

    return pl.pallas_call(
        body, name=name, grid=(J, S // tr), out_shape=jax.ShapeDtypeStruct((J, S, n), _MXU_DTYPE),
        in_specs=[tile, prev, cw], out_specs=pl.BlockSpec((None, tr, n), lambda j, i: (j, i, 0)),
        compiler_params=_cp("parallel", "parallel"))(u24, u24, cw24)


def _ffn_gate_bwd(name, u24, cw24, da4, after, tr=256):
    _, J, S, n = u24.shape
    tile, prev, nxt, cw = _ffn_specs(S, n, tr)
    nb = S // HALO
    ext_rows = tr + 2 * HALO

    def body(u_ref, up_ref, un_ref, w_ref, da_ref, dan_ref, after_ref, du_ref, dcw_ref):
        i = pl.program_id(1)
        first = i == 0
        keep_prev = (i > 0).astype(u_ref.dtype)
        keep_next = (i < S // tr - 1).astype(F32)
        w = [w_ref[h] for h in range(2)]
        m = tr + HALO
        xs, z = [], []
        for h in range(2):
            ext = jnp.concatenate([up_ref[h] * keep_prev, u_ref[h], un_ref[h]], axis=0)
            x1, x2 = _shifted_rows(ext, HALO, m)
            x0 = ext[HALO:HALO + m].astype(F32)
            xs.append((x2, x1, x0))
            z.append(w[h][0:1] * x2 + w[h][1:2] * x1 + w[h][2:3] * x0)
        zg, zu = z
        da = jnp.concatenate([da_ref[...].astype(F32), dan_ref[...].astype(F32) * keep_next], axis=0)
        sg = jax.nn.sigmoid(zg)
        dz = [da * zu * (sg * (1.0 + zg * (1.0 - sg))), da * (zg * sg)]
        for h in range(2):
            d = dz[h]
            du = w[h][2:3] * d[:tr] + w[h][1:2] * pltpu.roll(d, m - 1, axis=0)[:tr] + w[h][0:1] * pltpu.roll(d, m - 2, axis=0)[:tr]
            du_ref[h] = du.astype(du_ref.dtype)
            dt = d[:tr]
            parts = [jnp.sum(dt * xk[:tr], axis=0, keepdims=True) for xk in xs[h]]
            for k in range(3):
                @pl.when(first)
                def _(k=k, h=h):
                    dcw_ref[h, k:k + 1, :] = parts[k]

                @pl.when(jnp.logical_not(first))
                def _(k=k, h=h):
                    dcw_ref[h, k:k + 1, :] += parts[k]

    da_tile = pl.BlockSpec((None, tr, n), lambda j, i: (j, i, 0))
    da_next = pl.BlockSpec((None, HALO, n), lambda j, i: (j, jnp.minimum((i + 1) * (tr // HALO), nb - 1), 0))
    return pl.pallas_call(
        body, name=name, grid=(J, S // tr),
        out_shape=(jax.ShapeDtypeStruct((2, J, S, n), _MXU_DTYPE), jax.ShapeDtypeStruct((2, J, 3, n), F32)),
        in_specs=[tile, prev, nxt, cw, da_tile, da_next, ANY], out_specs=(tile, cw),
        compiler_params=_cp("parallel", "arbitrary"))(u24, u24, u24, cw24, da4, da4, after)


def _rms_rows(v, g):
    rstd = lax.rsqrt(jnp.mean(v * v, axis=-1, keepdims=True) + EPS)
    return v * rstd * g


def _rms_rows_bwd(v, g, dy):
    rstd = lax.rsqrt(jnp.mean(v * v, axis=-1, keepdims=True) + EPS)
    vhat = v * rstd
    dvhat = dy * g
    return rstd * (dvhat - vhat * jnp.mean(dvhat * vhat, axis=-1, keepdims=True)), dy * vhat


def _mla_prep_fwd(z, qg, kvg, tr=256):
    S = z.shape[0]

    def body(q_ref, kv_ref, qg_ref, kvg_ref, qn_ref, kvn_ref):
        qn_ref[...] = _rms_rows(q_ref[...], qg_ref[...]).astype(qn_ref.dtype)
        kvn_ref[...] = _rms_rows(kv_ref[...], kvg_ref[...]).astype(kvn_ref.dtype)

    return pl.pallas_call(
        body, name="mla_prep_fwd", grid=(S // tr,),
        out_shape=(jax.ShapeDtypeStruct((S, 256), _MXU_DTYPE), jax.ShapeDtypeStruct((S, 128), _MXU_DTYPE)),
        in_specs=[pl.BlockSpec((tr, 256), lambda i: (i, 0)), pl.BlockSpec((tr, 128), lambda i: (i, 2)), _vec_spec(256), _vec_spec(128)],
        out_specs=(_row_spec(tr, 256), _row_spec(tr, 128)), compiler_params=_cp("parallel"))(z, z, qg, kvg)


def _mla_prep_bwd(z, qg, kvg, dqn, dkvn, dkpe, duv, tr=256):
    S = z.shape[0]

    def body(q_ref, kv_ref, qg_ref, kvg_ref, dqn_ref, dkvn_ref, dkpe_ref, duv_ref, dz_ref, dqg_ref, dkvg_ref):
        first = pl.program_id(0) == 0
        dq, dqg = _rms_rows_bwd(q_ref[...], qg_ref[...], dqn_ref[...])
        dkv, dkvg = _rms_rows_bwd(kv_ref[...], kvg_ref[...], dkvn_ref[...])
        _acc_rows(dqg_ref, dqg, first)
        _acc_rows(dkvg_ref, dkvg, first)
        dz_ref[:, 0:256] = dq.astype(dz_ref.dtype)
        dz_ref[:, 256:384] = dkv.astype(dz_ref.dtype)
        dz_ref[:, 384:512] = dkpe_ref[...].astype(dz_ref.dtype)
        dz_ref[:, 512:1536] = duv_ref[...].astype(dz_ref.dtype)

    return pl.pallas_call(
        body, name="mla_prep_bwd", grid=(S // tr,),
        out_shape=(jax.ShapeDtypeStruct((S, 1536), _MXU_DTYPE), jax.ShapeDtypeStruct((1, 256), F32), jax.ShapeDtypeStruct((1, 128), F32)),
        in_specs=[pl.BlockSpec((tr, 256), lambda i: (i, 0)), pl.BlockSpec((tr, 128), lambda i: (i, 2)), _vec_spec(256), _vec_spec(128),
                  _row_spec(tr, 256), _row_spec(tr, 128), _row_spec(tr, 128), _row_spec(tr, 1024)],
        out_specs=(_row_spec(tr, 1536), _vec_spec(256), _vec_spec(128)),
        compiler_params=_cp("arbitrary"))(z, z, qg, kvg, dqn, dkvn, dkpe, duv)


def _rope(v, cos, sa, sb):
    return v * cos + pltpu.roll(v, 112, axis=1) * sa + pltpu.roll(v, 16, axis=1) * sb


def _rope_t(d, cos, sa, sb):
    return d * cos + pltpu.roll(d * sa, 16, axis=1) + pltpu.roll(d * sb, 112, axis=1)


def _rope_fwd(qraw, kvall, z, cosq, cosk, sa, sb, tr=256):
    S = qraw.shape[0]

    def body(q_ref, k_ref, v_ref, kpe_ref, cq_ref, ck_ref, sa_ref, sb_ref, qo_ref, ko_ref, vo_ref):
        cq, ck, sa_v, sb_v = cq_ref[...], ck_ref[...], sa_ref[...], sb_ref[...]
        kpe = _rope(kpe_ref[...], ck, sa_v, sb_v)
        for h in range(8):
            cols = slice(128 * h, 128 * h + 128)
            qo_ref[:, cols] = _rope(q_ref[:, cols], cq, sa_v, sb_v).astype(qo_ref.dtype)
            ko_ref[:, cols] = (k_ref[:, cols] + kpe).astype(ko_ref.dtype)
        vo_ref[...] = v_ref[...].astype(vo_ref.dtype)

    tab = _row_spec(tr, 128)
    return pl.pallas_call(
        body, name="rope_fwd", grid=(S // tr,),
        out_shape=(jax.ShapeDtypeStruct((S, 1024), _MXU_DTYPE), jax.ShapeDtypeStruct((S, 1024), _MXU_DTYPE),
                   jax.ShapeDtypeStruct((S, 512), _MXU_DTYPE)),
        in_specs=[_row_spec(tr, 1024), pl.BlockSpec((tr, 1024), lambda i: (i, 0)), pl.BlockSpec((tr, 512), lambda i: (i, 2)),
                  pl.BlockSpec((tr, 128), lambda i: (i, 3)), tab, tab, tab, tab],
        out_specs=(_row_spec(tr, 1024), _row_spec(tr, 1024), _row_spec(tr, 512)),
        compiler_params=_cp("parallel"))(qraw, kvall, kvall, z, cosq, cosk, sa, sb)


def _rope_bwd(dq, dk, dv, cosq, cosk, sa, sb, tr=256):
    S = dq.shape[0]

    def body(dq_ref, dk_ref, dv_ref, cq_ref, ck_ref, sa_ref, sb_ref, dqo_ref, dkv_ref, dkpe_ref):
        cq, ck, sa_v, sb_v = cq_ref[...], ck_ref[...], sa_ref[...], sb_ref[...]
        tot = jnp.zeros((tr, 128), F32)
        for h in range(8):
            cols = slice(128 * h, 128 * h + 128)
            dqo_ref[:, cols] = _rope_t(dq_ref[:, cols], cq, sa_v, sb_v).astype(dqo_ref.dtype)
            dkh = dk_ref[:, cols]
            tot = tot + dkh
            dkv_ref[:, cols] = dkh.astype(dkv_ref.dtype)
        dkv_ref[:, 1024:1536] = dv_ref[...].astype(dkv_ref.dtype)
        dkpe_ref[...] = _rope_t(tot, ck, sa_v, sb_v)

    tab = _row_spec(tr, 128)
    return pl.pallas_call(
        body, name="rope_bwd", grid=(S // tr,),
        out_shape=(jax.ShapeDtypeStruct((S, 1024), _MXU_DTYPE), jax.ShapeDtypeStruct((S, 1536), _MXU_DTYPE),
                   jax.ShapeDtypeStruct((S, 128), F32)),
        in_specs=[_row_spec(tr, 1024), _row_spec(tr, 1024), _row_spec(tr, 512), tab, tab, tab, tab],
        out_specs=(_row_spec(tr, 1024), _row_spec(tr, 1536), _row_spec(tr, 128)),
        compiler_params=_cp("parallel"))(dq, dk, dv, cosq, cosk, sa, sb)


NEG = -1e30


def _attn_fwd(q, k, v, tq=256, tk=256):
    S = q.shape[0]
    assert tq == tk

    def body(q_ref, k_ref, v_ref, o_ref, lse_ref):
        i = pl.program_id(1)
        qs = [q_ref[:, 0:128], q_ref[:, 128:256]]

        def step(kb, carry, diagonal=False):
            start = pl.multiple_of(kb * tk, tk)
            vv = v_ref[pl.ds(start, tk), :]
            out = []
            for h in range(2):
                m, l, acc = carry[3 * h:3 * h + 3]
                s = _dot(qs[h], k_ref[pl.ds(start, tk), 128 * h:128 * h + 128], "nt") * ATTN_SCALE
                if diagonal:
                    s = jnp.where(below, s, NEG)
                m_new = jnp.maximum(m, jnp.max(s, axis=-1, keepdims=True))
                alpha = jnp.exp(m - m_new)
                p = jnp.exp(s - m_new)
                out += [m_new, alpha * l + jnp.sum(p, axis=-1, keepdims=True), alpha * acc + _dot(p, vv, "nn")]
            return tuple(out)

        below = lax.broadcasted_iota(jnp.int32, (tq, tk), 1) <= lax.broadcasted_iota(jnp.int32, (tq, tk), 0)
        init = (jnp.full((tq, 1), NEG, F32), jnp.zeros((tq, 1), F32), jnp.zeros((tq, 128), F32)) * 2
        ma, la, acca, mb, lb, accb = step(i, lax.fori_loop(0, i, step, init), diagonal=True)
        lane = lax.broadcasted_iota(jnp.int32, (tq, 128), 1)
        o_ref[...] = jnp.where(lane < 64, acca / la, accb / lb)
        lse_ref[...] = jnp.where(lane < 64, ma + jnp.log(la), mb + jnp.log(lb))

    return pl.pallas_call(
        body, name="attn_fwd", grid=(4, S // tq),
        out_shape=(jax.ShapeDtypeStruct((S, 512), F32), jax.ShapeDtypeStruct((4, S, 128), F32)),
        in_specs=[pl.BlockSpec((tq, 256), lambda p, i: (i, p)), pl.BlockSpec((S, 256), lambda p, i: (0, p)),
                  pl.BlockSpec((S, 128), lambda p, i: (0, p))],
        out_specs=(pl.BlockSpec((tq, 128), lambda p, i: (i, p)), pl.BlockSpec((None, tq, 128), lambda p, i: (p, i, 0))),
        compiler_params=_cp("parallel", "parallel"))(q, k, v)


def _attn_bwd(q, k, v, o, lse, dycat2, tq=256, tk=256):
    S = q.shape[0]
    assert tq == tk

    def body(q_ref, k_ref, v_ref, o_ref, lse_ref, do_ref, dq_ref, dk_ref, dv_ref):
        j = pl.program_id(1)

        @pl.when(j == 0)
        def _():
            dq_ref[...] = jnp.zeros_like(dq_ref)

        below = lax.broadcasted_iota(jnp.int32, (tq, tk), 1) <= lax.broadcasted_iota(jnp.int32, (tq, tk), 0)
        lane = lax.broadcasted_iota(jnp.int32, (tq, 128), 1)
        ks = [k_ref[:, 0:128], k_ref[:, 128:256]]
        vv = v_ref[...]

        def step(qb, carry, diagonal=False):
            dka, dkb, dvp = carry
            start = pl.multiple_of(qb * tq, tq)
            rows = pl.ds(start, tq)
            do, lse_v = do_ref[rows, :], lse_ref[rows, :]
            prod = do * o_ref[rows, :]
            dks = [dka, dkb]
            for h in range(2):
                mine = (lane < 64) if h == 0 else (lane >= 64)
                delta = jnp.sum(jnp.where(mine, prod, 0.0), axis=-1, keepdims=True)
                do_h = jnp.where(mine, do, 0.0)
                qh = q_ref[rows, 128 * h:128 * h + 128]
                s = _dot(qh, ks[h], "nt") * ATTN_SCALE
                p = jnp.exp(s - lse_v[:, 64 * h:64 * h + 1])
                if diagonal:
                    p = jnp.where(below, p, 0.0)
                dvp = dvp + _dot(p, do_h, "tn")
                ds = p * (_dot(do_h, vv, "nt") - delta) * ATTN_SCALE
                dq_ref[rows, 128 * h:128 * h + 128] += _dot(ds, ks[h], "nn")
                dks[h] = dks[h] + _dot(ds, qh, "tn")
            return dks[0], dks[1], dvp

        zero = jnp.zeros((tk, 128), F32)
        dka, dkb, dvp = lax.fori_loop(j + 1, S // tq, step, step(j, (zero, zero, zero), diagonal=True))
        dk_ref[:, 0:128] = dka
        dk_ref[:, 128:256] = dkb
        dv_ref[...] = dvp

    return pl.pallas_call(
        body, name="attn_bwd", grid=(4, S // tk),
        out_shape=(jax.ShapeDtypeStruct((S, 1024), F32), jax.ShapeDtypeStruct((S, 1024), F32), jax.ShapeDtypeStruct((S, 512), F32)),
        in_specs=[pl.BlockSpec((S, 256), lambda p, j: (0, p)), pl.BlockSpec((tk, 256), lambda p, j: (j, p)),
                  pl.BlockSpec((tk, 128), lambda p, j: (j, p)), pl.BlockSpec((S, 128), lambda p, j: (0, p)),
                  pl.BlockSpec((None, S, 128), lambda p, j: (p, 0, 0)), pl.BlockSpec((None, S, 128), lambda p, j: (0, 0, p))],
        out_specs=(pl.BlockSpec((S, 256), lambda p, j: (0, p)), pl.BlockSpec((tk, 256), lambda p, j: (j, p)),
                   pl.BlockSpec((tk, 128), lambda p, j: (j, p))),
        compiler_params=_cp("parallel", "arbitrary"))(q, k, v, o, lse, dycat2)


CHUNK = 128
GELU_C = math.sqrt(2.0 / math.pi)


def _gelu(v):
    t = jnp.tanh(GELU_C * (v + 0.044715 * (v * v * v)))
    return v * (0.5 * (1.0 + t)), t


def _gelu_grad(v, t):
    return 0.5 * (1.0 + t) + v * (0.5 * (1.0 - t * t) * GELU_C * (1.0 + 3.0 * 0.044715 * v * v))


def _tril(w):
    r = lax.broadcasted_iota(jnp.int32, w.shape, 0)
    c = lax.broadcasted_iota(jnp.int32, w.shape, 1)
    return jnp.where(c <= r, w, 0.0)


def _layer_norm(v, g, b):
    xc = v - jnp.mean(v, axis=-1, keepdims=True)
    rstd = lax.rsqrt(jnp.mean(xc * xc, axis=-1, keepdims=True) + EPS)
    xhat = xc * rstd
    return xhat * g + b, xhat, rstd


def _sgu_fwd(z, o, ln_g, ln_b, w_s, b_st, tr=256):
    S = z.shape[0]

    def body(u_ref, v_ref, o_ref, g_ref, b_ref, ws_ref, bs_ref, y_ref):
        gu, _ = _gelu(u_ref[...])
        gv, _ = _gelu(v_ref[...])
        vln, _, _ = _layer_norm(gv, g_ref[...], b_ref[...])
        y_ref[0] = o_ref[...].astype(y_ref.dtype)
        for g in range(4):
            wt = _tril(ws_ref[g])
            cols = slice(128 * g, 128 * g + 128)
            for ch in range(tr // CHUNK):
                rows = slice(CHUNK * ch, CHUNK * ch + CHUNK)
                mixed = _dot(wt, vln[rows, cols], "nn") + bs_ref[:, g:g + 1]
                y_ref[1, rows, cols] = (gu[rows, cols] * mixed).astype(y_ref.dtype)

    return pl.pallas_call(
        body, name="sgu_fwd", grid=(S // tr,), out_shape=jax.ShapeDtypeStruct((2, S, 512), _MXU_DTYPE),
        in_specs=[pl.BlockSpec((tr, 512), lambda i: (i, 1)), pl.BlockSpec((tr, 512), lambda i: (i, 2)), _row_spec(tr, 512),
                  _vec_spec(512), _vec_spec(512), pl.BlockSpec((4, 128, 128), lambda i: (0, 0, 0)), pl.BlockSpec((128, 4), lambda i: (0, 0))],
        out_specs=pl.BlockSpec((2, tr, 512), lambda i: (0, i, 0)), compiler_params=_cp("parallel"))(z, z, o, ln_g, ln_b, w_s, b_st)


def _sgu_bwd(z, dycat2, ln_g, ln_b, w_s, b_st, tr=256):
    S = z.shape[0]

    def body(u_ref, v_ref, dy_ref, g_ref, b_ref, ws_ref, bs_ref, duv_ref, dg_ref, db_ref, dws_ref, dbs_ref):
        first = pl.program_id(0) == 0
        u_pre, v_pre = u_ref[...], v_ref[...]
        gu, tu = _gelu(u_pre)
        gv, tv = _gelu(v_pre)
        gain = g_ref[...]
        vln, xhat, rstd = _layer_norm(gv, gain, b_ref[...])

        @pl.when(first)
        def _():
            dws_ref[...] = jnp.zeros_like(dws_ref)
            dbs_ref[...] = jnp.zeros_like(dbs_ref)

        dvln_cols = []
        for g in range(4):
            wt = _tril(ws_ref[g])
            cols = slice(128 * g, 128 * g + 128)
            dmixed_sum = jnp.zeros((CHUNK, 128), F32)
            dw = jnp.zeros((CHUNK, CHUNK), F32)
            dvln_rows = []
            for ch in range(tr // CHUNK):
                rows = slice(CHUNK * ch, CHUNK * ch + CHUNK)
                vt = vln[rows, cols]
                mixed = _dot(wt, vt, "nn") + bs_ref[:, g:g + 1]
                dyd = dy_ref[rows, cols]
                duv_ref[rows, cols] = (dyd * mixed * _gelu_grad(u_pre[rows, cols], tu[rows, cols])).astype(duv_ref.dtype)
                dmixed = dyd * gu[rows, cols]
                dmixed_sum = dmixed_sum + dmixed
                dw = dw + _dot(dmixed, vt, "nt")
                dvln_rows.append(_dot(wt, dmixed, "tn"))
            dws_ref[g] += _tril(dw)
            dbs_ref[g:g + 1, :] += jnp.sum(dmixed_sum.T, axis=0, keepdims=True)
            dvln_cols.append(jnp.concatenate(dvln_rows, axis=0))
        dvln = jnp.concatenate(dvln_cols, axis=1)
        _acc_rows(dg_ref, dvln * xhat, first)
        _acc_rows(db_ref, dvln, first)
        dxhat = dvln * gain
        dgv = rstd * (dxhat - jnp.mean(dxhat, axis=-1, keepdims=True) - xhat * jnp.mean(dxhat * xhat, axis=-1, keepdims=True))
        duv_ref[:, 512:1024] = (dgv * _gelu_grad(v_pre, tv)).astype(duv_ref.dtype)

    return pl.pallas_call(
        body, name="sgu_bwd", grid=(S // tr,),
        out_shape=(jax.ShapeDtypeStruct((S, 1024), _MXU_DTYPE), jax.ShapeDtypeStruct((1, 512), F32), jax.ShapeDtypeStruct((1, 512), F32),
                   jax.ShapeDtypeStruct((4, 128, 128), F32), jax.ShapeDtypeStruct((4, 128), F32)),
        in_specs=[pl.BlockSpec((tr, 512), lambda i: (i, 1)), pl.BlockSpec((tr, 512), lambda i: (i, 2)),
                  pl.BlockSpec((None, tr, 512), lambda i: (1, i, 0)), _vec_spec(512), _vec_spec(512),
                  pl.BlockSpec((4, 128, 128), lambda i: (0, 0, 0)), pl.BlockSpec((128, 4), lambda i: (0, 0))],
        out_specs=(_row_spec(tr, 1024), _vec_spec(512), _vec_spec(512), pl.BlockSpec((4, 128, 128), lambda i: (0, 0, 0)),
                   pl.BlockSpec((4, 128), lambda i: (0, 0))),
        compiler_params=_cp("arbitrary"))(z, z, dycat2, ln_g, ln_b, w_s, b_st)


def _sum_parts(name, parts, tr=512):
    P, R, C = parts.shape
    tr = _tile(R, tr) if R % 8 == 0 else R

    def body(p_ref, o_ref):
        g = p_ref[0]
        for k in range(1, P):
            g = g + p_ref[k]
        o_ref[...] = g

    return pl.pallas_call(
        body, name=name, grid=(R // tr,), out_shape=jax.ShapeDtypeStruct((R, C), F32),
        in_specs=[pl.BlockSpec((P, tr, C), lambda i: (0, i, 0))], out_specs=_row_spec(tr, C),
        compiler_params=_cp("parallel"))(parts)


def _adamw_math(w, m, v, g):
    c1 = 1.0 / (1.0 - ADAM_B1 ** ADAM_STEP)
    c2 = 1.0 / (1.0 - ADAM_B2 ** ADAM_STEP)
    m2 = ADAM_B1 * m + (1.0 - ADAM_B1) * g
    v2 = ADAM_B2 * v + (1.0 - ADAM_B2) * (g * g)
    return -ADAM_LR * ((m2 * c1) / (jnp.sqrt(v2 * c2) + ADAM_EPS) + ADAM_WD * w), m2, v2


def _adamw_small(name, params, parts):
    n = len(params)

    def body(*refs):
        ins, outs = refs[:4 * n], refs[4 * n:]
        for i in range(n):
            w_ref, m_ref, v_ref, p_ref = ins[4 * i:4 * i + 4]
            g = p_ref[0]
            for k in range(1, N_DEV):
                g = g + p_ref[k]
            delta, m2, v2 = _adamw_math(w_ref[...], m_ref[...], v_ref[...], g)
            outs[4 * i][...] = g
            outs[4 * i + 1][...] = delta
            outs[4 * i + 2][...] = m2
            outs[4 * i + 3][...] = v2

    flat = [a for (w, m, v), p in zip(params, parts) for a in (w, m, v, p)]
    out = pl.pallas_call(
        body, name=name, out_shape=[jax.ShapeDtypeStruct(w.shape, F32) for (w, _, _) in params for _ in range(4)],
        compiler_params=pltpu.CompilerParams(vmem_limit_bytes=_VMEM_LIMIT))(*flat)
    return [out[4 * i:4 * i + 4] for i in range(n)]


ADAMW_BLOCK_BYTES = 36 * 2 ** 20


def _adamw(name, w, m, v, parts):
    L, R, C = w.shape
    P = parts[0].shape[0]
    row_bytes = 2 * C * (7 * 4 + P * parts[0].dtype.itemsize)
    tr = R
    if R * row_bytes > ADAMW_BLOCK_BYTES:
        tr = next(t for t in (1024, 512, 256, 128, 64, 32, 16) if R % t == 0 and t * row_bytes <= ADAMW_BLOCK_BYTES)
    nr = R // tr
    c1 = 1.0 / (1.0 - ADAM_B1 ** ADAM_STEP)
    c2 = 1.0 / (1.0 - ADAM_B2 ** ADAM_STEP)

    def body(w_ref, m_ref, v_ref, *rest):
        p_refs, (g_ref, d_ref, mo_ref, vo_ref) = rest[:L], rest[L:]
        for ll in range(L):
            @pl.when(pl.program_id(0) == ll)
            def _(p_ref=p_refs[ll]):
                g = p_ref[0].astype(F32)
                for k in range(1, P):
                    g = g + p_ref[k].astype(F32)
                m2 = ADAM_B1 * m_ref[...] + (1.0 - ADAM_B1) * g
                v2 = ADAM_B2 * v_ref[...] + (1.0 - ADAM_B2) * (g * g)
                g_ref[...] = g
                mo_ref[...] = m2
                vo_ref[...] = v2
                d_ref[...] = -ADAM_LR * ((m2 * c1) / (jnp.sqrt(v2 * c2) + ADAM_EPS) + ADAM_WD * w_ref[...])

    def part_spec(ll):
        return pl.BlockSpec((P, tr, C), lambda l, i: (0, jnp.where(l == ll, i, jnp.where(l < ll, 0, nr - 1)), 0))

    full = pl.BlockSpec((None, tr, C), lambda l, i: (l, i, 0))
    sds = jax.ShapeDtypeStruct((L, R, C), F32)
    return pl.pallas_call(
        body, name=name, grid=(L, nr), out_shape=(sds, sds, sds, sds),
        in_specs=[full] * 3 + [part_spec(ll) for ll in range(L)],
        out_specs=(full,) * 4, compiler_params=_cp("arbitrary", "arbitrary"))(w, m, v, *parts)


def _rope_tables(positions):
    half = 16
    inv_freq = 10000.0 ** (-jnp.arange(half, dtype=F32) / half)
    ang = positions.astype(F32)[:, None] * inv_freq
    cos, sin = jnp.cos(ang), jnp.sin(ang)
    S = positions.shape[0]
    z16, z32, z64 = jnp.zeros((S, 16), F32), jnp.zeros((S, 32), F32), jnp.zeros((S, 64), F32)
    cosk = jnp.concatenate([z64, cos, cos, z32], axis=1)
    cosq = jnp.concatenate([jnp.ones((S, 64), F32), cos, cos, z32], axis=1)
    sa = jnp.concatenate([z64, -sin, z16, z32], axis=1)
    sb = jnp.concatenate([z64, z16, sin, z32], axis=1)
    return cosq, cosk, sa, sb


def _ffn_fwd(l, x, mod, n2g, get_w_up8, cw24, get_w_down4):
    sh, sc, gate = mod
    h = _rmsmod_fwd(f"ffn{l}_norm", x, n2g, sc, sh, n2g)
    w_up8 = get_w_up8(h)
    u8 = _mm_cols(f"ffn{l}_up", h, w_up8, out_dtype=ACT_DTYPE, tm=1024)
    S, n = u8.shape[1], u8.shape[2]
    u24 = u8.reshape(2, 4, S, n)
    a4 = _ffn_gate_fwd(f"ffn{l}_gate", u24, cw24)
    w_down4 = get_w_down4(a4)
    f, x_new = _mm_rows_resid(f"ffn{l}_down", a4, w_down4, x, gate)
    return x_new, (x, h, u24, a4, f), w_up8, w_down4


def _ffn_bwd(l, dx, saved, mod, n2g, w_up8, cw24, w_down4, me):
    sh, sc, gate = mod
    x, h, u24, a4, f = saved
    df, dgate = _gate_bwd(f"ffn{l}_gate_bwd", dx, f, gate)
    da4 = _mm_rows_dx(f"ffn{l}_down_dx", df, w_down4, out_dtype=ACT_DTYPE, tm=2048)
    dw_down4 = _mm_rows_dw(f"ffn{l}_down_dw", a4, df, out_dtype=WIRE_DTYPE)
    sent_down, token = _exchange_start(f"scatter_ffn{l}_down", [dw_down4.reshape(8, 352, dw_down4.shape[2])], True, dgate, me)
    du24, dcw24 = _ffn_gate_bwd(f"ffn{l}_act_bwd", u24, cw24, da4, token)
    du8 = du24.reshape((8,) + du24.shape[2:])
    dw_up8t = _mm_cols_dwt(f"ffn{l}_up_dw", h, du8, out_dtype=WIRE_DTYPE, tk=1024)
    sent_up, token = _exchange_start(f"scatter_ffn{l}_up", [dw_up8t], True, dcw24, me)
    dh = _mm_cols_dx(f"ffn{l}_up_dx", du8, w_up8, tm=1024, jb=4)
    dx_new, dn2g, dsc, dsh = _rmsmod_bwd(f"ffn{l}_norm_bwd", x, n2g, sc, dh, dx, token)
    return dx_new, dict(sent_up=sent_up, sent_down=sent_down, cw24=dcw24, n2g=dn2g, mod=(dsh, dsc, dgate))


def kernel(x, c, positions, ada_w, ada_b, norm1_g, norm2_g, ab_w_in, a_conv_w, b_mix_w, b_scale, ab_w_out, cd_w_in, c_q_norm_g, c_w_uq, c_kv_norm_g, c_w_ukv, d_ln_g, d_ln_b, d_w_s, d_b_s, cd_w_out, ffn_w_up, ffn_conv_w, ffn_w_down, final_norm_g, loss_target, m_ada_w, m_ada_b, m_norm1_g, m_norm2_g, m_ab_w_in, m_a_conv_w, m_b_mix_w, m_b_scale, m_ab_w_out, m_cd_w_in, m_c_q_norm_g, m_c_w_uq, m_c_kv_norm_g, m_c_w_ukv, m_d_ln_g, m_d_ln_b, m_d_w_s, m_d_b_s, m_cd_w_out, m_ffn_w_up, m_ffn_conv_w, m_ffn_w_down, m_final_norm_g, v_ada_w, v_ada_b, v_norm1_g, v_norm2_g, v_ab_w_in, v_a_conv_w, v_b_mix_w, v_b_scale, v_ab_w_out, v_cd_w_in, v_c_q_norm_g, v_c_w_uq, v_c_kv_norm_g, v_c_w_ukv, v_d_ln_g, v_d_ln_b, v_d_w_s, v_d_b_s, v_cd_w_out, v_ffn_w_up, v_ffn_conv_w, v_ffn_w_down, v_final_norm_g):
    S, D = x.shape[1], x.shape[2]
    me = 4 * lax.axis_index("x") + 2 * lax.axis_index("y") + lax.axis_index("c")
    x0, target = x[0], loss_target[0]
    W = _MXU_DTYPE

    small_shapes = [(1024,), (3, 64), (32,), (64,), (64,), (2, 3, 704)]
    (g0,) = _exchange("gather_small", [[_pack([c, a_conv_w, c_q_norm_g, d_ln_g, d_ln_b, ffn_conv_w])]], scatter=False)
    c_all, aconv_s, qg_s, lng_s, lnb_s, fcw_s = _unpack(g0[:, 0], small_shapes, lead=(N_DEV,))
    conv_w = aconv_s.transpose(1, 0, 2).reshape(3, 512)
    qg, ln_g, ln_b = qg_s.reshape(1, 256), lng_s.reshape(1, 512), lnb_s.reshape(1, 512)
    cw24 = [fcw_s[:, l].reshape(2, 4, 3, 704) for l in range(2)]
    c16 = jnp.pad(c_all, ((0, 16 - N_DEV), (0, 0)))

    mod_cols = _ada_fwd(c16, ada_w)
    (g1,) = _exchange("gather_mod", [[_pack([mod_cols])]], scatter=False)
    mod_all = _unpack(g1[:, 0], [(2, 16, 768)], lead=(N_DEV,))[0]
    mod_mine = lax.dynamic_index_in_dim(mod_all, me, axis=2, keepdims=False)
    mod = mod_mine.transpose(1, 0, 2).reshape(2, 6 * D) + ada_b
    mods = [[mod[l, k * D:(k + 1) * D].reshape(1, D) for k in range(6)] for l in range(2)]

    gw_ab, token = _hier_gather_start("gather_w_ab", [ab_w_in[0].astype(W), ab_w_out[0].astype(W)], mod, me)
    gw_up0, token = _hier_gather_start("gather_w_ffn0_up", [ffn_w_up[0].astype(W)], token, me)
    gw_dn0, token = _exchange_start("gather_w_ffn0_down", [ffn_w_down[0].astype(W)], False, token, me)
    gw_cd, token = _exchange_start("gather_w_cd", [
        cd_w_in[0].astype(W).reshape(1440, 128), c_w_uq[0].astype(W).reshape(192, 128), c_w_ukv[0].astype(W),
        cd_w_out[0].astype(W)], False, token, me)
    gw_up1, token = _exchange_start("gather_w_ffn1_up", [ffn_w_up[1].astype(W)], False, token, me)
    gw_dn1, started = _exchange_start("gather_w_ffn1_down", [ffn_w_down[1].astype(W)], False, token, me)

    cosq, cosk, sa, sb = _rope_tables(positions[0])
    n1g = [norm1_g[l].reshape(1, D) for l in range(2)]
    n2g = [norm2_g[l].reshape(1, D) for l in range(2)]
    mix_w, scale = b_mix_w[0], b_scale
    kvg = c_kv_norm_g
    w_s, b_st = d_w_s[0], d_b_s[0].T

    sh1, sc1, g1m = mods[0][:3]
    h_ab = _rmsmod_fwd("ab_norm", x0, n1g[0], sc1, sh1, started)
    w_abin8, w_about = _hier_gather_wait("wait_w_ab", _hier_gather_forward("forward_w_ab", gw_ab, h_ab), h_ab)
    w_about2 = w_about.reshape(2, 512, D)
    z8 = _mm_cols("ab_in", h_ab, w_abin8, tm=2048)
    ycat_ab = _ab_mix_fwd(z8, conv_w, mix_w, scale)
    y_ab, x1 = _mm_rows_resid("ab_out", ycat_ab, w_about2, x0, g1m)
    w_up8, w_down4 = [None, None], [None, None]
    gw_up0 = _hier_gather_forward("forward_w_ffn0_up", gw_up0, x1)
    x2, ffn0_saved, w_up8[0], w_down4[0] = _ffn_fwd(
        0, x1, mods[0][3:], n2g[0], lambda after: _hier_gather_wait("wait_w_ffn0_up", gw_up0, after)[0], cw24[0],
        lambda after: _exchange_wait("wait_w_ffn0_down", gw_dn0, after)[0].reshape(4, 704, D))

    w_cdin, w_uq, w_ukv, w_cdout = _exchange_wait("wait_w_cd", gw_cd, x2)
    w_cdout2 = w_cdout.reshape(2, 512, D)
    w_cd = w_cdin.reshape(8, D, 180).transpose(1, 0, 2).reshape(D, 1440)
    zc = lambda n: jnp.zeros((D, n), W)
    w_cd_pad = jnp.concatenate([w_cd[:, :384], zc(64), w_cd[:, 384:416], zc(32), w_cd[:, 416:]], axis=1)
    w_uq_pad = jnp.pad(w_uq.reshape(8, 256, 96).transpose(1, 0, 2), ((0, 0), (0, 0), (0, 32))).reshape(256, 1024)
    w_ukv_h = w_ukv.transpose(1, 0, 2)
    w_k_pad = jnp.pad(w_ukv_h[:, :, :64], ((0, 0), (0, 0), (0, 64))).reshape(128, 1024)
    w_kv_pad = jnp.concatenate([w_k_pad, w_ukv_h[:, :, 64:].reshape(128, 512)], axis=1)

    sh1, sc1, g1c = mods[1][:3]
    h_cd = _rmsmod_fwd("cd_norm", x2, n1g[1], sc1, sh1, n1g[1])
    z_cd = _mm_nn("cd_in", h_cd, w_cd_pad)
    qn, kvn = _mla_prep_fwd(z_cd, qg, kvg)
    qraw = _mm_nn("cd_uq", qn, w_uq_pad)
    kvall = _mm_nn("cd_ukv", kvn, w_kv_pad)
    q_r, k_r, v_r = _rope_fwd(qraw, kvall, z_cd, cosq, cosk, sa, sb)
    o, lse = _attn_fwd(q_r, k_r, v_r)
    ycat_cd = _sgu_fwd(z_cd, o, ln_g, ln_b, w_s, b_st)
    y_cd, x3 = _mm_rows_resid("cd_out", ycat_cd, w_cdout2, x2, g1c)
    x4, ffn1_saved, w_up8[1], w_down4[1] = _ffn_fwd(
        1, x3, mods[1][3:], n2g[1], lambda after: _exchange_wait("wait_w_ffn1_up", gw_up1, after)[0], cw24[1],
        lambda after: _exchange_wait("wait_w_ffn1_down", gw_dn1, after)[0].reshape(4, 704, D))

    loss_local, dx4, dfg = _loss_head(x4, final_norm_g.reshape(1, D), target)

    dx3, gf1 = _ffn_bwd(1, dx4, ffn1_saved, mods[1][3:], n2g[1], w_up8[1], cw24[1], w_down4[1], me)

    dy, dg1c = _gate_bwd("cd_gate_bwd", dx3, y_cd, g1c)
    dycat = _mm_rows_dx("cd_out_dx", dy, w_cdout2)
    dw_cdout = _mm_rows_dw("cd_out_dw", ycat_cd, dy, out_dtype=WIRE_DTYPE)
    duv, dln_g, dln_b, dws, dbs = _sgu_bwd(z_cd, dycat, ln_g, ln_b, w_s, b_st)
    dq_r, dk_r, dv_r = _attn_bwd(q_r, k_r, v_r, o, lse, dycat)
    dqraw, dkvall, dkpe = _rope_bwd(dq_r, dk_r, dv_r, cosq, cosk, sa, sb)
    dqn = _mm_nt("cd_uq_dx", dqraw, w_uq_pad, tn=256)
    dkvn = _mm_nt("cd_ukv_dx", dkvall, w_kv_pad, tn=128)
    dw_uq_pad = _mm_tn("cd_uq_dw", qn, dqraw, tm=256)
    dw_kv_pad = _mm_tn("cd_ukv_dw", kvn, dkvall, tm=128)
    dz_cd, dqg, dkvg = _mla_prep_bwd(z_cd, qg, kvg, dqn, dkvn, dkpe, duv)
    dh_cd = _mm_nt("cd_in_dx", dz_cd, w_cd_pad)
    dw_cd_pad = _mm_tn("cd_in_dw", h_cd, dz_cd)
    dw_cd =jnp.concatenate([dw_cd_pad[:, :384], dw_cd_pad[:, 448:480], dw_cd_pad[:, 512:]], axis=1)
    dw_cd8 = dw_cd.reshape(D, 8, 180).transpose(1, 0, 2).reshape(8, 1440, 128).astype(WIRE_DTYPE)
    dw_uq8 = dw_uq_pad.reshape(256, 8, 128)[:, :, :96].transpose(1, 0, 2).reshape(8, 192, 128).astype(WIRE_DTYPE)
    dw_ukv8 = jnp.concatenate([dw_kv_pad[:, :1024].reshape(128, 8, 128)[:, :, :64], dw_kv_pad[:, 1024:].reshape(128, 8, 64)],
                              axis=2).transpose(1, 0, 2).astype(WIRE_DTYPE)
    sent_cd, token = _exchange_start("scatter_cd", [dw_cd8, dw_uq8, dw_ukv8, dw_cdout.reshape(8, 128, D)], True, dqg, me)
    early_names = ["c_kv_norm_g", "d_w_s", "d_b_s", "final_norm_g", "c_q_norm_g", "d_ln_g", "d_ln_b"]
    early_grads = [dkvg, dws.reshape(512, 128), dbs, dfg, dqg.reshape(8, 1, 32), dln_g.reshape(8, 1, 64), dln_b.reshape(8, 1, 64)]
    early_sent, token = _exchange_start("gather_small_grads_early", early_grads, [False] * 4 + [True] * 3, token, me)
    dx2, dn1g_cd, dsc1_cd, dsh1_cd = _rmsmod_bwd("cd_norm_bwd", x2, n1g[1], sc1, dh_cd, dx3, token)

    dx1, gf0 = _ffn_bwd(0, dx2, ffn0_saved, mods[0][3:], n2g[0], w_up8[0], cw24[0], w_down4[0], me)

    dy, dg1m = _gate_bwd("ab_gate_bwd", dx1, y_ab, g1m)
    dw_about = _mm_rows_dw("ab_out_dw", ycat_ab, dy, out_dtype=WIRE_DTYPE)
    sent_about, token = _exchange_start("scatter_ab_out", [dw_about.reshape(8, 128, D)], True, dg1m, me)
    dycat = _mm_rows_dx("ab_out_dx", dy, w_about2)
    dz8, dconv_w, dmix_w, dscale = _ab_mix_bwd(z8, dycat, conv_w, mix_w, scale, token)
    dz8 = dz8.reshape(8, S, 256)
    dw_abin8 = _mm_cols_dw("ab_in_dw", h_ab, dz8, out_dtype=WIRE_DTYPE, tk=1024)
    sent_abin, token = _exchange_start("scatter_ab_in", [dw_abin8], True, dscale, me)
    dh_ab = _mm_cols_dx("ab_in_dx", dz8, w_abin8)
    dx0, dn1g_ab, dsc1_ab, dsh1_ab = _rmsmod_bwd("ab_norm_bwd", x0, n1g[0], mods[0][1], dh_ab, dx1, token)

    dmod = jnp.stack([jnp.concatenate([dsh1_ab, dsc1_ab, dg1m, *gf0["mod"]], axis=1)[0],
                      jnp.concatenate([dsh1_cd, dsc1_cd, dg1c, *gf1["mod"]], axis=1)[0]])
    late_names = ["ada_b", "norm1_g", "norm2_g", "b_mix_w", "b_scale", "a_conv_w", "ffn_conv_w"]
    late_grads = [dmod, jnp.concatenate([dn1g_ab, dn1g_cd]), jnp.concatenate([gf0["n2g"], gf1["n2g"]]),
                  dmix_w.reshape(512, 128), dscale, dconv_w.reshape(3, 8, 64).transpose(1, 0, 2),
                  jnp.stack([gf0["cw24"].reshape(8, 3, 704), gf1["cw24"].reshape(8, 3, 704)], axis=1)]
    small_view = dict(ada_b=(2, 6 * D), norm1_g=(2, D), norm2_g=(2, D), b_mix_w=(512, 128), b_scale=(1, 512), c_kv_norm_g=(1, 128),
                      d_w_s=(512, 128), d_b_s=(4, 128), final_norm_g=(1, D),
                      a_conv_w=(3, 64), c_q_norm_g=(1, 32), d_ln_g=(1, 64), d_ln_b=(1, 64), ffn_conv_w=(2, 3, 704))
    late_sent, token = _exchange_start("gather_small_grads_late", late_grads, [False] * 5 + [True] * 2, dx0, me)

    res = {}

    def update(name, w, m, v, parts, shape3d):
        outs = _adamw("adamw_" + name, w.reshape(shape3d), m.reshape(shape3d), v.reshape(shape3d),
                      [p.reshape((p.shape[0],) + shape3d[1:]) for p in parts])
        res[name] = [o_.reshape(w.shape) for o_ in outs]

    p_cdin, p_uq, p_ukv, p_cdout = _exchange_wait("wait_scatter_cd", sent_cd, token)
    update("cd_w_in", cd_w_in, m_cd_w_in, v_cd_w_in, [p_cdin], (1, 1440, 128))
    update("c_w_uq", c_w_uq, m_c_w_uq, v_c_w_uq, [p_uq], (1, 192, 128))
    update("c_w_ukv", c_w_ukv, m_c_w_ukv, v_c_w_ukv, [p_ukv], (1, 128, 128))
    update("cd_w_out", cd_w_out, m_cd_w_out, v_cd_w_out, [p_cdout], (1, 128, D))
    (p_dn1,) = _exchange_wait("wait_scatter_ffn1_down", gf1["sent_down"], token)
    (p_dn0,) = _exchange_wait("wait_scatter_ffn0_down", gf0["sent_down"], res["cd_w_out"][0])
    update("ffn_w_down", ffn_w_down, m_ffn_w_down, v_ffn_w_down, [p_dn0, p_dn1], (2, 352, D))
    (p_up1,) = _exchange_wait("wait_scatter_ffn1_up", gf1["sent_up"], token)
    (p_up0,) = _exchange_wait("wait_scatter_ffn0_up", gf0["sent_up"], res["ffn_w_down"][0])
    swap = lambda a: jnp.swapaxes(a, 1, 2)
    update("ffn_w_up", swap(ffn_w_up), swap(m_ffn_w_up), swap(v_ffn_w_up), [p_up0, p_up1], (2, 704, D))
    up_done = res["ffn_w_up"][0]
    res["ffn_w_up"] = [swap(o_) for o_ in res["ffn_w_up"]]
    (p_about,) = _exchange_wait("wait_scatter_ab_out", sent_about, up_done)
    update("ab_w_out", ab_w_out, m_ab_w_out, v_ab_w_out, [p_about], (1, 128, D))
    (p_abin,) = _exchange_wait("wait_scatter_ab_in", sent_abin, res["ab_w_out"][0])
    update("ab_w_in", ab_w_in, m_ab_w_in, v_ab_w_in, [p_abin], (1, D, 256))

    early_parts = _exchange_wait("wait_small_grads_early", early_sent, res["ab_w_in"][0])
    late_parts = _exchange_wait("wait_small_grads_late", late_sent, res["ab_w_in"][0])
    small_names = early_names + late_names
    small_parts = list(early_parts) + list(late_parts)
    dmod_all = late_parts[0]
    dmod_cols = lax.dynamic_slice_in_dim(dmod_all, me * 768, 768, axis=2).transpose(1, 0, 2)
    g_ada_w = _ada_bwd(c16, jnp.pad(dmod_cols, ((0, 0), (0, 16 - N_DEV), (0, 0))))
    update("ada_w", ada_w, m_ada_w, v_ada_w, [g_ada_w[l][None] for l in range(2)], (2, D, 768))

    small_w = dict(ada_b=(ada_b, m_ada_b, v_ada_b), norm1_g=(norm1_g, m_norm1_g, v_norm1_g), norm2_g=(norm2_g, m_norm2_g, v_norm2_g),
                   b_mix_w=(b_mix_w, m_b_mix_w, v_b_mix_w), b_scale=(b_scale, m_b_scale, v_b_scale),
                   c_kv_norm_g=(c_kv_norm_g, m_c_kv_norm_g, v_c_kv_norm_g), d_w_s=(d_w_s, m_d_w_s, v_d_w_s),
                   d_b_s=(d_b_s, m_d_b_s, v_d_b_s), final_norm_g=(final_norm_g, m_final_norm_g, v_final_norm_g),
                   a_conv_w=(a_conv_w, m_a_conv_w, v_a_conv_w), c_q_norm_g=(c_q_norm_g, m_c_q_norm_g, v_c_q_norm_g),
                   d_ln_g=(d_ln_g, m_d_ln_g, v_d_ln_g), d_ln_b=(d_ln_b, m_d_ln_b, v_d_ln_b),
                   ffn_conv_w=(ffn_conv_w, m_ffn_conv_w, v_ffn_conv_w))
    small_out = _adamw_small("adamw_small", [tuple(a.reshape(small_view[n]) for a in small_w[n]) for n in small_names],
                             list(small_parts))
    for n, outs in zip(small_names, small_out):
        res[n] = [o_.reshape(small_w[n][0].shape) for o_ in outs]

    loss = lax.psum(loss_local[0, 0], ("x", "y", "c"))
    order = ["ada_w", "ada_b", "norm1_g", "norm2_g", "ab_w_in", "a_conv_w", "b_mix_w", "b_scale", "ab_w_out", "cd_w_in", "c_q_norm_g",
             "c_w_uq", "c_kv_norm_g", "c_w_ukv", "d_ln_g", "d_ln_b", "d_w_s", "d_b_s", "cd_w_out", "ffn_w_up", "ffn_conv_w",
             "ffn_w_down", "final_norm_g"]
    return (loss, dx0[None], *[res[n][0] for n in order], *[res[n][1] for n in order], *[res[n][2] for n in order],
            *[res[n][3] for n in order])
```

```python
import functools
import math

import jax
import jax.numpy as jnp
from jax import lax
from jax.experimental import pallas as pl
from jax.experimental.pallas import tpu as pltpu

F32 = jnp.float32
BF16 = jnp.bfloat16
_MXU_DTYPE = BF16
WIRE_DTYPE = BF16
ACT_DTYPE = BF16
_VMEM_LIMIT = 56 * 2 ** 20
N_DEV = 8
EPS = 1e-6
POOL_WINDOWS = (2, 4, 8, 16)
ATTN_SCALE = (64 + 32) ** -0.5
ADAM_LR, ADAM_B1, ADAM_B2, ADAM_EPS, ADAM_WD, ADAM_STEP = 0.001, 0.9, 0.999, 1e-08, 0.01, 10
MESH = pl.DeviceIdType.MESH
ANY = pl.BlockSpec(memory_space=pl.ANY)


def _cp(*sem):
    return pltpu.CompilerParams(dimension_semantics=sem, vmem_limit_bytes=_VMEM_LIMIT)


def _dot(a, b, contract):
    dn = {"nn": (((1,), (0,)), ((), ())), "nt": (((1,), (1,)), ((), ())), "tn": (((0,), (0,)), ((), ()))}[contract]
    return lax.dot_general(a.astype(_MXU_DTYPE), b.astype(_MXU_DTYPE), dn, preferred_element_type=F32)


def _my_position():
    x, y, c = lax.axis_index("x"), lax.axis_index("y"), lax.axis_index("c")
    return x, y, c, 4 * x + 2 * y + c


def _exchange(name, groups, scatter):
    flat = [a for g in groups for a in g]
    n_in, n_grp = len(flat), len(groups)
    out_shapes = []
    for g in groups:
        slab = g[0].shape[1:] if scatter else g[0].shape
        out_shapes.append(jax.ShapeDtypeStruct((N_DEV, len(g)) + tuple(slab), g[0].dtype))

    def body(*refs):
        ins, outs = refs[:n_in], refs[n_in:n_in + n_grp]
        send_sems, recv_sems, local_sems = refs[n_in + n_grp:]
        x, y, c, me = _my_position()
        i = 0
        for gi, g in enumerate(groups):
            for l in range(len(g)):
                src = ins[i]
                i += 1
                pltpu.make_async_copy(src.at[me] if scatter else src, outs[gi].at[me, l], local_sems.at[gi]).start()
                for k in range(1, N_DEV):
                    px = 1 - x if k & 4 else x
                    py = 1 - y if k & 2 else y
                    pc = 1 - c if k & 1 else c
                    peer = 4 * px + 2 * py + pc
                    pltpu.make_async_remote_copy(
                        src_ref=src.at[peer] if scatter else src, dst_ref=outs[gi].at[me, l],
                        send_sem=send_sems.at[gi], recv_sem=recv_sems.at[gi],
                        device_id=(px, py, pc), device_id_type=MESH).start()
        for gi in range(n_grp):
            mine = outs[gi].at[me]
            pltpu.make_async_copy(mine, mine, local_sems.at[gi]).wait()
            seven = outs[gi].at[pl.ds(0, N_DEV - 1)]
            w = pltpu.make_async_remote_copy(src_ref=seven, dst_ref=seven, send_sem=send_sems.at[gi],
                                             recv_sem=recv_sems.at[gi], device_id=(x, y, c), device_id_type=MESH)
            w.wait_send()
            w.wait_recv()

    return pl.pallas_call(
        body, name=name, out_shape=tuple(out_shapes),
        in_specs=[ANY] * n_in, out_specs=tuple([ANY] * n_grp),
        scratch_shapes=[pltpu.SemaphoreType.DMA((n_grp,)), pltpu.SemaphoreType.DMA((n_grp,)),
                        pltpu.SemaphoreType.DMA((n_grp,))],
        compiler_params=pltpu.CompilerParams(has_side_effects=True),
    )(*flat)


HBM_SPEC = pl.BlockSpec(memory_space=pltpu.HBM)
SEM_SPEC = pl.BlockSpec(memory_space=pltpu.SEMAPHORE)
EFFECT = pltpu.SideEffectType.DATAFLOW_SIDE_EFFECTING


def _put_mine(name, srcs, scatter, me):
    n = len(srcs)
    slabs = [tuple(s.shape[1:] if sc else s.shape) for s, sc in zip(srcs, scatter)]

    def body(me_ref, *refs):
        for i in range(n):
            refs[n + i][...] = refs[i][...]

    def at_me(slab):
        return pl.BlockSpec((None,) + slab, lambda g, me_ref, nd=len(slab): (me_ref[0],) + (0,) * nd)

    def whole(slab):
        return pl.BlockSpec(slab, lambda g, me_ref, nd=len(slab): (0,) * nd)

    return pl.pallas_call(
        body, name=name,
        grid_spec=pltpu.PrefetchScalarGridSpec(
            num_scalar_prefetch=1, grid=(1,),
            in_specs=[at_me(slab) if sc else whole(slab) for slab, sc in zip(slabs, scatter)],
            out_specs=[at_me(slab) for slab in slabs]),
        out_shape=[jax.ShapeDtypeStruct((N_DEV,) + slab, s.dtype) for slab, s in zip(slabs, srcs)],
        compiler_params=_cp("arbitrary"))(me.reshape(1), *srcs)


def _exchange_start(name, srcs, scatter, after, me):
    n = len(srcs)
    scatter = list(scatter) if isinstance(scatter, (list, tuple)) else [scatter] * n
    lands = _put_mine(name + "_mine", srcs, scatter, me)
    srcs = [pltpu.with_memory_space_constraint(a, pltpu.HBM) for a in srcs]
    lands = [pltpu.with_memory_space_constraint(a, pltpu.HBM) for a in lands]

    def body(*refs):
        ins, land = refs[:n], refs[n:2 * n]
        send_sems, recv_sems, token = refs[2 * n + 1], refs[2 * n + 2], refs[-1]
        x, y, c, me_in = _my_position()
        for i in range(n):
            for k in range(1, N_DEV):
                px = 1 - x if k & 4 else x
                py = 1 - y if k & 2 else y
                pc = 1 - c if k & 1 else c
                pltpu.make_async_remote_copy(
                    src_ref=ins[i].at[4 * px + 2 * py + pc] if scatter[i] else ins[i], dst_ref=land[i].at[me_in],
                    send_sem=send_sems.at[i], recv_sem=recv_sems.at[i],
                    device_id=(px, py, pc), device_id_type=MESH).start()
        token[...] = jnp.zeros_like(token)

    outs = pl.pallas_call(
        body, name=name,
        out_shape=(pltpu.SemaphoreType.DMA((n,)), pltpu.SemaphoreType.DMA((n,)),
                   *[pltpu.HBM(a.shape, a.dtype) for a in srcs], *[pltpu.HBM(a.shape, a.dtype) for a in lands],
                   jax.ShapeDtypeStruct((8, 128), F32)),
        in_specs=[HBM_SPEC] * (2 * n) + [ANY],
        out_specs=(SEM_SPEC, SEM_SPEC, *[HBM_SPEC] * (2 * n), pl.BlockSpec(memory_space=pltpu.VMEM)),
        input_output_aliases={i: 2 + i for i in range(2 * n)},
        compiler_params=pltpu.CompilerParams(has_side_effects=EFFECT),
    )(*srcs, *lands, after)
    return (outs[0], outs[1], outs[2:2 + n], outs[2 + n:2 + 2 * n]), outs[-1]


def _exchange_wait(name, handle, after):
    send_sems, recv_sems, srcs, lands = handle
    n = len(srcs)

    def body(*refs):
        land, send_ref, recv_ref = refs[n:2 * n], refs[2 * n], refs[2 * n + 1]
        x, y, c, _ = _my_position()
        for i in range(n):
            seven = land[i].at[pl.ds(0, N_DEV - 1)]
            w = pltpu.make_async_remote_copy(src_ref=seven, dst_ref=seven, send_sem=send_ref.at[i], recv_sem=recv_ref.at[i],
                                             device_id=(x, y, c), device_id_type=MESH)
            w.wait_send()
            w.wait_recv()

    outs = pl.pallas_call(
        body, name=name,
        out_shape=(*[pltpu.HBM(a.shape, a.dtype) for a in srcs], *[pltpu.HBM(a.shape, a.dtype) for a in lands]),
        in_specs=[HBM_SPEC] * (2 * n) + [SEM_SPEC, SEM_SPEC, ANY],
        out_specs=tuple([HBM_SPEC] * (2 * n)),
        input_output_aliases={i: i for i in range(2 * n)},
        compiler_params=pltpu.CompilerParams(has_side_effects=EFFECT),
    )(*srcs, *lands, send_sems, recv_sems, after)
    return outs[n:]


def _other_chips(x, y):
    return [(1 - x, y), (x, 1 - y), (1 - x, 1 - y)]


def _hier_gather_start(name, srcs, after, me):
    n = len(srcs)
    lands = _put_mine(name + "_mine", srcs, [False] * n, me)
    srcs = [pltpu.with_memory_space_constraint(a, pltpu.HBM) for a in srcs]
    lands = [pltpu.with_memory_space_constraint(a, pltpu.HBM) for a in lands]

    def body(*refs):
        ins, land = refs[:n], refs[n:2 * n]
        ici_send, ici_recv, d2d_send, d2d_recv = refs[2 * n + 1:2 * n + 5]
        token = refs[-1]
        x, y, c, me_in = _my_position()
        for i in range(n):
            pltpu.make_async_remote_copy(src_ref=ins[i], dst_ref=land[i].at[me_in], send_sem=d2d_send.at[i], recv_sem=d2d_recv.at[i],
                                         device_id=(x, y, 1 - c), device_id_type=MESH).start()
            for px, py in _other_chips(x, y):
                pltpu.make_async_remote_copy(src_ref=ins[i], dst_ref=land[i].at[me_in], send_sem=ici_send.at[i],
                                             recv_sem=ici_recv.at[i], device_id=(px, py, c), device_id_type=MESH).start()
        token[...] = jnp.zeros_like(token)

    sem = pltpu.SemaphoreType.DMA((n,))
    outs = pl.pallas_call(
        body, name=name,
        out_shape=(sem, sem, sem, sem, *[pltpu.HBM(a.shape, a.dtype) for a in srcs], *[pltpu.HBM(a.shape, a.dtype) for a in lands],
                   jax.ShapeDtypeStruct((8, 128), F32)),
        in_specs=[HBM_SPEC] * (2 * n) + [ANY],
        out_specs=(SEM_SPEC,) * 4 + (HBM_SPEC,) * (2 * n) + (pl.BlockSpec(memory_space=pltpu.VMEM),),
        input_output_aliases={i: 4 + i for i in range(2 * n)},
        compiler_params=pltpu.CompilerParams(has_side_effects=EFFECT),
    )(*srcs, *lands, after)
    return (outs[:4], outs[4:4 + n], outs[4 + n:4 + 2 * n]), outs[-1]


def _hier_gather_forward(name, handle, after):
    sems, srcs, lands = handle
    n = len(srcs)

    def body(*refs):
        land = refs[n:2 * n]
        ici_send, ici_recv, d2d_send, d2d_recv = refs[2 * n:2 * n + 4]
        x, y, c, _ = _my_position()
        for i in range(n):
            three = land[i].at[pl.ds(0, 3)]
            pltpu.make_async_remote_copy(src_ref=three, dst_ref=three, send_sem=ici_send.at[i], recv_sem=ici_recv.at[i],
                                         device_id=(x, y, c), device_id_type=MESH).wait_recv()
            for px, py in _other_chips(x, y):
                slab = land[i].at[4 * px + 2 * py + c]
                pltpu.make_async_remote_copy(src_ref=slab, dst_ref=slab, send_sem=d2d_send.at[i], recv_sem=d2d_recv.at[i],
                                             device_id=(x, y, 1 - c), device_id_type=MESH).start()

    outs = pl.pallas_call(
        body, name=name,
        out_shape=(*[pltpu.HBM(a.shape, a.dtype) for a in srcs], *[pltpu.HBM(a.shape, a.dtype) for a in lands]),
        in_specs=[HBM_SPEC] * (2 * n) + [SEM_SPEC] * 4 + [ANY],
        out_specs=tuple([HBM_SPEC] * (2 * n)),
        input_output_aliases={i: i for i in range(2 * n)},
        compiler_params=pltpu.CompilerParams(has_side_effects=EFFECT),
    )(*srcs, *lands, *sems, after)
    return (sems, outs[:n], outs[n:])


def _hier_gather_wait(name, handle, after):
    sems, srcs, lands = handle
    n = len(srcs)

    def body(*refs):
        land = refs[n:2 * n]
        ici_send, ici_recv, d2d_send, d2d_recv = refs[2 * n:2 * n + 4]
        x, y, c, _ = _my_position()
        for i in range(n):
            three, four = land[i].at[pl.ds(0, 3)], land[i].at[pl.ds(0, 4)]
            pltpu.make_async_remote_copy(src_ref=three, dst_ref=three, send_sem=ici_send.at[i], recv_sem=ici_recv.at[i],
                                         device_id=(x, y, c), device_id_type=MESH).wait_send()
            w = pltpu.make_async_remote_copy(src_ref=four, dst_ref=four, send_sem=d2d_send.at[i], recv_sem=d2d_recv.at[i],
                                             device_id=(x, y, c), device_id_type=MESH)
            w.wait_send()
            w.wait_recv()

    outs = pl.pallas_call(
        body, name=name,
        out_shape=(*[pltpu.HBM(a.shape, a.dtype) for a in srcs], *[pltpu.HBM(a.shape, a.dtype) for a in lands]),
        in_specs=[HBM_SPEC] * (2 * n) + [SEM_SPEC] * 4 + [ANY],
        out_specs=tuple([HBM_SPEC] * (2 * n)),
        input_output_aliases={i: i for i in range(2 * n)},
        compiler_params=pltpu.CompilerParams(has_side_effects=EFFECT),
    )(*srcs, *lands, *sems, after)
    return outs[n:]


def _pack(arrs):
    flat = jnp.concatenate([a.reshape(-1).astype(F32) for a in arrs])
    n = flat.shape[0]
    rows = -(-n // 1024) * 8
    return jnp.pad(flat, (0, rows * 128 - n)).reshape(rows, 128)


def _unpack(buf, shapes, lead=()):
    flat = buf.reshape(lead + (-1,))
    out, off = [], 0
    for s in shapes:
        n = math.prod(s)
        out.append(flat[..., off:off + n].reshape(lead + tuple(s)))
        off += n
    return out


def _mm(name, a, a_spec, b, b_spec, out_sds, o_spec, grid, contract, nk=1, stacked=0):
    o_blk = tuple(d for d in o_spec.block_shape if d is not None)

    def body(a_ref, b_ref, o_ref, *acc):
        if stacked:
            r = _dot(a_ref[0], b_ref[0], contract)
            for q in range(1, stacked):
                r = r + _dot(a_ref[q], b_ref[q], contract)
        else:
            r = _dot(a_ref[...], b_ref[...], contract)
        if nk == 1:
            o_ref[...] = r.astype(o_ref.dtype)
        else:
            k = pl.program_id(len(grid) - 1)

            @pl.when(k == 0)
            def _():
                acc[0][...] = r

            @pl.when(k > 0)
            def _():
                acc[0][...] += r

            @pl.when(k == nk - 1)
            def _():
                o_ref[...] = acc[0][...].astype(o_ref.dtype)

    sem = ("parallel",) * (len(grid) - 1) + (("arbitrary",) if nk > 1 else ("parallel",))
    return pl.pallas_call(
        body, name=name, out_shape=out_sds, grid=grid, in_specs=[a_spec, b_spec], out_specs=o_spec,
        scratch_shapes=[pltpu.VMEM(o_blk, F32)] if nk > 1 else [], compiler_params=_cp(*sem))(a, b)


def _tile(n, want):
    t = min(n, want)
    assert n % t == 0, (n, t)
    return t


def _mm_nn(name, a, b, out_dtype=F32, tm=512, tn=512):
    (M, K), N = a.shape, b.shape[1]
    tm, tn = _tile(M, tm), _tile(N, tn)
    return _mm(name, a, pl.BlockSpec((tm, K), lambda i, j: (i, 0)), b, pl.BlockSpec((K, tn), lambda i, j: (0, j)),
               jax.ShapeDtypeStruct((M, N), out_dtype), pl.BlockSpec((tm, tn), lambda i, j: (i, j)),
               (M // tm, N // tn), "nn")


def _mm_nt(name, a, b, out_dtype=F32, tm=512, tn=512):
    (M, K), N = a.shape, b.shape[0]
    tm, tn = _tile(M, tm), _tile(N, tn)
    return _mm(name, a, pl.BlockSpec((tm, K), lambda i, j: (i, 0)), b, pl.BlockSpec((tn, K), lambda i, j: (j, 0)),
               jax.ShapeDtypeStruct((M, N), out_dtype), pl.BlockSpec((tm, tn), lambda i, j: (i, j)),
               (M // tm, N // tn), "nt")


def _mm_tn(name, a, b, out_dtype=F32, tm=512, tn=512):
    (K, M), N = a.shape, b.shape[1]
    tm, tn = _tile(M, tm), _tile(N, tn)
    return _mm(name, a, pl.BlockSpec((K, tm), lambda i, j: (0, i)), b, pl.BlockSpec((K, tn), lambda i, j: (0, j)),
               jax.ShapeDtypeStruct((M, N), out_dtype), pl.BlockSpec((tm, tn), lambda i, j: (i, j)),
               (M // tm, N // tn), "tn")


def _mm_cols(name, a, w, out_dtype=F32, tm=512):
    (M, K), (J, _, n) = a.shape, w.shape
    tm = _tile(M, tm)
    return _mm(name, a, pl.BlockSpec((tm, K), lambda j, i: (i, 0)), w, pl.BlockSpec((None, K, n), lambda j, i: (j, 0, 0)),
               jax.ShapeDtypeStruct((J, M, n), out_dtype), pl.BlockSpec((None, tm, n), lambda j, i: (j, i, 0)),
               (J, M // tm), "nn")


def _mm_cols_dx(name, d, w, out_dtype=F32, tm=512, jb=None):
    (J, M, n), K = d.shape, w.shape[1]
    tm, jb = _tile(M, tm), J if jb is None else jb
    return _mm(name, d, pl.BlockSpec((jb, tm, n), lambda i, j: (j, i, 0)), w, pl.BlockSpec((jb, K, n), lambda i, j: (j, 0, 0)),
               jax.ShapeDtypeStruct((M, K), out_dtype), pl.BlockSpec((tm, K), lambda i, j: (i, 0)),
               (M // tm, J // jb), "nt", nk=J // jb, stacked=jb)


def _mm_cols_dw(name, a, d, out_dtype=F32, tk=512):
    (M, K), (J, _, n) = a.shape, d.shape
    tk = _tile(K, tk)
    return _mm(name, a, pl.BlockSpec((M, tk), lambda j, i: (0, i)), d, pl.BlockSpec((None, M, n), lambda j, i: (j, 0, 0)),
               jax.ShapeDtypeStruct((J, K, n), out_dtype), pl.BlockSpec((None, tk, n), lambda j, i: (j, i, 0)),
               (J, K // tk), "tn")


def _mm_cols_dwt(name, a, d, out_dtype=F32, tk=512):
    (M, K), (J, _, n) = a.shape, d.shape
    tk = _tile(K, tk)
    return _mm(name, d, pl.BlockSpec((None, M, n), lambda j, i: (j, 0, 0)), a, pl.BlockSpec((M, tk), lambda j, i: (0, i)),
               jax.ShapeDtypeStruct((J, n, K), out_dtype), pl.BlockSpec((None, n, tk), lambda j, i: (j, 0, i)),
               (J, K // tk), "tn")


def _mm_rows_resid(name, a, w, resid, gate, tm=512):
    (Q, M, k), N = a.shape, w.shape[2]
    tm = _tile(M, tm)

    def body(a_ref, w_ref, r_ref, g_ref, y_ref, x_ref):
        y = _dot(a_ref[0], w_ref[0], "nn")
        for q in range(1, Q):
            y = y + _dot(a_ref[q], w_ref[q], "nn")
        y_ref[...] = y
        x_ref[...] = r_ref[...] + g_ref[...] * y

    return pl.pallas_call(
        body, name=name, grid=(M // tm,),
        out_shape=(jax.ShapeDtypeStruct((M, N), F32), jax.ShapeDtypeStruct((M, N), F32)),
        in_specs=[pl.BlockSpec((Q, tm, k), lambda i: (0, i, 0)), pl.BlockSpec((Q, k, N), lambda i: (0, 0, 0)),
                  pl.BlockSpec((tm, N), lambda i: (i, 0)), pl.BlockSpec((1, N), lambda i: (0, 0))],
        out_specs=(pl.BlockSpec((tm, N), lambda i: (i, 0)), pl.BlockSpec((tm, N), lambda i: (i, 0))),
        compiler_params=_cp("parallel"))(a, w, resid, gate)


def _mm_rows_dx(name, d, w, out_dtype=F32, tm=512):
    (M, N), (Q, k, _) = d.shape, w.shape
    tm = _tile(M, tm)
    return _mm(name, d, pl.BlockSpec((tm, N), lambda q, i: (i, 0)), w, pl.BlockSpec((None, k, N), lambda q, i: (q, 0, 0)),
               jax.ShapeDtypeStruct((Q, M, k), out_dtype), pl.BlockSpec((None, tm, k), lambda q, i: (q, i, 0)),
               (Q, M // tm), "nt")


def _mm_rows_dw(name, a, d, out_dtype=F32, tn=512):
    (Q, M, k), N = a.shape, d.shape[1]
    tn = _tile(N, tn)
    return _mm(name, a, pl.BlockSpec((None, M, k), lambda q, j: (q, 0, 0)), d, pl.BlockSpec((M, tn), lambda q, j: (0, j)),
               jax.ShapeDtypeStruct((Q, k, N), out_dtype), pl.BlockSpec((None, k, tn), lambda q, j: (q, 0, j)),
               (Q, N // tn), "tn")


def _silu(v):
    return v * jax.nn.sigmoid(v)


def _ada_fwd(c16, ada_w):
    L, D, n = ada_w.shape

    def body(c_ref, w_ref, o_ref):
        o_ref[...] = _dot(_silu(c_ref[...]), w_ref[...], "nn")

    return pl.pallas_call(
        body, name="ada_fwd", grid=(L,), out_shape=jax.ShapeDtypeStruct((L, 16, n), F32),
        in_specs=[pl.BlockSpec((16, D), lambda l: (0, 0)), pl.BlockSpec((None, D, n), lambda l: (l, 0, 0))],
        out_specs=pl.BlockSpec((None, 16, n), lambda l: (l, 0, 0)), compiler_params=_cp("parallel"))(c16, ada_w)


def _ada_bwd(c16, dmod16):
    L, _, n = dmod16.shape
    D = c16.shape[1]

    def body(c_ref, d_ref, o_ref):
        o_ref[...] = _dot(_silu(c_ref[...]), d_ref[...], "tn")

    return pl.pallas_call(
        body, name="ada_bwd", grid=(L,), out_shape=jax.ShapeDtypeStruct((L, D, n), F32),
        in_specs=[pl.BlockSpec((16, D), lambda l: (0, 0)), pl.BlockSpec((None, 16, n), lambda l: (l, 0, 0))],
        out_specs=pl.BlockSpec((None, D, n), lambda l: (l, 0, 0)), compiler_params=_cp("parallel"))(c16, dmod16)


def _row_spec(tr, n):
    return pl.BlockSpec((tr, n), lambda i: (i, 0))


def _vec_spec(n):
    return pl.BlockSpec((1, n), lambda i: (0, 0))


def _rmsmod_fwd(name, x, g, sc, sh, after, tr=256):
    S, D = x.shape

    def body(x_ref, g_ref, sc_ref, sh_ref, after_ref, h_ref):
        xv = x_ref[...]
        rstd = lax.rsqrt(jnp.mean(xv * xv, axis=-1, keepdims=True) + EPS)
        y = xv * rstd * g_ref[...]
        h_ref[...] = (y * (1.0 + sc_ref[...]) + sh_ref[...]).astype(h_ref.dtype)

    return pl.pallas_call(
        body, name=name, grid=(S // tr,), out_shape=jax.ShapeDtypeStruct((S, D), _MXU_DTYPE),
        in_specs=[_row_spec(tr, D), _vec_spec(D), _vec_spec(D), _vec_spec(D), ANY], out_specs=_row_spec(tr, D),
        compiler_params=_cp("parallel"))(x, g, sc, sh, after)


def _acc_rows(ref, val, first):
    s = jnp.sum(val, axis=0, keepdims=True)

    @pl.when(first)
    def _():
        ref[...] = s

    @pl.when(jnp.logical_not(first))
    def _():
        ref[...] += s


def _rmsmod_bwd(name, x, g, sc, dh, dres, after, tr=256):
    S, D = x.shape

    def body(x_ref, g_ref, sc_ref, dh_ref, dres_ref, after_ref, dx_ref, dg_ref, dsc_ref, dsh_ref):
        first = pl.program_id(0) == 0
        xv, dh_v, gv = x_ref[...], dh_ref[...], g_ref[...]
        rstd = lax.rsqrt(jnp.mean(xv * xv, axis=-1, keepdims=True) + EPS)
        xhat = xv * rstd
        _acc_rows(dsh_ref, dh_v, first)
        _acc_rows(dsc_ref, dh_v * (xhat * gv), first)
        dyg = dh_v * (1.0 + sc_ref[...])
        _acc_rows(dg_ref, dyg * xhat, first)
        dxhat = dyg * gv
        dx_ref[...] = dres_ref[...] + rstd * (dxhat - xhat * jnp.mean(dxhat * xhat, axis=-1, keepdims=True))

    vec = jax.ShapeDtypeStruct((1, D), F32)
    return pl.pallas_call(
        body, name=name, grid=(S // tr,), out_shape=(jax.ShapeDtypeStruct((S, D), F32), vec, vec, vec),
        in_specs=[_row_spec(tr, D), _vec_spec(D), _vec_spec(D), _row_spec(tr, D), _row_spec(tr, D), ANY],
        out_specs=(_row_spec(tr, D), _vec_spec(D), _vec_spec(D), _vec_spec(D)),
        compiler_params=_cp("arbitrary"))(x, g, sc, dh, dres, after)


def _loss_head(x, g, target, tr=256):
    S, D = x.shape

    def body(x_ref, g_ref, t_ref, loss_ref, dx_ref, dg_ref):
        first = pl.program_id(0) == 0
        xv, gv = x_ref[...], g_ref[...]
        rstd = lax.rsqrt(jnp.mean(xv * xv, axis=-1, keepdims=True) + EPS)
        xhat = xv * rstd
        err = xhat * gv - t_ref[...]
        part = 0.5 * jnp.sum(jnp.mean(err * err, axis=-1, keepdims=True), axis=0, keepdims=True)

        @pl.when(first)
        def _():
            loss_ref[...] = part

        @pl.when(jnp.logical_not(first))
        def _():
            loss_ref[...] += part

        dout = err * (1.0 / D)
        _acc_rows(dg_ref, dout * xhat, first)
        dxhat = dout * gv
        dx_ref[...] = rstd * (dxhat - xhat * jnp.mean(dxhat * xhat, axis=-1, keepdims=True))

    return pl.pallas_call(
        body, name="loss_head", grid=(S // tr,),
        out_shape=(jax.ShapeDtypeStruct((1, 1), F32), jax.ShapeDtypeStruct((S, D), F32), jax.ShapeDtypeStruct((1, D), F32)),
        in_specs=[_row_spec(tr, D), _vec_spec(D), _row_spec(tr, D)],
        out_specs=(pl.BlockSpec((1, 1), lambda i: (0, 0)), _row_spec(tr, D), _vec_spec(D)),
        compiler_params=_cp("arbitrary"))(x, g, target)


def _gate_bwd(name, dx, y, gate, tr=256):
    S, D = dx.shape

    def body(dx_ref, y_ref, g_ref, dy_ref, dg_ref):
        dxv = dx_ref[...]
        dy_ref[...] = (g_ref[...] * dxv).astype(dy_ref.dtype)
        _acc_rows(dg_ref, dxv * y_ref[...], pl.program_id(0) == 0)

    return pl.pallas_call(
        body, name=name, grid=(S // tr,),
        out_shape=(jax.ShapeDtypeStruct((S, D), _MXU_DTYPE), jax.ShapeDtypeStruct((1, D), F32)),
        in_specs=[_row_spec(tr, D), _row_spec(tr, D), _vec_spec(D)], out_specs=(_row_spec(tr, D), _vec_spec(D)),
        compiler_params=_cp("arbitrary"))(dx, y, gate)


def _shift_down(v, k):
    t = lax.broadcasted_iota(jnp.int32, v.shape, 0)
    return jnp.where(t >= k, pltpu.roll(v, k, axis=0), 0.0)


def _shift_up(v, k):
    n = v.shape[0]
    t = lax.broadcasted_iota(jnp.int32, v.shape, 0)
    return jnp.where(t < n - k, pltpu.roll(v, n - k, axis=0), 0.0)


def _window_sum(p, w, shift):
    s, k = p, 1
    while k < w:
        s = s + shift(s, k)
        k *= 2
    return s


def _pool_count(shape, w):
    t = lax.broadcasted_iota(jnp.int32, shape, 0)
    return jnp.minimum(t + 1, w).astype(F32)


def _ab_specs(S):
    zs = [pl.BlockSpec((None, S, 128), functools.partial(lambda g, q: (2 * q + g // 2, 0, g % 2), q=q)) for q in range(4)]
    return zs


def _ab_mix_fwd(z8, conv_w, mix_w, scale):
    S = z8.shape[1]

    def body(b_ref, c_ref, a_ref, p_ref, w_ref, mix_ref, sc_ref, y_ref):
        g = pl.program_id(0)
        cg = c_ref[...] * a_ref[...]
        w = w_ref[...]
        conv = w[0:1] * _shift_down(cg, 2) + w[1:2] * _shift_down(cg, 1) + w[2:3] * cg
        y_ref[0] = (b_ref[...] * conv).astype(y_ref.dtype)
        for gg, win in enumerate(POOL_WINDOWS):
            @pl.when(g == gg)
            def _(win=win):
                p = p_ref[...]
                pooled = _window_sum(p, win, _shift_down) / _pool_count(p.shape, win) - p
                y_ref[1] = (_dot(pooled, mix_ref[...], "nn") * sc_ref[...]).astype(y_ref.dtype)

    return pl.pallas_call(
        body, name="ab_mix_fwd", grid=(4,), out_shape=jax.ShapeDtypeStruct((2, S, 512), _MXU_DTYPE),
        in_specs=_ab_specs(S) + [pl.BlockSpec((3, 128), lambda g: (0, g)), pl.BlockSpec((None, 128, 128), lambda g: (g, 0, 0)),
                                 pl.BlockSpec((1, 128), lambda g: (0, g))],
        out_specs=pl.BlockSpec((2, S, 128), lambda g: (0, 0, g)), compiler_params=_cp("parallel"))(z8, z8, z8, z8, conv_w, mix_w, scale)


def _ab_mix_bwd(z8, dycat2, conv_w, mix_w, scale, after):
    S = z8.shape[1]

    def body(b_ref, c_ref, a_ref, p_ref, dy_ref, w_ref, mix_ref, sc_ref, after_ref, dz_ref, dw_ref, dmix_ref, dsc_ref):
        g = pl.program_id(0)
        bv, cv, av, w = b_ref[...], c_ref[...], a_ref[...], w_ref[...]
        dya = dy_ref[0]
        cg = cv * av
        cg1, cg2 = _shift_down(cg, 1), _shift_down(cg, 2)
        conv = w[0:1] * cg2 + w[1:2] * cg1 + w[2:3] * cg
        dz_ref[0] = (dya * conv).astype(dz_ref.dtype)
        dconv = dya * bv
        dcg = w[2:3] * dconv + w[1:2] * _shift_up(dconv, 1) + w[0:1] * _shift_up(dconv, 2)
        dz_ref[1] = (dcg * av).astype(dz_ref.dtype)
        dz_ref[2] = (dcg * cv).astype(dz_ref.dtype)
        dw_ref[0:1, :] = jnp.sum(dconv * cg2, axis=0, keepdims=True)
        dw_ref[1:2, :] = jnp.sum(dconv * cg1, axis=0, keepdims=True)
        dw_ref[2:3, :] = jnp.sum(dconv * cg, axis=0, keepdims=True)
        for gg, win in enumerate(POOL_WINDOWS):
            @pl.when(g == gg)
            def _(win=win):
                p, dyb, mix = p_ref[...], dy_ref[1], mix_ref[...]
                cnt = _pool_count(p.shape, win)
                pooled = _window_sum(p, win, _shift_down) / cnt - p
                dsc_ref[...] = jnp.sum(dyb * _dot(pooled, mix, "nn"), axis=0, keepdims=True)
                dmixed = dyb * sc_ref[...]
                dmix_ref[...] = _dot(pooled, dmixed, "tn")
                dpooled = _dot(dmixed, mix, "nt")
                dz_ref[3] = (_window_sum(dpooled / cnt, win, _shift_up) - dpooled).astype(dz_ref.dtype)

    return pl.pallas_call(
        body, name="ab_mix_bwd", grid=(4,),
        out_shape=(jax.ShapeDtypeStruct((4, 2, S, 256), _MXU_DTYPE), jax.ShapeDtypeStruct((3, 512), F32),
                   jax.ShapeDtypeStruct((4, 128, 128), F32), jax.ShapeDtypeStruct((1, 512), F32)),
        in_specs=_ab_specs(S) + [pl.BlockSpec((2, S, 128), lambda g: (0, 0, g)), pl.BlockSpec((3, 128), lambda g: (0, g)),
                                 pl.BlockSpec((None, 128, 128), lambda g: (g, 0, 0)), pl.BlockSpec((1, 128), lambda g: (0, g)), ANY],
        out_specs=(pl.BlockSpec((4, None, S, 128), lambda g: (0, g // 2, 0, g % 2)), pl.BlockSpec((3, 128), lambda g: (0, g)),
                   pl.BlockSpec((None, 128, 128), lambda g: (g, 0, 0)), pl.BlockSpec((1, 128), lambda g: (0, g))),
        compiler_params=_cp("parallel"))(z8, z8, z8, z8, dycat2, conv_w, mix_w, scale, after)


HALO = 16


def _ffn_specs(S, n, tr):
    nb = S // HALO
    tile = pl.BlockSpec((2, None, tr, n), lambda j, i: (0, j, i, 0))
    prev = pl.BlockSpec((2, None, HALO, n), lambda j, i: (0, j, jnp.maximum(i * (tr // HALO) - 1, 0), 0))
    nxt = pl.BlockSpec((2, None, HALO, n), lambda j, i: (0, j, jnp.minimum((i + 1) * (tr // HALO), nb - 1), 0))
    cw = pl.BlockSpec((2, None, 3, n), lambda j, i: (0, j, 0, 0))
    return tile, prev, nxt, cw


def _shifted_rows(ext, lo, rows):
    ext = ext.astype(F32)
    return pltpu.roll(ext, 1, axis=0)[lo:lo + rows], pltpu.roll(ext, 2, axis=0)[lo:lo + rows]


def _ffn_gate_fwd(name, u24, cw24, tr=256):
    _, J, S, n = u24.shape
    tile, prev, _, cw = _ffn_specs(S, n, tr)

    def body(u_ref, up_ref, w_ref, a_ref):
        keep = (pl.program_id(1) > 0).astype(u_ref.dtype)
        z = []
        for h in range(2):
            ext = jnp.concatenate([up_ref[h] * keep, u_ref[h]], axis=0)
            x1, x2 = _shifted_rows(ext, HALO, tr)
            w = w_ref[h]
            z.append(w[0:1] * x2 + w[1:2] * x1 + w[2:3] * u_ref[h].astype(F32))
        a_ref[...] = (_silu(z[0]) * z[1]).astype(a_ref.dtype)

    return pl.pallas_call(
        body, name=name, grid=(J, S // tr), out_shape=jax.ShapeDtypeStruct((J, S, n), _MXU_DTYPE),
        in_specs=[tile, prev, cw], out_specs=pl.BlockSpec((None, tr, n), lambda j, i: (j, i, 0)),
        compiler_params=_cp("parallel", "parallel"))(u24, u24, cw24)


def _ffn_gate_bwd(name, u24, cw24, da4, after, tr=256):
    _, J, S, n = u24.shape
    tile, prev, nxt, cw = _ffn_specs(S, n, tr)
    nb = S // HALO
    ext_rows = tr + 2 * HALO

    def body(u_ref, up_ref, un_ref, w_ref, da_ref, dan_ref, after_ref, du_ref, dcw_ref):
        i = pl.program_id(1)
        first = i == 0
        keep_prev = (i > 0).astype(u_ref.dtype)
        keep_next = (i < S // tr - 1).astype(F32)
        w = [w_ref[h] for h in range(2)]
        m = tr + HALO
        xs, z = [], []
        for h in range(2):
            ext = jnp.concatenate([up_ref[h] * keep_prev, u_ref[h], un_ref[h]], axis=0)
            x1, x2 = _shifted_rows(ext, HALO, m)
            x0 = ext[HALO:HALO + m].astype(F32)
            xs.append((x2, x1, x0))
            z.append(w[h][0:1] * x2 + w[h][1:2] * x1 + w[h][2:3] * x0)
        zg, zu = z
        da = jnp.concatenate([da_ref[...].astype(F32), dan_ref[...].astype(F32) * keep_next], axis=0)
        sg = jax.nn.sigmoid(zg)
        dz = [da * zu * (sg * (1.0 + zg * (1.0 - sg))), da * (zg * sg)]
        for h in range(2):
            d = dz[h]
            du = w[h][2:3] * d[:tr] + w[h][1:2] * pltpu.roll(d, m - 1, axis=0)[:tr] + w[h][0:1] * pltpu.roll(d, m - 2, axis=0)[:tr]
            du_ref[h] = du.astype(du_ref.dtype)
            dt = d[:tr]
            parts = [jnp.sum(dt * xk[:tr], axis=0, keepdims=True) for xk in xs[h]]
            for k in range(3):
                @pl.when(first)
                def _(k=k, h=h):
                    dcw_ref[h, k:k + 1, :] = parts[k]

                @pl.when(jnp.logical_not(first))
                def _(k=k, h=h):
                    dcw_ref[h, k:k + 1, :] += parts[k]

    da_tile = pl.BlockSpec((None, tr, n), lambda j, i: (j, i, 0))
    da_next = pl.BlockSpec((None, HALO, n), lambda j, i: (j, jnp.minimum((i + 1) * (tr // HALO), nb - 1), 0))
    return pl.pallas_call(
        body, name=name, grid=(J, S // tr),
        out_shape=(jax.ShapeDtypeStruct((2, J, S, n), _MXU_DTYPE), jax.ShapeDtypeStruct((2, J, 3, n), F32)),
        in_specs=[tile, prev, nxt, cw, da_tile, da_next, ANY], out_specs=(tile, cw),
        compiler_params=_cp("parallel", "arbitrary"))(u24, u24, u24, cw24, da4, da4, after)


def _rms_rows(v, g):
    rstd = lax.rsqrt(jnp.mean(v * v, axis=-1, keepdims=True) + EPS)
    return v * rstd * g


def _rms_rows_bwd(v, g, dy):
    rstd = lax.rsqrt(jnp.mean(v * v, axis=-1, keepdims=True) + EPS)
    vhat = v * rstd
    dvhat = dy * g
    return rstd * (dvhat - vhat * jnp.mean(dvhat * vhat, axis=-1, keepdims=True)), dy * vhat


def _mla_prep_fwd(z, qg, kvg, tr=256):
    S = z.shape[0]

    def body(q_ref, kv_ref, qg_ref, kvg_ref, qn_ref, kvn_ref):
        qn_ref[...] = _rms_rows(q_ref[...], qg_ref[...]).astype(qn_ref.dtype)
        kvn_ref[...] = _rms_rows(kv_ref[...], kvg_ref[...]).astype(kvn_ref.dtype)

    return pl.pallas_call(
        body, name="mla_prep_fwd", grid=(S // tr,),
        out_shape=(jax.ShapeDtypeStruct((S, 256), _MXU_DTYPE), jax.ShapeDtypeStruct((S, 128), _MXU_DTYPE)),
        in_specs=[pl.BlockSpec((tr, 256), lambda i: (i, 0)), pl.BlockSpec((tr, 128), lambda i: (i, 2)), _vec_spec(256), _vec_spec(128)],
        out_specs=(_row_spec(tr, 256), _row_spec(tr, 128)), compiler_params=_cp("parallel"))(z, z, qg, kvg)


def _mla_prep_bwd(z, qg, kvg, dqn, dkvn, dkpe, duv, tr=256):
    S = z.shape[0]

    def body(q_ref, kv_ref, qg_ref, kvg_ref, dqn_ref, dkvn_ref, dkpe_ref, duv_ref, dz_ref, dqg_ref, dkvg_ref):
        first = pl.program_id(0) == 0
        dq, dqg = _rms_rows_bwd(q_ref[...], qg_ref[...], dqn_ref[...])
        dkv, dkvg = _rms_rows_bwd(kv_ref[...], kvg_ref[...], dkvn_ref[...])
        _acc_rows(dqg_ref, dqg, first)
        _acc_rows(dkvg_ref, dkvg, first)
        dz_ref[:, 0:256] = dq.astype(dz_ref.dtype)
        dz_ref[:, 256:384] = dkv.astype(dz_ref.dtype)
        dz_ref[:, 384:512] = dkpe_ref[...].astype(dz_ref.dtype)
        dz_ref[:, 512:1536] = duv_ref[...].astype(dz_ref.dtype)

    return pl.pallas_call(
        body, name="mla_prep_bwd", grid=(S // tr,),
        out_shape=(jax.ShapeDtypeStruct((S, 1536), _MXU_DTYPE), jax.ShapeDtypeStruct((1, 256), F32), jax.ShapeDtypeStruct((1, 128), F32)),
        in_specs=[pl.BlockSpec((tr, 256), lambda i: (i, 0)), pl.BlockSpec((tr, 128), lambda i: (i, 2)), _vec_spec(256), _vec_spec(128),
                  _row_spec(tr, 256), _row_spec(tr, 128), _row_spec(tr, 128), _row_spec(tr, 1024)],
        out_specs=(_row_spec(tr, 1536), _vec_spec(256), _vec_spec(128)),
        compiler_params=_cp("arbitrary"))(z, z, qg, kvg, dqn, dkvn, dkpe, duv)


def _rope(v, cos, sa, sb):
    return v * cos + pltpu.roll(v, 112, axis=1) * sa + pltpu.roll(v, 16, axis=1) * sb


def _rope_t(d, cos, sa, sb):
    return d * cos + pltpu.roll(d * sa, 16, axis=1) + pltpu.roll(d * sb, 112, axis=1)


def _rope_fwd(qraw, kvall, z, cosq, cosk, sa, sb, tr=256):
    S = qraw.shape[0]

    def body(q_ref, k_ref, v_ref, kpe_ref, cq_ref, ck_ref, sa_ref, sb_ref, qo_ref, ko_ref, vo_ref):
        cq, ck, sa_v, sb_v = cq_ref[...], ck_ref[...], sa_ref[...], sb_ref[...]
        kpe = _rope(kpe_ref[...], ck, sa_v, sb_v)
        for h in range(8):
            cols = slice(128 * h, 128 * h + 128)
            qo_ref[:, cols] = _rope(q_ref[:, cols], cq, sa_v, sb_v).astype(qo_ref.dtype)
            ko_ref[:, cols] = (k_ref[:, cols] + kpe).astype(ko_ref.dtype)
        vo_ref[...] = v_ref[...].astype(vo_ref.dtype)

    tab = _row_spec(tr, 128)
    return pl.pallas_call(
        body, name="rope_fwd", grid=(S // tr,),
        out_shape=(jax.ShapeDtypeStruct((S, 1024), _MXU_DTYPE), jax.ShapeDtypeStruct((S, 1024), _MXU_DTYPE),
                   jax.ShapeDtypeStruct((S, 512), _MXU_DTYPE)),
        in_specs=[_row_spec(tr, 1024), pl.BlockSpec((tr, 1024), lambda i: (i, 0)), pl.BlockSpec((tr, 512), lambda i: (i, 2)),
                  pl.BlockSpec((tr, 128), lambda i: (i, 3)), tab, tab, tab, tab],
        out_specs=(_row_spec(tr, 1024), _row_spec(tr, 1024), _row_spec(tr, 512)),
        compiler_params=_cp("parallel"))(qraw, kvall, kvall, z, cosq, cosk, sa, sb)


def _rope_bwd(dq, dk, dv, cosq, cosk, sa, sb, tr=256):
    S = dq.shape[0]

    def body(dq_ref, dk_ref, dv_ref, cq_ref, ck_ref, sa_ref, sb_ref, dqo_ref, dkv_ref, dkpe_ref):
        cq, ck, sa_v, sb_v = cq_ref[...], ck_ref[...], sa_ref[...], sb_ref[...]
        tot = jnp.zeros((tr, 128), F32)
        for h in range(8):
            cols = slice(128 * h, 128 * h + 128)
            dqo_ref[:, cols] = _rope_t(dq_ref[:, cols], cq, sa_v, sb_v).astype(dqo_ref.dtype)
            dkh = dk_ref[:, cols]
            tot = tot + dkh
            dkv_ref[:, cols] = dkh.astype(dkv_ref.dtype)
        dkv_ref[:, 1024:1536] = dv_ref[...].astype(dkv_ref.dtype)
        dkpe_ref[...] = _rope_t(tot, ck, sa_v, sb_v)

    tab = _row_spec(tr, 128)
    return pl.pallas_call(
        body, name="rope_bwd", grid=(S // tr,),
        out_shape=(jax.ShapeDtypeStruct((S, 1024), _MXU_DTYPE), jax.ShapeDtypeStruct((S, 1536), _MXU_DTYPE),
                   jax.ShapeDtypeStruct((S, 128), F32)),
        in_specs=[_row_spec(tr, 1024), _row_spec(tr, 1024), _row_spec(tr, 512), tab, tab, tab, tab],
        out_specs=(_row_spec(tr, 1024), _row_spec(tr, 1536), _row_spec(tr, 128)),
        compiler_params=_cp("parallel"))(dq, dk, dv, cosq, cosk, sa, sb)


NEG = -1e30


def _attn_fwd(q, k, v, tq=256, tk=256):
    S = q.shape[0]
    assert tq == tk

    def body(q_ref, k_ref, v_ref, o_ref, lse_ref):
        i = pl.program_id(1)
        qs = [q_ref[:, 0:128], q_ref[:, 128:256]]

        def step(kb, carry, diagonal=False):
            start = pl.multiple_of(kb * tk, tk)
            vv = v_ref[pl.ds(start, tk), :]
            out = []
            for h in range(2):
                m, l, acc = carry[3 * h:3 * h + 3]
                s = _dot(qs[h], k_ref[pl.ds(start, tk), 128 * h:128 * h + 128], "nt") * ATTN_SCALE
                if diagonal:
                    s = jnp.where(below, s, NEG)
                m_new = jnp.maximum(m, jnp.max(s, axis=-1, keepdims=True))
                alpha = jnp.exp(m - m_new)
                p = jnp.exp(s - m_new)
                out += [m_new, alpha * l + jnp.sum(p, axis=-1, keepdims=True), alpha * acc + _dot(p, vv, "nn")]
            return tuple(out)

        below = lax.broadcasted_iota(jnp.int32, (tq, tk), 1) <= lax.broadcasted_iota(jnp.int32, (tq, tk), 0)
        init = (jnp.full((tq, 1), NEG, F32), jnp.zeros((tq, 1), F32), jnp.zeros((tq, 128), F32)) * 2
        ma, la, acca, mb, lb, accb = step(i, lax.fori_loop(0, i, step, init), diagonal=True)
        lane = lax.broadcasted_iota(jnp.int32, (tq, 128), 1)
        o_ref[...] = jnp.where(lane < 64, acca / la, accb / lb)
        lse_ref[...] = jnp.where(lane < 64, ma + jnp.log(la), mb + jnp.log(lb))

    return pl.pallas_call(
        body, name="attn_fwd", grid=(4, S // tq),
        out_shape=(jax.ShapeDtypeStruct((S, 512), F32), jax.ShapeDtypeStruct((4, S, 128), F32)),
        in_specs=[pl.BlockSpec((tq, 256), lambda p, i: (i, p)), pl.BlockSpec((S, 256), lambda p, i: (0, p)),
                  pl.BlockSpec((S, 128), lambda p, i: (0, p))],
        out_specs=(pl.BlockSpec((tq, 128), lambda p, i: (i, p)), pl.BlockSpec((None, tq, 128), lambda p, i: (p, i, 0))),
        compiler_params=_cp("parallel", "parallel"))(q, k, v)


def _attn_bwd(q, k, v, o, lse, dycat2, tq=256, tk=256):
    S = q.shape[0]
    assert tq == tk

    def body(q_ref, k_ref, v_ref, o_ref, lse_ref, do_ref, dq_ref, dk_ref, dv_ref):
        j = pl.program_id(1)

        @pl.when(j == 0)
        def _():
            dq_ref[...] = jnp.zeros_like(dq_ref)

        below = lax.broadcasted_iota(jnp.int32, (tq, tk), 1) <= lax.broadcasted_iota(jnp.int32, (tq, tk), 0)
        lane = lax.broadcasted_iota(jnp.int32, (tq, 128), 1)
        ks = [k_ref[:, 0:128], k_ref[:, 128:256]]
        vv = v_ref[...]

        def step(qb, carry, diagonal=False):
            dka, dkb, dvp = carry
            start = pl.multiple_of(qb * tq, tq)
            rows = pl.ds(start, tq)
            do, lse_v = do_ref[rows, :], lse_ref[rows, :]
            prod = do * o_ref[rows, :]
            dks = [dka, dkb]
            for h in range(2):
                mine = (lane < 64) if h == 0 else (lane >= 64)
                delta = jnp.sum(jnp.where(mine, prod, 0.0), axis=-1, keepdims=True)
                do_h = jnp.where(mine, do, 0.0)
                qh = q_ref[rows, 128 * h:128 * h + 128]
                s = _dot(qh, ks[h], "nt") * ATTN_SCALE
                p = jnp.exp(s - lse_v[:, 64 * h:64 * h + 1])
                if diagonal:
                    p = jnp.where(below, p, 0.0)
                dvp = dvp + _dot(p, do_h, "tn")
                ds = p * (_dot(do_h, vv, "nt") - delta) * ATTN_SCALE
                dq_ref[rows, 128 * h:128 * h + 128] += _dot(ds, ks[h], "nn")
                dks[h] = dks[h] + _dot(ds, qh, "tn")
            return dks[0], dks[1], dvp

        zero = jnp.zeros((tk, 128), F32)
        dka, dkb, dvp = lax.fori_loop(j + 1, S // tq, step, step(j, (zero, zero, zero), diagonal=True))
        dk_ref[:, 0:128] = dka
        dk_ref[:, 128:256] = dkb
        dv_ref[...] = dvp

    return pl.pallas_call(
        body, name="attn_bwd", grid=(4, S // tk),
        out_shape=(jax.ShapeDtypeStruct((S, 1024), F32), jax.ShapeDtypeStruct((S, 1024), F32), jax.ShapeDtypeStruct((S, 512), F32)),
        in_specs=[pl.BlockSpec((S, 256), lambda p, j: (0, p)), pl.BlockSpec((tk, 256), lambda p, j: (j, p)),
                  pl.BlockSpec((tk, 128), lambda p, j: (j, p)), pl.BlockSpec((S, 128), lambda p, j: (0, p)),
                  pl.BlockSpec((None, S, 128), lambda p, j: (p, 0, 0)), pl.BlockSpec((None, S, 128), lambda p, j: (0, 0, p))],
        out_specs=(pl.BlockSpec((S, 256), lambda p, j: (0, p)), pl.BlockSpec((tk, 256), lambda p, j: (j, p)),
                   pl.BlockSpec((tk, 128), lambda p, j: (j, p))),
        compiler_params=_cp("parallel", "arbitrary"))(q, k, v, o, lse, dycat2)


CHUNK = 128
GELU_C = math.sqrt(2.0 / math.pi)


def _gelu(v):
    t = jnp.tanh(GELU_C * (v + 0.044715 * (v * v * v)))
    return v * (0.5 * (1.0 + t)), t


def _gelu_grad(v, t):
    return 0.5 * (1.0 + t) + v * (0.5 * (1.0 - t * t) * GELU_C * (1.0 + 3.0 * 0.044715 * v * v))


def _tril(w):
    r = lax.broadcasted_iota(jnp.int32, w.shape, 0)
    c = lax.broadcasted_iota(jnp.int32, w.shape, 1)
    return jnp.where(c <= r, w, 0.0)


def _layer_norm(v, g, b):
    xc = v - jnp.mean(v, axis=-1, keepdims=True)
    rstd = lax.rsqrt(jnp.mean(xc * xc, axis=-1, keepdims=True) + EPS)
    xhat = xc * rstd
    return xhat * g + b, xhat, rstd


def _sgu_fwd(z, o, ln_g, ln_b, w_s, b_st, tr=256):
    S = z.shape[0]

    def body(u_ref, v_ref, o_ref, g_ref, b_ref, ws_ref, bs_ref, y_ref):
        gu, _ = _gelu(u_ref[...])
        gv, _ = _gelu(v_ref[...])
        vln, _, _ = _layer_norm(gv, g_ref[...], b_ref[...])
        y_ref[0] = o_ref[...].astype(y_ref.dtype)
        for g in range(4):
            wt = _tril(ws_ref[g])
            cols = slice(128 * g, 128 * g + 128)
            for ch in range(tr // CHUNK):
                rows = slice(CHUNK * ch, CHUNK * ch + CHUNK)
                mixed = _dot(wt, vln[rows, cols], "nn") + bs_ref[:, g:g + 1]
                y_ref[1, rows, cols] = (gu[rows, cols] * mixed).astype(y_ref.dtype)

    return pl.pallas_call(
        body, name="sgu_fwd", grid=(S // tr,), out_shape=jax.ShapeDtypeStruct((2, S, 512), _MXU_DTYPE),
        in_specs=[pl.BlockSpec((tr, 512), lambda i: (i, 1)), pl.BlockSpec((tr, 512), lambda i: (i, 2)), _row_spec(tr, 512),
                  _vec_spec(512), _vec_spec(512), pl.BlockSpec((4, 128, 128), lambda i: (0, 0, 0)), pl.BlockSpec((128, 4), lambda i: (0, 0))],
        out_specs=pl.BlockSpec((2, tr, 512), lambda i: (0, i, 0)), compiler_params=_cp("parallel"))(z, z, o, ln_g, ln_b, w_s, b_st)


def _sgu_bwd(z, dycat2, ln_g, ln_b, w_s, b_st, tr=256):
    S = z.shape[0]

    def body(u_ref, v_ref, dy_ref, g_ref, b_ref, ws_ref, bs_ref, duv_ref, dg_ref, db_ref, dws_ref, dbs_ref):
        first = pl.program_id(0) == 0
        u_pre, v_pre = u_ref[...], v_ref[...]
        gu, tu = _gelu(u_pre)
        gv, tv = _gelu(v_pre)
        gain = g_ref[...]
        vln, xhat, rstd = _layer_norm(gv, gain, b_ref[...])

        @pl.when(first)
        def _():
            dws_ref[...] = jnp.zeros_like(dws_ref)
            dbs_ref[...] = jnp.zeros_like(dbs_ref)

        dvln_cols = []
        for g in range(4):
            wt = _tril(ws_ref[g])
            cols = slice(128 * g, 128 * g + 128)
            dmixed_sum = jnp.zeros((CHUNK, 128), F32)
            dw = jnp.zeros((CHUNK, CHUNK), F32)
            dvln_rows = []
            for ch in range(tr // CHUNK):
                rows = slice(CHUNK * ch, CHUNK * ch + CHUNK)
                vt = vln[rows, cols]
                mixed = _dot(wt, vt, "nn") + bs_ref[:, g:g + 1]
                dyd = dy_ref[rows, cols]
                duv_ref[rows, cols] = (dyd * mixed * _gelu_grad(u_pre[rows, cols], tu[rows, cols])).astype(duv_ref.dtype)
                dmixed = dyd * gu[rows, cols]
                dmixed_sum = dmixed_sum + dmixed
                dw = dw + _dot(dmixed, vt, "nt")
                dvln_rows.append(_dot(wt, dmixed, "tn"))
            dws_ref[g] += _tril(dw)
            dbs_ref[g:g + 1, :] += jnp.sum(dmixed_sum.T, axis=0, keepdims=True)
            dvln_cols.append(jnp.concatenate(dvln_rows, axis=0))
        dvln = jnp.concatenate(dvln_cols, axis=1)
        _acc_rows(dg_ref, dvln * xhat, first)
        _acc_rows(db_ref, dvln, first)
        dxhat = dvln * gain
        dgv = rstd * (dxhat - jnp.mean(dxhat, axis=-1, keepdims=True) - xhat * jnp.mean(dxhat * xhat, axis=-1, keepdims=True))
        duv_ref[:, 512:1024] = (dgv * _gelu_grad(v_pre, tv)).astype(duv_ref.dtype)

    return pl.pallas_call(
        body, name="sgu_bwd", grid=(S // tr,),
        out_shape=(jax.ShapeDtypeStruct((S, 1024), _MXU_DTYPE), jax.ShapeDtypeStruct((1, 512), F32), jax.ShapeDtypeStruct((1, 512), F32),
                   jax.ShapeDtypeStruct((4, 128, 128), F32), jax.ShapeDtypeStruct((4, 128), F32)),
        in_specs=[pl.BlockSpec((tr, 512), lambda i: (i, 1)), pl.BlockSpec((tr, 512), lambda i: (i, 2)),
                  pl.BlockSpec((None, tr, 512), lambda i: (1, i, 0)), _vec_spec(512), _vec_spec(512),
                  pl.BlockSpec((4, 128, 128), lambda i: (0, 0, 0)), pl.BlockSpec((128, 4), lambda i: (0, 0))],
        out_specs=(_row_spec(tr, 1024), _vec_spec(512), _vec_spec(512), pl.BlockSpec((4, 128, 128), lambda i: (0, 0, 0)),
                   pl.BlockSpec((4, 128), lambda i: (0, 0))),
        compiler_params=_cp("arbitrary"))(z, z, dycat2, ln_g, ln_b, w_s, b_st)


def _sum_parts(name, parts, tr=512):
    P, R, C = parts.shape
    tr = _tile(R, tr) if R % 8 == 0 else R

    def body(p_ref, o_ref):
        g = p_ref[0]
        for k in range(1, P):
            g = g + p_ref[k]
        o_ref[...] = g

    return pl.pallas_call(
        body, name=name, grid=(R // tr,), out_shape=jax.ShapeDtypeStruct((R, C), F32),
        in_specs=[pl.BlockSpec((P, tr, C), lambda i: (0, i, 0))], out_specs=_row_spec(tr, C),
        compiler_params=_cp("parallel"))(parts)


def _adamw_math(w, m, v, g):
    c1 = 1.0 / (1.0 - ADAM_B1 ** ADAM_STEP)
    c2 = 1.0 / (1.0 - ADAM_B2 ** ADAM_STEP)
    m2 = ADAM_B1 * m + (1.0 - ADAM_B1) * g
    v2 = ADAM_B2 * v + (1.0 - ADAM_B2) * (g * g)
    return -ADAM_LR * ((m2 * c1) / (jnp.sqrt(v2 * c2) + ADAM_EPS) + ADAM_WD * w), m2, v2


def _adamw_small(name, params, parts):
    n = len(params)

    def body(*refs):
        ins, outs = refs[:4 * n], refs[4 * n:]
        for i in range(n):
            w_ref, m_ref, v_ref, p_ref = ins[4 * i:4 * i + 4]
            g = p_ref[0]
            for k in range(1, N_DEV):
                g = g + p_ref[k]
            delta, m2, v2 = _adamw_math(w_ref[...], m_ref[...], v_ref[...], g)
            outs[4 * i][...] = g
            outs[4 * i + 1][...] = delta
            outs[4 * i + 2][...] = m2
            outs[4 * i + 3][...] = v2

    flat = [a for (w, m, v), p in zip(params, parts) for a in (w, m, v, p)]
    out = pl.pallas_call(
        body, name=name, out_shape=[jax.ShapeDtypeStruct(w.shape, F32) for (w, _, _) in params for _ in range(4)],
        compiler_params=pltpu.CompilerParams(vmem_limit_bytes=_VMEM_LIMIT))(*flat)
    return [out[4 * i:4 * i + 4] for i in range(n)]


ADAMW_BLOCK_BYTES = 36 * 2 ** 20


def _adamw(name, w, m, v, parts):
    L, R, C = w.shape
    P = parts[0].shape[0]
    row_bytes = 2 * C * (7 * 4 + P * parts[0].dtype.itemsize)
    tr = R
    if R * row_bytes > ADAMW_BLOCK_BYTES:
        tr = next(t for t in (1024, 512, 256, 128, 64, 32, 16) if R % t == 0 and t * row_bytes <= ADAMW_BLOCK_BYTES)
    nr = R // tr
    c1 = 1.0 / (1.0 - ADAM_B1 ** ADAM_STEP)
    c2 = 1.0 / (1.0 - ADAM_B2 ** ADAM_STEP)

    def body(w_ref, m_ref, v_ref, *rest):
        p_refs, (g_ref, d_ref, mo_ref, vo_ref) = rest[:L], rest[L:]
        for ll in range(L):
            @pl.when(pl.program_id(0) == ll)
            def _(p_ref=p_refs[ll]):
                g = p_ref[0].astype(F32)
                for k in range(1, P):
                    g = g + p_ref[k].astype(F32)
                m2 = ADAM_B1 * m_ref[...] + (1.0 - ADAM_B1) * g
                v2 = ADAM_B2 * v_ref[...] + (1.0 - ADAM_B2) * (g * g)
                g_ref[...] = g
                mo_ref[...] = m2
                vo_ref[...] = v2
                d_ref[...] = -ADAM_LR * ((m2 * c1) / (jnp.sqrt(v2 * c2) + ADAM_EPS) + ADAM_WD * w_ref[...])

    def part_spec(ll):
        return pl.BlockSpec((P, tr, C), lambda l, i: (0, jnp.where(l == ll, i, jnp.where(l < ll, 0, nr - 1)), 0))

    full = pl.BlockSpec((None, tr, C), lambda l, i: (l, i, 0))
    sds = jax.ShapeDtypeStruct((L, R, C), F32)
    return pl.pallas_call(
        body, name=name, grid=(L, nr), out_shape=(sds, sds, sds, sds),
        in_specs=[full] * 3 + [part_spec(ll) for ll in range(L)],
        out_specs=(full,) * 4, compiler_params=_cp("arbitrary", "arbitrary"))(w, m, v, *parts)


def _rope_tables(positions):
    half = 16
    inv_freq = 10000.0 ** (-jnp.arange(half, dtype=F32) / half)
    ang = positions.astype(F32)[:, None] * inv_freq
    cos, sin = jnp.cos(ang), jnp.sin(ang)
    S = positions.shape[0]
    z16, z32, z64 = jnp.zeros((S, 16), F32), jnp.zeros((S, 32), F32), jnp.zeros((S, 64), F32)
    cosk = jnp.concatenate([z64, cos, cos, z32], axis=1)
    cosq = jnp.concatenate([jnp.ones((S, 64), F32), cos, cos, z32], axis=1)
    sa = jnp.concatenate([z64, -sin, z16, z32], axis=1)
    sb = jnp.concatenate([z64, z16, sin, z32], axis=1)
    return cosq, cosk, sa, sb


def _ffn_fwd(l, x, mod, n2g, get_w_up8, cw24, get_w_down4):
    sh, sc, gate = mod
    h = _rmsmod_fwd(f"ffn{l}_norm", x, n2g, sc, sh, n2g)
    w_up8 = get_w_up8(h)
    u8 = _mm_cols(f"ffn{l}_up", h, w_up8, out_dtype=ACT_DTYPE, tm=1024)
    S, n = u8.shape[1], u8.shape[2]
    u24 = u8.reshape(2, 4, S, n)
    a4 = _ffn_gate_fwd(f"ffn{l}_gate", u24, cw24)
    w_down4 = get_w_down4(a4)
    f, x_new = _mm_rows_resid(f"ffn{l}_down", a4, w_down4, x, gate)
    return x_new, (x, h, u24, a4, f), w_up8, w_down4


def _ffn_bwd(l, dx, saved, mod, n2g, w_up8, cw24, w_down4, me):
    sh, sc, gate = mod
    x, h, u24, a4, f = saved
    df, dgate = _gate_bwd(f"ffn{l}_gate_bwd", dx, f, gate)
    da4 = _mm_rows_dx(f"ffn{l}_down_dx", df, w_down4, out_dtype=ACT_DTYPE, tm=2048)
    dw_down4 = _mm_rows_dw(f"ffn{l}_down_dw", a4, df, out_dtype=WIRE_DTYPE)
    sent_down, token = _exchange_start(f"scatter_ffn{l}_down", [dw_down4.reshape(8, 352, dw_down4.shape[2])], True, dgate, me)
    du24, dcw24 = _ffn_gate_bwd(f"ffn{l}_act_bwd", u24, cw24, da4, token)
    du8 = du24.reshape((8,) + du24.shape[2:])
    dw_up8t = _mm_cols_dwt(f"ffn{l}_up_dw", h, du8, out_dtype=WIRE_DTYPE, tk=1024)
    sent_up, token = _exchange_start(f"scatter_ffn{l}_up", [dw_up8t], True, dcw24, me)
    dh = _mm_cols_dx(f"ffn{l}_up_dx", du8, w_up8, tm=1024, jb=4)
    dx_new, dn2g, dsc, dsh = _rmsmod_bwd(f"ffn{l}_norm_bwd", x, n2g, sc, dh, dx, token)
    return dx_new, dict(sent_up=sent_up, sent_down=sent_down, cw24=dcw24, n2g=dn2g, mod=(dsh, dsc, dgate))


def kernel(x, c, positions, ada_w, ada_b, norm1_g, norm2_g, ab_w_in, a_conv_w, b_mix_w, b_scale, ab_w_out, cd_w_in, c_q_norm_g, c_w_uq, c_kv_norm_g, c_w_ukv, d_ln_g, d_ln_b, d_w_s, d_b_s, cd_w_out, ffn_w_up, ffn_conv_w, ffn_w_down, final_norm_g, loss_target, m_ada_w, m_ada_b, m_norm1_g, m_norm2_g, m_ab_w_in, m_a_conv_w, m_b_mix_w, m_b_scale, m_ab_w_out, m_cd_w_in, m_c_q_norm_g, m_c_w_uq, m_c_kv_norm_g, m_c_w_ukv, m_d_ln_g, m_d_ln_b, m_d_w_s, m_d_b_s, m_cd_w_out, m_ffn_w_up, m_ffn_conv_w, m_ffn_w_down, m_final_norm_g, v_ada_w, v_ada_b, v_norm1_g, v_norm2_g, v_ab_w_in, v_a_conv_w, v_b_mix_w, v_b_scale, v_ab_w_out, v_cd_w_in, v_c_q_norm_g, v_c_w_uq, v_c_kv_norm_g, v_c_w_ukv, v_d_ln_g, v_d_ln_b, v_d_w_s, v_d_b_s, v_cd_w_out, v_ffn_w_up, v_ffn_conv_w, v_ffn_w_down, v_final_norm_g):
    S, D = x.shape[1], x.shape[2]
    me = 4 * lax.axis_index("x") + 2 * lax.axis_index("y") + lax.axis_index("c")
    x0, target = x[0], loss_target[0]
    W = _MXU_DTYPE

    small_shapes = [(1024,), (3, 64), (32,), (64,), (64,), (2, 3, 704)]
    (g0,) = _exchange("gather_small", [[_pack([c, a_conv_w, c_q_norm_g, d_ln_g, d_ln_b, ffn_conv_w])]], scatter=False)
    c_all, aconv_s, qg_s, lng_s, lnb_s, fcw_s = _unpack(g0[:, 0], small_shapes, lead=(N_DEV,))
    conv_w = aconv_s.transpose(1, 0, 2).reshape(3, 512)
    qg, ln_g, ln_b = qg_s.reshape(1, 256), lng_s.reshape(1, 512), lnb_s.reshape(1, 512)
    cw24 = [fcw_s[:, l].reshape(2, 4, 3, 704) for l in range(2)]
    c16 = jnp.pad(c_all, ((0, 16 - N_DEV), (0, 0)))

    mod_cols = _ada_fwd(c16, ada_w)
    (g1,) = _exchange("gather_mod", [[_pack([mod_cols])]], scatter=False)
    mod_all = _unpack(g1[:, 0], [(2, 16, 768)], lead=(N_DEV,))[0]
    mod_mine = lax.dynamic_index_in_dim(mod_all, me, axis=2, keepdims=False)
    mod = mod_mine.transpose(1, 0, 2).reshape(2, 6 * D) + ada_b
    mods = [[mod[l, k * D:(k + 1) * D].reshape(1, D) for k in range(6)] for l in range(2)]

    gw_ab, token = _hier_gather_start("gather_w_ab", [ab_w_in[0].astype(W), ab_w_out[0].astype(W)], mod, me)
    gw_up0, token = _hier_gather_start("gather_w_ffn0_up", [ffn_w_up[0].astype(W)], token, me)
    gw_dn0, token = _exchange_start("gather_w_ffn0_down", [ffn_w_down[0].astype(W)], False, token, me)
    gw_cd, token = _exchange_start("gather_w_cd", [
        cd_w_in[0].T.astype(W), c_w_uq[0].T.astype(W), c_w_ukv[0].astype(W),
        cd_w_out[0].astype(W)], False, token, me)
    gw_up1, token = _exchange_start("gather_w_ffn1_up", [ffn_w_up[1].astype(W)], False, token, me)
    gw_dn1, started = _exchange_start("gather_w_ffn1_down", [ffn_w_down[1].astype(W)], False, token, me)

    cosq, cosk, sa, sb = _rope_tables(positions[0])
    n1g = [norm1_g[l].reshape(1, D) for l in range(2)]
    n2g = [norm2_g[l].reshape(1, D) for l in range(2)]
    mix_w, scale = b_mix_w[0], b_scale
    kvg = c_kv_norm_g
    w_s, b_st = d_w_s[0], d_b_s[0].T

    sh1, sc1, g1m = mods[0][:3]
    h_ab = _rmsmod_fwd("ab_norm", x0, n1g[0], sc1, sh1, started)
    w_abin8, w_about = _hier_gather_wait("wait_w_ab", _hier_gather_forward("forward_w_ab", gw_ab, h_ab), h_ab)
    w_about2 = w_about.reshape(2, 512, D)
    z8 = _mm_cols("ab_in", h_ab, w_abin8, tm=2048)
    ycat_ab = _ab_mix_fwd(z8, conv_w, mix_w, scale)
    y_ab, x1 = _mm_rows_resid("ab_out", ycat_ab, w_about2, x0, g1m)
    w_up8, w_down4 = [None, None], [None, None]
    gw_up0 = _hier_gather_forward("forward_w_ffn0_up", gw_up0, x1)
    x2, ffn0_saved, w_up8[0], w_down4[0] = _ffn_fwd(
        0, x1, mods[0][3:], n2g[0], lambda after: _hier_gather_wait("wait_w_ffn0_up", gw_up0, after)[0], cw24[0],
        lambda after: _exchange_wait("wait_w_ffn0_down", gw_dn0, after)[0].reshape(4, 704, D))

    w_cdin, w_uq, w_ukv, w_cdout = _exchange_wait("wait_w_cd", gw_cd, x2)
    w_cdout2 = w_cdout.reshape(2, 512, D)
    w_cd_t = w_cdin.reshape(1440, D)
    zr = lambda n: jnp.zeros((n, D), W)
    w_cd_pad = jnp.concatenate([w_cd_t[:384], zr(64), w_cd_t[384:416], zr(32), w_cd_t[416:]], axis=0)
    w_uq_pad = jnp.pad(w_uq, ((0, 0), (0, 32), (0, 0))).reshape(1024, 256)
    w_ukv_h = w_ukv.transpose(1, 0, 2)
    w_k_pad = jnp.pad(w_ukv_h[:, :, :64], ((0, 0), (0, 0), (0, 64))).reshape(128, 1024)
    w_kv_pad = jnp.concatenate([w_k_pad, w_ukv_h[:, :, 64:].reshape(128, 512)], axis=1)

    sh1, sc1, g1c = mods[1][:3]
    h_cd = _rmsmod_fwd("cd_norm", x2, n1g[1], sc1, sh1, n1g[1])
    z_cd = _mm_nt("cd_in", h_cd, w_cd_pad)
    qn, kvn = _mla_prep_fwd(z_cd, qg, kvg)
    qraw = _mm_nt("cd_uq", qn, w_uq_pad)
    kvall = _mm_nn("cd_ukv", kvn, w_kv_pad)
    q_r, k_r, v_r = _rope_fwd(qraw, kvall, z_cd, cosq, cosk, sa, sb)
    o, lse = _attn_fwd(q_r, k_r, v_r)
    ycat_cd = _sgu_fwd(z_cd, o, ln_g, ln_b, w_s, b_st)
    y_cd, x3 = _mm_rows_resid("cd_out", ycat_cd, w_cdout2, x2, g1c)
    x4, ffn1_saved, w_up8[1], w_down4[1] = _ffn_fwd(
        1, x3, mods[1][3:], n2g[1], lambda after: _exchange_wait("wait_w_ffn1_up", gw_up1, after)[0], cw24[1],
        lambda after: _exchange_wait("wait_w_ffn1_down", gw_dn1, after)[0].reshape(4, 704, D))

    loss_local, dx4, dfg = _loss_head(x4, final_norm_g.reshape(1, D), target)

    dx3, gf1 = _ffn_bwd(1, dx4, ffn1_saved, mods[1][3:], n2g[1], w_up8[1], cw24[1], w_down4[1], me)

    dy, dg1c = _gate_bwd("cd_gate_bwd", dx3, y_cd, g1c)
    dycat = _mm_rows_dx("cd_out_dx", dy, w_cdout2)
    dw_cdout = _mm_rows_dw("cd_out_dw", ycat_cd, dy, out_dtype=WIRE_DTYPE)
    duv, dln_g, dln_b, dws, dbs = _sgu_bwd(z_cd, dycat, ln_g, ln_b, w_s, b_st)
    dq_r, dk_r, dv_r = _attn_bwd(q_r, k_r, v_r, o, lse, dycat)
    dqraw, dkvall, dkpe = _rope_bwd(dq_r, dk_r, dv_r, cosq, cosk, sa, sb)
    dqn = _mm_nn("cd_uq_dx", dqraw, w_uq_pad, tn=256)
    dkvn = _mm_nt("cd_ukv_dx", dkvall, w_kv_pad, tn=128)
    dw_uq_pad = _mm_tn("cd_uq_dw", dqraw, qn, tn=256)
    dw_kv_pad = _mm_tn("cd_ukv_dw", kvn, dkvall, tm=128)
    dz_cd, dqg, dkvg = _mla_prep_bwd(z_cd, qg, kvg, dqn, dkvn, dkpe, duv)
    dh_cd = _mm_nn("cd_in_dx", dz_cd, w_cd_pad)
    dw_cd_pad = _mm_tn("cd_in_dw", dz_cd, h_cd)
    dw_cd8 = jnp.concatenate([dw_cd_pad[:384], dw_cd_pad[448:480], dw_cd_pad[512:]], axis=0).astype(WIRE_DTYPE).reshape(8, 180, D)
    dw_uq8 = dw_uq_pad.reshape(8, 128, 256)[:, :96].astype(WIRE_DTYPE)
    dw_ukv8 = jnp.concatenate([dw_kv_pad[:, :1024].reshape(128, 8, 128)[:, :, :64], dw_kv_pad[:, 1024:].reshape(128, 8, 64)],
                              axis=2).transpose(1, 0, 2).astype(WIRE_DTYPE)
    sent_cd, token = _exchange_start("scatter_cd", [dw_cd8, dw_uq8, dw_ukv8, dw_cdout.reshape(8, 128, D)], True, dqg, me)
    early_names = ["c_kv_norm_g", "d_w_s", "d_b_s", "final_norm_g", "c_q_norm_g", "d_ln_g", "d_ln_b"]
    early_grads = [dkvg, dws.reshape(512, 128), dbs, dfg, dqg.reshape(8, 1, 32), dln_g.reshape(8, 1, 64), dln_b.reshape(8, 1, 64)]
    early_sent, token = _exchange_start("gather_small_grads_early", early_grads, [False] * 4 + [True] * 3, token, me)
    dx2, dn1g_cd, dsc1_cd, dsh1_cd = _rmsmod_bwd("cd_norm_bwd", x2, n1g[1], sc1, dh_cd, dx3, token)

    dx1, gf0 = _ffn_bwd(0, dx2, ffn0_saved, mods[0][3:], n2g[0], w_up8[0], cw24[0], w_down4[0], me)

    dy, dg1m = _gate_bwd("ab_gate_bwd", dx1, y_ab, g1m)
    dw_about = _mm_rows_dw("ab_out_dw", ycat_ab, dy, out_dtype=WIRE_DTYPE)
    sent_about, token = _exchange_start("scatter_ab_out", [dw_about.reshape(8, 128, D)], True, dg1m, me)
    dycat = _mm_rows_dx("ab_out_dx", dy, w_about2)
    dz8, dconv_w, dmix_w, dscale = _ab_mix_bwd(z8, dycat, conv_w, mix_w, scale, token)
    dz8 = dz8.reshape(8, S, 256)
    dw_abin8 = _mm_cols_dw("ab_in_dw", h_ab, dz8, out_dtype=WIRE_DTYPE, tk=1024)
    sent_abin, token = _exchange_start("scatter_ab_in", [dw_abin8], True, dscale, me)
    dh_ab = _mm_cols_dx("ab_in_dx", dz8, w_abin8)
    dx0, dn1g_ab, dsc1_ab, dsh1_ab = _rmsmod_bwd("ab_norm_bwd", x0, n1g[0], mods[0][1], dh_ab, dx1, token)

    dmod = jnp.stack([jnp.concatenate([dsh1_ab, dsc1_ab, dg1m, *gf0["mod"]], axis=1)[0],
                      jnp.concatenate([dsh1_cd, dsc1_cd, dg1c, *gf1["mod"]], axis=1)[0]])
    late_names = ["ada_b", "norm1_g", "norm2_g", "b_mix_w", "b_scale", "a_conv_w", "ffn_conv_w"]
    late_grads = [dmod, jnp.concatenate([dn1g_ab, dn1g_cd]), jnp.concatenate([gf0["n2g"], gf1["n2g"]]),
                  dmix_w.reshape(512, 128), dscale, dconv_w.reshape(3, 8, 64).transpose(1, 0, 2),
                  jnp.stack([gf0["cw24"].reshape(8, 3, 704), gf1["cw24"].reshape(8, 3, 704)], axis=1)]
    small_view = dict(ada_b=(2, 6 * D), norm1_g=(2, D), norm2_g=(2, D), b_mix_w=(512, 128), b_scale=(1, 512), c_kv_norm_g=(1, 128),
                      d_w_s=(512, 128), d_b_s=(4, 128), final_norm_g=(1, D),
                      a_conv_w=(3, 64), c_q_norm_g=(1, 32), d_ln_g=(1, 64), d_ln_b=(1, 64), ffn_conv_w=(2, 3, 704))
    late_sent, token = _exchange_start("gather_small_grads_late", late_grads, [False] * 5 + [True] * 2, dx0, me)

    res = {}

    def update(name, w, m, v, parts, shape3d):
        outs = _adamw("adamw_" + name, w.reshape(shape3d), m.reshape(shape3d), v.reshape(shape3d),
                      [p.reshape((p.shape[0],) + shape3d[1:]) for p in parts])
        res[name] = [o_.reshape(w.shape) for o_ in outs]

    p_cdin, p_uq, p_ukv, p_cdout = _exchange_wait("wait_scatter_cd", sent_cd, token)
    swap = lambda a: jnp.swapaxes(a, 1, 2)
    update("cd_w_in", swap(cd_w_in), swap(m_cd_w_in), swap(v_cd_w_in), [p_cdin], (1, 180, D))
    update("c_w_uq", swap(c_w_uq), swap(m_c_w_uq), swap(v_c_w_uq), [p_uq], (1, 96, 256))
    for name in ("cd_w_in", "c_w_uq"):
        res[name] = [swap(o_) for o_ in res[name]]
    update("c_w_ukv", c_w_ukv, m_c_w_ukv, v_c_w_ukv, [p_ukv], (1, 128, 128))
    update("cd_w_out", cd_w_out, m_cd_w_out, v_cd_w_out, [p_cdout], (1, 128, D))
    (p_dn1,) = _exchange_wait("wait_scatter_ffn1_down", gf1["sent_down"], token)
    (p_dn0,) = _exchange_wait("wait_scatter_ffn0_down", gf0["sent_down"], res["cd_w_out"][0])
    update("ffn_w_down", ffn_w_down, m_ffn_w_down, v_ffn_w_down, [p_dn0, p_dn1], (2, 352, D))
    (p_up1,) = _exchange_wait("wait_scatter_ffn1_up", gf1["sent_up"], token)
    (p_up0,) = _exchange_wait("wait_scatter_ffn0_up", gf0["sent_up"], res["ffn_w_down"][0])
    swap = lambda a: jnp.swapaxes(a, 1, 2)
    update("ffn_w_up", swap(ffn_w_up), swap(m_ffn_w_up), swap(v_ffn_w_up), [p_up0, p_up1], (2, 704, D))
    up_done = res["ffn_w_up"][0]
    res["ffn_w_up"] = [swap(o_) for o_ in res["ffn_w_up"]]
    (p_about,) = _exchange_wait("wait_scatter_ab_out", sent_about, up_done)
    update("ab_w_out", ab_w_out, m_ab_w_out, v_ab_w_out, [p_about], (1, 128, D))
    (p_abin,) = _exchange_wait("wait_scatter_ab_in", sent_abin, res["ab_w_out"][0])
    update("ab_w_in", ab_w_in, m_ab_w_in, v_ab_w_in, [p_abin], (1, D, 256))

    early_parts = _exchange_wait("wait_small_grads_early", early_sent, res["ab_w_in"][0])
    late_parts = _exchange_wait("wait_small_grads_late", late_sent, res["ab_w_in"][0])
    small_names = early_names + late_names
    small_parts = list(early_parts) + list(late_parts)
    dmod_all = late_parts[0]
    dmod_cols = lax.dynamic_slice_in_dim(dmod_all, me * 768, 768, axis=2).transpose(1, 0, 2)
    g_ada_w = _ada_bwd(c16, jnp.pad(dmod_cols, ((0, 0), (0, 16 - N_DEV), (0, 0))))
    update("ada_w", ada_w, m_ada_w, v_ada_w, [g_ada_w[l][None] for l in range(2)], (2, D, 768))

    small_w = dict(ada_b=(ada_b, m_ada_b, v_ada_b), norm1_g=(norm1_g, m_norm1_g, v_norm1_g), norm2_g=(norm2_g, m_norm2_g, v_norm2_g),
                   b_mix_w=(b_mix_w, m_b_mix_w, v_b_mix_w), b_scale=(b_scale, m_b_scale, v_b_scale),
                   c_kv_norm_g=(c_kv_norm_g, m_c_kv_norm_g, v_c_kv_norm_g), d_w_s=(d_w_s, m_d_w_s, v_d_w_s),
                   d_b_s=(d_b_s, m_d_b_s, v_d_b_s), final_norm_g=(final_norm_g, m_final_norm_g, v_final_norm_g),
                   a_conv_w=(a_conv_w, m_a_conv_w, v_a_conv_w), c_q_norm_g=(c_q_norm_g, m_c_q_norm_g, v_c_q_norm_g),
                   d_ln_g=(d_ln_g, m_d_ln_g, v_d_ln_g), d_ln_b=(d_ln_b, m_d_ln_b, v_d_ln_b),
                   ffn_conv_w=(ffn_conv_w, m_ffn_conv_w, v_ffn_conv_w))
    small_out = _adamw_small("adamw_small", [tuple(a.reshape(small_view[n]) for a in small_w[n]) for n in small_names],
                             list(small_parts))
    for n, outs in zip(small_names, small_out):
        res[n] = [o_.reshape(small_w[n][0].shape) for o_ in outs]

    loss = lax.psum(loss_local[0, 0], ("x", "y", "c"))
    order = ["ada_w", "ada_b", "norm1_g", "norm2_g", "ab_w_in", "a_conv_w", "b_mix_w", "b_scale", "ab_w_out", "cd_w_in", "c_q_norm_g",
             "c_w_uq", "c_kv_norm_g", "c_w_ukv", "d_ln_g", "d_ln_b", "d_w_s", "d_b_s", "cd_w_out", "ffn_w_up", "ffn_conv_w",
             "ffn_w_down", "final_norm_g"]
    return (loss, dx0[None], *[res[n][0] for n in order], *[res[n][1] for n in order], *[res[n][2] for n in order],
            *[res[n][3] for n in order])
```

```python
import functools
import math

import jax
import jax.numpy as jnp
from jax import lax
from jax.experimental import pallas as pl
from jax.experimental.pallas import tpu as pltpu

F32 = jnp.float32
BF16 = jnp.bfloat16
_MXU_DTYPE = BF16
WIRE_DTYPE = BF16
ACT_DTYPE = BF16
_VMEM_LIMIT = 56 * 2 ** 20
N_DEV = 8
EPS = 1e-6
POOL_WINDOWS = (2, 4, 8, 16)
ATTN_SCALE = (64 + 32) ** -0.5
ADAM_LR, ADAM_B1, ADAM_B2, ADAM_EPS, ADAM_WD, ADAM_STEP = 0.001, 0.9, 0.999, 1e-08, 0.01, 10
MESH = pl.DeviceIdType.MESH
ANY = pl.BlockSpec(memory_space=pl.ANY)


def _cp(*sem):
    return pltpu.CompilerParams(dimension_semantics=sem, vmem_limit_bytes=_VMEM_LIMIT)


def _dot(a, b, contract):
    dn = {"nn": (((1,), (0,)), ((), ())), "nt": (((1,), (1,)), ((), ())), "tn": (((0,), (0,)), ((), ()))}[contract]
    return lax.dot_general(a.astype(_MXU_DTYPE), b.astype(_MXU_DTYPE), dn, preferred_element_type=F32)


def _my_position():
    x, y, c = lax.axis_index("x"), lax.axis_index("y"), lax.axis_index("c")
    return x, y, c, 4 * x + 2 * y + c


def _exchange(name, groups, scatter):
    flat = [a for g in groups for a in g]
    n_in, n_grp = len(flat), len(groups)
    out_shapes = []
    for g in groups:
        slab = g[0].shape[1:] if scatter else g[0].shape
        out_shapes.append(jax.ShapeDtypeStruct((N_DEV, len(g)) + tuple(slab), g[0].dtype))

    def body(*refs):
        ins, outs = refs[:n_in], refs[n_in:n_in + n_grp]
        send_sems, recv_sems, local_sems = refs[n_in + n_grp:]
        x, y, c, me = _my_position()
        i = 0
        for gi, g in enumerate(groups):
            for l in range(len(g)):
                src = ins[i]
                i += 1
                pltpu.make_async_copy(src.at[me] if scatter else src, outs[gi].at[me, l], local_sems.at[gi]).start()
                for k in range(1, N_DEV):
                    px = 1 - x if k & 4 else x
                    py = 1 - y if k & 2 else y
                    pc = 1 - c if k & 1 else c
                    peer = 4 * px + 2 * py + pc
                    pltpu.make_async_remote_copy(
                        src_ref=src.at[peer] if scatter else src, dst_ref=outs[gi].at[me, l],
                        send_sem=send_sems.at[gi], recv_sem=recv_sems.at[gi],
                        device_id=(px, py, pc), device_id_type=MESH).start()
        for gi in range(n_grp):
            mine = outs[gi].at[me]
            pltpu.make_async_copy(mine, mine, local_sems.at[gi]).wait()
            seven = outs[gi].at[pl.ds(0, N_DEV - 1)]
            w = pltpu.make_async_remote_copy(src_ref=seven, dst_ref=seven, send_sem=send_sems.at[gi],
                                             recv_sem=recv_sems.at[gi], device_id=(x, y, c), device_id_type=MESH)
            w.wait_send()
            w.wait_recv()

    return pl.pallas_call(
        body, name=name, out_shape=tuple(out_shapes),
        in_specs=[ANY] * n_in, out_specs=tuple([ANY] * n_grp),
        scratch_shapes=[pltpu.SemaphoreType.DMA((n_grp,)), pltpu.SemaphoreType.DMA((n_grp,)),
                        pltpu.SemaphoreType.DMA((n_grp,))],
        compiler_params=pltpu.CompilerParams(has_side_effects=True),
    )(*flat)


HBM_SPEC = pl.BlockSpec(memory_space=pltpu.HBM)
SEM_SPEC = pl.BlockSpec(memory_space=pltpu.SEMAPHORE)
EFFECT = pltpu.SideEffectType.DATAFLOW_SIDE_EFFECTING


def _put_mine(name, srcs, scatter, me):
    n = len(srcs)
    slabs = [tuple(s.shape[1:] if sc else s.shape) for s, sc in zip(srcs, scatter)]

    def body(me_ref, *refs):
        for i in range(n):
            refs[n + i][...] = refs[i][...]

    def at_me(slab):
        return pl.BlockSpec((None,) + slab, lambda g, me_ref, nd=len(slab): (me_ref[0],) + (0,) * nd)

    def whole(slab):
        return pl.BlockSpec(slab, lambda g, me_ref, nd=len(slab): (0,) * nd)

    return pl.pallas_call(
        body, name=name,
        grid_spec=pltpu.PrefetchScalarGridSpec(
            num_scalar_prefetch=1, grid=(1,),
            in_specs=[at_me(slab) if sc else whole(slab) for slab, sc in zip(slabs, scatter)],
            out_specs=[at_me(slab) for slab in slabs]),
        out_shape=[jax.ShapeDtypeStruct((N_DEV,) + slab, s.dtype) for slab, s in zip(slabs, srcs)],
        compiler_params=_cp("arbitrary"))(me.reshape(1), *srcs)


def _exchange_start(name, srcs, scatter, after, me):
    n = len(srcs)
    scatter = list(scatter) if isinstance(scatter, (list, tuple)) else [scatter] * n
    lands = _put_mine(name + "_mine", srcs, scatter, me)
    srcs = [pltpu.with_memory_space_constraint(a, pltpu.HBM) for a in srcs]
    lands = [pltpu.with_memory_space_constraint(a, pltpu.HBM) for a in lands]

    def body(*refs):
        ins, land = refs[:n], refs[n:2 * n]
        send_sems, recv_sems, token = refs[2 * n + 1], refs[2 * n + 2], refs[-1]
        x, y, c, me_in = _my_position()
        for i in range(n):
            for k in range(1, N_DEV):
                px = 1 - x if k & 4 else x
                py = 1 - y if k & 2 else y
                pc = 1 - c if k & 1 else c
                pltpu.make_async_remote_copy(
                    src_ref=ins[i].at[4 * px + 2 * py + pc] if scatter[i] else ins[i], dst_ref=land[i].at[me_in],
                    send_sem=send_sems.at[i], recv_sem=recv_sems.at[i],
                    device_id=(px, py, pc), device_id_type=MESH).start()
        token[...] = jnp.zeros_like(token)

    outs = pl.pallas_call(
        body, name=name,
        out_shape=(pltpu.SemaphoreType.DMA((n,)), pltpu.SemaphoreType.DMA((n,)),
                   *[pltpu.HBM(a.shape, a.dtype) for a in srcs], *[pltpu.HBM(a.shape, a.dtype) for a in lands],
                   jax.ShapeDtypeStruct((8, 128), F32)),
        in_specs=[HBM_SPEC] * (2 * n) + [ANY],
        out_specs=(SEM_SPEC, SEM_SPEC, *[HBM_SPEC] * (2 * n), pl.BlockSpec(memory_space=pltpu.VMEM)),
        input_output_aliases={i: 2 + i for i in range(2 * n)},
        compiler_params=pltpu.CompilerParams(has_side_effects=EFFECT),
    )(*srcs, *lands, after)
    return (outs[0], outs[1], outs[2:2 + n], outs[2 + n:2 + 2 * n]), outs[-1]


def _exchange_wait(name, handle, after, which=None):
    send_sems, recv_sems, srcs, lands = handle
    which = list(range(len(srcs))) if which is None else list(which)
    srcs, lands = [srcs[i] for i in which], [lands[i] for i in which]
    n = len(srcs)

    def body(*refs):
        land, send_ref, recv_ref = refs[n:2 * n], refs[2 * n], refs[2 * n + 1]
        x, y, c, _ = _my_position()
        for k, i in enumerate(which):
            seven = land[k].at[pl.ds(0, N_DEV - 1)]
            w = pltpu.make_async_remote_copy(src_ref=seven, dst_ref=seven, send_sem=send_ref.at[i], recv_sem=recv_ref.at[i],
                                             device_id=(x, y, c), device_id_type=MESH)
            w.wait_send()
            w.wait_recv()

    outs = pl.pallas_call(
        body, name=name,
        out_shape=(*[pltpu.HBM(a.shape, a.dtype) for a in srcs], *[pltpu.HBM(a.shape, a.dtype) for a in lands]),
        in_specs=[HBM_SPEC] * (2 * n) + [SEM_SPEC, SEM_SPEC, ANY],
        out_specs=tuple([HBM_SPEC] * (2 * n)),
        input_output_aliases={i: i for i in range(2 * n)},
        compiler_params=pltpu.CompilerParams(has_side_effects=EFFECT),
    )(*srcs, *lands, send_sems, recv_sems, after)
    return outs[n:]


def _other_chips(x, y):
    return [(1 - x, y), (x, 1 - y), (1 - x, 1 - y)]


def _hier_gather_start(name, srcs, after, me):
    n = len(srcs)
    lands = _put_mine(name + "_mine", srcs, [False] * n, me)
    srcs = [pltpu.with_memory_space_constraint(a, pltpu.HBM) for a in srcs]
    lands = [pltpu.with_memory_space_constraint(a, pltpu.HBM) for a in lands]

    def body(*refs):
        ins, land = refs[:n], refs[n:2 * n]
        ici_send, ici_recv, d2d_send, d2d_recv = refs[2 * n + 1:2 * n + 5]
        token = refs[-1]
        x, y, c, me_in = _my_position()
        for i in range(n):
            pltpu.make_async_remote_copy(src_ref=ins[i], dst_ref=land[i].at[me_in], send_sem=d2d_send.at[i], recv_sem=d2d_recv.at[i],
                                         device_id=(x, y, 1 - c), device_id_type=MESH).start()
            for px, py in _other_chips(x, y):
                pltpu.make_async_remote_copy(src_ref=ins[i], dst_ref=land[i].at[me_in], send_sem=ici_send.at[i],
                                             recv_sem=ici_recv.at[i], device_id=(px, py, c), device_id_type=MESH).start()
        token[...] = jnp.zeros_like(token)

    sem = pltpu.SemaphoreType.DMA((n,))
    outs = pl.pallas_call(
        body, name=name,
        out_shape=(sem, sem, sem, sem, *[pltpu.HBM(a.shape, a.dtype) for a in srcs], *[pltpu.HBM(a.shape, a.dtype) for a in lands],
                   jax.ShapeDtypeStruct((8, 128), F32)),
        in_specs=[HBM_SPEC] * (2 * n) + [ANY],
        out_specs=(SEM_SPEC,) * 4 + (HBM_SPEC,) * (2 * n) + (pl.BlockSpec(memory_space=pltpu.VMEM),),
        input_output_aliases={i: 4 + i for i in range(2 * n)},
        compiler_params=pltpu.CompilerParams(has_side_effects=EFFECT),
    )(*srcs, *lands, after)
    return (outs[:4], outs[4:4 + n], outs[4 + n:4 + 2 * n]), outs[-1]


def _hier_gather_forward(name, handle, after):
    sems, srcs, lands = handle
    n = len(srcs)

    def body(*refs):
        land = refs[n:2 * n]
        ici_send, ici_recv, d2d_send, d2d_recv = refs[2 * n:2 * n + 4]
        x, y, c, _ = _my_position()
        for i in range(n):
            three = land[i].at[pl.ds(0, 3)]
            pltpu.make_async_remote_copy(src_ref=three, dst_ref=three, send_sem=ici_send.at[i], recv_sem=ici_recv.at[i],
                                         device_id=(x, y, c), device_id_type=MESH).wait_recv()
            for px, py in _other_chips(x, y):
                slab = land[i].at[4 * px + 2 * py + c]
                pltpu.make_async_remote_copy(src_ref=slab, dst_ref=slab, send_sem=d2d_send.at[i], recv_sem=d2d_recv.at[i],
                                             device_id=(x, y, 1 - c), device_id_type=MESH).start()

    outs = pl.pallas_call(
        body, name=name,
        out_shape=(*[pltpu.HBM(a.shape, a.dtype) for a in srcs], *[pltpu.HBM(a.shape, a.dtype) for a in lands]),
        in_specs=[HBM_SPEC] * (2 * n) + [SEM_SPEC] * 4 + [ANY],
        out_specs=tuple([HBM_SPEC] * (2 * n)),
        input_output_aliases={i: i for i in range(2 * n)},
        compiler_params=pltpu.CompilerParams(has_side_effects=EFFECT),
    )(*srcs, *lands, *sems, after)
    return (sems, outs[:n], outs[n:])


def _hier_gather_wait(name, handle, after):
    sems, srcs, lands = handle
    n = len(srcs)

    def body(*refs):
        land = refs[n:2 * n]
        ici_send, ici_recv, d2d_send, d2d_recv = refs[2 * n:2 * n + 4]
        x, y, c, _ = _my_position()
        for i in range(n):
            three, four = land[i].at[pl.ds(0, 3)], land[i].at[pl.ds(0, 4)]
            pltpu.make_async_remote_copy(src_ref=three, dst_ref=three, send_sem=ici_send.at[i], recv_sem=ici_recv.at[i],
                                         device_id=(x, y, c), device_id_type=MESH).wait_send()
            w = pltpu.make_async_remote_copy(src_ref=four, dst_ref=four, send_sem=d2d_send.at[i], recv_sem=d2d_recv.at[i],
                                             device_id=(x, y, c), device_id_type=MESH)
            w.wait_send()
            w.wait_recv()

    outs = pl.pallas_call(
        body, name=name,
        out_shape=(*[pltpu.HBM(a.shape, a.dtype) for a in srcs], *[pltpu.HBM(a.shape, a.dtype) for a in lands]),
        in_specs=[HBM_SPEC] * (2 * n) + [SEM_SPEC] * 4 + [ANY],
        out_specs=tuple([HBM_SPEC] * (2 * n)),
        input_output_aliases={i: i for i in range(2 * n)},
        compiler_params=pltpu.CompilerParams(has_side_effects=EFFECT),
    )(*srcs, *lands, *sems, after)
    return outs[n:]


def _pack(arrs):
    flat = jnp.concatenate([a.reshape(-1).astype(F32) for a in arrs])
    n = flat.shape[0]
    rows = -(-n // 1024) * 8
    return jnp.pad(flat, (0, rows * 128 - n)).reshape(rows, 128)


def _unpack(buf, shapes, lead=()):
    flat = buf.reshape(lead + (-1,))
    out, off = [], 0
    for s in shapes:
        n = math.prod(s)
        out.append(flat[..., off:off + n].reshape(lead + tuple(s)))
        off += n
    return out


def _mm(name, a, a_spec, b, b_spec, out_sds, o_spec, grid, contract, nk=1, stacked=0):
    o_blk = tuple(d for d in o_spec.block_shape if d is not None)

    def body(a_ref, b_ref, o_ref, *acc):
        if stacked:
            r = _dot(a_ref[0], b_ref[0], contract)
            for q in range(1, stacked):
                r = r + _dot(a_ref[q], b_ref[q], contract)
        else:
            r = _dot(a_ref[...], b_ref[...], contract)
        if nk == 1:
            o_ref[...] = r.astype(o_ref.dtype)
        else:
            k = pl.program_id(len(grid) - 1)

            @pl.when(k == 0)
            def _():
                acc[0][...] = r

            @pl.when(k > 0)
            def _():
                acc[0][...] += r

            @pl.when(k == nk - 1)
            def _():
                o_ref[...] = acc[0][...].astype(o_ref.dtype)

    sem = ("parallel",) * (len(grid) - 1) + (("arbitrary",) if nk > 1 else ("parallel",))
    return pl.pallas_call(
        body, name=name, out_shape=out_sds, grid=grid, in_specs=[a_spec, b_spec], out_specs=o_spec,
        scratch_shapes=[pltpu.VMEM(o_blk, F32)] if nk > 1 else [], compiler_params=_cp(*sem))(a, b)


def _tile(n, want):
    t = min(n, want)
    assert n % t == 0, (n, t)
    return t


def _mm_nn(name, a, b, out_dtype=F32, tm=512, tn=512):
    (M, K), N = a.shape, b.shape[1]
    tm, tn = _tile(M, tm), _tile(N, tn)
    return _mm(name, a, pl.BlockSpec((tm, K), lambda i, j: (i, 0)), b, pl.BlockSpec((K, tn), lambda i, j: (0, j)),
               jax.ShapeDtypeStruct((M, N), out_dtype), pl.BlockSpec((tm, tn), lambda i, j: (i, j)),
               (M // tm, N // tn), "nn")


def _mm_nt(name, a, b, out_dtype=F32, tm=512, tn=512):
    (M, K), N = a.shape, b.shape[0]
    tm, tn = _tile(M, tm), _tile(N, tn)
    return _mm(name, a, pl.BlockSpec((tm, K), lambda i, j: (i, 0)), b, pl.BlockSpec((tn, K), lambda i, j: (j, 0)),
               jax.ShapeDtypeStruct((M, N), out_dtype), pl.BlockSpec((tm, tn), lambda i, j: (i, j)),
               (M // tm, N // tn), "nt")


def _mm_tn(name, a, b, out_dtype=F32, tm=512, tn=512):
    (K, M), N = a.shape, b.shape[1]
    tm, tn = _tile(M, tm), _tile(N, tn)
    return _mm(name, a, pl.BlockSpec((K, tm), lambda i, j: (0, i)), b, pl.BlockSpec((K, tn), lambda i, j: (0, j)),
               jax.ShapeDtypeStruct((M, N), out_dtype), pl.BlockSpec((tm, tn), lambda i, j: (i, j)),
               (M // tm, N // tn), "tn")


def _mm_cols(name, a, w, out_dtype=F32, tm=512):
    (M, K), (J, _, n) = a.shape, w.shape
    tm = _tile(M, tm)
    return _mm(name, a, pl.BlockSpec((tm, K), lambda j, i: (i, 0)), w, pl.BlockSpec((None, K, n), lambda j, i: (j, 0, 0)),
               jax.ShapeDtypeStruct((J, M, n), out_dtype), pl.BlockSpec((None, tm, n), lambda j, i: (j, i, 0)),
               (J, M // tm), "nn")


def _mm_cols_dx(name, d, w, out_dtype=F32, tm=512, jb=None):
    (J, M, n), K = d.shape, w.shape[1]
    tm, jb = _tile(M, tm), J if jb is None else jb
    return _mm(name, d, pl.BlockSpec((jb, tm, n), lambda i, j: (j, i, 0)), w, pl.BlockSpec((jb, K, n), lambda i, j: (j, 0, 0)),
               jax.ShapeDtypeStruct((M, K), out_dtype), pl.BlockSpec((tm, K), lambda i, j: (i, 0)),
               (M // tm, J // jb), "nt", nk=J // jb, stacked=jb)


def _mm_cols_dw(name, a, d, out_dtype=F32, tk=512):
    (M, K), (J, _, n) = a.shape, d.shape
    tk = _tile(K, tk)
    return _mm(name, a, pl.BlockSpec((M, tk), lambda j, i: (0, i)), d, pl.BlockSpec((None, M, n), lambda j, i: (j, 0, 0)),
               jax.ShapeDtypeStruct((J, K, n), out_dtype), pl.BlockSpec((None, tk, n), lambda j, i: (j, i, 0)),
               (J, K // tk), "tn")


def _mm_cols_dwt(name, a, d, out_dtype=F32, tk=512):
    (M, K), (J, _, n) = a.shape, d.shape
    tk = _tile(K, tk)
    return _mm(name, d, pl.BlockSpec((None, M, n), lambda j, i: (j, 0, 0)), a, pl.BlockSpec((M, tk), lambda j, i: (0, i)),
               jax.ShapeDtypeStruct((J, n, K), out_dtype), pl.BlockSpec((None, n, tk), lambda j, i: (j, 0, i)),
               (J, K // tk), "tn")


def _mm_rows_resid(name, a, w, resid, gate, tm=512):
    (Q, M, k), N = a.shape, w.shape[2]
    tm = _tile(M, tm)

    def body(a_ref, w_ref, r_ref, g_ref, y_ref, x_ref):
        y = _dot(a_ref[0], w_ref[0], "nn")
        for q in range(1, Q):
            y = y + _dot(a_ref[q], w_ref[q], "nn")
        y_ref[...] = y
        x_ref[...] = r_ref[...] + g_ref[...] * y

    return pl.pallas_call(
        body, name=name, grid=(M // tm,),
        out_shape=(jax.ShapeDtypeStruct((M, N), F32), jax.ShapeDtypeStruct((M, N), F32)),
        in_specs=[pl.BlockSpec((Q, tm, k), lambda i: (0, i, 0)), pl.BlockSpec((Q, k, N), lambda i: (0, 0, 0)),
                  pl.BlockSpec((tm, N), lambda i: (i, 0)), pl.BlockSpec((1, N), lambda i: (0, 0))],
        out_specs=(pl.BlockSpec((tm, N), lambda i: (i, 0)), pl.BlockSpec((tm, N), lambda i: (i, 0))),
        compiler_params=_cp("parallel"))(a, w, resid, gate)


def _mm_rows_dx(name, d, w, out_dtype=F32, tm=512):
    (M, N), (Q, k, _) = d.shape, w.shape
    tm = _tile(M, tm)
    return _mm(name, d, pl.BlockSpec((tm, N), lambda q, i: (i, 0)), w, pl.BlockSpec((None, k, N), lambda q, i: (q, 0, 0)),
               jax.ShapeDtypeStruct((Q, M, k), out_dtype), pl.BlockSpec((None, tm, k), lambda q, i: (q, i, 0)),
               (Q, M // tm), "nt")


def _mm_rows_dw(name, a, d, out_dtype=F32, tn=512):
    (Q, M, k), N = a.shape, d.shape[1]
    tn = _tile(N, tn)
    return _mm(name, a, pl.BlockSpec((None, M, k), lambda q, j: (q, 0, 0)), d, pl.BlockSpec((M, tn), lambda q, j: (0, j)),
               jax.ShapeDtypeStruct((Q, k, N), out_dtype), pl.BlockSpec((None, k, tn), lambda q, j: (q, 0, j)),
               (Q, N // tn), "tn")


def _silu(v):
    return v * jax.nn.sigmoid(v)


def _ada_fwd(c16, ada_w):
    L, D, n = ada_w.shape

    def body(c_ref, w_ref, o_ref):
        o_ref[...] = _dot(_silu(c_ref[...]), w_ref[...], "nn")

    return pl.pallas_call(
        body, name="ada_fwd", grid=(L,), out_shape=jax.ShapeDtypeStruct((L, 16, n), F32),
        in_specs=[pl.BlockSpec((16, D), lambda l: (0, 0)), pl.BlockSpec((None, D, n), lambda l: (l, 0, 0))],
        out_specs=pl.BlockSpec((None, 16, n), lambda l: (l, 0, 0)), compiler_params=_cp("parallel"))(c16, ada_w)


def _ada_bwd(c16, dmod16):
    L, _, n = dmod16.shape
    D = c16.shape[1]

    def body(c_ref, d_ref, o_ref):
        o_ref[...] = _dot(_silu(c_ref[...]), d_ref[...], "tn")

    return pl.pallas_call(
        body, name="ada_bwd", grid=(L,), out_shape=jax.ShapeDtypeStruct((L, D, n), F32),
        in_specs=[pl.BlockSpec((16, D), lambda l: (0, 0)), pl.BlockSpec((None, 16, n), lambda l: (l, 0, 0))],
        out_specs=pl.BlockSpec((None, D, n), lambda l: (l, 0, 0)), compiler_params=_cp("parallel"))(c16, dmod16)


def _row_spec(tr, n):
    return pl.BlockSpec((tr, n), lambda i: (i, 0))


def _vec_spec(n):
    return pl.BlockSpec((1, n), lambda i: (0, 0))


def _rmsmod_fwd(name, x, g, sc, sh, after, tr=256):
    S, D = x.shape

    def body(x_ref, g_ref, sc_ref, sh_ref, after_ref, h_ref):
        xv = x_ref[...]
        rstd = lax.rsqrt(jnp.mean(xv * xv, axis=-1, keepdims=True) + EPS)
        y = xv * rstd * g_ref[...]
        h_ref[...] = (y * (1.0 + sc_ref[...]) + sh_ref[...]).astype(h_ref.dtype)

    return pl.pallas_call(
        body, name=name, grid=(S // tr,), out_shape=jax.ShapeDtypeStruct((S, D), _MXU_DTYPE),
        in_specs=[_row_spec(tr, D), _vec_spec(D), _vec_spec(D), _vec_spec(D), ANY], out_specs=_row_spec(tr, D),
        compiler_params=_cp("parallel"))(x, g, sc, sh, after)


def _acc_rows(ref, val, first):
    s = jnp.sum(val, axis=0, keepdims=True)

    @pl.when(first)
    def _():
        ref[...] = s

    @pl.when(jnp.logical_not(first))
    def _():
        ref[...] += s


def _rmsmod_bwd(name, x, g, sc, dh, dres, after, tr=256):
    S, D = x.shape

    def body(x_ref, g_ref, sc_ref, dh_ref, dres_ref, after_ref, dx_ref, dg_ref, dsc_ref, dsh_ref):
        first = pl.program_id(0) == 0
        xv, dh_v, gv = x_ref[...], dh_ref[...], g_ref[...]
        rstd = lax.rsqrt(jnp.mean(xv * xv, axis=-1, keepdims=True) + EPS)
        xhat = xv * rstd
        _acc_rows(dsh_ref, dh_v, first)
        _acc_rows(dsc_ref, dh_v * (xhat * gv), first)
        dyg = dh_v * (1.0 + sc_ref[...])
        _acc_rows(dg_ref, dyg * xhat, first)
        dxhat = dyg * gv
        dx_ref[...] = dres_ref[...] + rstd * (dxhat - xhat * jnp.mean(dxhat * xhat, axis=-1, keepdims=True))

    vec = jax.ShapeDtypeStruct((1, D), F32)
    return pl.pallas_call(
        body, name=name, grid=(S // tr,), out_shape=(jax.ShapeDtypeStruct((S, D), F32), vec, vec, vec),
        in_specs=[_row_spec(tr, D), _vec_spec(D), _vec_spec(D), _row_spec(tr, D), _row_spec(tr, D), ANY],
        out_specs=(_row_spec(tr, D), _vec_spec(D), _vec_spec(D), _vec_spec(D)),
        compiler_params=_cp("arbitrary"))(x, g, sc, dh, dres, after)


def _loss_head(x, g, target, tr=256):
    S, D = x.shape

    def body(x_ref, g_ref, t_ref, loss_ref, dx_ref, dg_ref):
        first = pl.program_id(0) == 0
        xv, gv = x_ref[...], g_ref[...]
        rstd = lax.rsqrt(jnp.mean(xv * xv, axis=-1, keepdims=True) + EPS)
        xhat = xv * rstd
        err = xhat * gv - t_ref[...]
        part = 0.5 * jnp.sum(jnp.mean(err * err, axis=-1, keepdims=True), axis=0, keepdims=True)

        @pl.when(first)
        def _():
            loss_ref[...] = part

        @pl.when(jnp.logical_not(first))
        def _():
            loss_ref[...] += part

        dout = err * (1.0 / D)
        _acc_rows(dg_ref, dout * xhat, first)
        dxhat = dout * gv
        dx_ref[...] = rstd * (dxhat - xhat * jnp.mean(dxhat * xhat, axis=-1, keepdims=True))

    return pl.pallas_call(
        body, name="loss_head", grid=(S // tr,),
        out_shape=(jax.ShapeDtypeStruct((1, 1), F32), jax.ShapeDtypeStruct((S, D), F32), jax.ShapeDtypeStruct((1, D), F32)),
        in_specs=[_row_spec(tr, D), _vec_spec(D), _row_spec(tr, D)],
        out_specs=(pl.BlockSpec((1, 1), lambda i: (0, 0)), _row_spec(tr, D), _vec_spec(D)),
        compiler_params=_cp("arbitrary"))(x, g, target)


def _gate_bwd(name, dx, y, gate, tr=256):
    S, D = dx.shape

    def body(dx_ref, y_ref, g_ref, dy_ref, dg_ref):
        dxv = dx_ref[...]
        dy_ref[...] = (g_ref[...] * dxv).astype(dy_ref.dtype)
        _acc_rows(dg_ref, dxv * y_ref[...], pl.program_id(0) == 0)

    return pl.pallas_call(
        body, name=name, grid=(S // tr,),
        out_shape=(jax.ShapeDtypeStruct((S, D), _MXU_DTYPE), jax.ShapeDtypeStruct((1, D), F32)),
        in_specs=[_row_spec(tr, D), _row_spec(tr, D), _vec_spec(D)], out_specs=(_row_spec(tr, D), _vec_spec(D)),
        compiler_params=_cp("arbitrary"))(dx, y, gate)


def _shift_down(v, k):
    t = lax.broadcasted_iota(jnp.int32, v.shape, 0)
    return jnp.where(t >= k, pltpu.roll(v, k, axis=0), 0.0)


def _shift_up(v, k):
    n = v.shape[0]
    t = lax.broadcasted_iota(jnp.int32, v.shape, 0)
    return jnp.where(t < n - k, pltpu.roll(v, n - k, axis=0), 0.0)


def _window_sum(p, w, shift):
    s, k = p, 1
    while k < w:
        s = s + shift(s, k)
        k *= 2
    return s


def _pool_count(shape, w):
    t = lax.broadcasted_iota(jnp.int32, shape, 0)
    return jnp.minimum(t + 1, w).astype(F32)


def _ab_specs(S):
    zs = [pl.BlockSpec((None, S, 128), functools.partial(lambda g, q: (2 * q + g // 2, 0, g % 2), q=q)) for q in range(4)]
    return zs


def _ab_mix_fwd(z8, conv_w, mix_w, scale):
    S = z8.shape[1]

    def body(b_ref, c_ref, a_ref, p_ref, w_ref, mix_ref, sc_ref, y_ref):
        g = pl.program_id(0)
        cg = c_ref[...] * a_ref[...]
        w = w_ref[...]
        conv = w[0:1] * _shift_down(cg, 2) + w[1:2] * _shift_down(cg, 1) + w[2:3] * cg
        y_ref[0] = (b_ref[...] * conv).astype(y_ref.dtype)
        for gg, win in enumerate(POOL_WINDOWS):
            @pl.when(g == gg)
            def _(win=win):
                p = p_ref[...]
                pooled = _window_sum(p, win, _shift_down) / _pool_count(p.shape, win) - p
                y_ref[1] = (_dot(pooled, mix_ref[...], "nn") * sc_ref[...]).astype(y_ref.dtype)

    return pl.pallas_call(
        body, name="ab_mix_fwd", grid=(4,), out_shape=jax.ShapeDtypeStruct((2, S, 512), _MXU_DTYPE),
        in_specs=_ab_specs(S) + [pl.BlockSpec((3, 128), lambda g: (0, g)), pl.BlockSpec((None, 128, 128), lambda g: (g, 0, 0)),
                                 pl.BlockSpec((1, 128), lambda g: (0, g))],
        out_specs=pl.BlockSpec((2, S, 128), lambda g: (0, 0, g)), compiler_params=_cp("parallel"))(z8, z8, z8, z8, conv_w, mix_w, scale)


def _ab_mix_bwd(z8, dycat2, conv_w, mix_w, scale, after):
    S = z8.shape[1]

    def body(b_ref, c_ref, a_ref, p_ref, dy_ref, w_ref, mix_ref, sc_ref, after_ref, dz_ref, dw_ref, dmix_ref, dsc_ref):
        g = pl.program_id(0)
        bv, cv, av, w = b_ref[...], c_ref[...], a_ref[...], w_ref[...]
        dya = dy_ref[0]
        cg = cv * av
        cg1, cg2 = _shift_down(cg, 1), _shift_down(cg, 2)
        conv = w[0:1] * cg2 + w[1:2] * cg1 + w[2:3] * cg
        dz_ref[0] = (dya * conv).astype(dz_ref.dtype)
        dconv = dya * bv
        dcg = w[2:3] * dconv + w[1:2] * _shift_up(dconv, 1) + w[0:1] * _shift_up(dconv, 2)
        dz_ref[1] = (dcg * av).astype(dz_ref.dtype)
        dz_ref[2] = (dcg * cv).astype(dz_ref.dtype)
        dw_ref[0:1, :] = jnp.sum(dconv * cg2, axis=0, keepdims=True)
        dw_ref[1:2, :] = jnp.sum(dconv * cg1, axis=0, keepdims=True)
        dw_ref[2:3, :] = jnp.sum(dconv * cg, axis=0, keepdims=True)
        for gg, win in enumerate(POOL_WINDOWS):
            @pl.when(g == gg)
            def _(win=win):
                p, dyb, mix = p_ref[...], dy_ref[1], mix_ref[...]
                cnt = _pool_count(p.shape, win)
                pooled = _window_sum(p, win, _shift_down) / cnt - p
                dsc_ref[...] = jnp.sum(dyb * _dot(pooled, mix, "nn"), axis=0, keepdims=True)
                dmixed = dyb * sc_ref[...]
                dmix_ref[...] = _dot(pooled, dmixed, "tn")
                dpooled = _dot(dmixed, mix, "nt")
                dz_ref[3] = (_window_sum(dpooled / cnt, win, _shift_up) - dpooled).astype(dz_ref.dtype)

    return pl.pallas_call(
        body, name="ab_mix_bwd", grid=(4,),
        out_shape=(jax.ShapeDtypeStruct((4, 2, S, 256), _MXU_DTYPE), jax.ShapeDtypeStruct((3, 512), F32),
                   jax.ShapeDtypeStruct((4, 128, 128), F32), jax.ShapeDtypeStruct((1, 512), F32)),
        in_specs=_ab_specs(S) + [pl.BlockSpec((2, S, 128), lambda g: (0, 0, g)), pl.BlockSpec((3, 128), lambda g: (0, g)),
                                 pl.BlockSpec((None, 128, 128), lambda g: (g, 0, 0)), pl.BlockSpec((1, 128), lambda g: (0, g)), ANY],
        out_specs=(pl.BlockSpec((4, None, S, 128), lambda g: (0, g // 2, 0, g % 2)), pl.BlockSpec((3, 128), lambda g: (0, g)),
                   pl.BlockSpec((None, 128, 128), lambda g: (g, 0, 0)), pl.BlockSpec((1, 128), lambda g: (0, g))),
        compiler_params=_cp("parallel"))(z8, z8, z8, z8, dycat2, conv_w, mix_w, scale, after)


HALO = 16


def _ffn_specs(S, n, tr):
    nb = S // HALO
    tile = pl.BlockSpec((2, None, tr, n), lambda j, i: (0, j, i, 0))
    prev = pl.BlockSpec((2, None, HALO, n), lambda j, i: (0, j, jnp.maximum(i * (tr // HALO) - 1, 0), 0))
    nxt = pl.BlockSpec((2, None, HALO, n), lambda j, i: (0, j, jnp.minimum((i + 1) * (tr // HALO), nb - 1), 0))
    cw = pl.BlockSpec((2, None, 3, n), lambda j, i: (0, j, 0, 0))
    return tile, prev, nxt, cw


def _shifted_rows(ext, lo, rows):
    ext = ext.astype(F32)
    return pltpu.roll(ext, 1, axis=0)[lo:lo + rows], pltpu.roll(ext, 2, axis=0)[lo:lo + rows]


def _ffn_gate_fwd(name, u24, cw24, tr=256):
    _, J, S, n = u24.shape
    tile, prev, _, cw = _ffn_specs(S, n, tr)

    def body(u_ref, up_ref, w_ref, a_ref):
        keep = (pl.program_id(1) > 0).astype(u_ref.dtype)
        z = []
        for h in range(2):
            ext = jnp.concatenate([up_ref[h] * keep, u_ref[h]], axis=0)
            x1, x2 = _shifted_rows(ext, HALO, tr)
            w = w_ref[h]
            z.append(w[0:1] * x2 + w[1:2] * x1 + w[2:3] * u_ref[h].astype(F32))
        a_ref[...] = (_silu(z[0]) * z[1]).astype(a_ref.dtype)

    return pl.pallas_call(
        body, name=name, grid=(J, S // tr), out_shape=jax.ShapeDtypeStruct((J, S, n), _MXU_DTYPE),
        in_specs=[tile, prev, cw], out_specs=pl.BlockSpec((None, tr, n), lambda j, i: (j, i, 0)),
        compiler_params=_cp("parallel", "parallel"))(u24, u24, cw24)


def _ffn_gate_bwd(name, u24, cw24, da4, after, tr=256):
    _, J, S, n = u24.shape
    tile, prev, nxt, cw = _ffn_specs(S, n, tr)
    nb = S // HALO
    ext_rows = tr + 2 * HALO

    def body(u_ref, up_ref, un_ref, w_ref, da_ref, dan_ref, after_ref, du_ref, dcw_ref):
        i = pl.program_id(1)
        first = i == 0
        keep_prev = (i > 0).astype(u_ref.dtype)
        keep_next = (i < S // tr - 1).astype(F32)
        w = [w_ref[h] for h in range(2)]
        m = tr + HALO
        xs, z = [], []
        for h in range(2):
            ext = jnp.concatenate([up_ref[h] * keep_prev, u_ref[h], un_ref[h]], axis=0)
            x1, x2 = _shifted_rows(ext, HALO, m)
            x0 = ext[HALO:HALO + m].astype(F32)
            xs.append((x2, x1, x0))
            z.append(w[h][0:1] * x2 + w[h][1:2] * x1 + w[h][2:3] * x0)
        zg, zu = z
        da = jnp.concatenate([da_ref[...].astype(F32), dan_ref[...].astype(F32) * keep_next], axis=0)
        sg = jax.nn.sigmoid(zg)
        dz = [da * zu * (sg * (1.0 + zg * (1.0 - sg))), da * (zg * sg)]
        for h in range(2):
            d = dz[h]
            du = w[h][2:3] * d[:tr] + w[h][1:2] * pltpu.roll(d, m - 1, axis=0)[:tr] + w[h][0:1] * pltpu.roll(d, m - 2, axis=0)[:tr]
            du_ref[h] = du.astype(du_ref.dtype)
            dt = d[:tr]
            parts = [jnp.sum(dt * xk[:tr], axis=0, keepdims=True) for xk in xs[h]]
            for k in range(3):
                @pl.when(first)
                def _(k=k, h=h):
                    dcw_ref[h, k:k + 1, :] = parts[k]

                @pl.when(jnp.logical_not(first))
                def _(k=k, h=h):
                    dcw_ref[h, k:k + 1, :] += parts[k]

    da_tile = pl.BlockSpec((None, tr, n), lambda j, i: (j, i, 0))
    da_next = pl.BlockSpec((None, HALO, n), lambda j, i: (j, jnp.minimum((i + 1) * (tr // HALO), nb - 1), 0))
    return pl.pallas_call(
        body, name=name, grid=(J, S // tr),
        out_shape=(jax.ShapeDtypeStruct((2, J, S, n), _MXU_DTYPE), jax.ShapeDtypeStruct((2, J, 3, n), F32)),
        in_specs=[tile, prev, nxt, cw, da_tile, da_next, ANY], out_specs=(tile, cw),
        compiler_params=_cp("parallel", "arbitrary"))(u24, u24, u24, cw24, da4, da4, after)


def _rms_rows(v, g):
    rstd = lax.rsqrt(jnp.mean(v * v, axis=-1, keepdims=True) + EPS)
    return v * rstd * g


def _rms_rows_bwd(v, g, dy):
    rstd = lax.rsqrt(jnp.mean(v * v, axis=-1, keepdims=True) + EPS)
    vhat = v * rstd
    dvhat = dy * g
    return rstd * (dvhat - vhat * jnp.mean(dvhat * vhat, axis=-1, keepdims=True)), dy * vhat


def _mla_prep_fwd(z, qg, kvg, tr=256):
    S = z.shape[0]

    def body(q_ref, kv_ref, qg_ref, kvg_ref, qn_ref, kvn_ref):
        qn_ref[...] = _rms_rows(q_ref[...], qg_ref[...]).astype(qn_ref.dtype)
        kvn_ref[...] = _rms_rows(kv_ref[...], kvg_ref[...]).astype(kvn_ref.dtype)

    return pl.pallas_call(
        body, name="mla_prep_fwd", grid=(S // tr,),
        out_shape=(jax.ShapeDtypeStruct((S, 256), _MXU_DTYPE), jax.ShapeDtypeStruct((S, 128), _MXU_DTYPE)),
        in_specs=[pl.BlockSpec((tr, 256), lambda i: (i, 0)), pl.BlockSpec((tr, 128), lambda i: (i, 2)), _vec_spec(256), _vec_spec(128)],
        out_specs=(_row_spec(tr, 256), _row_spec(tr, 128)), compiler_params=_cp("parallel"))(z, z, qg, kvg)


def _mla_prep_bwd(z, qg, kvg, dqn, dkvn, dkpe, duv, tr=256):
    S = z.shape[0]

    def body(q_ref, kv_ref, qg_ref, kvg_ref, dqn_ref, dkvn_ref, dkpe_ref, duv_ref, dz_ref, dqg_ref, dkvg_ref):
        first = pl.program_id(0) == 0
        dq, dqg = _rms_rows_bwd(q_ref[...], qg_ref[...], dqn_ref[...])
        dkv, dkvg = _rms_rows_bwd(kv_ref[...], kvg_ref[...], dkvn_ref[...])
        _acc_rows(dqg_ref, dqg, first)
        _acc_rows(dkvg_ref, dkvg, first)
        dz_ref[:, 0:256] = dq.astype(dz_ref.dtype)
        dz_ref[:, 256:384] = dkv.astype(dz_ref.dtype)
        dz_ref[:, 384:512] = dkpe_ref[...].astype(dz_ref.dtype)
        dz_ref[:, 512:1536] = duv_ref[...].astype(dz_ref.dtype)

    return pl.pallas_call(
        body, name="mla_prep_bwd", grid=(S // tr,),
        out_shape=(jax.ShapeDtypeStruct((S, 1536), _MXU_DTYPE), jax.ShapeDtypeStruct((1, 256), F32), jax.ShapeDtypeStruct((1, 128), F32)),
        in_specs=[pl.BlockSpec((tr, 256), lambda i: (i, 0)), pl.BlockSpec((tr, 128), lambda i: (i, 2)), _vec_spec(256), _vec_spec(128),
                  _row_spec(tr, 256), _row_spec(tr, 128), _row_spec(tr, 128), _row_spec(tr, 1024)],
        out_specs=(_row_spec(tr, 1536), _vec_spec(256), _vec_spec(128)),
        compiler_params=_cp("arbitrary"))(z, z, qg, kvg, dqn, dkvn, dkpe, duv)


def _rope(v, cos, sa, sb):
    return v * cos + pltpu.roll(v, 112, axis=1) * sa + pltpu.roll(v, 16, axis=1) * sb


def _rope_t(d, cos, sa, sb):
    return d * cos + pltpu.roll(d * sa, 16, axis=1) + pltpu.roll(d * sb, 112, axis=1)


def _rope_fwd(qraw, kvall, z, cosq, cosk, sa, sb, tr=256):
    S = qraw.shape[0]

    def body(q_ref, k_ref, v_ref, kpe_ref, cq_ref, ck_ref, sa_ref, sb_ref, qo_ref, ko_ref, vo_ref):
        cq, ck, sa_v, sb_v = cq_ref[...], ck_ref[...], sa_ref[...], sb_ref[...]
        kpe = _rope(kpe_ref[...], ck, sa_v, sb_v)
        for h in range(8):
            cols = slice(128 * h, 128 * h + 128)
            qo_ref[:, cols] = _rope(q_ref[:, cols], cq, sa_v, sb_v).astype(qo_ref.dtype)
            ko_ref[:, cols] = (k_ref[:, cols] + kpe).astype(ko_ref.dtype)
        vo_ref[...] = v_ref[...].astype(vo_ref.dtype)

    tab = _row_spec(tr, 128)
    return pl.pallas_call(
        body, name="rope_fwd", grid=(S // tr,),
        out_shape=(jax.ShapeDtypeStruct((S, 1024), _MXU_DTYPE), jax.ShapeDtypeStruct((S, 1024), _MXU_DTYPE),
                   jax.ShapeDtypeStruct((S, 512), _MXU_DTYPE)),
        in_specs=[_row_spec(tr, 1024), pl.BlockSpec((tr, 1024), lambda i: (i, 0)), pl.BlockSpec((tr, 512), lambda i: (i, 2)),
                  pl.BlockSpec((tr, 128), lambda i: (i, 3)), tab, tab, tab, tab],
        out_specs=(_row_spec(tr, 1024), _row_spec(tr, 1024), _row_spec(tr, 512)),
        compiler_params=_cp("parallel"))(qraw, kvall, kvall, z, cosq, cosk, sa, sb)


def _rope_bwd(dq, dk, dv, cosq, cosk, sa, sb, tr=256):
    S = dq.shape[0]

    def body(dq_ref, dk_ref, dv_ref, cq_ref, ck_ref, sa_ref, sb_ref, dqo_ref, dkv_ref, dkpe_ref):
        cq, ck, sa_v, sb_v = cq_ref[...], ck_ref[...], sa_ref[...], sb_ref[...]
        tot = jnp.zeros((tr, 128), F32)
        for h in range(8):
            cols = slice(128 * h, 128 * h + 128)
            dqo_ref[:, cols] = _rope_t(dq_ref[:, cols], cq, sa_v, sb_v).astype(dqo_ref.dtype)
            dkh = dk_ref[:, cols]
            tot = tot + dkh
            dkv_ref[:, cols] = dkh.astype(dkv_ref.dtype)
        dkv_ref[:, 1024:1536] = dv_ref[...].astype(dkv_ref.dtype)
        dkpe_ref[...] = _rope_t(tot, ck, sa_v, sb_v)

    tab = _row_spec(tr, 128)
    return pl.pallas_call(
        body, name="rope_bwd", grid=(S // tr,),
        out_shape=(jax.ShapeDtypeStruct((S, 1024), _MXU_DTYPE), jax.ShapeDtypeStruct((S, 1536), _MXU_DTYPE),
                   jax.ShapeDtypeStruct((S, 128), F32)),
        in_specs=[_row_spec(tr, 1024), _row_spec(tr, 1024), _row_spec(tr, 512), tab, tab, tab, tab],
        out_specs=(_row_spec(tr, 1024), _row_spec(tr, 1536), _row_spec(tr, 128)),
        compiler_params=_cp("parallel"))(dq, dk, dv, cosq, cosk, sa, sb)


NEG = -1e30


def _attn_fwd(q, k, v, tq=256, tk=256):
    S = q.shape[0]
    assert tq == tk

    def body(q_ref, k_ref, v_ref, o_ref, lse_ref):
        i = pl.program_id(1)
        qs = [q_ref[:, 0:128], q_ref[:, 128:256]]

        def step(kb, carry, diagonal=False):
            start = pl.multiple_of(kb * tk, tk)
            vv = v_ref[pl.ds(start, tk), :]
            out = []
            for h in range(2):
                m, l, acc = carry[3 * h:3 * h + 3]
                s = _dot(qs[h], k_ref[pl.ds(start, tk), 128 * h:128 * h + 128], "nt") * ATTN_SCALE
                if diagonal:
                    s = jnp.where(below, s, NEG)
                m_new = jnp.maximum(m, jnp.max(s, axis=-1, keepdims=True))
                alpha = jnp.exp(m - m_new)
                p = jnp.exp(s - m_new)
                out += [m_new, alpha * l + jnp.sum(p, axis=-1, keepdims=True), alpha * acc + _dot(p, vv, "nn")]
            return tuple(out)

        below = lax.broadcasted_iota(jnp.int32, (tq, tk), 1) <= lax.broadcasted_iota(jnp.int32, (tq, tk), 0)
        init = (jnp.full((tq, 1), NEG, F32), jnp.zeros((tq, 1), F32), jnp.zeros((tq, 128), F32)) * 2
        ma, la, acca, mb, lb, accb = step(i, lax.fori_loop(0, i, step, init), diagonal=True)
        lane = lax.broadcasted_iota(jnp.int32, (tq, 128), 1)
        o_ref[...] = jnp.where(lane < 64, acca / la, accb / lb)
        lse_ref[...] = jnp.where(lane < 64, ma + jnp.log(la), mb + jnp.log(lb))

    return pl.pallas_call(
        body, name="attn_fwd", grid=(4, S // tq),
        out_shape=(jax.ShapeDtypeStruct((S, 512), F32), jax.ShapeDtypeStruct((4, S, 128), F32)),
        in_specs=[pl.BlockSpec((tq, 256), lambda p, i: (i, p)), pl.BlockSpec((S, 256), lambda p, i: (0, p)),
                  pl.BlockSpec((S, 128), lambda p, i: (0, p))],
        out_specs=(pl.BlockSpec((tq, 128), lambda p, i: (i, p)), pl.BlockSpec((None, tq, 128), lambda p, i: (p, i, 0))),
        compiler_params=_cp("parallel", "parallel"))(q, k, v)


def _attn_bwd(q, k, v, o, lse, dycat2, tq=256, tk=256):
    S = q.shape[0]
    assert tq == tk

    def body(q_ref, k_ref, v_ref, o_ref, lse_ref, do_ref, dq_ref, dk_ref, dv_ref):
        j = pl.program_id(1)

        @pl.when(j == 0)
        def _():
            dq_ref[...] = jnp.zeros_like(dq_ref)

        below = lax.broadcasted_iota(jnp.int32, (tq, tk), 1) <= lax.broadcasted_iota(jnp.int32, (tq, tk), 0)
        lane = lax.broadcasted_iota(jnp.int32, (tq, 128), 1)
        ks = [k_ref[:, 0:128], k_ref[:, 128:256]]
        vv = v_ref[...]

        def step(qb, carry, diagonal=False):
            dka, dkb, dvp = carry
            start = pl.multiple_of(qb * tq, tq)
            rows = pl.ds(start, tq)
            do, lse_v = do_ref[rows, :], lse_ref[rows, :]
            prod = do * o_ref[rows, :]
            dks = [dka, dkb]
            for h in range(2):
                mine = (lane < 64) if h == 0 else (lane >= 64)
                delta = jnp.sum(jnp.where(mine, prod, 0.0), axis=-1, keepdims=True)
                do_h = jnp.where(mine, do, 0.0)
                qh = q_ref[rows, 128 * h:128 * h + 128]
                s = _dot(qh, ks[h], "nt") * ATTN_SCALE
                p = jnp.exp(s - lse_v[:, 64 * h:64 * h + 1])
                if diagonal:
                    p = jnp.where(below, p, 0.0)
                dvp = dvp + _dot(p, do_h, "tn")
                ds = p * (_dot(do_h, vv, "nt") - delta) * ATTN_SCALE
                dq_ref[rows, 128 * h:128 * h + 128] += _dot(ds, ks[h], "nn")
                dks[h] = dks[h] + _dot(ds, qh, "tn")
            return dks[0], dks[1], dvp

        zero = jnp.zeros((tk, 128), F32)
        dka, dkb, dvp = lax.fori_loop(j + 1, S // tq, step, step(j, (zero, zero, zero), diagonal=True))
        dk_ref[:, 0:128] = dka
        dk_ref[:, 128:256] = dkb
        dv_ref[...] = dvp

    return pl.pallas_call(
        body, name="attn_bwd", grid=(4, S // tk),
        out_shape=(jax.ShapeDtypeStruct((S, 1024), F32), jax.ShapeDtypeStruct((S, 1024), F32), jax.ShapeDtypeStruct((S, 512), F32)),
        in_specs=[pl.BlockSpec((S, 256), lambda p, j: (0, p)), pl.BlockSpec((tk, 256), lambda p, j: (j, p)),
                  pl.BlockSpec((tk, 128), lambda p, j: (j, p)), pl.BlockSpec((S, 128), lambda p, j: (0, p)),
                  pl.BlockSpec((None, S, 128), lambda p, j: (p, 0, 0)), pl.BlockSpec((None, S, 128), lambda p, j: (0, 0, p))],
        out_specs=(pl.BlockSpec((S, 256), lambda p, j: (0, p)), pl.BlockSpec((tk, 256), lambda p, j: (j, p)),
                   pl.BlockSpec((tk, 128), lambda p, j: (j, p))),
        compiler_params=_cp("parallel", "arbitrary"))(q, k, v, o, lse, dycat2)


CHUNK = 128
GELU_C = math.sqrt(2.0 / math.pi)


def _gelu(v):
    t = jnp.tanh(GELU_C * (v + 0.044715 * (v * v * v)))
    return v * (0.5 * (1.0 + t)), t


def _gelu_grad(v, t):
    return 0.5 * (1.0 + t) + v * (0.5 * (1.0 - t * t) * GELU_C * (1.0 + 3.0 * 0.044715 * v * v))


def _tril(w):
    r = lax.broadcasted_iota(jnp.int32, w.shape, 0)
    c = lax.broadcasted_iota(jnp.int32, w.shape, 1)
    return jnp.where(c <= r, w, 0.0)


def _layer_norm(v, g, b):
    xc = v - jnp.mean(v, axis=-1, keepdims=True)
    rstd = lax.rsqrt(jnp.mean(xc * xc, axis=-1, keepdims=True) + EPS)
    xhat = xc * rstd
    return xhat * g + b, xhat, rstd


def _sgu_fwd(z, o, ln_g, ln_b, w_s, b_st, tr=256):
    S = z.shape[0]

    def body(u_ref, v_ref, o_ref, g_ref, b_ref, ws_ref, bs_ref, y_ref):
        gu, _ = _gelu(u_ref[...])
        gv, _ = _gelu(v_ref[...])
        vln, _, _ = _layer_norm(gv, g_ref[...], b_ref[...])
        y_ref[0] = o_ref[...].astype(y_ref.dtype)
        for g in range(4):
            wt = _tril(ws_ref[g])
            cols = slice(128 * g, 128 * g + 128)
            for ch in range(tr // CHUNK):
                rows = slice(CHUNK * ch, CHUNK * ch + CHUNK)
                mixed = _dot(wt, vln[rows, cols], "nn") + bs_ref[:, g:g + 1]
                y_ref[1, rows, cols] = (gu[rows, cols] * mixed).astype(y_ref.dtype)

    return pl.pallas_call(
        body, name="sgu_fwd", grid=(S // tr,), out_shape=jax.ShapeDtypeStruct((2, S, 512), _MXU_DTYPE),
        in_specs=[pl.BlockSpec((tr, 512), lambda i: (i, 1)), pl.BlockSpec((tr, 512), lambda i: (i, 2)), _row_spec(tr, 512),
                  _vec_spec(512), _vec_spec(512), pl.BlockSpec((4, 128, 128), lambda i: (0, 0, 0)), pl.BlockSpec((128, 4), lambda i: (0, 0))],
        out_specs=pl.BlockSpec((2, tr, 512), lambda i: (0, i, 0)), compiler_params=_cp("parallel"))(z, z, o, ln_g, ln_b, w_s, b_st)


def _sgu_bwd(z, dycat2, ln_g, ln_b, w_s, b_st, tr=256):
    S = z.shape[0]

    def body(u_ref, v_ref, dy_ref, g_ref, b_ref, ws_ref, bs_ref, duv_ref, dg_ref, db_ref, dws_ref, dbs_ref):
        first = pl.program_id(0) == 0
        u_pre, v_pre = u_ref[...], v_ref[...]
        gu, tu = _gelu(u_pre)
        gv, tv = _gelu(v_pre)
        gain = g_ref[...]
        vln, xhat, rstd = _layer_norm(gv, gain, b_ref[...])

        @pl.when(first)
        def _():
            dws_ref[...] = jnp.zeros_like(dws_ref)
            dbs_ref[...] = jnp.zeros_like(dbs_ref)

        dvln_cols = []
        for g in range(4):
            wt = _tril(ws_ref[g])
            cols = slice(128 * g, 128 * g + 128)
            dmixed_sum = jnp.zeros((CHUNK, 128), F32)
            dw = jnp.zeros((CHUNK, CHUNK), F32)
            dvln_rows = []
            for ch in range(tr // CHUNK):
                rows = slice(CHUNK * ch, CHUNK * ch + CHUNK)
                vt = vln[rows, cols]
                mixed = _dot(wt, vt, "nn") + bs_ref[:, g:g + 1]
                dyd = dy_ref[rows, cols]
                duv_ref[rows, cols] = (dyd * mixed * _gelu_grad(u_pre[rows, cols], tu[rows, cols])).astype(duv_ref.dtype)
                dmixed = dyd * gu[rows, cols]
                dmixed_sum = dmixed_sum + dmixed
                dw = dw + _dot(dmixed, vt, "nt")
                dvln_rows.append(_dot(wt, dmixed, "tn"))
            dws_ref[g] += _tril(dw)
            dbs_ref[g:g + 1, :] += jnp.sum(dmixed_sum.T, axis=0, keepdims=True)
            dvln_cols.append(jnp.concatenate(dvln_rows, axis=0))
        dvln = jnp.concatenate(dvln_cols, axis=1)
        _acc_rows(dg_ref, dvln * xhat, first)
        _acc_rows(db_ref, dvln, first)
        dxhat = dvln * gain
        dgv = rstd * (dxhat - jnp.mean(dxhat, axis=-1, keepdims=True) - xhat * jnp.mean(dxhat * xhat, axis=-1, keepdims=True))
        duv_ref[:, 512:1024] = (dgv * _gelu_grad(v_pre, tv)).astype(duv_ref.dtype)

    return pl.pallas_call(
        body, name="sgu_bwd", grid=(S // tr,),
        out_shape=(jax.ShapeDtypeStruct((S, 1024), _MXU_DTYPE), jax.ShapeDtypeStruct((1, 512), F32), jax.ShapeDtypeStruct((1, 512), F32),
                   jax.ShapeDtypeStruct((4, 128, 128), F32), jax.ShapeDtypeStruct((4, 128), F32)),
        in_specs=[pl.BlockSpec((tr, 512), lambda i: (i, 1)), pl.BlockSpec((tr, 512), lambda i: (i, 2)),
                  pl.BlockSpec((None, tr, 512), lambda i: (1, i, 0)), _vec_spec(512), _vec_spec(512),
                  pl.BlockSpec((4, 128, 128), lambda i: (0, 0, 0)), pl.BlockSpec((128, 4), lambda i: (0, 0))],
        out_specs=(_row_spec(tr, 1024), _vec_spec(512), _vec_spec(512), pl.BlockSpec((4, 128, 128), lambda i: (0, 0, 0)),
                   pl.BlockSpec((4, 128), lambda i: (0, 0))),
        compiler_params=_cp("arbitrary"))(z, z, dycat2, ln_g, ln_b, w_s, b_st)


def _sum_parts(name, parts, tr=512):
    P, R, C = parts.shape
    tr = _tile(R, tr) if R % 8 == 0 else R

    def body(p_ref, o_ref):
        g = p_ref[0]
        for k in range(1, P):
            g = g + p_ref[k]
        o_ref[...] = g

    return pl.pallas_call(
        body, name=name, grid=(R // tr,), out_shape=jax.ShapeDtypeStruct((R, C), F32),
        in_specs=[pl.BlockSpec((P, tr, C), lambda i: (0, i, 0))], out_specs=_row_spec(tr, C),
        compiler_params=_cp("parallel"))(parts)


def _adamw_math(w, m, v, g):
    c1 = 1.0 / (1.0 - ADAM_B1 ** ADAM_STEP)
    c2 = 1.0 / (1.0 - ADAM_B2 ** ADAM_STEP)
    m2 = ADAM_B1 * m + (1.0 - ADAM_B1) * g
    v2 = ADAM_B2 * v + (1.0 - ADAM_B2) * (g * g)
    return -ADAM_LR * ((m2 * c1) / (jnp.sqrt(v2 * c2) + ADAM_EPS) + ADAM_WD * w), m2, v2


def _adamw_small(name, params, parts):
    n = len(params)

    def body(*refs):
        ins, outs = refs[:4 * n], refs[4 * n:]
        for i in range(n):
            w_ref, m_ref, v_ref, p_ref = ins[4 * i:4 * i + 4]
            g = p_ref[0]
            for k in range(1, N_DEV):
                g = g + p_ref[k]
            delta, m2, v2 = _adamw_math(w_ref[...], m_ref[...], v_ref[...], g)
            outs[4 * i][...] = g
            outs[4 * i + 1][...] = delta
            outs[4 * i + 2][...] = m2
            outs[4 * i + 3][...] = v2

    flat = [a for (w, m, v), p in zip(params, parts) for a in (w, m, v, p)]
    out = pl.pallas_call(
        body, name=name, out_shape=[jax.ShapeDtypeStruct(w.shape, F32) for (w, _, _) in params for _ in range(4)],
        compiler_params=pltpu.CompilerParams(vmem_limit_bytes=_VMEM_LIMIT))(*flat)
    return [out[4 * i:4 * i + 4] for i in range(n)]


ADAMW_BLOCK_BYTES = 36 * 2 ** 20


def _adamw(name, w, m, v, parts):
    L, R, C = w.shape
    P = parts[0].shape[0]
    row_bytes = 2 * C * (7 * 4 + P * parts[0].dtype.itemsize)
    tr = R
    if R * row_bytes > ADAMW_BLOCK_BYTES:
        tr = next(t for t in (1024, 512, 256, 128, 64, 32, 16) if R % t == 0 and t * row_bytes <= ADAMW_BLOCK_BYTES)
    nr = R // tr
    c1 = 1.0 / (1.0 - ADAM_B1 ** ADAM_STEP)
    c2 = 1.0 / (1.0 - ADAM_B2 ** ADAM_STEP)

    def body(w_ref, m_ref, v_ref, *rest):
        p_refs, (g_ref, d_ref, mo_ref, vo_ref) = rest[:L], rest[L:]
        for ll in range(L):
            @pl.when(pl.program_id(0) == ll)
            def _(p_ref=p_refs[ll]):
                g = p_ref[0].astype(F32)
                for k in range(1, P):
                    g = g + p_ref[k].astype(F32)
                m2 = ADAM_B1 * m_ref[...] + (1.0 - ADAM_B1) * g
                v2 = ADAM_B2 * v_ref[...] + (1.0 - ADAM_B2) * (g * g)
                g_ref[...] = g
                mo_ref[...] = m2
                vo_ref[...] = v2
                d_ref[...] = -ADAM_LR * ((m2 * c1) / (jnp.sqrt(v2 * c2) + ADAM_EPS) + ADAM_WD * w_ref[...])

    def part_spec(ll):
        return pl.BlockSpec((P, tr, C), lambda l, i: (0, jnp.where(l == ll, i, jnp.where(l < ll, 0, nr - 1)), 0))

    full = pl.BlockSpec((None, tr, C), lambda l, i: (l, i, 0))
    sds = jax.ShapeDtypeStruct((L, R, C), F32)
    return pl.pallas_call(
        body, name=name, grid=(L, nr), out_shape=(sds, sds, sds, sds),
        in_specs=[full] * 3 + [part_spec(ll) for ll in range(L)],
        out_specs=(full,) * 4, compiler_params=_cp("arbitrary", "arbitrary"))(w, m, v, *parts)


def _rope_tables(positions):
    half = 16
    inv_freq = 10000.0 ** (-jnp.arange(half, dtype=F32) / half)
    ang = positions.astype(F32)[:, None] * inv_freq
    cos, sin = jnp.cos(ang), jnp.sin(ang)
    S = positions.shape[0]
    z16, z32, z64 = jnp.zeros((S, 16), F32), jnp.zeros((S, 32), F32), jnp.zeros((S, 64), F32)
    cosk = jnp.concatenate([z64, cos, cos, z32], axis=1)
    cosq = jnp.concatenate([jnp.ones((S, 64), F32), cos, cos, z32], axis=1)
    sa = jnp.concatenate([z64, -sin, z16, z32], axis=1)
    sb = jnp.concatenate([z64, z16, sin, z32], axis=1)
    return cosq, cosk, sa, sb


def _ffn_fwd(l, x, mod, n2g, get_w_up8, cw24, get_w_down4):
    sh, sc, gate = mod
    h = _rmsmod_fwd(f"ffn{l}_norm", x, n2g, sc, sh, n2g)
    w_up8 = get_w_up8(h)
    u8 = _mm_cols(f"ffn{l}_up", h, w_up8, out_dtype=ACT_DTYPE, tm=1024)
    S, n = u8.shape[1], u8.shape[2]
    u24 = u8.reshape(2, 4, S, n)
    a4 = _ffn_gate_fwd(f"ffn{l}_gate", u24, cw24)
    w_down4 = get_w_down4(a4)
    f, x_new = _mm_rows_resid(f"ffn{l}_down", a4, w_down4, x, gate)
    return x_new, (x, h, u24, a4, f), w_up8, w_down4


def _ffn_bwd(l, dx, saved, mod, n2g, w_up8, cw24, w_down4, me):
    sh, sc, gate = mod
    x, h, u24, a4, f = saved
    df, dgate = _gate_bwd(f"ffn{l}_gate_bwd", dx, f, gate)
    da4 = _mm_rows_dx(f"ffn{l}_down_dx", df, w_down4, out_dtype=ACT_DTYPE, tm=2048)
    dw_down4 = _mm_rows_dw(f"ffn{l}_down_dw", a4, df, out_dtype=WIRE_DTYPE)
    sent_down, token = _exchange_start(f"scatter_ffn{l}_down", [dw_down4.reshape(8, 352, dw_down4.shape[2])], True, dgate, me)
    du24, dcw24 = _ffn_gate_bwd(f"ffn{l}_act_bwd", u24, cw24, da4, token)
    du8 = du24.reshape((8,) + du24.shape[2:])
    dw_up8t = _mm_cols_dwt(f"ffn{l}_up_dw", h, du8, out_dtype=WIRE_DTYPE, tk=1024)
    sent_up, token = _exchange_start(f"scatter_ffn{l}_up", [dw_up8t], True, dcw24, me)
    dh = _mm_cols_dx(f"ffn{l}_up_dx", du8, w_up8, tm=1024, jb=4)
    dx_new, dn2g, dsc, dsh = _rmsmod_bwd(f"ffn{l}_norm_bwd", x, n2g, sc, dh, dx, token)
    return dx_new, dict(sent_up=sent_up, sent_down=sent_down, cw24=dcw24, n2g=dn2g, mod=(dsh, dsc, dgate))


def kernel(x, c, positions, ada_w, ada_b, norm1_g, norm2_g, ab_w_in, a_conv_w, b_mix_w, b_scale, ab_w_out, cd_w_in, c_q_norm_g, c_w_uq, c_kv_norm_g, c_w_ukv, d_ln_g, d_ln_b, d_w_s, d_b_s, cd_w_out, ffn_w_up, ffn_conv_w, ffn_w_down, final_norm_g, loss_target, m_ada_w, m_ada_b, m_norm1_g, m_norm2_g, m_ab_w_in, m_a_conv_w, m_b_mix_w, m_b_scale, m_ab_w_out, m_cd_w_in, m_c_q_norm_g, m_c_w_uq, m_c_kv_norm_g, m_c_w_ukv, m_d_ln_g, m_d_ln_b, m_d_w_s, m_d_b_s, m_cd_w_out, m_ffn_w_up, m_ffn_conv_w, m_ffn_w_down, m_final_norm_g, v_ada_w, v_ada_b, v_norm1_g, v_norm2_g, v_ab_w_in, v_a_conv_w, v_b_mix_w, v_b_scale, v_ab_w_out, v_cd_w_in, v_c_q_norm_g, v_c_w_uq, v_c_kv_norm_g, v_c_w_ukv, v_d_ln_g, v_d_ln_b, v_d_w_s, v_d_b_s, v_cd_w_out, v_ffn_w_up, v_ffn_conv_w, v_ffn_w_down, v_final_norm_g):
    S, D = x.shape[1], x.shape[2]
    me = 4 * lax.axis_index("x") + 2 * lax.axis_index("y") + lax.axis_index("c")
    x0, target = x[0], loss_target[0]
    W = _MXU_DTYPE

    small_shapes = [(1024,), (3, 64), (32,), (64,), (64,), (2, 3, 704)]
    (g0,) = _exchange("gather_small", [[_pack([c, a_conv_w, c_q_norm_g, d_ln_g, d_ln_b, ffn_conv_w])]], scatter=False)
    c_all, aconv_s, qg_s, lng_s, lnb_s, fcw_s = _unpack(g0[:, 0], small_shapes, lead=(N_DEV,))
    conv_w = aconv_s.transpose(1, 0, 2).reshape(3, 512)
    qg, ln_g, ln_b = qg_s.reshape(1, 256), lng_s.reshape(1, 512), lnb_s.reshape(1, 512)
    cw24 = [fcw_s[:, l].reshape(2, 4, 3, 704) for l in range(2)]
    c16 = jnp.pad(c_all, ((0, 16 - N_DEV), (0, 0)))

    mod_cols = _ada_fwd(c16, ada_w)
    (g1,) = _exchange("gather_mod", [[_pack([mod_cols])]], scatter=False)
    mod_all = _unpack(g1[:, 0], [(2, 16, 768)], lead=(N_DEV,))[0]
    mod_mine = lax.dynamic_index_in_dim(mod_all, me, axis=2, keepdims=False)
    mod = mod_mine.transpose(1, 0, 2).reshape(2, 6 * D) + ada_b
    mods = [[mod[l, k * D:(k + 1) * D].reshape(1, D) for k in range(6)] for l in range(2)]

    gw_ab, token = _hier_gather_start("gather_w_ab", [ab_w_in[0].astype(W), ab_w_out[0].astype(W)], mod, me)
    gw_up0, token = _hier_gather_start("gather_w_ffn0_up", [ffn_w_up[0].astype(W)], token, me)
    gw_rest, started = _exchange_start("gather_w_rest", [
        ffn_w_down[0].astype(W), cd_w_in[0].T.astype(W), c_w_uq[0].T.astype(W), c_w_ukv[0].astype(W), cd_w_out[0].astype(W),
        ffn_w_up[1].astype(W), ffn_w_down[1].astype(W)], False, token, me)

    cosq, cosk, sa, sb = _rope_tables(positions[0])
    n1g = [norm1_g[l].reshape(1, D) for l in range(2)]
    n2g = [norm2_g[l].reshape(1, D) for l in range(2)]
    mix_w, scale = b_mix_w[0], b_scale
    kvg = c_kv_norm_g
    w_s, b_st = d_w_s[0], d_b_s[0].T

    sh1, sc1, g1m = mods[0][:3]
    h_ab = _rmsmod_fwd("ab_norm", x0, n1g[0], sc1, sh1, started)
    w_abin8, w_about = _hier_gather_wait("wait_w_ab", _hier_gather_forward("forward_w_ab", gw_ab, h_ab), h_ab)
    w_about2 = w_about.reshape(2, 512, D)
    z8 = _mm_cols("ab_in", h_ab, w_abin8, tm=2048)
    ycat_ab = _ab_mix_fwd(z8, conv_w, mix_w, scale)
    y_ab, x1 = _mm_rows_resid("ab_out", ycat_ab, w_about2, x0, g1m)
    w_up8, w_down4 = [None, None], [None, None]
    gw_up0 = _hier_gather_forward("forward_w_ffn0_up", gw_up0, x1)
    x2, ffn0_saved, w_up8[0], w_down4[0] = _ffn_fwd(
        0, x1, mods[0][3:], n2g[0], lambda after: _hier_gather_wait("wait_w_ffn0_up", gw_up0, after)[0], cw24[0],
        lambda after: _exchange_wait("wait_w_ffn0_down", gw_rest, after, [0])[0].reshape(4, 704, D))

    w_cdin, w_uq, w_ukv, w_cdout = _exchange_wait("wait_w_cd", gw_rest, x2, [1, 2, 3, 4])
    w_cdout2 = w_cdout.reshape(2, 512, D)
    w_cd_t = w_cdin.reshape(1440, D)
    zr = lambda n: jnp.zeros((n, D), W)
    w_cd_pad = jnp.concatenate([w_cd_t[:384], zr(64), w_cd_t[384:416], zr(32), w_cd_t[416:]], axis=0)
    w_uq_pad = jnp.pad(w_uq, ((0, 0), (0, 32), (0, 0))).reshape(1024, 256)
    w_ukv_h = w_ukv.transpose(1, 0, 2)
    w_k_pad = jnp.pad(w_ukv_h[:, :, :64], ((0, 0), (0, 0), (0, 64))).reshape(128, 1024)
    w_kv_pad = jnp.concatenate([w_k_pad, w_ukv_h[:, :, 64:].reshape(128, 512)], axis=1)

    sh1, sc1, g1c = mods[1][:3]
    h_cd = _rmsmod_fwd("cd_norm", x2, n1g[1], sc1, sh1, n1g[1])
    z_cd = _mm_nt("cd_in", h_cd, w_cd_pad)
    qn, kvn = _mla_prep_fwd(z_cd, qg, kvg)
    qraw = _mm_nt("cd_uq", qn, w_uq_pad)
    kvall = _mm_nn("cd_ukv", kvn, w_kv_pad)
    q_r, k_r, v_r = _rope_fwd(qraw, kvall, z_cd, cosq, cosk, sa, sb)
    o, lse = _attn_fwd(q_r, k_r, v_r)
    ycat_cd = _sgu_fwd(z_cd, o, ln_g, ln_b, w_s, b_st)
    y_cd, x3 = _mm_rows_resid("cd_out", ycat_cd, w_cdout2, x2, g1c)
    x4, ffn1_saved, w_up8[1], w_down4[1] = _ffn_fwd(
        1, x3, mods[1][3:], n2g[1], lambda after: _exchange_wait("wait_w_ffn1_up", gw_rest, after, [5])[0], cw24[1],
        lambda after: _exchange_wait("wait_w_ffn1_down", gw_rest, after, [6])[0].reshape(4, 704, D))

    loss_local, dx4, dfg = _loss_head(x4, final_norm_g.reshape(1, D), target)

    dx3, gf1 = _ffn_bwd(1, dx4, ffn1_saved, mods[1][3:], n2g[1], w_up8[1], cw24[1], w_down4[1], me)

    dy, dg1c = _gate_bwd("cd_gate_bwd", dx3, y_cd, g1c)
    dycat = _mm_rows_dx("cd_out_dx", dy, w_cdout2)
    dw_cdout = _mm_rows_dw("cd_out_dw", ycat_cd, dy, out_dtype=WIRE_DTYPE)
    duv, dln_g, dln_b, dws, dbs = _sgu_bwd(z_cd, dycat, ln_g, ln_b, w_s, b_st)
    dq_r, dk_r, dv_r = _attn_bwd(q_r, k_r, v_r, o, lse, dycat)
    dqraw, dkvall, dkpe = _rope_bwd(dq_r, dk_r, dv_r, cosq, cosk, sa, sb)
    dqn = _mm_nn("cd_uq_dx", dqraw, w_uq_pad, tn=256)
    dkvn = _mm_nt("cd_ukv_dx", dkvall, w_kv_pad, tn=128)
    dw_uq_pad = _mm_tn("cd_uq_dw", dqraw, qn, tn=256)
    dw_kv_pad = _mm_tn("cd_ukv_dw", kvn, dkvall, tm=128)
    dz_cd, dqg, dkvg = _mla_prep_bwd(z_cd, qg, kvg, dqn, dkvn, dkpe, duv)
    dh_cd = _mm_nn("cd_in_dx", dz_cd, w_cd_pad)
    dw_cd_pad = _mm_tn("cd_in_dw", dz_cd, h_cd)
    dw_cd8 = jnp.concatenate([dw_cd_pad[:384], dw_cd_pad[448:480], dw_cd_pad[512:]], axis=0).astype(WIRE_DTYPE).reshape(8, 180, D)
    dw_uq8 = dw_uq_pad.reshape(8, 128, 256)[:, :96].astype(WIRE_DTYPE)
    dw_ukv8 = jnp.concatenate([dw_kv_pad[:, :1024].reshape(128, 8, 128)[:, :, :64], dw_kv_pad[:, 1024:].reshape(128, 8, 64)],
                              axis=2).transpose(1, 0, 2).astype(WIRE_DTYPE)
    sent_cd, token = _exchange_start("scatter_cd", [dw_cd8, dw_uq8, dw_ukv8, dw_cdout.reshape(8, 128, D)], True, dqg, me)
    early_names = ["c_kv_norm_g", "d_w_s", "d_b_s", "final_norm_g", "c_q_norm_g", "d_ln_g", "d_ln_b"]
    early_grads = [dkvg, dws.reshape(512, 128), dbs, dfg, dqg.reshape(8, 1, 32), dln_g.reshape(8, 1, 64), dln_b.reshape(8, 1, 64)]
    early_sent, token = _exchange_start("gather_small_grads_early", early_grads, [False] * 4 + [True] * 3, token, me)
    dx2, dn1g_cd, dsc1_cd, dsh1_cd = _rmsmod_bwd("cd_norm_bwd", x2, n1g[1], sc1, dh_cd, dx3, token)

    dx1, gf0 = _ffn_bwd(0, dx2, ffn0_saved, mods[0][3:], n2g[0], w_up8[0], cw24[0], w_down4[0], me)

    dy, dg1m = _gate_bwd("ab_gate_bwd", dx1, y_ab, g1m)
    dw_about = _mm_rows_dw("ab_out_dw", ycat_ab, dy, out_dtype=WIRE_DTYPE)
    sent_about, token = _exchange_start("scatter_ab_out", [dw_about.reshape(8, 128, D)], True, dg1m, me)
    dycat = _mm_rows_dx("ab_out_dx", dy, w_about2)
    dz8, dconv_w, dmix_w, dscale = _ab_mix_bwd(z8, dycat, conv_w, mix_w, scale, token)
    dz8 = dz8.reshape(8, S, 256)
    dw_abin8 = _mm_cols_dw("ab_in_dw", h_ab, dz8, out_dtype=WIRE_DTYPE, tk=1024)
    sent_abin, token = _exchange_start("scatter_ab_in", [dw_abin8], True, dscale, me)
    dh_ab = _mm_cols_dx("ab_in_dx", dz8, w_abin8)
    dx0, dn1g_ab, dsc1_ab, dsh1_ab = _rmsmod_bwd("ab_norm_bwd", x0, n1g[0], mods[0][1], dh_ab, dx1, token)

    dmod = jnp.stack([jnp.concatenate([dsh1_ab, dsc1_ab, dg1m, *gf0["mod"]], axis=1)[0],
                      jnp.concatenate([dsh1_cd, dsc1_cd, dg1c, *gf1["mod"]], axis=1)[0]])
    late_names = ["ada_b", "norm1_g", "norm2_g", "b_mix_w", "b_scale", "a_conv_w", "ffn_conv_w"]
    late_grads = [dmod, jnp.concatenate([dn1g_ab, dn1g_cd]), jnp.concatenate([gf0["n2g"], gf1["n2g"]]),
                  dmix_w.reshape(512, 128), dscale, dconv_w.reshape(3, 8, 64).transpose(1, 0, 2),
                  jnp.stack([gf0["cw24"].reshape(8, 3, 704), gf1["cw24"].reshape(8, 3, 704)], axis=1)]
    small_view = dict(ada_b=(2, 6 * D), norm1_g=(2, D), norm2_g=(2, D), b_mix_w=(512, 128), b_scale=(1, 512), c_kv_norm_g=(1, 128),
                      d_w_s=(512, 128), d_b_s=(4, 128), final_norm_g=(1, D),
                      a_conv_w=(3, 64), c_q_norm_g=(1, 32), d_ln_g=(1, 64), d_ln_b=(1, 64), ffn_conv_w=(2, 3, 704))
    late_sent, token = _exchange_start("gather_small_grads_late", late_grads, [False] * 5 + [True] * 2, dx0, me)

    res = {}

    def update(name, w, m, v, parts, shape3d):
        outs = _adamw("adamw_" + name, w.reshape(shape3d), m.reshape(shape3d), v.reshape(shape3d),
                      [p.reshape((p.shape[0],) + shape3d[1:]) for p in parts])
        res[name] = [o_.reshape(w.shape) for o_ in outs]

    p_cdin, p_uq, p_ukv, p_cdout = _exchange_wait("wait_scatter_cd", sent_cd, token)
    swap = lambda a: jnp.swapaxes(a, 1, 2)
    update("cd_w_in", swap(cd_w_in), swap(m_cd_w_in), swap(v_cd_w_in), [p_cdin], (1, 180, D))
    update("c_w_uq", swap(c_w_uq), swap(m_c_w_uq), swap(v_c_w_uq), [p_uq], (1, 96, 256))
    for name in ("cd_w_in", "c_w_uq"):
        res[name] = [swap(o_) for o_ in res[name]]
    update("c_w_ukv", c_w_ukv, m_c_w_ukv, v_c_w_ukv, [p_ukv], (1, 128, 128))
    update("cd_w_out", cd_w_out, m_cd_w_out, v_cd_w_out, [p_cdout], (1, 128, D))
    (p_dn1,) = _exchange_wait("wait_scatter_ffn1_down", gf1["sent_down"], token)
    (p_dn0,) = _exchange_wait("wait_scatter_ffn0_down", gf0["sent_down"], res["cd_w_out"][0])
    update("ffn_w_down", ffn_w_down, m_ffn_w_down, v_ffn_w_down, [p_dn0, p_dn1], (2, 352, D))
    (p_up1,) = _exchange_wait("wait_scatter_ffn1_up", gf1["sent_up"], token)
    (p_up0,) = _exchange_wait("wait_scatter_ffn0_up", gf0["sent_up"], res["ffn_w_down"][0])
    swap = lambda a: jnp.swapaxes(a, 1, 2)
    update("ffn_w_up", swap(ffn_w_up), swap(m_ffn_w_up), swap(v_ffn_w_up), [p_up0, p_up1], (2, 704, D))
    up_done = res["ffn_w_up"][0]
    res["ffn_w_up"] = [swap(o_) for o_ in res["ffn_w_up"]]
    (p_about,) = _exchange_wait("wait_scatter_ab_out", sent_about, up_done)
    update("ab_w_out", ab_w_out, m_ab_w_out, v_ab_w_out, [p_about], (1, 128, D))
    (p_abin,) = _exchange_wait("wait_scatter_ab_in", sent_abin, res["ab_w_out"][0])
    update("ab_w_in", ab_w_in, m_ab_w_in, v_ab_w_in, [p_abin], (1, D, 256))

    early_parts = _exchange_wait("wait_small_grads_early", early_sent, res["ab_w_in"][0])
    late_parts = _exchange_wait("wait_small_grads_late", late_sent, res["ab_w_in"][0])
    small_names = early_names + late_names
    small_parts = list(early_parts) + list(late_parts)
    dmod_all = late_parts[0]
    dmod_cols = lax.dynamic_slice_in_dim(dmod_all, me * 768, 768, axis=2).transpose(1, 0, 2)
    g_ada_w = _ada_bwd(c16, jnp.pad(dmod_cols, ((0, 0), (0, 16 - N_DEV), (0, 0))))
    update("ada_w", ada_w, m_ada_w, v_ada_w, [g_ada_w[l][None] for l in range(2)], (2, D, 768))

    small_w = dict(ada_b=(ada_b, m_ada_b, v_ada_b), norm1_g=(norm1_g, m_norm1_g, v_norm1_g), norm2_g=(norm2_g, m_norm2_g, v_norm2_g),
                   b_mix_w=(b_mix_w, m_b_mix_w, v_b_mix_w), b_scale=(b_scale, m_b_scale, v_b_scale),
                   c_kv_norm_g=(c_kv_norm_g, m_c_kv_norm_g, v_c_kv_norm_g), d_w_s=(d_w_s, m_d_w_s, v_d_w_s),
                   d_b_s=(d_b_s, m_d_b_s, v_d_b_s), final_norm_g=(final_norm_g, m_final_norm_g, v_final_norm_g),
                   a_conv_w=(a_conv_w, m_a_conv_w, v_a_conv_w), c_q_norm_g=(c_q_norm_g, m_c_q_norm_g, v_c_q_norm_g),
                   d_ln_g=(d_ln_g, m_d_ln_g, v_d_ln_g), d_ln_b=(d_ln_b, m_d_ln_b, v_d_ln_b),
                   ffn_conv_w=(ffn_conv_w, m_ffn_conv_w, v_ffn_conv_w))
    small_out = _adamw_small("adamw_small", [tuple(a.reshape(small_view[n]) for a in small_w[n]) for n in small_names],
                             list(small_parts))
    for n, outs in zip(small_names, small_out):
        res[n] = [o_.reshape(small_w[n][0].shape) for o_ in outs]

    loss = lax.psum(loss_local[0, 0], ("x", "y", "c"))
    order = ["ada_w", "ada_b", "norm1_g", "norm2_g", "ab_w_in", "a_conv_w", "b_mix_w", "b_scale", "ab_w_out", "cd_w_in", "c_q_norm_g",
             "c_w_uq", "c_kv_norm_g", "c_w_ukv", "d_ln_g", "d_ln_b", "d_w_s", "d_b_s", "cd_w_out", "ffn_w_up", "ffn_conv_w",
             "ffn_w_down", "final_norm_g"]
    return (loss, dx0[None], *[res[n][0] for n in order], *[res[n][1] for n in order], *[res[n][2] for n in order],
            *[res[n][3] for n in order])
```

```python
import functools
import math

import jax
import jax.numpy as jnp
from jax import lax
from jax.experimental import pallas as pl
from jax.experimental.pallas import tpu as pltpu

F32 = jnp.float32
BF16 = jnp.bfloat16
_MXU_DTYPE = BF16
WIRE_DTYPE = BF16
ACT_DTYPE = BF16
_VMEM_LIMIT = 56 * 2 ** 20
N_DEV = 8
EPS = 1e-6
POOL_WINDOWS = (2, 4, 8, 16)
ATTN_SCALE = (64 + 32) ** -0.5
ADAM_LR, ADAM_B1, ADAM_B2, ADAM_EPS, ADAM_WD, ADAM_STEP = 0.001, 0.9, 0.999, 1e-08, 0.01, 10
MESH = pl.DeviceIdType.MESH
ANY = pl.BlockSpec(memory_space=pl.ANY)


def _cp(*sem):
    return pltpu.CompilerParams(dimension_semantics=sem, vmem_limit_bytes=_VMEM_LIMIT)


def _dot(a, b, contract):
    dn = {"nn": (((1,), (0,)), ((), ())), "nt": (((1,), (1,)), ((), ())), "tn": (((0,), (0,)), ((), ()))}[contract]
    return lax.dot_general(a.astype(_MXU_DTYPE), b.astype(_MXU_DTYPE), dn, preferred_element_type=F32)


def _my_position():
    x, y, c = lax.axis_index("x"), lax.axis_index("y"), lax.axis_index("c")
    return x, y, c, 4 * x + 2 * y + c


def _exchange(name, groups, scatter):
    flat = [a for g in groups for a in g]
    n_in, n_grp = len(flat), len(groups)
    out_shapes = []
    for g in groups:
        slab = g[0].shape[1:] if scatter else g[0].shape
        out_shapes.append(jax.ShapeDtypeStruct((N_DEV, len(g)) + tuple(slab), g[0].dtype))

    def body(*refs):
        ins, outs = refs[:n_in], refs[n_in:n_in + n_grp]
        send_sems, recv_sems, local_sems = refs[n_in + n_grp:]
        x, y, c, me = _my_position()
        i = 0
        for gi, g in enumerate(groups):
            for l in range(len(g)):
                src = ins[i]
                i += 1
                pltpu.make_async_copy(src.at[me] if scatter else src, outs[gi].at[me, l], local_sems.at[gi]).start()
                for k in range(1, N_DEV):
                    px = 1 - x if k & 4 else x
                    py = 1 - y if k & 2 else y
                    pc = 1 - c if k & 1 else c
                    peer = 4 * px + 2 * py + pc
                    pltpu.make_async_remote_copy(
                        src_ref=src.at[peer] if scatter else src, dst_ref=outs[gi].at[me, l],
                        send_sem=send_sems.at[gi], recv_sem=recv_sems.at[gi],
                        device_id=(px, py, pc), device_id_type=MESH).start()
        for gi in range(n_grp):
            mine = outs[gi].at[me]
            pltpu.make_async_copy(mine, mine, local_sems.at[gi]).wait()
            seven = outs[gi].at[pl.ds(0, N_DEV - 1)]
            w = pltpu.make_async_remote_copy(src_ref=seven, dst_ref=seven, send_sem=send_sems.at[gi],
                                             recv_sem=recv_sems.at[gi], device_id=(x, y, c), device_id_type=MESH)
            w.wait_send()
            w.wait_recv()

    return pl.pallas_call(
        body, name=name, out_shape=tuple(out_shapes),
        in_specs=[ANY] * n_in, out_specs=tuple([ANY] * n_grp),
        scratch_shapes=[pltpu.SemaphoreType.DMA((n_grp,)), pltpu.SemaphoreType.DMA((n_grp,)),
                        pltpu.SemaphoreType.DMA((n_grp,))],
        compiler_params=pltpu.CompilerParams(has_side_effects=True),
    )(*flat)


HBM_SPEC = pl.BlockSpec(memory_space=pltpu.HBM)
SEM_SPEC = pl.BlockSpec(memory_space=pltpu.SEMAPHORE)
EFFECT = pltpu.SideEffectType.DATAFLOW_SIDE_EFFECTING


def _put_mine(name, srcs, scatter, me):
    n = len(srcs)
    slabs = [tuple(s.shape[1:] if sc else s.shape) for s, sc in zip(srcs, scatter)]

    def body(me_ref, *refs):
        for i in range(n):
            refs[n + i][...] = refs[i][...]

    def at_me(slab):
        return pl.BlockSpec((None,) + slab, lambda g, me_ref, nd=len(slab): (me_ref[0],) + (0,) * nd)

    def whole(slab):
        return pl.BlockSpec(slab, lambda g, me_ref, nd=len(slab): (0,) * nd)

    return pl.pallas_call(
        body, name=name,
        grid_spec=pltpu.PrefetchScalarGridSpec(
            num_scalar_prefetch=1, grid=(1,),
            in_specs=[at_me(slab) if sc else whole(slab) for slab, sc in zip(slabs, scatter)],
            out_specs=[at_me(slab) for slab in slabs]),
        out_shape=[jax.ShapeDtypeStruct((N_DEV,) + slab, s.dtype) for slab, s in zip(slabs, srcs)],
        compiler_params=_cp("arbitrary"))(me.reshape(1), *srcs)


def _exchange_start(name, srcs, scatter, after, me):
    n = len(srcs)
    scatter = list(scatter) if isinstance(scatter, (list, tuple)) else [scatter] * n
    lands = _put_mine(name + "_mine", srcs, scatter, me)
    srcs = [pltpu.with_memory_space_constraint(a, pltpu.HBM) for a in srcs]
    lands = [pltpu.with_memory_space_constraint(a, pltpu.HBM) for a in lands]

    def body(*refs):
        ins, land = refs[:n], refs[n:2 * n]
        send_sems, recv_sems, token = refs[2 * n + 1], refs[2 * n + 2], refs[-1]
        x, y, c, me_in = _my_position()
        for i in range(n):
            for k in range(1, N_DEV):
                px = 1 - x if k & 4 else x
                py = 1 - y if k & 2 else y
                pc = 1 - c if k & 1 else c
                pltpu.make_async_remote_copy(
                    src_ref=ins[i].at[4 * px + 2 * py + pc] if scatter[i] else ins[i], dst_ref=land[i].at[me_in],
                    send_sem=send_sems.at[i], recv_sem=recv_sems.at[i],
                    device_id=(px, py, pc), device_id_type=MESH).start()
        token[...] = jnp.zeros_like(token)

    outs = pl.pallas_call(
        body, name=name,
        out_shape=(pltpu.SemaphoreType.DMA((n,)), pltpu.SemaphoreType.DMA((n,)),
                   *[pltpu.HBM(a.shape, a.dtype) for a in srcs], *[pltpu.HBM(a.shape, a.dtype) for a in lands],
                   jax.ShapeDtypeStruct((8, 128), F32)),
        in_specs=[HBM_SPEC] * (2 * n) + [ANY],
        out_specs=(SEM_SPEC, SEM_SPEC, *[HBM_SPEC] * (2 * n), pl.BlockSpec(memory_space=pltpu.VMEM)),
        input_output_aliases={i: 2 + i for i in range(2 * n)},
        compiler_params=pltpu.CompilerParams(has_side_effects=EFFECT),
    )(*srcs, *lands, after)
    return (outs[0], outs[1], outs[2:2 + n], outs[2 + n:2 + 2 * n]), outs[-1]


def _exchange_wait(name, handle, after, which=None):
    send_sems, recv_sems, srcs, lands = handle
    which = list(range(len(srcs))) if which is None else list(which)
    srcs, lands = [srcs[i] for i in which], [lands[i] for i in which]
    n = len(srcs)

    def body(*refs):
        land, send_ref, recv_ref = refs[n:2 * n], refs[2 * n], refs[2 * n + 1]
        x, y, c, _ = _my_position()
        for k, i in enumerate(which):
            seven = land[k].at[pl.ds(0, N_DEV - 1)]
            w = pltpu.make_async_remote_copy(src_ref=seven, dst_ref=seven, send_sem=send_ref.at[i], recv_sem=recv_ref.at[i],
                                             device_id=(x, y, c), device_id_type=MESH)
            w.wait_send()
            w.wait_recv()

    outs = pl.pallas_call(
        body, name=name,
        out_shape=(*[pltpu.HBM(a.shape, a.dtype) for a in srcs], *[pltpu.HBM(a.shape, a.dtype) for a in lands]),
        in_specs=[HBM_SPEC] * (2 * n) + [SEM_SPEC, SEM_SPEC, ANY],
        out_specs=tuple([HBM_SPEC] * (2 * n)),
        input_output_aliases={i: i for i in range(2 * n)},
        compiler_params=pltpu.CompilerParams(has_side_effects=EFFECT),
    )(*srcs, *lands, send_sems, recv_sems, after)
    return outs[n:]


def _other_chips(x, y):
    return [(1 - x, y), (x, 1 - y), (1 - x, 1 - y)]


def _hier_gather_start(name, srcs, after, me):
    n = len(srcs)
    lands = _put_mine(name + "_mine", srcs, [False] * n, me)
    srcs = [pltpu.with_memory_space_constraint(a, pltpu.HBM) for a in srcs]
    lands = [pltpu.with_memory_space_constraint(a, pltpu.HBM) for a in lands]

    def body(*refs):
        ins, land = refs[:n], refs[n:2 * n]
        ici_send, ici_recv, d2d_send, d2d_recv = refs[2 * n + 1:2 * n + 5]
        token = refs[-1]
        x, y, c, me_in = _my_position()
        for i in range(n):
            pltpu.make_async_remote_copy(src_ref=ins[i], dst_ref=land[i].at[me_in], send_sem=d2d_send.at[i], recv_sem=d2d_recv.at[i],
                                         device_id=(x, y, 1 - c), device_id_type=MESH).start()
            for px, py in _other_chips(x, y):
                pltpu.make_async_remote_copy(src_ref=ins[i], dst_ref=land[i].at[me_in], send_sem=ici_send.at[i],
                                             recv_sem=ici_recv.at[i], device_id=(px, py, c), device_id_type=MESH).start()
        token[...] = jnp.zeros_like(token)

    sem = pltpu.SemaphoreType.DMA((n,))
    outs = pl.pallas_call(
        body, name=name,
        out_shape=(sem, sem, sem, sem, *[pltpu.HBM(a.shape, a.dtype) for a in srcs], *[pltpu.HBM(a.shape, a.dtype) for a in lands],
                   jax.ShapeDtypeStruct((8, 128), F32)),
        in_specs=[HBM_SPEC] * (2 * n) + [ANY],
        out_specs=(SEM_SPEC,) * 4 + (HBM_SPEC,) * (2 * n) + (pl.BlockSpec(memory_space=pltpu.VMEM),),
        input_output_aliases={i: 4 + i for i in range(2 * n)},
        compiler_params=pltpu.CompilerParams(has_side_effects=EFFECT),
    )(*srcs, *lands, after)
    return (outs[:4], outs[4:4 + n], outs[4 + n:4 + 2 * n]), outs[-1]


def _hier_gather_forward(name, handle, after):
    sems, srcs, lands = handle
    n = len(srcs)

    def body(*refs):
        land = refs[n:2 * n]
        ici_send, ici_recv, d2d_send, d2d_recv = refs[2 * n:2 * n + 4]
        x, y, c, _ = _my_position()
        for i in range(n):
            three = land[i].at[pl.ds(0, 3)]
            pltpu.make_async_remote_copy(src_ref=three, dst_ref=three, send_sem=ici_send.at[i], recv_sem=ici_recv.at[i],
                                         device_id=(x, y, c), device_id_type=MESH).wait_recv()
            for px, py in _other_chips(x, y):
                slab = land[i].at[4 * px + 2 * py + c]
                pltpu.make_async_remote_copy(src_ref=slab, dst_ref=slab, send_sem=d2d_send.at[i], recv_sem=d2d_recv.at[i],
                                             device_id=(x, y, 1 - c), device_id_type=MESH).start()

    outs = pl.pallas_call(
        body, name=name,
        out_shape=(*[pltpu.HBM(a.shape, a.dtype) for a in srcs], *[pltpu.HBM(a.shape, a.dtype) for a in lands]),
        in_specs=[HBM_SPEC] * (2 * n) + [SEM_SPEC] * 4 + [ANY],
        out_specs=tuple([HBM_SPEC] * (2 * n)),
        input_output_aliases={i: i for i in range(2 * n)},
        compiler_params=pltpu.CompilerParams(has_side_effects=EFFECT),
    )(*srcs, *lands, *sems, after)
    return (sems, outs[:n], outs[n:])


def _hier_gather_wait(name, handle, after):
    sems, srcs, lands = handle
    n = len(srcs)

    def body(*refs):
        land = refs[n:2 * n]
        ici_send, ici_recv, d2d_send, d2d_recv = refs[2 * n:2 * n + 4]
        x, y, c, _ = _my_position()
        for i in range(n):
            three, four = land[i].at[pl.ds(0, 3)], land[i].at[pl.ds(0, 4)]
            pltpu.make_async_remote_copy(src_ref=three, dst_ref=three, send_sem=ici_send.at[i], recv_sem=ici_recv.at[i],
                                         device_id=(x, y, c), device_id_type=MESH).wait_send()
            w = pltpu.make_async_remote_copy(src_ref=four, dst_ref=four, send_sem=d2d_send.at[i], recv_sem=d2d_recv.at[i],
                                             device_id=(x, y, c), device_id_type=MESH)
            w.wait_send()
            w.wait_recv()

    outs = pl.pallas_call(
        body, name=name,
        out_shape=(*[pltpu.HBM(a.shape, a.dtype) for a in srcs], *[pltpu.HBM(a.shape, a.dtype) for a in lands]),
        in_specs=[HBM_SPEC] * (2 * n) + [SEM_SPEC] * 4 + [ANY],
        out_specs=tuple([HBM_SPEC] * (2 * n)),
        input_output_aliases={i: i for i in range(2 * n)},
        compiler_params=pltpu.CompilerParams(has_side_effects=EFFECT),
    )(*srcs, *lands, *sems, after)
    return outs[n:]


def _pack(arrs):
    flat = jnp.concatenate([a.reshape(-1).astype(F32) for a in arrs])
    n = flat.shape[0]
    rows = -(-n // 1024) * 8
    return jnp.pad(flat, (0, rows * 128 - n)).reshape(rows, 128)


def _unpack(buf, shapes, lead=()):
    flat = buf.reshape(lead + (-1,))
    out, off = [], 0
    for s in shapes:
        n = math.prod(s)
        out.append(flat[..., off:off + n].reshape(lead + tuple(s)))
        off += n
    return out


def _mm(name, a, a_spec, b, b_spec, out_sds, o_spec, grid, contract, nk=1, stacked=0):
    o_blk = tuple(d for d in o_spec.block_shape if d is not None)

    def body(a_ref, b_ref, o_ref, *acc):
        if stacked:
            r = _dot(a_ref[0], b_ref[0], contract)
            for q in range(1, stacked):
                r = r + _dot(a_ref[q], b_ref[q], contract)
        else:
            r = _dot(a_ref[...], b_ref[...], contract)
        if nk == 1:
            o_ref[...] = r.astype(o_ref.dtype)
        else:
            k = pl.program_id(len(grid) - 1)

            @pl.when(k == 0)
            def _():
                acc[0][...] = r

            @pl.when(k > 0)
            def _():
                acc[0][...] += r

            @pl.when(k == nk - 1)
            def _():
                o_ref[...] = acc[0][...].astype(o_ref.dtype)

    sem = ("parallel",) * (len(grid) - 1) + (("arbitrary",) if nk > 1 else ("parallel",))
    return pl.pallas_call(
        body, name=name, out_shape=out_sds, grid=grid, in_specs=[a_spec, b_spec], out_specs=o_spec,
        scratch_shapes=[pltpu.VMEM(o_blk, F32)] if nk > 1 else [], compiler_params=_cp(*sem))(a, b)


def _tile(n, want):
    t = min(n, want)
    assert n % t == 0, (n, t)
    return t


def _mm_nn(name, a, b, out_dtype=F32, tm=512, tn=512):
    (M, K), N = a.shape, b.shape[1]
    tm, tn = _tile(M, tm), _tile(N, tn)
    return _mm(name, a, pl.BlockSpec((tm, K), lambda i, j: (i, 0)), b, pl.BlockSpec((K, tn), lambda i, j: (0, j)),
               jax.ShapeDtypeStruct((M, N), out_dtype), pl.BlockSpec((tm, tn), lambda i, j: (i, j)),
               (M // tm, N // tn), "nn")


def _mm_nt(name, a, b, out_dtype=F32, tm=512, tn=512):
    (M, K), N = a.shape, b.shape[0]
    tm, tn = _tile(M, tm), _tile(N, tn)
    return _mm(name, a, pl.BlockSpec((tm, K), lambda i, j: (i, 0)), b, pl.BlockSpec((tn, K), lambda i, j: (j, 0)),
               jax.ShapeDtypeStruct((M, N), out_dtype), pl.BlockSpec((tm, tn), lambda i, j: (i, j)),
               (M // tm, N // tn), "nt")


def _mm_tn(name, a, b, out_dtype=F32, tm=512, tn=512):
    (K, M), N = a.shape, b.shape[1]
    tm, tn = _tile(M, tm), _tile(N, tn)
    return _mm(name, a, pl.BlockSpec((K, tm), lambda i, j: (0, i)), b, pl.BlockSpec((K, tn), lambda i, j: (0, j)),
               jax.ShapeDtypeStruct((M, N), out_dtype), pl.BlockSpec((tm, tn), lambda i, j: (i, j)),
               (M // tm, N // tn), "tn")


def _mm_cols(name, a, w, out_dtype=F32, tm=512):
    (M, K), (J, _, n) = a.shape, w.shape
    tm = _tile(M, tm)
    return _mm(name, a, pl.BlockSpec((tm, K), lambda j, i: (i, 0)), w, pl.BlockSpec((None, K, n), lambda j, i: (j, 0, 0)),
               jax.ShapeDtypeStruct((J, M, n), out_dtype), pl.BlockSpec((None, tm, n), lambda j, i: (j, i, 0)),
               (J, M // tm), "nn")


def _mm_cols_dx(name, d, w, out_dtype=F32, tm=512, jb=None):
    (J, M, n), K = d.shape, w.shape[1]
    tm, jb = _tile(M, tm), J if jb is None else jb
    return _mm(name, d, pl.BlockSpec((jb, tm, n), lambda i, j: (j, i, 0)), w, pl.BlockSpec((jb, K, n), lambda i, j: (j, 0, 0)),
               jax.ShapeDtypeStruct((M, K), out_dtype), pl.BlockSpec((tm, K), lambda i, j: (i, 0)),
               (M // tm, J // jb), "nt", nk=J // jb, stacked=jb)


def _mm_cols_dw(name, a, d, out_dtype=F32, tk=512):
    (M, K), (J, _, n) = a.shape, d.shape
    tk = _tile(K, tk)
    return _mm(name, a, pl.BlockSpec((M, tk), lambda j, i: (0, i)), d, pl.BlockSpec((None, M, n), lambda j, i: (j, 0, 0)),
               jax.ShapeDtypeStruct((J, K, n), out_dtype), pl.BlockSpec((None, tk, n), lambda j, i: (j, i, 0)),
               (J, K // tk), "tn")


def _mm_cols_dwt(name, a, d, out_dtype=F32, tk=512):
    (M, K), (J, _, n) = a.shape, d.shape
    tk = _tile(K, tk)
    return _mm(name, d, pl.BlockSpec((None, M, n), lambda j, i: (j, 0, 0)), a, pl.BlockSpec((M, tk), lambda j, i: (0, i)),
               jax.ShapeDtypeStruct((J, n, K), out_dtype), pl.BlockSpec((None, n, tk), lambda j, i: (j, 0, i)),
               (J, K // tk), "tn")


def _mm_rows_resid(name, a, w, resid, gate, tm=512):
    (Q, M, k), N = a.shape, w.shape[2]
    tm = _tile(M, tm)

    def body(a_ref, w_ref, r_ref, g_ref, y_ref, x_ref):
        y = _dot(a_ref[0], w_ref[0], "nn")
        for q in range(1, Q):
            y = y + _dot(a_ref[q], w_ref[q], "nn")
        y_ref[...] = y
        x_ref[...] = r_ref[...] + g_ref[...] * y

    return pl.pallas_call(
        body, name=name, grid=(M // tm,),
        out_shape=(jax.ShapeDtypeStruct((M, N), F32), jax.ShapeDtypeStruct((M, N), F32)),
        in_specs=[pl.BlockSpec((Q, tm, k), lambda i: (0, i, 0)), pl.BlockSpec((Q, k, N), lambda i: (0, 0, 0)),
                  pl.BlockSpec((tm, N), lambda i: (i, 0)), pl.BlockSpec((1, N), lambda i: (0, 0))],
        out_specs=(pl.BlockSpec((tm, N), lambda i: (i, 0)), pl.BlockSpec((tm, N), lambda i: (i, 0))),
        compiler_params=_cp("parallel"))(a, w, resid, gate)


def _mm_rows_dx(name, d, w, out_dtype=F32, tm=512):
    (M, N), (Q, k, _) = d.shape, w.shape
    tm = _tile(M, tm)
    return _mm(name, d, pl.BlockSpec((tm, N), lambda q, i: (i, 0)), w, pl.BlockSpec((None, k, N), lambda q, i: (q, 0, 0)),
               jax.ShapeDtypeStruct((Q, M, k), out_dtype), pl.BlockSpec((None, tm, k), lambda q, i: (q, i, 0)),
               (Q, M // tm), "nt")


def _mm_rows_dw(name, a, d, out_dtype=F32, tn=512):
    (Q, M, k), N = a.shape, d.shape[1]
    tn = _tile(N, tn)
    return _mm(name, a, pl.BlockSpec((None, M, k), lambda q, j: (q, 0, 0)), d, pl.BlockSpec((M, tn), lambda q, j: (0, j)),
               jax.ShapeDtypeStruct((Q, k, N), out_dtype), pl.BlockSpec((None, k, tn), lambda q, j: (q, 0, j)),
               (Q, N // tn), "tn")


def _silu(v):
    return v * jax.nn.sigmoid(v)


def _ada_fwd(c16, ada_w):
    L, D, n = ada_w.shape

    def body(c_ref, w_ref, o_ref):
        o_ref[...] = _dot(_silu(c_ref[...]), w_ref[...], "nn")

    return pl.pallas_call(
        body, name="ada_fwd", grid=(L,), out_shape=jax.ShapeDtypeStruct((L, 16, n), F32),
        in_specs=[pl.BlockSpec((16, D), lambda l: (0, 0)), pl.BlockSpec((None, D, n), lambda l: (l, 0, 0))],
        out_specs=pl.BlockSpec((None, 16, n), lambda l: (l, 0, 0)), compiler_params=_cp("parallel"))(c16, ada_w)


def _ada_bwd(c16, dmod16):
    L, _, n = dmod16.shape
    D = c16.shape[1]

    def body(c_ref, d_ref, o_ref):
        o_ref[...] = _dot(_silu(c_ref[...]), d_ref[...], "tn")

    return pl.pallas_call(
        body, name="ada_bwd", grid=(L,), out_shape=jax.ShapeDtypeStruct((L, D, n), F32),
        in_specs=[pl.BlockSpec((16, D), lambda l: (0, 0)), pl.BlockSpec((None, 16, n), lambda l: (l, 0, 0))],
        out_specs=pl.BlockSpec((None, D, n), lambda l: (l, 0, 0)), compiler_params=_cp("parallel"))(c16, dmod16)


def _row_spec(tr, n):
    return pl.BlockSpec((tr, n), lambda i: (i, 0))


def _vec_spec(n):
    return pl.BlockSpec((1, n), lambda i: (0, 0))


def _rmsmod_fwd(name, x, g, sc, sh, after, tr=256):
    S, D = x.shape

    def body(x_ref, g_ref, sc_ref, sh_ref, after_ref, h_ref):
        xv = x_ref[...]
        rstd = lax.rsqrt(jnp.mean(xv * xv, axis=-1, keepdims=True) + EPS)
        y = xv * rstd * g_ref[...]
        h_ref[...] = (y * (1.0 + sc_ref[...]) + sh_ref[...]).astype(h_ref.dtype)

    return pl.pallas_call(
        body, name=name, grid=(S // tr,), out_shape=jax.ShapeDtypeStruct((S, D), _MXU_DTYPE),
        in_specs=[_row_spec(tr, D), _vec_spec(D), _vec_spec(D), _vec_spec(D), ANY], out_specs=_row_spec(tr, D),
        compiler_params=_cp("parallel"))(x, g, sc, sh, after)


def _acc_rows(ref, val, first):
    s = jnp.sum(val, axis=0, keepdims=True)

    @pl.when(first)
    def _():
        ref[...] = s

    @pl.when(jnp.logical_not(first))
    def _():
        ref[...] += s


def _rmsmod_bwd(name, x, g, sc, dh, dres, after, tr=256):
    S, D = x.shape

    def body(x_ref, g_ref, sc_ref, dh_ref, dres_ref, after_ref, dx_ref, dg_ref, dsc_ref, dsh_ref):
        first = pl.program_id(0) == 0
        xv, dh_v, gv = x_ref[...], dh_ref[...], g_ref[...]
        rstd = lax.rsqrt(jnp.mean(xv * xv, axis=-1, keepdims=True) + EPS)
        xhat = xv * rstd
        _acc_rows(dsh_ref, dh_v, first)
        _acc_rows(dsc_ref, dh_v * (xhat * gv), first)
        dyg = dh_v * (1.0 + sc_ref[...])
        _acc_rows(dg_ref, dyg * xhat, first)
        dxhat = dyg * gv
        dx_ref[...] = dres_ref[...] + rstd * (dxhat - xhat * jnp.mean(dxhat * xhat, axis=-1, keepdims=True))

    vec = jax.ShapeDtypeStruct((1, D), F32)
    return pl.pallas_call(
        body, name=name, grid=(S // tr,), out_shape=(jax.ShapeDtypeStruct((S, D), F32), vec, vec, vec),
        in_specs=[_row_spec(tr, D), _vec_spec(D), _vec_spec(D), _row_spec(tr, D), _row_spec(tr, D), ANY],
        out_specs=(_row_spec(tr, D), _vec_spec(D), _vec_spec(D), _vec_spec(D)),
        compiler_params=_cp("arbitrary"))(x, g, sc, dh, dres, after)


def _loss_head(x, g, target, tr=256):
    S, D = x.shape

    def body(x_ref, g_ref, t_ref, loss_ref, dx_ref, dg_ref):
        first = pl.program_id(0) == 0
        xv, gv = x_ref[...], g_ref[...]
        rstd = lax.rsqrt(jnp.mean(xv * xv, axis=-1, keepdims=True) + EPS)
        xhat = xv * rstd
        err = xhat * gv - t_ref[...]
        part = 0.5 * jnp.sum(jnp.mean(err * err, axis=-1, keepdims=True), axis=0, keepdims=True)

        @pl.when(first)
        def _():
            loss_ref[...] = part

        @pl.when(jnp.logical_not(first))
        def _():
            loss_ref[...] += part

        dout = err * (1.0 / D)
        _acc_rows(dg_ref, dout * xhat, first)
        dxhat = dout * gv
        dx_ref[...] = rstd * (dxhat - xhat * jnp.mean(dxhat * xhat, axis=-1, keepdims=True))

    return pl.pallas_call(
        body, name="loss_head", grid=(S // tr,),
        out_shape=(jax.ShapeDtypeStruct((1, 1), F32), jax.ShapeDtypeStruct((S, D), F32), jax.ShapeDtypeStruct((1, D), F32)),
        in_specs=[_row_spec(tr, D), _vec_spec(D), _row_spec(tr, D)],
        out_specs=(pl.BlockSpec((1, 1), lambda i: (0, 0)), _row_spec(tr, D), _vec_spec(D)),
        compiler_params=_cp("arbitrary"))(x, g, target)


def _gate_bwd(name, dx, y, gate, tr=256):
    S, D = dx.shape

    def body(dx_ref, y_ref, g_ref, dy_ref, dg_ref):
        dxv = dx_ref[...]
        dy_ref[...] = (g_ref[...] * dxv).astype(dy_ref.dtype)
        _acc_rows(dg_ref, dxv * y_ref[...], pl.program_id(0) == 0)

    return pl.pallas_call(
        body, name=name, grid=(S // tr,),
        out_shape=(jax.ShapeDtypeStruct((S, D), _MXU_DTYPE), jax.ShapeDtypeStruct((1, D), F32)),
        in_specs=[_row_spec(tr, D), _row_spec(tr, D), _vec_spec(D)], out_specs=(_row_spec(tr, D), _vec_spec(D)),
        compiler_params=_cp("arbitrary"))(dx, y, gate)


def _shift_down(v, k):
    t = lax.broadcasted_iota(jnp.int32, v.shape, 0)
    return jnp.where(t >= k, pltpu.roll(v, k, axis=0), 0.0)


def _shift_up(v, k):
    n = v.shape[0]
    t = lax.broadcasted_iota(jnp.int32, v.shape, 0)
    return jnp.where(t < n - k, pltpu.roll(v, n - k, axis=0), 0.0)


def _window_sum(p, w, shift):
    s, k = p, 1
    while k < w:
        s = s + shift(s, k)
        k *= 2
    return s


def _pool_count(shape, w):
    t = lax.broadcasted_iota(jnp.int32, shape, 0)
    return jnp.minimum(t + 1, w).astype(F32)


def _ab_specs(S):
    zs = [pl.BlockSpec((None, S, 128), functools.partial(lambda g, q: (2 * q + g // 2, 0, g % 2), q=q)) for q in range(4)]
    return zs


def _ab_mix_fwd(z8, conv_w, mix_w, scale):
    S = z8.shape[1]

    def body(b_ref, c_ref, a_ref, p_ref, w_ref, mix_ref, sc_ref, y_ref):
        g = pl.program_id(0)
        cg = c_ref[...] * a_ref[...]
        w = w_ref[...]
        conv = w[0:1] * _shift_down(cg, 2) + w[1:2] * _shift_down(cg, 1) + w[2:3] * cg
        y_ref[0] = (b_ref[...] * conv).astype(y_ref.dtype)
        for gg, win in enumerate(POOL_WINDOWS):
            @pl.when(g == gg)
            def _(win=win):
                p = p_ref[...]
                pooled = _window_sum(p, win, _shift_down) / _pool_count(p.shape, win) - p
                y_ref[1] = (_dot(pooled, mix_ref[...], "nn") * sc_ref[...]).astype(y_ref.dtype)

    return pl.pallas_call(
        body, name="ab_mix_fwd", grid=(4,), out_shape=jax.ShapeDtypeStruct((2, S, 512), _MXU_DTYPE),
        in_specs=_ab_specs(S) + [pl.BlockSpec((3, 128), lambda g: (0, g)), pl.BlockSpec((None, 128, 128), lambda g: (g, 0, 0)),
                                 pl.BlockSpec((1, 128), lambda g: (0, g))],
        out_specs=pl.BlockSpec((2, S, 128), lambda g: (0, 0, g)), compiler_params=_cp("parallel"))(z8, z8, z8, z8, conv_w, mix_w, scale)


def _ab_mix_bwd(z8, dycat2, conv_w, mix_w, scale, after):
    S = z8.shape[1]

    def body(b_ref, c_ref, a_ref, p_ref, dy_ref, w_ref, mix_ref, sc_ref, after_ref, dz_ref, dw_ref, dmix_ref, dsc_ref):
        g = pl.program_id(0)
        bv, cv, av, w = b_ref[...], c_ref[...], a_ref[...], w_ref[...]
        dya = dy_ref[0]
        cg = cv * av
        cg1, cg2 = _shift_down(cg, 1), _shift_down(cg, 2)
        conv = w[0:1] * cg2 + w[1:2] * cg1 + w[2:3] * cg
        dz_ref[0] = (dya * conv).astype(dz_ref.dtype)
        dconv = dya * bv
        dcg = w[2:3] * dconv + w[1:2] * _shift_up(dconv, 1) + w[0:1] * _shift_up(dconv, 2)
        dz_ref[1] = (dcg * av).astype(dz_ref.dtype)
        dz_ref[2] = (dcg * cv).astype(dz_ref.dtype)
        dw_ref[0:1, :] = jnp.sum(dconv * cg2, axis=0, keepdims=True)
        dw_ref[1:2, :] = jnp.sum(dconv * cg1, axis=0, keepdims=True)
        dw_ref[2:3, :] = jnp.sum(dconv * cg, axis=0, keepdims=True)
        for gg, win in enumerate(POOL_WINDOWS):
            @pl.when(g == gg)
            def _(win=win):
                p, dyb, mix = p_ref[...], dy_ref[1], mix_ref[...]
                cnt = _pool_count(p.shape, win)
                pooled = _window_sum(p, win, _shift_down) / cnt - p
                dsc_ref[...] = jnp.sum(dyb * _dot(pooled, mix, "nn"), axis=0, keepdims=True)
                dmixed = dyb * sc_ref[...]
                dmix_ref[...] = _dot(pooled, dmixed, "tn")
                dpooled = _dot(dmixed, mix, "nt")
                dz_ref[3] = (_window_sum(dpooled / cnt, win, _shift_up) - dpooled).astype(dz_ref.dtype)

    return pl.pallas_call(
        body, name="ab_mix_bwd", grid=(4,),
        out_shape=(jax.ShapeDtypeStruct((4, 2, S, 256), _MXU_DTYPE), jax.ShapeDtypeStruct((3, 512), F32),
                   jax.ShapeDtypeStruct((4, 128, 128), F32), jax.ShapeDtypeStruct((1, 512), F32)),
        in_specs=_ab_specs(S) + [pl.BlockSpec((2, S, 128), lambda g: (0, 0, g)), pl.BlockSpec((3, 128), lambda g: (0, g)),
                                 pl.BlockSpec((None, 128, 128), lambda g: (g, 0, 0)), pl.BlockSpec((1, 128), lambda g: (0, g)), ANY],
        out_specs=(pl.BlockSpec((4, None, S, 128), lambda g: (0, g // 2, 0, g % 2)), pl.BlockSpec((3, 128), lambda g: (0, g)),
                   pl.BlockSpec((None, 128, 128), lambda g: (g, 0, 0)), pl.BlockSpec((1, 128), lambda g: (0, g))),
        compiler_params=_cp("parallel"))(z8, z8, z8, z8, dycat2, conv_w, mix_w, scale, after)


HALO = 16


def _ffn_specs(S, n, tr):
    nb = S // HALO
    tile = pl.BlockSpec((2, None, tr, n), lambda j, i: (0, j, i, 0))
    prev = pl.BlockSpec((2, None, HALO, n), lambda j, i: (0, j, jnp.maximum(i * (tr // HALO) - 1, 0), 0))
    nxt = pl.BlockSpec((2, None, HALO, n), lambda j, i: (0, j, jnp.minimum((i + 1) * (tr // HALO), nb - 1), 0))
    cw = pl.BlockSpec((2, None, 3, n), lambda j, i: (0, j, 0, 0))
    return tile, prev, nxt, cw


def _shifted_rows(ext, lo, rows):
    ext = ext.astype(F32)
    return pltpu.roll(ext, 1, axis=0)[lo:lo + rows], pltpu.roll(ext, 2, axis=0)[lo:lo + rows]


def _ffn_gate_fwd(name, u24, cw24, tr=256):
    _, J, S, n = u24.shape
    tile, prev, _, cw = _ffn_specs(S, n, tr)

    def body(u_ref, up_ref, w_ref, a_ref):
        keep = (pl.program_id(1) > 0).astype(u_ref.dtype)
        z = []
        for h in range(2):
            ext = jnp.concatenate([up_ref[h] * keep, u_ref[h]], axis=0)
            x1, x2 = _shifted_rows(ext, HALO, tr)
            w = w_ref[h]
            z.append(w[0:1] * x2 + w[1:2] * x1 + w[2:3] * u_ref[h].astype(F32))
        a_ref[...] = (_silu(z[0]) * z[1]).astype(a_ref.dtype)

    return pl.pallas_call(
        body, name=name, grid=(J, S // tr), out_shape=jax.ShapeDtypeStruct((J, S, n), _MXU_DTYPE),
        in_specs=[tile, prev, cw], out_specs=pl.BlockSpec((None, tr, n), lambda j, i: (j, i, 0)),
        compiler_params=_cp("parallel", "parallel"))(u24, u24, cw24)


def _ffn_gate_bwd(name, u24, cw24, da4, w_up24, after, tr=256):
    _, J, S, n = u24.shape
    K = w_up24.shape[2]
    nb = S // HALO
    tile = pl.BlockSpec((2, None, tr, n), lambda i, j: (0, j, i, 0))
    prev = pl.BlockSpec((2, None, HALO, n), lambda i, j: (0, j, jnp.maximum(i * (tr // HALO) - 1, 0), 0))
    nxt = pl.BlockSpec((2, None, HALO, n), lambda i, j: (0, j, jnp.minimum((i + 1) * (tr // HALO), nb - 1), 0))
    whole = lambda shape: pl.BlockSpec(shape, lambda i, j: (0,) * len(shape))

    def body(u_ref, up_ref, un_ref, cw_ref, da_ref, dan_ref, wup_ref, after_ref, du_ref, dcw_ref, dh_ref, acc_ref):
        i, j = pl.program_id(0), pl.program_id(1)
        first = i == 0
        keep_prev = (i > 0).astype(u_ref.dtype)
        keep_next = (i < S // tr - 1).astype(F32)
        w = [cw_ref[h, j] for h in range(2)]
        m = tr + HALO
        xs, z = [], []
        for h in range(2):
            ext = jnp.concatenate([up_ref[h] * keep_prev, u_ref[h], un_ref[h]], axis=0)
            x1, x2 = _shifted_rows(ext, HALO, m)
            x0 = ext[HALO:HALO + m].astype(F32)
            xs.append((x2, x1, x0))
            z.append(w[h][0:1] * x2 + w[h][1:2] * x1 + w[h][2:3] * x0)
        zg, zu = z
        da = jnp.concatenate([da_ref[...].astype(F32), dan_ref[...].astype(F32) * keep_next], axis=0)
        sg = jax.nn.sigmoid(zg)
        dz = [da * zu * (sg * (1.0 + zg * (1.0 - sg))), da * (zg * sg)]
        dh = None
        for h in range(2):
            d = dz[h]
            du = w[h][2:3] * d[:tr] + w[h][1:2] * pltpu.roll(d, m - 1, axis=0)[:tr] + w[h][0:1] * pltpu.roll(d, m - 2, axis=0)[:tr]
            du = du.astype(du_ref.dtype)
            du_ref[h] = du
            part = _dot(du, wup_ref[h, j], "nt")
            dh = part if dh is None else dh + part
            dt = d[:tr]
            parts = [jnp.sum(dt * xk[:tr], axis=0, keepdims=True) for xk in xs[h]]
            for k in range(3):
                @pl.when(first)
                def _(k=k, h=h):
                    dcw_ref[h, j, k:k + 1, :] = parts[k]

                @pl.when(jnp.logical_not(first))
                def _(k=k, h=h):
                    dcw_ref[h, j, k:k + 1, :] += parts[k]

        @pl.when(j == 0)
        def _():
            acc_ref[...] = dh

        @pl.when(j > 0)
        def _():
            acc_ref[...] += dh

        @pl.when(j == J - 1)
        def _():
            dh_ref[...] = acc_ref[...]

    da_tile = pl.BlockSpec((None, tr, n), lambda i, j: (j, i, 0))
    da_next = pl.BlockSpec((None, HALO, n), lambda i, j: (j, jnp.minimum((i + 1) * (tr // HALO), nb - 1), 0))
    return pl.pallas_call(
        body, name=name, grid=(S // tr, J),
        out_shape=(jax.ShapeDtypeStruct((2, J, S, n), _MXU_DTYPE), jax.ShapeDtypeStruct((2, J, 3, n), F32),
                   jax.ShapeDtypeStruct((S, K), F32)),
        in_specs=[tile, prev, nxt, whole((2, J, 3, n)), da_tile, da_next, whole((2, J, K, n)), ANY],
        out_specs=(tile, whole((2, J, 3, n)), pl.BlockSpec((tr, K), lambda i, j: (i, 0))),
        scratch_shapes=[pltpu.VMEM((tr, K), F32)],
        compiler_params=_cp("arbitrary", "arbitrary"))(u24, u24, u24, cw24, da4, da4, w_up24, after)


def _rms_rows(v, g):
    rstd = lax.rsqrt(jnp.mean(v * v, axis=-1, keepdims=True) + EPS)
    return v * rstd * g


def _rms_rows_bwd(v, g, dy):
    rstd = lax.rsqrt(jnp.mean(v * v, axis=-1, keepdims=True) + EPS)
    vhat = v * rstd
    dvhat = dy * g
    return rstd * (dvhat - vhat * jnp.mean(dvhat * vhat, axis=-1, keepdims=True)), dy * vhat


def _mla_prep_fwd(z, qg, kvg, tr=256):
    S = z.shape[0]

    def body(q_ref, kv_ref, qg_ref, kvg_ref, qn_ref, kvn_ref):
        qn_ref[...] = _rms_rows(q_ref[...], qg_ref[...]).astype(qn_ref.dtype)
        kvn_ref[...] = _rms_rows(kv_ref[...], kvg_ref[...]).astype(kvn_ref.dtype)

    return pl.pallas_call(
        body, name="mla_prep_fwd", grid=(S // tr,),
        out_shape=(jax.ShapeDtypeStruct((S, 256), _MXU_DTYPE), jax.ShapeDtypeStruct((S, 128), _MXU_DTYPE)),
        in_specs=[pl.BlockSpec((tr, 256), lambda i: (i, 0)), pl.BlockSpec((tr, 128), lambda i: (i, 2)), _vec_spec(256), _vec_spec(128)],
        out_specs=(_row_spec(tr, 256), _row_spec(tr, 128)), compiler_params=_cp("parallel"))(z, z, qg, kvg)


def _mla_prep_bwd(z, qg, kvg, dqn, dkvn, dkpe, duv, tr=256):
    S = z.shape[0]

    def body(q_ref, kv_ref, qg_ref, kvg_ref, dqn_ref, dkvn_ref, dkpe_ref, duv_ref, dz_ref, dqg_ref, dkvg_ref):
        first = pl.program_id(0) == 0
        dq, dqg = _rms_rows_bwd(q_ref[...], qg_ref[...], dqn_ref[...])
        dkv, dkvg = _rms_rows_bwd(kv_ref[...], kvg_ref[...], dkvn_ref[...])
        _acc_rows(dqg_ref, dqg, first)
        _acc_rows(dkvg_ref, dkvg, first)
        dz_ref[:, 0:256] = dq.astype(dz_ref.dtype)
        dz_ref[:, 256:384] = dkv.astype(dz_ref.dtype)
        dz_ref[:, 384:512] = dkpe_ref[...].astype(dz_ref.dtype)
        dz_ref[:, 512:1536] = duv_ref[...].astype(dz_ref.dtype)

    return pl.pallas_call(
        body, name="mla_prep_bwd", grid=(S // tr,),
        out_shape=(jax.ShapeDtypeStruct((S, 1536), _MXU_DTYPE), jax.ShapeDtypeStruct((1, 256), F32), jax.ShapeDtypeStruct((1, 128), F32)),
        in_specs=[pl.BlockSpec((tr, 256), lambda i: (i, 0)), pl.BlockSpec((tr, 128), lambda i: (i, 2)), _vec_spec(256), _vec_spec(128),
                  _row_spec(tr, 256), _row_spec(tr, 128), _row_spec(tr, 128), _row_spec(tr, 1024)],
        out_specs=(_row_spec(tr, 1536), _vec_spec(256), _vec_spec(128)),
        compiler_params=_cp("arbitrary"))(z, z, qg, kvg, dqn, dkvn, dkpe, duv)


def _rope(v, cos, sa, sb):
    return v * cos + pltpu.roll(v, 112, axis=1) * sa + pltpu.roll(v, 16, axis=1) * sb


def _rope_t(d, cos, sa, sb):
    return d * cos + pltpu.roll(d * sa, 16, axis=1) + pltpu.roll(d * sb, 112, axis=1)


def _rope_fwd(qraw, kvall, z, cosq, cosk, sa, sb, tr=256):
    S = qraw.shape[0]

    def body(q_ref, k_ref, v_ref, kpe_ref, cq_ref, ck_ref, sa_ref, sb_ref, qo_ref, ko_ref, vo_ref):
        cq, ck, sa_v, sb_v = cq_ref[...], ck_ref[...], sa_ref[...], sb_ref[...]
        kpe = _rope(kpe_ref[...], ck, sa_v, sb_v)
        for h in range(8):
            cols = slice(128 * h, 128 * h + 128)
            qo_ref[:, cols] = _rope(q_ref[:, cols], cq, sa_v, sb_v).astype(qo_ref.dtype)
            ko_ref[:, cols] = (k_ref[:, cols] + kpe).astype(ko_ref.dtype)
        vo_ref[...] = v_ref[...].astype(vo_ref.dtype)

    tab = _row_spec(tr, 128)
    return pl.pallas_call(
        body, name="rope_fwd", grid=(S // tr,),
        out_shape=(jax.ShapeDtypeStruct((S, 1024), _MXU_DTYPE), jax.ShapeDtypeStruct((S, 1024), _MXU_DTYPE),
                   jax.ShapeDtypeStruct((S, 512), _MXU_DTYPE)),
        in_specs=[_row_spec(tr, 1024), pl.BlockSpec((tr, 1024), lambda i: (i, 0)), pl.BlockSpec((tr, 512), lambda i: (i, 2)),
                  pl.BlockSpec((tr, 128), lambda i: (i, 3)), tab, tab, tab, tab],
        out_specs=(_row_spec(tr, 1024), _row_spec(tr, 1024), _row_spec(tr, 512)),
        compiler_params=_cp("parallel"))(qraw, kvall, kvall, z, cosq, cosk, sa, sb)


def _rope_bwd(dq, dk, dv, cosq, cosk, sa, sb, tr=256):
    S = dq.shape[0]

    def body(dq_ref, dk_ref, dv_ref, cq_ref, ck_ref, sa_ref, sb_ref, dqo_ref, dkv_ref, dkpe_ref):
        cq, ck, sa_v, sb_v = cq_ref[...], ck_ref[...], sa_ref[...], sb_ref[...]
        tot = jnp.zeros((tr, 128), F32)
        for h in range(8):
            cols = slice(128 * h, 128 * h + 128)
            dqo_ref[:, cols] = _rope_t(dq_ref[:, cols], cq, sa_v, sb_v).astype(dqo_ref.dtype)
            dkh = dk_ref[:, cols]
            tot = tot + dkh
            dkv_ref[:, cols] = dkh.astype(dkv_ref.dtype)
        dkv_ref[:, 1024:1536] = dv_ref[...].astype(dkv_ref.dtype)
        dkpe_ref[...] = _rope_t(tot, ck, sa_v, sb_v)

    tab = _row_spec(tr, 128)
    return pl.pallas_call(
        body, name="rope_bwd", grid=(S // tr,),
        out_shape=(jax.ShapeDtypeStruct((S, 1024), _MXU_DTYPE), jax.ShapeDtypeStruct((S, 1536), _MXU_DTYPE),
                   jax.ShapeDtypeStruct((S, 128), F32)),
        in_specs=[_row_spec(tr, 1024), _row_spec(tr, 1024), _row_spec(tr, 512), tab, tab, tab, tab],
        out_specs=(_row_spec(tr, 1024), _row_spec(tr, 1536), _row_spec(tr, 128)),
        compiler_params=_cp("parallel"))(dq, dk, dv, cosq, cosk, sa, sb)


NEG = -1e30


def _attn_fwd(q, k, v, tq=256, tk=256):
    S = q.shape[0]
    assert tq == tk

    def body(q_ref, k_ref, v_ref, o_ref, lse_ref):
        i = pl.program_id(1)
        qs = [q_ref[:, 0:128], q_ref[:, 128:256]]

        def step(kb, carry, diagonal=False):
            start = pl.multiple_of(kb * tk, tk)
            vv = v_ref[pl.ds(start, tk), :]
            out = []
            for h in range(2):
                m, l, acc = carry[3 * h:3 * h + 3]
                s = _dot(qs[h], k_ref[pl.ds(start, tk), 128 * h:128 * h + 128], "nt") * ATTN_SCALE
                if diagonal:
                    s = jnp.where(below, s, NEG)
                m_new = jnp.maximum(m, jnp.max(s, axis=-1, keepdims=True))
                alpha = jnp.exp(m - m_new)
                p = jnp.exp(s - m_new)
                out += [m_new, alpha * l + jnp.sum(p, axis=-1, keepdims=True), alpha * acc + _dot(p, vv, "nn")]
            return tuple(out)

        below = lax.broadcasted_iota(jnp.int32, (tq, tk), 1) <= lax.broadcasted_iota(jnp.int32, (tq, tk), 0)
        init = (jnp.full((tq, 1), NEG, F32), jnp.zeros((tq, 1), F32), jnp.zeros((tq, 128), F32)) * 2
        ma, la, acca, mb, lb, accb = step(i, lax.fori_loop(0, i, step, init), diagonal=True)
        lane = lax.broadcasted_iota(jnp.int32, (tq, 128), 1)
        o_ref[...] = jnp.where(lane < 64, acca / la, accb / lb)
        lse_ref[...] = jnp.where(lane < 64, ma + jnp.log(la), mb + jnp.log(lb))

    return pl.pallas_call(
        body, name="attn_fwd", grid=(4, S // tq),
        out_shape=(jax.ShapeDtypeStruct((S, 512), F32), jax.ShapeDtypeStruct((4, S, 128), F32)),
        in_specs=[pl.BlockSpec((tq, 256), lambda p, i: (i, p)), pl.BlockSpec((S, 256), lambda p, i: (0, p)),
                  pl.BlockSpec((S, 128), lambda p, i: (0, p))],
        out_specs=(pl.BlockSpec((tq, 128), lambda p, i: (i, p)), pl.BlockSpec((None, tq, 128), lambda p, i: (p, i, 0))),
        compiler_params=_cp("parallel", "parallel"))(q, k, v)


def _attn_bwd(q, k, v, o, lse, dycat2, tq=256, tk=256):
    S = q.shape[0]
    assert tq == tk

    def body(q_ref, k_ref, v_ref, o_ref, lse_ref, do_ref, dq_ref, dk_ref, dv_ref):
        j = pl.program_id(1)

        @pl.when(j == 0)
        def _():
            dq_ref[...] = jnp.zeros_like(dq_ref)

        below = lax.broadcasted_iota(jnp.int32, (tq, tk), 1) <= lax.broadcasted_iota(jnp.int32, (tq, tk), 0)
        lane = lax.broadcasted_iota(jnp.int32, (tq, 128), 1)
        ks = [k_ref[:, 0:128], k_ref[:, 128:256]]
        vv = v_ref[...]

        def step(qb, carry, diagonal=False):
            dka, dkb, dvp = carry
            start = pl.multiple_of(qb * tq, tq)
            rows = pl.ds(start, tq)
            do, lse_v = do_ref[rows, :], lse_ref[rows, :]
            prod = do * o_ref[rows, :]
            dks = [dka, dkb]
            for h in range(2):
                mine = (lane < 64) if h == 0 else (lane >= 64)
                delta = jnp.sum(jnp.where(mine, prod, 0.0), axis=-1, keepdims=True)
                do_h = jnp.where(mine, do, 0.0)
                qh = q_ref[rows, 128 * h:128 * h + 128]
                s = _dot(qh, ks[h], "nt") * ATTN_SCALE
                p = jnp.exp(s - lse_v[:, 64 * h:64 * h + 1])
                if diagonal:
                    p = jnp.where(below, p, 0.0)
                dvp = dvp + _dot(p, do_h, "tn")
                ds = p * (_dot(do_h, vv, "nt") - delta) * ATTN_SCALE
                dq_ref[rows, 128 * h:128 * h + 128] += _dot(ds, ks[h], "nn")
                dks[h] = dks[h] + _dot(ds, qh, "tn")
            return dks[0], dks[1], dvp

        zero = jnp.zeros((tk, 128), F32)
        dka, dkb, dvp = lax.fori_loop(j + 1, S // tq, step, step(j, (zero, zero, zero), diagonal=True))
        dk_ref[:, 0:128] = dka
        dk_ref[:, 128:256] = dkb
        dv_ref[...] = dvp

    return pl.pallas_call(
        body, name="attn_bwd", grid=(4, S // tk),
        out_shape=(jax.ShapeDtypeStruct((S, 1024), F32), jax.ShapeDtypeStruct((S, 1024), F32), jax.ShapeDtypeStruct((S, 512), F32)),
        in_specs=[pl.BlockSpec((S, 256), lambda p, j: (0, p)), pl.BlockSpec((tk, 256), lambda p, j: (j, p)),
                  pl.BlockSpec((tk, 128), lambda p, j: (j, p)), pl.BlockSpec((S, 128), lambda p, j: (0, p)),
                  pl.BlockSpec((None, S, 128), lambda p, j: (p, 0, 0)), pl.BlockSpec((None, S, 128), lambda p, j: (0, 0, p))],
        out_specs=(pl.BlockSpec((S, 256), lambda p, j: (0, p)), pl.BlockSpec((tk, 256), lambda p, j: (j, p)),
                   pl.BlockSpec((tk, 128), lambda p, j: (j, p))),
        compiler_params=_cp("parallel", "arbitrary"))(q, k, v, o, lse, dycat2)


CHUNK = 128
GELU_C = math.sqrt(2.0 / math.pi)


def _gelu(v):
    t = jnp.tanh(GELU_C * (v + 0.044715 * (v * v * v)))
    return v * (0.5 * (1.0 + t)), t


def _gelu_grad(v, t):
    return 0.5 * (1.0 + t) + v * (0.5 * (1.0 - t * t) * GELU_C * (1.0 + 3.0 * 0.044715 * v * v))


def _tril(w):
    r = lax.broadcasted_iota(jnp.int32, w.shape, 0)
    c = lax.broadcasted_iota(jnp.int32, w.shape, 1)
    return jnp.where(c <= r, w, 0.0)


def _layer_norm(v, g, b):
    xc = v - jnp.mean(v, axis=-1, keepdims=True)
    rstd = lax.rsqrt(jnp.mean(xc * xc, axis=-1, keepdims=True) + EPS)
    xhat = xc * rstd
    return xhat * g + b, xhat, rstd


def _sgu_fwd(z, o, ln_g, ln_b, w_s, b_st, tr=256):
    S = z.shape[0]

    def body(u_ref, v_ref, o_ref, g_ref, b_ref, ws_ref, bs_ref, y_ref):
        gu, _ = _gelu(u_ref[...])
        gv, _ = _gelu(v_ref[...])
        vln, _, _ = _layer_norm(gv, g_ref[...], b_ref[...])
        y_ref[0] = o_ref[...].astype(y_ref.dtype)
        for g in range(4):
            wt = _tril(ws_ref[g])
            cols = slice(128 * g, 128 * g + 128)
            for ch in range(tr // CHUNK):
                rows = slice(CHUNK * ch, CHUNK * ch + CHUNK)
                mixed = _dot(wt, vln[rows, cols], "nn") + bs_ref[:, g:g + 1]
                y_ref[1, rows, cols] = (gu[rows, cols] * mixed).astype(y_ref.dtype)

    return pl.pallas_call(
        body, name="sgu_fwd", grid=(S // tr,), out_shape=jax.ShapeDtypeStruct((2, S, 512), _MXU_DTYPE),
        in_specs=[pl.BlockSpec((tr, 512), lambda i: (i, 1)), pl.BlockSpec((tr, 512), lambda i: (i, 2)), _row_spec(tr, 512),
                  _vec_spec(512), _vec_spec(512), pl.BlockSpec((4, 128, 128), lambda i: (0, 0, 0)), pl.BlockSpec((128, 4), lambda i: (0, 0))],
        out_specs=pl.BlockSpec((2, tr, 512), lambda i: (0, i, 0)), compiler_params=_cp("parallel"))(z, z, o, ln_g, ln_b, w_s, b_st)


def _sgu_bwd(z, dycat2, ln_g, ln_b, w_s, b_st, tr=256):
    S = z.shape[0]

    def body(u_ref, v_ref, dy_ref, g_ref, b_ref, ws_ref, bs_ref, duv_ref, dg_ref, db_ref, dws_ref, dbs_ref):
        first = pl.program_id(0) == 0
        u_pre, v_pre = u_ref[...], v_ref[...]
        gu, tu = _gelu(u_pre)
        gv, tv = _gelu(v_pre)
        gain = g_ref[...]
        vln, xhat, rstd = _layer_norm(gv, gain, b_ref[...])

        @pl.when(first)
        def _():
            dws_ref[...] = jnp.zeros_like(dws_ref)
            dbs_ref[...] = jnp.zeros_like(dbs_ref)

        dvln_cols = []
        for g in range(4):
            wt = _tril(ws_ref[g])
            cols = slice(128 * g, 128 * g + 128)
            dmixed_sum = jnp.zeros((CHUNK, 128), F32)
            dw = jnp.zeros((CHUNK, CHUNK), F32)
            dvln_rows = []
            for ch in range(tr // CHUNK):
                rows = slice(CHUNK * ch, CHUNK * ch + CHUNK)
                vt = vln[rows, cols]
                mixed = _dot(wt, vt, "nn") + bs_ref[:, g:g + 1]
                dyd = dy_ref[rows, cols]
                duv_ref[rows, cols] = (dyd * mixed * _gelu_grad(u_pre[rows, cols], tu[rows, cols])).astype(duv_ref.dtype)
                dmixed = dyd * gu[rows, cols]
                dmixed_sum = dmixed_sum + dmixed
                dw = dw + _dot(dmixed, vt, "nt")
                dvln_rows.append(_dot(wt, dmixed, "tn"))
            dws_ref[g] += _tril(dw)
            dbs_ref[g:g + 1, :] += jnp.sum(dmixed_sum.T, axis=0, keepdims=True)
            dvln_cols.append(jnp.concatenate(dvln_rows, axis=0))
        dvln = jnp.concatenate(dvln_cols, axis=1)
        _acc_rows(dg_ref, dvln * xhat, first)
        _acc_rows(db_ref, dvln, first)
        dxhat = dvln * gain
        dgv = rstd * (dxhat - jnp.mean(dxhat, axis=-1, keepdims=True) - xhat * jnp.mean(dxhat * xhat, axis=-1, keepdims=True))
        duv_ref[:, 512:1024] = (dgv * _gelu_grad(v_pre, tv)).astype(duv_ref.dtype)

    return pl.pallas_call(
        body, name="sgu_bwd", grid=(S // tr,),
        out_shape=(jax.ShapeDtypeStruct((S, 1024), _MXU_DTYPE), jax.ShapeDtypeStruct((1, 512), F32), jax.ShapeDtypeStruct((1, 512), F32),
                   jax.ShapeDtypeStruct((4, 128, 128), F32), jax.ShapeDtypeStruct((4, 128), F32)),
        in_specs=[pl.BlockSpec((tr, 512), lambda i: (i, 1)), pl.BlockSpec((tr, 512), lambda i: (i, 2)),
                  pl.BlockSpec((None, tr, 512), lambda i: (1, i, 0)), _vec_spec(512), _vec_spec(512),
                  pl.BlockSpec((4, 128, 128), lambda i: (0, 0, 0)), pl.BlockSpec((128, 4), lambda i: (0, 0))],
        out_specs=(_row_spec(tr, 1024), _vec_spec(512), _vec_spec(512), pl.BlockSpec((4, 128, 128), lambda i: (0, 0, 0)),
                   pl.BlockSpec((4, 128), lambda i: (0, 0))),
        compiler_params=_cp("arbitrary"))(z, z, dycat2, ln_g, ln_b, w_s, b_st)


def _sum_parts(name, parts, tr=512):
    P, R, C = parts.shape
    tr = _tile(R, tr) if R % 8 == 0 else R

    def body(p_ref, o_ref):
        g = p_ref[0]
        for k in range(1, P):
            g = g + p_ref[k]
        o_ref[...] = g

    return pl.pallas_call(
        body, name=name, grid=(R // tr,), out_shape=jax.ShapeDtypeStruct((R, C), F32),
        in_specs=[pl.BlockSpec((P, tr, C), lambda i: (0, i, 0))], out_specs=_row_spec(tr, C),
        compiler_params=_cp("parallel"))(parts)


def _adamw_math(w, m, v, g):
    c1 = 1.0 / (1.0 - ADAM_B1 ** ADAM_STEP)
    c2 = 1.0 / (1.0 - ADAM_B2 ** ADAM_STEP)
    m2 = ADAM_B1 * m + (1.0 - ADAM_B1) * g
    v2 = ADAM_B2 * v + (1.0 - ADAM_B2) * (g * g)
    return -ADAM_LR * ((m2 * c1) / (jnp.sqrt(v2 * c2) + ADAM_EPS) + ADAM_WD * w), m2, v2


def _adamw_small(name, params, parts):
    n = len(params)

    def body(*refs):
        ins, outs = refs[:4 * n], refs[4 * n:]
        for i in range(n):
            w_ref, m_ref, v_ref, p_ref = ins[4 * i:4 * i + 4]
            g = p_ref[0]
            for k in range(1, N_DEV):
                g = g + p_ref[k]
            delta, m2, v2 = _adamw_math(w_ref[...], m_ref[...], v_ref[...], g)
            outs[4 * i][...] = g
            outs[4 * i + 1][...] = delta
            outs[4 * i + 2][...] = m2
            outs[4 * i + 3][...] = v2

    flat = [a for (w, m, v), p in zip(params, parts) for a in (w, m, v, p)]
    out = pl.pallas_call(
        body, name=name, out_shape=[jax.ShapeDtypeStruct(w.shape, F32) for (w, _, _) in params for _ in range(4)],
        compiler_params=pltpu.CompilerParams(vmem_limit_bytes=_VMEM_LIMIT))(*flat)
    return [out[4 * i:4 * i + 4] for i in range(n)]


ADAMW_BLOCK_BYTES = 36 * 2 ** 20


def _adamw(name, w, m, v, parts):
    L, R, C = w.shape
    P = parts[0].shape[0]
    row_bytes = 2 * C * (7 * 4 + P * parts[0].dtype.itemsize)
    tr = R
    if R * row_bytes > ADAMW_BLOCK_BYTES:
        tr = next(t for t in (1024, 512, 256, 128, 64, 32, 16) if R % t == 0 and t * row_bytes <= ADAMW_BLOCK_BYTES)
    nr = R // tr
    c1 = 1.0 / (1.0 - ADAM_B1 ** ADAM_STEP)
    c2 = 1.0 / (1.0 - ADAM_B2 ** ADAM_STEP)

    def body(w_ref, m_ref, v_ref, *rest):
        p_refs, (g_ref, d_ref, mo_ref, vo_ref) = rest[:L], rest[L:]
        for ll in range(L):
            @pl.when(pl.program_id(0) == ll)
            def _(p_ref=p_refs[ll]):
                g = p_ref[0].astype(F32)
                for k in range(1, P):
                    g = g + p_ref[k].astype(F32)
                m2 = ADAM_B1 * m_ref[...] + (1.0 - ADAM_B1) * g
                v2 = ADAM_B2 * v_ref[...] + (1.0 - ADAM_B2) * (g * g)
                g_ref[...] = g
                mo_ref[...] = m2
                vo_ref[...] = v2
                d_ref[...] = -ADAM_LR * ((m2 * c1) / (jnp.sqrt(v2 * c2) + ADAM_EPS) + ADAM_WD * w_ref[...])

    def part_spec(ll):
        return pl.BlockSpec((P, tr, C), lambda l, i: (0, jnp.where(l == ll, i, jnp.where(l < ll, 0, nr - 1)), 0))

    full = pl.BlockSpec((None, tr, C), lambda l, i: (l, i, 0))
    sds = jax.ShapeDtypeStruct((L, R, C), F32)
    return pl.pallas_call(
        body, name=name, grid=(L, nr), out_shape=(sds, sds, sds, sds),
        in_specs=[full] * 3 + [part_spec(ll) for ll in range(L)],
        out_specs=(full,) * 4, compiler_params=_cp("arbitrary", "arbitrary"))(w, m, v, *parts)


def _rope_tables(positions):
    half = 16
    inv_freq = 10000.0 ** (-jnp.arange(half, dtype=F32) / half)
    ang = positions.astype(F32)[:, None] * inv_freq
    cos, sin = jnp.cos(ang), jnp.sin(ang)
    S = positions.shape[0]
    z16, z32, z64 = jnp.zeros((S, 16), F32), jnp.zeros((S, 32), F32), jnp.zeros((S, 64), F32)
    cosk = jnp.concatenate([z64, cos, cos, z32], axis=1)
    cosq = jnp.concatenate([jnp.ones((S, 64), F32), cos, cos, z32], axis=1)
    sa = jnp.concatenate([z64, -sin, z16, z32], axis=1)
    sb = jnp.concatenate([z64, z16, sin, z32], axis=1)
    return cosq, cosk, sa, sb


def _ffn_fwd(l, x, mod, n2g, get_w_up8, cw24, get_w_down4):
    sh, sc, gate = mod
    h = _rmsmod_fwd(f"ffn{l}_norm", x, n2g, sc, sh, n2g)
    w_up8 = get_w_up8(h)
    u8 = _mm_cols(f"ffn{l}_up", h, w_up8, out_dtype=ACT_DTYPE, tm=1024)
    S, n = u8.shape[1], u8.shape[2]
    u24 = u8.reshape(2, 4, S, n)
    a4 = _ffn_gate_fwd(f"ffn{l}_gate", u24, cw24)
    w_down4 = get_w_down4(a4)
    f, x_new = _mm_rows_resid(f"ffn{l}_down", a4, w_down4, x, gate)
    return x_new, (x, h, u24, a4, f), w_up8, w_down4


def _ffn_bwd(l, dx, saved, mod, n2g, w_up8, cw24, w_down4, me):
    sh, sc, gate = mod
    x, h, u24, a4, f = saved
    df, dgate = _gate_bwd(f"ffn{l}_gate_bwd", dx, f, gate)
    da4 = _mm_rows_dx(f"ffn{l}_down_dx", df, w_down4, out_dtype=ACT_DTYPE, tm=2048)
    dw_down4 = _mm_rows_dw(f"ffn{l}_down_dw", a4, df, out_dtype=WIRE_DTYPE)
    sent_down, token = _exchange_start(f"scatter_ffn{l}_down", [dw_down4.reshape(8, 352, dw_down4.shape[2])], True, dgate, me)
    du24, dcw24, dh = _ffn_gate_bwd(f"ffn{l}_act_bwd", u24, cw24, da4, w_up8.reshape((2, 4) + w_up8.shape[1:]), token)
    du8 = du24.reshape((8,) + du24.shape[2:])
    dw_up8t = _mm_cols_dwt(f"ffn{l}_up_dw", h, du8, out_dtype=WIRE_DTYPE, tk=1024)
    sent_up, token = _exchange_start(f"scatter_ffn{l}_up", [dw_up8t], True, dcw24, me)
    dx_new, dn2g, dsc, dsh = _rmsmod_bwd(f"ffn{l}_norm_bwd", x, n2g, sc, dh, dx, token)
    return dx_new, dict(sent_up=sent_up, sent_down=sent_down, cw24=dcw24, n2g=dn2g, mod=(dsh, dsc, dgate))


def kernel(x, c, positions, ada_w, ada_b, norm1_g, norm2_g, ab_w_in, a_conv_w, b_mix_w, b_scale, ab_w_out, cd_w_in, c_q_norm_g, c_w_uq, c_kv_norm_g, c_w_ukv, d_ln_g, d_ln_b, d_w_s, d_b_s, cd_w_out, ffn_w_up, ffn_conv_w, ffn_w_down, final_norm_g, loss_target, m_ada_w, m_ada_b, m_norm1_g, m_norm2_g, m_ab_w_in, m_a_conv_w, m_b_mix_w, m_b_scale, m_ab_w_out, m_cd_w_in, m_c_q_norm_g, m_c_w_uq, m_c_kv_norm_g, m_c_w_ukv, m_d_ln_g, m_d_ln_b, m_d_w_s, m_d_b_s, m_cd_w_out, m_ffn_w_up, m_ffn_conv_w, m_ffn_w_down, m_final_norm_g, v_ada_w, v_ada_b, v_norm1_g, v_norm2_g, v_ab_w_in, v_a_conv_w, v_b_mix_w, v_b_scale, v_ab_w_out, v_cd_w_in, v_c_q_norm_g, v_c_w_uq, v_c_kv_norm_g, v_c_w_ukv, v_d_ln_g, v_d_ln_b, v_d_w_s, v_d_b_s, v_cd_w_out, v_ffn_w_up, v_ffn_conv_w, v_ffn_w_down, v_final_norm_g):
    S, D = x.shape[1], x.shape[2]
    me = 4 * lax.axis_index("x") + 2 * lax.axis_index("y") + lax.axis_index("c")
    x0, target = x[0], loss_target[0]
    W = _MXU_DTYPE

    small_shapes = [(1024,), (3, 64), (32,), (64,), (64,), (2, 3, 704)]
    (g0,) = _exchange("gather_small", [[_pack([c, a_conv_w, c_q_norm_g, d_ln_g, d_ln_b, ffn_conv_w])]], scatter=False)
    c_all, aconv_s, qg_s, lng_s, lnb_s, fcw_s = _unpack(g0[:, 0], small_shapes, lead=(N_DEV,))
    conv_w = aconv_s.transpose(1, 0, 2).reshape(3, 512)
    qg, ln_g, ln_b = qg_s.reshape(1, 256), lng_s.reshape(1, 512), lnb_s.reshape(1, 512)
    cw24 = [fcw_s[:, l].reshape(2, 4, 3, 704) for l in range(2)]
    c16 = jnp.pad(c_all, ((0, 16 - N_DEV), (0, 0)))

    mod_cols = _ada_fwd(c16, ada_w)
    (g1,) = _exchange("gather_mod", [[_pack([mod_cols])]], scatter=False)
    mod_all = _unpack(g1[:, 0], [(2, 16, 768)], lead=(N_DEV,))[0]
    mod_mine = lax.dynamic_index_in_dim(mod_all, me, axis=2, keepdims=False)
    mod = mod_mine.transpose(1, 0, 2).reshape(2, 6 * D) + ada_b
    mods = [[mod[l, k * D:(k + 1) * D].reshape(1, D) for k in range(6)] for l in range(2)]

    gw_ab, token = _hier_gather_start("gather_w_ab", [ab_w_in[0].astype(W), ab_w_out[0].astype(W)], mod, me)
    gw_up0, token = _hier_gather_start("gather_w_ffn0_up", [ffn_w_up[0].astype(W)], token, me)
    gw_rest, started = _exchange_start("gather_w_rest", [
        ffn_w_down[0].astype(W), cd_w_in[0].T.astype(W), c_w_uq[0].T.astype(W), c_w_ukv[0].astype(W), cd_w_out[0].astype(W),
        ffn_w_up[1].astype(W), ffn_w_down[1].astype(W)], False, token, me)

    cosq, cosk, sa, sb = _rope_tables(positions[0])
    n1g = [norm1_g[l].reshape(1, D) for l in range(2)]
    n2g = [norm2_g[l].reshape(1, D) for l in range(2)]
    mix_w, scale = b_mix_w[0], b_scale
    kvg = c_kv_norm_g
    w_s, b_st = d_w_s[0], d_b_s[0].T

    sh1, sc1, g1m = mods[0][:3]
    h_ab = _rmsmod_fwd("ab_norm", x0, n1g[0], sc1, sh1, started)
    w_abin8, w_about = _hier_gather_wait("wait_w_ab", _hier_gather_forward("forward_w_ab", gw_ab, h_ab), h_ab)
    w_about2 = w_about.reshape(2, 512, D)
    z8 = _mm_cols("ab_in", h_ab, w_abin8, tm=2048)
    ycat_ab = _ab_mix_fwd(z8, conv_w, mix_w, scale)
    y_ab, x1 = _mm_rows_resid("ab_out", ycat_ab, w_about2, x0, g1m)
    w_up8, w_down4 = [None, None], [None, None]
    gw_up0 = _hier_gather_forward("forward_w_ffn0_up", gw_up0, x1)
    x2, ffn0_saved, w_up8[0], w_down4[0] = _ffn_fwd(
        0, x1, mods[0][3:], n2g[0], lambda after: _hier_gather_wait("wait_w_ffn0_up", gw_up0, after)[0], cw24[0],
        lambda after: _exchange_wait("wait_w_ffn0_down", gw_rest, after, [0])[0].reshape(4, 704, D))

    w_cdin, w_uq, w_ukv, w_cdout = _exchange_wait("wait_w_cd", gw_rest, x2, [1, 2, 3, 4])
    w_cdout2 = w_cdout.reshape(2, 512, D)
    w_cd_t = w_cdin.reshape(1440, D)
    zr = lambda n: jnp.zeros((n, D), W)
    w_cd_pad = jnp.concatenate([w_cd_t[:384], zr(64), w_cd_t[384:416], zr(32), w_cd_t[416:]], axis=0)
    w_uq_pad = jnp.pad(w_uq, ((0, 0), (0, 32), (0, 0))).reshape(1024, 256)
    w_ukv_h = w_ukv.transpose(1, 0, 2)
    w_k_pad = jnp.pad(w_ukv_h[:, :, :64], ((0, 0), (0, 0), (0, 64))).reshape(128, 1024)
    w_kv_pad = jnp.concatenate([w_k_pad, w_ukv_h[:, :, 64:].reshape(128, 512)], axis=1)

    sh1, sc1, g1c = mods[1][:3]
    h_cd = _rmsmod_fwd("cd_norm", x2, n1g[1], sc1, sh1, n1g[1])
    z_cd = _mm_nt("cd_in", h_cd, w_cd_pad)
    qn, kvn = _mla_prep_fwd(z_cd, qg, kvg)
    qraw = _mm_nt("cd_uq", qn, w_uq_pad)
    kvall = _mm_nn("cd_ukv", kvn, w_kv_pad)
    q_r, k_r, v_r = _rope_fwd(qraw, kvall, z_cd, cosq, cosk, sa, sb)
    o, lse = _attn_fwd(q_r, k_r, v_r)
    ycat_cd = _sgu_fwd(z_cd, o, ln_g, ln_b, w_s, b_st)
    y_cd, x3 = _mm_rows_resid("cd_out", ycat_cd, w_cdout2, x2, g1c)
    x4, ffn1_saved, w_up8[1], w_down4[1] = _ffn_fwd(
        1, x3, mods[1][3:], n2g[1], lambda after: _exchange_wait("wait_w_ffn1_up", gw_rest, after, [5])[0], cw24[1],
        lambda after: _exchange_wait("wait_w_ffn1_down", gw_rest, after, [6])[0].reshape(4, 704, D))

    loss_local, dx4, dfg = _loss_head(x4, final_norm_g.reshape(1, D), target)

    dx3, gf1 = _ffn_bwd(1, dx4, ffn1_saved, mods[1][3:], n2g[1], w_up8[1], cw24[1], w_down4[1], me)

    dy, dg1c = _gate_bwd("cd_gate_bwd", dx3, y_cd, g1c)
    dycat = _mm_rows_dx("cd_out_dx", dy, w_cdout2)
    dw_cdout = _mm_rows_dw("cd_out_dw", ycat_cd, dy, out_dtype=WIRE_DTYPE)
    duv, dln_g, dln_b, dws, dbs = _sgu_bwd(z_cd, dycat, ln_g, ln_b, w_s, b_st)
    dq_r, dk_r, dv_r = _attn_bwd(q_r, k_r, v_r, o, lse, dycat)
    dqraw, dkvall, dkpe = _rope_bwd(dq_r, dk_r, dv_r, cosq, cosk, sa, sb)
    dqn = _mm_nn("cd_uq_dx", dqraw, w_uq_pad, tn=256)
    dkvn = _mm_nt("cd_ukv_dx", dkvall, w_kv_pad, tn=128)
    dw_uq_pad = _mm_tn("cd_uq_dw", dqraw, qn, tn=256)
    dw_kv_pad = _mm_tn("cd_ukv_dw", kvn, dkvall, tm=128)
    dz_cd, dqg, dkvg = _mla_prep_bwd(z_cd, qg, kvg, dqn, dkvn, dkpe, duv)
    dh_cd = _mm_nn("cd_in_dx", dz_cd, w_cd_pad)
    dw_cd_pad = _mm_tn("cd_in_dw", dz_cd, h_cd)
    dw_cd8 = jnp.concatenate([dw_cd_pad[:384], dw_cd_pad[448:480], dw_cd_pad[512:]], axis=0).astype(WIRE_DTYPE).reshape(8, 180, D)
    dw_uq8 = dw_uq_pad.reshape(8, 128, 256)[:, :96].astype(WIRE_DTYPE)
    dw_ukv8 = jnp.concatenate([dw_kv_pad[:, :1024].reshape(128, 8, 128)[:, :, :64], dw_kv_pad[:, 1024:].reshape(128, 8, 64)],
                              axis=2).transpose(1, 0, 2).astype(WIRE_DTYPE)
    sent_cd, token = _exchange_start("scatter_cd", [dw_cd8, dw_uq8, dw_ukv8, dw_cdout.reshape(8, 128, D)], True, dqg, me)
    early_names = ["c_kv_norm_g", "d_w_s", "d_b_s", "final_norm_g", "c_q_norm_g", "d_ln_g", "d_ln_b"]
    early_grads = [dkvg, dws.reshape(512, 128), dbs, dfg, dqg.reshape(8, 1, 32), dln_g.reshape(8, 1, 64), dln_b.reshape(8, 1, 64)]
    early_sent, token = _exchange_start("gather_small_grads_early", early_grads, [False] * 4 + [True] * 3, token, me)
    dx2, dn1g_cd, dsc1_cd, dsh1_cd = _rmsmod_bwd("cd_norm_bwd", x2, n1g[1], sc1, dh_cd, dx3, token)

    dx1, gf0 = _ffn_bwd(0, dx2, ffn0_saved, mods[0][3:], n2g[0], w_up8[0], cw24[0], w_down4[0], me)

    dy, dg1m = _gate_bwd("ab_gate_bwd", dx1, y_ab, g1m)
    dw_about = _mm_rows_dw("ab_out_dw", ycat_ab, dy, out_dtype=WIRE_DTYPE)
    sent_about, token = _exchange_start("scatter_ab_out", [dw_about.reshape(8, 128, D)], True, dg1m, me)
    dycat = _mm_rows_dx("ab_out_dx", dy, w_about2)
    dz8, dconv_w, dmix_w, dscale = _ab_mix_bwd(z8, dycat, conv_w, mix_w, scale, token)
    dz8 = dz8.reshape(8, S, 256)
    dw_abin8 = _mm_cols_dw("ab_in_dw", h_ab, dz8, out_dtype=WIRE_DTYPE, tk=1024)
    sent_abin, token = _exchange_start("scatter_ab_in", [dw_abin8], True, dscale, me)
    dh_ab = _mm_cols_dx("ab_in_dx", dz8, w_abin8)
    dx0, dn1g_ab, dsc1_ab, dsh1_ab = _rmsmod_bwd("ab_norm_bwd", x0, n1g[0], mods[0][1], dh_ab, dx1, token)

    dmod = jnp.stack([jnp.concatenate([dsh1_ab, dsc1_ab, dg1m, *gf0["mod"]], axis=1)[0],
                      jnp.concatenate([dsh1_cd, dsc1_cd, dg1c, *gf1["mod"]], axis=1)[0]])
    late_names = ["ada_b", "norm1_g", "norm2_g", "b_mix_w", "b_scale", "a_conv_w", "ffn_conv_w"]
    late_grads = [dmod, jnp.concatenate([dn1g_ab, dn1g_cd]), jnp.concatenate([gf0["n2g"], gf1["n2g"]]),
                  dmix_w.reshape(512, 128), dscale, dconv_w.reshape(3, 8, 64).transpose(1, 0, 2),
                  jnp.stack([gf0["cw24"].reshape(8, 3, 704), gf1["cw24"].reshape(8, 3, 704)], axis=1)]
    small_view = dict(ada_b=(2, 6 * D), norm1_g=(2, D), norm2_g=(2, D), b_mix_w=(512, 128), b_scale=(1, 512), c_kv_norm_g=(1, 128),
                      d_w_s=(512, 128), d_b_s=(4, 128), final_norm_g=(1, D),
                      a_conv_w=(3, 64), c_q_norm_g=(1, 32), d_ln_g=(1, 64), d_ln_b=(1, 64), ffn_conv_w=(2, 3, 704))
    late_sent, token = _exchange_start("gather_small_grads_late", late_grads, [False] * 5 + [True] * 2, dx0, me)

    res = {}

    def update(name, w, m, v, parts, shape3d):
        outs = _adamw("adamw_" + name, w.reshape(shape3d), m.reshape(shape3d), v.reshape(shape3d),
                      [p.reshape((p.shape[0],) + shape3d[1:]) for p in parts])
        res[name] = [o_.reshape(w.shape) for o_ in outs]

    p_cdin, p_uq, p_ukv, p_cdout = _exchange_wait("wait_scatter_cd", sent_cd, token)
    swap = lambda a: jnp.swapaxes(a, 1, 2)
    update("cd_w_in", swap(cd_w_in), swap(m_cd_w_in), swap(v_cd_w_in), [p_cdin], (1, 180, D))
    update("c_w_uq", swap(c_w_uq), swap(m_c_w_uq), swap(v_c_w_uq), [p_uq], (1, 96, 256))
    for name in ("cd_w_in", "c_w_uq"):
        res[name] = [swap(o_) for o_ in res[name]]
    update("c_w_ukv", c_w_ukv, m_c_w_ukv, v_c_w_ukv, [p_ukv], (1, 128, 128))
    update("cd_w_out", cd_w_out, m_cd_w_out, v_cd_w_out, [p_cdout], (1, 128, D))
    (p_dn1,) = _exchange_wait("wait_scatter_ffn1_down", gf1["sent_down"], token)
    (p_dn0,) = _exchange_wait("wait_scatter_ffn0_down", gf0["sent_down"], res["cd_w_out"][0])
    update("ffn_w_down", ffn_w_down, m_ffn_w_down, v_ffn_w_down, [p_dn0, p_dn1], (2, 352, D))
    (p_up1,) = _exchange_wait("wait_scatter_ffn1_up", gf1["sent_up"], token)
    (p_up0,) = _exchange_wait("wait_scatter_ffn0_up", gf0["sent_up"], res["ffn_w_down"][0])
    swap = lambda a: jnp.swapaxes(a, 1, 2)
    update("ffn_w_up", swap(ffn_w_up), swap(m_ffn_w_up), swap(v_ffn_w_up), [p_up0, p_up1], (2, 704, D))
    up_done = res["ffn_w_up"][0]
    res["ffn_w_up"] = [swap(o_) for o_ in res["ffn_w_up"]]
    (p_about,) = _exchange_wait("wait_scatter_ab_out", sent_about, up_done)
    update("ab_w_out", ab_w_out, m_ab_w_out, v_ab_w_out, [p_about], (1, 128, D))
    (p_abin,) = _exchange_wait("wait_scatter_ab_in", sent_abin, res["ab_w_out"][0])
    update("ab_w_in", ab_w_in, m_ab_w_in, v_ab_w_in, [p_abin], (1, D, 256))

    early_parts = _exchange_wait("wait_small_grads_early", early_sent, res["ab_w_in"][0])
    late_parts = _exchange_wait("wait_small_grads_late", late_sent, res["ab_w_in"][0])
    small_names = early_names + late_names
    small_parts = list(early_parts) + list(late_parts)
    dmod_all = late_parts[0]
    dmod_cols = lax.dynamic_slice_in_dim(dmod_all, me * 768, 768, axis=2).transpose(1, 0, 2)
    g_ada_w = _ada_bwd(c16, jnp.pad(dmod_cols, ((0, 0), (0, 16 - N_DEV), (0, 0))))
    update("ada_w", ada_w, m_ada_w, v_ada_w, [g_ada_w[l][None] for l in range(2)], (2, D, 768))

    small_w = dict(ada_b=(ada_b, m_ada_b, v_ada_b), norm1_g=(norm1_g, m_norm1_g, v_norm1_g), norm2_g=(norm2_g, m_norm2_g, v_norm2_g),
                   b_mix_w=(b_mix_w, m_b_mix_w, v_b_mix_w), b_scale=(b_scale, m_b_scale, v_b_scale),
                   c_kv_norm_g=(c_kv_norm_g, m_c_kv_norm_g, v_c_kv_norm_g), d_w_s=(d_w_s, m_d_w_s, v_d_w_s),
                   d_b_s=(d_b_s, m_d_b_s, v_d_b_s), final_norm_g=(final_norm_g, m_final_norm_g, v_final_norm_g),
                   a_conv_w=(a_conv_w, m_a_conv_w, v_a_conv_w), c_q_norm_g=(c_q_norm_g, m_c_q_norm_g, v_c_q_norm_g),
                   d_ln_g=(d_ln_g, m_d_ln_g, v_d_ln_g), d_ln_b=(d_ln_b, m_d_ln_b, v_d_ln_b),
                   ffn_conv_w=(ffn_conv_w, m_ffn_conv_w, v_ffn_conv_w))
    small_out = _adamw_small("adamw_small", [tuple(a.reshape(small_view[n]) for a in small_w[n]) for n in small_names],
                             list(small_parts))
    for n, outs in zip(small_names, small_out):
        res[n] = [o_.reshape(small_w[n][0].shape) for o_ in outs]

    loss = lax.psum(loss_local[0, 0], ("x", "y", "c"))
    order = ["ada_w", "ada_b", "norm1_g", "norm2_g", "ab_w_in", "a_conv_w", "b_mix_w", "b_scale", "ab_w_out", "cd_w_in", "c_q_norm_g",
             "c_w_uq", "c_kv_norm_g", "c_w_ukv", "d_ln_g", "d_ln_b", "d_w_s", "d_b_s", "cd_w_out", "ffn_w_up", "ffn_conv_w",
             "ffn_w_down", "final_norm_g"]
    return (loss, dx0[None], *[res[n][0] for n in order], *[res[n][1] for n in order], *[res[n][2] for n in order],
            *[res[n][3] for n in order])
```

```python
import functools
import math

import jax
import jax.numpy as jnp
from jax import lax
from jax.experimental import pallas as pl
from jax.experimental.pallas import tpu as pltpu

F32 = jnp.float32
BF16 = jnp.bfloat16
_MXU_DTYPE = BF16
WIRE_DTYPE = BF16
ACT_DTYPE = BF16
_VMEM_LIMIT = 56 * 2 ** 20
N_DEV = 8
EPS = 1e-6
POOL_WINDOWS = (2, 4, 8, 16)
ATTN_SCALE = (64 + 32) ** -0.5
ADAM_LR, ADAM_B1, ADAM_B2, ADAM_EPS, ADAM_WD, ADAM_STEP = 0.001, 0.9, 0.999, 1e-08, 0.01, 10
MESH = pl.DeviceIdType.MESH
ANY = pl.BlockSpec(memory_space=pl.ANY)


def _cp(*sem):
    return pltpu.CompilerParams(dimension_semantics=sem, vmem_limit_bytes=_VMEM_LIMIT)


def _dot(a, b, contract):
    dn = {"nn": (((1,), (0,)), ((), ())), "nt": (((1,), (1,)), ((), ())), "tn": (((0,), (0,)), ((), ()))}[contract]
    return lax.dot_general(a.astype(_MXU_DTYPE), b.astype(_MXU_DTYPE), dn, preferred_element_type=F32)


def _my_position():
    x, y, c = lax.axis_index("x"), lax.axis_index("y"), lax.axis_index("c")
    return x, y, c, 4 * x + 2 * y + c


def _exchange(name, groups, scatter):
    flat = [a for g in groups for a in g]
    n_in, n_grp = len(flat), len(groups)
    out_shapes = []
    for g in groups:
        slab = g[0].shape[1:] if scatter else g[0].shape
        out_shapes.append(jax.ShapeDtypeStruct((N_DEV, len(g)) + tuple(slab), g[0].dtype))

    def body(*refs):
        ins, outs = refs[:n_in], refs[n_in:n_in + n_grp]
        send_sems, recv_sems, local_sems = refs[n_in + n_grp:]
        x, y, c, me = _my_position()
        i = 0
        for gi, g in enumerate(groups):
            for l in range(len(g)):
                src = ins[i]
                i += 1
                pltpu.make_async_copy(src.at[me] if scatter else src, outs[gi].at[me, l], local_sems.at[gi]).start()
                for k in range(1, N_DEV):
                    px = 1 - x if k & 4 else x
                    py = 1 - y if k & 2 else y
                    pc = 1 - c if k & 1 else c
                    peer = 4 * px + 2 * py + pc
                    pltpu.make_async_remote_copy(
                        src_ref=src.at[peer] if scatter else src, dst_ref=outs[gi].at[me, l],
                        send_sem=send_sems.at[gi], recv_sem=recv_sems.at[gi],
                        device_id=(px, py, pc), device_id_type=MESH).start()
        for gi in range(n_grp):
            mine = outs[gi].at[me]
            pltpu.make_async_copy(mine, mine, local_sems.at[gi]).wait()
            seven = outs[gi].at[pl.ds(0, N_DEV - 1)]
            w = pltpu.make_async_remote_copy(src_ref=seven, dst_ref=seven, send_sem=send_sems.at[gi],
                                             recv_sem=recv_sems.at[gi], device_id=(x, y, c), device_id_type=MESH)
            w.wait_send()
            w.wait_recv()

    return pl.pallas_call(
        body, name=name, out_shape=tuple(out_shapes),
        in_specs=[ANY] * n_in, out_specs=tuple([ANY] * n_grp),
        scratch_shapes=[pltpu.SemaphoreType.DMA((n_grp,)), pltpu.SemaphoreType.DMA((n_grp,)),
                        pltpu.SemaphoreType.DMA((n_grp,))],
        compiler_params=pltpu.CompilerParams(has_side_effects=True),
    )(*flat)


HBM_SPEC = pl.BlockSpec(memory_space=pltpu.HBM)
SEM_SPEC = pl.BlockSpec(memory_space=pltpu.SEMAPHORE)
EFFECT = pltpu.SideEffectType.DATAFLOW_SIDE_EFFECTING


def _put_mine(name, srcs, scatter, me):
    n = len(srcs)
    slabs = [tuple(s.shape[1:] if sc else s.shape) for s, sc in zip(srcs, scatter)]

    def body(me_ref, *refs):
        for i in range(n):
            refs[n + i][...] = refs[i][...]

    def at_me(slab):
        return pl.BlockSpec((None,) + slab, lambda g, me_ref, nd=len(slab): (me_ref[0],) + (0,) * nd)

    def whole(slab):
        return pl.BlockSpec(slab, lambda g, me_ref, nd=len(slab): (0,) * nd)

    return pl.pallas_call(
        body, name=name,
        grid_spec=pltpu.PrefetchScalarGridSpec(
            num_scalar_prefetch=1, grid=(1,),
            in_specs=[at_me(slab) if sc else whole(slab) for slab, sc in zip(slabs, scatter)],
            out_specs=[at_me(slab) for slab in slabs]),
        out_shape=[jax.ShapeDtypeStruct((N_DEV,) + slab, s.dtype) for slab, s in zip(slabs, srcs)],
        compiler_params=_cp("arbitrary"))(me.reshape(1), *srcs)


def _exchange_start(name, srcs, scatter, after, me):
    n = len(srcs)
    scatter = list(scatter) if isinstance(scatter, (list, tuple)) else [scatter] * n
    lands = _put_mine(name + "_mine", srcs, scatter, me)
    srcs = [pltpu.with_memory_space_constraint(a, pltpu.HBM) for a in srcs]
    lands = [pltpu.with_memory_space_constraint(a, pltpu.HBM) for a in lands]

    def body(*refs):
        ins, land = refs[:n], refs[n:2 * n]
        send_sems, recv_sems, token = refs[2 * n + 1], refs[2 * n + 2], refs[-1]
        x, y, c, me_in = _my_position()
        for i in range(n):
            for k in range(1, N_DEV):
                px = 1 - x if k & 4 else x
                py = 1 - y if k & 2 else y
                pc = 1 - c if k & 1 else c
                pltpu.make_async_remote_copy(
                    src_ref=ins[i].at[4 * px + 2 * py + pc] if scatter[i] else ins[i], dst_ref=land[i].at[me_in],
                    send_sem=send_sems.at[i], recv_sem=recv_sems.at[i],
                    device_id=(px, py, pc), device_id_type=MESH).start()
        token[...] = jnp.zeros_like(token)

    outs = pl.pallas_call(
        body, name=name,
        out_shape=(pltpu.SemaphoreType.DMA((n,)), pltpu.SemaphoreType.DMA((n,)),
                   *[pltpu.HBM(a.shape, a.dtype) for a in srcs], *[pltpu.HBM(a.shape, a.dtype) for a in lands],
                   jax.ShapeDtypeStruct((8, 128), F32)),
        in_specs=[HBM_SPEC] * (2 * n) + [ANY],
        out_specs=(SEM_SPEC, SEM_SPEC, *[HBM_SPEC] * (2 * n), pl.BlockSpec(memory_space=pltpu.VMEM)),
        input_output_aliases={i: 2 + i for i in range(2 * n)},
        compiler_params=pltpu.CompilerParams(has_side_effects=EFFECT),
    )(*srcs, *lands, after)
    return (outs[0], outs[1], outs[2:2 + n], outs[2 + n:2 + 2 * n]), outs[-1]


def _exchange_wait(name, handle, after, which=None):
    send_sems, recv_sems, srcs, lands = handle
    which = list(range(len(srcs))) if which is None else list(which)
    srcs, lands = [srcs[i] for i in which], [lands[i] for i in which]
    n = len(srcs)

    def body(*refs):
        land, send_ref, recv_ref = refs[n:2 * n], refs[2 * n], refs[2 * n + 1]
        x, y, c, _ = _my_position()
        for k, i in enumerate(which):
            seven = land[k].at[pl.ds(0, N_DEV - 1)]
            w = pltpu.make_async_remote_copy(src_ref=seven, dst_ref=seven, send_sem=send_ref.at[i], recv_sem=recv_ref.at[i],
                                             device_id=(x, y, c), device_id_type=MESH)
            w.wait_send()
            w.wait_recv()

    outs = pl.pallas_call(
        body, name=name,
        out_shape=(*[pltpu.HBM(a.shape, a.dtype) for a in srcs], *[pltpu.HBM(a.shape, a.dtype) for a in lands]),
        in_specs=[HBM_SPEC] * (2 * n) + [SEM_SPEC, SEM_SPEC, ANY],
        out_specs=tuple([HBM_SPEC] * (2 * n)),
        input_output_aliases={i: i for i in range(2 * n)},
        compiler_params=pltpu.CompilerParams(has_side_effects=EFFECT),
    )(*srcs, *lands, send_sems, recv_sems, after)
    return outs[n:]


def _other_chips(x, y):
    return [(1 - x, y), (x, 1 - y), (1 - x, 1 - y)]


def _hier_gather_start(name, srcs, after, me):
    n = len(srcs)
    lands = _put_mine(name + "_mine", srcs, [False] * n, me)
    srcs = [pltpu.with_memory_space_constraint(a, pltpu.HBM) for a in srcs]
    lands = [pltpu.with_memory_space_constraint(a, pltpu.HBM) for a in lands]

    def body(*refs):
        ins, land = refs[:n], refs[n:2 * n]
        ici_send, ici_recv, d2d_send, d2d_recv = refs[2 * n + 1:2 * n + 5]
        token = refs[-1]
        x, y, c, me_in = _my_position()
        for i in range(n):
            pltpu.make_async_remote_copy(src_ref=ins[i], dst_ref=land[i].at[me_in], send_sem=d2d_send.at[i], recv_sem=d2d_recv.at[i],
                                         device_id=(x, y, 1 - c), device_id_type=MESH).start()
            for px, py in _other_chips(x, y):
                pltpu.make_async_remote_copy(src_ref=ins[i], dst_ref=land[i].at[me_in], send_sem=ici_send.at[i],
                                             recv_sem=ici_recv.at[i], device_id=(px, py, c), device_id_type=MESH).start()
        token[...] = jnp.zeros_like(token)

    sem = pltpu.SemaphoreType.DMA((n,))
    outs = pl.pallas_call(
        body, name=name,
        out_shape=(sem, sem, sem, sem, *[pltpu.HBM(a.shape, a.dtype) for a in srcs], *[pltpu.HBM(a.shape, a.dtype) for a in lands],
                   jax.ShapeDtypeStruct((8, 128), F32)),
        in_specs=[HBM_SPEC] * (2 * n) + [ANY],
        out_specs=(SEM_SPEC,) * 4 + (HBM_SPEC,) * (2 * n) + (pl.BlockSpec(memory_space=pltpu.VMEM),),
        input_output_aliases={i: 4 + i for i in range(2 * n)},
        compiler_params=pltpu.CompilerParams(has_side_effects=EFFECT),
    )(*srcs, *lands, after)
    return (outs[:4], outs[4:4 + n], outs[4 + n:4 + 2 * n]), outs[-1]


def _hier_gather_forward(name, handle, after):
    sems, srcs, lands = handle
    n = len(srcs)

    def body(*refs):
        land = refs[n:2 * n]
        ici_send, ici_recv, d2d_send, d2d_recv = refs[2 * n:2 * n + 4]
        x, y, c, _ = _my_position()
        for i in range(n):
            three = land[i].at[pl.ds(0, 3)]
            pltpu.make_async_remote_copy(src_ref=three, dst_ref=three, send_sem=ici_send.at[i], recv_sem=ici_recv.at[i],
                                         device_id=(x, y, c), device_id_type=MESH).wait_recv()
            for px, py in _other_chips(x, y):
                slab = land[i].at[4 * px + 2 * py + c]
                pltpu.make_async_remote_copy(src_ref=slab, dst_ref=slab, send_sem=d2d_send.at[i], recv_sem=d2d_recv.at[i],
                                             device_id=(x, y, 1 - c), device_id_type=MESH).start()

    outs = pl.pallas_call(
        body, name=name,
        out_shape=(*[pltpu.HBM(a.shape, a.dtype) for a in srcs], *[pltpu.HBM(a.shape, a.dtype) for a in lands]),
        in_specs=[HBM_SPEC] * (2 * n) + [SEM_SPEC] * 4 + [ANY],
        out_specs=tuple([HBM_SPEC] * (2 * n)),
        input_output_aliases={i: i for i in range(2 * n)},
        compiler_params=pltpu.CompilerParams(has_side_effects=EFFECT),
    )(*srcs, *lands, *sems, after)
    return (sems, outs[:n], outs[n:])


def _hier_gather_wait(name, handle, after):
    sems, srcs, lands = handle
    n = len(srcs)

    def body(*refs):
        land = refs[n:2 * n]
        ici_send, ici_recv, d2d_send, d2d_recv = refs[2 * n:2 * n + 4]
        x, y, c, _ = _my_position()
        for i in range(n):
            three, four = land[i].at[pl.ds(0, 3)], land[i].at[pl.ds(0, 4)]
            pltpu.make_async_remote_copy(src_ref=three, dst_ref=three, send_sem=ici_send.at[i], recv_sem=ici_recv.at[i],
                                         device_id=(x, y, c), device_id_type=MESH).wait_send()
            w = pltpu.make_async_remote_copy(src_ref=four, dst_ref=four, send_sem=d2d_send.at[i], recv_sem=d2d_recv.at[i],
                                             device_id=(x, y, c), device_id_type=MESH)
            w.wait_send()
            w.wait_recv()

    outs = pl.pallas_call(
        body, name=name,
        out_shape=(*[pltpu.HBM(a.shape, a.dtype) for a in srcs], *[pltpu.HBM(a.shape, a.dtype) for a in lands]),
        in_specs=[HBM_SPEC] * (2 * n) + [SEM_SPEC] * 4 + [ANY],
        out_specs=tuple([HBM_SPEC] * (2 * n)),
        input_output_aliases={i: i for i in range(2 * n)},
        compiler_params=pltpu.CompilerParams(has_side_effects=EFFECT),
    )(*srcs, *lands, *sems, after)
    return outs[n:]


def _pack(arrs):
    flat = jnp.concatenate([a.reshape(-1).astype(F32) for a in arrs])
    n = flat.shape[0]
    rows = -(-n // 1024) * 8
    return jnp.pad(flat, (0, rows * 128 - n)).reshape(rows, 128)


def _unpack(buf, shapes, lead=()):
    flat = buf.reshape(lead + (-1,))
    out, off = [], 0
    for s in shapes:
        n = math.prod(s)
        out.append(flat[..., off:off + n].reshape(lead + tuple(s)))
        off += n
    return out


def _mm(name, a, a_spec, b, b_spec, out_sds, o_spec, grid, contract, nk=1, stacked=0):
    o_blk = tuple(d for d in o_spec.block_shape if d is not None)

    def body(a_ref, b_ref, o_ref, *acc):
        if stacked:
            r = _dot(a_ref[0], b_ref[0], contract)
            for q in range(1, stacked):
                r = r + _dot(a_ref[q], b_ref[q], contract)
        else:
            r = _dot(a_ref[...], b_ref[...], contract)
        if nk == 1:
            o_ref[...] = r.astype(o_ref.dtype)
        else:
            k = pl.program_id(len(grid) - 1)

            @pl.when(k == 0)
            def _():
                acc[0][...] = r

            @pl.when(k > 0)
            def _():
                acc[0][...] += r

            @pl.when(k == nk - 1)
            def _():
                o_ref[...] = acc[0][...].astype(o_ref.dtype)

    sem = ("parallel",) * (len(grid) - 1) + (("arbitrary",) if nk > 1 else ("parallel",))
    return pl.pallas_call(
        body, name=name, out_shape=out_sds, grid=grid, in_specs=[a_spec, b_spec], out_specs=o_spec,
        scratch_shapes=[pltpu.VMEM(o_blk, F32)] if nk > 1 else [], compiler_params=_cp(*sem))(a, b)


def _tile(n, want):
    t = min(n, want)
    assert n % t == 0, (n, t)
    return t


def _mm_nn(name, a, b, out_dtype=F32, tm=512, tn=512):
    (M, K), N = a.shape, b.shape[1]
    tm, tn = _tile(M, tm), _tile(N, tn)
    return _mm(name, a, pl.BlockSpec((tm, K), lambda i, j: (i, 0)), b, pl.BlockSpec((K, tn), lambda i, j: (0, j)),
               jax.ShapeDtypeStruct((M, N), out_dtype), pl.BlockSpec((tm, tn), lambda i, j: (i, j)),
               (M // tm, N // tn), "nn")


def _mm_nt(name, a, b, out_dtype=F32, tm=512, tn=512):
    (M, K), N = a.shape, b.shape[0]
    tm, tn = _tile(M, tm), _tile(N, tn)
    return _mm(name, a, pl.BlockSpec((tm, K), lambda i, j: (i, 0)), b, pl.BlockSpec((tn, K), lambda i, j: (j, 0)),
               jax.ShapeDtypeStruct((M, N), out_dtype), pl.BlockSpec((tm, tn), lambda i, j: (i, j)),
               (M // tm, N // tn), "nt")


def _mm_tn(name, a, b, out_dtype=F32, tm=512, tn=512):
    (K, M), N = a.shape, b.shape[1]
    tm, tn = _tile(M, tm), _tile(N, tn)
    return _mm(name, a, pl.BlockSpec((K, tm), lambda i, j: (0, i)), b, pl.BlockSpec((K, tn), lambda i, j: (0, j)),
               jax.ShapeDtypeStruct((M, N), out_dtype), pl.BlockSpec((tm, tn), lambda i, j: (i, j)),
               (M // tm, N // tn), "tn")


def _mm_cols(name, a, w, out_dtype=F32, tm=512):
    (M, K), (J, _, n) = a.shape, w.shape
    tm = _tile(M, tm)
    return _mm(name, a, pl.BlockSpec((tm, K), lambda j, i: (i, 0)), w, pl.BlockSpec((None, K, n), lambda j, i: (j, 0, 0)),
               jax.ShapeDtypeStruct((J, M, n), out_dtype), pl.BlockSpec((None, tm, n), lambda j, i: (j, i, 0)),
               (J, M // tm), "nn")


def _mm_cols_dx(name, d, w, out_dtype=F32, tm=512, jb=None):
    (J, M, n), K = d.shape, w.shape[1]
    tm, jb = _tile(M, tm), J if jb is None else jb
    return _mm(name, d, pl.BlockSpec((jb, tm, n), lambda i, j: (j, i, 0)), w, pl.BlockSpec((jb, K, n), lambda i, j: (j, 0, 0)),
               jax.ShapeDtypeStruct((M, K), out_dtype), pl.BlockSpec((tm, K), lambda i, j: (i, 0)),
               (M // tm, J // jb), "nt", nk=J // jb, stacked=jb)


def _mm_cols_dw(name, a, d, out_dtype=F32, tk=512):
    (M, K), (J, _, n) = a.shape, d.shape
    tk = _tile(K, tk)
    return _mm(name, a, pl.BlockSpec((M, tk), lambda j, i: (0, i)), d, pl.BlockSpec((None, M, n), lambda j, i: (j, 0, 0)),
               jax.ShapeDtypeStruct((J, K, n), out_dtype), pl.BlockSpec((None, tk, n), lambda j, i: (j, i, 0)),
               (J, K // tk), "tn")


def _mm_cols_dwt(name, a, d, out_dtype=F32, tk=512):
    (M, K), (J, _, n) = a.shape, d.shape
    tk = _tile(K, tk)
    return _mm(name, d, pl.BlockSpec((None, M, n), lambda j, i: (j, 0, 0)), a, pl.BlockSpec((M, tk), lambda j, i: (0, i)),
               jax.ShapeDtypeStruct((J, n, K), out_dtype), pl.BlockSpec((None, n, tk), lambda j, i: (j, 0, i)),
               (J, K // tk), "tn")


def _mm_rows_resid(name, a, w, resid, gate, tm=512):
    (Q, M, k), N = a.shape, w.shape[2]
    tm = _tile(M, tm)

    def body(a_ref, w_ref, r_ref, g_ref, y_ref, x_ref):
        y = _dot(a_ref[0], w_ref[0], "nn")
        for q in range(1, Q):
            y = y + _dot(a_ref[q], w_ref[q], "nn")
        y_ref[...] = y.astype(y_ref.dtype)
        x_ref[...] = r_ref[...] + g_ref[...] * y

    return pl.pallas_call(
        body, name=name, grid=(M // tm,),
        out_shape=(jax.ShapeDtypeStruct((M, N), ACT_DTYPE), jax.ShapeDtypeStruct((M, N), F32)),
        in_specs=[pl.BlockSpec((Q, tm, k), lambda i: (0, i, 0)), pl.BlockSpec((Q, k, N), lambda i: (0, 0, 0)),
                  pl.BlockSpec((tm, N), lambda i: (i, 0)), pl.BlockSpec((1, N), lambda i: (0, 0))],
        out_specs=(pl.BlockSpec((tm, N), lambda i: (i, 0)), pl.BlockSpec((tm, N), lambda i: (i, 0))),
        compiler_params=_cp("parallel"))(a, w, resid, gate)


def _mm_rows_dx(name, d, w, out_dtype=F32, tm=512):
    (M, N), (Q, k, _) = d.shape, w.shape
    tm = _tile(M, tm)
    return _mm(name, d, pl.BlockSpec((tm, N), lambda q, i: (i, 0)), w, pl.BlockSpec((None, k, N), lambda q, i: (q, 0, 0)),
               jax.ShapeDtypeStruct((Q, M, k), out_dtype), pl.BlockSpec((None, tm, k), lambda q, i: (q, i, 0)),
               (Q, M // tm), "nt")


def _mm_rows_dw(name, a, d, out_dtype=F32, tn=512):
    (Q, M, k), N = a.shape, d.shape[1]
    tn = _tile(N, tn)
    return _mm(name, a, pl.BlockSpec((None, M, k), lambda q, j: (q, 0, 0)), d, pl.BlockSpec((M, tn), lambda q, j: (0, j)),
               jax.ShapeDtypeStruct((Q, k, N), out_dtype), pl.BlockSpec((None, k, tn), lambda q, j: (q, 0, j)),
               (Q, N // tn), "tn")


def _silu(v):
    return v * jax.nn.sigmoid(v)


def _ada_fwd(c16, ada_w):
    L, D, n = ada_w.shape

    def body(c_ref, w_ref, o_ref):
        o_ref[...] = _dot(_silu(c_ref[...]), w_ref[...], "nn")

    return pl.pallas_call(
        body, name="ada_fwd", grid=(L,), out_shape=jax.ShapeDtypeStruct((L, 16, n), F32),
        in_specs=[pl.BlockSpec((16, D), lambda l: (0, 0)), pl.BlockSpec((None, D, n), lambda l: (l, 0, 0))],
        out_specs=pl.BlockSpec((None, 16, n), lambda l: (l, 0, 0)), compiler_params=_cp("parallel"))(c16, ada_w)


def _ada_bwd(c16, dmod16):
    L, _, n = dmod16.shape
    D = c16.shape[1]

    def body(c_ref, d_ref, o_ref):
        o_ref[...] = _dot(_silu(c_ref[...]), d_ref[...], "tn")

    return pl.pallas_call(
        body, name="ada_bwd", grid=(L,), out_shape=jax.ShapeDtypeStruct((L, D, n), F32),
        in_specs=[pl.BlockSpec((16, D), lambda l: (0, 0)), pl.BlockSpec((None, 16, n), lambda l: (l, 0, 0))],
        out_specs=pl.BlockSpec((None, D, n), lambda l: (l, 0, 0)), compiler_params=_cp("parallel"))(c16, dmod16)


def _row_spec(tr, n):
    return pl.BlockSpec((tr, n), lambda i: (i, 0))


def _vec_spec(n):
    return pl.BlockSpec((1, n), lambda i: (0, 0))


def _rmsmod_fwd(name, x, g, sc, sh, after, tr=256):
    S, D = x.shape

    def body(x_ref, g_ref, sc_ref, sh_ref, after_ref, h_ref):
        xv = x_ref[...]
        rstd = lax.rsqrt(jnp.mean(xv * xv, axis=-1, keepdims=True) + EPS)
        y = xv * rstd * g_ref[...]
        h_ref[...] = (y * (1.0 + sc_ref[...]) + sh_ref[...]).astype(h_ref.dtype)

    return pl.pallas_call(
        body, name=name, grid=(S // tr,), out_shape=jax.ShapeDtypeStruct((S, D), _MXU_DTYPE),
        in_specs=[_row_spec(tr, D), _vec_spec(D), _vec_spec(D), _vec_spec(D), ANY], out_specs=_row_spec(tr, D),
        compiler_params=_cp("parallel"))(x, g, sc, sh, after)


def _acc_rows(ref, val, first):
    s = jnp.sum(val, axis=0, keepdims=True)

    @pl.when(first)
    def _():
        ref[...] = s

    @pl.when(jnp.logical_not(first))
    def _():
        ref[...] += s


def _gate_bwd_tail(dx, y_ref, gate_ref, dy_ref, dgate_ref, first):
    dy_ref[...] = (gate_ref[...] * dx).astype(dy_ref.dtype)
    _acc_rows(dgate_ref, dx * y_ref[...].astype(F32), first)


def _rmsmod_bwd(name, x, g, sc, dh, dres, after, y=None, gate=None, tr=256):
    S, D = x.shape
    tail = y is not None

    def body(x_ref, g_ref, sc_ref, dh_ref, dres_ref, after_ref, *rest):
        (y_ref, gate_ref), rest = (rest[:2], rest[2:]) if tail else ((None, None), rest)
        dx_ref, dg_ref, dsc_ref, dsh_ref = rest[:4]
        first = pl.program_id(0) == 0
        xv, dh_v, gv = x_ref[...], dh_ref[...], g_ref[...]
        rstd = lax.rsqrt(jnp.mean(xv * xv, axis=-1, keepdims=True) + EPS)
        xhat = xv * rstd
        _acc_rows(dsh_ref, dh_v, first)
        _acc_rows(dsc_ref, dh_v * (xhat * gv), first)
        dyg = dh_v * (1.0 + sc_ref[...])
        _acc_rows(dg_ref, dyg * xhat, first)
        dxhat = dyg * gv
        dx = dres_ref[...] + rstd * (dxhat - xhat * jnp.mean(dxhat * xhat, axis=-1, keepdims=True))
        dx_ref[...] = dx
        if tail:
            _gate_bwd_tail(dx, y_ref, gate_ref, rest[4], rest[5], first)

    vec = jax.ShapeDtypeStruct((1, D), F32)
    return pl.pallas_call(
        body, name=name, grid=(S // tr,),
        out_shape=(jax.ShapeDtypeStruct((S, D), F32), vec, vec, vec) + ((jax.ShapeDtypeStruct((S, D), _MXU_DTYPE), vec) if tail else ()),
        in_specs=[_row_spec(tr, D), _vec_spec(D), _vec_spec(D), _row_spec(tr, D), _row_spec(tr, D), ANY]
        + ([_row_spec(tr, D), _vec_spec(D)] if tail else []),
        out_specs=(_row_spec(tr, D), _vec_spec(D), _vec_spec(D), _vec_spec(D)) + ((_row_spec(tr, D), _vec_spec(D)) if tail else ()),
        compiler_params=_cp("arbitrary"))(x, g, sc, dh, dres, after, *((y, gate) if tail else ()))


def _loss_head(x, g, target, y, gate, tr=256):
    S, D = x.shape

    def body(x_ref, g_ref, t_ref, y_ref, gate_ref, loss_ref, dx_ref, dg_ref, dy_ref, dgate_ref):
        first = pl.program_id(0) == 0
        xv, gv = x_ref[...], g_ref[...]
        rstd = lax.rsqrt(jnp.mean(xv * xv, axis=-1, keepdims=True) + EPS)
        xhat = xv * rstd
        err = xhat * gv - t_ref[...]
        part = 0.5 * jnp.sum(jnp.mean(err * err, axis=-1, keepdims=True), axis=0, keepdims=True)

        @pl.when(first)
        def _():
            loss_ref[...] = part

        @pl.when(jnp.logical_not(first))
        def _():
            loss_ref[...] += part

        dout = err * (1.0 / D)
        _acc_rows(dg_ref, dout * xhat, first)
        dxhat = dout * gv
        dx = rstd * (dxhat - xhat * jnp.mean(dxhat * xhat, axis=-1, keepdims=True))
        dx_ref[...] = dx
        _gate_bwd_tail(dx, y_ref, gate_ref, dy_ref, dgate_ref, first)

    vec = jax.ShapeDtypeStruct((1, D), F32)
    return pl.pallas_call(
        body, name="loss_head", grid=(S // tr,),
        out_shape=(jax.ShapeDtypeStruct((1, 1), F32), jax.ShapeDtypeStruct((S, D), F32), vec,
                   jax.ShapeDtypeStruct((S, D), _MXU_DTYPE), vec),
        in_specs=[_row_spec(tr, D), _vec_spec(D), _row_spec(tr, D), _row_spec(tr, D), _vec_spec(D)],
        out_specs=(pl.BlockSpec((1, 1), lambda i: (0, 0)), _row_spec(tr, D), _vec_spec(D), _row_spec(tr, D), _vec_spec(D)),
        compiler_params=_cp("arbitrary"))(x, g, target, y, gate)


def _gate_bwd(name, dx, y, gate, tr=256):
    S, D = dx.shape

    def body(dx_ref, y_ref, g_ref, dy_ref, dg_ref):
        dxv = dx_ref[...]
        dy_ref[...] = (g_ref[...] * dxv).astype(dy_ref.dtype)
        _acc_rows(dg_ref, dxv * y_ref[...], pl.program_id(0) == 0)

    return pl.pallas_call(
        body, name=name, grid=(S // tr,),
        out_shape=(jax.ShapeDtypeStruct((S, D), _MXU_DTYPE), jax.ShapeDtypeStruct((1, D), F32)),
        in_specs=[_row_spec(tr, D), _row_spec(tr, D), _vec_spec(D)], out_specs=(_row_spec(tr, D), _vec_spec(D)),
        compiler_params=_cp("arbitrary"))(dx, y, gate)


def _shift_down(v, k):
    t = lax.broadcasted_iota(jnp.int32, v.shape, 0)
    return jnp.where(t >= k, pltpu.roll(v, k, axis=0), 0.0)


def _shift_up(v, k):
    n = v.shape[0]
    t = lax.broadcasted_iota(jnp.int32, v.shape, 0)
    return jnp.where(t < n - k, pltpu.roll(v, n - k, axis=0), 0.0)


def _window_sum(p, w, shift):
    s, k = p, 1
    while k < w:
        s = s + shift(s, k)
        k *= 2
    return s


def _pool_count(shape, w):
    t = lax.broadcasted_iota(jnp.int32, shape, 0)
    return jnp.minimum(t + 1, w).astype(F32)


def _ab_specs(S):
    zs = [pl.BlockSpec((None, S, 128), functools.partial(lambda g, q: (2 * q + g // 2, 0, g % 2), q=q)) for q in range(4)]
    return zs


def _ab_mix_fwd(z8, conv_w, mix_w, scale):
    S = z8.shape[1]

    def body(b_ref, c_ref, a_ref, p_ref, w_ref, mix_ref, sc_ref, y_ref):
        g = pl.program_id(0)
        cg = c_ref[...] * a_ref[...]
        w = w_ref[...]
        conv = w[0:1] * _shift_down(cg, 2) + w[1:2] * _shift_down(cg, 1) + w[2:3] * cg
        y_ref[0] = (b_ref[...] * conv).astype(y_ref.dtype)
        for gg, win in enumerate(POOL_WINDOWS):
            @pl.when(g == gg)
            def _(win=win):
                p = p_ref[...]
                pooled = _window_sum(p, win, _shift_down) / _pool_count(p.shape, win) - p
                y_ref[1] = (_dot(pooled, mix_ref[...], "nn") * sc_ref[...]).astype(y_ref.dtype)

    return pl.pallas_call(
        body, name="ab_mix_fwd", grid=(4,), out_shape=jax.ShapeDtypeStruct((2, S, 512), _MXU_DTYPE),
        in_specs=_ab_specs(S) + [pl.BlockSpec((3, 128), lambda g: (0, g)), pl.BlockSpec((None, 128, 128), lambda g: (g, 0, 0)),
                                 pl.BlockSpec((1, 128), lambda g: (0, g))],
        out_specs=pl.BlockSpec((2, S, 128), lambda g: (0, 0, g)), compiler_params=_cp("parallel"))(z8, z8, z8, z8, conv_w, mix_w, scale)


def _ab_mix_bwd(z8, dycat2, conv_w, mix_w, scale, after):
    S = z8.shape[1]

    def body(b_ref, c_ref, a_ref, p_ref, dy_ref, w_ref, mix_ref, sc_ref, after_ref, dz_ref, dw_ref, dmix_ref, dsc_ref):
        g = pl.program_id(0)
        bv, cv, av, w = b_ref[...], c_ref[...], a_ref[...], w_ref[...]
        dya = dy_ref[0]
        cg = cv * av
        cg1, cg2 = _shift_down(cg, 1), _shift_down(cg, 2)
        conv = w[0:1] * cg2 + w[1:2] * cg1 + w[2:3] * cg
        dz_ref[0] = (dya * conv).astype(dz_ref.dtype)
        dconv = dya * bv
        dcg = w[2:3] * dconv + w[1:2] * _shift_up(dconv, 1) + w[0:1] * _shift_up(dconv, 2)
        dz_ref[1] = (dcg * av).astype(dz_ref.dtype)
        dz_ref[2] = (dcg * cv).astype(dz_ref.dtype)
        dw_ref[0:1, :] = jnp.sum(dconv * cg2, axis=0, keepdims=True)
        dw_ref[1:2, :] = jnp.sum(dconv * cg1, axis=0, keepdims=True)
        dw_ref[2:3, :] = jnp.sum(dconv * cg, axis=0, keepdims=True)
        for gg, win in enumerate(POOL_WINDOWS):
            @pl.when(g == gg)
            def _(win=win):
                p, dyb, mix = p_ref[...], dy_ref[1], mix_ref[...]
                cnt = _pool_count(p.shape, win)
                pooled = _window_sum(p, win, _shift_down) / cnt - p
                dsc_ref[...] = jnp.sum(dyb * _dot(pooled, mix, "nn"), axis=0, keepdims=True)
                dmixed = dyb * sc_ref[...]
                dmix_ref[...] = _dot(pooled, dmixed, "tn")
                dpooled = _dot(dmixed, mix, "nt")
                dz_ref[3] = (_window_sum(dpooled / cnt, win, _shift_up) - dpooled).astype(dz_ref.dtype)

    return pl.pallas_call(
        body, name="ab_mix_bwd", grid=(4,),
        out_shape=(jax.ShapeDtypeStruct((4, 2, S, 256), _MXU_DTYPE), jax.ShapeDtypeStruct((3, 512), F32),
                   jax.ShapeDtypeStruct((4, 128, 128), F32), jax.ShapeDtypeStruct((1, 512), F32)),
        in_specs=_ab_specs(S) + [pl.BlockSpec((2, S, 128), lambda g: (0, 0, g)), pl.BlockSpec((3, 128), lambda g: (0, g)),
                                 pl.BlockSpec((None, 128, 128), lambda g: (g, 0, 0)), pl.BlockSpec((1, 128), lambda g: (0, g)), ANY],
        out_specs=(pl.BlockSpec((4, None, S, 128), lambda g: (0, g // 2, 0, g % 2)), pl.BlockSpec((3, 128), lambda g: (0, g)),
                   pl.BlockSpec((None, 128, 128), lambda g: (g, 0, 0)), pl.BlockSpec((1, 128), lambda g: (0, g))),
        compiler_params=_cp("parallel"))(z8, z8, z8, z8, dycat2, conv_w, mix_w, scale, after)


HALO = 16


def _ffn_specs(S, n, tr):
    nb = S // HALO
    tile = pl.BlockSpec((2, None, tr, n), lambda j, i: (0, j, i, 0))
    prev = pl.BlockSpec((2, None, HALO, n), lambda j, i: (0, j, jnp.maximum(i * (tr // HALO) - 1, 0), 0))
    nxt = pl.BlockSpec((2, None, HALO, n), lambda j, i: (0, j, jnp.minimum((i + 1) * (tr // HALO), nb - 1), 0))
    cw = pl.BlockSpec((2, None, 3, n), lambda j, i: (0, j, 0, 0))
    return tile, prev, nxt, cw


def _shifted_rows(ext, lo, rows):
    ext = ext.astype(F32)
    return pltpu.roll(ext, 1, axis=0)[lo:lo + rows], pltpu.roll(ext, 2, axis=0)[lo:lo + rows]


def _ffn_gate_fwd(name, u24, cw24, tr=256):
    _, J, S, n = u24.shape
    tile, prev, _, cw = _ffn_specs(S, n, tr)

    def body(u_ref, up_ref, w_ref, a_ref):
        keep = (pl.program_id(1) > 0).astype(u_ref.dtype)
        z = []
        for h in range(2):
            ext = jnp.concatenate([up_ref[h] * keep, u_ref[h]], axis=0)
            x1, x2 = _shifted_rows(ext, HALO, tr)
            w = w_ref[h]
            z.append(w[0:1] * x2 + w[1:2] * x1 + w[2:3] * u_ref[h].astype(F32))
        a_ref[...] = (_silu(z[0]) * z[1]).astype(a_ref.dtype)

    return pl.pallas_call(
        body, name=name, grid=(J, S // tr), out_shape=jax.ShapeDtypeStruct((J, S, n), _MXU_DTYPE),
        in_specs=[tile, prev, cw], out_specs=pl.BlockSpec((None, tr, n), lambda j, i: (j, i, 0)),
        compiler_params=_cp("parallel", "parallel"))(u24, u24, cw24)


def _ffn_gate_bwd(name, u24, cw24, da4, w_up24, after, tr=256):
    _, J, S, n = u24.shape
    K = w_up24.shape[2]
    nb = S // HALO
    tile = pl.BlockSpec((2, None, tr, n), lambda i, j: (0, j, i, 0))
    prev = pl.BlockSpec((2, None, HALO, n), lambda i, j: (0, j, jnp.maximum(i * (tr // HALO) - 1, 0), 0))
    nxt = pl.BlockSpec((2, None, HALO, n), lambda i, j: (0, j, jnp.minimum((i + 1) * (tr // HALO), nb - 1), 0))
    whole = lambda shape: pl.BlockSpec(shape, lambda i, j: (0,) * len(shape))

    def body(u_ref, up_ref, un_ref, cw_ref, da_ref, dan_ref, wup_ref, after_ref, du_ref, dcw_ref, dh_ref, acc_ref):
        i, j = pl.program_id(0), pl.program_id(1)
        first = i == 0
        keep_prev = (i > 0).astype(u_ref.dtype)
        keep_next = (i < S // tr - 1).astype(F32)
        w = [cw_ref[h, j] for h in range(2)]
        m = tr + HALO
        xs, z = [], []
        for h in range(2):
            ext = jnp.concatenate([up_ref[h] * keep_prev, u_ref[h], un_ref[h]], axis=0)
            x1, x2 = _shifted_rows(ext, HALO, m)
            x0 = ext[HALO:HALO + m].astype(F32)
            xs.append((x2, x1, x0))
            z.append(w[h][0:1] * x2 + w[h][1:2] * x1 + w[h][2:3] * x0)
        zg, zu = z
        da = jnp.concatenate([da_ref[...].astype(F32), dan_ref[...].astype(F32) * keep_next], axis=0)
        sg = jax.nn.sigmoid(zg)
        dz = [da * zu * (sg * (1.0 + zg * (1.0 - sg))), da * (zg * sg)]
        dh = None
        for h in range(2):
            d = dz[h]
            du = w[h][2:3] * d[:tr] + w[h][1:2] * pltpu.roll(d, m - 1, axis=0)[:tr] + w[h][0:1] * pltpu.roll(d, m - 2, axis=0)[:tr]
            du = du.astype(du_ref.dtype)
            du_ref[h] = du
            part = _dot(du, wup_ref[h, j], "nt")
            dh = part if dh is None else dh + part
            dt = d[:tr]
            parts = [jnp.sum(dt * xk[:tr], axis=0, keepdims=True) for xk in xs[h]]
            for k in range(3):
                @pl.when(first)
                def _(k=k, h=h):
                    dcw_ref[h, j, k:k + 1, :] = parts[k]

                @pl.when(jnp.logical_not(first))
                def _(k=k, h=h):
                    dcw_ref[h, j, k:k + 1, :] += parts[k]

        @pl.when(j == 0)
        def _():
            acc_ref[...] = dh

        @pl.when(j > 0)
        def _():
            acc_ref[...] += dh

        @pl.when(j == J - 1)
        def _():
            dh_ref[...] = acc_ref[...]

    da_tile = pl.BlockSpec((None, tr, n), lambda i, j: (j, i, 0))
    da_next = pl.BlockSpec((None, HALO, n), lambda i, j: (j, jnp.minimum((i + 1) * (tr // HALO), nb - 1), 0))
    return pl.pallas_call(
        body, name=name, grid=(S // tr, J),
        out_shape=(jax.ShapeDtypeStruct((2, J, S, n), _MXU_DTYPE), jax.ShapeDtypeStruct((2, J, 3, n), F32),
                   jax.ShapeDtypeStruct((S, K), F32)),
        in_specs=[tile, prev, nxt, whole((2, J, 3, n)), da_tile, da_next, whole((2, J, K, n)), ANY],
        out_specs=(tile, whole((2, J, 3, n)), pl.BlockSpec((tr, K), lambda i, j: (i, 0))),
        scratch_shapes=[pltpu.VMEM((tr, K), F32)],
        compiler_params=_cp("arbitrary", "arbitrary"))(u24, u24, u24, cw24, da4, da4, w_up24, after)


def _rms_rows(v, g):
    rstd = lax.rsqrt(jnp.mean(v * v, axis=-1, keepdims=True) + EPS)
    return v * rstd * g


def _rms_rows_bwd(v, g, dy):
    rstd = lax.rsqrt(jnp.mean(v * v, axis=-1, keepdims=True) + EPS)
    vhat = v * rstd
    dvhat = dy * g
    return rstd * (dvhat - vhat * jnp.mean(dvhat * vhat, axis=-1, keepdims=True)), dy * vhat


def _mla_prep_fwd(z, qg, kvg, tr=256):
    S = z.shape[0]

    def body(q_ref, kv_ref, qg_ref, kvg_ref, qn_ref, kvn_ref):
        qn_ref[...] = _rms_rows(q_ref[...], qg_ref[...]).astype(qn_ref.dtype)
        kvn_ref[...] = _rms_rows(kv_ref[...], kvg_ref[...]).astype(kvn_ref.dtype)

    return pl.pallas_call(
        body, name="mla_prep_fwd", grid=(S // tr,),
        out_shape=(jax.ShapeDtypeStruct((S, 256), _MXU_DTYPE), jax.ShapeDtypeStruct((S, 128), _MXU_DTYPE)),
        in_specs=[pl.BlockSpec((tr, 256), lambda i: (i, 0)), pl.BlockSpec((tr, 128), lambda i: (i, 2)), _vec_spec(256), _vec_spec(128)],
        out_specs=(_row_spec(tr, 256), _row_spec(tr, 128)), compiler_params=_cp("parallel"))(z, z, qg, kvg)


def _mla_prep_bwd(z, qg, kvg, dqn, dkvn, dkpe, duv, tr=256):
    S = z.shape[0]

    def body(q_ref, kv_ref, qg_ref, kvg_ref, dqn_ref, dkvn_ref, dkpe_ref, duv_ref, dz_ref, dqg_ref, dkvg_ref):
        first = pl.program_id(0) == 0
        dq, dqg = _rms_rows_bwd(q_ref[...], qg_ref[...], dqn_ref[...])
        dkv, dkvg = _rms_rows_bwd(kv_ref[...], kvg_ref[...], dkvn_ref[...])
        _acc_rows(dqg_ref, dqg, first)
        _acc_rows(dkvg_ref, dkvg, first)
        dz_ref[:, 0:256] = dq.astype(dz_ref.dtype)
        dz_ref[:, 256:384] = dkv.astype(dz_ref.dtype)
        dz_ref[:, 384:512] = dkpe_ref[...].astype(dz_ref.dtype)
        dz_ref[:, 512:1536] = duv_ref[...].astype(dz_ref.dtype)

    return pl.pallas_call(
        body, name="mla_prep_bwd", grid=(S // tr,),
        out_shape=(jax.ShapeDtypeStruct((S, 1536), _MXU_DTYPE), jax.ShapeDtypeStruct((1, 256), F32), jax.ShapeDtypeStruct((1, 128), F32)),
        in_specs=[pl.BlockSpec((tr, 256), lambda i: (i, 0)), pl.BlockSpec((tr, 128), lambda i: (i, 2)), _vec_spec(256), _vec_spec(128),
                  _row_spec(tr, 256), _row_spec(tr, 128), _row_spec(tr, 128), _row_spec(tr, 1024)],
        out_specs=(_row_spec(tr, 1536), _vec_spec(256), _vec_spec(128)),
        compiler_params=_cp("arbitrary"))(z, z, qg, kvg, dqn, dkvn, dkpe, duv)


def _rope(v, cos, sa, sb):
    return v * cos + pltpu.roll(v, 112, axis=1) * sa + pltpu.roll(v, 16, axis=1) * sb


def _rope_t(d, cos, sa, sb):
    return d * cos + pltpu.roll(d * sa, 16, axis=1) + pltpu.roll(d * sb, 112, axis=1)


def _rope_fwd(qraw, kvall, z, cosq, cosk, sa, sb, tr=256):
    S = qraw.shape[0]

    def body(q_ref, k_ref, v_ref, kpe_ref, cq_ref, ck_ref, sa_ref, sb_ref, qo_ref, ko_ref, vo_ref):
        cq, ck, sa_v, sb_v = cq_ref[...], ck_ref[...], sa_ref[...], sb_ref[...]
        kpe = _rope(kpe_ref[...], ck, sa_v, sb_v)
        for h in range(8):
            cols = slice(128 * h, 128 * h + 128)
            qo_ref[:, cols] = _rope(q_ref[:, cols], cq, sa_v, sb_v).astype(qo_ref.dtype)
            ko_ref[:, cols] = (k_ref[:, cols] + kpe).astype(ko_ref.dtype)
        vo_ref[...] = v_ref[...].astype(vo_ref.dtype)

    tab = _row_spec(tr, 128)
    return pl.pallas_call(
        body, name="rope_fwd", grid=(S // tr,),
        out_shape=(jax.ShapeDtypeStruct((S, 1024), _MXU_DTYPE), jax.ShapeDtypeStruct((S, 1024), _MXU_DTYPE),
                   jax.ShapeDtypeStruct((S, 512), _MXU_DTYPE)),
        in_specs=[_row_spec(tr, 1024), pl.BlockSpec((tr, 1024), lambda i: (i, 0)), pl.BlockSpec((tr, 512), lambda i: (i, 2)),
                  pl.BlockSpec((tr, 128), lambda i: (i, 3)), tab, tab, tab, tab],
        out_specs=(_row_spec(tr, 1024), _row_spec(tr, 1024), _row_spec(tr, 512)),
        compiler_params=_cp("parallel"))(qraw, kvall, kvall, z, cosq, cosk, sa, sb)


def _rope_bwd(dq, dk, dv, cosq, cosk, sa, sb, tr=256):
    S = dq.shape[0]

    def body(dq_ref, dk_ref, dv_ref, cq_ref, ck_ref, sa_ref, sb_ref, dqo_ref, dkv_ref, dkpe_ref):
        cq, ck, sa_v, sb_v = cq_ref[...], ck_ref[...], sa_ref[...], sb_ref[...]
        tot = jnp.zeros((tr, 128), F32)
        for h in range(8):
            cols = slice(128 * h, 128 * h + 128)
            dqo_ref[:, cols] = _rope_t(dq_ref[:, cols], cq, sa_v, sb_v).astype(dqo_ref.dtype)
            dkh = dk_ref[:, cols]
            tot = tot + dkh
            dkv_ref[:, cols] = dkh.astype(dkv_ref.dtype)
        dkv_ref[:, 1024:1536] = dv_ref[...].astype(dkv_ref.dtype)
        dkpe_ref[...] = _rope_t(tot, ck, sa_v, sb_v)

    tab = _row_spec(tr, 128)
    return pl.pallas_call(
        body, name="rope_bwd", grid=(S // tr,),
        out_shape=(jax.ShapeDtypeStruct((S, 1024), _MXU_DTYPE), jax.ShapeDtypeStruct((S, 1536), _MXU_DTYPE),
                   jax.ShapeDtypeStruct((S, 128), F32)),
        in_specs=[_row_spec(tr, 1024), _row_spec(tr, 1024), _row_spec(tr, 512), tab, tab, tab, tab],
        out_specs=(_row_spec(tr, 1024), _row_spec(tr, 1536), _row_spec(tr, 128)),
        compiler_params=_cp("parallel"))(dq, dk, dv, cosq, cosk, sa, sb)


NEG = -1e30


def _attn_fwd(q, k, v, tq=256, tk=256):
    S = q.shape[0]
    assert tq == tk

    def body(q_ref, k_ref, v_ref, o_ref, lse_ref):
        i = pl.program_id(1)
        qs = [q_ref[:, 0:128], q_ref[:, 128:256]]

        def step(kb, carry, diagonal=False):
            start = pl.multiple_of(kb * tk, tk)
            vv = v_ref[pl.ds(start, tk), :]
            out = []
            for h in range(2):
                m, l, acc = carry[3 * h:3 * h + 3]
                s = _dot(qs[h], k_ref[pl.ds(start, tk), 128 * h:128 * h + 128], "nt") * ATTN_SCALE
                if diagonal:
                    s = jnp.where(below, s, NEG)
                m_new = jnp.maximum(m, jnp.max(s, axis=-1, keepdims=True))
                alpha = jnp.exp(m - m_new)
                p = jnp.exp(s - m_new)
                out += [m_new, alpha * l + jnp.sum(p, axis=-1, keepdims=True), alpha * acc + _dot(p, vv, "nn")]
            return tuple(out)

        below = lax.broadcasted_iota(jnp.int32, (tq, tk), 1) <= lax.broadcasted_iota(jnp.int32, (tq, tk), 0)
        init = (jnp.full((tq, 1), NEG, F32), jnp.zeros((tq, 1), F32), jnp.zeros((tq, 128), F32)) * 2
        ma, la, acca, mb, lb, accb = step(i, lax.fori_loop(0, i, step, init), diagonal=True)
        lane = lax.broadcasted_iota(jnp.int32, (tq, 128), 1)
        o_ref[...] = jnp.where(lane < 64, acca / la, accb / lb)
        lse_ref[...] = jnp.where(lane < 64, ma + jnp.log(la), mb + jnp.log(lb))

    return pl.pallas_call(
        body, name="attn_fwd", grid=(4, S // tq),
        out_shape=(jax.ShapeDtypeStruct((S, 512), F32), jax.ShapeDtypeStruct((4, S, 128), F32)),
        in_specs=[pl.BlockSpec((tq, 256), lambda p, i: (i, p)), pl.BlockSpec((S, 256), lambda p, i: (0, p)),
                  pl.BlockSpec((S, 128), lambda p, i: (0, p))],
        out_specs=(pl.BlockSpec((tq, 128), lambda p, i: (i, p)), pl.BlockSpec((None, tq, 128), lambda p, i: (p, i, 0))),
        compiler_params=_cp("parallel", "parallel"))(q, k, v)


def _attn_bwd(q, k, v, o, lse, dycat2, tq=256, tk=256):
    S = q.shape[0]
    assert tq == tk

    def body(q_ref, k_ref, v_ref, o_ref, lse_ref, do_ref, dq_ref, dk_ref, dv_ref):
        j = pl.program_id(1)

        @pl.when(j == 0)
        def _():
            dq_ref[...] = jnp.zeros_like(dq_ref)

        below = lax.broadcasted_iota(jnp.int32, (tq, tk), 1) <= lax.broadcasted_iota(jnp.int32, (tq, tk), 0)
        lane = lax.broadcasted_iota(jnp.int32, (tq, 128), 1)
        ks = [k_ref[:, 0:128], k_ref[:, 128:256]]
        vv = v_ref[...]

        def step(qb, carry, diagonal=False):
            dka, dkb, dvp = carry
            start = pl.multiple_of(qb * tq, tq)
            rows = pl.ds(start, tq)
            do, lse_v = do_ref[rows, :], lse_ref[rows, :]
            prod = do * o_ref[rows, :]
            dks = [dka, dkb]
            for h in range(2):
                mine = (lane < 64) if h == 0 else (lane >= 64)
                delta = jnp.sum(jnp.where(mine, prod, 0.0), axis=-1, keepdims=True)
                do_h = jnp.where(mine, do, 0.0)
                qh = q_ref[rows, 128 * h:128 * h + 128]
                s = _dot(qh, ks[h], "nt") * ATTN_SCALE
                p = jnp.exp(s - lse_v[:, 64 * h:64 * h + 1])
                if diagonal:
                    p = jnp.where(below, p, 0.0)
                dvp = dvp + _dot(p, do_h, "tn")
                ds = p * (_dot(do_h, vv, "nt") - delta) * ATTN_SCALE
                dq_ref[rows, 128 * h:128 * h + 128] += _dot(ds, ks[h], "nn")
                dks[h] = dks[h] + _dot(ds, qh, "tn")
            return dks[0], dks[1], dvp

        zero = jnp.zeros((tk, 128), F32)
        dka, dkb, dvp = lax.fori_loop(j + 1, S // tq, step, step(j, (zero, zero, zero), diagonal=True))
        dk_ref[:, 0:128] = dka
        dk_ref[:, 128:256] = dkb
        dv_ref[...] = dvp

    return pl.pallas_call(
        body, name="attn_bwd", grid=(4, S // tk),
        out_shape=(jax.ShapeDtypeStruct((S, 1024), F32), jax.ShapeDtypeStruct((S, 1024), F32), jax.ShapeDtypeStruct((S, 512), F32)),
        in_specs=[pl.BlockSpec((S, 256), lambda p, j: (0, p)), pl.BlockSpec((tk, 256), lambda p, j: (j, p)),
                  pl.BlockSpec((tk, 128), lambda p, j: (j, p)), pl.BlockSpec((S, 128), lambda p, j: (0, p)),
                  pl.BlockSpec((None, S, 128), lambda p, j: (p, 0, 0)), pl.BlockSpec((None, S, 128), lambda p, j: (0, 0, p))],
        out_specs=(pl.BlockSpec((S, 256), lambda p, j: (0, p)), pl.BlockSpec((tk, 256), lambda p, j: (j, p)),
                   pl.BlockSpec((tk, 128), lambda p, j: (j, p))),
        compiler_params=_cp("parallel", "arbitrary"))(q, k, v, o, lse, dycat2)


CHUNK = 128
GELU_C = math.sqrt(2.0 / math.pi)


def _gelu(v):
    t = jnp.tanh(GELU_C * (v + 0.044715 * (v * v * v)))
    return v * (0.5 * (1.0 + t)), t


def _gelu_grad(v, t):
    return 0.5 * (1.0 + t) + v * (0.5 * (1.0 - t * t) * GELU_C * (1.0 + 3.0 * 0.044715 * v * v))


def _tril(w):
    r = lax.broadcasted_iota(jnp.int32, w.shape, 0)
    c = lax.broadcasted_iota(jnp.int32, w.shape, 1)
    return jnp.where(c <= r, w, 0.0)


def _layer_norm(v, g, b):
    xc = v - jnp.mean(v, axis=-1, keepdims=True)
    rstd = lax.rsqrt(jnp.mean(xc * xc, axis=-1, keepdims=True) + EPS)
    xhat = xc * rstd
    return xhat * g + b, xhat, rstd


def _sgu_fwd(z, o, ln_g, ln_b, w_s, b_st, tr=256):
    S = z.shape[0]

    def body(u_ref, v_ref, o_ref, g_ref, b_ref, ws_ref, bs_ref, y_ref):
        gu, _ = _gelu(u_ref[...])
        gv, _ = _gelu(v_ref[...])
        vln, _, _ = _layer_norm(gv, g_ref[...], b_ref[...])
        y_ref[0] = o_ref[...].astype(y_ref.dtype)
        for g in range(4):
            wt = _tril(ws_ref[g])
            cols = slice(128 * g, 128 * g + 128)
            for ch in range(tr // CHUNK):
                rows = slice(CHUNK * ch, CHUNK * ch + CHUNK)
                mixed = _dot(wt, vln[rows, cols], "nn") + bs_ref[:, g:g + 1]
                y_ref[1, rows, cols] = (gu[rows, cols] * mixed).astype(y_ref.dtype)

    return pl.pallas_call(
        body, name="sgu_fwd", grid=(S // tr,), out_shape=jax.ShapeDtypeStruct((2, S, 512), _MXU_DTYPE),
        in_specs=[pl.BlockSpec((tr, 512), lambda i: (i, 1)), pl.BlockSpec((tr, 512), lambda i: (i, 2)), _row_spec(tr, 512),
                  _vec_spec(512), _vec_spec(512), pl.BlockSpec((4, 128, 128), lambda i: (0, 0, 0)), pl.BlockSpec((128, 4), lambda i: (0, 0))],
        out_specs=pl.BlockSpec((2, tr, 512), lambda i: (0, i, 0)), compiler_params=_cp("parallel"))(z, z, o, ln_g, ln_b, w_s, b_st)


def _sgu_bwd(z, dycat2, ln_g, ln_b, w_s, b_st, tr=256):
    S = z.shape[0]

    def body(u_ref, v_ref, dy_ref, g_ref, b_ref, ws_ref, bs_ref, duv_ref, dg_ref, db_ref, dws_ref, dbs_ref):
        first = pl.program_id(0) == 0
        u_pre, v_pre = u_ref[...], v_ref[...]
        gu, tu = _gelu(u_pre)
        gv, tv = _gelu(v_pre)
        gain = g_ref[...]
        vln, xhat, rstd = _layer_norm(gv, gain, b_ref[...])

        @pl.when(first)
        def _():
            dws_ref[...] = jnp.zeros_like(dws_ref)
            dbs_ref[...] = jnp.zeros_like(dbs_ref)

        dvln_cols = []
        for g in range(4):
            wt = _tril(ws_ref[g])
            cols = slice(128 * g, 128 * g + 128)
            dmixed_sum = jnp.zeros((CHUNK, 128), F32)
            dw = jnp.zeros((CHUNK, CHUNK), F32)
            dvln_rows = []
            for ch in range(tr // CHUNK):
                rows = slice(CHUNK * ch, CHUNK * ch + CHUNK)
                vt = vln[rows, cols]
                mixed = _dot(wt, vt, "nn") + bs_ref[:, g:g + 1]
                dyd = dy_ref[rows, cols]
                duv_ref[rows, cols] = (dyd * mixed * _gelu_grad(u_pre[rows, cols], tu[rows, cols])).astype(duv_ref.dtype)
                dmixed = dyd * gu[rows, cols]
                dmixed_sum = dmixed_sum + dmixed
                dw = dw + _dot(dmixed, vt, "nt")
                dvln_rows.append(_dot(wt, dmixed, "tn"))
            dws_ref[g] += _tril(dw)
            dbs_ref[g:g + 1, :] += jnp.sum(dmixed_sum.T, axis=0, keepdims=True)
            dvln_cols.append(jnp.concatenate(dvln_rows, axis=0))
        dvln = jnp.concatenate(dvln_cols, axis=1)
        _acc_rows(dg_ref, dvln * xhat, first)
        _acc_rows(db_ref, dvln, first)
        dxhat = dvln * gain
        dgv = rstd * (dxhat - jnp.mean(dxhat, axis=-1, keepdims=True) - xhat * jnp.mean(dxhat * xhat, axis=-1, keepdims=True))
        duv_ref[:, 512:1024] = (dgv * _gelu_grad(v_pre, tv)).astype(duv_ref.dtype)

    return pl.pallas_call(
        body, name="sgu_bwd", grid=(S // tr,),
        out_shape=(jax.ShapeDtypeStruct((S, 1024), _MXU_DTYPE), jax.ShapeDtypeStruct((1, 512), F32), jax.ShapeDtypeStruct((1, 512), F32),
                   jax.ShapeDtypeStruct((4, 128, 128), F32), jax.ShapeDtypeStruct((4, 128), F32)),
        in_specs=[pl.BlockSpec((tr, 512), lambda i: (i, 1)), pl.BlockSpec((tr, 512), lambda i: (i, 2)),
                  pl.BlockSpec((None, tr, 512), lambda i: (1, i, 0)), _vec_spec(512), _vec_spec(512),
                  pl.BlockSpec((4, 128, 128), lambda i: (0, 0, 0)), pl.BlockSpec((128, 4), lambda i: (0, 0))],
        out_specs=(_row_spec(tr, 1024), _vec_spec(512), _vec_spec(512), pl.BlockSpec((4, 128, 128), lambda i: (0, 0, 0)),
                   pl.BlockSpec((4, 128), lambda i: (0, 0))),
        compiler_params=_cp("arbitrary"))(z, z, dycat2, ln_g, ln_b, w_s, b_st)


def _sum_parts(name, parts, tr=512):
    P, R, C = parts.shape
    tr = _tile(R, tr) if R % 8 == 0 else R

    def body(p_ref, o_ref):
        g = p_ref[0]
        for k in range(1, P):
            g = g + p_ref[k]
        o_ref[...] = g

    return pl.pallas_call(
        body, name=name, grid=(R // tr,), out_shape=jax.ShapeDtypeStruct((R, C), F32),
        in_specs=[pl.BlockSpec((P, tr, C), lambda i: (0, i, 0))], out_specs=_row_spec(tr, C),
        compiler_params=_cp("parallel"))(parts)


def _adamw_math(w, m, v, g):
    c1 = 1.0 / (1.0 - ADAM_B1 ** ADAM_STEP)
    c2 = 1.0 / (1.0 - ADAM_B2 ** ADAM_STEP)
    m2 = ADAM_B1 * m + (1.0 - ADAM_B1) * g
    v2 = ADAM_B2 * v + (1.0 - ADAM_B2) * (g * g)
    return -ADAM_LR * ((m2 * c1) / (jnp.sqrt(v2 * c2) + ADAM_EPS) + ADAM_WD * w), m2, v2


def _adamw_small(name, params, parts):
    n = len(params)

    def body(*refs):
        ins, outs = refs[:4 * n], refs[4 * n:]
        for i in range(n):
            w_ref, m_ref, v_ref, p_ref = ins[4 * i:4 * i + 4]
            g = p_ref[0]
            for k in range(1, N_DEV):
                g = g + p_ref[k]
            delta, m2, v2 = _adamw_math(w_ref[...], m_ref[...], v_ref[...], g)
            outs[4 * i][...] = g
            outs[4 * i + 1][...] = delta
            outs[4 * i + 2][...] = m2
            outs[4 * i + 3][...] = v2

    flat = [a for (w, m, v), p in zip(params, parts) for a in (w, m, v, p)]
    out = pl.pallas_call(
        body, name=name, out_shape=[jax.ShapeDtypeStruct(w.shape, F32) for (w, _, _) in params for _ in range(4)],
        compiler_params=pltpu.CompilerParams(vmem_limit_bytes=_VMEM_LIMIT))(*flat)
    return [out[4 * i:4 * i + 4] for i in range(n)]


ADAMW_BLOCK_BYTES = 36 * 2 ** 20


def _adamw(name, w, m, v, parts):
    L, R, C = w.shape
    P = parts[0].shape[0]
    row_bytes = 2 * C * (7 * 4 + P * parts[0].dtype.itemsize)
    tr = R
    if R * row_bytes > ADAMW_BLOCK_BYTES:
        tr = next(t for t in (1024, 512, 256, 128, 64, 32, 16) if R % t == 0 and t * row_bytes <= ADAMW_BLOCK_BYTES)
    nr = R // tr
    c1 = 1.0 / (1.0 - ADAM_B1 ** ADAM_STEP)
    c2 = 1.0 / (1.0 - ADAM_B2 ** ADAM_STEP)

    def body(w_ref, m_ref, v_ref, *rest):
        p_refs, (g_ref, d_ref, mo_ref, vo_ref) = rest[:L], rest[L:]
        for ll in range(L):
            @pl.when(pl.program_id(0) == ll)
            def _(p_ref=p_refs[ll]):
                g = p_ref[0].astype(F32)
                for k in range(1, P):
                    g = g + p_ref[k].astype(F32)
                m2 = ADAM_B1 * m_ref[...] + (1.0 - ADAM_B1) * g
                v2 = ADAM_B2 * v_ref[...] + (1.0 - ADAM_B2) * (g * g)
                g_ref[...] = g
                mo_ref[...] = m2
                vo_ref[...] = v2
                d_ref[...] = -ADAM_LR * ((m2 * c1) / (jnp.sqrt(v2 * c2) + ADAM_EPS) + ADAM_WD * w_ref[...])

    def part_spec(ll):
        return pl.BlockSpec((P, tr, C), lambda l, i: (0, jnp.where(l == ll, i, jnp.where(l < ll, 0, nr - 1)), 0))

    full = pl.BlockSpec((None, tr, C), lambda l, i: (l, i, 0))
    sds = jax.ShapeDtypeStruct((L, R, C), F32)
    return pl.pallas_call(
        body, name=name, grid=(L, nr), out_shape=(sds, sds, sds, sds),
        in_specs=[full] * 3 + [part_spec(ll) for ll in range(L)],
        out_specs=(full,) * 4, compiler_params=_cp("arbitrary", "arbitrary"))(w, m, v, *parts)


def _rope_tables(positions):
    half = 16
    inv_freq = 10000.0 ** (-jnp.arange(half, dtype=F32) / half)
    ang = positions.astype(F32)[:, None] * inv_freq
    cos, sin = jnp.cos(ang), jnp.sin(ang)
    S = positions.shape[0]
    z16, z32, z64 = jnp.zeros((S, 16), F32), jnp.zeros((S, 32), F32), jnp.zeros((S, 64), F32)
    cosk = jnp.concatenate([z64, cos, cos, z32], axis=1)
    cosq = jnp.concatenate([jnp.ones((S, 64), F32), cos, cos, z32], axis=1)
    sa = jnp.concatenate([z64, -sin, z16, z32], axis=1)
    sb = jnp.concatenate([z64, z16, sin, z32], axis=1)
    return cosq, cosk, sa, sb


def _ffn_fwd(l, x, mod, n2g, get_w_up8, cw24, get_w_down4):
    sh, sc, gate = mod
    h = _rmsmod_fwd(f"ffn{l}_norm", x, n2g, sc, sh, n2g)
    w_up8 = get_w_up8(h)
    u8 = _mm_cols(f"ffn{l}_up", h, w_up8, out_dtype=ACT_DTYPE, tm=2048)
    S, n = u8.shape[1], u8.shape[2]
    u24 = u8.reshape(2, 4, S, n)
    a4 = _ffn_gate_fwd(f"ffn{l}_gate", u24, cw24)
    w_down4 = get_w_down4(a4)
    f, x_new = _mm_rows_resid(f"ffn{l}_down", a4, w_down4, x, gate)
    return x_new, (x, h, u24, a4, f), w_up8, w_down4


def _ffn_bwd(l, dx, df, dgate, saved, mod, n2g, w_up8, cw24, w_down4, me, y_prev, gate_prev):
    sh, sc, gate = mod
    x, h, u24, a4, f = saved
    da4 = _mm_rows_dx(f"ffn{l}_down_dx", df, w_down4, out_dtype=ACT_DTYPE, tm=2048)
    dw_down4 = _mm_rows_dw(f"ffn{l}_down_dw", a4, df, out_dtype=WIRE_DTYPE)
    sent_down, token = _exchange_start(f"scatter_ffn{l}_down", [dw_down4.reshape(8, 352, dw_down4.shape[2])], True, dgate, me)
    du24, dcw24, dh = _ffn_gate_bwd(f"ffn{l}_act_bwd", u24, cw24, da4, w_up8.reshape((2, 4) + w_up8.shape[1:]), token)
    du8 = du24.reshape((8,) + du24.shape[2:])
    dw_up8t = _mm_cols_dwt(f"ffn{l}_up_dw", h, du8, out_dtype=WIRE_DTYPE, tk=1024)
    sent_up, token = _exchange_start(f"scatter_ffn{l}_up", [dw_up8t], True, dcw24, me)
    dx_new, dn2g, dsc, dsh, dy_prev, dgate_prev = _rmsmod_bwd(f"ffn{l}_norm_bwd", x, n2g, sc, dh, dx, token, y_prev, gate_prev)
    return dx_new, dict(sent_up=sent_up, sent_down=sent_down, cw24=dcw24, n2g=dn2g, mod=(dsh, dsc, dgate)), dy_prev, dgate_prev


def kernel(x, c, positions, ada_w, ada_b, norm1_g, norm2_g, ab_w_in, a_conv_w, b_mix_w, b_scale, ab_w_out, cd_w_in, c_q_norm_g, c_w_uq, c_kv_norm_g, c_w_ukv, d_ln_g, d_ln_b, d_w_s, d_b_s, cd_w_out, ffn_w_up, ffn_conv_w, ffn_w_down, final_norm_g, loss_target, m_ada_w, m_ada_b, m_norm1_g, m_norm2_g, m_ab_w_in, m_a_conv_w, m_b_mix_w, m_b_scale, m_ab_w_out, m_cd_w_in, m_c_q_norm_g, m_c_w_uq, m_c_kv_norm_g, m_c_w_ukv, m_d_ln_g, m_d_ln_b, m_d_w_s, m_d_b_s, m_cd_w_out, m_ffn_w_up, m_ffn_conv_w, m_ffn_w_down, m_final_norm_g, v_ada_w, v_ada_b, v_norm1_g, v_norm2_g, v_ab_w_in, v_a_conv_w, v_b_mix_w, v_b_scale, v_ab_w_out, v_cd_w_in, v_c_q_norm_g, v_c_w_uq, v_c_kv_norm_g, v_c_w_ukv, v_d_ln_g, v_d_ln_b, v_d_w_s, v_d_b_s, v_cd_w_out, v_ffn_w_up, v_ffn_conv_w, v_ffn_w_down, v_final_norm_g):
    S, D = x.shape[1], x.shape[2]
    me = 4 * lax.axis_index("x") + 2 * lax.axis_index("y") + lax.axis_index("c")
    x0, target = x[0], loss_target[0]
    W = _MXU_DTYPE

    small_shapes = [(1024,), (3, 64), (32,), (64,), (64,), (2, 3, 704)]
    (g0,) = _exchange("gather_small", [[_pack([c, a_conv_w, c_q_norm_g, d_ln_g, d_ln_b, ffn_conv_w])]], scatter=False)
    c_all, aconv_s, qg_s, lng_s, lnb_s, fcw_s = _unpack(g0[:, 0], small_shapes, lead=(N_DEV,))
    conv_w = aconv_s.transpose(1, 0, 2).reshape(3, 512)
    qg, ln_g, ln_b = qg_s.reshape(1, 256), lng_s.reshape(1, 512), lnb_s.reshape(1, 512)
    cw24 = [fcw_s[:, l].reshape(2, 4, 3, 704) for l in range(2)]
    c16 = jnp.pad(c_all, ((0, 16 - N_DEV), (0, 0)))

    mod_cols = _ada_fwd(c16, ada_w)
    (g1,) = _exchange("gather_mod", [[_pack([mod_cols])]], scatter=False)
    mod_all = _unpack(g1[:, 0], [(2, 16, 768)], lead=(N_DEV,))[0]
    mod_mine = lax.dynamic_index_in_dim(mod_all, me, axis=2, keepdims=False)
    mod = mod_mine.transpose(1, 0, 2).reshape(2, 6 * D) + ada_b
    mods = [[mod[l, k * D:(k + 1) * D].reshape(1, D) for k in range(6)] for l in range(2)]

    gw_ab, token = _hier_gather_start("gather_w_ab", [ab_w_in[0].astype(W), ab_w_out[0].astype(W)], mod, me)
    gw_up0, token = _hier_gather_start("gather_w_ffn0_up", [ffn_w_up[0].astype(W)], token, me)
    gw_rest, started = _exchange_start("gather_w_rest", [
        ffn_w_down[0].astype(W), cd_w_in[0].T.astype(W), c_w_uq[0].T.astype(W), c_w_ukv[0].astype(W), cd_w_out[0].astype(W),
        ffn_w_up[1].astype(W), ffn_w_down[1].astype(W)], False, token, me)

    cosq, cosk, sa, sb = _rope_tables(positions[0])
    n1g = [norm1_g[l].reshape(1, D) for l in range(2)]
    n2g = [norm2_g[l].reshape(1, D) for l in range(2)]
    mix_w, scale = b_mix_w[0], b_scale
    kvg = c_kv_norm_g
    w_s, b_st = d_w_s[0], d_b_s[0].T

    sh1, sc1, g1m = mods[0][:3]
    h_ab = _rmsmod_fwd("ab_norm", x0, n1g[0], sc1, sh1, started)
    w_abin8, w_about = _hier_gather_wait("wait_w_ab", _hier_gather_forward("forward_w_ab", gw_ab, h_ab), h_ab)
    w_about2 = w_about.reshape(2, 512, D)
    z8 = _mm_cols("ab_in", h_ab, w_abin8, tm=2048)
    ycat_ab = _ab_mix_fwd(z8, conv_w, mix_w, scale)
    y_ab, x1 = _mm_rows_resid("ab_out", ycat_ab, w_about2, x0, g1m)
    w_up8, w_down4 = [None, None], [None, None]
    gw_up0 = _hier_gather_forward("forward_w_ffn0_up", gw_up0, x1)
    x2, ffn0_saved, w_up8[0], w_down4[0] = _ffn_fwd(
        0, x1, mods[0][3:], n2g[0], lambda after: _hier_gather_wait("wait_w_ffn0_up", gw_up0, after)[0], cw24[0],
        lambda after: _exchange_wait("wait_w_ffn0_down", gw_rest, after, [0])[0].reshape(4, 704, D))

    w_cdin, w_uq, w_ukv, w_cdout = _exchange_wait("wait_w_cd", gw_rest, x2, [1, 2, 3, 4])
    w_cdout2 = w_cdout.reshape(2, 512, D)
    w_cd_t = w_cdin.reshape(1440, D)
    zr = lambda n: jnp.zeros((n, D), W)
    w_cd_pad = jnp.concatenate([w_cd_t[:384], zr(64), w_cd_t[384:416], zr(32), w_cd_t[416:]], axis=0)
    w_uq_pad = jnp.pad(w_uq, ((0, 0), (0, 32), (0, 0))).reshape(1024, 256)
    w_ukv_h = w_ukv.transpose(1, 0, 2)
    w_k_pad = jnp.pad(w_ukv_h[:, :, :64], ((0, 0), (0, 0), (0, 64))).reshape(128, 1024)
    w_kv_pad = jnp.concatenate([w_k_pad, w_ukv_h[:, :, 64:].reshape(128, 512)], axis=1)

    sh1, sc1, g1c = mods[1][:3]
    h_cd = _rmsmod_fwd("cd_norm", x2, n1g[1], sc1, sh1, n1g[1])
    z_cd = _mm_nt("cd_in", h_cd, w_cd_pad)
    qn, kvn = _mla_prep_fwd(z_cd, qg, kvg)
    qraw = _mm_nt("cd_uq", qn, w_uq_pad)
    kvall = _mm_nn("cd_ukv", kvn, w_kv_pad)
    q_r, k_r, v_r = _rope_fwd(qraw, kvall, z_cd, cosq, cosk, sa, sb)
    o, lse = _attn_fwd(q_r, k_r, v_r)
    ycat_cd = _sgu_fwd(z_cd, o, ln_g, ln_b, w_s, b_st)
    y_cd, x3 = _mm_rows_resid("cd_out", ycat_cd, w_cdout2, x2, g1c)
    x4, ffn1_saved, w_up8[1], w_down4[1] = _ffn_fwd(
        1, x3, mods[1][3:], n2g[1], lambda after: _exchange_wait("wait_w_ffn1_up", gw_rest, after, [5])[0], cw24[1],
        lambda after: _exchange_wait("wait_w_ffn1_down", gw_rest, after, [6])[0].reshape(4, 704, D))

    loss_local, dx4, dfg, df1, dgate1 = _loss_head(x4, final_norm_g.reshape(1, D), target, ffn1_saved[4], mods[1][5])

    dx3, gf1, dy, dg1c = _ffn_bwd(1, dx4, df1, dgate1, ffn1_saved, mods[1][3:], n2g[1], w_up8[1], cw24[1], w_down4[1], me, y_cd, g1c)

    dycat = _mm_rows_dx("cd_out_dx", dy, w_cdout2)
    dw_cdout = _mm_rows_dw("cd_out_dw", ycat_cd, dy, out_dtype=WIRE_DTYPE)
    duv, dln_g, dln_b, dws, dbs = _sgu_bwd(z_cd, dycat, ln_g, ln_b, w_s, b_st)
    dq_r, dk_r, dv_r = _attn_bwd(q_r, k_r, v_r, o, lse, dycat)
    dqraw, dkvall, dkpe = _rope_bwd(dq_r, dk_r, dv_r, cosq, cosk, sa, sb)
    dqn = _mm_nn("cd_uq_dx", dqraw, w_uq_pad, tn=256)
    dkvn = _mm_nt("cd_ukv_dx", dkvall, w_kv_pad, tn=128)
    dw_uq_pad = _mm_tn("cd_uq_dw", dqraw, qn, tn=256)
    dw_kv_pad = _mm_tn("cd_ukv_dw", kvn, dkvall, tm=128)
    dz_cd, dqg, dkvg = _mla_prep_bwd(z_cd, qg, kvg, dqn, dkvn, dkpe, duv)
    dh_cd = _mm_nn("cd_in_dx", dz_cd, w_cd_pad)
    dw_cd_pad = _mm_tn("cd_in_dw", dz_cd, h_cd)
    dw_cd8 = jnp.concatenate([dw_cd_pad[:384], dw_cd_pad[448:480], dw_cd_pad[512:]], axis=0).astype(WIRE_DTYPE).reshape(8, 180, D)
    dw_uq8 = dw_uq_pad.reshape(8, 128, 256)[:, :96].astype(WIRE_DTYPE)
    dw_ukv8 = jnp.concatenate([dw_kv_pad[:, :1024].reshape(128, 8, 128)[:, :, :64], dw_kv_pad[:, 1024:].reshape(128, 8, 64)],
                              axis=2).transpose(1, 0, 2).astype(WIRE_DTYPE)
    sent_cd, token = _exchange_start("scatter_cd", [dw_cd8, dw_uq8, dw_ukv8, dw_cdout.reshape(8, 128, D)], True, dqg, me)
    early_names = ["c_kv_norm_g", "d_w_s", "d_b_s", "final_norm_g", "c_q_norm_g", "d_ln_g", "d_ln_b"]
    early_grads = [dkvg, dws.reshape(512, 128), dbs, dfg, dqg.reshape(8, 1, 32), dln_g.reshape(8, 1, 64), dln_b.reshape(8, 1, 64)]
    early_sent, token = _exchange_start("gather_small_grads_early", early_grads, [False] * 4 + [True] * 3, token, me)
    dx2, dn1g_cd, dsc1_cd, dsh1_cd, df0, dgate0 = _rmsmod_bwd("cd_norm_bwd", x2, n1g[1], sc1, dh_cd, dx3, token,
                                                              ffn0_saved[4], mods[0][5])

    dx1, gf0, dy, dg1m = _ffn_bwd(0, dx2, df0, dgate0, ffn0_saved, mods[0][3:], n2g[0], w_up8[0], cw24[0], w_down4[0], me, y_ab, g1m)

    dw_about = _mm_rows_dw("ab_out_dw", ycat_ab, dy, out_dtype=WIRE_DTYPE)
    sent_about, token = _exchange_start("scatter_ab_out", [dw_about.reshape(8, 128, D)], True, dg1m, me)
    dycat = _mm_rows_dx("ab_out_dx", dy, w_about2)
    dz8, dconv_w, dmix_w, dscale = _ab_mix_bwd(z8, dycat, conv_w, mix_w, scale, token)
    dz8 = dz8.reshape(8, S, 256)
    dw_abin8 = _mm_cols_dw("ab_in_dw", h_ab, dz8, out_dtype=WIRE_DTYPE, tk=1024)
    sent_abin, token = _exchange_start("scatter_ab_in", [dw_abin8], True, dscale, me)
    dh_ab = _mm_cols_dx("ab_in_dx", dz8, w_abin8)
    dx0, dn1g_ab, dsc1_ab, dsh1_ab = _rmsmod_bwd("ab_norm_bwd", x0, n1g[0], mods[0][1], dh_ab, dx1, token)

    dmod = jnp.stack([jnp.concatenate([dsh1_ab, dsc1_ab, dg1m, *gf0["mod"]], axis=1)[0],
                      jnp.concatenate([dsh1_cd, dsc1_cd, dg1c, *gf1["mod"]], axis=1)[0]])
    late_names = ["ada_b", "norm1_g", "norm2_g", "b_mix_w", "b_scale", "a_conv_w", "ffn_conv_w"]
    late_grads = [dmod, jnp.concatenate([dn1g_ab, dn1g_cd]), jnp.concatenate([gf0["n2g"], gf1["n2g"]]),
                  dmix_w.reshape(512, 128), dscale, dconv_w.reshape(3, 8, 64).transpose(1, 0, 2),
                  jnp.stack([gf0["cw24"].reshape(8, 3, 704), gf1["cw24"].reshape(8, 3, 704)], axis=1)]
    small_view = dict(ada_b=(2, 6 * D), norm1_g=(2, D), norm2_g=(2, D), b_mix_w=(512, 128), b_scale=(1, 512), c_kv_norm_g=(1, 128),
                      d_w_s=(512, 128), d_b_s=(4, 128), final_norm_g=(1, D),
                      a_conv_w=(3, 64), c_q_norm_g=(1, 32), d_ln_g=(1, 64), d_ln_b=(1, 64), ffn_conv_w=(2, 3, 704))
    late_sent, token = _exchange_start("gather_small_grads_late", late_grads, [False] * 5 + [True] * 2, dx0, me)

    res = {}

    def update(name, w, m, v, parts, shape3d):
        outs = _adamw("adamw_" + name, w.reshape(shape3d), m.reshape(shape3d), v.reshape(shape3d),
                      [p.reshape((p.shape[0],) + shape3d[1:]) for p in parts])
        res[name] = [o_.reshape(w.shape) for o_ in outs]

    p_cdin, p_uq, p_ukv, p_cdout = _exchange_wait("wait_scatter_cd", sent_cd, token)
    swap = lambda a: jnp.swapaxes(a, 1, 2)
    update("cd_w_in", swap(cd_w_in), swap(m_cd_w_in), swap(v_cd_w_in), [p_cdin], (1, 180, D))
    update("c_w_uq", swap(c_w_uq), swap(m_c_w_uq), swap(v_c_w_uq), [p_uq], (1, 96, 256))
    for name in ("cd_w_in", "c_w_uq"):
        res[name] = [swap(o_) for o_ in res[name]]
    update("c_w_ukv", c_w_ukv, m_c_w_ukv, v_c_w_ukv, [p_ukv], (1, 128, 128))
    update("cd_w_out", cd_w_out, m_cd_w_out, v_cd_w_out, [p_cdout], (1, 128, D))
    (p_dn1,) = _exchange_wait("wait_scatter_ffn1_down", gf1["sent_down"], token)
    (p_dn0,) = _exchange_wait("wait_scatter_ffn0_down", gf0["sent_down"], res["cd_w_out"][0])
    update("ffn_w_down", ffn_w_down, m_ffn_w_down, v_ffn_w_down, [p_dn0, p_dn1], (2, 352, D))
    (p_up1,) = _exchange_wait("wait_scatter_ffn1_up", gf1["sent_up"], token)
    (p_up0,) = _exchange_wait("wait_scatter_ffn0_up", gf0["sent_up"], res["ffn_w_down"][0])
    swap = lambda a: jnp.swapaxes(a, 1, 2)
    update("ffn_w_up", swap(ffn_w_up), swap(m_ffn_w_up), swap(v_ffn_w_up), [p_up0, p_up1], (2, 704, D))
    up_done = res["ffn_w_up"][0]
    res["ffn_w_up"] = [swap(o_) for o_ in res["ffn_w_up"]]
    (p_about,) = _exchange_wait("wait_scatter_ab_out", sent_about, up_done)
    update("ab_w_out", ab_w_out, m_ab_w_out, v_ab_w_out, [p_about], (1, 128, D))
    (p_abin,) = _exchange_wait("wait_scatter_ab_in", sent_abin, res["ab_w_out"][0])
    update("ab_w_in", ab_w_in, m_ab_w_in, v_ab_w_in, [p_abin], (1, D, 256))

    early_parts = _exchange_wait("wait_small_grads_early", early_sent, res["ab_w_in"][0])
    late_parts = _exchange_wait("wait_small_grads_late", late_sent, res["ab_w_in"][0])
    small_names = early_names + late_names
    small_parts = list(early_parts) + list(late_parts)
    dmod_all = late_parts[0]
    dmod_cols = lax.dynamic_slice_in_dim(dmod_all, me * 768, 768, axis=2).transpose(1, 0, 2)
    g_ada_w = _ada_bwd(c16, jnp.pad(dmod_cols, ((0, 0), (0, 16 - N_DEV), (0, 0))))
    update("ada_w", ada_w, m_ada_w, v_ada_w, [g_ada_w[l][None] for l in range(2)], (2, D, 768))

    small_w = dict(ada_b=(ada_b, m_ada_b, v_ada_b), norm1_g=(norm1_g, m_norm1_g, v_norm1_g), norm2_g=(norm2_g, m_norm2_g, v_norm2_g),
                   b_mix_w=(b_mix_w, m_b_mix_w, v_b_mix_w), b_scale=(b_scale, m_b_scale, v_b_scale),
                   c_kv_norm_g=(c_kv_norm_g, m_c_kv_norm_g, v_c_kv_norm_g), d_w_s=(d_w_s, m_d_w_s, v_d_w_s),
                   d_b_s=(d_b_s, m_d_b_s, v_d_b_s), final_norm_g=(final_norm_g, m_final_norm_g, v_final_norm_g),
                   a_conv_w=(a_conv_w, m_a_conv_w, v_a_conv_w), c_q_norm_g=(c_q_norm_g, m_c_q_norm_g, v_c_q_norm_g),
                   d_ln_g=(d_ln_g, m_d_ln_g, v_d_ln_g), d_ln_b=(d_ln_b, m_d_ln_b, v_d_ln_b),
                   ffn_conv_w=(ffn_conv_w, m_ffn_conv_w, v_ffn_conv_w))
    small_out = _adamw_small("adamw_small", [tuple(a.reshape(small_view[n]) for a in small_w[n]) for n in small_names],
                             list(small_parts))
    for n, outs in zip(small_names, small_out):
        res[n] = [o_.reshape(small_w[n][0].shape) for o_ in outs]

    loss = lax.psum(loss_local[0, 0], ("x", "y", "c"))
    order = ["ada_w", "ada_b", "norm1_g", "norm2_g", "ab_w_in", "a_conv_w", "b_mix_w", "b_scale", "ab_w_out", "cd_w_in", "c_q_norm_g",
             "c_w_uq", "c_kv_norm_g", "c_w_ukv", "d_ln_g", "d_ln_b", "d_w_s", "d_b_s", "cd_w_out", "ffn_w_up", "ffn_conv_w",
             "ffn_w_down", "final_norm_g"]
    return (loss, dx0[None], *[res[n][0] for n in order], *[res[n][1] for n in order], *[res[n][2] for n in order],
            *[res[n][3] for n in order])
```

```python
import functools
import math

import jax
import jax.numpy as jnp
from jax import lax
from jax.experimental import pallas as pl
from jax.experimental.pallas import tpu as pltpu

F32 = jnp.float32
BF16 = jnp.bfloat16
_MXU_DTYPE = BF16
WIRE_DTYPE = BF16
ACT_DTYPE = BF16
_VMEM_LIMIT = 56 * 2 ** 20
N_DEV = 8
EPS = 1e-6
POOL_WINDOWS = (2, 4, 8, 16)
ATTN_SCALE = (64 + 32) ** -0.5
ADAM_LR, ADAM_B1, ADAM_B2, ADAM_EPS, ADAM_WD, ADAM_STEP = 0.001, 0.9, 0.999, 1e-08, 0.01, 10
MESH = pl.DeviceIdType.MESH
ANY = pl.BlockSpec(memory_space=pl.ANY)


def _cp(*sem):
    return pltpu.CompilerParams(dimension_semantics=sem, vmem_limit_bytes=_VMEM_LIMIT)


def _dot(a, b, contract):
    dn = {"nn": (((1,), (0,)), ((), ())), "nt": (((1,), (1,)), ((), ())), "tn": (((0,), (0,)), ((), ()))}[contract]
    return lax.dot_general(a.astype(_MXU_DTYPE), b.astype(_MXU_DTYPE), dn, preferred_element_type=F32)


def _my_position():
    x, y, c = lax.axis_index("x"), lax.axis_index("y"), lax.axis_index("c")
    return x, y, c, 4 * x + 2 * y + c


def _exchange(name, groups, scatter):
    flat = [a for g in groups for a in g]
    n_in, n_grp = len(flat), len(groups)
    out_shapes = []
    for g in groups:
        slab = g[0].shape[1:] if scatter else g[0].shape
        out_shapes.append(jax.ShapeDtypeStruct((N_DEV, len(g)) + tuple(slab), g[0].dtype))

    def body(*refs):
        ins, outs = refs[:n_in], refs[n_in:n_in + n_grp]
        send_sems, recv_sems, local_sems = refs[n_in + n_grp:]
        x, y, c, me = _my_position()
        i = 0
        for gi, g in enumerate(groups):
            for l in range(len(g)):
                src = ins[i]
                i += 1
                pltpu.make_async_copy(src.at[me] if scatter else src, outs[gi].at[me, l], local_sems.at[gi]).start()
                for k in range(1, N_DEV):
                    px = 1 - x if k & 4 else x
                    py = 1 - y if k & 2 else y
                    pc = 1 - c if k & 1 else c
                    peer = 4 * px + 2 * py + pc
                    pltpu.make_async_remote_copy(
                        src_ref=src.at[peer] if scatter else src, dst_ref=outs[gi].at[me, l],
                        send_sem=send_sems.at[gi], recv_sem=recv_sems.at[gi],
                        device_id=(px, py, pc), device_id_type=MESH).start()
        for gi in range(n_grp):
            mine = outs[gi].at[me]
            pltpu.make_async_copy(mine, mine, local_sems.at[gi]).wait()
            seven = outs[gi].at[pl.ds(0, N_DEV - 1)]
            w = pltpu.make_async_remote_copy(src_ref=seven, dst_ref=seven, send_sem=send_sems.at[gi],
                                             recv_sem=recv_sems.at[gi], device_id=(x, y, c), device_id_type=MESH)
            w.wait_send()
            w.wait_recv()

    return pl.pallas_call(
        body, name=name, out_shape=tuple(out_shapes),
        in_specs=[ANY] * n_in, out_specs=tuple([ANY] * n_grp),
        scratch_shapes=[pltpu.SemaphoreType.DMA((n_grp,)), pltpu.SemaphoreType.DMA((n_grp,)),
                        pltpu.SemaphoreType.DMA((n_grp,))],
        compiler_params=pltpu.CompilerParams(has_side_effects=True),
    )(*flat)


HBM_SPEC = pl.BlockSpec(memory_space=pltpu.HBM)
SEM_SPEC = pl.BlockSpec(memory_space=pltpu.SEMAPHORE)
EFFECT = pltpu.SideEffectType.DATAFLOW_SIDE_EFFECTING


def _put_mine(name, srcs, scatter, me):
    n = len(srcs)
    slabs = [tuple(s.shape[1:] if sc else s.shape) for s, sc in zip(srcs, scatter)]

    def body(me_ref, *refs):
        for i in range(n):
            refs[n + i][...] = refs[i][...]

    def at_me(slab):
        return pl.BlockSpec((None,) + slab, lambda g, me_ref, nd=len(slab): (me_ref[0],) + (0,) * nd)

    def whole(slab):
        return pl.BlockSpec(slab, lambda g, me_ref, nd=len(slab): (0,) * nd)

    return pl.pallas_call(
        body, name=name,
        grid_spec=pltpu.PrefetchScalarGridSpec(
            num_scalar_prefetch=1, grid=(1,),
            in_specs=[at_me(slab) if sc else whole(slab) for slab, sc in zip(slabs, scatter)],
            out_specs=[at_me(slab) for slab in slabs]),
        out_shape=[jax.ShapeDtypeStruct((N_DEV,) + slab, s.dtype) for slab, s in zip(slabs, srcs)],
        compiler_params=_cp("arbitrary"))(me.reshape(1), *srcs)


def _exchange_start(name, srcs, scatter, after, me):
    n = len(srcs)
    scatter = list(scatter) if isinstance(scatter, (list, tuple)) else [scatter] * n
    lands = _put_mine(name + "_mine", srcs, scatter, me)
    srcs = [pltpu.with_memory_space_constraint(a, pltpu.HBM) for a in srcs]
    lands = [pltpu.with_memory_space_constraint(a, pltpu.HBM) for a in lands]

    def body(*refs):
        ins, land = refs[:n], refs[n:2 * n]
        send_sems, recv_sems, token = refs[2 * n + 1], refs[2 * n + 2], refs[-1]
        x, y, c, me_in = _my_position()
        for i in range(n):
            for k in range(1, N_DEV):
                px = 1 - x if k & 4 else x
                py = 1 - y if k & 2 else y
                pc = 1 - c if k & 1 else c
                pltpu.make_async_remote_copy(
                    src_ref=ins[i].at[4 * px + 2 * py + pc] if scatter[i] else ins[i], dst_ref=land[i].at[me_in],
                    send_sem=send_sems.at[i], recv_sem=recv_sems.at[i],
                    device_id=(px, py, pc), device_id_type=MESH).start()
        token[...] = jnp.zeros_like(token)

    outs = pl.pallas_call(
        body, name=name,
        out_shape=(pltpu.SemaphoreType.DMA((n,)), pltpu.SemaphoreType.DMA((n,)),
                   *[pltpu.HBM(a.shape, a.dtype) for a in srcs], *[pltpu.HBM(a.shape, a.dtype) for a in lands],
                   jax.ShapeDtypeStruct((8, 128), F32)),
        in_specs=[HBM_SPEC] * (2 * n) + [ANY],
        out_specs=(SEM_SPEC, SEM_SPEC, *[HBM_SPEC] * (2 * n), pl.BlockSpec(memory_space=pltpu.VMEM)),
        input_output_aliases={i: 2 + i for i in range(2 * n)},
        compiler_params=pltpu.CompilerParams(has_side_effects=EFFECT),
    )(*srcs, *lands, after)
    return (outs[0], outs[1], outs[2:2 + n], outs[2 + n:2 + 2 * n]), outs[-1]


def _exchange_wait(name, handle, after, which=None):
    send_sems, recv_sems, srcs, lands = handle
    which = list(range(len(srcs))) if which is None else list(which)
    srcs, lands = [srcs[i] for i in which], [lands[i] for i in which]
    n = len(srcs)

    def body(*refs):
        land, send_ref, recv_ref = refs[n:2 * n], refs[2 * n], refs[2 * n + 1]
        x, y, c, _ = _my_position()
        for k, i in enumerate(which):
            seven = land[k].at[pl.ds(0, N_DEV - 1)]
            w = pltpu.make_async_remote_copy(src_ref=seven, dst_ref=seven, send_sem=send_ref.at[i], recv_sem=recv_ref.at[i],
                                             device_id=(x, y, c), device_id_type=MESH)
            w.wait_send()
            w.wait_recv()

    outs = pl.pallas_call(
        body, name=name,
        out_shape=(*[pltpu.HBM(a.shape, a.dtype) for a in srcs], *[pltpu.HBM(a.shape, a.dtype) for a in lands]),
        in_specs=[HBM_SPEC] * (2 * n) + [SEM_SPEC, SEM_SPEC, ANY],
        out_specs=tuple([HBM_SPEC] * (2 * n)),
        input_output_aliases={i: i for i in range(2 * n)},
        compiler_params=pltpu.CompilerParams(has_side_effects=EFFECT),
    )(*srcs, *lands, send_sems, recv_sems, after)
    return outs[n:]


def _other_chips(x, y):
    return [(1 - x, y), (x, 1 - y), (1 - x, 1 - y)]


def _hier_gather_start(name, srcs, after, me):
    n = len(srcs)
    lands = _put_mine(name + "_mine", srcs, [False] * n, me)
    srcs = [pltpu.with_memory_space_constraint(a, pltpu.HBM) for a in srcs]
    lands = [pltpu.with_memory_space_constraint(a, pltpu.HBM) for a in lands]

    def body(*refs):
        ins, land = refs[:n], refs[n:2 * n]
        ici_send, ici_recv, d2d_send, d2d_recv = refs[2 * n + 1:2 * n + 5]
        token = refs[-1]
        x, y, c, me_in = _my_position()
        for i in range(n):
            pltpu.make_async_remote_copy(src_ref=ins[i], dst_ref=land[i].at[me_in], send_sem=d2d_send.at[i], recv_sem=d2d_recv.at[i],
                                         device_id=(x, y, 1 - c), device_id_type=MESH).start()
            for px, py in _other_chips(x, y):
                pltpu.make_async_remote_copy(src_ref=ins[i], dst_ref=land[i].at[me_in], send_sem=ici_send.at[i],
                                             recv_sem=ici_recv.at[i], device_id=(px, py, c), device_id_type=MESH).start()
        token[...] = jnp.zeros_like(token)

    sem = pltpu.SemaphoreType.DMA((n,))
    outs = pl.pallas_call(
        body, name=name,
        out_shape=(sem, sem, sem, sem, *[pltpu.HBM(a.shape, a.dtype) for a in srcs], *[pltpu.HBM(a.shape, a.dtype) for a in lands],
                   jax.ShapeDtypeStruct((8, 128), F32)),
        in_specs=[HBM_SPEC] * (2 * n) + [ANY],
        out_specs=(SEM_SPEC,) * 4 + (HBM_SPEC,) * (2 * n) + (pl.BlockSpec(memory_space=pltpu.VMEM),),
        input_output_aliases={i: 4 + i for i in range(2 * n)},
        compiler_params=pltpu.CompilerParams(has_side_effects=EFFECT),
    )(*srcs, *lands, after)
    return (outs[:4], outs[4:4 + n], outs[4 + n:4 + 2 * n]), outs[-1]


def _hier_gather_forward(name, handle, after):
    sems, srcs, lands = handle
    n = len(srcs)

    def body(*refs):
        land = refs[n:2 * n]
        ici_send, ici_recv, d2d_send, d2d_recv = refs[2 * n:2 * n + 4]
        x, y, c, _ = _my_position()
        for i in range(n):
            three = land[i].at[pl.ds(0, 3)]
            pltpu.make_async_remote_copy(src_ref=three, dst_ref=three, send_sem=ici_send.at[i], recv_sem=ici_recv.at[i],
                                         device_id=(x, y, c), device_id_type=MESH).wait_recv()
            for px, py in _other_chips(x, y):
                slab = land[i].at[4 * px + 2 * py + c]
                pltpu.make_async_remote_copy(src_ref=slab, dst_ref=slab, send_sem=d2d_send.at[i], recv_sem=d2d_recv.at[i],
                                             device_id=(x, y, 1 - c), device_id_type=MESH).start()

    outs = pl.pallas_call(
        body, name=name,
        out_shape=(*[pltpu.HBM(a.shape, a.dtype) for a in srcs], *[pltpu.HBM(a.shape, a.dtype) for a in lands]),
        in_specs=[HBM_SPEC] * (2 * n) + [SEM_SPEC] * 4 + [ANY],
        out_specs=tuple([HBM_SPEC] * (2 * n)),
        input_output_aliases={i: i for i in range(2 * n)},
        compiler_params=pltpu.CompilerParams(has_side_effects=EFFECT),
    )(*srcs, *lands, *sems, after)
    return (sems, outs[:n], outs[n:])


def _hier_gather_wait(name, handle, after):
    sems, srcs, lands = handle
    n = len(srcs)

    def body(*refs):
        land = refs[n:2 * n]
        ici_send, ici_recv, d2d_send, d2d_recv = refs[2 * n:2 * n + 4]
        x, y, c, _ = _my_position()
        for i in range(n):
            three, four = land[i].at[pl.ds(0, 3)], land[i].at[pl.ds(0, 4)]
            pltpu.make_async_remote_copy(src_ref=three, dst_ref=three, send_sem=ici_send.at[i], recv_sem=ici_recv.at[i],
                                         device_id=(x, y, c), device_id_type=MESH).wait_send()
            w = pltpu.make_async_remote_copy(src_ref=four, dst_ref=four, send_sem=d2d_send.at[i], recv_sem=d2d_recv.at[i],
                                             device_id=(x, y, c), device_id_type=MESH)
            w.wait_send()
            w.wait_recv()

    outs = pl.pallas_call(
        body, name=name,
        out_shape=(*[pltpu.HBM(a.shape, a.dtype) for a in srcs], *[pltpu.HBM(a.shape, a.dtype) for a in lands]),
        in_specs=[HBM_SPEC] * (2 * n) + [SEM_SPEC] * 4 + [ANY],
        out_specs=tuple([HBM_SPEC] * (2 * n)),
        input_output_aliases={i: i for i in range(2 * n)},
        compiler_params=pltpu.CompilerParams(has_side_effects=EFFECT),
    )(*srcs, *lands, *sems, after)
    return outs[n:]


def _pack(arrs):
    flat = jnp.concatenate([a.reshape(-1).astype(F32) for a in arrs])
    n = flat.shape[0]
    rows = -(-n // 1024) * 8
    return jnp.pad(flat, (0, rows * 128 - n)).reshape(rows, 128)


def _unpack(buf, shapes, lead=()):
    flat = buf.reshape(lead + (-1,))
    out, off = [], 0
    for s in shapes:
        n = math.prod(s)
        out.append(flat[..., off:off + n].reshape(lead + tuple(s)))
        off += n
    return out


def _mm(name, a, a_spec, b, b_spec, out_sds, o_spec, grid, contract, nk=1, stacked=0):
    o_blk = tuple(d for d in o_spec.block_shape if d is not None)

    def body(a_ref, b_ref, o_ref, *acc):
        if stacked:
            r = _dot(a_ref[0], b_ref[0], contract)
            for q in range(1, stacked):
                r = r + _dot(a_ref[q], b_ref[q], contract)
        else:
            r = _dot(a_ref[...], b_ref[...], contract)
        if nk == 1:
            o_ref[...] = r.astype(o_ref.dtype)
        else:
            k = pl.program_id(len(grid) - 1)

            @pl.when(k == 0)
            def _():
                acc[0][...] = r

            @pl.when(k > 0)
            def _():
                acc[0][...] += r

            @pl.when(k == nk - 1)
            def _():
                o_ref[...] = acc[0][...].astype(o_ref.dtype)

    sem = ("parallel",) * (len(grid) - 1) + (("arbitrary",) if nk > 1 else ("parallel",))
    return pl.pallas_call(
        body, name=name, out_shape=out_sds, grid=grid, in_specs=[a_spec, b_spec], out_specs=o_spec,
        scratch_shapes=[pltpu.VMEM(o_blk, F32)] if nk > 1 else [], compiler_params=_cp(*sem))(a, b)


def _tile(n, want):
    t = min(n, want)
    assert n % t == 0, (n, t)
    return t


def _mm_nn(name, a, b, out_dtype=F32, tm=512, tn=512):
    (M, K), N = a.shape, b.shape[1]
    tm, tn = _tile(M, tm), _tile(N, tn)
    return _mm(name, a, pl.BlockSpec((tm, K), lambda i, j: (i, 0)), b, pl.BlockSpec((K, tn), lambda i, j: (0, j)),
               jax.ShapeDtypeStruct((M, N), out_dtype), pl.BlockSpec((tm, tn), lambda i, j: (i, j)),
               (M // tm, N // tn), "nn")


def _mm_nt(name, a, b, out_dtype=F32, tm=512, tn=512):
    (M, K), N = a.shape, b.shape[0]
    tm, tn = _tile(M, tm), _tile(N, tn)
    return _mm(name, a, pl.BlockSpec((tm, K), lambda i, j: (i, 0)), b, pl.BlockSpec((tn, K), lambda i, j: (j, 0)),
               jax.ShapeDtypeStruct((M, N), out_dtype), pl.BlockSpec((tm, tn), lambda i, j: (i, j)),
               (M // tm, N // tn), "nt")


def _mm_tn(name, a, b, out_dtype=F32, tm=512, tn=512):
    (K, M), N = a.shape, b.shape[1]
    tm, tn = _tile(M, tm), _tile(N, tn)
    return _mm(name, a, pl.BlockSpec((K, tm), lambda i, j: (0, i)), b, pl.BlockSpec((K, tn), lambda i, j: (0, j)),
               jax.ShapeDtypeStruct((M, N), out_dtype), pl.BlockSpec((tm, tn), lambda i, j: (i, j)),
               (M // tm, N // tn), "tn")


def _mm_cols(name, a, w, out_dtype=F32, tm=512):
    (M, K), (J, _, n) = a.shape, w.shape
    tm = _tile(M, tm)
    return _mm(name, a, pl.BlockSpec((tm, K), lambda j, i: (i, 0)), w, pl.BlockSpec((None, K, n), lambda j, i: (j, 0, 0)),
               jax.ShapeDtypeStruct((J, M, n), out_dtype), pl.BlockSpec((None, tm, n), lambda j, i: (j, i, 0)),
               (J, M // tm), "nn")


def _mm_cols_dx(name, d, w, out_dtype=F32, tm=512, jb=None):
    (J, M, n), K = d.shape, w.shape[1]
    tm, jb = _tile(M, tm), J if jb is None else jb
    return _mm(name, d, pl.BlockSpec((jb, tm, n), lambda i, j: (j, i, 0)), w, pl.BlockSpec((jb, K, n), lambda i, j: (j, 0, 0)),
               jax.ShapeDtypeStruct((M, K), out_dtype), pl.BlockSpec((tm, K), lambda i, j: (i, 0)),
               (M // tm, J // jb), "nt", nk=J // jb, stacked=jb)


def _mm_cols_dw(name, a, d, out_dtype=F32, tk=512):
    (M, K), (J, _, n) = a.shape, d.shape
    tk = _tile(K, tk)
    return _mm(name, a, pl.BlockSpec((M, tk), lambda j, i: (0, i)), d, pl.BlockSpec((None, M, n), lambda j, i: (j, 0, 0)),
               jax.ShapeDtypeStruct((J, K, n), out_dtype), pl.BlockSpec((None, tk, n), lambda j, i: (j, i, 0)),
               (J, K // tk), "tn")


def _mm_cols_dwt(name, a, d, out_dtype=F32, tk=512):
    (M, K), (J, _, n) = a.shape, d.shape
    tk = _tile(K, tk)
    return _mm(name, d, pl.BlockSpec((None, M, n), lambda j, i: (j, 0, 0)), a, pl.BlockSpec((M, tk), lambda j, i: (0, i)),
               jax.ShapeDtypeStruct((J, n, K), out_dtype), pl.BlockSpec((None, n, tk), lambda j, i: (j, 0, i)),
               (J, K // tk), "tn")


def _mm_rows_resid(name, a, w, resid, gate, tm=512):
    (Q, M, k), N = a.shape, w.shape[2]
    tm = _tile(M, tm)

    def body(a_ref, w_ref, r_ref, g_ref, y_ref, x_ref):
        y = _dot(a_ref[0], w_ref[0], "nn")
        for q in range(1, Q):
            y = y + _dot(a_ref[q], w_ref[q], "nn")
        y_ref[...] = y.astype(y_ref.dtype)
        x_ref[...] = r_ref[...] + g_ref[...] * y

    return pl.pallas_call(
        body, name=name, grid=(M // tm,),
        out_shape=(jax.ShapeDtypeStruct((M, N), ACT_DTYPE), jax.ShapeDtypeStruct((M, N), F32)),
        in_specs=[pl.BlockSpec((Q, tm, k), lambda i: (0, i, 0)), pl.BlockSpec((Q, k, N), lambda i: (0, 0, 0)),
                  pl.BlockSpec((tm, N), lambda i: (i, 0)), pl.BlockSpec((1, N), lambda i: (0, 0))],
        out_specs=(pl.BlockSpec((tm, N), lambda i: (i, 0)), pl.BlockSpec((tm, N), lambda i: (i, 0))),
        compiler_params=_cp("parallel"))(a, w, resid, gate)


def _mm_rows_dx(name, d, w, out_dtype=F32, tm=512):
    (M, N), (Q, k, _) = d.shape, w.shape
    tm = _tile(M, tm)
    return _mm(name, d, pl.BlockSpec((tm, N), lambda q, i: (i, 0)), w, pl.BlockSpec((None, k, N), lambda q, i: (q, 0, 0)),
               jax.ShapeDtypeStruct((Q, M, k), out_dtype), pl.BlockSpec((None, tm, k), lambda q, i: (q, i, 0)),
               (Q, M // tm), "nt")


def _mm_rows_dw(name, a, d, out_dtype=F32, tn=512):
    (Q, M, k), N = a.shape, d.shape[1]
    tn = _tile(N, tn)
    return _mm(name, a, pl.BlockSpec((None, M, k), lambda q, j: (q, 0, 0)), d, pl.BlockSpec((M, tn), lambda q, j: (0, j)),
               jax.ShapeDtypeStruct((Q, k, N), out_dtype), pl.BlockSpec((None, k, tn), lambda q, j: (q, 0, j)),
               (Q, N // tn), "tn")


def _silu(v):
    return v * jax.nn.sigmoid(v)


def _ada_fwd(c16, ada_w):
    L, D, n = ada_w.shape

    def body(c_ref, w_ref, o_ref):
        o_ref[...] = _dot(_silu(c_ref[...]), w_ref[...], "nn")

    return pl.pallas_call(
        body, name="ada_fwd", grid=(L,), out_shape=jax.ShapeDtypeStruct((L, 16, n), F32),
        in_specs=[pl.BlockSpec((16, D), lambda l: (0, 0)), pl.BlockSpec((None, D, n), lambda l: (l, 0, 0))],
        out_specs=pl.BlockSpec((None, 16, n), lambda l: (l, 0, 0)), compiler_params=_cp("parallel"))(c16, ada_w)


def _ada_bwd(c16, dmod16):
    L, _, n = dmod16.shape
    D = c16.shape[1]

    def body(c_ref, d_ref, o_ref):
        o_ref[...] = _dot(_silu(c_ref[...]), d_ref[...], "tn")

    return pl.pallas_call(
        body, name="ada_bwd", grid=(L,), out_shape=jax.ShapeDtypeStruct((L, D, n), F32),
        in_specs=[pl.BlockSpec((16, D), lambda l: (0, 0)), pl.BlockSpec((None, 16, n), lambda l: (l, 0, 0))],
        out_specs=pl.BlockSpec((None, D, n), lambda l: (l, 0, 0)), compiler_params=_cp("parallel"))(c16, dmod16)


def _row_spec(tr, n):
    return pl.BlockSpec((tr, n), lambda i: (i, 0))


def _vec_spec(n):
    return pl.BlockSpec((1, n), lambda i: (0, 0))


def _rmsmod_fwd(name, x, g, sc, sh, after, tr=256):
    S, D = x.shape

    def body(x_ref, g_ref, sc_ref, sh_ref, after_ref, h_ref):
        xv = x_ref[...]
        rstd = lax.rsqrt(jnp.mean(xv * xv, axis=-1, keepdims=True) + EPS)
        y = xv * rstd * g_ref[...]
        h_ref[...] = (y * (1.0 + sc_ref[...]) + sh_ref[...]).astype(h_ref.dtype)

    return pl.pallas_call(
        body, name=name, grid=(S // tr,), out_shape=jax.ShapeDtypeStruct((S, D), _MXU_DTYPE),
        in_specs=[_row_spec(tr, D), _vec_spec(D), _vec_spec(D), _vec_spec(D), ANY], out_specs=_row_spec(tr, D),
        compiler_params=_cp("parallel"))(x, g, sc, sh, after)


def _acc_rows(ref, val, first):
    s = jnp.sum(val, axis=0, keepdims=True)

    @pl.when(first)
    def _():
        ref[...] = s

    @pl.when(jnp.logical_not(first))
    def _():
        ref[...] += s


def _gate_bwd_tail(dx, y_ref, gate_ref, dy_ref, dgate_ref, first):
    dy_ref[...] = (gate_ref[...] * dx).astype(dy_ref.dtype)
    _acc_rows(dgate_ref, dx * y_ref[...].astype(F32), first)


def _rmsmod_bwd(name, x, g, sc, dh, dres, after, y=None, gate=None, tr=256):
    S, D = x.shape
    tail = y is not None

    def body(x_ref, g_ref, sc_ref, dh_ref, dres_ref, after_ref, *rest):
        (y_ref, gate_ref), rest = (rest[:2], rest[2:]) if tail else ((None, None), rest)
        dx_ref, dg_ref, dsc_ref, dsh_ref = rest[:4]
        first = pl.program_id(0) == 0
        xv, dh_v, gv = x_ref[...], dh_ref[...], g_ref[...]
        rstd = lax.rsqrt(jnp.mean(xv * xv, axis=-1, keepdims=True) + EPS)
        xhat = xv * rstd
        _acc_rows(dsh_ref, dh_v, first)
        _acc_rows(dsc_ref, dh_v * (xhat * gv), first)
        dyg = dh_v * (1.0 + sc_ref[...])
        _acc_rows(dg_ref, dyg * xhat, first)
        dxhat = dyg * gv
        dx = dres_ref[...] + rstd * (dxhat - xhat * jnp.mean(dxhat * xhat, axis=-1, keepdims=True))
        dx_ref[...] = dx
        if tail:
            _gate_bwd_tail(dx, y_ref, gate_ref, rest[4], rest[5], first)

    vec = jax.ShapeDtypeStruct((1, D), F32)
    return pl.pallas_call(
        body, name=name, grid=(S // tr,),
        out_shape=(jax.ShapeDtypeStruct((S, D), F32), vec, vec, vec) + ((jax.ShapeDtypeStruct((S, D), _MXU_DTYPE), vec) if tail else ()),
        in_specs=[_row_spec(tr, D), _vec_spec(D), _vec_spec(D), _row_spec(tr, D), _row_spec(tr, D), ANY]
        + ([_row_spec(tr, D), _vec_spec(D)] if tail else []),
        out_specs=(_row_spec(tr, D), _vec_spec(D), _vec_spec(D), _vec_spec(D)) + ((_row_spec(tr, D), _vec_spec(D)) if tail else ()),
        compiler_params=_cp("arbitrary"))(x, g, sc, dh, dres, after, *((y, gate) if tail else ()))


def _loss_head(x, g, target, y, gate, tr=256):
    S, D = x.shape

    def body(x_ref, g_ref, t_ref, y_ref, gate_ref, loss_ref, dx_ref, dg_ref, dy_ref, dgate_ref):
        first = pl.program_id(0) == 0
        xv, gv = x_ref[...], g_ref[...]
        rstd = lax.rsqrt(jnp.mean(xv * xv, axis=-1, keepdims=True) + EPS)
        xhat = xv * rstd
        err = xhat * gv - t_ref[...]
        part = 0.5 * jnp.sum(jnp.mean(err * err, axis=-1, keepdims=True), axis=0, keepdims=True)

        @pl.when(first)
        def _():
            loss_ref[...] = part

        @pl.when(jnp.logical_not(first))
        def _():
            loss_ref[...] += part

        dout = err * (1.0 / D)
        _acc_rows(dg_ref, dout * xhat, first)
        dxhat = dout * gv
        dx = rstd * (dxhat - xhat * jnp.mean(dxhat * xhat, axis=-1, keepdims=True))
        dx_ref[...] = dx
        _gate_bwd_tail(dx, y_ref, gate_ref, dy_ref, dgate_ref, first)

    vec = jax.ShapeDtypeStruct((1, D), F32)
    return pl.pallas_call(
        body, name="loss_head", grid=(S // tr,),
        out_shape=(jax.ShapeDtypeStruct((1, 1), F32), jax.ShapeDtypeStruct((S, D), F32), vec,
                   jax.ShapeDtypeStruct((S, D), _MXU_DTYPE), vec),
        in_specs=[_row_spec(tr, D), _vec_spec(D), _row_spec(tr, D), _row_spec(tr, D), _vec_spec(D)],
        out_specs=(pl.BlockSpec((1, 1), lambda i: (0, 0)), _row_spec(tr, D), _vec_spec(D), _row_spec(tr, D), _vec_spec(D)),
        compiler_params=_cp("arbitrary"))(x, g, target, y, gate)


def _gate_bwd(name, dx, y, gate, tr=256):
    S, D = dx.shape

    def body(dx_ref, y_ref, g_ref, dy_ref, dg_ref):
        dxv = dx_ref[...]
        dy_ref[...] = (g_ref[...] * dxv).astype(dy_ref.dtype)
        _acc_rows(dg_ref, dxv * y_ref[...], pl.program_id(0) == 0)

    return pl.pallas_call(
        body, name=name, grid=(S // tr,),
        out_shape=(jax.ShapeDtypeStruct((S, D), _MXU_DTYPE), jax.ShapeDtypeStruct((1, D), F32)),
        in_specs=[_row_spec(tr, D), _row_spec(tr, D), _vec_spec(D)], out_specs=(_row_spec(tr, D), _vec_spec(D)),
        compiler_params=_cp("arbitrary"))(dx, y, gate)


def _shift_down(v, k):
    t = lax.broadcasted_iota(jnp.int32, v.shape, 0)
    return jnp.where(t >= k, pltpu.roll(v, k, axis=0), 0.0)


def _shift_up(v, k):
    n = v.shape[0]
    t = lax.broadcasted_iota(jnp.int32, v.shape, 0)
    return jnp.where(t < n - k, pltpu.roll(v, n - k, axis=0), 0.0)


def _window_sum(p, w, shift):
    s, k = p, 1
    while k < w:
        s = s + shift(s, k)
        k *= 2
    return s


def _pool_count(shape, w):
    t = lax.broadcasted_iota(jnp.int32, shape, 0)
    return jnp.minimum(t + 1, w).astype(F32)


def _ab_specs(S):
    zs = [pl.BlockSpec((None, S, 128), functools.partial(lambda g, q: (2 * q + g // 2, 0, g % 2), q=q)) for q in range(4)]
    return zs


def _ab_mix_fwd(z8, conv_w, mix_w, scale):
    S = z8.shape[1]

    def body(b_ref, c_ref, a_ref, p_ref, w_ref, mix_ref, sc_ref, y_ref):
        g = pl.program_id(0)
        cg = c_ref[...] * a_ref[...]
        w = w_ref[...]
        conv = w[0:1] * _shift_down(cg, 2) + w[1:2] * _shift_down(cg, 1) + w[2:3] * cg
        y_ref[0] = (b_ref[...] * conv).astype(y_ref.dtype)
        for gg, win in enumerate(POOL_WINDOWS):
            @pl.when(g == gg)
            def _(win=win):
                p = p_ref[...]
                pooled = _window_sum(p, win, _shift_down) / _pool_count(p.shape, win) - p
                y_ref[1] = (_dot(pooled, mix_ref[...], "nn") * sc_ref[...]).astype(y_ref.dtype)

    return pl.pallas_call(
        body, name="ab_mix_fwd", grid=(4,), out_shape=jax.ShapeDtypeStruct((2, S, 512), _MXU_DTYPE),
        in_specs=_ab_specs(S) + [pl.BlockSpec((3, 128), lambda g: (0, g)), pl.BlockSpec((None, 128, 128), lambda g: (g, 0, 0)),
                                 pl.BlockSpec((1, 128), lambda g: (0, g))],
        out_specs=pl.BlockSpec((2, S, 128), lambda g: (0, 0, g)), compiler_params=_cp("parallel"))(z8, z8, z8, z8, conv_w, mix_w, scale)


def _ab_mix_bwd(z8, dycat2, conv_w, mix_w, scale, after):
    S = z8.shape[1]

    def body(b_ref, c_ref, a_ref, p_ref, dy_ref, w_ref, mix_ref, sc_ref, after_ref, dz_ref, dw_ref, dmix_ref, dsc_ref):
        g = pl.program_id(0)
        bv, cv, av, w = b_ref[...], c_ref[...], a_ref[...], w_ref[...]
        dya = dy_ref[0]
        cg = cv * av
        cg1, cg2 = _shift_down(cg, 1), _shift_down(cg, 2)
        conv = w[0:1] * cg2 + w[1:2] * cg1 + w[2:3] * cg
        dz_ref[0] = (dya * conv).astype(dz_ref.dtype)
        dconv = dya * bv
        dcg = w[2:3] * dconv + w[1:2] * _shift_up(dconv, 1) + w[0:1] * _shift_up(dconv, 2)
        dz_ref[1] = (dcg * av).astype(dz_ref.dtype)
        dz_ref[2] = (dcg * cv).astype(dz_ref.dtype)
        dw_ref[0:1, :] = jnp.sum(dconv * cg2, axis=0, keepdims=True)
        dw_ref[1:2, :] = jnp.sum(dconv * cg1, axis=0, keepdims=True)
        dw_ref[2:3, :] = jnp.sum(dconv * cg, axis=0, keepdims=True)
        for gg, win in enumerate(POOL_WINDOWS):
            @pl.when(g == gg)
            def _(win=win):
                p, dyb, mix = p_ref[...], dy_ref[1], mix_ref[...]
                cnt = _pool_count(p.shape, win)
                pooled = _window_sum(p, win, _shift_down) / cnt - p
                dsc_ref[...] = jnp.sum(dyb * _dot(pooled, mix, "nn"), axis=0, keepdims=True)
                dmixed = dyb * sc_ref[...]
                dmix_ref[...] = _dot(pooled, dmixed, "tn")
                dpooled = _dot(dmixed, mix, "nt")
                dz_ref[3] = (_window_sum(dpooled / cnt, win, _shift_up) - dpooled).astype(dz_ref.dtype)

    return pl.pallas_call(
        body, name="ab_mix_bwd", grid=(4,),
        out_shape=(jax.ShapeDtypeStruct((4, 2, S, 256), _MXU_DTYPE), jax.ShapeDtypeStruct((3, 512), F32),
                   jax.ShapeDtypeStruct((4, 128, 128), F32), jax.ShapeDtypeStruct((1, 512), F32)),
        in_specs=_ab_specs(S) + [pl.BlockSpec((2, S, 128), lambda g: (0, 0, g)), pl.BlockSpec((3, 128), lambda g: (0, g)),
                                 pl.BlockSpec((None, 128, 128), lambda g: (g, 0, 0)), pl.BlockSpec((1, 128), lambda g: (0, g)), ANY],
        out_specs=(pl.BlockSpec((4, None, S, 128), lambda g: (0, g // 2, 0, g % 2)), pl.BlockSpec((3, 128), lambda g: (0, g)),
                   pl.BlockSpec((None, 128, 128), lambda g: (g, 0, 0)), pl.BlockSpec((1, 128), lambda g: (0, g))),
        compiler_params=_cp("parallel"))(z8, z8, z8, z8, dycat2, conv_w, mix_w, scale, after)


HALO = 16


def _ffn_specs(S, n, tr):
    nb = S // HALO
    tile = pl.BlockSpec((2, None, tr, n), lambda j, i: (0, j, i, 0))
    prev = pl.BlockSpec((2, None, HALO, n), lambda j, i: (0, j, jnp.maximum(i * (tr // HALO) - 1, 0), 0))
    nxt = pl.BlockSpec((2, None, HALO, n), lambda j, i: (0, j, jnp.minimum((i + 1) * (tr // HALO), nb - 1), 0))
    cw = pl.BlockSpec((2, None, 3, n), lambda j, i: (0, j, 0, 0))
    return tile, prev, nxt, cw


def _shifted_rows(ext, lo, rows):
    ext = ext.astype(F32)
    return pltpu.roll(ext, 1, axis=0)[lo:lo + rows], pltpu.roll(ext, 2, axis=0)[lo:lo + rows]


def _ffn_gate_fwd(name, u24, cw24, tr=256):
    _, J, S, n = u24.shape
    tile, prev, _, cw = _ffn_specs(S, n, tr)

    def body(u_ref, up_ref, w_ref, a_ref):
        keep = (pl.program_id(1) > 0).astype(u_ref.dtype)
        z = []
        for h in range(2):
            ext = jnp.concatenate([up_ref[h] * keep, u_ref[h]], axis=0)
            x1, x2 = _shifted_rows(ext, HALO, tr)
            w = w_ref[h]
            z.append(w[0:1] * x2 + w[1:2] * x1 + w[2:3] * u_ref[h].astype(F32))
        a_ref[...] = (_silu(z[0]) * z[1]).astype(a_ref.dtype)

    return pl.pallas_call(
        body, name=name, grid=(J, S // tr), out_shape=jax.ShapeDtypeStruct((J, S, n), _MXU_DTYPE),
        in_specs=[tile, prev, cw], out_specs=pl.BlockSpec((None, tr, n), lambda j, i: (j, i, 0)),
        compiler_params=_cp("parallel", "parallel"))(u24, u24, cw24)


def _ffn_gate_bwd(name, u24, cw24, da4, w_up24, after, tr=256):
    _, J, S, n = u24.shape
    K = w_up24.shape[2]
    nb = S // HALO
    tile = pl.BlockSpec((2, None, tr, n), lambda i, j: (0, j, i, 0))
    prev = pl.BlockSpec((2, None, HALO, n), lambda i, j: (0, j, jnp.maximum(i * (tr // HALO) - 1, 0), 0))
    nxt = pl.BlockSpec((2, None, HALO, n), lambda i, j: (0, j, jnp.minimum((i + 1) * (tr // HALO), nb - 1), 0))
    whole = lambda shape: pl.BlockSpec(shape, lambda i, j: (0,) * len(shape))

    def body(u_ref, up_ref, un_ref, cw_ref, da_ref, dan_ref, wup_ref, after_ref, du_ref, dcw_ref, dh_ref, acc_ref):
        i, j = pl.program_id(0), pl.program_id(1)
        first = i == 0
        keep_prev = (i > 0).astype(u_ref.dtype)
        keep_next = (i < S // tr - 1).astype(F32)
        w = [cw_ref[h, j] for h in range(2)]
        m = tr + HALO
        xs, z = [], []
        for h in range(2):
            ext = jnp.concatenate([up_ref[h] * keep_prev, u_ref[h], un_ref[h]], axis=0)
            x1, x2 = _shifted_rows(ext, HALO, m)
            x0 = ext[HALO:HALO + m].astype(F32)
            xs.append((x2, x1, x0))
            z.append(w[h][0:1] * x2 + w[h][1:2] * x1 + w[h][2:3] * x0)
        zg, zu = z
        da = jnp.concatenate([da_ref[...].astype(F32), dan_ref[...].astype(F32) * keep_next], axis=0)
        sg = jax.nn.sigmoid(zg)
        dz = [da * zu * (sg * (1.0 + zg * (1.0 - sg))), da * (zg * sg)]
        dh = None
        for h in range(2):
            d = dz[h]
            du = w[h][2:3] * d[:tr] + w[h][1:2] * pltpu.roll(d, m - 1, axis=0)[:tr] + w[h][0:1] * pltpu.roll(d, m - 2, axis=0)[:tr]
            du = du.astype(du_ref.dtype)
            du_ref[h] = du
            part = _dot(du, wup_ref[h, j], "nt")
            dh = part if dh is None else dh + part
            dt = d[:tr]
            parts = [jnp.sum(dt * xk[:tr], axis=0, keepdims=True) for xk in xs[h]]
            for k in range(3):
                @pl.when(first)
                def _(k=k, h=h):
                    dcw_ref[h, j, k:k + 1, :] = parts[k]

                @pl.when(jnp.logical_not(first))
                def _(k=k, h=h):
                    dcw_ref[h, j, k:k + 1, :] += parts[k]

        @pl.when(j == 0)
        def _():
            acc_ref[...] = dh

        @pl.when(j > 0)
        def _():
            acc_ref[...] += dh

        @pl.when(j == J - 1)
        def _():
            dh_ref[...] = acc_ref[...]

    da_tile = pl.BlockSpec((None, tr, n), lambda i, j: (j, i, 0))
    da_next = pl.BlockSpec((None, HALO, n), lambda i, j: (j, jnp.minimum((i + 1) * (tr // HALO), nb - 1), 0))
    return pl.pallas_call(
        body, name=name, grid=(S // tr, J),
        out_shape=(jax.ShapeDtypeStruct((2, J, S, n), _MXU_DTYPE), jax.ShapeDtypeStruct((2, J, 3, n), F32),
                   jax.ShapeDtypeStruct((S, K), F32)),
        in_specs=[tile, prev, nxt, whole((2, J, 3, n)), da_tile, da_next, whole((2, J, K, n)), ANY],
        out_specs=(tile, whole((2, J, 3, n)), pl.BlockSpec((tr, K), lambda i, j: (i, 0))),
        scratch_shapes=[pltpu.VMEM((tr, K), F32)],
        compiler_params=_cp("arbitrary", "arbitrary"))(u24, u24, u24, cw24, da4, da4, w_up24, after)


def _rms_rows(v, g):
    rstd = lax.rsqrt(jnp.mean(v * v, axis=-1, keepdims=True) + EPS)
    return v * rstd * g


def _rms_rows_bwd(v, g, dy):
    rstd = lax.rsqrt(jnp.mean(v * v, axis=-1, keepdims=True) + EPS)
    vhat = v * rstd
    dvhat = dy * g
    return rstd * (dvhat - vhat * jnp.mean(dvhat * vhat, axis=-1, keepdims=True)), dy * vhat


def _mla_prep_fwd(z, qg, kvg, tr=256):
    S = z.shape[0]

    def body(q_ref, kv_ref, qg_ref, kvg_ref, qn_ref, kvn_ref):
        qn_ref[...] = _rms_rows(q_ref[...], qg_ref[...]).astype(qn_ref.dtype)
        kvn_ref[...] = _rms_rows(kv_ref[...], kvg_ref[...]).astype(kvn_ref.dtype)

    return pl.pallas_call(
        body, name="mla_prep_fwd", grid=(S // tr,),
        out_shape=(jax.ShapeDtypeStruct((S, 256), _MXU_DTYPE), jax.ShapeDtypeStruct((S, 128), _MXU_DTYPE)),
        in_specs=[pl.BlockSpec((tr, 256), lambda i: (i, 0)), pl.BlockSpec((tr, 128), lambda i: (i, 2)), _vec_spec(256), _vec_spec(128)],
        out_specs=(_row_spec(tr, 256), _row_spec(tr, 128)), compiler_params=_cp("parallel"))(z, z, qg, kvg)


def _mla_prep_bwd(z, qg, kvg, dqn, dkvn, dkpe, duv, tr=256):
    S = z.shape[0]

    def body(q_ref, kv_ref, qg_ref, kvg_ref, dqn_ref, dkvn_ref, dkpe_ref, duv_ref, dz_ref, dqg_ref, dkvg_ref):
        first = pl.program_id(0) == 0
        dq, dqg = _rms_rows_bwd(q_ref[...], qg_ref[...], dqn_ref[...])
        dkv, dkvg = _rms_rows_bwd(kv_ref[...], kvg_ref[...], dkvn_ref[...])
        _acc_rows(dqg_ref, dqg, first)
        _acc_rows(dkvg_ref, dkvg, first)
        dz_ref[:, 0:256] = dq.astype(dz_ref.dtype)
        dz_ref[:, 256:384] = dkv.astype(dz_ref.dtype)
        dz_ref[:, 384:512] = dkpe_ref[...].astype(dz_ref.dtype)
        dz_ref[:, 512:1536] = duv_ref[...].astype(dz_ref.dtype)

    return pl.pallas_call(
        body, name="mla_prep_bwd", grid=(S // tr,),
        out_shape=(jax.ShapeDtypeStruct((S, 1536), _MXU_DTYPE), jax.ShapeDtypeStruct((1, 256), F32), jax.ShapeDtypeStruct((1, 128), F32)),
        in_specs=[pl.BlockSpec((tr, 256), lambda i: (i, 0)), pl.BlockSpec((tr, 128), lambda i: (i, 2)), _vec_spec(256), _vec_spec(128),
                  _row_spec(tr, 256), _row_spec(tr, 128), _row_spec(tr, 128), _row_spec(tr, 1024)],
        out_specs=(_row_spec(tr, 1536), _vec_spec(256), _vec_spec(128)),
        compiler_params=_cp("arbitrary"))(z, z, qg, kvg, dqn, dkvn, dkpe, duv)


def _rope(v, cos, sa, sb):
    return v * cos + pltpu.roll(v, 112, axis=1) * sa + pltpu.roll(v, 16, axis=1) * sb


def _rope_t(d, cos, sa, sb):
    return d * cos + pltpu.roll(d * sa, 16, axis=1) + pltpu.roll(d * sb, 112, axis=1)


def _qkv_rope_fwd(qn, kvn, z, w_uq_t, w_kv, cosq, cosk, sa, sb, tr=256):
    S = qn.shape[0]

    def body(qn_ref, kvn_ref, kpe_ref, wq_ref, wkv_ref, cq_ref, ck_ref, sa_ref, sb_ref, qo_ref, ko_ref, vo_ref):
        cq, ck, sa_v, sb_v = cq_ref[...], ck_ref[...], sa_ref[...], sb_ref[...]
        q = _dot(qn_ref[...], wq_ref[...], "nt")
        kv = _dot(kvn_ref[...], wkv_ref[...], "nn")
        kpe = _rope(kpe_ref[...], ck, sa_v, sb_v)
        for h in range(8):
            cols = slice(128 * h, 128 * h + 128)
            qo_ref[:, cols] = _rope(q[:, cols], cq, sa_v, sb_v).astype(qo_ref.dtype)
            ko_ref[:, cols] = (kv[:, cols] + kpe).astype(ko_ref.dtype)
        vo_ref[...] = kv[:, 1024:1536].astype(vo_ref.dtype)

    tab = _row_spec(tr, 128)
    whole = lambda a: pl.BlockSpec(a.shape, lambda i: (0, 0))
    return pl.pallas_call(
        body, name="qkv_rope_fwd", grid=(S // tr,),
        out_shape=(jax.ShapeDtypeStruct((S, 1024), _MXU_DTYPE), jax.ShapeDtypeStruct((S, 1024), _MXU_DTYPE),
                   jax.ShapeDtypeStruct((S, 512), _MXU_DTYPE)),
        in_specs=[_row_spec(tr, 256), _row_spec(tr, 128), pl.BlockSpec((tr, 128), lambda i: (i, 3)), whole(w_uq_t), whole(w_kv),
                  tab, tab, tab, tab],
        out_specs=(_row_spec(tr, 1024), _row_spec(tr, 1024), _row_spec(tr, 512)),
        compiler_params=_cp("parallel"))(qn, kvn, z, w_uq_t, w_kv, cosq, cosk, sa, sb)


def _attn_bwd_prep(o, dycat2, tr=256):
    S = o.shape[0]

    def body(o_ref, do_ref, delta_ref, doa_ref, dob_ref):
        do = do_ref[...]
        prod = do * o_ref[...]
        lane = lax.broadcasted_iota(jnp.int32, do.shape, 1)
        for p in range(4):
            cols = slice(128 * p, 128 * p + 128)
            first = lane[:, cols] < 128 * p + 64
            da = jnp.sum(jnp.where(first, prod[:, cols], 0.0), axis=-1, keepdims=True)
            db = jnp.sum(jnp.where(first, 0.0, prod[:, cols]), axis=-1, keepdims=True)
            delta_ref[p] = jnp.where(first, da, db)
            doa_ref[p] = jnp.where(first, do[:, cols], 0.0).astype(doa_ref.dtype)
            dob_ref[p] = jnp.where(first, 0.0, do[:, cols]).astype(dob_ref.dtype)

    pair = pl.BlockSpec((4, tr, 128), lambda i: (0, i, 0))
    return pl.pallas_call(
        body, name="attn_bwd_prep", grid=(S // tr,),
        out_shape=(jax.ShapeDtypeStruct((4, S, 128), F32), jax.ShapeDtypeStruct((4, S, 128), _MXU_DTYPE),
                   jax.ShapeDtypeStruct((4, S, 128), _MXU_DTYPE)),
        in_specs=[_row_spec(tr, 512), pl.BlockSpec((None, tr, 512), lambda i: (0, i, 0))],
        out_specs=(pair, pair, pair), compiler_params=_cp("parallel"))(o, dycat2)


def _rope_bwd(dq, dk, dv, cosq, cosk, sa, sb, tr=256):
    S = dq.shape[0]

    def body(dq_ref, dk_ref, dv_ref, cq_ref, ck_ref, sa_ref, sb_ref, dqo_ref, dkv_ref, dkpe_ref):
        cq, ck, sa_v, sb_v = cq_ref[...], ck_ref[...], sa_ref[...], sb_ref[...]
        tot = jnp.zeros((tr, 128), F32)
        for h in range(8):
            cols = slice(128 * h, 128 * h + 128)
            dqo_ref[:, cols] = _rope_t(dq_ref[:, cols], cq, sa_v, sb_v).astype(dqo_ref.dtype)
            dkh = dk_ref[:, cols]
            tot = tot + dkh
            dkv_ref[:, cols] = dkh.astype(dkv_ref.dtype)
        dkv_ref[:, 1024:1536] = dv_ref[...].astype(dkv_ref.dtype)
        dkpe_ref[...] = _rope_t(tot, ck, sa_v, sb_v)

    tab = _row_spec(tr, 128)
    return pl.pallas_call(
        body, name="rope_bwd", grid=(S // tr,),
        out_shape=(jax.ShapeDtypeStruct((S, 1024), _MXU_DTYPE), jax.ShapeDtypeStruct((S, 1536), _MXU_DTYPE),
                   jax.ShapeDtypeStruct((S, 128), F32)),
        in_specs=[_row_spec(tr, 1024), _row_spec(tr, 1024), _row_spec(tr, 512), tab, tab, tab, tab],
        out_specs=(_row_spec(tr, 1024), _row_spec(tr, 1536), _row_spec(tr, 128)),
        compiler_params=_cp("parallel"))(dq, dk, dv, cosq, cosk, sa, sb)


NEG = -1e30


def _attn_fwd(q, k, v, tq=256, tk=256):
    S = q.shape[0]
    assert tq == tk

    def body(q_ref, k_ref, v_ref, o_ref, lse_ref):
        i = pl.program_id(1)
        qs = [q_ref[:, 0:128], q_ref[:, 128:256]]

        def step(kb, carry, diagonal=False):
            start = pl.multiple_of(kb * tk, tk)
            vv = v_ref[pl.ds(start, tk), :]
            out = []
            for h in range(2):
                m, l, acc = carry[3 * h:3 * h + 3]
                s = _dot(qs[h], k_ref[pl.ds(start, tk), 128 * h:128 * h + 128], "nt") * ATTN_SCALE
                if diagonal:
                    s = jnp.where(below, s, NEG)
                m_new = jnp.maximum(m, jnp.max(s, axis=-1, keepdims=True))
                alpha = jnp.exp(m - m_new)
                p = jnp.exp(s - m_new)
                out += [m_new, alpha * l + jnp.sum(p, axis=-1, keepdims=True), alpha * acc + _dot(p, vv, "nn")]
            return tuple(out)

        below = lax.broadcasted_iota(jnp.int32, (tq, tk), 1) <= lax.broadcasted_iota(jnp.int32, (tq, tk), 0)
        init = (jnp.full((tq, 1), NEG, F32), jnp.zeros((tq, 1), F32), jnp.zeros((tq, 128), F32)) * 2
        ma, la, acca, mb, lb, accb = step(i, lax.fori_loop(0, i, step, init), diagonal=True)
        lane = lax.broadcasted_iota(jnp.int32, (tq, 128), 1)
        o_ref[...] = jnp.where(lane < 64, acca / la, accb / lb)
        lse_ref[...] = jnp.where(lane < 64, ma + jnp.log(la), mb + jnp.log(lb))

    return pl.pallas_call(
        body, name="attn_fwd", grid=(4, S // tq),
        out_shape=(jax.ShapeDtypeStruct((S, 512), F32), jax.ShapeDtypeStruct((4, S, 128), F32)),
        in_specs=[pl.BlockSpec((tq, 256), lambda p, i: (i, p)), pl.BlockSpec((S, 256), lambda p, i: (0, p)),
                  pl.BlockSpec((S, 128), lambda p, i: (0, p))],
        out_specs=(pl.BlockSpec((tq, 128), lambda p, i: (i, p)), pl.BlockSpec((None, tq, 128), lambda p, i: (p, i, 0))),
        compiler_params=_cp("parallel", "parallel"))(q, k, v)


def _attn_bwd(q, k, v, lse, delta, doa, dob, tq=256, tk=256):
    S = q.shape[0]
    assert tq == tk

    def body(q_ref, k_ref, v_ref, lse_ref, delta_ref, doa_ref, dob_ref, dq_ref, dk_ref, dv_ref):
        j = pl.program_id(1)

        @pl.when(j == 0)
        def _():
            dq_ref[...] = jnp.zeros_like(dq_ref)

        below = lax.broadcasted_iota(jnp.int32, (tq, tk), 1) <= lax.broadcasted_iota(jnp.int32, (tq, tk), 0)
        ks = [k_ref[:, 0:128], k_ref[:, 128:256]]
        vv = v_ref[...]

        def step(qb, carry, diagonal=False):
            dka, dkb, dvp = carry
            start = pl.multiple_of(qb * tq, tq)
            rows = pl.ds(start, tq)
            lse_v, delta_v = lse_ref[rows, :], delta_ref[rows, :]
            dos = [doa_ref[rows, :], dob_ref[rows, :]]
            dks = [dka, dkb]
            for h in range(2):
                delta = delta_v[:, 64 * h:64 * h + 1]
                do_h = dos[h]
                qh = q_ref[rows, 128 * h:128 * h + 128]
                s = _dot(qh, ks[h], "nt") * ATTN_SCALE
                p = jnp.exp(s - lse_v[:, 64 * h:64 * h + 1])
                if diagonal:
                    p = jnp.where(below, p, 0.0)
                dvp = dvp + _dot(p, do_h, "tn")
                ds = p * (_dot(do_h, vv, "nt") - delta) * ATTN_SCALE
                dq_ref[rows, 128 * h:128 * h + 128] += _dot(ds, ks[h], "nn")
                dks[h] = dks[h] + _dot(ds, qh, "tn")
            return dks[0], dks[1], dvp

        zero = jnp.zeros((tk, 128), F32)
        dka, dkb, dvp = lax.fori_loop(j + 1, S // tq, step, step(j, (zero, zero, zero), diagonal=True))
        dk_ref[:, 0:128] = dka
        dk_ref[:, 128:256] = dkb
        dv_ref[...] = dvp

    return pl.pallas_call(
        body, name="attn_bwd", grid=(4, S // tk),
        out_shape=(jax.ShapeDtypeStruct((S, 1024), F32), jax.ShapeDtypeStruct((S, 1024), F32), jax.ShapeDtypeStruct((S, 512), F32)),
        in_specs=[pl.BlockSpec((S, 256), lambda p, j: (0, p)), pl.BlockSpec((tk, 256), lambda p, j: (j, p)),
                  pl.BlockSpec((tk, 128), lambda p, j: (j, p))] + [pl.BlockSpec((None, S, 128), lambda p, j: (p, 0, 0))] * 4,
        out_specs=(pl.BlockSpec((S, 256), lambda p, j: (0, p)), pl.BlockSpec((tk, 256), lambda p, j: (j, p)),
                   pl.BlockSpec((tk, 128), lambda p, j: (j, p))),
        compiler_params=_cp("parallel", "arbitrary"))(q, k, v, lse, delta, doa, dob)


CHUNK = 128
GELU_C = math.sqrt(2.0 / math.pi)


def _gelu(v):
    t = jnp.tanh(GELU_C * (v + 0.044715 * (v * v * v)))
    return v * (0.5 * (1.0 + t)), t


def _gelu_grad(v, t):
    return 0.5 * (1.0 + t) + v * (0.5 * (1.0 - t * t) * GELU_C * (1.0 + 3.0 * 0.044715 * v * v))


def _tril(w):
    r = lax.broadcasted_iota(jnp.int32, w.shape, 0)
    c = lax.broadcasted_iota(jnp.int32, w.shape, 1)
    return jnp.where(c <= r, w, 0.0)


def _layer_norm(v, g, b):
    xc = v - jnp.mean(v, axis=-1, keepdims=True)
    rstd = lax.rsqrt(jnp.mean(xc * xc, axis=-1, keepdims=True) + EPS)
    xhat = xc * rstd
    return xhat * g + b, xhat, rstd


def _sgu_fwd(z, o, ln_g, ln_b, w_s, b_st, tr=256):
    S = z.shape[0]

    def body(u_ref, v_ref, o_ref, g_ref, b_ref, ws_ref, bs_ref, y_ref):
        gu, _ = _gelu(u_ref[...])
        gv, _ = _gelu(v_ref[...])
        vln, _, _ = _layer_norm(gv, g_ref[...], b_ref[...])
        y_ref[0] = o_ref[...].astype(y_ref.dtype)
        for g in range(4):
            wt = _tril(ws_ref[g])
            cols = slice(128 * g, 128 * g + 128)
            for ch in range(tr // CHUNK):
                rows = slice(CHUNK * ch, CHUNK * ch + CHUNK)
                mixed = _dot(wt, vln[rows, cols], "nn") + bs_ref[:, g:g + 1]
                y_ref[1, rows, cols] = (gu[rows, cols] * mixed).astype(y_ref.dtype)

    return pl.pallas_call(
        body, name="sgu_fwd", grid=(S // tr,), out_shape=jax.ShapeDtypeStruct((2, S, 512), _MXU_DTYPE),
        in_specs=[pl.BlockSpec((tr, 512), lambda i: (i, 1)), pl.BlockSpec((tr, 512), lambda i: (i, 2)), _row_spec(tr, 512),
                  _vec_spec(512), _vec_spec(512), pl.BlockSpec((4, 128, 128), lambda i: (0, 0, 0)), pl.BlockSpec((128, 4), lambda i: (0, 0))],
        out_specs=pl.BlockSpec((2, tr, 512), lambda i: (0, i, 0)), compiler_params=_cp("parallel"))(z, z, o, ln_g, ln_b, w_s, b_st)


def _sgu_bwd(z, dycat2, ln_g, ln_b, w_s, b_st, tr=256):
    S = z.shape[0]

    def body(u_ref, v_ref, dy_ref, g_ref, b_ref, ws_ref, bs_ref, duv_ref, dg_ref, db_ref, dws_ref, dbs_ref):
        first = pl.program_id(0) == 0
        u_pre, v_pre = u_ref[...], v_ref[...]
        gu, tu = _gelu(u_pre)
        gv, tv = _gelu(v_pre)
        gain = g_ref[...]
        vln, xhat, rstd = _layer_norm(gv, gain, b_ref[...])

        @pl.when(first)
        def _():
            dws_ref[...] = jnp.zeros_like(dws_ref)
            dbs_ref[...] = jnp.zeros_like(dbs_ref)

        dvln_cols = []
        for g in range(4):
            wt = _tril(ws_ref[g])
            cols = slice(128 * g, 128 * g + 128)
            dmixed_sum = jnp.zeros((CHUNK, 128), F32)
            dw = jnp.zeros((CHUNK, CHUNK), F32)
            dvln_rows = []
            for ch in range(tr // CHUNK):
                rows = slice(CHUNK * ch, CHUNK * ch + CHUNK)
                vt = vln[rows, cols]
                mixed = _dot(wt, vt, "nn") + bs_ref[:, g:g + 1]
                dyd = dy_ref[rows, cols]
                duv_ref[rows, cols] = (dyd * mixed * _gelu_grad(u_pre[rows, cols], tu[rows, cols])).astype(duv_ref.dtype)
                dmixed = dyd * gu[rows, cols]
                dmixed_sum = dmixed_sum + dmixed
                dw = dw + _dot(dmixed, vt, "nt")
                dvln_rows.append(_dot(wt, dmixed, "tn"))
            dws_ref[g] += _tril(dw)
            dbs_ref[g:g + 1, :] += jnp.sum(dmixed_sum.T, axis=0, keepdims=True)
            dvln_cols.append(jnp.concatenate(dvln_rows, axis=0))
        dvln = jnp.concatenate(dvln_cols, axis=1)
        _acc_rows(dg_ref, dvln * xhat, first)
        _acc_rows(db_ref, dvln, first)
        dxhat = dvln * gain
        dgv = rstd * (dxhat - jnp.mean(dxhat, axis=-1, keepdims=True) - xhat * jnp.mean(dxhat * xhat, axis=-1, keepdims=True))
        duv_ref[:, 512:1024] = (dgv * _gelu_grad(v_pre, tv)).astype(duv_ref.dtype)

    return pl.pallas_call(
        body, name="sgu_bwd", grid=(S // tr,),
        out_shape=(jax.ShapeDtypeStruct((S, 1024), _MXU_DTYPE), jax.ShapeDtypeStruct((1, 512), F32), jax.ShapeDtypeStruct((1, 512), F32),
                   jax.ShapeDtypeStruct((4, 128, 128), F32), jax.ShapeDtypeStruct((4, 128), F32)),
        in_specs=[pl.BlockSpec((tr, 512), lambda i: (i, 1)), pl.BlockSpec((tr, 512), lambda i: (i, 2)),
                  pl.BlockSpec((None, tr, 512), lambda i: (1, i, 0)), _vec_spec(512), _vec_spec(512),
                  pl.BlockSpec((4, 128, 128), lambda i: (0, 0, 0)), pl.BlockSpec((128, 4), lambda i: (0, 0))],
        out_specs=(_row_spec(tr, 1024), _vec_spec(512), _vec_spec(512), pl.BlockSpec((4, 128, 128), lambda i: (0, 0, 0)),
                   pl.BlockSpec((4, 128), lambda i: (0, 0))),
        compiler_params=_cp("arbitrary"))(z, z, dycat2, ln_g, ln_b, w_s, b_st)


def _sum_parts(name, parts, tr=512):
    P, R, C = parts.shape
    tr = _tile(R, tr) if R % 8 == 0 else R

    def body(p_ref, o_ref):
        g = p_ref[0]
        for k in range(1, P):
            g = g + p_ref[k]
        o_ref[...] = g

    return pl.pallas_call(
        body, name=name, grid=(R // tr,), out_shape=jax.ShapeDtypeStruct((R, C), F32),
        in_specs=[pl.BlockSpec((P, tr, C), lambda i: (0, i, 0))], out_specs=_row_spec(tr, C),
        compiler_params=_cp("parallel"))(parts)


def _adamw_math(w, m, v, g):
    c1 = 1.0 / (1.0 - ADAM_B1 ** ADAM_STEP)
    c2 = 1.0 / (1.0 - ADAM_B2 ** ADAM_STEP)
    m2 = ADAM_B1 * m + (1.0 - ADAM_B1) * g
    v2 = ADAM_B2 * v + (1.0 - ADAM_B2) * (g * g)
    return -ADAM_LR * ((m2 * c1) / (jnp.sqrt(v2 * c2) + ADAM_EPS) + ADAM_WD * w), m2, v2


def _adamw_small(name, params, parts):
    n = len(params)

    def body(*refs):
        ins, outs = refs[:4 * n], refs[4 * n:]
        for i in range(n):
            w_ref, m_ref, v_ref, p_ref = ins[4 * i:4 * i + 4]
            g = p_ref[0]
            for k in range(1, N_DEV):
                g = g + p_ref[k]
            delta, m2, v2 = _adamw_math(w_ref[...], m_ref[...], v_ref[...], g)
            outs[4 * i][...] = g
            outs[4 * i + 1][...] = delta
            outs[4 * i + 2][...] = m2
            outs[4 * i + 3][...] = v2

    flat = [a for (w, m, v), p in zip(params, parts) for a in (w, m, v, p)]
    out = pl.pallas_call(
        body, name=name, out_shape=[jax.ShapeDtypeStruct(w.shape, F32) for (w, _, _) in params for _ in range(4)],
        compiler_params=pltpu.CompilerParams(vmem_limit_bytes=_VMEM_LIMIT))(*flat)
    return [out[4 * i:4 * i + 4] for i in range(n)]


ADAMW_BLOCK_BYTES = 36 * 2 ** 20


def _adamw(name, w, m, v, parts):
    L, R, C = w.shape
    P = parts[0].shape[0]
    row_bytes = 2 * C * (7 * 4 + P * parts[0].dtype.itemsize)
    tr = R
    if R * row_bytes > ADAMW_BLOCK_BYTES:
        tr = next(t for t in (1024, 512, 256, 128, 64, 32, 16) if R % t == 0 and t * row_bytes <= ADAMW_BLOCK_BYTES)
    nr = R // tr
    c1 = 1.0 / (1.0 - ADAM_B1 ** ADAM_STEP)
    c2 = 1.0 / (1.0 - ADAM_B2 ** ADAM_STEP)

    def body(w_ref, m_ref, v_ref, *rest):
        p_refs, (g_ref, d_ref, mo_ref, vo_ref) = rest[:L], rest[L:]
        for ll in range(L):
            @pl.when(pl.program_id(0) == ll)
            def _(p_ref=p_refs[ll]):
                g = p_ref[0].astype(F32)
                for k in range(1, P):
                    g = g + p_ref[k].astype(F32)
                m2 = ADAM_B1 * m_ref[...] + (1.0 - ADAM_B1) * g
                v2 = ADAM_B2 * v_ref[...] + (1.0 - ADAM_B2) * (g * g)
                g_ref[...] = g
                mo_ref[...] = m2
                vo_ref[...] = v2
                d_ref[...] = -ADAM_LR * ((m2 * c1) / (jnp.sqrt(v2 * c2) + ADAM_EPS) + ADAM_WD * w_ref[...])

    def part_spec(ll):
        return pl.BlockSpec((P, tr, C), lambda l, i: (0, jnp.where(l == ll, i, jnp.where(l < ll, 0, nr - 1)), 0))

    full = pl.BlockSpec((None, tr, C), lambda l, i: (l, i, 0))
    sds = jax.ShapeDtypeStruct((L, R, C), F32)
    return pl.pallas_call(
        body, name=name, grid=(L, nr), out_shape=(sds, sds, sds, sds),
        in_specs=[full] * 3 + [part_spec(ll) for ll in range(L)],
        out_specs=(full,) * 4, compiler_params=_cp("arbitrary", "arbitrary"))(w, m, v, *parts)


def _rope_tables(positions):
    half = 16
    inv_freq = 10000.0 ** (-jnp.arange(half, dtype=F32) / half)
    ang = positions.astype(F32)[:, None] * inv_freq
    cos, sin = jnp.cos(ang), jnp.sin(ang)
    S = positions.shape[0]
    z16, z32, z64 = jnp.zeros((S, 16), F32), jnp.zeros((S, 32), F32), jnp.zeros((S, 64), F32)
    cosk = jnp.concatenate([z64, cos, cos, z32], axis=1)
    cosq = jnp.concatenate([jnp.ones((S, 64), F32), cos, cos, z32], axis=1)
    sa = jnp.concatenate([z64, -sin, z16, z32], axis=1)
    sb = jnp.concatenate([z64, z16, sin, z32], axis=1)
    return cosq, cosk, sa, sb


def _ffn_fwd(l, x, mod, n2g, get_w_up8, cw24, get_w_down4):
    sh, sc, gate = mod
    h = _rmsmod_fwd(f"ffn{l}_norm", x, n2g, sc, sh, n2g)
    w_up8 = get_w_up8(h)
    u8 = _mm_cols(f"ffn{l}_up", h, w_up8, out_dtype=ACT_DTYPE, tm=2048)
    S, n = u8.shape[1], u8.shape[2]
    u24 = u8.reshape(2, 4, S, n)
    a4 = _ffn_gate_fwd(f"ffn{l}_gate", u24, cw24)
    w_down4 = get_w_down4(a4)
    f, x_new = _mm_rows_resid(f"ffn{l}_down", a4, w_down4, x, gate)
    return x_new, (x, h, u24, a4, f), w_up8, w_down4


def _ffn_bwd(l, dx, df, dgate, saved, mod, n2g, w_up8, cw24, w_down4, me, y_prev, gate_prev):
    sh, sc, gate = mod
    x, h, u24, a4, f = saved
    da4 = _mm_rows_dx(f"ffn{l}_down_dx", df, w_down4, out_dtype=ACT_DTYPE, tm=2048)
    dw_down4 = _mm_rows_dw(f"ffn{l}_down_dw", a4, df, out_dtype=WIRE_DTYPE)
    sent_down, token = _exchange_start(f"scatter_ffn{l}_down", [dw_down4.reshape(8, 352, dw_down4.shape[2])], True, dgate, me)
    du24, dcw24, dh = _ffn_gate_bwd(f"ffn{l}_act_bwd", u24, cw24, da4, w_up8.reshape((2, 4) + w_up8.shape[1:]), token)
    du8 = du24.reshape((8,) + du24.shape[2:])
    dw_up8t = _mm_cols_dwt(f"ffn{l}_up_dw", h, du8, out_dtype=WIRE_DTYPE, tk=1024)
    sent_up, token = _exchange_start(f"scatter_ffn{l}_up", [dw_up8t], True, dcw24, me)
    dx_new, dn2g, dsc, dsh, dy_prev, dgate_prev = _rmsmod_bwd(f"ffn{l}_norm_bwd", x, n2g, sc, dh, dx, token, y_prev, gate_prev)
    return dx_new, dict(sent_up=sent_up, sent_down=sent_down, cw24=dcw24, n2g=dn2g, mod=(dsh, dsc, dgate)), dy_prev, dgate_prev


def kernel(x, c, positions, ada_w, ada_b, norm1_g, norm2_g, ab_w_in, a_conv_w, b_mix_w, b_scale, ab_w_out, cd_w_in, c_q_norm_g, c_w_uq, c_kv_norm_g, c_w_ukv, d_ln_g, d_ln_b, d_w_s, d_b_s, cd_w_out, ffn_w_up, ffn_conv_w, ffn_w_down, final_norm_g, loss_target, m_ada_w, m_ada_b, m_norm1_g, m_norm2_g, m_ab_w_in, m_a_conv_w, m_b_mix_w, m_b_scale, m_ab_w_out, m_cd_w_in, m_c_q_norm_g, m_c_w_uq, m_c_kv_norm_g, m_c_w_ukv, m_d_ln_g, m_d_ln_b, m_d_w_s, m_d_b_s, m_cd_w_out, m_ffn_w_up, m_ffn_conv_w, m_ffn_w_down, m_final_norm_g, v_ada_w, v_ada_b, v_norm1_g, v_norm2_g, v_ab_w_in, v_a_conv_w, v_b_mix_w, v_b_scale, v_ab_w_out, v_cd_w_in, v_c_q_norm_g, v_c_w_uq, v_c_kv_norm_g, v_c_w_ukv, v_d_ln_g, v_d_ln_b, v_d_w_s, v_d_b_s, v_cd_w_out, v_ffn_w_up, v_ffn_conv_w, v_ffn_w_down, v_final_norm_g):
    S, D = x.shape[1], x.shape[2]
    me = 4 * lax.axis_index("x") + 2 * lax.axis_index("y") + lax.axis_index("c")
    x0, target = x[0], loss_target[0]
    W = _MXU_DTYPE

    small_shapes = [(1024,), (3, 64), (32,), (64,), (64,), (2, 3, 704)]
    (g0,) = _exchange("gather_small", [[_pack([c, a_conv_w, c_q_norm_g, d_ln_g, d_ln_b, ffn_conv_w])]], scatter=False)
    c_all, aconv_s, qg_s, lng_s, lnb_s, fcw_s = _unpack(g0[:, 0], small_shapes, lead=(N_DEV,))
    conv_w = aconv_s.transpose(1, 0, 2).reshape(3, 512)
    qg, ln_g, ln_b = qg_s.reshape(1, 256), lng_s.reshape(1, 512), lnb_s.reshape(1, 512)
    cw24 = [fcw_s[:, l].reshape(2, 4, 3, 704) for l in range(2)]
    c16 = jnp.pad(c_all, ((0, 16 - N_DEV), (0, 0)))

    mod_cols = _ada_fwd(c16, ada_w)
    (g1,) = _exchange("gather_mod", [[_pack([mod_cols])]], scatter=False)
    mod_all = _unpack(g1[:, 0], [(2, 16, 768)], lead=(N_DEV,))[0]
    mod_mine = lax.dynamic_index_in_dim(mod_all, me, axis=2, keepdims=False)
    mod = mod_mine.transpose(1, 0, 2).reshape(2, 6 * D) + ada_b
    mods = [[mod[l, k * D:(k + 1) * D].reshape(1, D) for k in range(6)] for l in range(2)]

    gw_ab, token = _hier_gather_start("gather_w_ab", [ab_w_in[0].astype(W), ab_w_out[0].astype(W)], mod, me)
    gw_up0, token = _hier_gather_start("gather_w_ffn0_up", [ffn_w_up[0].astype(W)], token, me)
    gw_rest, started = _exchange_start("gather_w_rest", [
        ffn_w_down[0].astype(W), cd_w_in[0].T.astype(W), c_w_uq[0].T.astype(W), c_w_ukv[0].astype(W), cd_w_out[0].astype(W),
        ffn_w_up[1].astype(W), ffn_w_down[1].astype(W)], False, token, me)

    cosq, cosk, sa, sb = _rope_tables(positions[0])
    n1g = [norm1_g[l].reshape(1, D) for l in range(2)]
    n2g = [norm2_g[l].reshape(1, D) for l in range(2)]
    mix_w, scale = b_mix_w[0], b_scale
    kvg = c_kv_norm_g
    w_s, b_st = d_w_s[0], d_b_s[0].T

    sh1, sc1, g1m = mods[0][:3]
    h_ab = _rmsmod_fwd("ab_norm", x0, n1g[0], sc1, sh1, started)
    w_abin8, w_about = _hier_gather_wait("wait_w_ab", _hier_gather_forward("forward_w_ab", gw_ab, h_ab), h_ab)
    w_about2 = w_about.reshape(2, 512, D)
    z8 = _mm_cols("ab_in", h_ab, w_abin8, tm=2048)
    ycat_ab = _ab_mix_fwd(z8, conv_w, mix_w, scale)
    y_ab, x1 = _mm_rows_resid("ab_out", ycat_ab, w_about2, x0, g1m)
    w_up8, w_down4 = [None, None], [None, None]
    gw_up0 = _hier_gather_forward("forward_w_ffn0_up", gw_up0, x1)
    x2, ffn0_saved, w_up8[0], w_down4[0] = _ffn_fwd(
        0, x1, mods[0][3:], n2g[0], lambda after: _hier_gather_wait("wait_w_ffn0_up", gw_up0, after)[0], cw24[0],
        lambda after: _exchange_wait("wait_w_ffn0_down", gw_rest, after, [0])[0].reshape(4, 704, D))

    w_cdin, w_uq, w_ukv, w_cdout = _exchange_wait("wait_w_cd", gw_rest, x2, [1, 2, 3, 4])
    w_cdout2 = w_cdout.reshape(2, 512, D)
    w_cd_t = w_cdin.reshape(1440, D)
    zr = lambda n: jnp.zeros((n, D), W)
    w_cd_pad = jnp.concatenate([w_cd_t[:384], zr(64), w_cd_t[384:416], zr(32), w_cd_t[416:]], axis=0)
    w_uq_pad = jnp.pad(w_uq, ((0, 0), (0, 32), (0, 0))).reshape(1024, 256)
    w_ukv_h = w_ukv.transpose(1, 0, 2)
    w_k_pad = jnp.pad(w_ukv_h[:, :, :64], ((0, 0), (0, 0), (0, 64))).reshape(128, 1024)
    w_kv_pad = jnp.concatenate([w_k_pad, w_ukv_h[:, :, 64:].reshape(128, 512)], axis=1)

    sh1, sc1, g1c = mods[1][:3]
    h_cd = _rmsmod_fwd("cd_norm", x2, n1g[1], sc1, sh1, n1g[1])
    z_cd = _mm_nt("cd_in", h_cd, w_cd_pad)
    qn, kvn = _mla_prep_fwd(z_cd, qg, kvg)
    q_r, k_r, v_r = _qkv_rope_fwd(qn, kvn, z_cd, w_uq_pad, w_kv_pad, cosq, cosk, sa, sb)
    o, lse = _attn_fwd(q_r, k_r, v_r)
    ycat_cd = _sgu_fwd(z_cd, o, ln_g, ln_b, w_s, b_st)
    y_cd, x3 = _mm_rows_resid("cd_out", ycat_cd, w_cdout2, x2, g1c)
    x4, ffn1_saved, w_up8[1], w_down4[1] = _ffn_fwd(
        1, x3, mods[1][3:], n2g[1], lambda after: _exchange_wait("wait_w_ffn1_up", gw_rest, after, [5])[0], cw24[1],
        lambda after: _exchange_wait("wait_w_ffn1_down", gw_rest, after, [6])[0].reshape(4, 704, D))

    loss_local, dx4, dfg, df1, dgate1 = _loss_head(x4, final_norm_g.reshape(1, D), target, ffn1_saved[4], mods[1][5])

    dx3, gf1, dy, dg1c = _ffn_bwd(1, dx4, df1, dgate1, ffn1_saved, mods[1][3:], n2g[1], w_up8[1], cw24[1], w_down4[1], me, y_cd, g1c)

    dycat = _mm_rows_dx("cd_out_dx", dy, w_cdout2)
    dw_cdout = _mm_rows_dw("cd_out_dw", ycat_cd, dy, out_dtype=WIRE_DTYPE)
    duv, dln_g, dln_b, dws, dbs = _sgu_bwd(z_cd, dycat, ln_g, ln_b, w_s, b_st)
    dq_r, dk_r, dv_r = _attn_bwd(q_r, k_r, v_r, lse, *_attn_bwd_prep(o, dycat))
    dqraw, dkvall, dkpe = _rope_bwd(dq_r, dk_r, dv_r, cosq, cosk, sa, sb)
    dqn = _mm_nn("cd_uq_dx", dqraw, w_uq_pad, tn=256)
    dkvn = _mm_nt("cd_ukv_dx", dkvall, w_kv_pad, tn=128)
    dw_uq_pad = _mm_tn("cd_uq_dw", dqraw, qn, tn=256)
    dw_kv_pad = _mm_tn("cd_ukv_dw", kvn, dkvall, tm=128)
    dz_cd, dqg, dkvg = _mla_prep_bwd(z_cd, qg, kvg, dqn, dkvn, dkpe, duv)
    dh_cd = _mm_nn("cd_in_dx", dz_cd, w_cd_pad)
    dw_cd_pad = _mm_tn("cd_in_dw", dz_cd, h_cd)
    dw_cd8 = jnp.concatenate([dw_cd_pad[:384], dw_cd_pad[448:480], dw_cd_pad[512:]], axis=0).astype(WIRE_DTYPE).reshape(8, 180, D)
    dw_uq8 = dw_uq_pad.reshape(8, 128, 256)[:, :96].astype(WIRE_DTYPE)
    dw_ukv8 = jnp.concatenate([dw_kv_pad[:, :1024].reshape(128, 8, 128)[:, :, :64], dw_kv_pad[:, 1024:].reshape(128, 8, 64)],
                              axis=2).transpose(1, 0, 2).astype(WIRE_DTYPE)
    sent_cd, token = _exchange_start("scatter_cd", [dw_cd8, dw_uq8, dw_ukv8, dw_cdout.reshape(8, 128, D)], True, dqg, me)
    early_names = ["c_kv_norm_g", "d_w_s", "d_b_s", "final_norm_g", "c_q_norm_g", "d_ln_g", "d_ln_b"]
    early_grads = [dkvg, dws.reshape(512, 128), dbs, dfg, dqg.reshape(8, 1, 32), dln_g.reshape(8, 1, 64), dln_b.reshape(8, 1, 64)]
    early_sent, token = _exchange_start("gather_small_grads_early", early_grads, [False] * 4 + [True] * 3, token, me)
    dx2, dn1g_cd, dsc1_cd, dsh1_cd, df0, dgate0 = _rmsmod_bwd("cd_norm_bwd", x2, n1g[1], sc1, dh_cd, dx3, token,
                                                              ffn0_saved[4], mods[0][5])

    dx1, gf0, dy, dg1m = _ffn_bwd(0, dx2, df0, dgate0, ffn0_saved, mods[0][3:], n2g[0], w_up8[0], cw24[0], w_down4[0], me, y_ab, g1m)

    dw_about = _mm_rows_dw("ab_out_dw", ycat_ab, dy, out_dtype=WIRE_DTYPE)
    sent_about, token = _exchange_start("scatter_ab_out", [dw_about.reshape(8, 128, D)], True, dg1m, me)
    dycat = _mm_rows_dx("ab_out_dx", dy, w_about2)
    dz8, dconv_w, dmix_w, dscale = _ab_mix_bwd(z8, dycat, conv_w, mix_w, scale, token)
    dz8 = dz8.reshape(8, S, 256)
    dw_abin8 = _mm_cols_dw("ab_in_dw", h_ab, dz8, out_dtype=WIRE_DTYPE, tk=1024)
    sent_abin, token = _exchange_start("scatter_ab_in", [dw_abin8], True, dscale, me)
    dh_ab = _mm_cols_dx("ab_in_dx", dz8, w_abin8)
    dx0, dn1g_ab, dsc1_ab, dsh1_ab = _rmsmod_bwd("ab_norm_bwd", x0, n1g[0], mods[0][1], dh_ab, dx1, token)

    dmod = jnp.stack([jnp.concatenate([dsh1_ab, dsc1_ab, dg1m, *gf0["mod"]], axis=1)[0],
                      jnp.concatenate([dsh1_cd, dsc1_cd, dg1c, *gf1["mod"]], axis=1)[0]])
    late_names = ["ada_b", "norm1_g", "norm2_g", "b_mix_w", "b_scale", "a_conv_w", "ffn_conv_w"]
    late_grads = [dmod, jnp.concatenate([dn1g_ab, dn1g_cd]), jnp.concatenate([gf0["n2g"], gf1["n2g"]]),
                  dmix_w.reshape(512, 128), dscale, dconv_w.reshape(3, 8, 64).transpose(1, 0, 2),
                  jnp.stack([gf0["cw24"].reshape(8, 3, 704), gf1["cw24"].reshape(8, 3, 704)], axis=1),
                  jnp.pad(loss_local, ((0, 0), (0, 127)))]
    small_view = dict(ada_b=(2, 6 * D), norm1_g=(2, D), norm2_g=(2, D), b_mix_w=(512, 128), b_scale=(1, 512), c_kv_norm_g=(1, 128),
                      d_w_s=(512, 128), d_b_s=(4, 128), final_norm_g=(1, D),
                      a_conv_w=(3, 64), c_q_norm_g=(1, 32), d_ln_g=(1, 64), d_ln_b=(1, 64), ffn_conv_w=(2, 3, 704))
    late_sent, token = _exchange_start("gather_small_grads_late", late_grads, [False] * 5 + [True] * 2 + [False], dx0, me)

    res = {}

    def update(name, w, m, v, parts, shape3d):
        outs = _adamw("adamw_" + name, w.reshape(shape3d), m.reshape(shape3d), v.reshape(shape3d),
                      [p.reshape((p.shape[0],) + shape3d[1:]) for p in parts])
        res[name] = [o_.reshape(w.shape) for o_ in outs]

    p_cdin, p_uq, p_ukv, p_cdout = _exchange_wait("wait_scatter_cd", sent_cd, token)
    swap = lambda a: jnp.swapaxes(a, 1, 2)
    update("cd_w_in", swap(cd_w_in), swap(m_cd_w_in), swap(v_cd_w_in), [p_cdin], (1, 180, D))
    update("c_w_uq", swap(c_w_uq), swap(m_c_w_uq), swap(v_c_w_uq), [p_uq], (1, 96, 256))
    for name in ("cd_w_in", "c_w_uq"):
        res[name] = [swap(o_) for o_ in res[name]]
    update("c_w_ukv", c_w_ukv, m_c_w_ukv, v_c_w_ukv, [p_ukv], (1, 128, 128))
    update("cd_w_out", cd_w_out, m_cd_w_out, v_cd_w_out, [p_cdout], (1, 128, D))
    (p_dn1,) = _exchange_wait("wait_scatter_ffn1_down", gf1["sent_down"], token)
    (p_dn0,) = _exchange_wait("wait_scatter_ffn0_down", gf0["sent_down"], res["cd_w_out"][0])
    update("ffn_w_down", ffn_w_down, m_ffn_w_down, v_ffn_w_down, [p_dn0, p_dn1], (2, 352, D))
    (p_up1,) = _exchange_wait("wait_scatter_ffn1_up", gf1["sent_up"], token)
    (p_up0,) = _exchange_wait("wait_scatter_ffn0_up", gf0["sent_up"], res["ffn_w_down"][0])
    swap = lambda a: jnp.swapaxes(a, 1, 2)
    update("ffn_w_up", swap(ffn_w_up), swap(m_ffn_w_up), swap(v_ffn_w_up), [p_up0, p_up1], (2, 704, D))
    up_done = res["ffn_w_up"][0]
    res["ffn_w_up"] = [swap(o_) for o_ in res["ffn_w_up"]]
    (p_about,) = _exchange_wait("wait_scatter_ab_out", sent_about, up_done)
    update("ab_w_out", ab_w_out, m_ab_w_out, v_ab_w_out, [p_about], (1, 128, D))
    (p_abin,) = _exchange_wait("wait_scatter_ab_in", sent_abin, res["ab_w_out"][0])
    update("ab_w_in", ab_w_in, m_ab_w_in, v_ab_w_in, [p_abin], (1, D, 256))

    early_parts = _exchange_wait("wait_small_grads_early", early_sent, res["ab_w_in"][0])
    late_parts = _exchange_wait("wait_small_grads_late", late_sent, res["ab_w_in"][0])
    small_names = early_names + late_names
    small_parts = list(early_parts) + list(late_parts[:7])
    loss = jnp.sum(late_parts[7][:, 0, 0])
    dmod_all = late_parts[0]
    dmod_cols = lax.dynamic_slice_in_dim(dmod_all, me * 768, 768, axis=2).transpose(1, 0, 2)
    g_ada_w = _ada_bwd(c16, jnp.pad(dmod_cols, ((0, 0), (0, 16 - N_DEV), (0, 0))))
    update("ada_w", ada_w, m_ada_w, v_ada_w, [g_ada_w[l][None] for l in range(2)], (2, D, 768))

    small_w = dict(ada_b=(ada_b, m_ada_b, v_ada_b), norm1_g=(norm1_g, m_norm1_g, v_norm1_g), norm2_g=(norm2_g, m_norm2_g, v_norm2_g),
                   b_mix_w=(b_mix_w, m_b_mix_w, v_b_mix_w), b_scale=(b_scale, m_b_scale, v_b_scale),
                   c_kv_norm_g=(c_kv_norm_g, m_c_kv_norm_g, v_c_kv_norm_g), d_w_s=(d_w_s, m_d_w_s, v_d_w_s),
                   d_b_s=(d_b_s, m_d_b_s, v_d_b_s), final_norm_g=(final_norm_g, m_final_norm_g, v_final_norm_g),
                   a_conv_w=(a_conv_w, m_a_conv_w, v_a_conv_w), c_q_norm_g=(c_q_norm_g, m_c_q_norm_g, v_c_q_norm_g),
                   d_ln_g=(d_ln_g, m_d_ln_g, v_d_ln_g), d_ln_b=(d_ln_b, m_d_ln_b, v_d_ln_b),
                   ffn_conv_w=(ffn_conv_w, m_ffn_conv_w, v_ffn_conv_w))
    small_out = _adamw_small("adamw_small", [tuple(a.reshape(small_view[n]) for a in small_w[n]) for n in small_names],
                             list(small_parts))
    for n, outs in zip(small_names, small_out):
        res[n] = [o_.reshape(small_w[n][0].shape) for o_ in outs]

    order = ["ada_w", "ada_b", "norm1_g", "norm2_g", "ab_w_in", "a_conv_w", "b_mix_w", "b_scale", "ab_w_out", "cd_w_in", "c_q_norm_g",
             "c_w_uq", "c_kv_norm_g", "c_w_ukv", "d_ln_g", "d_ln_b", "d_w_s", "d_b_s", "cd_w_out", "ffn_w_up", "ffn_conv_w",
             "ffn_w_down", "final_norm_g"]
    return (loss, dx0[None], *[res[n][0] for n in order], *[res[n][1] for n in order], *[res[n][2] for n in order],
            *[res[n][3] for n in order])
```

```python
import functools
import math

import jax
import jax.numpy as jnp
from jax import lax
from jax.experimental import pallas as pl
from jax.experimental.pallas import tpu as pltpu

F32 = jnp.float32
BF16 = jnp.bfloat16
_MXU_DTYPE = BF16
WIRE_DTYPE = BF16
ACT_DTYPE = BF16
_VMEM_LIMIT = 56 * 2 ** 20
N_DEV = 8
EPS = 1e-6
POOL_WINDOWS = (2, 4, 8, 16)
ATTN_SCALE = (64 + 32) ** -0.5
ADAM_LR, ADAM_B1, ADAM_B2, ADAM_EPS, ADAM_WD, ADAM_STEP = 0.001, 0.9, 0.999, 1e-08, 0.01, 10
MESH = pl.DeviceIdType.MESH
ANY = pl.BlockSpec(memory_space=pl.ANY)


def _cp(*sem):
    return pltpu.CompilerParams(dimension_semantics=sem, vmem_limit_bytes=_VMEM_LIMIT)


def _dot(a, b, contract):
    dn = {"nn": (((1,), (0,)), ((), ())), "nt": (((1,), (1,)), ((), ())), "tn": (((0,), (0,)), ((), ()))}[contract]
    return lax.dot_general(a.astype(_MXU_DTYPE), b.astype(_MXU_DTYPE), dn, preferred_element_type=F32)


def _my_position():
    x, y, c = lax.axis_index("x"), lax.axis_index("y"), lax.axis_index("c")
    return x, y, c, 4 * x + 2 * y + c


def _exchange(name, groups, scatter):
    flat = [a for g in groups for a in g]
    n_in, n_grp = len(flat), len(groups)
    out_shapes = []
    for g in groups:
        slab = g[0].shape[1:] if scatter else g[0].shape
        out_shapes.append(jax.ShapeDtypeStruct((N_DEV, len(g)) + tuple(slab), g[0].dtype))

    def body(*refs):
        ins, outs = refs[:n_in], refs[n_in:n_in + n_grp]
        send_sems, recv_sems, local_sems = refs[n_in + n_grp:]
        x, y, c, me = _my_position()
        i = 0
        for gi, g in enumerate(groups):
            for l in range(len(g)):
                src = ins[i]
                i += 1
                pltpu.make_async_copy(src.at[me] if scatter else src, outs[gi].at[me, l], local_sems.at[gi]).start()
                for k in range(1, N_DEV):
                    px = 1 - x if k & 4 else x
                    py = 1 - y if k & 2 else y
                    pc = 1 - c if k & 1 else c
                    peer = 4 * px + 2 * py + pc
                    pltpu.make_async_remote_copy(
                        src_ref=src.at[peer] if scatter else src, dst_ref=outs[gi].at[me, l],
                        send_sem=send_sems.at[gi], recv_sem=recv_sems.at[gi],
                        device_id=(px, py, pc), device_id_type=MESH).start()
        for gi in range(n_grp):
            mine = outs[gi].at[me]
            pltpu.make_async_copy(mine, mine, local_sems.at[gi]).wait()
            seven = outs[gi].at[pl.ds(0, N_DEV - 1)]
            w = pltpu.make_async_remote_copy(src_ref=seven, dst_ref=seven, send_sem=send_sems.at[gi],
                                             recv_sem=recv_sems.at[gi], device_id=(x, y, c), device_id_type=MESH)
            w.wait_send()
            w.wait_recv()

    return pl.pallas_call(
        body, name=name, out_shape=tuple(out_shapes),
        in_specs=[ANY] * n_in, out_specs=tuple([ANY] * n_grp),
        scratch_shapes=[pltpu.SemaphoreType.DMA((n_grp,)), pltpu.SemaphoreType.DMA((n_grp,)),
                        pltpu.SemaphoreType.DMA((n_grp,))],
        compiler_params=pltpu.CompilerParams(has_side_effects=True),
    )(*flat)


HBM_SPEC = pl.BlockSpec(memory_space=pltpu.HBM)
SEM_SPEC = pl.BlockSpec(memory_space=pltpu.SEMAPHORE)
EFFECT = pltpu.SideEffectType.DATAFLOW_SIDE_EFFECTING


def _put_mine(name, srcs, scatter, me):
    n = len(srcs)
    slabs = [tuple(s.shape[1:] if sc else s.shape) for s, sc in zip(srcs, scatter)]

    def body(me_ref, *refs):
        for i in range(n):
            refs[n + i][...] = refs[i][...]

    def at_me(slab):
        return pl.BlockSpec((None,) + slab, lambda g, me_ref, nd=len(slab): (me_ref[0],) + (0,) * nd)

    def whole(slab):
        return pl.BlockSpec(slab, lambda g, me_ref, nd=len(slab): (0,) * nd)

    return pl.pallas_call(
        body, name=name,
        grid_spec=pltpu.PrefetchScalarGridSpec(
            num_scalar_prefetch=1, grid=(1,),
            in_specs=[at_me(slab) if sc else whole(slab) for slab, sc in zip(slabs, scatter)],
            out_specs=[at_me(slab) for slab in slabs]),
        out_shape=[jax.ShapeDtypeStruct((N_DEV,) + slab, s.dtype) for slab, s in zip(slabs, srcs)],
        compiler_params=_cp("arbitrary"))(me.reshape(1), *srcs)


def _exchange_start(name, srcs, scatter, after, me):
    n = len(srcs)
    scatter = list(scatter) if isinstance(scatter, (list, tuple)) else [scatter] * n
    lands = _put_mine(name + "_mine", srcs, scatter, me)
    srcs = [pltpu.with_memory_space_constraint(a, pltpu.HBM) for a in srcs]
    lands = [pltpu.with_memory_space_constraint(a, pltpu.HBM) for a in lands]

    def body(*refs):
        ins, land = refs[:n], refs[n:2 * n]
        send_sems, recv_sems, token = refs[2 * n + 1], refs[2 * n + 2], refs[-1]
        x, y, c, me_in = _my_position()
        for i in range(n):
            for k in range(1, N_DEV):
                px = 1 - x if k & 4 else x
                py = 1 - y if k & 2 else y
                pc = 1 - c if k & 1 else c
                pltpu.make_async_remote_copy(
                    src_ref=ins[i].at[4 * px + 2 * py + pc] if scatter[i] else ins[i], dst_ref=land[i].at[me_in],
                    send_sem=send_sems.at[i], recv_sem=recv_sems.at[i],
                    device_id=(px, py, pc), device_id_type=MESH).start()
        token[...] = jnp.zeros_like(token)

    outs = pl.pallas_call(
        body, name=name,
        out_shape=(pltpu.SemaphoreType.DMA((n,)), pltpu.SemaphoreType.DMA((n,)),
                   *[pltpu.HBM(a.shape, a.dtype) for a in srcs], *[pltpu.HBM(a.shape, a.dtype) for a in lands],
                   jax.ShapeDtypeStruct((8, 128), F32)),
        in_specs=[HBM_SPEC] * (2 * n) + [ANY],
        out_specs=(SEM_SPEC, SEM_SPEC, *[HBM_SPEC] * (2 * n), pl.BlockSpec(memory_space=pltpu.VMEM)),
        input_output_aliases={i: 2 + i for i in range(2 * n)},
        compiler_params=pltpu.CompilerParams(has_side_effects=EFFECT),
    )(*srcs, *lands, after)
    return (outs[0], outs[1], outs[2:2 + n], outs[2 + n:2 + 2 * n]), outs[-1]


def _exchange_wait(name, handle, after, which=None):
    send_sems, recv_sems, srcs, lands = handle
    which = list(range(len(srcs))) if which is None else list(which)
    srcs, lands = [srcs[i] for i in which], [lands[i] for i in which]
    n = len(srcs)

    def body(*refs):
        land, send_ref, recv_ref = refs[n:2 * n], refs[2 * n], refs[2 * n + 1]
        x, y, c, _ = _my_position()
        for k, i in enumerate(which):
            seven = land[k].at[pl.ds(0, N_DEV - 1)]
            w = pltpu.make_async_remote_copy(src_ref=seven, dst_ref=seven, send_sem=send_ref.at[i], recv_sem=recv_ref.at[i],
                                             device_id=(x, y, c), device_id_type=MESH)
            w.wait_send()
            w.wait_recv()

    outs = pl.pallas_call(
        body, name=name,
        out_shape=(*[pltpu.HBM(a.shape, a.dtype) for a in srcs], *[pltpu.HBM(a.shape, a.dtype) for a in lands]),
        in_specs=[HBM_SPEC] * (2 * n) + [SEM_SPEC, SEM_SPEC, ANY],
        out_specs=tuple([HBM_SPEC] * (2 * n)),
        input_output_aliases={i: i for i in range(2 * n)},
        compiler_params=pltpu.CompilerParams(has_side_effects=EFFECT),
    )(*srcs, *lands, send_sems, recv_sems, after)
    return outs[n:]


def _other_chips(x, y):
    return [(1 - x, y), (x, 1 - y), (1 - x, 1 - y)]


def _hier_gather_start(name, srcs, after, me):
    n = len(srcs)
    lands = _put_mine(name + "_mine", srcs, [False] * n, me)
    srcs = [pltpu.with_memory_space_constraint(a, pltpu.HBM) for a in srcs]
    lands = [pltpu.with_memory_space_constraint(a, pltpu.HBM) for a in lands]

    def body(*refs):
        ins, land = refs[:n], refs[n:2 * n]
        ici_send, ici_recv, d2d_send, d2d_recv = refs[2 * n + 1:2 * n + 5]
        token = refs[-1]
        x, y, c, me_in = _my_position()
        for i in range(n):
            pltpu.make_async_remote_copy(src_ref=ins[i], dst_ref=land[i].at[me_in], send_sem=d2d_send.at[i], recv_sem=d2d_recv.at[i],
                                         device_id=(x, y, 1 - c), device_id_type=MESH).start()
            for px, py in _other_chips(x, y):
                pltpu.make_async_remote_copy(src_ref=ins[i], dst_ref=land[i].at[me_in], send_sem=ici_send.at[i],
                                             recv_sem=ici_recv.at[i], device_id=(px, py, c), device_id_type=MESH).start()
        token[...] = jnp.zeros_like(token)

    sem = pltpu.SemaphoreType.DMA((n,))
    outs = pl.pallas_call(
        body, name=name,
        out_shape=(sem, sem, sem, sem, *[pltpu.HBM(a.shape, a.dtype) for a in srcs], *[pltpu.HBM(a.shape, a.dtype) for a in lands],
                   jax.ShapeDtypeStruct((8, 128), F32)),
        in_specs=[HBM_SPEC] * (2 * n) + [ANY],
        out_specs=(SEM_SPEC,) * 4 + (HBM_SPEC,) * (2 * n) + (pl.BlockSpec(memory_space=pltpu.VMEM),),
        input_output_aliases={i: 4 + i for i in range(2 * n)},
        compiler_params=pltpu.CompilerParams(has_side_effects=EFFECT),
    )(*srcs, *lands, after)
    return (outs[:4], outs[4:4 + n], outs[4 + n:4 + 2 * n]), outs[-1]


def _hier_gather_forward(name, handle, after):
    sems, srcs, lands = handle
    n = len(srcs)

    def body(*refs):
        land = refs[n:2 * n]
        ici_send, ici_recv, d2d_send, d2d_recv = refs[2 * n:2 * n + 4]
        x, y, c, _ = _my_position()
        for i in range(n):
            three = land[i].at[pl.ds(0, 3)]
            pltpu.make_async_remote_copy(src_ref=three, dst_ref=three, send_sem=ici_send.at[i], recv_sem=ici_recv.at[i],
                                         device_id=(x, y, c), device_id_type=MESH).wait_recv()
            for px, py in _other_chips(x, y):
                slab = land[i].at[4 * px + 2 * py + c]
                pltpu.make_async_remote_copy(src_ref=slab, dst_ref=slab, send_sem=d2d_send.at[i], recv_sem=d2d_recv.at[i],
                                             device_id=(x, y, 1 - c), device_id_type=MESH).start()

    outs = pl.pallas_call(
        body, name=name,
        out_shape=(*[pltpu.HBM(a.shape, a.dtype) for a in srcs], *[pltpu.HBM(a.shape, a.dtype) for a in lands]),
        in_specs=[HBM_SPEC] * (2 * n) + [SEM_SPEC] * 4 + [ANY],
        out_specs=tuple([HBM_SPEC] * (2 * n)),
        input_output_aliases={i: i for i in range(2 * n)},
        compiler_params=pltpu.CompilerParams(has_side_effects=EFFECT),
    )(*srcs, *lands, *sems, after)
    return (sems, outs[:n], outs[n:])


def _hier_gather_wait(name, handle, after):
    sems, srcs, lands = handle
    n = len(srcs)

    def body(*refs):
        land = refs[n:2 * n]
        ici_send, ici_recv, d2d_send, d2d_recv = refs[2 * n:2 * n + 4]
        x, y, c, _ = _my_position()
        for i in range(n):
            three, four = land[i].at[pl.ds(0, 3)], land[i].at[pl.ds(0, 4)]
            pltpu.make_async_remote_copy(src_ref=three, dst_ref=three, send_sem=ici_send.at[i], recv_sem=ici_recv.at[i],
                                         device_id=(x, y, c), device_id_type=MESH).wait_send()
            w = pltpu.make_async_remote_copy(src_ref=four, dst_ref=four, send_sem=d2d_send.at[i], recv_sem=d2d_recv.at[i],
                                             device_id=(x, y, c), device_id_type=MESH)
            w.wait_send()
            w.wait_recv()

    outs = pl.pallas_call(
        body, name=name,
        out_shape=(*[pltpu.HBM(a.shape, a.dtype) for a in srcs], *[pltpu.HBM(a.shape, a.dtype) for a in lands]),
        in_specs=[HBM_SPEC] * (2 * n) + [SEM_SPEC] * 4 + [ANY],
        out_specs=tuple([HBM_SPEC] * (2 * n)),
        input_output_aliases={i: i for i in range(2 * n)},
        compiler_params=pltpu.CompilerParams(has_side_effects=EFFECT),
    )(*srcs, *lands, *sems, after)
    return outs[n:]


def _pack(arrs):
    flat = jnp.concatenate([a.reshape(-1).astype(F32) for a in arrs])
    n = flat.shape[0]
    rows = -(-n // 1024) * 8
    return jnp.pad(flat, (0, rows * 128 - n)).reshape(rows, 128)


def _unpack(buf, shapes, lead=()):
    flat = buf.reshape(lead + (-1,))
    out, off = [], 0
    for s in shapes:
        n = math.prod(s)
        out.append(flat[..., off:off + n].reshape(lead + tuple(s)))
        off += n
    return out


def _mm(name, a, a_spec, b, b_spec, out_sds, o_spec, grid, contract, nk=1, stacked=0):
    o_blk = tuple(d for d in o_spec.block_shape if d is not None)

    def body(a_ref, b_ref, o_ref, *acc):
        if stacked:
            r = _dot(a_ref[0], b_ref[0], contract)
            for q in range(1, stacked):
                r = r + _dot(a_ref[q], b_ref[q], contract)
        else:
            r = _dot(a_ref[...], b_ref[...], contract)
        if nk == 1:
            o_ref[...] = r.astype(o_ref.dtype)
        else:
            k = pl.program_id(len(grid) - 1)

            @pl.when(k == 0)
            def _():
                acc[0][...] = r

            @pl.when(k > 0)
            def _():
                acc[0][...] += r

            @pl.when(k == nk - 1)
            def _():
                o_ref[...] = acc[0][...].astype(o_ref.dtype)

    sem = ("parallel",) * (len(grid) - 1) + (("arbitrary",) if nk > 1 else ("parallel",))
    return pl.pallas_call(
        body, name=name, out_shape=out_sds, grid=grid, in_specs=[a_spec, b_spec], out_specs=o_spec,
        scratch_shapes=[pltpu.VMEM(o_blk, F32)] if nk > 1 else [], compiler_params=_cp(*sem))(a, b)


def _tile(n, want):
    t = min(n, want)
    assert n % t == 0, (n, t)
    return t


def _mm_nn(name, a, b, out_dtype=F32, tm=512, tn=512):
    (M, K), N = a.shape, b.shape[1]
    tm, tn = _tile(M, tm), _tile(N, tn)
    return _mm(name, a, pl.BlockSpec((tm, K), lambda i, j: (i, 0)), b, pl.BlockSpec((K, tn), lambda i, j: (0, j)),
               jax.ShapeDtypeStruct((M, N), out_dtype), pl.BlockSpec((tm, tn), lambda i, j: (i, j)),
               (M // tm, N // tn), "nn")


def _mm_nt(name, a, b, out_dtype=F32, tm=512, tn=512):
    (M, K), N = a.shape, b.shape[0]
    tm, tn = _tile(M, tm), _tile(N, tn)
    return _mm(name, a, pl.BlockSpec((tm, K), lambda i, j: (i, 0)), b, pl.BlockSpec((tn, K), lambda i, j: (j, 0)),
               jax.ShapeDtypeStruct((M, N), out_dtype), pl.BlockSpec((tm, tn), lambda i, j: (i, j)),
               (M // tm, N // tn), "nt")


def _mm_tn(name, a, b, out_dtype=F32, tm=512, tn=512):
    (K, M), N = a.shape, b.shape[1]
    tm, tn = _tile(M, tm), _tile(N, tn)
    return _mm(name, a, pl.BlockSpec((K, tm), lambda i, j: (0, i)), b, pl.BlockSpec((K, tn), lambda i, j: (0, j)),
               jax.ShapeDtypeStruct((M, N), out_dtype), pl.BlockSpec((tm, tn), lambda i, j: (i, j)),
               (M // tm, N // tn), "tn")


def _mm_cols(name, a, w, out_dtype=F32, tm=512):
    (M, K), (J, _, n) = a.shape, w.shape
    tm = _tile(M, tm)
    return _mm(name, a, pl.BlockSpec((tm, K), lambda j, i: (i, 0)), w, pl.BlockSpec((None, K, n), lambda j, i: (j, 0, 0)),
               jax.ShapeDtypeStruct((J, M, n), out_dtype), pl.BlockSpec((None, tm, n), lambda j, i: (j, i, 0)),
               (J, M // tm), "nn")


def _mm_cols_dx(name, d, w, out_dtype=F32, tm=512, jb=None):
    (J, M, n), K = d.shape, w.shape[1]
    tm, jb = _tile(M, tm), J if jb is None else jb
    return _mm(name, d, pl.BlockSpec((jb, tm, n), lambda i, j: (j, i, 0)), w, pl.BlockSpec((jb, K, n), lambda i, j: (j, 0, 0)),
               jax.ShapeDtypeStruct((M, K), out_dtype), pl.BlockSpec((tm, K), lambda i, j: (i, 0)),
               (M // tm, J // jb), "nt", nk=J // jb, stacked=jb)


def _mm_cols_dw(name, a, d, out_dtype=F32, tk=512):
    (M, K), (J, _, n) = a.shape, d.shape
    tk = _tile(K, tk)
    return _mm(name, a, pl.BlockSpec((M, tk), lambda j, i: (0, i)), d, pl.BlockSpec((None, M, n), lambda j, i: (j, 0, 0)),
               jax.ShapeDtypeStruct((J, K, n), out_dtype), pl.BlockSpec((None, tk, n), lambda j, i: (j, i, 0)),
               (J, K // tk), "tn")


def _mm_cols_dwt(name, a, d, out_dtype=F32, tk=512):
    (M, K), (J, _, n) = a.shape, d.shape
    tk = _tile(K, tk)
    return _mm(name, d, pl.BlockSpec((None, M, n), lambda j, i: (j, 0, 0)), a, pl.BlockSpec((M, tk), lambda j, i: (0, i)),
               jax.ShapeDtypeStruct((J, n, K), out_dtype), pl.BlockSpec((None, n, tk), lambda j, i: (j, 0, i)),
               (J, K // tk), "tn")


def _mm_rows_resid(name, a, w, resid, gate, tm=512):
    (Q, M, k), N = a.shape, w.shape[2]
    tm = _tile(M, tm)

    def body(a_ref, w_ref, r_ref, g_ref, y_ref, x_ref):
        y = _dot(a_ref[0], w_ref[0], "nn")
        for q in range(1, Q):
            y = y + _dot(a_ref[q], w_ref[q], "nn")
        y_ref[...] = y.astype(y_ref.dtype)
        x_ref[...] = r_ref[...] + g_ref[...] * y

    return pl.pallas_call(
        body, name=name, grid=(M // tm,),
        out_shape=(jax.ShapeDtypeStruct((M, N), ACT_DTYPE), jax.ShapeDtypeStruct((M, N), F32)),
        in_specs=[pl.BlockSpec((Q, tm, k), lambda i: (0, i, 0)), pl.BlockSpec((Q, k, N), lambda i: (0, 0, 0)),
                  pl.BlockSpec((tm, N), lambda i: (i, 0)), pl.BlockSpec((1, N), lambda i: (0, 0))],
        out_specs=(pl.BlockSpec((tm, N), lambda i: (i, 0)), pl.BlockSpec((tm, N), lambda i: (i, 0))),
        compiler_params=_cp("parallel"))(a, w, resid, gate)


def _mm_rows_dx(name, d, w, out_dtype=F32, tm=512):
    (M, N), (Q, k, _) = d.shape, w.shape
    tm = _tile(M, tm)
    return _mm(name, d, pl.BlockSpec((tm, N), lambda q, i: (i, 0)), w, pl.BlockSpec((None, k, N), lambda q, i: (q, 0, 0)),
               jax.ShapeDtypeStruct((Q, M, k), out_dtype), pl.BlockSpec((None, tm, k), lambda q, i: (q, i, 0)),
               (Q, M // tm), "nt")


def _mm_rows_dw(name, a, d, out_dtype=F32, tn=512):
    (Q, M, k), N = a.shape, d.shape[1]
    tn = _tile(N, tn)
    return _mm(name, a, pl.BlockSpec((None, M, k), lambda q, j: (q, 0, 0)), d, pl.BlockSpec((M, tn), lambda q, j: (0, j)),
               jax.ShapeDtypeStruct((Q, k, N), out_dtype), pl.BlockSpec((None, k, tn), lambda q, j: (q, 0, j)),
               (Q, N // tn), "tn")


def _silu(v):
    return v * jax.nn.sigmoid(v)


def _ada_fwd(c16, ada_w):
    L, D, n = ada_w.shape

    def body(c_ref, w_ref, o_ref):
        o_ref[...] = _dot(_silu(c_ref[...]), w_ref[...], "nn")

    return pl.pallas_call(
        body, name="ada_fwd", grid=(L,), out_shape=jax.ShapeDtypeStruct((L, 16, n), F32),
        in_specs=[pl.BlockSpec((16, D), lambda l: (0, 0)), pl.BlockSpec((None, D, n), lambda l: (l, 0, 0))],
        out_specs=pl.BlockSpec((None, 16, n), lambda l: (l, 0, 0)), compiler_params=_cp("parallel"))(c16, ada_w)


def _ada_bwd(c16, dmod16):
    L, _, n = dmod16.shape
    D = c16.shape[1]

    def body(c_ref, d_ref, o_ref):
        o_ref[...] = _dot(_silu(c_ref[...]), d_ref[...], "tn")

    return pl.pallas_call(
        body, name="ada_bwd", grid=(L,), out_shape=jax.ShapeDtypeStruct((L, D, n), F32),
        in_specs=[pl.BlockSpec((16, D), lambda l: (0, 0)), pl.BlockSpec((None, 16, n), lambda l: (l, 0, 0))],
        out_specs=pl.BlockSpec((None, D, n), lambda l: (l, 0, 0)), compiler_params=_cp("parallel"))(c16, dmod16)


def _row_spec(tr, n):
    return pl.BlockSpec((tr, n), lambda i: (i, 0))


def _vec_spec(n):
    return pl.BlockSpec((1, n), lambda i: (0, 0))


def _rmsmod_fwd(name, x, g, sc, sh, after, tr=256):
    S, D = x.shape

    def body(x_ref, g_ref, sc_ref, sh_ref, after_ref, h_ref):
        xv = x_ref[...]
        rstd = lax.rsqrt(jnp.mean(xv * xv, axis=-1, keepdims=True) + EPS)
        y = xv * rstd * g_ref[...]
        h_ref[...] = (y * (1.0 + sc_ref[...]) + sh_ref[...]).astype(h_ref.dtype)

    return pl.pallas_call(
        body, name=name, grid=(S // tr,), out_shape=jax.ShapeDtypeStruct((S, D), _MXU_DTYPE),
        in_specs=[_row_spec(tr, D), _vec_spec(D), _vec_spec(D), _vec_spec(D), ANY], out_specs=_row_spec(tr, D),
        compiler_params=_cp("parallel"))(x, g, sc, sh, after)


def _acc_rows(ref, val, first):
    s = jnp.sum(val, axis=0, keepdims=True)

    @pl.when(first)
    def _():
        ref[...] = s

    @pl.when(jnp.logical_not(first))
    def _():
        ref[...] += s


def _gate_bwd_tail(dx, y_ref, gate_ref, dy_ref, dgate_ref, first):
    dy_ref[...] = (gate_ref[...] * dx).astype(dy_ref.dtype)
    _acc_rows(dgate_ref, dx * y_ref[...].astype(F32), first)


def _rmsmod_bwd(name, x, g, sc, dh, dres, after, y=None, gate=None, tr=256):
    S, D = x.shape
    tail = y is not None

    def body(x_ref, g_ref, sc_ref, dh_ref, dres_ref, after_ref, *rest):
        (y_ref, gate_ref), rest = (rest[:2], rest[2:]) if tail else ((None, None), rest)
        dx_ref, dg_ref, dsc_ref, dsh_ref = rest[:4]
        first = pl.program_id(0) == 0
        xv, dh_v, gv = x_ref[...], dh_ref[...], g_ref[...]
        rstd = lax.rsqrt(jnp.mean(xv * xv, axis=-1, keepdims=True) + EPS)
        xhat = xv * rstd
        _acc_rows(dsh_ref, dh_v, first)
        _acc_rows(dsc_ref, dh_v * (xhat * gv), first)
        dyg = dh_v * (1.0 + sc_ref[...])
        _acc_rows(dg_ref, dyg * xhat, first)
        dxhat = dyg * gv
        dx = dres_ref[...] + rstd * (dxhat - xhat * jnp.mean(dxhat * xhat, axis=-1, keepdims=True))
        dx_ref[...] = dx
        if tail:
            _gate_bwd_tail(dx, y_ref, gate_ref, rest[4], rest[5], first)

    vec = jax.ShapeDtypeStruct((1, D), F32)
    return pl.pallas_call(
        body, name=name, grid=(S // tr,),
        out_shape=(jax.ShapeDtypeStruct((S, D), F32), vec, vec, vec) + ((jax.ShapeDtypeStruct((S, D), _MXU_DTYPE), vec) if tail else ()),
        in_specs=[_row_spec(tr, D), _vec_spec(D), _vec_spec(D), _row_spec(tr, D), _row_spec(tr, D), ANY]
        + ([_row_spec(tr, D), _vec_spec(D)] if tail else []),
        out_specs=(_row_spec(tr, D), _vec_spec(D), _vec_spec(D), _vec_spec(D)) + ((_row_spec(tr, D), _vec_spec(D)) if tail else ()),
        compiler_params=_cp("arbitrary"))(x, g, sc, dh, dres, after, *((y, gate) if tail else ()))


def _loss_head(x, g, target, y, gate, tr=256):
    S, D = x.shape

    def body(x_ref, g_ref, t_ref, y_ref, gate_ref, loss_ref, dx_ref, dg_ref, dy_ref, dgate_ref):
        first = pl.program_id(0) == 0
        xv, gv = x_ref[...], g_ref[...]
        rstd = lax.rsqrt(jnp.mean(xv * xv, axis=-1, keepdims=True) + EPS)
        xhat = xv * rstd
        err = xhat * gv - t_ref[...]
        part = 0.5 * jnp.sum(jnp.mean(err * err, axis=-1, keepdims=True), axis=0, keepdims=True)

        @pl.when(first)
        def _():
            loss_ref[...] = part

        @pl.when(jnp.logical_not(first))
        def _():
            loss_ref[...] += part

        dout = err * (1.0 / D)
        _acc_rows(dg_ref, dout * xhat, first)
        dxhat = dout * gv
        dx = rstd * (dxhat - xhat * jnp.mean(dxhat * xhat, axis=-1, keepdims=True))
        dx_ref[...] = dx
        _gate_bwd_tail(dx, y_ref, gate_ref, dy_ref, dgate_ref, first)

    vec = jax.ShapeDtypeStruct((1, D), F32)
    return pl.pallas_call(
        body, name="loss_head", grid=(S // tr,),
        out_shape=(jax.ShapeDtypeStruct((1, 1), F32), jax.ShapeDtypeStruct((S, D), F32), vec,
                   jax.ShapeDtypeStruct((S, D), _MXU_DTYPE), vec),
        in_specs=[_row_spec(tr, D), _vec_spec(D), _row_spec(tr, D), _row_spec(tr, D), _vec_spec(D)],
        out_specs=(pl.BlockSpec((1, 1), lambda i: (0, 0)), _row_spec(tr, D), _vec_spec(D), _row_spec(tr, D), _vec_spec(D)),
        compiler_params=_cp("arbitrary"))(x, g, target, y, gate)


def _gate_bwd(name, dx, y, gate, tr=256):
    S, D = dx.shape

    def body(dx_ref, y_ref, g_ref, dy_ref, dg_ref):
        dxv = dx_ref[...]
        dy_ref[...] = (g_ref[...] * dxv).astype(dy_ref.dtype)
        _acc_rows(dg_ref, dxv * y_ref[...], pl.program_id(0) == 0)

    return pl.pallas_call(
        body, name=name, grid=(S // tr,),
        out_shape=(jax.ShapeDtypeStruct((S, D), _MXU_DTYPE), jax.ShapeDtypeStruct((1, D), F32)),
        in_specs=[_row_spec(tr, D), _row_spec(tr, D), _vec_spec(D)], out_specs=(_row_spec(tr, D), _vec_spec(D)),
        compiler_params=_cp("arbitrary"))(dx, y, gate)


def _shift_down(v, k):
    t = lax.broadcasted_iota(jnp.int32, v.shape, 0)
    return jnp.where(t >= k, pltpu.roll(v, k, axis=0), 0.0)


def _shift_up(v, k):
    n = v.shape[0]
    t = lax.broadcasted_iota(jnp.int32, v.shape, 0)
    return jnp.where(t < n - k, pltpu.roll(v, n - k, axis=0), 0.0)


def _window_sum(p, w, shift):
    s, k = p, 1
    while k < w:
        s = s + shift(s, k)
        k *= 2
    return s


def _pool_count(shape, w):
    t = lax.broadcasted_iota(jnp.int32, shape, 0)
    return jnp.minimum(t + 1, w).astype(F32)


def _ab_specs(S):
    zs = [pl.BlockSpec((None, S, 128), functools.partial(lambda g, q: (2 * q + g // 2, 0, g % 2), q=q)) for q in range(4)]
    return zs


def _ab_mix_fwd(z8, conv_w, mix_w, scale):
    S = z8.shape[1]

    def body(b_ref, c_ref, a_ref, p_ref, w_ref, mix_ref, sc_ref, y_ref):
        g = pl.program_id(0)
        cg = c_ref[...] * a_ref[...]
        w = w_ref[...]
        conv = w[0:1] * _shift_down(cg, 2) + w[1:2] * _shift_down(cg, 1) + w[2:3] * cg
        y_ref[0] = (b_ref[...] * conv).astype(y_ref.dtype)
        for gg, win in enumerate(POOL_WINDOWS):
            @pl.when(g == gg)
            def _(win=win):
                p = p_ref[...]
                pooled = _window_sum(p, win, _shift_down) / _pool_count(p.shape, win) - p
                y_ref[1] = (_dot(pooled, mix_ref[...], "nn") * sc_ref[...]).astype(y_ref.dtype)

    return pl.pallas_call(
        body, name="ab_mix_fwd", grid=(4,), out_shape=jax.ShapeDtypeStruct((2, S, 512), _MXU_DTYPE),
        in_specs=_ab_specs(S) + [pl.BlockSpec((3, 128), lambda g: (0, g)), pl.BlockSpec((None, 128, 128), lambda g: (g, 0, 0)),
                                 pl.BlockSpec((1, 128), lambda g: (0, g))],
        out_specs=pl.BlockSpec((2, S, 128), lambda g: (0, 0, g)), compiler_params=_cp("parallel"))(z8, z8, z8, z8, conv_w, mix_w, scale)


def _ab_mix_bwd(z8, dycat2, conv_w, mix_w, scale, after):
    S = z8.shape[1]

    def body(b_ref, c_ref, a_ref, p_ref, dy_ref, w_ref, mix_ref, sc_ref, after_ref, dz_ref, dw_ref, dmix_ref, dsc_ref):
        g = pl.program_id(0)
        bv, cv, av, w = b_ref[...], c_ref[...], a_ref[...], w_ref[...]
        dya = dy_ref[0]
        cg = cv * av
        cg1, cg2 = _shift_down(cg, 1), _shift_down(cg, 2)
        conv = w[0:1] * cg2 + w[1:2] * cg1 + w[2:3] * cg
        dz_ref[0] = (dya * conv).astype(dz_ref.dtype)
        dconv = dya * bv
        dcg = w[2:3] * dconv + w[1:2] * _shift_up(dconv, 1) + w[0:1] * _shift_up(dconv, 2)
        dz_ref[1] = (dcg * av).astype(dz_ref.dtype)
        dz_ref[2] = (dcg * cv).astype(dz_ref.dtype)
        dw_ref[0:1, :] = jnp.sum(dconv * cg2, axis=0, keepdims=True)
        dw_ref[1:2, :] = jnp.sum(dconv * cg1, axis=0, keepdims=True)
        dw_ref[2:3, :] = jnp.sum(dconv * cg, axis=0, keepdims=True)
        for gg, win in enumerate(POOL_WINDOWS):
            @pl.when(g == gg)
            def _(win=win):
                p, dyb, mix = p_ref[...], dy_ref[1], mix_ref[...]
                cnt = _pool_count(p.shape, win)
                pooled = _window_sum(p, win, _shift_down) / cnt - p
                dsc_ref[...] = jnp.sum(dyb * _dot(pooled, mix, "nn"), axis=0, keepdims=True)
                dmixed = dyb * sc_ref[...]
                dmix_ref[...] = _dot(pooled, dmixed, "tn")
                dpooled = _dot(dmixed, mix, "nt")
                dz_ref[3] = (_window_sum(dpooled / cnt, win, _shift_up) - dpooled).astype(dz_ref.dtype)

    return pl.pallas_call(
        body, name="ab_mix_bwd", grid=(4,),
        out_shape=(jax.ShapeDtypeStruct((4, 2, S, 256), _MXU_DTYPE), jax.ShapeDtypeStruct((3, 512), F32),
                   jax.ShapeDtypeStruct((4, 128, 128), F32), jax.ShapeDtypeStruct((1, 512), F32)),
        in_specs=_ab_specs(S) + [pl.BlockSpec((2, S, 128), lambda g: (0, 0, g)), pl.BlockSpec((3, 128), lambda g: (0, g)),
                                 pl.BlockSpec((None, 128, 128), lambda g: (g, 0, 0)), pl.BlockSpec((1, 128), lambda g: (0, g)), ANY],
        out_specs=(pl.BlockSpec((4, None, S, 128), lambda g: (0, g // 2, 0, g % 2)), pl.BlockSpec((3, 128), lambda g: (0, g)),
                   pl.BlockSpec((None, 128, 128), lambda g: (g, 0, 0)), pl.BlockSpec((1, 128), lambda g: (0, g))),
        compiler_params=_cp("parallel"))(z8, z8, z8, z8, dycat2, conv_w, mix_w, scale, after)


HALO = 16


def _ffn_specs(S, n, tr):
    nb = S // HALO
    tile = pl.BlockSpec((2, None, tr, n), lambda j, i: (0, j, i, 0))
    prev = pl.BlockSpec((2, None, HALO, n), lambda j, i: (0, j, jnp.maximum(i * (tr // HALO) - 1, 0), 0))
    nxt = pl.BlockSpec((2, None, HALO, n), lambda j, i: (0, j, jnp.minimum((i + 1) * (tr // HALO), nb - 1), 0))
    cw = pl.BlockSpec((2, None, 3, n), lambda j, i: (0, j, 0, 0))
    return tile, prev, nxt, cw


def _shifted_rows(ext, lo, rows):
    ext = ext.astype(F32)
    return pltpu.roll(ext, 1, axis=0)[lo:lo + rows], pltpu.roll(ext, 2, axis=0)[lo:lo + rows]


def _ffn_gate_fwd(name, u24, cw24, tr=256):
    _, J, S, n = u24.shape
    tile, prev, _, cw = _ffn_specs(S, n, tr)

    def body(u_ref, up_ref, w_ref, a_ref):
        keep = (pl.program_id(1) > 0).astype(u_ref.dtype)
        z = []
        for h in range(2):
            ext = jnp.concatenate([up_ref[h] * keep, u_ref[h]], axis=0)
            x1, x2 = _shifted_rows(ext, HALO, tr)
            w = w_ref[h]
            z.append(w[0:1] * x2 + w[1:2] * x1 + w[2:3] * u_ref[h].astype(F32))
        a_ref[...] = (_silu(z[0]) * z[1]).astype(a_ref.dtype)

    return pl.pallas_call(
        body, name=name, grid=(J, S // tr), out_shape=jax.ShapeDtypeStruct((J, S, n), _MXU_DTYPE),
        in_specs=[tile, prev, cw], out_specs=pl.BlockSpec((None, tr, n), lambda j, i: (j, i, 0)),
        compiler_params=_cp("parallel", "parallel"))(u24, u24, cw24)


def _ffn_gate_bwd(name, u24, cw24, da4, w_up24, after, tr=256):
    _, J, S, n = u24.shape
    K = w_up24.shape[2]
    nb = S // HALO
    tile = pl.BlockSpec((2, None, tr, n), lambda i, j: (0, j, i, 0))
    prev = pl.BlockSpec((2, None, HALO, n), lambda i, j: (0, j, jnp.maximum(i * (tr // HALO) - 1, 0), 0))
    nxt = pl.BlockSpec((2, None, HALO, n), lambda i, j: (0, j, jnp.minimum((i + 1) * (tr // HALO), nb - 1), 0))
    whole = lambda shape: pl.BlockSpec(shape, lambda i, j: (0,) * len(shape))

    def body(u_ref, up_ref, un_ref, cw_ref, da_ref, dan_ref, wup_ref, after_ref, du_ref, dcw_ref, dh_ref, acc_ref):
        i, j = pl.program_id(0), pl.program_id(1)
        first = i == 0
        keep_prev = (i > 0).astype(u_ref.dtype)
        keep_next = (i < S // tr - 1).astype(F32)
        w = [cw_ref[h, j] for h in range(2)]
        m = tr + HALO
        xs, z = [], []
        for h in range(2):
            ext = jnp.concatenate([up_ref[h] * keep_prev, u_ref[h], un_ref[h]], axis=0)
            x1, x2 = _shifted_rows(ext, HALO, m)
            x0 = ext[HALO:HALO + m].astype(F32)
            xs.append((x2, x1, x0))
            z.append(w[h][0:1] * x2 + w[h][1:2] * x1 + w[h][2:3] * x0)
        zg, zu = z
        da = jnp.concatenate([da_ref[...].astype(F32), dan_ref[...].astype(F32) * keep_next], axis=0)
        sg = jax.nn.sigmoid(zg)
        dz = [da * zu * (sg * (1.0 + zg * (1.0 - sg))), da * (zg * sg)]
        dh = None
        for h in range(2):
            d = dz[h]
            du = w[h][2:3] * d[:tr] + w[h][1:2] * pltpu.roll(d, m - 1, axis=0)[:tr] + w[h][0:1] * pltpu.roll(d, m - 2, axis=0)[:tr]
            du = du.astype(du_ref.dtype)
            du_ref[h] = du
            part = _dot(du, wup_ref[h, j], "nt")
            dh = part if dh is None else dh + part
            dt = d[:tr]
            parts = [jnp.sum(dt * xk[:tr], axis=0, keepdims=True) for xk in xs[h]]
            for k in range(3):
                @pl.when(first)
                def _(k=k, h=h):
                    dcw_ref[h, j, k:k + 1, :] = parts[k]

                @pl.when(jnp.logical_not(first))
                def _(k=k, h=h):
                    dcw_ref[h, j, k:k + 1, :] += parts[k]

        @pl.when(j == 0)
        def _():
            acc_ref[...] = dh

        @pl.when(j > 0)
        def _():
            acc_ref[...] += dh

        @pl.when(j == J - 1)
        def _():
            dh_ref[...] = acc_ref[...]

    da_tile = pl.BlockSpec((None, tr, n), lambda i, j: (j, i, 0))
    da_next = pl.BlockSpec((None, HALO, n), lambda i, j: (j, jnp.minimum((i + 1) * (tr // HALO), nb - 1), 0))
    return pl.pallas_call(
        body, name=name, grid=(S // tr, J),
        out_shape=(jax.ShapeDtypeStruct((2, J, S, n), _MXU_DTYPE), jax.ShapeDtypeStruct((2, J, 3, n), F32),
                   jax.ShapeDtypeStruct((S, K), F32)),
        in_specs=[tile, prev, nxt, whole((2, J, 3, n)), da_tile, da_next, whole((2, J, K, n)), ANY],
        out_specs=(tile, whole((2, J, 3, n)), pl.BlockSpec((tr, K), lambda i, j: (i, 0))),
        scratch_shapes=[pltpu.VMEM((tr, K), F32)],
        compiler_params=_cp("arbitrary", "arbitrary"))(u24, u24, u24, cw24, da4, da4, w_up24, after)


def _rms_rows(v, g):
    rstd = lax.rsqrt(jnp.mean(v * v, axis=-1, keepdims=True) + EPS)
    return v * rstd * g


def _rms_rows_bwd(v, g, dy):
    rstd = lax.rsqrt(jnp.mean(v * v, axis=-1, keepdims=True) + EPS)
    vhat = v * rstd
    dvhat = dy * g
    return rstd * (dvhat - vhat * jnp.mean(dvhat * vhat, axis=-1, keepdims=True)), dy * vhat


def _rope(v, cos, sa, sb):
    return v * cos + pltpu.roll(v, 112, axis=1) * sa + pltpu.roll(v, 16, axis=1) * sb


def _rope_t(d, cos, sa, sb):
    return d * cos + pltpu.roll(d * sa, 16, axis=1) + pltpu.roll(d * sb, 112, axis=1)


def _qkv_rope_fwd(z, qg, kvg, w_uq_t, w_kv, cosq, cosk, sa, sb, tr=256):
    S = z.shape[0]

    def body(ql_ref, kvl_ref, kpe_ref, qg_ref, kvg_ref, wq_ref, wkv_ref, cq_ref, ck_ref, sa_ref, sb_ref,
             qn_ref, kvn_ref, qo_ref, ko_ref, vo_ref):
        cq, ck, sa_v, sb_v = cq_ref[...], ck_ref[...], sa_ref[...], sb_ref[...]
        qn = _rms_rows(ql_ref[...], qg_ref[...]).astype(qn_ref.dtype)
        kvn = _rms_rows(kvl_ref[...], kvg_ref[...]).astype(kvn_ref.dtype)
        qn_ref[...] = qn
        kvn_ref[...] = kvn
        q = _dot(qn, wq_ref[...], "nt")
        kv = _dot(kvn, wkv_ref[...], "nn")
        kpe = _rope(kpe_ref[...], ck, sa_v, sb_v)
        for h in range(8):
            cols = slice(128 * h, 128 * h + 128)
            qo_ref[:, cols] = _rope(q[:, cols], cq, sa_v, sb_v).astype(qo_ref.dtype)
            ko_ref[:, cols] = (kv[:, cols] + kpe).astype(ko_ref.dtype)
        vo_ref[...] = kv[:, 1024:1536].astype(vo_ref.dtype)

    tab = _row_spec(tr, 128)
    whole = lambda a: pl.BlockSpec(a.shape, lambda i: (0, 0))
    return pl.pallas_call(
        body, name="qkv_rope_fwd", grid=(S // tr,),
        out_shape=(jax.ShapeDtypeStruct((S, 256), _MXU_DTYPE), jax.ShapeDtypeStruct((S, 128), _MXU_DTYPE),
                   jax.ShapeDtypeStruct((S, 1024), _MXU_DTYPE), jax.ShapeDtypeStruct((S, 1024), _MXU_DTYPE),
                   jax.ShapeDtypeStruct((S, 512), _MXU_DTYPE)),
        in_specs=[pl.BlockSpec((tr, 256), lambda i: (i, 0)), pl.BlockSpec((tr, 128), lambda i: (i, 2)),
                  pl.BlockSpec((tr, 128), lambda i: (i, 3)), _vec_spec(256), _vec_spec(128), whole(w_uq_t), whole(w_kv),
                  tab, tab, tab, tab],
        out_specs=(_row_spec(tr, 256), _row_spec(tr, 128), _row_spec(tr, 1024), _row_spec(tr, 1024), _row_spec(tr, 512)),
        compiler_params=_cp("parallel"))(z, z, z, qg, kvg, w_uq_t, w_kv, cosq, cosk, sa, sb)


def _attn_bwd_prep(o, dycat2, tr=256):
    S = o.shape[0]

    def body(o_ref, do_ref, delta_ref, doa_ref, dob_ref):
        do = do_ref[...]
        prod = do * o_ref[...]
        lane = lax.broadcasted_iota(jnp.int32, do.shape, 1)
        for p in range(4):
            cols = slice(128 * p, 128 * p + 128)
            first = lane[:, cols] < 128 * p + 64
            da = jnp.sum(jnp.where(first, prod[:, cols], 0.0), axis=-1, keepdims=True)
            db = jnp.sum(jnp.where(first, 0.0, prod[:, cols]), axis=-1, keepdims=True)
            delta_ref[p] = jnp.where(first, da, db)
            doa_ref[p] = jnp.where(first, do[:, cols], 0.0).astype(doa_ref.dtype)
            dob_ref[p] = jnp.where(first, 0.0, do[:, cols]).astype(dob_ref.dtype)

    pair = pl.BlockSpec((4, tr, 128), lambda i: (0, i, 0))
    return pl.pallas_call(
        body, name="attn_bwd_prep", grid=(S // tr,),
        out_shape=(jax.ShapeDtypeStruct((4, S, 128), F32), jax.ShapeDtypeStruct((4, S, 128), _MXU_DTYPE),
                   jax.ShapeDtypeStruct((4, S, 128), _MXU_DTYPE)),
        in_specs=[_row_spec(tr, 512), pl.BlockSpec((None, tr, 512), lambda i: (0, i, 0))],
        out_specs=(pair, pair, pair), compiler_params=_cp("parallel"))(o, dycat2)


def _qkv_rope_bwd(z, qg, kvg, dq, dk, dv, duv, w_uq_t, w_kv, cosq, cosk, sa, sb, tr=256):
    S = z.shape[0]

    def body(ql_ref, kvl_ref, qg_ref, kvg_ref, dq_ref, dk_ref, dv_ref, duv_ref, wq_ref, wkv_ref, cq_ref, ck_ref, sa_ref, sb_ref,
             dqo_ref, dkv_ref, dz_ref, dqg_ref, dkvg_ref):
        first = pl.program_id(0) == 0
        cq, ck, sa_v, sb_v = cq_ref[...], ck_ref[...], sa_ref[...], sb_ref[...]
        tot = jnp.zeros((tr, 128), F32)
        for h in range(8):
            cols = slice(128 * h, 128 * h + 128)
            dqo_ref[:, cols] = _rope_t(dq_ref[:, cols], cq, sa_v, sb_v).astype(dqo_ref.dtype)
            dkh = dk_ref[:, cols]
            tot = tot + dkh
            dkv_ref[:, cols] = dkh.astype(dkv_ref.dtype)
        dkv_ref[:, 1024:1536] = dv_ref[...].astype(dkv_ref.dtype)
        dqn = _dot(dqo_ref[...], wq_ref[...], "nn")
        dkvn = _dot(dkv_ref[...], wkv_ref[...], "nt")
        dql, dqg = _rms_rows_bwd(ql_ref[...], qg_ref[...], dqn)
        dkvl, dkvg = _rms_rows_bwd(kvl_ref[...], kvg_ref[...], dkvn)
        _acc_rows(dqg_ref, dqg, first)
        _acc_rows(dkvg_ref, dkvg, first)
        dz_ref[:, 0:256] = dql.astype(dz_ref.dtype)
        dz_ref[:, 256:384] = dkvl.astype(dz_ref.dtype)
        dz_ref[:, 384:512] = _rope_t(tot, ck, sa_v, sb_v).astype(dz_ref.dtype)
        dz_ref[:, 512:1536] = duv_ref[...].astype(dz_ref.dtype)

    tab = _row_spec(tr, 128)
    whole = lambda a: pl.BlockSpec(a.shape, lambda i: (0, 0))
    return pl.pallas_call(
        body, name="qkv_rope_bwd", grid=(S // tr,),
        out_shape=(jax.ShapeDtypeStruct((S, 1024), _MXU_DTYPE), jax.ShapeDtypeStruct((S, 1536), _MXU_DTYPE),
                   jax.ShapeDtypeStruct((S, 1536), _MXU_DTYPE), jax.ShapeDtypeStruct((1, 256), F32), jax.ShapeDtypeStruct((1, 128), F32)),
        in_specs=[pl.BlockSpec((tr, 256), lambda i: (i, 0)), pl.BlockSpec((tr, 128), lambda i: (i, 2)), _vec_spec(256), _vec_spec(128),
                  _row_spec(tr, 1024), _row_spec(tr, 1024), _row_spec(tr, 512), _row_spec(tr, 1024), whole(w_uq_t), whole(w_kv),
                  tab, tab, tab, tab],
        out_specs=(_row_spec(tr, 1024), _row_spec(tr, 1536), _row_spec(tr, 1536), _vec_spec(256), _vec_spec(128)),
        compiler_params=_cp("arbitrary"))(z, z, qg, kvg, dq, dk, dv, duv, w_uq_t, w_kv, cosq, cosk, sa, sb)


NEG = -1e30


def _attn_fwd(q, k, v, tq=256, tk=256):
    S = q.shape[0]
    assert tq == tk

    def body(q_ref, k_ref, v_ref, o_ref, lse_ref):
        i = pl.program_id(1)
        qs = [q_ref[:, 0:128], q_ref[:, 128:256]]

        def step(kb, carry, diagonal=False):
            start = pl.multiple_of(kb * tk, tk)
            vv = v_ref[pl.ds(start, tk), :]
            out = []
            for h in range(2):
                m, l, acc = carry[3 * h:3 * h + 3]
                s = _dot(qs[h], k_ref[pl.ds(start, tk), 128 * h:128 * h + 128], "nt") * ATTN_SCALE
                if diagonal:
                    s = jnp.where(below, s, NEG)
                m_new = jnp.maximum(m, jnp.max(s, axis=-1, keepdims=True))
                alpha = jnp.exp(m - m_new)
                p = jnp.exp(s - m_new)
                out += [m_new, alpha * l + jnp.sum(p, axis=-1, keepdims=True), alpha * acc + _dot(p, vv, "nn")]
            return tuple(out)

        below = lax.broadcasted_iota(jnp.int32, (tq, tk), 1) <= lax.broadcasted_iota(jnp.int32, (tq, tk), 0)
        init = (jnp.full((tq, 1), NEG, F32), jnp.zeros((tq, 1), F32), jnp.zeros((tq, 128), F32)) * 2
        ma, la, acca, mb, lb, accb = step(i, lax.fori_loop(0, i, step, init), diagonal=True)
        lane = lax.broadcasted_iota(jnp.int32, (tq, 128), 1)
        o_ref[...] = jnp.where(lane < 64, acca / la, accb / lb)
        lse_ref[...] = jnp.where(lane < 64, ma + jnp.log(la), mb + jnp.log(lb))

    return pl.pallas_call(
        body, name="attn_fwd", grid=(4, S // tq),
        out_shape=(jax.ShapeDtypeStruct((S, 512), F32), jax.ShapeDtypeStruct((4, S, 128), F32)),
        in_specs=[pl.BlockSpec((tq, 256), lambda p, i: (i, p)), pl.BlockSpec((S, 256), lambda p, i: (0, p)),
                  pl.BlockSpec((S, 128), lambda p, i: (0, p))],
        out_specs=(pl.BlockSpec((tq, 128), lambda p, i: (i, p)), pl.BlockSpec((None, tq, 128), lambda p, i: (p, i, 0))),
        compiler_params=_cp("parallel", "parallel"))(q, k, v)


def _attn_bwd(q, k, v, lse, delta, doa, dob, tq=256, tk=256):
    S = q.shape[0]
    assert tq == tk

    def body(q_ref, k_ref, v_ref, lse_ref, delta_ref, doa_ref, dob_ref, dq_ref, dk_ref, dv_ref):
        j = pl.program_id(1)

        @pl.when(j == 0)
        def _():
            dq_ref[...] = jnp.zeros_like(dq_ref)

        below = lax.broadcasted_iota(jnp.int32, (tq, tk), 1) <= lax.broadcasted_iota(jnp.int32, (tq, tk), 0)
        ks = [k_ref[:, 0:128], k_ref[:, 128:256]]
        vv = v_ref[...]

        def step(qb, carry, diagonal=False):
            dka, dkb, dvp = carry
            start = pl.multiple_of(qb * tq, tq)
            rows = pl.ds(start, tq)
            lse_v, delta_v = lse_ref[rows, :], delta_ref[rows, :]
            dos = [doa_ref[rows, :], dob_ref[rows, :]]
            dks = [dka, dkb]
            for h in range(2):
                delta = delta_v[:, 64 * h:64 * h + 1]
                do_h = dos[h]
                qh = q_ref[rows, 128 * h:128 * h + 128]
                s = _dot(qh, ks[h], "nt") * ATTN_SCALE
                p = jnp.exp(s - lse_v[:, 64 * h:64 * h + 1])
                if diagonal:
                    p = jnp.where(below, p, 0.0)
                dvp = dvp + _dot(p, do_h, "tn")
                ds = p * (_dot(do_h, vv, "nt") - delta) * ATTN_SCALE
                dq_ref[rows, 128 * h:128 * h + 128] += _dot(ds, ks[h], "nn")
                dks[h] = dks[h] + _dot(ds, qh, "tn")
            return dks[0], dks[1], dvp

        zero = jnp.zeros((tk, 128), F32)
        dka, dkb, dvp = lax.fori_loop(j + 1, S // tq, step, step(j, (zero, zero, zero), diagonal=True))
        dk_ref[:, 0:128] = dka
        dk_ref[:, 128:256] = dkb
        dv_ref[...] = dvp

    return pl.pallas_call(
        body, name="attn_bwd", grid=(4, S // tk),
        out_shape=(jax.ShapeDtypeStruct((S, 1024), F32), jax.ShapeDtypeStruct((S, 1024), F32), jax.ShapeDtypeStruct((S, 512), F32)),
        in_specs=[pl.BlockSpec((S, 256), lambda p, j: (0, p)), pl.BlockSpec((tk, 256), lambda p, j: (j, p)),
                  pl.BlockSpec((tk, 128), lambda p, j: (j, p))] + [pl.BlockSpec((None, S, 128), lambda p, j: (p, 0, 0))] * 4,
        out_specs=(pl.BlockSpec((S, 256), lambda p, j: (0, p)), pl.BlockSpec((tk, 256), lambda p, j: (j, p)),
                   pl.BlockSpec((tk, 128), lambda p, j: (j, p))),
        compiler_params=_cp("parallel", "arbitrary"))(q, k, v, lse, delta, doa, dob)


CHUNK = 128
GELU_C = math.sqrt(2.0 / math.pi)


def _gelu(v):
    t = jnp.tanh(GELU_C * (v + 0.044715 * (v * v * v)))
    return v * (0.5 * (1.0 + t)), t


def _gelu_grad(v, t):
    return 0.5 * (1.0 + t) + v * (0.5 * (1.0 - t * t) * GELU_C * (1.0 + 3.0 * 0.044715 * v * v))


def _tril(w):
    r = lax.broadcasted_iota(jnp.int32, w.shape, 0)
    c = lax.broadcasted_iota(jnp.int32, w.shape, 1)
    return jnp.where(c <= r, w, 0.0)


def _layer_norm(v, g, b):
    xc = v - jnp.mean(v, axis=-1, keepdims=True)
    rstd = lax.rsqrt(jnp.mean(xc * xc, axis=-1, keepdims=True) + EPS)
    xhat = xc * rstd
    return xhat * g + b, xhat, rstd


def _sgu_fwd(z, o, ln_g, ln_b, w_s, b_st, tr=256):
    S = z.shape[0]

    def body(u_ref, v_ref, o_ref, g_ref, b_ref, ws_ref, bs_ref, y_ref):
        gu, _ = _gelu(u_ref[...])
        gv, _ = _gelu(v_ref[...])
        vln, _, _ = _layer_norm(gv, g_ref[...], b_ref[...])
        y_ref[0] = o_ref[...].astype(y_ref.dtype)
        for g in range(4):
            wt = _tril(ws_ref[g])
            cols = slice(128 * g, 128 * g + 128)
            for ch in range(tr // CHUNK):
                rows = slice(CHUNK * ch, CHUNK * ch + CHUNK)
                mixed = _dot(wt, vln[rows, cols], "nn") + bs_ref[:, g:g + 1]
                y_ref[1, rows, cols] = (gu[rows, cols] * mixed).astype(y_ref.dtype)

    return pl.pallas_call(
        body, name="sgu_fwd", grid=(S // tr,), out_shape=jax.ShapeDtypeStruct((2, S, 512), _MXU_DTYPE),
        in_specs=[pl.BlockSpec((tr, 512), lambda i: (i, 1)), pl.BlockSpec((tr, 512), lambda i: (i, 2)), _row_spec(tr, 512),
                  _vec_spec(512), _vec_spec(512), pl.BlockSpec((4, 128, 128), lambda i: (0, 0, 0)), pl.BlockSpec((128, 4), lambda i: (0, 0))],
        out_specs=pl.BlockSpec((2, tr, 512), lambda i: (0, i, 0)), compiler_params=_cp("parallel"))(z, z, o, ln_g, ln_b, w_s, b_st)


def _sgu_bwd(z, dycat2, ln_g, ln_b, w_s, b_st, tr=256):
    S = z.shape[0]

    def body(u_ref, v_ref, dy_ref, g_ref, b_ref, ws_ref, bs_ref, duv_ref, dg_ref, db_ref, dws_ref, dbs_ref):
        first = pl.program_id(0) == 0
        u_pre, v_pre = u_ref[...], v_ref[...]
        gu, tu = _gelu(u_pre)
        gv, tv = _gelu(v_pre)
        gain = g_ref[...]
        vln, xhat, rstd = _layer_norm(gv, gain, b_ref[...])

        @pl.when(first)
        def _():
            dws_ref[...] = jnp.zeros_like(dws_ref)
            dbs_ref[...] = jnp.zeros_like(dbs_ref)

        dvln_cols = []
        for g in range(4):
            wt = _tril(ws_ref[g])
            cols = slice(128 * g, 128 * g + 128)
            dmixed_sum = jnp.zeros((CHUNK, 128), F32)
            dw = jnp.zeros((CHUNK, CHUNK), F32)
            dvln_rows = []
            for ch in range(tr // CHUNK):
                rows = slice(CHUNK * ch, CHUNK * ch + CHUNK)
                vt = vln[rows, cols]
                mixed = _dot(wt, vt, "nn") + bs_ref[:, g:g + 1]
                dyd = dy_ref[rows, cols]
                duv_ref[rows, cols] = (dyd * mixed * _gelu_grad(u_pre[rows, cols], tu[rows, cols])).astype(duv_ref.dtype)
                dmixed = dyd * gu[rows, cols]
                dmixed_sum = dmixed_sum + dmixed
                dw = dw + _dot(dmixed, vt, "nt")
                dvln_rows.append(_dot(wt, dmixed, "tn"))
            dws_ref[g] += _tril(dw)
            dbs_ref[g:g + 1, :] += jnp.sum(dmixed_sum.T, axis=0, keepdims=True)
            dvln_cols.append(jnp.concatenate(dvln_rows, axis=0))
        dvln = jnp.concatenate(dvln_cols, axis=1)
        _acc_rows(dg_ref, dvln * xhat, first)
        _acc_rows(db_ref, dvln, first)
        dxhat = dvln * gain
        dgv = rstd * (dxhat - jnp.mean(dxhat, axis=-1, keepdims=True) - xhat * jnp.mean(dxhat * xhat, axis=-1, keepdims=True))
        duv_ref[:, 512:1024] = (dgv * _gelu_grad(v_pre, tv)).astype(duv_ref.dtype)

    return pl.pallas_call(
        body, name="sgu_bwd", grid=(S // tr,),
        out_shape=(jax.ShapeDtypeStruct((S, 1024), _MXU_DTYPE), jax.ShapeDtypeStruct((1, 512), F32), jax.ShapeDtypeStruct((1, 512), F32),
                   jax.ShapeDtypeStruct((4, 128, 128), F32), jax.ShapeDtypeStruct((4, 128), F32)),
        in_specs=[pl.BlockSpec((tr, 512), lambda i: (i, 1)), pl.BlockSpec((tr, 512), lambda i: (i, 2)),
                  pl.BlockSpec((None, tr, 512), lambda i: (1, i, 0)), _vec_spec(512), _vec_spec(512),
                  pl.BlockSpec((4, 128, 128), lambda i: (0, 0, 0)), pl.BlockSpec((128, 4), lambda i: (0, 0))],
        out_specs=(_row_spec(tr, 1024), _vec_spec(512), _vec_spec(512), pl.BlockSpec((4, 128, 128), lambda i: (0, 0, 0)),
                   pl.BlockSpec((4, 128), lambda i: (0, 0))),
        compiler_params=_cp("arbitrary"))(z, z, dycat2, ln_g, ln_b, w_s, b_st)


def _sum_parts(name, parts, tr=512):
    P, R, C = parts.shape
    tr = _tile(R, tr) if R % 8 == 0 else R

    def body(p_ref, o_ref):
        g = p_ref[0]
        for k in range(1, P):
            g = g + p_ref[k]
        o_ref[...] = g

    return pl.pallas_call(
        body, name=name, grid=(R // tr,), out_shape=jax.ShapeDtypeStruct((R, C), F32),
        in_specs=[pl.BlockSpec((P, tr, C), lambda i: (0, i, 0))], out_specs=_row_spec(tr, C),
        compiler_params=_cp("parallel"))(parts)


def _adamw_math(w, m, v, g):
    c1 = 1.0 / (1.0 - ADAM_B1 ** ADAM_STEP)
    c2 = 1.0 / (1.0 - ADAM_B2 ** ADAM_STEP)
    m2 = ADAM_B1 * m + (1.0 - ADAM_B1) * g
    v2 = ADAM_B2 * v + (1.0 - ADAM_B2) * (g * g)
    return -ADAM_LR * ((m2 * c1) / (jnp.sqrt(v2 * c2) + ADAM_EPS) + ADAM_WD * w), m2, v2


def _adamw_small(name, params, parts):
    n = len(params)

    def body(*refs):
        ins, outs = refs[:4 * n], refs[4 * n:]
        for i in range(n):
            w_ref, m_ref, v_ref, p_ref = ins[4 * i:4 * i + 4]
            g = p_ref[0].astype(F32)
            for k in range(1, N_DEV):
                g = g + p_ref[k].astype(F32)
            delta, m2, v2 = _adamw_math(w_ref[...], m_ref[...], v_ref[...], g)
            outs[4 * i][...] = g
            outs[4 * i + 1][...] = delta
            outs[4 * i + 2][...] = m2
            outs[4 * i + 3][...] = v2

    flat = [a for (w, m, v), p in zip(params, parts) for a in (w, m, v, p)]
    out = pl.pallas_call(
        body, name=name, out_shape=[jax.ShapeDtypeStruct(w.shape, F32) for (w, _, _) in params for _ in range(4)],
        compiler_params=pltpu.CompilerParams(vmem_limit_bytes=_VMEM_LIMIT))(*flat)
    return [out[4 * i:4 * i + 4] for i in range(n)]


ADAMW_BLOCK_BYTES = 36 * 2 ** 20


def _adamw(name, w, m, v, parts):
    L, R, C = w.shape
    P = parts[0].shape[0]
    row_bytes = 2 * C * (7 * 4 + P * parts[0].dtype.itemsize)
    tr = R
    if R * row_bytes > ADAMW_BLOCK_BYTES:
        tr = next(t for t in (1024, 512, 256, 128, 64, 32, 16) if R % t == 0 and t * row_bytes <= ADAMW_BLOCK_BYTES)
    nr = R // tr
    c1 = 1.0 / (1.0 - ADAM_B1 ** ADAM_STEP)
    c2 = 1.0 / (1.0 - ADAM_B2 ** ADAM_STEP)

    def body(w_ref, m_ref, v_ref, *rest):
        p_refs, (g_ref, d_ref, mo_ref, vo_ref) = rest[:L], rest[L:]
        for ll in range(L):
            @pl.when(pl.program_id(0) == ll)
            def _(p_ref=p_refs[ll]):
                g = p_ref[0].astype(F32)
                for k in range(1, P):
                    g = g + p_ref[k].astype(F32)
                m2 = ADAM_B1 * m_ref[...] + (1.0 - ADAM_B1) * g
                v2 = ADAM_B2 * v_ref[...] + (1.0 - ADAM_B2) * (g * g)
                g_ref[...] = g
                mo_ref[...] = m2
                vo_ref[...] = v2
                d_ref[...] = -ADAM_LR * ((m2 * c1) / (jnp.sqrt(v2 * c2) + ADAM_EPS) + ADAM_WD * w_ref[...])

    def part_spec(ll):
        return pl.BlockSpec((P, tr, C), lambda l, i: (0, jnp.where(l == ll, i, jnp.where(l < ll, 0, nr - 1)), 0))

    full = pl.BlockSpec((None, tr, C), lambda l, i: (l, i, 0))
    sds = jax.ShapeDtypeStruct((L, R, C), F32)
    return pl.pallas_call(
        body, name=name, grid=(L, nr), out_shape=(sds, sds, sds, sds),
        in_specs=[full] * 3 + [part_spec(ll) for ll in range(L)],
        out_specs=(full,) * 4, compiler_params=_cp("arbitrary", "arbitrary"))(w, m, v, *parts)


def _rope_tables(positions):
    half = 16
    inv_freq = 10000.0 ** (-jnp.arange(half, dtype=F32) / half)
    ang = positions.astype(F32)[:, None] * inv_freq
    cos, sin = jnp.cos(ang), jnp.sin(ang)
    S = positions.shape[0]
    z16, z32, z64 = jnp.zeros((S, 16), F32), jnp.zeros((S, 32), F32), jnp.zeros((S, 64), F32)
    cosk = jnp.concatenate([z64, cos, cos, z32], axis=1)
    cosq = jnp.concatenate([jnp.ones((S, 64), F32), cos, cos, z32], axis=1)
    sa = jnp.concatenate([z64, -sin, z16, z32], axis=1)
    sb = jnp.concatenate([z64, z16, sin, z32], axis=1)
    return cosq, cosk, sa, sb


def _ffn_fwd(l, x, mod, n2g, get_w_up8, cw24, get_w_down4):
    sh, sc, gate = mod
    h = _rmsmod_fwd(f"ffn{l}_norm", x, n2g, sc, sh, n2g)
    w_up8 = get_w_up8(h)
    u8 = _mm_cols(f"ffn{l}_up", h, w_up8, out_dtype=ACT_DTYPE, tm=2048)
    S, n = u8.shape[1], u8.shape[2]
    u24 = u8.reshape(2, 4, S, n)
    a4 = _ffn_gate_fwd(f"ffn{l}_gate", u24, cw24)
    w_down4 = get_w_down4(a4)
    f, x_new = _mm_rows_resid(f"ffn{l}_down", a4, w_down4, x, gate)
    return x_new, (x, h, u24, a4, f), w_up8, w_down4


def _ffn_bwd(l, dx, df, dgate, saved, mod, n2g, w_up8, cw24, w_down4, me, y_prev, gate_prev):
    sh, sc, gate = mod
    x, h, u24, a4, f = saved
    da4 = _mm_rows_dx(f"ffn{l}_down_dx", df, w_down4, out_dtype=ACT_DTYPE, tm=2048)
    dw_down4 = _mm_rows_dw(f"ffn{l}_down_dw", a4, df, out_dtype=WIRE_DTYPE)
    sent_down, token = _exchange_start(f"scatter_ffn{l}_down", [dw_down4.reshape(8, 352, dw_down4.shape[2])], True, dgate, me)
    du24, dcw24, dh = _ffn_gate_bwd(f"ffn{l}_act_bwd", u24, cw24, da4, w_up8.reshape((2, 4) + w_up8.shape[1:]), token)
    du8 = du24.reshape((8,) + du24.shape[2:])
    dw_up8t = _mm_cols_dwt(f"ffn{l}_up_dw", h, du8, out_dtype=WIRE_DTYPE, tk=1024)
    sent_up, token = _exchange_start(f"scatter_ffn{l}_up", [dw_up8t], True, dcw24, me)
    dx_new, dn2g, dsc, dsh, dy_prev, dgate_prev = _rmsmod_bwd(f"ffn{l}_norm_bwd", x, n2g, sc, dh, dx, token, y_prev, gate_prev)
    return dx_new, dict(sent_up=sent_up, sent_down=sent_down, cw24=dcw24, n2g=dn2g, mod=(dsh, dsc, dgate)), dy_prev, dgate_prev


def kernel(x, c, positions, ada_w, ada_b, norm1_g, norm2_g, ab_w_in, a_conv_w, b_mix_w, b_scale, ab_w_out, cd_w_in, c_q_norm_g, c_w_uq, c_kv_norm_g, c_w_ukv, d_ln_g, d_ln_b, d_w_s, d_b_s, cd_w_out, ffn_w_up, ffn_conv_w, ffn_w_down, final_norm_g, loss_target, m_ada_w, m_ada_b, m_norm1_g, m_norm2_g, m_ab_w_in, m_a_conv_w, m_b_mix_w, m_b_scale, m_ab_w_out, m_cd_w_in, m_c_q_norm_g, m_c_w_uq, m_c_kv_norm_g, m_c_w_ukv, m_d_ln_g, m_d_ln_b, m_d_w_s, m_d_b_s, m_cd_w_out, m_ffn_w_up, m_ffn_conv_w, m_ffn_w_down, m_final_norm_g, v_ada_w, v_ada_b, v_norm1_g, v_norm2_g, v_ab_w_in, v_a_conv_w, v_b_mix_w, v_b_scale, v_ab_w_out, v_cd_w_in, v_c_q_norm_g, v_c_w_uq, v_c_kv_norm_g, v_c_w_ukv, v_d_ln_g, v_d_ln_b, v_d_w_s, v_d_b_s, v_cd_w_out, v_ffn_w_up, v_ffn_conv_w, v_ffn_w_down, v_final_norm_g):
    S, D = x.shape[1], x.shape[2]
    me = 4 * lax.axis_index("x") + 2 * lax.axis_index("y") + lax.axis_index("c")
    x0, target = x[0], loss_target[0]
    W = _MXU_DTYPE

    small_shapes = [(1024,), (3, 64), (32,), (64,), (64,), (2, 3, 704)]
    (g0,) = _exchange("gather_small", [[_pack([c, a_conv_w, c_q_norm_g, d_ln_g, d_ln_b, ffn_conv_w])]], scatter=False)
    c_all, aconv_s, qg_s, lng_s, lnb_s, fcw_s = _unpack(g0[:, 0], small_shapes, lead=(N_DEV,))
    conv_w = aconv_s.transpose(1, 0, 2).reshape(3, 512)
    qg, ln_g, ln_b = qg_s.reshape(1, 256), lng_s.reshape(1, 512), lnb_s.reshape(1, 512)
    cw24 = [fcw_s[:, l].reshape(2, 4, 3, 704) for l in range(2)]
    c16 = jnp.pad(c_all, ((0, 16 - N_DEV), (0, 0)))

    mod_cols = _ada_fwd(c16, ada_w)
    (g1,) = _exchange("gather_mod", [[_pack([mod_cols])]], scatter=False)
    mod_all = _unpack(g1[:, 0], [(2, 16, 768)], lead=(N_DEV,))[0]
    mod_mine = lax.dynamic_index_in_dim(mod_all, me, axis=2, keepdims=False)
    mod = mod_mine.transpose(1, 0, 2).reshape(2, 6 * D) + ada_b
    mods = [[mod[l, k * D:(k + 1) * D].reshape(1, D) for k in range(6)] for l in range(2)]

    gw_ab, token = _hier_gather_start("gather_w_ab", [ab_w_in[0].astype(W), ab_w_out[0].astype(W)], mod, me)
    gw_up0, token = _hier_gather_start("gather_w_ffn0_up", [ffn_w_up[0].astype(W)], token, me)
    gw_rest, started = _exchange_start("gather_w_rest", [
        ffn_w_down[0].astype(W), cd_w_in[0].T.astype(W), c_w_uq[0].T.astype(W), c_w_ukv[0].astype(W), cd_w_out[0].astype(W),
        ffn_w_up[1].astype(W), ffn_w_down[1].astype(W)], False, token, me)

    cosq, cosk, sa, sb = _rope_tables(positions[0])
    n1g = [norm1_g[l].reshape(1, D) for l in range(2)]
    n2g = [norm2_g[l].reshape(1, D) for l in range(2)]
    mix_w, scale = b_mix_w[0], b_scale
    kvg = c_kv_norm_g
    w_s, b_st = d_w_s[0], d_b_s[0].T

    sh1, sc1, g1m = mods[0][:3]
    h_ab = _rmsmod_fwd("ab_norm", x0, n1g[0], sc1, sh1, started)
    w_abin8, w_about = _hier_gather_wait("wait_w_ab", _hier_gather_forward("forward_w_ab", gw_ab, h_ab), h_ab)
    w_about2 = w_about.reshape(2, 512, D)
    z8 = _mm_cols("ab_in", h_ab, w_abin8, tm=2048)
    ycat_ab = _ab_mix_fwd(z8, conv_w, mix_w, scale)
    y_ab, x1 = _mm_rows_resid("ab_out", ycat_ab, w_about2, x0, g1m)
    w_up8, w_down4 = [None, None], [None, None]
    gw_up0 = _hier_gather_forward("forward_w_ffn0_up", gw_up0, x1)
    x2, ffn0_saved, w_up8[0], w_down4[0] = _ffn_fwd(
        0, x1, mods[0][3:], n2g[0], lambda after: _hier_gather_wait("wait_w_ffn0_up", gw_up0, after)[0], cw24[0],
        lambda after: _exchange_wait("wait_w_ffn0_down", gw_rest, after, [0])[0].reshape(4, 704, D))

    w_cdin, w_uq, w_ukv, w_cdout = _exchange_wait("wait_w_cd", gw_rest, x2, [1, 2, 3, 4])
    w_cdout2 = w_cdout.reshape(2, 512, D)
    w_cd_t = w_cdin.reshape(1440, D)
    zr = lambda n: jnp.zeros((n, D), W)
    w_cd_pad = jnp.concatenate([w_cd_t[:384], zr(64), w_cd_t[384:416], zr(32), w_cd_t[416:]], axis=0)
    w_uq_pad = jnp.pad(w_uq, ((0, 0), (0, 32), (0, 0))).reshape(1024, 256)
    w_ukv_h = w_ukv.transpose(1, 0, 2)
    w_k_pad = jnp.pad(w_ukv_h[:, :, :64], ((0, 0), (0, 0), (0, 64))).reshape(128, 1024)
    w_kv_pad = jnp.concatenate([w_k_pad, w_ukv_h[:, :, 64:].reshape(128, 512)], axis=1)

    sh1, sc1, g1c = mods[1][:3]
    h_cd = _rmsmod_fwd("cd_norm", x2, n1g[1], sc1, sh1, n1g[1])
    z_cd = _mm_nt("cd_in", h_cd, w_cd_pad)
    qn, kvn, q_r, k_r, v_r = _qkv_rope_fwd(z_cd, qg, kvg, w_uq_pad, w_kv_pad, cosq, cosk, sa, sb)
    o, lse = _attn_fwd(q_r, k_r, v_r)
    ycat_cd = _sgu_fwd(z_cd, o, ln_g, ln_b, w_s, b_st)
    y_cd, x3 = _mm_rows_resid("cd_out", ycat_cd, w_cdout2, x2, g1c)
    x4, ffn1_saved, w_up8[1], w_down4[1] = _ffn_fwd(
        1, x3, mods[1][3:], n2g[1], lambda after: _exchange_wait("wait_w_ffn1_up", gw_rest, after, [5])[0], cw24[1],
        lambda after: _exchange_wait("wait_w_ffn1_down", gw_rest, after, [6])[0].reshape(4, 704, D))

    loss_local, dx4, dfg, df1, dgate1 = _loss_head(x4, final_norm_g.reshape(1, D), target, ffn1_saved[4], mods[1][5])

    dx3, gf1, dy, dg1c = _ffn_bwd(1, dx4, df1, dgate1, ffn1_saved, mods[1][3:], n2g[1], w_up8[1], cw24[1], w_down4[1], me, y_cd, g1c)

    dycat = _mm_rows_dx("cd_out_dx", dy, w_cdout2)
    dw_cdout = _mm_rows_dw("cd_out_dw", ycat_cd, dy, out_dtype=WIRE_DTYPE)
    duv, dln_g, dln_b, dws, dbs = _sgu_bwd(z_cd, dycat, ln_g, ln_b, w_s, b_st)
    dq_r, dk_r, dv_r = _attn_bwd(q_r, k_r, v_r, lse, *_attn_bwd_prep(o, dycat))
    dqraw, dkvall, dz_cd, dqg, dkvg = _qkv_rope_bwd(z_cd, qg, kvg, dq_r, dk_r, dv_r, duv, w_uq_pad, w_kv_pad, cosq, cosk, sa, sb)
    dw_uq_pad = _mm_tn("cd_uq_dw", dqraw, qn, tn=256)
    dw_kv_pad = _mm_tn("cd_ukv_dw", kvn, dkvall, tm=128)
    dh_cd = _mm_nn("cd_in_dx", dz_cd, w_cd_pad)
    dw_cd_pad = _mm_tn("cd_in_dw", dz_cd, h_cd)
    dw_cd8 = jnp.concatenate([dw_cd_pad[:384], dw_cd_pad[448:480], dw_cd_pad[512:]], axis=0).astype(WIRE_DTYPE).reshape(8, 180, D)
    dw_uq8 = dw_uq_pad.reshape(8, 128, 256)[:, :96].astype(WIRE_DTYPE)
    dw_ukv8 = jnp.concatenate([dw_kv_pad[:, :1024].reshape(128, 8, 128)[:, :, :64], dw_kv_pad[:, 1024:].reshape(128, 8, 64)],
                              axis=2).transpose(1, 0, 2).astype(WIRE_DTYPE)
    sent_cd, token = _exchange_start("scatter_cd", [dw_cd8, dw_uq8, dw_ukv8, dw_cdout.reshape(8, 128, D)], True, dqg, me)
    early_names = ["c_kv_norm_g", "d_w_s", "d_b_s", "final_norm_g", "c_q_norm_g", "d_ln_g", "d_ln_b"]
    early_grads = [dkvg, dws.reshape(512, 128).astype(WIRE_DTYPE), dbs, dfg, dqg.reshape(8, 1, 32), dln_g.reshape(8, 1, 64),
                   dln_b.reshape(8, 1, 64)]
    early_sent, token = _exchange_start("gather_small_grads_early", early_grads, [False] * 4 + [True] * 3, token, me)
    dx2, dn1g_cd, dsc1_cd, dsh1_cd, df0, dgate0 = _rmsmod_bwd("cd_norm_bwd", x2, n1g[1], sc1, dh_cd, dx3, token,
                                                              ffn0_saved[4], mods[0][5])

    dx1, gf0, dy, dg1m = _ffn_bwd(0, dx2, df0, dgate0, ffn0_saved, mods[0][3:], n2g[0], w_up8[0], cw24[0], w_down4[0], me, y_ab, g1m)

    dw_about = _mm_rows_dw("ab_out_dw", ycat_ab, dy, out_dtype=WIRE_DTYPE)
    sent_about, token = _exchange_start("scatter_ab_out", [dw_about.reshape(8, 128, D)], True, dg1m, me)
    dycat = _mm_rows_dx("ab_out_dx", dy, w_about2)
    dz8, dconv_w, dmix_w, dscale = _ab_mix_bwd(z8, dycat, conv_w, mix_w, scale, token)
    dz8 = dz8.reshape(8, S, 256)
    dw_abin8 = _mm_cols_dw("ab_in_dw", h_ab, dz8, out_dtype=WIRE_DTYPE, tk=1024)
    sent_abin, token = _exchange_start("scatter_ab_in", [dw_abin8], True, dscale, me)
    dh_ab = _mm_cols_dx("ab_in_dx", dz8, w_abin8)
    dx0, dn1g_ab, dsc1_ab, dsh1_ab = _rmsmod_bwd("ab_norm_bwd", x0, n1g[0], mods[0][1], dh_ab, dx1, token)

    dmod = jnp.stack([jnp.concatenate([dsh1_ab, dsc1_ab, dg1m, *gf0["mod"]], axis=1)[0],
                      jnp.concatenate([dsh1_cd, dsc1_cd, dg1c, *gf1["mod"]], axis=1)[0]])
    late_names = ["ada_b", "norm1_g", "norm2_g", "b_mix_w", "b_scale", "a_conv_w", "ffn_conv_w"]
    late_grads = [dmod, jnp.concatenate([dn1g_ab, dn1g_cd]), jnp.concatenate([gf0["n2g"], gf1["n2g"]]),
                  dmix_w.reshape(512, 128).astype(WIRE_DTYPE), dscale, dconv_w.reshape(3, 8, 64).transpose(1, 0, 2),
                  jnp.stack([gf0["cw24"].reshape(8, 3, 704), gf1["cw24"].reshape(8, 3, 704)], axis=1),
                  jnp.pad(loss_local, ((0, 0), (0, 127)))]
    small_view = dict(ada_b=(2, 6 * D), norm1_g=(2, D), norm2_g=(2, D), b_mix_w=(512, 128), b_scale=(1, 512), c_kv_norm_g=(1, 128),
                      d_w_s=(512, 128), d_b_s=(4, 128), final_norm_g=(1, D),
                      a_conv_w=(3, 64), c_q_norm_g=(1, 32), d_ln_g=(1, 64), d_ln_b=(1, 64), ffn_conv_w=(2, 3, 704))
    late_sent, token = _exchange_start("gather_small_grads_late", late_grads, [False] * 5 + [True] * 2 + [False], dx0, me)

    res = {}

    def update(name, w, m, v, parts, shape3d):
        outs = _adamw("adamw_" + name, w.reshape(shape3d), m.reshape(shape3d), v.reshape(shape3d),
                      [p.reshape((p.shape[0],) + shape3d[1:]) for p in parts])
        res[name] = [o_.reshape(w.shape) for o_ in outs]

    p_cdin, p_uq, p_ukv, p_cdout = _exchange_wait("wait_scatter_cd", sent_cd, token)
    swap = lambda a: jnp.swapaxes(a, 1, 2)
    update("cd_w_in", swap(cd_w_in), swap(m_cd_w_in), swap(v_cd_w_in), [p_cdin], (1, 180, D))
    update("c_w_uq", swap(c_w_uq), swap(m_c_w_uq), swap(v_c_w_uq), [p_uq], (1, 96, 256))
    for name in ("cd_w_in", "c_w_uq"):
        res[name] = [swap(o_) for o_ in res[name]]
    update("c_w_ukv", c_w_ukv, m_c_w_ukv, v_c_w_ukv, [p_ukv], (1, 128, 128))
    update("cd_w_out", cd_w_out, m_cd_w_out, v_cd_w_out, [p_cdout], (1, 128, D))
    (p_dn1,) = _exchange_wait("wait_scatter_ffn1_down", gf1["sent_down"], token)
    (p_dn0,) = _exchange_wait("wait_scatter_ffn0_down", gf0["sent_down"], res["cd_w_out"][0])
    update("ffn_w_down", ffn_w_down, m_ffn_w_down, v_ffn_w_down, [p_dn0, p_dn1], (2, 352, D))
    (p_up1,) = _exchange_wait("wait_scatter_ffn1_up", gf1["sent_up"], token)
    (p_up0,) = _exchange_wait("wait_scatter_ffn0_up", gf0["sent_up"], res["ffn_w_down"][0])
    swap = lambda a: jnp.swapaxes(a, 1, 2)
    update("ffn_w_up", swap(ffn_w_up), swap(m_ffn_w_up), swap(v_ffn_w_up), [p_up0, p_up1], (2, 704, D))
    up_done = res["ffn_w_up"][0]
    res["ffn_w_up"] = [swap(o_) for o_ in res["ffn_w_up"]]
    (p_about,) = _exchange_wait("wait_scatter_ab_out", sent_about, up_done)
    update("ab_w_out", ab_w_out, m_ab_w_out, v_ab_w_out, [p_about], (1, 128, D))
    (p_abin,) = _exchange_wait("wait_scatter_ab_in", sent_abin, res["ab_w_out"][0])
    update("ab_w_in", ab_w_in, m_ab_w_in, v_ab_w_in, [p_abin], (1, D, 256))

    early_parts = _exchange_wait("wait_small_grads_early", early_sent, res["ab_w_in"][0])
    late_parts = _exchange_wait("wait_small_grads_late", late_sent, res["ab_w_in"][0])
    small_names = early_names + late_names
    small_parts = list(early_parts) + list(late_parts[:7])
    loss = jnp.sum(late_parts[7][:, 0, 0])
    dmod_all = late_parts[0]
    dmod_cols = lax.dynamic_slice_in_dim(dmod_all, me * 768, 768, axis=2).transpose(1, 0, 2)
    g_ada_w = _ada_bwd(c16, jnp.pad(dmod_cols, ((0, 0), (0, 16 - N_DEV), (0, 0))))
    update("ada_w", ada_w, m_ada_w, v_ada_w, [g_ada_w[l][None] for l in range(2)], (2, D, 768))

    small_w = dict(ada_b=(ada_b, m_ada_b, v_ada_b), norm1_g=(norm1_g, m_norm1_g, v_norm1_g), norm2_g=(norm2_g, m_norm2_g, v_norm2_g),
                   b_mix_w=(b_mix_w, m_b_mix_w, v_b_mix_w), b_scale=(b_scale, m_b_scale, v_b_scale),
                   c_kv_norm_g=(c_kv_norm_g, m_c_kv_norm_g, v_c_kv_norm_g), d_w_s=(d_w_s, m_d_w_s, v_d_w_s),
                   d_b_s=(d_b_s, m_d_b_s, v_d_b_s), final_norm_g=(final_norm_g, m_final_norm_g, v_final_norm_g),
                   a_conv_w=(a_conv_w, m_a_conv_w, v_a_conv_w), c_q_norm_g=(c_q_norm_g, m_c_q_norm_g, v_c_q_norm_g),
                   d_ln_g=(d_ln_g, m_d_ln_g, v_d_ln_g), d_ln_b=(d_ln_b, m_d_ln_b, v_d_ln_b),
                   ffn_conv_w=(ffn_conv_w, m_ffn_conv_w, v_ffn_conv_w))
    small_out = _adamw_small("adamw_small", [tuple(a.reshape(small_view[n]) for a in small_w[n]) for n in small_names],
                             list(small_parts))
    for n, outs in zip(small_names, small_out):
        res[n] = [o_.reshape(small_w[n][0].shape) for o_ in outs]

    order = ["ada_w", "ada_b", "norm1_g", "norm2_g", "ab_w_in", "a_conv_w", "b_mix_w", "b_scale", "ab_w_out", "cd_w_in", "c_q_norm_g",
             "c_w_uq", "c_kv_norm_g", "c_w_ukv", "d_ln_g", "d_ln_b", "d_w_s", "d_b_s", "cd_w_out", "ffn_w_up", "ffn_conv_w",
             "ffn_w_down", "final_norm_g"]
    return (loss, dx0[None], *[res[n][0] for n in order], *[res[n][1] for n in order], *[res[n][2] for n in order],
            *[res[n][3] for n in order])
```

```python
import functools
import math

import jax
import jax.numpy as jnp
from jax import lax
from jax.experimental import pallas as pl
from jax.experimental.pallas import tpu as pltpu

F32 = jnp.float32
BF16 = jnp.bfloat16
_MXU_DTYPE = BF16
WIRE_DTYPE = BF16
ACT_DTYPE = BF16
_VMEM_LIMIT = 56 * 2 ** 20
N_DEV = 8
EPS = 1e-6
POOL_WINDOWS = (2, 4, 8, 16)
ATTN_SCALE = (64 + 32) ** -0.5
ADAM_LR, ADAM_B1, ADAM_B2, ADAM_EPS, ADAM_WD, ADAM_STEP = 0.001, 0.9, 0.999, 1e-08, 0.01, 10
MESH = pl.DeviceIdType.MESH
ANY = pl.BlockSpec(memory_space=pl.ANY)


def _cp(*sem):
    return pltpu.CompilerParams(dimension_semantics=sem, vmem_limit_bytes=_VMEM_LIMIT)


def _dot(a, b, contract):
    dn = {"nn": (((1,), (0,)), ((), ())), "nt": (((1,), (1,)), ((), ())), "tn": (((0,), (0,)), ((), ()))}[contract]
    return lax.dot_general(a.astype(_MXU_DTYPE), b.astype(_MXU_DTYPE), dn, preferred_element_type=F32)


def _my_position():
    x, y, c = lax.axis_index("x"), lax.axis_index("y"), lax.axis_index("c")
    return x, y, c, 4 * x + 2 * y + c


def _exchange(name, groups, scatter):
    flat = [a for g in groups for a in g]
    n_in, n_grp = len(flat), len(groups)
    out_shapes = []
    for g in groups:
        slab = g[0].shape[1:] if scatter else g[0].shape
        out_shapes.append(jax.ShapeDtypeStruct((N_DEV, len(g)) + tuple(slab), g[0].dtype))

    def body(*refs):
        ins, outs = refs[:n_in], refs[n_in:n_in + n_grp]
        send_sems, recv_sems, local_sems = refs[n_in + n_grp:]
        x, y, c, me = _my_position()
        i = 0
        for gi, g in enumerate(groups):
            for l in range(len(g)):
                src = ins[i]
                i += 1
                pltpu.make_async_copy(src.at[me] if scatter else src, outs[gi].at[me, l], local_sems.at[gi]).start()
                for k in range(1, N_DEV):
                    px = 1 - x if k & 4 else x
                    py = 1 - y if k & 2 else y
                    pc = 1 - c if k & 1 else c
                    peer = 4 * px + 2 * py + pc
                    pltpu.make_async_remote_copy(
                        src_ref=src.at[peer] if scatter else src, dst_ref=outs[gi].at[me, l],
                        send_sem=send_sems.at[gi], recv_sem=recv_sems.at[gi],
                        device_id=(px, py, pc), device_id_type=MESH).start()
        for gi in range(n_grp):
            mine = outs[gi].at[me]
            pltpu.make_async_copy(mine, mine, local_sems.at[gi]).wait()
            seven = outs[gi].at[pl.ds(0, N_DEV - 1)]
            w = pltpu.make_async_remote_copy(src_ref=seven, dst_ref=seven, send_sem=send_sems.at[gi],
                                             recv_sem=recv_sems.at[gi], device_id=(x, y, c), device_id_type=MESH)
            w.wait_send()
            w.wait_recv()

    return pl.pallas_call(
        body, name=name, out_shape=tuple(out_shapes),
        in_specs=[ANY] * n_in, out_specs=tuple([ANY] * n_grp),
        scratch_shapes=[pltpu.SemaphoreType.DMA((n_grp,)), pltpu.SemaphoreType.DMA((n_grp,)),
                        pltpu.SemaphoreType.DMA((n_grp,))],
        compiler_params=pltpu.CompilerParams(has_side_effects=True),
    )(*flat)


HBM_SPEC = pl.BlockSpec(memory_space=pltpu.HBM)
SEM_SPEC = pl.BlockSpec(memory_space=pltpu.SEMAPHORE)
EFFECT = pltpu.SideEffectType.DATAFLOW_SIDE_EFFECTING


def _put_mine(name, srcs, scatter, me):
    n = len(srcs)
    slabs = [tuple(s.shape[1:] if sc else s.shape) for s, sc in zip(srcs, scatter)]

    def body(me_ref, *refs):
        for i in range(n):
            refs[n + i][...] = refs[i][...]

    def at_me(slab):
        return pl.BlockSpec((None,) + slab, lambda g, me_ref, nd=len(slab): (me_ref[0],) + (0,) * nd)

    def whole(slab):
        return pl.BlockSpec(slab, lambda g, me_ref, nd=len(slab): (0,) * nd)

    return pl.pallas_call(
        body, name=name,
        grid_spec=pltpu.PrefetchScalarGridSpec(
            num_scalar_prefetch=1, grid=(1,),
            in_specs=[at_me(slab) if sc else whole(slab) for slab, sc in zip(slabs, scatter)],
            out_specs=[at_me(slab) for slab in slabs]),
        out_shape=[jax.ShapeDtypeStruct((N_DEV,) + slab, s.dtype) for slab, s in zip(slabs, srcs)],
        compiler_params=_cp("arbitrary"))(me.reshape(1), *srcs)


def _exchange_start(name, srcs, scatter, after, me):
    n = len(srcs)
    scatter = list(scatter) if isinstance(scatter, (list, tuple)) else [scatter] * n
    lands = _put_mine(name + "_mine", srcs, scatter, me)
    srcs = [pltpu.with_memory_space_constraint(a, pltpu.HBM) for a in srcs]
    lands = [pltpu.with_memory_space_constraint(a, pltpu.HBM) for a in lands]

    def body(*refs):
        ins, land = refs[:n], refs[n:2 * n]
        send_sems, recv_sems, token = refs[2 * n + 1], refs[2 * n + 2], refs[-1]
        x, y, c, me_in = _my_position()
        for i in range(n):
            for k in range(1, N_DEV):
                px = 1 - x if k & 4 else x
                py = 1 - y if k & 2 else y
                pc = 1 - c if k & 1 else c
                pltpu.make_async_remote_copy(
                    src_ref=ins[i].at[4 * px + 2 * py + pc] if scatter[i] else ins[i], dst_ref=land[i].at[me_in],
                    send_sem=send_sems.at[i], recv_sem=recv_sems.at[i],
                    device_id=(px, py, pc), device_id_type=MESH).start()
        token[...] = jnp.zeros_like(token)

    outs = pl.pallas_call(
        body, name=name,
        out_shape=(pltpu.SemaphoreType.DMA((n,)), pltpu.SemaphoreType.DMA((n,)),
                   *[pltpu.HBM(a.shape, a.dtype) for a in srcs], *[pltpu.HBM(a.shape, a.dtype) for a in lands],
                   jax.ShapeDtypeStruct((8, 128), F32)),
        in_specs=[HBM_SPEC] * (2 * n) + [ANY],
        out_specs=(SEM_SPEC, SEM_SPEC, *[HBM_SPEC] * (2 * n), pl.BlockSpec(memory_space=pltpu.VMEM)),
        input_output_aliases={i: 2 + i for i in range(2 * n)},
        compiler_params=pltpu.CompilerParams(has_side_effects=EFFECT),
    )(*srcs, *lands, after)
    return (outs[0], outs[1], outs[2:2 + n], outs[2 + n:2 + 2 * n]), outs[-1]


def _exchange_wait(name, handle, after, which=None):
    send_sems, recv_sems, srcs, lands = handle
    which = list(range(len(srcs))) if which is None else list(which)
    srcs, lands = [srcs[i] for i in which], [lands[i] for i in which]
    n = len(srcs)

    def body(*refs):
        land, send_ref, recv_ref = refs[n:2 * n], refs[2 * n], refs[2 * n + 1]
        x, y, c, _ = _my_position()
        for k, i in enumerate(which):
            seven = land[k].at[pl.ds(0, N_DEV - 1)]
            w = pltpu.make_async_remote_copy(src_ref=seven, dst_ref=seven, send_sem=send_ref.at[i], recv_sem=recv_ref.at[i],
                                             device_id=(x, y, c), device_id_type=MESH)
            w.wait_send()
            w.wait_recv()

    outs = pl.pallas_call(
        body, name=name,
        out_shape=(*[pltpu.HBM(a.shape, a.dtype) for a in srcs], *[pltpu.HBM(a.shape, a.dtype) for a in lands]),
        in_specs=[HBM_SPEC] * (2 * n) + [SEM_SPEC, SEM_SPEC, ANY],
        out_specs=tuple([HBM_SPEC] * (2 * n)),
        input_output_aliases={i: i for i in range(2 * n)},
        compiler_params=pltpu.CompilerParams(has_side_effects=EFFECT),
    )(*srcs, *lands, send_sems, recv_sems, after)
    return outs[n:]


def _other_chips(x, y):
    return [(1 - x, y), (x, 1 - y), (1 - x, 1 - y)]


def _hier_gather_start(name, srcs, after, me):
    n = len(srcs)
    lands = _put_mine(name + "_mine", srcs, [False] * n, me)
    srcs = [pltpu.with_memory_space_constraint(a, pltpu.HBM) for a in srcs]
    lands = [pltpu.with_memory_space_constraint(a, pltpu.HBM) for a in lands]

    def body(*refs):
        ins, land = refs[:n], refs[n:2 * n]
        ici_send, ici_recv, d2d_send, d2d_recv = refs[2 * n + 1:2 * n + 5]
        token = refs[-1]
        x, y, c, me_in = _my_position()
        for i in range(n):
            pltpu.make_async_remote_copy(src_ref=ins[i], dst_ref=land[i].at[me_in], send_sem=d2d_send.at[i], recv_sem=d2d_recv.at[i],
                                         device_id=(x, y, 1 - c), device_id_type=MESH).start()
            for px, py in _other_chips(x, y):
                pltpu.make_async_remote_copy(src_ref=ins[i], dst_ref=land[i].at[me_in], send_sem=ici_send.at[i],
                                             recv_sem=ici_recv.at[i], device_id=(px, py, c), device_id_type=MESH).start()
        token[...] = jnp.zeros_like(token)

    sem = pltpu.SemaphoreType.DMA((n,))
    outs = pl.pallas_call(
        body, name=name,
        out_shape=(sem, sem, sem, sem, *[pltpu.HBM(a.shape, a.dtype) for a in srcs], *[pltpu.HBM(a.shape, a.dtype) for a in lands],
                   jax.ShapeDtypeStruct((8, 128), F32)),
        in_specs=[HBM_SPEC] * (2 * n) + [ANY],
        out_specs=(SEM_SPEC,) * 4 + (HBM_SPEC,) * (2 * n) + (pl.BlockSpec(memory_space=pltpu.VMEM),),
        input_output_aliases={i: 4 + i for i in range(2 * n)},
        compiler_params=pltpu.CompilerParams(has_side_effects=EFFECT),
    )(*srcs, *lands, after)
    return (outs[:4], outs[4:4 + n], outs[4 + n:4 + 2 * n]), outs[-1]


def _hier_gather_forward(name, handle, after):
    sems, srcs, lands = handle
    n = len(srcs)

    def body(*refs):
        land = refs[n:2 * n]
        ici_send, ici_recv, d2d_send, d2d_recv = refs[2 * n:2 * n + 4]
        x, y, c, _ = _my_position()
        for i in range(n):
            three = land[i].at[pl.ds(0, 3)]
            pltpu.make_async_remote_copy(src_ref=three, dst_ref=three, send_sem=ici_send.at[i], recv_sem=ici_recv.at[i],
                                         device_id=(x, y, c), device_id_type=MESH).wait_recv()
            for px, py in _other_chips(x, y):
                slab = land[i].at[4 * px + 2 * py + c]
                pltpu.make_async_remote_copy(src_ref=slab, dst_ref=slab, send_sem=d2d_send.at[i], recv_sem=d2d_recv.at[i],
                                             device_id=(x, y, 1 - c), device_id_type=MESH).start()

    outs = pl.pallas_call(
        body, name=name,
        out_shape=(*[pltpu.HBM(a.shape, a.dtype) for a in srcs], *[pltpu.HBM(a.shape, a.dtype) for a in lands]),
        in_specs=[HBM_SPEC] * (2 * n) + [SEM_SPEC] * 4 + [ANY],
        out_specs=tuple([HBM_SPEC] * (2 * n)),
        input_output_aliases={i: i for i in range(2 * n)},
        compiler_params=pltpu.CompilerParams(has_side_effects=EFFECT),
    )(*srcs, *lands, *sems, after)
    return (sems, outs[:n], outs[n:])


def _hier_gather_wait(name, handle, after):
    sems, srcs, lands = handle
    n = len(srcs)

    def body(*refs):
        land = refs[n:2 * n]
        ici_send, ici_recv, d2d_send, d2d_recv = refs[2 * n:2 * n + 4]
        x, y, c, _ = _my_position()
        for i in range(n):
            three, four = land[i].at[pl.ds(0, 3)], land[i].at[pl.ds(0, 4)]
            pltpu.make_async_remote_copy(src_ref=three, dst_ref=three, send_sem=ici_send.at[i], recv_sem=ici_recv.at[i],
                                         device_id=(x, y, c), device_id_type=MESH).wait_send()
            w = pltpu.make_async_remote_copy(src_ref=four, dst_ref=four, send_sem=d2d_send.at[i], recv_sem=d2d_recv.at[i],
                                             device_id=(x, y, c), device_id_type=MESH)
            w.wait_send()
            w.wait_recv()

    outs = pl.pallas_call(
        body, name=name,
        out_shape=(*[pltpu.HBM(a.shape, a.dtype) for a in srcs], *[pltpu.HBM(a.shape, a.dtype) for a in lands]),
        in_specs=[HBM_SPEC] * (2 * n) + [SEM_SPEC] * 4 + [ANY],
        out_specs=tuple([HBM_SPEC] * (2 * n)),
        input_output_aliases={i: i for i in range(2 * n)},
        compiler_params=pltpu.CompilerParams(has_side_effects=EFFECT),
    )(*srcs, *lands, *sems, after)
    return outs[n:]


def _pack(arrs):
    flat = jnp.concatenate([a.reshape(-1).astype(F32) for a in arrs])
    n = flat.shape[0]
    rows = -(-n // 1024) * 8
    return jnp.pad(flat, (0, rows * 128 - n)).reshape(rows, 128)


def _unpack(buf, shapes, lead=()):
    flat = buf.reshape(lead + (-1,))
    out, off = [], 0
    for s in shapes:
        n = math.prod(s)
        out.append(flat[..., off:off + n].reshape(lead + tuple(s)))
        off += n
    return out


def _mm(name, a, a_spec, b, b_spec, out_sds, o_spec, grid, contract, nk=1, stacked=0):
    o_blk = tuple(d for d in o_spec.block_shape if d is not None)

    def body(a_ref, b_ref, o_ref, *acc):
        if stacked:
            r = _dot(a_ref[0], b_ref[0], contract)
            for q in range(1, stacked):
                r = r + _dot(a_ref[q], b_ref[q], contract)
        else:
            r = _dot(a_ref[...], b_ref[...], contract)
        if nk == 1:
            o_ref[...] = r.astype(o_ref.dtype)
        else:
            k = pl.program_id(len(grid) - 1)

            @pl.when(k == 0)
            def _():
                acc[0][...] = r

            @pl.when(k > 0)
            def _():
                acc[0][...] += r

            @pl.when(k == nk - 1)
            def _():
                o_ref[...] = acc[0][...].astype(o_ref.dtype)

    sem = ("parallel",) * (len(grid) - 1) + (("arbitrary",) if nk > 1 else ("parallel",))
    return pl.pallas_call(
        body, name=name, out_shape=out_sds, grid=grid, in_specs=[a_spec, b_spec], out_specs=o_spec,
        scratch_shapes=[pltpu.VMEM(o_blk, F32)] if nk > 1 else [], compiler_params=_cp(*sem))(a, b)


def _tile(n, want):
    t = min(n, want)
    assert n % t == 0, (n, t)
    return t


def _mm_nn(name, a, b, out_dtype=F32, tm=512, tn=512):
    (M, K), N = a.shape, b.shape[1]
    tm, tn = _tile(M, tm), _tile(N, tn)
    return _mm(name, a, pl.BlockSpec((tm, K), lambda i, j: (i, 0)), b, pl.BlockSpec((K, tn), lambda i, j: (0, j)),
               jax.ShapeDtypeStruct((M, N), out_dtype), pl.BlockSpec((tm, tn), lambda i, j: (i, j)),
               (M // tm, N // tn), "nn")


def _mm_nt(name, a, b, out_dtype=F32, tm=512, tn=512):
    (M, K), N = a.shape, b.shape[0]
    tm, tn = _tile(M, tm), _tile(N, tn)
    return _mm(name, a, pl.BlockSpec((tm, K), lambda i, j: (i, 0)), b, pl.BlockSpec((tn, K), lambda i, j: (j, 0)),
               jax.ShapeDtypeStruct((M, N), out_dtype), pl.BlockSpec((tm, tn), lambda i, j: (i, j)),
               (M // tm, N // tn), "nt")


def _mm_tn(name, a, b, out_dtype=F32, tm=512, tn=512):
    (K, M), N = a.shape, b.shape[1]
    tm, tn = _tile(M, tm), _tile(N, tn)
    return _mm(name, a, pl.BlockSpec((K, tm), lambda i, j: (0, i)), b, pl.BlockSpec((K, tn), lambda i, j: (0, j)),
               jax.ShapeDtypeStruct((M, N), out_dtype), pl.BlockSpec((tm, tn), lambda i, j: (i, j)),
               (M // tm, N // tn), "tn")


def _mm_cols(name, a, w, out_dtype=F32, tm=512):
    (M, K), (J, _, n) = a.shape, w.shape
    tm = _tile(M, tm)
    return _mm(name, a, pl.BlockSpec((tm, K), lambda j, i: (i, 0)), w, pl.BlockSpec((None, K, n), lambda j, i: (j, 0, 0)),
               jax.ShapeDtypeStruct((J, M, n), out_dtype), pl.BlockSpec((None, tm, n), lambda j, i: (j, i, 0)),
               (J, M // tm), "nn")


def _mm_cols_dx(name, d, w, out_dtype=F32, tm=512, jb=None):
    (J, M, n), K = d.shape, w.shape[1]
    tm, jb = _tile(M, tm), J if jb is None else jb
    return _mm(name, d, pl.BlockSpec((jb, tm, n), lambda i, j: (j, i, 0)), w, pl.BlockSpec((jb, K, n), lambda i, j: (j, 0, 0)),
               jax.ShapeDtypeStruct((M, K), out_dtype), pl.BlockSpec((tm, K), lambda i, j: (i, 0)),
               (M // tm, J // jb), "nt", nk=J // jb, stacked=jb)


def _mm_cols_dw(name, a, d, out_dtype=F32, tk=512):
    (M, K), (J, _, n) = a.shape, d.shape
    tk = _tile(K, tk)
    return _mm(name, a, pl.BlockSpec((M, tk), lambda j, i: (0, i)), d, pl.BlockSpec((None, M, n), lambda j, i: (j, 0, 0)),
               jax.ShapeDtypeStruct((J, K, n), out_dtype), pl.BlockSpec((None, tk, n), lambda j, i: (j, i, 0)),
               (J, K // tk), "tn")


def _mm_cols_dwt(name, a, d, out_dtype=F32, tk=512):
    (M, K), (J, _, n) = a.shape, d.shape
    tk = _tile(K, tk)
    return _mm(name, d, pl.BlockSpec((None, M, n), lambda j, i: (j, 0, 0)), a, pl.BlockSpec((M, tk), lambda j, i: (0, i)),
               jax.ShapeDtypeStruct((J, n, K), out_dtype), pl.BlockSpec((None, n, tk), lambda j, i: (j, 0, i)),
               (J, K // tk), "tn")


def _mm_rows_resid(name, a, w, resid, gate, tm=512):
    (Q, M, k), N = a.shape, w.shape[2]
    tm = _tile(M, tm)

    def body(a_ref, w_ref, r_ref, g_ref, y_ref, x_ref):
        y = _dot(a_ref[0], w_ref[0], "nn")
        for q in range(1, Q):
            y = y + _dot(a_ref[q], w_ref[q], "nn")
        y_ref[...] = y.astype(y_ref.dtype)
        x_ref[...] = r_ref[...] + g_ref[...] * y

    return pl.pallas_call(
        body, name=name, grid=(M // tm,),
        out_shape=(jax.ShapeDtypeStruct((M, N), ACT_DTYPE), jax.ShapeDtypeStruct((M, N), F32)),
        in_specs=[pl.BlockSpec((Q, tm, k), lambda i: (0, i, 0)), pl.BlockSpec((Q, k, N), lambda i: (0, 0, 0)),
                  pl.BlockSpec((tm, N), lambda i: (i, 0)), pl.BlockSpec((1, N), lambda i: (0, 0))],
        out_specs=(pl.BlockSpec((tm, N), lambda i: (i, 0)), pl.BlockSpec((tm, N), lambda i: (i, 0))),
        compiler_params=_cp("parallel"))(a, w, resid, gate)


def _mm_rows_dx(name, d, w, out_dtype=F32, tm=512):
    (M, N), (Q, k, _) = d.shape, w.shape
    tm = _tile(M, tm)
    return _mm(name, d, pl.BlockSpec((tm, N), lambda q, i: (i, 0)), w, pl.BlockSpec((None, k, N), lambda q, i: (q, 0, 0)),
               jax.ShapeDtypeStruct((Q, M, k), out_dtype), pl.BlockSpec((None, tm, k), lambda q, i: (q, i, 0)),
               (Q, M // tm), "nt")


def _mm_rows_dw(name, a, d, out_dtype=F32, tn=512):
    (Q, M, k), N = a.shape, d.shape[1]
    tn = _tile(N, tn)
    return _mm(name, a, pl.BlockSpec((None, M, k), lambda q, j: (q, 0, 0)), d, pl.BlockSpec((M, tn), lambda q, j: (0, j)),
               jax.ShapeDtypeStruct((Q, k, N), out_dtype), pl.BlockSpec((None, k, tn), lambda q, j: (q, 0, j)),
               (Q, N // tn), "tn")


def _silu(v):
    return v * jax.nn.sigmoid(v)


def _ada_fwd(c16, ada_w):
    L, D, n = ada_w.shape

    def body(c_ref, w_ref, o_ref):
        o_ref[...] = _dot(_silu(c_ref[...]), w_ref[...], "nn")

    return pl.pallas_call(
        body, name="ada_fwd", grid=(L,), out_shape=jax.ShapeDtypeStruct((L, 16, n), F32),
        in_specs=[pl.BlockSpec((16, D), lambda l: (0, 0)), pl.BlockSpec((None, D, n), lambda l: (l, 0, 0))],
        out_specs=pl.BlockSpec((None, 16, n), lambda l: (l, 0, 0)), compiler_params=_cp("parallel"))(c16, ada_w)


def _ada_bwd(c16, dmod16):
    L, _, n = dmod16.shape
    D = c16.shape[1]

    def body(c_ref, d_ref, o_ref):
        o_ref[...] = _dot(_silu(c_ref[...]), d_ref[...], "tn")

    return pl.pallas_call(
        body, name="ada_bwd", grid=(L,), out_shape=jax.ShapeDtypeStruct((L, D, n), F32),
        in_specs=[pl.BlockSpec((16, D), lambda l: (0, 0)), pl.BlockSpec((None, 16, n), lambda l: (l, 0, 0))],
        out_specs=pl.BlockSpec((None, D, n), lambda l: (l, 0, 0)), compiler_params=_cp("parallel"))(c16, dmod16)


def _row_spec(tr, n):
    return pl.BlockSpec((tr, n), lambda i: (i, 0))


def _vec_spec(n):
    return pl.BlockSpec((1, n), lambda i: (0, 0))


def _rmsmod_fwd(name, x, g, sc, sh, after, tr=256):
    S, D = x.shape

    def body(x_ref, g_ref, sc_ref, sh_ref, after_ref, h_ref):
        xv = x_ref[...]
        rstd = lax.rsqrt(jnp.mean(xv * xv, axis=-1, keepdims=True) + EPS)
        y = xv * rstd * g_ref[...]
        h_ref[...] = (y * (1.0 + sc_ref[...]) + sh_ref[...]).astype(h_ref.dtype)

    return pl.pallas_call(
        body, name=name, grid=(S // tr,), out_shape=jax.ShapeDtypeStruct((S, D), _MXU_DTYPE),
        in_specs=[_row_spec(tr, D), _vec_spec(D), _vec_spec(D), _vec_spec(D), ANY], out_specs=_row_spec(tr, D),
        compiler_params=_cp("parallel"))(x, g, sc, sh, after)


def _acc_rows(ref, val, first):
    s = jnp.sum(val, axis=0, keepdims=True)

    @pl.when(first)
    def _():
        ref[...] = s

    @pl.when(jnp.logical_not(first))
    def _():
        ref[...] += s


def _gate_bwd_tail(dx, y_ref, gate_ref, dy_ref, dgate_ref, first):
    dy_ref[...] = (gate_ref[...] * dx).astype(dy_ref.dtype)
    _acc_rows(dgate_ref, dx * y_ref[...].astype(F32), first)


def _rmsmod_bwd(name, x, g, sc, dh, dres, after, y=None, gate=None, tr=256):
    S, D = x.shape
    tail = y is not None

    def body(x_ref, g_ref, sc_ref, dh_ref, dres_ref, after_ref, *rest):
        (y_ref, gate_ref), rest = (rest[:2], rest[2:]) if tail else ((None, None), rest)
        dx_ref, dg_ref, dsc_ref, dsh_ref = rest[:4]
        first = pl.program_id(0) == 0
        xv, dh_v, gv = x_ref[...], dh_ref[...], g_ref[...]
        rstd = lax.rsqrt(jnp.mean(xv * xv, axis=-1, keepdims=True) + EPS)
        xhat = xv * rstd
        _acc_rows(dsh_ref, dh_v, first)
        _acc_rows(dsc_ref, dh_v * (xhat * gv), first)
        dyg = dh_v * (1.0 + sc_ref[...])
        _acc_rows(dg_ref, dyg * xhat, first)
        dxhat = dyg * gv
        dx = dres_ref[...] + rstd * (dxhat - xhat * jnp.mean(dxhat * xhat, axis=-1, keepdims=True))
        dx_ref[...] = dx
        if tail:
            _gate_bwd_tail(dx, y_ref, gate_ref, rest[4], rest[5], first)

    vec = jax.ShapeDtypeStruct((1, D), F32)
    return pl.pallas_call(
        body, name=name, grid=(S // tr,),
        out_shape=(jax.ShapeDtypeStruct((S, D), F32), vec, vec, vec) + ((jax.ShapeDtypeStruct((S, D), _MXU_DTYPE), vec) if tail else ()),
        in_specs=[_row_spec(tr, D), _vec_spec(D), _vec_spec(D), _row_spec(tr, D), _row_spec(tr, D), ANY]
        + ([_row_spec(tr, D), _vec_spec(D)] if tail else []),
        out_specs=(_row_spec(tr, D), _vec_spec(D), _vec_spec(D), _vec_spec(D)) + ((_row_spec(tr, D), _vec_spec(D)) if tail else ()),
        compiler_params=_cp("arbitrary"))(x, g, sc, dh, dres, after, *((y, gate) if tail else ()))


def _loss_head(x, g, target, y, gate, tr=256):
    S, D = x.shape

    def body(x_ref, g_ref, t_ref, y_ref, gate_ref, loss_ref, dx_ref, dg_ref, dy_ref, dgate_ref):
        first = pl.program_id(0) == 0
        xv, gv = x_ref[...], g_ref[...]
        rstd = lax.rsqrt(jnp.mean(xv * xv, axis=-1, keepdims=True) + EPS)
        xhat = xv * rstd
        err = xhat * gv - t_ref[...]
        part = 0.5 * jnp.sum(jnp.mean(err * err, axis=-1, keepdims=True), axis=0, keepdims=True)

        @pl.when(first)
        def _():
            loss_ref[...] = part

        @pl.when(jnp.logical_not(first))
        def _():
            loss_ref[...] += part

        dout = err * (1.0 / D)
        _acc_rows(dg_ref, dout * xhat, first)
        dxhat = dout * gv
        dx = rstd * (dxhat - xhat * jnp.mean(dxhat * xhat, axis=-1, keepdims=True))
        dx_ref[...] = dx
        _gate_bwd_tail(dx, y_ref, gate_ref, dy_ref, dgate_ref, first)

    vec = jax.ShapeDtypeStruct((1, D), F32)
    return pl.pallas_call(
        body, name="loss_head", grid=(S // tr,),
        out_shape=(jax.ShapeDtypeStruct((1, 1), F32), jax.ShapeDtypeStruct((S, D), F32), vec,
                   jax.ShapeDtypeStruct((S, D), _MXU_DTYPE), vec),
        in_specs=[_row_spec(tr, D), _vec_spec(D), _row_spec(tr, D), _row_spec(tr, D), _vec_spec(D)],
        out_specs=(pl.BlockSpec((1, 1), lambda i: (0, 0)), _row_spec(tr, D), _vec_spec(D), _row_spec(tr, D), _vec_spec(D)),
        compiler_params=_cp("arbitrary"))(x, g, target, y, gate)


def _gate_bwd(name, dx, y, gate, tr=256):
    S, D = dx.shape

    def body(dx_ref, y_ref, g_ref, dy_ref, dg_ref):
        dxv = dx_ref[...]
        dy_ref[...] = (g_ref[...] * dxv).astype(dy_ref.dtype)
        _acc_rows(dg_ref, dxv * y_ref[...], pl.program_id(0) == 0)

    return pl.pallas_call(
        body, name=name, grid=(S // tr,),
        out_shape=(jax.ShapeDtypeStruct((S, D), _MXU_DTYPE), jax.ShapeDtypeStruct((1, D), F32)),
        in_specs=[_row_spec(tr, D), _row_spec(tr, D), _vec_spec(D)], out_specs=(_row_spec(tr, D), _vec_spec(D)),
        compiler_params=_cp("arbitrary"))(dx, y, gate)


def _shift_down(v, k):
    t = lax.broadcasted_iota(jnp.int32, v.shape, 0)
    return jnp.where(t >= k, pltpu.roll(v, k, axis=0), 0.0)


def _shift_up(v, k):
    n = v.shape[0]
    t = lax.broadcasted_iota(jnp.int32, v.shape, 0)
    return jnp.where(t < n - k, pltpu.roll(v, n - k, axis=0), 0.0)


def _window_sum(p, w, shift):
    s, k = p, 1
    while k < w:
        s = s + shift(s, k)
        k *= 2
    return s


def _pool_count(shape, w):
    t = lax.broadcasted_iota(jnp.int32, shape, 0)
    return jnp.minimum(t + 1, w).astype(F32)


def _ab_specs(S):
    zs = [pl.BlockSpec((None, S, 128), functools.partial(lambda g, q: (2 * q + g // 2, 0, g % 2), q=q)) for q in range(4)]
    return zs


def _ab_mix_fwd(z8, conv_w, mix_w, scale):
    S = z8.shape[1]

    def body(b_ref, c_ref, a_ref, p_ref, w_ref, mix_ref, sc_ref, y_ref):
        g = pl.program_id(0)
        cg = c_ref[...] * a_ref[...]
        w = w_ref[...]
        conv = w[0:1] * _shift_down(cg, 2) + w[1:2] * _shift_down(cg, 1) + w[2:3] * cg
        y_ref[0] = (b_ref[...] * conv).astype(y_ref.dtype)
        for gg, win in enumerate(POOL_WINDOWS):
            @pl.when(g == gg)
            def _(win=win):
                p = p_ref[...]
                pooled = _window_sum(p, win, _shift_down) / _pool_count(p.shape, win) - p
                y_ref[1] = (_dot(pooled, mix_ref[...], "nn") * sc_ref[...]).astype(y_ref.dtype)

    return pl.pallas_call(
        body, name="ab_mix_fwd", grid=(4,), out_shape=jax.ShapeDtypeStruct((2, S, 512), _MXU_DTYPE),
        in_specs=_ab_specs(S) + [pl.BlockSpec((3, 128), lambda g: (0, g)), pl.BlockSpec((None, 128, 128), lambda g: (g, 0, 0)),
                                 pl.BlockSpec((1, 128), lambda g: (0, g))],
        out_specs=pl.BlockSpec((2, S, 128), lambda g: (0, 0, g)), compiler_params=_cp("parallel"))(z8, z8, z8, z8, conv_w, mix_w, scale)


def _ab_mix_bwd(z8, dycat2, conv_w, mix_w, scale, after):
    S = z8.shape[1]

    def body(b_ref, c_ref, a_ref, p_ref, dy_ref, w_ref, mix_ref, sc_ref, after_ref, dz_ref, dw_ref, dmix_ref, dsc_ref):
        g = pl.program_id(0)
        bv, cv, av, w = b_ref[...], c_ref[...], a_ref[...], w_ref[...]
        dya = dy_ref[0]
        cg = cv * av
        cg1, cg2 = _shift_down(cg, 1), _shift_down(cg, 2)
        conv = w[0:1] * cg2 + w[1:2] * cg1 + w[2:3] * cg
        dz_ref[0] = (dya * conv).astype(dz_ref.dtype)
        dconv = dya * bv
        dcg = w[2:3] * dconv + w[1:2] * _shift_up(dconv, 1) + w[0:1] * _shift_up(dconv, 2)
        dz_ref[1] = (dcg * av).astype(dz_ref.dtype)
        dz_ref[2] = (dcg * cv).astype(dz_ref.dtype)
        dw_ref[0:1, :] = jnp.sum(dconv * cg2, axis=0, keepdims=True)
        dw_ref[1:2, :] = jnp.sum(dconv * cg1, axis=0, keepdims=True)
        dw_ref[2:3, :] = jnp.sum(dconv * cg, axis=0, keepdims=True)
        for gg, win in enumerate(POOL_WINDOWS):
            @pl.when(g == gg)
            def _(win=win):
                p, dyb, mix = p_ref[...], dy_ref[1], mix_ref[...]
                cnt = _pool_count(p.shape, win)
                pooled = _window_sum(p, win, _shift_down) / cnt - p
                dsc_ref[...] = jnp.sum(dyb * _dot(pooled, mix, "nn"), axis=0, keepdims=True)
                dmixed = dyb * sc_ref[...]
                dmix_ref[...] = _dot(pooled, dmixed, "tn")
                dpooled = _dot(dmixed, mix, "nt")
                dz_ref[3] = (_window_sum(dpooled / cnt, win, _shift_up) - dpooled).astype(dz_ref.dtype)

    return pl.pallas_call(
        body, name="ab_mix_bwd", grid=(4,),
        out_shape=(jax.ShapeDtypeStruct((4, 2, S, 256), _MXU_DTYPE), jax.ShapeDtypeStruct((3, 512), F32),
                   jax.ShapeDtypeStruct((4, 128, 128), F32), jax.ShapeDtypeStruct((1, 512), F32)),
        in_specs=_ab_specs(S) + [pl.BlockSpec((2, S, 128), lambda g: (0, 0, g)), pl.BlockSpec((3, 128), lambda g: (0, g)),
                                 pl.BlockSpec((None, 128, 128), lambda g: (g, 0, 0)), pl.BlockSpec((1, 128), lambda g: (0, g)), ANY],
        out_specs=(pl.BlockSpec((4, None, S, 128), lambda g: (0, g // 2, 0, g % 2)), pl.BlockSpec((3, 128), lambda g: (0, g)),
                   pl.BlockSpec((None, 128, 128), lambda g: (g, 0, 0)), pl.BlockSpec((1, 128), lambda g: (0, g))),
        compiler_params=_cp("parallel"))(z8, z8, z8, z8, dycat2, conv_w, mix_w, scale, after)


HALO = 16


def _ffn_specs(S, n, tr):
    nb = S // HALO
    tile = pl.BlockSpec((2, None, tr, n), lambda j, i: (0, j, i, 0))
    prev = pl.BlockSpec((2, None, HALO, n), lambda j, i: (0, j, jnp.maximum(i * (tr // HALO) - 1, 0), 0))
    nxt = pl.BlockSpec((2, None, HALO, n), lambda j, i: (0, j, jnp.minimum((i + 1) * (tr // HALO), nb - 1), 0))
    cw = pl.BlockSpec((2, None, 3, n), lambda j, i: (0, j, 0, 0))
    return tile, prev, nxt, cw


def _shifted_rows(ext, lo, rows):
    ext = ext.astype(F32)
    return pltpu.roll(ext, 1, axis=0)[lo:lo + rows], pltpu.roll(ext, 2, axis=0)[lo:lo + rows]


def _ffn_gate_fwd(name, u24, cw24, tr=256):
    _, J, S, n = u24.shape
    tile, prev, _, cw = _ffn_specs(S, n, tr)

    def body(u_ref, up_ref, w_ref, a_ref):
        keep = (pl.program_id(1) > 0).astype(u_ref.dtype)
        z = []
        for h in range(2):
            ext = jnp.concatenate([up_ref[h] * keep, u_ref[h]], axis=0)
            x1, x2 = _shifted_rows(ext, HALO, tr)
            w = w_ref[h]
            z.append(w[0:1] * x2 + w[1:2] * x1 + w[2:3] * u_ref[h].astype(F32))
        a_ref[...] = (_silu(z[0]) * z[1]).astype(a_ref.dtype)

    return pl.pallas_call(
        body, name=name, grid=(J, S // tr), out_shape=jax.ShapeDtypeStruct((J, S, n), _MXU_DTYPE),
        in_specs=[tile, prev, cw], out_specs=pl.BlockSpec((None, tr, n), lambda j, i: (j, i, 0)),
        compiler_params=_cp("parallel", "parallel"))(u24, u24, cw24)


def _ffn_gate_bwd(name, u24, cw24, da4, w_up24, after, tr=256):
    _, J, S, n = u24.shape
    K = w_up24.shape[2]
    nb = S // HALO
    tile = pl.BlockSpec((2, None, tr, n), lambda i, j: (0, j, i, 0))
    prev = pl.BlockSpec((2, None, HALO, n), lambda i, j: (0, j, jnp.maximum(i * (tr // HALO) - 1, 0), 0))
    nxt = pl.BlockSpec((2, None, HALO, n), lambda i, j: (0, j, jnp.minimum((i + 1) * (tr // HALO), nb - 1), 0))
    whole = lambda shape: pl.BlockSpec(shape, lambda i, j: (0,) * len(shape))

    def body(u_ref, up_ref, un_ref, cw_ref, da_ref, dan_ref, wup_ref, after_ref, du_ref, dcw_ref, dh_ref, acc_ref):
        i, j = pl.program_id(0), pl.program_id(1)
        first = i == 0
        keep_prev = (i > 0).astype(u_ref.dtype)
        keep_next = (i < S // tr - 1).astype(F32)
        w = [cw_ref[h, j] for h in range(2)]
        m = tr + HALO
        xs, z = [], []
        for h in range(2):
            ext = jnp.concatenate([up_ref[h] * keep_prev, u_ref[h], un_ref[h]], axis=0)
            x1, x2 = _shifted_rows(ext, HALO, m)
            x0 = ext[HALO:HALO + m].astype(F32)
            xs.append((x2, x1, x0))
            z.append(w[h][0:1] * x2 + w[h][1:2] * x1 + w[h][2:3] * x0)
        zg, zu = z
        da = jnp.concatenate([da_ref[...].astype(F32), dan_ref[...].astype(F32) * keep_next], axis=0)
        sg = jax.nn.sigmoid(zg)
        dz = [da * zu * (sg * (1.0 + zg * (1.0 - sg))), da * (zg * sg)]
        dh = None
        for h in range(2):
            d = dz[h]
            du = w[h][2:3] * d[:tr] + w[h][1:2] * pltpu.roll(d, m - 1, axis=0)[:tr] + w[h][0:1] * pltpu.roll(d, m - 2, axis=0)[:tr]
            du = du.astype(du_ref.dtype)
            du_ref[h] = du
            part = _dot(du, wup_ref[h, j], "nt")
            dh = part if dh is None else dh + part
            dt = d[:tr]
            parts = [jnp.sum(dt * xk[:tr], axis=0, keepdims=True) for xk in xs[h]]
            for k in range(3):
                @pl.when(first)
                def _(k=k, h=h):
                    dcw_ref[h, j, k:k + 1, :] = parts[k]

                @pl.when(jnp.logical_not(first))
                def _(k=k, h=h):
                    dcw_ref[h, j, k:k + 1, :] += parts[k]

        @pl.when(j == 0)
        def _():
            acc_ref[...] = dh

        @pl.when(j > 0)
        def _():
            acc_ref[...] += dh

        @pl.when(j == J - 1)
        def _():
            dh_ref[...] = acc_ref[...]

    da_tile = pl.BlockSpec((None, tr, n), lambda i, j: (j, i, 0))
    da_next = pl.BlockSpec((None, HALO, n), lambda i, j: (j, jnp.minimum((i + 1) * (tr // HALO), nb - 1), 0))
    return pl.pallas_call(
        body, name=name, grid=(S // tr, J),
        out_shape=(jax.ShapeDtypeStruct((2, J, S, n), _MXU_DTYPE), jax.ShapeDtypeStruct((2, J, 3, n), F32),
                   jax.ShapeDtypeStruct((S, K), F32)),
        in_specs=[tile, prev, nxt, whole((2, J, 3, n)), da_tile, da_next, whole((2, J, K, n)), ANY],
        out_specs=(tile, whole((2, J, 3, n)), pl.BlockSpec((tr, K), lambda i, j: (i, 0))),
        scratch_shapes=[pltpu.VMEM((tr, K), F32)],
        compiler_params=_cp("arbitrary", "arbitrary"))(u24, u24, u24, cw24, da4, da4, w_up24, after)


def _rms_rows(v, g):
    rstd = lax.rsqrt(jnp.mean(v * v, axis=-1, keepdims=True) + EPS)
    return v * rstd * g


def _rms_rows_bwd(v, g, dy):
    rstd = lax.rsqrt(jnp.mean(v * v, axis=-1, keepdims=True) + EPS)
    vhat = v * rstd
    dvhat = dy * g
    return rstd * (dvhat - vhat * jnp.mean(dvhat * vhat, axis=-1, keepdims=True)), dy * vhat


def _rope(v, cos, sa, sb):
    return v * cos + pltpu.roll(v, 112, axis=1) * sa + pltpu.roll(v, 16, axis=1) * sb


def _rope_t(d, cos, sa, sb):
    return d * cos + pltpu.roll(d * sa, 16, axis=1) + pltpu.roll(d * sb, 112, axis=1)


def _qkv_rope_fwd(z, qg, kvg, w_uq_t, w_kv, cosq, cosk, sa, sb, tr=256):
    S = z.shape[0]

    def body(ql_ref, kvl_ref, kpe_ref, qg_ref, kvg_ref, wq_ref, wkv_ref, cq_ref, ck_ref, sa_ref, sb_ref,
             qn_ref, kvn_ref, qo_ref, ko_ref, vo_ref):
        cq, ck, sa_v, sb_v = cq_ref[...], ck_ref[...], sa_ref[...], sb_ref[...]
        qn = _rms_rows(ql_ref[...], qg_ref[...]).astype(qn_ref.dtype)
        kvn = _rms_rows(kvl_ref[...], kvg_ref[...]).astype(kvn_ref.dtype)
        qn_ref[...] = qn
        kvn_ref[...] = kvn
        q = _dot(qn, wq_ref[...], "nt")
        kv = _dot(kvn, wkv_ref[...], "nn")
        kpe = _rope(kpe_ref[...], ck, sa_v, sb_v)
        for h in range(8):
            cols = slice(128 * h, 128 * h + 128)
            qo_ref[:, cols] = _rope(q[:, cols], cq, sa_v, sb_v).astype(qo_ref.dtype)
            ko_ref[:, cols] = (kv[:, cols] + kpe).astype(ko_ref.dtype)
        vo_ref[...] = kv[:, 1024:1536].astype(vo_ref.dtype)

    tab = _row_spec(tr, 128)
    whole = lambda a: pl.BlockSpec(a.shape, lambda i: (0, 0))
    return pl.pallas_call(
        body, name="qkv_rope_fwd", grid=(S // tr,),
        out_shape=(jax.ShapeDtypeStruct((S, 256), _MXU_DTYPE), jax.ShapeDtypeStruct((S, 128), _MXU_DTYPE),
                   jax.ShapeDtypeStruct((S, 1024), _MXU_DTYPE), jax.ShapeDtypeStruct((S, 1024), _MXU_DTYPE),
                   jax.ShapeDtypeStruct((S, 512), _MXU_DTYPE)),
        in_specs=[pl.BlockSpec((tr, 256), lambda i: (i, 0)), pl.BlockSpec((tr, 128), lambda i: (i, 2)),
                  pl.BlockSpec((tr, 128), lambda i: (i, 3)), _vec_spec(256), _vec_spec(128), whole(w_uq_t), whole(w_kv),
                  tab, tab, tab, tab],
        out_specs=(_row_spec(tr, 256), _row_spec(tr, 128), _row_spec(tr, 1024), _row_spec(tr, 1024), _row_spec(tr, 512)),
        compiler_params=_cp("parallel"))(z, z, z, qg, kvg, w_uq_t, w_kv, cosq, cosk, sa, sb)


def _attn_bwd_prep(o, dycat2, tr=256):
    S = o.shape[0]

    def body(o_ref, do_ref, delta_ref, doa_ref, dob_ref):
        do = do_ref[...]
        prod = do * o_ref[...]
        lane = lax.broadcasted_iota(jnp.int32, do.shape, 1)
        for p in range(4):
            cols = slice(128 * p, 128 * p + 128)
            first = lane[:, cols] < 128 * p + 64
            da = jnp.sum(jnp.where(first, prod[:, cols], 0.0), axis=-1, keepdims=True)
            db = jnp.sum(jnp.where(first, 0.0, prod[:, cols]), axis=-1, keepdims=True)
            delta_ref[p] = jnp.where(first, da, db)
            doa_ref[p] = jnp.where(first, do[:, cols], 0.0).astype(doa_ref.dtype)
            dob_ref[p] = jnp.where(first, 0.0, do[:, cols]).astype(dob_ref.dtype)

    pair = pl.BlockSpec((4, tr, 128), lambda i: (0, i, 0))
    return pl.pallas_call(
        body, name="attn_bwd_prep", grid=(S // tr,),
        out_shape=(jax.ShapeDtypeStruct((4, S, 128), F32), jax.ShapeDtypeStruct((4, S, 128), _MXU_DTYPE),
                   jax.ShapeDtypeStruct((4, S, 128), _MXU_DTYPE)),
        in_specs=[_row_spec(tr, 512), pl.BlockSpec((None, tr, 512), lambda i: (0, i, 0))],
        out_specs=(pair, pair, pair), compiler_params=_cp("parallel"))(o, dycat2)


def _qkv_rope_bwd(z, qg, kvg, dq, dk, dv, duv, w_uq_t, w_kv, cosq, cosk, sa, sb, tr=256):
    S = z.shape[0]

    def body(ql_ref, kvl_ref, qg_ref, kvg_ref, dq_ref, dk_ref, dv_ref, duv_ref, wq_ref, wkv_ref, cq_ref, ck_ref, sa_ref, sb_ref,
             dqo_ref, dkv_ref, dz_ref, dqg_ref, dkvg_ref):
        first = pl.program_id(0) == 0
        cq, ck, sa_v, sb_v = cq_ref[...], ck_ref[...], sa_ref[...], sb_ref[...]
        tot = jnp.zeros((tr, 128), F32)
        for h in range(8):
            cols = slice(128 * h, 128 * h + 128)
            dqo_ref[:, cols] = _rope_t(dq_ref[:, cols], cq, sa_v, sb_v).astype(dqo_ref.dtype)
            dkh = dk_ref[:, cols]
            tot = tot + dkh
            dkv_ref[:, cols] = dkh.astype(dkv_ref.dtype)
        dkv_ref[:, 1024:1536] = dv_ref[...].astype(dkv_ref.dtype)
        dqn = _dot(dqo_ref[...], wq_ref[...], "nn")
        dkvn = _dot(dkv_ref[...], wkv_ref[...], "nt")
        dql, dqg = _rms_rows_bwd(ql_ref[...], qg_ref[...], dqn)
        dkvl, dkvg = _rms_rows_bwd(kvl_ref[...], kvg_ref[...], dkvn)
        _acc_rows(dqg_ref, dqg, first)
        _acc_rows(dkvg_ref, dkvg, first)
        dz_ref[:, 0:256] = dql.astype(dz_ref.dtype)
        dz_ref[:, 256:384] = dkvl.astype(dz_ref.dtype)
        dz_ref[:, 384:512] = _rope_t(tot, ck, sa_v, sb_v).astype(dz_ref.dtype)
        dz_ref[:, 512:1536] = duv_ref[...].astype(dz_ref.dtype)

    tab = _row_spec(tr, 128)
    whole = lambda a: pl.BlockSpec(a.shape, lambda i: (0, 0))
    return pl.pallas_call(
        body, name="qkv_rope_bwd", grid=(S // tr,),
        out_shape=(jax.ShapeDtypeStruct((S, 1024), _MXU_DTYPE), jax.ShapeDtypeStruct((S, 1536), _MXU_DTYPE),
                   jax.ShapeDtypeStruct((S, 1536), _MXU_DTYPE), jax.ShapeDtypeStruct((1, 256), F32), jax.ShapeDtypeStruct((1, 128), F32)),
        in_specs=[pl.BlockSpec((tr, 256), lambda i: (i, 0)), pl.BlockSpec((tr, 128), lambda i: (i, 2)), _vec_spec(256), _vec_spec(128),
                  _row_spec(tr, 1024), _row_spec(tr, 1024), _row_spec(tr, 512), _row_spec(tr, 1024), whole(w_uq_t), whole(w_kv),
                  tab, tab, tab, tab],
        out_specs=(_row_spec(tr, 1024), _row_spec(tr, 1536), _row_spec(tr, 1536), _vec_spec(256), _vec_spec(128)),
        compiler_params=_cp("arbitrary"))(z, z, qg, kvg, dq, dk, dv, duv, w_uq_t, w_kv, cosq, cosk, sa, sb)


NEG = -1e30


def _attn_fwd(q, k, v, tq=512, tk=512):
    S = q.shape[0]
    assert tq == tk

    def body(q_ref, k_ref, v_ref, o_ref, lse_ref):
        i = pl.program_id(1)
        qs = [q_ref[:, 0:128], q_ref[:, 128:256]]

        def step(kb, carry, diagonal=False):
            start = pl.multiple_of(kb * tk, tk)
            vv = v_ref[pl.ds(start, tk), :]
            out = []
            for h in range(2):
                m, l, acc = carry[3 * h:3 * h + 3]
                s = _dot(qs[h], k_ref[pl.ds(start, tk), 128 * h:128 * h + 128], "nt") * ATTN_SCALE
                if diagonal:
                    s = jnp.where(below, s, NEG)
                m_new = jnp.maximum(m, jnp.max(s, axis=-1, keepdims=True))
                alpha = jnp.exp(m - m_new)
                p = jnp.exp(s - m_new)
                out += [m_new, alpha * l + jnp.sum(p, axis=-1, keepdims=True), alpha * acc + _dot(p, vv, "nn")]
            return tuple(out)

        below = lax.broadcasted_iota(jnp.int32, (tq, tk), 1) <= lax.broadcasted_iota(jnp.int32, (tq, tk), 0)
        init = (jnp.full((tq, 1), NEG, F32), jnp.zeros((tq, 1), F32), jnp.zeros((tq, 128), F32)) * 2
        ma, la, acca, mb, lb, accb = step(i, lax.fori_loop(0, i, step, init), diagonal=True)
        lane = lax.broadcasted_iota(jnp.int32, (tq, 128), 1)
        o_ref[...] = jnp.where(lane < 64, acca / la, accb / lb)
        lse_ref[...] = jnp.where(lane < 64, ma + jnp.log(la), mb + jnp.log(lb))

    return pl.pallas_call(
        body, name="attn_fwd", grid=(4, S // tq),
        out_shape=(jax.ShapeDtypeStruct((S, 512), F32), jax.ShapeDtypeStruct((4, S, 128), F32)),
        in_specs=[pl.BlockSpec((tq, 256), lambda p, i: (i, p)), pl.BlockSpec((S, 256), lambda p, i: (0, p)),
                  pl.BlockSpec((S, 128), lambda p, i: (0, p))],
        out_specs=(pl.BlockSpec((tq, 128), lambda p, i: (i, p)), pl.BlockSpec((None, tq, 128), lambda p, i: (p, i, 0))),
        compiler_params=_cp("parallel", "parallel"))(q, k, v)


def _attn_bwd(q, k, v, lse, delta, doa, dob, tq=512, tk=512):
    S = q.shape[0]
    assert tq == tk

    def body(q_ref, k_ref, v_ref, lse_ref, delta_ref, doa_ref, dob_ref, dq_ref, dk_ref, dv_ref):
        j = pl.program_id(1)

        @pl.when(j == 0)
        def _():
            dq_ref[...] = jnp.zeros_like(dq_ref)

        below = lax.broadcasted_iota(jnp.int32, (tq, tk), 1) <= lax.broadcasted_iota(jnp.int32, (tq, tk), 0)
        ks = [k_ref[:, 0:128], k_ref[:, 128:256]]
        vv = v_ref[...]

        def step(qb, carry, diagonal=False):
            dka, dkb, dvp = carry
            start = pl.multiple_of(qb * tq, tq)
            rows = pl.ds(start, tq)
            lse_v, delta_v = lse_ref[rows, :], delta_ref[rows, :]
            dos = [doa_ref[rows, :], dob_ref[rows, :]]
            dks = [dka, dkb]
            for h in range(2):
                delta = delta_v[:, 64 * h:64 * h + 1]
                do_h = dos[h]
                qh = q_ref[rows, 128 * h:128 * h + 128]
                s = _dot(qh, ks[h], "nt") * ATTN_SCALE
                p = jnp.exp(s - lse_v[:, 64 * h:64 * h + 1])
                if diagonal:
                    p = jnp.where(below, p, 0.0)
                dvp = dvp + _dot(p, do_h, "tn")
                ds = p * (_dot(do_h, vv, "nt") - delta) * ATTN_SCALE
                dq_ref[rows, 128 * h:128 * h + 128] += _dot(ds, ks[h], "nn")
                dks[h] = dks[h] + _dot(ds, qh, "tn")
            return dks[0], dks[1], dvp

        zero = jnp.zeros((tk, 128), F32)
        dka, dkb, dvp = lax.fori_loop(j + 1, S // tq, step, step(j, (zero, zero, zero), diagonal=True))
        dk_ref[:, 0:128] = dka
        dk_ref[:, 128:256] = dkb
        dv_ref[...] = dvp

    return pl.pallas_call(
        body, name="attn_bwd", grid=(4, S // tk),
        out_shape=(jax.ShapeDtypeStruct((S, 1024), F32), jax.ShapeDtypeStruct((S, 1024), F32), jax.ShapeDtypeStruct((S, 512), F32)),
        in_specs=[pl.BlockSpec((S, 256), lambda p, j: (0, p)), pl.BlockSpec((tk, 256), lambda p, j: (j, p)),
                  pl.BlockSpec((tk, 128), lambda p, j: (j, p))] + [pl.BlockSpec((None, S, 128), lambda p, j: (p, 0, 0))] * 4,
        out_specs=(pl.BlockSpec((S, 256), lambda p, j: (0, p)), pl.BlockSpec((tk, 256), lambda p, j: (j, p)),
                   pl.BlockSpec((tk, 128), lambda p, j: (j, p))),
        compiler_params=_cp("parallel", "arbitrary"))(q, k, v, lse, delta, doa, dob)


CHUNK = 128
GELU_C = math.sqrt(2.0 / math.pi)


def _gelu(v):
    t = jnp.tanh(GELU_C * (v + 0.044715 * (v * v * v)))
    return v * (0.5 * (1.0 + t)), t


def _gelu_grad(v, t):
    return 0.5 * (1.0 + t) + v * (0.5 * (1.0 - t * t) * GELU_C * (1.0 + 3.0 * 0.044715 * v * v))


def _tril(w):
    r = lax.broadcasted_iota(jnp.int32, w.shape, 0)
    c = lax.broadcasted_iota(jnp.int32, w.shape, 1)
    return jnp.where(c <= r, w, 0.0)


def _layer_norm(v, g, b):
    xc = v - jnp.mean(v, axis=-1, keepdims=True)
    rstd = lax.rsqrt(jnp.mean(xc * xc, axis=-1, keepdims=True) + EPS)
    xhat = xc * rstd
    return xhat * g + b, xhat, rstd


def _sgu_fwd(z, o, ln_g, ln_b, w_s, b_st, tr=256):
    S = z.shape[0]

    def body(u_ref, v_ref, o_ref, g_ref, b_ref, ws_ref, bs_ref, y_ref):
        gu, _ = _gelu(u_ref[...])
        gv, _ = _gelu(v_ref[...])
        vln, _, _ = _layer_norm(gv, g_ref[...], b_ref[...])
        y_ref[0] = o_ref[...].astype(y_ref.dtype)
        for g in range(4):
            wt = _tril(ws_ref[g])
            cols = slice(128 * g, 128 * g + 128)
            for ch in range(tr // CHUNK):
                rows = slice(CHUNK * ch, CHUNK * ch + CHUNK)
                mixed = _dot(wt, vln[rows, cols], "nn") + bs_ref[:, g:g + 1]
                y_ref[1, rows, cols] = (gu[rows, cols] * mixed).astype(y_ref.dtype)

    return pl.pallas_call(
        body, name="sgu_fwd", grid=(S // tr,), out_shape=jax.ShapeDtypeStruct((2, S, 512), _MXU_DTYPE),
        in_specs=[pl.BlockSpec((tr, 512), lambda i: (i, 1)), pl.BlockSpec((tr, 512), lambda i: (i, 2)), _row_spec(tr, 512),
                  _vec_spec(512), _vec_spec(512), pl.BlockSpec((4, 128, 128), lambda i: (0, 0, 0)), pl.BlockSpec((128, 4), lambda i: (0, 0))],
        out_specs=pl.BlockSpec((2, tr, 512), lambda i: (0, i, 0)), compiler_params=_cp("parallel"))(z, z, o, ln_g, ln_b, w_s, b_st)


def _sgu_bwd(z, dycat2, ln_g, ln_b, w_s, b_st, tr=256):
    S = z.shape[0]

    def body(u_ref, v_ref, dy_ref, g_ref, b_ref, ws_ref, bs_ref, duv_ref, dg_ref, db_ref, dws_ref, dbs_ref):
        first = pl.program_id(0) == 0
        u_pre, v_pre = u_ref[...], v_ref[...]
        gu, tu = _gelu(u_pre)
        gv, tv = _gelu(v_pre)
        gain = g_ref[...]
        vln, xhat, rstd = _layer_norm(gv, gain, b_ref[...])

        @pl.when(first)
        def _():
            dws_ref[...] = jnp.zeros_like(dws_ref)
            dbs_ref[...] = jnp.zeros_like(dbs_ref)

        dvln_cols = []
        for g in range(4):
            wt = _tril(ws_ref[g])
            cols = slice(128 * g, 128 * g + 128)
            dmixed_sum = jnp.zeros((CHUNK, 128), F32)
            dw = jnp.zeros((CHUNK, CHUNK), F32)
            dvln_rows = []
            for ch in range(tr // CHUNK):
                rows = slice(CHUNK * ch, CHUNK * ch + CHUNK)
                vt = vln[rows, cols]
                mixed = _dot(wt, vt, "nn") + bs_ref[:, g:g + 1]
                dyd = dy_ref[rows, cols]
                duv_ref[rows, cols] = (dyd * mixed * _gelu_grad(u_pre[rows, cols], tu[rows, cols])).astype(duv_ref.dtype)
                dmixed = dyd * gu[rows, cols]
                dmixed_sum = dmixed_sum + dmixed
                dw = dw + _dot(dmixed, vt, "nt")
                dvln_rows.append(_dot(wt, dmixed, "tn"))
            dws_ref[g] += _tril(dw)
            dbs_ref[g:g + 1, :] += jnp.sum(dmixed_sum.T, axis=0, keepdims=True)
            dvln_cols.append(jnp.concatenate(dvln_rows, axis=0))
        dvln = jnp.concatenate(dvln_cols, axis=1)
        _acc_rows(dg_ref, dvln * xhat, first)
        _acc_rows(db_ref, dvln, first)
        dxhat = dvln * gain
        dgv = rstd * (dxhat - jnp.mean(dxhat, axis=-1, keepdims=True) - xhat * jnp.mean(dxhat * xhat, axis=-1, keepdims=True))
        duv_ref[:, 512:1024] = (dgv * _gelu_grad(v_pre, tv)).astype(duv_ref.dtype)

    return pl.pallas_call(
        body, name="sgu_bwd", grid=(S // tr,),
        out_shape=(jax.ShapeDtypeStruct((S, 1024), _MXU_DTYPE), jax.ShapeDtypeStruct((1, 512), F32), jax.ShapeDtypeStruct((1, 512), F32),
                   jax.ShapeDtypeStruct((4, 128, 128), F32), jax.ShapeDtypeStruct((4, 128), F32)),
        in_specs=[pl.BlockSpec((tr, 512), lambda i: (i, 1)), pl.BlockSpec((tr, 512), lambda i: (i, 2)),
                  pl.BlockSpec((None, tr, 512), lambda i: (1, i, 0)), _vec_spec(512), _vec_spec(512),
                  pl.BlockSpec((4, 128, 128), lambda i: (0, 0, 0)), pl.BlockSpec((128, 4), lambda i: (0, 0))],
        out_specs=(_row_spec(tr, 1024), _vec_spec(512), _vec_spec(512), pl.BlockSpec((4, 128, 128), lambda i: (0, 0, 0)),
                   pl.BlockSpec((4, 128), lambda i: (0, 0))),
        compiler_params=_cp("arbitrary"))(z, z, dycat2, ln_g, ln_b, w_s, b_st)


def _sum_parts(name, parts, tr=512):
    P, R, C = parts.shape
    tr = _tile(R, tr) if R % 8 == 0 else R

    def body(p_ref, o_ref):
        g = p_ref[0]
        for k in range(1, P):
            g = g + p_ref[k]
        o_ref[...] = g

    return pl.pallas_call(
        body, name=name, grid=(R // tr,), out_shape=jax.ShapeDtypeStruct((R, C), F32),
        in_specs=[pl.BlockSpec((P, tr, C), lambda i: (0, i, 0))], out_specs=_row_spec(tr, C),
        compiler_params=_cp("parallel"))(parts)


def _adamw_math(w, m, v, g):
    c1 = 1.0 / (1.0 - ADAM_B1 ** ADAM_STEP)
    c2 = 1.0 / (1.0 - ADAM_B2 ** ADAM_STEP)
    m2 = ADAM_B1 * m + (1.0 - ADAM_B1) * g
    v2 = ADAM_B2 * v + (1.0 - ADAM_B2) * (g * g)
    return -ADAM_LR * ((m2 * c1) / (jnp.sqrt(v2 * c2) + ADAM_EPS) + ADAM_WD * w), m2, v2


def _adamw_small(name, params, parts):
    n = len(params)

    def body(*refs):
        ins, outs = refs[:4 * n], refs[4 * n:]
        for i in range(n):
            w_ref, m_ref, v_ref, p_ref = ins[4 * i:4 * i + 4]
            g = p_ref[0].astype(F32)
            for k in range(1, N_DEV):
                g = g + p_ref[k].astype(F32)
            delta, m2, v2 = _adamw_math(w_ref[...], m_ref[...], v_ref[...], g)
            outs[4 * i][...] = g
            outs[4 * i + 1][...] = delta
            outs[4 * i + 2][...] = m2
            outs[4 * i + 3][...] = v2

    flat = [a for (w, m, v), p in zip(params, parts) for a in (w, m, v, p)]
    out = pl.pallas_call(
        body, name=name, out_shape=[jax.ShapeDtypeStruct(w.shape, F32) for (w, _, _) in params for _ in range(4)],
        compiler_params=pltpu.CompilerParams(vmem_limit_bytes=_VMEM_LIMIT))(*flat)
    return [out[4 * i:4 * i + 4] for i in range(n)]


ADAMW_BLOCK_BYTES = 36 * 2 ** 20


def _adamw(name, w, m, v, parts):
    L, R, C = w.shape
    P = parts[0].shape[0]
    row_bytes = 2 * C * (7 * 4 + P * parts[0].dtype.itemsize)
    tr = R
    if R * row_bytes > ADAMW_BLOCK_BYTES:
        tr = next(t for t in (1024, 512, 256, 128, 64, 32, 16) if R % t == 0 and t * row_bytes <= ADAMW_BLOCK_BYTES)
    nr = R // tr
    c1 = 1.0 / (1.0 - ADAM_B1 ** ADAM_STEP)
    c2 = 1.0 / (1.0 - ADAM_B2 ** ADAM_STEP)

    def body(w_ref, m_ref, v_ref, *rest):
        p_refs, (g_ref, d_ref, mo_ref, vo_ref) = rest[:L], rest[L:]
        for ll in range(L):
            @pl.when(pl.program_id(0) == ll)
            def _(p_ref=p_refs[ll]):
                g = p_ref[0].astype(F32)
                for k in range(1, P):
                    g = g + p_ref[k].astype(F32)
                m2 = ADAM_B1 * m_ref[...] + (1.0 - ADAM_B1) * g
                v2 = ADAM_B2 * v_ref[...] + (1.0 - ADAM_B2) * (g * g)
                g_ref[...] = g
                mo_ref[...] = m2
                vo_ref[...] = v2
                d_ref[...] = -ADAM_LR * ((m2 * c1) / (jnp.sqrt(v2 * c2) + ADAM_EPS) + ADAM_WD * w_ref[...])

    def part_spec(ll):
        return pl.BlockSpec((P, tr, C), lambda l, i: (0, jnp.where(l == ll, i, jnp.where(l < ll, 0, nr - 1)), 0))

    full = pl.BlockSpec((None, tr, C), lambda l, i: (l, i, 0))
    sds = jax.ShapeDtypeStruct((L, R, C), F32)
    return pl.pallas_call(
        body, name=name, grid=(L, nr), out_shape=(sds, sds, sds, sds),
        in_specs=[full] * 3 + [part_spec(ll) for ll in range(L)],
        out_specs=(full,) * 4, compiler_params=_cp("arbitrary", "arbitrary"))(w, m, v, *parts)


def _rope_tables(positions):
    half = 16
    inv_freq = 10000.0 ** (-jnp.arange(half, dtype=F32) / half)
    ang = positions.astype(F32)[:, None] * inv_freq
    cos, sin = jnp.cos(ang), jnp.sin(ang)
    S = positions.shape[0]
    z16, z32, z64 = jnp.zeros((S, 16), F32), jnp.zeros((S, 32), F32), jnp.zeros((S, 64), F32)
    cosk = jnp.concatenate([z64, cos, cos, z32], axis=1)
    cosq = jnp.concatenate([jnp.ones((S, 64), F32), cos, cos, z32], axis=1)
    sa = jnp.concatenate([z64, -sin, z16, z32], axis=1)
    sb = jnp.concatenate([z64, z16, sin, z32], axis=1)
    return cosq, cosk, sa, sb


def _ffn_fwd(l, x, mod, n2g, get_w_up8, cw24, get_w_down4):
    sh, sc, gate = mod
    h = _rmsmod_fwd(f"ffn{l}_norm", x, n2g, sc, sh, n2g)
    w_up8 = get_w_up8(h)
    u8 = _mm_cols(f"ffn{l}_up", h, w_up8, out_dtype=ACT_DTYPE, tm=2048)
    S, n = u8.shape[1], u8.shape[2]
    u24 = u8.reshape(2, 4, S, n)
    a4 = _ffn_gate_fwd(f"ffn{l}_gate", u24, cw24)
    w_down4 = get_w_down4(a4)
    f, x_new = _mm_rows_resid(f"ffn{l}_down", a4, w_down4, x, gate)
    return x_new, (x, h, u24, a4, f), w_up8, w_down4


def _ffn_bwd(l, dx, df, dgate, saved, mod, n2g, w_up8, cw24, w_down4, me, y_prev, gate_prev):
    sh, sc, gate = mod
    x, h, u24, a4, f = saved
    da4 = _mm_rows_dx(f"ffn{l}_down_dx", df, w_down4, out_dtype=ACT_DTYPE, tm=2048)
    dw_down4 = _mm_rows_dw(f"ffn{l}_down_dw", a4, df, out_dtype=WIRE_DTYPE)
    sent_down, token = _exchange_start(f"scatter_ffn{l}_down", [dw_down4.reshape(8, 352, dw_down4.shape[2])], True, dgate, me)
    du24, dcw24, dh = _ffn_gate_bwd(f"ffn{l}_act_bwd", u24, cw24, da4, w_up8.reshape((2, 4) + w_up8.shape[1:]), token)
    du8 = du24.reshape((8,) + du24.shape[2:])
    dw_up8t = _mm_cols_dwt(f"ffn{l}_up_dw", h, du8, out_dtype=WIRE_DTYPE, tk=1024)
    sent_up, token = _exchange_start(f"scatter_ffn{l}_up", [dw_up8t], True, dcw24, me)
    dx_new, dn2g, dsc, dsh, dy_prev, dgate_prev = _rmsmod_bwd(f"ffn{l}_norm_bwd", x, n2g, sc, dh, dx, token, y_prev, gate_prev)
    return dx_new, dict(sent_up=sent_up, sent_down=sent_down, cw24=dcw24, n2g=dn2g, mod=(dsh, dsc, dgate)), dy_prev, dgate_prev


def kernel(x, c, positions, ada_w, ada_b, norm1_g, norm2_g, ab_w_in, a_conv_w, b_mix_w, b_scale, ab_w_out, cd_w_in, c_q_norm_g, c_w_uq, c_kv_norm_g, c_w_ukv, d_ln_g, d_ln_b, d_w_s, d_b_s, cd_w_out, ffn_w_up, ffn_conv_w, ffn_w_down, final_norm_g, loss_target, m_ada_w, m_ada_b, m_norm1_g, m_norm2_g, m_ab_w_in, m_a_conv_w, m_b_mix_w, m_b_scale, m_ab_w_out, m_cd_w_in, m_c_q_norm_g, m_c_w_uq, m_c_kv_norm_g, m_c_w_ukv, m_d_ln_g, m_d_ln_b, m_d_w_s, m_d_b_s, m_cd_w_out, m_ffn_w_up, m_ffn_conv_w, m_ffn_w_down, m_final_norm_g, v_ada_w, v_ada_b, v_norm1_g, v_norm2_g, v_ab_w_in, v_a_conv_w, v_b_mix_w, v_b_scale, v_ab_w_out, v_cd_w_in, v_c_q_norm_g, v_c_w_uq, v_c_kv_norm_g, v_c_w_ukv, v_d_ln_g, v_d_ln_b, v_d_w_s, v_d_b_s, v_cd_w_out, v_ffn_w_up, v_ffn_conv_w, v_ffn_w_down, v_final_norm_g):
    S, D = x.shape[1], x.shape[2]
    me = 4 * lax.axis_index("x") + 2 * lax.axis_index("y") + lax.axis_index("c")
    x0, target = x[0], loss_target[0]
    W = _MXU_DTYPE

    small_shapes = [(1024,), (3, 64), (32,), (64,), (64,), (2, 3, 704)]
    (g0,) = _exchange("gather_small", [[_pack([c, a_conv_w, c_q_norm_g, d_ln_g, d_ln_b, ffn_conv_w])]], scatter=False)
    c_all, aconv_s, qg_s, lng_s, lnb_s, fcw_s = _unpack(g0[:, 0], small_shapes, lead=(N_DEV,))
    conv_w = aconv_s.transpose(1, 0, 2).reshape(3, 512)
    qg, ln_g, ln_b = qg_s.reshape(1, 256), lng_s.reshape(1, 512), lnb_s.reshape(1, 512)
    cw24 = [fcw_s[:, l].reshape(2, 4, 3, 704) for l in range(2)]
    c16 = jnp.pad(c_all, ((0, 16 - N_DEV), (0, 0)))

    mod_cols = _ada_fwd(c16, ada_w)
    (g1,) = _exchange("gather_mod", [[_pack([mod_cols])]], scatter=False)
    mod_all = _unpack(g1[:, 0], [(2, 16, 768)], lead=(N_DEV,))[0]
    mod_mine = lax.dynamic_index_in_dim(mod_all, me, axis=2, keepdims=False)
    mod = mod_mine.transpose(1, 0, 2).reshape(2, 6 * D) + ada_b
    mods = [[mod[l, k * D:(k + 1) * D].reshape(1, D) for k in range(6)] for l in range(2)]

    gw_ab, token = _hier_gather_start("gather_w_ab", [ab_w_in[0].astype(W), ab_w_out[0].astype(W)], mod, me)
    gw_up0, token = _hier_gather_start("gather_w_ffn0_up", [ffn_w_up[0].astype(W)], token, me)
    gw_rest, started = _exchange_start("gather_w_rest", [
        ffn_w_down[0].astype(W), cd_w_in[0].T.astype(W), c_w_uq[0].T.astype(W), c_w_ukv[0].astype(W), cd_w_out[0].astype(W),
        ffn_w_up[1].astype(W), ffn_w_down[1].astype(W)], False, token, me)

    cosq, cosk, sa, sb = _rope_tables(positions[0])
    n1g = [norm1_g[l].reshape(1, D) for l in range(2)]
    n2g = [norm2_g[l].reshape(1, D) for l in range(2)]
    mix_w, scale = b_mix_w[0], b_scale
    kvg = c_kv_norm_g
    w_s, b_st = d_w_s[0], d_b_s[0].T

    sh1, sc1, g1m = mods[0][:3]
    h_ab = _rmsmod_fwd("ab_norm", x0, n1g[0], sc1, sh1, started)
    w_abin8, w_about = _hier_gather_wait("wait_w_ab", _hier_gather_forward("forward_w_ab", gw_ab, h_ab), h_ab)
    w_about2 = w_about.reshape(2, 512, D)
    z8 = _mm_cols("ab_in", h_ab, w_abin8, tm=2048)
    ycat_ab = _ab_mix_fwd(z8, conv_w, mix_w, scale)
    y_ab, x1 = _mm_rows_resid("ab_out", ycat_ab, w_about2, x0, g1m)
    w_up8, w_down4 = [None, None], [None, None]
    gw_up0 = _hier_gather_forward("forward_w_ffn0_up", gw_up0, x1)
    x2, ffn0_saved, w_up8[0], w_down4[0] = _ffn_fwd(
        0, x1, mods[0][3:], n2g[0], lambda after: _hier_gather_wait("wait_w_ffn0_up", gw_up0, after)[0], cw24[0],
        lambda after: _exchange_wait("wait_w_ffn0_down", gw_rest, after, [0])[0].reshape(4, 704, D))

    w_cdin, w_uq, w_ukv, w_cdout = _exchange_wait("wait_w_cd", gw_rest, x2, [1, 2, 3, 4])
    w_cdout2 = w_cdout.reshape(2, 512, D)
    w_cd_t = w_cdin.reshape(1440, D)
    zr = lambda n: jnp.zeros((n, D), W)
    w_cd_pad = jnp.concatenate([w_cd_t[:384], zr(64), w_cd_t[384:416], zr(32), w_cd_t[416:]], axis=0)
    w_uq_pad = jnp.pad(w_uq, ((0, 0), (0, 32), (0, 0))).reshape(1024, 256)
    w_ukv_h = w_ukv.transpose(1, 0, 2)
    w_k_pad = jnp.pad(w_ukv_h[:, :, :64], ((0, 0), (0, 0), (0, 64))).reshape(128, 1024)
    w_kv_pad = jnp.concatenate([w_k_pad, w_ukv_h[:, :, 64:].reshape(128, 512)], axis=1)

    sh1, sc1, g1c = mods[1][:3]
    h_cd = _rmsmod_fwd("cd_norm", x2, n1g[1], sc1, sh1, n1g[1])
    z_cd = _mm_nt("cd_in", h_cd, w_cd_pad)
    qn, kvn, q_r, k_r, v_r = _qkv_rope_fwd(z_cd, qg, kvg, w_uq_pad, w_kv_pad, cosq, cosk, sa, sb)
    o, lse = _attn_fwd(q_r, k_r, v_r)
    ycat_cd = _sgu_fwd(z_cd, o, ln_g, ln_b, w_s, b_st)
    y_cd, x3 = _mm_rows_resid("cd_out", ycat_cd, w_cdout2, x2, g1c)
    x4, ffn1_saved, w_up8[1], w_down4[1] = _ffn_fwd(
        1, x3, mods[1][3:], n2g[1], lambda after: _exchange_wait("wait_w_ffn1_up", gw_rest, after, [5])[0], cw24[1],
        lambda after: _exchange_wait("wait_w_ffn1_down", gw_rest, after, [6])[0].reshape(4, 704, D))

    loss_local, dx4, dfg, df1, dgate1 = _loss_head(x4, final_norm_g.reshape(1, D), target, ffn1_saved[4], mods[1][5])

    dx3, gf1, dy, dg1c = _ffn_bwd(1, dx4, df1, dgate1, ffn1_saved, mods[1][3:], n2g[1], w_up8[1], cw24[1], w_down4[1], me, y_cd, g1c)

    dycat = _mm_rows_dx("cd_out_dx", dy, w_cdout2)
    dw_cdout = _mm_rows_dw("cd_out_dw", ycat_cd, dy, out_dtype=WIRE_DTYPE)
    duv, dln_g, dln_b, dws, dbs = _sgu_bwd(z_cd, dycat, ln_g, ln_b, w_s, b_st)
    dq_r, dk_r, dv_r = _attn_bwd(q_r, k_r, v_r, lse, *_attn_bwd_prep(o, dycat))
    dqraw, dkvall, dz_cd, dqg, dkvg = _qkv_rope_bwd(z_cd, qg, kvg, dq_r, dk_r, dv_r, duv, w_uq_pad, w_kv_pad, cosq, cosk, sa, sb)
    dw_uq_pad = _mm_tn("cd_uq_dw", dqraw, qn, tn=256)
    dw_kv_pad = _mm_tn("cd_ukv_dw", kvn, dkvall, tm=128)
    dh_cd = _mm_nn("cd_in_dx", dz_cd, w_cd_pad)
    dw_cd_pad = _mm_tn("cd_in_dw", dz_cd, h_cd)
    dw_cd8 = jnp.concatenate([dw_cd_pad[:384], dw_cd_pad[448:480], dw_cd_pad[512:]], axis=0).astype(WIRE_DTYPE).reshape(8, 180, D)
    dw_uq8 = dw_uq_pad.reshape(8, 128, 256)[:, :96].astype(WIRE_DTYPE)
    dw_ukv8 = jnp.concatenate([dw_kv_pad[:, :1024].reshape(128, 8, 128)[:, :, :64], dw_kv_pad[:, 1024:].reshape(128, 8, 64)],
                              axis=2).transpose(1, 0, 2).astype(WIRE_DTYPE)
    sent_cd, token = _exchange_start("scatter_cd", [dw_cd8, dw_uq8, dw_ukv8, dw_cdout.reshape(8, 128, D)], True, dqg, me)
    early_names = ["c_kv_norm_g", "d_w_s", "d_b_s", "final_norm_g", "c_q_norm_g", "d_ln_g", "d_ln_b"]
    early_grads = [dkvg, dws.reshape(512, 128).astype(WIRE_DTYPE), dbs, dfg, dqg.reshape(8, 1, 32), dln_g.reshape(8, 1, 64),
                   dln_b.reshape(8, 1, 64)]
    early_sent, token = _exchange_start("gather_small_grads_early", early_grads, [False] * 4 + [True] * 3, token, me)
    dx2, dn1g_cd, dsc1_cd, dsh1_cd, df0, dgate0 = _rmsmod_bwd("cd_norm_bwd", x2, n1g[1], sc1, dh_cd, dx3, token,
                                                              ffn0_saved[4], mods[0][5])

    dx1, gf0, dy, dg1m = _ffn_bwd(0, dx2, df0, dgate0, ffn0_saved, mods[0][3:], n2g[0], w_up8[0], cw24[0], w_down4[0], me, y_ab, g1m)

    dw_about = _mm_rows_dw("ab_out_dw", ycat_ab, dy, out_dtype=WIRE_DTYPE)
    sent_about, token = _exchange_start("scatter_ab_out", [dw_about.reshape(8, 128, D)], True, dg1m, me)
    dycat = _mm_rows_dx("ab_out_dx", dy, w_about2)
    dz8, dconv_w, dmix_w, dscale = _ab_mix_bwd(z8, dycat, conv_w, mix_w, scale, token)
    dz8 = dz8.reshape(8, S, 256)
    dw_abin8 = _mm_cols_dw("ab_in_dw", h_ab, dz8, out_dtype=WIRE_DTYPE, tk=1024)
    sent_abin, token = _exchange_start("scatter_ab_in", [dw_abin8], True, dscale, me)
    dh_ab = _mm_cols_dx("ab_in_dx", dz8, w_abin8)
    dx0, dn1g_ab, dsc1_ab, dsh1_ab = _rmsmod_bwd("ab_norm_bwd", x0, n1g[0], mods[0][1], dh_ab, dx1, token)

    dmod = jnp.stack([jnp.concatenate([dsh1_ab, dsc1_ab, dg1m, *gf0["mod"]], axis=1)[0],
                      jnp.concatenate([dsh1_cd, dsc1_cd, dg1c, *gf1["mod"]], axis=1)[0]])
    late_names = ["ada_b", "norm1_g", "norm2_g", "b_mix_w", "b_scale", "a_conv_w", "ffn_conv_w"]
    late_grads = [dmod, jnp.concatenate([dn1g_ab, dn1g_cd]), jnp.concatenate([gf0["n2g"], gf1["n2g"]]),
                  dmix_w.reshape(512, 128).astype(WIRE_DTYPE), dscale, dconv_w.reshape(3, 8, 64).transpose(1, 0, 2),
                  jnp.stack([gf0["cw24"].reshape(8, 3, 704), gf1["cw24"].reshape(8, 3, 704)], axis=1),
                  jnp.pad(loss_local, ((0, 0), (0, 127)))]
    small_view = dict(ada_b=(2, 6 * D), norm1_g=(2, D), norm2_g=(2, D), b_mix_w=(512, 128), b_scale=(1, 512), c_kv_norm_g=(1, 128),
                      d_w_s=(512, 128), d_b_s=(4, 128), final_norm_g=(1, D),
                      a_conv_w=(3, 64), c_q_norm_g=(1, 32), d_ln_g=(1, 64), d_ln_b=(1, 64), ffn_conv_w=(2, 3, 704))
    late_sent, token = _exchange_start("gather_small_grads_late", late_grads, [False] * 5 + [True] * 2 + [False], dx0, me)

    res = {}

    def update(name, w, m, v, parts, shape3d):
        outs = _adamw("adamw_" + name, w.reshape(shape3d), m.reshape(shape3d), v.reshape(shape3d),
                      [p.reshape((p.shape[0],) + shape3d[1:]) for p in parts])
        res[name] = [o_.reshape(w.shape) for o_ in outs]

    p_cdin, p_uq, p_ukv, p_cdout = _exchange_wait("wait_scatter_cd", sent_cd, token)
    swap = lambda a: jnp.swapaxes(a, 1, 2)
    update("cd_w_in", swap(cd_w_in), swap(m_cd_w_in), swap(v_cd_w_in), [p_cdin], (1, 180, D))
    update("c_w_uq", swap(c_w_uq), swap(m_c_w_uq), swap(v_c_w_uq), [p_uq], (1, 96, 256))
    for name in ("cd_w_in", "c_w_uq"):
        res[name] = [swap(o_) for o_ in res[name]]
    update("c_w_ukv", c_w_ukv, m_c_w_ukv, v_c_w_ukv, [p_ukv], (1, 128, 128))
    update("cd_w_out", cd_w_out, m_cd_w_out, v_cd_w_out, [p_cdout], (1, 128, D))
    (p_dn1,) = _exchange_wait("wait_scatter_ffn1_down", gf1["sent_down"], token)
    (p_dn0,) = _exchange_wait("wait_scatter_ffn0_down", gf0["sent_down"], res["cd_w_out"][0])
    update("ffn_w_down", ffn_w_down, m_ffn_w_down, v_ffn_w_down, [p_dn0, p_dn1], (2, 352, D))
    (p_up1,) = _exchange_wait("wait_scatter_ffn1_up", gf1["sent_up"], token)
    (p_up0,) = _exchange_wait("wait_scatter_ffn0_up", gf0["sent_up"], res["ffn_w_down"][0])
    swap = lambda a: jnp.swapaxes(a, 1, 2)
    update("ffn_w_up", swap(ffn_w_up), swap(m_ffn_w_up), swap(v_ffn_w_up), [p_up0, p_up1], (2, 704, D))
    up_done = res["ffn_w_up"][0]
    res["ffn_w_up"] = [swap(o_) for o_ in res["ffn_w_up"]]
    (p_about,) = _exchange_wait("wait_scatter_ab_out", sent_about, up_done)
    update("ab_w_out", ab_w_out, m_ab_w_out, v_ab_w_out, [p_about], (1, 128, D))
    (p_abin,) = _exchange_wait("wait_scatter_ab_in", sent_abin, res["ab_w_out"][0])
    update("ab_w_in", ab_w_in, m_ab_w_in, v_ab_w_in, [p_abin], (1, D, 256))

    early_parts = _exchange_wait("wait_small_grads_early", early_sent, res["ab_w_in"][0])
    late_parts = _exchange_wait("wait_small_grads_late", late_sent, res["ab_w_in"][0])
    small_names = early_names + late_names
    small_parts = list(early_parts) + list(late_parts[:7])
    loss = jnp.sum(late_parts[7][:, 0, 0])
    dmod_all = late_parts[0]
    dmod_cols = lax.dynamic_slice_in_dim(dmod_all, me * 768, 768, axis=2).transpose(1, 0, 2)
    g_ada_w = _ada_bwd(c16, jnp.pad(dmod_cols, ((0, 0), (0, 16 - N_DEV), (0, 0))))
    update("ada_w", ada_w, m_ada_w, v_ada_w, [g_ada_w[l][None] for l in range(2)], (2, D, 768))

    small_w = dict(ada_b=(ada_b, m_ada_b, v_ada_b), norm1_g=(norm1_g, m_norm1_g, v_norm1_g), norm2_g=(norm2_g, m_norm2_g, v_norm2_g),
                   b_mix_w=(b_mix_w, m_b_mix_w, v_b_mix_w), b_scale=(b_scale, m_b_scale, v_b_scale),
                   c_kv_norm_g=(c_kv_norm_g, m_c_kv_norm_g, v_c_kv_norm_g), d_w_s=(d_w_s, m_d_w_s, v_d_w_s),
                   d_b_s=(d_b_s, m_d_b_s, v_d_b_s), final_norm_g=(final_norm_g, m_final_norm_g, v_final_norm_g),
                   a_conv_w=(a_conv_w, m_a_conv_w, v_a_conv_w), c_q_norm_g=(c_q_norm_g, m_c_q_norm_g, v_c_q_norm_g),
                   d_ln_g=(d_ln_g, m_d_ln_g, v_d_ln_g), d_ln_b=(d_ln_b, m_d_ln_b, v_d_ln_b),
                   ffn_conv_w=(ffn_conv_w, m_ffn_conv_w, v_ffn_conv_w))
    small_out = _adamw_small("adamw_small", [tuple(a.reshape(small_view[n]) for a in small_w[n]) for n in small_names],
                             list(small_parts))
    for n, outs in zip(small_names, small_out):
        res[n] = [o_.reshape(small_w[n][0].shape) for o_ in outs]

    order = ["ada_w", "ada_b", "norm1_g", "norm2_g", "ab_w_in", "a_conv_w", "b_mix_w", "b_scale", "ab_w_out", "cd_w_in", "c_q_norm_g",
             "c_w_uq", "c_kv_norm_g", "c_w_ukv", "d_ln_g", "d_ln_b", "d_w_s", "d_b_s", "cd_w_out", "ffn_w_up", "ffn_conv_w",
             "ffn_w_down", "final_norm_g"]
    return (loss, dx0[None], *[res[n][0] for n in order], *[res[n][1] for n in order], *[res[n][2] for n in order],
            *[res[n][3] for n in order])
```

```python
import functools
import math

import jax
import jax.numpy as jnp
from jax import lax
from jax.experimental import pallas as pl
from jax.experimental.pallas import tpu as pltpu

F32 = jnp.float32
BF16 = jnp.bfloat16
_MXU_DTYPE = BF16
WIRE_DTYPE = BF16
ACT_DTYPE = BF16
_VMEM_LIMIT = 56 * 2 ** 20
N_DEV = 8
EPS = 1e-6
POOL_WINDOWS = (2, 4, 8, 16)
ATTN_SCALE = (64 + 32) ** -0.5
ADAM_LR, ADAM_B1, ADAM_B2, ADAM_EPS, ADAM_WD, ADAM_STEP = 0.001, 0.9, 0.999, 1e-08, 0.01, 10
MESH = pl.DeviceIdType.MESH
ANY = pl.BlockSpec(memory_space=pl.ANY)


def _cp(*sem):
    return pltpu.CompilerParams(dimension_semantics=sem, vmem_limit_bytes=_VMEM_LIMIT)


def _dot(a, b, contract):
    dn = {"nn": (((1,), (0,)), ((), ())), "nt": (((1,), (1,)), ((), ())), "tn": (((0,), (0,)), ((), ()))}[contract]
    return lax.dot_general(a.astype(_MXU_DTYPE), b.astype(_MXU_DTYPE), dn, preferred_element_type=F32)


def _my_position():
    x, y, c = lax.axis_index("x"), lax.axis_index("y"), lax.axis_index("c")
    return x, y, c, 4 * x + 2 * y + c


def _exchange(name, groups, scatter):
    flat = [a for g in groups for a in g]
    n_in, n_grp = len(flat), len(groups)
    out_shapes = []
    for g in groups:
        slab = g[0].shape[1:] if scatter else g[0].shape
        out_shapes.append(jax.ShapeDtypeStruct((N_DEV, len(g)) + tuple(slab), g[0].dtype))

    def body(*refs):
        ins, outs = refs[:n_in], refs[n_in:n_in + n_grp]
        send_sems, recv_sems, local_sems = refs[n_in + n_grp:]
        x, y, c, me = _my_position()
        i = 0
        for gi, g in enumerate(groups):
            for l in range(len(g)):
                src = ins[i]
                i += 1
                pltpu.make_async_copy(src.at[me] if scatter else src, outs[gi].at[me, l], local_sems.at[gi]).start()
                for k in range(1, N_DEV):
                    px = 1 - x if k & 4 else x
                    py = 1 - y if k & 2 else y
                    pc = 1 - c if k & 1 else c
                    peer = 4 * px + 2 * py + pc
                    pltpu.make_async_remote_copy(
                        src_ref=src.at[peer] if scatter else src, dst_ref=outs[gi].at[me, l],
                        send_sem=send_sems.at[gi], recv_sem=recv_sems.at[gi],
                        device_id=(px, py, pc), device_id_type=MESH).start()
        for gi in range(n_grp):
            mine = outs[gi].at[me]
            pltpu.make_async_copy(mine, mine, local_sems.at[gi]).wait()
            seven = outs[gi].at[pl.ds(0, N_DEV - 1)]
            w = pltpu.make_async_remote_copy(src_ref=seven, dst_ref=seven, send_sem=send_sems.at[gi],
                                             recv_sem=recv_sems.at[gi], device_id=(x, y, c), device_id_type=MESH)
            w.wait_send()
            w.wait_recv()

    return pl.pallas_call(
        body, name=name, out_shape=tuple(out_shapes),
        in_specs=[ANY] * n_in, out_specs=tuple([ANY] * n_grp),
        scratch_shapes=[pltpu.SemaphoreType.DMA((n_grp,)), pltpu.SemaphoreType.DMA((n_grp,)),
                        pltpu.SemaphoreType.DMA((n_grp,))],
        compiler_params=pltpu.CompilerParams(has_side_effects=True),
    )(*flat)


HBM_SPEC = pl.BlockSpec(memory_space=pltpu.HBM)
SEM_SPEC = pl.BlockSpec(memory_space=pltpu.SEMAPHORE)
EFFECT = pltpu.SideEffectType.DATAFLOW_SIDE_EFFECTING


def _put_mine(name, srcs, scatter, me):
    n = len(srcs)
    slabs = [tuple(s.shape[1:] if sc else s.shape) for s, sc in zip(srcs, scatter)]

    def body(me_ref, *refs):
        for i in range(n):
            refs[n + i][...] = refs[i][...]

    def at_me(slab):
        return pl.BlockSpec((None,) + slab, lambda g, me_ref, nd=len(slab): (me_ref[0],) + (0,) * nd)

    def whole(slab):
        return pl.BlockSpec(slab, lambda g, me_ref, nd=len(slab): (0,) * nd)

    return pl.pallas_call(
        body, name=name,
        grid_spec=pltpu.PrefetchScalarGridSpec(
            num_scalar_prefetch=1, grid=(1,),
            in_specs=[at_me(slab) if sc else whole(slab) for slab, sc in zip(slabs, scatter)],
            out_specs=[at_me(slab) for slab in slabs]),
        out_shape=[jax.ShapeDtypeStruct((N_DEV,) + slab, s.dtype) for slab, s in zip(slabs, srcs)],
        compiler_params=_cp("arbitrary"))(me.reshape(1), *srcs)


def _exchange_start(name, srcs, scatter, after, me):
    n = len(srcs)
    scatter = list(scatter) if isinstance(scatter, (list, tuple)) else [scatter] * n
    lands = _put_mine(name + "_mine", srcs, scatter, me)
    srcs = [pltpu.with_memory_space_constraint(a, pltpu.HBM) for a in srcs]
    lands = [pltpu.with_memory_space_constraint(a, pltpu.HBM) for a in lands]

    def body(*refs):
        ins, land = refs[:n], refs[n:2 * n]
        send_sems, recv_sems, token = refs[2 * n + 1], refs[2 * n + 2], refs[-1]
        x, y, c, me_in = _my_position()
        for i in range(n):
            for k in range(1, N_DEV):
                px = 1 - x if k & 4 else x
                py = 1 - y if k & 2 else y
                pc = 1 - c if k & 1 else c
                pltpu.make_async_remote_copy(
                    src_ref=ins[i].at[4 * px + 2 * py + pc] if scatter[i] else ins[i], dst_ref=land[i].at[me_in],
                    send_sem=send_sems.at[i], recv_sem=recv_sems.at[i],
                    device_id=(px, py, pc), device_id_type=MESH).start()
        token[...] = jnp.zeros_like(token)

    outs = pl.pallas_call(
        body, name=name,
        out_shape=(pltpu.SemaphoreType.DMA((n,)), pltpu.SemaphoreType.DMA((n,)),
                   *[pltpu.HBM(a.shape, a.dtype) for a in srcs], *[pltpu.HBM(a.shape, a.dtype) for a in lands],
                   jax.ShapeDtypeStruct((8, 128), F32)),
        in_specs=[HBM_SPEC] * (2 * n) + [ANY],
        out_specs=(SEM_SPEC, SEM_SPEC, *[HBM_SPEC] * (2 * n), pl.BlockSpec(memory_space=pltpu.VMEM)),
        input_output_aliases={i: 2 + i for i in range(2 * n)},
        compiler_params=pltpu.CompilerParams(has_side_effects=EFFECT),
    )(*srcs, *lands, after)
    return (outs[0], outs[1], outs[2:2 + n], outs[2 + n:2 + 2 * n]), outs[-1]


def _exchange_wait(name, handle, after, which=None):
    send_sems, recv_sems, srcs, lands = handle
    which = list(range(len(srcs))) if which is None else list(which)
    srcs, lands = [srcs[i] for i in which], [lands[i] for i in which]
    n = len(srcs)

    def body(*refs):
        land, send_ref, recv_ref = refs[n:2 * n], refs[2 * n], refs[2 * n + 1]
        x, y, c, _ = _my_position()
        for k, i in enumerate(which):
            seven = land[k].at[pl.ds(0, N_DEV - 1)]
            w = pltpu.make_async_remote_copy(src_ref=seven, dst_ref=seven, send_sem=send_ref.at[i], recv_sem=recv_ref.at[i],
                                             device_id=(x, y, c), device_id_type=MESH)
            w.wait_send()
            w.wait_recv()

    outs = pl.pallas_call(
        body, name=name,
        out_shape=(*[pltpu.HBM(a.shape, a.dtype) for a in srcs], *[pltpu.HBM(a.shape, a.dtype) for a in lands]),
        in_specs=[HBM_SPEC] * (2 * n) + [SEM_SPEC, SEM_SPEC, ANY],
        out_specs=tuple([HBM_SPEC] * (2 * n)),
        input_output_aliases={i: i for i in range(2 * n)},
        compiler_params=pltpu.CompilerParams(has_side_effects=EFFECT),
    )(*srcs, *lands, send_sems, recv_sems, after)
    return outs[n:]


def _other_chips(x, y):
    return [(1 - x, y), (x, 1 - y), (1 - x, 1 - y)]


def _hier_gather_start(name, srcs, after, me):
    n = len(srcs)
    lands = _put_mine(name + "_mine", srcs, [False] * n, me)
    srcs = [pltpu.with_memory_space_constraint(a, pltpu.HBM) for a in srcs]
    lands = [pltpu.with_memory_space_constraint(a, pltpu.HBM) for a in lands]

    def body(*refs):
        ins, land = refs[:n], refs[n:2 * n]
        ici_send, ici_recv, d2d_send, d2d_recv = refs[2 * n + 1:2 * n + 5]
        token = refs[-1]
        x, y, c, me_in = _my_position()
        for i in range(n):
            pltpu.make_async_remote_copy(src_ref=ins[i], dst_ref=land[i].at[me_in], send_sem=d2d_send.at[i], recv_sem=d2d_recv.at[i],
                                         device_id=(x, y, 1 - c), device_id_type=MESH).start()
            for px, py in _other_chips(x, y):
                pltpu.make_async_remote_copy(src_ref=ins[i], dst_ref=land[i].at[me_in], send_sem=ici_send.at[i],
                                             recv_sem=ici_recv.at[i], device_id=(px, py, c), device_id_type=MESH).start()
        token[...] = jnp.zeros_like(token)

    sem = pltpu.SemaphoreType.DMA((n,))
    outs = pl.pallas_call(
        body, name=name,
        out_shape=(sem, sem, sem, sem, *[pltpu.HBM(a.shape, a.dtype) for a in srcs], *[pltpu.HBM(a.shape, a.dtype) for a in lands],
                   jax.ShapeDtypeStruct((8, 128), F32)),
        in_specs=[HBM_SPEC] * (2 * n) + [ANY],
        out_specs=(SEM_SPEC,) * 4 + (HBM_SPEC,) * (2 * n) + (pl.BlockSpec(memory_space=pltpu.VMEM),),
        input_output_aliases={i: 4 + i for i in range(2 * n)},
        compiler_params=pltpu.CompilerParams(has_side_effects=EFFECT),
    )(*srcs, *lands, after)
    return (outs[:4], outs[4:4 + n], outs[4 + n:4 + 2 * n]), outs[-1]


def _hier_gather_forward(name, handle, after):
    sems, srcs, lands = handle
    n = len(srcs)

    def body(*refs):
        land = refs[n:2 * n]
        ici_send, ici_recv, d2d_send, d2d_recv = refs[2 * n:2 * n + 4]
        x, y, c, _ = _my_position()
        for i in range(n):
            three = land[i].at[pl.ds(0, 3)]
            pltpu.make_async_remote_copy(src_ref=three, dst_ref=three, send_sem=ici_send.at[i], recv_sem=ici_recv.at[i],
                                         device_id=(x, y, c), device_id_type=MESH).wait_recv()
            for px, py in _other_chips(x, y):
                slab = land[i].at[4 * px + 2 * py + c]
                pltpu.make_async_remote_copy(src_ref=slab, dst_ref=slab, send_sem=d2d_send.at[i], recv_sem=d2d_recv.at[i],
                                             device_id=(x, y, 1 - c), device_id_type=MESH).start()

    outs = pl.pallas_call(
        body, name=name,
        out_shape=(*[pltpu.HBM(a.shape, a.dtype) for a in srcs], *[pltpu.HBM(a.shape, a.dtype) for a in lands]),
        in_specs=[HBM_SPEC] * (2 * n) + [SEM_SPEC] * 4 + [ANY],
        out_specs=tuple([HBM_SPEC] * (2 * n)),
        input_output_aliases={i: i for i in range(2 * n)},
        compiler_params=pltpu.CompilerParams(has_side_effects=EFFECT),
    )(*srcs, *lands, *sems, after)
    return (sems, outs[:n], outs[n:])


def _hier_gather_wait(name, handle, after):
    sems, srcs, lands = handle
    n = len(srcs)

    def body(*refs):
        land = refs[n:2 * n]
        ici_send, ici_recv, d2d_send, d2d_recv = refs[2 * n:2 * n + 4]
        x, y, c, _ = _my_position()
        for i in range(n):
            three, four = land[i].at[pl.ds(0, 3)], land[i].at[pl.ds(0, 4)]
            pltpu.make_async_remote_copy(src_ref=three, dst_ref=three, send_sem=ici_send.at[i], recv_sem=ici_recv.at[i],
                                         device_id=(x, y, c), device_id_type=MESH).wait_send()
            w = pltpu.make_async_remote_copy(src_ref=four, dst_ref=four, send_sem=d2d_send.at[i], recv_sem=d2d_recv.at[i],
                                             device_id=(x, y, c), device_id_type=MESH)
            w.wait_send()
            w.wait_recv()

    outs = pl.pallas_call(
        body, name=name,
        out_shape=(*[pltpu.HBM(a.shape, a.dtype) for a in srcs], *[pltpu.HBM(a.shape, a.dtype) for a in lands]),
        in_specs=[HBM_SPEC] * (2 * n) + [SEM_SPEC] * 4 + [ANY],
        out_specs=tuple([HBM_SPEC] * (2 * n)),
        input_output_aliases={i: i for i in range(2 * n)},
        compiler_params=pltpu.CompilerParams(has_side_effects=EFFECT),
    )(*srcs, *lands, *sems, after)
    return outs[n:]


def _pack(arrs):
    flat = jnp.concatenate([a.reshape(-1).astype(F32) for a in arrs])
    n = flat.shape[0]
    rows = -(-n // 1024) * 8
    return jnp.pad(flat, (0, rows * 128 - n)).reshape(rows, 128)


def _unpack(buf, shapes, lead=()):
    flat = buf.reshape(lead + (-1,))
    out, off = [], 0
    for s in shapes:
        n = math.prod(s)
        out.append(flat[..., off:off + n].reshape(lead + tuple(s)))
        off += n
    return out


def _mm(name, a, a_spec, b, b_spec, out_sds, o_spec, grid, contract, nk=1, stacked=0):
    o_blk = tuple(d for d in o_spec.block_shape if d is not None)

    def body(a_ref, b_ref, o_ref, *acc):
        if stacked:
            r = _dot(a_ref[0], b_ref[0], contract)
            for q in range(1, stacked):
                r = r + _dot(a_ref[q], b_ref[q], contract)
        else:
            r = _dot(a_ref[...], b_ref[...], contract)
        if nk == 1:
            o_ref[...] = r.astype(o_ref.dtype)
        else:
            k = pl.program_id(len(grid) - 1)

            @pl.when(k == 0)
            def _():
                acc[0][...] = r

            @pl.when(k > 0)
            def _():
                acc[0][...] += r

            @pl.when(k == nk - 1)
            def _():
                o_ref[...] = acc[0][...].astype(o_ref.dtype)

    sem = ("parallel",) * (len(grid) - 1) + (("arbitrary",) if nk > 1 else ("parallel",))
    return pl.pallas_call(
        body, name=name, out_shape=out_sds, grid=grid, in_specs=[a_spec, b_spec], out_specs=o_spec,
        scratch_shapes=[pltpu.VMEM(o_blk, F32)] if nk > 1 else [], compiler_params=_cp(*sem))(a, b)


def _tile(n, want):
    t = min(n, want)
    assert n % t == 0, (n, t)
    return t


def _mm_nn(name, a, b, out_dtype=F32, tm=512, tn=512):
    (M, K), N = a.shape, b.shape[1]
    tm, tn = _tile(M, tm), _tile(N, tn)
    return _mm(name, a, pl.BlockSpec((tm, K), lambda i, j: (i, 0)), b, pl.BlockSpec((K, tn), lambda i, j: (0, j)),
               jax.ShapeDtypeStruct((M, N), out_dtype), pl.BlockSpec((tm, tn), lambda i, j: (i, j)),
               (M // tm, N // tn), "nn")


def _mm_nt(name, a, b, out_dtype=F32, tm=512, tn=512):
    (M, K), N = a.shape, b.shape[0]
    tm, tn = _tile(M, tm), _tile(N, tn)
    return _mm(name, a, pl.BlockSpec((tm, K), lambda i, j: (i, 0)), b, pl.BlockSpec((tn, K), lambda i, j: (j, 0)),
               jax.ShapeDtypeStruct((M, N), out_dtype), pl.BlockSpec((tm, tn), lambda i, j: (i, j)),
               (M // tm, N // tn), "nt")


def _mm_tn(name, a, b, out_dtype=F32, tm=512, tn=512):
    (K, M), N = a.shape, b.shape[1]
    tm, tn = _tile(M, tm), _tile(N, tn)
    return _mm(name, a, pl.BlockSpec((K, tm), lambda i, j: (0, i)), b, pl.BlockSpec((K, tn), lambda i, j: (0, j)),
               jax.ShapeDtypeStruct((M, N), out_dtype), pl.BlockSpec((tm, tn), lambda i, j: (i, j)),
               (M // tm, N // tn), "tn")


def _mm_cols(name, a, w, out_dtype=F32, tm=512):
    (M, K), (J, _, n) = a.shape, w.shape
    tm = _tile(M, tm)
    return _mm(name, a, pl.BlockSpec((tm, K), lambda j, i: (i, 0)), w, pl.BlockSpec((None, K, n), lambda j, i: (j, 0, 0)),
               jax.ShapeDtypeStruct((J, M, n), out_dtype), pl.BlockSpec((None, tm, n), lambda j, i: (j, i, 0)),
               (J, M // tm), "nn")


def _mm_cols_dx(name, d, w, out_dtype=F32, tm=512, jb=None):
    (J, M, n), K = d.shape, w.shape[1]
    tm, jb = _tile(M, tm), J if jb is None else jb
    return _mm(name, d, pl.BlockSpec((jb, tm, n), lambda i, j: (j, i, 0)), w, pl.BlockSpec((jb, K, n), lambda i, j: (j, 0, 0)),
               jax.ShapeDtypeStruct((M, K), out_dtype), pl.BlockSpec((tm, K), lambda i, j: (i, 0)),
               (M // tm, J // jb), "nt", nk=J // jb, stacked=jb)


def _mm_cols_dw(name, a, d, out_dtype=F32, tk=512):
    (M, K), (J, _, n) = a.shape, d.shape
    tk = _tile(K, tk)
    return _mm(name, a, pl.BlockSpec((M, tk), lambda j, i: (0, i)), d, pl.BlockSpec((None, M, n), lambda j, i: (j, 0, 0)),
               jax.ShapeDtypeStruct((J, K, n), out_dtype), pl.BlockSpec((None, tk, n), lambda j, i: (j, i, 0)),
               (J, K // tk), "tn")


def _mm_cols_dwt(name, a, d, out_dtype=F32, tk=512):
    (M, K), (J, _, n) = a.shape, d.shape
    tk = _tile(K, tk)
    return _mm(name, d, pl.BlockSpec((None, M, n), lambda j, i: (j, 0, 0)), a, pl.BlockSpec((M, tk), lambda j, i: (0, i)),
               jax.ShapeDtypeStruct((J, n, K), out_dtype), pl.BlockSpec((None, n, tk), lambda j, i: (j, 0, i)),
               (J, K // tk), "tn")


def _mm_rows_resid(name, a, w, resid, gate, tm=512):
    (Q, M, k), N = a.shape, w.shape[2]
    tm = _tile(M, tm)

    def body(a_ref, w_ref, r_ref, g_ref, y_ref, x_ref):
        y = _dot(a_ref[0], w_ref[0], "nn")
        for q in range(1, Q):
            y = y + _dot(a_ref[q], w_ref[q], "nn")
        y_ref[...] = y.astype(y_ref.dtype)
        x_ref[...] = r_ref[...] + g_ref[...] * y

    return pl.pallas_call(
        body, name=name, grid=(M // tm,),
        out_shape=(jax.ShapeDtypeStruct((M, N), ACT_DTYPE), jax.ShapeDtypeStruct((M, N), F32)),
        in_specs=[pl.BlockSpec((Q, tm, k), lambda i: (0, i, 0)), pl.BlockSpec((Q, k, N), lambda i: (0, 0, 0)),
                  pl.BlockSpec((tm, N), lambda i: (i, 0)), pl.BlockSpec((1, N), lambda i: (0, 0))],
        out_specs=(pl.BlockSpec((tm, N), lambda i: (i, 0)), pl.BlockSpec((tm, N), lambda i: (i, 0))),
        compiler_params=_cp("parallel"))(a, w, resid, gate)


def _mm_rows_dx(name, d, w, out_dtype=F32, tm=512):
    (M, N), (Q, k, _) = d.shape, w.shape
    tm = _tile(M, tm)
    return _mm(name, d, pl.BlockSpec((tm, N), lambda q, i: (i, 0)), w, pl.BlockSpec((None, k, N), lambda q, i: (q, 0, 0)),
               jax.ShapeDtypeStruct((Q, M, k), out_dtype), pl.BlockSpec((None, tm, k), lambda q, i: (q, i, 0)),
               (Q, M // tm), "nt")


def _mm_rows_dw(name, a, d, out_dtype=F32, tn=512):
    (Q, M, k), N = a.shape, d.shape[1]
    tn = _tile(N, tn)
    return _mm(name, a, pl.BlockSpec((None, M, k), lambda q, j: (q, 0, 0)), d, pl.BlockSpec((M, tn), lambda q, j: (0, j)),
               jax.ShapeDtypeStruct((Q, k, N), out_dtype), pl.BlockSpec((None, k, tn), lambda q, j: (q, 0, j)),
               (Q, N // tn), "tn")


def _silu(v):
    return v * jax.nn.sigmoid(v)


def _ada_fwd(c16, ada_w):
    L, D, n = ada_w.shape

    def body(c_ref, w_ref, o_ref):
        o_ref[...] = _dot(_silu(c_ref[...]), w_ref[...], "nn")

    return pl.pallas_call(
        body, name="ada_fwd", grid=(L,), out_shape=jax.ShapeDtypeStruct((L, 16, n), F32),
        in_specs=[pl.BlockSpec((16, D), lambda l: (0, 0)), pl.BlockSpec((None, D, n), lambda l: (l, 0, 0))],
        out_specs=pl.BlockSpec((None, 16, n), lambda l: (l, 0, 0)), compiler_params=_cp("parallel"))(c16, ada_w)


def _ada_bwd(c16, dmod16):
    L, _, n = dmod16.shape
    D = c16.shape[1]

    def body(c_ref, d_ref, o_ref):
        o_ref[...] = _dot(_silu(c_ref[...]), d_ref[...], "tn")

    return pl.pallas_call(
        body, name="ada_bwd", grid=(L,), out_shape=jax.ShapeDtypeStruct((L, D, n), F32),
        in_specs=[pl.BlockSpec((16, D), lambda l: (0, 0)), pl.BlockSpec((None, 16, n), lambda l: (l, 0, 0))],
        out_specs=pl.BlockSpec((None, D, n), lambda l: (l, 0, 0)), compiler_params=_cp("parallel"))(c16, dmod16)


def _row_spec(tr, n):
    return pl.BlockSpec((tr, n), lambda i: (i, 0))


def _vec_spec(n):
    return pl.BlockSpec((1, n), lambda i: (0, 0))


def _rmsmod_fwd(name, x, g, sc, sh, after, tr=256):
    S, D = x.shape

    def body(x_ref, g_ref, sc_ref, sh_ref, after_ref, h_ref):
        xv = x_ref[...]
        rstd = lax.rsqrt(jnp.mean(xv * xv, axis=-1, keepdims=True) + EPS)
        y = xv * rstd * g_ref[...]
        h_ref[...] = (y * (1.0 + sc_ref[...]) + sh_ref[...]).astype(h_ref.dtype)

    return pl.pallas_call(
        body, name=name, grid=(S // tr,), out_shape=jax.ShapeDtypeStruct((S, D), _MXU_DTYPE),
        in_specs=[_row_spec(tr, D), _vec_spec(D), _vec_spec(D), _vec_spec(D), ANY], out_specs=_row_spec(tr, D),
        compiler_params=_cp("parallel"))(x, g, sc, sh, after)


def _acc_rows(ref, val, first):
    s = jnp.sum(val, axis=0, keepdims=True)

    @pl.when(first)
    def _():
        ref[...] = s

    @pl.when(jnp.logical_not(first))
    def _():
        ref[...] += s


def _gate_bwd_tail(dx, y_ref, gate_ref, dy_ref, dgate_ref, first):
    dy_ref[...] = (gate_ref[...] * dx).astype(dy_ref.dtype)
    _acc_rows(dgate_ref, dx * y_ref[...].astype(F32), first)


def _rmsmod_bwd(name, x, g, sc, dh, dres, after, y=None, gate=None, tr=256):
    S, D = x.shape
    tail = y is not None

    def body(x_ref, g_ref, sc_ref, dh_ref, dres_ref, after_ref, *rest):
        (y_ref, gate_ref), rest = (rest[:2], rest[2:]) if tail else ((None, None), rest)
        dx_ref, dg_ref, dsc_ref, dsh_ref = rest[:4]
        first = pl.program_id(0) == 0
        xv, dh_v, gv = x_ref[...], dh_ref[...], g_ref[...]
        rstd = lax.rsqrt(jnp.mean(xv * xv, axis=-1, keepdims=True) + EPS)
        xhat = xv * rstd
        _acc_rows(dsh_ref, dh_v, first)
        _acc_rows(dsc_ref, dh_v * (xhat * gv), first)
        dyg = dh_v * (1.0 + sc_ref[...])
        _acc_rows(dg_ref, dyg * xhat, first)
        dxhat = dyg * gv
        dx = dres_ref[...] + rstd * (dxhat - xhat * jnp.mean(dxhat * xhat, axis=-1, keepdims=True))
        dx_ref[...] = dx
        if tail:
            _gate_bwd_tail(dx, y_ref, gate_ref, rest[4], rest[5], first)

    vec = jax.ShapeDtypeStruct((1, D), F32)
    return pl.pallas_call(
        body, name=name, grid=(S // tr,),
        out_shape=(jax.ShapeDtypeStruct((S, D), F32), vec, vec, vec) + ((jax.ShapeDtypeStruct((S, D), _MXU_DTYPE), vec) if tail else ()),
        in_specs=[_row_spec(tr, D), _vec_spec(D), _vec_spec(D), _row_spec(tr, D), _row_spec(tr, D), ANY]
        + ([_row_spec(tr, D), _vec_spec(D)] if tail else []),
        out_specs=(_row_spec(tr, D), _vec_spec(D), _vec_spec(D), _vec_spec(D)) + ((_row_spec(tr, D), _vec_spec(D)) if tail else ()),
        compiler_params=_cp("arbitrary"))(x, g, sc, dh, dres, after, *((y, gate) if tail else ()))


def _loss_head(x, g, target, y, gate, tr=256):
    S, D = x.shape

    def body(x_ref, g_ref, t_ref, y_ref, gate_ref, loss_ref, dx_ref, dg_ref, dy_ref, dgate_ref):
        first = pl.program_id(0) == 0
        xv, gv = x_ref[...], g_ref[...]
        rstd = lax.rsqrt(jnp.mean(xv * xv, axis=-1, keepdims=True) + EPS)
        xhat = xv * rstd
        err = xhat * gv - t_ref[...]
        part = 0.5 * jnp.sum(jnp.mean(err * err, axis=-1, keepdims=True), axis=0, keepdims=True)

        @pl.when(first)
        def _():
            loss_ref[...] = part

        @pl.when(jnp.logical_not(first))
        def _():
            loss_ref[...] += part

        dout = err * (1.0 / D)
        _acc_rows(dg_ref, dout * xhat, first)
        dxhat = dout * gv
        dx = rstd * (dxhat - xhat * jnp.mean(dxhat * xhat, axis=-1, keepdims=True))
        dx_ref[...] = dx
        _gate_bwd_tail(dx, y_ref, gate_ref, dy_ref, dgate_ref, first)

    vec = jax.ShapeDtypeStruct((1, D), F32)
    return pl.pallas_call(
        body, name="loss_head", grid=(S // tr,),
        out_shape=(jax.ShapeDtypeStruct((1, 1), F32), jax.ShapeDtypeStruct((S, D), F32), vec,
                   jax.ShapeDtypeStruct((S, D), _MXU_DTYPE), vec),
        in_specs=[_row_spec(tr, D), _vec_spec(D), _row_spec(tr, D), _row_spec(tr, D), _vec_spec(D)],
        out_specs=(pl.BlockSpec((1, 1), lambda i: (0, 0)), _row_spec(tr, D), _vec_spec(D), _row_spec(tr, D), _vec_spec(D)),
        compiler_params=_cp("arbitrary"))(x, g, target, y, gate)


def _gate_bwd(name, dx, y, gate, tr=256):
    S, D = dx.shape

    def body(dx_ref, y_ref, g_ref, dy_ref, dg_ref):
        dxv = dx_ref[...]
        dy_ref[...] = (g_ref[...] * dxv).astype(dy_ref.dtype)
        _acc_rows(dg_ref, dxv * y_ref[...], pl.program_id(0) == 0)

    return pl.pallas_call(
        body, name=name, grid=(S // tr,),
        out_shape=(jax.ShapeDtypeStruct((S, D), _MXU_DTYPE), jax.ShapeDtypeStruct((1, D), F32)),
        in_specs=[_row_spec(tr, D), _row_spec(tr, D), _vec_spec(D)], out_specs=(_row_spec(tr, D), _vec_spec(D)),
        compiler_params=_cp("arbitrary"))(dx, y, gate)


def _shift_down(v, k):
    t = lax.broadcasted_iota(jnp.int32, v.shape, 0)
    return jnp.where(t >= k, pltpu.roll(v, k, axis=0), 0.0)


def _shift_up(v, k):
    n = v.shape[0]
    t = lax.broadcasted_iota(jnp.int32, v.shape, 0)
    return jnp.where(t < n - k, pltpu.roll(v, n - k, axis=0), 0.0)


def _window_sum(p, w, shift):
    s, k = p, 1
    while k < w:
        s = s + shift(s, k)
        k *= 2
    return s


def _pool_count(shape, w):
    t = lax.broadcasted_iota(jnp.int32, shape, 0)
    return jnp.minimum(t + 1, w).astype(F32)


def _ab_specs(S):
    zs = [pl.BlockSpec((None, S, 128), functools.partial(lambda g, q: (2 * q + g // 2, 0, g % 2), q=q)) for q in range(4)]
    return zs


def _ab_mix_fwd(z8, conv_w, mix_w, scale):
    S = z8.shape[1]

    def body(b_ref, c_ref, a_ref, p_ref, w_ref, mix_ref, sc_ref, y_ref):
        g = pl.program_id(0)
        cg = c_ref[...] * a_ref[...]
        w = w_ref[...]
        conv = w[0:1] * _shift_down(cg, 2) + w[1:2] * _shift_down(cg, 1) + w[2:3] * cg
        y_ref[0] = (b_ref[...] * conv).astype(y_ref.dtype)
        for gg, win in enumerate(POOL_WINDOWS):
            @pl.when(g == gg)
            def _(win=win):
                p = p_ref[...]
                pooled = _window_sum(p, win, _shift_down) / _pool_count(p.shape, win) - p
                y_ref[1] = (_dot(pooled, mix_ref[...], "nn") * sc_ref[...]).astype(y_ref.dtype)

    return pl.pallas_call(
        body, name="ab_mix_fwd", grid=(4,), out_shape=jax.ShapeDtypeStruct((2, S, 512), _MXU_DTYPE),
        in_specs=_ab_specs(S) + [pl.BlockSpec((3, 128), lambda g: (0, g)), pl.BlockSpec((None, 128, 128), lambda g: (g, 0, 0)),
                                 pl.BlockSpec((1, 128), lambda g: (0, g))],
        out_specs=pl.BlockSpec((2, S, 128), lambda g: (0, 0, g)), compiler_params=_cp("parallel"))(z8, z8, z8, z8, conv_w, mix_w, scale)


def _ab_mix_bwd(z8, dycat2, conv_w, mix_w, scale, after):
    S = z8.shape[1]

    def body(b_ref, c_ref, a_ref, p_ref, dy_ref, w_ref, mix_ref, sc_ref, after_ref, dz_ref, dw_ref, dmix_ref, dsc_ref):
        g = pl.program_id(0)
        bv, cv, av, w = b_ref[...], c_ref[...], a_ref[...], w_ref[...]
        dya = dy_ref[0]
        cg = cv * av
        cg1, cg2 = _shift_down(cg, 1), _shift_down(cg, 2)
        conv = w[0:1] * cg2 + w[1:2] * cg1 + w[2:3] * cg
        dz_ref[0] = (dya * conv).astype(dz_ref.dtype)
        dconv = dya * bv
        dcg = w[2:3] * dconv + w[1:2] * _shift_up(dconv, 1) + w[0:1] * _shift_up(dconv, 2)
        dz_ref[1] = (dcg * av).astype(dz_ref.dtype)
        dz_ref[2] = (dcg * cv).astype(dz_ref.dtype)
        dw_ref[0:1, :] = jnp.sum(dconv * cg2, axis=0, keepdims=True)
        dw_ref[1:2, :] = jnp.sum(dconv * cg1, axis=0, keepdims=True)
        dw_ref[2:3, :] = jnp.sum(dconv * cg, axis=0, keepdims=True)
        for gg, win in enumerate(POOL_WINDOWS):
            @pl.when(g == gg)
            def _(win=win):
                p, dyb, mix = p_ref[...], dy_ref[1], mix_ref[...]
                cnt = _pool_count(p.shape, win)
                pooled = _window_sum(p, win, _shift_down) / cnt - p
                dsc_ref[...] = jnp.sum(dyb * _dot(pooled, mix, "nn"), axis=0, keepdims=True)
                dmixed = dyb * sc_ref[...]
                dmix_ref[...] = _dot(pooled, dmixed, "tn")
                dpooled = _dot(dmixed, mix, "nt")
                dz_ref[3] = (_window_sum(dpooled / cnt, win, _shift_up) - dpooled).astype(dz_ref.dtype)

    return pl.pallas_call(
        body, name="ab_mix_bwd", grid=(4,),
        out_shape=(jax.ShapeDtypeStruct((4, 2, S, 256), _MXU_DTYPE), jax.ShapeDtypeStruct((3, 512), F32),
                   jax.ShapeDtypeStruct((4, 128, 128), F32), jax.ShapeDtypeStruct((1, 512), F32)),
        in_specs=_ab_specs(S) + [pl.BlockSpec((2, S, 128), lambda g: (0, 0, g)), pl.BlockSpec((3, 128), lambda g: (0, g)),
                                 pl.BlockSpec((None, 128, 128), lambda g: (g, 0, 0)), pl.BlockSpec((1, 128), lambda g: (0, g)), ANY],
        out_specs=(pl.BlockSpec((4, None, S, 128), lambda g: (0, g // 2, 0, g % 2)), pl.BlockSpec((3, 128), lambda g: (0, g)),
                   pl.BlockSpec((None, 128, 128), lambda g: (g, 0, 0)), pl.BlockSpec((1, 128), lambda g: (0, g))),
        compiler_params=_cp("parallel"))(z8, z8, z8, z8, dycat2, conv_w, mix_w, scale, after)


HALO = 16


def _ffn_specs(S, n, tr):
    nb = S // HALO
    tile = pl.BlockSpec((2, None, tr, n), lambda j, i: (0, j, i, 0))
    prev = pl.BlockSpec((2, None, HALO, n), lambda j, i: (0, j, jnp.maximum(i * (tr // HALO) - 1, 0), 0))
    nxt = pl.BlockSpec((2, None, HALO, n), lambda j, i: (0, j, jnp.minimum((i + 1) * (tr // HALO), nb - 1), 0))
    cw = pl.BlockSpec((2, None, 3, n), lambda j, i: (0, j, 0, 0))
    return tile, prev, nxt, cw


def _shifted_rows(ext, lo, rows):
    ext = ext.astype(F32)
    return pltpu.roll(ext, 1, axis=0)[lo:lo + rows], pltpu.roll(ext, 2, axis=0)[lo:lo + rows]


def _ffn_gate_fwd(name, u24, cw24, tr=256):
    _, J, S, n = u24.shape
    tile, prev, _, cw = _ffn_specs(S, n, tr)

    def body(u_ref, up_ref, w_ref, a_ref, z_ref):
        keep = (pl.program_id(1) > 0).astype(u_ref.dtype)
        z = []
        for h in range(2):
            ext = jnp.concatenate([up_ref[h] * keep, u_ref[h]], axis=0)
            x1, x2 = _shifted_rows(ext, HALO, tr)
            w = w_ref[h]
            zh = w[0:1] * x2 + w[1:2] * x1 + w[2:3] * u_ref[h].astype(F32)
            z_ref[h] = zh.astype(z_ref.dtype)
            z.append(zh)
        a_ref[...] = (_silu(z[0]) * z[1]).astype(a_ref.dtype)

    return pl.pallas_call(
        body, name=name, grid=(J, S // tr),
        out_shape=(jax.ShapeDtypeStruct((J, S, n), _MXU_DTYPE), jax.ShapeDtypeStruct((2, J, S, n), ACT_DTYPE)),
        in_specs=[tile, prev, cw], out_specs=(pl.BlockSpec((None, tr, n), lambda j, i: (j, i, 0)), tile),
        compiler_params=_cp("parallel", "parallel"))(u24, u24, cw24)


def _ffn_gate_bwd(name, u24, z24, cw24, da4, w_up24, after, tr=256):
    _, J, S, n = u24.shape
    K = w_up24.shape[2]
    nb = S // HALO
    tile = pl.BlockSpec((2, None, tr, n), lambda i, j: (0, j, i, 0))
    nxt = pl.BlockSpec((2, None, HALO, n), lambda i, j: (0, j, jnp.minimum((i + 1) * (tr // HALO), nb - 1), 0))
    whole = lambda shape: pl.BlockSpec(shape, lambda i, j: (0,) * len(shape))

    def body(u_ref, z_ref, zn_ref, cw_ref, da_ref, dan_ref, wup_ref, after_ref, du_ref, dcw_ref, dh_ref, acc_ref):
        i, j = pl.program_id(0), pl.program_id(1)
        first = i == 0
        keep_next = (i < S // tr - 1).astype(F32)
        w = [cw_ref[h, j] for h in range(2)]
        m = tr + HALO
        zg, zu = [jnp.concatenate([z_ref[h], zn_ref[h]], axis=0).astype(F32) for h in range(2)]
        da = jnp.concatenate([da_ref[...].astype(F32), dan_ref[...].astype(F32) * keep_next], axis=0)
        sg = jax.nn.sigmoid(zg)
        dz = [da * zu * (sg * (1.0 + zg * (1.0 - sg))), da * (zg * sg)]
        dh = None
        for h in range(2):
            d = dz[h]
            d0, d1, d2 = d[:tr], pltpu.roll(d, m - 1, axis=0)[:tr], pltpu.roll(d, m - 2, axis=0)[:tr]
            du = (w[h][2:3] * d0 + w[h][1:2] * d1 + w[h][0:1] * d2).astype(du_ref.dtype)
            du_ref[h] = du
            part = _dot(du, wup_ref[h, j], "nt")
            dh = part if dh is None else dh + part
            x0 = u_ref[h].astype(F32)
            parts = [jnp.sum(x0 * dk, axis=0, keepdims=True) for dk in (d2, d1, d0)]
            for k in range(3):
                @pl.when(first)
                def _(k=k, h=h):
                    dcw_ref[h, j, k:k + 1, :] = parts[k]

                @pl.when(jnp.logical_not(first))
                def _(k=k, h=h):
                    dcw_ref[h, j, k:k + 1, :] += parts[k]

        @pl.when(j == 0)
        def _():
            acc_ref[...] = dh

        @pl.when(j > 0)
        def _():
            acc_ref[...] += dh

        @pl.when(j == J - 1)
        def _():
            dh_ref[...] = acc_ref[...]

    da_tile = pl.BlockSpec((None, tr, n), lambda i, j: (j, i, 0))
    da_next = pl.BlockSpec((None, HALO, n), lambda i, j: (j, jnp.minimum((i + 1) * (tr // HALO), nb - 1), 0))
    return pl.pallas_call(
        body, name=name, grid=(S // tr, J),
        out_shape=(jax.ShapeDtypeStruct((2, J, S, n), _MXU_DTYPE), jax.ShapeDtypeStruct((2, J, 3, n), F32),
                   jax.ShapeDtypeStruct((S, K), F32)),
        in_specs=[tile, tile, nxt, whole((2, J, 3, n)), da_tile, da_next, whole((2, J, K, n)), ANY],
        out_specs=(tile, whole((2, J, 3, n)), pl.BlockSpec((tr, K), lambda i, j: (i, 0))),
        scratch_shapes=[pltpu.VMEM((tr, K), F32)],
        compiler_params=_cp("arbitrary", "arbitrary"))(u24, z24, z24, cw24, da4, da4, w_up24, after)


def _rms_rows(v, g):
    rstd = lax.rsqrt(jnp.mean(v * v, axis=-1, keepdims=True) + EPS)
    return v * rstd * g


def _rms_rows_bwd(v, g, dy):
    rstd = lax.rsqrt(jnp.mean(v * v, axis=-1, keepdims=True) + EPS)
    vhat = v * rstd
    dvhat = dy * g
    return rstd * (dvhat - vhat * jnp.mean(dvhat * vhat, axis=-1, keepdims=True)), dy * vhat


def _rope(v, cos, sa, sb):
    return v * cos + pltpu.roll(v, 112, axis=1) * sa + pltpu.roll(v, 16, axis=1) * sb


def _rope_t(d, cos, sa, sb):
    return d * cos + pltpu.roll(d * sa, 16, axis=1) + pltpu.roll(d * sb, 112, axis=1)


def _qkv_rope_fwd(z, qg, kvg, w_uq_t, w_kv, cosq, cosk, sa, sb, tr=256):
    S = z.shape[0]

    def body(ql_ref, kvl_ref, kpe_ref, qg_ref, kvg_ref, wq_ref, wkv_ref, cq_ref, ck_ref, sa_ref, sb_ref,
             qn_ref, kvn_ref, qo_ref, ko_ref, vo_ref):
        cq, ck, sa_v, sb_v = cq_ref[...], ck_ref[...], sa_ref[...], sb_ref[...]
        qn = _rms_rows(ql_ref[...], qg_ref[...]).astype(qn_ref.dtype)
        kvn = _rms_rows(kvl_ref[...], kvg_ref[...]).astype(kvn_ref.dtype)
        qn_ref[...] = qn
        kvn_ref[...] = kvn
        q = _dot(qn, wq_ref[...], "nt")
        kv = _dot(kvn, wkv_ref[...], "nn")
        kpe = _rope(kpe_ref[...], ck, sa_v, sb_v)
        for h in range(8):
            cols = slice(128 * h, 128 * h + 128)
            qo_ref[:, cols] = _rope(q[:, cols], cq, sa_v, sb_v).astype(qo_ref.dtype)
            ko_ref[:, cols] = (kv[:, cols] + kpe).astype(ko_ref.dtype)
        vo_ref[...] = kv[:, 1024:1536].astype(vo_ref.dtype)

    tab = _row_spec(tr, 128)
    whole = lambda a: pl.BlockSpec(a.shape, lambda i: (0, 0))
    return pl.pallas_call(
        body, name="qkv_rope_fwd", grid=(S // tr,),
        out_shape=(jax.ShapeDtypeStruct((S, 256), _MXU_DTYPE), jax.ShapeDtypeStruct((S, 128), _MXU_DTYPE),
                   jax.ShapeDtypeStruct((S, 1024), _MXU_DTYPE), jax.ShapeDtypeStruct((S, 1024), _MXU_DTYPE),
                   jax.ShapeDtypeStruct((S, 512), _MXU_DTYPE)),
        in_specs=[pl.BlockSpec((tr, 256), lambda i: (i, 0)), pl.BlockSpec((tr, 128), lambda i: (i, 2)),
                  pl.BlockSpec((tr, 128), lambda i: (i, 3)), _vec_spec(256), _vec_spec(128), whole(w_uq_t), whole(w_kv),
                  tab, tab, tab, tab],
        out_specs=(_row_spec(tr, 256), _row_spec(tr, 128), _row_spec(tr, 1024), _row_spec(tr, 1024), _row_spec(tr, 512)),
        compiler_params=_cp("parallel"))(z, z, z, qg, kvg, w_uq_t, w_kv, cosq, cosk, sa, sb)


def _attn_bwd_prep(o, dycat2, tr=256):
    S = o.shape[0]

    def body(o_ref, do_ref, delta_ref, doa_ref, dob_ref):
        do = do_ref[...]
        prod = do * o_ref[...]
        lane = lax.broadcasted_iota(jnp.int32, do.shape, 1)
        for p in range(4):
            cols = slice(128 * p, 128 * p + 128)
            first = lane[:, cols] < 128 * p + 64
            da = jnp.sum(jnp.where(first, prod[:, cols], 0.0), axis=-1, keepdims=True)
            db = jnp.sum(jnp.where(first, 0.0, prod[:, cols]), axis=-1, keepdims=True)
            delta_ref[p] = jnp.where(first, da, db)
            doa_ref[p] = jnp.where(first, do[:, cols], 0.0).astype(doa_ref.dtype)
            dob_ref[p] = jnp.where(first, 0.0, do[:, cols]).astype(dob_ref.dtype)

    pair = pl.BlockSpec((4, tr, 128), lambda i: (0, i, 0))
    return pl.pallas_call(
        body, name="attn_bwd_prep", grid=(S // tr,),
        out_shape=(jax.ShapeDtypeStruct((4, S, 128), F32), jax.ShapeDtypeStruct((4, S, 128), _MXU_DTYPE),
                   jax.ShapeDtypeStruct((4, S, 128), _MXU_DTYPE)),
        in_specs=[_row_spec(tr, 512), pl.BlockSpec((None, tr, 512), lambda i: (0, i, 0))],
        out_specs=(pair, pair, pair), compiler_params=_cp("parallel"))(o, dycat2)


def _qkv_rope_bwd(z, qg, kvg, dq, dk, dv, duv, w_uq_t, w_kv, cosq, cosk, sa, sb, tr=256):
    S = z.shape[0]

    def body(ql_ref, kvl_ref, qg_ref, kvg_ref, dq_ref, dk_ref, dv_ref, duv_ref, wq_ref, wkv_ref, cq_ref, ck_ref, sa_ref, sb_ref,
             dqo_ref, dkv_ref, dz_ref, dqg_ref, dkvg_ref):
        first = pl.program_id(0) == 0
        cq, ck, sa_v, sb_v = cq_ref[...], ck_ref[...], sa_ref[...], sb_ref[...]
        tot = jnp.zeros((tr, 128), F32)
        for h in range(8):
            cols = slice(128 * h, 128 * h + 128)
            dqo_ref[:, cols] = _rope_t(dq_ref[:, cols], cq, sa_v, sb_v).astype(dqo_ref.dtype)
            dkh = dk_ref[:, cols]
            tot = tot + dkh
            dkv_ref[:, cols] = dkh.astype(dkv_ref.dtype)
        dkv_ref[:, 1024:1536] = dv_ref[...].astype(dkv_ref.dtype)
        dqn = _dot(dqo_ref[...], wq_ref[...], "nn")
        dkvn = _dot(dkv_ref[...], wkv_ref[...], "nt")
        dql, dqg = _rms_rows_bwd(ql_ref[...], qg_ref[...], dqn)
        dkvl, dkvg = _rms_rows_bwd(kvl_ref[...], kvg_ref[...], dkvn)
        _acc_rows(dqg_ref, dqg, first)
        _acc_rows(dkvg_ref, dkvg, first)
        dz_ref[:, 0:256] = dql.astype(dz_ref.dtype)
        dz_ref[:, 256:384] = dkvl.astype(dz_ref.dtype)
        dz_ref[:, 384:512] = _rope_t(tot, ck, sa_v, sb_v).astype(dz_ref.dtype)
        dz_ref[:, 512:1536] = duv_ref[...].astype(dz_ref.dtype)

    tab = _row_spec(tr, 128)
    whole = lambda a: pl.BlockSpec(a.shape, lambda i: (0, 0))
    return pl.pallas_call(
        body, name="qkv_rope_bwd", grid=(S // tr,),
        out_shape=(jax.ShapeDtypeStruct((S, 1024), _MXU_DTYPE), jax.ShapeDtypeStruct((S, 1536), _MXU_DTYPE),
                   jax.ShapeDtypeStruct((S, 1536), _MXU_DTYPE), jax.ShapeDtypeStruct((1, 256), F32), jax.ShapeDtypeStruct((1, 128), F32)),
        in_specs=[pl.BlockSpec((tr, 256), lambda i: (i, 0)), pl.BlockSpec((tr, 128), lambda i: (i, 2)), _vec_spec(256), _vec_spec(128),
                  _row_spec(tr, 1024), _row_spec(tr, 1024), _row_spec(tr, 512), _row_spec(tr, 1024), whole(w_uq_t), whole(w_kv),
                  tab, tab, tab, tab],
        out_specs=(_row_spec(tr, 1024), _row_spec(tr, 1536), _row_spec(tr, 1536), _vec_spec(256), _vec_spec(128)),
        compiler_params=_cp("arbitrary"))(z, z, qg, kvg, dq, dk, dv, duv, w_uq_t, w_kv, cosq, cosk, sa, sb)


NEG = -1e30


def _attn_fwd(q, k, v, tq=512, tk=512):
    S = q.shape[0]
    assert tq == tk

    def body(q_ref, k_ref, v_ref, o_ref, lse_ref):
        i = pl.program_id(1)
        qs = [q_ref[:, 0:128], q_ref[:, 128:256]]

        def step(kb, carry, diagonal=False):
            start = pl.multiple_of(kb * tk, tk)
            vv = v_ref[pl.ds(start, tk), :]
            out = []
            for h in range(2):
                m, l, acc = carry[3 * h:3 * h + 3]
                s = _dot(qs[h], k_ref[pl.ds(start, tk), 128 * h:128 * h + 128], "nt") * ATTN_SCALE
                if diagonal:
                    s = jnp.where(below, s, NEG)
                m_new = jnp.maximum(m, jnp.max(s, axis=-1, keepdims=True))
                alpha = jnp.exp(m - m_new)
                p = jnp.exp(s - m_new)
                out += [m_new, alpha * l + jnp.sum(p, axis=-1, keepdims=True), alpha * acc + _dot(p, vv, "nn")]
            return tuple(out)

        below = lax.broadcasted_iota(jnp.int32, (tq, tk), 1) <= lax.broadcasted_iota(jnp.int32, (tq, tk), 0)
        init = (jnp.full((tq, 1), NEG, F32), jnp.zeros((tq, 1), F32), jnp.zeros((tq, 128), F32)) * 2
        ma, la, acca, mb, lb, accb = step(i, lax.fori_loop(0, i, step, init), diagonal=True)
        lane = lax.broadcasted_iota(jnp.int32, (tq, 128), 1)
        o_ref[...] = jnp.where(lane < 64, acca / la, accb / lb)
        lse_ref[...] = jnp.where(lane < 64, ma + jnp.log(la), mb + jnp.log(lb))

    return pl.pallas_call(
        body, name="attn_fwd", grid=(4, S // tq),
        out_shape=(jax.ShapeDtypeStruct((S, 512), F32), jax.ShapeDtypeStruct((4, S, 128), F32)),
        in_specs=[pl.BlockSpec((tq, 256), lambda p, i: (i, p)), pl.BlockSpec((S, 256), lambda p, i: (0, p)),
                  pl.BlockSpec((S, 128), lambda p, i: (0, p))],
        out_specs=(pl.BlockSpec((tq, 128), lambda p, i: (i, p)), pl.BlockSpec((None, tq, 128), lambda p, i: (p, i, 0))),
        compiler_params=_cp("parallel", "parallel"))(q, k, v)


def _attn_bwd(q, k, v, lse, delta, doa, dob, tq=512, tk=512):
    S = q.shape[0]
    assert tq == tk

    def body(q_ref, k_ref, v_ref, lse_ref, delta_ref, doa_ref, dob_ref, dq_ref, dk_ref, dv_ref):
        j = pl.program_id(1)

        @pl.when(j == 0)
        def _():
            dq_ref[...] = jnp.zeros_like(dq_ref)

        below = lax.broadcasted_iota(jnp.int32, (tq, tk), 1) <= lax.broadcasted_iota(jnp.int32, (tq, tk), 0)
        ks = [k_ref[:, 0:128], k_ref[:, 128:256]]
        vv = v_ref[...]

        def step(qb, carry, diagonal=False):
            dka, dkb, dvp = carry
            start = pl.multiple_of(qb * tq, tq)
            rows = pl.ds(start, tq)
            lse_v, delta_v = lse_ref[rows, :], delta_ref[rows, :]
            dos = [doa_ref[rows, :], dob_ref[rows, :]]
            dks = [dka, dkb]
            for h in range(2):
                delta = delta_v[:, 64 * h:64 * h + 1]
                do_h = dos[h]
                qh = q_ref[rows, 128 * h:128 * h + 128]
                s = _dot(qh, ks[h], "nt") * ATTN_SCALE
                p = jnp.exp(s - lse_v[:, 64 * h:64 * h + 1])
                if diagonal:
                    p = jnp.where(below, p, 0.0)
                dvp = dvp + _dot(p, do_h, "tn")
                ds = p * (_dot(do_h, vv, "nt") - delta) * ATTN_SCALE
                dq_ref[rows, 128 * h:128 * h + 128] += _dot(ds, ks[h], "nn")
                dks[h] = dks[h] + _dot(ds, qh, "tn")
            return dks[0], dks[1], dvp

        zero = jnp.zeros((tk, 128), F32)
        dka, dkb, dvp = lax.fori_loop(j + 1, S // tq, step, step(j, (zero, zero, zero), diagonal=True))
        dk_ref[:, 0:128] = dka
        dk_ref[:, 128:256] = dkb
        dv_ref[...] = dvp

    return pl.pallas_call(
        body, name="attn_bwd", grid=(4, S // tk),
        out_shape=(jax.ShapeDtypeStruct((S, 1024), F32), jax.ShapeDtypeStruct((S, 1024), F32), jax.ShapeDtypeStruct((S, 512), F32)),
        in_specs=[pl.BlockSpec((S, 256), lambda p, j: (0, p)), pl.BlockSpec((tk, 256), lambda p, j: (j, p)),
                  pl.BlockSpec((tk, 128), lambda p, j: (j, p))] + [pl.BlockSpec((None, S, 128), lambda p, j: (p, 0, 0))] * 4,
        out_specs=(pl.BlockSpec((S, 256), lambda p, j: (0, p)), pl.BlockSpec((tk, 256), lambda p, j: (j, p)),
                   pl.BlockSpec((tk, 128), lambda p, j: (j, p))),
        compiler_params=_cp("parallel", "arbitrary"))(q, k, v, lse, delta, doa, dob)


CHUNK = 128
GELU_C = math.sqrt(2.0 / math.pi)


def _gelu(v):
    t = jnp.tanh(GELU_C * (v + 0.044715 * (v * v * v)))
    return v * (0.5 * (1.0 + t)), t


def _gelu_grad(v, t):
    return 0.5 * (1.0 + t) + v * (0.5 * (1.0 - t * t) * GELU_C * (1.0 + 3.0 * 0.044715 * v * v))


def _tril(w):
    r = lax.broadcasted_iota(jnp.int32, w.shape, 0)
    c = lax.broadcasted_iota(jnp.int32, w.shape, 1)
    return jnp.where(c <= r, w, 0.0)


def _layer_norm(v, g, b):
    xc = v - jnp.mean(v, axis=-1, keepdims=True)
    rstd = lax.rsqrt(jnp.mean(xc * xc, axis=-1, keepdims=True) + EPS)
    xhat = xc * rstd
    return xhat * g + b, xhat, rstd


def _sgu_fwd(z, o, ln_g, ln_b, w_s, b_st, tr=256):
    S = z.shape[0]

    def body(u_ref, v_ref, o_ref, g_ref, b_ref, ws_ref, bs_ref, y_ref):
        gu, _ = _gelu(u_ref[...])
        gv, _ = _gelu(v_ref[...])
        vln, _, _ = _layer_norm(gv, g_ref[...], b_ref[...])
        y_ref[0] = o_ref[...].astype(y_ref.dtype)
        for g in range(4):
            wt = _tril(ws_ref[g])
            cols = slice(128 * g, 128 * g + 128)
            for ch in range(tr // CHUNK):
                rows = slice(CHUNK * ch, CHUNK * ch + CHUNK)
                mixed = _dot(wt, vln[rows, cols], "nn") + bs_ref[:, g:g + 1]
                y_ref[1, rows, cols] = (gu[rows, cols] * mixed).astype(y_ref.dtype)

    return pl.pallas_call(
        body, name="sgu_fwd", grid=(S // tr,), out_shape=jax.ShapeDtypeStruct((2, S, 512), _MXU_DTYPE),
        in_specs=[pl.BlockSpec((tr, 512), lambda i: (i, 1)), pl.BlockSpec((tr, 512), lambda i: (i, 2)), _row_spec(tr, 512),
                  _vec_spec(512), _vec_spec(512), pl.BlockSpec((4, 128, 128), lambda i: (0, 0, 0)), pl.BlockSpec((128, 4), lambda i: (0, 0))],
        out_specs=pl.BlockSpec((2, tr, 512), lambda i: (0, i, 0)), compiler_params=_cp("parallel"))(z, z, o, ln_g, ln_b, w_s, b_st)


def _sgu_bwd(z, dycat2, ln_g, ln_b, w_s, b_st, tr=256):
    S = z.shape[0]

    def body(u_ref, v_ref, dy_ref, g_ref, b_ref, ws_ref, bs_ref, duv_ref, dg_ref, db_ref, dws_ref, dbs_ref):
        first = pl.program_id(0) == 0
        u_pre, v_pre = u_ref[...], v_ref[...]
        gu, tu = _gelu(u_pre)
        gv, tv = _gelu(v_pre)
        gain = g_ref[...]
        vln, xhat, rstd = _layer_norm(gv, gain, b_ref[...])

        @pl.when(first)
        def _():
            dws_ref[...] = jnp.zeros_like(dws_ref)
            dbs_ref[...] = jnp.zeros_like(dbs_ref)

        dvln_cols = []
        for g in range(4):
            wt = _tril(ws_ref[g])
            cols = slice(128 * g, 128 * g + 128)
            dmixed_sum = jnp.zeros((CHUNK, 128), F32)
            dw = jnp.zeros((CHUNK, CHUNK), F32)
            dvln_rows = []
            for ch in range(tr // CHUNK):
                rows = slice(CHUNK * ch, CHUNK * ch + CHUNK)
                vt = vln[rows, cols]
                mixed = _dot(wt, vt, "nn") + bs_ref[:, g:g + 1]
                dyd = dy_ref[rows, cols]
                duv_ref[rows, cols] = (dyd * mixed * _gelu_grad(u_pre[rows, cols], tu[rows, cols])).astype(duv_ref.dtype)
                dmixed = dyd * gu[rows, cols]
                dmixed_sum = dmixed_sum + dmixed
                dw = dw + _dot(dmixed, vt, "nt")
                dvln_rows.append(_dot(wt, dmixed, "tn"))
            dws_ref[g] += _tril(dw)
            dbs_ref[g:g + 1, :] += jnp.sum(dmixed_sum.T, axis=0, keepdims=True)
            dvln_cols.append(jnp.concatenate(dvln_rows, axis=0))
        dvln = jnp.concatenate(dvln_cols, axis=1)
        _acc_rows(dg_ref, dvln * xhat, first)
        _acc_rows(db_ref, dvln, first)
        dxhat = dvln * gain
        dgv = rstd * (dxhat - jnp.mean(dxhat, axis=-1, keepdims=True) - xhat * jnp.mean(dxhat * xhat, axis=-1, keepdims=True))
        duv_ref[:, 512:1024] = (dgv * _gelu_grad(v_pre, tv)).astype(duv_ref.dtype)

    return pl.pallas_call(
        body, name="sgu_bwd", grid=(S // tr,),
        out_shape=(jax.ShapeDtypeStruct((S, 1024), _MXU_DTYPE), jax.ShapeDtypeStruct((1, 512), F32), jax.ShapeDtypeStruct((1, 512), F32),
                   jax.ShapeDtypeStruct((4, 128, 128), F32), jax.ShapeDtypeStruct((4, 128), F32)),
        in_specs=[pl.BlockSpec((tr, 512), lambda i: (i, 1)), pl.BlockSpec((tr, 512), lambda i: (i, 2)),
                  pl.BlockSpec((None, tr, 512), lambda i: (1, i, 0)), _vec_spec(512), _vec_spec(512),
                  pl.BlockSpec((4, 128, 128), lambda i: (0, 0, 0)), pl.BlockSpec((128, 4), lambda i: (0, 0))],
        out_specs=(_row_spec(tr, 1024), _vec_spec(512), _vec_spec(512), pl.BlockSpec((4, 128, 128), lambda i: (0, 0, 0)),
                   pl.BlockSpec((4, 128), lambda i: (0, 0))),
        compiler_params=_cp("arbitrary"))(z, z, dycat2, ln_g, ln_b, w_s, b_st)


def _sum_parts(name, parts, tr=512):
    P, R, C = parts.shape
    tr = _tile(R, tr) if R % 8 == 0 else R

    def body(p_ref, o_ref):
        g = p_ref[0]
        for k in range(1, P):
            g = g + p_ref[k]
        o_ref[...] = g

    return pl.pallas_call(
        body, name=name, grid=(R // tr,), out_shape=jax.ShapeDtypeStruct((R, C), F32),
        in_specs=[pl.BlockSpec((P, tr, C), lambda i: (0, i, 0))], out_specs=_row_spec(tr, C),
        compiler_params=_cp("parallel"))(parts)


def _adamw_math(w, m, v, g):
    c1 = 1.0 / (1.0 - ADAM_B1 ** ADAM_STEP)
    c2 = 1.0 / (1.0 - ADAM_B2 ** ADAM_STEP)
    m2 = ADAM_B1 * m + (1.0 - ADAM_B1) * g
    v2 = ADAM_B2 * v + (1.0 - ADAM_B2) * (g * g)
    return -ADAM_LR * ((m2 * c1) / (jnp.sqrt(v2 * c2) + ADAM_EPS) + ADAM_WD * w), m2, v2


def _adamw_small(name, params, parts):
    n = len(params)

    def body(*refs):
        ins, outs = refs[:4 * n], refs[4 * n:]
        for i in range(n):
            w_ref, m_ref, v_ref, p_ref = ins[4 * i:4 * i + 4]
            g = p_ref[0].astype(F32)
            for k in range(1, N_DEV):
                g = g + p_ref[k].astype(F32)
            delta, m2, v2 = _adamw_math(w_ref[...], m_ref[...], v_ref[...], g)
            outs[4 * i][...] = g
            outs[4 * i + 1][...] = delta
            outs[4 * i + 2][...] = m2
            outs[4 * i + 3][...] = v2

    flat = [a for (w, m, v), p in zip(params, parts) for a in (w, m, v, p)]
    out = pl.pallas_call(
        body, name=name, out_shape=[jax.ShapeDtypeStruct(w.shape, F32) for (w, _, _) in params for _ in range(4)],
        compiler_params=pltpu.CompilerParams(vmem_limit_bytes=_VMEM_LIMIT))(*flat)
    return [out[4 * i:4 * i + 4] for i in range(n)]


ADAMW_BLOCK_BYTES = 36 * 2 ** 20


def _adamw(name, w, m, v, parts):
    L, R, C = w.shape
    P = parts[0].shape[0]
    row_bytes = 2 * C * (7 * 4 + P * parts[0].dtype.itemsize)
    tr = R
    if R * row_bytes > ADAMW_BLOCK_BYTES:
        tr = next(t for t in (1024, 512, 256, 128, 64, 32, 16) if R % t == 0 and t * row_bytes <= ADAMW_BLOCK_BYTES)
    nr = R // tr
    c1 = 1.0 / (1.0 - ADAM_B1 ** ADAM_STEP)
    c2 = 1.0 / (1.0 - ADAM_B2 ** ADAM_STEP)

    def body(w_ref, m_ref, v_ref, *rest):
        p_refs, (g_ref, d_ref, mo_ref, vo_ref) = rest[:L], rest[L:]
        for ll in range(L):
            @pl.when(pl.program_id(0) == ll)
            def _(p_ref=p_refs[ll]):
                g = p_ref[0].astype(F32)
                for k in range(1, P):
                    g = g + p_ref[k].astype(F32)
                m2 = ADAM_B1 * m_ref[...] + (1.0 - ADAM_B1) * g
                v2 = ADAM_B2 * v_ref[...] + (1.0 - ADAM_B2) * (g * g)
                g_ref[...] = g
                mo_ref[...] = m2
                vo_ref[...] = v2
                d_ref[...] = -ADAM_LR * ((m2 * c1) / (jnp.sqrt(v2 * c2) + ADAM_EPS) + ADAM_WD * w_ref[...])

    def part_spec(ll):
        return pl.BlockSpec((P, tr, C), lambda l, i: (0, jnp.where(l == ll, i, jnp.where(l < ll, 0, nr - 1)), 0))

    full = pl.BlockSpec((None, tr, C), lambda l, i: (l, i, 0))
    sds = jax.ShapeDtypeStruct((L, R, C), F32)
    return pl.pallas_call(
        body, name=name, grid=(L, nr), out_shape=(sds, sds, sds, sds),
        in_specs=[full] * 3 + [part_spec(ll) for ll in range(L)],
        out_specs=(full,) * 4, compiler_params=_cp("arbitrary", "arbitrary"))(w, m, v, *parts)


def _rope_tables(positions):
    half = 16
    inv_freq = 10000.0 ** (-jnp.arange(half, dtype=F32) / half)
    ang = positions.astype(F32)[:, None] * inv_freq
    cos, sin = jnp.cos(ang), jnp.sin(ang)
    S = positions.shape[0]
    z16, z32, z64 = jnp.zeros((S, 16), F32), jnp.zeros((S, 32), F32), jnp.zeros((S, 64), F32)
    cosk = jnp.concatenate([z64, cos, cos, z32], axis=1)
    cosq = jnp.concatenate([jnp.ones((S, 64), F32), cos, cos, z32], axis=1)
    sa = jnp.concatenate([z64, -sin, z16, z32], axis=1)
    sb = jnp.concatenate([z64, z16, sin, z32], axis=1)
    return cosq, cosk, sa, sb


def _ffn_fwd(l, x, mod, n2g, get_w_up8, cw24, get_w_down4):
    sh, sc, gate = mod
    h = _rmsmod_fwd(f"ffn{l}_norm", x, n2g, sc, sh, n2g)
    w_up8 = get_w_up8(h)
    u8 = _mm_cols(f"ffn{l}_up", h, w_up8, out_dtype=ACT_DTYPE, tm=2048)
    S, n = u8.shape[1], u8.shape[2]
    u24 = u8.reshape(2, 4, S, n)
    a4, z24 = _ffn_gate_fwd(f"ffn{l}_gate", u24, cw24)
    w_down4 = get_w_down4(a4)
    f, x_new = _mm_rows_resid(f"ffn{l}_down", a4, w_down4, x, gate)
    return x_new, (x, h, u24, a4, f, z24), w_up8, w_down4


def _ffn_bwd(l, dx, df, dgate, saved, mod, n2g, w_up8, cw24, w_down4, me, y_prev, gate_prev):
    sh, sc, gate = mod
    x, h, u24, a4, f, z24 = saved
    da4 = _mm_rows_dx(f"ffn{l}_down_dx", df, w_down4, out_dtype=ACT_DTYPE, tm=2048)
    dw_down4 = _mm_rows_dw(f"ffn{l}_down_dw", a4, df, out_dtype=WIRE_DTYPE)
    sent_down, token = _exchange_start(f"scatter_ffn{l}_down", [dw_down4.reshape(8, 352, dw_down4.shape[2])], True, dgate, me)
    du24, dcw24, dh = _ffn_gate_bwd(f"ffn{l}_act_bwd", u24, z24, cw24, da4, w_up8.reshape((2, 4) + w_up8.shape[1:]), token)
    du8 = du24.reshape((8,) + du24.shape[2:])
    dw_up8t = _mm_cols_dwt(f"ffn{l}_up_dw", h, du8, out_dtype=WIRE_DTYPE, tk=1024)
    sent_up, token = _exchange_start(f"scatter_ffn{l}_up", [dw_up8t], True, dcw24, me)
    dx_new, dn2g, dsc, dsh, dy_prev, dgate_prev = _rmsmod_bwd(f"ffn{l}_norm_bwd", x, n2g, sc, dh, dx, token, y_prev, gate_prev)
    return dx_new, dict(sent_up=sent_up, sent_down=sent_down, cw24=dcw24, n2g=dn2g, mod=(dsh, dsc, dgate)), dy_prev, dgate_prev


def kernel(x, c, positions, ada_w, ada_b, norm1_g, norm2_g, ab_w_in, a_conv_w, b_mix_w, b_scale, ab_w_out, cd_w_in, c_q_norm_g, c_w_uq, c_kv_norm_g, c_w_ukv, d_ln_g, d_ln_b, d_w_s, d_b_s, cd_w_out, ffn_w_up, ffn_conv_w, ffn_w_down, final_norm_g, loss_target, m_ada_w, m_ada_b, m_norm1_g, m_norm2_g, m_ab_w_in, m_a_conv_w, m_b_mix_w, m_b_scale, m_ab_w_out, m_cd_w_in, m_c_q_norm_g, m_c_w_uq, m_c_kv_norm_g, m_c_w_ukv, m_d_ln_g, m_d_ln_b, m_d_w_s, m_d_b_s, m_cd_w_out, m_ffn_w_up, m_ffn_conv_w, m_ffn_w_down, m_final_norm_g, v_ada_w, v_ada_b, v_norm1_g, v_norm2_g, v_ab_w_in, v_a_conv_w, v_b_mix_w, v_b_scale, v_ab_w_out, v_cd_w_in, v_c_q_norm_g, v_c_w_uq, v_c_kv_norm_g, v_c_w_ukv, v_d_ln_g, v_d_ln_b, v_d_w_s, v_d_b_s, v_cd_w_out, v_ffn_w_up, v_ffn_conv_w, v_ffn_w_down, v_final_norm_g):
    S, D = x.shape[1], x.shape[2]
    me = 4 * lax.axis_index("x") + 2 * lax.axis_index("y") + lax.axis_index("c")
    x0, target = x[0], loss_target[0]
    W = _MXU_DTYPE

    small_shapes = [(1024,), (3, 64), (32,), (64,), (64,), (2, 3, 704)]
    (g0,) = _exchange("gather_small", [[_pack([c, a_conv_w, c_q_norm_g, d_ln_g, d_ln_b, ffn_conv_w])]], scatter=False)
    c_all, aconv_s, qg_s, lng_s, lnb_s, fcw_s = _unpack(g0[:, 0], small_shapes, lead=(N_DEV,))
    conv_w = aconv_s.transpose(1, 0, 2).reshape(3, 512)
    qg, ln_g, ln_b = qg_s.reshape(1, 256), lng_s.reshape(1, 512), lnb_s.reshape(1, 512)
    cw24 = [fcw_s[:, l].reshape(2, 4, 3, 704) for l in range(2)]
    c16 = jnp.pad(c_all, ((0, 16 - N_DEV), (0, 0)))

    mod_cols = _ada_fwd(c16, ada_w)
    (g1,) = _exchange("gather_mod", [[_pack([mod_cols])]], scatter=False)
    mod_all = _unpack(g1[:, 0], [(2, 16, 768)], lead=(N_DEV,))[0]
    mod_mine = lax.dynamic_index_in_dim(mod_all, me, axis=2, keepdims=False)
    mod = mod_mine.transpose(1, 0, 2).reshape(2, 6 * D) + ada_b
    mods = [[mod[l, k * D:(k + 1) * D].reshape(1, D) for k in range(6)] for l in range(2)]

    gw_ab, token = _hier_gather_start("gather_w_ab", [ab_w_in[0].astype(W), ab_w_out[0].astype(W)], mod, me)
    gw_up0, token = _hier_gather_start("gather_w_ffn0_up", [ffn_w_up[0].astype(W)], token, me)
    gw_rest, started = _exchange_start("gather_w_rest", [
        ffn_w_down[0].astype(W), cd_w_in[0].T.astype(W), c_w_uq[0].T.astype(W), c_w_ukv[0].astype(W), cd_w_out[0].astype(W),
        ffn_w_up[1].astype(W), ffn_w_down[1].astype(W)], False, token, me)

    cosq, cosk, sa, sb = _rope_tables(positions[0])
    n1g = [norm1_g[l].reshape(1, D) for l in range(2)]
    n2g = [norm2_g[l].reshape(1, D) for l in range(2)]
    mix_w, scale = b_mix_w[0], b_scale
    kvg = c_kv_norm_g
    w_s, b_st = d_w_s[0], d_b_s[0].T

    sh1, sc1, g1m = mods[0][:3]
    h_ab = _rmsmod_fwd("ab_norm", x0, n1g[0], sc1, sh1, started)
    w_abin8, w_about = _hier_gather_wait("wait_w_ab", _hier_gather_forward("forward_w_ab", gw_ab, h_ab), h_ab)
    w_about2 = w_about.reshape(2, 512, D)
    z8 = _mm_cols("ab_in", h_ab, w_abin8, tm=2048)
    ycat_ab = _ab_mix_fwd(z8, conv_w, mix_w, scale)
    y_ab, x1 = _mm_rows_resid("ab_out", ycat_ab, w_about2, x0, g1m)
    w_up8, w_down4 = [None, None], [None, None]
    gw_up0 = _hier_gather_forward("forward_w_ffn0_up", gw_up0, x1)
    x2, ffn0_saved, w_up8[0], w_down4[0] = _ffn_fwd(
        0, x1, mods[0][3:], n2g[0], lambda after: _hier_gather_wait("wait_w_ffn0_up", gw_up0, after)[0], cw24[0],
        lambda after: _exchange_wait("wait_w_ffn0_down", gw_rest, after, [0])[0].reshape(4, 704, D))

    w_cdin, w_uq, w_ukv, w_cdout = _exchange_wait("wait_w_cd", gw_rest, x2, [1, 2, 3, 4])
    w_cdout2 = w_cdout.reshape(2, 512, D)
    w_cd_t = w_cdin.reshape(1440, D)
    zr = lambda n: jnp.zeros((n, D), W)
    w_cd_pad = jnp.concatenate([w_cd_t[:384], zr(64), w_cd_t[384:416], zr(32), w_cd_t[416:]], axis=0)
    w_uq_pad = jnp.pad(w_uq, ((0, 0), (0, 32), (0, 0))).reshape(1024, 256)
    w_ukv_h = w_ukv.transpose(1, 0, 2)
    w_k_pad = jnp.pad(w_ukv_h[:, :, :64], ((0, 0), (0, 0), (0, 64))).reshape(128, 1024)
    w_kv_pad = jnp.concatenate([w_k_pad, w_ukv_h[:, :, 64:].reshape(128, 512)], axis=1)

    sh1, sc1, g1c = mods[1][:3]
    h_cd = _rmsmod_fwd("cd_norm", x2, n1g[1], sc1, sh1, n1g[1])
    z_cd = _mm_nt("cd_in", h_cd, w_cd_pad)
    qn, kvn, q_r, k_r, v_r = _qkv_rope_fwd(z_cd, qg, kvg, w_uq_pad, w_kv_pad, cosq, cosk, sa, sb)
    o, lse = _attn_fwd(q_r, k_r, v_r)
    ycat_cd = _sgu_fwd(z_cd, o, ln_g, ln_b, w_s, b_st)
    y_cd, x3 = _mm_rows_resid("cd_out", ycat_cd, w_cdout2, x2, g1c)
    x4, ffn1_saved, w_up8[1], w_down4[1] = _ffn_fwd(
        1, x3, mods[1][3:], n2g[1], lambda after: _exchange_wait("wait_w_ffn1_up", gw_rest, after, [5])[0], cw24[1],
        lambda after: _exchange_wait("wait_w_ffn1_down", gw_rest, after, [6])[0].reshape(4, 704, D))

    loss_local, dx4, dfg, df1, dgate1 = _loss_head(x4, final_norm_g.reshape(1, D), target, ffn1_saved[4], mods[1][5])

    dx3, gf1, dy, dg1c = _ffn_bwd(1, dx4, df1, dgate1, ffn1_saved, mods[1][3:], n2g[1], w_up8[1], cw24[1], w_down4[1], me, y_cd, g1c)

    dycat = _mm_rows_dx("cd_out_dx", dy, w_cdout2)
    dw_cdout = _mm_rows_dw("cd_out_dw", ycat_cd, dy, out_dtype=WIRE_DTYPE)
    duv, dln_g, dln_b, dws, dbs = _sgu_bwd(z_cd, dycat, ln_g, ln_b, w_s, b_st)
    dq_r, dk_r, dv_r = _attn_bwd(q_r, k_r, v_r, lse, *_attn_bwd_prep(o, dycat))
    dqraw, dkvall, dz_cd, dqg, dkvg = _qkv_rope_bwd(z_cd, qg, kvg, dq_r, dk_r, dv_r, duv, w_uq_pad, w_kv_pad, cosq, cosk, sa, sb)
    dw_uq_pad = _mm_tn("cd_uq_dw", dqraw, qn, tn=256)
    dw_kv_pad = _mm_tn("cd_ukv_dw", kvn, dkvall, tm=128)
    dh_cd = _mm_nn("cd_in_dx", dz_cd, w_cd_pad)
    dw_cd_pad = _mm_tn("cd_in_dw", dz_cd, h_cd)
    dw_cd8 = jnp.concatenate([dw_cd_pad[:384], dw_cd_pad[448:480], dw_cd_pad[512:]], axis=0).astype(WIRE_DTYPE).reshape(8, 180, D)
    dw_uq8 = dw_uq_pad.reshape(8, 128, 256)[:, :96].astype(WIRE_DTYPE)
    dw_ukv8 = jnp.concatenate([dw_kv_pad[:, :1024].reshape(128, 8, 128)[:, :, :64], dw_kv_pad[:, 1024:].reshape(128, 8, 64)],
                              axis=2).transpose(1, 0, 2).astype(WIRE_DTYPE)
    sent_cd, token = _exchange_start("scatter_cd", [dw_cd8, dw_uq8, dw_ukv8, dw_cdout.reshape(8, 128, D)], True, dqg, me)
    early_names = ["c_kv_norm_g", "d_w_s", "d_b_s", "final_norm_g", "c_q_norm_g", "d_ln_g", "d_ln_b"]
    early_grads = [dkvg, dws.reshape(512, 128).astype(WIRE_DTYPE), dbs, dfg, dqg.reshape(8, 1, 32), dln_g.reshape(8, 1, 64),
                   dln_b.reshape(8, 1, 64)]
    early_sent, token = _exchange_start("gather_small_grads_early", early_grads, [False] * 4 + [True] * 3, token, me)
    dx2, dn1g_cd, dsc1_cd, dsh1_cd, df0, dgate0 = _rmsmod_bwd("cd_norm_bwd", x2, n1g[1], sc1, dh_cd, dx3, token,
                                                              ffn0_saved[4], mods[0][5])

    dx1, gf0, dy, dg1m = _ffn_bwd(0, dx2, df0, dgate0, ffn0_saved, mods[0][3:], n2g[0], w_up8[0], cw24[0], w_down4[0], me, y_ab, g1m)

    dw_about = _mm_rows_dw("ab_out_dw", ycat_ab, dy, out_dtype=WIRE_DTYPE)
    sent_about, token = _exchange_start("scatter_ab_out", [dw_about.reshape(8, 128, D)], True, dg1m, me)
    dycat = _mm_rows_dx("ab_out_dx", dy, w_about2)
    dz8, dconv_w, dmix_w, dscale = _ab_mix_bwd(z8, dycat, conv_w, mix_w, scale, token)
    dz8 = dz8.reshape(8, S, 256)
    dw_abin8 = _mm_cols_dw("ab_in_dw", h_ab, dz8, out_dtype=WIRE_DTYPE, tk=1024)
    sent_abin, token = _exchange_start("scatter_ab_in", [dw_abin8], True, dscale, me)
    dh_ab = _mm_cols_dx("ab_in_dx", dz8, w_abin8)
    dx0, dn1g_ab, dsc1_ab, dsh1_ab = _rmsmod_bwd("ab_norm_bwd", x0, n1g[0], mods[0][1], dh_ab, dx1, token)

    dmod = jnp.stack([jnp.concatenate([dsh1_ab, dsc1_ab, dg1m, *gf0["mod"]], axis=1)[0],
                      jnp.concatenate([dsh1_cd, dsc1_cd, dg1c, *gf1["mod"]], axis=1)[0]])
    late_names = ["ada_b", "norm1_g", "norm2_g", "b_mix_w", "b_scale", "a_conv_w", "ffn_conv_w"]
    late_grads = [dmod, jnp.concatenate([dn1g_ab, dn1g_cd]), jnp.concatenate([gf0["n2g"], gf1["n2g"]]),
                  dmix_w.reshape(512, 128).astype(WIRE_DTYPE), dscale, dconv_w.reshape(3, 8, 64).transpose(1, 0, 2),
                  jnp.stack([gf0["cw24"].reshape(8, 3, 704), gf1["cw24"].reshape(8, 3, 704)], axis=1),
                  jnp.pad(loss_local, ((0, 0), (0, 127)))]
    small_view = dict(ada_b=(2, 6 * D), norm1_g=(2, D), norm2_g=(2, D), b_mix_w=(512, 128), b_scale=(1, 512), c_kv_norm_g=(1, 128),
                      d_w_s=(512, 128), d_b_s=(4, 128), final_norm_g=(1, D),
                      a_conv_w=(3, 64), c_q_norm_g=(1, 32), d_ln_g=(1, 64), d_ln_b=(1, 64), ffn_conv_w=(2, 3, 704))
    late_sent, token = _exchange_start("gather_small_grads_late", late_grads, [False] * 5 + [True] * 2 + [False], dx0, me)

    res = {}

    def update(name, w, m, v, parts, shape3d):
        outs = _adamw("adamw_" + name, w.reshape(shape3d), m.reshape(shape3d), v.reshape(shape3d),
                      [p.reshape((p.shape[0],) + shape3d[1:]) for p in parts])
        res[name] = [o_.reshape(w.shape) for o_ in outs]

    p_cdin, p_uq, p_ukv, p_cdout = _exchange_wait("wait_scatter_cd", sent_cd, token)
    swap = lambda a: jnp.swapaxes(a, 1, 2)
    update("cd_w_in", swap(cd_w_in), swap(m_cd_w_in), swap(v_cd_w_in), [p_cdin], (1, 180, D))
    update("c_w_uq", swap(c_w_uq), swap(m_c_w_uq), swap(v_c_w_uq), [p_uq], (1, 96, 256))
    for name in ("cd_w_in", "c_w_uq"):
        res[name] = [swap(o_) for o_ in res[name]]
    update("c_w_ukv", c_w_ukv, m_c_w_ukv, v_c_w_ukv, [p_ukv], (1, 128, 128))
    update("cd_w_out", cd_w_out, m_cd_w_out, v_cd_w_out, [p_cdout], (1, 128, D))
    (p_dn1,) = _exchange_wait("wait_scatter_ffn1_down", gf1["sent_down"], token)
    (p_dn0,) = _exchange_wait("wait_scatter_ffn0_down", gf0["sent_down"], res["cd_w_out"][0])
    update("ffn_w_down", ffn_w_down, m_ffn_w_down, v_ffn_w_down, [p_dn0, p_dn1], (2, 352, D))
    (p_up1,) = _exchange_wait("wait_scatter_ffn1_up", gf1["sent_up"], token)
    (p_up0,) = _exchange_wait("wait_scatter_ffn0_up", gf0["sent_up"], res["ffn_w_down"][0])
    swap = lambda a: jnp.swapaxes(a, 1, 2)
    update("ffn_w_up", swap(ffn_w_up), swap(m_ffn_w_up), swap(v_ffn_w_up), [p_up0, p_up1], (2, 704, D))
    up_done = res["ffn_w_up"][0]
    res["ffn_w_up"] = [swap(o_) for o_ in res["ffn_w_up"]]
    (p_about,) = _exchange_wait("wait_scatter_ab_out", sent_about, up_done)
    update("ab_w_out", ab_w_out, m_ab_w_out, v_ab_w_out, [p_about], (1, 128, D))
    (p_abin,) = _exchange_wait("wait_scatter_ab_in", sent_abin, res["ab_w_out"][0])
    update("ab_w_in", ab_w_in, m_ab_w_in, v_ab_w_in, [p_abin], (1, D, 256))

    early_parts = _exchange_wait("wait_small_grads_early", early_sent, res["ab_w_in"][0])
    late_parts = _exchange_wait("wait_small_grads_late", late_sent, res["ab_w_in"][0])
    small_names = early_names + late_names
    small_parts = list(early_parts) + list(late_parts[:7])
    loss = jnp.sum(late_parts[7][:, 0, 0])
    dmod_all = late_parts[0]
    dmod_cols = lax.dynamic_slice_in_dim(dmod_all, me * 768, 768, axis=2).transpose(1, 0, 2)
    g_ada_w = _ada_bwd(c16, jnp.pad(dmod_cols, ((0, 0), (0, 16 - N_DEV), (0, 0))))
    update("ada_w", ada_w, m_ada_w, v_ada_w, [g_ada_w[l][None] for l in range(2)], (2, D, 768))

    small_w = dict(ada_b=(ada_b, m_ada_b, v_ada_b), norm1_g=(norm1_g, m_norm1_g, v_norm1_g), norm2_g=(norm2_g, m_norm2_g, v_norm2_g),
                   b_mix_w=(b_mix_w, m_b_mix_w, v_b_mix_w), b_scale=(b_scale, m_b_scale, v_b_scale),
                   c_kv_norm_g=(c_kv_norm_g, m_c_kv_norm_g, v_c_kv_norm_g), d_w_s=(d_w_s, m_d_w_s, v_d_w_s),
                   d_b_s=(d_b_s, m_d_b_s, v_d_b_s), final_norm_g=(final_norm_g, m_final_norm_g, v_final_norm_g),
                   a_conv_w=(a_conv_w, m_a_conv_w, v_a_conv_w), c_q_norm_g=(c_q_norm_g, m_c_q_norm_g, v_c_q_norm_g),
                   d_ln_g=(d_ln_g, m_d_ln_g, v_d_ln_g), d_ln_b=(d_ln_b, m_d_ln_b, v_d_ln_b),
                   ffn_conv_w=(ffn_conv_w, m_ffn_conv_w, v_ffn_conv_w))
    small_out = _adamw_small("adamw_small", [tuple(a.reshape(small_view[n]) for a in small_w[n]) for n in small_names],
                             list(small_parts))
    for n, outs in zip(small_names, small_out):
        res[n] = [o_.reshape(small_w[n][0].shape) for o_ in outs]

    order = ["ada_w", "ada_b", "norm1_g", "norm2_g", "ab_w_in", "a_conv_w", "b_mix_w", "b_scale", "ab_w_out", "cd_w_in", "c_q_norm_g",
             "c_w_uq", "c_kv_norm_g", "c_w_ukv", "d_ln_g", "d_ln_b", "d_w_s", "d_b_s", "cd_w_out", "ffn_w_up", "ffn_conv_w",
             "ffn_w_down", "final_norm_g"]
    return (loss, dx0[None], *[res[n][0] for n in order], *[res[n][1] for n in order], *[res[n][2] for n in order],
            *[res[n][3] for n in order])
```

```python
import functools
import math

import jax
import jax.numpy as jnp
from jax import lax
from jax.experimental import pallas as pl
from jax.experimental.pallas import tpu as pltpu

F32 = jnp.float32
BF16 = jnp.bfloat16
_MXU_DTYPE = BF16
WIRE_DTYPE = BF16
ACT_DTYPE = BF16
_VMEM_LIMIT = 56 * 2 ** 20
N_DEV = 8
EPS = 1e-6
POOL_WINDOWS = (2, 4, 8, 16)
ATTN_SCALE = (64 + 32) ** -0.5
ADAM_LR, ADAM_B1, ADAM_B2, ADAM_EPS, ADAM_WD, ADAM_STEP = 0.001, 0.9, 0.999, 1e-08, 0.01, 10
MESH = pl.DeviceIdType.MESH
ANY = pl.BlockSpec(memory_space=pl.ANY)


def _cp(*sem):
    return pltpu.CompilerParams(dimension_semantics=sem, vmem_limit_bytes=_VMEM_LIMIT)


def _dot(a, b, contract):
    dn = {"nn": (((1,), (0,)), ((), ())), "nt": (((1,), (1,)), ((), ())), "tn": (((0,), (0,)), ((), ()))}[contract]
    return lax.dot_general(a.astype(_MXU_DTYPE), b.astype(_MXU_DTYPE), dn, preferred_element_type=F32)


def _my_position():
    x, y, c = lax.axis_index("x"), lax.axis_index("y"), lax.axis_index("c")
    return x, y, c, 4 * x + 2 * y + c


def _exchange(name, groups, scatter):
    flat = [a for g in groups for a in g]
    n_in, n_grp = len(flat), len(groups)
    out_shapes = []
    for g in groups:
        slab = g[0].shape[1:] if scatter else g[0].shape
        out_shapes.append(jax.ShapeDtypeStruct((N_DEV, len(g)) + tuple(slab), g[0].dtype))

    def body(*refs):
        ins, outs = refs[:n_in], refs[n_in:n_in + n_grp]
        send_sems, recv_sems, local_sems = refs[n_in + n_grp:]
        x, y, c, me = _my_position()
        i = 0
        for gi, g in enumerate(groups):
            for l in range(len(g)):
                src = ins[i]
                i += 1
                pltpu.make_async_copy(src.at[me] if scatter else src, outs[gi].at[me, l], local_sems.at[gi]).start()
                for k in range(1, N_DEV):
                    px = 1 - x if k & 4 else x
                    py = 1 - y if k & 2 else y
                    pc = 1 - c if k & 1 else c
                    peer = 4 * px + 2 * py + pc
                    pltpu.make_async_remote_copy(
                        src_ref=src.at[peer] if scatter else src, dst_ref=outs[gi].at[me, l],
                        send_sem=send_sems.at[gi], recv_sem=recv_sems.at[gi],
                        device_id=(px, py, pc), device_id_type=MESH).start()
        for gi in range(n_grp):
            mine = outs[gi].at[me]
            pltpu.make_async_copy(mine, mine, local_sems.at[gi]).wait()
            seven = outs[gi].at[pl.ds(0, N_DEV - 1)]
            w = pltpu.make_async_remote_copy(src_ref=seven, dst_ref=seven, send_sem=send_sems.at[gi],
                                             recv_sem=recv_sems.at[gi], device_id=(x, y, c), device_id_type=MESH)
            w.wait_send()
            w.wait_recv()

    return pl.pallas_call(
        body, name=name, out_shape=tuple(out_shapes),
        in_specs=[ANY] * n_in, out_specs=tuple([ANY] * n_grp),
        scratch_shapes=[pltpu.SemaphoreType.DMA((n_grp,)), pltpu.SemaphoreType.DMA((n_grp,)),
                        pltpu.SemaphoreType.DMA((n_grp,))],
        compiler_params=pltpu.CompilerParams(has_side_effects=True),
    )(*flat)


HBM_SPEC = pl.BlockSpec(memory_space=pltpu.HBM)
SEM_SPEC = pl.BlockSpec(memory_space=pltpu.SEMAPHORE)
EFFECT = pltpu.SideEffectType.DATAFLOW_SIDE_EFFECTING


def _put_mine(name, srcs, scatter, me):
    n = len(srcs)
    slabs = [tuple(s.shape[1:] if sc else s.shape) for s, sc in zip(srcs, scatter)]

    def body(me_ref, *refs):
        for i in range(n):
            refs[n + i][...] = refs[i][...]

    def at_me(slab):
        return pl.BlockSpec((None,) + slab, lambda g, me_ref, nd=len(slab): (me_ref[0],) + (0,) * nd)

    def whole(slab):
        return pl.BlockSpec(slab, lambda g, me_ref, nd=len(slab): (0,) * nd)

    return pl.pallas_call(
        body, name=name,
        grid_spec=pltpu.PrefetchScalarGridSpec(
            num_scalar_prefetch=1, grid=(1,),
            in_specs=[at_me(slab) if sc else whole(slab) for slab, sc in zip(slabs, scatter)],
            out_specs=[at_me(slab) for slab in slabs]),
        out_shape=[jax.ShapeDtypeStruct((N_DEV,) + slab, s.dtype) for slab, s in zip(slabs, srcs)],
        compiler_params=_cp("arbitrary"))(me.reshape(1), *srcs)


def _exchange_start(name, srcs, scatter, after, me):
    n = len(srcs)
    scatter = list(scatter) if isinstance(scatter, (list, tuple)) else [scatter] * n
    lands = _put_mine(name + "_mine", srcs, scatter, me)
    srcs = [pltpu.with_memory_space_constraint(a, pltpu.HBM) for a in srcs]
    lands = [pltpu.with_memory_space_constraint(a, pltpu.HBM) for a in lands]

    def body(*refs):
        ins, land = refs[:n], refs[n:2 * n]
        send_sems, recv_sems, token = refs[2 * n + 1], refs[2 * n + 2], refs[-1]
        x, y, c, me_in = _my_position()
        for i in range(n):
            for k in range(1, N_DEV):
                px = 1 - x if k & 4 else x
                py = 1 - y if k & 2 else y
                pc = 1 - c if k & 1 else c
                pltpu.make_async_remote_copy(
                    src_ref=ins[i].at[4 * px + 2 * py + pc] if scatter[i] else ins[i], dst_ref=land[i].at[me_in],
                    send_sem=send_sems.at[i], recv_sem=recv_sems.at[i],
                    device_id=(px, py, pc), device_id_type=MESH).start()
        token[...] = jnp.zeros_like(token)

    outs = pl.pallas_call(
        body, name=name,
        out_shape=(pltpu.SemaphoreType.DMA((n,)), pltpu.SemaphoreType.DMA((n,)),
                   *[pltpu.HBM(a.shape, a.dtype) for a in srcs], *[pltpu.HBM(a.shape, a.dtype) for a in lands],
                   jax.ShapeDtypeStruct((8, 128), F32)),
        in_specs=[HBM_SPEC] * (2 * n) + [ANY],
        out_specs=(SEM_SPEC, SEM_SPEC, *[HBM_SPEC] * (2 * n), pl.BlockSpec(memory_space=pltpu.VMEM)),
        input_output_aliases={i: 2 + i for i in range(2 * n)},
        compiler_params=pltpu.CompilerParams(has_side_effects=EFFECT),
    )(*srcs, *lands, after)
    return (outs[0], outs[1], outs[2:2 + n], outs[2 + n:2 + 2 * n]), outs[-1]


def _exchange_wait(name, handle, after, which=None):
    send_sems, recv_sems, srcs, lands = handle
    which = list(range(len(srcs))) if which is None else list(which)
    srcs, lands = [srcs[i] for i in which], [lands[i] for i in which]
    n = len(srcs)

    def body(*refs):
        land, send_ref, recv_ref = refs[n:2 * n], refs[2 * n], refs[2 * n + 1]
        x, y, c, _ = _my_position()
        for k, i in enumerate(which):
            seven = land[k].at[pl.ds(0, N_DEV - 1)]
            w = pltpu.make_async_remote_copy(src_ref=seven, dst_ref=seven, send_sem=send_ref.at[i], recv_sem=recv_ref.at[i],
                                             device_id=(x, y, c), device_id_type=MESH)
            w.wait_send()
            w.wait_recv()

    outs = pl.pallas_call(
        body, name=name,
        out_shape=(*[pltpu.HBM(a.shape, a.dtype) for a in srcs], *[pltpu.HBM(a.shape, a.dtype) for a in lands]),
        in_specs=[HBM_SPEC] * (2 * n) + [SEM_SPEC, SEM_SPEC, ANY],
        out_specs=tuple([HBM_SPEC] * (2 * n)),
        input_output_aliases={i: i for i in range(2 * n)},
        compiler_params=pltpu.CompilerParams(has_side_effects=EFFECT),
    )(*srcs, *lands, send_sems, recv_sems, after)
    return outs[n:]


def _other_chips(x, y):
    return [(1 - x, y), (x, 1 - y), (1 - x, 1 - y)]


def _hier_gather_start(name, srcs, after, me):
    n = len(srcs)
    lands = _put_mine(name + "_mine", srcs, [False] * n, me)
    srcs = [pltpu.with_memory_space_constraint(a, pltpu.HBM) for a in srcs]
    lands = [pltpu.with_memory_space_constraint(a, pltpu.HBM) for a in lands]

    def body(*refs):
        ins, land = refs[:n], refs[n:2 * n]
        ici_send, ici_recv, d2d_send, d2d_recv = refs[2 * n + 1:2 * n + 5]
        token = refs[-1]
        x, y, c, me_in = _my_position()
        for i in range(n):
            pltpu.make_async_remote_copy(src_ref=ins[i], dst_ref=land[i].at[me_in], send_sem=d2d_send.at[i], recv_sem=d2d_recv.at[i],
                                         device_id=(x, y, 1 - c), device_id_type=MESH).start()
            for px, py in _other_chips(x, y):
                pltpu.make_async_remote_copy(src_ref=ins[i], dst_ref=land[i].at[me_in], send_sem=ici_send.at[i],
                                             recv_sem=ici_recv.at[i], device_id=(px, py, c), device_id_type=MESH).start()
        token[...] = jnp.zeros_like(token)

    sem = pltpu.SemaphoreType.DMA((n,))
    outs = pl.pallas_call(
        body, name=name,
        out_shape=(sem, sem, sem, sem, *[pltpu.HBM(a.shape, a.dtype) for a in srcs], *[pltpu.HBM(a.shape, a.dtype) for a in lands],
                   jax.ShapeDtypeStruct((8, 128), F32)),
        in_specs=[HBM_SPEC] * (2 * n) + [ANY],
        out_specs=(SEM_SPEC,) * 4 + (HBM_SPEC,) * (2 * n) + (pl.BlockSpec(memory_space=pltpu.VMEM),),
        input_output_aliases={i: 4 + i for i in range(2 * n)},
        compiler_params=pltpu.CompilerParams(has_side_effects=EFFECT),
    )(*srcs, *lands, after)
    return (outs[:4], outs[4:4 + n], outs[4 + n:4 + 2 * n]), outs[-1]


def _hier_gather_forward(name, handle, after):
    sems, srcs, lands = handle
    n = len(srcs)

    def body(*refs):
        land = refs[n:2 * n]
        ici_send, ici_recv, d2d_send, d2d_recv = refs[2 * n:2 * n + 4]
        x, y, c, _ = _my_position()
        for i in range(n):
            three = land[i].at[pl.ds(0, 3)]
            pltpu.make_async_remote_copy(src_ref=three, dst_ref=three, send_sem=ici_send.at[i], recv_sem=ici_recv.at[i],
                                         device_id=(x, y, c), device_id_type=MESH).wait_recv()
            for px, py in _other_chips(x, y):
                slab = land[i].at[4 * px + 2 * py + c]
                pltpu.make_async_remote_copy(src_ref=slab, dst_ref=slab, send_sem=d2d_send.at[i], recv_sem=d2d_recv.at[i],
                                             device_id=(x, y, 1 - c), device_id_type=MESH).start()

    outs = pl.pallas_call(
        body, name=name,
        out_shape=(*[pltpu.HBM(a.shape, a.dtype) for a in srcs], *[pltpu.HBM(a.shape, a.dtype) for a in lands]),
        in_specs=[HBM_SPEC] * (2 * n) + [SEM_SPEC] * 4 + [ANY],
        out_specs=tuple([HBM_SPEC] * (2 * n)),
        input_output_aliases={i: i for i in range(2 * n)},
        compiler_params=pltpu.CompilerParams(has_side_effects=EFFECT),
    )(*srcs, *lands, *sems, after)
    return (sems, outs[:n], outs[n:])


def _hier_gather_wait(name, handle, after):
    sems, srcs, lands = handle
    n = len(srcs)

    def body(*refs):
        land = refs[n:2 * n]
        ici_send, ici_recv, d2d_send, d2d_recv = refs[2 * n:2 * n + 4]
        x, y, c, _ = _my_position()
        for i in range(n):
            three, four = land[i].at[pl.ds(0, 3)], land[i].at[pl.ds(0, 4)]
            pltpu.make_async_remote_copy(src_ref=three, dst_ref=three, send_sem=ici_send.at[i], recv_sem=ici_recv.at[i],
                                         device_id=(x, y, c), device_id_type=MESH).wait_send()
            w = pltpu.make_async_remote_copy(src_ref=four, dst_ref=four, send_sem=d2d_send.at[i], recv_sem=d2d_recv.at[i],
                                             device_id=(x, y, c), device_id_type=MESH)
            w.wait_send()
            w.wait_recv()

    outs = pl.pallas_call(
        body, name=name,
        out_shape=(*[pltpu.HBM(a.shape, a.dtype) for a in srcs], *[pltpu.HBM(a.shape, a.dtype) for a in lands]),
        in_specs=[HBM_SPEC] * (2 * n) + [SEM_SPEC] * 4 + [ANY],
        out_specs=tuple([HBM_SPEC] * (2 * n)),
        input_output_aliases={i: i for i in range(2 * n)},
        compiler_params=pltpu.CompilerParams(has_side_effects=EFFECT),
    )(*srcs, *lands, *sems, after)
    return outs[n:]


def _pack(arrs):
    flat = jnp.concatenate([a.reshape(-1).astype(F32) for a in arrs])
    n = flat.shape[0]
    rows = -(-n // 1024) * 8
    return jnp.pad(flat, (0, rows * 128 - n)).reshape(rows, 128)


def _unpack(buf, shapes, lead=()):
    flat = buf.reshape(lead + (-1,))
    out, off = [], 0
    for s in shapes:
        n = math.prod(s)
        out.append(flat[..., off:off + n].reshape(lead + tuple(s)))
        off += n
    return out


def _mm(name, a, a_spec, b, b_spec, out_sds, o_spec, grid, contract, nk=1, stacked=0):
    o_blk = tuple(d for d in o_spec.block_shape if d is not None)

    def body(a_ref, b_ref, o_ref, *acc):
        if stacked:
            r = _dot(a_ref[0], b_ref[0], contract)
            for q in range(1, stacked):
                r = r + _dot(a_ref[q], b_ref[q], contract)
        else:
            r = _dot(a_ref[...], b_ref[...], contract)
        if nk == 1:
            o_ref[...] = r.astype(o_ref.dtype)
        else:
            k = pl.program_id(len(grid) - 1)

            @pl.when(k == 0)
            def _():
                acc[0][...] = r

            @pl.when(k > 0)
            def _():
                acc[0][...] += r

            @pl.when(k == nk - 1)
            def _():
                o_ref[...] = acc[0][...].astype(o_ref.dtype)

    sem = ("parallel",) * (len(grid) - 1) + (("arbitrary",) if nk > 1 else ("parallel",))
    return pl.pallas_call(
        body, name=name, out_shape=out_sds, grid=grid, in_specs=[a_spec, b_spec], out_specs=o_spec,
        scratch_shapes=[pltpu.VMEM(o_blk, F32)] if nk > 1 else [], compiler_params=_cp(*sem))(a, b)


def _tile(n, want):
    t = min(n, want)
    assert n % t == 0, (n, t)
    return t


def _mm_nn(name, a, b, out_dtype=F32, tm=512, tn=512):
    (M, K), N = a.shape, b.shape[1]
    tm, tn = _tile(M, tm), _tile(N, tn)
    return _mm(name, a, pl.BlockSpec((tm, K), lambda i, j: (i, 0)), b, pl.BlockSpec((K, tn), lambda i, j: (0, j)),
               jax.ShapeDtypeStruct((M, N), out_dtype), pl.BlockSpec((tm, tn), lambda i, j: (i, j)),
               (M // tm, N // tn), "nn")


def _mm_nt(name, a, b, out_dtype=F32, tm=512, tn=512):
    (M, K), N = a.shape, b.shape[0]
    tm, tn = _tile(M, tm), _tile(N, tn)
    return _mm(name, a, pl.BlockSpec((tm, K), lambda i, j: (i, 0)), b, pl.BlockSpec((tn, K), lambda i, j: (j, 0)),
               jax.ShapeDtypeStruct((M, N), out_dtype), pl.BlockSpec((tm, tn), lambda i, j: (i, j)),
               (M // tm, N // tn), "nt")


def _mm_tn(name, a, b, out_dtype=F32, tm=512, tn=512):
    (K, M), N = a.shape, b.shape[1]
    tm, tn = _tile(M, tm), _tile(N, tn)
    return _mm(name, a, pl.BlockSpec((K, tm), lambda i, j: (0, i)), b, pl.BlockSpec((K, tn), lambda i, j: (0, j)),
               jax.ShapeDtypeStruct((M, N), out_dtype), pl.BlockSpec((tm, tn), lambda i, j: (i, j)),
               (M // tm, N // tn), "tn")


def _mm_cols(name, a, w, out_dtype=F32, tm=512):
    (M, K), (J, _, n) = a.shape, w.shape
    tm = _tile(M, tm)
    return _mm(name, a, pl.BlockSpec((tm, K), lambda j, i: (i, 0)), w, pl.BlockSpec((None, K, n), lambda j, i: (j, 0, 0)),
               jax.ShapeDtypeStruct((J, M, n), out_dtype), pl.BlockSpec((None, tm, n), lambda j, i: (j, i, 0)),
               (J, M // tm), "nn")


def _mm_cols_dx(name, d, w, out_dtype=F32, tm=512, jb=None):
    (J, M, n), K = d.shape, w.shape[1]
    tm, jb = _tile(M, tm), J if jb is None else jb
    return _mm(name, d, pl.BlockSpec((jb, tm, n), lambda i, j: (j, i, 0)), w, pl.BlockSpec((jb, K, n), lambda i, j: (j, 0, 0)),
               jax.ShapeDtypeStruct((M, K), out_dtype), pl.BlockSpec((tm, K), lambda i, j: (i, 0)),
               (M // tm, J // jb), "nt", nk=J // jb, stacked=jb)


def _mm_cols_dw(name, a, d, out_dtype=F32, tk=512):
    (M, K), (J, _, n) = a.shape, d.shape
    tk = _tile(K, tk)
    return _mm(name, a, pl.BlockSpec((M, tk), lambda j, i: (0, i)), d, pl.BlockSpec((None, M, n), lambda j, i: (j, 0, 0)),
               jax.ShapeDtypeStruct((J, K, n), out_dtype), pl.BlockSpec((None, tk, n), lambda j, i: (j, i, 0)),
               (J, K // tk), "tn")


def _mm_cols_dwt(name, a, d, out_dtype=F32, tk=512):
    (M, K), (J, _, n) = a.shape, d.shape
    tk = _tile(K, tk)
    return _mm(name, d, pl.BlockSpec((None, M, n), lambda j, i: (j, 0, 0)), a, pl.BlockSpec((M, tk), lambda j, i: (0, i)),
               jax.ShapeDtypeStruct((J, n, K), out_dtype), pl.BlockSpec((None, n, tk), lambda j, i: (j, 0, i)),
               (J, K // tk), "tn")


def _mm_rows_resid(name, a, w, resid, gate, tm=512):
    (Q, M, k), N = a.shape, w.shape[2]
    tm = _tile(M, tm)

    def body(a_ref, w_ref, r_ref, g_ref, y_ref, x_ref):
        y = _dot(a_ref[0], w_ref[0], "nn")
        for q in range(1, Q):
            y = y + _dot(a_ref[q], w_ref[q], "nn")
        y_ref[...] = y.astype(y_ref.dtype)
        x_ref[...] = r_ref[...] + g_ref[...] * y

    return pl.pallas_call(
        body, name=name, grid=(M // tm,),
        out_shape=(jax.ShapeDtypeStruct((M, N), ACT_DTYPE), jax.ShapeDtypeStruct((M, N), F32)),
        in_specs=[pl.BlockSpec((Q, tm, k), lambda i: (0, i, 0)), pl.BlockSpec((Q, k, N), lambda i: (0, 0, 0)),
                  pl.BlockSpec((tm, N), lambda i: (i, 0)), pl.BlockSpec((1, N), lambda i: (0, 0))],
        out_specs=(pl.BlockSpec((tm, N), lambda i: (i, 0)), pl.BlockSpec((tm, N), lambda i: (i, 0))),
        compiler_params=_cp("parallel"))(a, w, resid, gate)


def _mm_rows_dx(name, d, w, out_dtype=F32, tm=512):
    (M, N), (Q, k, _) = d.shape, w.shape
    tm = _tile(M, tm)
    return _mm(name, d, pl.BlockSpec((tm, N), lambda q, i: (i, 0)), w, pl.BlockSpec((None, k, N), lambda q, i: (q, 0, 0)),
               jax.ShapeDtypeStruct((Q, M, k), out_dtype), pl.BlockSpec((None, tm, k), lambda q, i: (q, i, 0)),
               (Q, M // tm), "nt")


def _mm_rows_dw(name, a, d, out_dtype=F32, tn=512):
    (Q, M, k), N = a.shape, d.shape[1]
    tn = _tile(N, tn)
    return _mm(name, a, pl.BlockSpec((None, M, k), lambda q, j: (q, 0, 0)), d, pl.BlockSpec((M, tn), lambda q, j: (0, j)),
               jax.ShapeDtypeStruct((Q, k, N), out_dtype), pl.BlockSpec((None, k, tn), lambda q, j: (q, 0, j)),
               (Q, N // tn), "tn")


def _silu(v):
    return v * jax.nn.sigmoid(v)


def _ada_fwd(c16, ada_w):
    L, D, n = ada_w.shape

    def body(c_ref, w_ref, o_ref):
        o_ref[...] = _dot(_silu(c_ref[...]), w_ref[...], "nn")

    return pl.pallas_call(
        body, name="ada_fwd", grid=(L,), out_shape=jax.ShapeDtypeStruct((L, 16, n), F32),
        in_specs=[pl.BlockSpec((16, D), lambda l: (0, 0)), pl.BlockSpec((None, D, n), lambda l: (l, 0, 0))],
        out_specs=pl.BlockSpec((None, 16, n), lambda l: (l, 0, 0)), compiler_params=_cp("parallel"))(c16, ada_w)


def _ada_bwd(c16, dmod16):
    L, _, n = dmod16.shape
    D = c16.shape[1]

    def body(c_ref, d_ref, o_ref):
        o_ref[...] = _dot(_silu(c_ref[...]), d_ref[...], "tn")

    return pl.pallas_call(
        body, name="ada_bwd", grid=(L,), out_shape=jax.ShapeDtypeStruct((L, D, n), F32),
        in_specs=[pl.BlockSpec((16, D), lambda l: (0, 0)), pl.BlockSpec((None, 16, n), lambda l: (l, 0, 0))],
        out_specs=pl.BlockSpec((None, D, n), lambda l: (l, 0, 0)), compiler_params=_cp("parallel"))(c16, dmod16)


def _row_spec(tr, n):
    return pl.BlockSpec((tr, n), lambda i: (i, 0))


def _vec_spec(n):
    return pl.BlockSpec((1, n), lambda i: (0, 0))


def _rmsmod_fwd(name, x, g, sc, sh, after, tr=512):
    S, D = x.shape

    def body(x_ref, g_ref, sc_ref, sh_ref, after_ref, h_ref):
        xv = x_ref[...]
        rstd = lax.rsqrt(jnp.mean(xv * xv, axis=-1, keepdims=True) + EPS)
        y = xv * rstd * g_ref[...]
        h_ref[...] = (y * (1.0 + sc_ref[...]) + sh_ref[...]).astype(h_ref.dtype)

    return pl.pallas_call(
        body, name=name, grid=(S // tr,), out_shape=jax.ShapeDtypeStruct((S, D), _MXU_DTYPE),
        in_specs=[_row_spec(tr, D), _vec_spec(D), _vec_spec(D), _vec_spec(D), ANY], out_specs=_row_spec(tr, D),
        compiler_params=_cp("parallel"))(x, g, sc, sh, after)


def _acc_rows(ref, val, first):
    s = jnp.sum(val, axis=0, keepdims=True)

    @pl.when(first)
    def _():
        ref[...] = s

    @pl.when(jnp.logical_not(first))
    def _():
        ref[...] += s


def _gate_bwd_tail(dx, y_ref, gate_ref, dy_ref, dgate_ref, first):
    dy_ref[...] = (gate_ref[...] * dx).astype(dy_ref.dtype)
    _acc_rows(dgate_ref, dx * y_ref[...].astype(F32), first)


def _rmsmod_bwd(name, x, g, sc, dh, dres, after, y=None, gate=None, tr=512):
    S, D = x.shape
    tail = y is not None

    def body(x_ref, g_ref, sc_ref, dh_ref, dres_ref, after_ref, *rest):
        (y_ref, gate_ref), rest = (rest[:2], rest[2:]) if tail else ((None, None), rest)
        dx_ref, dg_ref, dsc_ref, dsh_ref = rest[:4]
        first = pl.program_id(0) == 0
        xv, dh_v, gv = x_ref[...], dh_ref[...].astype(F32), g_ref[...]
        rstd = lax.rsqrt(jnp.mean(xv * xv, axis=-1, keepdims=True) + EPS)
        xhat = xv * rstd
        _acc_rows(dsh_ref, dh_v, first)
        _acc_rows(dsc_ref, dh_v * (xhat * gv), first)
        dyg = dh_v * (1.0 + sc_ref[...])
        _acc_rows(dg_ref, dyg * xhat, first)
        dxhat = dyg * gv
        dx = dres_ref[...] + rstd * (dxhat - xhat * jnp.mean(dxhat * xhat, axis=-1, keepdims=True))
        dx_ref[...] = dx
        if tail:
            _gate_bwd_tail(dx, y_ref, gate_ref, rest[4], rest[5], first)

    vec = jax.ShapeDtypeStruct((1, D), F32)
    return pl.pallas_call(
        body, name=name, grid=(S // tr,),
        out_shape=(jax.ShapeDtypeStruct((S, D), F32), vec, vec, vec) + ((jax.ShapeDtypeStruct((S, D), _MXU_DTYPE), vec) if tail else ()),
        in_specs=[_row_spec(tr, D), _vec_spec(D), _vec_spec(D), _row_spec(tr, D), _row_spec(tr, D), ANY]
        + ([_row_spec(tr, D), _vec_spec(D)] if tail else []),
        out_specs=(_row_spec(tr, D), _vec_spec(D), _vec_spec(D), _vec_spec(D)) + ((_row_spec(tr, D), _vec_spec(D)) if tail else ()),
        compiler_params=_cp("arbitrary"))(x, g, sc, dh, dres, after, *((y, gate) if tail else ()))


def _loss_head(x, g, target, y, gate, tr=512):
    S, D = x.shape

    def body(x_ref, g_ref, t_ref, y_ref, gate_ref, loss_ref, dx_ref, dg_ref, dy_ref, dgate_ref):
        first = pl.program_id(0) == 0
        xv, gv = x_ref[...], g_ref[...]
        rstd = lax.rsqrt(jnp.mean(xv * xv, axis=-1, keepdims=True) + EPS)
        xhat = xv * rstd
        err = xhat * gv - t_ref[...]
        part = 0.5 * jnp.sum(jnp.mean(err * err, axis=-1, keepdims=True), axis=0, keepdims=True)

        @pl.when(first)
        def _():
            loss_ref[...] = part

        @pl.when(jnp.logical_not(first))
        def _():
            loss_ref[...] += part

        dout = err * (1.0 / D)
        _acc_rows(dg_ref, dout * xhat, first)
        dxhat = dout * gv
        dx = rstd * (dxhat - xhat * jnp.mean(dxhat * xhat, axis=-1, keepdims=True))
        dx_ref[...] = dx
        _gate_bwd_tail(dx, y_ref, gate_ref, dy_ref, dgate_ref, first)

    vec = jax.ShapeDtypeStruct((1, D), F32)
    return pl.pallas_call(
        body, name="loss_head", grid=(S // tr,),
        out_shape=(jax.ShapeDtypeStruct((1, 1), F32), jax.ShapeDtypeStruct((S, D), F32), vec,
                   jax.ShapeDtypeStruct((S, D), _MXU_DTYPE), vec),
        in_specs=[_row_spec(tr, D), _vec_spec(D), _row_spec(tr, D), _row_spec(tr, D), _vec_spec(D)],
        out_specs=(pl.BlockSpec((1, 1), lambda i: (0, 0)), _row_spec(tr, D), _vec_spec(D), _row_spec(tr, D), _vec_spec(D)),
        compiler_params=_cp("arbitrary"))(x, g, target, y, gate)


def _gate_bwd(name, dx, y, gate, tr=256):
    S, D = dx.shape

    def body(dx_ref, y_ref, g_ref, dy_ref, dg_ref):
        dxv = dx_ref[...]
        dy_ref[...] = (g_ref[...] * dxv).astype(dy_ref.dtype)
        _acc_rows(dg_ref, dxv * y_ref[...], pl.program_id(0) == 0)

    return pl.pallas_call(
        body, name=name, grid=(S // tr,),
        out_shape=(jax.ShapeDtypeStruct((S, D), _MXU_DTYPE), jax.ShapeDtypeStruct((1, D), F32)),
        in_specs=[_row_spec(tr, D), _row_spec(tr, D), _vec_spec(D)], out_specs=(_row_spec(tr, D), _vec_spec(D)),
        compiler_params=_cp("arbitrary"))(dx, y, gate)


def _shift_down(v, k):
    t = lax.broadcasted_iota(jnp.int32, v.shape, 0)
    return jnp.where(t >= k, pltpu.roll(v, k, axis=0), 0.0)


def _shift_up(v, k):
    n = v.shape[0]
    t = lax.broadcasted_iota(jnp.int32, v.shape, 0)
    return jnp.where(t < n - k, pltpu.roll(v, n - k, axis=0), 0.0)


def _window_sum(p, w, shift):
    s, k = p, 1
    while k < w:
        s = s + shift(s, k)
        k *= 2
    return s


def _pool_count(shape, w):
    t = lax.broadcasted_iota(jnp.int32, shape, 0)
    return jnp.minimum(t + 1, w).astype(F32)


def _ab_specs(S):
    zs = [pl.BlockSpec((None, S, 128), functools.partial(lambda g, q: (2 * q + g // 2, 0, g % 2), q=q)) for q in range(4)]
    return zs


def _ab_mix_fwd(z8, conv_w, mix_w, scale):
    S = z8.shape[1]

    def body(b_ref, c_ref, a_ref, p_ref, w_ref, mix_ref, sc_ref, y_ref):
        g = pl.program_id(0)
        cg = c_ref[...] * a_ref[...]
        w = w_ref[...]
        conv = w[0:1] * _shift_down(cg, 2) + w[1:2] * _shift_down(cg, 1) + w[2:3] * cg
        y_ref[0] = (b_ref[...] * conv).astype(y_ref.dtype)
        for gg, win in enumerate(POOL_WINDOWS):
            @pl.when(g == gg)
            def _(win=win):
                p = p_ref[...]
                pooled = _window_sum(p, win, _shift_down) / _pool_count(p.shape, win) - p
                y_ref[1] = (_dot(pooled, mix_ref[...], "nn") * sc_ref[...]).astype(y_ref.dtype)

    return pl.pallas_call(
        body, name="ab_mix_fwd", grid=(4,), out_shape=jax.ShapeDtypeStruct((2, S, 512), _MXU_DTYPE),
        in_specs=_ab_specs(S) + [pl.BlockSpec((3, 128), lambda g: (0, g)), pl.BlockSpec((None, 128, 128), lambda g: (g, 0, 0)),
                                 pl.BlockSpec((1, 128), lambda g: (0, g))],
        out_specs=pl.BlockSpec((2, S, 128), lambda g: (0, 0, g)), compiler_params=_cp("parallel"))(z8, z8, z8, z8, conv_w, mix_w, scale)


def _ab_mix_bwd(z8, dycat2, conv_w, mix_w, scale, after):
    S = z8.shape[1]

    def body(b_ref, c_ref, a_ref, p_ref, dy_ref, w_ref, mix_ref, sc_ref, after_ref, dz_ref, dw_ref, dmix_ref, dsc_ref):
        g = pl.program_id(0)
        bv, cv, av, w = b_ref[...], c_ref[...], a_ref[...], w_ref[...]
        dya = dy_ref[0]
        cg = cv * av
        cg1, cg2 = _shift_down(cg, 1), _shift_down(cg, 2)
        conv = w[0:1] * cg2 + w[1:2] * cg1 + w[2:3] * cg
        dz_ref[0] = (dya * conv).astype(dz_ref.dtype)
        dconv = dya * bv
        dcg = w[2:3] * dconv + w[1:2] * _shift_up(dconv, 1) + w[0:1] * _shift_up(dconv, 2)
        dz_ref[1] = (dcg * av).astype(dz_ref.dtype)
        dz_ref[2] = (dcg * cv).astype(dz_ref.dtype)
        dw_ref[0:1, :] = jnp.sum(dconv * cg2, axis=0, keepdims=True)
        dw_ref[1:2, :] = jnp.sum(dconv * cg1, axis=0, keepdims=True)
        dw_ref[2:3, :] = jnp.sum(dconv * cg, axis=0, keepdims=True)
        for gg, win in enumerate(POOL_WINDOWS):
            @pl.when(g == gg)
            def _(win=win):
                p, dyb, mix = p_ref[...], dy_ref[1], mix_ref[...]
                cnt = _pool_count(p.shape, win)
                pooled = _window_sum(p, win, _shift_down) / cnt - p
                dsc_ref[...] = jnp.sum(dyb * _dot(pooled, mix, "nn"), axis=0, keepdims=True)
                dmixed = dyb * sc_ref[...]
                dmix_ref[...] = _dot(pooled, dmixed, "tn")
                dpooled = _dot(dmixed, mix, "nt")
                dz_ref[3] = (_window_sum(dpooled / cnt, win, _shift_up) - dpooled).astype(dz_ref.dtype)

    return pl.pallas_call(
        body, name="ab_mix_bwd", grid=(4,),
        out_shape=(jax.ShapeDtypeStruct((4, 2, S, 256), _MXU_DTYPE), jax.ShapeDtypeStruct((3, 512), F32),
                   jax.ShapeDtypeStruct((4, 128, 128), F32), jax.ShapeDtypeStruct((1, 512), F32)),
        in_specs=_ab_specs(S) + [pl.BlockSpec((2, S, 128), lambda g: (0, 0, g)), pl.BlockSpec((3, 128), lambda g: (0, g)),
                                 pl.BlockSpec((None, 128, 128), lambda g: (g, 0, 0)), pl.BlockSpec((1, 128), lambda g: (0, g)), ANY],
        out_specs=(pl.BlockSpec((4, None, S, 128), lambda g: (0, g // 2, 0, g % 2)), pl.BlockSpec((3, 128), lambda g: (0, g)),
                   pl.BlockSpec((None, 128, 128), lambda g: (g, 0, 0)), pl.BlockSpec((1, 128), lambda g: (0, g))),
        compiler_params=_cp("parallel"))(z8, z8, z8, z8, dycat2, conv_w, mix_w, scale, after)


HALO = 16


def _ffn_specs(S, n, tr):
    nb = S // HALO
    tile = pl.BlockSpec((2, None, tr, n), lambda j, i: (0, j, i, 0))
    prev = pl.BlockSpec((2, None, HALO, n), lambda j, i: (0, j, jnp.maximum(i * (tr // HALO) - 1, 0), 0))
    nxt = pl.BlockSpec((2, None, HALO, n), lambda j, i: (0, j, jnp.minimum((i + 1) * (tr // HALO), nb - 1), 0))
    cw = pl.BlockSpec((2, None, 3, n), lambda j, i: (0, j, 0, 0))
    return tile, prev, nxt, cw


def _shifted_rows(ext, lo, rows):
    ext = ext.astype(F32)
    return pltpu.roll(ext, 1, axis=0)[lo:lo + rows], pltpu.roll(ext, 2, axis=0)[lo:lo + rows]


def _ffn_gate_fwd(name, u24, cw24, tr=256):
    _, J, S, n = u24.shape
    tile, prev, _, cw = _ffn_specs(S, n, tr)

    def body(u_ref, up_ref, w_ref, a_ref, z_ref):
        keep = (pl.program_id(1) > 0).astype(u_ref.dtype)
        z = []
        for h in range(2):
            ext = jnp.concatenate([up_ref[h] * keep, u_ref[h]], axis=0)
            x1, x2 = _shifted_rows(ext, HALO, tr)
            w = w_ref[h]
            zh = w[0:1] * x2 + w[1:2] * x1 + w[2:3] * u_ref[h].astype(F32)
            z_ref[h] = zh.astype(z_ref.dtype)
            z.append(zh)
        a_ref[...] = (_silu(z[0]) * z[1]).astype(a_ref.dtype)

    return pl.pallas_call(
        body, name=name, grid=(J, S // tr),
        out_shape=(jax.ShapeDtypeStruct((J, S, n), _MXU_DTYPE), jax.ShapeDtypeStruct((2, J, S, n), ACT_DTYPE)),
        in_specs=[tile, prev, cw], out_specs=(pl.BlockSpec((None, tr, n), lambda j, i: (j, i, 0)), tile),
        compiler_params=_cp("parallel", "parallel"))(u24, u24, cw24)


def _ffn_gate_bwd(name, u24, z24, cw24, da4, w_up24, after, tr=256):
    _, J, S, n = u24.shape
    K = w_up24.shape[2]
    nb = S // HALO
    tile = pl.BlockSpec((2, None, tr, n), lambda i, j: (0, j, i, 0))
    nxt = pl.BlockSpec((2, None, HALO, n), lambda i, j: (0, j, jnp.minimum((i + 1) * (tr // HALO), nb - 1), 0))
    whole = lambda shape: pl.BlockSpec(shape, lambda i, j: (0,) * len(shape))

    def body(u_ref, z_ref, zn_ref, cw_ref, da_ref, dan_ref, wup_ref, after_ref, du_ref, dcw_ref, dh_ref, acc_ref):
        i, j = pl.program_id(0), pl.program_id(1)
        first = i == 0
        keep_next = (i < S // tr - 1).astype(F32)
        w = [cw_ref[h, j] for h in range(2)]
        m = tr + HALO
        zg, zu = [jnp.concatenate([z_ref[h], zn_ref[h]], axis=0).astype(F32) for h in range(2)]
        da = jnp.concatenate([da_ref[...].astype(F32), dan_ref[...].astype(F32) * keep_next], axis=0)
        sg = jax.nn.sigmoid(zg)
        dz = [da * zu * (sg * (1.0 + zg * (1.0 - sg))), da * (zg * sg)]
        dh = None
        for h in range(2):
            d = dz[h]
            d0, d1, d2 = d[:tr], pltpu.roll(d, m - 1, axis=0)[:tr], pltpu.roll(d, m - 2, axis=0)[:tr]
            du = (w[h][2:3] * d0 + w[h][1:2] * d1 + w[h][0:1] * d2).astype(du_ref.dtype)
            du_ref[h] = du
            part = _dot(du, wup_ref[h, j], "nt")
            dh = part if dh is None else dh + part
            x0 = u_ref[h].astype(F32)
            parts = [jnp.sum(x0 * dk, axis=0, keepdims=True) for dk in (d2, d1, d0)]
            for k in range(3):
                @pl.when(first)
                def _(k=k, h=h):
                    dcw_ref[h, j, k:k + 1, :] = parts[k]

                @pl.when(jnp.logical_not(first))
                def _(k=k, h=h):
                    dcw_ref[h, j, k:k + 1, :] += parts[k]

        @pl.when(j == 0)
        def _():
            acc_ref[...] = dh

        @pl.when(j > 0)
        def _():
            acc_ref[...] += dh

        @pl.when(j == J - 1)
        def _():
            dh_ref[...] = acc_ref[...].astype(dh_ref.dtype)

    da_tile = pl.BlockSpec((None, tr, n), lambda i, j: (j, i, 0))
    da_next = pl.BlockSpec((None, HALO, n), lambda i, j: (j, jnp.minimum((i + 1) * (tr // HALO), nb - 1), 0))
    return pl.pallas_call(
        body, name=name, grid=(S // tr, J),
        out_shape=(jax.ShapeDtypeStruct((2, J, S, n), _MXU_DTYPE), jax.ShapeDtypeStruct((2, J, 3, n), F32),
                   jax.ShapeDtypeStruct((S, K), ACT_DTYPE)),
        in_specs=[tile, tile, nxt, whole((2, J, 3, n)), da_tile, da_next, whole((2, J, K, n)), ANY],
        out_specs=(tile, whole((2, J, 3, n)), pl.BlockSpec((tr, K), lambda i, j: (i, 0))),
        scratch_shapes=[pltpu.VMEM((tr, K), F32)],
        compiler_params=_cp("arbitrary", "arbitrary"))(u24, z24, z24, cw24, da4, da4, w_up24, after)


def _rms_rows(v, g):
    rstd = lax.rsqrt(jnp.mean(v * v, axis=-1, keepdims=True) + EPS)
    return v * rstd * g


def _rms_rows_bwd(v, g, dy):
    rstd = lax.rsqrt(jnp.mean(v * v, axis=-1, keepdims=True) + EPS)
    vhat = v * rstd
    dvhat = dy * g
    return rstd * (dvhat - vhat * jnp.mean(dvhat * vhat, axis=-1, keepdims=True)), dy * vhat


def _rope(v, cos, sa, sb):
    return v * cos + pltpu.roll(v, 112, axis=1) * sa + pltpu.roll(v, 16, axis=1) * sb


def _rope_t(d, cos, sa, sb):
    return d * cos + pltpu.roll(d * sa, 16, axis=1) + pltpu.roll(d * sb, 112, axis=1)


def _qkv_rope_fwd(z, qg, kvg, w_uq_t, w_kv, cosq, cosk, sa, sb, tr=256):
    S = z.shape[0]

    def body(ql_ref, kvl_ref, kpe_ref, qg_ref, kvg_ref, wq_ref, wkv_ref, cq_ref, ck_ref, sa_ref, sb_ref,
             qn_ref, kvn_ref, qo_ref, ko_ref, vo_ref):
        cq, ck, sa_v, sb_v = cq_ref[...], ck_ref[...], sa_ref[...], sb_ref[...]
        qn = _rms_rows(ql_ref[...], qg_ref[...]).astype(qn_ref.dtype)
        kvn = _rms_rows(kvl_ref[...], kvg_ref[...]).astype(kvn_ref.dtype)
        qn_ref[...] = qn
        kvn_ref[...] = kvn
        q = _dot(qn, wq_ref[...], "nt")
        kv = _dot(kvn, wkv_ref[...], "nn")
        kpe = _rope(kpe_ref[...], ck, sa_v, sb_v)
        for h in range(8):
            cols = slice(128 * h, 128 * h + 128)
            qo_ref[:, cols] = _rope(q[:, cols], cq, sa_v, sb_v).astype(qo_ref.dtype)
            ko_ref[:, cols] = (kv[:, cols] + kpe).astype(ko_ref.dtype)
        vo_ref[...] = kv[:, 1024:1536].astype(vo_ref.dtype)

    tab = _row_spec(tr, 128)
    whole = lambda a: pl.BlockSpec(a.shape, lambda i: (0, 0))
    return pl.pallas_call(
        body, name="qkv_rope_fwd", grid=(S // tr,),
        out_shape=(jax.ShapeDtypeStruct((S, 256), _MXU_DTYPE), jax.ShapeDtypeStruct((S, 128), _MXU_DTYPE),
                   jax.ShapeDtypeStruct((S, 1024), _MXU_DTYPE), jax.ShapeDtypeStruct((S, 1024), _MXU_DTYPE),
                   jax.ShapeDtypeStruct((S, 512), _MXU_DTYPE)),
        in_specs=[pl.BlockSpec((tr, 256), lambda i: (i, 0)), pl.BlockSpec((tr, 128), lambda i: (i, 2)),
                  pl.BlockSpec((tr, 128), lambda i: (i, 3)), _vec_spec(256), _vec_spec(128), whole(w_uq_t), whole(w_kv),
                  tab, tab, tab, tab],
        out_specs=(_row_spec(tr, 256), _row_spec(tr, 128), _row_spec(tr, 1024), _row_spec(tr, 1024), _row_spec(tr, 512)),
        compiler_params=_cp("parallel"))(z, z, z, qg, kvg, w_uq_t, w_kv, cosq, cosk, sa, sb)


def _attn_bwd_prep(o, dycat2, tr=256):
    S = o.shape[0]

    def body(o_ref, do_ref, delta_ref, doa_ref, dob_ref):
        do = do_ref[...]
        prod = do * o_ref[...]
        lane = lax.broadcasted_iota(jnp.int32, do.shape, 1)
        for p in range(4):
            cols = slice(128 * p, 128 * p + 128)
            first = lane[:, cols] < 128 * p + 64
            da = jnp.sum(jnp.where(first, prod[:, cols], 0.0), axis=-1, keepdims=True)
            db = jnp.sum(jnp.where(first, 0.0, prod[:, cols]), axis=-1, keepdims=True)
            delta_ref[p] = jnp.where(first, da, db)
            doa_ref[p] = jnp.where(first, do[:, cols], 0.0).astype(doa_ref.dtype)
            dob_ref[p] = jnp.where(first, 0.0, do[:, cols]).astype(dob_ref.dtype)

    pair = pl.BlockSpec((4, tr, 128), lambda i: (0, i, 0))
    return pl.pallas_call(
        body, name="attn_bwd_prep", grid=(S // tr,),
        out_shape=(jax.ShapeDtypeStruct((4, S, 128), F32), jax.ShapeDtypeStruct((4, S, 128), _MXU_DTYPE),
                   jax.ShapeDtypeStruct((4, S, 128), _MXU_DTYPE)),
        in_specs=[_row_spec(tr, 512), pl.BlockSpec((None, tr, 512), lambda i: (0, i, 0))],
        out_specs=(pair, pair, pair), compiler_params=_cp("parallel"))(o, dycat2)


def _qkv_rope_bwd(z, qg, kvg, dq, dk, dv, duv, w_uq_t, w_kv, cosq, cosk, sa, sb, tr=256):
    S = z.shape[0]

    def body(ql_ref, kvl_ref, qg_ref, kvg_ref, dq_ref, dk_ref, dv_ref, duv_ref, wq_ref, wkv_ref, cq_ref, ck_ref, sa_ref, sb_ref,
             dqo_ref, dkv_ref, dz_ref, dqg_ref, dkvg_ref):
        first = pl.program_id(0) == 0
        cq, ck, sa_v, sb_v = cq_ref[...], ck_ref[...], sa_ref[...], sb_ref[...]
        tot = jnp.zeros((tr, 128), F32)
        for h in range(8):
            cols = slice(128 * h, 128 * h + 128)
            dqo_ref[:, cols] = _rope_t(dq_ref[:, cols], cq, sa_v, sb_v).astype(dqo_ref.dtype)
            dkh = dk_ref[:, cols]
            tot = tot + dkh
            dkv_ref[:, cols] = dkh.astype(dkv_ref.dtype)
        dkv_ref[:, 1024:1536] = dv_ref[...].astype(dkv_ref.dtype)
        dqn = _dot(dqo_ref[...], wq_ref[...], "nn")
        dkvn = _dot(dkv_ref[...], wkv_ref[...], "nt")
        dql, dqg = _rms_rows_bwd(ql_ref[...], qg_ref[...], dqn)
        dkvl, dkvg = _rms_rows_bwd(kvl_ref[...], kvg_ref[...], dkvn)
        _acc_rows(dqg_ref, dqg, first)
        _acc_rows(dkvg_ref, dkvg, first)
        dz_ref[:, 0:256] = dql.astype(dz_ref.dtype)
        dz_ref[:, 256:384] = dkvl.astype(dz_ref.dtype)
        dz_ref[:, 384:512] = _rope_t(tot, ck, sa_v, sb_v).astype(dz_ref.dtype)
        dz_ref[:, 512:1536] = duv_ref[...].astype(dz_ref.dtype)

    tab = _row_spec(tr, 128)
    whole = lambda a: pl.BlockSpec(a.shape, lambda i: (0, 0))
    return pl.pallas_call(
        body, name="qkv_rope_bwd", grid=(S // tr,),
        out_shape=(jax.ShapeDtypeStruct((S, 1024), _MXU_DTYPE), jax.ShapeDtypeStruct((S, 1536), _MXU_DTYPE),
                   jax.ShapeDtypeStruct((S, 1536), _MXU_DTYPE), jax.ShapeDtypeStruct((1, 256), F32), jax.ShapeDtypeStruct((1, 128), F32)),
        in_specs=[pl.BlockSpec((tr, 256), lambda i: (i, 0)), pl.BlockSpec((tr, 128), lambda i: (i, 2)), _vec_spec(256), _vec_spec(128),
                  _row_spec(tr, 1024), _row_spec(tr, 1024), _row_spec(tr, 512), _row_spec(tr, 1024), whole(w_uq_t), whole(w_kv),
                  tab, tab, tab, tab],
        out_specs=(_row_spec(tr, 1024), _row_spec(tr, 1536), _row_spec(tr, 1536), _vec_spec(256), _vec_spec(128)),
        compiler_params=_cp("arbitrary"))(z, z, qg, kvg, dq, dk, dv, duv, w_uq_t, w_kv, cosq, cosk, sa, sb)


NEG = -1e30


def _attn_fwd(q, k, v, tq=512, tk=512):
    S = q.shape[0]
    assert tq == tk

    def body(q_ref, k_ref, v_ref, o_ref, lse_ref):
        i = pl.program_id(1)
        qs = [q_ref[:, 0:128], q_ref[:, 128:256]]

        def step(kb, carry, diagonal=False):
            start = pl.multiple_of(kb * tk, tk)
            vv = v_ref[pl.ds(start, tk), :]
            out = []
            for h in range(2):
                m, l, acc = carry[3 * h:3 * h + 3]
                s = _dot(qs[h], k_ref[pl.ds(start, tk), 128 * h:128 * h + 128], "nt") * ATTN_SCALE
                if diagonal:
                    s = jnp.where(below, s, NEG)
                m_new = jnp.maximum(m, jnp.max(s, axis=-1, keepdims=True))
                alpha = jnp.exp(m - m_new)
                p = jnp.exp(s - m_new)
                out += [m_new, alpha * l + jnp.sum(p, axis=-1, keepdims=True), alpha * acc + _dot(p, vv, "nn")]
            return tuple(out)

        below = lax.broadcasted_iota(jnp.int32, (tq, tk), 1) <= lax.broadcasted_iota(jnp.int32, (tq, tk), 0)
        init = (jnp.full((tq, 1), NEG, F32), jnp.zeros((tq, 1), F32), jnp.zeros((tq, 128), F32)) * 2
        ma, la, acca, mb, lb, accb = step(i, lax.fori_loop(0, i, step, init), diagonal=True)
        lane = lax.broadcasted_iota(jnp.int32, (tq, 128), 1)
        o_ref[...] = jnp.where(lane < 64, acca / la, accb / lb)
        lse_ref[...] = jnp.where(lane < 64, ma + jnp.log(la), mb + jnp.log(lb))

    return pl.pallas_call(
        body, name="attn_fwd", grid=(4, S // tq),
        out_shape=(jax.ShapeDtypeStruct((S, 512), F32), jax.ShapeDtypeStruct((4, S, 128), F32)),
        in_specs=[pl.BlockSpec((tq, 256), lambda p, i: (i, p)), pl.BlockSpec((S, 256), lambda p, i: (0, p)),
                  pl.BlockSpec((S, 128), lambda p, i: (0, p))],
        out_specs=(pl.BlockSpec((tq, 128), lambda p, i: (i, p)), pl.BlockSpec((None, tq, 128), lambda p, i: (p, i, 0))),
        compiler_params=_cp("parallel", "parallel"))(q, k, v)


def _attn_bwd(q, k, v, lse, delta, doa, dob, tq=512, tk=512):
    S = q.shape[0]
    assert tq == tk

    def body(q_ref, k_ref, v_ref, lse_ref, delta_ref, doa_ref, dob_ref, dq_ref, dk_ref, dv_ref):
        j = pl.program_id(1)

        @pl.when(j == 0)
        def _():
            dq_ref[...] = jnp.zeros_like(dq_ref)

        below = lax.broadcasted_iota(jnp.int32, (tq, tk), 1) <= lax.broadcasted_iota(jnp.int32, (tq, tk), 0)
        ks = [k_ref[:, 0:128], k_ref[:, 128:256]]
        vv = v_ref[...]

        def step(qb, carry, diagonal=False):
            dka, dkb, dvp = carry
            start = pl.multiple_of(qb * tq, tq)
            rows = pl.ds(start, tq)
            lse_v, delta_v = lse_ref[rows, :], delta_ref[rows, :]
            dos = [doa_ref[rows, :], dob_ref[rows, :]]
            dks = [dka, dkb]
            for h in range(2):
                delta = delta_v[:, 64 * h:64 * h + 1]
                do_h = dos[h]
                qh = q_ref[rows, 128 * h:128 * h + 128]
                s = _dot(qh, ks[h], "nt") * ATTN_SCALE
                p = jnp.exp(s - lse_v[:, 64 * h:64 * h + 1])
                if diagonal:
                    p = jnp.where(below, p, 0.0)
                dvp = dvp + _dot(p, do_h, "tn")
                ds = p * (_dot(do_h, vv, "nt") - delta) * ATTN_SCALE
                dq_ref[rows, 128 * h:128 * h + 128] += _dot(ds, ks[h], "nn")
                dks[h] = dks[h] + _dot(ds, qh, "tn")
            return dks[0], dks[1], dvp

        zero = jnp.zeros((tk, 128), F32)
        dka, dkb, dvp = lax.fori_loop(j + 1, S // tq, step, step(j, (zero, zero, zero), diagonal=True))
        dk_ref[:, 0:128] = dka
        dk_ref[:, 128:256] = dkb
        dv_ref[...] = dvp

    return pl.pallas_call(
        body, name="attn_bwd", grid=(4, S // tk),
        out_shape=(jax.ShapeDtypeStruct((S, 1024), F32), jax.ShapeDtypeStruct((S, 1024), F32), jax.ShapeDtypeStruct((S, 512), F32)),
        in_specs=[pl.BlockSpec((S, 256), lambda p, j: (0, p)), pl.BlockSpec((tk, 256), lambda p, j: (j, p)),
                  pl.BlockSpec((tk, 128), lambda p, j: (j, p))] + [pl.BlockSpec((None, S, 128), lambda p, j: (p, 0, 0))] * 4,
        out_specs=(pl.BlockSpec((S, 256), lambda p, j: (0, p)), pl.BlockSpec((tk, 256), lambda p, j: (j, p)),
                   pl.BlockSpec((tk, 128), lambda p, j: (j, p))),
        compiler_params=_cp("parallel", "arbitrary"))(q, k, v, lse, delta, doa, dob)


CHUNK = 128
GELU_C = math.sqrt(2.0 / math.pi)


def _gelu(v):
    t = jnp.tanh(GELU_C * (v + 0.044715 * (v * v * v)))
    return v * (0.5 * (1.0 + t)), t


def _gelu_grad(v, t):
    return 0.5 * (1.0 + t) + v * (0.5 * (1.0 - t * t) * GELU_C * (1.0 + 3.0 * 0.044715 * v * v))


def _tril(w):
    r = lax.broadcasted_iota(jnp.int32, w.shape, 0)
    c = lax.broadcasted_iota(jnp.int32, w.shape, 1)
    return jnp.where(c <= r, w, 0.0)


def _layer_norm(v, g, b):
    xc = v - jnp.mean(v, axis=-1, keepdims=True)
    rstd = lax.rsqrt(jnp.mean(xc * xc, axis=-1, keepdims=True) + EPS)
    xhat = xc * rstd
    return xhat * g + b, xhat, rstd


def _sgu_fwd(z, o, ln_g, ln_b, w_s, b_st, tr=256):
    S = z.shape[0]

    def body(u_ref, v_ref, o_ref, g_ref, b_ref, ws_ref, bs_ref, y_ref):
        gu, _ = _gelu(u_ref[...])
        gv, _ = _gelu(v_ref[...])
        vln, _, _ = _layer_norm(gv, g_ref[...], b_ref[...])
        y_ref[0] = o_ref[...].astype(y_ref.dtype)
        for g in range(4):
            wt = _tril(ws_ref[g])
            cols = slice(128 * g, 128 * g + 128)
            for ch in range(tr // CHUNK):
                rows = slice(CHUNK * ch, CHUNK * ch + CHUNK)
                mixed = _dot(wt, vln[rows, cols], "nn") + bs_ref[:, g:g + 1]
                y_ref[1, rows, cols] = (gu[rows, cols] * mixed).astype(y_ref.dtype)

    return pl.pallas_call(
        body, name="sgu_fwd", grid=(S // tr,), out_shape=jax.ShapeDtypeStruct((2, S, 512), _MXU_DTYPE),
        in_specs=[pl.BlockSpec((tr, 512), lambda i: (i, 1)), pl.BlockSpec((tr, 512), lambda i: (i, 2)), _row_spec(tr, 512),
                  _vec_spec(512), _vec_spec(512), pl.BlockSpec((4, 128, 128), lambda i: (0, 0, 0)), pl.BlockSpec((128, 4), lambda i: (0, 0))],
        out_specs=pl.BlockSpec((2, tr, 512), lambda i: (0, i, 0)), compiler_params=_cp("parallel"))(z, z, o, ln_g, ln_b, w_s, b_st)


def _sgu_bwd(z, dycat2, ln_g, ln_b, w_s, b_st, tr=256):
    S = z.shape[0]

    def body(u_ref, v_ref, dy_ref, g_ref, b_ref, ws_ref, bs_ref, duv_ref, dg_ref, db_ref, dws_ref, dbs_ref):
        first = pl.program_id(0) == 0
        u_pre, v_pre = u_ref[...], v_ref[...]
        gu, tu = _gelu(u_pre)
        gv, tv = _gelu(v_pre)
        gain = g_ref[...]
        vln, xhat, rstd = _layer_norm(gv, gain, b_ref[...])

        @pl.when(first)
        def _():
            dws_ref[...] = jnp.zeros_like(dws_ref)
            dbs_ref[...] = jnp.zeros_like(dbs_ref)

        dvln_cols = []
        for g in range(4):
            wt = _tril(ws_ref[g])
            cols = slice(128 * g, 128 * g + 128)
            dmixed_sum = jnp.zeros((CHUNK, 128), F32)
            dw = jnp.zeros((CHUNK, CHUNK), F32)
            dvln_rows = []
            for ch in range(tr // CHUNK):
                rows = slice(CHUNK * ch, CHUNK * ch + CHUNK)
                vt = vln[rows, cols]
                mixed = _dot(wt, vt, "nn") + bs_ref[:, g:g + 1]
                dyd = dy_ref[rows, cols]
                duv_ref[rows, cols] = (dyd * mixed * _gelu_grad(u_pre[rows, cols], tu[rows, cols])).astype(duv_ref.dtype)
                dmixed = dyd * gu[rows, cols]
                dmixed_sum = dmixed_sum + dmixed
                dw = dw + _dot(dmixed, vt, "nt")
                dvln_rows.append(_dot(wt, dmixed, "tn"))
            dws_ref[g] += _tril(dw)
            dbs_ref[g:g + 1, :] += jnp.sum(dmixed_sum.T, axis=0, keepdims=True)
            dvln_cols.append(jnp.concatenate(dvln_rows, axis=0))
        dvln = jnp.concatenate(dvln_cols, axis=1)
        _acc_rows(dg_ref, dvln * xhat, first)
        _acc_rows(db_ref, dvln, first)
        dxhat = dvln * gain
        dgv = rstd * (dxhat - jnp.mean(dxhat, axis=-1, keepdims=True) - xhat * jnp.mean(dxhat * xhat, axis=-1, keepdims=True))
        duv_ref[:, 512:1024] = (dgv * _gelu_grad(v_pre, tv)).astype(duv_ref.dtype)

    return pl.pallas_call(
        body, name="sgu_bwd", grid=(S // tr,),
        out_shape=(jax.ShapeDtypeStruct((S, 1024), _MXU_DTYPE), jax.ShapeDtypeStruct((1, 512), F32), jax.ShapeDtypeStruct((1, 512), F32),
                   jax.ShapeDtypeStruct((4, 128, 128), F32), jax.ShapeDtypeStruct((4, 128), F32)),
        in_specs=[pl.BlockSpec((tr, 512), lambda i: (i, 1)), pl.BlockSpec((tr, 512), lambda i: (i, 2)),
                  pl.BlockSpec((None, tr, 512), lambda i: (1, i, 0)), _vec_spec(512), _vec_spec(512),
                  pl.BlockSpec((4, 128, 128), lambda i: (0, 0, 0)), pl.BlockSpec((128, 4), lambda i: (0, 0))],
        out_specs=(_row_spec(tr, 1024), _vec_spec(512), _vec_spec(512), pl.BlockSpec((4, 128, 128), lambda i: (0, 0, 0)),
                   pl.BlockSpec((4, 128), lambda i: (0, 0))),
        compiler_params=_cp("arbitrary"))(z, z, dycat2, ln_g, ln_b, w_s, b_st)


def _sum_parts(name, parts, tr=512):
    P, R, C = parts.shape
    tr = _tile(R, tr) if R % 8 == 0 else R

    def body(p_ref, o_ref):
        g = p_ref[0]
        for k in range(1, P):
            g = g + p_ref[k]
        o_ref[...] = g

    return pl.pallas_call(
        body, name=name, grid=(R // tr,), out_shape=jax.ShapeDtypeStruct((R, C), F32),
        in_specs=[pl.BlockSpec((P, tr, C), lambda i: (0, i, 0))], out_specs=_row_spec(tr, C),
        compiler_params=_cp("parallel"))(parts)


def _adamw_math(w, m, v, g):
    c1 = 1.0 / (1.0 - ADAM_B1 ** ADAM_STEP)
    c2 = 1.0 / (1.0 - ADAM_B2 ** ADAM_STEP)
    m2 = ADAM_B1 * m + (1.0 - ADAM_B1) * g
    v2 = ADAM_B2 * v + (1.0 - ADAM_B2) * (g * g)
    return -ADAM_LR * ((m2 * c1) / (jnp.sqrt(v2 * c2) + ADAM_EPS) + ADAM_WD * w), m2, v2


def _adamw_small(name, params, parts):
    n = len(params)

    def body(*refs):
        ins, outs = refs[:4 * n], refs[4 * n:]
        for i in range(n):
            w_ref, m_ref, v_ref, p_ref = ins[4 * i:4 * i + 4]
            g = p_ref[0].astype(F32)
            for k in range(1, N_DEV):
                g = g + p_ref[k].astype(F32)
            delta, m2, v2 = _adamw_math(w_ref[...], m_ref[...], v_ref[...], g)
            outs[4 * i][...] = g
            outs[4 * i + 1][...] = delta
            outs[4 * i + 2][...] = m2
            outs[4 * i + 3][...] = v2

    flat = [a for (w, m, v), p in zip(params, parts) for a in (w, m, v, p)]
    out = pl.pallas_call(
        body, name=name, out_shape=[jax.ShapeDtypeStruct(w.shape, F32) for (w, _, _) in params for _ in range(4)],
        compiler_params=pltpu.CompilerParams(vmem_limit_bytes=_VMEM_LIMIT))(*flat)
    return [out[4 * i:4 * i + 4] for i in range(n)]


ADAMW_BLOCK_BYTES = 36 * 2 ** 20


def _adamw(name, w, m, v, parts):
    L, R, C = w.shape
    P = parts[0].shape[0]
    row_bytes = 2 * C * (7 * 4 + P * parts[0].dtype.itemsize)
    tr = R
    if R * row_bytes > ADAMW_BLOCK_BYTES:
        tr = next(t for t in (1024, 512, 256, 128, 64, 32, 16) if R % t == 0 and t * row_bytes <= ADAMW_BLOCK_BYTES)
    nr = R // tr
    c1 = 1.0 / (1.0 - ADAM_B1 ** ADAM_STEP)
    c2 = 1.0 / (1.0 - ADAM_B2 ** ADAM_STEP)

    def body(w_ref, m_ref, v_ref, *rest):
        p_refs, (g_ref, d_ref, mo_ref, vo_ref) = rest[:L], rest[L:]
        for ll in range(L):
            @pl.when(pl.program_id(0) == ll)
            def _(p_ref=p_refs[ll]):
                g = p_ref[0].astype(F32)
                for k in range(1, P):
                    g = g + p_ref[k].astype(F32)
                m2 = ADAM_B1 * m_ref[...] + (1.0 - ADAM_B1) * g
                v2 = ADAM_B2 * v_ref[...] + (1.0 - ADAM_B2) * (g * g)
                g_ref[...] = g
                mo_ref[...] = m2
                vo_ref[...] = v2
                d_ref[...] = -ADAM_LR * ((m2 * c1) / (jnp.sqrt(v2 * c2) + ADAM_EPS) + ADAM_WD * w_ref[...])

    def part_spec(ll):
        return pl.BlockSpec((P, tr, C), lambda l, i: (0, jnp.where(l == ll, i, jnp.where(l < ll, 0, nr - 1)), 0))

    full = pl.BlockSpec((None, tr, C), lambda l, i: (l, i, 0))
    sds = jax.ShapeDtypeStruct((L, R, C), F32)
    return pl.pallas_call(
        body, name=name, grid=(L, nr), out_shape=(sds, sds, sds, sds),
        in_specs=[full] * 3 + [part_spec(ll) for ll in range(L)],
        out_specs=(full,) * 4, compiler_params=_cp("arbitrary", "arbitrary"))(w, m, v, *parts)


def _rope_tables(positions):
    half = 16
    inv_freq = 10000.0 ** (-jnp.arange(half, dtype=F32) / half)
    ang = positions.astype(F32)[:, None] * inv_freq
    cos, sin = jnp.cos(ang), jnp.sin(ang)
    S = positions.shape[0]
    z16, z32, z64 = jnp.zeros((S, 16), F32), jnp.zeros((S, 32), F32), jnp.zeros((S, 64), F32)
    cosk = jnp.concatenate([z64, cos, cos, z32], axis=1)
    cosq = jnp.concatenate([jnp.ones((S, 64), F32), cos, cos, z32], axis=1)
    sa = jnp.concatenate([z64, -sin, z16, z32], axis=1)
    sb = jnp.concatenate([z64, z16, sin, z32], axis=1)
    return cosq, cosk, sa, sb


def _ffn_fwd(l, x, mod, n2g, get_w_up8, cw24, get_w_down4):
    sh, sc, gate = mod
    h = _rmsmod_fwd(f"ffn{l}_norm", x, n2g, sc, sh, n2g)
    w_up8 = get_w_up8(h)
    u8 = _mm_cols(f"ffn{l}_up", h, w_up8, out_dtype=ACT_DTYPE, tm=2048)
    S, n = u8.shape[1], u8.shape[2]
    u24 = u8.reshape(2, 4, S, n)
    a4, z24 = _ffn_gate_fwd(f"ffn{l}_gate", u24, cw24)
    w_down4 = get_w_down4(a4)
    f, x_new = _mm_rows_resid(f"ffn{l}_down", a4, w_down4, x, gate)
    return x_new, (x, h, u24, a4, f, z24), w_up8, w_down4


def _ffn_bwd(l, dx, df, dgate, saved, mod, n2g, w_up8, cw24, w_down4, me, y_prev, gate_prev):
    sh, sc, gate = mod
    x, h, u24, a4, f, z24 = saved
    da4 = _mm_rows_dx(f"ffn{l}_down_dx", df, w_down4, out_dtype=ACT_DTYPE, tm=2048)
    dw_down4 = _mm_rows_dw(f"ffn{l}_down_dw", a4, df, out_dtype=WIRE_DTYPE)
    sent_down, token = _exchange_start(f"scatter_ffn{l}_down", [dw_down4.reshape(8, 352, dw_down4.shape[2])], True, dgate, me)
    du24, dcw24, dh = _ffn_gate_bwd(f"ffn{l}_act_bwd", u24, z24, cw24, da4, w_up8.reshape((2, 4) + w_up8.shape[1:]), token)
    du8 = du24.reshape((8,) + du24.shape[2:])
    dw_up8t = _mm_cols_dwt(f"ffn{l}_up_dw", h, du8, out_dtype=WIRE_DTYPE, tk=1024)
    sent_up, token = _exchange_start(f"scatter_ffn{l}_up", [dw_up8t], True, dcw24, me)
    dx_new, dn2g, dsc, dsh, dy_prev, dgate_prev = _rmsmod_bwd(f"ffn{l}_norm_bwd", x, n2g, sc, dh, dx, token, y_prev, gate_prev)
    return dx_new, dict(sent_up=sent_up, sent_down=sent_down, cw24=dcw24, n2g=dn2g, mod=(dsh, dsc, dgate)), dy_prev, dgate_prev


def kernel(x, c, positions, ada_w, ada_b, norm1_g, norm2_g, ab_w_in, a_conv_w, b_mix_w, b_scale, ab_w_out, cd_w_in, c_q_norm_g, c_w_uq, c_kv_norm_g, c_w_ukv, d_ln_g, d_ln_b, d_w_s, d_b_s, cd_w_out, ffn_w_up, ffn_conv_w, ffn_w_down, final_norm_g, loss_target, m_ada_w, m_ada_b, m_norm1_g, m_norm2_g, m_ab_w_in, m_a_conv_w, m_b_mix_w, m_b_scale, m_ab_w_out, m_cd_w_in, m_c_q_norm_g, m_c_w_uq, m_c_kv_norm_g, m_c_w_ukv, m_d_ln_g, m_d_ln_b, m_d_w_s, m_d_b_s, m_cd_w_out, m_ffn_w_up, m_ffn_conv_w, m_ffn_w_down, m_final_norm_g, v_ada_w, v_ada_b, v_norm1_g, v_norm2_g, v_ab_w_in, v_a_conv_w, v_b_mix_w, v_b_scale, v_ab_w_out, v_cd_w_in, v_c_q_norm_g, v_c_w_uq, v_c_kv_norm_g, v_c_w_ukv, v_d_ln_g, v_d_ln_b, v_d_w_s, v_d_b_s, v_cd_w_out, v_ffn_w_up, v_ffn_conv_w, v_ffn_w_down, v_final_norm_g):
    S, D = x.shape[1], x.shape[2]
    me = 4 * lax.axis_index("x") + 2 * lax.axis_index("y") + lax.axis_index("c")
    x0, target = x[0], loss_target[0]
    W = _MXU_DTYPE

    small_shapes = [(1024,), (3, 64), (32,), (64,), (64,), (2, 3, 704)]
    (g0,) = _exchange("gather_small", [[_pack([c, a_conv_w, c_q_norm_g, d_ln_g, d_ln_b, ffn_conv_w])]], scatter=False)
    c_all, aconv_s, qg_s, lng_s, lnb_s, fcw_s = _unpack(g0[:, 0], small_shapes, lead=(N_DEV,))
    conv_w = aconv_s.transpose(1, 0, 2).reshape(3, 512)
    qg, ln_g, ln_b = qg_s.reshape(1, 256), lng_s.reshape(1, 512), lnb_s.reshape(1, 512)
    cw24 = [fcw_s[:, l].reshape(2, 4, 3, 704) for l in range(2)]
    c16 = jnp.pad(c_all, ((0, 16 - N_DEV), (0, 0)))

    mod_cols = _ada_fwd(c16, ada_w)
    (g1,) = _exchange("gather_mod", [[_pack([mod_cols])]], scatter=False)
    mod_all = _unpack(g1[:, 0], [(2, 16, 768)], lead=(N_DEV,))[0]
    mod_mine = lax.dynamic_index_in_dim(mod_all, me, axis=2, keepdims=False)
    mod = mod_mine.transpose(1, 0, 2).reshape(2, 6 * D) + ada_b
    mods = [[mod[l, k * D:(k + 1) * D].reshape(1, D) for k in range(6)] for l in range(2)]

    gw_ab, token = _hier_gather_start("gather_w_ab", [ab_w_in[0].astype(W), ab_w_out[0].astype(W)], mod, me)
    gw_up0, token = _hier_gather_start("gather_w_ffn0_up", [ffn_w_up[0].astype(W)], token, me)
    gw_rest, started = _exchange_start("gather_w_rest", [
        ffn_w_down[0].astype(W), cd_w_in[0].T.astype(W), c_w_uq[0].T.astype(W), c_w_ukv[0].astype(W), cd_w_out[0].astype(W),
        ffn_w_up[1].astype(W), ffn_w_down[1].astype(W)], False, token, me)

    cosq, cosk, sa, sb = _rope_tables(positions[0])
    n1g = [norm1_g[l].reshape(1, D) for l in range(2)]
    n2g = [norm2_g[l].reshape(1, D) for l in range(2)]
    mix_w, scale = b_mix_w[0], b_scale
    kvg = c_kv_norm_g
    w_s, b_st = d_w_s[0], d_b_s[0].T

    sh1, sc1, g1m = mods[0][:3]
    h_ab = _rmsmod_fwd("ab_norm", x0, n1g[0], sc1, sh1, started)
    w_abin8, w_about = _hier_gather_wait("wait_w_ab", _hier_gather_forward("forward_w_ab", gw_ab, h_ab), h_ab)
    w_about2 = w_about.reshape(2, 512, D)
    z8 = _mm_cols("ab_in", h_ab, w_abin8, tm=2048)
    ycat_ab = _ab_mix_fwd(z8, conv_w, mix_w, scale)
    y_ab, x1 = _mm_rows_resid("ab_out", ycat_ab, w_about2, x0, g1m)
    w_up8, w_down4 = [None, None], [None, None]
    gw_up0 = _hier_gather_forward("forward_w_ffn0_up", gw_up0, x1)
    x2, ffn0_saved, w_up8[0], w_down4[0] = _ffn_fwd(
        0, x1, mods[0][3:], n2g[0], lambda after: _hier_gather_wait("wait_w_ffn0_up", gw_up0, after)[0], cw24[0],
        lambda after: _exchange_wait("wait_w_ffn0_down", gw_rest, after, [0])[0].reshape(4, 704, D))

    w_cdin, w_uq, w_ukv, w_cdout = _exchange_wait("wait_w_cd", gw_rest, x2, [1, 2, 3, 4])
    w_cdout2 = w_cdout.reshape(2, 512, D)
    w_cd_t = w_cdin.reshape(1440, D)
    zr = lambda n: jnp.zeros((n, D), W)
    w_cd_pad = jnp.concatenate([w_cd_t[:384], zr(64), w_cd_t[384:416], zr(32), w_cd_t[416:]], axis=0)
    w_uq_pad = jnp.pad(w_uq, ((0, 0), (0, 32), (0, 0))).reshape(1024, 256)
    w_ukv_h = w_ukv.transpose(1, 0, 2)
    w_k_pad = jnp.pad(w_ukv_h[:, :, :64], ((0, 0), (0, 0), (0, 64))).reshape(128, 1024)
    w_kv_pad = jnp.concatenate([w_k_pad, w_ukv_h[:, :, 64:].reshape(128, 512)], axis=1)

    sh1, sc1, g1c = mods[1][:3]
    h_cd = _rmsmod_fwd("cd_norm", x2, n1g[1], sc1, sh1, n1g[1])
    z_cd = _mm_nt("cd_in", h_cd, w_cd_pad)
    qn, kvn, q_r, k_r, v_r = _qkv_rope_fwd(z_cd, qg, kvg, w_uq_pad, w_kv_pad, cosq, cosk, sa, sb)
    o, lse = _attn_fwd(q_r, k_r, v_r)
    ycat_cd = _sgu_fwd(z_cd, o, ln_g, ln_b, w_s, b_st)
    y_cd, x3 = _mm_rows_resid("cd_out", ycat_cd, w_cdout2, x2, g1c)
    x4, ffn1_saved, w_up8[1], w_down4[1] = _ffn_fwd(
        1, x3, mods[1][3:], n2g[1], lambda after: _exchange_wait("wait_w_ffn1_up", gw_rest, after, [5])[0], cw24[1],
        lambda after: _exchange_wait("wait_w_ffn1_down", gw_rest, after, [6])[0].reshape(4, 704, D))

    loss_local, dx4, dfg, df1, dgate1 = _loss_head(x4, final_norm_g.reshape(1, D), target, ffn1_saved[4], mods[1][5])

    dx3, gf1, dy, dg1c = _ffn_bwd(1, dx4, df1, dgate1, ffn1_saved, mods[1][3:], n2g[1], w_up8[1], cw24[1], w_down4[1], me, y_cd, g1c)

    dycat = _mm_rows_dx("cd_out_dx", dy, w_cdout2)
    dw_cdout = _mm_rows_dw("cd_out_dw", ycat_cd, dy, out_dtype=WIRE_DTYPE)
    duv, dln_g, dln_b, dws, dbs = _sgu_bwd(z_cd, dycat, ln_g, ln_b, w_s, b_st)
    dq_r, dk_r, dv_r = _attn_bwd(q_r, k_r, v_r, lse, *_attn_bwd_prep(o, dycat))
    dqraw, dkvall, dz_cd, dqg, dkvg = _qkv_rope_bwd(z_cd, qg, kvg, dq_r, dk_r, dv_r, duv, w_uq_pad, w_kv_pad, cosq, cosk, sa, sb)
    dw_uq_pad = _mm_tn("cd_uq_dw", dqraw, qn, tn=256)
    dw_kv_pad = _mm_tn("cd_ukv_dw", kvn, dkvall, tm=128)
    dh_cd = _mm_nn("cd_in_dx", dz_cd, w_cd_pad, out_dtype=ACT_DTYPE)
    dw_cd_pad = _mm_tn("cd_in_dw", dz_cd, h_cd)
    dw_cd8 = jnp.concatenate([dw_cd_pad[:384], dw_cd_pad[448:480], dw_cd_pad[512:]], axis=0).astype(WIRE_DTYPE).reshape(8, 180, D)
    dw_uq8 = dw_uq_pad.reshape(8, 128, 256)[:, :96].astype(WIRE_DTYPE)
    dw_ukv8 = jnp.concatenate([dw_kv_pad[:, :1024].reshape(128, 8, 128)[:, :, :64], dw_kv_pad[:, 1024:].reshape(128, 8, 64)],
                              axis=2).transpose(1, 0, 2).astype(WIRE_DTYPE)
    sent_cd, token = _exchange_start("scatter_cd", [dw_cd8, dw_uq8, dw_ukv8, dw_cdout.reshape(8, 128, D)], True, dqg, me)
    early_names = ["c_kv_norm_g", "d_w_s", "d_b_s", "final_norm_g", "c_q_norm_g", "d_ln_g", "d_ln_b"]
    early_grads = [dkvg, dws.reshape(512, 128).astype(WIRE_DTYPE), dbs, dfg, dqg.reshape(8, 1, 32), dln_g.reshape(8, 1, 64),
                   dln_b.reshape(8, 1, 64)]
    early_sent, token = _exchange_start("gather_small_grads_early", early_grads, [False] * 4 + [True] * 3, token, me)
    dx2, dn1g_cd, dsc1_cd, dsh1_cd, df0, dgate0 = _rmsmod_bwd("cd_norm_bwd", x2, n1g[1], sc1, dh_cd, dx3, token,
                                                              ffn0_saved[4], mods[0][5])

    dx1, gf0, dy, dg1m = _ffn_bwd(0, dx2, df0, dgate0, ffn0_saved, mods[0][3:], n2g[0], w_up8[0], cw24[0], w_down4[0], me, y_ab, g1m)

    dw_about = _mm_rows_dw("ab_out_dw", ycat_ab, dy, out_dtype=WIRE_DTYPE)
    sent_about, token = _exchange_start("scatter_ab_out", [dw_about.reshape(8, 128, D)], True, dg1m, me)
    dycat = _mm_rows_dx("ab_out_dx", dy, w_about2)
    dz8, dconv_w, dmix_w, dscale = _ab_mix_bwd(z8, dycat, conv_w, mix_w, scale, token)
    dz8 = dz8.reshape(8, S, 256)
    dw_abin8 = _mm_cols_dw("ab_in_dw", h_ab, dz8, out_dtype=WIRE_DTYPE, tk=1024)
    sent_abin, token = _exchange_start("scatter_ab_in", [dw_abin8], True, dscale, me)
    dh_ab = _mm_cols_dx("ab_in_dx", dz8, w_abin8, out_dtype=ACT_DTYPE)
    dx0, dn1g_ab, dsc1_ab, dsh1_ab = _rmsmod_bwd("ab_norm_bwd", x0, n1g[0], mods[0][1], dh_ab, dx1, token)

    dmod = jnp.stack([jnp.concatenate([dsh1_ab, dsc1_ab, dg1m, *gf0["mod"]], axis=1)[0],
                      jnp.concatenate([dsh1_cd, dsc1_cd, dg1c, *gf1["mod"]], axis=1)[0]])
    late_names = ["ada_b", "norm1_g", "norm2_g", "b_mix_w", "b_scale", "a_conv_w", "ffn_conv_w"]
    late_grads = [dmod, jnp.concatenate([dn1g_ab, dn1g_cd]), jnp.concatenate([gf0["n2g"], gf1["n2g"]]),
                  dmix_w.reshape(512, 128).astype(WIRE_DTYPE), dscale, dconv_w.reshape(3, 8, 64).transpose(1, 0, 2),
                  jnp.stack([gf0["cw24"].reshape(8, 3, 704), gf1["cw24"].reshape(8, 3, 704)], axis=1),
                  jnp.pad(loss_local, ((0, 0), (0, 127)))]
    small_view = dict(ada_b=(2, 6 * D), norm1_g=(2, D), norm2_g=(2, D), b_mix_w=(512, 128), b_scale=(1, 512), c_kv_norm_g=(1, 128),
                      d_w_s=(512, 128), d_b_s=(4, 128), final_norm_g=(1, D),
                      a_conv_w=(3, 64), c_q_norm_g=(1, 32), d_ln_g=(1, 64), d_ln_b=(1, 64), ffn_conv_w=(2, 3, 704))
    late_sent, token = _exchange_start("gather_small_grads_late", late_grads, [False] * 5 + [True] * 2 + [False], dx0, me)

    res = {}

    def update(name, w, m, v, parts, shape3d):
        outs = _adamw("adamw_" + name, w.reshape(shape3d), m.reshape(shape3d), v.reshape(shape3d),
                      [p.reshape((p.shape[0],) + shape3d[1:]) for p in parts])
        res[name] = [o_.reshape(w.shape) for o_ in outs]

    p_cdin, p_uq, p_ukv, p_cdout = _exchange_wait("wait_scatter_cd", sent_cd, token)
    swap = lambda a: jnp.swapaxes(a, 1, 2)
    update("cd_w_in", swap(cd_w_in), swap(m_cd_w_in), swap(v_cd_w_in), [p_cdin], (1, 180, D))
    update("c_w_uq", swap(c_w_uq), swap(m_c_w_uq), swap(v_c_w_uq), [p_uq], (1, 96, 256))
    for name in ("cd_w_in", "c_w_uq"):
        res[name] = [swap(o_) for o_ in res[name]]
    update("c_w_ukv", c_w_ukv, m_c_w_ukv, v_c_w_ukv, [p_ukv], (1, 128, 128))
    update("cd_w_out", cd_w_out, m_cd_w_out, v_cd_w_out, [p_cdout], (1, 128, D))
    (p_dn1,) = _exchange_wait("wait_scatter_ffn1_down", gf1["sent_down"], token)
    (p_dn0,) = _exchange_wait("wait_scatter_ffn0_down", gf0["sent_down"], res["cd_w_out"][0])
    update("ffn_w_down", ffn_w_down, m_ffn_w_down, v_ffn_w_down, [p_dn0, p_dn1], (2, 352, D))
    (p_up1,) = _exchange_wait("wait_scatter_ffn1_up", gf1["sent_up"], token)
    (p_up0,) = _exchange_wait("wait_scatter_ffn0_up", gf0["sent_up"], res["ffn_w_down"][0])
    swap = lambda a: jnp.swapaxes(a, 1, 2)
    update("ffn_w_up", swap(ffn_w_up), swap(m_ffn_w_up), swap(v_ffn_w_up), [p_up0, p_up1], (2, 704, D))
    up_done = res["ffn_w_up"][0]
    res["ffn_w_up"] = [swap(o_) for o_ in res["ffn_w_up"]]
    (p_about,) = _exchange_wait("wait_scatter_ab_out", sent_about, up_done)
    update("ab_w_out", ab_w_out, m_ab_w_out, v_ab_w_out, [p_about], (1, 128, D))
    (p_abin,) = _exchange_wait("wait_scatter_ab_in", sent_abin, res["ab_w_out"][0])
    update("ab_w_in", ab_w_in, m_ab_w_in, v_ab_w_in, [p_abin], (1, D, 256))

    early_parts = _exchange_wait("wait_small_grads_early", early_sent, res["ab_w_in"][0])
    late_parts = _exchange_wait("wait_small_grads_late", late_sent, res["ab_w_in"][0])
    small_names = early_names + late_names
    small_parts = list(early_parts) + list(late_parts[:7])
    loss = jnp.sum(late_parts[7][:, 0, 0])
    dmod_all = late_parts[0]
    dmod_cols = lax.dynamic_slice_in_dim(dmod_all, me * 768, 768, axis=2).transpose(1, 0, 2)
    g_ada_w = _ada_bwd(c16, jnp.pad(dmod_cols, ((0, 0), (0, 16 - N_DEV), (0, 0))))
    update("ada_w", ada_w, m_ada_w, v_ada_w, [g_ada_w[l][None] for l in range(2)], (2, D, 768))

    small_w = dict(ada_b=(ada_b, m_ada_b, v_ada_b), norm1_g=(norm1_g, m_norm1_g, v_norm1_g), norm2_g=(norm2_g, m_norm2_g, v_norm2_g),
                   b_mix_w=(b_mix_w, m_b_mix_w, v_b_mix_w), b_scale=(b_scale, m_b_scale, v_b_scale),
                   c_kv_norm_g=(c_kv_norm_g, m_c_kv_norm_g, v_c_kv_norm_g), d_w_s=(d_w_s, m_d_w_s, v_d_w_s),
                   d_b_s=(d_b_s, m_d_b_s, v_d_b_s), final_norm_g=(final_norm_g, m_final_norm_g, v_final_norm_g),
                   a_conv_w=(a_conv_w, m_a_conv_w, v_a_conv_w), c_q_norm_g=(c_q_norm_g, m_c_q_norm_g, v_c_q_norm_g),
                   d_ln_g=(d_ln_g, m_d_ln_g, v_d_ln_g), d_ln_b=(d_ln_b, m_d_ln_b, v_d_ln_b),
                   ffn_conv_w=(ffn_conv_w, m_ffn_conv_w, v_ffn_conv_w))
    small_out = _adamw_small("adamw_small", [tuple(a.reshape(small_view[n]) for a in small_w[n]) for n in small_names],
                             list(small_parts))
    for n, outs in zip(small_names, small_out):
        res[n] = [o_.reshape(small_w[n][0].shape) for o_ in outs]

    order = ["ada_w", "ada_b", "norm1_g", "norm2_g", "ab_w_in", "a_conv_w", "b_mix_w", "b_scale", "ab_w_out", "cd_w_in", "c_q_norm_g",
             "c_w_uq", "c_kv_norm_g", "c_w_ukv", "d_ln_g", "d_ln_b", "d_w_s", "d_b_s", "cd_w_out", "ffn_w_up", "ffn_conv_w",
             "ffn_w_down", "final_norm_g"]
    return (loss, dx0[None], *[res[n][0] for n in order], *[res[n][1] for n in order], *[res[n][2] for n in order],
            *[res[n][3] for n in order])
```

```python
import functools
import math

import jax
import jax.numpy as jnp
from jax import lax
from jax.experimental import pallas as pl
from jax.experimental.pallas import tpu as pltpu

F32 = jnp.float32
BF16 = jnp.bfloat16
_MXU_DTYPE = BF16
WIRE_DTYPE = BF16
ACT_DTYPE = BF16
_VMEM_LIMIT = 56 * 2 ** 20
N_DEV = 8
EPS = 1e-6
POOL_WINDOWS = (2, 4, 8, 16)
ATTN_SCALE = (64 + 32) ** -0.5
ADAM_LR, ADAM_B1, ADAM_B2, ADAM_EPS, ADAM_WD, ADAM_STEP = 0.001, 0.9, 0.999, 1e-08, 0.01, 10
MESH = pl.DeviceIdType.MESH
ANY = pl.BlockSpec(memory_space=pl.ANY)


def _cp(*sem):
    return pltpu.CompilerParams(dimension_semantics=sem, vmem_limit_bytes=_VMEM_LIMIT)


def _dot(a, b, contract):
    dn = {"nn": (((1,), (0,)), ((), ())), "nt": (((1,), (1,)), ((), ())), "tn": (((0,), (0,)), ((), ()))}[contract]
    return lax.dot_general(a.astype(_MXU_DTYPE), b.astype(_MXU_DTYPE), dn, preferred_element_type=F32)


def _my_position():
    x, y, c = lax.axis_index("x"), lax.axis_index("y"), lax.axis_index("c")
    return x, y, c, 4 * x + 2 * y + c


def _exchange(name, groups, scatter):
    flat = [a for g in groups for a in g]
    n_in, n_grp = len(flat), len(groups)
    out_shapes = []
    for g in groups:
        slab = g[0].shape[1:] if scatter else g[0].shape
        out_shapes.append(jax.ShapeDtypeStruct((N_DEV, len(g)) + tuple(slab), g[0].dtype))

    def body(*refs):
        ins, outs = refs[:n_in], refs[n_in:n_in + n_grp]
        send_sems, recv_sems, local_sems = refs[n_in + n_grp:]
        x, y, c, me = _my_position()
        i = 0
        for gi, g in enumerate(groups):
            for l in range(len(g)):
                src = ins[i]
                i += 1
                pltpu.make_async_copy(src.at[me] if scatter else src, outs[gi].at[me, l], local_sems.at[gi]).start()
                for k in range(1, N_DEV):
                    px = 1 - x if k & 4 else x
                    py = 1 - y if k & 2 else y
                    pc = 1 - c if k & 1 else c
                    peer = 4 * px + 2 * py + pc
                    pltpu.make_async_remote_copy(
                        src_ref=src.at[peer] if scatter else src, dst_ref=outs[gi].at[me, l],
                        send_sem=send_sems.at[gi], recv_sem=recv_sems.at[gi],
                        device_id=(px, py, pc), device_id_type=MESH).start()
        for gi in range(n_grp):
            mine = outs[gi].at[me]
            pltpu.make_async_copy(mine, mine, local_sems.at[gi]).wait()
            seven = outs[gi].at[pl.ds(0, N_DEV - 1)]
            w = pltpu.make_async_remote_copy(src_ref=seven, dst_ref=seven, send_sem=send_sems.at[gi],
                                             recv_sem=recv_sems.at[gi], device_id=(x, y, c), device_id_type=MESH)
            w.wait_send()
            w.wait_recv()

    return pl.pallas_call(
        body, name=name, out_shape=tuple(out_shapes),
        in_specs=[ANY] * n_in, out_specs=tuple([ANY] * n_grp),
        scratch_shapes=[pltpu.SemaphoreType.DMA((n_grp,)), pltpu.SemaphoreType.DMA((n_grp,)),
                        pltpu.SemaphoreType.DMA((n_grp,))],
        compiler_params=pltpu.CompilerParams(has_side_effects=True),
    )(*flat)


HBM_SPEC = pl.BlockSpec(memory_space=pltpu.HBM)
SEM_SPEC = pl.BlockSpec(memory_space=pltpu.SEMAPHORE)
EFFECT = pltpu.SideEffectType.DATAFLOW_SIDE_EFFECTING


def _put_mine(name, srcs, scatter, me):
    n = len(srcs)
    slabs = [tuple(s.shape[1:] if sc else s.shape) for s, sc in zip(srcs, scatter)]

    def body(me_ref, *refs):
        for i in range(n):
            refs[n + i][...] = refs[i][...]

    def at_me(slab):
        return pl.BlockSpec((None,) + slab, lambda g, me_ref, nd=len(slab): (me_ref[0],) + (0,) * nd)

    def whole(slab):
        return pl.BlockSpec(slab, lambda g, me_ref, nd=len(slab): (0,) * nd)

    return pl.pallas_call(
        body, name=name,
        grid_spec=pltpu.PrefetchScalarGridSpec(
            num_scalar_prefetch=1, grid=(1,),
            in_specs=[at_me(slab) if sc else whole(slab) for slab, sc in zip(slabs, scatter)],
            out_specs=[at_me(slab) for slab in slabs]),
        out_shape=[jax.ShapeDtypeStruct((N_DEV,) + slab, s.dtype) for slab, s in zip(slabs, srcs)],
        compiler_params=_cp("arbitrary"))(me.reshape(1), *srcs)


def _exchange_start(name, srcs, scatter, after, me):
    n = len(srcs)
    scatter = list(scatter) if isinstance(scatter, (list, tuple)) else [scatter] * n
    lands = _put_mine(name + "_mine", srcs, scatter, me)
    srcs = [pltpu.with_memory_space_constraint(a, pltpu.HBM) for a in srcs]
    lands = [pltpu.with_memory_space_constraint(a, pltpu.HBM) for a in lands]

    def body(*refs):
        ins, land = refs[:n], refs[n:2 * n]
        send_sems, recv_sems, token = refs[2 * n + 1], refs[2 * n + 2], refs[-1]
        x, y, c, me_in = _my_position()
        for i in range(n):
            for k in range(1, N_DEV):
                px = 1 - x if k & 4 else x
                py = 1 - y if k & 2 else y
                pc = 1 - c if k & 1 else c
                pltpu.make_async_remote_copy(
                    src_ref=ins[i].at[4 * px + 2 * py + pc] if scatter[i] else ins[i], dst_ref=land[i].at[me_in],
                    send_sem=send_sems.at[i], recv_sem=recv_sems.at[i],
                    device_id=(px, py, pc), device_id_type=MESH).start()
        token[...] = jnp.zeros_like(token)

    outs = pl.pallas_call(
        body, name=name,
        out_shape=(pltpu.SemaphoreType.DMA((n,)), pltpu.SemaphoreType.DMA((n,)),
                   *[pltpu.HBM(a.shape, a.dtype) for a in srcs], *[pltpu.HBM(a.shape, a.dtype) for a in lands],
                   jax.ShapeDtypeStruct((8, 128), F32)),
        in_specs=[HBM_SPEC] * (2 * n) + [ANY],
        out_specs=(SEM_SPEC, SEM_SPEC, *[HBM_SPEC] * (2 * n), pl.BlockSpec(memory_space=pltpu.VMEM)),
        input_output_aliases={i: 2 + i for i in range(2 * n)},
        compiler_params=pltpu.CompilerParams(has_side_effects=EFFECT),
    )(*srcs, *lands, after)
    return (outs[0], outs[1], outs[2:2 + n], outs[2 + n:2 + 2 * n]), outs[-1]


def _exchange_wait(name, handle, after, which=None):
    send_sems, recv_sems, srcs, lands = handle
    which = list(range(len(srcs))) if which is None else list(which)
    srcs, lands = [srcs[i] for i in which], [lands[i] for i in which]
    n = len(srcs)

    def body(*refs):
        land, send_ref, recv_ref = refs[n:2 * n], refs[2 * n], refs[2 * n + 1]
        x, y, c, _ = _my_position()
        for k, i in enumerate(which):
            seven = land[k].at[pl.ds(0, N_DEV - 1)]
            w = pltpu.make_async_remote_copy(src_ref=seven, dst_ref=seven, send_sem=send_ref.at[i], recv_sem=recv_ref.at[i],
                                             device_id=(x, y, c), device_id_type=MESH)
            w.wait_send()
            w.wait_recv()

    outs = pl.pallas_call(
        body, name=name,
        out_shape=(*[pltpu.HBM(a.shape, a.dtype) for a in srcs], *[pltpu.HBM(a.shape, a.dtype) for a in lands]),
        in_specs=[HBM_SPEC] * (2 * n) + [SEM_SPEC, SEM_SPEC, ANY],
        out_specs=tuple([HBM_SPEC] * (2 * n)),
        input_output_aliases={i: i for i in range(2 * n)},
        compiler_params=pltpu.CompilerParams(has_side_effects=EFFECT),
    )(*srcs, *lands, send_sems, recv_sems, after)
    return outs[n:]


def _other_chips(x, y):
    return [(1 - x, y), (x, 1 - y), (1 - x, 1 - y)]


def _hier_gather_start(name, srcs, after, me):
    n = len(srcs)
    lands = _put_mine(name + "_mine", srcs, [False] * n, me)
    srcs = [pltpu.with_memory_space_constraint(a, pltpu.HBM) for a in srcs]
    lands = [pltpu.with_memory_space_constraint(a, pltpu.HBM) for a in lands]

    def body(*refs):
        ins, land = refs[:n], refs[n:2 * n]
        ici_send, ici_recv, d2d_send, d2d_recv = refs[2 * n + 1:2 * n + 5]
        token = refs[-1]
        x, y, c, me_in = _my_position()
        for i in range(n):
            pltpu.make_async_remote_copy(src_ref=ins[i], dst_ref=land[i].at[me_in], send_sem=d2d_send.at[i], recv_sem=d2d_recv.at[i],
                                         device_id=(x, y, 1 - c), device_id_type=MESH).start()
            for px, py in _other_chips(x, y):
                pltpu.make_async_remote_copy(src_ref=ins[i], dst_ref=land[i].at[me_in], send_sem=ici_send.at[i],
                                             recv_sem=ici_recv.at[i], device_id=(px, py, c), device_id_type=MESH).start()
        token[...] = jnp.zeros_like(token)

    sem = pltpu.SemaphoreType.DMA((n,))
    outs = pl.pallas_call(
        body, name=name,
        out_shape=(sem, sem, sem, sem, *[pltpu.HBM(a.shape, a.dtype) for a in srcs], *[pltpu.HBM(a.shape, a.dtype) for a in lands],
                   jax.ShapeDtypeStruct((8, 128), F32)),
        in_specs=[HBM_SPEC] * (2 * n) + [ANY],
        out_specs=(SEM_SPEC,) * 4 + (HBM_SPEC,) * (2 * n) + (pl.BlockSpec(memory_space=pltpu.VMEM),),
        input_output_aliases={i: 4 + i for i in range(2 * n)},
        compiler_params=pltpu.CompilerParams(has_side_effects=EFFECT),
    )(*srcs, *lands, after)
    return (outs[:4], outs[4:4 + n], outs[4 + n:4 + 2 * n]), outs[-1]


def _hier_gather_forward(name, handle, after):
    sems, srcs, lands = handle
    n = len(srcs)

    def body(*refs):
        land = refs[n:2 * n]
        ici_send, ici_recv, d2d_send, d2d_recv = refs[2 * n:2 * n + 4]
        x, y, c, _ = _my_position()
        for i in range(n):
            three = land[i].at[pl.ds(0, 3)]
            pltpu.make_async_remote_copy(src_ref=three, dst_ref=three, send_sem=ici_send.at[i], recv_sem=ici_recv.at[i],
                                         device_id=(x, y, c), device_id_type=MESH).wait_recv()
            for px, py in _other_chips(x, y):
                slab = land[i].at[4 * px + 2 * py + c]
                pltpu.make_async_remote_copy(src_ref=slab, dst_ref=slab, send_sem=d2d_send.at[i], recv_sem=d2d_recv.at[i],
                                             device_id=(x, y, 1 - c), device_id_type=MESH).start()

    outs = pl.pallas_call(
        body, name=name,
        out_shape=(*[pltpu.HBM(a.shape, a.dtype) for a in srcs], *[pltpu.HBM(a.shape, a.dtype) for a in lands]),
        in_specs=[HBM_SPEC] * (2 * n) + [SEM_SPEC] * 4 + [ANY],
        out_specs=tuple([HBM_SPEC] * (2 * n)),
        input_output_aliases={i: i for i in range(2 * n)},
        compiler_params=pltpu.CompilerParams(has_side_effects=EFFECT),
    )(*srcs, *lands, *sems, after)
    return (sems, outs[:n], outs[n:])


def _hier_gather_wait(name, handle, after):
    sems, srcs, lands = handle
    n = len(srcs)

    def body(*refs):
        land = refs[n:2 * n]
        ici_send, ici_recv, d2d_send, d2d_recv = refs[2 * n:2 * n + 4]
        x, y, c, _ = _my_position()
        for i in range(n):
            three, four = land[i].at[pl.ds(0, 3)], land[i].at[pl.ds(0, 4)]
            pltpu.make_async_remote_copy(src_ref=three, dst_ref=three, send_sem=ici_send.at[i], recv_sem=ici_recv.at[i],
                                         device_id=(x, y, c), device_id_type=MESH).wait_send()
            w = pltpu.make_async_remote_copy(src_ref=four, dst_ref=four, send_sem=d2d_send.at[i], recv_sem=d2d_recv.at[i],
                                             device_id=(x, y, c), device_id_type=MESH)
            w.wait_send()
            w.wait_recv()

    outs = pl.pallas_call(
        body, name=name,
        out_shape=(*[pltpu.HBM(a.shape, a.dtype) for a in srcs], *[pltpu.HBM(a.shape, a.dtype) for a in lands]),
        in_specs=[HBM_SPEC] * (2 * n) + [SEM_SPEC] * 4 + [ANY],
        out_specs=tuple([HBM_SPEC] * (2 * n)),
        input_output_aliases={i: i for i in range(2 * n)},
        compiler_params=pltpu.CompilerParams(has_side_effects=EFFECT),
    )(*srcs, *lands, *sems, after)
    return outs[n:]


def _pack(arrs):
    flat = jnp.concatenate([a.reshape(-1).astype(F32) for a in arrs])
    n = flat.shape[0]
    rows = -(-n // 1024) * 8
    return jnp.pad(flat, (0, rows * 128 - n)).reshape(rows, 128)


def _unpack(buf, shapes, lead=()):
    flat = buf.reshape(lead + (-1,))
    out, off = [], 0
    for s in shapes:
        n = math.prod(s)
        out.append(flat[..., off:off + n].reshape(lead + tuple(s)))
        off += n
    return out


def _mm(name, a, a_spec, b, b_spec, out_sds, o_spec, grid, contract, nk=1, stacked=0):
    o_blk = tuple(d for d in o_spec.block_shape if d is not None)

    def body(a_ref, b_ref, o_ref, *acc):
        if stacked:
            r = _dot(a_ref[0], b_ref[0], contract)
            for q in range(1, stacked):
                r = r + _dot(a_ref[q], b_ref[q], contract)
        else:
            r = _dot(a_ref[...], b_ref[...], contract)
        if nk == 1:
            o_ref[...] = r.astype(o_ref.dtype)
        else:
            k = pl.program_id(len(grid) - 1)

            @pl.when(k == 0)
            def _():
                acc[0][...] = r

            @pl.when(k > 0)
            def _():
                acc[0][...] += r

            @pl.when(k == nk - 1)
            def _():
                o_ref[...] = acc[0][...].astype(o_ref.dtype)

    sem = ("parallel",) * (len(grid) - 1) + (("arbitrary",) if nk > 1 else ("parallel",))
    return pl.pallas_call(
        body, name=name, out_shape=out_sds, grid=grid, in_specs=[a_spec, b_spec], out_specs=o_spec,
        scratch_shapes=[pltpu.VMEM(o_blk, F32)] if nk > 1 else [], compiler_params=_cp(*sem))(a, b)


def _tile(n, want):
    t = min(n, want)
    assert n % t == 0, (n, t)
    return t


def _mm_nn(name, a, b, out_dtype=F32, tm=512, tn=512):
    (M, K), N = a.shape, b.shape[1]
    tm, tn = _tile(M, tm), _tile(N, tn)
    return _mm(name, a, pl.BlockSpec((tm, K), lambda i, j: (i, 0)), b, pl.BlockSpec((K, tn), lambda i, j: (0, j)),
               jax.ShapeDtypeStruct((M, N), out_dtype), pl.BlockSpec((tm, tn), lambda i, j: (i, j)),
               (M // tm, N // tn), "nn")


def _mm_nt(name, a, b, out_dtype=F32, tm=512, tn=512):
    (M, K), N = a.shape, b.shape[0]
    tm, tn = _tile(M, tm), _tile(N, tn)
    return _mm(name, a, pl.BlockSpec((tm, K), lambda i, j: (i, 0)), b, pl.BlockSpec((tn, K), lambda i, j: (j, 0)),
               jax.ShapeDtypeStruct((M, N), out_dtype), pl.BlockSpec((tm, tn), lambda i, j: (i, j)),
               (M // tm, N // tn), "nt")


def _mm_tn(name, a, b, out_dtype=F32, tm=512, tn=512):
    (K, M), N = a.shape, b.shape[1]
    tm, tn = _tile(M, tm), _tile(N, tn)
    return _mm(name, a, pl.BlockSpec((K, tm), lambda i, j: (0, i)), b, pl.BlockSpec((K, tn), lambda i, j: (0, j)),
               jax.ShapeDtypeStruct((M, N), out_dtype), pl.BlockSpec((tm, tn), lambda i, j: (i, j)),
               (M // tm, N // tn), "tn")


def _mm_cols(name, a, w, out_dtype=F32, tm=512):
    (M, K), (J, _, n) = a.shape, w.shape
    tm = _tile(M, tm)
    return _mm(name, a, pl.BlockSpec((tm, K), lambda j, i: (i, 0)), w, pl.BlockSpec((None, K, n), lambda j, i: (j, 0, 0)),
               jax.ShapeDtypeStruct((J, M, n), out_dtype), pl.BlockSpec((None, tm, n), lambda j, i: (j, i, 0)),
               (J, M // tm), "nn")


def _mm_cols_dx(name, d, w, out_dtype=F32, tm=512, jb=None):
    (J, M, n), K = d.shape, w.shape[1]
    tm, jb = _tile(M, tm), J if jb is None else jb
    return _mm(name, d, pl.BlockSpec((jb, tm, n), lambda i, j: (j, i, 0)), w, pl.BlockSpec((jb, K, n), lambda i, j: (j, 0, 0)),
               jax.ShapeDtypeStruct((M, K), out_dtype), pl.BlockSpec((tm, K), lambda i, j: (i, 0)),
               (M // tm, J // jb), "nt", nk=J // jb, stacked=jb)


def _mm_cols_dw(name, a, d, out_dtype=F32, tk=512):
    (M, K), (J, _, n) = a.shape, d.shape
    tk = _tile(K, tk)
    return _mm(name, a, pl.BlockSpec((M, tk), lambda j, i: (0, i)), d, pl.BlockSpec((None, M, n), lambda j, i: (j, 0, 0)),
               jax.ShapeDtypeStruct((J, K, n), out_dtype), pl.BlockSpec((None, tk, n), lambda j, i: (j, i, 0)),
               (J, K // tk), "tn")


def _mm_cols_dwt(name, a, d, out_dtype=F32, tk=512):
    (M, K), (J, _, n) = a.shape, d.shape
    tk = _tile(K, tk)
    return _mm(name, d, pl.BlockSpec((None, M, n), lambda j, i: (j, 0, 0)), a, pl.BlockSpec((M, tk), lambda j, i: (0, i)),
               jax.ShapeDtypeStruct((J, n, K), out_dtype), pl.BlockSpec((None, n, tk), lambda j, i: (j, 0, i)),
               (J, K // tk), "tn")


def _mm_rows_resid(name, a, w, resid, gate, tm=512):
    (Q, M, k), N = a.shape, w.shape[2]
    tm = _tile(M, tm)

    def body(a_ref, w_ref, r_ref, g_ref, y_ref, x_ref):
        y = _dot(a_ref[0], w_ref[0], "nn")
        for q in range(1, Q):
            y = y + _dot(a_ref[q], w_ref[q], "nn")
        y_ref[...] = y.astype(y_ref.dtype)
        x_ref[...] = r_ref[...] + g_ref[...] * y

    return pl.pallas_call(
        body, name=name, grid=(M // tm,),
        out_shape=(jax.ShapeDtypeStruct((M, N), ACT_DTYPE), jax.ShapeDtypeStruct((M, N), F32)),
        in_specs=[pl.BlockSpec((Q, tm, k), lambda i: (0, i, 0)), pl.BlockSpec((Q, k, N), lambda i: (0, 0, 0)),
                  pl.BlockSpec((tm, N), lambda i: (i, 0)), pl.BlockSpec((1, N), lambda i: (0, 0))],
        out_specs=(pl.BlockSpec((tm, N), lambda i: (i, 0)), pl.BlockSpec((tm, N), lambda i: (i, 0))),
        compiler_params=_cp("parallel"))(a, w, resid, gate)


def _mm_rows_dx(name, d, w, out_dtype=F32, tm=512):
    (M, N), (Q, k, _) = d.shape, w.shape
    tm = _tile(M, tm)
    return _mm(name, d, pl.BlockSpec((tm, N), lambda q, i: (i, 0)), w, pl.BlockSpec((None, k, N), lambda q, i: (q, 0, 0)),
               jax.ShapeDtypeStruct((Q, M, k), out_dtype), pl.BlockSpec((None, tm, k), lambda q, i: (q, i, 0)),
               (Q, M // tm), "nt")


def _mm_rows_dw(name, a, d, out_dtype=F32, tn=512):
    (Q, M, k), N = a.shape, d.shape[1]
    tn = _tile(N, tn)
    return _mm(name, a, pl.BlockSpec((None, M, k), lambda q, j: (q, 0, 0)), d, pl.BlockSpec((M, tn), lambda q, j: (0, j)),
               jax.ShapeDtypeStruct((Q, k, N), out_dtype), pl.BlockSpec((None, k, tn), lambda q, j: (q, 0, j)),
               (Q, N // tn), "tn")


def _silu(v):
    return v * jax.nn.sigmoid(v)


def _ada_fwd(c16, ada_w):
    L, D, n = ada_w.shape

    def body(c_ref, w_ref, o_ref):
        o_ref[...] = _dot(_silu(c_ref[...]), w_ref[...], "nn")

    return pl.pallas_call(
        body, name="ada_fwd", grid=(L,), out_shape=jax.ShapeDtypeStruct((L, 16, n), F32),
        in_specs=[pl.BlockSpec((16, D), lambda l: (0, 0)), pl.BlockSpec((None, D, n), lambda l: (l, 0, 0))],
        out_specs=pl.BlockSpec((None, 16, n), lambda l: (l, 0, 0)), compiler_params=_cp("parallel"))(c16, ada_w)


def _ada_bwd(c16, dmod16):
    L, _, n = dmod16.shape
    D = c16.shape[1]

    def body(c_ref, d_ref, o_ref):
        o_ref[...] = _dot(_silu(c_ref[...]), d_ref[...], "tn")

    return pl.pallas_call(
        body, name="ada_bwd", grid=(L,), out_shape=jax.ShapeDtypeStruct((L, D, n), F32),
        in_specs=[pl.BlockSpec((16, D), lambda l: (0, 0)), pl.BlockSpec((None, 16, n), lambda l: (l, 0, 0))],
        out_specs=pl.BlockSpec((None, D, n), lambda l: (l, 0, 0)), compiler_params=_cp("parallel"))(c16, dmod16)


def _row_spec(tr, n):
    return pl.BlockSpec((tr, n), lambda i: (i, 0))


def _vec_spec(n):
    return pl.BlockSpec((1, n), lambda i: (0, 0))


def _rmsmod_fwd(name, x, g, sc, sh, after, tr=512):
    S, D = x.shape

    def body(x_ref, g_ref, sc_ref, sh_ref, after_ref, h_ref):
        xv = x_ref[...]
        rstd = lax.rsqrt(jnp.mean(xv * xv, axis=-1, keepdims=True) + EPS)
        y = xv * rstd * g_ref[...]
        h_ref[...] = (y * (1.0 + sc_ref[...]) + sh_ref[...]).astype(h_ref.dtype)

    return pl.pallas_call(
        body, name=name, grid=(S // tr,), out_shape=jax.ShapeDtypeStruct((S, D), _MXU_DTYPE),
        in_specs=[_row_spec(tr, D), _vec_spec(D), _vec_spec(D), _vec_spec(D), ANY], out_specs=_row_spec(tr, D),
        compiler_params=_cp("parallel"))(x, g, sc, sh, after)


def _acc_rows(ref, val, first):
    s = jnp.sum(val, axis=0, keepdims=True)

    @pl.when(first)
    def _():
        ref[...] = s

    @pl.when(jnp.logical_not(first))
    def _():
        ref[...] += s


def _gate_bwd_tail(dx, y_ref, gate_ref, dy_ref, dgate_ref, first):
    dy_ref[...] = (gate_ref[...] * dx).astype(dy_ref.dtype)
    _acc_rows(dgate_ref, dx * y_ref[...].astype(F32), first)


def _rmsmod_bwd(name, x, g, sc, dh, dres, after, y=None, gate=None, tr=512):
    S, D = x.shape
    tail = y is not None

    def body(x_ref, g_ref, sc_ref, dh_ref, dres_ref, after_ref, *rest):
        (y_ref, gate_ref), rest = (rest[:2], rest[2:]) if tail else ((None, None), rest)
        dx_ref, dg_ref, dsc_ref, dsh_ref = rest[:4]
        first = pl.program_id(0) == 0
        xv, dh_v, gv = x_ref[...], dh_ref[...].astype(F32), g_ref[...]
        rstd = lax.rsqrt(jnp.mean(xv * xv, axis=-1, keepdims=True) + EPS)
        xhat = xv * rstd
        _acc_rows(dsh_ref, dh_v, first)
        _acc_rows(dsc_ref, dh_v * (xhat * gv), first)
        dyg = dh_v * (1.0 + sc_ref[...])
        _acc_rows(dg_ref, dyg * xhat, first)
        dxhat = dyg * gv
        dx = dres_ref[...] + rstd * (dxhat - xhat * jnp.mean(dxhat * xhat, axis=-1, keepdims=True))
        dx_ref[...] = dx
        if tail:
            _gate_bwd_tail(dx, y_ref, gate_ref, rest[4], rest[5], first)

    vec = jax.ShapeDtypeStruct((1, D), F32)
    return pl.pallas_call(
        body, name=name, grid=(S // tr,),
        out_shape=(jax.ShapeDtypeStruct((S, D), F32), vec, vec, vec) + ((jax.ShapeDtypeStruct((S, D), _MXU_DTYPE), vec) if tail else ()),
        in_specs=[_row_spec(tr, D), _vec_spec(D), _vec_spec(D), _row_spec(tr, D), _row_spec(tr, D), ANY]
        + ([_row_spec(tr, D), _vec_spec(D)] if tail else []),
        out_specs=(_row_spec(tr, D), _vec_spec(D), _vec_spec(D), _vec_spec(D)) + ((_row_spec(tr, D), _vec_spec(D)) if tail else ()),
        compiler_params=_cp("arbitrary"))(x, g, sc, dh, dres, after, *((y, gate) if tail else ()))


def _loss_head(x, g, target, y, gate, tr=512):
    S, D = x.shape

    def body(x_ref, g_ref, t_ref, y_ref, gate_ref, loss_ref, dx_ref, dg_ref, dy_ref, dgate_ref):
        first = pl.program_id(0) == 0
        xv, gv = x_ref[...], g_ref[...]
        rstd = lax.rsqrt(jnp.mean(xv * xv, axis=-1, keepdims=True) + EPS)
        xhat = xv * rstd
        err = xhat * gv - t_ref[...]
        part = 0.5 * jnp.sum(jnp.mean(err * err, axis=-1, keepdims=True), axis=0, keepdims=True)

        @pl.when(first)
        def _():
            loss_ref[...] = part

        @pl.when(jnp.logical_not(first))
        def _():
            loss_ref[...] += part

        dout = err * (1.0 / D)
        _acc_rows(dg_ref, dout * xhat, first)
        dxhat = dout * gv
        dx = rstd * (dxhat - xhat * jnp.mean(dxhat * xhat, axis=-1, keepdims=True))
        dx_ref[...] = dx
        _gate_bwd_tail(dx, y_ref, gate_ref, dy_ref, dgate_ref, first)

    vec = jax.ShapeDtypeStruct((1, D), F32)
    return pl.pallas_call(
        body, name="loss_head", grid=(S // tr,),
        out_shape=(jax.ShapeDtypeStruct((1, 1), F32), jax.ShapeDtypeStruct((S, D), F32), vec,
                   jax.ShapeDtypeStruct((S, D), _MXU_DTYPE), vec),
        in_specs=[_row_spec(tr, D), _vec_spec(D), _row_spec(tr, D), _row_spec(tr, D), _vec_spec(D)],
        out_specs=(pl.BlockSpec((1, 1), lambda i: (0, 0)), _row_spec(tr, D), _vec_spec(D), _row_spec(tr, D), _vec_spec(D)),
        compiler_params=_cp("arbitrary"))(x, g, target, y, gate)


def _gate_bwd(name, dx, y, gate, tr=256):
    S, D = dx.shape

    def body(dx_ref, y_ref, g_ref, dy_ref, dg_ref):
        dxv = dx_ref[...]
        dy_ref[...] = (g_ref[...] * dxv).astype(dy_ref.dtype)
        _acc_rows(dg_ref, dxv * y_ref[...], pl.program_id(0) == 0)

    return pl.pallas_call(
        body, name=name, grid=(S // tr,),
        out_shape=(jax.ShapeDtypeStruct((S, D), _MXU_DTYPE), jax.ShapeDtypeStruct((1, D), F32)),
        in_specs=[_row_spec(tr, D), _row_spec(tr, D), _vec_spec(D)], out_specs=(_row_spec(tr, D), _vec_spec(D)),
        compiler_params=_cp("arbitrary"))(dx, y, gate)


def _shift_down(v, k):
    t = lax.broadcasted_iota(jnp.int32, v.shape, 0)
    return jnp.where(t >= k, pltpu.roll(v, k, axis=0), 0.0)


def _shift_up(v, k):
    n = v.shape[0]
    t = lax.broadcasted_iota(jnp.int32, v.shape, 0)
    return jnp.where(t < n - k, pltpu.roll(v, n - k, axis=0), 0.0)


def _window_sum(p, w, shift):
    s, k = p, 1
    while k < w:
        s = s + shift(s, k)
        k *= 2
    return s


def _pool_count(shape, w):
    t = lax.broadcasted_iota(jnp.int32, shape, 0)
    return jnp.minimum(t + 1, w).astype(F32)


def _ab_specs(S):
    zs = [pl.BlockSpec((None, S, 128), functools.partial(lambda g, q: (2 * q + g // 2, 0, g % 2), q=q)) for q in range(4)]
    return zs


def _ab_mix_fwd(z8, conv_w, mix_w, scale):
    S = z8.shape[1]

    def body(b_ref, c_ref, a_ref, p_ref, w_ref, mix_ref, sc_ref, y_ref):
        g = pl.program_id(0)
        cg = c_ref[...].astype(F32) * a_ref[...].astype(F32)
        w = w_ref[...]
        conv = w[0:1] * _shift_down(cg, 2) + w[1:2] * _shift_down(cg, 1) + w[2:3] * cg
        y_ref[0] = (b_ref[...].astype(F32) * conv).astype(y_ref.dtype)
        for gg, win in enumerate(POOL_WINDOWS):
            @pl.when(g == gg)
            def _(win=win):
                p = p_ref[...].astype(F32)
                pooled = _window_sum(p, win, _shift_down) / _pool_count(p.shape, win) - p
                y_ref[1] = (_dot(pooled, mix_ref[...], "nn") * sc_ref[...]).astype(y_ref.dtype)

    return pl.pallas_call(
        body, name="ab_mix_fwd", grid=(4,), out_shape=jax.ShapeDtypeStruct((2, S, 512), _MXU_DTYPE),
        in_specs=_ab_specs(S) + [pl.BlockSpec((3, 128), lambda g: (0, g)), pl.BlockSpec((None, 128, 128), lambda g: (g, 0, 0)),
                                 pl.BlockSpec((1, 128), lambda g: (0, g))],
        out_specs=pl.BlockSpec((2, S, 128), lambda g: (0, 0, g)), compiler_params=_cp("parallel"))(z8, z8, z8, z8, conv_w, mix_w, scale)


def _ab_mix_bwd(z8, dycat2, conv_w, mix_w, scale, after):
    S = z8.shape[1]

    def body(b_ref, c_ref, a_ref, p_ref, dy_ref, w_ref, mix_ref, sc_ref, after_ref, dz_ref, dw_ref, dmix_ref, dsc_ref):
        g = pl.program_id(0)
        bv, cv, av, w = b_ref[...].astype(F32), c_ref[...].astype(F32), a_ref[...].astype(F32), w_ref[...]
        dya = dy_ref[0]
        cg = cv * av
        cg1, cg2 = _shift_down(cg, 1), _shift_down(cg, 2)
        conv = w[0:1] * cg2 + w[1:2] * cg1 + w[2:3] * cg
        dz_ref[0] = (dya * conv).astype(dz_ref.dtype)
        dconv = dya * bv
        dcg = w[2:3] * dconv + w[1:2] * _shift_up(dconv, 1) + w[0:1] * _shift_up(dconv, 2)
        dz_ref[1] = (dcg * av).astype(dz_ref.dtype)
        dz_ref[2] = (dcg * cv).astype(dz_ref.dtype)
        dw_ref[0:1, :] = jnp.sum(dconv * cg2, axis=0, keepdims=True)
        dw_ref[1:2, :] = jnp.sum(dconv * cg1, axis=0, keepdims=True)
        dw_ref[2:3, :] = jnp.sum(dconv * cg, axis=0, keepdims=True)
        for gg, win in enumerate(POOL_WINDOWS):
            @pl.when(g == gg)
            def _(win=win):
                p, dyb, mix = p_ref[...].astype(F32), dy_ref[1], mix_ref[...]
                cnt = _pool_count(p.shape, win)
                pooled = _window_sum(p, win, _shift_down) / cnt - p
                dsc_ref[...] = jnp.sum(dyb * _dot(pooled, mix, "nn"), axis=0, keepdims=True)
                dmixed = dyb * sc_ref[...]
                dmix_ref[...] = _dot(pooled, dmixed, "tn")
                dpooled = _dot(dmixed, mix, "nt")
                dz_ref[3] = (_window_sum(dpooled / cnt, win, _shift_up) - dpooled).astype(dz_ref.dtype)

    return pl.pallas_call(
        body, name="ab_mix_bwd", grid=(4,),
        out_shape=(jax.ShapeDtypeStruct((4, 2, S, 256), _MXU_DTYPE), jax.ShapeDtypeStruct((3, 512), F32),
                   jax.ShapeDtypeStruct((4, 128, 128), F32), jax.ShapeDtypeStruct((1, 512), F32)),
        in_specs=_ab_specs(S) + [pl.BlockSpec((2, S, 128), lambda g: (0, 0, g)), pl.BlockSpec((3, 128), lambda g: (0, g)),
                                 pl.BlockSpec((None, 128, 128), lambda g: (g, 0, 0)), pl.BlockSpec((1, 128), lambda g: (0, g)), ANY],
        out_specs=(pl.BlockSpec((4, None, S, 128), lambda g: (0, g // 2, 0, g % 2)), pl.BlockSpec((3, 128), lambda g: (0, g)),
                   pl.BlockSpec((None, 128, 128), lambda g: (g, 0, 0)), pl.BlockSpec((1, 128), lambda g: (0, g))),
        compiler_params=_cp("parallel"))(z8, z8, z8, z8, dycat2, conv_w, mix_w, scale, after)


HALO = 16


def _ffn_specs(S, n, tr):
    nb = S // HALO
    tile = pl.BlockSpec((2, None, tr, n), lambda j, i: (0, j, i, 0))
    prev = pl.BlockSpec((2, None, HALO, n), lambda j, i: (0, j, jnp.maximum(i * (tr // HALO) - 1, 0), 0))
    nxt = pl.BlockSpec((2, None, HALO, n), lambda j, i: (0, j, jnp.minimum((i + 1) * (tr // HALO), nb - 1), 0))
    cw = pl.BlockSpec((2, None, 3, n), lambda j, i: (0, j, 0, 0))
    return tile, prev, nxt, cw


def _shifted_rows(ext, lo, rows):
    ext = ext.astype(F32)
    return pltpu.roll(ext, 1, axis=0)[lo:lo + rows], pltpu.roll(ext, 2, axis=0)[lo:lo + rows]


def _ffn_gate_fwd(name, u24, cw24, tr=256):
    _, J, S, n = u24.shape
    tile, prev, _, cw = _ffn_specs(S, n, tr)

    def body(u_ref, up_ref, w_ref, a_ref, z_ref):
        keep = (pl.program_id(1) > 0).astype(u_ref.dtype)
        z = []
        for h in range(2):
            ext = jnp.concatenate([up_ref[h] * keep, u_ref[h]], axis=0)
            x1, x2 = _shifted_rows(ext, HALO, tr)
            w = w_ref[h]
            zh = w[0:1] * x2 + w[1:2] * x1 + w[2:3] * u_ref[h].astype(F32)
            z_ref[h] = zh.astype(z_ref.dtype)
            z.append(zh)
        a_ref[...] = (_silu(z[0]) * z[1]).astype(a_ref.dtype)

    return pl.pallas_call(
        body, name=name, grid=(J, S // tr),
        out_shape=(jax.ShapeDtypeStruct((J, S, n), _MXU_DTYPE), jax.ShapeDtypeStruct((2, J, S, n), ACT_DTYPE)),
        in_specs=[tile, prev, cw], out_specs=(pl.BlockSpec((None, tr, n), lambda j, i: (j, i, 0)), tile),
        compiler_params=_cp("parallel", "parallel"))(u24, u24, cw24)


def _ffn_gate_bwd(name, u24, z24, cw24, da4, w_up24, after, tr=256):
    _, J, S, n = u24.shape
    K = w_up24.shape[2]
    nb = S // HALO
    tile = pl.BlockSpec((2, None, tr, n), lambda i, j: (0, j, i, 0))
    nxt = pl.BlockSpec((2, None, HALO, n), lambda i, j: (0, j, jnp.minimum((i + 1) * (tr // HALO), nb - 1), 0))
    whole = lambda shape: pl.BlockSpec(shape, lambda i, j: (0,) * len(shape))

    def body(u_ref, z_ref, zn_ref, cw_ref, da_ref, dan_ref, wup_ref, after_ref, du_ref, dcw_ref, dh_ref, acc_ref):
        i, j = pl.program_id(0), pl.program_id(1)
        first = i == 0
        keep_next = (i < S // tr - 1).astype(F32)
        w = [cw_ref[h, j] for h in range(2)]
        m = tr + HALO
        zg, zu = [jnp.concatenate([z_ref[h], zn_ref[h]], axis=0).astype(F32) for h in range(2)]
        da = jnp.concatenate([da_ref[...].astype(F32), dan_ref[...].astype(F32) * keep_next], axis=0)
        sg = jax.nn.sigmoid(zg)
        dz = [da * zu * (sg * (1.0 + zg * (1.0 - sg))), da * (zg * sg)]
        dh = None
        for h in range(2):
            d = dz[h]
            d0, d1, d2 = d[:tr], pltpu.roll(d, m - 1, axis=0)[:tr], pltpu.roll(d, m - 2, axis=0)[:tr]
            du = (w[h][2:3] * d0 + w[h][1:2] * d1 + w[h][0:1] * d2).astype(du_ref.dtype)
            du_ref[h] = du
            part = _dot(du, wup_ref[h, j], "nt")
            dh = part if dh is None else dh + part
            x0 = u_ref[h].astype(F32)
            parts = [jnp.sum(x0 * dk, axis=0, keepdims=True) for dk in (d2, d1, d0)]
            for k in range(3):
                @pl.when(first)
                def _(k=k, h=h):
                    dcw_ref[h, j, k:k + 1, :] = parts[k]

                @pl.when(jnp.logical_not(first))
                def _(k=k, h=h):
                    dcw_ref[h, j, k:k + 1, :] += parts[k]

        @pl.when(j == 0)
        def _():
            acc_ref[...] = dh

        @pl.when(j > 0)
        def _():
            acc_ref[...] += dh

        @pl.when(j == J - 1)
        def _():
            dh_ref[...] = acc_ref[...].astype(dh_ref.dtype)

    da_tile = pl.BlockSpec((None, tr, n), lambda i, j: (j, i, 0))
    da_next = pl.BlockSpec((None, HALO, n), lambda i, j: (j, jnp.minimum((i + 1) * (tr // HALO), nb - 1), 0))
    return pl.pallas_call(
        body, name=name, grid=(S // tr, J),
        out_shape=(jax.ShapeDtypeStruct((2, J, S, n), _MXU_DTYPE), jax.ShapeDtypeStruct((2, J, 3, n), F32),
                   jax.ShapeDtypeStruct((S, K), ACT_DTYPE)),
        in_specs=[tile, tile, nxt, whole((2, J, 3, n)), da_tile, da_next, whole((2, J, K, n)), ANY],
        out_specs=(tile, whole((2, J, 3, n)), pl.BlockSpec((tr, K), lambda i, j: (i, 0))),
        scratch_shapes=[pltpu.VMEM((tr, K), F32)],
        compiler_params=_cp("arbitrary", "arbitrary"))(u24, z24, z24, cw24, da4, da4, w_up24, after)


def _rms_rows(v, g):
    rstd = lax.rsqrt(jnp.mean(v * v, axis=-1, keepdims=True) + EPS)
    return v * rstd * g


def _rms_rows_bwd(v, g, dy):
    rstd = lax.rsqrt(jnp.mean(v * v, axis=-1, keepdims=True) + EPS)
    vhat = v * rstd
    dvhat = dy * g
    return rstd * (dvhat - vhat * jnp.mean(dvhat * vhat, axis=-1, keepdims=True)), dy * vhat


def _rope(v, cos, sa, sb):
    return v * cos + pltpu.roll(v, 112, axis=1) * sa + pltpu.roll(v, 16, axis=1) * sb


def _rope_t(d, cos, sa, sb):
    return d * cos + pltpu.roll(d * sa, 16, axis=1) + pltpu.roll(d * sb, 112, axis=1)


def _qkv_rope_fwd(z, qg, kvg, w_uq_t, w_kv, cosq, cosk, sa, sb, tr=256):
    S = z.shape[0]

    def body(ql_ref, kvl_ref, kpe_ref, qg_ref, kvg_ref, wq_ref, wkv_ref, cq_ref, ck_ref, sa_ref, sb_ref,
             qn_ref, kvn_ref, qo_ref, ko_ref, vo_ref):
        cq, ck, sa_v, sb_v = cq_ref[...], ck_ref[...], sa_ref[...], sb_ref[...]
        qn = _rms_rows(ql_ref[...], qg_ref[...]).astype(qn_ref.dtype)
        kvn = _rms_rows(kvl_ref[...], kvg_ref[...]).astype(kvn_ref.dtype)
        qn_ref[...] = qn
        kvn_ref[...] = kvn
        q = _dot(qn, wq_ref[...], "nt")
        kv = _dot(kvn, wkv_ref[...], "nn")
        kpe = _rope(kpe_ref[...], ck, sa_v, sb_v)
        for h in range(8):
            cols = slice(128 * h, 128 * h + 128)
            qo_ref[:, cols] = _rope(q[:, cols], cq, sa_v, sb_v).astype(qo_ref.dtype)
            ko_ref[:, cols] = (kv[:, cols] + kpe).astype(ko_ref.dtype)
        vo_ref[...] = kv[:, 1024:1536].astype(vo_ref.dtype)

    tab = _row_spec(tr, 128)
    whole = lambda a: pl.BlockSpec(a.shape, lambda i: (0, 0))
    return pl.pallas_call(
        body, name="qkv_rope_fwd", grid=(S // tr,),
        out_shape=(jax.ShapeDtypeStruct((S, 256), _MXU_DTYPE), jax.ShapeDtypeStruct((S, 128), _MXU_DTYPE),
                   jax.ShapeDtypeStruct((S, 1024), _MXU_DTYPE), jax.ShapeDtypeStruct((S, 1024), _MXU_DTYPE),
                   jax.ShapeDtypeStruct((S, 512), _MXU_DTYPE)),
        in_specs=[pl.BlockSpec((tr, 256), lambda i: (i, 0)), pl.BlockSpec((tr, 128), lambda i: (i, 2)),
                  pl.BlockSpec((tr, 128), lambda i: (i, 3)), _vec_spec(256), _vec_spec(128), whole(w_uq_t), whole(w_kv),
                  tab, tab, tab, tab],
        out_specs=(_row_spec(tr, 256), _row_spec(tr, 128), _row_spec(tr, 1024), _row_spec(tr, 1024), _row_spec(tr, 512)),
        compiler_params=_cp("parallel"))(z, z, z, qg, kvg, w_uq_t, w_kv, cosq, cosk, sa, sb)


def _attn_bwd_prep(o, dycat2, tr=256):
    S = o.shape[0]

    def body(o_ref, do_ref, delta_ref, doa_ref, dob_ref):
        do = do_ref[...]
        prod = do * o_ref[...]
        lane = lax.broadcasted_iota(jnp.int32, do.shape, 1)
        for p in range(4):
            cols = slice(128 * p, 128 * p + 128)
            first = lane[:, cols] < 128 * p + 64
            da = jnp.sum(jnp.where(first, prod[:, cols], 0.0), axis=-1, keepdims=True)
            db = jnp.sum(jnp.where(first, 0.0, prod[:, cols]), axis=-1, keepdims=True)
            delta_ref[p] = jnp.where(first, da, db)
            doa_ref[p] = jnp.where(first, do[:, cols], 0.0).astype(doa_ref.dtype)
            dob_ref[p] = jnp.where(first, 0.0, do[:, cols]).astype(dob_ref.dtype)

    pair = pl.BlockSpec((4, tr, 128), lambda i: (0, i, 0))
    return pl.pallas_call(
        body, name="attn_bwd_prep", grid=(S // tr,),
        out_shape=(jax.ShapeDtypeStruct((4, S, 128), F32), jax.ShapeDtypeStruct((4, S, 128), _MXU_DTYPE),
                   jax.ShapeDtypeStruct((4, S, 128), _MXU_DTYPE)),
        in_specs=[_row_spec(tr, 512), pl.BlockSpec((None, tr, 512), lambda i: (0, i, 0))],
        out_specs=(pair, pair, pair), compiler_params=_cp("parallel"))(o, dycat2)


def _qkv_rope_bwd(z, qg, kvg, dq, dk, dv, duv, w_uq_t, w_kv, cosq, cosk, sa, sb, tr=256):
    S = z.shape[0]

    def body(ql_ref, kvl_ref, qg_ref, kvg_ref, dq_ref, dk_ref, dv_ref, duv_ref, wq_ref, wkv_ref, cq_ref, ck_ref, sa_ref, sb_ref,
             dqo_ref, dkv_ref, dz_ref, dqg_ref, dkvg_ref):
        first = pl.program_id(0) == 0
        cq, ck, sa_v, sb_v = cq_ref[...], ck_ref[...], sa_ref[...], sb_ref[...]
        tot = jnp.zeros((tr, 128), F32)
        for h in range(8):
            cols = slice(128 * h, 128 * h + 128)
            dqo_ref[:, cols] = _rope_t(dq_ref[:, cols], cq, sa_v, sb_v).astype(dqo_ref.dtype)
            dkh = dk_ref[:, cols]
            tot = tot + dkh
            dkv_ref[:, cols] = dkh.astype(dkv_ref.dtype)
        dkv_ref[:, 1024:1536] = dv_ref[...].astype(dkv_ref.dtype)
        dqn = _dot(dqo_ref[...], wq_ref[...], "nn")
        dkvn = _dot(dkv_ref[...], wkv_ref[...], "nt")
        dql, dqg = _rms_rows_bwd(ql_ref[...], qg_ref[...], dqn)
        dkvl, dkvg = _rms_rows_bwd(kvl_ref[...], kvg_ref[...], dkvn)
        _acc_rows(dqg_ref, dqg, first)
        _acc_rows(dkvg_ref, dkvg, first)
        dz_ref[:, 0:256] = dql.astype(dz_ref.dtype)
        dz_ref[:, 256:384] = dkvl.astype(dz_ref.dtype)
        dz_ref[:, 384:512] = _rope_t(tot, ck, sa_v, sb_v).astype(dz_ref.dtype)
        dz_ref[:, 512:1536] = duv_ref[...].astype(dz_ref.dtype)

    tab = _row_spec(tr, 128)
    whole = lambda a: pl.BlockSpec(a.shape, lambda i: (0, 0))
    return pl.pallas_call(
        body, name="qkv_rope_bwd", grid=(S // tr,),
        out_shape=(jax.ShapeDtypeStruct((S, 1024), _MXU_DTYPE), jax.ShapeDtypeStruct((S, 1536), _MXU_DTYPE),
                   jax.ShapeDtypeStruct((S, 1536), _MXU_DTYPE), jax.ShapeDtypeStruct((1, 256), F32), jax.ShapeDtypeStruct((1, 128), F32)),
        in_specs=[pl.BlockSpec((tr, 256), lambda i: (i, 0)), pl.BlockSpec((tr, 128), lambda i: (i, 2)), _vec_spec(256), _vec_spec(128),
                  _row_spec(tr, 1024), _row_spec(tr, 1024), _row_spec(tr, 512), _row_spec(tr, 1024), whole(w_uq_t), whole(w_kv),
                  tab, tab, tab, tab],
        out_specs=(_row_spec(tr, 1024), _row_spec(tr, 1536), _row_spec(tr, 1536), _vec_spec(256), _vec_spec(128)),
        compiler_params=_cp("arbitrary"))(z, z, qg, kvg, dq, dk, dv, duv, w_uq_t, w_kv, cosq, cosk, sa, sb)


NEG = -1e30


def _attn_fwd(q, k, v, tq=512, tk=512):
    S = q.shape[0]
    assert tq == tk

    def body(q_ref, k_ref, v_ref, o_ref, lse_ref):
        i = pl.program_id(1)
        qs = [q_ref[:, 0:128], q_ref[:, 128:256]]

        def step(kb, carry, diagonal=False):
            start = pl.multiple_of(kb * tk, tk)
            vv = v_ref[pl.ds(start, tk), :]
            out = []
            for h in range(2):
                m, l, acc = carry[3 * h:3 * h + 3]
                s = _dot(qs[h], k_ref[pl.ds(start, tk), 128 * h:128 * h + 128], "nt") * ATTN_SCALE
                if diagonal:
                    s = jnp.where(below, s, NEG)
                m_new = jnp.maximum(m, jnp.max(s, axis=-1, keepdims=True))
                alpha = jnp.exp(m - m_new)
                p = jnp.exp(s - m_new)
                out += [m_new, alpha * l + jnp.sum(p, axis=-1, keepdims=True), alpha * acc + _dot(p, vv, "nn")]
            return tuple(out)

        below = lax.broadcasted_iota(jnp.int32, (tq, tk), 1) <= lax.broadcasted_iota(jnp.int32, (tq, tk), 0)
        init = (jnp.full((tq, 1), NEG, F32), jnp.zeros((tq, 1), F32), jnp.zeros((tq, 128), F32)) * 2
        ma, la, acca, mb, lb, accb = step(i, lax.fori_loop(0, i, step, init), diagonal=True)
        lane = lax.broadcasted_iota(jnp.int32, (tq, 128), 1)
        o_ref[...] = jnp.where(lane < 64, acca / la, accb / lb)
        lse_ref[...] = jnp.where(lane < 64, ma + jnp.log(la), mb + jnp.log(lb))

    return pl.pallas_call(
        body, name="attn_fwd", grid=(4, S // tq),
        out_shape=(jax.ShapeDtypeStruct((S, 512), F32), jax.ShapeDtypeStruct((4, S, 128), F32)),
        in_specs=[pl.BlockSpec((tq, 256), lambda p, i: (i, p)), pl.BlockSpec((S, 256), lambda p, i: (0, p)),
                  pl.BlockSpec((S, 128), lambda p, i: (0, p))],
        out_specs=(pl.BlockSpec((tq, 128), lambda p, i: (i, p)), pl.BlockSpec((None, tq, 128), lambda p, i: (p, i, 0))),
        compiler_params=_cp("parallel", "parallel"))(q, k, v)


def _attn_bwd(q, k, v, lse, delta, doa, dob, tq=512, tk=512):
    S = q.shape[0]
    assert tq == tk

    def body(q_ref, k_ref, v_ref, lse_ref, delta_ref, doa_ref, dob_ref, dq_ref, dk_ref, dv_ref):
        j = pl.program_id(1)

        @pl.when(j == 0)
        def _():
            dq_ref[...] = jnp.zeros_like(dq_ref)

        below = lax.broadcasted_iota(jnp.int32, (tq, tk), 1) <= lax.broadcasted_iota(jnp.int32, (tq, tk), 0)
        ks = [k_ref[:, 0:128], k_ref[:, 128:256]]
        vv = v_ref[...]

        def step(qb, carry, diagonal=False):
            dka, dkb, dvp = carry
            start = pl.multiple_of(qb * tq, tq)
            rows = pl.ds(start, tq)
            lse_v, delta_v = lse_ref[rows, :], delta_ref[rows, :]
            dos = [doa_ref[rows, :], dob_ref[rows, :]]
            dks = [dka, dkb]
            for h in range(2):
                delta = delta_v[:, 64 * h:64 * h + 1]
                do_h = dos[h]
                qh = q_ref[rows, 128 * h:128 * h + 128]
                s = _dot(qh, ks[h], "nt") * ATTN_SCALE
                p = jnp.exp(s - lse_v[:, 64 * h:64 * h + 1])
                if diagonal:
                    p = jnp.where(below, p, 0.0)
                dvp = dvp + _dot(p, do_h, "tn")
                ds = p * (_dot(do_h, vv, "nt") - delta) * ATTN_SCALE
                dq_ref[rows, 128 * h:128 * h + 128] += _dot(ds, ks[h], "nn")
                dks[h] = dks[h] + _dot(ds, qh, "tn")
            return dks[0], dks[1], dvp

        zero = jnp.zeros((tk, 128), F32)
        dka, dkb, dvp = lax.fori_loop(j + 1, S // tq, step, step(j, (zero, zero, zero), diagonal=True))
        dk_ref[:, 0:128] = dka
        dk_ref[:, 128:256] = dkb
        dv_ref[...] = dvp

    return pl.pallas_call(
        body, name="attn_bwd", grid=(4, S // tk),
        out_shape=(jax.ShapeDtypeStruct((S, 1024), F32), jax.ShapeDtypeStruct((S, 1024), F32), jax.ShapeDtypeStruct((S, 512), F32)),
        in_specs=[pl.BlockSpec((S, 256), lambda p, j: (0, p)), pl.BlockSpec((tk, 256), lambda p, j: (j, p)),
                  pl.BlockSpec((tk, 128), lambda p, j: (j, p))] + [pl.BlockSpec((None, S, 128), lambda p, j: (p, 0, 0))] * 4,
        out_specs=(pl.BlockSpec((S, 256), lambda p, j: (0, p)), pl.BlockSpec((tk, 256), lambda p, j: (j, p)),
                   pl.BlockSpec((tk, 128), lambda p, j: (j, p))),
        compiler_params=_cp("parallel", "arbitrary"))(q, k, v, lse, delta, doa, dob)


CHUNK = 128
GELU_C = math.sqrt(2.0 / math.pi)


def _gelu(v):
    t = jnp.tanh(GELU_C * (v + 0.044715 * (v * v * v)))
    return v * (0.5 * (1.0 + t)), t


def _gelu_grad(v, t):
    return 0.5 * (1.0 + t) + v * (0.5 * (1.0 - t * t) * GELU_C * (1.0 + 3.0 * 0.044715 * v * v))


def _tril(w):
    r = lax.broadcasted_iota(jnp.int32, w.shape, 0)
    c = lax.broadcasted_iota(jnp.int32, w.shape, 1)
    return jnp.where(c <= r, w, 0.0)


def _layer_norm(v, g, b):
    xc = v - jnp.mean(v, axis=-1, keepdims=True)
    rstd = lax.rsqrt(jnp.mean(xc * xc, axis=-1, keepdims=True) + EPS)
    xhat = xc * rstd
    return xhat * g + b, xhat, rstd


def _sgu_fwd(z, o, ln_g, ln_b, w_s, b_st, tr=256):
    S = z.shape[0]

    def body(u_ref, v_ref, o_ref, g_ref, b_ref, ws_ref, bs_ref, y_ref):
        gu, _ = _gelu(u_ref[...])
        gv, _ = _gelu(v_ref[...])
        vln, _, _ = _layer_norm(gv, g_ref[...], b_ref[...])
        y_ref[0] = o_ref[...].astype(y_ref.dtype)
        for g in range(4):
            wt = _tril(ws_ref[g])
            cols = slice(128 * g, 128 * g + 128)
            for ch in range(tr // CHUNK):
                rows = slice(CHUNK * ch, CHUNK * ch + CHUNK)
                mixed = _dot(wt, vln[rows, cols], "nn") + bs_ref[:, g:g + 1]
                y_ref[1, rows, cols] = (gu[rows, cols] * mixed).astype(y_ref.dtype)

    return pl.pallas_call(
        body, name="sgu_fwd", grid=(S // tr,), out_shape=jax.ShapeDtypeStruct((2, S, 512), _MXU_DTYPE),
        in_specs=[pl.BlockSpec((tr, 512), lambda i: (i, 1)), pl.BlockSpec((tr, 512), lambda i: (i, 2)), _row_spec(tr, 512),
                  _vec_spec(512), _vec_spec(512), pl.BlockSpec((4, 128, 128), lambda i: (0, 0, 0)), pl.BlockSpec((128, 4), lambda i: (0, 0))],
        out_specs=pl.BlockSpec((2, tr, 512), lambda i: (0, i, 0)), compiler_params=_cp("parallel"))(z, z, o, ln_g, ln_b, w_s, b_st)


def _sgu_bwd(z, dycat2, ln_g, ln_b, w_s, b_st, tr=256):
    S = z.shape[0]

    def body(u_ref, v_ref, dy_ref, g_ref, b_ref, ws_ref, bs_ref, duv_ref, dg_ref, db_ref, dws_ref, dbs_ref):
        first = pl.program_id(0) == 0
        u_pre, v_pre = u_ref[...], v_ref[...]
        gu, tu = _gelu(u_pre)
        gv, tv = _gelu(v_pre)
        gain = g_ref[...]
        vln, xhat, rstd = _layer_norm(gv, gain, b_ref[...])

        @pl.when(first)
        def _():
            dws_ref[...] = jnp.zeros_like(dws_ref)
            dbs_ref[...] = jnp.zeros_like(dbs_ref)

        dvln_cols = []
        for g in range(4):
            wt = _tril(ws_ref[g])
            cols = slice(128 * g, 128 * g + 128)
            dmixed_sum = jnp.zeros((CHUNK, 128), F32)
            dw = jnp.zeros((CHUNK, CHUNK), F32)
            dvln_rows = []
            for ch in range(tr // CHUNK):
                rows = slice(CHUNK * ch, CHUNK * ch + CHUNK)
                vt = vln[rows, cols]
                mixed = _dot(wt, vt, "nn") + bs_ref[:, g:g + 1]
                dyd = dy_ref[rows, cols]
                duv_ref[rows, cols] = (dyd * mixed * _gelu_grad(u_pre[rows, cols], tu[rows, cols])).astype(duv_ref.dtype)
                dmixed = dyd * gu[rows, cols]
                dmixed_sum = dmixed_sum + dmixed
                dw = dw + _dot(dmixed, vt, "nt")
                dvln_rows.append(_dot(wt, dmixed, "tn"))
            dws_ref[g] += _tril(dw)
            dbs_ref[g:g + 1, :] += jnp.sum(dmixed_sum.T, axis=0, keepdims=True)
            dvln_cols.append(jnp.concatenate(dvln_rows, axis=0))
        dvln = jnp.concatenate(dvln_cols, axis=1)
        _acc_rows(dg_ref, dvln * xhat, first)
        _acc_rows(db_ref, dvln, first)
        dxhat = dvln * gain
        dgv = rstd * (dxhat - jnp.mean(dxhat, axis=-1, keepdims=True) - xhat * jnp.mean(dxhat * xhat, axis=-1, keepdims=True))
        duv_ref[:, 512:1024] = (dgv * _gelu_grad(v_pre, tv)).astype(duv_ref.dtype)

    return pl.pallas_call(
        body, name="sgu_bwd", grid=(S // tr,),
        out_shape=(jax.ShapeDtypeStruct((S, 1024), _MXU_DTYPE), jax.ShapeDtypeStruct((1, 512), F32), jax.ShapeDtypeStruct((1, 512), F32),
                   jax.ShapeDtypeStruct((4, 128, 128), F32), jax.ShapeDtypeStruct((4, 128), F32)),
        in_specs=[pl.BlockSpec((tr, 512), lambda i: (i, 1)), pl.BlockSpec((tr, 512), lambda i: (i, 2)),
                  pl.BlockSpec((None, tr, 512), lambda i: (1, i, 0)), _vec_spec(512), _vec_spec(512),
                  pl.BlockSpec((4, 128, 128), lambda i: (0, 0, 0)), pl.BlockSpec((128, 4), lambda i: (0, 0))],
        out_specs=(_row_spec(tr, 1024), _vec_spec(512), _vec_spec(512), pl.BlockSpec((4, 128, 128), lambda i: (0, 0, 0)),
                   pl.BlockSpec((4, 128), lambda i: (0, 0))),
        compiler_params=_cp("arbitrary"))(z, z, dycat2, ln_g, ln_b, w_s, b_st)


def _sum_parts(name, parts, tr=512):
    P, R, C = parts.shape
    tr = _tile(R, tr) if R % 8 == 0 else R

    def body(p_ref, o_ref):
        g = p_ref[0]
        for k in range(1, P):
            g = g + p_ref[k]
        o_ref[...] = g

    return pl.pallas_call(
        body, name=name, grid=(R // tr,), out_shape=jax.ShapeDtypeStruct((R, C), F32),
        in_specs=[pl.BlockSpec((P, tr, C), lambda i: (0, i, 0))], out_specs=_row_spec(tr, C),
        compiler_params=_cp("parallel"))(parts)


def _adamw_math(w, m, v, g):
    c1 = 1.0 / (1.0 - ADAM_B1 ** ADAM_STEP)
    c2 = 1.0 / (1.0 - ADAM_B2 ** ADAM_STEP)
    m2 = ADAM_B1 * m + (1.0 - ADAM_B1) * g
    v2 = ADAM_B2 * v + (1.0 - ADAM_B2) * (g * g)
    return -ADAM_LR * ((m2 * c1) / (jnp.sqrt(v2 * c2) + ADAM_EPS) + ADAM_WD * w), m2, v2


def _adamw_small(name, params, parts):
    n = len(params)

    def body(*refs):
        ins, outs = refs[:4 * n], refs[4 * n:]
        for i in range(n):
            w_ref, m_ref, v_ref, p_ref = ins[4 * i:4 * i + 4]
            g = p_ref[0].astype(F32)
            for k in range(1, N_DEV):
                g = g + p_ref[k].astype(F32)
            delta, m2, v2 = _adamw_math(w_ref[...], m_ref[...], v_ref[...], g)
            outs[4 * i][...] = g
            outs[4 * i + 1][...] = delta
            outs[4 * i + 2][...] = m2
            outs[4 * i + 3][...] = v2

    flat = [a for (w, m, v), p in zip(params, parts) for a in (w, m, v, p)]
    out = pl.pallas_call(
        body, name=name, out_shape=[jax.ShapeDtypeStruct(w.shape, F32) for (w, _, _) in params for _ in range(4)],
        compiler_params=pltpu.CompilerParams(vmem_limit_bytes=_VMEM_LIMIT))(*flat)
    return [out[4 * i:4 * i + 4] for i in range(n)]


ADAMW_BLOCK_BYTES = 36 * 2 ** 20


def _adamw(name, w, m, v, parts):
    L, R, C = w.shape
    P = parts[0].shape[0]
    row_bytes = 2 * C * (7 * 4 + P * parts[0].dtype.itemsize)
    tr = R
    if R * row_bytes > ADAMW_BLOCK_BYTES:
        tr = next(t for t in (1024, 512, 256, 128, 64, 32, 16) if R % t == 0 and t * row_bytes <= ADAMW_BLOCK_BYTES)
    nr = R // tr
    c1 = 1.0 / (1.0 - ADAM_B1 ** ADAM_STEP)
    c2 = 1.0 / (1.0 - ADAM_B2 ** ADAM_STEP)

    def body(w_ref, m_ref, v_ref, *rest):
        p_refs, (g_ref, d_ref, mo_ref, vo_ref) = rest[:L], rest[L:]
        for ll in range(L):
            @pl.when(pl.program_id(0) == ll)
            def _(p_ref=p_refs[ll]):
                g = p_ref[0].astype(F32)
                for k in range(1, P):
                    g = g + p_ref[k].astype(F32)
                m2 = ADAM_B1 * m_ref[...] + (1.0 - ADAM_B1) * g
                v2 = ADAM_B2 * v_ref[...] + (1.0 - ADAM_B2) * (g * g)
                g_ref[...] = g
                mo_ref[...] = m2
                vo_ref[...] = v2
                d_ref[...] = -ADAM_LR * ((m2 * c1) / (jnp.sqrt(v2 * c2) + ADAM_EPS) + ADAM_WD * w_ref[...])

    def part_spec(ll):
        return pl.BlockSpec((P, tr, C), lambda l, i: (0, jnp.where(l == ll, i, jnp.where(l < ll, 0, nr - 1)), 0))

    full = pl.BlockSpec((None, tr, C), lambda l, i: (l, i, 0))
    sds = jax.ShapeDtypeStruct((L, R, C), F32)
    return pl.pallas_call(
        body, name=name, grid=(L, nr), out_shape=(sds, sds, sds, sds),
        in_specs=[full] * 3 + [part_spec(ll) for ll in range(L)],
        out_specs=(full,) * 4, compiler_params=_cp("arbitrary", "arbitrary"))(w, m, v, *parts)


def _rope_tables(positions):
    half = 16
    inv_freq = 10000.0 ** (-jnp.arange(half, dtype=F32) / half)
    ang = positions.astype(F32)[:, None] * inv_freq
    cos, sin = jnp.cos(ang), jnp.sin(ang)
    S = positions.shape[0]
    z16, z32, z64 = jnp.zeros((S, 16), F32), jnp.zeros((S, 32), F32), jnp.zeros((S, 64), F32)
    cosk = jnp.concatenate([z64, cos, cos, z32], axis=1)
    cosq = jnp.concatenate([jnp.ones((S, 64), F32), cos, cos, z32], axis=1)
    sa = jnp.concatenate([z64, -sin, z16, z32], axis=1)
    sb = jnp.concatenate([z64, z16, sin, z32], axis=1)
    return cosq, cosk, sa, sb


def _ffn_fwd(l, x, mod, n2g, get_w_up8, cw24, get_w_down4):
    sh, sc, gate = mod
    h = _rmsmod_fwd(f"ffn{l}_norm", x, n2g, sc, sh, n2g)
    w_up8 = get_w_up8(h)
    u8 = _mm_cols(f"ffn{l}_up", h, w_up8, out_dtype=ACT_DTYPE, tm=2048)
    S, n = u8.shape[1], u8.shape[2]
    u24 = u8.reshape(2, 4, S, n)
    a4, z24 = _ffn_gate_fwd(f"ffn{l}_gate", u24, cw24)
    w_down4 = get_w_down4(a4)
    f, x_new = _mm_rows_resid(f"ffn{l}_down", a4, w_down4, x, gate)
    return x_new, (x, h, u24, a4, f, z24), w_up8, w_down4


def _ffn_bwd(l, dx, df, dgate, saved, mod, n2g, w_up8, cw24, w_down4, me, y_prev, gate_prev):
    sh, sc, gate = mod
    x, h, u24, a4, f, z24 = saved
    da4 = _mm_rows_dx(f"ffn{l}_down_dx", df, w_down4, out_dtype=ACT_DTYPE, tm=2048)
    dw_down4 = _mm_rows_dw(f"ffn{l}_down_dw", a4, df, out_dtype=WIRE_DTYPE)
    sent_down, token = _exchange_start(f"scatter_ffn{l}_down", [dw_down4.reshape(8, 352, dw_down4.shape[2])], True, dgate, me)
    du24, dcw24, dh = _ffn_gate_bwd(f"ffn{l}_act_bwd", u24, z24, cw24, da4, w_up8.reshape((2, 4) + w_up8.shape[1:]), token)
    du8 = du24.reshape((8,) + du24.shape[2:])
    dw_up8t = _mm_cols_dwt(f"ffn{l}_up_dw", h, du8, out_dtype=WIRE_DTYPE, tk=1024)
    sent_up, token = _exchange_start(f"scatter_ffn{l}_up", [dw_up8t], True, dcw24, me)
    dx_new, dn2g, dsc, dsh, dy_prev, dgate_prev = _rmsmod_bwd(f"ffn{l}_norm_bwd", x, n2g, sc, dh, dx, token, y_prev, gate_prev)
    return dx_new, dict(sent_up=sent_up, sent_down=sent_down, cw24=dcw24, n2g=dn2g, mod=(dsh, dsc, dgate)), dy_prev, dgate_prev


def kernel(x, c, positions, ada_w, ada_b, norm1_g, norm2_g, ab_w_in, a_conv_w, b_mix_w, b_scale, ab_w_out, cd_w_in, c_q_norm_g, c_w_uq, c_kv_norm_g, c_w_ukv, d_ln_g, d_ln_b, d_w_s, d_b_s, cd_w_out, ffn_w_up, ffn_conv_w, ffn_w_down, final_norm_g, loss_target, m_ada_w, m_ada_b, m_norm1_g, m_norm2_g, m_ab_w_in, m_a_conv_w, m_b_mix_w, m_b_scale, m_ab_w_out, m_cd_w_in, m_c_q_norm_g, m_c_w_uq, m_c_kv_norm_g, m_c_w_ukv, m_d_ln_g, m_d_ln_b, m_d_w_s, m_d_b_s, m_cd_w_out, m_ffn_w_up, m_ffn_conv_w, m_ffn_w_down, m_final_norm_g, v_ada_w, v_ada_b, v_norm1_g, v_norm2_g, v_ab_w_in, v_a_conv_w, v_b_mix_w, v_b_scale, v_ab_w_out, v_cd_w_in, v_c_q_norm_g, v_c_w_uq, v_c_kv_norm_g, v_c_w_ukv, v_d_ln_g, v_d_ln_b, v_d_w_s, v_d_b_s, v_cd_w_out, v_ffn_w_up, v_ffn_conv_w, v_ffn_w_down, v_final_norm_g):
    S, D = x.shape[1], x.shape[2]
    me = 4 * lax.axis_index("x") + 2 * lax.axis_index("y") + lax.axis_index("c")
    x0, target = x[0], loss_target[0]
    W = _MXU_DTYPE

    small_shapes = [(1024,), (3, 64), (32,), (64,), (64,), (2, 3, 704)]
    gw_ab, token = _hier_gather_start("gather_w_ab", [ab_w_in[0].astype(W), ab_w_out[0].astype(W)], c, me)
    (g0,) = _exchange("gather_small", [[_pack([c, a_conv_w, c_q_norm_g, d_ln_g, d_ln_b, ffn_conv_w]) + 0.0 * token[:1]]],
                      scatter=False)
    c_all, aconv_s, qg_s, lng_s, lnb_s, fcw_s = _unpack(g0[:, 0], small_shapes, lead=(N_DEV,))
    conv_w = aconv_s.transpose(1, 0, 2).reshape(3, 512)
    qg, ln_g, ln_b = qg_s.reshape(1, 256), lng_s.reshape(1, 512), lnb_s.reshape(1, 512)
    cw24 = [fcw_s[:, l].reshape(2, 4, 3, 704) for l in range(2)]
    c16 = jnp.pad(c_all, ((0, 16 - N_DEV), (0, 0)))

    mod_cols = _ada_fwd(c16, ada_w)
    (g1,) = _exchange("gather_mod", [[_pack([mod_cols])]], scatter=False)
    mod_all = _unpack(g1[:, 0], [(2, 16, 768)], lead=(N_DEV,))[0]
    mod_mine = lax.dynamic_index_in_dim(mod_all, me, axis=2, keepdims=False)
    mod = mod_mine.transpose(1, 0, 2).reshape(2, 6 * D) + ada_b
    mods = [[mod[l, k * D:(k + 1) * D].reshape(1, D) for k in range(6)] for l in range(2)]

    gw_up0, token = _hier_gather_start("gather_w_ffn0_up", [ffn_w_up[0].astype(W)], mod, me)
    gw_rest, started = _exchange_start("gather_w_rest", [
        ffn_w_down[0].astype(W), cd_w_in[0].T.astype(W), c_w_uq[0].T.astype(W), c_w_ukv[0].astype(W), cd_w_out[0].astype(W),
        ffn_w_up[1].astype(W), ffn_w_down[1].astype(W)], False, token, me)

    cosq, cosk, sa, sb = _rope_tables(positions[0])
    n1g = [norm1_g[l].reshape(1, D) for l in range(2)]
    n2g = [norm2_g[l].reshape(1, D) for l in range(2)]
    mix_w, scale = b_mix_w[0], b_scale
    kvg = c_kv_norm_g
    w_s, b_st = d_w_s[0], d_b_s[0].T

    sh1, sc1, g1m = mods[0][:3]
    h_ab = _rmsmod_fwd("ab_norm", x0, n1g[0], sc1, sh1, started)
    w_abin8, w_about = _hier_gather_wait("wait_w_ab", _hier_gather_forward("forward_w_ab", gw_ab, h_ab), h_ab)
    w_about2 = w_about.reshape(2, 512, D)
    z8 = _mm_cols("ab_in", h_ab, w_abin8, out_dtype=ACT_DTYPE, tm=2048)
    ycat_ab = _ab_mix_fwd(z8, conv_w, mix_w, scale)
    y_ab, x1 = _mm_rows_resid("ab_out", ycat_ab, w_about2, x0, g1m)
    w_up8, w_down4 = [None, None], [None, None]
    gw_up0 = _hier_gather_forward("forward_w_ffn0_up", gw_up0, x1)
    x2, ffn0_saved, w_up8[0], w_down4[0] = _ffn_fwd(
        0, x1, mods[0][3:], n2g[0], lambda after: _hier_gather_wait("wait_w_ffn0_up", gw_up0, after)[0], cw24[0],
        lambda after: _exchange_wait("wait_w_ffn0_down", gw_rest, after, [0])[0].reshape(4, 704, D))

    w_cdin, w_uq, w_ukv, w_cdout = _exchange_wait("wait_w_cd", gw_rest, x2, [1, 2, 3, 4])
    w_cdout2 = w_cdout.reshape(2, 512, D)
    w_cd_t = w_cdin.reshape(1440, D)
    zr = lambda n: jnp.zeros((n, D), W)
    w_cd_pad = jnp.concatenate([w_cd_t[:384], zr(64), w_cd_t[384:416], zr(32), w_cd_t[416:]], axis=0)
    w_uq_pad = jnp.pad(w_uq, ((0, 0), (0, 32), (0, 0))).reshape(1024, 256)
    w_ukv_h = w_ukv.transpose(1, 0, 2)
    w_k_pad = jnp.pad(w_ukv_h[:, :, :64], ((0, 0), (0, 0), (0, 64))).reshape(128, 1024)
    w_kv_pad = jnp.concatenate([w_k_pad, w_ukv_h[:, :, 64:].reshape(128, 512)], axis=1)

    sh1, sc1, g1c = mods[1][:3]
    h_cd = _rmsmod_fwd("cd_norm", x2, n1g[1], sc1, sh1, n1g[1])
    z_cd = _mm_nt("cd_in", h_cd, w_cd_pad)
    qn, kvn, q_r, k_r, v_r = _qkv_rope_fwd(z_cd, qg, kvg, w_uq_pad, w_kv_pad, cosq, cosk, sa, sb)
    o, lse = _attn_fwd(q_r, k_r, v_r)
    ycat_cd = _sgu_fwd(z_cd, o, ln_g, ln_b, w_s, b_st)
    y_cd, x3 = _mm_rows_resid("cd_out", ycat_cd, w_cdout2, x2, g1c)
    x4, ffn1_saved, w_up8[1], w_down4[1] = _ffn_fwd(
        1, x3, mods[1][3:], n2g[1], lambda after: _exchange_wait("wait_w_ffn1_up", gw_rest, after, [5])[0], cw24[1],
        lambda after: _exchange_wait("wait_w_ffn1_down", gw_rest, after, [6])[0].reshape(4, 704, D))

    loss_local, dx4, dfg, df1, dgate1 = _loss_head(x4, final_norm_g.reshape(1, D), target, ffn1_saved[4], mods[1][5])

    dx3, gf1, dy, dg1c = _ffn_bwd(1, dx4, df1, dgate1, ffn1_saved, mods[1][3:], n2g[1], w_up8[1], cw24[1], w_down4[1], me, y_cd, g1c)

    dycat = _mm_rows_dx("cd_out_dx", dy, w_cdout2)
    dw_cdout = _mm_rows_dw("cd_out_dw", ycat_cd, dy, out_dtype=WIRE_DTYPE)
    duv, dln_g, dln_b, dws, dbs = _sgu_bwd(z_cd, dycat, ln_g, ln_b, w_s, b_st)
    dq_r, dk_r, dv_r = _attn_bwd(q_r, k_r, v_r, lse, *_attn_bwd_prep(o, dycat))
    dqraw, dkvall, dz_cd, dqg, dkvg = _qkv_rope_bwd(z_cd, qg, kvg, dq_r, dk_r, dv_r, duv, w_uq_pad, w_kv_pad, cosq, cosk, sa, sb)
    dw_uq_pad = _mm_tn("cd_uq_dw", dqraw, qn, tn=256)
    dw_kv_pad = _mm_tn("cd_ukv_dw", kvn, dkvall, tm=128)
    dh_cd = _mm_nn("cd_in_dx", dz_cd, w_cd_pad, out_dtype=ACT_DTYPE)
    dw_cd_pad = _mm_tn("cd_in_dw", dz_cd, h_cd)
    dw_cd8 = jnp.concatenate([dw_cd_pad[:384], dw_cd_pad[448:480], dw_cd_pad[512:]], axis=0).astype(WIRE_DTYPE).reshape(8, 180, D)
    dw_uq8 = dw_uq_pad.reshape(8, 128, 256)[:, :96].astype(WIRE_DTYPE)
    dw_ukv8 = jnp.concatenate([dw_kv_pad[:, :1024].reshape(128, 8, 128)[:, :, :64], dw_kv_pad[:, 1024:].reshape(128, 8, 64)],
                              axis=2).transpose(1, 0, 2).astype(WIRE_DTYPE)
    sent_cd, token = _exchange_start("scatter_cd", [dw_cd8, dw_uq8, dw_ukv8, dw_cdout.reshape(8, 128, D)], True, dqg, me)
    early_names = ["c_kv_norm_g", "d_w_s", "d_b_s", "final_norm_g", "c_q_norm_g", "d_ln_g", "d_ln_b"]
    early_grads = [dkvg, dws.reshape(512, 128).astype(WIRE_DTYPE), dbs, dfg, dqg.reshape(8, 1, 32), dln_g.reshape(8, 1, 64),
                   dln_b.reshape(8, 1, 64)]
    early_sent, token = _exchange_start("gather_small_grads_early", early_grads, [False] * 4 + [True] * 3, token, me)
    dx2, dn1g_cd, dsc1_cd, dsh1_cd, df0, dgate0 = _rmsmod_bwd("cd_norm_bwd", x2, n1g[1], sc1, dh_cd, dx3, token,
                                                              ffn0_saved[4], mods[0][5])

    dx1, gf0, dy, dg1m = _ffn_bwd(0, dx2, df0, dgate0, ffn0_saved, mods[0][3:], n2g[0], w_up8[0], cw24[0], w_down4[0], me, y_ab, g1m)

    dw_about = _mm_rows_dw("ab_out_dw", ycat_ab, dy, out_dtype=WIRE_DTYPE)
    sent_about, token = _exchange_start("scatter_ab_out", [dw_about.reshape(8, 128, D)], True, dg1m, me)
    dycat = _mm_rows_dx("ab_out_dx", dy, w_about2)
    dz8, dconv_w, dmix_w, dscale = _ab_mix_bwd(z8, dycat, conv_w, mix_w, scale, token)
    dz8 = dz8.reshape(8, S, 256)
    dw_abin8 = _mm_cols_dw("ab_in_dw", h_ab, dz8, out_dtype=WIRE_DTYPE, tk=1024)
    sent_abin, token = _exchange_start("scatter_ab_in", [dw_abin8], True, dscale, me)
    dh_ab = _mm_cols_dx("ab_in_dx", dz8, w_abin8, out_dtype=ACT_DTYPE)
    dx0, dn1g_ab, dsc1_ab, dsh1_ab = _rmsmod_bwd("ab_norm_bwd", x0, n1g[0], mods[0][1], dh_ab, dx1, token)

    dmod = jnp.stack([jnp.concatenate([dsh1_ab, dsc1_ab, dg1m, *gf0["mod"]], axis=1)[0],
                      jnp.concatenate([dsh1_cd, dsc1_cd, dg1c, *gf1["mod"]], axis=1)[0]])
    late_names = ["ada_b", "norm1_g", "norm2_g", "b_mix_w", "b_scale", "a_conv_w", "ffn_conv_w"]
    late_grads = [dmod, jnp.concatenate([dn1g_ab, dn1g_cd]), jnp.concatenate([gf0["n2g"], gf1["n2g"]]),
                  dmix_w.reshape(512, 128).astype(WIRE_DTYPE), dscale, dconv_w.reshape(3, 8, 64).transpose(1, 0, 2),
                  jnp.stack([gf0["cw24"].reshape(8, 3, 704), gf1["cw24"].reshape(8, 3, 704)], axis=1),
                  jnp.pad(loss_local, ((0, 0), (0, 127)))]
    small_view = dict(ada_b=(2, 6 * D), norm1_g=(2, D), norm2_g=(2, D), b_mix_w=(512, 128), b_scale=(1, 512), c_kv_norm_g=(1, 128),
                      d_w_s=(512, 128), d_b_s=(4, 128), final_norm_g=(1, D),
                      a_conv_w=(3, 64), c_q_norm_g=(1, 32), d_ln_g=(1, 64), d_ln_b=(1, 64), ffn_conv_w=(2, 3, 704))
    late_sent, token = _exchange_start("gather_small_grads_late", late_grads, [False] * 5 + [True] * 2 + [False], dx0, me)

    res = {}

    def update(name, w, m, v, parts, shape3d):
        outs = _adamw("adamw_" + name, w.reshape(shape3d), m.reshape(shape3d), v.reshape(shape3d),
                      [p.reshape((p.shape[0],) + shape3d[1:]) for p in parts])
        res[name] = [o_.reshape(w.shape) for o_ in outs]

    p_cdin, p_uq, p_ukv, p_cdout = _exchange_wait("wait_scatter_cd", sent_cd, token)
    swap = lambda a: jnp.swapaxes(a, 1, 2)
    update("cd_w_in", swap(cd_w_in), swap(m_cd_w_in), swap(v_cd_w_in), [p_cdin], (1, 180, D))
    update("c_w_uq", swap(c_w_uq), swap(m_c_w_uq), swap(v_c_w_uq), [p_uq], (1, 96, 256))
    for name in ("cd_w_in", "c_w_uq"):
        res[name] = [swap(o_) for o_ in res[name]]
    update("c_w_ukv", c_w_ukv, m_c_w_ukv, v_c_w_ukv, [p_ukv], (1, 128, 128))
    update("cd_w_out", cd_w_out, m_cd_w_out, v_cd_w_out, [p_cdout], (1, 128, D))
    (p_dn1,) = _exchange_wait("wait_scatter_ffn1_down", gf1["sent_down"], token)
    (p_dn0,) = _exchange_wait("wait_scatter_ffn0_down", gf0["sent_down"], res["cd_w_out"][0])
    update("ffn_w_down", ffn_w_down, m_ffn_w_down, v_ffn_w_down, [p_dn0, p_dn1], (2, 352, D))
    (p_up1,) = _exchange_wait("wait_scatter_ffn1_up", gf1["sent_up"], token)
    (p_up0,) = _exchange_wait("wait_scatter_ffn0_up", gf0["sent_up"], res["ffn_w_down"][0])
    swap = lambda a: jnp.swapaxes(a, 1, 2)
    update("ffn_w_up", swap(ffn_w_up), swap(m_ffn_w_up), swap(v_ffn_w_up), [p_up0, p_up1], (2, 704, D))
    up_done = res["ffn_w_up"][0]
    res["ffn_w_up"] = [swap(o_) for o_ in res["ffn_w_up"]]
    (p_about,) = _exchange_wait("wait_scatter_ab_out", sent_about, up_done)
    update("ab_w_out", ab_w_out, m_ab_w_out, v_ab_w_out, [p_about], (1, 128, D))
    (p_abin,) = _exchange_wait("wait_scatter_ab_in", sent_abin, res["ab_w_out"][0])
    update("ab_w_in", ab_w_in, m_ab_w_in, v_ab_w_in, [p_abin], (1, D, 256))

    early_parts = _exchange_wait("wait_small_grads_early", early_sent, res["ab_w_in"][0])
    late_parts = _exchange_wait("wait_small_grads_late", late_sent, res["ab_w_in"][0])
    small_names = early_names + late_names
    small_parts = list(early_parts) + list(late_parts[:7])
    loss = jnp.sum(late_parts[7][:, 0, 0])
    dmod_all = late_parts[0]
    dmod_cols = lax.dynamic_slice_in_dim(dmod_all, me * 768, 768, axis=2).transpose(1, 0, 2)
    g_ada_w = _ada_bwd(c16, jnp.pad(dmod_cols, ((0, 0), (0, 16 - N_DEV), (0, 0))))
    update("ada_w", ada_w, m_ada_w, v_ada_w, [g_ada_w[None]], (1, 2 * D, 768))

    small_w = dict(ada_b=(ada_b, m_ada_b, v_ada_b), norm1_g=(norm1_g, m_norm1_g, v_norm1_g), norm2_g=(norm2_g, m_norm2_g, v_norm2_g),
                   b_mix_w=(b_mix_w, m_b_mix_w, v_b_mix_w), b_scale=(b_scale, m_b_scale, v_b_scale),
                   c_kv_norm_g=(c_kv_norm_g, m_c_kv_norm_g, v_c_kv_norm_g), d_w_s=(d_w_s, m_d_w_s, v_d_w_s),
                   d_b_s=(d_b_s, m_d_b_s, v_d_b_s), final_norm_g=(final_norm_g, m_final_norm_g, v_final_norm_g),
                   a_conv_w=(a_conv_w, m_a_conv_w, v_a_conv_w), c_q_norm_g=(c_q_norm_g, m_c_q_norm_g, v_c_q_norm_g),
                   d_ln_g=(d_ln_g, m_d_ln_g, v_d_ln_g), d_ln_b=(d_ln_b, m_d_ln_b, v_d_ln_b),
                   ffn_conv_w=(ffn_conv_w, m_ffn_conv_w, v_ffn_conv_w))
    small_out = _adamw_small("adamw_small", [tuple(a.reshape(small_view[n]) for a in small_w[n]) for n in small_names],
                             list(small_parts))
    for n, outs in zip(small_names, small_out):
        res[n] = [o_.reshape(small_w[n][0].shape) for o_ in outs]

    order = ["ada_w", "ada_b", "norm1_g", "norm2_g", "ab_w_in", "a_conv_w", "b_mix_w", "b_scale", "ab_w_out", "cd_w_in", "c_q_norm_g",
             "c_w_uq", "c_kv_norm_g", "c_w_ukv", "d_ln_g", "d_ln_b", "d_w_s", "d_b_s", "cd_w_out", "ffn_w_up", "ffn_conv_w",
             "ffn_w_down", "final_norm_g"]
    return (loss, dx0[None], *[res[n][0] for n in order], *[res[n][1] for n in order], *[res[n][2] for n in order],
            *[res[n][3] for n in order])
```

```python
import functools
import math

import jax
import jax.numpy as jnp
from jax import lax
from jax.experimental import pallas as pl
from jax.experimental.pallas import tpu as pltpu

F32 = jnp.float32
BF16 = jnp.bfloat16
_MXU_DTYPE = BF16
WIRE_DTYPE = BF16
ACT_DTYPE = BF16
_VMEM_LIMIT = 56 * 2 ** 20
N_DEV = 8
EPS = 1e-6
POOL_WINDOWS = (2, 4, 8, 16)
ATTN_SCALE = (64 + 32) ** -0.5
ADAM_LR, ADAM_B1, ADAM_B2, ADAM_EPS, ADAM_WD, ADAM_STEP = 0.001, 0.9, 0.999, 1e-08, 0.01, 10
MESH = pl.DeviceIdType.MESH
ANY = pl.BlockSpec(memory_space=pl.ANY)


def _cp(*sem):
    return pltpu.CompilerParams(dimension_semantics=sem, vmem_limit_bytes=_VMEM_LIMIT)


def _dot(a, b, contract):
    dn = {"nn": (((1,), (0,)), ((), ())), "nt": (((1,), (1,)), ((), ())), "tn": (((0,), (0,)), ((), ()))}[contract]
    return lax.dot_general(a.astype(_MXU_DTYPE), b.astype(_MXU_DTYPE), dn, preferred_element_type=F32)


def _my_position():
    x, y, c = lax.axis_index("x"), lax.axis_index("y"), lax.axis_index("c")
    return x, y, c, 4 * x + 2 * y + c


def _exchange(name, groups, scatter):
    flat = [a for g in groups for a in g]
    n_in, n_grp = len(flat), len(groups)
    out_shapes = []
    for g in groups:
        slab = g[0].shape[1:] if scatter else g[0].shape
        out_shapes.append(jax.ShapeDtypeStruct((N_DEV, len(g)) + tuple(slab), g[0].dtype))

    def body(*refs):
        ins, outs = refs[:n_in], refs[n_in:n_in + n_grp]
        send_sems, recv_sems, local_sems = refs[n_in + n_grp:]
        x, y, c, me = _my_position()
        i = 0
        for gi, g in enumerate(groups):
            for l in range(len(g)):
                src = ins[i]
                i += 1
                pltpu.make_async_copy(src.at[me] if scatter else src, outs[gi].at[me, l], local_sems.at[gi]).start()
                for k in range(1, N_DEV):
                    px = 1 - x if k & 4 else x
                    py = 1 - y if k & 2 else y
                    pc = 1 - c if k & 1 else c
                    peer = 4 * px + 2 * py + pc
                    pltpu.make_async_remote_copy(
                        src_ref=src.at[peer] if scatter else src, dst_ref=outs[gi].at[me, l],
                        send_sem=send_sems.at[gi], recv_sem=recv_sems.at[gi],
                        device_id=(px, py, pc), device_id_type=MESH).start()
        for gi in range(n_grp):
            mine = outs[gi].at[me]
            pltpu.make_async_copy(mine, mine, local_sems.at[gi]).wait()
            seven = outs[gi].at[pl.ds(0, N_DEV - 1)]
            w = pltpu.make_async_remote_copy(src_ref=seven, dst_ref=seven, send_sem=send_sems.at[gi],
                                             recv_sem=recv_sems.at[gi], device_id=(x, y, c), device_id_type=MESH)
            w.wait_send()
            w.wait_recv()

    return pl.pallas_call(
        body, name=name, out_shape=tuple(out_shapes),
        in_specs=[ANY] * n_in, out_specs=tuple([ANY] * n_grp),
        scratch_shapes=[pltpu.SemaphoreType.DMA((n_grp,)), pltpu.SemaphoreType.DMA((n_grp,)),
                        pltpu.SemaphoreType.DMA((n_grp,))],
        compiler_params=pltpu.CompilerParams(has_side_effects=True),
    )(*flat)


HBM_SPEC = pl.BlockSpec(memory_space=pltpu.HBM)
SEM_SPEC = pl.BlockSpec(memory_space=pltpu.SEMAPHORE)
EFFECT = pltpu.SideEffectType.DATAFLOW_SIDE_EFFECTING


def _put_mine(name, srcs, scatter, me):
    n = len(srcs)
    slabs = [tuple(s.shape[1:] if sc else s.shape) for s, sc in zip(srcs, scatter)]

    def body(me_ref, *refs):
        for i in range(n):
            refs[n + i][...] = refs[i][...]

    def at_me(slab):
        return pl.BlockSpec((None,) + slab, lambda g, me_ref, nd=len(slab): (me_ref[0],) + (0,) * nd)

    def whole(slab):
        return pl.BlockSpec(slab, lambda g, me_ref, nd=len(slab): (0,) * nd)

    return pl.pallas_call(
        body, name=name,
        grid_spec=pltpu.PrefetchScalarGridSpec(
            num_scalar_prefetch=1, grid=(1,),
            in_specs=[at_me(slab) if sc else whole(slab) for slab, sc in zip(slabs, scatter)],
            out_specs=[at_me(slab) for slab in slabs]),
        out_shape=[jax.ShapeDtypeStruct((N_DEV,) + slab, s.dtype) for slab, s in zip(slabs, srcs)],
        compiler_params=_cp("arbitrary"))(me.reshape(1), *srcs)


def _exchange_start(name, srcs, scatter, after, me):
    n = len(srcs)
    scatter = list(scatter) if isinstance(scatter, (list, tuple)) else [scatter] * n
    lands = _put_mine(name + "_mine", srcs, scatter, me)
    srcs = [pltpu.with_memory_space_constraint(a, pltpu.HBM) for a in srcs]
    lands = [pltpu.with_memory_space_constraint(a, pltpu.HBM) for a in lands]

    def body(*refs):
        ins, land = refs[:n], refs[n:2 * n]
        send_sems, recv_sems, token = refs[2 * n + 1], refs[2 * n + 2], refs[-1]
        x, y, c, me_in = _my_position()
        for i in range(n):
            for k in range(1, N_DEV):
                px = 1 - x if k & 4 else x
                py = 1 - y if k & 2 else y
                pc = 1 - c if k & 1 else c
                pltpu.make_async_remote_copy(
                    src_ref=ins[i].at[4 * px + 2 * py + pc] if scatter[i] else ins[i], dst_ref=land[i].at[me_in],
                    send_sem=send_sems.at[i], recv_sem=recv_sems.at[i],
                    device_id=(px, py, pc), device_id_type=MESH).start()
        token[...] = jnp.zeros_like(token)

    outs = pl.pallas_call(
        body, name=name,
        out_shape=(pltpu.SemaphoreType.DMA((n,)), pltpu.SemaphoreType.DMA((n,)),
                   *[pltpu.HBM(a.shape, a.dtype) for a in srcs], *[pltpu.HBM(a.shape, a.dtype) for a in lands],
                   jax.ShapeDtypeStruct((8, 128), F32)),
        in_specs=[HBM_SPEC] * (2 * n) + [ANY],
        out_specs=(SEM_SPEC, SEM_SPEC, *[HBM_SPEC] * (2 * n), pl.BlockSpec(memory_space=pltpu.VMEM)),
        input_output_aliases={i: 2 + i for i in range(2 * n)},
        compiler_params=pltpu.CompilerParams(has_side_effects=EFFECT),
    )(*srcs, *lands, after)
    return (outs[0], outs[1], outs[2:2 + n], outs[2 + n:2 + 2 * n]), outs[-1]


def _exchange_wait(name, handle, after, which=None):
    send_sems, recv_sems, srcs, lands = handle
    which = list(range(len(srcs))) if which is None else list(which)
    srcs, lands = [srcs[i] for i in which], [lands[i] for i in which]
    n = len(srcs)

    def body(*refs):
        land, send_ref, recv_ref = refs[n:2 * n], refs[2 * n], refs[2 * n + 1]
        x, y, c, _ = _my_position()
        for k, i in enumerate(which):
            seven = land[k].at[pl.ds(0, N_DEV - 1)]
            w = pltpu.make_async_remote_copy(src_ref=seven, dst_ref=seven, send_sem=send_ref.at[i], recv_sem=recv_ref.at[i],
                                             device_id=(x, y, c), device_id_type=MESH)
            w.wait_send()
            w.wait_recv()

    outs = pl.pallas_call(
        body, name=name,
        out_shape=(*[pltpu.HBM(a.shape, a.dtype) for a in srcs], *[pltpu.HBM(a.shape, a.dtype) for a in lands]),
        in_specs=[HBM_SPEC] * (2 * n) + [SEM_SPEC, SEM_SPEC, ANY],
        out_specs=tuple([HBM_SPEC] * (2 * n)),
        input_output_aliases={i: i for i in range(2 * n)},
        compiler_params=pltpu.CompilerParams(has_side_effects=EFFECT),
    )(*srcs, *lands, send_sems, recv_sems, after)
    return outs[n:]


def _other_chips(x, y):
    return [(1 - x, y), (x, 1 - y), (1 - x, 1 - y)]


def _hier_gather_start(name, srcs, after, me):
    n = len(srcs)
    lands = _put_mine(name + "_mine", srcs, [False] * n, me)
    srcs = [pltpu.with_memory_space_constraint(a, pltpu.HBM) for a in srcs]
    lands = [pltpu.with_memory_space_constraint(a, pltpu.HBM) for a in lands]

    def body(*refs):
        ins, land = refs[:n], refs[n:2 * n]
        ici_send, ici_recv, d2d_send, d2d_recv = refs[2 * n + 1:2 * n + 5]
        token = refs[-1]
        x, y, c, me_in = _my_position()
        for i in range(n):
            pltpu.make_async_remote_copy(src_ref=ins[i], dst_ref=land[i].at[me_in], send_sem=d2d_send.at[i], recv_sem=d2d_recv.at[i],
                                         device_id=(x, y, 1 - c), device_id_type=MESH).start()
            for px, py in _other_chips(x, y):
                pltpu.make_async_remote_copy(src_ref=ins[i], dst_ref=land[i].at[me_in], send_sem=ici_send.at[i],
                                             recv_sem=ici_recv.at[i], device_id=(px, py, c), device_id_type=MESH).start()
        token[...] = jnp.zeros_like(token)

    sem = pltpu.SemaphoreType.DMA((n,))
    outs = pl.pallas_call(
        body, name=name,
        out_shape=(sem, sem, sem, sem, *[pltpu.HBM(a.shape, a.dtype) for a in srcs], *[pltpu.HBM(a.shape, a.dtype) for a in lands],
                   jax.ShapeDtypeStruct((8, 128), F32)),
        in_specs=[HBM_SPEC] * (2 * n) + [ANY],
        out_specs=(SEM_SPEC,) * 4 + (HBM_SPEC,) * (2 * n) + (pl.BlockSpec(memory_space=pltpu.VMEM),),
        input_output_aliases={i: 4 + i for i in range(2 * n)},
        compiler_params=pltpu.CompilerParams(has_side_effects=EFFECT),
    )(*srcs, *lands, after)
    return (outs[:4], outs[4:4 + n], outs[4 + n:4 + 2 * n]), outs[-1]


def _hier_gather_forward(name, handle, after):
    sems, srcs, lands = handle
    n = len(srcs)

    def body(*refs):
        land = refs[n:2 * n]
        ici_send, ici_recv, d2d_send, d2d_recv = refs[2 * n:2 * n + 4]
        x, y, c, _ = _my_position()
        for i in range(n):
            three = land[i].at[pl.ds(0, 3)]
            pltpu.make_async_remote_copy(src_ref=three, dst_ref=three, send_sem=ici_send.at[i], recv_sem=ici_recv.at[i],
                                         device_id=(x, y, c), device_id_type=MESH).wait_recv()
            for px, py in _other_chips(x, y):
                slab = land[i].at[4 * px + 2 * py + c]
                pltpu.make_async_remote_copy(src_ref=slab, dst_ref=slab, send_sem=d2d_send.at[i], recv_sem=d2d_recv.at[i],
                                             device_id=(x, y, 1 - c), device_id_type=MESH).start()

    outs = pl.pallas_call(
        body, name=name,
        out_shape=(*[pltpu.HBM(a.shape, a.dtype) for a in srcs], *[pltpu.HBM(a.shape, a.dtype) for a in lands]),
        in_specs=[HBM_SPEC] * (2 * n) + [SEM_SPEC] * 4 + [ANY],
        out_specs=tuple([HBM_SPEC] * (2 * n)),
        input_output_aliases={i: i for i in range(2 * n)},
        compiler_params=pltpu.CompilerParams(has_side_effects=EFFECT),
    )(*srcs, *lands, *sems, after)
    return (sems, outs[:n], outs[n:])


def _hier_gather_wait(name, handle, after):
    sems, srcs, lands = handle
    n = len(srcs)

    def body(*refs):
        land = refs[n:2 * n]
        ici_send, ici_recv, d2d_send, d2d_recv = refs[2 * n:2 * n + 4]
        x, y, c, _ = _my_position()
        for i in range(n):
            three, four = land[i].at[pl.ds(0, 3)], land[i].at[pl.ds(0, 4)]
            pltpu.make_async_remote_copy(src_ref=three, dst_ref=three, send_sem=ici_send.at[i], recv_sem=ici_recv.at[i],
                                         device_id=(x, y, c), device_id_type=MESH).wait_send()
            w = pltpu.make_async_remote_copy(src_ref=four, dst_ref=four, send_sem=d2d_send.at[i], recv_sem=d2d_recv.at[i],
                                             device_id=(x, y, c), device_id_type=MESH)
            w.wait_send()
            w.wait_recv()

    outs = pl.pallas_call(
        body, name=name,
        out_shape=(*[pltpu.HBM(a.shape, a.dtype) for a in srcs], *[pltpu.HBM(a.shape, a.dtype) for a in lands]),
        in_specs=[HBM_SPEC] * (2 * n) + [SEM_SPEC] * 4 + [ANY],
        out_specs=tuple([HBM_SPEC] * (2 * n)),
        input_output_aliases={i: i for i in range(2 * n)},
        compiler_params=pltpu.CompilerParams(has_side_effects=EFFECT),
    )(*srcs, *lands, *sems, after)
    return outs[n:]


def _pack(arrs):
    flat = jnp.concatenate([a.reshape(-1).astype(F32) for a in arrs])
    n = flat.shape[0]
    rows = -(-n // 1024) * 8
    return jnp.pad(flat, (0, rows * 128 - n)).reshape(rows, 128)


def _unpack(buf, shapes, lead=()):
    flat = buf.reshape(lead + (-1,))
    out, off = [], 0
    for s in shapes:
        n = math.prod(s)
        out.append(flat[..., off:off + n].reshape(lead + tuple(s)))
        off += n
    return out


def _mm(name, a, a_spec, b, b_spec, out_sds, o_spec, grid, contract, nk=1, stacked=0):
    o_blk = tuple(d for d in o_spec.block_shape if d is not None)

    def body(a_ref, b_ref, o_ref, *acc):
        if stacked:
            r = _dot(a_ref[0], b_ref[0], contract)
            for q in range(1, stacked):
                r = r + _dot(a_ref[q], b_ref[q], contract)
        else:
            r = _dot(a_ref[...], b_ref[...], contract)
        if nk == 1:
            o_ref[...] = r.astype(o_ref.dtype)
        else:
            k = pl.program_id(len(grid) - 1)

            @pl.when(k == 0)
            def _():
                acc[0][...] = r

            @pl.when(k > 0)
            def _():
                acc[0][...] += r

            @pl.when(k == nk - 1)
            def _():
                o_ref[...] = acc[0][...].astype(o_ref.dtype)

    sem = ("parallel",) * (len(grid) - 1) + (("arbitrary",) if nk > 1 else ("parallel",))
    return pl.pallas_call(
        body, name=name, out_shape=out_sds, grid=grid, in_specs=[a_spec, b_spec], out_specs=o_spec,
        scratch_shapes=[pltpu.VMEM(o_blk, F32)] if nk > 1 else [], compiler_params=_cp(*sem))(a, b)


def _tile(n, want):
    t = min(n, want)
    assert n % t == 0, (n, t)
    return t


def _mm_nn(name, a, b, out_dtype=F32, tm=512, tn=512):
    (M, K), N = a.shape, b.shape[1]
    tm, tn = _tile(M, tm), _tile(N, tn)
    return _mm(name, a, pl.BlockSpec((tm, K), lambda i, j: (i, 0)), b, pl.BlockSpec((K, tn), lambda i, j: (0, j)),
               jax.ShapeDtypeStruct((M, N), out_dtype), pl.BlockSpec((tm, tn), lambda i, j: (i, j)),
               (M // tm, N // tn), "nn")


def _mm_nt(name, a, b, out_dtype=F32, tm=512, tn=512):
    (M, K), N = a.shape, b.shape[0]
    tm, tn = _tile(M, tm), _tile(N, tn)
    return _mm(name, a, pl.BlockSpec((tm, K), lambda i, j: (i, 0)), b, pl.BlockSpec((tn, K), lambda i, j: (j, 0)),
               jax.ShapeDtypeStruct((M, N), out_dtype), pl.BlockSpec((tm, tn), lambda i, j: (i, j)),
               (M // tm, N // tn), "nt")


def _mm_tn(name, a, b, out_dtype=F32, tm=512, tn=512):
    (K, M), N = a.shape, b.shape[1]
    tm, tn = _tile(M, tm), _tile(N, tn)
    return _mm(name, a, pl.BlockSpec((K, tm), lambda i, j: (0, i)), b, pl.BlockSpec((K, tn), lambda i, j: (0, j)),
               jax.ShapeDtypeStruct((M, N), out_dtype), pl.BlockSpec((tm, tn), lambda i, j: (i, j)),
               (M // tm, N // tn), "tn")


def _mm_cols(name, a, w, out_dtype=F32, tm=512):
    (M, K), (J, _, n) = a.shape, w.shape
    tm = _tile(M, tm)
    return _mm(name, a, pl.BlockSpec((tm, K), lambda j, i: (i, 0)), w, pl.BlockSpec((None, K, n), lambda j, i: (j, 0, 0)),
               jax.ShapeDtypeStruct((J, M, n), out_dtype), pl.BlockSpec((None, tm, n), lambda j, i: (j, i, 0)),
               (J, M // tm), "nn")


def _mm_cols_dx(name, d, w, out_dtype=F32, tm=512, jb=None):
    (J, M, n), K = d.shape, w.shape[1]
    tm, jb = _tile(M, tm), J if jb is None else jb
    return _mm(name, d, pl.BlockSpec((jb, tm, n), lambda i, j: (j, i, 0)), w, pl.BlockSpec((jb, K, n), lambda i, j: (j, 0, 0)),
               jax.ShapeDtypeStruct((M, K), out_dtype), pl.BlockSpec((tm, K), lambda i, j: (i, 0)),
               (M // tm, J // jb), "nt", nk=J // jb, stacked=jb)


def _mm_cols_dw(name, a, d, out_dtype=F32, tk=512):
    (M, K), (J, _, n) = a.shape, d.shape
    tk = _tile(K, tk)
    return _mm(name, a, pl.BlockSpec((M, tk), lambda j, i: (0, i)), d, pl.BlockSpec((None, M, n), lambda j, i: (j, 0, 0)),
               jax.ShapeDtypeStruct((J, K, n), out_dtype), pl.BlockSpec((None, tk, n), lambda j, i: (j, i, 0)),
               (J, K // tk), "tn")


def _mm_cols_dwt(name, a, d, out_dtype=F32, tk=512):
    (M, K), (J, _, n) = a.shape, d.shape
    tk = _tile(K, tk)
    return _mm(name, d, pl.BlockSpec((None, M, n), lambda j, i: (j, 0, 0)), a, pl.BlockSpec((M, tk), lambda j, i: (0, i)),
               jax.ShapeDtypeStruct((J, n, K), out_dtype), pl.BlockSpec((None, n, tk), lambda j, i: (j, 0, i)),
               (J, K // tk), "tn")


def _mm_rows_resid(name, a, w, resid, gate, tm=512):
    (Q, M, k), N = a.shape, w.shape[2]
    tm = _tile(M, tm)

    def body(a_ref, w_ref, r_ref, g_ref, y_ref, x_ref):
        y = _dot(a_ref[0], w_ref[0], "nn")
        for q in range(1, Q):
            y = y + _dot(a_ref[q], w_ref[q], "nn")
        y_ref[...] = y.astype(y_ref.dtype)
        x_ref[...] = r_ref[...] + g_ref[...] * y

    return pl.pallas_call(
        body, name=name, grid=(M // tm,),
        out_shape=(jax.ShapeDtypeStruct((M, N), ACT_DTYPE), jax.ShapeDtypeStruct((M, N), F32)),
        in_specs=[pl.BlockSpec((Q, tm, k), lambda i: (0, i, 0)), pl.BlockSpec((Q, k, N), lambda i: (0, 0, 0)),
                  pl.BlockSpec((tm, N), lambda i: (i, 0)), pl.BlockSpec((1, N), lambda i: (0, 0))],
        out_specs=(pl.BlockSpec((tm, N), lambda i: (i, 0)), pl.BlockSpec((tm, N), lambda i: (i, 0))),
        compiler_params=_cp("parallel"))(a, w, resid, gate)


def _mm_rows_dx(name, d, w, out_dtype=F32, tm=512):
    (M, N), (Q, k, _) = d.shape, w.shape
    tm = _tile(M, tm)
    return _mm(name, d, pl.BlockSpec((tm, N), lambda q, i: (i, 0)), w, pl.BlockSpec((None, k, N), lambda q, i: (q, 0, 0)),
               jax.ShapeDtypeStruct((Q, M, k), out_dtype), pl.BlockSpec((None, tm, k), lambda q, i: (q, i, 0)),
               (Q, M // tm), "nt")


def _mm_rows_dw(name, a, d, out_dtype=F32, tn=512):
    (Q, M, k), N = a.shape, d.shape[1]
    tn = _tile(N, tn)
    return _mm(name, a, pl.BlockSpec((None, M, k), lambda q, j: (q, 0, 0)), d, pl.BlockSpec((M, tn), lambda q, j: (0, j)),
               jax.ShapeDtypeStruct((Q, k, N), out_dtype), pl.BlockSpec((None, k, tn), lambda q, j: (q, 0, j)),
               (Q, N // tn), "tn")


def _silu(v):
    return v * jax.nn.sigmoid(v)


def _ada_fwd(c16, ada_w):
    L, D, n = ada_w.shape

    def body(c_ref, w_ref, o_ref):
        o_ref[...] = _dot(_silu(c_ref[...]), w_ref[...], "nn")

    return pl.pallas_call(
        body, name="ada_fwd", grid=(L,), out_shape=jax.ShapeDtypeStruct((L, 16, n), F32),
        in_specs=[pl.BlockSpec((16, D), lambda l: (0, 0)), pl.BlockSpec((None, D, n), lambda l: (l, 0, 0))],
        out_specs=pl.BlockSpec((None, 16, n), lambda l: (l, 0, 0)), compiler_params=_cp("parallel"))(c16, ada_w)


def _ada_bwd(c16, dmod16):
    L, _, n = dmod16.shape
    D = c16.shape[1]

    def body(c_ref, d_ref, o_ref):
        o_ref[...] = _dot(_silu(c_ref[...]), d_ref[...], "tn")

    return pl.pallas_call(
        body, name="ada_bwd", grid=(L,), out_shape=jax.ShapeDtypeStruct((L, D, n), F32),
        in_specs=[pl.BlockSpec((16, D), lambda l: (0, 0)), pl.BlockSpec((None, 16, n), lambda l: (l, 0, 0))],
        out_specs=pl.BlockSpec((None, D, n), lambda l: (l, 0, 0)), compiler_params=_cp("parallel"))(c16, dmod16)


def _row_spec(tr, n):
    return pl.BlockSpec((tr, n), lambda i: (i, 0))


def _vec_spec(n):
    return pl.BlockSpec((1, n), lambda i: (0, 0))


def _rmsmod_fwd(name, x, g, sc, sh, after, tr=512):
    S, D = x.shape

    def body(x_ref, g_ref, sc_ref, sh_ref, after_ref, h_ref):
        xv = x_ref[...]
        rstd = lax.rsqrt(jnp.mean(xv * xv, axis=-1, keepdims=True) + EPS)
        y = xv * rstd * g_ref[...]
        h_ref[...] = (y * (1.0 + sc_ref[...]) + sh_ref[...]).astype(h_ref.dtype)

    return pl.pallas_call(
        body, name=name, grid=(S // tr,), out_shape=jax.ShapeDtypeStruct((S, D), _MXU_DTYPE),
        in_specs=[_row_spec(tr, D), _vec_spec(D), _vec_spec(D), _vec_spec(D), ANY], out_specs=_row_spec(tr, D),
        compiler_params=_cp("parallel"))(x, g, sc, sh, after)


def _acc_rows(ref, val, first):
    s = jnp.sum(val, axis=0, keepdims=True)

    @pl.when(first)
    def _():
        ref[...] = s

    @pl.when(jnp.logical_not(first))
    def _():
        ref[...] += s


def _gate_bwd_tail(dx, y_ref, gate_ref, dy_ref, dgate_ref, first):
    dy_ref[...] = (gate_ref[...] * dx).astype(dy_ref.dtype)
    _acc_rows(dgate_ref, dx * y_ref[...].astype(F32), first)


def _rmsmod_bwd(name, x, g, sc, dh, dres, after, y=None, gate=None, tr=512):
    S, D = x.shape
    tail = y is not None

    def body(x_ref, g_ref, sc_ref, dh_ref, dres_ref, after_ref, *rest):
        (y_ref, gate_ref), rest = (rest[:2], rest[2:]) if tail else ((None, None), rest)
        dx_ref, dg_ref, dsc_ref, dsh_ref = rest[:4]
        first = pl.program_id(0) == 0
        xv, dh_v, gv = x_ref[...], dh_ref[...].astype(F32), g_ref[...]
        rstd = lax.rsqrt(jnp.mean(xv * xv, axis=-1, keepdims=True) + EPS)
        xhat = xv * rstd
        _acc_rows(dsh_ref, dh_v, first)
        _acc_rows(dsc_ref, dh_v * (xhat * gv), first)
        dyg = dh_v * (1.0 + sc_ref[...])
        _acc_rows(dg_ref, dyg * xhat, first)
        dxhat = dyg * gv
        dx = dres_ref[...] + rstd * (dxhat - xhat * jnp.mean(dxhat * xhat, axis=-1, keepdims=True))
        dx_ref[...] = dx
        if tail:
            _gate_bwd_tail(dx, y_ref, gate_ref, rest[4], rest[5], first)

    vec = jax.ShapeDtypeStruct((1, D), F32)
    return pl.pallas_call(
        body, name=name, grid=(S // tr,),
        out_shape=(jax.ShapeDtypeStruct((S, D), F32), vec, vec, vec) + ((jax.ShapeDtypeStruct((S, D), _MXU_DTYPE), vec) if tail else ()),
        in_specs=[_row_spec(tr, D), _vec_spec(D), _vec_spec(D), _row_spec(tr, D), _row_spec(tr, D), ANY]
        + ([_row_spec(tr, D), _vec_spec(D)] if tail else []),
        out_specs=(_row_spec(tr, D), _vec_spec(D), _vec_spec(D), _vec_spec(D)) + ((_row_spec(tr, D), _vec_spec(D)) if tail else ()),
        compiler_params=_cp("arbitrary"))(x, g, sc, dh, dres, after, *((y, gate) if tail else ()))


def _loss_head(x, g, target, y, gate, tr=512):
    S, D = x.shape

    def body(x_ref, g_ref, t_ref, y_ref, gate_ref, loss_ref, dx_ref, dg_ref, dy_ref, dgate_ref):
        first = pl.program_id(0) == 0
        xv, gv = x_ref[...], g_ref[...]
        rstd = lax.rsqrt(jnp.mean(xv * xv, axis=-1, keepdims=True) + EPS)
        xhat = xv * rstd
        err = xhat * gv - t_ref[...]
        part = 0.5 * jnp.sum(jnp.mean(err * err, axis=-1, keepdims=True), axis=0, keepdims=True)

        @pl.when(first)
        def _():
            loss_ref[...] = part

        @pl.when(jnp.logical_not(first))
        def _():
            loss_ref[...] += part

        dout = err * (1.0 / D)
        _acc_rows(dg_ref, dout * xhat, first)
        dxhat = dout * gv
        dx = rstd * (dxhat - xhat * jnp.mean(dxhat * xhat, axis=-1, keepdims=True))
        dx_ref[...] = dx
        _gate_bwd_tail(dx, y_ref, gate_ref, dy_ref, dgate_ref, first)

    vec = jax.ShapeDtypeStruct((1, D), F32)
    return pl.pallas_call(
        body, name="loss_head", grid=(S // tr,),
        out_shape=(jax.ShapeDtypeStruct((1, 1), F32), jax.ShapeDtypeStruct((S, D), F32), vec,
                   jax.ShapeDtypeStruct((S, D), _MXU_DTYPE), vec),
        in_specs=[_row_spec(tr, D), _vec_spec(D), _row_spec(tr, D), _row_spec(tr, D), _vec_spec(D)],
        out_specs=(pl.BlockSpec((1, 1), lambda i: (0, 0)), _row_spec(tr, D), _vec_spec(D), _row_spec(tr, D), _vec_spec(D)),
        compiler_params=_cp("arbitrary"))(x, g, target, y, gate)


def _gate_bwd(name, dx, y, gate, tr=256):
    S, D = dx.shape

    def body(dx_ref, y_ref, g_ref, dy_ref, dg_ref):
        dxv = dx_ref[...]
        dy_ref[...] = (g_ref[...] * dxv).astype(dy_ref.dtype)
        _acc_rows(dg_ref, dxv * y_ref[...], pl.program_id(0) == 0)

    return pl.pallas_call(
        body, name=name, grid=(S // tr,),
        out_shape=(jax.ShapeDtypeStruct((S, D), _MXU_DTYPE), jax.ShapeDtypeStruct((1, D), F32)),
        in_specs=[_row_spec(tr, D), _row_spec(tr, D), _vec_spec(D)], out_specs=(_row_spec(tr, D), _vec_spec(D)),
        compiler_params=_cp("arbitrary"))(dx, y, gate)


def _shift_down(v, k):
    t = lax.broadcasted_iota(jnp.int32, v.shape, 0)
    return jnp.where(t >= k, pltpu.roll(v, k, axis=0), 0.0)


def _shift_up(v, k):
    n = v.shape[0]
    t = lax.broadcasted_iota(jnp.int32, v.shape, 0)
    return jnp.where(t < n - k, pltpu.roll(v, n - k, axis=0), 0.0)


def _window_sum(p, w, shift):
    s, k = p, 1
    while k < w:
        s = s + shift(s, k)
        k *= 2
    return s


def _pool_count(shape, w):
    t = lax.broadcasted_iota(jnp.int32, shape, 0)
    return jnp.minimum(t + 1, w).astype(F32)


def _ab_specs(S):
    zs = [pl.BlockSpec((None, S, 128), functools.partial(lambda g, q: (2 * q + g // 2, 0, g % 2), q=q)) for q in range(4)]
    return zs


def _ab_mix_fwd(z8, conv_w, mix_w, scale):
    S = z8.shape[1]

    def body(b_ref, c_ref, a_ref, p_ref, w_ref, mix_ref, sc_ref, y_ref):
        g = pl.program_id(0)
        cg = c_ref[...].astype(F32) * a_ref[...].astype(F32)
        w = w_ref[...]
        conv = w[0:1] * _shift_down(cg, 2) + w[1:2] * _shift_down(cg, 1) + w[2:3] * cg
        y_ref[0] = (b_ref[...].astype(F32) * conv).astype(y_ref.dtype)
        for gg, win in enumerate(POOL_WINDOWS):
            @pl.when(g == gg)
            def _(win=win):
                p = p_ref[...].astype(F32)
                pooled = _window_sum(p, win, _shift_down) / _pool_count(p.shape, win) - p
                y_ref[1] = (_dot(pooled, mix_ref[...], "nn") * sc_ref[...]).astype(y_ref.dtype)

    return pl.pallas_call(
        body, name="ab_mix_fwd", grid=(4,), out_shape=jax.ShapeDtypeStruct((2, S, 512), _MXU_DTYPE),
        in_specs=_ab_specs(S) + [pl.BlockSpec((3, 128), lambda g: (0, g)), pl.BlockSpec((None, 128, 128), lambda g: (g, 0, 0)),
                                 pl.BlockSpec((1, 128), lambda g: (0, g))],
        out_specs=pl.BlockSpec((2, S, 128), lambda g: (0, 0, g)), compiler_params=_cp("parallel"))(z8, z8, z8, z8, conv_w, mix_w, scale)


def _ab_mix_bwd(z8, dycat2, conv_w, mix_w, scale, after):
    S = z8.shape[1]

    def body(b_ref, c_ref, a_ref, p_ref, dy_ref, w_ref, mix_ref, sc_ref, after_ref, dz_ref, dw_ref, dmix_ref, dsc_ref):
        g = pl.program_id(0)
        bv, cv, av, w = b_ref[...].astype(F32), c_ref[...].astype(F32), a_ref[...].astype(F32), w_ref[...]
        dya = dy_ref[0]
        cg = cv * av
        cg1, cg2 = _shift_down(cg, 1), _shift_down(cg, 2)
        conv = w[0:1] * cg2 + w[1:2] * cg1 + w[2:3] * cg
        dz_ref[0] = (dya * conv).astype(dz_ref.dtype)
        dconv = dya * bv
        dcg = w[2:3] * dconv + w[1:2] * _shift_up(dconv, 1) + w[0:1] * _shift_up(dconv, 2)
        dz_ref[1] = (dcg * av).astype(dz_ref.dtype)
        dz_ref[2] = (dcg * cv).astype(dz_ref.dtype)
        dw_ref[0:1, :] = jnp.sum(dconv * cg2, axis=0, keepdims=True)
        dw_ref[1:2, :] = jnp.sum(dconv * cg1, axis=0, keepdims=True)
        dw_ref[2:3, :] = jnp.sum(dconv * cg, axis=0, keepdims=True)
        for gg, win in enumerate(POOL_WINDOWS):
            @pl.when(g == gg)
            def _(win=win):
                p, dyb, mix = p_ref[...].astype(F32), dy_ref[1], mix_ref[...]
                cnt = _pool_count(p.shape, win)
                pooled = _window_sum(p, win, _shift_down) / cnt - p
                dsc_ref[...] = jnp.sum(dyb * _dot(pooled, mix, "nn"), axis=0, keepdims=True)
                dmixed = dyb * sc_ref[...]
                dmix_ref[...] = _dot(pooled, dmixed, "tn")
                dpooled = _dot(dmixed, mix, "nt")
                dz_ref[3] = (_window_sum(dpooled / cnt, win, _shift_up) - dpooled).astype(dz_ref.dtype)

    return pl.pallas_call(
        body, name="ab_mix_bwd", grid=(4,),
        out_shape=(jax.ShapeDtypeStruct((4, 2, S, 256), _MXU_DTYPE), jax.ShapeDtypeStruct((3, 512), F32),
                   jax.ShapeDtypeStruct((4, 128, 128), F32), jax.ShapeDtypeStruct((1, 512), F32)),
        in_specs=_ab_specs(S) + [pl.BlockSpec((2, S, 128), lambda g: (0, 0, g)), pl.BlockSpec((3, 128), lambda g: (0, g)),
                                 pl.BlockSpec((None, 128, 128), lambda g: (g, 0, 0)), pl.BlockSpec((1, 128), lambda g: (0, g)), ANY],
        out_specs=(pl.BlockSpec((4, None, S, 128), lambda g: (0, g // 2, 0, g % 2)), pl.BlockSpec((3, 128), lambda g: (0, g)),
                   pl.BlockSpec((None, 128, 128), lambda g: (g, 0, 0)), pl.BlockSpec((1, 128), lambda g: (0, g))),
        compiler_params=_cp("parallel"))(z8, z8, z8, z8, dycat2, conv_w, mix_w, scale, after)


HALO = 16


def _ffn_specs(S, n, tr):
    nb = S // HALO
    tile = pl.BlockSpec((2, None, tr, n), lambda j, i: (0, j, i, 0))
    prev = pl.BlockSpec((2, None, HALO, n), lambda j, i: (0, j, jnp.maximum(i * (tr // HALO) - 1, 0), 0))
    nxt = pl.BlockSpec((2, None, HALO, n), lambda j, i: (0, j, jnp.minimum((i + 1) * (tr // HALO), nb - 1), 0))
    cw = pl.BlockSpec((2, None, 3, n), lambda j, i: (0, j, 0, 0))
    return tile, prev, nxt, cw


def _shifted_rows(ext, lo, rows):
    ext = ext.astype(F32)
    return pltpu.roll(ext, 1, axis=0)[lo:lo + rows], pltpu.roll(ext, 2, axis=0)[lo:lo + rows]


def _ffn_gate_fwd(name, u24, cw24, tr=256):
    _, J, S, n = u24.shape
    tile, prev, _, cw = _ffn_specs(S, n, tr)

    def body(u_ref, up_ref, w_ref, a_ref, z_ref):
        keep = (pl.program_id(1) > 0).astype(u_ref.dtype)
        z = []
        for h in range(2):
            ext = jnp.concatenate([up_ref[h] * keep, u_ref[h]], axis=0)
            x1, x2 = _shifted_rows(ext, HALO, tr)
            w = w_ref[h]
            zh = w[0:1] * x2 + w[1:2] * x1 + w[2:3] * u_ref[h].astype(F32)
            z_ref[h] = zh.astype(z_ref.dtype)
            z.append(zh)
        a_ref[...] = (_silu(z[0]) * z[1]).astype(a_ref.dtype)

    return pl.pallas_call(
        body, name=name, grid=(J, S // tr),
        out_shape=(jax.ShapeDtypeStruct((J, S, n), _MXU_DTYPE), jax.ShapeDtypeStruct((2, J, S, n), ACT_DTYPE)),
        in_specs=[tile, prev, cw], out_specs=(pl.BlockSpec((None, tr, n), lambda j, i: (j, i, 0)), tile),
        compiler_params=_cp("parallel", "parallel"))(u24, u24, cw24)


def _ffn_gate_bwd(name, u24, z24, cw24, da4, w_up24, after, tr=256):
    _, J, S, n = u24.shape
    K = w_up24.shape[2]
    nb = S // HALO
    tile = pl.BlockSpec((2, None, tr, n), lambda i, j: (0, j, i, 0))
    nxt = pl.BlockSpec((2, None, HALO, n), lambda i, j: (0, j, jnp.minimum((i + 1) * (tr // HALO), nb - 1), 0))
    whole = lambda shape: pl.BlockSpec(shape, lambda i, j: (0,) * len(shape))

    def body(u_ref, z_ref, zn_ref, cw_ref, da_ref, dan_ref, wup_ref, after_ref, du_ref, dcw_ref, dh_ref, acc_ref):
        i, j = pl.program_id(0), pl.program_id(1)
        first = i == 0
        keep_next = (i < S // tr - 1).astype(F32)
        w = [cw_ref[h, j] for h in range(2)]
        m = tr + HALO
        zg, zu = [jnp.concatenate([z_ref[h], zn_ref[h]], axis=0).astype(F32) for h in range(2)]
        da = jnp.concatenate([da_ref[...].astype(F32), dan_ref[...].astype(F32) * keep_next], axis=0)
        sg = jax.nn.sigmoid(zg)
        dz = [da * zu * (sg * (1.0 + zg * (1.0 - sg))), da * (zg * sg)]
        dh = None
        for h in range(2):
            d = dz[h]
            d0, d1, d2 = d[:tr], pltpu.roll(d, m - 1, axis=0)[:tr], pltpu.roll(d, m - 2, axis=0)[:tr]
            du = (w[h][2:3] * d0 + w[h][1:2] * d1 + w[h][0:1] * d2).astype(du_ref.dtype)
            du_ref[h] = du
            part = _dot(du, wup_ref[h, j], "nt")
            dh = part if dh is None else dh + part
            x0 = u_ref[h].astype(F32)
            parts = [jnp.sum(x0 * dk, axis=0, keepdims=True) for dk in (d2, d1, d0)]
            for k in range(3):
                @pl.when(first)
                def _(k=k, h=h):
                    dcw_ref[h, j, k:k + 1, :] = parts[k]

                @pl.when(jnp.logical_not(first))
                def _(k=k, h=h):
                    dcw_ref[h, j, k:k + 1, :] += parts[k]

        @pl.when(j == 0)
        def _():
            acc_ref[...] = dh

        @pl.when(j > 0)
        def _():
            acc_ref[...] += dh

        @pl.when(j == J - 1)
        def _():
            dh_ref[...] = acc_ref[...].astype(dh_ref.dtype)

    da_tile = pl.BlockSpec((None, tr, n), lambda i, j: (j, i, 0))
    da_next = pl.BlockSpec((None, HALO, n), lambda i, j: (j, jnp.minimum((i + 1) * (tr // HALO), nb - 1), 0))
    return pl.pallas_call(
        body, name=name, grid=(S // tr, J),
        out_shape=(jax.ShapeDtypeStruct((2, J, S, n), _MXU_DTYPE), jax.ShapeDtypeStruct((2, J, 3, n), F32),
                   jax.ShapeDtypeStruct((S, K), ACT_DTYPE)),
        in_specs=[tile, tile, nxt, whole((2, J, 3, n)), da_tile, da_next, whole((2, J, K, n)), ANY],
        out_specs=(tile, whole((2, J, 3, n)), pl.BlockSpec((tr, K), lambda i, j: (i, 0))),
        scratch_shapes=[pltpu.VMEM((tr, K), F32)],
        compiler_params=_cp("arbitrary", "arbitrary"))(u24, z24, z24, cw24, da4, da4, w_up24, after)


def _rms_rows(v, g):
    rstd = lax.rsqrt(jnp.mean(v * v, axis=-1, keepdims=True) + EPS)
    return v * rstd * g


def _rms_rows_bwd(v, g, dy):
    rstd = lax.rsqrt(jnp.mean(v * v, axis=-1, keepdims=True) + EPS)
    vhat = v * rstd
    dvhat = dy * g
    return rstd * (dvhat - vhat * jnp.mean(dvhat * vhat, axis=-1, keepdims=True)), dy * vhat


def _rope(v, cos, sa, sb):
    return v * cos + pltpu.roll(v, 112, axis=1) * sa + pltpu.roll(v, 16, axis=1) * sb


def _rope_t(d, cos, sa, sb):
    return d * cos + pltpu.roll(d * sa, 16, axis=1) + pltpu.roll(d * sb, 112, axis=1)


def _qkv_rope_fwd(z, qg, kvg, w_uq_t, w_kv, cosq, cosk, sa, sb, tr=256):
    S = z.shape[0]

    def body(ql_ref, kvl_ref, kpe_ref, qg_ref, kvg_ref, wq_ref, wkv_ref, cq_ref, ck_ref, sa_ref, sb_ref,
             qn_ref, kvn_ref, qo_ref, ko_ref, vo_ref):
        cq, ck, sa_v, sb_v = cq_ref[...], ck_ref[...], sa_ref[...], sb_ref[...]
        qn = _rms_rows(ql_ref[...], qg_ref[...]).astype(qn_ref.dtype)
        kvn = _rms_rows(kvl_ref[...], kvg_ref[...]).astype(kvn_ref.dtype)
        qn_ref[...] = qn
        kvn_ref[...] = kvn
        q = _dot(qn, wq_ref[...], "nt")
        kv = _dot(kvn, wkv_ref[...], "nn")
        kpe = _rope(kpe_ref[...], ck, sa_v, sb_v)
        for h in range(8):
            cols = slice(128 * h, 128 * h + 128)
            qo_ref[:, cols] = _rope(q[:, cols], cq, sa_v, sb_v).astype(qo_ref.dtype)
            ko_ref[:, cols] = (kv[:, cols] + kpe).astype(ko_ref.dtype)
        vo_ref[...] = kv[:, 1024:1536].astype(vo_ref.dtype)

    tab = _row_spec(tr, 128)
    whole = lambda a: pl.BlockSpec(a.shape, lambda i: (0, 0))
    return pl.pallas_call(
        body, name="qkv_rope_fwd", grid=(S // tr,),
        out_shape=(jax.ShapeDtypeStruct((S, 256), _MXU_DTYPE), jax.ShapeDtypeStruct((S, 128), _MXU_DTYPE),
                   jax.ShapeDtypeStruct((S, 1024), _MXU_DTYPE), jax.ShapeDtypeStruct((S, 1024), _MXU_DTYPE),
                   jax.ShapeDtypeStruct((S, 512), _MXU_DTYPE)),
        in_specs=[pl.BlockSpec((tr, 256), lambda i: (i, 0)), pl.BlockSpec((tr, 128), lambda i: (i, 2)),
                  pl.BlockSpec((tr, 128), lambda i: (i, 3)), _vec_spec(256), _vec_spec(128), whole(w_uq_t), whole(w_kv),
                  tab, tab, tab, tab],
        out_specs=(_row_spec(tr, 256), _row_spec(tr, 128), _row_spec(tr, 1024), _row_spec(tr, 1024), _row_spec(tr, 512)),
        compiler_params=_cp("parallel"))(z, z, z, qg, kvg, w_uq_t, w_kv, cosq, cosk, sa, sb)


def _attn_bwd_prep(o, dycat2, tr=256):
    S = o.shape[0]

    def body(o_ref, do_ref, delta_ref, doa_ref, dob_ref):
        do = do_ref[...]
        prod = do * o_ref[...]
        lane = lax.broadcasted_iota(jnp.int32, do.shape, 1)
        for p in range(4):
            cols = slice(128 * p, 128 * p + 128)
            first = lane[:, cols] < 128 * p + 64
            da = jnp.sum(jnp.where(first, prod[:, cols], 0.0), axis=-1, keepdims=True)
            db = jnp.sum(jnp.where(first, 0.0, prod[:, cols]), axis=-1, keepdims=True)
            delta_ref[p] = jnp.where(first, da, db)
            doa_ref[p] = jnp.where(first, do[:, cols], 0.0).astype(doa_ref.dtype)
            dob_ref[p] = jnp.where(first, 0.0, do[:, cols]).astype(dob_ref.dtype)

    pair = pl.BlockSpec((4, tr, 128), lambda i: (0, i, 0))
    return pl.pallas_call(
        body, name="attn_bwd_prep", grid=(S // tr,),
        out_shape=(jax.ShapeDtypeStruct((4, S, 128), F32), jax.ShapeDtypeStruct((4, S, 128), _MXU_DTYPE),
                   jax.ShapeDtypeStruct((4, S, 128), _MXU_DTYPE)),
        in_specs=[_row_spec(tr, 512), pl.BlockSpec((None, tr, 512), lambda i: (0, i, 0))],
        out_specs=(pair, pair, pair), compiler_params=_cp("parallel"))(o, dycat2)


def _qkv_rope_bwd(z, qg, kvg, dq, dk, dv, duv, w_uq_t, w_kv, cosq, cosk, sa, sb, tr=256):
    S = z.shape[0]

    def body(ql_ref, kvl_ref, qg_ref, kvg_ref, dq_ref, dk_ref, dv_ref, duv_ref, wq_ref, wkv_ref, cq_ref, ck_ref, sa_ref, sb_ref,
             dqo_ref, dkv_ref, dz_ref, dqg_ref, dkvg_ref):
        first = pl.program_id(0) == 0
        cq, ck, sa_v, sb_v = cq_ref[...], ck_ref[...], sa_ref[...], sb_ref[...]
        tot = jnp.zeros((tr, 128), F32)
        for h in range(8):
            cols = slice(128 * h, 128 * h + 128)
            dqo_ref[:, cols] = _rope_t(dq_ref[:, cols], cq, sa_v, sb_v).astype(dqo_ref.dtype)
            dkh = dk_ref[:, cols]
            tot = tot + dkh
            dkv_ref[:, cols] = dkh.astype(dkv_ref.dtype)
        dkv_ref[:, 1024:1536] = dv_ref[...].astype(dkv_ref.dtype)
        dqn = _dot(dqo_ref[...], wq_ref[...], "nn")
        dkvn = _dot(dkv_ref[...], wkv_ref[...], "nt")
        dql, dqg = _rms_rows_bwd(ql_ref[...], qg_ref[...], dqn)
        dkvl, dkvg = _rms_rows_bwd(kvl_ref[...], kvg_ref[...], dkvn)
        _acc_rows(dqg_ref, dqg, first)
        _acc_rows(dkvg_ref, dkvg, first)
        dz_ref[:, 0:256] = dql.astype(dz_ref.dtype)
        dz_ref[:, 256:384] = dkvl.astype(dz_ref.dtype)
        dz_ref[:, 384:512] = _rope_t(tot, ck, sa_v, sb_v).astype(dz_ref.dtype)
        dz_ref[:, 512:1536] = duv_ref[...].astype(dz_ref.dtype)

    tab = _row_spec(tr, 128)
    whole = lambda a: pl.BlockSpec(a.shape, lambda i: (0, 0))
    return pl.pallas_call(
        body, name="qkv_rope_bwd", grid=(S // tr,),
        out_shape=(jax.ShapeDtypeStruct((S, 1024), _MXU_DTYPE), jax.ShapeDtypeStruct((S, 1536), _MXU_DTYPE),
                   jax.ShapeDtypeStruct((S, 1536), _MXU_DTYPE), jax.ShapeDtypeStruct((1, 256), F32), jax.ShapeDtypeStruct((1, 128), F32)),
        in_specs=[pl.BlockSpec((tr, 256), lambda i: (i, 0)), pl.BlockSpec((tr, 128), lambda i: (i, 2)), _vec_spec(256), _vec_spec(128),
                  _row_spec(tr, 1024), _row_spec(tr, 1024), _row_spec(tr, 512), _row_spec(tr, 1024), whole(w_uq_t), whole(w_kv),
                  tab, tab, tab, tab],
        out_specs=(_row_spec(tr, 1024), _row_spec(tr, 1536), _row_spec(tr, 1536), _vec_spec(256), _vec_spec(128)),
        compiler_params=_cp("arbitrary"))(z, z, qg, kvg, dq, dk, dv, duv, w_uq_t, w_kv, cosq, cosk, sa, sb)


NEG = -1e30


def _attn_fwd(q, k, v, tq=512, tk=512):
    S = q.shape[0]
    assert tq == tk

    def body(q_ref, k_ref, v_ref, o_ref, lse_ref):
        i = pl.program_id(1)
        qs = [q_ref[:, 0:128], q_ref[:, 128:256]]

        def step(kb, carry, diagonal=False):
            start = pl.multiple_of(kb * tk, tk)
            vv = v_ref[pl.ds(start, tk), :]
            out = []
            for h in range(2):
                m, l, acc = carry[3 * h:3 * h + 3]
                s = _dot(qs[h], k_ref[pl.ds(start, tk), 128 * h:128 * h + 128], "nt") * ATTN_SCALE
                if diagonal:
                    s = jnp.where(below, s, NEG)
                m_new = jnp.maximum(m, jnp.max(s, axis=-1, keepdims=True))
                alpha = jnp.exp(m - m_new)
                p = jnp.exp(s - m_new)
                out += [m_new, alpha * l + jnp.sum(p, axis=-1, keepdims=True), alpha * acc + _dot(p, vv, "nn")]
            return tuple(out)

        below = lax.broadcasted_iota(jnp.int32, (tq, tk), 1) <= lax.broadcasted_iota(jnp.int32, (tq, tk), 0)
        init = (jnp.full((tq, 1), NEG, F32), jnp.zeros((tq, 1), F32), jnp.zeros((tq, 128), F32)) * 2
        ma, la, acca, mb, lb, accb = step(i, lax.fori_loop(0, i, step, init), diagonal=True)
        lane = lax.broadcasted_iota(jnp.int32, (tq, 128), 1)
        o_ref[...] = jnp.where(lane < 64, acca / la, accb / lb)
        lse_ref[...] = jnp.where(lane < 64, ma + jnp.log(la), mb + jnp.log(lb))

    return pl.pallas_call(
        body, name="attn_fwd", grid=(4, S // tq),
        out_shape=(jax.ShapeDtypeStruct((S, 512), F32), jax.ShapeDtypeStruct((4, S, 128), F32)),
        in_specs=[pl.BlockSpec((tq, 256), lambda p, i: (i, p)), pl.BlockSpec((S, 256), lambda p, i: (0, p)),
                  pl.BlockSpec((S, 128), lambda p, i: (0, p))],
        out_specs=(pl.BlockSpec((tq, 128), lambda p, i: (i, p)), pl.BlockSpec((None, tq, 128), lambda p, i: (p, i, 0))),
        compiler_params=_cp("parallel", "parallel"))(q, k, v)


def _attn_bwd(q, k, v, lse, delta, doa, dob, tq=512, tk=512):
    S = q.shape[0]
    assert tq == tk

    def body(q_ref, k_ref, v_ref, lse_ref, delta_ref, doa_ref, dob_ref, dq_ref, dk_ref, dv_ref):
        j = pl.program_id(1)

        @pl.when(j == 0)
        def _():
            dq_ref[...] = jnp.zeros_like(dq_ref)

        below = lax.broadcasted_iota(jnp.int32, (tq, tk), 1) <= lax.broadcasted_iota(jnp.int32, (tq, tk), 0)
        ks = [k_ref[:, 0:128], k_ref[:, 128:256]]
        vv = v_ref[...]

        def step(qb, carry, diagonal=False):
            dka, dkb, dvp = carry
            start = pl.multiple_of(qb * tq, tq)
            rows = pl.ds(start, tq)
            lse_v, delta_v = lse_ref[rows, :], delta_ref[rows, :]
            dos = [doa_ref[rows, :], dob_ref[rows, :]]
            dks = [dka, dkb]
            for h in range(2):
                delta = delta_v[:, 64 * h:64 * h + 1]
                do_h = dos[h]
                qh = q_ref[rows, 128 * h:128 * h + 128]
                s = _dot(qh, ks[h], "nt") * ATTN_SCALE
                p = jnp.exp(s - lse_v[:, 64 * h:64 * h + 1])
                if diagonal:
                    p = jnp.where(below, p, 0.0)
                dvp = dvp + _dot(p, do_h, "tn")
                ds = p * (_dot(do_h, vv, "nt") - delta) * ATTN_SCALE
                dq_ref[rows, 128 * h:128 * h + 128] += _dot(ds, ks[h], "nn")
                dks[h] = dks[h] + _dot(ds, qh, "tn")
            return dks[0], dks[1], dvp

        zero = jnp.zeros((tk, 128), F32)
        dka, dkb, dvp = lax.fori_loop(j + 1, S // tq, step, step(j, (zero, zero, zero), diagonal=True))
        dk_ref[:, 0:128] = dka
        dk_ref[:, 128:256] = dkb
        dv_ref[...] = dvp

    return pl.pallas_call(
        body, name="attn_bwd", grid=(4, S // tk),
        out_shape=(jax.ShapeDtypeStruct((S, 1024), F32), jax.ShapeDtypeStruct((S, 1024), F32), jax.ShapeDtypeStruct((S, 512), F32)),
        in_specs=[pl.BlockSpec((S, 256), lambda p, j: (0, p)), pl.BlockSpec((tk, 256), lambda p, j: (j, p)),
                  pl.BlockSpec((tk, 128), lambda p, j: (j, p))] + [pl.BlockSpec((None, S, 128), lambda p, j: (p, 0, 0))] * 4,
        out_specs=(pl.BlockSpec((S, 256), lambda p, j: (0, p)), pl.BlockSpec((tk, 256), lambda p, j: (j, p)),
                   pl.BlockSpec((tk, 128), lambda p, j: (j, p))),
        compiler_params=_cp("parallel", "arbitrary"))(q, k, v, lse, delta, doa, dob)


CHUNK = 128
GELU_C = math.sqrt(2.0 / math.pi)


def _gelu(v):
    t = jnp.tanh(GELU_C * (v + 0.044715 * (v * v * v)))
    return v * (0.5 * (1.0 + t)), t


def _gelu_grad(v, t):
    return 0.5 * (1.0 + t) + v * (0.5 * (1.0 - t * t) * GELU_C * (1.0 + 3.0 * 0.044715 * v * v))


def _tril(w):
    r = lax.broadcasted_iota(jnp.int32, w.shape, 0)
    c = lax.broadcasted_iota(jnp.int32, w.shape, 1)
    return jnp.where(c <= r, w, 0.0)


def _layer_norm(v, g, b):
    xc = v - jnp.mean(v, axis=-1, keepdims=True)
    rstd = lax.rsqrt(jnp.mean(xc * xc, axis=-1, keepdims=True) + EPS)
    xhat = xc * rstd
    return xhat * g + b, xhat, rstd


def _sgu_fwd(z, o, ln_g, ln_b, w_s, b_st, tr=256):
    S = z.shape[0]

    def body(u_ref, v_ref, o_ref, g_ref, b_ref, ws_ref, bs_ref, y_ref):
        gu, _ = _gelu(u_ref[...])
        gv, _ = _gelu(v_ref[...])
        vln, _, _ = _layer_norm(gv, g_ref[...], b_ref[...])
        y_ref[0] = o_ref[...].astype(y_ref.dtype)
        for g in range(4):
            wt = _tril(ws_ref[g])
            cols = slice(128 * g, 128 * g + 128)
            for ch in range(tr // CHUNK):
                rows = slice(CHUNK * ch, CHUNK * ch + CHUNK)
                mixed = _dot(wt, vln[rows, cols], "nn") + bs_ref[:, g:g + 1]
                y_ref[1, rows, cols] = (gu[rows, cols] * mixed).astype(y_ref.dtype)

    return pl.pallas_call(
        body, name="sgu_fwd", grid=(S // tr,), out_shape=jax.ShapeDtypeStruct((2, S, 512), _MXU_DTYPE),
        in_specs=[pl.BlockSpec((tr, 512), lambda i: (i, 1)), pl.BlockSpec((tr, 512), lambda i: (i, 2)), _row_spec(tr, 512),
                  _vec_spec(512), _vec_spec(512), pl.BlockSpec((4, 128, 128), lambda i: (0, 0, 0)), pl.BlockSpec((128, 4), lambda i: (0, 0))],
        out_specs=pl.BlockSpec((2, tr, 512), lambda i: (0, i, 0)), compiler_params=_cp("parallel"))(z, z, o, ln_g, ln_b, w_s, b_st)


def _sgu_bwd(z, dycat2, ln_g, ln_b, w_s, b_st, tr=256):
    S = z.shape[0]

    def body(u_ref, v_ref, dy_ref, g_ref, b_ref, ws_ref, bs_ref, duv_ref, dg_ref, db_ref, dws_ref, dbs_ref):
        first = pl.program_id(0) == 0
        u_pre, v_pre = u_ref[...], v_ref[...]
        gu, tu = _gelu(u_pre)
        gv, tv = _gelu(v_pre)
        gain = g_ref[...]
        vln, xhat, rstd = _layer_norm(gv, gain, b_ref[...])

        @pl.when(first)
        def _():
            dws_ref[...] = jnp.zeros_like(dws_ref)
            dbs_ref[...] = jnp.zeros_like(dbs_ref)

        dvln_cols = []
        for g in range(4):
            wt = _tril(ws_ref[g])
            cols = slice(128 * g, 128 * g + 128)
            dmixed_sum = jnp.zeros((CHUNK, 128), F32)
            dw = jnp.zeros((CHUNK, CHUNK), F32)
            dvln_rows = []
            for ch in range(tr // CHUNK):
                rows = slice(CHUNK * ch, CHUNK * ch + CHUNK)
                vt = vln[rows, cols]
                mixed = _dot(wt, vt, "nn") + bs_ref[:, g:g + 1]
                dyd = dy_ref[rows, cols]
                duv_ref[rows, cols] = (dyd * mixed * _gelu_grad(u_pre[rows, cols], tu[rows, cols])).astype(duv_ref.dtype)
                dmixed = dyd * gu[rows, cols]
                dmixed_sum = dmixed_sum + dmixed
                dw = dw + _dot(dmixed, vt, "nt")
                dvln_rows.append(_dot(wt, dmixed, "tn"))
            dws_ref[g] += _tril(dw)
            dbs_ref[g:g + 1, :] += jnp.sum(dmixed_sum.T, axis=0, keepdims=True)
            dvln_cols.append(jnp.concatenate(dvln_rows, axis=0))
        dvln = jnp.concatenate(dvln_cols, axis=1)
        _acc_rows(dg_ref, dvln * xhat, first)
        _acc_rows(db_ref, dvln, first)
        dxhat = dvln * gain
        dgv = rstd * (dxhat - jnp.mean(dxhat, axis=-1, keepdims=True) - xhat * jnp.mean(dxhat * xhat, axis=-1, keepdims=True))
        duv_ref[:, 512:1024] = (dgv * _gelu_grad(v_pre, tv)).astype(duv_ref.dtype)

    return pl.pallas_call(
        body, name="sgu_bwd", grid=(S // tr,),
        out_shape=(jax.ShapeDtypeStruct((S, 1024), _MXU_DTYPE), jax.ShapeDtypeStruct((1, 512), F32), jax.ShapeDtypeStruct((1, 512), F32),
                   jax.ShapeDtypeStruct((4, 128, 128), F32), jax.ShapeDtypeStruct((4, 128), F32)),
        in_specs=[pl.BlockSpec((tr, 512), lambda i: (i, 1)), pl.BlockSpec((tr, 512), lambda i: (i, 2)),
                  pl.BlockSpec((None, tr, 512), lambda i: (1, i, 0)), _vec_spec(512), _vec_spec(512),
                  pl.BlockSpec((4, 128, 128), lambda i: (0, 0, 0)), pl.BlockSpec((128, 4), lambda i: (0, 0))],
        out_specs=(_row_spec(tr, 1024), _vec_spec(512), _vec_spec(512), pl.BlockSpec((4, 128, 128), lambda i: (0, 0, 0)),
                   pl.BlockSpec((4, 128), lambda i: (0, 0))),
        compiler_params=_cp("arbitrary"))(z, z, dycat2, ln_g, ln_b, w_s, b_st)


def _sum_parts(name, parts, tr=512):
    P, R, C = parts.shape
    tr = _tile(R, tr) if R % 8 == 0 else R

    def body(p_ref, o_ref):
        g = p_ref[0]
        for k in range(1, P):
            g = g + p_ref[k]
        o_ref[...] = g

    return pl.pallas_call(
        body, name=name, grid=(R // tr,), out_shape=jax.ShapeDtypeStruct((R, C), F32),
        in_specs=[pl.BlockSpec((P, tr, C), lambda i: (0, i, 0))], out_specs=_row_spec(tr, C),
        compiler_params=_cp("parallel"))(parts)


def _adamw_math(w, m, v, g):
    c1 = 1.0 / (1.0 - ADAM_B1 ** ADAM_STEP)
    c2 = 1.0 / (1.0 - ADAM_B2 ** ADAM_STEP)
    m2 = ADAM_B1 * m + (1.0 - ADAM_B1) * g
    v2 = ADAM_B2 * v + (1.0 - ADAM_B2) * (g * g)
    return -ADAM_LR * ((m2 * c1) / (jnp.sqrt(v2 * c2) + ADAM_EPS) + ADAM_WD * w), m2, v2


def _adamw_small(name, params, parts):
    n = len(params)

    def body(*refs):
        ins, outs = refs[:4 * n], refs[4 * n:]
        for i in range(n):
            w_ref, m_ref, v_ref, p_ref = ins[4 * i:4 * i + 4]
            g = p_ref[0].astype(F32)
            for k in range(1, N_DEV):
                g = g + p_ref[k].astype(F32)
            delta, m2, v2 = _adamw_math(w_ref[...], m_ref[...], v_ref[...], g)
            outs[4 * i][...] = g
            outs[4 * i + 1][...] = delta
            outs[4 * i + 2][...] = m2
            outs[4 * i + 3][...] = v2

    flat = [a for (w, m, v), p in zip(params, parts) for a in (w, m, v, p)]
    out = pl.pallas_call(
        body, name=name, out_shape=[jax.ShapeDtypeStruct(w.shape, F32) for (w, _, _) in params for _ in range(4)],
        compiler_params=pltpu.CompilerParams(vmem_limit_bytes=_VMEM_LIMIT))(*flat)
    return [out[4 * i:4 * i + 4] for i in range(n)]


ADAMW_BLOCK_BYTES = 36 * 2 ** 20


def _adamw(name, w, m, v, parts):
    L, R, C = w.shape
    P = parts[0].shape[0]
    row_bytes = 2 * C * (7 * 4 + P * parts[0].dtype.itemsize)
    tr = R
    if R * row_bytes > ADAMW_BLOCK_BYTES:
        tr = next(t for t in (1024, 512, 256, 128, 64, 32, 16) if R % t == 0 and t * row_bytes <= ADAMW_BLOCK_BYTES)
    nr = R // tr
    c1 = 1.0 / (1.0 - ADAM_B1 ** ADAM_STEP)
    c2 = 1.0 / (1.0 - ADAM_B2 ** ADAM_STEP)

    def body(w_ref, m_ref, v_ref, *rest):
        p_refs, (g_ref, d_ref, mo_ref, vo_ref) = rest[:L], rest[L:]
        for ll in range(L):
            @pl.when(pl.program_id(0) == ll)
            def _(p_ref=p_refs[ll]):
                g = p_ref[0].astype(F32)
                for k in range(1, P):
                    g = g + p_ref[k].astype(F32)
                m2 = ADAM_B1 * m_ref[...] + (1.0 - ADAM_B1) * g
                v2 = ADAM_B2 * v_ref[...] + (1.0 - ADAM_B2) * (g * g)
                g_ref[...] = g
                mo_ref[...] = m2
                vo_ref[...] = v2
                d_ref[...] = -ADAM_LR * ((m2 * c1) / (jnp.sqrt(v2 * c2) + ADAM_EPS) + ADAM_WD * w_ref[...])

    def part_spec(ll):
        return pl.BlockSpec((P, tr, C), lambda l, i: (0, jnp.where(l == ll, i, jnp.where(l < ll, 0, nr - 1)), 0))

    full = pl.BlockSpec((None, tr, C), lambda l, i: (l, i, 0))
    sds = jax.ShapeDtypeStruct((L, R, C), F32)
    return pl.pallas_call(
        body, name=name, grid=(L, nr), out_shape=(sds, sds, sds, sds),
        in_specs=[full] * 3 + [part_spec(ll) for ll in range(L)],
        out_specs=(full,) * 4, compiler_params=_cp("arbitrary", "arbitrary"))(w, m, v, *parts)


def _rope_tables(positions):
    half = 16
    inv_freq = 10000.0 ** (-jnp.arange(half, dtype=F32) / half)
    ang = positions.astype(F32)[:, None] * inv_freq
    cos, sin = jnp.cos(ang), jnp.sin(ang)
    S = positions.shape[0]
    z16, z32, z64 = jnp.zeros((S, 16), F32), jnp.zeros((S, 32), F32), jnp.zeros((S, 64), F32)
    cosk = jnp.concatenate([z64, cos, cos, z32], axis=1)
    cosq = jnp.concatenate([jnp.ones((S, 64), F32), cos, cos, z32], axis=1)
    sa = jnp.concatenate([z64, -sin, z16, z32], axis=1)
    sb = jnp.concatenate([z64, z16, sin, z32], axis=1)
    return cosq, cosk, sa, sb


def _ffn_fwd(l, x, mod, n2g, get_w_up8, cw24, get_w_down4):
    sh, sc, gate = mod
    h = _rmsmod_fwd(f"ffn{l}_norm", x, n2g, sc, sh, n2g)
    w_up8 = get_w_up8(h)
    u8 = _mm_cols(f"ffn{l}_up", h, w_up8, out_dtype=ACT_DTYPE, tm=2048)
    S, n = u8.shape[1], u8.shape[2]
    u24 = u8.reshape(2, 4, S, n)
    a4, z24 = _ffn_gate_fwd(f"ffn{l}_gate", u24, cw24)
    w_down4 = get_w_down4(a4)
    f, x_new = _mm_rows_resid(f"ffn{l}_down", a4, w_down4, x, gate)
    return x_new, (x, h, u24, a4, f, z24), w_up8, w_down4


def _ffn_bwd(l, dx, df, dgate, saved, mod, n2g, w_up8, cw24, w_down4, me, y_prev, gate_prev):
    sh, sc, gate = mod
    x, h, u24, a4, f, z24 = saved
    da4 = _mm_rows_dx(f"ffn{l}_down_dx", df, w_down4, out_dtype=ACT_DTYPE, tm=2048)
    dw_down4 = _mm_rows_dw(f"ffn{l}_down_dw", a4, df, out_dtype=WIRE_DTYPE, tn=1024)
    sent_down, token = _exchange_start(f"scatter_ffn{l}_down", [dw_down4.reshape(8, 352, dw_down4.shape[2])], True, dgate, me)
    du24, dcw24, dh = _ffn_gate_bwd(f"ffn{l}_act_bwd", u24, z24, cw24, da4, w_up8.reshape((2, 4) + w_up8.shape[1:]), token)
    du8 = du24.reshape((8,) + du24.shape[2:])
    dw_up8t = _mm_cols_dwt(f"ffn{l}_up_dw", h, du8, out_dtype=WIRE_DTYPE, tk=1024)
    sent_up, token = _exchange_start(f"scatter_ffn{l}_up", [dw_up8t], True, dcw24, me)
    dx_new, dn2g, dsc, dsh, dy_prev, dgate_prev = _rmsmod_bwd(f"ffn{l}_norm_bwd", x, n2g, sc, dh, dx, token, y_prev, gate_prev)
    return dx_new, dict(sent_up=sent_up, sent_down=sent_down, cw24=dcw24, n2g=dn2g, mod=(dsh, dsc, dgate)), dy_prev, dgate_prev


def kernel(x, c, positions, ada_w, ada_b, norm1_g, norm2_g, ab_w_in, a_conv_w, b_mix_w, b_scale, ab_w_out, cd_w_in, c_q_norm_g, c_w_uq, c_kv_norm_g, c_w_ukv, d_ln_g, d_ln_b, d_w_s, d_b_s, cd_w_out, ffn_w_up, ffn_conv_w, ffn_w_down, final_norm_g, loss_target, m_ada_w, m_ada_b, m_norm1_g, m_norm2_g, m_ab_w_in, m_a_conv_w, m_b_mix_w, m_b_scale, m_ab_w_out, m_cd_w_in, m_c_q_norm_g, m_c_w_uq, m_c_kv_norm_g, m_c_w_ukv, m_d_ln_g, m_d_ln_b, m_d_w_s, m_d_b_s, m_cd_w_out, m_ffn_w_up, m_ffn_conv_w, m_ffn_w_down, m_final_norm_g, v_ada_w, v_ada_b, v_norm1_g, v_norm2_g, v_ab_w_in, v_a_conv_w, v_b_mix_w, v_b_scale, v_ab_w_out, v_cd_w_in, v_c_q_norm_g, v_c_w_uq, v_c_kv_norm_g, v_c_w_ukv, v_d_ln_g, v_d_ln_b, v_d_w_s, v_d_b_s, v_cd_w_out, v_ffn_w_up, v_ffn_conv_w, v_ffn_w_down, v_final_norm_g):
    S, D = x.shape[1], x.shape[2]
    me = 4 * lax.axis_index("x") + 2 * lax.axis_index("y") + lax.axis_index("c")
    x0, target = x[0], loss_target[0]
    W = _MXU_DTYPE

    small_shapes = [(1024,), (3, 64), (32,), (64,), (64,), (2, 3, 704)]
    gw_ab, token = _hier_gather_start("gather_w_ab", [ab_w_in[0].astype(W), ab_w_out[0].astype(W)], c, me)
    (g0,) = _exchange("gather_small", [[_pack([c, a_conv_w, c_q_norm_g, d_ln_g, d_ln_b, ffn_conv_w]) + 0.0 * token[:1]]],
                      scatter=False)
    c_all, aconv_s, qg_s, lng_s, lnb_s, fcw_s = _unpack(g0[:, 0], small_shapes, lead=(N_DEV,))
    conv_w = aconv_s.transpose(1, 0, 2).reshape(3, 512)
    qg, ln_g, ln_b = qg_s.reshape(1, 256), lng_s.reshape(1, 512), lnb_s.reshape(1, 512)
    cw24 = [fcw_s[:, l].reshape(2, 4, 3, 704) for l in range(2)]
    c16 = jnp.pad(c_all, ((0, 16 - N_DEV), (0, 0)))

    mod_cols = _ada_fwd(c16, ada_w)
    (g1,) = _exchange("gather_mod", [[_pack([mod_cols])]], scatter=False)
    mod_all = _unpack(g1[:, 0], [(2, 16, 768)], lead=(N_DEV,))[0]
    mod_mine = lax.dynamic_index_in_dim(mod_all, me, axis=2, keepdims=False)
    mod = mod_mine.transpose(1, 0, 2).reshape(2, 6 * D) + ada_b
    mods = [[mod[l, k * D:(k + 1) * D].reshape(1, D) for k in range(6)] for l in range(2)]

    gw_up0, token = _hier_gather_start("gather_w_ffn0_up", [ffn_w_up[0].astype(W)], mod, me)
    gw_rest, started = _exchange_start("gather_w_rest", [
        ffn_w_down[0].astype(W), cd_w_in[0].T.astype(W), c_w_uq[0].T.astype(W), c_w_ukv[0].astype(W), cd_w_out[0].astype(W),
        ffn_w_up[1].astype(W), ffn_w_down[1].astype(W)], False, token, me)

    cosq, cosk, sa, sb = _rope_tables(positions[0])
    n1g = [norm1_g[l].reshape(1, D) for l in range(2)]
    n2g = [norm2_g[l].reshape(1, D) for l in range(2)]
    mix_w, scale = b_mix_w[0], b_scale
    kvg = c_kv_norm_g
    w_s, b_st = d_w_s[0], d_b_s[0].T

    sh1, sc1, g1m = mods[0][:3]
    h_ab = _rmsmod_fwd("ab_norm", x0, n1g[0], sc1, sh1, started)
    w_abin8, w_about = _hier_gather_wait("wait_w_ab", _hier_gather_forward("forward_w_ab", gw_ab, h_ab), h_ab)
    w_about2 = w_about.reshape(2, 512, D)
    z8 = _mm_cols("ab_in", h_ab, w_abin8, out_dtype=ACT_DTYPE, tm=2048)
    ycat_ab = _ab_mix_fwd(z8, conv_w, mix_w, scale)
    y_ab, x1 = _mm_rows_resid("ab_out", ycat_ab, w_about2, x0, g1m)
    w_up8, w_down4 = [None, None], [None, None]
    gw_up0 = _hier_gather_forward("forward_w_ffn0_up", gw_up0, x1)
    x2, ffn0_saved, w_up8[0], w_down4[0] = _ffn_fwd(
        0, x1, mods[0][3:], n2g[0], lambda after: _hier_gather_wait("wait_w_ffn0_up", gw_up0, after)[0], cw24[0],
        lambda after: _exchange_wait("wait_w_ffn0_down", gw_rest, after, [0])[0].reshape(4, 704, D))

    w_cdin, w_uq, w_ukv, w_cdout = _exchange_wait("wait_w_cd", gw_rest, x2, [1, 2, 3, 4])
    w_cdout2 = w_cdout.reshape(2, 512, D)
    w_cd_t = w_cdin.reshape(1440, D)
    zr = lambda n: jnp.zeros((n, D), W)
    w_cd_pad = jnp.concatenate([w_cd_t[:384], zr(64), w_cd_t[384:416], zr(32), w_cd_t[416:]], axis=0)
    w_uq_pad = jnp.pad(w_uq, ((0, 0), (0, 32), (0, 0))).reshape(1024, 256)
    w_ukv_h = w_ukv.transpose(1, 0, 2)
    w_k_pad = jnp.pad(w_ukv_h[:, :, :64], ((0, 0), (0, 0), (0, 64))).reshape(128, 1024)
    w_kv_pad = jnp.concatenate([w_k_pad, w_ukv_h[:, :, 64:].reshape(128, 512)], axis=1)

    sh1, sc1, g1c = mods[1][:3]
    h_cd = _rmsmod_fwd("cd_norm", x2, n1g[1], sc1, sh1, n1g[1])
    z_cd = _mm_nt("cd_in", h_cd, w_cd_pad, tm=1024, tn=1536)
    qn, kvn, q_r, k_r, v_r = _qkv_rope_fwd(z_cd, qg, kvg, w_uq_pad, w_kv_pad, cosq, cosk, sa, sb)
    o, lse = _attn_fwd(q_r, k_r, v_r)
    ycat_cd = _sgu_fwd(z_cd, o, ln_g, ln_b, w_s, b_st)
    y_cd, x3 = _mm_rows_resid("cd_out", ycat_cd, w_cdout2, x2, g1c)
    x4, ffn1_saved, w_up8[1], w_down4[1] = _ffn_fwd(
        1, x3, mods[1][3:], n2g[1], lambda after: _exchange_wait("wait_w_ffn1_up", gw_rest, after, [5])[0], cw24[1],
        lambda after: _exchange_wait("wait_w_ffn1_down", gw_rest, after, [6])[0].reshape(4, 704, D))

    loss_local, dx4, dfg, df1, dgate1 = _loss_head(x4, final_norm_g.reshape(1, D), target, ffn1_saved[4], mods[1][5])

    dx3, gf1, dy, dg1c = _ffn_bwd(1, dx4, df1, dgate1, ffn1_saved, mods[1][3:], n2g[1], w_up8[1], cw24[1], w_down4[1], me, y_cd, g1c)

    dycat = _mm_rows_dx("cd_out_dx", dy, w_cdout2, tm=2048)
    dw_cdout = _mm_rows_dw("cd_out_dw", ycat_cd, dy, out_dtype=WIRE_DTYPE, tn=1024)
    duv, dln_g, dln_b, dws, dbs = _sgu_bwd(z_cd, dycat, ln_g, ln_b, w_s, b_st)
    dq_r, dk_r, dv_r = _attn_bwd(q_r, k_r, v_r, lse, *_attn_bwd_prep(o, dycat))
    dqraw, dkvall, dz_cd, dqg, dkvg = _qkv_rope_bwd(z_cd, qg, kvg, dq_r, dk_r, dv_r, duv, w_uq_pad, w_kv_pad, cosq, cosk, sa, sb)
    dw_uq_pad = _mm_tn("cd_uq_dw", dqraw, qn, tn=256)
    dw_kv_pad = _mm_tn("cd_ukv_dw", kvn, dkvall, tm=128)
    dh_cd = _mm_nn("cd_in_dx", dz_cd, w_cd_pad, out_dtype=ACT_DTYPE, tm=1024, tn=1024)
    dw_cd_pad = _mm_tn("cd_in_dw", dz_cd, h_cd, tm=768, tn=1024)
    dw_cd8 = jnp.concatenate([dw_cd_pad[:384], dw_cd_pad[448:480], dw_cd_pad[512:]], axis=0).astype(WIRE_DTYPE).reshape(8, 180, D)
    dw_uq8 = dw_uq_pad.reshape(8, 128, 256)[:, :96].astype(WIRE_DTYPE)
    dw_ukv8 = jnp.concatenate([dw_kv_pad[:, :1024].reshape(128, 8, 128)[:, :, :64], dw_kv_pad[:, 1024:].reshape(128, 8, 64)],
                              axis=2).transpose(1, 0, 2).astype(WIRE_DTYPE)
    sent_cd, token = _exchange_start("scatter_cd", [dw_cd8, dw_uq8, dw_ukv8, dw_cdout.reshape(8, 128, D)], True, dqg, me)
    early_names = ["c_kv_norm_g", "d_w_s", "d_b_s", "final_norm_g", "c_q_norm_g", "d_ln_g", "d_ln_b"]
    early_grads = [dkvg, dws.reshape(512, 128).astype(WIRE_DTYPE), dbs, dfg, dqg.reshape(8, 1, 32), dln_g.reshape(8, 1, 64),
                   dln_b.reshape(8, 1, 64)]
    early_sent, token = _exchange_start("gather_small_grads_early", early_grads, [False] * 4 + [True] * 3, token, me)
    dx2, dn1g_cd, dsc1_cd, dsh1_cd, df0, dgate0 = _rmsmod_bwd("cd_norm_bwd", x2, n1g[1], sc1, dh_cd, dx3, token,
                                                              ffn0_saved[4], mods[0][5])

    dx1, gf0, dy, dg1m = _ffn_bwd(0, dx2, df0, dgate0, ffn0_saved, mods[0][3:], n2g[0], w_up8[0], cw24[0], w_down4[0], me, y_ab, g1m)

    dw_about = _mm_rows_dw("ab_out_dw", ycat_ab, dy, out_dtype=WIRE_DTYPE, tn=1024)
    sent_about, token = _exchange_start("scatter_ab_out", [dw_about.reshape(8, 128, D)], True, dg1m, me)
    dycat = _mm_rows_dx("ab_out_dx", dy, w_about2, tm=2048)
    dz8, dconv_w, dmix_w, dscale = _ab_mix_bwd(z8, dycat, conv_w, mix_w, scale, token)
    dz8 = dz8.reshape(8, S, 256)
    dw_abin8 = _mm_cols_dw("ab_in_dw", h_ab, dz8, out_dtype=WIRE_DTYPE, tk=1024)
    sent_abin, token = _exchange_start("scatter_ab_in", [dw_abin8], True, dscale, me)
    dh_ab = _mm_cols_dx("ab_in_dx", dz8, w_abin8, out_dtype=ACT_DTYPE)
    dx0, dn1g_ab, dsc1_ab, dsh1_ab = _rmsmod_bwd("ab_norm_bwd", x0, n1g[0], mods[0][1], dh_ab, dx1, token)

    dmod = jnp.stack([jnp.concatenate([dsh1_ab, dsc1_ab, dg1m, *gf0["mod"]], axis=1)[0],
                      jnp.concatenate([dsh1_cd, dsc1_cd, dg1c, *gf1["mod"]], axis=1)[0]])
    late_names = ["ada_b", "norm1_g", "norm2_g", "b_mix_w", "b_scale", "a_conv_w", "ffn_conv_w"]
    late_grads = [dmod, jnp.concatenate([dn1g_ab, dn1g_cd]), jnp.concatenate([gf0["n2g"], gf1["n2g"]]),
                  dmix_w.reshape(512, 128).astype(WIRE_DTYPE), dscale, dconv_w.reshape(3, 8, 64).transpose(1, 0, 2),
                  jnp.stack([gf0["cw24"].reshape(8, 3, 704), gf1["cw24"].reshape(8, 3, 704)], axis=1),
                  jnp.pad(loss_local, ((0, 0), (0, 127)))]
    small_view = dict(ada_b=(2, 6 * D), norm1_g=(2, D), norm2_g=(2, D), b_mix_w=(512, 128), b_scale=(1, 512), c_kv_norm_g=(1, 128),
                      d_w_s=(512, 128), d_b_s=(4, 128), final_norm_g=(1, D),
                      a_conv_w=(3, 64), c_q_norm_g=(1, 32), d_ln_g=(1, 64), d_ln_b=(1, 64), ffn_conv_w=(2, 3, 704))
    late_sent, token = _exchange_start("gather_small_grads_late", late_grads, [False] * 5 + [True] * 2 + [False], dx0, me)

    res = {}

    def update(name, w, m, v, parts, shape3d):
        outs = _adamw("adamw_" + name, w.reshape(shape3d), m.reshape(shape3d), v.reshape(shape3d),
                      [p.reshape((p.shape[0],) + shape3d[1:]) for p in parts])
        res[name] = [o_.reshape(w.shape) for o_ in outs]

    p_cdin, p_uq, p_ukv, p_cdout = _exchange_wait("wait_scatter_cd", sent_cd, token)
    swap = lambda a: jnp.swapaxes(a, 1, 2)
    update("cd_w_in", swap(cd_w_in), swap(m_cd_w_in), swap(v_cd_w_in), [p_cdin], (1, 180, D))
    update("c_w_uq", swap(c_w_uq), swap(m_c_w_uq), swap(v_c_w_uq), [p_uq], (1, 96, 256))
    for name in ("cd_w_in", "c_w_uq"):
        res[name] = [swap(o_) for o_ in res[name]]
    update("c_w_ukv", c_w_ukv, m_c_w_ukv, v_c_w_ukv, [p_ukv], (1, 128, 128))
    update("cd_w_out", cd_w_out, m_cd_w_out, v_cd_w_out, [p_cdout], (1, 128, D))
    (p_dn1,) = _exchange_wait("wait_scatter_ffn1_down", gf1["sent_down"], token)
    (p_dn0,) = _exchange_wait("wait_scatter_ffn0_down", gf0["sent_down"], res["cd_w_out"][0])
    update("ffn_w_down", ffn_w_down, m_ffn_w_down, v_ffn_w_down, [p_dn0, p_dn1], (2, 352, D))
    (p_up1,) = _exchange_wait("wait_scatter_ffn1_up", gf1["sent_up"], token)
    (p_up0,) = _exchange_wait("wait_scatter_ffn0_up", gf0["sent_up"], res["ffn_w_down"][0])
    swap = lambda a: jnp.swapaxes(a, 1, 2)
    update("ffn_w_up", swap(ffn_w_up), swap(m_ffn_w_up), swap(v_ffn_w_up), [p_up0, p_up1], (2, 704, D))
    up_done = res["ffn_w_up"][0]
    res["ffn_w_up"] = [swap(o_) for o_ in res["ffn_w_up"]]
    (p_about,) = _exchange_wait("wait_scatter_ab_out", sent_about, up_done)
    update("ab_w_out", ab_w_out, m_ab_w_out, v_ab_w_out, [p_about], (1, 128, D))
    (p_abin,) = _exchange_wait("wait_scatter_ab_in", sent_abin, res["ab_w_out"][0])
    update("ab_w_in", ab_w_in, m_ab_w_in, v_ab_w_in, [p_abin], (1, D, 256))

    early_parts = _exchange_wait("wait_small_grads_early", early_sent, res["ab_w_in"][0])
    late_parts = _exchange_wait("wait_small_grads_late", late_sent, res["ab_w_in"][0])
    small_names = early_names + late_names
    small_parts = list(early_parts) + list(late_parts[:7])
    loss = jnp.sum(late_parts[7][:, 0, 0])
    dmod_all = late_parts[0]
    dmod_cols = lax.dynamic_slice_in_dim(dmod_all, me * 768, 768, axis=2).transpose(1, 0, 2)
    g_ada_w = _ada_bwd(c16, jnp.pad(dmod_cols, ((0, 0), (0, 16 - N_DEV), (0, 0))))
    update("ada_w", ada_w, m_ada_w, v_ada_w, [g_ada_w[None]], (1, 2 * D, 768))

    small_w = dict(ada_b=(ada_b, m_ada_b, v_ada_b), norm1_g=(norm1_g, m_norm1_g, v_norm1_g), norm2_g=(norm2_g, m_norm2_g, v_norm2_g),
                   b_mix_w=(b_mix_w, m_b_mix_w, v_b_mix_w), b_scale=(b_scale, m_b_scale, v_b_scale),
                   c_kv_norm_g=(c_kv_norm_g, m_c_kv_norm_g, v_c_kv_norm_g), d_w_s=(d_w_s, m_d_w_s, v_d_w_s),
                   d_b_s=(d_b_s, m_d_b_s, v_d_b_s), final_norm_g=(final_norm_g, m_final_norm_g, v_final_norm_g),
                   a_conv_w=(a_conv_w, m_a_conv_w, v_a_conv_w), c_q_norm_g=(c_q_norm_g, m_c_q_norm_g, v_c_q_norm_g),
                   d_ln_g=(d_ln_g, m_d_ln_g, v_d_ln_g), d_ln_b=(d_ln_b, m_d_ln_b, v_d_ln_b),
                   ffn_conv_w=(ffn_conv_w, m_ffn_conv_w, v_ffn_conv_w))
    small_out = _adamw_small("adamw_small", [tuple(a.reshape(small_view[n]) for a in small_w[n]) for n in small_names],
                             list(small_parts))
    for n, outs in zip(small_names, small_out):
        res[n] = [o_.reshape(small_w[n][0].shape) for o_ in outs]

    order = ["ada_w", "ada_b", "norm1_g", "norm2_g", "ab_w_in", "a_conv_w", "b_mix_w", "b_scale", "ab_w_out", "cd_w_in", "c_q_norm_g",
             "c_w_uq", "c_kv_norm_g", "c_w_ukv", "d_ln_g", "d_ln_b", "d_w_s", "d_b_s", "cd_w_out", "ffn_w_up", "ffn_conv_w",
             "ffn_w_down", "final_norm_g"]
    return (loss, dx0[None], *[res[n][0] for n in order], *[res[n][1] for n in order], *[res[n][2] for n in order],
            *[res[n][3] for n in order])
```

```python
import functools
import math

import jax
import jax.numpy as jnp
from jax import lax
from jax.experimental import pallas as pl
from jax.experimental.pallas import tpu as pltpu

F32 = jnp.float32
BF16 = jnp.bfloat16
_MXU_DTYPE = BF16
WIRE_DTYPE = BF16
ACT_DTYPE = BF16
_VMEM_LIMIT = 56 * 2 ** 20
N_DEV = 8
EPS = 1e-6
POOL_WINDOWS = (2, 4, 8, 16)
ATTN_SCALE = (64 + 32) ** -0.5
ADAM_LR, ADAM_B1, ADAM_B2, ADAM_EPS, ADAM_WD, ADAM_STEP = 0.001, 0.9, 0.999, 1e-08, 0.01, 10
MESH = pl.DeviceIdType.MESH
ANY = pl.BlockSpec(memory_space=pl.ANY)


def _cp(*sem):
    return pltpu.CompilerParams(dimension_semantics=sem, vmem_limit_bytes=_VMEM_LIMIT)


def _dot(a, b, contract):
    dn = {"nn": (((1,), (0,)), ((), ())), "nt": (((1,), (1,)), ((), ())), "tn": (((0,), (0,)), ((), ()))}[contract]
    return lax.dot_general(a.astype(_MXU_DTYPE), b.astype(_MXU_DTYPE), dn, preferred_element_type=F32)


def _my_position():
    x, y, c = lax.axis_index("x"), lax.axis_index("y"), lax.axis_index("c")
    return x, y, c, 4 * x + 2 * y + c


def _exchange(name, groups, scatter):
    flat = [a for g in groups for a in g]
    n_in, n_grp = len(flat), len(groups)
    out_shapes = []
    for g in groups:
        slab = g[0].shape[1:] if scatter else g[0].shape
        out_shapes.append(jax.ShapeDtypeStruct((N_DEV, len(g)) + tuple(slab), g[0].dtype))

    def body(*refs):
        ins, outs = refs[:n_in], refs[n_in:n_in + n_grp]
        send_sems, recv_sems, local_sems = refs[n_in + n_grp:]
        x, y, c, me = _my_position()
        i = 0
        for gi, g in enumerate(groups):
            for l in range(len(g)):
                src = ins[i]
                i += 1
                pltpu.make_async_copy(src.at[me] if scatter else src, outs[gi].at[me, l], local_sems.at[gi]).start()
                for k in range(1, N_DEV):
                    px = 1 - x if k & 4 else x
                    py = 1 - y if k & 2 else y
                    pc = 1 - c if k & 1 else c
                    peer = 4 * px + 2 * py + pc
                    pltpu.make_async_remote_copy(
                        src_ref=src.at[peer] if scatter else src, dst_ref=outs[gi].at[me, l],
                        send_sem=send_sems.at[gi], recv_sem=recv_sems.at[gi],
                        device_id=(px, py, pc), device_id_type=MESH).start()
        for gi in range(n_grp):
            mine = outs[gi].at[me]
            pltpu.make_async_copy(mine, mine, local_sems.at[gi]).wait()
            seven = outs[gi].at[pl.ds(0, N_DEV - 1)]
            w = pltpu.make_async_remote_copy(src_ref=seven, dst_ref=seven, send_sem=send_sems.at[gi],
                                             recv_sem=recv_sems.at[gi], device_id=(x, y, c), device_id_type=MESH)
            w.wait_send()
            w.wait_recv()

    return pl.pallas_call(
        body, name=name, out_shape=tuple(out_shapes),
        in_specs=[ANY] * n_in, out_specs=tuple([ANY] * n_grp),
        scratch_shapes=[pltpu.SemaphoreType.DMA((n_grp,)), pltpu.SemaphoreType.DMA((n_grp,)),
                        pltpu.SemaphoreType.DMA((n_grp,))],
        compiler_params=pltpu.CompilerParams(has_side_effects=True),
    )(*flat)


HBM_SPEC = pl.BlockSpec(memory_space=pltpu.HBM)
SEM_SPEC = pl.BlockSpec(memory_space=pltpu.SEMAPHORE)
EFFECT = pltpu.SideEffectType.DATAFLOW_SIDE_EFFECTING


def _put_mine(name, srcs, scatter, me):
    n = len(srcs)
    slabs = [tuple(s.shape[1:] if sc else s.shape) for s, sc in zip(srcs, scatter)]

    def body(me_ref, *refs):
        for i in range(n):
            refs[n + i][...] = refs[i][...]

    def at_me(slab):
        return pl.BlockSpec((None,) + slab, lambda g, me_ref, nd=len(slab): (me_ref[0],) + (0,) * nd)

    def whole(slab):
        return pl.BlockSpec(slab, lambda g, me_ref, nd=len(slab): (0,) * nd)

    return pl.pallas_call(
        body, name=name,
        grid_spec=pltpu.PrefetchScalarGridSpec(
            num_scalar_prefetch=1, grid=(1,),
            in_specs=[at_me(slab) if sc else whole(slab) for slab, sc in zip(slabs, scatter)],
            out_specs=[at_me(slab) for slab in slabs]),
        out_shape=[jax.ShapeDtypeStruct((N_DEV,) + slab, s.dtype) for slab, s in zip(slabs, srcs)],
        compiler_params=_cp("arbitrary"))(me.reshape(1), *srcs)


def _exchange_start(name, srcs, scatter, after, me):
    n = len(srcs)
    scatter = list(scatter) if isinstance(scatter, (list, tuple)) else [scatter] * n
    lands = _put_mine(name + "_mine", srcs, scatter, me)
    srcs = [pltpu.with_memory_space_constraint(a, pltpu.HBM) for a in srcs]
    lands = [pltpu.with_memory_space_constraint(a, pltpu.HBM) for a in lands]

    def body(*refs):
        ins, land = refs[:n], refs[n:2 * n]
        send_sems, recv_sems, token = refs[2 * n + 1], refs[2 * n + 2], refs[-1]
        x, y, c, me_in = _my_position()
        for i in range(n):
            for k in range(1, N_DEV):
                px = 1 - x if k & 4 else x
                py = 1 - y if k & 2 else y
                pc = 1 - c if k & 1 else c
                pltpu.make_async_remote_copy(
                    src_ref=ins[i].at[4 * px + 2 * py + pc] if scatter[i] else ins[i], dst_ref=land[i].at[me_in],
                    send_sem=send_sems.at[i], recv_sem=recv_sems.at[i],
                    device_id=(px, py, pc), device_id_type=MESH).start()
        token[...] = jnp.zeros_like(token)

    outs = pl.pallas_call(
        body, name=name,
        out_shape=(pltpu.SemaphoreType.DMA((n,)), pltpu.SemaphoreType.DMA((n,)),
                   *[pltpu.HBM(a.shape, a.dtype) for a in srcs], *[pltpu.HBM(a.shape, a.dtype) for a in lands],
                   jax.ShapeDtypeStruct((8, 128), F32)),
        in_specs=[HBM_SPEC] * (2 * n) + [ANY],
        out_specs=(SEM_SPEC, SEM_SPEC, *[HBM_SPEC] * (2 * n), pl.BlockSpec(memory_space=pltpu.VMEM)),
        input_output_aliases={i: 2 + i for i in range(2 * n)},
        compiler_params=pltpu.CompilerParams(has_side_effects=EFFECT),
    )(*srcs, *lands, after)
    return (outs[0], outs[1], outs[2:2 + n], outs[2 + n:2 + 2 * n]), outs[-1]


def _exchange_wait(name, handle, after, which=None):
    send_sems, recv_sems, srcs, lands = handle
    which = list(range(len(srcs))) if which is None else list(which)
    srcs, lands = [srcs[i] for i in which], [lands[i] for i in which]
    n = len(srcs)

    def body(*refs):
        land, send_ref, recv_ref = refs[n:2 * n], refs[2 * n], refs[2 * n + 1]
        x, y, c, _ = _my_position()
        for k, i in enumerate(which):
            seven = land[k].at[pl.ds(0, N_DEV - 1)]
            w = pltpu.make_async_remote_copy(src_ref=seven, dst_ref=seven, send_sem=send_ref.at[i], recv_sem=recv_ref.at[i],
                                             device_id=(x, y, c), device_id_type=MESH)
            w.wait_send()
            w.wait_recv()

    outs = pl.pallas_call(
        body, name=name,
        out_shape=(*[pltpu.HBM(a.shape, a.dtype) for a in srcs], *[pltpu.HBM(a.shape, a.dtype) for a in lands]),
        in_specs=[HBM_SPEC] * (2 * n) + [SEM_SPEC, SEM_SPEC, ANY],
        out_specs=tuple([HBM_SPEC] * (2 * n)),
        input_output_aliases={i: i for i in range(2 * n)},
        compiler_params=pltpu.CompilerParams(has_side_effects=EFFECT),
    )(*srcs, *lands, send_sems, recv_sems, after)
    return outs[n:]


def _other_chips(x, y):
    return [(1 - x, y), (x, 1 - y), (1 - x, 1 - y)]


def _hier_gather_start(name, srcs, after, me):
    n = len(srcs)
    lands = _put_mine(name + "_mine", srcs, [False] * n, me)
    srcs = [pltpu.with_memory_space_constraint(a, pltpu.HBM) for a in srcs]
    lands = [pltpu.with_memory_space_constraint(a, pltpu.HBM) for a in lands]

    def body(*refs):
        ins, land = refs[:n], refs[n:2 * n]
        ici_send, ici_recv, d2d_send, d2d_recv = refs[2 * n + 1:2 * n + 5]
        token = refs[-1]
        x, y, c, me_in = _my_position()
        for i in range(n):
            pltpu.make_async_remote_copy(src_ref=ins[i], dst_ref=land[i].at[me_in], send_sem=d2d_send.at[i], recv_sem=d2d_recv.at[i],
                                         device_id=(x, y, 1 - c), device_id_type=MESH).start()
            for px, py in _other_chips(x, y):
                pltpu.make_async_remote_copy(src_ref=ins[i], dst_ref=land[i].at[me_in], send_sem=ici_send.at[i],
                                             recv_sem=ici_recv.at[i], device_id=(px, py, c), device_id_type=MESH).start()
        token[...] = jnp.zeros_like(token)

    sem = pltpu.SemaphoreType.DMA((n,))
    outs = pl.pallas_call(
        body, name=name,
        out_shape=(sem, sem, sem, sem, *[pltpu.HBM(a.shape, a.dtype) for a in srcs], *[pltpu.HBM(a.shape, a.dtype) for a in lands],
                   jax.ShapeDtypeStruct((8, 128), F32)),
        in_specs=[HBM_SPEC] * (2 * n) + [ANY],
        out_specs=(SEM_SPEC,) * 4 + (HBM_SPEC,) * (2 * n) + (pl.BlockSpec(memory_space=pltpu.VMEM),),
        input_output_aliases={i: 4 + i for i in range(2 * n)},
        compiler_params=pltpu.CompilerParams(has_side_effects=EFFECT),
    )(*srcs, *lands, after)
    return (outs[:4], outs[4:4 + n], outs[4 + n:4 + 2 * n]), outs[-1]


def _hier_gather_forward(name, handle, after):
    sems, srcs, lands = handle
    n = len(srcs)

    def body(*refs):
        land = refs[n:2 * n]
        ici_send, ici_recv, d2d_send, d2d_recv = refs[2 * n:2 * n + 4]
        x, y, c, _ = _my_position()
        for i in range(n):
            three = land[i].at[pl.ds(0, 3)]
            pltpu.make_async_remote_copy(src_ref=three, dst_ref=three, send_sem=ici_send.at[i], recv_sem=ici_recv.at[i],
                                         device_id=(x, y, c), device_id_type=MESH).wait_recv()
            for px, py in _other_chips(x, y):
                slab = land[i].at[4 * px + 2 * py + c]
                pltpu.make_async_remote_copy(src_ref=slab, dst_ref=slab, send_sem=d2d_send.at[i], recv_sem=d2d_recv.at[i],
                                             device_id=(x, y, 1 - c), device_id_type=MESH).start()

    outs = pl.pallas_call(
        body, name=name,
        out_shape=(*[pltpu.HBM(a.shape, a.dtype) for a in srcs], *[pltpu.HBM(a.shape, a.dtype) for a in lands]),
        in_specs=[HBM_SPEC] * (2 * n) + [SEM_SPEC] * 4 + [ANY],
        out_specs=tuple([HBM_SPEC] * (2 * n)),
        input_output_aliases={i: i for i in range(2 * n)},
        compiler_params=pltpu.CompilerParams(has_side_effects=EFFECT),
    )(*srcs, *lands, *sems, after)
    return (sems, outs[:n], outs[n:])


def _hier_gather_wait(name, handle, after):
    sems, srcs, lands = handle
    n = len(srcs)

    def body(*refs):
        land = refs[n:2 * n]
        ici_send, ici_recv, d2d_send, d2d_recv = refs[2 * n:2 * n + 4]
        x, y, c, _ = _my_position()
        for i in range(n):
            three, four = land[i].at[pl.ds(0, 3)], land[i].at[pl.ds(0, 4)]
            pltpu.make_async_remote_copy(src_ref=three, dst_ref=three, send_sem=ici_send.at[i], recv_sem=ici_recv.at[i],
                                         device_id=(x, y, c), device_id_type=MESH).wait_send()
            w = pltpu.make_async_remote_copy(src_ref=four, dst_ref=four, send_sem=d2d_send.at[i], recv_sem=d2d_recv.at[i],
                                             device_id=(x, y, c), device_id_type=MESH)
            w.wait_send()
            w.wait_recv()

    outs = pl.pallas_call(
        body, name=name,
        out_shape=(*[pltpu.HBM(a.shape, a.dtype) for a in srcs], *[pltpu.HBM(a.shape, a.dtype) for a in lands]),
        in_specs=[HBM_SPEC] * (2 * n) + [SEM_SPEC] * 4 + [ANY],
        out_specs=tuple([HBM_SPEC] * (2 * n)),
        input_output_aliases={i: i for i in range(2 * n)},
        compiler_params=pltpu.CompilerParams(has_side_effects=EFFECT),
    )(*srcs, *lands, *sems, after)
    return outs[n:]


def _pack(arrs):
    flat = jnp.concatenate([a.reshape(-1).astype(F32) for a in arrs])
    n = flat.shape[0]
    rows = -(-n // 1024) * 8
    return jnp.pad(flat, (0, rows * 128 - n)).reshape(rows, 128)


def _unpack(buf, shapes, lead=()):
    flat = buf.reshape(lead + (-1,))
    out, off = [], 0
    for s in shapes:
        n = math.prod(s)
        out.append(flat[..., off:off + n].reshape(lead + tuple(s)))
        off += n
    return out


def _mm(name, a, a_spec, b, b_spec, out_sds, o_spec, grid, contract, nk=1, stacked=0):
    o_blk = tuple(d for d in o_spec.block_shape if d is not None)

    def body(a_ref, b_ref, o_ref, *acc):
        if stacked:
            r = _dot(a_ref[0], b_ref[0], contract)
            for q in range(1, stacked):
                r = r + _dot(a_ref[q], b_ref[q], contract)
        else:
            r = _dot(a_ref[...], b_ref[...], contract)
        if nk == 1:
            o_ref[...] = r.astype(o_ref.dtype)
        else:
            k = pl.program_id(len(grid) - 1)

            @pl.when(k == 0)
            def _():
                acc[0][...] = r

            @pl.when(k > 0)
            def _():
                acc[0][...] += r

            @pl.when(k == nk - 1)
            def _():
                o_ref[...] = acc[0][...].astype(o_ref.dtype)

    sem = ("parallel",) * (len(grid) - 1) + (("arbitrary",) if nk > 1 else ("parallel",))
    return pl.pallas_call(
        body, name=name, out_shape=out_sds, grid=grid, in_specs=[a_spec, b_spec], out_specs=o_spec,
        scratch_shapes=[pltpu.VMEM(o_blk, F32)] if nk > 1 else [], compiler_params=_cp(*sem))(a, b)


def _tile(n, want):
    t = min(n, want)
    assert n % t == 0, (n, t)
    return t


def _mm_nn(name, a, b, out_dtype=F32, tm=512, tn=512):
    (M, K), N = a.shape, b.shape[1]
    tm, tn = _tile(M, tm), _tile(N, tn)
    return _mm(name, a, pl.BlockSpec((tm, K), lambda i, j: (i, 0)), b, pl.BlockSpec((K, tn), lambda i, j: (0, j)),
               jax.ShapeDtypeStruct((M, N), out_dtype), pl.BlockSpec((tm, tn), lambda i, j: (i, j)),
               (M // tm, N // tn), "nn")


def _mm_nt(name, a, b, out_dtype=F32, tm=512, tn=512):
    (M, K), N = a.shape, b.shape[0]
    tm, tn = _tile(M, tm), _tile(N, tn)
    return _mm(name, a, pl.BlockSpec((tm, K), lambda i, j: (i, 0)), b, pl.BlockSpec((tn, K), lambda i, j: (j, 0)),
               jax.ShapeDtypeStruct((M, N), out_dtype), pl.BlockSpec((tm, tn), lambda i, j: (i, j)),
               (M // tm, N // tn), "nt")


def _mm_tn(name, a, b, out_dtype=F32, tm=512, tn=512):
    (K, M), N = a.shape, b.shape[1]
    tm, tn = _tile(M, tm), _tile(N, tn)
    return _mm(name, a, pl.BlockSpec((K, tm), lambda i, j: (0, i)), b, pl.BlockSpec((K, tn), lambda i, j: (0, j)),
               jax.ShapeDtypeStruct((M, N), out_dtype), pl.BlockSpec((tm, tn), lambda i, j: (i, j)),
               (M // tm, N // tn), "tn")


def _mm_cols(name, a, w, out_dtype=F32, tm=512):
    (M, K), (J, _, n) = a.shape, w.shape
    tm = _tile(M, tm)
    return _mm(name, a, pl.BlockSpec((tm, K), lambda j, i: (i, 0)), w, pl.BlockSpec((None, K, n), lambda j, i: (j, 0, 0)),
               jax.ShapeDtypeStruct((J, M, n), out_dtype), pl.BlockSpec((None, tm, n), lambda j, i: (j, i, 0)),
               (J, M // tm), "nn")


def _mm_cols_dx(name, d, w, out_dtype=F32, tm=512, jb=None):
    (J, M, n), K = d.shape, w.shape[1]
    tm, jb = _tile(M, tm), J if jb is None else jb
    return _mm(name, d, pl.BlockSpec((jb, tm, n), lambda i, j: (j, i, 0)), w, pl.BlockSpec((jb, K, n), lambda i, j: (j, 0, 0)),
               jax.ShapeDtypeStruct((M, K), out_dtype), pl.BlockSpec((tm, K), lambda i, j: (i, 0)),
               (M // tm, J // jb), "nt", nk=J // jb, stacked=jb)


def _mm_cols_dw(name, a, d, out_dtype=F32, tk=512):
    (M, K), (J, _, n) = a.shape, d.shape
    tk = _tile(K, tk)
    return _mm(name, a, pl.BlockSpec((M, tk), lambda j, i: (0, i)), d, pl.BlockSpec((None, M, n), lambda j, i: (j, 0, 0)),
               jax.ShapeDtypeStruct((J, K, n), out_dtype), pl.BlockSpec((None, tk, n), lambda j, i: (j, i, 0)),
               (J, K // tk), "tn")


def _mm_cols_dwt(name, a, d, out_dtype=F32, tk=512):
    (M, K), (J, _, n) = a.shape, d.shape
    tk = _tile(K, tk)
    return _mm(name, d, pl.BlockSpec((None, M, n), lambda j, i: (j, 0, 0)), a, pl.BlockSpec((M, tk), lambda j, i: (0, i)),
               jax.ShapeDtypeStruct((J, n, K), out_dtype), pl.BlockSpec((None, n, tk), lambda j, i: (j, 0, i)),
               (J, K // tk), "tn")


def _mm_rows_resid(name, a, w, resid, gate, tm=512):
    (Q, M, k), N = a.shape, w.shape[2]
    tm = _tile(M, tm)

    def body(a_ref, w_ref, r_ref, g_ref, y_ref, x_ref):
        y = _dot(a_ref[0], w_ref[0], "nn")
        for q in range(1, Q):
            y = y + _dot(a_ref[q], w_ref[q], "nn")
        y_ref[...] = y.astype(y_ref.dtype)
        x_ref[...] = r_ref[...] + g_ref[...] * y

    return pl.pallas_call(
        body, name=name, grid=(M // tm,),
        out_shape=(jax.ShapeDtypeStruct((M, N), ACT_DTYPE), jax.ShapeDtypeStruct((M, N), F32)),
        in_specs=[pl.BlockSpec((Q, tm, k), lambda i: (0, i, 0)), pl.BlockSpec((Q, k, N), lambda i: (0, 0, 0)),
                  pl.BlockSpec((tm, N), lambda i: (i, 0)), pl.BlockSpec((1, N), lambda i: (0, 0))],
        out_specs=(pl.BlockSpec((tm, N), lambda i: (i, 0)), pl.BlockSpec((tm, N), lambda i: (i, 0))),
        compiler_params=_cp("parallel"))(a, w, resid, gate)


def _mm_rows_dx(name, d, w, out_dtype=F32, tm=512):
    (M, N), (Q, k, _) = d.shape, w.shape
    tm = _tile(M, tm)
    return _mm(name, d, pl.BlockSpec((tm, N), lambda q, i: (i, 0)), w, pl.BlockSpec((None, k, N), lambda q, i: (q, 0, 0)),
               jax.ShapeDtypeStruct((Q, M, k), out_dtype), pl.BlockSpec((None, tm, k), lambda q, i: (q, i, 0)),
               (Q, M // tm), "nt")


def _mm_rows_dw(name, a, d, out_dtype=F32, tn=512):
    (Q, M, k), N = a.shape, d.shape[1]
    tn = _tile(N, tn)
    return _mm(name, a, pl.BlockSpec((None, M, k), lambda q, j: (q, 0, 0)), d, pl.BlockSpec((M, tn), lambda q, j: (0, j)),
               jax.ShapeDtypeStruct((Q, k, N), out_dtype), pl.BlockSpec((None, k, tn), lambda q, j: (q, 0, j)),
               (Q, N // tn), "tn")


def _silu(v):
    return v * jax.nn.sigmoid(v)


def _ada_fwd(c16, ada_w):
    L, D, n = ada_w.shape

    def body(c_ref, w_ref, o_ref):
        o_ref[...] = _dot(_silu(c_ref[...]), w_ref[...], "nn")

    return pl.pallas_call(
        body, name="ada_fwd", grid=(L,), out_shape=jax.ShapeDtypeStruct((L, 16, n), F32),
        in_specs=[pl.BlockSpec((16, D), lambda l: (0, 0)), pl.BlockSpec((None, D, n), lambda l: (l, 0, 0))],
        out_specs=pl.BlockSpec((None, 16, n), lambda l: (l, 0, 0)), compiler_params=_cp("parallel"))(c16, ada_w)


def _ada_bwd(c16, dmod16):
    L, _, n = dmod16.shape
    D = c16.shape[1]

    def body(c_ref, d_ref, o_ref):
        o_ref[...] = _dot(_silu(c_ref[...]), d_ref[...], "tn")

    return pl.pallas_call(
        body, name="ada_bwd", grid=(L,), out_shape=jax.ShapeDtypeStruct((L, D, n), F32),
        in_specs=[pl.BlockSpec((16, D), lambda l: (0, 0)), pl.BlockSpec((None, 16, n), lambda l: (l, 0, 0))],
        out_specs=pl.BlockSpec((None, D, n), lambda l: (l, 0, 0)), compiler_params=_cp("parallel"))(c16, dmod16)


def _row_spec(tr, n):
    return pl.BlockSpec((tr, n), lambda i: (i, 0))


def _vec_spec(n):
    return pl.BlockSpec((1, n), lambda i: (0, 0))


def _rmsmod_fwd(name, x, g, sc, sh, after, tr=512):
    S, D = x.shape

    def body(x_ref, g_ref, sc_ref, sh_ref, after_ref, h_ref):
        xv = x_ref[...]
        rstd = lax.rsqrt(jnp.mean(xv * xv, axis=-1, keepdims=True) + EPS)
        y = xv * rstd * g_ref[...]
        h_ref[...] = (y * (1.0 + sc_ref[...]) + sh_ref[...]).astype(h_ref.dtype)

    return pl.pallas_call(
        body, name=name, grid=(S // tr,), out_shape=jax.ShapeDtypeStruct((S, D), _MXU_DTYPE),
        in_specs=[_row_spec(tr, D), _vec_spec(D), _vec_spec(D), _vec_spec(D), ANY], out_specs=_row_spec(tr, D),
        compiler_params=_cp("parallel"))(x, g, sc, sh, after)


def _acc_rows(ref, val, first):
    s = jnp.sum(val, axis=0, keepdims=True)

    @pl.when(first)
    def _():
        ref[...] = s

    @pl.when(jnp.logical_not(first))
    def _():
        ref[...] += s


def _gate_bwd_tail(dx, y_ref, gate_ref, dy_ref, dgate_ref, first):
    dy_ref[...] = (gate_ref[...] * dx).astype(dy_ref.dtype)
    _acc_rows(dgate_ref, dx * y_ref[...].astype(F32), first)


def _rmsmod_bwd(name, x, g, sc, dh, dres, after, y=None, gate=None, tr=512):
    S, D = x.shape
    tail = y is not None

    def body(x_ref, g_ref, sc_ref, dh_ref, dres_ref, after_ref, *rest):
        (y_ref, gate_ref), rest = (rest[:2], rest[2:]) if tail else ((None, None), rest)
        dx_ref, dg_ref, dsc_ref, dsh_ref = rest[:4]
        first = pl.program_id(0) == 0
        xv, dh_v, gv = x_ref[...], dh_ref[...].astype(F32), g_ref[...]
        rstd = lax.rsqrt(jnp.mean(xv * xv, axis=-1, keepdims=True) + EPS)
        xhat = xv * rstd
        _acc_rows(dsh_ref, dh_v, first)
        _acc_rows(dsc_ref, dh_v * (xhat * gv), first)
        dyg = dh_v * (1.0 + sc_ref[...])
        _acc_rows(dg_ref, dyg * xhat, first)
        dxhat = dyg * gv
        dx = dres_ref[...] + rstd * (dxhat - xhat * jnp.mean(dxhat * xhat, axis=-1, keepdims=True))
        dx_ref[...] = dx
        if tail:
            _gate_bwd_tail(dx, y_ref, gate_ref, rest[4], rest[5], first)

    vec = jax.ShapeDtypeStruct((1, D), F32)
    return pl.pallas_call(
        body, name=name, grid=(S // tr,),
        out_shape=(jax.ShapeDtypeStruct((S, D), F32), vec, vec, vec) + ((jax.ShapeDtypeStruct((S, D), _MXU_DTYPE), vec) if tail else ()),
        in_specs=[_row_spec(tr, D), _vec_spec(D), _vec_spec(D), _row_spec(tr, D), _row_spec(tr, D), ANY]
        + ([_row_spec(tr, D), _vec_spec(D)] if tail else []),
        out_specs=(_row_spec(tr, D), _vec_spec(D), _vec_spec(D), _vec_spec(D)) + ((_row_spec(tr, D), _vec_spec(D)) if tail else ()),
        compiler_params=_cp("arbitrary"))(x, g, sc, dh, dres, after, *((y, gate) if tail else ()))


def _loss_head(x, g, target, y, gate, tr=512):
    S, D = x.shape

    def body(x_ref, g_ref, t_ref, y_ref, gate_ref, loss_ref, dx_ref, dg_ref, dy_ref, dgate_ref):
        first = pl.program_id(0) == 0
        xv, gv = x_ref[...], g_ref[...]
        rstd = lax.rsqrt(jnp.mean(xv * xv, axis=-1, keepdims=True) + EPS)
        xhat = xv * rstd
        err = xhat * gv - t_ref[...]
        part = 0.5 * jnp.sum(jnp.mean(err * err, axis=-1, keepdims=True), axis=0, keepdims=True)

        @pl.when(first)
        def _():
            loss_ref[...] = part

        @pl.when(jnp.logical_not(first))
        def _():
            loss_ref[...] += part

        dout = err * (1.0 / D)
        _acc_rows(dg_ref, dout * xhat, first)
        dxhat = dout * gv
        dx = rstd * (dxhat - xhat * jnp.mean(dxhat * xhat, axis=-1, keepdims=True))
        dx_ref[...] = dx
        _gate_bwd_tail(dx, y_ref, gate_ref, dy_ref, dgate_ref, first)

    vec = jax.ShapeDtypeStruct((1, D), F32)
    return pl.pallas_call(
        body, name="loss_head", grid=(S // tr,),
        out_shape=(jax.ShapeDtypeStruct((1, 1), F32), jax.ShapeDtypeStruct((S, D), F32), vec,
                   jax.ShapeDtypeStruct((S, D), _MXU_DTYPE), vec),
        in_specs=[_row_spec(tr, D), _vec_spec(D), _row_spec(tr, D), _row_spec(tr, D), _vec_spec(D)],
        out_specs=(pl.BlockSpec((1, 1), lambda i: (0, 0)), _row_spec(tr, D), _vec_spec(D), _row_spec(tr, D), _vec_spec(D)),
        compiler_params=_cp("arbitrary"))(x, g, target, y, gate)


def _gate_bwd(name, dx, y, gate, tr=256):
    S, D = dx.shape

    def body(dx_ref, y_ref, g_ref, dy_ref, dg_ref):
        dxv = dx_ref[...]
        dy_ref[...] = (g_ref[...] * dxv).astype(dy_ref.dtype)
        _acc_rows(dg_ref, dxv * y_ref[...], pl.program_id(0) == 0)

    return pl.pallas_call(
        body, name=name, grid=(S // tr,),
        out_shape=(jax.ShapeDtypeStruct((S, D), _MXU_DTYPE), jax.ShapeDtypeStruct((1, D), F32)),
        in_specs=[_row_spec(tr, D), _row_spec(tr, D), _vec_spec(D)], out_specs=(_row_spec(tr, D), _vec_spec(D)),
        compiler_params=_cp("arbitrary"))(dx, y, gate)


def _shift_down(v, k):
    t = lax.broadcasted_iota(jnp.int32, v.shape, 0)
    return jnp.where(t >= k, pltpu.roll(v, k, axis=0), 0.0)


def _shift_up(v, k):
    n = v.shape[0]
    t = lax.broadcasted_iota(jnp.int32, v.shape, 0)
    return jnp.where(t < n - k, pltpu.roll(v, n - k, axis=0), 0.0)


def _window_sum(p, w, shift):
    s, k = p, 1
    while k < w:
        s = s + shift(s, k)
        k *= 2
    return s


def _pool_count(shape, w):
    t = lax.broadcasted_iota(jnp.int32, shape, 0)
    return jnp.minimum(t + 1, w).astype(F32)


def _ab_specs(S):
    zs = [pl.BlockSpec((None, S, 128), functools.partial(lambda g, q: (2 * q + g // 2, 0, g % 2), q=q)) for q in range(4)]
    return zs


def _ab_mix_fwd(z8, conv_w, mix_w, scale):
    S = z8.shape[1]

    def body(b_ref, c_ref, a_ref, p_ref, w_ref, mix_ref, sc_ref, y_ref):
        g = pl.program_id(0)
        cg = c_ref[...].astype(F32) * a_ref[...].astype(F32)
        w = w_ref[...]
        conv = w[0:1] * _shift_down(cg, 2) + w[1:2] * _shift_down(cg, 1) + w[2:3] * cg
        y_ref[0] = (b_ref[...].astype(F32) * conv).astype(y_ref.dtype)
        for gg, win in enumerate(POOL_WINDOWS):
            @pl.when(g == gg)
            def _(win=win):
                p = p_ref[...].astype(F32)
                pooled = _window_sum(p, win, _shift_down) / _pool_count(p.shape, win) - p
                y_ref[1] = (_dot(pooled, mix_ref[...], "nn") * sc_ref[...]).astype(y_ref.dtype)

    return pl.pallas_call(
        body, name="ab_mix_fwd", grid=(4,), out_shape=jax.ShapeDtypeStruct((2, S, 512), _MXU_DTYPE),
        in_specs=_ab_specs(S) + [pl.BlockSpec((3, 128), lambda g: (0, g)), pl.BlockSpec((None, 128, 128), lambda g: (g, 0, 0)),
                                 pl.BlockSpec((1, 128), lambda g: (0, g))],
        out_specs=pl.BlockSpec((2, S, 128), lambda g: (0, 0, g)), compiler_params=_cp("parallel"))(z8, z8, z8, z8, conv_w, mix_w, scale)


def _ab_mix_bwd(z8, dycat2, conv_w, mix_w, scale, after):
    S = z8.shape[1]

    def body(b_ref, c_ref, a_ref, p_ref, dy_ref, w_ref, mix_ref, sc_ref, after_ref, dz_ref, dw_ref, dmix_ref, dsc_ref):
        g = pl.program_id(0)
        bv, cv, av, w = b_ref[...].astype(F32), c_ref[...].astype(F32), a_ref[...].astype(F32), w_ref[...]
        dya = dy_ref[0]
        cg = cv * av
        cg1, cg2 = _shift_down(cg, 1), _shift_down(cg, 2)
        conv = w[0:1] * cg2 + w[1:2] * cg1 + w[2:3] * cg
        dz_ref[0] = (dya * conv).astype(dz_ref.dtype)
        dconv = dya * bv
        dcg = w[2:3] * dconv + w[1:2] * _shift_up(dconv, 1) + w[0:1] * _shift_up(dconv, 2)
        dz_ref[1] = (dcg * av).astype(dz_ref.dtype)
        dz_ref[2] = (dcg * cv).astype(dz_ref.dtype)
        dw_ref[0:1, :] = jnp.sum(dconv * cg2, axis=0, keepdims=True)
        dw_ref[1:2, :] = jnp.sum(dconv * cg1, axis=0, keepdims=True)
        dw_ref[2:3, :] = jnp.sum(dconv * cg, axis=0, keepdims=True)
        for gg, win in enumerate(POOL_WINDOWS):
            @pl.when(g == gg)
            def _(win=win):
                p, dyb, mix = p_ref[...].astype(F32), dy_ref[1], mix_ref[...]
                cnt = _pool_count(p.shape, win)
                pooled = _window_sum(p, win, _shift_down) / cnt - p
                dsc_ref[...] = jnp.sum(dyb * _dot(pooled, mix, "nn"), axis=0, keepdims=True)
                dmixed = dyb * sc_ref[...]
                dmix_ref[...] = _dot(pooled, dmixed, "tn")
                dpooled = _dot(dmixed, mix, "nt")
                dz_ref[3] = (_window_sum(dpooled / cnt, win, _shift_up) - dpooled).astype(dz_ref.dtype)

    return pl.pallas_call(
        body, name="ab_mix_bwd", grid=(4,),
        out_shape=(jax.ShapeDtypeStruct((4, 2, S, 256), _MXU_DTYPE), jax.ShapeDtypeStruct((3, 512), F32),
                   jax.ShapeDtypeStruct((4, 128, 128), F32), jax.ShapeDtypeStruct((1, 512), F32)),
        in_specs=_ab_specs(S) + [pl.BlockSpec((2, S, 128), lambda g: (0, 0, g)), pl.BlockSpec((3, 128), lambda g: (0, g)),
                                 pl.BlockSpec((None, 128, 128), lambda g: (g, 0, 0)), pl.BlockSpec((1, 128), lambda g: (0, g)), ANY],
        out_specs=(pl.BlockSpec((4, None, S, 128), lambda g: (0, g // 2, 0, g % 2)), pl.BlockSpec((3, 128), lambda g: (0, g)),
                   pl.BlockSpec((None, 128, 128), lambda g: (g, 0, 0)), pl.BlockSpec((1, 128), lambda g: (0, g))),
        compiler_params=_cp("parallel"))(z8, z8, z8, z8, dycat2, conv_w, mix_w, scale, after)


HALO = 16


def _ffn_specs(S, n, tr):
    nb = S // HALO
    tile = pl.BlockSpec((2, None, tr, n), lambda j, i: (0, j, i, 0))
    prev = pl.BlockSpec((2, None, HALO, n), lambda j, i: (0, j, jnp.maximum(i * (tr // HALO) - 1, 0), 0))
    nxt = pl.BlockSpec((2, None, HALO, n), lambda j, i: (0, j, jnp.minimum((i + 1) * (tr // HALO), nb - 1), 0))
    cw = pl.BlockSpec((2, None, 3, n), lambda j, i: (0, j, 0, 0))
    return tile, prev, nxt, cw


def _shifted_rows(ext, lo, rows):
    ext = ext.astype(F32)
    return pltpu.roll(ext, 1, axis=0)[lo:lo + rows], pltpu.roll(ext, 2, axis=0)[lo:lo + rows]


def _ffn_gate_fwd(name, u24, cw24, tr=256):
    _, J, S, n = u24.shape
    tile, prev, _, cw = _ffn_specs(S, n, tr)

    def body(u_ref, up_ref, w_ref, a_ref, z_ref):
        keep = (pl.program_id(1) > 0).astype(u_ref.dtype)
        z = []
        for h in range(2):
            ext = jnp.concatenate([up_ref[h] * keep, u_ref[h]], axis=0)
            x1, x2 = _shifted_rows(ext, HALO, tr)
            w = w_ref[h]
            z.append(w[0:1] * x2 + w[1:2] * x1 + w[2:3] * u_ref[h].astype(F32))
        zg, zu = z
        sg = jax.nn.sigmoid(zg)
        silu = zg * sg
        a_ref[...] = (silu * zu).astype(a_ref.dtype)
        z_ref[0] = (zu * (sg * (1.0 + zg * (1.0 - sg)))).astype(z_ref.dtype)
        z_ref[1] = silu.astype(z_ref.dtype)

    return pl.pallas_call(
        body, name=name, grid=(J, S // tr),
        out_shape=(jax.ShapeDtypeStruct((J, S, n), _MXU_DTYPE), jax.ShapeDtypeStruct((2, J, S, n), ACT_DTYPE)),
        in_specs=[tile, prev, cw], out_specs=(pl.BlockSpec((None, tr, n), lambda j, i: (j, i, 0)), tile),
        compiler_params=_cp("parallel", "parallel"))(u24, u24, cw24)


def _ffn_gate_bwd(name, u24, z24, cw24, da4, w_up24, after, tr=256):
    _, J, S, n = u24.shape
    K = w_up24.shape[2]
    nb = S // HALO
    tile = pl.BlockSpec((2, None, tr, n), lambda i, j: (0, j, i, 0))
    nxt = pl.BlockSpec((2, None, HALO, n), lambda i, j: (0, j, jnp.minimum((i + 1) * (tr // HALO), nb - 1), 0))
    whole = lambda shape: pl.BlockSpec(shape, lambda i, j: (0,) * len(shape))

    def body(u_ref, z_ref, zn_ref, cw_ref, da_ref, dan_ref, wup_ref, after_ref, du_ref, dcw_ref, dh_ref, acc_ref):
        i, j = pl.program_id(0), pl.program_id(1)
        first = i == 0
        keep_next = (i < S // tr - 1).astype(F32)
        w = [cw_ref[h, j] for h in range(2)]
        m = tr + HALO
        da = jnp.concatenate([da_ref[...].astype(F32), dan_ref[...].astype(F32) * keep_next], axis=0)
        dz = [da * jnp.concatenate([z_ref[h], zn_ref[h]], axis=0).astype(F32) for h in range(2)]
        dh = None
        for h in range(2):
            d = dz[h]
            d0, d1, d2 = d[:tr], pltpu.roll(d, m - 1, axis=0)[:tr], pltpu.roll(d, m - 2, axis=0)[:tr]
            du = (w[h][2:3] * d0 + w[h][1:2] * d1 + w[h][0:1] * d2).astype(du_ref.dtype)
            du_ref[h] = du
            part = _dot(du, wup_ref[h, j], "nt")
            dh = part if dh is None else dh + part
            x0 = u_ref[h].astype(F32)
            parts = [jnp.sum(x0 * dk, axis=0, keepdims=True) for dk in (d2, d1, d0)]
            for k in range(3):
                @pl.when(first)
                def _(k=k, h=h):
                    dcw_ref[h, j, k:k + 1, :] = parts[k]

                @pl.when(jnp.logical_not(first))
                def _(k=k, h=h):
                    dcw_ref[h, j, k:k + 1, :] += parts[k]

        @pl.when(j == 0)
        def _():
            acc_ref[...] = dh

        @pl.when(j > 0)
        def _():
            acc_ref[...] += dh

        @pl.when(j == J - 1)
        def _():
            dh_ref[...] = acc_ref[...].astype(dh_ref.dtype)

    da_tile = pl.BlockSpec((None, tr, n), lambda i, j: (j, i, 0))
    da_next = pl.BlockSpec((None, HALO, n), lambda i, j: (j, jnp.minimum((i + 1) * (tr // HALO), nb - 1), 0))
    return pl.pallas_call(
        body, name=name, grid=(S // tr, J),
        out_shape=(jax.ShapeDtypeStruct((2, J, S, n), _MXU_DTYPE), jax.ShapeDtypeStruct((2, J, 3, n), F32),
                   jax.ShapeDtypeStruct((S, K), ACT_DTYPE)),
        in_specs=[tile, tile, nxt, whole((2, J, 3, n)), da_tile, da_next, whole((2, J, K, n)), ANY],
        out_specs=(tile, whole((2, J, 3, n)), pl.BlockSpec((tr, K), lambda i, j: (i, 0))),
        scratch_shapes=[pltpu.VMEM((tr, K), F32)],
        compiler_params=_cp("arbitrary", "arbitrary"))(u24, z24, z24, cw24, da4, da4, w_up24, after)


def _rms_rows(v, g):
    rstd = lax.rsqrt(jnp.mean(v * v, axis=-1, keepdims=True) + EPS)
    return v * rstd * g


def _rms_rows_bwd(v, g, dy):
    rstd = lax.rsqrt(jnp.mean(v * v, axis=-1, keepdims=True) + EPS)
    vhat = v * rstd
    dvhat = dy * g
    return rstd * (dvhat - vhat * jnp.mean(dvhat * vhat, axis=-1, keepdims=True)), dy * vhat


def _rope(v, cos, sa, sb):
    return v * cos + pltpu.roll(v, 112, axis=1) * sa + pltpu.roll(v, 16, axis=1) * sb


def _rope_t(d, cos, sa, sb):
    return d * cos + pltpu.roll(d * sa, 16, axis=1) + pltpu.roll(d * sb, 112, axis=1)


def _qkv_rope_fwd(z, qg, kvg, w_uq_t, w_kv, cosq, cosk, sa, sb, tr=256):
    S = z.shape[0]

    def body(ql_ref, kvl_ref, kpe_ref, qg_ref, kvg_ref, wq_ref, wkv_ref, cq_ref, ck_ref, sa_ref, sb_ref,
             qn_ref, kvn_ref, qo_ref, ko_ref, vo_ref):
        cq, ck, sa_v, sb_v = cq_ref[...], ck_ref[...], sa_ref[...], sb_ref[...]
        qn = _rms_rows(ql_ref[...], qg_ref[...]).astype(qn_ref.dtype)
        kvn = _rms_rows(kvl_ref[...], kvg_ref[...]).astype(kvn_ref.dtype)
        qn_ref[...] = qn
        kvn_ref[...] = kvn
        q = _dot(qn, wq_ref[...], "nt")
        kv = _dot(kvn, wkv_ref[...], "nn")
        kpe = _rope(kpe_ref[...], ck, sa_v, sb_v)
        for h in range(8):
            cols = slice(128 * h, 128 * h + 128)
            qo_ref[:, cols] = _rope(q[:, cols], cq, sa_v, sb_v).astype(qo_ref.dtype)
            ko_ref[:, cols] = (kv[:, cols] + kpe).astype(ko_ref.dtype)
        vo_ref[...] = kv[:, 1024:1536].astype(vo_ref.dtype)

    tab = _row_spec(tr, 128)
    whole = lambda a: pl.BlockSpec(a.shape, lambda i: (0, 0))
    return pl.pallas_call(
        body, name="qkv_rope_fwd", grid=(S // tr,),
        out_shape=(jax.ShapeDtypeStruct((S, 256), _MXU_DTYPE), jax.ShapeDtypeStruct((S, 128), _MXU_DTYPE),
                   jax.ShapeDtypeStruct((S, 1024), _MXU_DTYPE), jax.ShapeDtypeStruct((S, 1024), _MXU_DTYPE),
                   jax.ShapeDtypeStruct((S, 512), _MXU_DTYPE)),
        in_specs=[pl.BlockSpec((tr, 256), lambda i: (i, 0)), pl.BlockSpec((tr, 128), lambda i: (i, 2)),
                  pl.BlockSpec((tr, 128), lambda i: (i, 3)), _vec_spec(256), _vec_spec(128), whole(w_uq_t), whole(w_kv),
                  tab, tab, tab, tab],
        out_specs=(_row_spec(tr, 256), _row_spec(tr, 128), _row_spec(tr, 1024), _row_spec(tr, 1024), _row_spec(tr, 512)),
        compiler_params=_cp("parallel"))(z, z, z, qg, kvg, w_uq_t, w_kv, cosq, cosk, sa, sb)


def _attn_bwd_prep(o, dycat2, tr=256):
    S = o.shape[0]

    def body(o_ref, do_ref, delta_ref, doa_ref, dob_ref):
        do = do_ref[...]
        prod = do * o_ref[...]
        lane = lax.broadcasted_iota(jnp.int32, do.shape, 1)
        for p in range(4):
            cols = slice(128 * p, 128 * p + 128)
            first = lane[:, cols] < 128 * p + 64
            da = jnp.sum(jnp.where(first, prod[:, cols], 0.0), axis=-1, keepdims=True)
            db = jnp.sum(jnp.where(first, 0.0, prod[:, cols]), axis=-1, keepdims=True)
            delta_ref[p] = jnp.where(first, da, db)
            doa_ref[p] = jnp.where(first, do[:, cols], 0.0).astype(doa_ref.dtype)
            dob_ref[p] = jnp.where(first, 0.0, do[:, cols]).astype(dob_ref.dtype)

    pair = pl.BlockSpec((4, tr, 128), lambda i: (0, i, 0))
    return pl.pallas_call(
        body, name="attn_bwd_prep", grid=(S // tr,),
        out_shape=(jax.ShapeDtypeStruct((4, S, 128), F32), jax.ShapeDtypeStruct((4, S, 128), _MXU_DTYPE),
                   jax.ShapeDtypeStruct((4, S, 128), _MXU_DTYPE)),
        in_specs=[_row_spec(tr, 512), pl.BlockSpec((None, tr, 512), lambda i: (0, i, 0))],
        out_specs=(pair, pair, pair), compiler_params=_cp("parallel"))(o, dycat2)


def _qkv_rope_bwd(z, qg, kvg, dq, dk, dv, duv, w_uq_t, w_kv, cosq, cosk, sa, sb, tr=256):
    S = z.shape[0]

    def body(ql_ref, kvl_ref, qg_ref, kvg_ref, dq_ref, dk_ref, dv_ref, duv_ref, wq_ref, wkv_ref, cq_ref, ck_ref, sa_ref, sb_ref,
             dqo_ref, dkv_ref, dz_ref, dqg_ref, dkvg_ref):
        first = pl.program_id(0) == 0
        cq, ck, sa_v, sb_v = cq_ref[...], ck_ref[...], sa_ref[...], sb_ref[...]
        tot = jnp.zeros((tr, 128), F32)
        for h in range(8):
            cols = slice(128 * h, 128 * h + 128)
            dqo_ref[:, cols] = _rope_t(dq_ref[:, cols], cq, sa_v, sb_v).astype(dqo_ref.dtype)
            dkh = dk_ref[:, cols]
            tot = tot + dkh
            dkv_ref[:, cols] = dkh.astype(dkv_ref.dtype)
        dkv_ref[:, 1024:1536] = dv_ref[...].astype(dkv_ref.dtype)
        dqn = _dot(dqo_ref[...], wq_ref[...], "nn")
        dkvn = _dot(dkv_ref[...], wkv_ref[...], "nt")
        dql, dqg = _rms_rows_bwd(ql_ref[...], qg_ref[...], dqn)
        dkvl, dkvg = _rms_rows_bwd(kvl_ref[...], kvg_ref[...], dkvn)
        _acc_rows(dqg_ref, dqg, first)
        _acc_rows(dkvg_ref, dkvg, first)
        dz_ref[:, 0:256] = dql.astype(dz_ref.dtype)
        dz_ref[:, 256:384] = dkvl.astype(dz_ref.dtype)
        dz_ref[:, 384:512] = _rope_t(tot, ck, sa_v, sb_v).astype(dz_ref.dtype)
        dz_ref[:, 512:1536] = duv_ref[...].astype(dz_ref.dtype)

    tab = _row_spec(tr, 128)
    whole = lambda a: pl.BlockSpec(a.shape, lambda i: (0, 0))
    return pl.pallas_call(
        body, name="qkv_rope_bwd", grid=(S // tr,),
        out_shape=(jax.ShapeDtypeStruct((S, 1024), _MXU_DTYPE), jax.ShapeDtypeStruct((S, 1536), _MXU_DTYPE),
                   jax.ShapeDtypeStruct((S, 1536), _MXU_DTYPE), jax.ShapeDtypeStruct((1, 256), F32), jax.ShapeDtypeStruct((1, 128), F32)),
        in_specs=[pl.BlockSpec((tr, 256), lambda i: (i, 0)), pl.BlockSpec((tr, 128), lambda i: (i, 2)), _vec_spec(256), _vec_spec(128),
                  _row_spec(tr, 1024), _row_spec(tr, 1024), _row_spec(tr, 512), _row_spec(tr, 1024), whole(w_uq_t), whole(w_kv),
                  tab, tab, tab, tab],
        out_specs=(_row_spec(tr, 1024), _row_spec(tr, 1536), _row_spec(tr, 1536), _vec_spec(256), _vec_spec(128)),
        compiler_params=_cp("arbitrary"))(z, z, qg, kvg, dq, dk, dv, duv, w_uq_t, w_kv, cosq, cosk, sa, sb)


NEG = -1e30


def _attn_fwd(q, k, v, tq=512, tk=512):
    S = q.shape[0]
    assert tq == tk

    def body(q_ref, k_ref, v_ref, o_ref, lse_ref):
        i = pl.program_id(1)
        qs = [q_ref[:, 0:128], q_ref[:, 128:256]]

        def step(kb, carry, diagonal=False):
            start = pl.multiple_of(kb * tk, tk)
            vv = v_ref[pl.ds(start, tk), :]
            out = []
            for h in range(2):
                m, l, acc = carry[3 * h:3 * h + 3]
                s = _dot(qs[h], k_ref[pl.ds(start, tk), 128 * h:128 * h + 128], "nt") * ATTN_SCALE
                if diagonal:
                    s = jnp.where(below, s, NEG)
                m_new = jnp.maximum(m, jnp.max(s, axis=-1, keepdims=True))
                alpha = jnp.exp(m - m_new)
                p = jnp.exp(s - m_new)
                out += [m_new, alpha * l + jnp.sum(p, axis=-1, keepdims=True), alpha * acc + _dot(p, vv, "nn")]
            return tuple(out)

        below = lax.broadcasted_iota(jnp.int32, (tq, tk), 1) <= lax.broadcasted_iota(jnp.int32, (tq, tk), 0)
        init = (jnp.full((tq, 1), NEG, F32), jnp.zeros((tq, 1), F32), jnp.zeros((tq, 128), F32)) * 2
        ma, la, acca, mb, lb, accb = step(i, lax.fori_loop(0, i, step, init), diagonal=True)
        lane = lax.broadcasted_iota(jnp.int32, (tq, 128), 1)
        o_ref[...] = jnp.where(lane < 64, acca / la, accb / lb)
        lse_ref[...] = jnp.where(lane < 64, ma + jnp.log(la), mb + jnp.log(lb))

    return pl.pallas_call(
        body, name="attn_fwd", grid=(4, S // tq),
        out_shape=(jax.ShapeDtypeStruct((S, 512), F32), jax.ShapeDtypeStruct((4, S, 128), F32)),
        in_specs=[pl.BlockSpec((tq, 256), lambda p, i: (i, p)), pl.BlockSpec((S, 256), lambda p, i: (0, p)),
                  pl.BlockSpec((S, 128), lambda p, i: (0, p))],
        out_specs=(pl.BlockSpec((tq, 128), lambda p, i: (i, p)), pl.BlockSpec((None, tq, 128), lambda p, i: (p, i, 0))),
        compiler_params=_cp("parallel", "parallel"))(q, k, v)


def _attn_bwd(q, k, v, lse, delta, doa, dob, tq=512, tk=512):
    S = q.shape[0]
    assert tq == tk

    def body(q_ref, k_ref, v_ref, lse_ref, delta_ref, doa_ref, dob_ref, dq_ref, dk_ref, dv_ref):
        j = pl.program_id(1)

        @pl.when(j == 0)
        def _():
            dq_ref[...] = jnp.zeros_like(dq_ref)

        below = lax.broadcasted_iota(jnp.int32, (tq, tk), 1) <= lax.broadcasted_iota(jnp.int32, (tq, tk), 0)
        ks = [k_ref[:, 0:128], k_ref[:, 128:256]]
        vv = v_ref[...]

        def step(qb, carry, diagonal=False):
            dka, dkb, dvp = carry
            start = pl.multiple_of(qb * tq, tq)
            rows = pl.ds(start, tq)
            lse_v, delta_v = lse_ref[rows, :], delta_ref[rows, :]
            dos = [doa_ref[rows, :], dob_ref[rows, :]]
            dks = [dka, dkb]
            for h in range(2):
                delta = delta_v[:, 64 * h:64 * h + 1]
                do_h = dos[h]
                qh = q_ref[rows, 128 * h:128 * h + 128]
                s = _dot(qh, ks[h], "nt") * ATTN_SCALE
                p = jnp.exp(s - lse_v[:, 64 * h:64 * h + 1])
                if diagonal:
                    p = jnp.where(below, p, 0.0)
                dvp = dvp + _dot(p, do_h, "tn")
                ds = p * (_dot(do_h, vv, "nt") - delta) * ATTN_SCALE
                dq_ref[rows, 128 * h:128 * h + 128] += _dot(ds, ks[h], "nn")
                dks[h] = dks[h] + _dot(ds, qh, "tn")
            return dks[0], dks[1], dvp

        zero = jnp.zeros((tk, 128), F32)
        dka, dkb, dvp = lax.fori_loop(j + 1, S // tq, step, step(j, (zero, zero, zero), diagonal=True))
        dk_ref[:, 0:128] = dka
        dk_ref[:, 128:256] = dkb
        dv_ref[...] = dvp

    return pl.pallas_call(
        body, name="attn_bwd", grid=(4, S // tk),
        out_shape=(jax.ShapeDtypeStruct((S, 1024), F32), jax.ShapeDtypeStruct((S, 1024), F32), jax.ShapeDtypeStruct((S, 512), F32)),
        in_specs=[pl.BlockSpec((S, 256), lambda p, j: (0, p)), pl.BlockSpec((tk, 256), lambda p, j: (j, p)),
                  pl.BlockSpec((tk, 128), lambda p, j: (j, p))] + [pl.BlockSpec((None, S, 128), lambda p, j: (p, 0, 0))] * 4,
        out_specs=(pl.BlockSpec((S, 256), lambda p, j: (0, p)), pl.BlockSpec((tk, 256), lambda p, j: (j, p)),
                   pl.BlockSpec((tk, 128), lambda p, j: (j, p))),
        compiler_params=_cp("parallel", "arbitrary"))(q, k, v, lse, delta, doa, dob)


CHUNK = 128
GELU_C = math.sqrt(2.0 / math.pi)


def _gelu(v):
    t = jnp.tanh(GELU_C * (v + 0.044715 * (v * v * v)))
    return v * (0.5 * (1.0 + t)), t


def _gelu_grad(v, t):
    return 0.5 * (1.0 + t) + v * (0.5 * (1.0 - t * t) * GELU_C * (1.0 + 3.0 * 0.044715 * v * v))


def _tril(w):
    r = lax.broadcasted_iota(jnp.int32, w.shape, 0)
    c = lax.broadcasted_iota(jnp.int32, w.shape, 1)
    return jnp.where(c <= r, w, 0.0)


def _layer_norm(v, g, b):
    xc = v - jnp.mean(v, axis=-1, keepdims=True)
    rstd = lax.rsqrt(jnp.mean(xc * xc, axis=-1, keepdims=True) + EPS)
    xhat = xc * rstd
    return xhat * g + b, xhat, rstd


def _sgu_fwd(z, o, ln_g, ln_b, w_s, b_st, tr=256):
    S = z.shape[0]

    def body(u_ref, v_ref, o_ref, g_ref, b_ref, ws_ref, bs_ref, y_ref):
        gu, _ = _gelu(u_ref[...])
        gv, _ = _gelu(v_ref[...])
        vln, _, _ = _layer_norm(gv, g_ref[...], b_ref[...])
        y_ref[0] = o_ref[...].astype(y_ref.dtype)
        for g in range(4):
            wt = _tril(ws_ref[g])
            cols = slice(128 * g, 128 * g + 128)
            for ch in range(tr // CHUNK):
                rows = slice(CHUNK * ch, CHUNK * ch + CHUNK)
                mixed = _dot(wt, vln[rows, cols], "nn") + bs_ref[:, g:g + 1]
                y_ref[1, rows, cols] = (gu[rows, cols] * mixed).astype(y_ref.dtype)

    return pl.pallas_call(
        body, name="sgu_fwd", grid=(S // tr,), out_shape=jax.ShapeDtypeStruct((2, S, 512), _MXU_DTYPE),
        in_specs=[pl.BlockSpec((tr, 512), lambda i: (i, 1)), pl.BlockSpec((tr, 512), lambda i: (i, 2)), _row_spec(tr, 512),
                  _vec_spec(512), _vec_spec(512), pl.BlockSpec((4, 128, 128), lambda i: (0, 0, 0)), pl.BlockSpec((128, 4), lambda i: (0, 0))],
        out_specs=pl.BlockSpec((2, tr, 512), lambda i: (0, i, 0)), compiler_params=_cp("parallel"))(z, z, o, ln_g, ln_b, w_s, b_st)


def _sgu_bwd(z, dycat2, ln_g, ln_b, w_s, b_st, tr=256):
    S = z.shape[0]

    def body(u_ref, v_ref, dy_ref, g_ref, b_ref, ws_ref, bs_ref, duv_ref, dg_ref, db_ref, dws_ref, dbs_ref):
        first = pl.program_id(0) == 0
        u_pre, v_pre = u_ref[...], v_ref[...]
        gu, tu = _gelu(u_pre)
        gv, tv = _gelu(v_pre)
        gain = g_ref[...]
        vln, xhat, rstd = _layer_norm(gv, gain, b_ref[...])

        @pl.when(first)
        def _():
            dws_ref[...] = jnp.zeros_like(dws_ref)
            dbs_ref[...] = jnp.zeros_like(dbs_ref)

        dvln_cols = []
        for g in range(4):
            wt = _tril(ws_ref[g])
            cols = slice(128 * g, 128 * g + 128)
            dmixed_sum = jnp.zeros((CHUNK, 128), F32)
            dw = jnp.zeros((CHUNK, CHUNK), F32)
            dvln_rows = []
            for ch in range(tr // CHUNK):
                rows = slice(CHUNK * ch, CHUNK * ch + CHUNK)
                vt = vln[rows, cols]
                mixed = _dot(wt, vt, "nn") + bs_ref[:, g:g + 1]
                dyd = dy_ref[rows, cols]
                duv_ref[rows, cols] = (dyd * mixed * _gelu_grad(u_pre[rows, cols], tu[rows, cols])).astype(duv_ref.dtype)
                dmixed = dyd * gu[rows, cols]
                dmixed_sum = dmixed_sum + dmixed
                dw = dw + _dot(dmixed, vt, "nt")
                dvln_rows.append(_dot(wt, dmixed, "tn"))
            dws_ref[g] += _tril(dw)
            dbs_ref[g:g + 1, :] += jnp.sum(dmixed_sum.T, axis=0, keepdims=True)
            dvln_cols.append(jnp.concatenate(dvln_rows, axis=0))
        dvln = jnp.concatenate(dvln_cols, axis=1)
        _acc_rows(dg_ref, dvln * xhat, first)
        _acc_rows(db_ref, dvln, first)
        dxhat = dvln * gain
        dgv = rstd * (dxhat - jnp.mean(dxhat, axis=-1, keepdims=True) - xhat * jnp.mean(dxhat * xhat, axis=-1, keepdims=True))
        duv_ref[:, 512:1024] = (dgv * _gelu_grad(v_pre, tv)).astype(duv_ref.dtype)

    return pl.pallas_call(
        body, name="sgu_bwd", grid=(S // tr,),
        out_shape=(jax.ShapeDtypeStruct((S, 1024), _MXU_DTYPE), jax.ShapeDtypeStruct((1, 512), F32), jax.ShapeDtypeStruct((1, 512), F32),
                   jax.ShapeDtypeStruct((4, 128, 128), F32), jax.ShapeDtypeStruct((4, 128), F32)),
        in_specs=[pl.BlockSpec((tr, 512), lambda i: (i, 1)), pl.BlockSpec((tr, 512), lambda i: (i, 2)),
                  pl.BlockSpec((None, tr, 512), lambda i: (1, i, 0)), _vec_spec(512), _vec_spec(512),
                  pl.BlockSpec((4, 128, 128), lambda i: (0, 0, 0)), pl.BlockSpec((128, 4), lambda i: (0, 0))],
        out_specs=(_row_spec(tr, 1024), _vec_spec(512), _vec_spec(512), pl.BlockSpec((4, 128, 128), lambda i: (0, 0, 0)),
                   pl.BlockSpec((4, 128), lambda i: (0, 0))),
        compiler_params=_cp("arbitrary"))(z, z, dycat2, ln_g, ln_b, w_s, b_st)


def _sum_parts(name, parts, tr=512):
    P, R, C = parts.shape
    tr = _tile(R, tr) if R % 8 == 0 else R

    def body(p_ref, o_ref):
        g = p_ref[0]
        for k in range(1, P):
            g = g + p_ref[k]
        o_ref[...] = g

    return pl.pallas_call(
        body, name=name, grid=(R // tr,), out_shape=jax.ShapeDtypeStruct((R, C), F32),
        in_specs=[pl.BlockSpec((P, tr, C), lambda i: (0, i, 0))], out_specs=_row_spec(tr, C),
        compiler_params=_cp("parallel"))(parts)


def _adamw_math(w, m, v, g):
    c1 = 1.0 / (1.0 - ADAM_B1 ** ADAM_STEP)
    c2 = 1.0 / (1.0 - ADAM_B2 ** ADAM_STEP)
    m2 = ADAM_B1 * m + (1.0 - ADAM_B1) * g
    v2 = ADAM_B2 * v + (1.0 - ADAM_B2) * (g * g)
    return -ADAM_LR * ((m2 * c1) / (jnp.sqrt(v2 * c2) + ADAM_EPS) + ADAM_WD * w), m2, v2


def _adamw_small(name, params, parts):
    n = len(params)

    def body(*refs):
        ins, outs = refs[:4 * n], refs[4 * n:]
        for i in range(n):
            w_ref, m_ref, v_ref, p_ref = ins[4 * i:4 * i + 4]
            g = p_ref[0].astype(F32)
            for k in range(1, N_DEV):
                g = g + p_ref[k].astype(F32)
            delta, m2, v2 = _adamw_math(w_ref[...], m_ref[...], v_ref[...], g)
            outs[4 * i][...] = g
            outs[4 * i + 1][...] = delta
            outs[4 * i + 2][...] = m2
            outs[4 * i + 3][...] = v2

    flat = [a for (w, m, v), p in zip(params, parts) for a in (w, m, v, p)]
    out = pl.pallas_call(
        body, name=name, out_shape=[jax.ShapeDtypeStruct(w.shape, F32) for (w, _, _) in params for _ in range(4)],
        compiler_params=pltpu.CompilerParams(vmem_limit_bytes=_VMEM_LIMIT))(*flat)
    return [out[4 * i:4 * i + 4] for i in range(n)]


ADAMW_BLOCK_BYTES = 36 * 2 ** 20


def _adamw(name, w, m, v, parts):
    L, R, C = w.shape
    P = parts[0].shape[0]
    row_bytes = 2 * C * (7 * 4 + P * parts[0].dtype.itemsize)
    tr = R
    if R * row_bytes > ADAMW_BLOCK_BYTES:
        tr = next(t for t in (1024, 512, 256, 128, 64, 32, 16) if R % t == 0 and t * row_bytes <= ADAMW_BLOCK_BYTES)
    nr = R // tr
    c1 = 1.0 / (1.0 - ADAM_B1 ** ADAM_STEP)
    c2 = 1.0 / (1.0 - ADAM_B2 ** ADAM_STEP)

    def body(w_ref, m_ref, v_ref, *rest):
        p_refs, (g_ref, d_ref, mo_ref, vo_ref) = rest[:L], rest[L:]
        for ll in range(L):
            @pl.when(pl.program_id(0) == ll)
            def _(p_ref=p_refs[ll]):
                g = p_ref[0].astype(F32)
                for k in range(1, P):
                    g = g + p_ref[k].astype(F32)
                m2 = ADAM_B1 * m_ref[...] + (1.0 - ADAM_B1) * g
                v2 = ADAM_B2 * v_ref[...] + (1.0 - ADAM_B2) * (g * g)
                g_ref[...] = g
                mo_ref[...] = m2
                vo_ref[...] = v2
                d_ref[...] = -ADAM_LR * ((m2 * c1) / (jnp.sqrt(v2 * c2) + ADAM_EPS) + ADAM_WD * w_ref[...])

    def part_spec(ll):
        return pl.BlockSpec((P, tr, C), lambda l, i: (0, jnp.where(l == ll, i, jnp.where(l < ll, 0, nr - 1)), 0))

    full = pl.BlockSpec((None, tr, C), lambda l, i: (l, i, 0))
    sds = jax.ShapeDtypeStruct((L, R, C), F32)
    return pl.pallas_call(
        body, name=name, grid=(L, nr), out_shape=(sds, sds, sds, sds),
        in_specs=[full] * 3 + [part_spec(ll) for ll in range(L)],
        out_specs=(full,) * 4, compiler_params=_cp("arbitrary", "arbitrary"))(w, m, v, *parts)


def _rope_tables(positions):
    half = 16
    inv_freq = 10000.0 ** (-jnp.arange(half, dtype=F32) / half)
    ang = positions.astype(F32)[:, None] * inv_freq
    cos, sin = jnp.cos(ang), jnp.sin(ang)
    S = positions.shape[0]
    z16, z32, z64 = jnp.zeros((S, 16), F32), jnp.zeros((S, 32), F32), jnp.zeros((S, 64), F32)
    cosk = jnp.concatenate([z64, cos, cos, z32], axis=1)
    cosq = jnp.concatenate([jnp.ones((S, 64), F32), cos, cos, z32], axis=1)
    sa = jnp.concatenate([z64, -sin, z16, z32], axis=1)
    sb = jnp.concatenate([z64, z16, sin, z32], axis=1)
    return cosq, cosk, sa, sb


def _ffn_fwd(l, x, mod, n2g, get_w_up8, cw24, get_w_down4):
    sh, sc, gate = mod
    h = _rmsmod_fwd(f"ffn{l}_norm", x, n2g, sc, sh, n2g)
    w_up8 = get_w_up8(h)
    u8 = _mm_cols(f"ffn{l}_up", h, w_up8, out_dtype=ACT_DTYPE, tm=2048)
    S, n = u8.shape[1], u8.shape[2]
    u24 = u8.reshape(2, 4, S, n)
    a4, z24 = _ffn_gate_fwd(f"ffn{l}_gate", u24, cw24)
    w_down4 = get_w_down4(a4)
    f, x_new = _mm_rows_resid(f"ffn{l}_down", a4, w_down4, x, gate)
    return x_new, (x, h, u24, a4, f, z24), w_up8, w_down4


def _ffn_bwd(l, dx, df, dgate, saved, mod, n2g, w_up8, cw24, w_down4, me, y_prev, gate_prev):
    sh, sc, gate = mod
    x, h, u24, a4, f, z24 = saved
    da4 = _mm_rows_dx(f"ffn{l}_down_dx", df, w_down4, out_dtype=ACT_DTYPE, tm=2048)
    dw_down4 = _mm_rows_dw(f"ffn{l}_down_dw", a4, df, out_dtype=WIRE_DTYPE, tn=1024)
    sent_down, token = _exchange_start(f"scatter_ffn{l}_down", [dw_down4.reshape(8, 352, dw_down4.shape[2])], True, dgate, me)
    du24, dcw24, dh = _ffn_gate_bwd(f"ffn{l}_act_bwd", u24, z24, cw24, da4, w_up8.reshape((2, 4) + w_up8.shape[1:]), token)
    du8 = du24.reshape((8,) + du24.shape[2:])
    dw_up8t = _mm_cols_dwt(f"ffn{l}_up_dw", h, du8, out_dtype=WIRE_DTYPE, tk=1024)
    sent_up, token = _exchange_start(f"scatter_ffn{l}_up", [dw_up8t], True, dcw24, me)
    dx_new, dn2g, dsc, dsh, dy_prev, dgate_prev = _rmsmod_bwd(f"ffn{l}_norm_bwd", x, n2g, sc, dh, dx, token, y_prev, gate_prev)
    return dx_new, dict(sent_up=sent_up, sent_down=sent_down, cw24=dcw24, n2g=dn2g, mod=(dsh, dsc, dgate)), dy_prev, dgate_prev


def kernel(x, c, positions, ada_w, ada_b, norm1_g, norm2_g, ab_w_in, a_conv_w, b_mix_w, b_scale, ab_w_out, cd_w_in, c_q_norm_g, c_w_uq, c_kv_norm_g, c_w_ukv, d_ln_g, d_ln_b, d_w_s, d_b_s, cd_w_out, ffn_w_up, ffn_conv_w, ffn_w_down, final_norm_g, loss_target, m_ada_w, m_ada_b, m_norm1_g, m_norm2_g, m_ab_w_in, m_a_conv_w, m_b_mix_w, m_b_scale, m_ab_w_out, m_cd_w_in, m_c_q_norm_g, m_c_w_uq, m_c_kv_norm_g, m_c_w_ukv, m_d_ln_g, m_d_ln_b, m_d_w_s, m_d_b_s, m_cd_w_out, m_ffn_w_up, m_ffn_conv_w, m_ffn_w_down, m_final_norm_g, v_ada_w, v_ada_b, v_norm1_g, v_norm2_g, v_ab_w_in, v_a_conv_w, v_b_mix_w, v_b_scale, v_ab_w_out, v_cd_w_in, v_c_q_norm_g, v_c_w_uq, v_c_kv_norm_g, v_c_w_ukv, v_d_ln_g, v_d_ln_b, v_d_w_s, v_d_b_s, v_cd_w_out, v_ffn_w_up, v_ffn_conv_w, v_ffn_w_down, v_final_norm_g):
    S, D = x.shape[1], x.shape[2]
    me = 4 * lax.axis_index("x") + 2 * lax.axis_index("y") + lax.axis_index("c")
    x0, target = x[0], loss_target[0]
    W = _MXU_DTYPE

    small_shapes = [(1024,), (3, 64), (32,), (64,), (64,), (2, 3, 704)]
    gw_ab, token = _hier_gather_start("gather_w_ab", [ab_w_in[0].astype(W), ab_w_out[0].astype(W)], c, me)
    (g0,) = _exchange("gather_small", [[_pack([c, a_conv_w, c_q_norm_g, d_ln_g, d_ln_b, ffn_conv_w]) + 0.0 * token[:1]]],
                      scatter=False)
    c_all, aconv_s, qg_s, lng_s, lnb_s, fcw_s = _unpack(g0[:, 0], small_shapes, lead=(N_DEV,))
    conv_w = aconv_s.transpose(1, 0, 2).reshape(3, 512)
    qg, ln_g, ln_b = qg_s.reshape(1, 256), lng_s.reshape(1, 512), lnb_s.reshape(1, 512)
    cw24 = [fcw_s[:, l].reshape(2, 4, 3, 704) for l in range(2)]
    c16 = jnp.pad(c_all, ((0, 16 - N_DEV), (0, 0)))

    mod_cols = _ada_fwd(c16, ada_w)
    (g1,) = _exchange("gather_mod", [[_pack([mod_cols])]], scatter=False)
    mod_all = _unpack(g1[:, 0], [(2, 16, 768)], lead=(N_DEV,))[0]
    mod_mine = lax.dynamic_index_in_dim(mod_all, me, axis=2, keepdims=False)
    mod = mod_mine.transpose(1, 0, 2).reshape(2, 6 * D) + ada_b
    mods = [[mod[l, k * D:(k + 1) * D].reshape(1, D) for k in range(6)] for l in range(2)]

    gw_up0, token = _hier_gather_start("gather_w_ffn0_up", [ffn_w_up[0].astype(W)], mod, me)
    gw_rest, started = _exchange_start("gather_w_rest", [
        ffn_w_down[0].astype(W), cd_w_in[0].T.astype(W), c_w_uq[0].T.astype(W), c_w_ukv[0].astype(W), cd_w_out[0].astype(W),
        ffn_w_up[1].astype(W), ffn_w_down[1].astype(W)], False, token, me)

    cosq, cosk, sa, sb = _rope_tables(positions[0])
    n1g = [norm1_g[l].reshape(1, D) for l in range(2)]
    n2g = [norm2_g[l].reshape(1, D) for l in range(2)]
    mix_w, scale = b_mix_w[0], b_scale
    kvg = c_kv_norm_g
    w_s, b_st = d_w_s[0], d_b_s[0].T

    sh1, sc1, g1m = mods[0][:3]
    h_ab = _rmsmod_fwd("ab_norm", x0, n1g[0], sc1, sh1, started)
    w_abin8, w_about = _hier_gather_wait("wait_w_ab", _hier_gather_forward("forward_w_ab", gw_ab, h_ab), h_ab)
    w_about2 = w_about.reshape(2, 512, D)
    z8 = _mm_cols("ab_in", h_ab, w_abin8, out_dtype=ACT_DTYPE, tm=2048)
    ycat_ab = _ab_mix_fwd(z8, conv_w, mix_w, scale)
    y_ab, x1 = _mm_rows_resid("ab_out", ycat_ab, w_about2, x0, g1m)
    w_up8, w_down4 = [None, None], [None, None]
    gw_up0 = _hier_gather_forward("forward_w_ffn0_up", gw_up0, x1)
    x2, ffn0_saved, w_up8[0], w_down4[0] = _ffn_fwd(
        0, x1, mods[0][3:], n2g[0], lambda after: _hier_gather_wait("wait_w_ffn0_up", gw_up0, after)[0], cw24[0],
        lambda after: _exchange_wait("wait_w_ffn0_down", gw_rest, after, [0])[0].reshape(4, 704, D))

    w_cdin, w_uq, w_ukv, w_cdout = _exchange_wait("wait_w_cd", gw_rest, x2, [1, 2, 3, 4])
    w_cdout2 = w_cdout.reshape(2, 512, D)
    w_cd_t = w_cdin.reshape(1440, D)
    zr = lambda n: jnp.zeros((n, D), W)
    w_cd_pad = jnp.concatenate([w_cd_t[:384], zr(64), w_cd_t[384:416], zr(32), w_cd_t[416:]], axis=0)
    w_uq_pad = jnp.pad(w_uq, ((0, 0), (0, 32), (0, 0))).reshape(1024, 256)
    w_ukv_h = w_ukv.transpose(1, 0, 2)
    w_k_pad = jnp.pad(w_ukv_h[:, :, :64], ((0, 0), (0, 0), (0, 64))).reshape(128, 1024)
    w_kv_pad = jnp.concatenate([w_k_pad, w_ukv_h[:, :, 64:].reshape(128, 512)], axis=1)

    sh1, sc1, g1c = mods[1][:3]
    h_cd = _rmsmod_fwd("cd_norm", x2, n1g[1], sc1, sh1, n1g[1])
    z_cd = _mm_nt("cd_in", h_cd, w_cd_pad, tm=1024, tn=1536)
    qn, kvn, q_r, k_r, v_r = _qkv_rope_fwd(z_cd, qg, kvg, w_uq_pad, w_kv_pad, cosq, cosk, sa, sb)
    o, lse = _attn_fwd(q_r, k_r, v_r)
    ycat_cd = _sgu_fwd(z_cd, o, ln_g, ln_b, w_s, b_st)
    y_cd, x3 = _mm_rows_resid("cd_out", ycat_cd, w_cdout2, x2, g1c)
    x4, ffn1_saved, w_up8[1], w_down4[1] = _ffn_fwd(
        1, x3, mods[1][3:], n2g[1], lambda after: _exchange_wait("wait_w_ffn1_up", gw_rest, after, [5])[0], cw24[1],
        lambda after: _exchange_wait("wait_w_ffn1_down", gw_rest, after, [6])[0].reshape(4, 704, D))

    loss_local, dx4, dfg, df1, dgate1 = _loss_head(x4, final_norm_g.reshape(1, D), target, ffn1_saved[4], mods[1][5])

    dx3, gf1, dy, dg1c = _ffn_bwd(1, dx4, df1, dgate1, ffn1_saved, mods[1][3:], n2g[1], w_up8[1], cw24[1], w_down4[1], me, y_cd, g1c)

    dycat = _mm_rows_dx("cd_out_dx", dy, w_cdout2, tm=2048)
    dw_cdout = _mm_rows_dw("cd_out_dw", ycat_cd, dy, out_dtype=WIRE_DTYPE, tn=1024)
    duv, dln_g, dln_b, dws, dbs = _sgu_bwd(z_cd, dycat, ln_g, ln_b, w_s, b_st)
    dq_r, dk_r, dv_r = _attn_bwd(q_r, k_r, v_r, lse, *_attn_bwd_prep(o, dycat))
    dqraw, dkvall, dz_cd, dqg, dkvg = _qkv_rope_bwd(z_cd, qg, kvg, dq_r, dk_r, dv_r, duv, w_uq_pad, w_kv_pad, cosq, cosk, sa, sb)
    dw_uq_pad = _mm_tn("cd_uq_dw", dqraw, qn, tn=256)
    dw_kv_pad = _mm_tn("cd_ukv_dw", kvn, dkvall, tm=128)
    dh_cd = _mm_nn("cd_in_dx", dz_cd, w_cd_pad, out_dtype=ACT_DTYPE, tm=1024, tn=1024)
    dw_cd_pad = _mm_tn("cd_in_dw", dz_cd, h_cd, tm=768, tn=1024)
    dw_cd8 = jnp.concatenate([dw_cd_pad[:384], dw_cd_pad[448:480], dw_cd_pad[512:]], axis=0).astype(WIRE_DTYPE).reshape(8, 180, D)
    dw_uq8 = dw_uq_pad.reshape(8, 128, 256)[:, :96].astype(WIRE_DTYPE)
    dw_ukv8 = jnp.concatenate([dw_kv_pad[:, :1024].reshape(128, 8, 128)[:, :, :64], dw_kv_pad[:, 1024:].reshape(128, 8, 64)],
                              axis=2).transpose(1, 0, 2).astype(WIRE_DTYPE)
    sent_cd, token = _exchange_start("scatter_cd", [dw_cd8, dw_uq8, dw_ukv8, dw_cdout.reshape(8, 128, D)], True, dqg, me)
    early_names = ["c_kv_norm_g", "d_w_s", "d_b_s", "final_norm_g", "c_q_norm_g", "d_ln_g", "d_ln_b"]
    early_grads = [dkvg, dws.reshape(512, 128).astype(WIRE_DTYPE), dbs, dfg, dqg.reshape(8, 1, 32), dln_g.reshape(8, 1, 64),
                   dln_b.reshape(8, 1, 64)]
    early_sent, token = _exchange_start("gather_small_grads_early", early_grads, [False] * 4 + [True] * 3, token, me)
    dx2, dn1g_cd, dsc1_cd, dsh1_cd, df0, dgate0 = _rmsmod_bwd("cd_norm_bwd", x2, n1g[1], sc1, dh_cd, dx3, token,
                                                              ffn0_saved[4], mods[0][5])

    dx1, gf0, dy, dg1m = _ffn_bwd(0, dx2, df0, dgate0, ffn0_saved, mods[0][3:], n2g[0], w_up8[0], cw24[0], w_down4[0], me, y_ab, g1m)

    dw_about = _mm_rows_dw("ab_out_dw", ycat_ab, dy, out_dtype=WIRE_DTYPE, tn=1024)
    sent_about, token = _exchange_start("scatter_ab_out", [dw_about.reshape(8, 128, D)], True, dg1m, me)
    dycat = _mm_rows_dx("ab_out_dx", dy, w_about2, tm=2048)
    dz8, dconv_w, dmix_w, dscale = _ab_mix_bwd(z8, dycat, conv_w, mix_w, scale, token)
    dz8 = dz8.reshape(8, S, 256)
    dw_abin8 = _mm_cols_dw("ab_in_dw", h_ab, dz8, out_dtype=WIRE_DTYPE, tk=1024)
    sent_abin, token = _exchange_start("scatter_ab_in", [dw_abin8], True, dscale, me)
    dh_ab = _mm_cols_dx("ab_in_dx", dz8, w_abin8, out_dtype=ACT_DTYPE)
    dx0, dn1g_ab, dsc1_ab, dsh1_ab = _rmsmod_bwd("ab_norm_bwd", x0, n1g[0], mods[0][1], dh_ab, dx1, token)

    dmod = jnp.stack([jnp.concatenate([dsh1_ab, dsc1_ab, dg1m, *gf0["mod"]], axis=1)[0],
                      jnp.concatenate([dsh1_cd, dsc1_cd, dg1c, *gf1["mod"]], axis=1)[0]])
    late_names = ["ada_b", "norm1_g", "norm2_g", "b_mix_w", "b_scale", "a_conv_w", "ffn_conv_w"]
    late_grads = [dmod, jnp.concatenate([dn1g_ab, dn1g_cd]), jnp.concatenate([gf0["n2g"], gf1["n2g"]]),
                  dmix_w.reshape(512, 128).astype(WIRE_DTYPE), dscale, dconv_w.reshape(3, 8, 64).transpose(1, 0, 2),
                  jnp.stack([gf0["cw24"].reshape(8, 3, 704), gf1["cw24"].reshape(8, 3, 704)], axis=1),
                  jnp.pad(loss_local, ((0, 0), (0, 127)))]
    small_view = dict(ada_b=(2, 6 * D), norm1_g=(2, D), norm2_g=(2, D), b_mix_w=(512, 128), b_scale=(1, 512), c_kv_norm_g=(1, 128),
                      d_w_s=(512, 128), d_b_s=(4, 128), final_norm_g=(1, D),
                      a_conv_w=(3, 64), c_q_norm_g=(1, 32), d_ln_g=(1, 64), d_ln_b=(1, 64), ffn_conv_w=(2, 3, 704))
    late_sent, token = _exchange_start("gather_small_grads_late", late_grads, [False] * 5 + [True] * 2 + [False], dx0, me)

    res = {}

    def update(name, w, m, v, parts, shape3d):
        outs = _adamw("adamw_" + name, w.reshape(shape3d), m.reshape(shape3d), v.reshape(shape3d),
                      [p.reshape((p.shape[0],) + shape3d[1:]) for p in parts])
        res[name] = [o_.reshape(w.shape) for o_ in outs]

    p_cdin, p_uq, p_ukv, p_cdout = _exchange_wait("wait_scatter_cd", sent_cd, token)
    swap = lambda a: jnp.swapaxes(a, 1, 2)
    update("cd_w_in", swap(cd_w_in), swap(m_cd_w_in), swap(v_cd_w_in), [p_cdin], (1, 180, D))
    update("c_w_uq", swap(c_w_uq), swap(m_c_w_uq), swap(v_c_w_uq), [p_uq], (1, 96, 256))
    for name in ("cd_w_in", "c_w_uq"):
        res[name] = [swap(o_) for o_ in res[name]]
    update("c_w_ukv", c_w_ukv, m_c_w_ukv, v_c_w_ukv, [p_ukv], (1, 128, 128))
    update("cd_w_out", cd_w_out, m_cd_w_out, v_cd_w_out, [p_cdout], (1, 128, D))
    (p_dn1,) = _exchange_wait("wait_scatter_ffn1_down", gf1["sent_down"], token)
    (p_dn0,) = _exchange_wait("wait_scatter_ffn0_down", gf0["sent_down"], res["cd_w_out"][0])
    update("ffn_w_down", ffn_w_down, m_ffn_w_down, v_ffn_w_down, [p_dn0, p_dn1], (2, 352, D))
    (p_up1,) = _exchange_wait("wait_scatter_ffn1_up", gf1["sent_up"], token)
    (p_up0,) = _exchange_wait("wait_scatter_ffn0_up", gf0["sent_up"], res["ffn_w_down"][0])
    swap = lambda a: jnp.swapaxes(a, 1, 2)
    update("ffn_w_up", swap(ffn_w_up), swap(m_ffn_w_up), swap(v_ffn_w_up), [p_up0, p_up1], (2, 704, D))
    up_done = res["ffn_w_up"][0]
    res["ffn_w_up"] = [swap(o_) for o_ in res["ffn_w_up"]]
    (p_about,) = _exchange_wait("wait_scatter_ab_out", sent_about, up_done)
    update("ab_w_out", ab_w_out, m_ab_w_out, v_ab_w_out, [p_about], (1, 128, D))
    (p_abin,) = _exchange_wait("wait_scatter_ab_in", sent_abin, res["ab_w_out"][0])
    update("ab_w_in", ab_w_in, m_ab_w_in, v_ab_w_in, [p_abin], (1, D, 256))

    early_parts = _exchange_wait("wait_small_grads_early", early_sent, res["ab_w_in"][0])
    late_parts = _exchange_wait("wait_small_grads_late", late_sent, res["ab_w_in"][0])
    small_names = early_names + late_names
    small_parts = list(early_parts) + list(late_parts[:7])
    loss = jnp.sum(late_parts[7][:, 0, 0])
    dmod_all = late_parts[0]
    dmod_cols = lax.dynamic_slice_in_dim(dmod_all, me * 768, 768, axis=2).transpose(1, 0, 2)
    g_ada_w = _ada_bwd(c16, jnp.pad(dmod_cols, ((0, 0), (0, 16 - N_DEV), (0, 0))))
    update("ada_w", ada_w, m_ada_w, v_ada_w, [g_ada_w[None]], (1, 2 * D, 768))

    small_w = dict(ada_b=(ada_b, m_ada_b, v_ada_b), norm1_g=(norm1_g, m_norm1_g, v_norm1_g), norm2_g=(norm2_g, m_norm2_g, v_norm2_g),
                   b_mix_w=(b_mix_w, m_b_mix_w, v_b_mix_w), b_scale=(b_scale, m_b_scale, v_b_scale),
                   c_kv_norm_g=(c_kv_norm_g, m_c_kv_norm_g, v_c_kv_norm_g), d_w_s=(d_w_s, m_d_w_s, v_d_w_s),
                   d_b_s=(d_b_s, m_d_b_s, v_d_b_s), final_norm_g=(final_norm_g, m_final_norm_g, v_final_norm_g),
                   a_conv_w=(a_conv_w, m_a_conv_w, v_a_conv_w), c_q_norm_g=(c_q_norm_g, m_c_q_norm_g, v_c_q_norm_g),
                   d_ln_g=(d_ln_g, m_d_ln_g, v_d_ln_g), d_ln_b=(d_ln_b, m_d_ln_b, v_d_ln_b),
                   ffn_conv_w=(ffn_conv_w, m_ffn_conv_w, v_ffn_conv_w))
    small_out = _adamw_small("adamw_small", [tuple(a.reshape(small_view[n]) for a in small_w[n]) for n in small_names],
                             list(small_parts))
    for n, outs in zip(small_names, small_out):
        res[n] = [o_.reshape(small_w[n][0].shape) for o_ in outs]

    order = ["ada_w", "ada_b", "norm1_g", "norm2_g", "ab_w_in", "a_conv_w", "b_mix_w", "b_scale", "ab_w_out", "cd_w_in", "c_q_norm_g",
             "c_w_uq", "c_kv_norm_g", "c_w_ukv", "d_ln_g", "d_ln_b", "d_w_s", "d_b_s", "cd_w_out", "ffn_w_up", "ffn_conv_w",
             "ffn_w_down", "final_norm_g"]
    return (loss, dx0[None], *[res[n][0] for n in order], *[res[n][1] for n in order], *[res[n][2] for n in order],
            *[res[n][3] for n in order])
```

```python
import functools
import math

import jax
import jax.numpy as jnp
from jax import lax
from jax.experimental import pallas as pl
from jax.experimental.pallas import tpu as pltpu

F32 = jnp.float32
BF16 = jnp.bfloat16
_MXU_DTYPE = BF16
WIRE_DTYPE = BF16
ACT_DTYPE = BF16
_VMEM_LIMIT = 56 * 2 ** 20
N_DEV = 8
EPS = 1e-6
POOL_WINDOWS = (2, 4, 8, 16)
ATTN_SCALE = (64 + 32) ** -0.5
ADAM_LR, ADAM_B1, ADAM_B2, ADAM_EPS, ADAM_WD, ADAM_STEP = 0.001, 0.9, 0.999, 1e-08, 0.01, 10
MESH = pl.DeviceIdType.MESH
ANY = pl.BlockSpec(memory_space=pl.ANY)


def _cp(*sem):
    return pltpu.CompilerParams(dimension_semantics=sem, vmem_limit_bytes=_VMEM_LIMIT)


def _dot(a, b, contract):
    dn = {"nn": (((1,), (0,)), ((), ())), "nt": (((1,), (1,)), ((), ())), "tn": (((0,), (0,)), ((), ()))}[contract]
    return lax.dot_general(a.astype(_MXU_DTYPE), b.astype(_MXU_DTYPE), dn, preferred_element_type=F32)


def _my_position():
    x, y, c = lax.axis_index("x"), lax.axis_index("y"), lax.axis_index("c")
    return x, y, c, 4 * x + 2 * y + c


def _exchange(name, groups, scatter):
    flat = [a for g in groups for a in g]
    n_in, n_grp = len(flat), len(groups)
    out_shapes = []
    for g in groups:
        slab = g[0].shape[1:] if scatter else g[0].shape
        out_shapes.append(jax.ShapeDtypeStruct((N_DEV, len(g)) + tuple(slab), g[0].dtype))

    def body(*refs):
        ins, outs = refs[:n_in], refs[n_in:n_in + n_grp]
        send_sems, recv_sems, local_sems = refs[n_in + n_grp:]
        x, y, c, me = _my_position()
        i = 0
        for gi, g in enumerate(groups):
            for l in range(len(g)):
                src = ins[i]
                i += 1
                pltpu.make_async_copy(src.at[me] if scatter else src, outs[gi].at[me, l], local_sems.at[gi]).start()
                for k in range(1, N_DEV):
                    px = 1 - x if k & 4 else x
                    py = 1 - y if k & 2 else y
                    pc = 1 - c if k & 1 else c
                    peer = 4 * px + 2 * py + pc
                    pltpu.make_async_remote_copy(
                        src_ref=src.at[peer] if scatter else src, dst_ref=outs[gi].at[me, l],
                        send_sem=send_sems.at[gi], recv_sem=recv_sems.at[gi],
                        device_id=(px, py, pc), device_id_type=MESH).start()
        for gi in range(n_grp):
            mine = outs[gi].at[me]
            pltpu.make_async_copy(mine, mine, local_sems.at[gi]).wait()
            seven = outs[gi].at[pl.ds(0, N_DEV - 1)]
            w = pltpu.make_async_remote_copy(src_ref=seven, dst_ref=seven, send_sem=send_sems.at[gi],
                                             recv_sem=recv_sems.at[gi], device_id=(x, y, c), device_id_type=MESH)
            w.wait_send()
            w.wait_recv()

    return pl.pallas_call(
        body, name=name, out_shape=tuple(out_shapes),
        in_specs=[ANY] * n_in, out_specs=tuple([ANY] * n_grp),
        scratch_shapes=[pltpu.SemaphoreType.DMA((n_grp,)), pltpu.SemaphoreType.DMA((n_grp,)),
                        pltpu.SemaphoreType.DMA((n_grp,))],
        compiler_params=pltpu.CompilerParams(has_side_effects=True),
    )(*flat)


HBM_SPEC = pl.BlockSpec(memory_space=pltpu.HBM)
SEM_SPEC = pl.BlockSpec(memory_space=pltpu.SEMAPHORE)
EFFECT = pltpu.SideEffectType.DATAFLOW_SIDE_EFFECTING


def _put_mine(name, srcs, scatter, me):
    n = len(srcs)
    slabs = [tuple(s.shape[1:] if sc else s.shape) for s, sc in zip(srcs, scatter)]

    def body(me_ref, *refs):
        for i in range(n):
            refs[n + i][...] = refs[i][...]

    def at_me(slab):
        return pl.BlockSpec((None,) + slab, lambda g, me_ref, nd=len(slab): (me_ref[0],) + (0,) * nd)

    def whole(slab):
        return pl.BlockSpec(slab, lambda g, me_ref, nd=len(slab): (0,) * nd)

    return pl.pallas_call(
        body, name=name,
        grid_spec=pltpu.PrefetchScalarGridSpec(
            num_scalar_prefetch=1, grid=(1,),
            in_specs=[at_me(slab) if sc else whole(slab) for slab, sc in zip(slabs, scatter)],
            out_specs=[at_me(slab) for slab in slabs]),
        out_shape=[jax.ShapeDtypeStruct((N_DEV,) + slab, s.dtype) for slab, s in zip(slabs, srcs)],
        compiler_params=_cp("arbitrary"))(me.reshape(1), *srcs)


def _exchange_start(name, srcs, scatter, after, me):
    n = len(srcs)
    scatter = list(scatter) if isinstance(scatter, (list, tuple)) else [scatter] * n
    lands = _put_mine(name + "_mine", srcs, scatter, me)
    srcs = [pltpu.with_memory_space_constraint(a, pltpu.HBM) for a in srcs]
    lands = [pltpu.with_memory_space_constraint(a, pltpu.HBM) for a in lands]

    def body(*refs):
        ins, land = refs[:n], refs[n:2 * n]
        send_sems, recv_sems, token = refs[2 * n + 1], refs[2 * n + 2], refs[-1]
        x, y, c, me_in = _my_position()
        for i in range(n):
            for k in range(1, N_DEV):
                px = 1 - x if k & 4 else x
                py = 1 - y if k & 2 else y
                pc = 1 - c if k & 1 else c
                pltpu.make_async_remote_copy(
                    src_ref=ins[i].at[4 * px + 2 * py + pc] if scatter[i] else ins[i], dst_ref=land[i].at[me_in],
                    send_sem=send_sems.at[i], recv_sem=recv_sems.at[i],
                    device_id=(px, py, pc), device_id_type=MESH).start()
        token[...] = jnp.zeros_like(token)

    outs = pl.pallas_call(
        body, name=name,
        out_shape=(pltpu.SemaphoreType.DMA((n,)), pltpu.SemaphoreType.DMA((n,)),
                   *[pltpu.HBM(a.shape, a.dtype) for a in srcs], *[pltpu.HBM(a.shape, a.dtype) for a in lands],
                   jax.ShapeDtypeStruct((8, 128), F32)),
        in_specs=[HBM_SPEC] * (2 * n) + [ANY],
        out_specs=(SEM_SPEC, SEM_SPEC, *[HBM_SPEC] * (2 * n), pl.BlockSpec(memory_space=pltpu.VMEM)),
        input_output_aliases={i: 2 + i for i in range(2 * n)},
        compiler_params=pltpu.CompilerParams(has_side_effects=EFFECT),
    )(*srcs, *lands, after)
    return (outs[0], outs[1], outs[2:2 + n], outs[2 + n:2 + 2 * n]), outs[-1]


def _exchange_wait(name, handle, after, which=None):
    send_sems, recv_sems, srcs, lands = handle
    which = list(range(len(srcs))) if which is None else list(which)
    srcs, lands = [srcs[i] for i in which], [lands[i] for i in which]
    n = len(srcs)

    def body(*refs):
        land, send_ref, recv_ref = refs[n:2 * n], refs[2 * n], refs[2 * n + 1]
        x, y, c, _ = _my_position()
        for k, i in enumerate(which):
            seven = land[k].at[pl.ds(0, N_DEV - 1)]
            w = pltpu.make_async_remote_copy(src_ref=seven, dst_ref=seven, send_sem=send_ref.at[i], recv_sem=recv_ref.at[i],
                                             device_id=(x, y, c), device_id_type=MESH)
            w.wait_send()
            w.wait_recv()

    outs = pl.pallas_call(
        body, name=name,
        out_shape=(*[pltpu.HBM(a.shape, a.dtype) for a in srcs], *[pltpu.HBM(a.shape, a.dtype) for a in lands]),
        in_specs=[HBM_SPEC] * (2 * n) + [SEM_SPEC, SEM_SPEC, ANY],
        out_specs=tuple([HBM_SPEC] * (2 * n)),
        input_output_aliases={i: i for i in range(2 * n)},
        compiler_params=pltpu.CompilerParams(has_side_effects=EFFECT),
    )(*srcs, *lands, send_sems, recv_sems, after)
    return outs[n:]


def _other_chips(x, y):
    return [(1 - x, y), (x, 1 - y), (1 - x, 1 - y)]


def _hier_gather_start(name, srcs, after, me):
    n = len(srcs)
    lands = _put_mine(name + "_mine", srcs, [False] * n, me)
    srcs = [pltpu.with_memory_space_constraint(a, pltpu.HBM) for a in srcs]
    lands = [pltpu.with_memory_space_constraint(a, pltpu.HBM) for a in lands]

    def body(*refs):
        ins, land = refs[:n], refs[n:2 * n]
        ici_send, ici_recv, d2d_send, d2d_recv = refs[2 * n + 1:2 * n + 5]
        token = refs[-1]
        x, y, c, me_in = _my_position()
        for i in range(n):
            pltpu.make_async_remote_copy(src_ref=ins[i], dst_ref=land[i].at[me_in], send_sem=d2d_send.at[i], recv_sem=d2d_recv.at[i],
                                         device_id=(x, y, 1 - c), device_id_type=MESH).start()
            for px, py in _other_chips(x, y):
                pltpu.make_async_remote_copy(src_ref=ins[i], dst_ref=land[i].at[me_in], send_sem=ici_send.at[i],
                                             recv_sem=ici_recv.at[i], device_id=(px, py, c), device_id_type=MESH).start()
        token[...] = jnp.zeros_like(token)

    sem = pltpu.SemaphoreType.DMA((n,))
    outs = pl.pallas_call(
        body, name=name,
        out_shape=(sem, sem, sem, sem, *[pltpu.HBM(a.shape, a.dtype) for a in srcs], *[pltpu.HBM(a.shape, a.dtype) for a in lands],
                   jax.ShapeDtypeStruct((8, 128), F32)),
        in_specs=[HBM_SPEC] * (2 * n) + [ANY],
        out_specs=(SEM_SPEC,) * 4 + (HBM_SPEC,) * (2 * n) + (pl.BlockSpec(memory_space=pltpu.VMEM),),
        input_output_aliases={i: 4 + i for i in range(2 * n)},
        compiler_params=pltpu.CompilerParams(has_side_effects=EFFECT),
    )(*srcs, *lands, after)
    return (outs[:4], outs[4:4 + n], outs[4 + n:4 + 2 * n]), outs[-1]


def _hier_gather_forward(name, handle, after):
    sems, srcs, lands = handle
    n = len(srcs)

    def body(*refs):
        land = refs[n:2 * n]
        ici_send, ici_recv, d2d_send, d2d_recv = refs[2 * n:2 * n + 4]
        x, y, c, _ = _my_position()
        for i in range(n):
            three = land[i].at[pl.ds(0, 3)]
            pltpu.make_async_remote_copy(src_ref=three, dst_ref=three, send_sem=ici_send.at[i], recv_sem=ici_recv.at[i],
                                         device_id=(x, y, c), device_id_type=MESH).wait_recv()
            for px, py in _other_chips(x, y):
                slab = land[i].at[4 * px + 2 * py + c]
                pltpu.make_async_remote_copy(src_ref=slab, dst_ref=slab, send_sem=d2d_send.at[i], recv_sem=d2d_recv.at[i],
                                             device_id=(x, y, 1 - c), device_id_type=MESH).start()

    outs = pl.pallas_call(
        body, name=name,
        out_shape=(*[pltpu.HBM(a.shape, a.dtype) for a in srcs], *[pltpu.HBM(a.shape, a.dtype) for a in lands]),
        in_specs=[HBM_SPEC] * (2 * n) + [SEM_SPEC] * 4 + [ANY],
        out_specs=tuple([HBM_SPEC] * (2 * n)),
        input_output_aliases={i: i for i in range(2 * n)},
        compiler_params=pltpu.CompilerParams(has_side_effects=EFFECT),
    )(*srcs, *lands, *sems, after)
    return (sems, outs[:n], outs[n:])


def _hier_gather_wait(name, handle, after):
    sems, srcs, lands = handle
    n = len(srcs)

    def body(*refs):
        land = refs[n:2 * n]
        ici_send, ici_recv, d2d_send, d2d_recv = refs[2 * n:2 * n + 4]
        x, y, c, _ = _my_position()
        for i in range(n):
            three, four = land[i].at[pl.ds(0, 3)], land[i].at[pl.ds(0, 4)]
            pltpu.make_async_remote_copy(src_ref=three, dst_ref=three, send_sem=ici_send.at[i], recv_sem=ici_recv.at[i],
                                         device_id=(x, y, c), device_id_type=MESH).wait_send()
            w = pltpu.make_async_remote_copy(src_ref=four, dst_ref=four, send_sem=d2d_send.at[i], recv_sem=d2d_recv.at[i],
                                             device_id=(x, y, c), device_id_type=MESH)
            w.wait_send()
            w.wait_recv()

    outs = pl.pallas_call(
        body, name=name,
        out_shape=(*[pltpu.HBM(a.shape, a.dtype) for a in srcs], *[pltpu.HBM(a.shape, a.dtype) for a in lands]),
        in_specs=[HBM_SPEC] * (2 * n) + [SEM_SPEC] * 4 + [ANY],
        out_specs=tuple([HBM_SPEC] * (2 * n)),
        input_output_aliases={i: i for i in range(2 * n)},
        compiler_params=pltpu.CompilerParams(has_side_effects=EFFECT),
    )(*srcs, *lands, *sems, after)
    return outs[n:]


def _pack(arrs):
    flat = jnp.concatenate([a.reshape(-1).astype(F32) for a in arrs])
    n = flat.shape[0]
    rows = -(-n // 1024) * 8
    return jnp.pad(flat, (0, rows * 128 - n)).reshape(rows, 128)


def _unpack(buf, shapes, lead=()):
    flat = buf.reshape(lead + (-1,))
    out, off = [], 0
    for s in shapes:
        n = math.prod(s)
        out.append(flat[..., off:off + n].reshape(lead + tuple(s)))
        off += n
    return out


def _mm(name, a, a_spec, b, b_spec, out_sds, o_spec, grid, contract, nk=1, stacked=0):
    o_blk = tuple(d for d in o_spec.block_shape if d is not None)

    def body(a_ref, b_ref, o_ref, *acc):
        if stacked:
            r = _dot(a_ref[0], b_ref[0], contract)
            for q in range(1, stacked):
                r = r + _dot(a_ref[q], b_ref[q], contract)
        else:
            r = _dot(a_ref[...], b_ref[...], contract)
        if nk == 1:
            o_ref[...] = r.astype(o_ref.dtype)
        else:
            k = pl.program_id(len(grid) - 1)

            @pl.when(k == 0)
            def _():
                acc[0][...] = r

            @pl.when(k > 0)
            def _():
                acc[0][...] += r

            @pl.when(k == nk - 1)
            def _():
                o_ref[...] = acc[0][...].astype(o_ref.dtype)

    sem = ("parallel",) * (len(grid) - 1) + (("arbitrary",) if nk > 1 else ("parallel",))
    return pl.pallas_call(
        body, name=name, out_shape=out_sds, grid=grid, in_specs=[a_spec, b_spec], out_specs=o_spec,
        scratch_shapes=[pltpu.VMEM(o_blk, F32)] if nk > 1 else [], compiler_params=_cp(*sem))(a, b)


def _tile(n, want):
    t = min(n, want)
    assert n % t == 0, (n, t)
    return t


def _mm_nn(name, a, b, out_dtype=F32, tm=512, tn=512):
    (M, K), N = a.shape, b.shape[1]
    tm, tn = _tile(M, tm), _tile(N, tn)
    return _mm(name, a, pl.BlockSpec((tm, K), lambda i, j: (i, 0)), b, pl.BlockSpec((K, tn), lambda i, j: (0, j)),
               jax.ShapeDtypeStruct((M, N), out_dtype), pl.BlockSpec((tm, tn), lambda i, j: (i, j)),
               (M // tm, N // tn), "nn")


def _mm_nt(name, a, b, out_dtype=F32, tm=512, tn=512):
    (M, K), N = a.shape, b.shape[0]
    tm, tn = _tile(M, tm), _tile(N, tn)
    return _mm(name, a, pl.BlockSpec((tm, K), lambda i, j: (i, 0)), b, pl.BlockSpec((tn, K), lambda i, j: (j, 0)),
               jax.ShapeDtypeStruct((M, N), out_dtype), pl.BlockSpec((tm, tn), lambda i, j: (i, j)),
               (M // tm, N // tn), "nt")


def _mm_tn(name, a, b, out_dtype=F32, tm=512, tn=512):
    (K, M), N = a.shape, b.shape[1]
    tm, tn = _tile(M, tm), _tile(N, tn)
    return _mm(name, a, pl.BlockSpec((K, tm), lambda i, j: (0, i)), b, pl.BlockSpec((K, tn), lambda i, j: (0, j)),
               jax.ShapeDtypeStruct((M, N), out_dtype), pl.BlockSpec((tm, tn), lambda i, j: (i, j)),
               (M // tm, N // tn), "tn")


def _mm_cols(name, a, w, out_dtype=F32, tm=512):
    (M, K), (J, _, n) = a.shape, w.shape
    tm = _tile(M, tm)
    return _mm(name, a, pl.BlockSpec((tm, K), lambda j, i: (i, 0)), w, pl.BlockSpec((None, K, n), lambda j, i: (j, 0, 0)),
               jax.ShapeDtypeStruct((J, M, n), out_dtype), pl.BlockSpec((None, tm, n), lambda j, i: (j, i, 0)),
               (J, M // tm), "nn")


def _mm_cols_dx(name, d, w, out_dtype=F32, tm=512, jb=None):
    (J, M, n), K = d.shape, w.shape[1]
    tm, jb = _tile(M, tm), J if jb is None else jb
    return _mm(name, d, pl.BlockSpec((jb, tm, n), lambda i, j: (j, i, 0)), w, pl.BlockSpec((jb, K, n), lambda i, j: (j, 0, 0)),
               jax.ShapeDtypeStruct((M, K), out_dtype), pl.BlockSpec((tm, K), lambda i, j: (i, 0)),
               (M // tm, J // jb), "nt", nk=J // jb, stacked=jb)


def _mm_cols_dw(name, a, d, out_dtype=F32, tk=512):
    (M, K), (J, _, n) = a.shape, d.shape
    tk = _tile(K, tk)
    return _mm(name, a, pl.BlockSpec((M, tk), lambda j, i: (0, i)), d, pl.BlockSpec((None, M, n), lambda j, i: (j, 0, 0)),
               jax.ShapeDtypeStruct((J, K, n), out_dtype), pl.BlockSpec((None, tk, n), lambda j, i: (j, i, 0)),
               (J, K // tk), "tn")


def _mm_cols_dwt(name, a, d, out_dtype=F32, tk=512):
    (M, K), (J, _, n) = a.shape, d.shape
    tk = _tile(K, tk)
    return _mm(name, d, pl.BlockSpec((None, M, n), lambda j, i: (j, 0, 0)), a, pl.BlockSpec((M, tk), lambda j, i: (0, i)),
               jax.ShapeDtypeStruct((J, n, K), out_dtype), pl.BlockSpec((None, n, tk), lambda j, i: (j, 0, i)),
               (J, K // tk), "tn")


def _mm_rows_resid(name, a, w, resid, gate, tm=512):
    (Q, M, k), N = a.shape, w.shape[2]
    tm = _tile(M, tm)

    def body(a_ref, w_ref, r_ref, g_ref, y_ref, x_ref):
        y = _dot(a_ref[0], w_ref[0], "nn")
        for q in range(1, Q):
            y = y + _dot(a_ref[q], w_ref[q], "nn")
        y_ref[...] = y.astype(y_ref.dtype)
        x_ref[...] = r_ref[...] + g_ref[...] * y

    return pl.pallas_call(
        body, name=name, grid=(M // tm,),
        out_shape=(jax.ShapeDtypeStruct((M, N), ACT_DTYPE), jax.ShapeDtypeStruct((M, N), F32)),
        in_specs=[pl.BlockSpec((Q, tm, k), lambda i: (0, i, 0)), pl.BlockSpec((Q, k, N), lambda i: (0, 0, 0)),
                  pl.BlockSpec((tm, N), lambda i: (i, 0)), pl.BlockSpec((1, N), lambda i: (0, 0))],
        out_specs=(pl.BlockSpec((tm, N), lambda i: (i, 0)), pl.BlockSpec((tm, N), lambda i: (i, 0))),
        compiler_params=_cp("parallel"))(a, w, resid, gate)


def _mm_rows_dx(name, d, w, out_dtype=F32, tm=512):
    (M, N), (Q, k, _) = d.shape, w.shape
    tm = _tile(M, tm)
    return _mm(name, d, pl.BlockSpec((tm, N), lambda q, i: (i, 0)), w, pl.BlockSpec((None, k, N), lambda q, i: (q, 0, 0)),
               jax.ShapeDtypeStruct((Q, M, k), out_dtype), pl.BlockSpec((None, tm, k), lambda q, i: (q, i, 0)),
               (Q, M // tm), "nt")


def _mm_rows_dw(name, a, d, out_dtype=F32, tn=512):
    (Q, M, k), N = a.shape, d.shape[1]
    tn = _tile(N, tn)
    return _mm(name, a, pl.BlockSpec((None, M, k), lambda q, j: (q, 0, 0)), d, pl.BlockSpec((M, tn), lambda q, j: (0, j)),
               jax.ShapeDtypeStruct((Q, k, N), out_dtype), pl.BlockSpec((None, k, tn), lambda q, j: (q, 0, j)),
               (Q, N // tn), "tn")


def _silu(v):
    return v * jax.nn.sigmoid(v)


def _ada_fwd(c16, ada_w):
    L, D, n = ada_w.shape

    def body(c_ref, w_ref, o_ref):
        o_ref[...] = _dot(_silu(c_ref[...]), w_ref[...], "nn")

    return pl.pallas_call(
        body, name="ada_fwd", grid=(L,), out_shape=jax.ShapeDtypeStruct((L, 16, n), F32),
        in_specs=[pl.BlockSpec((16, D), lambda l: (0, 0)), pl.BlockSpec((None, D, n), lambda l: (l, 0, 0))],
        out_specs=pl.BlockSpec((None, 16, n), lambda l: (l, 0, 0)), compiler_params=_cp("parallel"))(c16, ada_w)


def _ada_bwd(c16, dmod16):
    L, _, n = dmod16.shape
    D = c16.shape[1]

    def body(c_ref, d_ref, o_ref):
        o_ref[...] = _dot(_silu(c_ref[...]), d_ref[...], "tn")

    return pl.pallas_call(
        body, name="ada_bwd", grid=(L,), out_shape=jax.ShapeDtypeStruct((L, D, n), F32),
        in_specs=[pl.BlockSpec((16, D), lambda l: (0, 0)), pl.BlockSpec((None, 16, n), lambda l: (l, 0, 0))],
        out_specs=pl.BlockSpec((None, D, n), lambda l: (l, 0, 0)), compiler_params=_cp("parallel"))(c16, dmod16)


def _row_spec(tr, n):
    return pl.BlockSpec((tr, n), lambda i: (i, 0))


def _vec_spec(n):
    return pl.BlockSpec((1, n), lambda i: (0, 0))


def _rmsmod_fwd(name, x, g, sc, sh, after, tr=512):
    S, D = x.shape

    def body(x_ref, g_ref, sc_ref, sh_ref, after_ref, h_ref):
        xv = x_ref[...]
        rstd = lax.rsqrt(jnp.mean(xv * xv, axis=-1, keepdims=True) + EPS)
        y = xv * rstd * g_ref[...]
        h_ref[...] = (y * (1.0 + sc_ref[...]) + sh_ref[...]).astype(h_ref.dtype)

    return pl.pallas_call(
        body, name=name, grid=(S // tr,), out_shape=jax.ShapeDtypeStruct((S, D), _MXU_DTYPE),
        in_specs=[_row_spec(tr, D), _vec_spec(D), _vec_spec(D), _vec_spec(D), ANY], out_specs=_row_spec(tr, D),
        compiler_params=_cp("parallel"))(x, g, sc, sh, after)


def _acc_rows(ref, val, first):
    s = jnp.sum(val, axis=0, keepdims=True)

    @pl.when(first)
    def _():
        ref[...] = s

    @pl.when(jnp.logical_not(first))
    def _():
        ref[...] += s


def _gate_bwd_tail(dx, y_ref, gate_ref, dy_ref, dgate_ref, first):
    dy_ref[...] = (gate_ref[...] * dx).astype(dy_ref.dtype)
    _acc_rows(dgate_ref, dx * y_ref[...].astype(F32), first)


def _rmsmod_bwd(name, x, g, sc, dh, dres, after, y=None, gate=None, tr=512):
    S, D = x.shape
    tail = y is not None

    def body(x_ref, g_ref, sc_ref, dh_ref, dres_ref, after_ref, *rest):
        (y_ref, gate_ref), rest = (rest[:2], rest[2:]) if tail else ((None, None), rest)
        dx_ref, dg_ref, dsc_ref, dsh_ref = rest[:4]
        first = pl.program_id(0) == 0
        xv, dh_v, gv = x_ref[...], dh_ref[...].astype(F32), g_ref[...]
        rstd = lax.rsqrt(jnp.mean(xv * xv, axis=-1, keepdims=True) + EPS)
        xhat = xv * rstd
        _acc_rows(dsh_ref, dh_v, first)
        _acc_rows(dsc_ref, dh_v * (xhat * gv), first)
        dyg = dh_v * (1.0 + sc_ref[...])
        _acc_rows(dg_ref, dyg * xhat, first)
        dxhat = dyg * gv
        dx = dres_ref[...] + rstd * (dxhat - xhat * jnp.mean(dxhat * xhat, axis=-1, keepdims=True))
        dx_ref[...] = dx
        if tail:
            _gate_bwd_tail(dx, y_ref, gate_ref, rest[4], rest[5], first)

    vec = jax.ShapeDtypeStruct((1, D), F32)
    return pl.pallas_call(
        body, name=name, grid=(S // tr,),
        out_shape=(jax.ShapeDtypeStruct((S, D), F32), vec, vec, vec) + ((jax.ShapeDtypeStruct((S, D), _MXU_DTYPE), vec) if tail else ()),
        in_specs=[_row_spec(tr, D), _vec_spec(D), _vec_spec(D), _row_spec(tr, D), _row_spec(tr, D), ANY]
        + ([_row_spec(tr, D), _vec_spec(D)] if tail else []),
        out_specs=(_row_spec(tr, D), _vec_spec(D), _vec_spec(D), _vec_spec(D)) + ((_row_spec(tr, D), _vec_spec(D)) if tail else ()),
        compiler_params=_cp("arbitrary"))(x, g, sc, dh, dres, after, *((y, gate) if tail else ()))


def _loss_head(x, g, target, y, gate, tr=512):
    S, D = x.shape

    def body(x_ref, g_ref, t_ref, y_ref, gate_ref, loss_ref, dx_ref, dg_ref, dy_ref, dgate_ref):
        first = pl.program_id(0) == 0
        xv, gv = x_ref[...], g_ref[...]
        rstd = lax.rsqrt(jnp.mean(xv * xv, axis=-1, keepdims=True) + EPS)
        xhat = xv * rstd
        err = xhat * gv - t_ref[...]
        part = 0.5 * jnp.sum(jnp.mean(err * err, axis=-1, keepdims=True), axis=0, keepdims=True)

        @pl.when(first)
        def _():
            loss_ref[...] = part

        @pl.when(jnp.logical_not(first))
        def _():
            loss_ref[...] += part

        dout = err * (1.0 / D)
        _acc_rows(dg_ref, dout * xhat, first)
        dxhat = dout * gv
        dx = rstd * (dxhat - xhat * jnp.mean(dxhat * xhat, axis=-1, keepdims=True))
        dx_ref[...] = dx
        _gate_bwd_tail(dx, y_ref, gate_ref, dy_ref, dgate_ref, first)

    vec = jax.ShapeDtypeStruct((1, D), F32)
    return pl.pallas_call(
        body, name="loss_head", grid=(S // tr,),
        out_shape=(jax.ShapeDtypeStruct((1, 1), F32), jax.ShapeDtypeStruct((S, D), F32), vec,
                   jax.ShapeDtypeStruct((S, D), _MXU_DTYPE), vec),
        in_specs=[_row_spec(tr, D), _vec_spec(D), _row_spec(tr, D), _row_spec(tr, D), _vec_spec(D)],
        out_specs=(pl.BlockSpec((1, 1), lambda i: (0, 0)), _row_spec(tr, D), _vec_spec(D), _row_spec(tr, D), _vec_spec(D)),
        compiler_params=_cp("arbitrary"))(x, g, target, y, gate)


def _shift_down(v, k):
    t = lax.broadcasted_iota(jnp.int32, v.shape, 0)
    return jnp.where(t >= k, pltpu.roll(v, k, axis=0), 0.0)


def _shift_up(v, k):
    n = v.shape[0]
    t = lax.broadcasted_iota(jnp.int32, v.shape, 0)
    return jnp.where(t < n - k, pltpu.roll(v, n - k, axis=0), 0.0)


def _window_sum(p, w, shift):
    s, k = p, 1
    while k < w:
        s = s + shift(s, k)
        k *= 2
    return s


def _pool_count(shape, w):
    t = lax.broadcasted_iota(jnp.int32, shape, 0)
    return jnp.minimum(t + 1, w).astype(F32)


def _ab_specs(S):
    zs = [pl.BlockSpec((None, S, 128), functools.partial(lambda g, q: (2 * q + g // 2, 0, g % 2), q=q)) for q in range(4)]
    return zs


def _ab_mix_fwd(z8, conv_w, mix_w, scale):
    S = z8.shape[1]

    def body(b_ref, c_ref, a_ref, p_ref, w_ref, mix_ref, sc_ref, y_ref):
        g = pl.program_id(0)
        cg = c_ref[...].astype(F32) * a_ref[...].astype(F32)
        w = w_ref[...]
        conv = w[0:1] * _shift_down(cg, 2) + w[1:2] * _shift_down(cg, 1) + w[2:3] * cg
        y_ref[0] = (b_ref[...].astype(F32) * conv).astype(y_ref.dtype)
        for gg, win in enumerate(POOL_WINDOWS):
            @pl.when(g == gg)
            def _(win=win):
                p = p_ref[...].astype(F32)
                pooled = _window_sum(p, win, _shift_down) / _pool_count(p.shape, win) - p
                y_ref[1] = (_dot(pooled, mix_ref[...], "nn") * sc_ref[...]).astype(y_ref.dtype)

    return pl.pallas_call(
        body, name="ab_mix_fwd", grid=(4,), out_shape=jax.ShapeDtypeStruct((2, S, 512), _MXU_DTYPE),
        in_specs=_ab_specs(S) + [pl.BlockSpec((3, 128), lambda g: (0, g)), pl.BlockSpec((None, 128, 128), lambda g: (g, 0, 0)),
                                 pl.BlockSpec((1, 128), lambda g: (0, g))],
        out_specs=pl.BlockSpec((2, S, 128), lambda g: (0, 0, g)), compiler_params=_cp("parallel"))(z8, z8, z8, z8, conv_w, mix_w, scale)


def _ab_mix_bwd(z8, dycat2, conv_w, mix_w, scale, after):
    S = z8.shape[1]

    def body(b_ref, c_ref, a_ref, p_ref, dy_ref, w_ref, mix_ref, sc_ref, after_ref, dz_ref, dw_ref, dmix_ref, dsc_ref):
        g = pl.program_id(0)
        bv, cv, av, w = b_ref[...].astype(F32), c_ref[...].astype(F32), a_ref[...].astype(F32), w_ref[...]
        dya = dy_ref[0]
        cg = cv * av
        cg1, cg2 = _shift_down(cg, 1), _shift_down(cg, 2)
        conv = w[0:1] * cg2 + w[1:2] * cg1 + w[2:3] * cg
        dz_ref[0] = (dya * conv).astype(dz_ref.dtype)
        dconv = dya * bv
        dcg = w[2:3] * dconv + w[1:2] * _shift_up(dconv, 1) + w[0:1] * _shift_up(dconv, 2)
        dz_ref[1] = (dcg * av).astype(dz_ref.dtype)
        dz_ref[2] = (dcg * cv).astype(dz_ref.dtype)
        dw_ref[0:1, :] = jnp.sum(dconv * cg2, axis=0, keepdims=True)
        dw_ref[1:2, :] = jnp.sum(dconv * cg1, axis=0, keepdims=True)
        dw_ref[2:3, :] = jnp.sum(dconv * cg, axis=0, keepdims=True)
        for gg, win in enumerate(POOL_WINDOWS):
            @pl.when(g == gg)
            def _(win=win):
                p, dyb, mix = p_ref[...].astype(F32), dy_ref[1], mix_ref[...]
                cnt = _pool_count(p.shape, win)
                pooled = _window_sum(p, win, _shift_down) / cnt - p
                dsc_ref[...] = jnp.sum(dyb * _dot(pooled, mix, "nn"), axis=0, keepdims=True)
                dmixed = dyb * sc_ref[...]
                dmix_ref[...] = _dot(pooled, dmixed, "tn")
                dpooled = _dot(dmixed, mix, "nt")
                dz_ref[3] = (_window_sum(dpooled / cnt, win, _shift_up) - dpooled).astype(dz_ref.dtype)

    return pl.pallas_call(
        body, name="ab_mix_bwd", grid=(4,),
        out_shape=(jax.ShapeDtypeStruct((4, 2, S, 256), _MXU_DTYPE), jax.ShapeDtypeStruct((3, 512), F32),
                   jax.ShapeDtypeStruct((4, 128, 128), F32), jax.ShapeDtypeStruct((1, 512), F32)),
        in_specs=_ab_specs(S) + [pl.BlockSpec((2, S, 128), lambda g: (0, 0, g)), pl.BlockSpec((3, 128), lambda g: (0, g)),
                                 pl.BlockSpec((None, 128, 128), lambda g: (g, 0, 0)), pl.BlockSpec((1, 128), lambda g: (0, g)), ANY],
        out_specs=(pl.BlockSpec((4, None, S, 128), lambda g: (0, g // 2, 0, g % 2)), pl.BlockSpec((3, 128), lambda g: (0, g)),
                   pl.BlockSpec((None, 128, 128), lambda g: (g, 0, 0)), pl.BlockSpec((1, 128), lambda g: (0, g))),
        compiler_params=_cp("parallel"))(z8, z8, z8, z8, dycat2, conv_w, mix_w, scale, after)


HALO = 16


def _ffn_specs(S, n, tr):
    nb = S // HALO
    tile = pl.BlockSpec((2, None, tr, n), lambda j, i: (0, j, i, 0))
    prev = pl.BlockSpec((2, None, HALO, n), lambda j, i: (0, j, jnp.maximum(i * (tr // HALO) - 1, 0), 0))
    nxt = pl.BlockSpec((2, None, HALO, n), lambda j, i: (0, j, jnp.minimum((i + 1) * (tr // HALO), nb - 1), 0))
    cw = pl.BlockSpec((2, None, 3, n), lambda j, i: (0, j, 0, 0))
    return tile, prev, nxt, cw


def _shifted_rows(ext, lo, rows):
    ext = ext.astype(F32)
    return pltpu.roll(ext, 1, axis=0)[lo:lo + rows], pltpu.roll(ext, 2, axis=0)[lo:lo + rows]


def _ffn_gate_fwd(name, u24, cw24, tr=256):
    _, J, S, n = u24.shape
    tile, prev, _, cw = _ffn_specs(S, n, tr)

    def body(u_ref, up_ref, w_ref, a_ref, z_ref):
        keep = (pl.program_id(1) > 0).astype(u_ref.dtype)
        z = []
        for h in range(2):
            ext = jnp.concatenate([up_ref[h] * keep, u_ref[h]], axis=0)
            x1, x2 = _shifted_rows(ext, HALO, tr)
            w = w_ref[h]
            z.append(w[0:1] * x2 + w[1:2] * x1 + w[2:3] * u_ref[h].astype(F32))
        zg, zu = z
        sg = jax.nn.sigmoid(zg)
        silu = zg * sg
        a_ref[...] = (silu * zu).astype(a_ref.dtype)
        z_ref[0] = (zu * (sg * (1.0 + zg * (1.0 - sg)))).astype(z_ref.dtype)
        z_ref[1] = silu.astype(z_ref.dtype)

    return pl.pallas_call(
        body, name=name, grid=(J, S // tr),
        out_shape=(jax.ShapeDtypeStruct((J, S, n), _MXU_DTYPE), jax.ShapeDtypeStruct((2, J, S, n), ACT_DTYPE)),
        in_specs=[tile, prev, cw], out_specs=(pl.BlockSpec((None, tr, n), lambda j, i: (j, i, 0)), tile),
        compiler_params=_cp("parallel", "parallel"))(u24, u24, cw24)


def _ffn_gate_bwd(name, u24, z24, cw24, da4, w_up24, after, tr=256):
    _, J, S, n = u24.shape
    K = w_up24.shape[2]
    nb = S // HALO
    tile = pl.BlockSpec((2, None, tr, n), lambda i, j: (0, j, i, 0))
    nxt = pl.BlockSpec((2, None, HALO, n), lambda i, j: (0, j, jnp.minimum((i + 1) * (tr // HALO), nb - 1), 0))
    whole = lambda shape: pl.BlockSpec(shape, lambda i, j: (0,) * len(shape))

    def body(u_ref, z_ref, zn_ref, cw_ref, da_ref, dan_ref, wup_ref, after_ref, du_ref, dcw_ref, dh_ref, acc_ref):
        i, j = pl.program_id(0), pl.program_id(1)
        first = i == 0
        keep_next = (i < S // tr - 1).astype(F32)
        w = [cw_ref[h, j] for h in range(2)]
        m = tr + HALO
        da = jnp.concatenate([da_ref[...].astype(F32), dan_ref[...].astype(F32) * keep_next], axis=0)
        dz = [da * jnp.concatenate([z_ref[h], zn_ref[h]], axis=0).astype(F32) for h in range(2)]
        dh = None
        for h in range(2):
            d = dz[h]
            d0, d1, d2 = d[:tr], pltpu.roll(d, m - 1, axis=0)[:tr], pltpu.roll(d, m - 2, axis=0)[:tr]
            du = (w[h][2:3] * d0 + w[h][1:2] * d1 + w[h][0:1] * d2).astype(du_ref.dtype)
            du_ref[h] = du
            part = _dot(du, wup_ref[h, j], "nt")
            dh = part if dh is None else dh + part
            x0 = u_ref[h].astype(F32)
            parts = [jnp.sum(x0 * dk, axis=0, keepdims=True) for dk in (d2, d1, d0)]
            for k in range(3):
                @pl.when(first)
                def _(k=k, h=h):
                    dcw_ref[h, j, k:k + 1, :] = parts[k]

                @pl.when(jnp.logical_not(first))
                def _(k=k, h=h):
                    dcw_ref[h, j, k:k + 1, :] += parts[k]

        @pl.when(j == 0)
        def _():
            acc_ref[...] = dh

        @pl.when(j > 0)
        def _():
            acc_ref[...] += dh

        @pl.when(j == J - 1)
        def _():
            dh_ref[...] = acc_ref[...].astype(dh_ref.dtype)

    da_tile = pl.BlockSpec((None, tr, n), lambda i, j: (j, i, 0))
    da_next = pl.BlockSpec((None, HALO, n), lambda i, j: (j, jnp.minimum((i + 1) * (tr // HALO), nb - 1), 0))
    return pl.pallas_call(
        body, name=name, grid=(S // tr, J),
        out_shape=(jax.ShapeDtypeStruct((2, J, S, n), _MXU_DTYPE), jax.ShapeDtypeStruct((2, J, 3, n), F32),
                   jax.ShapeDtypeStruct((S, K), ACT_DTYPE)),
        in_specs=[tile, tile, nxt, whole((2, J, 3, n)), da_tile, da_next, whole((2, J, K, n)), ANY],
        out_specs=(tile, whole((2, J, 3, n)), pl.BlockSpec((tr, K), lambda i, j: (i, 0))),
        scratch_shapes=[pltpu.VMEM((tr, K), F32)],
        compiler_params=_cp("arbitrary", "arbitrary"))(u24, z24, z24, cw24, da4, da4, w_up24, after)


def _rms_rows(v, g):
    rstd = lax.rsqrt(jnp.mean(v * v, axis=-1, keepdims=True) + EPS)
    return v * rstd * g


def _rms_rows_bwd(v, g, dy):
    rstd = lax.rsqrt(jnp.mean(v * v, axis=-1, keepdims=True) + EPS)
    vhat = v * rstd
    dvhat = dy * g
    return rstd * (dvhat - vhat * jnp.mean(dvhat * vhat, axis=-1, keepdims=True)), dy * vhat


def _rope(v, cos, sa, sb):
    return v * cos + pltpu.roll(v, 112, axis=1) * sa + pltpu.roll(v, 16, axis=1) * sb


def _rope_t(d, cos, sa, sb):
    return d * cos + pltpu.roll(d * sa, 16, axis=1) + pltpu.roll(d * sb, 112, axis=1)


def _qkv_rope_fwd(z, qg, kvg, w_uq_t, w_kv, cosq, cosk, sa, sb, tr=512):
    S = z.shape[0]

    def body(ql_ref, kvl_ref, kpe_ref, qg_ref, kvg_ref, wq_ref, wkv_ref, cq_ref, ck_ref, sa_ref, sb_ref,
             qn_ref, kvn_ref, qo_ref, ko_ref, vo_ref):
        cq, ck, sa_v, sb_v = cq_ref[...], ck_ref[...], sa_ref[...], sb_ref[...]
        qn = _rms_rows(ql_ref[...], qg_ref[...]).astype(qn_ref.dtype)
        kvn = _rms_rows(kvl_ref[...], kvg_ref[...]).astype(kvn_ref.dtype)
        qn_ref[...] = qn
        kvn_ref[...] = kvn
        q = _dot(qn, wq_ref[...], "nt")
        kv = _dot(kvn, wkv_ref[...], "nn")
        kpe = _rope(kpe_ref[...], ck, sa_v, sb_v)
        for h in range(8):
            cols = slice(128 * h, 128 * h + 128)
            qo_ref[:, cols] = _rope(q[:, cols], cq, sa_v, sb_v).astype(qo_ref.dtype)
            ko_ref[:, cols] = (kv[:, cols] + kpe).astype(ko_ref.dtype)
        vo_ref[...] = kv[:, 1024:1536].astype(vo_ref.dtype)

    tab = _row_spec(tr, 128)
    whole = lambda a: pl.BlockSpec(a.shape, lambda i: (0, 0))
    return pl.pallas_call(
        body, name="qkv_rope_fwd", grid=(S // tr,),
        out_shape=(jax.ShapeDtypeStruct((S, 256), _MXU_DTYPE), jax.ShapeDtypeStruct((S, 128), _MXU_DTYPE),
                   jax.ShapeDtypeStruct((S, 1024), _MXU_DTYPE), jax.ShapeDtypeStruct((S, 1024), _MXU_DTYPE),
                   jax.ShapeDtypeStruct((S, 512), _MXU_DTYPE)),
        in_specs=[pl.BlockSpec((tr, 256), lambda i: (i, 0)), pl.BlockSpec((tr, 128), lambda i: (i, 2)),
                  pl.BlockSpec((tr, 128), lambda i: (i, 3)), _vec_spec(256), _vec_spec(128), whole(w_uq_t), whole(w_kv),
                  tab, tab, tab, tab],
        out_specs=(_row_spec(tr, 256), _row_spec(tr, 128), _row_spec(tr, 1024), _row_spec(tr, 1024), _row_spec(tr, 512)),
        compiler_params=_cp("parallel"))(z, z, z, qg, kvg, w_uq_t, w_kv, cosq, cosk, sa, sb)


def _attn_bwd_prep(o, dycat2, tr=512):
    S = o.shape[0]

    def body(o_ref, do_ref, delta_ref, doa_ref, dob_ref):
        do = do_ref[...]
        prod = do * o_ref[...]
        lane = lax.broadcasted_iota(jnp.int32, do.shape, 1)
        for p in range(4):
            cols = slice(128 * p, 128 * p + 128)
            first = lane[:, cols] < 128 * p + 64
            da = jnp.sum(jnp.where(first, prod[:, cols], 0.0), axis=-1, keepdims=True)
            db = jnp.sum(jnp.where(first, 0.0, prod[:, cols]), axis=-1, keepdims=True)
            delta_ref[p] = jnp.where(first, da, db)
            doa_ref[p] = jnp.where(first, do[:, cols], 0.0).astype(doa_ref.dtype)
            dob_ref[p] = jnp.where(first, 0.0, do[:, cols]).astype(dob_ref.dtype)

    pair = pl.BlockSpec((4, tr, 128), lambda i: (0, i, 0))
    return pl.pallas_call(
        body, name="attn_bwd_prep", grid=(S // tr,),
        out_shape=(jax.ShapeDtypeStruct((4, S, 128), F32), jax.ShapeDtypeStruct((4, S, 128), _MXU_DTYPE),
                   jax.ShapeDtypeStruct((4, S, 128), _MXU_DTYPE)),
        in_specs=[_row_spec(tr, 512), pl.BlockSpec((None, tr, 512), lambda i: (0, i, 0))],
        out_specs=(pair, pair, pair), compiler_params=_cp("parallel"))(o, dycat2)


def _qkv_rope_bwd(z, qg, kvg, dq, dk, dv, duv, w_uq_t, w_kv, cosq, cosk, sa, sb, tr=512):
    S = z.shape[0]

    def body(ql_ref, kvl_ref, qg_ref, kvg_ref, dq_ref, dk_ref, dv_ref, duv_ref, wq_ref, wkv_ref, cq_ref, ck_ref, sa_ref, sb_ref,
             dqo_ref, dkv_ref, dz_ref, dqg_ref, dkvg_ref):
        first = pl.program_id(0) == 0
        cq, ck, sa_v, sb_v = cq_ref[...], ck_ref[...], sa_ref[...], sb_ref[...]
        tot = jnp.zeros((tr, 128), F32)
        for h in range(8):
            cols = slice(128 * h, 128 * h + 128)
            dqo_ref[:, cols] = _rope_t(dq_ref[:, cols], cq, sa_v, sb_v).astype(dqo_ref.dtype)
            dkh = dk_ref[:, cols]
            tot = tot + dkh
            dkv_ref[:, cols] = dkh.astype(dkv_ref.dtype)
        dkv_ref[:, 1024:1536] = dv_ref[...].astype(dkv_ref.dtype)
        dqn = _dot(dqo_ref[...], wq_ref[...], "nn")
        dkvn = _dot(dkv_ref[...], wkv_ref[...], "nt")
        dql, dqg = _rms_rows_bwd(ql_ref[...], qg_ref[...], dqn)
        dkvl, dkvg = _rms_rows_bwd(kvl_ref[...], kvg_ref[...], dkvn)
        _acc_rows(dqg_ref, dqg, first)
        _acc_rows(dkvg_ref, dkvg, first)
        dz_ref[:, 0:256] = dql.astype(dz_ref.dtype)
        dz_ref[:, 256:384] = dkvl.astype(dz_ref.dtype)
        dz_ref[:, 384:512] = _rope_t(tot, ck, sa_v, sb_v).astype(dz_ref.dtype)
        dz_ref[:, 512:1536] = duv_ref[...].astype(dz_ref.dtype)

    tab = _row_spec(tr, 128)
    whole = lambda a: pl.BlockSpec(a.shape, lambda i: (0, 0))
    return pl.pallas_call(
        body, name="qkv_rope_bwd", grid=(S // tr,),
        out_shape=(jax.ShapeDtypeStruct((S, 1024), _MXU_DTYPE), jax.ShapeDtypeStruct((S, 1536), _MXU_DTYPE),
                   jax.ShapeDtypeStruct((S, 1536), _MXU_DTYPE), jax.ShapeDtypeStruct((1, 256), F32), jax.ShapeDtypeStruct((1, 128), F32)),
        in_specs=[pl.BlockSpec((tr, 256), lambda i: (i, 0)), pl.BlockSpec((tr, 128), lambda i: (i, 2)), _vec_spec(256), _vec_spec(128),
                  _row_spec(tr, 1024), _row_spec(tr, 1024), _row_spec(tr, 512), _row_spec(tr, 1024), whole(w_uq_t), whole(w_kv),
                  tab, tab, tab, tab],
        out_specs=(_row_spec(tr, 1024), _row_spec(tr, 1536), _row_spec(tr, 1536), _vec_spec(256), _vec_spec(128)),
        compiler_params=_cp("arbitrary"))(z, z, qg, kvg, dq, dk, dv, duv, w_uq_t, w_kv, cosq, cosk, sa, sb)


NEG = -1e30


def _attn_fwd(q, k, v, tq=512, tk=512):
    S = q.shape[0]
    assert tq == tk

    def body(q_ref, k_ref, v_ref, o_ref, lse_ref):
        i = pl.program_id(1)
        qs = [q_ref[:, 0:128], q_ref[:, 128:256]]

        def step(kb, carry, diagonal=False):
            start = pl.multiple_of(kb * tk, tk)
            vv = v_ref[pl.ds(start, tk), :]
            out = []
            for h in range(2):
                m, l, acc = carry[3 * h:3 * h + 3]
                s = _dot(qs[h], k_ref[pl.ds(start, tk), 128 * h:128 * h + 128], "nt") * ATTN_SCALE
                if diagonal:
                    s = jnp.where(below, s, NEG)
                m_new = jnp.maximum(m, jnp.max(s, axis=-1, keepdims=True))
                alpha = jnp.exp(m - m_new)
                p = jnp.exp(s - m_new)
                out += [m_new, alpha * l + jnp.sum(p, axis=-1, keepdims=True), alpha * acc + _dot(p, vv, "nn")]
            return tuple(out)

        below = lax.broadcasted_iota(jnp.int32, (tq, tk), 1) <= lax.broadcasted_iota(jnp.int32, (tq, tk), 0)
        init = (jnp.full((tq, 1), NEG, F32), jnp.zeros((tq, 1), F32), jnp.zeros((tq, 128), F32)) * 2
        ma, la, acca, mb, lb, accb = step(i, lax.fori_loop(0, i, step, init), diagonal=True)
        lane = lax.broadcasted_iota(jnp.int32, (tq, 128), 1)
        o_ref[...] = jnp.where(lane < 64, acca / la, accb / lb)
        lse_ref[...] = jnp.where(lane < 64, ma + jnp.log(la), mb + jnp.log(lb))

    return pl.pallas_call(
        body, name="attn_fwd", grid=(4, S // tq),
        out_shape=(jax.ShapeDtypeStruct((S, 512), F32), jax.ShapeDtypeStruct((4, S, 128), F32)),
        in_specs=[pl.BlockSpec((tq, 256), lambda p, i: (i, p)), pl.BlockSpec((S, 256), lambda p, i: (0, p)),
                  pl.BlockSpec((S, 128), lambda p, i: (0, p))],
        out_specs=(pl.BlockSpec((tq, 128), lambda p, i: (i, p)), pl.BlockSpec((None, tq, 128), lambda p, i: (p, i, 0))),
        compiler_params=_cp("parallel", "parallel"))(q, k, v)


def _attn_bwd(q, k, v, lse, delta, doa, dob, tq=512, tk=512):
    S = q.shape[0]
    assert tq == tk

    def body(q_ref, k_ref, v_ref, lse_ref, delta_ref, doa_ref, dob_ref, dq_ref, dk_ref, dv_ref):
        j = pl.program_id(1)

        @pl.when(j == 0)
        def _():
            dq_ref[...] = jnp.zeros_like(dq_ref)

        below = lax.broadcasted_iota(jnp.int32, (tq, tk), 1) <= lax.broadcasted_iota(jnp.int32, (tq, tk), 0)
        ks = [k_ref[:, 0:128], k_ref[:, 128:256]]
        vv = v_ref[...]

        def step(qb, carry, diagonal=False):
            dka, dkb, dvp = carry
            start = pl.multiple_of(qb * tq, tq)
            rows = pl.ds(start, tq)
            lse_v, delta_v = lse_ref[rows, :], delta_ref[rows, :]
            dos = [doa_ref[rows, :], dob_ref[rows, :]]
            dks = [dka, dkb]
            for h in range(2):
                delta = delta_v[:, 64 * h:64 * h + 1]
                do_h = dos[h]
                qh = q_ref[rows, 128 * h:128 * h + 128]
                s = _dot(qh, ks[h], "nt") * ATTN_SCALE
                p = jnp.exp(s - lse_v[:, 64 * h:64 * h + 1])
                if diagonal:
                    p = jnp.where(below, p, 0.0)
                dvp = dvp + _dot(p, do_h, "tn")
                ds = p * (_dot(do_h, vv, "nt") - delta) * ATTN_SCALE
                dq_ref[rows, 128 * h:128 * h + 128] += _dot(ds, ks[h], "nn")
                dks[h] = dks[h] + _dot(ds, qh, "tn")
            return dks[0], dks[1], dvp

        zero = jnp.zeros((tk, 128), F32)
        dka, dkb, dvp = lax.fori_loop(j + 1, S // tq, step, step(j, (zero, zero, zero), diagonal=True))
        dk_ref[:, 0:128] = dka
        dk_ref[:, 128:256] = dkb
        dv_ref[...] = dvp

    return pl.pallas_call(
        body, name="attn_bwd", grid=(4, S // tk),
        out_shape=(jax.ShapeDtypeStruct((S, 1024), F32), jax.ShapeDtypeStruct((S, 1024), F32), jax.ShapeDtypeStruct((S, 512), F32)),
        in_specs=[pl.BlockSpec((S, 256), lambda p, j: (0, p)), pl.BlockSpec((tk, 256), lambda p, j: (j, p)),
                  pl.BlockSpec((tk, 128), lambda p, j: (j, p))] + [pl.BlockSpec((None, S, 128), lambda p, j: (p, 0, 0))] * 4,
        out_specs=(pl.BlockSpec((S, 256), lambda p, j: (0, p)), pl.BlockSpec((tk, 256), lambda p, j: (j, p)),
                   pl.BlockSpec((tk, 128), lambda p, j: (j, p))),
        compiler_params=_cp("parallel", "arbitrary"))(q, k, v, lse, delta, doa, dob)


CHUNK = 128
GELU_C = math.sqrt(2.0 / math.pi)


def _gelu(v):
    t = jnp.tanh(GELU_C * (v + 0.044715 * (v * v * v)))
    return v * (0.5 * (1.0 + t)), t


def _gelu_grad(v, t):
    return 0.5 * (1.0 + t) + v * (0.5 * (1.0 - t * t) * GELU_C * (1.0 + 3.0 * 0.044715 * v * v))


def _tril(w):
    r = lax.broadcasted_iota(jnp.int32, w.shape, 0)
    c = lax.broadcasted_iota(jnp.int32, w.shape, 1)
    return jnp.where(c <= r, w, 0.0)


def _layer_norm(v, g, b):
    xc = v - jnp.mean(v, axis=-1, keepdims=True)
    rstd = lax.rsqrt(jnp.mean(xc * xc, axis=-1, keepdims=True) + EPS)
    xhat = xc * rstd
    return xhat * g + b, xhat, rstd


def _sgu_fwd(z, o, ln_g, ln_b, w_s, b_st, tr=256):
    S = z.shape[0]

    def body(u_ref, v_ref, o_ref, g_ref, b_ref, ws_ref, bs_ref, y_ref):
        gu, _ = _gelu(u_ref[...])
        gv, _ = _gelu(v_ref[...])
        vln, _, _ = _layer_norm(gv, g_ref[...], b_ref[...])
        y_ref[0] = o_ref[...].astype(y_ref.dtype)
        for g in range(4):
            wt = _tril(ws_ref[g])
            cols = slice(128 * g, 128 * g + 128)
            for ch in range(tr // CHUNK):
                rows = slice(CHUNK * ch, CHUNK * ch + CHUNK)
                mixed = _dot(wt, vln[rows, cols], "nn") + bs_ref[:, g:g + 1]
                y_ref[1, rows, cols] = (gu[rows, cols] * mixed).astype(y_ref.dtype)

    return pl.pallas_call(
        body, name="sgu_fwd", grid=(S // tr,), out_shape=jax.ShapeDtypeStruct((2, S, 512), _MXU_DTYPE),
        in_specs=[pl.BlockSpec((tr, 512), lambda i: (i, 1)), pl.BlockSpec((tr, 512), lambda i: (i, 2)), _row_spec(tr, 512),
                  _vec_spec(512), _vec_spec(512), pl.BlockSpec((4, 128, 128), lambda i: (0, 0, 0)), pl.BlockSpec((128, 4), lambda i: (0, 0))],
        out_specs=pl.BlockSpec((2, tr, 512), lambda i: (0, i, 0)), compiler_params=_cp("parallel"))(z, z, o, ln_g, ln_b, w_s, b_st)


def _sgu_bwd(z, dycat2, ln_g, ln_b, w_s, b_st, tr=256):
    S = z.shape[0]

    def body(u_ref, v_ref, dy_ref, g_ref, b_ref, ws_ref, bs_ref, duv_ref, dg_ref, db_ref, dws_ref, dbs_ref):
        first = pl.program_id(0) == 0
        u_pre, v_pre = u_ref[...], v_ref[...]
        gu, tu = _gelu(u_pre)
        gv, tv = _gelu(v_pre)
        gain = g_ref[...]
        vln, xhat, rstd = _layer_norm(gv, gain, b_ref[...])

        @pl.when(first)
        def _():
            dws_ref[...] = jnp.zeros_like(dws_ref)
            dbs_ref[...] = jnp.zeros_like(dbs_ref)

        dvln_cols = []
        for g in range(4):
            wt = _tril(ws_ref[g])
            cols = slice(128 * g, 128 * g + 128)
            dmixed_sum = jnp.zeros((CHUNK, 128), F32)
            dw = jnp.zeros((CHUNK, CHUNK), F32)
            dvln_rows = []
            for ch in range(tr // CHUNK):
                rows = slice(CHUNK * ch, CHUNK * ch + CHUNK)
                vt = vln[rows, cols]
                mixed = _dot(wt, vt, "nn") + bs_ref[:, g:g + 1]
                dyd = dy_ref[rows, cols]
                duv_ref[rows, cols] = (dyd * mixed * _gelu_grad(u_pre[rows, cols], tu[rows, cols])).astype(duv_ref.dtype)
                dmixed = dyd * gu[rows, cols]
                dmixed_sum = dmixed_sum + dmixed
                dw = dw + _dot(dmixed, vt, "nt")
                dvln_rows.append(_dot(wt, dmixed, "tn"))
            dws_ref[g] += _tril(dw)
            dbs_ref[g:g + 1, :] += jnp.sum(dmixed_sum.T, axis=0, keepdims=True)
            dvln_cols.append(jnp.concatenate(dvln_rows, axis=0))
        dvln = jnp.concatenate(dvln_cols, axis=1)
        _acc_rows(dg_ref, dvln * xhat, first)
        _acc_rows(db_ref, dvln, first)
        dxhat = dvln * gain
        dgv = rstd * (dxhat - jnp.mean(dxhat, axis=-1, keepdims=True) - xhat * jnp.mean(dxhat * xhat, axis=-1, keepdims=True))
        duv_ref[:, 512:1024] = (dgv * _gelu_grad(v_pre, tv)).astype(duv_ref.dtype)

    return pl.pallas_call(
        body, name="sgu_bwd", grid=(S // tr,),
        out_shape=(jax.ShapeDtypeStruct((S, 1024), _MXU_DTYPE), jax.ShapeDtypeStruct((1, 512), F32), jax.ShapeDtypeStruct((1, 512), F32),
                   jax.ShapeDtypeStruct((4, 128, 128), F32), jax.ShapeDtypeStruct((4, 128), F32)),
        in_specs=[pl.BlockSpec((tr, 512), lambda i: (i, 1)), pl.BlockSpec((tr, 512), lambda i: (i, 2)),
                  pl.BlockSpec((None, tr, 512), lambda i: (1, i, 0)), _vec_spec(512), _vec_spec(512),
                  pl.BlockSpec((4, 128, 128), lambda i: (0, 0, 0)), pl.BlockSpec((128, 4), lambda i: (0, 0))],
        out_specs=(_row_spec(tr, 1024), _vec_spec(512), _vec_spec(512), pl.BlockSpec((4, 128, 128), lambda i: (0, 0, 0)),
                   pl.BlockSpec((4, 128), lambda i: (0, 0))),
        compiler_params=_cp("arbitrary"))(z, z, dycat2, ln_g, ln_b, w_s, b_st)


def _adamw_math(w, m, v, g):
    c1 = 1.0 / (1.0 - ADAM_B1 ** ADAM_STEP)
    c2 = 1.0 / (1.0 - ADAM_B2 ** ADAM_STEP)
    m2 = ADAM_B1 * m + (1.0 - ADAM_B1) * g
    v2 = ADAM_B2 * v + (1.0 - ADAM_B2) * (g * g)
    return -ADAM_LR * ((m2 * c1) / (jnp.sqrt(v2 * c2) + ADAM_EPS) + ADAM_WD * w), m2, v2


def _adamw_small(name, params, parts):
    n = len(params)

    def body(*refs):
        ins, outs = refs[:4 * n], refs[4 * n:]
        for i in range(n):
            w_ref, m_ref, v_ref, p_ref = ins[4 * i:4 * i + 4]
            g = p_ref[0].astype(F32)
            for k in range(1, N_DEV):
                g = g + p_ref[k].astype(F32)
            delta, m2, v2 = _adamw_math(w_ref[...], m_ref[...], v_ref[...], g)
            outs[4 * i][...] = g
            outs[4 * i + 1][...] = delta
            outs[4 * i + 2][...] = m2
            outs[4 * i + 3][...] = v2

    flat = [a for (w, m, v), p in zip(params, parts) for a in (w, m, v, p)]
    out = pl.pallas_call(
        body, name=name, out_shape=[jax.ShapeDtypeStruct(w.shape, F32) for (w, _, _) in params for _ in range(4)],
        compiler_params=pltpu.CompilerParams(vmem_limit_bytes=_VMEM_LIMIT))(*flat)
    return [out[4 * i:4 * i + 4] for i in range(n)]


ADAMW_BLOCK_BYTES = 36 * 2 ** 20


def _adamw(name, w, m, v, parts):
    L, R, C = w.shape
    P = parts[0].shape[0]
    row_bytes = 2 * C * (7 * 4 + P * parts[0].dtype.itemsize)
    tr = R
    if R * row_bytes > ADAMW_BLOCK_BYTES:
        tr = next(t for t in (1024, 512, 256, 128, 64, 32, 16) if R % t == 0 and t * row_bytes <= ADAMW_BLOCK_BYTES)
    nr = R // tr
    c1 = 1.0 / (1.0 - ADAM_B1 ** ADAM_STEP)
    c2 = 1.0 / (1.0 - ADAM_B2 ** ADAM_STEP)

    def body(w_ref, m_ref, v_ref, *rest):
        p_refs, (g_ref, d_ref, mo_ref, vo_ref) = rest[:L], rest[L:]
        for ll in range(L):
            @pl.when(pl.program_id(0) == ll)
            def _(p_ref=p_refs[ll]):
                g = p_ref[0].astype(F32)
                for k in range(1, P):
                    g = g + p_ref[k].astype(F32)
                m2 = ADAM_B1 * m_ref[...] + (1.0 - ADAM_B1) * g
                v2 = ADAM_B2 * v_ref[...] + (1.0 - ADAM_B2) * (g * g)
                g_ref[...] = g
                mo_ref[...] = m2
                vo_ref[...] = v2
                d_ref[...] = -ADAM_LR * ((m2 * c1) / (jnp.sqrt(v2 * c2) + ADAM_EPS) + ADAM_WD * w_ref[...])

    def part_spec(ll):
        return pl.BlockSpec((P, tr, C), lambda l, i: (0, jnp.where(l == ll, i, jnp.where(l < ll, 0, nr - 1)), 0))

    full = pl.BlockSpec((None, tr, C), lambda l, i: (l, i, 0))
    sds = jax.ShapeDtypeStruct((L, R, C), F32)
    return pl.pallas_call(
        body, name=name, grid=(L, nr), out_shape=(sds, sds, sds, sds),
        in_specs=[full] * 3 + [part_spec(ll) for ll in range(L)],
        out_specs=(full,) * 4, compiler_params=_cp("arbitrary", "arbitrary"))(w, m, v, *parts)


def _rope_tables(positions):
    half = 16
    inv_freq = 10000.0 ** (-jnp.arange(half, dtype=F32) / half)
    ang = positions.astype(F32)[:, None] * inv_freq
    cos, sin = jnp.cos(ang), jnp.sin(ang)
    S = positions.shape[0]
    z16, z32, z64 = jnp.zeros((S, 16), F32), jnp.zeros((S, 32), F32), jnp.zeros((S, 64), F32)
    cosk = jnp.concatenate([z64, cos, cos, z32], axis=1)
    cosq = jnp.concatenate([jnp.ones((S, 64), F32), cos, cos, z32], axis=1)
    sa = jnp.concatenate([z64, -sin, z16, z32], axis=1)
    sb = jnp.concatenate([z64, z16, sin, z32], axis=1)
    return cosq, cosk, sa, sb


def _ffn_fwd(l, x, mod, n2g, get_w_up8, cw24, get_w_down4):
    sh, sc, gate = mod
    h = _rmsmod_fwd(f"ffn{l}_norm", x, n2g, sc, sh, n2g)
    w_up8 = get_w_up8(h)
    u8 = _mm_cols(f"ffn{l}_up", h, w_up8, out_dtype=ACT_DTYPE, tm=2048)
    S, n = u8.shape[1], u8.shape[2]
    u24 = u8.reshape(2, 4, S, n)
    a4, z24 = _ffn_gate_fwd(f"ffn{l}_gate", u24, cw24)
    w_down4 = get_w_down4(a4)
    f, x_new = _mm_rows_resid(f"ffn{l}_down", a4, w_down4, x, gate)
    return x_new, (x, h, u24, a4, f, z24), w_up8, w_down4


def _ffn_bwd(l, dx, df, dgate, saved, mod, n2g, w_up8, cw24, w_down4, me, y_prev, gate_prev):
    sh, sc, gate = mod
    x, h, u24, a4, f, z24 = saved
    da4 = _mm_rows_dx(f"ffn{l}_down_dx", df, w_down4, out_dtype=ACT_DTYPE, tm=2048)
    dw_down4 = _mm_rows_dw(f"ffn{l}_down_dw", a4, df, out_dtype=WIRE_DTYPE, tn=1024)
    sent_down, token = _exchange_start(f"scatter_ffn{l}_down", [dw_down4.reshape(8, 352, dw_down4.shape[2])], True, dgate, me)
    du24, dcw24, dh = _ffn_gate_bwd(f"ffn{l}_act_bwd", u24, z24, cw24, da4, w_up8.reshape((2, 4) + w_up8.shape[1:]), token)
    du8 = du24.reshape((8,) + du24.shape[2:])
    dw_up8t = _mm_cols_dwt(f"ffn{l}_up_dw", h, du8, out_dtype=WIRE_DTYPE, tk=1024)
    sent_up, token = _exchange_start(f"scatter_ffn{l}_up", [dw_up8t], True, dcw24, me)
    dx_new, dn2g, dsc, dsh, dy_prev, dgate_prev = _rmsmod_bwd(f"ffn{l}_norm_bwd", x, n2g, sc, dh, dx, token, y_prev, gate_prev)
    return dx_new, dict(sent_up=sent_up, sent_down=sent_down, cw24=dcw24, n2g=dn2g, mod=(dsh, dsc, dgate)), dy_prev, dgate_prev


def kernel(x, c, positions, ada_w, ada_b, norm1_g, norm2_g, ab_w_in, a_conv_w, b_mix_w, b_scale, ab_w_out, cd_w_in, c_q_norm_g, c_w_uq, c_kv_norm_g, c_w_ukv, d_ln_g, d_ln_b, d_w_s, d_b_s, cd_w_out, ffn_w_up, ffn_conv_w, ffn_w_down, final_norm_g, loss_target, m_ada_w, m_ada_b, m_norm1_g, m_norm2_g, m_ab_w_in, m_a_conv_w, m_b_mix_w, m_b_scale, m_ab_w_out, m_cd_w_in, m_c_q_norm_g, m_c_w_uq, m_c_kv_norm_g, m_c_w_ukv, m_d_ln_g, m_d_ln_b, m_d_w_s, m_d_b_s, m_cd_w_out, m_ffn_w_up, m_ffn_conv_w, m_ffn_w_down, m_final_norm_g, v_ada_w, v_ada_b, v_norm1_g, v_norm2_g, v_ab_w_in, v_a_conv_w, v_b_mix_w, v_b_scale, v_ab_w_out, v_cd_w_in, v_c_q_norm_g, v_c_w_uq, v_c_kv_norm_g, v_c_w_ukv, v_d_ln_g, v_d_ln_b, v_d_w_s, v_d_b_s, v_cd_w_out, v_ffn_w_up, v_ffn_conv_w, v_ffn_w_down, v_final_norm_g):
    S, D = x.shape[1], x.shape[2]
    me = 4 * lax.axis_index("x") + 2 * lax.axis_index("y") + lax.axis_index("c")
    x0, target = x[0], loss_target[0]
    W = _MXU_DTYPE

    small_shapes = [(1024,), (3, 64), (32,), (64,), (64,), (2, 3, 704)]
    (g0,) = _exchange("gather_small", [[_pack([c, a_conv_w, c_q_norm_g, d_ln_g, d_ln_b, ffn_conv_w])]], scatter=False)
    c_all, aconv_s, qg_s, lng_s, lnb_s, fcw_s = _unpack(g0[:, 0], small_shapes, lead=(N_DEV,))
    conv_w = aconv_s.transpose(1, 0, 2).reshape(3, 512)
    qg, ln_g, ln_b = qg_s.reshape(1, 256), lng_s.reshape(1, 512), lnb_s.reshape(1, 512)
    cw24 = [fcw_s[:, l].reshape(2, 4, 3, 704) for l in range(2)]
    c16 = jnp.pad(c_all, ((0, 16 - N_DEV), (0, 0)))

    mod_cols = _ada_fwd(c16, ada_w)
    (g1,) = _exchange("gather_mod", [[_pack([mod_cols])]], scatter=False)
    mod_all = _unpack(g1[:, 0], [(2, 16, 768)], lead=(N_DEV,))[0]
    mod_mine = lax.dynamic_index_in_dim(mod_all, me, axis=2, keepdims=False)
    mod = mod_mine.transpose(1, 0, 2).reshape(2, 6 * D) + ada_b
    mods = [[mod[l, k * D:(k + 1) * D].reshape(1, D) for k in range(6)] for l in range(2)]

    gw_ab, token = _hier_gather_start("gather_w_ab", [ab_w_in[0].astype(W), ab_w_out[0].astype(W)], mod, me)
    gw_up0, token = _hier_gather_start("gather_w_ffn0_up", [ffn_w_up[0].astype(W)], token, me)
    gw_rest, started = _exchange_start("gather_w_rest", [
        ffn_w_down[0].astype(W), cd_w_in[0].T.astype(W), c_w_uq[0].T.astype(W), c_w_ukv[0].astype(W), cd_w_out[0].astype(W),
        ffn_w_up[1].astype(W), ffn_w_down[1].astype(W)], False, token, me)

    cosq, cosk, sa, sb = _rope_tables(positions[0])
    n1g = [norm1_g[l].reshape(1, D) for l in range(2)]
    n2g = [norm2_g[l].reshape(1, D) for l in range(2)]
    mix_w, scale = b_mix_w[0], b_scale
    kvg = c_kv_norm_g
    w_s, b_st = d_w_s[0], d_b_s[0].T

    sh1, sc1, g1m = mods[0][:3]
    h_ab = _rmsmod_fwd("ab_norm", x0, n1g[0], sc1, sh1, started)
    w_abin8, w_about = _hier_gather_wait("wait_w_ab", _hier_gather_forward("forward_w_ab", gw_ab, h_ab), h_ab)
    w_about2 = w_about.reshape(2, 512, D)
    z8 = _mm_cols("ab_in", h_ab, w_abin8, out_dtype=ACT_DTYPE, tm=2048)
    ycat_ab = _ab_mix_fwd(z8, conv_w, mix_w, scale)
    y_ab, x1 = _mm_rows_resid("ab_out", ycat_ab, w_about2, x0, g1m)
    w_up8, w_down4 = [None, None], [None, None]
    gw_up0 = _hier_gather_forward("forward_w_ffn0_up", gw_up0, x1)
    x2, ffn0_saved, w_up8[0], w_down4[0] = _ffn_fwd(
        0, x1, mods[0][3:], n2g[0], lambda after: _hier_gather_wait("wait_w_ffn0_up", gw_up0, after)[0], cw24[0],
        lambda after: _exchange_wait("wait_w_ffn0_down", gw_rest, after, [0])[0].reshape(4, 704, D))

    w_cdin, w_uq, w_ukv, w_cdout = _exchange_wait("wait_w_cd", gw_rest, x2, [1, 2, 3, 4])
    w_cdout2 = w_cdout.reshape(2, 512, D)
    w_cd_t = w_cdin.reshape(1440, D)
    zr = lambda n: jnp.zeros((n, D), W)
    w_cd_pad = jnp.concatenate([w_cd_t[:384], zr(64), w_cd_t[384:416], zr(32), w_cd_t[416:]], axis=0)
    w_uq_pad = jnp.pad(w_uq, ((0, 0), (0, 32), (0, 0))).reshape(1024, 256)
    w_ukv_h = w_ukv.transpose(1, 0, 2)
    w_k_pad = jnp.pad(w_ukv_h[:, :, :64], ((0, 0), (0, 0), (0, 64))).reshape(128, 1024)
    w_kv_pad = jnp.concatenate([w_k_pad, w_ukv_h[:, :, 64:].reshape(128, 512)], axis=1)

    sh1, sc1, g1c = mods[1][:3]
    h_cd = _rmsmod_fwd("cd_norm", x2, n1g[1], sc1, sh1, n1g[1])
    z_cd = _mm_nt("cd_in", h_cd, w_cd_pad, tm=1024, tn=1536)
    qn, kvn, q_r, k_r, v_r = _qkv_rope_fwd(z_cd, qg, kvg, w_uq_pad, w_kv_pad, cosq, cosk, sa, sb)
    o, lse = _attn_fwd(q_r, k_r, v_r)
    ycat_cd = _sgu_fwd(z_cd, o, ln_g, ln_b, w_s, b_st)
    y_cd, x3 = _mm_rows_resid("cd_out", ycat_cd, w_cdout2, x2, g1c)
    x4, ffn1_saved, w_up8[1], w_down4[1] = _ffn_fwd(
        1, x3, mods[1][3:], n2g[1], lambda after: _exchange_wait("wait_w_ffn1_up", gw_rest, after, [5])[0], cw24[1],
        lambda after: _exchange_wait("wait_w_ffn1_down", gw_rest, after, [6])[0].reshape(4, 704, D))

    loss_local, dx4, dfg, df1, dgate1 = _loss_head(x4, final_norm_g.reshape(1, D), target, ffn1_saved[4], mods[1][5])

    dx3, gf1, dy, dg1c = _ffn_bwd(1, dx4, df1, dgate1, ffn1_saved, mods[1][3:], n2g[1], w_up8[1], cw24[1], w_down4[1], me, y_cd, g1c)

    dycat = _mm_rows_dx("cd_out_dx", dy, w_cdout2, tm=2048)
    dw_cdout = _mm_rows_dw("cd_out_dw", ycat_cd, dy, out_dtype=WIRE_DTYPE, tn=1024)
    duv, dln_g, dln_b, dws, dbs = _sgu_bwd(z_cd, dycat, ln_g, ln_b, w_s, b_st)
    dq_r, dk_r, dv_r = _attn_bwd(q_r, k_r, v_r, lse, *_attn_bwd_prep(o, dycat))
    dqraw, dkvall, dz_cd, dqg, dkvg = _qkv_rope_bwd(z_cd, qg, kvg, dq_r, dk_r, dv_r, duv, w_uq_pad, w_kv_pad, cosq, cosk, sa, sb)
    dw_uq_pad = _mm_tn("cd_uq_dw", dqraw, qn, tn=256)
    dw_kv_pad = _mm_tn("cd_ukv_dw", kvn, dkvall, tm=128)
    dh_cd = _mm_nn("cd_in_dx", dz_cd, w_cd_pad, out_dtype=ACT_DTYPE, tm=1024, tn=1024)
    dw_cd_pad = _mm_tn("cd_in_dw", dz_cd, h_cd, tm=768, tn=1024)
    dw_cd8 = jnp.concatenate([dw_cd_pad[:384], dw_cd_pad[448:480], dw_cd_pad[512:]], axis=0).astype(WIRE_DTYPE).reshape(8, 180, D)
    dw_uq8 = dw_uq_pad.reshape(8, 128, 256)[:, :96].astype(WIRE_DTYPE)
    dw_ukv8 = jnp.concatenate([dw_kv_pad[:, :1024].reshape(128, 8, 128)[:, :, :64], dw_kv_pad[:, 1024:].reshape(128, 8, 64)],
                              axis=2).transpose(1, 0, 2).astype(WIRE_DTYPE)
    sent_cd, token = _exchange_start("scatter_cd", [dw_cd8, dw_uq8, dw_ukv8, dw_cdout.reshape(8, 128, D)], True, dqg, me)
    early_names = ["c_kv_norm_g", "d_w_s", "d_b_s", "final_norm_g", "c_q_norm_g", "d_ln_g", "d_ln_b"]
    early_grads = [dkvg, dws.reshape(512, 128).astype(WIRE_DTYPE), dbs, dfg, dqg.reshape(8, 1, 32), dln_g.reshape(8, 1, 64),
                   dln_b.reshape(8, 1, 64)]
    early_sent, token = _exchange_start("gather_small_grads_early", early_grads, [False] * 4 + [True] * 3, token, me)
    dx2, dn1g_cd, dsc1_cd, dsh1_cd, df0, dgate0 = _rmsmod_bwd("cd_norm_bwd", x2, n1g[1], sc1, dh_cd, dx3, token,
                                                              ffn0_saved[4], mods[0][5])

    dx1, gf0, dy, dg1m = _ffn_bwd(0, dx2, df0, dgate0, ffn0_saved, mods[0][3:], n2g[0], w_up8[0], cw24[0], w_down4[0], me, y_ab, g1m)

    dw_about = _mm_rows_dw("ab_out_dw", ycat_ab, dy, out_dtype=WIRE_DTYPE, tn=1024)
    sent_about, token = _exchange_start("scatter_ab_out", [dw_about.reshape(8, 128, D)], True, dg1m, me)
    dycat = _mm_rows_dx("ab_out_dx", dy, w_about2, tm=2048)
    dz8, dconv_w, dmix_w, dscale = _ab_mix_bwd(z8, dycat, conv_w, mix_w, scale, token)
    dz8 = dz8.reshape(8, S, 256)
    dw_abin8 = _mm_cols_dw("ab_in_dw", h_ab, dz8, out_dtype=WIRE_DTYPE, tk=1024)
    sent_abin, token = _exchange_start("scatter_ab_in", [dw_abin8], True, dscale, me)
    dh_ab = _mm_cols_dx("ab_in_dx", dz8, w_abin8, out_dtype=ACT_DTYPE)
    dx0, dn1g_ab, dsc1_ab, dsh1_ab = _rmsmod_bwd("ab_norm_bwd", x0, n1g[0], mods[0][1], dh_ab, dx1, token)

    dmod = jnp.stack([jnp.concatenate([dsh1_ab, dsc1_ab, dg1m, *gf0["mod"]], axis=1)[0],
                      jnp.concatenate([dsh1_cd, dsc1_cd, dg1c, *gf1["mod"]], axis=1)[0]])
    late_names = ["ada_b", "norm1_g", "norm2_g", "b_mix_w", "b_scale", "a_conv_w", "ffn_conv_w"]
    late_grads = [dmod, jnp.concatenate([dn1g_ab, dn1g_cd]), jnp.concatenate([gf0["n2g"], gf1["n2g"]]),
                  dmix_w.reshape(512, 128).astype(WIRE_DTYPE), dscale, dconv_w.reshape(3, 8, 64).transpose(1, 0, 2),
                  jnp.stack([gf0["cw24"].reshape(8, 3, 704), gf1["cw24"].reshape(8, 3, 704)], axis=1),
                  jnp.pad(loss_local, ((0, 0), (0, 127)))]
    small_view = dict(ada_b=(2, 6 * D), norm1_g=(2, D), norm2_g=(2, D), b_mix_w=(512, 128), b_scale=(1, 512), c_kv_norm_g=(1, 128),
                      d_w_s=(512, 128), d_b_s=(4, 128), final_norm_g=(1, D),
                      a_conv_w=(3, 64), c_q_norm_g=(1, 32), d_ln_g=(1, 64), d_ln_b=(1, 64), ffn_conv_w=(2, 3, 704))
    late_sent, token = _exchange_start("gather_small_grads_late", late_grads, [False] * 5 + [True] * 2 + [False], dx0, me)

    res = {}

    def update(name, w, m, v, parts, shape3d):
        outs = _adamw("adamw_" + name, w.reshape(shape3d), m.reshape(shape3d), v.reshape(shape3d),
                      [p.reshape((p.shape[0],) + shape3d[1:]) for p in parts])
        res[name] = [o_.reshape(w.shape) for o_ in outs]

    p_cdin, p_uq, p_ukv, p_cdout = _exchange_wait("wait_scatter_cd", sent_cd, token)
    swap = lambda a: jnp.swapaxes(a, 1, 2)
    update("cd_w_in", swap(cd_w_in), swap(m_cd_w_in), swap(v_cd_w_in), [p_cdin], (1, 180, D))
    update("c_w_uq", swap(c_w_uq), swap(m_c_w_uq), swap(v_c_w_uq), [p_uq], (1, 96, 256))
    for name in ("cd_w_in", "c_w_uq"):
        res[name] = [swap(o_) for o_ in res[name]]
    update("c_w_ukv", c_w_ukv, m_c_w_ukv, v_c_w_ukv, [p_ukv], (1, 128, 128))
    update("cd_w_out", cd_w_out, m_cd_w_out, v_cd_w_out, [p_cdout], (1, 128, D))
    (p_dn1,) = _exchange_wait("wait_scatter_ffn1_down", gf1["sent_down"], token)
    (p_dn0,) = _exchange_wait("wait_scatter_ffn0_down", gf0["sent_down"], res["cd_w_out"][0])
    update("ffn_w_down", ffn_w_down, m_ffn_w_down, v_ffn_w_down, [p_dn0, p_dn1], (2, 352, D))
    (p_up1,) = _exchange_wait("wait_scatter_ffn1_up", gf1["sent_up"], token)
    (p_up0,) = _exchange_wait("wait_scatter_ffn0_up", gf0["sent_up"], res["ffn_w_down"][0])
    swap = lambda a: jnp.swapaxes(a, 1, 2)
    update("ffn_w_up", swap(ffn_w_up), swap(m_ffn_w_up), swap(v_ffn_w_up), [p_up0, p_up1], (2, 704, D))
    up_done = res["ffn_w_up"][0]
    res["ffn_w_up"] = [swap(o_) for o_ in res["ffn_w_up"]]
    (p_about,) = _exchange_wait("wait_scatter_ab_out", sent_about, up_done)
    update("ab_w_out", ab_w_out, m_ab_w_out, v_ab_w_out, [p_about], (1, 128, D))
    (p_abin,) = _exchange_wait("wait_scatter_ab_in", sent_abin, res["ab_w_out"][0])
    update("ab_w_in", ab_w_in, m_ab_w_in, v_ab_w_in, [p_abin], (1, D, 256))

    early_parts = _exchange_wait("wait_small_grads_early", early_sent, res["ab_w_in"][0])
    late_parts = _exchange_wait("wait_small_grads_late", late_sent, res["ab_w_in"][0])
    small_names = early_names + late_names
    small_parts = list(early_parts) + list(late_parts[:7])
    loss = jnp.sum(late_parts[7][:, 0, 0])
    dmod_all = late_parts[0]
    dmod_cols = lax.dynamic_slice_in_dim(dmod_all, me * 768, 768, axis=2).transpose(1, 0, 2)
    g_ada_w = _ada_bwd(c16, jnp.pad(dmod_cols, ((0, 0), (0, 16 - N_DEV), (0, 0))))
    update("ada_w", ada_w, m_ada_w, v_ada_w, [g_ada_w[None]], (1, 2 * D, 768))

    small_w = dict(ada_b=(ada_b, m_ada_b, v_ada_b), norm1_g=(norm1_g, m_norm1_g, v_norm1_g), norm2_g=(norm2_g, m_norm2_g, v_norm2_g),
                   b_mix_w=(b_mix_w, m_b_mix_w, v_b_mix_w), b_scale=(b_scale, m_b_scale, v_b_scale),
                   c_kv_norm_g=(c_kv_norm_g, m_c_kv_norm_g, v_c_kv_norm_g), d_w_s=(d_w_s, m_d_w_s, v_d_w_s),
                   d_b_s=(d_b_s, m_d_b_s, v_d_b_s), final_norm_g=(final_norm_g, m_final_norm_g, v_final_norm_g),
                   a_conv_w=(a_conv_w, m_a_conv_w, v_a_conv_w), c_q_norm_g=(c_q_norm_g, m_c_q_norm_g, v_c_q_norm_g),
                   d_ln_g=(d_ln_g, m_d_ln_g, v_d_ln_g), d_ln_b=(d_ln_b, m_d_ln_b, v_d_ln_b),
                   ffn_conv_w=(ffn_conv_w, m_ffn_conv_w, v_ffn_conv_w))
    small_out = _adamw_small("adamw_small", [tuple(a.reshape(small_view[n]) for a in small_w[n]) for n in small_names],
                             list(small_parts))
    for n, outs in zip(small_names, small_out):
        res[n] = [o_.reshape(small_w[n][0].shape) for o_ in outs]

    order = ["ada_w", "ada_b", "norm1_g", "norm2_g", "ab_w_in", "a_conv_w", "b_mix_w", "b_scale", "ab_w_out", "cd_w_in", "c_q_norm_g",
             "c_w_uq", "c_kv_norm_g", "c_w_ukv", "d_ln_g", "d_ln_b", "d_w_s", "d_b_s", "cd_w_out", "ffn_w_up", "ffn_conv_w",
             "ffn_w_down", "final_norm_g"]
    return (loss, dx0[None], *[res[n][0] for n in order], *[res[n][1] for n in order], *[res[n][2] for n in order],
            *[res[n][3] for n in order])
```

```python
import functools
import math

import jax
import jax.numpy as jnp
from jax import lax
from jax.experimental import pallas as pl
from jax.experimental.pallas import tpu as pltpu

F32 = jnp.float32
BF16 = jnp.bfloat16
_MXU_DTYPE = BF16
WIRE_DTYPE = BF16
ACT_DTYPE = BF16
_VMEM_LIMIT = 56 * 2 ** 20
N_DEV = 8
EPS = 1e-6
POOL_WINDOWS = (2, 4, 8, 16)
ATTN_SCALE = (64 + 32) ** -0.5
ADAM_LR, ADAM_B1, ADAM_B2, ADAM_EPS, ADAM_WD, ADAM_STEP = 0.001, 0.9, 0.999, 1e-08, 0.01, 10
MESH = pl.DeviceIdType.MESH
ANY = pl.BlockSpec(memory_space=pl.ANY)


def _cp(*sem):
    return pltpu.CompilerParams(dimension_semantics=sem, vmem_limit_bytes=_VMEM_LIMIT)


def _dot(a, b, contract):
    dn = {"nn": (((1,), (0,)), ((), ())), "nt": (((1,), (1,)), ((), ())), "tn": (((0,), (0,)), ((), ()))}[contract]
    return lax.dot_general(a.astype(_MXU_DTYPE), b.astype(_MXU_DTYPE), dn, preferred_element_type=F32)


def _my_position():
    x, y, c = lax.axis_index("x"), lax.axis_index("y"), lax.axis_index("c")
    return x, y, c, 4 * x + 2 * y + c


def _exchange(name, groups, scatter):
    flat = [a for g in groups for a in g]
    n_in, n_grp = len(flat), len(groups)
    out_shapes = []
    for g in groups:
        slab = g[0].shape[1:] if scatter else g[0].shape
        out_shapes.append(jax.ShapeDtypeStruct((N_DEV, len(g)) + tuple(slab), g[0].dtype))

    def body(*refs):
        ins, outs = refs[:n_in], refs[n_in:n_in + n_grp]
        send_sems, recv_sems, local_sems = refs[n_in + n_grp:]
        x, y, c, me = _my_position()
        i = 0
        for gi, g in enumerate(groups):
            for l in range(len(g)):
                src = ins[i]
                i += 1
                pltpu.make_async_copy(src.at[me] if scatter else src, outs[gi].at[me, l], local_sems.at[gi]).start()
                for k in range(1, N_DEV):
                    px = 1 - x if k & 4 else x
                    py = 1 - y if k & 2 else y
                    pc = 1 - c if k & 1 else c
                    peer = 4 * px + 2 * py + pc
                    pltpu.make_async_remote_copy(
                        src_ref=src.at[peer] if scatter else src, dst_ref=outs[gi].at[me, l],
                        send_sem=send_sems.at[gi], recv_sem=recv_sems.at[gi],
                        device_id=(px, py, pc), device_id_type=MESH).start()
        for gi in range(n_grp):
            mine = outs[gi].at[me]
            pltpu.make_async_copy(mine, mine, local_sems.at[gi]).wait()
            seven = outs[gi].at[pl.ds(0, N_DEV - 1)]
            w = pltpu.make_async_remote_copy(src_ref=seven, dst_ref=seven, send_sem=send_sems.at[gi],
                                             recv_sem=recv_sems.at[gi], device_id=(x, y, c), device_id_type=MESH)
            w.wait_send()
            w.wait_recv()

    return pl.pallas_call(
        body, name=name, out_shape=tuple(out_shapes),
        in_specs=[ANY] * n_in, out_specs=tuple([ANY] * n_grp),
        scratch_shapes=[pltpu.SemaphoreType.DMA((n_grp,)), pltpu.SemaphoreType.DMA((n_grp,)),
                        pltpu.SemaphoreType.DMA((n_grp,))],
        compiler_params=pltpu.CompilerParams(has_side_effects=True),
    )(*flat)


HBM_SPEC = pl.BlockSpec(memory_space=pltpu.HBM)
SEM_SPEC = pl.BlockSpec(memory_space=pltpu.SEMAPHORE)
EFFECT = pltpu.SideEffectType.DATAFLOW_SIDE_EFFECTING


def _put_mine(name, srcs, scatter, me):
    n = len(srcs)
    slabs = [tuple(s.shape[1:] if sc else s.shape) for s, sc in zip(srcs, scatter)]

    def body(me_ref, *refs):
        for i in range(n):
            refs[n + i][...] = refs[i][...]

    def at_me(slab):
        return pl.BlockSpec((None,) + slab, lambda g, me_ref, nd=len(slab): (me_ref[0],) + (0,) * nd)

    def whole(slab):
        return pl.BlockSpec(slab, lambda g, me_ref, nd=len(slab): (0,) * nd)

    return pl.pallas_call(
        body, name=name,
        grid_spec=pltpu.PrefetchScalarGridSpec(
            num_scalar_prefetch=1, grid=(1,),
            in_specs=[at_me(slab) if sc else whole(slab) for slab, sc in zip(slabs, scatter)],
            out_specs=[at_me(slab) for slab in slabs]),
        out_shape=[jax.ShapeDtypeStruct((N_DEV,) + slab, s.dtype) for slab, s in zip(slabs, srcs)],
        compiler_params=_cp("arbitrary"))(me.reshape(1), *srcs)


def _exchange_start(name, srcs, scatter, after, me):
    n = len(srcs)
    scatter = list(scatter) if isinstance(scatter, (list, tuple)) else [scatter] * n
    lands = _put_mine(name + "_mine", srcs, scatter, me)
    srcs = [pltpu.with_memory_space_constraint(a, pltpu.HBM) for a in srcs]
    lands = [pltpu.with_memory_space_constraint(a, pltpu.HBM) for a in lands]

    def body(*refs):
        ins, land = refs[:n], refs[n:2 * n]
        send_sems, recv_sems, token = refs[2 * n + 1], refs[2 * n + 2], refs[-1]
        x, y, c, me_in = _my_position()
        for i in range(n):
            for k in range(1, N_DEV):
                px = 1 - x if k & 4 else x
                py = 1 - y if k & 2 else y
                pc = 1 - c if k & 1 else c
                pltpu.make_async_remote_copy(
                    src_ref=ins[i].at[4 * px + 2 * py + pc] if scatter[i] else ins[i], dst_ref=land[i].at[me_in],
                    send_sem=send_sems.at[i], recv_sem=recv_sems.at[i],
                    device_id=(px, py, pc), device_id_type=MESH).start()
        token[...] = jnp.zeros_like(token)

    outs = pl.pallas_call(
        body, name=name,
        out_shape=(pltpu.SemaphoreType.DMA((n,)), pltpu.SemaphoreType.DMA((n,)),
                   *[pltpu.HBM(a.shape, a.dtype) for a in srcs], *[pltpu.HBM(a.shape, a.dtype) for a in lands],
                   jax.ShapeDtypeStruct((8, 128), F32)),
        in_specs=[HBM_SPEC] * (2 * n) + [ANY],
        out_specs=(SEM_SPEC, SEM_SPEC, *[HBM_SPEC] * (2 * n), pl.BlockSpec(memory_space=pltpu.VMEM)),
        input_output_aliases={i: 2 + i for i in range(2 * n)},
        compiler_params=pltpu.CompilerParams(has_side_effects=EFFECT),
    )(*srcs, *lands, after)
    return (outs[0], outs[1], outs[2:2 + n], outs[2 + n:2 + 2 * n]), outs[-1]


def _exchange_wait(name, handle, after, which=None):
    send_sems, recv_sems, srcs, lands = handle
    which = list(range(len(srcs))) if which is None else list(which)
    srcs, lands = [srcs[i] for i in which], [lands[i] for i in which]
    n = len(srcs)

    def body(*refs):
        land, send_ref, recv_ref = refs[n:2 * n], refs[2 * n], refs[2 * n + 1]
        x, y, c, _ = _my_position()
        for k, i in enumerate(which):
            seven = land[k].at[pl.ds(0, N_DEV - 1)]
            w = pltpu.make_async_remote_copy(src_ref=seven, dst_ref=seven, send_sem=send_ref.at[i], recv_sem=recv_ref.at[i],
                                             device_id=(x, y, c), device_id_type=MESH)
            w.wait_send()
            w.wait_recv()

    outs = pl.pallas_call(
        body, name=name,
        out_shape=(*[pltpu.HBM(a.shape, a.dtype) for a in srcs], *[pltpu.HBM(a.shape, a.dtype) for a in lands]),
        in_specs=[HBM_SPEC] * (2 * n) + [SEM_SPEC, SEM_SPEC, ANY],
        out_specs=tuple([HBM_SPEC] * (2 * n)),
        input_output_aliases={i: i for i in range(2 * n)},
        compiler_params=pltpu.CompilerParams(has_side_effects=EFFECT),
    )(*srcs, *lands, send_sems, recv_sems, after)
    return outs[n:]


def _other_chips(x, y):
    return [(1 - x, y), (x, 1 - y), (1 - x, 1 - y)]


def _hier_gather_start(name, srcs, after, me):
    n = len(srcs)
    lands = _put_mine(name + "_mine", srcs, [False] * n, me)
    srcs = [pltpu.with_memory_space_constraint(a, pltpu.HBM) for a in srcs]
    lands = [pltpu.with_memory_space_constraint(a, pltpu.HBM) for a in lands]

    def body(*refs):
        ins, land = refs[:n], refs[n:2 * n]
        ici_send, ici_recv, d2d_send, d2d_recv = refs[2 * n + 1:2 * n + 5]
        token = refs[-1]
        x, y, c, me_in = _my_position()
        for i in range(n):
            pltpu.make_async_remote_copy(src_ref=ins[i], dst_ref=land[i].at[me_in], send_sem=d2d_send.at[i], recv_sem=d2d_recv.at[i],
                                         device_id=(x, y, 1 - c), device_id_type=MESH).start()
            for px, py in _other_chips(x, y):
                pltpu.make_async_remote_copy(src_ref=ins[i], dst_ref=land[i].at[me_in], send_sem=ici_send.at[i],
                                             recv_sem=ici_recv.at[i], device_id=(px, py, c), device_id_type=MESH).start()
        token[...] = jnp.zeros_like(token)

    sem = pltpu.SemaphoreType.DMA((n,))
    outs = pl.pallas_call(
        body, name=name,
        out_shape=(sem, sem, sem, sem, *[pltpu.HBM(a.shape, a.dtype) for a in srcs], *[pltpu.HBM(a.shape, a.dtype) for a in lands],
                   jax.ShapeDtypeStruct((8, 128), F32)),
        in_specs=[HBM_SPEC] * (2 * n) + [ANY],
        out_specs=(SEM_SPEC,) * 4 + (HBM_SPEC,) * (2 * n) + (pl.BlockSpec(memory_space=pltpu.VMEM),),
        input_output_aliases={i: 4 + i for i in range(2 * n)},
        compiler_params=pltpu.CompilerParams(has_side_effects=EFFECT),
    )(*srcs, *lands, after)
    return (outs[:4], outs[4:4 + n], outs[4 + n:4 + 2 * n]), outs[-1]


def _hier_gather_forward(name, handle, after):
    sems, srcs, lands = handle
    n = len(srcs)

    def body(*refs):
        land = refs[n:2 * n]
        ici_send, ici_recv, d2d_send, d2d_recv = refs[2 * n:2 * n + 4]
        x, y, c, _ = _my_position()
        for i in range(n):
            three = land[i].at[pl.ds(0, 3)]
            pltpu.make_async_remote_copy(src_ref=three, dst_ref=three, send_sem=ici_send.at[i], recv_sem=ici_recv.at[i],
                                         device_id=(x, y, c), device_id_type=MESH).wait_recv()
            for px, py in _other_chips(x, y):
                slab = land[i].at[4 * px + 2 * py + c]
                pltpu.make_async_remote_copy(src_ref=slab, dst_ref=slab, send_sem=d2d_send.at[i], recv_sem=d2d_recv.at[i],
                                             device_id=(x, y, 1 - c), device_id_type=MESH).start()

    outs = pl.pallas_call(
        body, name=name,
        out_shape=(*[pltpu.HBM(a.shape, a.dtype) for a in srcs], *[pltpu.HBM(a.shape, a.dtype) for a in lands]),
        in_specs=[HBM_SPEC] * (2 * n) + [SEM_SPEC] * 4 + [ANY],
        out_specs=tuple([HBM_SPEC] * (2 * n)),
        input_output_aliases={i: i for i in range(2 * n)},
        compiler_params=pltpu.CompilerParams(has_side_effects=EFFECT),
    )(*srcs, *lands, *sems, after)
    return (sems, outs[:n], outs[n:])


def _hier_gather_wait(name, handle, after):
    sems, srcs, lands = handle
    n = len(srcs)

    def body(*refs):
        land = refs[n:2 * n]
        ici_send, ici_recv, d2d_send, d2d_recv = refs[2 * n:2 * n + 4]
        x, y, c, _ = _my_position()
        for i in range(n):
            three, four = land[i].at[pl.ds(0, 3)], land[i].at[pl.ds(0, 4)]
            pltpu.make_async_remote_copy(src_ref=three, dst_ref=three, send_sem=ici_send.at[i], recv_sem=ici_recv.at[i],
                                         device_id=(x, y, c), device_id_type=MESH).wait_send()
            w = pltpu.make_async_remote_copy(src_ref=four, dst_ref=four, send_sem=d2d_send.at[i], recv_sem=d2d_recv.at[i],
                                             device_id=(x, y, c), device_id_type=MESH)
            w.wait_send()
            w.wait_recv()

    outs = pl.pallas_call(
        body, name=name,
        out_shape=(*[pltpu.HBM(a.shape, a.dtype) for a in srcs], *[pltpu.HBM(a.shape, a.dtype) for a in lands]),
        in_specs=[HBM_SPEC] * (2 * n) + [SEM_SPEC] * 4 + [ANY],
        out_specs=tuple([HBM_SPEC] * (2 * n)),
        input_output_aliases={i: i for i in range(2 * n)},
        compiler_params=pltpu.CompilerParams(has_side_effects=EFFECT),
    )(*srcs, *lands, *sems, after)
    return outs[n:]


def _pack(arrs):
    flat = jnp.concatenate([a.reshape(-1).astype(F32) for a in arrs])
    n = flat.shape[0]
    rows = -(-n // 1024) * 8
    return jnp.pad(flat, (0, rows * 128 - n)).reshape(rows, 128)


def _unpack(buf, shapes, lead=()):
    flat = buf.reshape(lead + (-1,))
    out, off = [], 0
    for s in shapes:
        n = math.prod(s)
        out.append(flat[..., off:off + n].reshape(lead + tuple(s)))
        off += n
    return out


def _mm(name, a, a_spec, b, b_spec, out_sds, o_spec, grid, contract, nk=1, stacked=0):
    o_blk = tuple(d for d in o_spec.block_shape if d is not None)

    def body(a_ref, b_ref, o_ref, *acc):
        if stacked:
            r = _dot(a_ref[0], b_ref[0], contract)
            for q in range(1, stacked):
                r = r + _dot(a_ref[q], b_ref[q], contract)
        else:
            r = _dot(a_ref[...], b_ref[...], contract)
        if nk == 1:
            o_ref[...] = r.astype(o_ref.dtype)
        else:
            k = pl.program_id(len(grid) - 1)

            @pl.when(k == 0)
            def _():
                acc[0][...] = r

            @pl.when(k > 0)
            def _():
                acc[0][...] += r

            @pl.when(k == nk - 1)
            def _():
                o_ref[...] = acc[0][...].astype(o_ref.dtype)

    sem = ("parallel",) * (len(grid) - 1) + (("arbitrary",) if nk > 1 else ("parallel",))
    return pl.pallas_call(
        body, name=name, out_shape=out_sds, grid=grid, in_specs=[a_spec, b_spec], out_specs=o_spec,
        scratch_shapes=[pltpu.VMEM(o_blk, F32)] if nk > 1 else [], compiler_params=_cp(*sem))(a, b)


def _tile(n, want):
    t = min(n, want)
    assert n % t == 0, (n, t)
    return t


def _mm_nn(name, a, b, out_dtype=F32, tm=512, tn=512):
    (M, K), N = a.shape, b.shape[1]
    tm, tn = _tile(M, tm), _tile(N, tn)
    return _mm(name, a, pl.BlockSpec((tm, K), lambda i, j: (i, 0)), b, pl.BlockSpec((K, tn), lambda i, j: (0, j)),
               jax.ShapeDtypeStruct((M, N), out_dtype), pl.BlockSpec((tm, tn), lambda i, j: (i, j)),
               (M // tm, N // tn), "nn")


def _mm_nt(name, a, b, out_dtype=F32, tm=512, tn=512):
    (M, K), N = a.shape, b.shape[0]
    tm, tn = _tile(M, tm), _tile(N, tn)
    return _mm(name, a, pl.BlockSpec((tm, K), lambda i, j: (i, 0)), b, pl.BlockSpec((tn, K), lambda i, j: (j, 0)),
               jax.ShapeDtypeStruct((M, N), out_dtype), pl.BlockSpec((tm, tn), lambda i, j: (i, j)),
               (M // tm, N // tn), "nt")


def _mm_tn(name, a, b, out_dtype=F32, tm=512, tn=512):
    (K, M), N = a.shape, b.shape[1]
    tm, tn = _tile(M, tm), _tile(N, tn)
    return _mm(name, a, pl.BlockSpec((K, tm), lambda i, j: (0, i)), b, pl.BlockSpec((K, tn), lambda i, j: (0, j)),
               jax.ShapeDtypeStruct((M, N), out_dtype), pl.BlockSpec((tm, tn), lambda i, j: (i, j)),
               (M // tm, N // tn), "tn")


def _mm_cols(name, a, w, out_dtype=F32, tm=512):
    (M, K), (J, _, n) = a.shape, w.shape
    tm = _tile(M, tm)
    return _mm(name, a, pl.BlockSpec((tm, K), lambda j, i: (i, 0)), w, pl.BlockSpec((None, K, n), lambda j, i: (j, 0, 0)),
               jax.ShapeDtypeStruct((J, M, n), out_dtype), pl.BlockSpec((None, tm, n), lambda j, i: (j, i, 0)),
               (J, M // tm), "nn")


def _mm_cols_dx(name, d, w, out_dtype=F32, tm=512, jb=None):
    (J, M, n), K = d.shape, w.shape[1]
    tm, jb = _tile(M, tm), J if jb is None else jb
    return _mm(name, d, pl.BlockSpec((jb, tm, n), lambda i, j: (j, i, 0)), w, pl.BlockSpec((jb, K, n), lambda i, j: (j, 0, 0)),
               jax.ShapeDtypeStruct((M, K), out_dtype), pl.BlockSpec((tm, K), lambda i, j: (i, 0)),
               (M // tm, J // jb), "nt", nk=J // jb, stacked=jb)


def _mm_cols_dw(name, a, d, out_dtype=F32, tk=512):
    (M, K), (J, _, n) = a.shape, d.shape
    tk = _tile(K, tk)
    return _mm(name, a, pl.BlockSpec((M, tk), lambda j, i: (0, i)), d, pl.BlockSpec((None, M, n), lambda j, i: (j, 0, 0)),
               jax.ShapeDtypeStruct((J, K, n), out_dtype), pl.BlockSpec((None, tk, n), lambda j, i: (j, i, 0)),
               (J, K // tk), "tn")


def _mm_cols_dwt(name, a, d, out_dtype=F32, tk=512):
    (M, K), (J, _, n) = a.shape, d.shape
    tk = _tile(K, tk)
    return _mm(name, d, pl.BlockSpec((None, M, n), lambda j, i: (j, 0, 0)), a, pl.BlockSpec((M, tk), lambda j, i: (0, i)),
               jax.ShapeDtypeStruct((J, n, K), out_dtype), pl.BlockSpec((None, n, tk), lambda j, i: (j, 0, i)),
               (J, K // tk), "tn")


def _mm_rows_resid(name, a, w, resid, gate, tm=512):
    (Q, M, k), N = a.shape, w.shape[2]
    tm = _tile(M, tm)

    def body(a_ref, w_ref, r_ref, g_ref, y_ref, x_ref):
        y = _dot(a_ref[0], w_ref[0], "nn")
        for q in range(1, Q):
            y = y + _dot(a_ref[q], w_ref[q], "nn")
        y_ref[...] = y.astype(y_ref.dtype)
        x_ref[...] = r_ref[...] + g_ref[...] * y

    return pl.pallas_call(
        body, name=name, grid=(M // tm,),
        out_shape=(jax.ShapeDtypeStruct((M, N), ACT_DTYPE), jax.ShapeDtypeStruct((M, N), F32)),
        in_specs=[pl.BlockSpec((Q, tm, k), lambda i: (0, i, 0)), pl.BlockSpec((Q, k, N), lambda i: (0, 0, 0)),
                  pl.BlockSpec((tm, N), lambda i: (i, 0)), pl.BlockSpec((1, N), lambda i: (0, 0))],
        out_specs=(pl.BlockSpec((tm, N), lambda i: (i, 0)), pl.BlockSpec((tm, N), lambda i: (i, 0))),
        compiler_params=_cp("parallel"))(a, w, resid, gate)


def _mm_rows_dx(name, d, w, out_dtype=F32, tm=512):
    (M, N), (Q, k, _) = d.shape, w.shape
    tm = _tile(M, tm)
    return _mm(name, d, pl.BlockSpec((tm, N), lambda q, i: (i, 0)), w, pl.BlockSpec((None, k, N), lambda q, i: (q, 0, 0)),
               jax.ShapeDtypeStruct((Q, M, k), out_dtype), pl.BlockSpec((None, tm, k), lambda q, i: (q, i, 0)),
               (Q, M // tm), "nt")


def _mm_rows_dw(name, a, d, out_dtype=F32, tn=512):
    (Q, M, k), N = a.shape, d.shape[1]
    tn = _tile(N, tn)
    return _mm(name, a, pl.BlockSpec((None, M, k), lambda q, j: (q, 0, 0)), d, pl.BlockSpec((M, tn), lambda q, j: (0, j)),
               jax.ShapeDtypeStruct((Q, k, N), out_dtype), pl.BlockSpec((None, k, tn), lambda q, j: (q, 0, j)),
               (Q, N // tn), "tn")


def _silu(v):
    return v * jax.nn.sigmoid(v)


def _ada_fwd(c16, ada_w):
    L, D, n = ada_w.shape

    def body(c_ref, w_ref, o_ref):
        o_ref[...] = _dot(_silu(c_ref[...]), w_ref[...], "nn")

    return pl.pallas_call(
        body, name="ada_fwd", grid=(L,), out_shape=jax.ShapeDtypeStruct((L, 16, n), F32),
        in_specs=[pl.BlockSpec((16, D), lambda l: (0, 0)), pl.BlockSpec((None, D, n), lambda l: (l, 0, 0))],
        out_specs=pl.BlockSpec((None, 16, n), lambda l: (l, 0, 0)), compiler_params=_cp("parallel"))(c16, ada_w)


def _ada_bwd(c16, dmod16):
    L, _, n = dmod16.shape
    D = c16.shape[1]

    def body(c_ref, d_ref, o_ref):
        o_ref[...] = _dot(_silu(c_ref[...]), d_ref[...], "tn")

    return pl.pallas_call(
        body, name="ada_bwd", grid=(L,), out_shape=jax.ShapeDtypeStruct((L, D, n), F32),
        in_specs=[pl.BlockSpec((16, D), lambda l: (0, 0)), pl.BlockSpec((None, 16, n), lambda l: (l, 0, 0))],
        out_specs=pl.BlockSpec((None, D, n), lambda l: (l, 0, 0)), compiler_params=_cp("parallel"))(c16, dmod16)


def _row_spec(tr, n):
    return pl.BlockSpec((tr, n), lambda i: (i, 0))


def _vec_spec(n):
    return pl.BlockSpec((1, n), lambda i: (0, 0))


def _rmsmod_fwd(name, x, g, sc, sh, after, tr=512):
    S, D = x.shape

    def body(x_ref, g_ref, sc_ref, sh_ref, after_ref, h_ref):
        xv = x_ref[...]
        rstd = lax.rsqrt(jnp.mean(xv * xv, axis=-1, keepdims=True) + EPS)
        y = xv * rstd * g_ref[...]
        h_ref[...] = (y * (1.0 + sc_ref[...]) + sh_ref[...]).astype(h_ref.dtype)

    return pl.pallas_call(
        body, name=name, grid=(S // tr,), out_shape=jax.ShapeDtypeStruct((S, D), _MXU_DTYPE),
        in_specs=[_row_spec(tr, D), _vec_spec(D), _vec_spec(D), _vec_spec(D), ANY], out_specs=_row_spec(tr, D),
        compiler_params=_cp("parallel"))(x, g, sc, sh, after)


def _acc_rows(ref, val, first):
    s = jnp.sum(val, axis=0, keepdims=True)

    @pl.when(first)
    def _():
        ref[...] = s

    @pl.when(jnp.logical_not(first))
    def _():
        ref[...] += s


def _gate_bwd_tail(dx, y_ref, gate_ref, dy_ref, dgate_ref, first):
    dy_ref[...] = (gate_ref[...] * dx).astype(dy_ref.dtype)
    _acc_rows(dgate_ref, dx * y_ref[...].astype(F32), first)


def _rmsmod_bwd(name, x, g, sc, dh, dres, after, y=None, gate=None, tr=512):
    S, D = x.shape
    tail = y is not None

    def body(x_ref, g_ref, sc_ref, dh_ref, dres_ref, after_ref, *rest):
        (y_ref, gate_ref), rest = (rest[:2], rest[2:]) if tail else ((None, None), rest)
        dx_ref, dg_ref, dsc_ref, dsh_ref = rest[:4]
        first = pl.program_id(0) == 0
        xv, dh_v, gv = x_ref[...], dh_ref[...].astype(F32), g_ref[...]
        rstd = lax.rsqrt(jnp.mean(xv * xv, axis=-1, keepdims=True) + EPS)
        xhat = xv * rstd
        _acc_rows(dsh_ref, dh_v, first)
        _acc_rows(dsc_ref, dh_v * (xhat * gv), first)
        dyg = dh_v * (1.0 + sc_ref[...])
        _acc_rows(dg_ref, dyg * xhat, first)
        dxhat = dyg * gv
        dx = dres_ref[...] + rstd * (dxhat - xhat * jnp.mean(dxhat * xhat, axis=-1, keepdims=True))
        dx_ref[...] = dx
        if tail:
            _gate_bwd_tail(dx, y_ref, gate_ref, rest[4], rest[5], first)

    vec = jax.ShapeDtypeStruct((1, D), F32)
    return pl.pallas_call(
        body, name=name, grid=(S // tr,),
        out_shape=(jax.ShapeDtypeStruct((S, D), F32), vec, vec, vec) + ((jax.ShapeDtypeStruct((S, D), _MXU_DTYPE), vec) if tail else ()),
        in_specs=[_row_spec(tr, D), _vec_spec(D), _vec_spec(D), _row_spec(tr, D), _row_spec(tr, D), ANY]
        + ([_row_spec(tr, D), _vec_spec(D)] if tail else []),
        out_specs=(_row_spec(tr, D), _vec_spec(D), _vec_spec(D), _vec_spec(D)) + ((_row_spec(tr, D), _vec_spec(D)) if tail else ()),
        compiler_params=_cp("arbitrary"))(x, g, sc, dh, dres, after, *((y, gate) if tail else ()))


def _loss_head(x, g, target, y, gate, tr=512):
    S, D = x.shape

    def body(x_ref, g_ref, t_ref, y_ref, gate_ref, loss_ref, dx_ref, dg_ref, dy_ref, dgate_ref):
        first = pl.program_id(0) == 0
        xv, gv = x_ref[...], g_ref[...]
        rstd = lax.rsqrt(jnp.mean(xv * xv, axis=-1, keepdims=True) + EPS)
        xhat = xv * rstd
        err = xhat * gv - t_ref[...]
        part = 0.5 * jnp.sum(jnp.mean(err * err, axis=-1, keepdims=True), axis=0, keepdims=True)

        @pl.when(first)
        def _():
            loss_ref[...] = part

        @pl.when(jnp.logical_not(first))
        def _():
            loss_ref[...] += part

        dout = err * (1.0 / D)
        _acc_rows(dg_ref, dout * xhat, first)
        dxhat = dout * gv
        dx = rstd * (dxhat - xhat * jnp.mean(dxhat * xhat, axis=-1, keepdims=True))
        dx_ref[...] = dx
        _gate_bwd_tail(dx, y_ref, gate_ref, dy_ref, dgate_ref, first)

    vec = jax.ShapeDtypeStruct((1, D), F32)
    return pl.pallas_call(
        body, name="loss_head", grid=(S // tr,),
        out_shape=(jax.ShapeDtypeStruct((1, 1), F32), jax.ShapeDtypeStruct((S, D), F32), vec,
                   jax.ShapeDtypeStruct((S, D), _MXU_DTYPE), vec),
        in_specs=[_row_spec(tr, D), _vec_spec(D), _row_spec(tr, D), _row_spec(tr, D), _vec_spec(D)],
        out_specs=(pl.BlockSpec((1, 1), lambda i: (0, 0)), _row_spec(tr, D), _vec_spec(D), _row_spec(tr, D), _vec_spec(D)),
        compiler_params=_cp("arbitrary"))(x, g, target, y, gate)


def _shift_down(v, k):
    t = lax.broadcasted_iota(jnp.int32, v.shape, 0)
    return jnp.where(t >= k, pltpu.roll(v, k, axis=0), 0.0)


def _shift_up(v, k):
    n = v.shape[0]
    t = lax.broadcasted_iota(jnp.int32, v.shape, 0)
    return jnp.where(t < n - k, pltpu.roll(v, n - k, axis=0), 0.0)


def _window_sum(p, w, shift):
    s, k = p, 1
    while k < w:
        s = s + shift(s, k)
        k *= 2
    return s


def _pool_count(shape, w):
    t = lax.broadcasted_iota(jnp.int32, shape, 0)
    return jnp.minimum(t + 1, w).astype(F32)


def _ab_specs(S):
    zs = [pl.BlockSpec((None, S, 128), functools.partial(lambda g, q: (2 * q + g // 2, 0, g % 2), q=q)) for q in range(4)]
    return zs


def _ab_mix_fwd(z8, conv_w, mix_w, scale):
    S = z8.shape[1]

    def body(b_ref, c_ref, a_ref, p_ref, w_ref, mix_ref, sc_ref, y_ref):
        g = pl.program_id(0)
        cg = c_ref[...].astype(F32) * a_ref[...].astype(F32)
        w = w_ref[...]
        conv = w[0:1] * _shift_down(cg, 2) + w[1:2] * _shift_down(cg, 1) + w[2:3] * cg
        y_ref[0] = (b_ref[...].astype(F32) * conv).astype(y_ref.dtype)
        for gg, win in enumerate(POOL_WINDOWS):
            @pl.when(g == gg)
            def _(win=win):
                p = p_ref[...].astype(F32)
                pooled = _window_sum(p, win, _shift_down) / _pool_count(p.shape, win) - p
                y_ref[1] = (_dot(pooled, mix_ref[...], "nn") * sc_ref[...]).astype(y_ref.dtype)

    return pl.pallas_call(
        body, name="ab_mix_fwd", grid=(4,), out_shape=jax.ShapeDtypeStruct((2, S, 512), _MXU_DTYPE),
        in_specs=_ab_specs(S) + [pl.BlockSpec((3, 128), lambda g: (0, g)), pl.BlockSpec((None, 128, 128), lambda g: (g, 0, 0)),
                                 pl.BlockSpec((1, 128), lambda g: (0, g))],
        out_specs=pl.BlockSpec((2, S, 128), lambda g: (0, 0, g)), compiler_params=_cp("parallel"))(z8, z8, z8, z8, conv_w, mix_w, scale)


def _ab_mix_bwd(z8, dycat2, conv_w, mix_w, scale, after):
    S = z8.shape[1]

    def body(b_ref, c_ref, a_ref, p_ref, dy_ref, w_ref, mix_ref, sc_ref, after_ref, dz_ref, dw_ref, dmix_ref, dsc_ref):
        g = pl.program_id(0)
        bv, cv, av, w = b_ref[...].astype(F32), c_ref[...].astype(F32), a_ref[...].astype(F32), w_ref[...]
        dya = dy_ref[0]
        cg = cv * av
        cg1, cg2 = _shift_down(cg, 1), _shift_down(cg, 2)
        conv = w[0:1] * cg2 + w[1:2] * cg1 + w[2:3] * cg
        dz_ref[0] = (dya * conv).astype(dz_ref.dtype)
        dconv = dya * bv
        dcg = w[2:3] * dconv + w[1:2] * _shift_up(dconv, 1) + w[0:1] * _shift_up(dconv, 2)
        dz_ref[1] = (dcg * av).astype(dz_ref.dtype)
        dz_ref[2] = (dcg * cv).astype(dz_ref.dtype)
        dw_ref[0:1, :] = jnp.sum(dconv * cg2, axis=0, keepdims=True)
        dw_ref[1:2, :] = jnp.sum(dconv * cg1, axis=0, keepdims=True)
        dw_ref[2:3, :] = jnp.sum(dconv * cg, axis=0, keepdims=True)
        for gg, win in enumerate(POOL_WINDOWS):
            @pl.when(g == gg)
            def _(win=win):
                p, dyb, mix = p_ref[...].astype(F32), dy_ref[1], mix_ref[...]
                cnt = _pool_count(p.shape, win)
                pooled = _window_sum(p, win, _shift_down) / cnt - p
                dsc_ref[...] = jnp.sum(dyb * _dot(pooled, mix, "nn"), axis=0, keepdims=True)
                dmixed = dyb * sc_ref[...]
                dmix_ref[...] = _dot(pooled, dmixed, "tn")
                dpooled = _dot(dmixed, mix, "nt")
                dz_ref[3] = (_window_sum(dpooled / cnt, win, _shift_up) - dpooled).astype(dz_ref.dtype)

    return pl.pallas_call(
        body, name="ab_mix_bwd", grid=(4,),
        out_shape=(jax.ShapeDtypeStruct((4, 2, S, 256), _MXU_DTYPE), jax.ShapeDtypeStruct((3, 512), F32),
                   jax.ShapeDtypeStruct((4, 128, 128), F32), jax.ShapeDtypeStruct((1, 512), F32)),
        in_specs=_ab_specs(S) + [pl.BlockSpec((2, S, 128), lambda g: (0, 0, g)), pl.BlockSpec((3, 128), lambda g: (0, g)),
                                 pl.BlockSpec((None, 128, 128), lambda g: (g, 0, 0)), pl.BlockSpec((1, 128), lambda g: (0, g)), ANY],
        out_specs=(pl.BlockSpec((4, None, S, 128), lambda g: (0, g // 2, 0, g % 2)), pl.BlockSpec((3, 128), lambda g: (0, g)),
                   pl.BlockSpec((None, 128, 128), lambda g: (g, 0, 0)), pl.BlockSpec((1, 128), lambda g: (0, g))),
        compiler_params=_cp("parallel"))(z8, z8, z8, z8, dycat2, conv_w, mix_w, scale, after)


HALO = 16


def _ffn_specs(S, n, tr):
    nb = S // HALO
    tile = pl.BlockSpec((2, None, tr, n), lambda j, i: (0, j, i, 0))
    prev = pl.BlockSpec((2, None, HALO, n), lambda j, i: (0, j, jnp.maximum(i * (tr // HALO) - 1, 0), 0))
    nxt = pl.BlockSpec((2, None, HALO, n), lambda j, i: (0, j, jnp.minimum((i + 1) * (tr // HALO), nb - 1), 0))
    cw = pl.BlockSpec((2, None, 3, n), lambda j, i: (0, j, 0, 0))
    return tile, prev, nxt, cw


def _shifted_rows(ext, lo, rows):
    ext = ext.astype(F32)
    return pltpu.roll(ext, 1, axis=0)[lo:lo + rows], pltpu.roll(ext, 2, axis=0)[lo:lo + rows]


def _ffn_gate_fwd(name, u24, cw24, tr=256):
    _, J, S, n = u24.shape
    tile, prev, _, cw = _ffn_specs(S, n, tr)

    def body(u_ref, up_ref, w_ref, a_ref, z_ref):
        keep = (pl.program_id(1) > 0).astype(u_ref.dtype)
        z = []
        for h in range(2):
            ext = jnp.concatenate([up_ref[h] * keep, u_ref[h]], axis=0)
            x1, x2 = _shifted_rows(ext, HALO, tr)
            w = w_ref[h]
            z.append(w[0:1] * x2 + w[1:2] * x1 + w[2:3] * u_ref[h].astype(F32))
        zg, zu = z
        sg = jax.nn.sigmoid(zg)
        silu = zg * sg
        a_ref[...] = (silu * zu).astype(a_ref.dtype)
        z_ref[0] = (zu * (sg * (1.0 + zg * (1.0 - sg)))).astype(z_ref.dtype)
        z_ref[1] = silu.astype(z_ref.dtype)

    return pl.pallas_call(
        body, name=name, grid=(J, S // tr),
        out_shape=(jax.ShapeDtypeStruct((J, S, n), _MXU_DTYPE), jax.ShapeDtypeStruct((2, J, S, n), ACT_DTYPE)),
        in_specs=[tile, prev, cw], out_specs=(pl.BlockSpec((None, tr, n), lambda j, i: (j, i, 0)), tile),
        compiler_params=_cp("parallel", "parallel"))(u24, u24, cw24)


def _ffn_gate_bwd(name, u24, z24, cw24, da4, w_up24, after, tr=256):
    _, J, S, n = u24.shape
    K = w_up24.shape[2]
    nb = S // HALO
    tile = pl.BlockSpec((2, None, tr, n), lambda i, j: (0, j, i, 0))
    nxt = pl.BlockSpec((2, None, HALO, n), lambda i, j: (0, j, jnp.minimum((i + 1) * (tr // HALO), nb - 1), 0))
    whole = lambda shape: pl.BlockSpec(shape, lambda i, j: (0,) * len(shape))

    def body(u_ref, z_ref, zn_ref, cw_ref, da_ref, dan_ref, wup_ref, after_ref, du_ref, dcw_ref, dh_ref, acc_ref):
        i, j = pl.program_id(0), pl.program_id(1)
        first = i == 0
        keep_next = (i < S // tr - 1).astype(F32)
        w = [cw_ref[h, j] for h in range(2)]
        m = tr + HALO
        da = jnp.concatenate([da_ref[...].astype(F32), dan_ref[...].astype(F32) * keep_next], axis=0)
        dz = [da * jnp.concatenate([z_ref[h], zn_ref[h]], axis=0).astype(F32) for h in range(2)]
        dh = None
        for h in range(2):
            d = dz[h]
            d0, d1, d2 = d[:tr], pltpu.roll(d, m - 1, axis=0)[:tr], pltpu.roll(d, m - 2, axis=0)[:tr]
            du = (w[h][2:3] * d0 + w[h][1:2] * d1 + w[h][0:1] * d2).astype(du_ref.dtype)
            du_ref[h] = du
            part = _dot(du, wup_ref[h, j], "nt")
            dh = part if dh is None else dh + part
            x0 = u_ref[h].astype(F32)
            parts = [jnp.sum(x0 * dk, axis=0, keepdims=True) for dk in (d2, d1, d0)]
            for k in range(3):
                @pl.when(first)
                def _(k=k, h=h):
                    dcw_ref[h, j, k:k + 1, :] = parts[k]

                @pl.when(jnp.logical_not(first))
                def _(k=k, h=h):
                    dcw_ref[h, j, k:k + 1, :] += parts[k]

        @pl.when(j == 0)
        def _():
            acc_ref[...] = dh

        @pl.when(j > 0)
        def _():
            acc_ref[...] += dh

        @pl.when(j == J - 1)
        def _():
            dh_ref[...] = acc_ref[...].astype(dh_ref.dtype)

    da_tile = pl.BlockSpec((None, tr, n), lambda i, j: (j, i, 0))
    da_next = pl.BlockSpec((None, HALO, n), lambda i, j: (j, jnp.minimum((i + 1) * (tr // HALO), nb - 1), 0))
    return pl.pallas_call(
        body, name=name, grid=(S // tr, J),
        out_shape=(jax.ShapeDtypeStruct((2, J, S, n), _MXU_DTYPE), jax.ShapeDtypeStruct((2, J, 3, n), F32),
                   jax.ShapeDtypeStruct((S, K), ACT_DTYPE)),
        in_specs=[tile, tile, nxt, whole((2, J, 3, n)), da_tile, da_next, whole((2, J, K, n)), ANY],
        out_specs=(tile, whole((2, J, 3, n)), pl.BlockSpec((tr, K), lambda i, j: (i, 0))),
        scratch_shapes=[pltpu.VMEM((tr, K), F32)],
        compiler_params=_cp("arbitrary", "arbitrary"))(u24, z24, z24, cw24, da4, da4, w_up24, after)


def _rms_rows(v, g):
    rstd = lax.rsqrt(jnp.mean(v * v, axis=-1, keepdims=True) + EPS)
    return v * rstd * g


def _rms_rows_bwd(v, g, dy):
    rstd = lax.rsqrt(jnp.mean(v * v, axis=-1, keepdims=True) + EPS)
    vhat = v * rstd
    dvhat = dy * g
    return rstd * (dvhat - vhat * jnp.mean(dvhat * vhat, axis=-1, keepdims=True)), dy * vhat


def _rope(v, cos, sa, sb):
    return v * cos + pltpu.roll(v, 112, axis=1) * sa + pltpu.roll(v, 16, axis=1) * sb


def _rope_t(d, cos, sa, sb):
    return d * cos + pltpu.roll(d * sa, 16, axis=1) + pltpu.roll(d * sb, 112, axis=1)


def _qkv_rope_fwd(z, qg, kvg, w_uq_t, w_kv, cosq, cosk, sa, sb, tr=512):
    S = z.shape[0]

    def body(ql_ref, kvl_ref, kpe_ref, qg_ref, kvg_ref, wq_ref, wkv_ref, cq_ref, ck_ref, sa_ref, sb_ref,
             qn_ref, kvn_ref, qo_ref, ko_ref, vo_ref):
        cq, ck, sa_v, sb_v = cq_ref[...], ck_ref[...], sa_ref[...], sb_ref[...]
        qn = _rms_rows(ql_ref[...], qg_ref[...]).astype(qn_ref.dtype)
        kvn = _rms_rows(kvl_ref[...], kvg_ref[...]).astype(kvn_ref.dtype)
        qn_ref[...] = qn
        kvn_ref[...] = kvn
        q = _dot(qn, wq_ref[...], "nt")
        kv = _dot(kvn, wkv_ref[...], "nn")
        kpe = _rope(kpe_ref[...], ck, sa_v, sb_v)
        for h in range(8):
            cols = slice(128 * h, 128 * h + 128)
            qo_ref[:, cols] = _rope(q[:, cols], cq, sa_v, sb_v).astype(qo_ref.dtype)
            ko_ref[:, cols] = (kv[:, cols] + kpe).astype(ko_ref.dtype)
        vo_ref[...] = kv[:, 1024:1536].astype(vo_ref.dtype)

    tab = _row_spec(tr, 128)
    whole = lambda a: pl.BlockSpec(a.shape, lambda i: (0, 0))
    return pl.pallas_call(
        body, name="qkv_rope_fwd", grid=(S // tr,),
        out_shape=(jax.ShapeDtypeStruct((S, 256), _MXU_DTYPE), jax.ShapeDtypeStruct((S, 128), _MXU_DTYPE),
                   jax.ShapeDtypeStruct((S, 1024), _MXU_DTYPE), jax.ShapeDtypeStruct((S, 1024), _MXU_DTYPE),
                   jax.ShapeDtypeStruct((S, 512), _MXU_DTYPE)),
        in_specs=[pl.BlockSpec((tr, 256), lambda i: (i, 0)), pl.BlockSpec((tr, 128), lambda i: (i, 2)),
                  pl.BlockSpec((tr, 128), lambda i: (i, 3)), _vec_spec(256), _vec_spec(128), whole(w_uq_t), whole(w_kv),
                  tab, tab, tab, tab],
        out_specs=(_row_spec(tr, 256), _row_spec(tr, 128), _row_spec(tr, 1024), _row_spec(tr, 1024), _row_spec(tr, 512)),
        compiler_params=_cp("parallel"))(z, z, z, qg, kvg, w_uq_t, w_kv, cosq, cosk, sa, sb)


def _attn_bwd_prep(o, dycat2, tr=512):
    S = o.shape[0]

    def body(o_ref, do_ref, delta_ref, doa_ref, dob_ref):
        do = do_ref[...]
        prod = do * o_ref[...]
        lane = lax.broadcasted_iota(jnp.int32, do.shape, 1)
        for p in range(4):
            cols = slice(128 * p, 128 * p + 128)
            first = lane[:, cols] < 128 * p + 64
            da = jnp.sum(jnp.where(first, prod[:, cols], 0.0), axis=-1, keepdims=True)
            db = jnp.sum(jnp.where(first, 0.0, prod[:, cols]), axis=-1, keepdims=True)
            delta_ref[p] = jnp.where(first, da, db)
            doa_ref[p] = jnp.where(first, do[:, cols], 0.0).astype(doa_ref.dtype)
            dob_ref[p] = jnp.where(first, 0.0, do[:, cols]).astype(dob_ref.dtype)

    pair = pl.BlockSpec((4, tr, 128), lambda i: (0, i, 0))
    return pl.pallas_call(
        body, name="attn_bwd_prep", grid=(S // tr,),
        out_shape=(jax.ShapeDtypeStruct((4, S, 128), F32), jax.ShapeDtypeStruct((4, S, 128), _MXU_DTYPE),
                   jax.ShapeDtypeStruct((4, S, 128), _MXU_DTYPE)),
        in_specs=[_row_spec(tr, 512), pl.BlockSpec((None, tr, 512), lambda i: (0, i, 0))],
        out_specs=(pair, pair, pair), compiler_params=_cp("parallel"))(o, dycat2)


def _qkv_rope_bwd(z, qg, kvg, dq, dk, dv, duv, w_uq_t, w_kv, cosq, cosk, sa, sb, tr=512):
    S = z.shape[0]

    def body(ql_ref, kvl_ref, qg_ref, kvg_ref, dq_ref, dk_ref, dv_ref, duv_ref, wq_ref, wkv_ref, cq_ref, ck_ref, sa_ref, sb_ref,
             dqo_ref, dkv_ref, dz_ref, dqg_ref, dkvg_ref):
        first = pl.program_id(0) == 0
        cq, ck, sa_v, sb_v = cq_ref[...], ck_ref[...], sa_ref[...], sb_ref[...]
        tot = jnp.zeros((tr, 128), F32)
        for h in range(8):
            cols = slice(128 * h, 128 * h + 128)
            dqo_ref[:, cols] = _rope_t(dq_ref[:, cols], cq, sa_v, sb_v).astype(dqo_ref.dtype)
            dkh = dk_ref[:, cols]
            tot = tot + dkh
            dkv_ref[:, cols] = dkh.astype(dkv_ref.dtype)
        dkv_ref[:, 1024:1536] = dv_ref[...].astype(dkv_ref.dtype)
        dqn = _dot(dqo_ref[...], wq_ref[...], "nn")
        dkvn = _dot(dkv_ref[...], wkv_ref[...], "nt")
        dql, dqg = _rms_rows_bwd(ql_ref[...], qg_ref[...], dqn)
        dkvl, dkvg = _rms_rows_bwd(kvl_ref[...], kvg_ref[...], dkvn)
        _acc_rows(dqg_ref, dqg, first)
        _acc_rows(dkvg_ref, dkvg, first)
        dz_ref[:, 0:256] = dql.astype(dz_ref.dtype)
        dz_ref[:, 256:384] = dkvl.astype(dz_ref.dtype)
        dz_ref[:, 384:512] = _rope_t(tot, ck, sa_v, sb_v).astype(dz_ref.dtype)
        dz_ref[:, 512:1536] = duv_ref[...].astype(dz_ref.dtype)

    tab = _row_spec(tr, 128)
    whole = lambda a: pl.BlockSpec(a.shape, lambda i: (0, 0))
    return pl.pallas_call(
        body, name="qkv_rope_bwd", grid=(S // tr,),
        out_shape=(jax.ShapeDtypeStruct((S, 1024), _MXU_DTYPE), jax.ShapeDtypeStruct((S, 1536), _MXU_DTYPE),
                   jax.ShapeDtypeStruct((S, 1536), _MXU_DTYPE), jax.ShapeDtypeStruct((1, 256), F32), jax.ShapeDtypeStruct((1, 128), F32)),
        in_specs=[pl.BlockSpec((tr, 256), lambda i: (i, 0)), pl.BlockSpec((tr, 128), lambda i: (i, 2)), _vec_spec(256), _vec_spec(128),
                  _row_spec(tr, 1024), _row_spec(tr, 1024), _row_spec(tr, 512), _row_spec(tr, 1024), whole(w_uq_t), whole(w_kv),
                  tab, tab, tab, tab],
        out_specs=(_row_spec(tr, 1024), _row_spec(tr, 1536), _row_spec(tr, 1536), _vec_spec(256), _vec_spec(128)),
        compiler_params=_cp("arbitrary"))(z, z, qg, kvg, dq, dk, dv, duv, w_uq_t, w_kv, cosq, cosk, sa, sb)


NEG = -1e30


def _attn_fwd(q, k, v, tq=512, tk=512):
    S = q.shape[0]
    assert tq == tk

    def body(q_ref, k_ref, v_ref, o_ref, lse_ref):
        i = pl.program_id(1)
        qs = [q_ref[:, 0:128], q_ref[:, 128:256]]

        def step(kb, carry, diagonal=False):
            start = pl.multiple_of(kb * tk, tk)
            vv = v_ref[pl.ds(start, tk), :]
            out = []
            for h in range(2):
                m, l, acc = carry[3 * h:3 * h + 3]
                s = _dot(qs[h], k_ref[pl.ds(start, tk), 128 * h:128 * h + 128], "nt") * ATTN_SCALE
                if diagonal:
                    s = jnp.where(below, s, NEG)
                m_new = jnp.maximum(m, jnp.max(s, axis=-1, keepdims=True))
                alpha = jnp.exp(m - m_new)
                p = jnp.exp(s - m_new)
                out += [m_new, alpha * l + jnp.sum(p, axis=-1, keepdims=True), alpha * acc + _dot(p, vv, "nn")]
            return tuple(out)

        below = lax.broadcasted_iota(jnp.int32, (tq, tk), 1) <= lax.broadcasted_iota(jnp.int32, (tq, tk), 0)
        init = (jnp.full((tq, 1), NEG, F32), jnp.zeros((tq, 1), F32), jnp.zeros((tq, 128), F32)) * 2
        ma, la, acca, mb, lb, accb = step(i, lax.fori_loop(0, i, step, init), diagonal=True)
        lane = lax.broadcasted_iota(jnp.int32, (tq, 128), 1)
        o_ref[...] = jnp.where(lane < 64, acca / la, accb / lb)
        lse_ref[...] = jnp.where(lane < 64, ma + jnp.log(la), mb + jnp.log(lb))

    return pl.pallas_call(
        body, name="attn_fwd", grid=(4, S // tq),
        out_shape=(jax.ShapeDtypeStruct((S, 512), F32), jax.ShapeDtypeStruct((4, S, 128), F32)),
        in_specs=[pl.BlockSpec((tq, 256), lambda p, i: (i, p)), pl.BlockSpec((S, 256), lambda p, i: (0, p)),
                  pl.BlockSpec((S, 128), lambda p, i: (0, p))],
        out_specs=(pl.BlockSpec((tq, 128), lambda p, i: (i, p)), pl.BlockSpec((None, tq, 128), lambda p, i: (p, i, 0))),
        compiler_params=_cp("parallel", "parallel"))(q, k, v)


def _attn_bwd(q, k, v, lse, delta, doa, dob, tq=512, tk=512):
    S = q.shape[0]
    assert tq == tk

    def body(q_ref, k_ref, v_ref, lse_ref, delta_ref, doa_ref, dob_ref, dq_ref, dk_ref, dv_ref):
        j = pl.program_id(1)

        @pl.when(j == 0)
        def _():
            dq_ref[...] = jnp.zeros_like(dq_ref)

        below = lax.broadcasted_iota(jnp.int32, (tq, tk), 1) <= lax.broadcasted_iota(jnp.int32, (tq, tk), 0)
        ks = [k_ref[:, 0:128], k_ref[:, 128:256]]
        vv = v_ref[...]

        def step(qb, carry, diagonal=False):
            dka, dkb, dvp = carry
            start = pl.multiple_of(qb * tq, tq)
            rows = pl.ds(start, tq)
            lse_v, delta_v = lse_ref[rows, :], delta_ref[rows, :]
            dos = [doa_ref[rows, :], dob_ref[rows, :]]
            dks = [dka, dkb]
            for h in range(2):
                delta = delta_v[:, 64 * h:64 * h + 1]
                do_h = dos[h]
                qh = q_ref[rows, 128 * h:128 * h + 128]
                s = _dot(qh, ks[h], "nt") * ATTN_SCALE
                p = jnp.exp(s - lse_v[:, 64 * h:64 * h + 1])
                if diagonal:
                    p = jnp.where(below, p, 0.0)
                dvp = dvp + _dot(p, do_h, "tn")
                ds = p * (_dot(do_h, vv, "nt") - delta) * ATTN_SCALE
                dq_ref[rows, 128 * h:128 * h + 128] += _dot(ds, ks[h], "nn")
                dks[h] = dks[h] + _dot(ds, qh, "tn")
            return dks[0], dks[1], dvp

        zero = jnp.zeros((tk, 128), F32)
        dka, dkb, dvp = lax.fori_loop(j + 1, S // tq, step, step(j, (zero, zero, zero), diagonal=True))
        dk_ref[:, 0:128] = dka
        dk_ref[:, 128:256] = dkb
        dv_ref[...] = dvp

    return pl.pallas_call(
        body, name="attn_bwd", grid=(4, S // tk),
        out_shape=(jax.ShapeDtypeStruct((S, 1024), F32), jax.ShapeDtypeStruct((S, 1024), F32), jax.ShapeDtypeStruct((S, 512), F32)),
        in_specs=[pl.BlockSpec((S, 256), lambda p, j: (0, p)), pl.BlockSpec((tk, 256), lambda p, j: (j, p)),
                  pl.BlockSpec((tk, 128), lambda p, j: (j, p))] + [pl.BlockSpec((None, S, 128), lambda p, j: (p, 0, 0))] * 4,
        out_specs=(pl.BlockSpec((S, 256), lambda p, j: (0, p)), pl.BlockSpec((tk, 256), lambda p, j: (j, p)),
                   pl.BlockSpec((tk, 128), lambda p, j: (j, p))),
        compiler_params=_cp("parallel", "arbitrary"))(q, k, v, lse, delta, doa, dob)


CHUNK = 128
GELU_C = math.sqrt(2.0 / math.pi)


def _gelu(v):
    t = jnp.tanh(GELU_C * (v + 0.044715 * (v * v * v)))
    return v * (0.5 * (1.0 + t)), t


def _gelu_grad(v, t):
    return 0.5 * (1.0 + t) + v * (0.5 * (1.0 - t * t) * GELU_C * (1.0 + 3.0 * 0.044715 * v * v))


def _tril(w):
    r = lax.broadcasted_iota(jnp.int32, w.shape, 0)
    c = lax.broadcasted_iota(jnp.int32, w.shape, 1)
    return jnp.where(c <= r, w, 0.0)


def _layer_norm(v, g, b):
    xc = v - jnp.mean(v, axis=-1, keepdims=True)
    rstd = lax.rsqrt(jnp.mean(xc * xc, axis=-1, keepdims=True) + EPS)
    xhat = xc * rstd
    return xhat * g + b, xhat, rstd


def _sgu_fwd(z, o, ln_g, ln_b, w_s, b_st, tr=256):
    S = z.shape[0]

    def body(u_ref, v_ref, o_ref, g_ref, b_ref, ws_ref, bs_ref, y_ref):
        gu, _ = _gelu(u_ref[...])
        gv, _ = _gelu(v_ref[...])
        vln, _, _ = _layer_norm(gv, g_ref[...], b_ref[...])
        y_ref[0] = o_ref[...].astype(y_ref.dtype)
        for g in range(4):
            wt = _tril(ws_ref[g])
            cols = slice(128 * g, 128 * g + 128)
            for ch in range(tr // CHUNK):
                rows = slice(CHUNK * ch, CHUNK * ch + CHUNK)
                mixed = _dot(wt, vln[rows, cols], "nn") + bs_ref[:, g:g + 1]
                y_ref[1, rows, cols] = (gu[rows, cols] * mixed).astype(y_ref.dtype)

    return pl.pallas_call(
        body, name="sgu_fwd", grid=(S // tr,), out_shape=jax.ShapeDtypeStruct((2, S, 512), _MXU_DTYPE),
        in_specs=[pl.BlockSpec((tr, 512), lambda i: (i, 1)), pl.BlockSpec((tr, 512), lambda i: (i, 2)), _row_spec(tr, 512),
                  _vec_spec(512), _vec_spec(512), pl.BlockSpec((4, 128, 128), lambda i: (0, 0, 0)), pl.BlockSpec((128, 4), lambda i: (0, 0))],
        out_specs=pl.BlockSpec((2, tr, 512), lambda i: (0, i, 0)), compiler_params=_cp("parallel"))(z, z, o, ln_g, ln_b, w_s, b_st)


def _sgu_bwd(z, dycat2, ln_g, ln_b, w_s, b_st, tr=256):
    S = z.shape[0]

    def body(u_ref, v_ref, dy_ref, g_ref, b_ref, ws_ref, bs_ref, duv_ref, dg_ref, db_ref, dws_ref, dbs_ref):
        first = pl.program_id(0) == 0
        u_pre, v_pre = u_ref[...], v_ref[...]
        gu, tu = _gelu(u_pre)
        gv, tv = _gelu(v_pre)
        gain = g_ref[...]
        vln, xhat, rstd = _layer_norm(gv, gain, b_ref[...])

        @pl.when(first)
        def _():
            dws_ref[...] = jnp.zeros_like(dws_ref)
            dbs_ref[...] = jnp.zeros_like(dbs_ref)

        dvln_cols = []
        for g in range(4):
            wt = _tril(ws_ref[g])
            cols = slice(128 * g, 128 * g + 128)
            dmixed_sum = jnp.zeros((CHUNK, 128), F32)
            dw = jnp.zeros((CHUNK, CHUNK), F32)
            dvln_rows = []
            for ch in range(tr // CHUNK):
                rows = slice(CHUNK * ch, CHUNK * ch + CHUNK)
                vt = vln[rows, cols]
                mixed = _dot(wt, vt, "nn") + bs_ref[:, g:g + 1]
                dyd = dy_ref[rows, cols]
                duv_ref[rows, cols] = (dyd * mixed * _gelu_grad(u_pre[rows, cols], tu[rows, cols])).astype(duv_ref.dtype)
                dmixed = dyd * gu[rows, cols]
                dmixed_sum = dmixed_sum + dmixed
                dw = dw + _dot(dmixed, vt, "nt")
                dvln_rows.append(_dot(wt, dmixed, "tn"))
            dws_ref[g] += _tril(dw)
            dbs_ref[g:g + 1, :] += jnp.sum(dmixed_sum.T, axis=0, keepdims=True)
            dvln_cols.append(jnp.concatenate(dvln_rows, axis=0))
        dvln = jnp.concatenate(dvln_cols, axis=1)
        _acc_rows(dg_ref, dvln * xhat, first)
        _acc_rows(db_ref, dvln, first)
        dxhat = dvln * gain
        dgv = rstd * (dxhat - jnp.mean(dxhat, axis=-1, keepdims=True) - xhat * jnp.mean(dxhat * xhat, axis=-1, keepdims=True))
        duv_ref[:, 512:1024] = (dgv * _gelu_grad(v_pre, tv)).astype(duv_ref.dtype)

    return pl.pallas_call(
        body, name="sgu_bwd", grid=(S // tr,),
        out_shape=(jax.ShapeDtypeStruct((S, 1024), _MXU_DTYPE), jax.ShapeDtypeStruct((1, 512), F32), jax.ShapeDtypeStruct((1, 512), F32),
                   jax.ShapeDtypeStruct((4, 128, 128), F32), jax.ShapeDtypeStruct((4, 128), F32)),
        in_specs=[pl.BlockSpec((tr, 512), lambda i: (i, 1)), pl.BlockSpec((tr, 512), lambda i: (i, 2)),
                  pl.BlockSpec((None, tr, 512), lambda i: (1, i, 0)), _vec_spec(512), _vec_spec(512),
                  pl.BlockSpec((4, 128, 128), lambda i: (0, 0, 0)), pl.BlockSpec((128, 4), lambda i: (0, 0))],
        out_specs=(_row_spec(tr, 1024), _vec_spec(512), _vec_spec(512), pl.BlockSpec((4, 128, 128), lambda i: (0, 0, 0)),
                   pl.BlockSpec((4, 128), lambda i: (0, 0))),
        compiler_params=_cp("arbitrary"))(z, z, dycat2, ln_g, ln_b, w_s, b_st)


def _adamw_math(w, m, v, g):
    c1 = 1.0 / (1.0 - ADAM_B1 ** ADAM_STEP)
    c2 = 1.0 / (1.0 - ADAM_B2 ** ADAM_STEP)
    m2 = ADAM_B1 * m + (1.0 - ADAM_B1) * g
    v2 = ADAM_B2 * v + (1.0 - ADAM_B2) * (g * g)
    return -ADAM_LR * ((m2 * c1) / (jnp.sqrt(v2 * c2) + ADAM_EPS) + ADAM_WD * w), m2, v2


def _adamw_small(name, params, parts):
    n = len(params)

    def body(*refs):
        ins, outs = refs[:4 * n], refs[4 * n:]
        for i in range(n):
            w_ref, m_ref, v_ref, p_ref = ins[4 * i:4 * i + 4]
            g = p_ref[0].astype(F32)
            for k in range(1, N_DEV):
                g = g + p_ref[k].astype(F32)
            delta, m2, v2 = _adamw_math(w_ref[...], m_ref[...], v_ref[...], g)
            outs[4 * i][...] = g
            outs[4 * i + 1][...] = delta
            outs[4 * i + 2][...] = m2
            outs[4 * i + 3][...] = v2

    flat = [a for (w, m, v), p in zip(params, parts) for a in (w, m, v, p)]
    out = pl.pallas_call(
        body, name=name, out_shape=[jax.ShapeDtypeStruct(w.shape, F32) for (w, _, _) in params for _ in range(4)],
        compiler_params=pltpu.CompilerParams(vmem_limit_bytes=_VMEM_LIMIT))(*flat)
    return [out[4 * i:4 * i + 4] for i in range(n)]


ADAMW_BLOCK_BYTES = 36 * 2 ** 20


def _adamw(name, w, m, v, parts):
    L, R, C = w.shape
    P = parts[0].shape[0]
    row_bytes = 2 * C * (7 * 4 + P * parts[0].dtype.itemsize)
    tr = R
    if R * row_bytes > ADAMW_BLOCK_BYTES:
        tr = next(t for t in (1024, 512, 256, 128, 64, 32, 16) if R % t == 0 and t * row_bytes <= ADAMW_BLOCK_BYTES)
    nr = R // tr
    c1 = 1.0 / (1.0 - ADAM_B1 ** ADAM_STEP)
    c2 = 1.0 / (1.0 - ADAM_B2 ** ADAM_STEP)

    def body(w_ref, m_ref, v_ref, *rest):
        p_refs, (g_ref, d_ref, mo_ref, vo_ref) = rest[:L], rest[L:]
        for ll in range(L):
            @pl.when(pl.program_id(0) == ll)
            def _(p_ref=p_refs[ll]):
                g = p_ref[0].astype(F32)
                for k in range(1, P):
                    g = g + p_ref[k].astype(F32)
                m2 = ADAM_B1 * m_ref[...] + (1.0 - ADAM_B1) * g
                v2 = ADAM_B2 * v_ref[...] + (1.0 - ADAM_B2) * (g * g)
                g_ref[...] = g
                mo_ref[...] = m2
                vo_ref[...] = v2
                d_ref[...] = -ADAM_LR * ((m2 * c1) / (jnp.sqrt(v2 * c2) + ADAM_EPS) + ADAM_WD * w_ref[...])

    def part_spec(ll):
        return pl.BlockSpec((P, tr, C), lambda l, i: (0, jnp.where(l == ll, i, jnp.where(l < ll, 0, nr - 1)), 0))

    full = pl.BlockSpec((None, tr, C), lambda l, i: (l, i, 0))
    sds = jax.ShapeDtypeStruct((L, R, C), F32)
    return pl.pallas_call(
        body, name=name, grid=(L, nr), out_shape=(sds, sds, sds, sds),
        in_specs=[full] * 3 + [part_spec(ll) for ll in range(L)],
        out_specs=(full,) * 4, compiler_params=_cp("arbitrary", "arbitrary"))(w, m, v, *parts)


def _rope_tables(positions):
    half = 16
    inv_freq = 10000.0 ** (-jnp.arange(half, dtype=F32) / half)
    ang = positions.astype(F32)[:, None] * inv_freq
    cos, sin = jnp.cos(ang), jnp.sin(ang)
    S = positions.shape[0]
    z16, z32, z64 = jnp.zeros((S, 16), F32), jnp.zeros((S, 32), F32), jnp.zeros((S, 64), F32)
    cosk = jnp.concatenate([z64, cos, cos, z32], axis=1)
    cosq = jnp.concatenate([jnp.ones((S, 64), F32), cos, cos, z32], axis=1)
    sa = jnp.concatenate([z64, -sin, z16, z32], axis=1)
    sb = jnp.concatenate([z64, z16, sin, z32], axis=1)
    return cosq, cosk, sa, sb


def _ffn_fwd(l, x, mod, n2g, get_w_up8, cw24, get_w_down4):
    sh, sc, gate = mod
    h = _rmsmod_fwd(f"ffn{l}_norm", x, n2g, sc, sh, n2g)
    w_up8 = get_w_up8(h)
    u8 = _mm_cols(f"ffn{l}_up", h, w_up8, out_dtype=ACT_DTYPE, tm=2048)
    S, n = u8.shape[1], u8.shape[2]
    u24 = u8.reshape(2, 4, S, n)
    a4, z24 = _ffn_gate_fwd(f"ffn{l}_gate", u24, cw24)
    w_down4 = get_w_down4(a4)
    f, x_new = _mm_rows_resid(f"ffn{l}_down", a4, w_down4, x, gate)
    return x_new, (x, h, u24, a4, f, z24), w_up8, w_down4


def _ffn_bwd(l, dx, df, dgate, saved, mod, n2g, w_up8, cw24, w_down4, me, y_prev, gate_prev):
    sh, sc, gate = mod
    x, h, u24, a4, f, z24 = saved
    da4 = _mm_rows_dx(f"ffn{l}_down_dx", df, w_down4, out_dtype=ACT_DTYPE, tm=2048)
    dw_down4 = _mm_rows_dw(f"ffn{l}_down_dw", a4, df, out_dtype=WIRE_DTYPE, tn=1024)
    sent_down, token = _exchange_start(f"scatter_ffn{l}_down", [dw_down4.reshape(8, 352, dw_down4.shape[2])], True, dgate, me)
    du24, dcw24, dh = _ffn_gate_bwd(f"ffn{l}_act_bwd", u24, z24, cw24, da4, w_up8.reshape((2, 4) + w_up8.shape[1:]), token)
    du8 = du24.reshape((8,) + du24.shape[2:])
    dw_up8t = _mm_cols_dwt(f"ffn{l}_up_dw", h, du8, out_dtype=WIRE_DTYPE, tk=1024)
    sent_up, token = _exchange_start(f"scatter_ffn{l}_up", [dw_up8t], True, dcw24, me)
    dx_new, dn2g, dsc, dsh, dy_prev, dgate_prev = _rmsmod_bwd(f"ffn{l}_norm_bwd", x, n2g, sc, dh, dx, token, y_prev, gate_prev)
    return dx_new, dict(sent_up=sent_up, sent_down=sent_down, cw24=dcw24, n2g=dn2g, mod=(dsh, dsc, dgate)), dy_prev, dgate_prev


def kernel(x, c, positions, ada_w, ada_b, norm1_g, norm2_g, ab_w_in, a_conv_w, b_mix_w, b_scale, ab_w_out, cd_w_in, c_q_norm_g, c_w_uq, c_kv_norm_g, c_w_ukv, d_ln_g, d_ln_b, d_w_s, d_b_s, cd_w_out, ffn_w_up, ffn_conv_w, ffn_w_down, final_norm_g, loss_target, m_ada_w, m_ada_b, m_norm1_g, m_norm2_g, m_ab_w_in, m_a_conv_w, m_b_mix_w, m_b_scale, m_ab_w_out, m_cd_w_in, m_c_q_norm_g, m_c_w_uq, m_c_kv_norm_g, m_c_w_ukv, m_d_ln_g, m_d_ln_b, m_d_w_s, m_d_b_s, m_cd_w_out, m_ffn_w_up, m_ffn_conv_w, m_ffn_w_down, m_final_norm_g, v_ada_w, v_ada_b, v_norm1_g, v_norm2_g, v_ab_w_in, v_a_conv_w, v_b_mix_w, v_b_scale, v_ab_w_out, v_cd_w_in, v_c_q_norm_g, v_c_w_uq, v_c_kv_norm_g, v_c_w_ukv, v_d_ln_g, v_d_ln_b, v_d_w_s, v_d_b_s, v_cd_w_out, v_ffn_w_up, v_ffn_conv_w, v_ffn_w_down, v_final_norm_g):
    S, D = x.shape[1], x.shape[2]
    me = 4 * lax.axis_index("x") + 2 * lax.axis_index("y") + lax.axis_index("c")
    x0, target = x[0], loss_target[0]
    W = _MXU_DTYPE

    small_shapes = [(1024,), (3, 64), (32,), (64,), (64,), (2, 3, 704)]
    (g0,) = _exchange("gather_small", [[_pack([c, a_conv_w, c_q_norm_g, d_ln_g, d_ln_b, ffn_conv_w])]], scatter=False)
    c_all, aconv_s, qg_s, lng_s, lnb_s, fcw_s = _unpack(g0[:, 0], small_shapes, lead=(N_DEV,))
    conv_w = aconv_s.transpose(1, 0, 2).reshape(3, 512)
    qg, ln_g, ln_b = qg_s.reshape(1, 256), lng_s.reshape(1, 512), lnb_s.reshape(1, 512)
    cw24 = [fcw_s[:, l].reshape(2, 4, 3, 704) for l in range(2)]
    c16 = jnp.pad(c_all, ((0, 16 - N_DEV), (0, 0)))

    mod_cols = _ada_fwd(c16, ada_w)
    (g1,) = _exchange("scatter_mod", [[mod_cols[:, :N_DEV].transpose(1, 0, 2)]], scatter=True)
    mod_mine = g1[:, 0]
    mod = mod_mine.transpose(1, 0, 2).reshape(2, 6 * D) + ada_b
    mods = [[mod[l, k * D:(k + 1) * D].reshape(1, D) for k in range(6)] for l in range(2)]

    gw_ab, token = _hier_gather_start("gather_w_ab", [ab_w_in[0].astype(W), ab_w_out[0].astype(W)], mod, me)
    gw_up0, token = _hier_gather_start("gather_w_ffn0_up", [ffn_w_up[0].astype(W)], token, me)
    gw_rest, started = _exchange_start("gather_w_rest", [
        ffn_w_down[0].astype(W), cd_w_in[0].T.astype(W), c_w_uq[0].T.astype(W), c_w_ukv[0].astype(W), cd_w_out[0].astype(W),
        ffn_w_up[1].astype(W), ffn_w_down[1].astype(W)], False, token, me)

    cosq, cosk, sa, sb = _rope_tables(positions[0])
    n1g = [norm1_g[l].reshape(1, D) for l in range(2)]
    n2g = [norm2_g[l].reshape(1, D) for l in range(2)]
    mix_w, scale = b_mix_w[0], b_scale
    kvg = c_kv_norm_g
    w_s, b_st = d_w_s[0], d_b_s[0].T

    sh1, sc1, g1m = mods[0][:3]
    h_ab = _rmsmod_fwd("ab_norm", x0, n1g[0], sc1, sh1, started)
    w_abin8, w_about = _hier_gather_wait("wait_w_ab", _hier_gather_forward("forward_w_ab", gw_ab, h_ab), h_ab)
    w_about2 = w_about.reshape(2, 512, D)
    z8 = _mm_cols("ab_in", h_ab, w_abin8, out_dtype=ACT_DTYPE, tm=2048)
    ycat_ab = _ab_mix_fwd(z8, conv_w, mix_w, scale)
    y_ab, x1 = _mm_rows_resid("ab_out", ycat_ab, w_about2, x0, g1m)
    w_up8, w_down4 = [None, None], [None, None]
    gw_up0 = _hier_gather_forward("forward_w_ffn0_up", gw_up0, x1)
    x2, ffn0_saved, w_up8[0], w_down4[0] = _ffn_fwd(
        0, x1, mods[0][3:], n2g[0], lambda after: _hier_gather_wait("wait_w_ffn0_up", gw_up0, after)[0], cw24[0],
        lambda after: _exchange_wait("wait_w_ffn0_down", gw_rest, after, [0])[0].reshape(4, 704, D))

    w_cdin, w_uq, w_ukv, w_cdout = _exchange_wait("wait_w_cd", gw_rest, x2, [1, 2, 3, 4])
    w_cdout2 = w_cdout.reshape(2, 512, D)
    w_cd_t = w_cdin.reshape(1440, D)
    zr = lambda n: jnp.zeros((n, D), W)
    w_cd_pad = jnp.concatenate([w_cd_t[:384], zr(64), w_cd_t[384:416], zr(32), w_cd_t[416:]], axis=0)
    w_uq_pad = jnp.pad(w_uq, ((0, 0), (0, 32), (0, 0))).reshape(1024, 256)
    w_ukv_h = w_ukv.transpose(1, 0, 2)
    w_k_pad = jnp.pad(w_ukv_h[:, :, :64], ((0, 0), (0, 0), (0, 64))).reshape(128, 1024)
    w_kv_pad = jnp.concatenate([w_k_pad, w_ukv_h[:, :, 64:].reshape(128, 512)], axis=1)

    sh1, sc1, g1c = mods[1][:3]
    h_cd = _rmsmod_fwd("cd_norm", x2, n1g[1], sc1, sh1, n1g[1])
    z_cd = _mm_nt("cd_in", h_cd, w_cd_pad, tm=1024, tn=1536)
    qn, kvn, q_r, k_r, v_r = _qkv_rope_fwd(z_cd, qg, kvg, w_uq_pad, w_kv_pad, cosq, cosk, sa, sb)
    o, lse = _attn_fwd(q_r, k_r, v_r)
    ycat_cd = _sgu_fwd(z_cd, o, ln_g, ln_b, w_s, b_st)
    y_cd, x3 = _mm_rows_resid("cd_out", ycat_cd, w_cdout2, x2, g1c)
    x4, ffn1_saved, w_up8[1], w_down4[1] = _ffn_fwd(
        1, x3, mods[1][3:], n2g[1], lambda after: _exchange_wait("wait_w_ffn1_up", gw_rest, after, [5])[0], cw24[1],
        lambda after: _exchange_wait("wait_w_ffn1_down", gw_rest, after, [6])[0].reshape(4, 704, D))

    loss_local, dx4, dfg, df1, dgate1 = _loss_head(x4, final_norm_g.reshape(1, D), target, ffn1_saved[4], mods[1][5])

    dx3, gf1, dy, dg1c = _ffn_bwd(1, dx4, df1, dgate1, ffn1_saved, mods[1][3:], n2g[1], w_up8[1], cw24[1], w_down4[1], me, y_cd, g1c)

    dycat = _mm_rows_dx("cd_out_dx", dy, w_cdout2, tm=2048)
    dw_cdout = _mm_rows_dw("cd_out_dw", ycat_cd, dy, out_dtype=WIRE_DTYPE, tn=1024)
    duv, dln_g, dln_b, dws, dbs = _sgu_bwd(z_cd, dycat, ln_g, ln_b, w_s, b_st)
    dq_r, dk_r, dv_r = _attn_bwd(q_r, k_r, v_r, lse, *_attn_bwd_prep(o, dycat))
    dqraw, dkvall, dz_cd, dqg, dkvg = _qkv_rope_bwd(z_cd, qg, kvg, dq_r, dk_r, dv_r, duv, w_uq_pad, w_kv_pad, cosq, cosk, sa, sb)
    dw_uq_pad = _mm_tn("cd_uq_dw", dqraw, qn, tn=256)
    dw_kv_pad = _mm_tn("cd_ukv_dw", kvn, dkvall, tm=128)
    dh_cd = _mm_nn("cd_in_dx", dz_cd, w_cd_pad, out_dtype=ACT_DTYPE, tm=1024, tn=1024)
    dw_cd_pad = _mm_tn("cd_in_dw", dz_cd, h_cd, tm=768, tn=1024)
    dw_cd8 = jnp.concatenate([dw_cd_pad[:384], dw_cd_pad[448:480], dw_cd_pad[512:]], axis=0).astype(WIRE_DTYPE).reshape(8, 180, D)
    dw_uq8 = dw_uq_pad.reshape(8, 128, 256)[:, :96].astype(WIRE_DTYPE)
    dw_ukv8 = jnp.concatenate([dw_kv_pad[:, :1024].reshape(128, 8, 128)[:, :, :64], dw_kv_pad[:, 1024:].reshape(128, 8, 64)],
                              axis=2).transpose(1, 0, 2).astype(WIRE_DTYPE)
    early_names = ["c_kv_norm_g", "d_w_s", "d_b_s", "final_norm_g", "c_q_norm_g", "d_ln_g", "d_ln_b"]
    early_grads = [dkvg, dws.reshape(512, 128).astype(WIRE_DTYPE), dbs, dfg, dqg.reshape(8, 1, 32), dln_g.reshape(8, 1, 64),
                   dln_b.reshape(8, 1, 64)]
    sent_cd, token = _exchange_start("scatter_cd", [dw_cd8, dw_uq8, dw_ukv8, dw_cdout.reshape(8, 128, D)] + early_grads,
                                     [True] * 4 + [False] * 4 + [True] * 3, dqg, me)
    dx2, dn1g_cd, dsc1_cd, dsh1_cd, df0, dgate0 = _rmsmod_bwd("cd_norm_bwd", x2, n1g[1], sc1, dh_cd, dx3, token,
                                                              ffn0_saved[4], mods[0][5])

    dx1, gf0, dy, dg1m = _ffn_bwd(0, dx2, df0, dgate0, ffn0_saved, mods[0][3:], n2g[0], w_up8[0], cw24[0], w_down4[0], me, y_ab, g1m)

    dw_about = _mm_rows_dw("ab_out_dw", ycat_ab, dy, out_dtype=WIRE_DTYPE, tn=1024)
    sent_about, token = _exchange_start("scatter_ab_out", [dw_about.reshape(8, 128, D)], True, dg1m, me)
    dycat = _mm_rows_dx("ab_out_dx", dy, w_about2, tm=2048)
    dz8, dconv_w, dmix_w, dscale = _ab_mix_bwd(z8, dycat, conv_w, mix_w, scale, token)
    dz8 = dz8.reshape(8, S, 256)
    dw_abin8 = _mm_cols_dw("ab_in_dw", h_ab, dz8, out_dtype=WIRE_DTYPE, tk=1024)
    sent_abin, token = _exchange_start("scatter_ab_in", [dw_abin8], True, dscale, me)
    dh_ab = _mm_cols_dx("ab_in_dx", dz8, w_abin8, out_dtype=ACT_DTYPE)
    dx0, dn1g_ab, dsc1_ab, dsh1_ab = _rmsmod_bwd("ab_norm_bwd", x0, n1g[0], mods[0][1], dh_ab, dx1, token)

    dmod = jnp.stack([jnp.concatenate([dsh1_ab, dsc1_ab, dg1m, *gf0["mod"]], axis=1)[0],
                      jnp.concatenate([dsh1_cd, dsc1_cd, dg1c, *gf1["mod"]], axis=1)[0]])
    late_names = ["ada_b", "norm1_g", "norm2_g", "b_mix_w", "b_scale", "a_conv_w", "ffn_conv_w"]
    late_grads = [dmod, jnp.concatenate([dn1g_ab, dn1g_cd]), jnp.concatenate([gf0["n2g"], gf1["n2g"]]),
                  dmix_w.reshape(512, 128).astype(WIRE_DTYPE), dscale, dconv_w.reshape(3, 8, 64).transpose(1, 0, 2),
                  jnp.stack([gf0["cw24"].reshape(8, 3, 704), gf1["cw24"].reshape(8, 3, 704)], axis=1),
                  jnp.pad(loss_local, ((0, 0), (0, 127)))]
    small_view = dict(ada_b=(2, 6 * D), norm1_g=(2, D), norm2_g=(2, D), b_mix_w=(512, 128), b_scale=(1, 512), c_kv_norm_g=(1, 128),
                      d_w_s=(512, 128), d_b_s=(4, 128), final_norm_g=(1, D),
                      a_conv_w=(3, 64), c_q_norm_g=(1, 32), d_ln_g=(1, 64), d_ln_b=(1, 64), ffn_conv_w=(2, 3, 704))
    late_sent, token = _exchange_start("gather_small_grads_late", late_grads, [False] * 5 + [True] * 2 + [False], dx0, me)

    res = {}

    def update(name, w, m, v, parts, shape3d):
        outs = _adamw("adamw_" + name, w.reshape(shape3d), m.reshape(shape3d), v.reshape(shape3d),
                      [p.reshape((p.shape[0],) + shape3d[1:]) for p in parts])
        res[name] = [o_.reshape(w.shape) for o_ in outs]

    p_cdin, p_uq, p_ukv, p_cdout = _exchange_wait("wait_scatter_cd", sent_cd, token, [0, 1, 2, 3])
    swap = lambda a: jnp.swapaxes(a, 1, 2)
    update("cd_w_in", swap(cd_w_in), swap(m_cd_w_in), swap(v_cd_w_in), [p_cdin], (1, 180, D))
    update("c_w_uq", swap(c_w_uq), swap(m_c_w_uq), swap(v_c_w_uq), [p_uq], (1, 96, 256))
    for name in ("cd_w_in", "c_w_uq"):
        res[name] = [swap(o_) for o_ in res[name]]
    update("c_w_ukv", c_w_ukv, m_c_w_ukv, v_c_w_ukv, [p_ukv], (1, 128, 128))
    update("cd_w_out", cd_w_out, m_cd_w_out, v_cd_w_out, [p_cdout], (1, 128, D))
    (p_dn1,) = _exchange_wait("wait_scatter_ffn1_down", gf1["sent_down"], token)
    (p_dn0,) = _exchange_wait("wait_scatter_ffn0_down", gf0["sent_down"], res["cd_w_out"][0])
    update("ffn_w_down", ffn_w_down, m_ffn_w_down, v_ffn_w_down, [p_dn0, p_dn1], (2, 352, D))
    (p_up1,) = _exchange_wait("wait_scatter_ffn1_up", gf1["sent_up"], token)
    (p_up0,) = _exchange_wait("wait_scatter_ffn0_up", gf0["sent_up"], res["ffn_w_down"][0])
    swap = lambda a: jnp.swapaxes(a, 1, 2)
    update("ffn_w_up", swap(ffn_w_up), swap(m_ffn_w_up), swap(v_ffn_w_up), [p_up0, p_up1], (2, 704, D))
    up_done = res["ffn_w_up"][0]
    res["ffn_w_up"] = [swap(o_) for o_ in res["ffn_w_up"]]
    (p_about,) = _exchange_wait("wait_scatter_ab_out", sent_about, up_done)
    update("ab_w_out", ab_w_out, m_ab_w_out, v_ab_w_out, [p_about], (1, 128, D))
    (p_abin,) = _exchange_wait("wait_scatter_ab_in", sent_abin, res["ab_w_out"][0])
    update("ab_w_in", ab_w_in, m_ab_w_in, v_ab_w_in, [p_abin], (1, D, 256))

    early_parts = _exchange_wait("wait_small_grads_early", sent_cd, res["ab_w_in"][0], list(range(4, 11)))
    late_parts = _exchange_wait("wait_small_grads_late", late_sent, res["ab_w_in"][0])
    small_names = early_names + late_names
    small_parts = list(early_parts) + list(late_parts[:7])
    loss = jnp.sum(late_parts[7][:, 0, 0])
    dmod_all = late_parts[0]
    dmod_cols = lax.dynamic_slice_in_dim(dmod_all, me * 768, 768, axis=2).transpose(1, 0, 2)
    g_ada_w = _ada_bwd(c16, jnp.pad(dmod_cols, ((0, 0), (0, 16 - N_DEV), (0, 0))))
    update("ada_w", ada_w, m_ada_w, v_ada_w, [g_ada_w[None]], (1, 2 * D, 768))

    small_w = dict(ada_b=(ada_b, m_ada_b, v_ada_b), norm1_g=(norm1_g, m_norm1_g, v_norm1_g), norm2_g=(norm2_g, m_norm2_g, v_norm2_g),
                   b_mix_w=(b_mix_w, m_b_mix_w, v_b_mix_w), b_scale=(b_scale, m_b_scale, v_b_scale),
                   c_kv_norm_g=(c_kv_norm_g, m_c_kv_norm_g, v_c_kv_norm_g), d_w_s=(d_w_s, m_d_w_s, v_d_w_s),
                   d_b_s=(d_b_s, m_d_b_s, v_d_b_s), final_norm_g=(final_norm_g, m_final_norm_g, v_final_norm_g),
                   a_conv_w=(a_conv_w, m_a_conv_w, v_a_conv_w), c_q_norm_g=(c_q_norm_g, m_c_q_norm_g, v_c_q_norm_g),
                   d_ln_g=(d_ln_g, m_d_ln_g, v_d_ln_g), d_ln_b=(d_ln_b, m_d_ln_b, v_d_ln_b),
                   ffn_conv_w=(ffn_conv_w, m_ffn_conv_w, v_ffn_conv_w))
    small_out = _adamw_small("adamw_small", [tuple(a.reshape(small_view[n]) for a in small_w[n]) for n in small_names],
                             list(small_parts))
    for n, outs in zip(small_names, small_out):
        res[n] = [o_.reshape(small_w[n][0].shape) for o_ in outs]

    order = ["ada_w", "ada_b", "norm1_g", "norm2_g", "ab_w_in", "a_conv_w", "b_mix_w", "b_scale", "ab_w_out", "cd_w_in", "c_q_norm_g",
             "c_w_uq", "c_kv_norm_g", "c_w_ukv", "d_ln_g", "d_ln_b", "d_w_s", "d_b_s", "cd_w_out", "ffn_w_up", "ffn_conv_w",
             "ffn_w_down", "final_norm_g"]
    return (loss, dx0[None], *[res[n][0] for n in order], *[res[n][1] for n in order], *[res[n][2] for n in order],
            *[res[n][3] for n in order])
```

```python
import functools
import math

import jax
import jax.numpy as jnp
from jax import lax
from jax.experimental import pallas as pl
from jax.experimental.pallas import tpu as pltpu

F32 = jnp.float32
BF16 = jnp.bfloat16
_MXU_DTYPE = BF16
WIRE_DTYPE = BF16
ACT_DTYPE = BF16
_VMEM_LIMIT = 56 * 2 ** 20
N_DEV = 8
EPS = 1e-6
POOL_WINDOWS = (2, 4, 8, 16)
ATTN_SCALE = (64 + 32) ** -0.5
ADAM_LR, ADAM_B1, ADAM_B2, ADAM_EPS, ADAM_WD, ADAM_STEP = 0.001, 0.9, 0.999, 1e-08, 0.01, 10
MESH = pl.DeviceIdType.MESH
ANY = pl.BlockSpec(memory_space=pl.ANY)


def _cp(*sem):
    return pltpu.CompilerParams(dimension_semantics=sem, vmem_limit_bytes=_VMEM_LIMIT)


def _dot(a, b, contract):
    dn = {"nn": (((1,), (0,)), ((), ())), "nt": (((1,), (1,)), ((), ())), "tn": (((0,), (0,)), ((), ()))}[contract]
    return lax.dot_general(a.astype(_MXU_DTYPE), b.astype(_MXU_DTYPE), dn, preferred_element_type=F32)


def _my_position():
    x, y, c = lax.axis_index("x"), lax.axis_index("y"), lax.axis_index("c")
    return x, y, c, 4 * x + 2 * y + c


def _exchange(name, groups, scatter):
    flat = [a for g in groups for a in g]
    n_in, n_grp = len(flat), len(groups)
    out_shapes = []
    for g in groups:
        slab = g[0].shape[1:] if scatter else g[0].shape
        out_shapes.append(jax.ShapeDtypeStruct((N_DEV, len(g)) + tuple(slab), g[0].dtype))

    def body(*refs):
        ins, outs = refs[:n_in], refs[n_in:n_in + n_grp]
        send_sems, recv_sems, local_sems = refs[n_in + n_grp:]
        x, y, c, me = _my_position()
        i = 0
        for gi, g in enumerate(groups):
            for l in range(len(g)):
                src = ins[i]
                i += 1
                pltpu.make_async_copy(src.at[me] if scatter else src, outs[gi].at[me, l], local_sems.at[gi]).start()
                for k in range(1, N_DEV):
                    px = 1 - x if k & 4 else x
                    py = 1 - y if k & 2 else y
                    pc = 1 - c if k & 1 else c
                    peer = 4 * px + 2 * py + pc
                    pltpu.make_async_remote_copy(
                        src_ref=src.at[peer] if scatter else src, dst_ref=outs[gi].at[me, l],
                        send_sem=send_sems.at[gi], recv_sem=recv_sems.at[gi],
                        device_id=(px, py, pc), device_id_type=MESH).start()
        for gi in range(n_grp):
            mine = outs[gi].at[me]
            pltpu.make_async_copy(mine, mine, local_sems.at[gi]).wait()
            seven = outs[gi].at[pl.ds(0, N_DEV - 1)]
            w = pltpu.make_async_remote_copy(src_ref=seven, dst_ref=seven, send_sem=send_sems.at[gi],
                                             recv_sem=recv_sems.at[gi], device_id=(x, y, c), device_id_type=MESH)
            w.wait_send()
            w.wait_recv()

    return pl.pallas_call(
        body, name=name, out_shape=tuple(out_shapes),
        in_specs=[ANY] * n_in, out_specs=tuple([ANY] * n_grp),
        scratch_shapes=[pltpu.SemaphoreType.DMA((n_grp,)), pltpu.SemaphoreType.DMA((n_grp,)),
                        pltpu.SemaphoreType.DMA((n_grp,))],
        compiler_params=pltpu.CompilerParams(has_side_effects=True),
    )(*flat)


HBM_SPEC = pl.BlockSpec(memory_space=pltpu.HBM)
SEM_SPEC = pl.BlockSpec(memory_space=pltpu.SEMAPHORE)
EFFECT = pltpu.SideEffectType.DATAFLOW_SIDE_EFFECTING


def _put_mine(name, srcs, scatter, me):
    n = len(srcs)
    slabs = [tuple(s.shape[1:] if sc else s.shape) for s, sc in zip(srcs, scatter)]

    def body(me_ref, *refs):
        for i in range(n):
            refs[n + i][...] = refs[i][...]

    def at_me(slab):
        return pl.BlockSpec((None,) + slab, lambda g, me_ref, nd=len(slab): (me_ref[0],) + (0,) * nd)

    def whole(slab):
        return pl.BlockSpec(slab, lambda g, me_ref, nd=len(slab): (0,) * nd)

    return pl.pallas_call(
        body, name=name,
        grid_spec=pltpu.PrefetchScalarGridSpec(
            num_scalar_prefetch=1, grid=(1,),
            in_specs=[at_me(slab) if sc else whole(slab) for slab, sc in zip(slabs, scatter)],
            out_specs=[at_me(slab) for slab in slabs]),
        out_shape=[jax.ShapeDtypeStruct((N_DEV,) + slab, s.dtype) for slab, s in zip(slabs, srcs)],
        compiler_params=_cp("arbitrary"))(me.reshape(1), *srcs)


def _exchange_start(name, srcs, scatter, after, me):
    n = len(srcs)
    scatter = list(scatter) if isinstance(scatter, (list, tuple)) else [scatter] * n
    lands = _put_mine(name + "_mine", srcs, scatter, me)
    srcs = [pltpu.with_memory_space_constraint(a, pltpu.HBM) for a in srcs]
    lands = [pltpu.with_memory_space_constraint(a, pltpu.HBM) for a in lands]

    def body(*refs):
        ins, land = refs[:n], refs[n:2 * n]
        send_sems, recv_sems, token = refs[2 * n + 1], refs[2 * n + 2], refs[-1]
        x, y, c, me_in = _my_position()
        for i in range(n):
            for k in range(1, N_DEV):
                px = 1 - x if k & 4 else x
                py = 1 - y if k & 2 else y
                pc = 1 - c if k & 1 else c
                pltpu.make_async_remote_copy(
                    src_ref=ins[i].at[4 * px + 2 * py + pc] if scatter[i] else ins[i], dst_ref=land[i].at[me_in],
                    send_sem=send_sems.at[i], recv_sem=recv_sems.at[i],
                    device_id=(px, py, pc), device_id_type=MESH).start()
        token[...] = jnp.zeros_like(token)

    outs = pl.pallas_call(
        body, name=name,
        out_shape=(pltpu.SemaphoreType.DMA((n,)), pltpu.SemaphoreType.DMA((n,)),
                   *[pltpu.HBM(a.shape, a.dtype) for a in srcs], *[pltpu.HBM(a.shape, a.dtype) for a in lands],
                   jax.ShapeDtypeStruct((8, 128), F32)),
        in_specs=[HBM_SPEC] * (2 * n) + [ANY],
        out_specs=(SEM_SPEC, SEM_SPEC, *[HBM_SPEC] * (2 * n), pl.BlockSpec(memory_space=pltpu.VMEM)),
        input_output_aliases={i: 2 + i for i in range(2 * n)},
        compiler_params=pltpu.CompilerParams(has_side_effects=EFFECT),
    )(*srcs, *lands, after)
    return (outs[0], outs[1], outs[2:2 + n], outs[2 + n:2 + 2 * n]), outs[-1]


def _exchange_wait(name, handle, after, which=None):
    send_sems, recv_sems, srcs, lands = handle
    which = list(range(len(srcs))) if which is None else list(which)
    srcs, lands = [srcs[i] for i in which], [lands[i] for i in which]
    n = len(srcs)

    def body(*refs):
        land, send_ref, recv_ref = refs[n:2 * n], refs[2 * n], refs[2 * n + 1]
        x, y, c, _ = _my_position()
        for k, i in enumerate(which):
            seven = land[k].at[pl.ds(0, N_DEV - 1)]
            w = pltpu.make_async_remote_copy(src_ref=seven, dst_ref=seven, send_sem=send_ref.at[i], recv_sem=recv_ref.at[i],
                                             device_id=(x, y, c), device_id_type=MESH)
            w.wait_send()
            w.wait_recv()

    outs = pl.pallas_call(
        body, name=name,
        out_shape=(*[pltpu.HBM(a.shape, a.dtype) for a in srcs], *[pltpu.HBM(a.shape, a.dtype) for a in lands]),
        in_specs=[HBM_SPEC] * (2 * n) + [SEM_SPEC, SEM_SPEC, ANY],
        out_specs=tuple([HBM_SPEC] * (2 * n)),
        input_output_aliases={i: i for i in range(2 * n)},
        compiler_params=pltpu.CompilerParams(has_side_effects=EFFECT),
    )(*srcs, *lands, send_sems, recv_sems, after)
    return outs[n:]


def _other_chips(x, y):
    return [(1 - x, y), (x, 1 - y), (1 - x, 1 - y)]


def _hier_gather_start(name, srcs, after, me):
    n = len(srcs)
    lands = _put_mine(name + "_mine", srcs, [False] * n, me)
    srcs = [pltpu.with_memory_space_constraint(a, pltpu.HBM) for a in srcs]
    lands = [pltpu.with_memory_space_constraint(a, pltpu.HBM) for a in lands]

    def body(*refs):
        ins, land = refs[:n], refs[n:2 * n]
        ici_send, ici_recv, d2d_send, d2d_recv = refs[2 * n + 1:2 * n + 5]
        token = refs[-1]
        x, y, c, me_in = _my_position()
        for i in range(n):
            pltpu.make_async_remote_copy(src_ref=ins[i], dst_ref=land[i].at[me_in], send_sem=d2d_send.at[i], recv_sem=d2d_recv.at[i],
                                         device_id=(x, y, 1 - c), device_id_type=MESH).start()
            for px, py in _other_chips(x, y):
                pltpu.make_async_remote_copy(src_ref=ins[i], dst_ref=land[i].at[me_in], send_sem=ici_send.at[i],
                                             recv_sem=ici_recv.at[i], device_id=(px, py, c), device_id_type=MESH).start()
        token[...] = jnp.zeros_like(token)

    sem = pltpu.SemaphoreType.DMA((n,))
    outs = pl.pallas_call(
        body, name=name,
        out_shape=(sem, sem, sem, sem, *[pltpu.HBM(a.shape, a.dtype) for a in srcs], *[pltpu.HBM(a.shape, a.dtype) for a in lands],
                   jax.ShapeDtypeStruct((8, 128), F32)),
        in_specs=[HBM_SPEC] * (2 * n) + [ANY],
        out_specs=(SEM_SPEC,) * 4 + (HBM_SPEC,) * (2 * n) + (pl.BlockSpec(memory_space=pltpu.VMEM),),
        input_output_aliases={i: 4 + i for i in range(2 * n)},
        compiler_params=pltpu.CompilerParams(has_side_effects=EFFECT),
    )(*srcs, *lands, after)
    return (outs[:4], outs[4:4 + n], outs[4 + n:4 + 2 * n]), outs[-1]


def _hier_gather_forward(name, handle, after):
    sems, srcs, lands = handle
    n = len(srcs)

    def body(*refs):
        land = refs[n:2 * n]
        ici_send, ici_recv, d2d_send, d2d_recv = refs[2 * n:2 * n + 4]
        x, y, c, _ = _my_position()
        for i in range(n):
            three = land[i].at[pl.ds(0, 3)]
            pltpu.make_async_remote_copy(src_ref=three, dst_ref=three, send_sem=ici_send.at[i], recv_sem=ici_recv.at[i],
                                         device_id=(x, y, c), device_id_type=MESH).wait_recv()
            for px, py in _other_chips(x, y):
                slab = land[i].at[4 * px + 2 * py + c]
                pltpu.make_async_remote_copy(src_ref=slab, dst_ref=slab, send_sem=d2d_send.at[i], recv_sem=d2d_recv.at[i],
                                             device_id=(x, y, 1 - c), device_id_type=MESH).start()

    outs = pl.pallas_call(
        body, name=name,
        out_shape=(*[pltpu.HBM(a.shape, a.dtype) for a in srcs], *[pltpu.HBM(a.shape, a.dtype) for a in lands]),
        in_specs=[HBM_SPEC] * (2 * n) + [SEM_SPEC] * 4 + [ANY],
        out_specs=tuple([HBM_SPEC] * (2 * n)),
        input_output_aliases={i: i for i in range(2 * n)},
        compiler_params=pltpu.CompilerParams(has_side_effects=EFFECT),
    )(*srcs, *lands, *sems, after)
    return (sems, outs[:n], outs[n:])


def _hier_gather_wait(name, handle, after):
    sems, srcs, lands = handle
    n = len(srcs)

    def body(*refs):
        land = refs[n:2 * n]
        ici_send, ici_recv, d2d_send, d2d_recv = refs[2 * n:2 * n + 4]
        x, y, c, _ = _my_position()
        for i in range(n):
            three, four = land[i].at[pl.ds(0, 3)], land[i].at[pl.ds(0, 4)]
            pltpu.make_async_remote_copy(src_ref=three, dst_ref=three, send_sem=ici_send.at[i], recv_sem=ici_recv.at[i],
                                         device_id=(x, y, c), device_id_type=MESH).wait_send()
            w = pltpu.make_async_remote_copy(src_ref=four, dst_ref=four, send_sem=d2d_send.at[i], recv_sem=d2d_recv.at[i],
                                             device_id=(x, y, c), device_id_type=MESH)
            w.wait_send()
            w.wait_recv()

    outs = pl.pallas_call(
        body, name=name,
        out_shape=(*[pltpu.HBM(a.shape, a.dtype) for a in srcs], *[pltpu.HBM(a.shape, a.dtype) for a in lands]),
        in_specs=[HBM_SPEC] * (2 * n) + [SEM_SPEC] * 4 + [ANY],
        out_specs=tuple([HBM_SPEC] * (2 * n)),
        input_output_aliases={i: i for i in range(2 * n)},
        compiler_params=pltpu.CompilerParams(has_side_effects=EFFECT),
    )(*srcs, *lands, *sems, after)
    return outs[n:]


def _pack(arrs):
    flat = jnp.concatenate([a.reshape(-1).astype(F32) for a in arrs])
    n = flat.shape[0]
    rows = -(-n // 1024) * 8
    return jnp.pad(flat, (0, rows * 128 - n)).reshape(rows, 128)


def _unpack(buf, shapes, lead=()):
    flat = buf.reshape(lead + (-1,))
    out, off = [], 0
    for s in shapes:
        n = math.prod(s)
        out.append(flat[..., off:off + n].reshape(lead + tuple(s)))
        off += n
    return out


def _mm(name, a, a_spec, b, b_spec, out_sds, o_spec, grid, contract, nk=1, stacked=0):
    o_blk = tuple(d for d in o_spec.block_shape if d is not None)

    def body(a_ref, b_ref, o_ref, *acc):
        if stacked:
            r = _dot(a_ref[0], b_ref[0], contract)
            for q in range(1, stacked):
                r = r + _dot(a_ref[q], b_ref[q], contract)
        else:
            r = _dot(a_ref[...], b_ref[...], contract)
        if nk == 1:
            o_ref[...] = r.astype(o_ref.dtype)
        else:
            k = pl.program_id(len(grid) - 1)

            @pl.when(k == 0)
            def _():
                acc[0][...] = r

            @pl.when(k > 0)
            def _():
                acc[0][...] += r

            @pl.when(k == nk - 1)
            def _():
                o_ref[...] = acc[0][...].astype(o_ref.dtype)

    sem = ("parallel",) * (len(grid) - 1) + (("arbitrary",) if nk > 1 else ("parallel",))
    return pl.pallas_call(
        body, name=name, out_shape=out_sds, grid=grid, in_specs=[a_spec, b_spec], out_specs=o_spec,
        scratch_shapes=[pltpu.VMEM(o_blk, F32)] if nk > 1 else [], compiler_params=_cp(*sem))(a, b)


def _tile(n, want):
    t = min(n, want)
    assert n % t == 0, (n, t)
    return t


def _mm_nn(name, a, b, out_dtype=F32, tm=512, tn=512):
    (M, K), N = a.shape, b.shape[1]
    tm, tn = _tile(M, tm), _tile(N, tn)
    return _mm(name, a, pl.BlockSpec((tm, K), lambda i, j: (i, 0)), b, pl.BlockSpec((K, tn), lambda i, j: (0, j)),
               jax.ShapeDtypeStruct((M, N), out_dtype), pl.BlockSpec((tm, tn), lambda i, j: (i, j)),
               (M // tm, N // tn), "nn")


def _mm_nt(name, a, b, out_dtype=F32, tm=512, tn=512):
    (M, K), N = a.shape, b.shape[0]
    tm, tn = _tile(M, tm), _tile(N, tn)
    return _mm(name, a, pl.BlockSpec((tm, K), lambda i, j: (i, 0)), b, pl.BlockSpec((tn, K), lambda i, j: (j, 0)),
               jax.ShapeDtypeStruct((M, N), out_dtype), pl.BlockSpec((tm, tn), lambda i, j: (i, j)),
               (M // tm, N // tn), "nt")


def _mm_tn(name, a, b, out_dtype=F32, tm=512, tn=512):
    (K, M), N = a.shape, b.shape[1]
    tm, tn = _tile(M, tm), _tile(N, tn)
    return _mm(name, a, pl.BlockSpec((K, tm), lambda i, j: (0, i)), b, pl.BlockSpec((K, tn), lambda i, j: (0, j)),
               jax.ShapeDtypeStruct((M, N), out_dtype), pl.BlockSpec((tm, tn), lambda i, j: (i, j)),
               (M // tm, N // tn), "tn")


def _mm_cols(name, a, w, out_dtype=F32, tm=512):
    (M, K), (J, _, n) = a.shape, w.shape
    tm = _tile(M, tm)
    return _mm(name, a, pl.BlockSpec((tm, K), lambda j, i: (i, 0)), w, pl.BlockSpec((None, K, n), lambda j, i: (j, 0, 0)),
               jax.ShapeDtypeStruct((J, M, n), out_dtype), pl.BlockSpec((None, tm, n), lambda j, i: (j, i, 0)),
               (J, M // tm), "nn")


def _mm_cols_dx(name, d, w, out_dtype=F32, tm=512, jb=None):
    (J, M, n), K = d.shape, w.shape[1]
    tm, jb = _tile(M, tm), J if jb is None else jb
    return _mm(name, d, pl.BlockSpec((jb, tm, n), lambda i, j: (j, i, 0)), w, pl.BlockSpec((jb, K, n), lambda i, j: (j, 0, 0)),
               jax.ShapeDtypeStruct((M, K), out_dtype), pl.BlockSpec((tm, K), lambda i, j: (i, 0)),
               (M // tm, J // jb), "nt", nk=J // jb, stacked=jb)


def _mm_cols_dw(name, a, d, out_dtype=F32, tk=512):
    (M, K), (J, _, n) = a.shape, d.shape
    tk = _tile(K, tk)
    return _mm(name, a, pl.BlockSpec((M, tk), lambda j, i: (0, i)), d, pl.BlockSpec((None, M, n), lambda j, i: (j, 0, 0)),
               jax.ShapeDtypeStruct((J, K, n), out_dtype), pl.BlockSpec((None, tk, n), lambda j, i: (j, i, 0)),
               (J, K // tk), "tn")


def _mm_cols_dwt(name, a, d, out_dtype=F32, tk=512):
    (M, K), (J, _, n) = a.shape, d.shape
    tk = _tile(K, tk)
    return _mm(name, d, pl.BlockSpec((None, M, n), lambda j, i: (j, 0, 0)), a, pl.BlockSpec((M, tk), lambda j, i: (0, i)),
               jax.ShapeDtypeStruct((J, n, K), out_dtype), pl.BlockSpec((None, n, tk), lambda j, i: (j, 0, i)),
               (J, K // tk), "tn")


def _mm_rows_resid(name, a, w, resid, gate, tm=512):
    (Q, M, k), N = a.shape, w.shape[2]
    tm = _tile(M, tm)

    def body(a_ref, w_ref, r_ref, g_ref, y_ref, x_ref):
        y = _dot(a_ref[0], w_ref[0], "nn")
        for q in range(1, Q):
            y = y + _dot(a_ref[q], w_ref[q], "nn")
        y_ref[...] = y.astype(y_ref.dtype)
        x_ref[...] = r_ref[...] + g_ref[...] * y

    return pl.pallas_call(
        body, name=name, grid=(M // tm,),
        out_shape=(jax.ShapeDtypeStruct((M, N), ACT_DTYPE), jax.ShapeDtypeStruct((M, N), F32)),
        in_specs=[pl.BlockSpec((Q, tm, k), lambda i: (0, i, 0)), pl.BlockSpec((Q, k, N), lambda i: (0, 0, 0)),
                  pl.BlockSpec((tm, N), lambda i: (i, 0)), pl.BlockSpec((1, N), lambda i: (0, 0))],
        out_specs=(pl.BlockSpec((tm, N), lambda i: (i, 0)), pl.BlockSpec((tm, N), lambda i: (i, 0))),
        compiler_params=_cp("parallel"))(a, w, resid, gate)


def _mm_rows_dx(name, d, w, out_dtype=F32, tm=512):
    (M, N), (Q, k, _) = d.shape, w.shape
    tm = _tile(M, tm)
    return _mm(name, d, pl.BlockSpec((tm, N), lambda q, i: (i, 0)), w, pl.BlockSpec((None, k, N), lambda q, i: (q, 0, 0)),
               jax.ShapeDtypeStruct((Q, M, k), out_dtype), pl.BlockSpec((None, tm, k), lambda q, i: (q, i, 0)),
               (Q, M // tm), "nt")


def _mm_rows_dw(name, a, d, out_dtype=F32, tn=512):
    (Q, M, k), N = a.shape, d.shape[1]
    tn = _tile(N, tn)
    return _mm(name, a, pl.BlockSpec((None, M, k), lambda q, j: (q, 0, 0)), d, pl.BlockSpec((M, tn), lambda q, j: (0, j)),
               jax.ShapeDtypeStruct((Q, k, N), out_dtype), pl.BlockSpec((None, k, tn), lambda q, j: (q, 0, j)),
               (Q, N // tn), "tn")


def _silu(v):
    return v * jax.nn.sigmoid(v)


def _ada_fwd(c16, ada_w):
    L, D, n = ada_w.shape

    def body(c_ref, w_ref, o_ref):
        o_ref[...] = _dot(_silu(c_ref[...]), w_ref[...], "nn")

    return pl.pallas_call(
        body, name="ada_fwd", grid=(L,), out_shape=jax.ShapeDtypeStruct((L, 16, n), F32),
        in_specs=[pl.BlockSpec((16, D), lambda l: (0, 0)), pl.BlockSpec((None, D, n), lambda l: (l, 0, 0))],
        out_specs=pl.BlockSpec((None, 16, n), lambda l: (l, 0, 0)), compiler_params=_cp("parallel"))(c16, ada_w)


def _ada_bwd(c16, dmod16):
    L, _, n = dmod16.shape
    D = c16.shape[1]

    def body(c_ref, d_ref, o_ref):
        o_ref[...] = _dot(_silu(c_ref[...]), d_ref[...], "tn")

    return pl.pallas_call(
        body, name="ada_bwd", grid=(L,), out_shape=jax.ShapeDtypeStruct((L, D, n), F32),
        in_specs=[pl.BlockSpec((16, D), lambda l: (0, 0)), pl.BlockSpec((None, 16, n), lambda l: (l, 0, 0))],
        out_specs=pl.BlockSpec((None, D, n), lambda l: (l, 0, 0)), compiler_params=_cp("parallel"))(c16, dmod16)


def _row_spec(tr, n):
    return pl.BlockSpec((tr, n), lambda i: (i, 0))


def _vec_spec(n):
    return pl.BlockSpec((1, n), lambda i: (0, 0))


def _rmsmod_fwd(name, x, g, sc, sh, after, tr=512):
    S, D = x.shape

    def body(x_ref, g_ref, sc_ref, sh_ref, after_ref, h_ref):
        xv = x_ref[...]
        rstd = lax.rsqrt(jnp.mean(xv * xv, axis=-1, keepdims=True) + EPS)
        y = xv * rstd * g_ref[...]
        h_ref[...] = (y * (1.0 + sc_ref[...]) + sh_ref[...]).astype(h_ref.dtype)

    return pl.pallas_call(
        body, name=name, grid=(S // tr,), out_shape=jax.ShapeDtypeStruct((S, D), _MXU_DTYPE),
        in_specs=[_row_spec(tr, D), _vec_spec(D), _vec_spec(D), _vec_spec(D), ANY], out_specs=_row_spec(tr, D),
        compiler_params=_cp("parallel"))(x, g, sc, sh, after)


def _acc_rows(ref, val, first):
    s = jnp.sum(val, axis=0, keepdims=True)

    @pl.when(first)
    def _():
        ref[...] = s

    @pl.when(jnp.logical_not(first))
    def _():
        ref[...] += s


def _gate_bwd_tail(dx, y_ref, gate_ref, dy_ref, dgate_ref, first):
    dy_ref[...] = (gate_ref[...] * dx).astype(dy_ref.dtype)
    _acc_rows(dgate_ref, dx * y_ref[...].astype(F32), first)


def _rmsmod_bwd(name, x, g, sc, dh, dres, after, y=None, gate=None, tr=512):
    S, D = x.shape
    tail = y is not None

    def body(x_ref, g_ref, sc_ref, dh_ref, dres_ref, after_ref, *rest):
        (y_ref, gate_ref), rest = (rest[:2], rest[2:]) if tail else ((None, None), rest)
        dx_ref, dg_ref, dsc_ref, dsh_ref = rest[:4]
        first = pl.program_id(0) == 0
        xv, dh_v, gv = x_ref[...], dh_ref[...].astype(F32), g_ref[...]
        rstd = lax.rsqrt(jnp.mean(xv * xv, axis=-1, keepdims=True) + EPS)
        xhat = xv * rstd
        _acc_rows(dsh_ref, dh_v, first)
        _acc_rows(dsc_ref, dh_v * (xhat * gv), first)
        dyg = dh_v * (1.0 + sc_ref[...])
        _acc_rows(dg_ref, dyg * xhat, first)
        dxhat = dyg * gv
        dx = dres_ref[...] + rstd * (dxhat - xhat * jnp.mean(dxhat * xhat, axis=-1, keepdims=True))
        dx_ref[...] = dx
        if tail:
            _gate_bwd_tail(dx, y_ref, gate_ref, rest[4], rest[5], first)

    vec = jax.ShapeDtypeStruct((1, D), F32)
    return pl.pallas_call(
        body, name=name, grid=(S // tr,),
        out_shape=(jax.ShapeDtypeStruct((S, D), F32), vec, vec, vec) + ((jax.ShapeDtypeStruct((S, D), _MXU_DTYPE), vec) if tail else ()),
        in_specs=[_row_spec(tr, D), _vec_spec(D), _vec_spec(D), _row_spec(tr, D), _row_spec(tr, D), ANY]
        + ([_row_spec(tr, D), _vec_spec(D)] if tail else []),
        out_specs=(_row_spec(tr, D), _vec_spec(D), _vec_spec(D), _vec_spec(D)) + ((_row_spec(tr, D), _vec_spec(D)) if tail else ()),
        compiler_params=_cp("arbitrary"))(x, g, sc, dh, dres, after, *((y, gate) if tail else ()))


def _loss_head(x, g, target, y, gate, tr=512):
    S, D = x.shape

    def body(x_ref, g_ref, t_ref, y_ref, gate_ref, loss_ref, dx_ref, dg_ref, dy_ref, dgate_ref):
        first = pl.program_id(0) == 0
        xv, gv = x_ref[...], g_ref[...]
        rstd = lax.rsqrt(jnp.mean(xv * xv, axis=-1, keepdims=True) + EPS)
        xhat = xv * rstd
        err = xhat * gv - t_ref[...]
        part = 0.5 * jnp.sum(jnp.mean(err * err, axis=-1, keepdims=True), axis=0, keepdims=True)

        @pl.when(first)
        def _():
            loss_ref[...] = part

        @pl.when(jnp.logical_not(first))
        def _():
            loss_ref[...] += part

        dout = err * (1.0 / D)
        _acc_rows(dg_ref, dout * xhat, first)
        dxhat = dout * gv
        dx = rstd * (dxhat - xhat * jnp.mean(dxhat * xhat, axis=-1, keepdims=True))
        dx_ref[...] = dx
        _gate_bwd_tail(dx, y_ref, gate_ref, dy_ref, dgate_ref, first)

    vec = jax.ShapeDtypeStruct((1, D), F32)
    return pl.pallas_call(
        body, name="loss_head", grid=(S // tr,),
        out_shape=(jax.ShapeDtypeStruct((1, 1), F32), jax.ShapeDtypeStruct((S, D), F32), vec,
                   jax.ShapeDtypeStruct((S, D), _MXU_DTYPE), vec),
        in_specs=[_row_spec(tr, D), _vec_spec(D), _row_spec(tr, D), _row_spec(tr, D), _vec_spec(D)],
        out_specs=(pl.BlockSpec((1, 1), lambda i: (0, 0)), _row_spec(tr, D), _vec_spec(D), _row_spec(tr, D), _vec_spec(D)),
        compiler_params=_cp("arbitrary"))(x, g, target, y, gate)


def _shift_down(v, k):
    t = lax.broadcasted_iota(jnp.int32, v.shape, 0)
    return jnp.where(t >= k, pltpu.roll(v, k, axis=0), 0.0)


def _shift_up(v, k):
    n = v.shape[0]
    t = lax.broadcasted_iota(jnp.int32, v.shape, 0)
    return jnp.where(t < n - k, pltpu.roll(v, n - k, axis=0), 0.0)


def _window_sum(p, w, shift):
    s, k = p, 1
    while k < w:
        s = s + shift(s, k)
        k *= 2
    return s


def _pool_count(shape, w):
    t = lax.broadcasted_iota(jnp.int32, shape, 0)
    return jnp.minimum(t + 1, w).astype(F32)


def _ab_specs(S):
    zs = [pl.BlockSpec((None, S, 128), functools.partial(lambda g, q: (2 * q + g // 2, 0, g % 2), q=q)) for q in range(4)]
    return zs


def _ab_mix_fwd(z8, conv_w, mix_w, scale):
    S = z8.shape[1]

    def body(b_ref, c_ref, a_ref, p_ref, w_ref, mix_ref, sc_ref, y_ref):
        g = pl.program_id(0)
        cg = c_ref[...].astype(F32) * a_ref[...].astype(F32)
        w = w_ref[...]
        conv = w[0:1] * _shift_down(cg, 2) + w[1:2] * _shift_down(cg, 1) + w[2:3] * cg
        y_ref[0] = (b_ref[...].astype(F32) * conv).astype(y_ref.dtype)
        for gg, win in enumerate(POOL_WINDOWS):
            @pl.when(g == gg)
            def _(win=win):
                p = p_ref[...].astype(F32)
                pooled = _window_sum(p, win, _shift_down) / _pool_count(p.shape, win) - p
                y_ref[1] = (_dot(pooled, mix_ref[...], "nn") * sc_ref[...]).astype(y_ref.dtype)

    return pl.pallas_call(
        body, name="ab_mix_fwd", grid=(4,), out_shape=jax.ShapeDtypeStruct((2, S, 512), _MXU_DTYPE),
        in_specs=_ab_specs(S) + [pl.BlockSpec((3, 128), lambda g: (0, g)), pl.BlockSpec((None, 128, 128), lambda g: (g, 0, 0)),
                                 pl.BlockSpec((1, 128), lambda g: (0, g))],
        out_specs=pl.BlockSpec((2, S, 128), lambda g: (0, 0, g)), compiler_params=_cp("parallel"))(z8, z8, z8, z8, conv_w, mix_w, scale)


def _ab_mix_bwd(z8, dycat2, conv_w, mix_w, scale, after):
    S = z8.shape[1]

    def body(b_ref, c_ref, a_ref, p_ref, dy_ref, w_ref, mix_ref, sc_ref, after_ref, dz_ref, dw_ref, dmix_ref, dsc_ref):
        g = pl.program_id(0)
        bv, cv, av, w = b_ref[...].astype(F32), c_ref[...].astype(F32), a_ref[...].astype(F32), w_ref[...]
        dya = dy_ref[0]
        cg = cv * av
        cg1, cg2 = _shift_down(cg, 1), _shift_down(cg, 2)
        conv = w[0:1] * cg2 + w[1:2] * cg1 + w[2:3] * cg
        dz_ref[0] = (dya * conv).astype(dz_ref.dtype)
        dconv = dya * bv
        dcg = w[2:3] * dconv + w[1:2] * _shift_up(dconv, 1) + w[0:1] * _shift_up(dconv, 2)
        dz_ref[1] = (dcg * av).astype(dz_ref.dtype)
        dz_ref[2] = (dcg * cv).astype(dz_ref.dtype)
        dw_ref[0:1, :] = jnp.sum(dconv * cg2, axis=0, keepdims=True)
        dw_ref[1:2, :] = jnp.sum(dconv * cg1, axis=0, keepdims=True)
        dw_ref[2:3, :] = jnp.sum(dconv * cg, axis=0, keepdims=True)
        for gg, win in enumerate(POOL_WINDOWS):
            @pl.when(g == gg)
            def _(win=win):
                p, dyb, mix = p_ref[...].astype(F32), dy_ref[1], mix_ref[...]
                cnt = _pool_count(p.shape, win)
                pooled = _window_sum(p, win, _shift_down) / cnt - p
                dsc_ref[...] = jnp.sum(dyb * _dot(pooled, mix, "nn"), axis=0, keepdims=True)
                dmixed = dyb * sc_ref[...]
                dmix_ref[...] = _dot(pooled, dmixed, "tn")
                dpooled = _dot(dmixed, mix, "nt")
                dz_ref[3] = (_window_sum(dpooled / cnt, win, _shift_up) - dpooled).astype(dz_ref.dtype)

    return pl.pallas_call(
        body, name="ab_mix_bwd", grid=(4,),
        out_shape=(jax.ShapeDtypeStruct((4, 2, S, 256), _MXU_DTYPE), jax.ShapeDtypeStruct((3, 512), F32),
                   jax.ShapeDtypeStruct((4, 128, 128), F32), jax.ShapeDtypeStruct((1, 512), F32)),
        in_specs=_ab_specs(S) + [pl.BlockSpec((2, S, 128), lambda g: (0, 0, g)), pl.BlockSpec((3, 128), lambda g: (0, g)),
                                 pl.BlockSpec((None, 128, 128), lambda g: (g, 0, 0)), pl.BlockSpec((1, 128), lambda g: (0, g)), ANY],
        out_specs=(pl.BlockSpec((4, None, S, 128), lambda g: (0, g // 2, 0, g % 2)), pl.BlockSpec((3, 128), lambda g: (0, g)),
                   pl.BlockSpec((None, 128, 128), lambda g: (g, 0, 0)), pl.BlockSpec((1, 128), lambda g: (0, g))),
        compiler_params=_cp("parallel"))(z8, z8, z8, z8, dycat2, conv_w, mix_w, scale, after)


HALO = 16


def _ffn_specs(S, n, tr):
    nb = S // HALO
    tile = pl.BlockSpec((2, None, tr, n), lambda j, i: (0, j, i, 0))
    prev = pl.BlockSpec((2, None, HALO, n), lambda j, i: (0, j, jnp.maximum(i * (tr // HALO) - 1, 0), 0))
    nxt = pl.BlockSpec((2, None, HALO, n), lambda j, i: (0, j, jnp.minimum((i + 1) * (tr // HALO), nb - 1), 0))
    cw = pl.BlockSpec((2, None, 3, n), lambda j, i: (0, j, 0, 0))
    return tile, prev, nxt, cw


def _shifted_rows(ext, lo, rows):
    ext = ext.astype(F32)
    return pltpu.roll(ext, 1, axis=0)[lo:lo + rows], pltpu.roll(ext, 2, axis=0)[lo:lo + rows]


def _ffn_gate_fwd(name, u24, cw24, tr=256):
    _, J, S, n = u24.shape
    tile, prev, _, cw = _ffn_specs(S, n, tr)

    def body(u_ref, up_ref, w_ref, a_ref, z_ref):
        keep = (pl.program_id(1) > 0).astype(u_ref.dtype)
        z = []
        for h in range(2):
            ext = jnp.concatenate([up_ref[h] * keep, u_ref[h]], axis=0)
            x1, x2 = _shifted_rows(ext, HALO, tr)
            w = w_ref[h]
            z.append(w[0:1] * x2 + w[1:2] * x1 + w[2:3] * u_ref[h].astype(F32))
        zg, zu = z
        sg = jax.nn.sigmoid(zg)
        silu = zg * sg
        a_ref[...] = (silu * zu).astype(a_ref.dtype)
        z_ref[0] = (zu * (sg * (1.0 + zg * (1.0 - sg)))).astype(z_ref.dtype)
        z_ref[1] = silu.astype(z_ref.dtype)

    return pl.pallas_call(
        body, name=name, grid=(J, S // tr),
        out_shape=(jax.ShapeDtypeStruct((J, S, n), _MXU_DTYPE), jax.ShapeDtypeStruct((2, J, S, n), ACT_DTYPE)),
        in_specs=[tile, prev, cw], out_specs=(pl.BlockSpec((None, tr, n), lambda j, i: (j, i, 0)), tile),
        compiler_params=_cp("parallel", "parallel"))(u24, u24, cw24)


def _ffn_gate_bwd(name, u24, z24, cw24, da4, w_up24, after, tr=256):
    _, J, S, n = u24.shape
    K = w_up24.shape[2]
    nb = S // HALO
    tile = pl.BlockSpec((2, None, tr, n), lambda i, j: (0, j, i, 0))
    nxt = pl.BlockSpec((2, None, HALO, n), lambda i, j: (0, j, jnp.minimum((i + 1) * (tr // HALO), nb - 1), 0))
    whole = lambda shape: pl.BlockSpec(shape, lambda i, j: (0,) * len(shape))

    def body(u_ref, z_ref, zn_ref, cw_ref, da_ref, dan_ref, wup_ref, after_ref, du_ref, dcw_ref, dh_ref, acc_ref):
        i, j = pl.program_id(0), pl.program_id(1)
        first = i == 0
        keep_next = (i < S // tr - 1).astype(F32)
        w = [cw_ref[h, j] for h in range(2)]
        m = tr + HALO
        da = jnp.concatenate([da_ref[...].astype(F32), dan_ref[...].astype(F32) * keep_next], axis=0)
        dz = [da * jnp.concatenate([z_ref[h], zn_ref[h]], axis=0).astype(F32) for h in range(2)]
        dh = None
        for h in range(2):
            d = dz[h]
            d0, d1, d2 = d[:tr], pltpu.roll(d, m - 1, axis=0)[:tr], pltpu.roll(d, m - 2, axis=0)[:tr]
            du = (w[h][2:3] * d0 + w[h][1:2] * d1 + w[h][0:1] * d2).astype(du_ref.dtype)
            du_ref[h] = du
            part = _dot(du, wup_ref[h, j], "nt")
            dh = part if dh is None else dh + part
            x0 = u_ref[h].astype(F32)
            parts = [jnp.sum(x0 * dk, axis=0, keepdims=True) for dk in (d2, d1, d0)]
            for k in range(3):
                @pl.when(first)
                def _(k=k, h=h):
                    dcw_ref[h, j, k:k + 1, :] = parts[k]

                @pl.when(jnp.logical_not(first))
                def _(k=k, h=h):
                    dcw_ref[h, j, k:k + 1, :] += parts[k]

        @pl.when(j == 0)
        def _():
            acc_ref[...] = dh

        @pl.when(j > 0)
        def _():
            acc_ref[...] += dh

        @pl.when(j == J - 1)
        def _():
            dh_ref[...] = acc_ref[...].astype(dh_ref.dtype)

    da_tile = pl.BlockSpec((None, tr, n), lambda i, j: (j, i, 0))
    da_next = pl.BlockSpec((None, HALO, n), lambda i, j: (j, jnp.minimum((i + 1) * (tr // HALO), nb - 1), 0))
    return pl.pallas_call(
        body, name=name, grid=(S // tr, J),
        out_shape=(jax.ShapeDtypeStruct((2, J, S, n), _MXU_DTYPE), jax.ShapeDtypeStruct((2, J, 3, n), F32),
                   jax.ShapeDtypeStruct((S, K), ACT_DTYPE)),
        in_specs=[tile, tile, nxt, whole((2, J, 3, n)), da_tile, da_next, whole((2, J, K, n)), ANY],
        out_specs=(tile, whole((2, J, 3, n)), pl.BlockSpec((tr, K), lambda i, j: (i, 0))),
        scratch_shapes=[pltpu.VMEM((tr, K), F32)],
        compiler_params=_cp("arbitrary", "arbitrary"))(u24, z24, z24, cw24, da4, da4, w_up24, after)


def _rms_rows(v, g):
    rstd = lax.rsqrt(jnp.mean(v * v, axis=-1, keepdims=True) + EPS)
    return v * rstd * g


def _rms_rows_bwd(v, g, dy):
    rstd = lax.rsqrt(jnp.mean(v * v, axis=-1, keepdims=True) + EPS)
    vhat = v * rstd
    dvhat = dy * g
    return rstd * (dvhat - vhat * jnp.mean(dvhat * vhat, axis=-1, keepdims=True)), dy * vhat


def _rope(v, cos, sa, sb):
    return v * cos + pltpu.roll(v, 112, axis=1) * sa + pltpu.roll(v, 16, axis=1) * sb


def _rope_t(d, cos, sa, sb):
    return d * cos + pltpu.roll(d * sa, 16, axis=1) + pltpu.roll(d * sb, 112, axis=1)


def _qkv_rope_fwd(z, qg, kvg, w_uq_t, w_kv, cosq, cosk, sa, sb, tr=512):
    S = z.shape[0]

    def body(ql_ref, kvl_ref, kpe_ref, qg_ref, kvg_ref, wq_ref, wkv_ref, cq_ref, ck_ref, sa_ref, sb_ref,
             qn_ref, kvn_ref, qo_ref, ko_ref, vo_ref):
        cq, ck, sa_v, sb_v = cq_ref[...], ck_ref[...], sa_ref[...], sb_ref[...]
        qn = _rms_rows(ql_ref[...], qg_ref[...]).astype(qn_ref.dtype)
        kvn = _rms_rows(kvl_ref[...], kvg_ref[...]).astype(kvn_ref.dtype)
        qn_ref[...] = qn
        kvn_ref[...] = kvn
        q = _dot(qn, wq_ref[...], "nt")
        kv = _dot(kvn, wkv_ref[...], "nn")
        kpe = _rope(kpe_ref[...], ck, sa_v, sb_v)
        for h in range(8):
            cols = slice(128 * h, 128 * h + 128)
            qo_ref[:, cols] = _rope(q[:, cols], cq, sa_v, sb_v).astype(qo_ref.dtype)
            ko_ref[:, cols] = (kv[:, cols] + kpe).astype(ko_ref.dtype)
        vo_ref[...] = kv[:, 1024:1536].astype(vo_ref.dtype)

    tab = _row_spec(tr, 128)
    whole = lambda a: pl.BlockSpec(a.shape, lambda i: (0, 0))
    return pl.pallas_call(
        body, name="qkv_rope_fwd", grid=(S // tr,),
        out_shape=(jax.ShapeDtypeStruct((S, 256), _MXU_DTYPE), jax.ShapeDtypeStruct((S, 128), _MXU_DTYPE),
                   jax.ShapeDtypeStruct((S, 1024), _MXU_DTYPE), jax.ShapeDtypeStruct((S, 1024), _MXU_DTYPE),
                   jax.ShapeDtypeStruct((S, 512), _MXU_DTYPE)),
        in_specs=[pl.BlockSpec((tr, 256), lambda i: (i, 0)), pl.BlockSpec((tr, 128), lambda i: (i, 2)),
                  pl.BlockSpec((tr, 128), lambda i: (i, 3)), _vec_spec(256), _vec_spec(128), whole(w_uq_t), whole(w_kv),
                  tab, tab, tab, tab],
        out_specs=(_row_spec(tr, 256), _row_spec(tr, 128), _row_spec(tr, 1024), _row_spec(tr, 1024), _row_spec(tr, 512)),
        compiler_params=_cp("parallel"))(z, z, z, qg, kvg, w_uq_t, w_kv, cosq, cosk, sa, sb)


def _attn_bwd_prep(o, dycat2, tr=512):
    S = o.shape[0]

    def body(o_ref, do_ref, delta_ref, doa_ref, dob_ref):
        do = do_ref[...]
        prod = do * o_ref[...]
        lane = lax.broadcasted_iota(jnp.int32, do.shape, 1)
        for p in range(4):
            cols = slice(128 * p, 128 * p + 128)
            first = lane[:, cols] < 128 * p + 64
            da = jnp.sum(jnp.where(first, prod[:, cols], 0.0), axis=-1, keepdims=True)
            db = jnp.sum(jnp.where(first, 0.0, prod[:, cols]), axis=-1, keepdims=True)
            delta_ref[p] = jnp.where(first, da, db)
            doa_ref[p] = jnp.where(first, do[:, cols], 0.0).astype(doa_ref.dtype)
            dob_ref[p] = jnp.where(first, 0.0, do[:, cols]).astype(dob_ref.dtype)

    pair = pl.BlockSpec((4, tr, 128), lambda i: (0, i, 0))
    return pl.pallas_call(
        body, name="attn_bwd_prep", grid=(S // tr,),
        out_shape=(jax.ShapeDtypeStruct((4, S, 128), F32), jax.ShapeDtypeStruct((4, S, 128), _MXU_DTYPE),
                   jax.ShapeDtypeStruct((4, S, 128), _MXU_DTYPE)),
        in_specs=[_row_spec(tr, 512), pl.BlockSpec((None, tr, 512), lambda i: (0, i, 0))],
        out_specs=(pair, pair, pair), compiler_params=_cp("parallel"))(o, dycat2)


def _qkv_rope_bwd(z, qg, kvg, dq, dk, dv, duv, w_uq_t, w_kv, cosq, cosk, sa, sb, tr=512):
    S = z.shape[0]

    def body(ql_ref, kvl_ref, qg_ref, kvg_ref, dq_ref, dk_ref, dv_ref, duv_ref, wq_ref, wkv_ref, cq_ref, ck_ref, sa_ref, sb_ref,
             dqo_ref, dkv_ref, dz_ref, dqg_ref, dkvg_ref):
        first = pl.program_id(0) == 0
        cq, ck, sa_v, sb_v = cq_ref[...], ck_ref[...], sa_ref[...], sb_ref[...]
        tot = jnp.zeros((tr, 128), F32)
        for h in range(8):
            cols = slice(128 * h, 128 * h + 128)
            dqo_ref[:, cols] = _rope_t(dq_ref[:, cols], cq, sa_v, sb_v).astype(dqo_ref.dtype)
            dkh = dk_ref[:, cols]
            tot = tot + dkh
            dkv_ref[:, cols] = dkh.astype(dkv_ref.dtype)
        dkv_ref[:, 1024:1536] = dv_ref[...].astype(dkv_ref.dtype)
        dqn = _dot(dqo_ref[...], wq_ref[...], "nn")
        dkvn = _dot(dkv_ref[...], wkv_ref[...], "nt")
        dql, dqg = _rms_rows_bwd(ql_ref[...], qg_ref[...], dqn)
        dkvl, dkvg = _rms_rows_bwd(kvl_ref[...], kvg_ref[...], dkvn)
        _acc_rows(dqg_ref, dqg, first)
        _acc_rows(dkvg_ref, dkvg, first)
        dz_ref[:, 0:256] = dql.astype(dz_ref.dtype)
        dz_ref[:, 256:384] = dkvl.astype(dz_ref.dtype)
        dz_ref[:, 384:512] = _rope_t(tot, ck, sa_v, sb_v).astype(dz_ref.dtype)
        dz_ref[:, 512:1536] = duv_ref[...].astype(dz_ref.dtype)

    tab = _row_spec(tr, 128)
    whole = lambda a: pl.BlockSpec(a.shape, lambda i: (0, 0))
    return pl.pallas_call(
        body, name="qkv_rope_bwd", grid=(S // tr,),
        out_shape=(jax.ShapeDtypeStruct((S, 1024), _MXU_DTYPE), jax.ShapeDtypeStruct((S, 1536), _MXU_DTYPE),
                   jax.ShapeDtypeStruct((S, 1536), _MXU_DTYPE), jax.ShapeDtypeStruct((1, 256), F32), jax.ShapeDtypeStruct((1, 128), F32)),
        in_specs=[pl.BlockSpec((tr, 256), lambda i: (i, 0)), pl.BlockSpec((tr, 128), lambda i: (i, 2)), _vec_spec(256), _vec_spec(128),
                  _row_spec(tr, 1024), _row_spec(tr, 1024), _row_spec(tr, 512), _row_spec(tr, 1024), whole(w_uq_t), whole(w_kv),
                  tab, tab, tab, tab],
        out_specs=(_row_spec(tr, 1024), _row_spec(tr, 1536), _row_spec(tr, 1536), _vec_spec(256), _vec_spec(128)),
        compiler_params=_cp("arbitrary"))(z, z, qg, kvg, dq, dk, dv, duv, w_uq_t, w_kv, cosq, cosk, sa, sb)


NEG = -1e30


def _attn_fwd(q, k, v, tq=1024, tk=1024):
    S = q.shape[0]
    assert tq == tk

    def body(q_ref, k_ref, v_ref, o_ref, lse_ref):
        i = pl.program_id(1)
        qs = [q_ref[:, 0:128], q_ref[:, 128:256]]

        def step(kb, carry, diagonal=False):
            start = pl.multiple_of(kb * tk, tk)
            vv = v_ref[pl.ds(start, tk), :]
            out = []
            for h in range(2):
                m, l, acc = carry[3 * h:3 * h + 3]
                s = _dot(qs[h], k_ref[pl.ds(start, tk), 128 * h:128 * h + 128], "nt") * ATTN_SCALE
                if diagonal:
                    s = jnp.where(below, s, NEG)
                m_new = jnp.maximum(m, jnp.max(s, axis=-1, keepdims=True))
                alpha = jnp.exp(m - m_new)
                p = jnp.exp(s - m_new)
                out += [m_new, alpha * l + jnp.sum(p, axis=-1, keepdims=True), alpha * acc + _dot(p, vv, "nn")]
            return tuple(out)

        below = lax.broadcasted_iota(jnp.int32, (tq, tk), 1) <= lax.broadcasted_iota(jnp.int32, (tq, tk), 0)
        init = (jnp.full((tq, 1), NEG, F32), jnp.zeros((tq, 1), F32), jnp.zeros((tq, 128), F32)) * 2
        ma, la, acca, mb, lb, accb = step(i, lax.fori_loop(0, i, step, init), diagonal=True)
        lane = lax.broadcasted_iota(jnp.int32, (tq, 128), 1)
        o_ref[...] = jnp.where(lane < 64, acca / la, accb / lb)
        lse_ref[...] = jnp.where(lane < 64, ma + jnp.log(la), mb + jnp.log(lb))

    return pl.pallas_call(
        body, name="attn_fwd", grid=(4, S // tq),
        out_shape=(jax.ShapeDtypeStruct((S, 512), F32), jax.ShapeDtypeStruct((4, S, 128), F32)),
        in_specs=[pl.BlockSpec((tq, 256), lambda p, i: (i, p)), pl.BlockSpec((S, 256), lambda p, i: (0, p)),
                  pl.BlockSpec((S, 128), lambda p, i: (0, p))],
        out_specs=(pl.BlockSpec((tq, 128), lambda p, i: (i, p)), pl.BlockSpec((None, tq, 128), lambda p, i: (p, i, 0))),
        compiler_params=_cp("parallel", "parallel"))(q, k, v)


def _attn_bwd(q, k, v, lse, delta, doa, dob, tq=512, tk=512):
    S = q.shape[0]
    assert tq == tk

    def body(q_ref, k_ref, v_ref, lse_ref, delta_ref, doa_ref, dob_ref, dq_ref, dk_ref, dv_ref):
        j = pl.program_id(1)

        @pl.when(j == 0)
        def _():
            dq_ref[...] = jnp.zeros_like(dq_ref)

        below = lax.broadcasted_iota(jnp.int32, (tq, tk), 1) <= lax.broadcasted_iota(jnp.int32, (tq, tk), 0)
        ks = [k_ref[:, 0:128], k_ref[:, 128:256]]
        vv = v_ref[...]

        def step(qb, carry, diagonal=False):
            dka, dkb, dvp = carry
            start = pl.multiple_of(qb * tq, tq)
            rows = pl.ds(start, tq)
            lse_v, delta_v = lse_ref[rows, :], delta_ref[rows, :]
            dos = [doa_ref[rows, :], dob_ref[rows, :]]
            dks = [dka, dkb]
            for h in range(2):
                delta = delta_v[:, 64 * h:64 * h + 1]
                do_h = dos[h]
                qh = q_ref[rows, 128 * h:128 * h + 128]
                s = _dot(qh, ks[h], "nt") * ATTN_SCALE
                p = jnp.exp(s - lse_v[:, 64 * h:64 * h + 1])
                if diagonal:
                    p = jnp.where(below, p, 0.0)
                dvp = dvp + _dot(p, do_h, "tn")
                ds = p * (_dot(do_h, vv, "nt") - delta) * ATTN_SCALE
                dq_ref[rows, 128 * h:128 * h + 128] += _dot(ds, ks[h], "nn")
                dks[h] = dks[h] + _dot(ds, qh, "tn")
            return dks[0], dks[1], dvp

        zero = jnp.zeros((tk, 128), F32)
        dka, dkb, dvp = lax.fori_loop(j + 1, S // tq, step, step(j, (zero, zero, zero), diagonal=True))
        dk_ref[:, 0:128] = dka
        dk_ref[:, 128:256] = dkb
        dv_ref[...] = dvp

    return pl.pallas_call(
        body, name="attn_bwd", grid=(4, S // tk),
        out_shape=(jax.ShapeDtypeStruct((S, 1024), F32), jax.ShapeDtypeStruct((S, 1024), F32), jax.ShapeDtypeStruct((S, 512), F32)),
        in_specs=[pl.BlockSpec((S, 256), lambda p, j: (0, p)), pl.BlockSpec((tk, 256), lambda p, j: (j, p)),
                  pl.BlockSpec((tk, 128), lambda p, j: (j, p))] + [pl.BlockSpec((None, S, 128), lambda p, j: (p, 0, 0))] * 4,
        out_specs=(pl.BlockSpec((S, 256), lambda p, j: (0, p)), pl.BlockSpec((tk, 256), lambda p, j: (j, p)),
                   pl.BlockSpec((tk, 128), lambda p, j: (j, p))),
        compiler_params=_cp("parallel", "arbitrary"))(q, k, v, lse, delta, doa, dob)


CHUNK = 128
GELU_C = math.sqrt(2.0 / math.pi)


def _gelu(v):
    t = jnp.tanh(GELU_C * (v + 0.044715 * (v * v * v)))
    return v * (0.5 * (1.0 + t)), t


def _gelu_grad(v, t):
    return 0.5 * (1.0 + t) + v * (0.5 * (1.0 - t * t) * GELU_C * (1.0 + 3.0 * 0.044715 * v * v))


def _tril(w):
    r = lax.broadcasted_iota(jnp.int32, w.shape, 0)
    c = lax.broadcasted_iota(jnp.int32, w.shape, 1)
    return jnp.where(c <= r, w, 0.0)


def _layer_norm(v, g, b):
    xc = v - jnp.mean(v, axis=-1, keepdims=True)
    rstd = lax.rsqrt(jnp.mean(xc * xc, axis=-1, keepdims=True) + EPS)
    xhat = xc * rstd
    return xhat * g + b, xhat, rstd


def _sgu_fwd(z, o, ln_g, ln_b, w_s, b_st, tr=512):
    S = z.shape[0]

    def body(u_ref, v_ref, o_ref, g_ref, b_ref, ws_ref, bs_ref, y_ref):
        gu, _ = _gelu(u_ref[...])
        gv, _ = _gelu(v_ref[...])
        vln, _, _ = _layer_norm(gv, g_ref[...], b_ref[...])
        y_ref[0] = o_ref[...].astype(y_ref.dtype)
        for g in range(4):
            wt = _tril(ws_ref[g])
            cols = slice(128 * g, 128 * g + 128)
            for ch in range(tr // CHUNK):
                rows = slice(CHUNK * ch, CHUNK * ch + CHUNK)
                mixed = _dot(wt, vln[rows, cols], "nn") + bs_ref[:, g:g + 1]
                y_ref[1, rows, cols] = (gu[rows, cols] * mixed).astype(y_ref.dtype)

    return pl.pallas_call(
        body, name="sgu_fwd", grid=(S // tr,), out_shape=jax.ShapeDtypeStruct((2, S, 512), _MXU_DTYPE),
        in_specs=[pl.BlockSpec((tr, 512), lambda i: (i, 1)), pl.BlockSpec((tr, 512), lambda i: (i, 2)), _row_spec(tr, 512),
                  _vec_spec(512), _vec_spec(512), pl.BlockSpec((4, 128, 128), lambda i: (0, 0, 0)), pl.BlockSpec((128, 4), lambda i: (0, 0))],
        out_specs=pl.BlockSpec((2, tr, 512), lambda i: (0, i, 0)), compiler_params=_cp("parallel"))(z, z, o, ln_g, ln_b, w_s, b_st)


def _sgu_bwd(z, dycat2, ln_g, ln_b, w_s, b_st, tr=512):
    S = z.shape[0]

    def body(u_ref, v_ref, dy_ref, g_ref, b_ref, ws_ref, bs_ref, duv_ref, dg_ref, db_ref, dws_ref, dbs_ref):
        first = pl.program_id(0) == 0
        u_pre, v_pre = u_ref[...], v_ref[...]
        gu, tu = _gelu(u_pre)
        gv, tv = _gelu(v_pre)
        gain = g_ref[...]
        vln, xhat, rstd = _layer_norm(gv, gain, b_ref[...])

        @pl.when(first)
        def _():
            dws_ref[...] = jnp.zeros_like(dws_ref)
            dbs_ref[...] = jnp.zeros_like(dbs_ref)

        dvln_cols = []
        for g in range(4):
            wt = _tril(ws_ref[g])
            cols = slice(128 * g, 128 * g + 128)
            dmixed_sum = jnp.zeros((CHUNK, 128), F32)
            dw = jnp.zeros((CHUNK, CHUNK), F32)
            dvln_rows = []
            for ch in range(tr // CHUNK):
                rows = slice(CHUNK * ch, CHUNK * ch + CHUNK)
                vt = vln[rows, cols]
                mixed = _dot(wt, vt, "nn") + bs_ref[:, g:g + 1]
                dyd = dy_ref[rows, cols]
                duv_ref[rows, cols] = (dyd * mixed * _gelu_grad(u_pre[rows, cols], tu[rows, cols])).astype(duv_ref.dtype)
                dmixed = dyd * gu[rows, cols]
                dmixed_sum = dmixed_sum + dmixed
                dw = dw + _dot(dmixed, vt, "nt")
                dvln_rows.append(_dot(wt, dmixed, "tn"))
            dws_ref[g] += _tril(dw)
            dbs_ref[g:g + 1, :] += jnp.sum(dmixed_sum.T, axis=0, keepdims=True)
            dvln_cols.append(jnp.concatenate(dvln_rows, axis=0))
        dvln = jnp.concatenate(dvln_cols, axis=1)
        _acc_rows(dg_ref, dvln * xhat, first)
        _acc_rows(db_ref, dvln, first)
        dxhat = dvln * gain
        dgv = rstd * (dxhat - jnp.mean(dxhat, axis=-1, keepdims=True) - xhat * jnp.mean(dxhat * xhat, axis=-1, keepdims=True))
        duv_ref[:, 512:1024] = (dgv * _gelu_grad(v_pre, tv)).astype(duv_ref.dtype)

    return pl.pallas_call(
        body, name="sgu_bwd", grid=(S // tr,),
        out_shape=(jax.ShapeDtypeStruct((S, 1024), _MXU_DTYPE), jax.ShapeDtypeStruct((1, 512), F32), jax.ShapeDtypeStruct((1, 512), F32),
                   jax.ShapeDtypeStruct((4, 128, 128), F32), jax.ShapeDtypeStruct((4, 128), F32)),
        in_specs=[pl.BlockSpec((tr, 512), lambda i: (i, 1)), pl.BlockSpec((tr, 512), lambda i: (i, 2)),
                  pl.BlockSpec((None, tr, 512), lambda i: (1, i, 0)), _vec_spec(512), _vec_spec(512),
                  pl.BlockSpec((4, 128, 128), lambda i: (0, 0, 0)), pl.BlockSpec((128, 4), lambda i: (0, 0))],
        out_specs=(_row_spec(tr, 1024), _vec_spec(512), _vec_spec(512), pl.BlockSpec((4, 128, 128), lambda i: (0, 0, 0)),
                   pl.BlockSpec((4, 128), lambda i: (0, 0))),
        compiler_params=_cp("arbitrary"))(z, z, dycat2, ln_g, ln_b, w_s, b_st)


def _adamw_math(w, m, v, g):
    c1 = 1.0 / (1.0 - ADAM_B1 ** ADAM_STEP)
    c2 = 1.0 / (1.0 - ADAM_B2 ** ADAM_STEP)
    m2 = ADAM_B1 * m + (1.0 - ADAM_B1) * g
    v2 = ADAM_B2 * v + (1.0 - ADAM_B2) * (g * g)
    return -ADAM_LR * ((m2 * c1) / (jnp.sqrt(v2 * c2) + ADAM_EPS) + ADAM_WD * w), m2, v2


def _adamw_small(name, params, parts):
    n = len(params)

    def body(*refs):
        ins, outs = refs[:4 * n], refs[4 * n:]
        for i in range(n):
            w_ref, m_ref, v_ref, p_ref = ins[4 * i:4 * i + 4]
            g = p_ref[0].astype(F32)
            for k in range(1, N_DEV):
                g = g + p_ref[k].astype(F32)
            delta, m2, v2 = _adamw_math(w_ref[...], m_ref[...], v_ref[...], g)
            outs[4 * i][...] = g
            outs[4 * i + 1][...] = delta
            outs[4 * i + 2][...] = m2
            outs[4 * i + 3][...] = v2

    flat = [a for (w, m, v), p in zip(params, parts) for a in (w, m, v, p)]
    out = pl.pallas_call(
        body, name=name, out_shape=[jax.ShapeDtypeStruct(w.shape, F32) for (w, _, _) in params for _ in range(4)],
        compiler_params=pltpu.CompilerParams(vmem_limit_bytes=_VMEM_LIMIT))(*flat)
    return [out[4 * i:4 * i + 4] for i in range(n)]


ADAMW_BLOCK_BYTES = 36 * 2 ** 20


def _adamw(name, w, m, v, parts):
    L, R, C = w.shape
    P = parts[0].shape[0]
    row_bytes = 2 * C * (7 * 4 + P * parts[0].dtype.itemsize)
    tr = R
    if R * row_bytes > ADAMW_BLOCK_BYTES:
        tr = next(t for t in (1024, 512, 256, 128, 64, 32, 16) if R % t == 0 and t * row_bytes <= ADAMW_BLOCK_BYTES)
    nr = R // tr
    c1 = 1.0 / (1.0 - ADAM_B1 ** ADAM_STEP)
    c2 = 1.0 / (1.0 - ADAM_B2 ** ADAM_STEP)

    def body(w_ref, m_ref, v_ref, *rest):
        p_refs, (g_ref, d_ref, mo_ref, vo_ref) = rest[:L], rest[L:]
        for ll in range(L):
            @pl.when(pl.program_id(0) == ll)
            def _(p_ref=p_refs[ll]):
                g = p_ref[0].astype(F32)
                for k in range(1, P):
                    g = g + p_ref[k].astype(F32)
                m2 = ADAM_B1 * m_ref[...] + (1.0 - ADAM_B1) * g
                v2 = ADAM_B2 * v_ref[...] + (1.0 - ADAM_B2) * (g * g)
                g_ref[...] = g
                mo_ref[...] = m2
                vo_ref[...] = v2
                d_ref[...] = -ADAM_LR * ((m2 * c1) / (jnp.sqrt(v2 * c2) + ADAM_EPS) + ADAM_WD * w_ref[...])

    def part_spec(ll):
        return pl.BlockSpec((P, tr, C), lambda l, i: (0, jnp.where(l == ll, i, jnp.where(l < ll, 0, nr - 1)), 0))

    full = pl.BlockSpec((None, tr, C), lambda l, i: (l, i, 0))
    sds = jax.ShapeDtypeStruct((L, R, C), F32)
    return pl.pallas_call(
        body, name=name, grid=(L, nr), out_shape=(sds, sds, sds, sds),
        in_specs=[full] * 3 + [part_spec(ll) for ll in range(L)],
        out_specs=(full,) * 4, compiler_params=_cp("arbitrary", "arbitrary"))(w, m, v, *parts)


def _rope_tables(positions):
    half = 16
    inv_freq = 10000.0 ** (-jnp.arange(half, dtype=F32) / half)
    ang = positions.astype(F32)[:, None] * inv_freq
    cos, sin = jnp.cos(ang), jnp.sin(ang)
    S = positions.shape[0]
    z16, z32, z64 = jnp.zeros((S, 16), F32), jnp.zeros((S, 32), F32), jnp.zeros((S, 64), F32)
    cosk = jnp.concatenate([z64, cos, cos, z32], axis=1)
    cosq = jnp.concatenate([jnp.ones((S, 64), F32), cos, cos, z32], axis=1)
    sa = jnp.concatenate([z64, -sin, z16, z32], axis=1)
    sb = jnp.concatenate([z64, z16, sin, z32], axis=1)
    return cosq, cosk, sa, sb


def _ffn_fwd(l, x, mod, n2g, get_w_up8, cw24, get_w_down4):
    sh, sc, gate = mod
    h = _rmsmod_fwd(f"ffn{l}_norm", x, n2g, sc, sh, n2g)
    w_up8 = get_w_up8(h)
    u8 = _mm_cols(f"ffn{l}_up", h, w_up8, out_dtype=ACT_DTYPE, tm=2048)
    S, n = u8.shape[1], u8.shape[2]
    u24 = u8.reshape(2, 4, S, n)
    a4, z24 = _ffn_gate_fwd(f"ffn{l}_gate", u24, cw24)
    w_down4 = get_w_down4(a4)
    f, x_new = _mm_rows_resid(f"ffn{l}_down", a4, w_down4, x, gate)
    return x_new, (x, h, u24, a4, f, z24), w_up8, w_down4


def _ffn_bwd(l, dx, df, dgate, saved, mod, n2g, w_up8, cw24, w_down4, me, y_prev, gate_prev):
    sh, sc, gate = mod
    x, h, u24, a4, f, z24 = saved
    da4 = _mm_rows_dx(f"ffn{l}_down_dx", df, w_down4, out_dtype=ACT_DTYPE, tm=2048)
    dw_down4 = _mm_rows_dw(f"ffn{l}_down_dw", a4, df, out_dtype=WIRE_DTYPE, tn=1024)
    sent_down, token = _exchange_start(f"scatter_ffn{l}_down", [dw_down4.reshape(8, 352, dw_down4.shape[2])], True, dgate, me)
    du24, dcw24, dh = _ffn_gate_bwd(f"ffn{l}_act_bwd", u24, z24, cw24, da4, w_up8.reshape((2, 4) + w_up8.shape[1:]), token)
    du8 = du24.reshape((8,) + du24.shape[2:])
    dw_up8t = _mm_cols_dwt(f"ffn{l}_up_dw", h, du8, out_dtype=WIRE_DTYPE, tk=1024)
    sent_up, token = _exchange_start(f"scatter_ffn{l}_up", [dw_up8t], True, dcw24, me)
    dx_new, dn2g, dsc, dsh, dy_prev, dgate_prev = _rmsmod_bwd(f"ffn{l}_norm_bwd", x, n2g, sc, dh, dx, token, y_prev, gate_prev)
    return dx_new, dict(sent_up=sent_up, sent_down=sent_down, cw24=dcw24, n2g=dn2g, mod=(dsh, dsc, dgate)), dy_prev, dgate_prev


def kernel(x, c, positions, ada_w, ada_b, norm1_g, norm2_g, ab_w_in, a_conv_w, b_mix_w, b_scale, ab_w_out, cd_w_in, c_q_norm_g, c_w_uq, c_kv_norm_g, c_w_ukv, d_ln_g, d_ln_b, d_w_s, d_b_s, cd_w_out, ffn_w_up, ffn_conv_w, ffn_w_down, final_norm_g, loss_target, m_ada_w, m_ada_b, m_norm1_g, m_norm2_g, m_ab_w_in, m_a_conv_w, m_b_mix_w, m_b_scale, m_ab_w_out, m_cd_w_in, m_c_q_norm_g, m_c_w_uq, m_c_kv_norm_g, m_c_w_ukv, m_d_ln_g, m_d_ln_b, m_d_w_s, m_d_b_s, m_cd_w_out, m_ffn_w_up, m_ffn_conv_w, m_ffn_w_down, m_final_norm_g, v_ada_w, v_ada_b, v_norm1_g, v_norm2_g, v_ab_w_in, v_a_conv_w, v_b_mix_w, v_b_scale, v_ab_w_out, v_cd_w_in, v_c_q_norm_g, v_c_w_uq, v_c_kv_norm_g, v_c_w_ukv, v_d_ln_g, v_d_ln_b, v_d_w_s, v_d_b_s, v_cd_w_out, v_ffn_w_up, v_ffn_conv_w, v_ffn_w_down, v_final_norm_g):
    S, D = x.shape[1], x.shape[2]
    me = 4 * lax.axis_index("x") + 2 * lax.axis_index("y") + lax.axis_index("c")
    x0, target = x[0], loss_target[0]
    W = _MXU_DTYPE

    small_shapes = [(1024,), (3, 64), (32,), (64,), (64,), (2, 3, 704)]
    (g0,) = _exchange("gather_small", [[_pack([c, a_conv_w, c_q_norm_g, d_ln_g, d_ln_b, ffn_conv_w])]], scatter=False)
    c_all, aconv_s, qg_s, lng_s, lnb_s, fcw_s = _unpack(g0[:, 0], small_shapes, lead=(N_DEV,))
    conv_w = aconv_s.transpose(1, 0, 2).reshape(3, 512)
    qg, ln_g, ln_b = qg_s.reshape(1, 256), lng_s.reshape(1, 512), lnb_s.reshape(1, 512)
    cw24 = [fcw_s[:, l].reshape(2, 4, 3, 704) for l in range(2)]
    c16 = jnp.pad(c_all, ((0, 16 - N_DEV), (0, 0)))

    mod_cols = _ada_fwd(c16, ada_w)
    (g1,) = _exchange("scatter_mod", [[mod_cols[:, :N_DEV].transpose(1, 0, 2)]], scatter=True)
    mod_mine = g1[:, 0]
    mod = mod_mine.transpose(1, 0, 2).reshape(2, 6 * D) + ada_b
    mods = [[mod[l, k * D:(k + 1) * D].reshape(1, D) for k in range(6)] for l in range(2)]

    gw_ab, token = _hier_gather_start("gather_w_ab", [ab_w_in[0].astype(W), ab_w_out[0].astype(W)], mod, me)
    gw_up0, token = _hier_gather_start("gather_w_ffn0_up", [ffn_w_up[0].astype(W)], token, me)
    gw_rest, started = _exchange_start("gather_w_rest", [
        ffn_w_down[0].astype(W), cd_w_in[0].T.astype(W), c_w_uq[0].T.astype(W), c_w_ukv[0].astype(W), cd_w_out[0].astype(W),
        ffn_w_up[1].astype(W), ffn_w_down[1].astype(W)], False, token, me)

    cosq, cosk, sa, sb = _rope_tables(positions[0])
    n1g = [norm1_g[l].reshape(1, D) for l in range(2)]
    n2g = [norm2_g[l].reshape(1, D) for l in range(2)]
    mix_w, scale = b_mix_w[0], b_scale
    kvg = c_kv_norm_g
    w_s, b_st = d_w_s[0], d_b_s[0].T

    sh1, sc1, g1m = mods[0][:3]
    h_ab = _rmsmod_fwd("ab_norm", x0, n1g[0], sc1, sh1, started)
    w_abin8, w_about = _hier_gather_wait("wait_w_ab", _hier_gather_forward("forward_w_ab", gw_ab, h_ab), h_ab)
    w_about2 = w_about.reshape(2, 512, D)
    z8 = _mm_cols("ab_in", h_ab, w_abin8, out_dtype=ACT_DTYPE, tm=2048)
    ycat_ab = _ab_mix_fwd(z8, conv_w, mix_w, scale)
    y_ab, x1 = _mm_rows_resid("ab_out", ycat_ab, w_about2, x0, g1m)
    w_up8, w_down4 = [None, None], [None, None]
    gw_up0 = _hier_gather_forward("forward_w_ffn0_up", gw_up0, x1)
    x2, ffn0_saved, w_up8[0], w_down4[0] = _ffn_fwd(
        0, x1, mods[0][3:], n2g[0], lambda after: _hier_gather_wait("wait_w_ffn0_up", gw_up0, after)[0], cw24[0],
        lambda after: _exchange_wait("wait_w_ffn0_down", gw_rest, after, [0])[0].reshape(4, 704, D))

    w_cdin, w_uq, w_ukv, w_cdout = _exchange_wait("wait_w_cd", gw_rest, x2, [1, 2, 3, 4])
    w_cdout2 = w_cdout.reshape(2, 512, D)
    w_cd_t = w_cdin.reshape(1440, D)
    zr = lambda n: jnp.zeros((n, D), W)
    w_cd_pad = jnp.concatenate([w_cd_t[:384], zr(64), w_cd_t[384:416], zr(32), w_cd_t[416:]], axis=0)
    w_uq_pad = jnp.pad(w_uq, ((0, 0), (0, 32), (0, 0))).reshape(1024, 256)
    w_ukv_h = w_ukv.transpose(1, 0, 2)
    w_k_pad = jnp.pad(w_ukv_h[:, :, :64], ((0, 0), (0, 0), (0, 64))).reshape(128, 1024)
    w_kv_pad = jnp.concatenate([w_k_pad, w_ukv_h[:, :, 64:].reshape(128, 512)], axis=1)

    sh1, sc1, g1c = mods[1][:3]
    h_cd = _rmsmod_fwd("cd_norm", x2, n1g[1], sc1, sh1, n1g[1])
    z_cd = _mm_nt("cd_in", h_cd, w_cd_pad, tm=1024, tn=1536)
    qn, kvn, q_r, k_r, v_r = _qkv_rope_fwd(z_cd, qg, kvg, w_uq_pad, w_kv_pad, cosq, cosk, sa, sb)
    o, lse = _attn_fwd(q_r, k_r, v_r)
    ycat_cd = _sgu_fwd(z_cd, o, ln_g, ln_b, w_s, b_st)
    y_cd, x3 = _mm_rows_resid("cd_out", ycat_cd, w_cdout2, x2, g1c)
    x4, ffn1_saved, w_up8[1], w_down4[1] = _ffn_fwd(
        1, x3, mods[1][3:], n2g[1], lambda after: _exchange_wait("wait_w_ffn1_up", gw_rest, after, [5])[0], cw24[1],
        lambda after: _exchange_wait("wait_w_ffn1_down", gw_rest, after, [6])[0].reshape(4, 704, D))

    loss_local, dx4, dfg, df1, dgate1 = _loss_head(x4, final_norm_g.reshape(1, D), target, ffn1_saved[4], mods[1][5])

    dx3, gf1, dy, dg1c = _ffn_bwd(1, dx4, df1, dgate1, ffn1_saved, mods[1][3:], n2g[1], w_up8[1], cw24[1], w_down4[1], me, y_cd, g1c)

    dycat = _mm_rows_dx("cd_out_dx", dy, w_cdout2, tm=2048)
    dw_cdout = _mm_rows_dw("cd_out_dw", ycat_cd, dy, out_dtype=WIRE_DTYPE, tn=1024)
    duv, dln_g, dln_b, dws, dbs = _sgu_bwd(z_cd, dycat, ln_g, ln_b, w_s, b_st)
    dq_r, dk_r, dv_r = _attn_bwd(q_r, k_r, v_r, lse, *_attn_bwd_prep(o, dycat))
    dqraw, dkvall, dz_cd, dqg, dkvg = _qkv_rope_bwd(z_cd, qg, kvg, dq_r, dk_r, dv_r, duv, w_uq_pad, w_kv_pad, cosq, cosk, sa, sb)
    dw_uq_pad = _mm_tn("cd_uq_dw", dqraw, qn, tn=256)
    dw_kv_pad = _mm_tn("cd_ukv_dw", kvn, dkvall, tm=128)
    dh_cd = _mm_nn("cd_in_dx", dz_cd, w_cd_pad, out_dtype=ACT_DTYPE, tm=1024, tn=1024)
    dw_cd_pad = _mm_tn("cd_in_dw", dz_cd, h_cd, tm=768, tn=1024)
    dw_cd8 = jnp.concatenate([dw_cd_pad[:384], dw_cd_pad[448:480], dw_cd_pad[512:]], axis=0).astype(WIRE_DTYPE).reshape(8, 180, D)
    dw_uq8 = dw_uq_pad.reshape(8, 128, 256)[:, :96].astype(WIRE_DTYPE)
    dw_ukv8 = jnp.concatenate([dw_kv_pad[:, :1024].reshape(128, 8, 128)[:, :, :64], dw_kv_pad[:, 1024:].reshape(128, 8, 64)],
                              axis=2).transpose(1, 0, 2).astype(WIRE_DTYPE)
    early_names = ["c_kv_norm_g", "d_w_s", "d_b_s", "final_norm_g", "c_q_norm_g", "d_ln_g", "d_ln_b"]
    early_grads = [dkvg, dws.reshape(512, 128).astype(WIRE_DTYPE), dbs, dfg, dqg.reshape(8, 1, 32), dln_g.reshape(8, 1, 64),
                   dln_b.reshape(8, 1, 64)]
    sent_cd, token = _exchange_start("scatter_cd", [dw_cd8, dw_uq8, dw_ukv8, dw_cdout.reshape(8, 128, D)] + early_grads,
                                     [True] * 4 + [False] * 4 + [True] * 3, dqg, me)
    dx2, dn1g_cd, dsc1_cd, dsh1_cd, df0, dgate0 = _rmsmod_bwd("cd_norm_bwd", x2, n1g[1], sc1, dh_cd, dx3, token,
                                                              ffn0_saved[4], mods[0][5])

    dx1, gf0, dy, dg1m = _ffn_bwd(0, dx2, df0, dgate0, ffn0_saved, mods[0][3:], n2g[0], w_up8[0], cw24[0], w_down4[0], me, y_ab, g1m)

    dw_about = _mm_rows_dw("ab_out_dw", ycat_ab, dy, out_dtype=WIRE_DTYPE, tn=1024)
    sent_about, token = _exchange_start("scatter_ab_out", [dw_about.reshape(8, 128, D)], True, dg1m, me)
    dycat = _mm_rows_dx("ab_out_dx", dy, w_about2, tm=2048)
    dz8, dconv_w, dmix_w, dscale = _ab_mix_bwd(z8, dycat, conv_w, mix_w, scale, token)
    dz8 = dz8.reshape(8, S, 256)
    dw_abin8 = _mm_cols_dw("ab_in_dw", h_ab, dz8, out_dtype=WIRE_DTYPE, tk=1024)
    sent_abin, token = _exchange_start("scatter_ab_in", [dw_abin8], True, dscale, me)
    dh_ab = _mm_cols_dx("ab_in_dx", dz8, w_abin8, out_dtype=ACT_DTYPE)
    dx0, dn1g_ab, dsc1_ab, dsh1_ab = _rmsmod_bwd("ab_norm_bwd", x0, n1g[0], mods[0][1], dh_ab, dx1, token)

    dmod = jnp.stack([jnp.concatenate([dsh1_ab, dsc1_ab, dg1m, *gf0["mod"]], axis=1)[0],
                      jnp.concatenate([dsh1_cd, dsc1_cd, dg1c, *gf1["mod"]], axis=1)[0]])
    late_names = ["ada_b", "norm1_g", "norm2_g", "b_mix_w", "b_scale", "a_conv_w", "ffn_conv_w"]
    late_grads = [dmod, jnp.concatenate([dn1g_ab, dn1g_cd]), jnp.concatenate([gf0["n2g"], gf1["n2g"]]),
                  dmix_w.reshape(512, 128).astype(WIRE_DTYPE), dscale, dconv_w.reshape(3, 8, 64).transpose(1, 0, 2),
                  jnp.stack([gf0["cw24"].reshape(8, 3, 704), gf1["cw24"].reshape(8, 3, 704)], axis=1),
                  jnp.pad(loss_local, ((0, 0), (0, 127)))]
    small_view = dict(ada_b=(2, 6 * D), norm1_g=(2, D), norm2_g=(2, D), b_mix_w=(512, 128), b_scale=(1, 512), c_kv_norm_g=(1, 128),
                      d_w_s=(512, 128), d_b_s=(4, 128), final_norm_g=(1, D),
                      a_conv_w=(3, 64), c_q_norm_g=(1, 32), d_ln_g=(1, 64), d_ln_b=(1, 64), ffn_conv_w=(2, 3, 704))
    late_sent, token = _exchange_start("gather_small_grads_late", late_grads, [False] * 5 + [True] * 2 + [False], dx0, me)

    res = {}

    def update(name, w, m, v, parts, shape3d):
        outs = _adamw("adamw_" + name, w.reshape(shape3d), m.reshape(shape3d), v.reshape(shape3d),
                      [p.reshape((p.shape[0],) + shape3d[1:]) for p in parts])
        res[name] = [o_.reshape(w.shape) for o_ in outs]

    p_cdin, p_uq, p_ukv, p_cdout = _exchange_wait("wait_scatter_cd", sent_cd, token, [0, 1, 2, 3])
    swap = lambda a: jnp.swapaxes(a, 1, 2)
    update("cd_w_in", swap(cd_w_in), swap(m_cd_w_in), swap(v_cd_w_in), [p_cdin], (1, 180, D))
    update("c_w_uq", swap(c_w_uq), swap(m_c_w_uq), swap(v_c_w_uq), [p_uq], (1, 96, 256))
    for name in ("cd_w_in", "c_w_uq"):
        res[name] = [swap(o_) for o_ in res[name]]
    update("c_w_ukv", c_w_ukv, m_c_w_ukv, v_c_w_ukv, [p_ukv], (1, 128, 128))
    update("cd_w_out", cd_w_out, m_cd_w_out, v_cd_w_out, [p_cdout], (1, 128, D))
    (p_dn1,) = _exchange_wait("wait_scatter_ffn1_down", gf1["sent_down"], token)
    (p_dn0,) = _exchange_wait("wait_scatter_ffn0_down", gf0["sent_down"], res["cd_w_out"][0])
    update("ffn_w_down", ffn_w_down, m_ffn_w_down, v_ffn_w_down, [p_dn0, p_dn1], (2, 352, D))
    (p_up1,) = _exchange_wait("wait_scatter_ffn1_up", gf1["sent_up"], token)
    (p_up0,) = _exchange_wait("wait_scatter_ffn0_up", gf0["sent_up"], res["ffn_w_down"][0])
    swap = lambda a: jnp.swapaxes(a, 1, 2)
    update("ffn_w_up", swap(ffn_w_up), swap(m_ffn_w_up), swap(v_ffn_w_up), [p_up0, p_up1], (2, 704, D))
    up_done = res["ffn_w_up"][0]
    res["ffn_w_up"] = [swap(o_) for o_ in res["ffn_w_up"]]
    (p_about,) = _exchange_wait("wait_scatter_ab_out", sent_about, up_done)
    update("ab_w_out", ab_w_out, m_ab_w_out, v_ab_w_out, [p_about], (1, 128, D))
    (p_abin,) = _exchange_wait("wait_scatter_ab_in", sent_abin, res["ab_w_out"][0])
    update("ab_w_in", ab_w_in, m_ab_w_in, v_ab_w_in, [p_abin], (1, D, 256))

    early_parts = _exchange_wait("wait_small_grads_early", sent_cd, res["ab_w_in"][0], list(range(4, 11)))
    late_parts = _exchange_wait("wait_small_grads_late", late_sent, res["ab_w_in"][0])
    small_names = early_names + late_names
    small_parts = list(early_parts) + list(late_parts[:7])
    loss = jnp.sum(late_parts[7][:, 0, 0])
    dmod_all = late_parts[0]
    dmod_cols = lax.dynamic_slice_in_dim(dmod_all, me * 768, 768, axis=2).transpose(1, 0, 2)
    g_ada_w = _ada_bwd(c16, jnp.pad(dmod_cols, ((0, 0), (0, 16 - N_DEV), (0, 0))))
    update("ada_w", ada_w, m_ada_w, v_ada_w, [g_ada_w[None]], (1, 2 * D, 768))

    small_w = dict(ada_b=(ada_b, m_ada_b, v_ada_b), norm1_g=(norm1_g, m_norm1_g, v_norm1_g), norm2_g=(norm2_g, m_norm2_g, v_norm2_g),
                   b_mix_w=(b_mix_w, m_b_mix_w, v_b_mix_w), b_scale=(b_scale, m_b_scale, v_b_scale),
                   c_kv_norm_g=(c_kv_norm_g, m_c_kv_norm_g, v_c_kv_norm_g), d_w_s=(d_w_s, m_d_w_s, v_d_w_s),
                   d_b_s=(d_b_s, m_d_b_s, v_d_b_s), final_norm_g=(final_norm_g, m_final_norm_g, v_final_norm_g),
                   a_conv_w=(a_conv_w, m_a_conv_w, v_a_conv_w), c_q_norm_g=(c_q_norm_g, m_c_q_norm_g, v_c_q_norm_g),
                   d_ln_g=(d_ln_g, m_d_ln_g, v_d_ln_g), d_ln_b=(d_ln_b, m_d_ln_b, v_d_ln_b),
                   ffn_conv_w=(ffn_conv_w, m_ffn_conv_w, v_ffn_conv_w))
    small_out = _adamw_small("adamw_small", [tuple(a.reshape(small_view[n]) for a in small_w[n]) for n in small_names],
                             list(small_parts))
    for n, outs in zip(small_names, small_out):
        res[n] = [o_.reshape(small_w[n][0].shape) for o_ in outs]

    order = ["ada_w", "ada_b", "norm1_g", "norm2_g", "ab_w_in", "a_conv_w", "b_mix_w", "b_scale", "ab_w_out", "cd_w_in", "c_q_norm_g",
             "c_w_uq", "c_kv_norm_g", "c_w_ukv", "d_ln_g", "d_ln_b", "d_w_s", "d_b_s", "cd_w_out", "ffn_w_up", "ffn_conv_w",
             "ffn_w_down", "final_norm_g"]
    return (loss, dx0[None], *[res[n][0] for n in order], *[res[n][1] for n in order], *[res[n][2] for n in order],
            *[res[n][3] for n in order])
```

```python
import functools
import math

import jax
import jax.numpy as jnp
from jax import lax
from jax.experimental import pallas as pl
from jax.experimental.pallas import tpu as pltpu

F32 = jnp.float32
BF16 = jnp.bfloat16
_MXU_DTYPE = BF16
WIRE_DTYPE = BF16
ACT_DTYPE = BF16
_VMEM_LIMIT = 56 * 2 ** 20
N_DEV = 8
EPS = 1e-6
POOL_WINDOWS = (2, 4, 8, 16)
ATTN_SCALE = (64 + 32) ** -0.5
ADAM_LR, ADAM_B1, ADAM_B2, ADAM_EPS, ADAM_WD, ADAM_STEP = 0.001, 0.9, 0.999, 1e-08, 0.01, 10
MESH = pl.DeviceIdType.MESH
ANY = pl.BlockSpec(memory_space=pl.ANY)


def _cp(*sem):
    return pltpu.CompilerParams(dimension_semantics=sem, vmem_limit_bytes=_VMEM_LIMIT)


def _dot(a, b, contract):
    dn = {"nn": (((1,), (0,)), ((), ())), "nt": (((1,), (1,)), ((), ())), "tn": (((0,), (0,)), ((), ()))}[contract]
    return lax.dot_general(a.astype(_MXU_DTYPE), b.astype(_MXU_DTYPE), dn, preferred_element_type=F32)


def _my_position():
    x, y, c = lax.axis_index("x"), lax.axis_index("y"), lax.axis_index("c")
    return x, y, c, 4 * x + 2 * y + c


def _exchange(name, groups, scatter):
    flat = [a for g in groups for a in g]
    n_in, n_grp = len(flat), len(groups)
    out_shapes = []
    for g in groups:
        slab = g[0].shape[1:] if scatter else g[0].shape
        out_shapes.append(jax.ShapeDtypeStruct((N_DEV, len(g)) + tuple(slab), g[0].dtype))

    def body(*refs):
        ins, outs = refs[:n_in], refs[n_in:n_in + n_grp]
        send_sems, recv_sems, local_sems = refs[n_in + n_grp:]
        x, y, c, me = _my_position()
        i = 0
        for gi, g in enumerate(groups):
            for l in range(len(g)):
                src = ins[i]
                i += 1
                pltpu.make_async_copy(src.at[me] if scatter else src, outs[gi].at[me, l], local_sems.at[gi]).start()
                for k in range(1, N_DEV):
                    px = 1 - x if k & 4 else x
                    py = 1 - y if k & 2 else y
                    pc = 1 - c if k & 1 else c
                    peer = 4 * px + 2 * py + pc
                    pltpu.make_async_remote_copy(
                        src_ref=src.at[peer] if scatter else src, dst_ref=outs[gi].at[me, l],
                        send_sem=send_sems.at[gi], recv_sem=recv_sems.at[gi],
                        device_id=(px, py, pc), device_id_type=MESH).start()
        for gi in range(n_grp):
            mine = outs[gi].at[me]
            pltpu.make_async_copy(mine, mine, local_sems.at[gi]).wait()
            seven = outs[gi].at[pl.ds(0, N_DEV - 1)]
            w = pltpu.make_async_remote_copy(src_ref=seven, dst_ref=seven, send_sem=send_sems.at[gi],
                                             recv_sem=recv_sems.at[gi], device_id=(x, y, c), device_id_type=MESH)
            w.wait_send()
            w.wait_recv()

    return pl.pallas_call(
        body, name=name, out_shape=tuple(out_shapes),
        in_specs=[ANY] * n_in, out_specs=tuple([ANY] * n_grp),
        scratch_shapes=[pltpu.SemaphoreType.DMA((n_grp,)), pltpu.SemaphoreType.DMA((n_grp,)),
                        pltpu.SemaphoreType.DMA((n_grp,))],
        compiler_params=pltpu.CompilerParams(has_side_effects=True),
    )(*flat)


HBM_SPEC = pl.BlockSpec(memory_space=pltpu.HBM)
SEM_SPEC = pl.BlockSpec(memory_space=pltpu.SEMAPHORE)
EFFECT = pltpu.SideEffectType.DATAFLOW_SIDE_EFFECTING


def _put_mine(name, srcs, scatter, me):
    n = len(srcs)
    slabs = [tuple(s.shape[1:] if sc else s.shape) for s, sc in zip(srcs, scatter)]

    def body(me_ref, *refs):
        for i in range(n):
            refs[n + i][...] = refs[i][...]

    def at_me(slab):
        return pl.BlockSpec((None,) + slab, lambda g, me_ref, nd=len(slab): (me_ref[0],) + (0,) * nd)

    def whole(slab):
        return pl.BlockSpec(slab, lambda g, me_ref, nd=len(slab): (0,) * nd)

    return pl.pallas_call(
        body, name=name,
        grid_spec=pltpu.PrefetchScalarGridSpec(
            num_scalar_prefetch=1, grid=(1,),
            in_specs=[at_me(slab) if sc else whole(slab) for slab, sc in zip(slabs, scatter)],
            out_specs=[at_me(slab) for slab in slabs]),
        out_shape=[jax.ShapeDtypeStruct((N_DEV,) + slab, s.dtype) for slab, s in zip(slabs, srcs)],
        compiler_params=_cp("arbitrary"))(me.reshape(1), *srcs)


def _exchange_start(name, srcs, scatter, after, me):
    n = len(srcs)
    scatter = list(scatter) if isinstance(scatter, (list, tuple)) else [scatter] * n
    lands = _put_mine(name + "_mine", srcs, scatter, me)
    srcs = [pltpu.with_memory_space_constraint(a, pltpu.HBM) for a in srcs]
    lands = [pltpu.with_memory_space_constraint(a, pltpu.HBM) for a in lands]

    def body(*refs):
        ins, land = refs[:n], refs[n:2 * n]
        send_sems, recv_sems, token = refs[2 * n + 1], refs[2 * n + 2], refs[-1]
        x, y, c, me_in = _my_position()
        for i in range(n):
            for k in range(1, N_DEV):
                px = 1 - x if k & 4 else x
                py = 1 - y if k & 2 else y
                pc = 1 - c if k & 1 else c
                pltpu.make_async_remote_copy(
                    src_ref=ins[i].at[4 * px + 2 * py + pc] if scatter[i] else ins[i], dst_ref=land[i].at[me_in],
                    send_sem=send_sems.at[i], recv_sem=recv_sems.at[i],
                    device_id=(px, py, pc), device_id_type=MESH).start()
        token[...] = jnp.zeros_like(token)

    outs = pl.pallas_call(
        body, name=name,
        out_shape=(pltpu.SemaphoreType.DMA((n,)), pltpu.SemaphoreType.DMA((n,)),
                   *[pltpu.HBM(a.shape, a.dtype) for a in srcs], *[pltpu.HBM(a.shape, a.dtype) for a in lands],
                   jax.ShapeDtypeStruct((8, 128), F32)),
        in_specs=[HBM_SPEC] * (2 * n) + [ANY],
        out_specs=(SEM_SPEC, SEM_SPEC, *[HBM_SPEC] * (2 * n), pl.BlockSpec(memory_space=pltpu.VMEM)),
        input_output_aliases={i: 2 + i for i in range(2 * n)},
        compiler_params=pltpu.CompilerParams(has_side_effects=EFFECT),
    )(*srcs, *lands, after)
    return (outs[0], outs[1], outs[2:2 + n], outs[2 + n:2 + 2 * n]), outs[-1]


def _exchange_wait(name, handle, after, which=None):
    send_sems, recv_sems, srcs, lands = handle
    which = list(range(len(srcs))) if which is None else list(which)
    srcs, lands = [srcs[i] for i in which], [lands[i] for i in which]
    n = len(srcs)

    def body(*refs):
        land, send_ref, recv_ref = refs[n:2 * n], refs[2 * n], refs[2 * n + 1]
        x, y, c, _ = _my_position()
        for k, i in enumerate(which):
            seven = land[k].at[pl.ds(0, N_DEV - 1)]
            w = pltpu.make_async_remote_copy(src_ref=seven, dst_ref=seven, send_sem=send_ref.at[i], recv_sem=recv_ref.at[i],
                                             device_id=(x, y, c), device_id_type=MESH)
            w.wait_send()
            w.wait_recv()

    outs = pl.pallas_call(
        body, name=name,
        out_shape=(*[pltpu.HBM(a.shape, a.dtype) for a in srcs], *[pltpu.HBM(a.shape, a.dtype) for a in lands]),
        in_specs=[HBM_SPEC] * (2 * n) + [SEM_SPEC, SEM_SPEC, ANY],
        out_specs=tuple([HBM_SPEC] * (2 * n)),
        input_output_aliases={i: i for i in range(2 * n)},
        compiler_params=pltpu.CompilerParams(has_side_effects=EFFECT),
    )(*srcs, *lands, send_sems, recv_sems, after)
    return outs[n:]


def _other_chips(x, y):
    return [(1 - x, y), (x, 1 - y), (1 - x, 1 - y)]


def _hier_gather_start(name, srcs, after, me):
    n = len(srcs)
    lands = _put_mine(name + "_mine", srcs, [False] * n, me)
    srcs = [pltpu.with_memory_space_constraint(a, pltpu.HBM) for a in srcs]
    lands = [pltpu.with_memory_space_constraint(a, pltpu.HBM) for a in lands]

    def body(*refs):
        ins, land = refs[:n], refs[n:2 * n]
        ici_send, ici_recv, d2d_send, d2d_recv = refs[2 * n + 1:2 * n + 5]
        token = refs[-1]
        x, y, c, me_in = _my_position()
        for i in range(n):
            pltpu.make_async_remote_copy(src_ref=ins[i], dst_ref=land[i].at[me_in], send_sem=d2d_send.at[i], recv_sem=d2d_recv.at[i],
                                         device_id=(x, y, 1 - c), device_id_type=MESH).start()
            for px, py in _other_chips(x, y):
                pltpu.make_async_remote_copy(src_ref=ins[i], dst_ref=land[i].at[me_in], send_sem=ici_send.at[i],
                                             recv_sem=ici_recv.at[i], device_id=(px, py, c), device_id_type=MESH).start()
        token[...] = jnp.zeros_like(token)

    sem = pltpu.SemaphoreType.DMA((n,))
    outs = pl.pallas_call(
        body, name=name,
        out_shape=(sem, sem, sem, sem, *[pltpu.HBM(a.shape, a.dtype) for a in srcs], *[pltpu.HBM(a.shape, a.dtype) for a in lands],
                   jax.ShapeDtypeStruct((8, 128), F32)),
        in_specs=[HBM_SPEC] * (2 * n) + [ANY],
        out_specs=(SEM_SPEC,) * 4 + (HBM_SPEC,) * (2 * n) + (pl.BlockSpec(memory_space=pltpu.VMEM),),
        input_output_aliases={i: 4 + i for i in range(2 * n)},
        compiler_params=pltpu.CompilerParams(has_side_effects=EFFECT),
    )(*srcs, *lands, after)
    return (outs[:4], outs[4:4 + n], outs[4 + n:4 + 2 * n]), outs[-1]


def _hier_gather_forward(name, handle, after):
    sems, srcs, lands = handle
    n = len(srcs)

    def body(*refs):
        land = refs[n:2 * n]
        ici_send, ici_recv, d2d_send, d2d_recv = refs[2 * n:2 * n + 4]
        x, y, c, _ = _my_position()
        for i in range(n):
            three = land[i].at[pl.ds(0, 3)]
            pltpu.make_async_remote_copy(src_ref=three, dst_ref=three, send_sem=ici_send.at[i], recv_sem=ici_recv.at[i],
                                         device_id=(x, y, c), device_id_type=MESH).wait_recv()
            for px, py in _other_chips(x, y):
                slab = land[i].at[4 * px + 2 * py + c]
                pltpu.make_async_remote_copy(src_ref=slab, dst_ref=slab, send_sem=d2d_send.at[i], recv_sem=d2d_recv.at[i],
                                             device_id=(x, y, 1 - c), device_id_type=MESH).start()

    outs = pl.pallas_call(
        body, name=name,
        out_shape=(*[pltpu.HBM(a.shape, a.dtype) for a in srcs], *[pltpu.HBM(a.shape, a.dtype) for a in lands]),
        in_specs=[HBM_SPEC] * (2 * n) + [SEM_SPEC] * 4 + [ANY],
        out_specs=tuple([HBM_SPEC] * (2 * n)),
        input_output_aliases={i: i for i in range(2 * n)},
        compiler_params=pltpu.CompilerParams(has_side_effects=EFFECT),
    )(*srcs, *lands, *sems, after)
    return (sems, outs[:n], outs[n:])


def _hier_gather_wait(name, handle, after):
    sems, srcs, lands = handle
    n = len(srcs)

    def body(*refs):
        land = refs[n:2 * n]
        ici_send, ici_recv, d2d_send, d2d_recv = refs[2 * n:2 * n + 4]
        x, y, c, _ = _my_position()
        for i in range(n):
            three, four = land[i].at[pl.ds(0, 3)], land[i].at[pl.ds(0, 4)]
            pltpu.make_async_remote_copy(src_ref=three, dst_ref=three, send_sem=ici_send.at[i], recv_sem=ici_recv.at[i],
                                         device_id=(x, y, c), device_id_type=MESH).wait_send()
            w = pltpu.make_async_remote_copy(src_ref=four, dst_ref=four, send_sem=d2d_send.at[i], recv_sem=d2d_recv.at[i],
                                             device_id=(x, y, c), device_id_type=MESH)
            w.wait_send()
            w.wait_recv()

    outs = pl.pallas_call(
        body, name=name,
        out_shape=(*[pltpu.HBM(a.shape, a.dtype) for a in srcs], *[pltpu.HBM(a.shape, a.dtype) for a in lands]),
        in_specs=[HBM_SPEC] * (2 * n) + [SEM_SPEC] * 4 + [ANY],
        out_specs=tuple([HBM_SPEC] * (2 * n)),
        input_output_aliases={i: i for i in range(2 * n)},
        compiler_params=pltpu.CompilerParams(has_side_effects=EFFECT),
    )(*srcs, *lands, *sems, after)
    return outs[n:]


def _pack(arrs):
    flat = jnp.concatenate([a.reshape(-1).astype(F32) for a in arrs])
    n = flat.shape[0]
    rows = -(-n // 1024) * 8
    return jnp.pad(flat, (0, rows * 128 - n)).reshape(rows, 128)


def _unpack(buf, shapes, lead=()):
    flat = buf.reshape(lead + (-1,))
    out, off = [], 0
    for s in shapes:
        n = math.prod(s)
        out.append(flat[..., off:off + n].reshape(lead + tuple(s)))
        off += n
    return out


def _mm(name, a, a_spec, b, b_spec, out_sds, o_spec, grid, contract, nk=1, stacked=0):
    o_blk = tuple(d for d in o_spec.block_shape if d is not None)

    def body(a_ref, b_ref, o_ref, *acc):
        if stacked:
            r = _dot(a_ref[0], b_ref[0], contract)
            for q in range(1, stacked):
                r = r + _dot(a_ref[q], b_ref[q], contract)
        else:
            r = _dot(a_ref[...], b_ref[...], contract)
        if nk == 1:
            o_ref[...] = r.astype(o_ref.dtype)
        else:
            k = pl.program_id(len(grid) - 1)

            @pl.when(k == 0)
            def _():
                acc[0][...] = r

            @pl.when(k > 0)
            def _():
                acc[0][...] += r

            @pl.when(k == nk - 1)
            def _():
                o_ref[...] = acc[0][...].astype(o_ref.dtype)

    sem = ("parallel",) * (len(grid) - 1) + (("arbitrary",) if nk > 1 else ("parallel",))
    return pl.pallas_call(
        body, name=name, out_shape=out_sds, grid=grid, in_specs=[a_spec, b_spec], out_specs=o_spec,
        scratch_shapes=[pltpu.VMEM(o_blk, F32)] if nk > 1 else [], compiler_params=_cp(*sem))(a, b)


def _tile(n, want):
    t = min(n, want)
    assert n % t == 0, (n, t)
    return t


def _mm_nn(name, a, b, out_dtype=F32, tm=512, tn=512):
    (M, K), N = a.shape, b.shape[1]
    tm, tn = _tile(M, tm), _tile(N, tn)
    return _mm(name, a, pl.BlockSpec((tm, K), lambda i, j: (i, 0)), b, pl.BlockSpec((K, tn), lambda i, j: (0, j)),
               jax.ShapeDtypeStruct((M, N), out_dtype), pl.BlockSpec((tm, tn), lambda i, j: (i, j)),
               (M // tm, N // tn), "nn")


def _mm_nt(name, a, b, out_dtype=F32, tm=512, tn=512):
    (M, K), N = a.shape, b.shape[0]
    tm, tn = _tile(M, tm), _tile(N, tn)
    return _mm(name, a, pl.BlockSpec((tm, K), lambda i, j: (i, 0)), b, pl.BlockSpec((tn, K), lambda i, j: (j, 0)),
               jax.ShapeDtypeStruct((M, N), out_dtype), pl.BlockSpec((tm, tn), lambda i, j: (i, j)),
               (M // tm, N // tn), "nt")


def _mm_tn(name, a, b, out_dtype=F32, tm=512, tn=512):
    (K, M), N = a.shape, b.shape[1]
    tm, tn = _tile(M, tm), _tile(N, tn)
    return _mm(name, a, pl.BlockSpec((K, tm), lambda i, j: (0, i)), b, pl.BlockSpec((K, tn), lambda i, j: (0, j)),
               jax.ShapeDtypeStruct((M, N), out_dtype), pl.BlockSpec((tm, tn), lambda i, j: (i, j)),
               (M // tm, N // tn), "tn")


def _mm_cols(name, a, w, out_dtype=F32, tm=512):
    (M, K), (J, _, n) = a.shape, w.shape
    tm = _tile(M, tm)
    return _mm(name, a, pl.BlockSpec((tm, K), lambda j, i: (i, 0)), w, pl.BlockSpec((None, K, n), lambda j, i: (j, 0, 0)),
               jax.ShapeDtypeStruct((J, M, n), out_dtype), pl.BlockSpec((None, tm, n), lambda j, i: (j, i, 0)),
               (J, M // tm), "nn")


def _mm_cols_dx(name, d, w, out_dtype=F32, tm=512, jb=None):
    (J, M, n), K = d.shape, w.shape[1]
    tm, jb = _tile(M, tm), J if jb is None else jb
    return _mm(name, d, pl.BlockSpec((jb, tm, n), lambda i, j: (j, i, 0)), w, pl.BlockSpec((jb, K, n), lambda i, j: (j, 0, 0)),
               jax.ShapeDtypeStruct((M, K), out_dtype), pl.BlockSpec((tm, K), lambda i, j: (i, 0)),
               (M // tm, J // jb), "nt", nk=J // jb, stacked=jb)


def _mm_cols_dw(name, a, d, out_dtype=F32, tk=512):
    (M, K), (J, _, n) = a.shape, d.shape
    tk = _tile(K, tk)
    return _mm(name, a, pl.BlockSpec((M, tk), lambda j, i: (0, i)), d, pl.BlockSpec((None, M, n), lambda j, i: (j, 0, 0)),
               jax.ShapeDtypeStruct((J, K, n), out_dtype), pl.BlockSpec((None, tk, n), lambda j, i: (j, i, 0)),
               (J, K // tk), "tn")


def _mm_cols_dwt(name, a, d, out_dtype=F32, tk=512):
    (M, K), (J, _, n) = a.shape, d.shape
    tk = _tile(K, tk)
    return _mm(name, d, pl.BlockSpec((None, M, n), lambda j, i: (j, 0, 0)), a, pl.BlockSpec((M, tk), lambda j, i: (0, i)),
               jax.ShapeDtypeStruct((J, n, K), out_dtype), pl.BlockSpec((None, n, tk), lambda j, i: (j, 0, i)),
               (J, K // tk), "tn")


def _mm_rows_resid(name, a, w, resid, gate, tm=512):
    (Q, M, k), N = a.shape, w.shape[2]
    tm = _tile(M, tm)

    def body(a_ref, w_ref, r_ref, g_ref, y_ref, x_ref):
        y = _dot(a_ref[0], w_ref[0], "nn")
        for q in range(1, Q):
            y = y + _dot(a_ref[q], w_ref[q], "nn")
        y_ref[...] = y.astype(y_ref.dtype)
        x_ref[...] = r_ref[...] + g_ref[...] * y

    return pl.pallas_call(
        body, name=name, grid=(M // tm,),
        out_shape=(jax.ShapeDtypeStruct((M, N), ACT_DTYPE), jax.ShapeDtypeStruct((M, N), F32)),
        in_specs=[pl.BlockSpec((Q, tm, k), lambda i: (0, i, 0)), pl.BlockSpec((Q, k, N), lambda i: (0, 0, 0)),
                  pl.BlockSpec((tm, N), lambda i: (i, 0)), pl.BlockSpec((1, N), lambda i: (0, 0))],
        out_specs=(pl.BlockSpec((tm, N), lambda i: (i, 0)), pl.BlockSpec((tm, N), lambda i: (i, 0))),
        compiler_params=_cp("parallel"))(a, w, resid, gate)


def _mm_rows_dx(name, d, w, out_dtype=F32, tm=512):
    (M, N), (Q, k, _) = d.shape, w.shape
    tm = _tile(M, tm)
    return _mm(name, d, pl.BlockSpec((tm, N), lambda q, i: (i, 0)), w, pl.BlockSpec((None, k, N), lambda q, i: (q, 0, 0)),
               jax.ShapeDtypeStruct((Q, M, k), out_dtype), pl.BlockSpec((None, tm, k), lambda q, i: (q, i, 0)),
               (Q, M // tm), "nt")


def _mm_rows_dw(name, a, d, out_dtype=F32, tn=512):
    (Q, M, k), N = a.shape, d.shape[1]
    tn = _tile(N, tn)
    return _mm(name, a, pl.BlockSpec((None, M, k), lambda q, j: (q, 0, 0)), d, pl.BlockSpec((M, tn), lambda q, j: (0, j)),
               jax.ShapeDtypeStruct((Q, k, N), out_dtype), pl.BlockSpec((None, k, tn), lambda q, j: (q, 0, j)),
               (Q, N // tn), "tn")


def _silu(v):
    return v * jax.nn.sigmoid(v)


def _ada_fwd(c16, ada_w):
    L, D, n = ada_w.shape

    def body(c_ref, w_ref, o_ref):
        o_ref[...] = _dot(_silu(c_ref[...]), w_ref[...], "nn")

    return pl.pallas_call(
        body, name="ada_fwd", grid=(L,), out_shape=jax.ShapeDtypeStruct((L, 16, n), F32),
        in_specs=[pl.BlockSpec((16, D), lambda l: (0, 0)), pl.BlockSpec((None, D, n), lambda l: (l, 0, 0))],
        out_specs=pl.BlockSpec((None, 16, n), lambda l: (l, 0, 0)), compiler_params=_cp("parallel"))(c16, ada_w)


def _ada_bwd(c16, dmod16):
    L, _, n = dmod16.shape
    D = c16.shape[1]

    def body(c_ref, d_ref, o_ref):
        o_ref[...] = _dot(_silu(c_ref[...]), d_ref[...], "tn")

    return pl.pallas_call(
        body, name="ada_bwd", grid=(L,), out_shape=jax.ShapeDtypeStruct((L, D, n), F32),
        in_specs=[pl.BlockSpec((16, D), lambda l: (0, 0)), pl.BlockSpec((None, 16, n), lambda l: (l, 0, 0))],
        out_specs=pl.BlockSpec((None, D, n), lambda l: (l, 0, 0)), compiler_params=_cp("parallel"))(c16, dmod16)


def _row_spec(tr, n):
    return pl.BlockSpec((tr, n), lambda i: (i, 0))


def _vec_spec(n):
    return pl.BlockSpec((1, n), lambda i: (0, 0))


def _rmsmod_fwd(name, x, g, sc, sh, after, tr=512):
    S, D = x.shape

    def body(x_ref, g_ref, sc_ref, sh_ref, after_ref, h_ref):
        xv = x_ref[...]
        rstd = lax.rsqrt(jnp.mean(xv * xv, axis=-1, keepdims=True) + EPS)
        y = xv * rstd * g_ref[...]
        h_ref[...] = (y * (1.0 + sc_ref[...]) + sh_ref[...]).astype(h_ref.dtype)

    return pl.pallas_call(
        body, name=name, grid=(S // tr,), out_shape=jax.ShapeDtypeStruct((S, D), _MXU_DTYPE),
        in_specs=[_row_spec(tr, D), _vec_spec(D), _vec_spec(D), _vec_spec(D), ANY], out_specs=_row_spec(tr, D),
        compiler_params=_cp("parallel"))(x, g, sc, sh, after)


def _acc_rows(ref, val, first):
    s = jnp.sum(val, axis=0, keepdims=True)

    @pl.when(first)
    def _():
        ref[...] = s

    @pl.when(jnp.logical_not(first))
    def _():
        ref[...] += s


def _gate_bwd_tail(dx, y_ref, gate_ref, dy_ref, dgate_ref, first):
    dy_ref[...] = (gate_ref[...] * dx).astype(dy_ref.dtype)
    _acc_rows(dgate_ref, dx * y_ref[...].astype(F32), first)


def _rmsmod_bwd(name, x, g, sc, dh, dres, after, y=None, gate=None, tr=512):
    S, D = x.shape
    tail = y is not None

    def body(x_ref, g_ref, sc_ref, dh_ref, dres_ref, after_ref, *rest):
        (y_ref, gate_ref), rest = (rest[:2], rest[2:]) if tail else ((None, None), rest)
        dx_ref, dg_ref, dsc_ref, dsh_ref = rest[:4]
        first = pl.program_id(0) == 0
        xv, dh_v, gv = x_ref[...], dh_ref[...].astype(F32), g_ref[...]
        rstd = lax.rsqrt(jnp.mean(xv * xv, axis=-1, keepdims=True) + EPS)
        xhat = xv * rstd
        _acc_rows(dsh_ref, dh_v, first)
        _acc_rows(dsc_ref, dh_v * (xhat * gv), first)
        dyg = dh_v * (1.0 + sc_ref[...])
        _acc_rows(dg_ref, dyg * xhat, first)
        dxhat = dyg * gv
        dx = dres_ref[...] + rstd * (dxhat - xhat * jnp.mean(dxhat * xhat, axis=-1, keepdims=True))
        dx_ref[...] = dx
        if tail:
            _gate_bwd_tail(dx, y_ref, gate_ref, rest[4], rest[5], first)

    vec = jax.ShapeDtypeStruct((1, D), F32)
    return pl.pallas_call(
        body, name=name, grid=(S // tr,),
        out_shape=(jax.ShapeDtypeStruct((S, D), F32), vec, vec, vec) + ((jax.ShapeDtypeStruct((S, D), _MXU_DTYPE), vec) if tail else ()),
        in_specs=[_row_spec(tr, D), _vec_spec(D), _vec_spec(D), _row_spec(tr, D), _row_spec(tr, D), ANY]
        + ([_row_spec(tr, D), _vec_spec(D)] if tail else []),
        out_specs=(_row_spec(tr, D), _vec_spec(D), _vec_spec(D), _vec_spec(D)) + ((_row_spec(tr, D), _vec_spec(D)) if tail else ()),
        compiler_params=_cp("arbitrary"))(x, g, sc, dh, dres, after, *((y, gate) if tail else ()))


def _loss_head(x, g, target, y, gate, tr=512):
    S, D = x.shape

    def body(x_ref, g_ref, t_ref, y_ref, gate_ref, loss_ref, dx_ref, dg_ref, dy_ref, dgate_ref):
        first = pl.program_id(0) == 0
        xv, gv = x_ref[...], g_ref[...]
        rstd = lax.rsqrt(jnp.mean(xv * xv, axis=-1, keepdims=True) + EPS)
        xhat = xv * rstd
        err = xhat * gv - t_ref[...]
        part = 0.5 * jnp.sum(jnp.mean(err * err, axis=-1, keepdims=True), axis=0, keepdims=True)

        @pl.when(first)
        def _():
            loss_ref[...] = part

        @pl.when(jnp.logical_not(first))
        def _():
            loss_ref[...] += part

        dout = err * (1.0 / D)
        _acc_rows(dg_ref, dout * xhat, first)
        dxhat = dout * gv
        dx = rstd * (dxhat - xhat * jnp.mean(dxhat * xhat, axis=-1, keepdims=True))
        dx_ref[...] = dx
        _gate_bwd_tail(dx, y_ref, gate_ref, dy_ref, dgate_ref, first)

    vec = jax.ShapeDtypeStruct((1, D), F32)
    return pl.pallas_call(
        body, name="loss_head", grid=(S // tr,),
        out_shape=(jax.ShapeDtypeStruct((1, 1), F32), jax.ShapeDtypeStruct((S, D), F32), vec,
                   jax.ShapeDtypeStruct((S, D), _MXU_DTYPE), vec),
        in_specs=[_row_spec(tr, D), _vec_spec(D), _row_spec(tr, D), _row_spec(tr, D), _vec_spec(D)],
        out_specs=(pl.BlockSpec((1, 1), lambda i: (0, 0)), _row_spec(tr, D), _vec_spec(D), _row_spec(tr, D), _vec_spec(D)),
        compiler_params=_cp("arbitrary"))(x, g, target, y, gate)


def _shift_down(v, k):
    t = lax.broadcasted_iota(jnp.int32, v.shape, 0)
    return jnp.where(t >= k, pltpu.roll(v, k, axis=0), 0.0)


def _shift_up(v, k):
    n = v.shape[0]
    t = lax.broadcasted_iota(jnp.int32, v.shape, 0)
    return jnp.where(t < n - k, pltpu.roll(v, n - k, axis=0), 0.0)


def _window_sum(p, w, shift):
    s, k = p, 1
    while k < w:
        s = s + shift(s, k)
        k *= 2
    return s


def _pool_count(shape, w):
    t = lax.broadcasted_iota(jnp.int32, shape, 0)
    return jnp.minimum(t + 1, w).astype(F32)


def _ab_specs(S):
    zs = [pl.BlockSpec((None, S, 128), functools.partial(lambda g, q: (2 * q + g // 2, 0, g % 2), q=q)) for q in range(4)]
    return zs


def _ab_mix_fwd(z8, conv_w, mix_w, scale):
    S = z8.shape[1]

    def body(b_ref, c_ref, a_ref, p_ref, w_ref, mix_ref, sc_ref, y_ref):
        g = pl.program_id(0)
        cg = c_ref[...].astype(F32) * a_ref[...].astype(F32)
        w = w_ref[...]
        conv = w[0:1] * _shift_down(cg, 2) + w[1:2] * _shift_down(cg, 1) + w[2:3] * cg
        y_ref[0] = (b_ref[...].astype(F32) * conv).astype(y_ref.dtype)
        for gg, win in enumerate(POOL_WINDOWS):
            @pl.when(g == gg)
            def _(win=win):
                p = p_ref[...].astype(F32)
                pooled = _window_sum(p, win, _shift_down) / _pool_count(p.shape, win) - p
                y_ref[1] = (_dot(pooled, mix_ref[...], "nn") * sc_ref[...]).astype(y_ref.dtype)

    return pl.pallas_call(
        body, name="ab_mix_fwd", grid=(4,), out_shape=jax.ShapeDtypeStruct((2, S, 512), _MXU_DTYPE),
        in_specs=_ab_specs(S) + [pl.BlockSpec((3, 128), lambda g: (0, g)), pl.BlockSpec((None, 128, 128), lambda g: (g, 0, 0)),
                                 pl.BlockSpec((1, 128), lambda g: (0, g))],
        out_specs=pl.BlockSpec((2, S, 128), lambda g: (0, 0, g)), compiler_params=_cp("parallel"))(z8, z8, z8, z8, conv_w, mix_w, scale)


def _ab_mix_bwd(z8, dycat2, conv_w, mix_w, scale, after):
    S = z8.shape[1]

    def body(b_ref, c_ref, a_ref, p_ref, dy_ref, w_ref, mix_ref, sc_ref, after_ref, dz_ref, dw_ref, dmix_ref, dsc_ref):
        g = pl.program_id(0)
        bv, cv, av, w = b_ref[...].astype(F32), c_ref[...].astype(F32), a_ref[...].astype(F32), w_ref[...]
        dya = dy_ref[0]
        cg = cv * av
        cg1, cg2 = _shift_down(cg, 1), _shift_down(cg, 2)
        conv = w[0:1] * cg2 + w[1:2] * cg1 + w[2:3] * cg
        dz_ref[0] = (dya * conv).astype(dz_ref.dtype)
        dconv = dya * bv
        dcg = w[2:3] * dconv + w[1:2] * _shift_up(dconv, 1) + w[0:1] * _shift_up(dconv, 2)
        dz_ref[1] = (dcg * av).astype(dz_ref.dtype)
        dz_ref[2] = (dcg * cv).astype(dz_ref.dtype)
        dw_ref[0:1, :] = jnp.sum(dconv * cg2, axis=0, keepdims=True)
        dw_ref[1:2, :] = jnp.sum(dconv * cg1, axis=0, keepdims=True)
        dw_ref[2:3, :] = jnp.sum(dconv * cg, axis=0, keepdims=True)
        for gg, win in enumerate(POOL_WINDOWS):
            @pl.when(g == gg)
            def _(win=win):
                p, dyb, mix = p_ref[...].astype(F32), dy_ref[1], mix_ref[...]
                cnt = _pool_count(p.shape, win)
                pooled = _window_sum(p, win, _shift_down) / cnt - p
                dsc_ref[...] = jnp.sum(dyb * _dot(pooled, mix, "nn"), axis=0, keepdims=True)
                dmixed = dyb * sc_ref[...]
                dmix_ref[...] = _dot(pooled, dmixed, "tn")
                dpooled = _dot(dmixed, mix, "nt")
                dz_ref[3] = (_window_sum(dpooled / cnt, win, _shift_up) - dpooled).astype(dz_ref.dtype)

    return pl.pallas_call(
        body, name="ab_mix_bwd", grid=(4,),
        out_shape=(jax.ShapeDtypeStruct((4, 2, S, 256), _MXU_DTYPE), jax.ShapeDtypeStruct((3, 512), F32),
                   jax.ShapeDtypeStruct((4, 128, 128), F32), jax.ShapeDtypeStruct((1, 512), F32)),
        in_specs=_ab_specs(S) + [pl.BlockSpec((2, S, 128), lambda g: (0, 0, g)), pl.BlockSpec((3, 128), lambda g: (0, g)),
                                 pl.BlockSpec((None, 128, 128), lambda g: (g, 0, 0)), pl.BlockSpec((1, 128), lambda g: (0, g)), ANY],
        out_specs=(pl.BlockSpec((4, None, S, 128), lambda g: (0, g // 2, 0, g % 2)), pl.BlockSpec((3, 128), lambda g: (0, g)),
                   pl.BlockSpec((None, 128, 128), lambda g: (g, 0, 0)), pl.BlockSpec((1, 128), lambda g: (0, g))),
        compiler_params=_cp("parallel"))(z8, z8, z8, z8, dycat2, conv_w, mix_w, scale, after)


HALO = 16


def _ffn_specs(S, n, tr):
    nb = S // HALO
    tile = pl.BlockSpec((2, None, tr, n), lambda j, i: (0, j, i, 0))
    prev = pl.BlockSpec((2, None, HALO, n), lambda j, i: (0, j, jnp.maximum(i * (tr // HALO) - 1, 0), 0))
    nxt = pl.BlockSpec((2, None, HALO, n), lambda j, i: (0, j, jnp.minimum((i + 1) * (tr // HALO), nb - 1), 0))
    cw = pl.BlockSpec((2, None, 3, n), lambda j, i: (0, j, 0, 0))
    return tile, prev, nxt, cw


def _shifted_rows(ext, lo, rows):
    ext = ext.astype(F32)
    return pltpu.roll(ext, 1, axis=0)[lo:lo + rows], pltpu.roll(ext, 2, axis=0)[lo:lo + rows]


def _ffn_gate_fwd(name, u24, cw24, tr=256):
    _, J, S, n = u24.shape
    tile, prev, _, cw = _ffn_specs(S, n, tr)

    def body(u_ref, up_ref, w_ref, a_ref, z_ref):
        keep = (pl.program_id(1) > 0).astype(u_ref.dtype)
        z = []
        for h in range(2):
            ext = jnp.concatenate([up_ref[h] * keep, u_ref[h]], axis=0)
            x1, x2 = _shifted_rows(ext, HALO, tr)
            w = w_ref[h]
            z.append(w[0:1] * x2 + w[1:2] * x1 + w[2:3] * u_ref[h].astype(F32))
        zg, zu = z
        sg = jax.nn.sigmoid(zg)
        silu = zg * sg
        a_ref[...] = (silu * zu).astype(a_ref.dtype)
        z_ref[0] = (zu * (sg * (1.0 + zg * (1.0 - sg)))).astype(z_ref.dtype)
        z_ref[1] = silu.astype(z_ref.dtype)

    return pl.pallas_call(
        body, name=name, grid=(J, S // tr),
        out_shape=(jax.ShapeDtypeStruct((J, S, n), _MXU_DTYPE), jax.ShapeDtypeStruct((2, J, S, n), ACT_DTYPE)),
        in_specs=[tile, prev, cw], out_specs=(pl.BlockSpec((None, tr, n), lambda j, i: (j, i, 0)), tile),
        compiler_params=_cp("parallel", "parallel"))(u24, u24, cw24)


def _ffn_gate_bwd(name, u24, z24, cw24, da4, w_up24, after, tr=256):
    _, J, S, n = u24.shape
    K = w_up24.shape[2]
    nb = S // HALO
    tile = pl.BlockSpec((2, None, tr, n), lambda i, j: (0, j, i, 0))
    nxt = pl.BlockSpec((2, None, HALO, n), lambda i, j: (0, j, jnp.minimum((i + 1) * (tr // HALO), nb - 1), 0))
    whole = lambda shape: pl.BlockSpec(shape, lambda i, j: (0,) * len(shape))

    def body(u_ref, z_ref, zn_ref, cw_ref, da_ref, dan_ref, wup_ref, after_ref, du_ref, dcw_ref, dh_ref, acc_ref):
        i, j = pl.program_id(0), pl.program_id(1)
        first = i == 0
        keep_next = (i < S // tr - 1).astype(F32)
        w = [cw_ref[h, j] for h in range(2)]
        m = tr + HALO
        da = jnp.concatenate([da_ref[...].astype(F32), dan_ref[...].astype(F32) * keep_next], axis=0)
        dz = [da * jnp.concatenate([z_ref[h], zn_ref[h]], axis=0).astype(F32) for h in range(2)]
        dh = None
        for h in range(2):
            d = dz[h]
            d0, d1, d2 = d[:tr], pltpu.roll(d, m - 1, axis=0)[:tr], pltpu.roll(d, m - 2, axis=0)[:tr]
            du = (w[h][2:3] * d0 + w[h][1:2] * d1 + w[h][0:1] * d2).astype(du_ref.dtype)
            du_ref[h] = du
            part = _dot(du, wup_ref[h, j], "nt")
            dh = part if dh is None else dh + part
            x0 = u_ref[h].astype(F32)
            parts = [jnp.sum(x0 * dk, axis=0, keepdims=True) for dk in (d2, d1, d0)]
            for k in range(3):
                @pl.when(first)
                def _(k=k, h=h):
                    dcw_ref[h, j, k:k + 1, :] = parts[k]

                @pl.when(jnp.logical_not(first))
                def _(k=k, h=h):
                    dcw_ref[h, j, k:k + 1, :] += parts[k]

        @pl.when(j == 0)
        def _():
            acc_ref[...] = dh

        @pl.when(j > 0)
        def _():
            acc_ref[...] += dh

        @pl.when(j == J - 1)
        def _():
            dh_ref[...] = acc_ref[...].astype(dh_ref.dtype)

    da_tile = pl.BlockSpec((None, tr, n), lambda i, j: (j, i, 0))
    da_next = pl.BlockSpec((None, HALO, n), lambda i, j: (j, jnp.minimum((i + 1) * (tr // HALO), nb - 1), 0))
    return pl.pallas_call(
        body, name=name, grid=(S // tr, J),
        out_shape=(jax.ShapeDtypeStruct((2, J, S, n), _MXU_DTYPE), jax.ShapeDtypeStruct((2, J, 3, n), F32),
                   jax.ShapeDtypeStruct((S, K), ACT_DTYPE)),
        in_specs=[tile, tile, nxt, whole((2, J, 3, n)), da_tile, da_next, whole((2, J, K, n)), ANY],
        out_specs=(tile, whole((2, J, 3, n)), pl.BlockSpec((tr, K), lambda i, j: (i, 0))),
        scratch_shapes=[pltpu.VMEM((tr, K), F32)],
        compiler_params=_cp("arbitrary", "arbitrary"))(u24, z24, z24, cw24, da4, da4, w_up24, after)


def _rms_rows(v, g):
    rstd = lax.rsqrt(jnp.mean(v * v, axis=-1, keepdims=True) + EPS)
    return v * rstd * g


def _rms_rows_bwd(v, g, dy):
    rstd = lax.rsqrt(jnp.mean(v * v, axis=-1, keepdims=True) + EPS)
    vhat = v * rstd
    dvhat = dy * g
    return rstd * (dvhat - vhat * jnp.mean(dvhat * vhat, axis=-1, keepdims=True)), dy * vhat


def _rope(v, cos, sa, sb):
    return v * cos + pltpu.roll(v, 112, axis=1) * sa + pltpu.roll(v, 16, axis=1) * sb


def _rope_t(d, cos, sa, sb):
    return d * cos + pltpu.roll(d * sa, 16, axis=1) + pltpu.roll(d * sb, 112, axis=1)


def _qkv_rope_fwd(z, qg, kvg, w_uq_t, w_kv, cosq, cosk, sa, sb, tr=512):
    S = z.shape[0]

    def body(ql_ref, kvl_ref, kpe_ref, qg_ref, kvg_ref, wq_ref, wkv_ref, cq_ref, ck_ref, sa_ref, sb_ref,
             qn_ref, kvn_ref, qo_ref, ko_ref, vo_ref):
        cq, ck, sa_v, sb_v = cq_ref[...], ck_ref[...], sa_ref[...], sb_ref[...]
        qn = _rms_rows(ql_ref[...], qg_ref[...]).astype(qn_ref.dtype)
        kvn = _rms_rows(kvl_ref[...], kvg_ref[...]).astype(kvn_ref.dtype)
        qn_ref[...] = qn
        kvn_ref[...] = kvn
        q = _dot(qn, wq_ref[...], "nt")
        kv = _dot(kvn, wkv_ref[...], "nn")
        kpe = _rope(kpe_ref[...], ck, sa_v, sb_v)
        for h in range(8):
            cols = slice(128 * h, 128 * h + 128)
            qo_ref[:, cols] = _rope(q[:, cols], cq, sa_v, sb_v).astype(qo_ref.dtype)
            ko_ref[:, cols] = (kv[:, cols] + kpe).astype(ko_ref.dtype)
        vo_ref[...] = kv[:, 1024:1536].astype(vo_ref.dtype)

    tab = _row_spec(tr, 128)
    whole = lambda a: pl.BlockSpec(a.shape, lambda i: (0, 0))
    return pl.pallas_call(
        body, name="qkv_rope_fwd", grid=(S // tr,),
        out_shape=(jax.ShapeDtypeStruct((S, 256), _MXU_DTYPE), jax.ShapeDtypeStruct((S, 128), _MXU_DTYPE),
                   jax.ShapeDtypeStruct((S, 1024), _MXU_DTYPE), jax.ShapeDtypeStruct((S, 1024), _MXU_DTYPE),
                   jax.ShapeDtypeStruct((S, 512), _MXU_DTYPE)),
        in_specs=[pl.BlockSpec((tr, 256), lambda i: (i, 0)), pl.BlockSpec((tr, 128), lambda i: (i, 2)),
                  pl.BlockSpec((tr, 128), lambda i: (i, 3)), _vec_spec(256), _vec_spec(128), whole(w_uq_t), whole(w_kv),
                  tab, tab, tab, tab],
        out_specs=(_row_spec(tr, 256), _row_spec(tr, 128), _row_spec(tr, 1024), _row_spec(tr, 1024), _row_spec(tr, 512)),
        compiler_params=_cp("parallel"))(z, z, z, qg, kvg, w_uq_t, w_kv, cosq, cosk, sa, sb)


def _attn_bwd_prep(o, dycat2, tr=512):
    S = o.shape[0]

    def body(o_ref, do_ref, delta_ref, doa_ref, dob_ref):
        do = do_ref[...]
        prod = do * o_ref[...]
        lane = lax.broadcasted_iota(jnp.int32, do.shape, 1)
        for p in range(4):
            cols = slice(128 * p, 128 * p + 128)
            first = lane[:, cols] < 128 * p + 64
            da = jnp.sum(jnp.where(first, prod[:, cols], 0.0), axis=-1, keepdims=True)
            db = jnp.sum(jnp.where(first, 0.0, prod[:, cols]), axis=-1, keepdims=True)
            delta_ref[p] = jnp.where(first, da, db)
            doa_ref[p] = jnp.where(first, do[:, cols], 0.0).astype(doa_ref.dtype)
            dob_ref[p] = jnp.where(first, 0.0, do[:, cols]).astype(dob_ref.dtype)

    pair = pl.BlockSpec((4, tr, 128), lambda i: (0, i, 0))
    return pl.pallas_call(
        body, name="attn_bwd_prep", grid=(S // tr,),
        out_shape=(jax.ShapeDtypeStruct((4, S, 128), F32), jax.ShapeDtypeStruct((4, S, 128), _MXU_DTYPE),
                   jax.ShapeDtypeStruct((4, S, 128), _MXU_DTYPE)),
        in_specs=[_row_spec(tr, 512), pl.BlockSpec((None, tr, 512), lambda i: (0, i, 0))],
        out_specs=(pair, pair, pair), compiler_params=_cp("parallel"))(o, dycat2)


def _qkv_rope_bwd(z, qg, kvg, dq, dk, dv, duv, w_uq_t, w_kv, cosq, cosk, sa, sb, tr=512):
    S = z.shape[0]

    def body(ql_ref, kvl_ref, qg_ref, kvg_ref, dq_ref, dk_ref, dv_ref, duv_ref, wq_ref, wkv_ref, cq_ref, ck_ref, sa_ref, sb_ref,
             dqo_ref, dkv_ref, dz_ref, dqg_ref, dkvg_ref):
        first = pl.program_id(0) == 0
        cq, ck, sa_v, sb_v = cq_ref[...], ck_ref[...], sa_ref[...], sb_ref[...]
        tot = jnp.zeros((tr, 128), F32)
        for h in range(8):
            cols = slice(128 * h, 128 * h + 128)
            dqo_ref[:, cols] = _rope_t(dq_ref[:, cols], cq, sa_v, sb_v).astype(dqo_ref.dtype)
            dkh = dk_ref[:, cols]
            tot = tot + dkh
            dkv_ref[:, cols] = dkh.astype(dkv_ref.dtype)
        dkv_ref[:, 1024:1536] = dv_ref[...].astype(dkv_ref.dtype)
        dqn = _dot(dqo_ref[...], wq_ref[...], "nn")
        dkvn = _dot(dkv_ref[...], wkv_ref[...], "nt")
        dql, dqg = _rms_rows_bwd(ql_ref[...], qg_ref[...], dqn)
        dkvl, dkvg = _rms_rows_bwd(kvl_ref[...], kvg_ref[...], dkvn)
        _acc_rows(dqg_ref, dqg, first)
        _acc_rows(dkvg_ref, dkvg, first)
        dz_ref[:, 0:256] = dql.astype(dz_ref.dtype)
        dz_ref[:, 256:384] = dkvl.astype(dz_ref.dtype)
        dz_ref[:, 384:512] = _rope_t(tot, ck, sa_v, sb_v).astype(dz_ref.dtype)
        dz_ref[:, 512:1536] = duv_ref[...].astype(dz_ref.dtype)

    tab = _row_spec(tr, 128)
    whole = lambda a: pl.BlockSpec(a.shape, lambda i: (0, 0))
    return pl.pallas_call(
        body, name="qkv_rope_bwd", grid=(S // tr,),
        out_shape=(jax.ShapeDtypeStruct((S, 1024), _MXU_DTYPE), jax.ShapeDtypeStruct((S, 1536), _MXU_DTYPE),
                   jax.ShapeDtypeStruct((S, 1536), _MXU_DTYPE), jax.ShapeDtypeStruct((1, 256), F32), jax.ShapeDtypeStruct((1, 128), F32)),
        in_specs=[pl.BlockSpec((tr, 256), lambda i: (i, 0)), pl.BlockSpec((tr, 128), lambda i: (i, 2)), _vec_spec(256), _vec_spec(128),
                  _row_spec(tr, 1024), _row_spec(tr, 1024), _row_spec(tr, 512), _row_spec(tr, 1024), whole(w_uq_t), whole(w_kv),
                  tab, tab, tab, tab],
        out_specs=(_row_spec(tr, 1024), _row_spec(tr, 1536), _row_spec(tr, 1536), _vec_spec(256), _vec_spec(128)),
        compiler_params=_cp("arbitrary"))(z, z, qg, kvg, dq, dk, dv, duv, w_uq_t, w_kv, cosq, cosk, sa, sb)


NEG = -1e30


def _attn_fwd(q, k, v, tq=1024, tk=1024):
    S = q.shape[0]
    assert tq == tk

    def body(q_ref, k_ref, v_ref, o_ref, lse_ref):
        i = pl.program_id(1)
        qs = [q_ref[:, 0:128], q_ref[:, 128:256]]

        def step(kb, carry, diagonal=False):
            start = pl.multiple_of(kb * tk, tk)
            vv = v_ref[pl.ds(start, tk), :]
            out = []
            for h in range(2):
                m, l, acc = carry[3 * h:3 * h + 3]
                s = _dot(qs[h], k_ref[pl.ds(start, tk), 128 * h:128 * h + 128], "nt") * ATTN_SCALE
                if diagonal:
                    s = jnp.where(below, s, NEG)
                m_new = jnp.maximum(m, jnp.max(s, axis=-1, keepdims=True))
                alpha = jnp.exp(m - m_new)
                p = jnp.exp(s - m_new)
                out += [m_new, alpha * l + jnp.sum(p, axis=-1, keepdims=True), alpha * acc + _dot(p, vv, "nn")]
            return tuple(out)

        below = lax.broadcasted_iota(jnp.int32, (tq, tk), 1) <= lax.broadcasted_iota(jnp.int32, (tq, tk), 0)
        init = (jnp.full((tq, 1), NEG, F32), jnp.zeros((tq, 1), F32), jnp.zeros((tq, 128), F32)) * 2
        ma, la, acca, mb, lb, accb = step(i, lax.fori_loop(0, i, step, init), diagonal=True)
        lane = lax.broadcasted_iota(jnp.int32, (tq, 128), 1)
        o_ref[...] = jnp.where(lane < 64, acca / la, accb / lb)
        lse_ref[...] = jnp.where(lane < 64, ma + jnp.log(la), mb + jnp.log(lb))

    return pl.pallas_call(
        body, name="attn_fwd", grid=(4, S // tq),
        out_shape=(jax.ShapeDtypeStruct((S, 512), F32), jax.ShapeDtypeStruct((4, S, 128), F32)),
        in_specs=[pl.BlockSpec((tq, 256), lambda p, i: (i, p)), pl.BlockSpec((S, 256), lambda p, i: (0, p)),
                  pl.BlockSpec((S, 128), lambda p, i: (0, p))],
        out_specs=(pl.BlockSpec((tq, 128), lambda p, i: (i, p)), pl.BlockSpec((None, tq, 128), lambda p, i: (p, i, 0))),
        compiler_params=_cp("parallel", "parallel"))(q, k, v)


def _attn_bwd(q, k, v, lse, delta, doa, dob, tq=512, tk=512):
    S = q.shape[0]
    assert tq == tk

    def body(q_ref, k_ref, v_ref, lse_ref, delta_ref, doa_ref, dob_ref, dq_ref, dk_ref, dv_ref):
        j = pl.program_id(1)

        @pl.when(j == 0)
        def _():
            dq_ref[...] = jnp.zeros_like(dq_ref)

        below = lax.broadcasted_iota(jnp.int32, (tq, tk), 1) <= lax.broadcasted_iota(jnp.int32, (tq, tk), 0)
        ks = [k_ref[:, 0:128], k_ref[:, 128:256]]
        vv = v_ref[...]

        def step(qb, carry, diagonal=False):
            dka, dkb, dvp = carry
            start = pl.multiple_of(qb * tq, tq)
            rows = pl.ds(start, tq)
            lse_v, delta_v = lse_ref[rows, :], delta_ref[rows, :]
            dos = [doa_ref[rows, :], dob_ref[rows, :]]
            dks = [dka, dkb]
            for h in range(2):
                delta = delta_v[:, 64 * h:64 * h + 1]
                do_h = dos[h]
                qh = q_ref[rows, 128 * h:128 * h + 128]
                s = _dot(qh, ks[h], "nt") * ATTN_SCALE
                p = jnp.exp(s - lse_v[:, 64 * h:64 * h + 1])
                if diagonal:
                    p = jnp.where(below, p, 0.0)
                dvp = dvp + _dot(p, do_h, "tn")
                ds = p * (_dot(do_h, vv, "nt") - delta) * ATTN_SCALE
                dq_ref[rows, 128 * h:128 * h + 128] += _dot(ds, ks[h], "nn")
                dks[h] = dks[h] + _dot(ds, qh, "tn")
            return dks[0], dks[1], dvp

        zero = jnp.zeros((tk, 128), F32)
        dka, dkb, dvp = lax.fori_loop(j + 1, S // tq, step, step(j, (zero, zero, zero), diagonal=True))
        dk_ref[:, 0:128] = dka
        dk_ref[:, 128:256] = dkb
        dv_ref[...] = dvp

    return pl.pallas_call(
        body, name="attn_bwd", grid=(4, S // tk),
        out_shape=(jax.ShapeDtypeStruct((S, 1024), F32), jax.ShapeDtypeStruct((S, 1024), F32), jax.ShapeDtypeStruct((S, 512), F32)),
        in_specs=[pl.BlockSpec((S, 256), lambda p, j: (0, p)), pl.BlockSpec((tk, 256), lambda p, j: (j, p)),
                  pl.BlockSpec((tk, 128), lambda p, j: (j, p))] + [pl.BlockSpec((None, S, 128), lambda p, j: (p, 0, 0))] * 4,
        out_specs=(pl.BlockSpec((S, 256), lambda p, j: (0, p)), pl.BlockSpec((tk, 256), lambda p, j: (j, p)),
                   pl.BlockSpec((tk, 128), lambda p, j: (j, p))),
        compiler_params=_cp("parallel", "arbitrary"))(q, k, v, lse, delta, doa, dob)


CHUNK = 128
GELU_C = math.sqrt(2.0 / math.pi)


def _gelu(v):
    t = jnp.tanh(GELU_C * (v + 0.044715 * (v * v * v)))
    return v * (0.5 * (1.0 + t)), t


def _gelu_grad(v, t):
    return 0.5 * (1.0 + t) + v * (0.5 * (1.0 - t * t) * GELU_C * (1.0 + 3.0 * 0.044715 * v * v))


def _tril(w):
    r = lax.broadcasted_iota(jnp.int32, w.shape, 0)
    c = lax.broadcasted_iota(jnp.int32, w.shape, 1)
    return jnp.where(c <= r, w, 0.0)


def _layer_norm(v, g, b):
    xc = v - jnp.mean(v, axis=-1, keepdims=True)
    rstd = lax.rsqrt(jnp.mean(xc * xc, axis=-1, keepdims=True) + EPS)
    xhat = xc * rstd
    return xhat * g + b, xhat, rstd


def _sgu_fwd(z, o, ln_g, ln_b, w_s, b_st, tr=512):
    S = z.shape[0]

    def body(u_ref, v_ref, o_ref, g_ref, b_ref, ws_ref, bs_ref, y_ref):
        gu, _ = _gelu(u_ref[...])
        gv, _ = _gelu(v_ref[...])
        vln, _, _ = _layer_norm(gv, g_ref[...], b_ref[...])
        y_ref[0] = o_ref[...].astype(y_ref.dtype)
        for g in range(4):
            wt = _tril(ws_ref[g])
            cols = slice(128 * g, 128 * g + 128)
            for ch in range(tr // CHUNK):
                rows = slice(CHUNK * ch, CHUNK * ch + CHUNK)
                mixed = _dot(wt, vln[rows, cols], "nn") + bs_ref[:, g:g + 1]
                y_ref[1, rows, cols] = (gu[rows, cols] * mixed).astype(y_ref.dtype)

    return pl.pallas_call(
        body, name="sgu_fwd", grid=(S // tr,), out_shape=jax.ShapeDtypeStruct((2, S, 512), _MXU_DTYPE),
        in_specs=[pl.BlockSpec((tr, 512), lambda i: (i, 1)), pl.BlockSpec((tr, 512), lambda i: (i, 2)), _row_spec(tr, 512),
                  _vec_spec(512), _vec_spec(512), pl.BlockSpec((4, 128, 128), lambda i: (0, 0, 0)), pl.BlockSpec((128, 4), lambda i: (0, 0))],
        out_specs=pl.BlockSpec((2, tr, 512), lambda i: (0, i, 0)), compiler_params=_cp("parallel"))(z, z, o, ln_g, ln_b, w_s, b_st)


def _sgu_bwd(z, dycat2, ln_g, ln_b, w_s, b_st, tr=512):
    S = z.shape[0]

    def body(u_ref, v_ref, dy_ref, g_ref, b_ref, ws_ref, bs_ref, duv_ref, dg_ref, db_ref, dws_ref, dbs_ref):
        first = pl.program_id(0) == 0
        u_pre, v_pre = u_ref[...], v_ref[...]
        gu, tu = _gelu(u_pre)
        gv, tv = _gelu(v_pre)
        gain = g_ref[...]
        vln, xhat, rstd = _layer_norm(gv, gain, b_ref[...])

        @pl.when(first)
        def _():
            dws_ref[...] = jnp.zeros_like(dws_ref)
            dbs_ref[...] = jnp.zeros_like(dbs_ref)

        dvln_cols = []
        for g in range(4):
            wt = _tril(ws_ref[g])
            cols = slice(128 * g, 128 * g + 128)
            dmixed_sum = jnp.zeros((CHUNK, 128), F32)
            dw = jnp.zeros((CHUNK, CHUNK), F32)
            dvln_rows = []
            for ch in range(tr // CHUNK):
                rows = slice(CHUNK * ch, CHUNK * ch + CHUNK)
                vt = vln[rows, cols]
                mixed = _dot(wt, vt, "nn") + bs_ref[:, g:g + 1]
                dyd = dy_ref[rows, cols]
                duv_ref[rows, cols] = (dyd * mixed * _gelu_grad(u_pre[rows, cols], tu[rows, cols])).astype(duv_ref.dtype)
                dmixed = dyd * gu[rows, cols]
                dmixed_sum = dmixed_sum + dmixed
                dw = dw + _dot(dmixed, vt, "nt")
                dvln_rows.append(_dot(wt, dmixed, "tn"))
            dws_ref[g] += _tril(dw)
            dbs_ref[g:g + 1, :] += jnp.sum(dmixed_sum.T, axis=0, keepdims=True)
            dvln_cols.append(jnp.concatenate(dvln_rows, axis=0))
        dvln = jnp.concatenate(dvln_cols, axis=1)
        _acc_rows(dg_ref, dvln * xhat, first)
        _acc_rows(db_ref, dvln, first)
        dxhat = dvln * gain
        dgv = rstd * (dxhat - jnp.mean(dxhat, axis=-1, keepdims=True) - xhat * jnp.mean(dxhat * xhat, axis=-1, keepdims=True))
        duv_ref[:, 512:1024] = (dgv * _gelu_grad(v_pre, tv)).astype(duv_ref.dtype)

    return pl.pallas_call(
        body, name="sgu_bwd", grid=(S // tr,),
        out_shape=(jax.ShapeDtypeStruct((S, 1024), _MXU_DTYPE), jax.ShapeDtypeStruct((1, 512), F32), jax.ShapeDtypeStruct((1, 512), F32),
                   jax.ShapeDtypeStruct((4, 128, 128), F32), jax.ShapeDtypeStruct((4, 128), F32)),
        in_specs=[pl.BlockSpec((tr, 512), lambda i: (i, 1)), pl.BlockSpec((tr, 512), lambda i: (i, 2)),
                  pl.BlockSpec((None, tr, 512), lambda i: (1, i, 0)), _vec_spec(512), _vec_spec(512),
                  pl.BlockSpec((4, 128, 128), lambda i: (0, 0, 0)), pl.BlockSpec((128, 4), lambda i: (0, 0))],
        out_specs=(_row_spec(tr, 1024), _vec_spec(512), _vec_spec(512), pl.BlockSpec((4, 128, 128), lambda i: (0, 0, 0)),
                   pl.BlockSpec((4, 128), lambda i: (0, 0))),
        compiler_params=_cp("arbitrary"))(z, z, dycat2, ln_g, ln_b, w_s, b_st)


def _adamw_math(w, m, v, g):
    c1 = 1.0 / (1.0 - ADAM_B1 ** ADAM_STEP)
    c2 = 1.0 / (1.0 - ADAM_B2 ** ADAM_STEP)
    m2 = ADAM_B1 * m + (1.0 - ADAM_B1) * g
    v2 = ADAM_B2 * v + (1.0 - ADAM_B2) * (g * g)
    return -ADAM_LR * ((m2 * c1) / (jnp.sqrt(v2 * c2) + ADAM_EPS) + ADAM_WD * w), m2, v2


def _adamw_small(name, params, parts):
    n = len(params)

    def body(*refs):
        ins, outs = refs[:4 * n], refs[4 * n:]
        for i in range(n):
            w_ref, m_ref, v_ref, p_ref = ins[4 * i:4 * i + 4]
            g = p_ref[0].astype(F32)
            for k in range(1, N_DEV):
                g = g + p_ref[k].astype(F32)
            delta, m2, v2 = _adamw_math(w_ref[...], m_ref[...], v_ref[...], g)
            outs[4 * i][...] = g
            outs[4 * i + 1][...] = delta
            outs[4 * i + 2][...] = m2
            outs[4 * i + 3][...] = v2

    flat = [a for (w, m, v), p in zip(params, parts) for a in (w, m, v, p)]
    out = pl.pallas_call(
        body, name=name, out_shape=[jax.ShapeDtypeStruct(w.shape, F32) for (w, _, _) in params for _ in range(4)],
        compiler_params=pltpu.CompilerParams(vmem_limit_bytes=_VMEM_LIMIT))(*flat)
    return [out[4 * i:4 * i + 4] for i in range(n)]


ADAMW_BLOCK_BYTES = 36 * 2 ** 20


def _adamw(name, w, m, v, parts):
    L, R, C = w.shape
    P = parts[0].shape[0]
    row_bytes = 2 * C * (7 * 4 + P * parts[0].dtype.itemsize)
    tr = R
    if R * row_bytes > ADAMW_BLOCK_BYTES:
        tr = next(t for t in (1024, 512, 256, 128, 64, 32, 16) if R % t == 0 and t * row_bytes <= ADAMW_BLOCK_BYTES)
    nr = R // tr
    c1 = 1.0 / (1.0 - ADAM_B1 ** ADAM_STEP)
    c2 = 1.0 / (1.0 - ADAM_B2 ** ADAM_STEP)

    def body(w_ref, m_ref, v_ref, *rest):
        p_refs, (g_ref, d_ref, mo_ref, vo_ref) = rest[:L], rest[L:]
        for ll in range(L):
            @pl.when(pl.program_id(0) == ll)
            def _(p_ref=p_refs[ll]):
                g = p_ref[0].astype(F32)
                for k in range(1, P):
                    g = g + p_ref[k].astype(F32)
                m2 = ADAM_B1 * m_ref[...] + (1.0 - ADAM_B1) * g
                v2 = ADAM_B2 * v_ref[...] + (1.0 - ADAM_B2) * (g * g)
                g_ref[...] = g
                mo_ref[...] = m2
                vo_ref[...] = v2
                d_ref[...] = -ADAM_LR * ((m2 * c1) / (jnp.sqrt(v2 * c2) + ADAM_EPS) + ADAM_WD * w_ref[...])

    def part_spec(ll):
        return pl.BlockSpec((P, tr, C), lambda l, i: (0, jnp.where(l == ll, i, jnp.where(l < ll, 0, nr - 1)), 0))

    full = pl.BlockSpec((None, tr, C), lambda l, i: (l, i, 0))
    sds = jax.ShapeDtypeStruct((L, R, C), F32)
    return pl.pallas_call(
        body, name=name, grid=(L, nr), out_shape=(sds, sds, sds, sds),
        in_specs=[full] * 3 + [part_spec(ll) for ll in range(L)],
        out_specs=(full,) * 4, compiler_params=_cp("arbitrary", "arbitrary"))(w, m, v, *parts)


def _rope_tables(positions):
    half = 16
    inv_freq = 10000.0 ** (-jnp.arange(half, dtype=F32) / half)
    ang = positions.astype(F32)[:, None] * inv_freq
    cos, sin = jnp.cos(ang), jnp.sin(ang)
    S = positions.shape[0]
    z16, z32, z64 = jnp.zeros((S, 16), F32), jnp.zeros((S, 32), F32), jnp.zeros((S, 64), F32)
    cosk = jnp.concatenate([z64, cos, cos, z32], axis=1)
    cosq = jnp.concatenate([jnp.ones((S, 64), F32), cos, cos, z32], axis=1)
    sa = jnp.concatenate([z64, -sin, z16, z32], axis=1)
    sb = jnp.concatenate([z64, z16, sin, z32], axis=1)
    return cosq, cosk, sa, sb


def _ffn_fwd(l, x, mod, n2g, get_w_up8, cw24, get_w_down4):
    sh, sc, gate = mod
    h = _rmsmod_fwd(f"ffn{l}_norm", x, n2g, sc, sh, n2g)
    w_up8 = get_w_up8(h)
    u8 = _mm_cols(f"ffn{l}_up", h, w_up8, out_dtype=ACT_DTYPE, tm=2048)
    S, n = u8.shape[1], u8.shape[2]
    u24 = u8.reshape(2, 4, S, n)
    a4, z24 = _ffn_gate_fwd(f"ffn{l}_gate", u24, cw24)
    w_down4 = get_w_down4(a4)
    f, x_new = _mm_rows_resid(f"ffn{l}_down", a4, w_down4, x, gate)
    return x_new, (x, h, u24, a4, f, z24), w_up8, w_down4


def _ffn_bwd(l, dx, df, dgate, saved, mod, n2g, w_up8, cw24, w_down4, me, y_prev, gate_prev):
    sh, sc, gate = mod
    x, h, u24, a4, f, z24 = saved
    da4 = _mm_rows_dx(f"ffn{l}_down_dx", df, w_down4, out_dtype=ACT_DTYPE, tm=2048)
    dw_down4 = _mm_rows_dw(f"ffn{l}_down_dw", a4, df, out_dtype=WIRE_DTYPE, tn=1024)
    sent_down, token = _exchange_start(f"scatter_ffn{l}_down", [dw_down4.reshape(8, 352, dw_down4.shape[2])], True, dgate, me)
    du24, dcw24, dh = _ffn_gate_bwd(f"ffn{l}_act_bwd", u24, z24, cw24, da4, w_up8.reshape((2, 4) + w_up8.shape[1:]), token)
    du8 = du24.reshape((8,) + du24.shape[2:])
    dw_up8t = _mm_cols_dwt(f"ffn{l}_up_dw", h, du8, out_dtype=WIRE_DTYPE, tk=1024)
    sent_up, token = _exchange_start(f"scatter_ffn{l}_up", [dw_up8t], True, dcw24, me)
    dx_new, dn2g, dsc, dsh, dy_prev, dgate_prev = _rmsmod_bwd(f"ffn{l}_norm_bwd", x, n2g, sc, dh, dx, token, y_prev, gate_prev)
    return dx_new, dict(sent_up=sent_up, sent_down=sent_down, cw24=dcw24, n2g=dn2g, mod=(dsh, dsc, dgate)), dy_prev, dgate_prev


def kernel(x, c, positions, ada_w, ada_b, norm1_g, norm2_g, ab_w_in, a_conv_w, b_mix_w, b_scale, ab_w_out, cd_w_in, c_q_norm_g, c_w_uq, c_kv_norm_g, c_w_ukv, d_ln_g, d_ln_b, d_w_s, d_b_s, cd_w_out, ffn_w_up, ffn_conv_w, ffn_w_down, final_norm_g, loss_target, m_ada_w, m_ada_b, m_norm1_g, m_norm2_g, m_ab_w_in, m_a_conv_w, m_b_mix_w, m_b_scale, m_ab_w_out, m_cd_w_in, m_c_q_norm_g, m_c_w_uq, m_c_kv_norm_g, m_c_w_ukv, m_d_ln_g, m_d_ln_b, m_d_w_s, m_d_b_s, m_cd_w_out, m_ffn_w_up, m_ffn_conv_w, m_ffn_w_down, m_final_norm_g, v_ada_w, v_ada_b, v_norm1_g, v_norm2_g, v_ab_w_in, v_a_conv_w, v_b_mix_w, v_b_scale, v_ab_w_out, v_cd_w_in, v_c_q_norm_g, v_c_w_uq, v_c_kv_norm_g, v_c_w_ukv, v_d_ln_g, v_d_ln_b, v_d_w_s, v_d_b_s, v_cd_w_out, v_ffn_w_up, v_ffn_conv_w, v_ffn_w_down, v_final_norm_g):
    S, D = x.shape[1], x.shape[2]
    me = 4 * lax.axis_index("x") + 2 * lax.axis_index("y") + lax.axis_index("c")
    x0, target = x[0], loss_target[0]
    W = _MXU_DTYPE

    small_shapes = [(1024,), (3, 64), (32,), (64,), (64,), (2, 3, 704)]
    (g0,) = _exchange("gather_small", [[_pack([c, a_conv_w, c_q_norm_g, d_ln_g, d_ln_b, ffn_conv_w])]], scatter=False)
    c_all, aconv_s, qg_s, lng_s, lnb_s, fcw_s = _unpack(g0[:, 0], small_shapes, lead=(N_DEV,))
    conv_w = aconv_s.transpose(1, 0, 2).reshape(3, 512)
    qg, ln_g, ln_b = qg_s.reshape(1, 256), lng_s.reshape(1, 512), lnb_s.reshape(1, 512)
    cw24 = [fcw_s[:, l].reshape(2, 4, 3, 704) for l in range(2)]
    c16 = jnp.pad(c_all, ((0, 16 - N_DEV), (0, 0)))

    mod_cols = _ada_fwd(c16, ada_w)
    (g1,) = _exchange("scatter_mod", [[mod_cols[:, :N_DEV].transpose(1, 0, 2)]], scatter=True)
    mod_mine = g1[:, 0]
    mod = mod_mine.transpose(1, 0, 2).reshape(2, 6 * D) + ada_b
    mods = [[mod[l, k * D:(k + 1) * D].reshape(1, D) for k in range(6)] for l in range(2)]

    gw_ab, token = _hier_gather_start("gather_w_ab", [ab_w_in[0].astype(W), ab_w_out[0].astype(W)], mod, me)
    gw_up0, token = _hier_gather_start("gather_w_ffn0_up", [ffn_w_up[0].astype(W)], token, me)
    gw_rest, started = _exchange_start("gather_w_rest", [
        ffn_w_down[0].astype(W), cd_w_in[0].T.astype(W), c_w_uq[0].T.astype(W), c_w_ukv[0].astype(W), cd_w_out[0].astype(W),
        ffn_w_up[1].astype(W), ffn_w_down[1].astype(W)], False, token, me)

    cosq, cosk, sa, sb = _rope_tables(positions[0])
    n1g = [norm1_g[l].reshape(1, D) for l in range(2)]
    n2g = [norm2_g[l].reshape(1, D) for l in range(2)]
    mix_w, scale = b_mix_w[0], b_scale
    kvg = c_kv_norm_g
    w_s, b_st = d_w_s[0], d_b_s[0].T

    sh1, sc1, g1m = mods[0][:3]
    h_ab = _rmsmod_fwd("ab_norm", x0, n1g[0], sc1, sh1, started)
    w_abin8, w_about = _hier_gather_wait("wait_w_ab", _hier_gather_forward("forward_w_ab", gw_ab, h_ab), h_ab)
    w_about2 = w_about.reshape(2, 512, D)
    z8 = _mm_cols("ab_in", h_ab, w_abin8, out_dtype=ACT_DTYPE, tm=2048)
    ycat_ab = _ab_mix_fwd(z8, conv_w, mix_w, scale)
    y_ab, x1 = _mm_rows_resid("ab_out", ycat_ab, w_about2, x0, g1m, tm=1024)
    w_up8, w_down4 = [None, None], [None, None]
    gw_up0 = _hier_gather_forward("forward_w_ffn0_up", gw_up0, x1)
    x2, ffn0_saved, w_up8[0], w_down4[0] = _ffn_fwd(
        0, x1, mods[0][3:], n2g[0], lambda after: _hier_gather_wait("wait_w_ffn0_up", gw_up0, after)[0], cw24[0],
        lambda after: _exchange_wait("wait_w_ffn0_down", gw_rest, after, [0])[0].reshape(4, 704, D))

    w_cdin, w_uq, w_ukv, w_cdout = _exchange_wait("wait_w_cd", gw_rest, x2, [1, 2, 3, 4])
    w_cdout2 = w_cdout.reshape(2, 512, D)
    w_cd_t = w_cdin.reshape(1440, D)
    zr = lambda n: jnp.zeros((n, D), W)
    w_cd_pad = jnp.concatenate([w_cd_t[:384], zr(64), w_cd_t[384:416], zr(32), w_cd_t[416:]], axis=0)
    w_uq_pad = jnp.pad(w_uq, ((0, 0), (0, 32), (0, 0))).reshape(1024, 256)
    w_ukv_h = w_ukv.transpose(1, 0, 2)
    w_k_pad = jnp.pad(w_ukv_h[:, :, :64], ((0, 0), (0, 0), (0, 64))).reshape(128, 1024)
    w_kv_pad = jnp.concatenate([w_k_pad, w_ukv_h[:, :, 64:].reshape(128, 512)], axis=1)

    sh1, sc1, g1c = mods[1][:3]
    h_cd = _rmsmod_fwd("cd_norm", x2, n1g[1], sc1, sh1, n1g[1])
    z_cd = _mm_nt("cd_in", h_cd, w_cd_pad, tm=1024, tn=1536)
    qn, kvn, q_r, k_r, v_r = _qkv_rope_fwd(z_cd, qg, kvg, w_uq_pad, w_kv_pad, cosq, cosk, sa, sb)
    o, lse = _attn_fwd(q_r, k_r, v_r)
    ycat_cd = _sgu_fwd(z_cd, o, ln_g, ln_b, w_s, b_st)
    y_cd, x3 = _mm_rows_resid("cd_out", ycat_cd, w_cdout2, x2, g1c, tm=1024)
    x4, ffn1_saved, w_up8[1], w_down4[1] = _ffn_fwd(
        1, x3, mods[1][3:], n2g[1], lambda after: _exchange_wait("wait_w_ffn1_up", gw_rest, after, [5])[0], cw24[1],
        lambda after: _exchange_wait("wait_w_ffn1_down", gw_rest, after, [6])[0].reshape(4, 704, D))

    loss_local, dx4, dfg, df1, dgate1 = _loss_head(x4, final_norm_g.reshape(1, D), target, ffn1_saved[4], mods[1][5])

    dx3, gf1, dy, dg1c = _ffn_bwd(1, dx4, df1, dgate1, ffn1_saved, mods[1][3:], n2g[1], w_up8[1], cw24[1], w_down4[1], me, y_cd, g1c)

    dycat = _mm_rows_dx("cd_out_dx", dy, w_cdout2, tm=2048)
    dw_cdout = _mm_rows_dw("cd_out_dw", ycat_cd, dy, out_dtype=WIRE_DTYPE, tn=1024)
    duv, dln_g, dln_b, dws, dbs = _sgu_bwd(z_cd, dycat, ln_g, ln_b, w_s, b_st)
    dq_r, dk_r, dv_r = _attn_bwd(q_r, k_r, v_r, lse, *_attn_bwd_prep(o, dycat))
    dqraw, dkvall, dz_cd, dqg, dkvg = _qkv_rope_bwd(z_cd, qg, kvg, dq_r, dk_r, dv_r, duv, w_uq_pad, w_kv_pad, cosq, cosk, sa, sb)
    dw_uq_pad = _mm_tn("cd_uq_dw", dqraw, qn, tn=256)
    dw_kv_pad = _mm_tn("cd_ukv_dw", kvn, dkvall, tm=128)
    dh_cd = _mm_nn("cd_in_dx", dz_cd, w_cd_pad, out_dtype=ACT_DTYPE, tm=1024, tn=1024)
    dw_cd_pad = _mm_tn("cd_in_dw", dz_cd, h_cd, tm=768, tn=1024)
    dw_cd8 = jnp.concatenate([dw_cd_pad[:384], dw_cd_pad[448:480], dw_cd_pad[512:]], axis=0).astype(WIRE_DTYPE).reshape(8, 180, D)
    dw_uq8 = dw_uq_pad.reshape(8, 128, 256)[:, :96].astype(WIRE_DTYPE)
    dw_ukv8 = jnp.concatenate([dw_kv_pad[:, :1024].reshape(128, 8, 128)[:, :, :64], dw_kv_pad[:, 1024:].reshape(128, 8, 64)],
                              axis=2).transpose(1, 0, 2).astype(WIRE_DTYPE)
    early_names = ["c_kv_norm_g", "d_w_s", "d_b_s", "final_norm_g", "c_q_norm_g", "d_ln_g", "d_ln_b"]
    early_grads = [dkvg, dws.reshape(512, 128).astype(WIRE_DTYPE), dbs, dfg, dqg.reshape(8, 1, 32), dln_g.reshape(8, 1, 64),
                   dln_b.reshape(8, 1, 64)]
    sent_cd, token = _exchange_start("scatter_cd", [dw_cd8, dw_uq8, dw_ukv8, dw_cdout.reshape(8, 128, D)] + early_grads,
                                     [True] * 4 + [False] * 4 + [True] * 3, dqg, me)
    dx2, dn1g_cd, dsc1_cd, dsh1_cd, df0, dgate0 = _rmsmod_bwd("cd_norm_bwd", x2, n1g[1], sc1, dh_cd, dx3, token,
                                                              ffn0_saved[4], mods[0][5])

    dx1, gf0, dy, dg1m = _ffn_bwd(0, dx2, df0, dgate0, ffn0_saved, mods[0][3:], n2g[0], w_up8[0], cw24[0], w_down4[0], me, y_ab, g1m)

    dw_about = _mm_rows_dw("ab_out_dw", ycat_ab, dy, out_dtype=WIRE_DTYPE, tn=1024)
    sent_about, token = _exchange_start("scatter_ab_out", [dw_about.reshape(8, 128, D)], True, dg1m, me)
    dycat = _mm_rows_dx("ab_out_dx", dy, w_about2, tm=2048)
    dz8, dconv_w, dmix_w, dscale = _ab_mix_bwd(z8, dycat, conv_w, mix_w, scale, token)
    dz8 = dz8.reshape(8, S, 256)
    dw_abin8 = _mm_cols_dw("ab_in_dw", h_ab, dz8, out_dtype=WIRE_DTYPE, tk=1024)
    sent_abin, token = _exchange_start("scatter_ab_in", [dw_abin8], True, dscale, me)
    dh_ab = _mm_cols_dx("ab_in_dx", dz8, w_abin8, out_dtype=ACT_DTYPE, tm=1024)
    dx0, dn1g_ab, dsc1_ab, dsh1_ab = _rmsmod_bwd("ab_norm_bwd", x0, n1g[0], mods[0][1], dh_ab, dx1, token)

    dmod = jnp.stack([jnp.concatenate([dsh1_ab, dsc1_ab, dg1m, *gf0["mod"]], axis=1)[0],
                      jnp.concatenate([dsh1_cd, dsc1_cd, dg1c, *gf1["mod"]], axis=1)[0]])
    late_names = ["ada_b", "norm1_g", "norm2_g", "b_mix_w", "b_scale", "a_conv_w", "ffn_conv_w"]
    late_grads = [dmod, jnp.concatenate([dn1g_ab, dn1g_cd]), jnp.concatenate([gf0["n2g"], gf1["n2g"]]),
                  dmix_w.reshape(512, 128).astype(WIRE_DTYPE), dscale, dconv_w.reshape(3, 8, 64).transpose(1, 0, 2),
                  jnp.stack([gf0["cw24"].reshape(8, 3, 704), gf1["cw24"].reshape(8, 3, 704)], axis=1),
                  jnp.pad(loss_local, ((0, 0), (0, 127)))]
    small_view = dict(ada_b=(2, 6 * D), norm1_g=(2, D), norm2_g=(2, D), b_mix_w=(512, 128), b_scale=(1, 512), c_kv_norm_g=(1, 128),
                      d_w_s=(512, 128), d_b_s=(4, 128), final_norm_g=(1, D),
                      a_conv_w=(3, 64), c_q_norm_g=(1, 32), d_ln_g=(1, 64), d_ln_b=(1, 64), ffn_conv_w=(2, 3, 704))
    late_sent, token = _exchange_start("gather_small_grads_late", late_grads, [False] * 5 + [True] * 2 + [False], dx0, me)

    res = {}

    def update(name, w, m, v, parts, shape3d):
        outs = _adamw("adamw_" + name, w.reshape(shape3d), m.reshape(shape3d), v.reshape(shape3d),
                      [p.reshape((p.shape[0],) + shape3d[1:]) for p in parts])
        res[name] = [o_.reshape(w.shape) for o_ in outs]

    p_cdin, p_uq, p_ukv, p_cdout = _exchange_wait("wait_scatter_cd", sent_cd, token, [0, 1, 2, 3])
    swap = lambda a: jnp.swapaxes(a, 1, 2)
    update("cd_w_in", swap(cd_w_in), swap(m_cd_w_in), swap(v_cd_w_in), [p_cdin], (1, 180, D))
    update("c_w_uq", swap(c_w_uq), swap(m_c_w_uq), swap(v_c_w_uq), [p_uq], (1, 96, 256))
    for name in ("cd_w_in", "c_w_uq"):
        res[name] = [swap(o_) for o_ in res[name]]
    update("c_w_ukv", c_w_ukv, m_c_w_ukv, v_c_w_ukv, [p_ukv], (1, 128, 128))
    update("cd_w_out", cd_w_out, m_cd_w_out, v_cd_w_out, [p_cdout], (1, 128, D))
    (p_dn1,) = _exchange_wait("wait_scatter_ffn1_down", gf1["sent_down"], token)
    (p_dn0,) = _exchange_wait("wait_scatter_ffn0_down", gf0["sent_down"], res["cd_w_out"][0])
    update("ffn_w_down", ffn_w_down, m_ffn_w_down, v_ffn_w_down, [p_dn0, p_dn1], (2, 352, D))
    (p_up1,) = _exchange_wait("wait_scatter_ffn1_up", gf1["sent_up"], token)
    (p_up0,) = _exchange_wait("wait_scatter_ffn0_up", gf0["sent_up"], res["ffn_w_down"][0])
    swap = lambda a: jnp.swapaxes(a, 1, 2)
    update("ffn_w_up", swap(ffn_w_up), swap(m_ffn_w_up), swap(v_ffn_w_up), [p_up0, p_up1], (2, 704, D))
    up_done = res["ffn_w_up"][0]
    res["ffn_w_up"] = [swap(o_) for o_ in res["ffn_w_up"]]
    (p_about,) = _exchange_wait("wait_scatter_ab_out", sent_about, up_done)
    update("ab_w_out", ab_w_out, m_ab_w_out, v_ab_w_out, [p_about], (1, 128, D))
    (p_abin,) = _exchange_wait("wait_scatter_ab_in", sent_abin, res["ab_w_out"][0])
    update("ab_w_in", ab_w_in, m_ab_w_in, v_ab_w_in, [p_abin], (1, D, 256))

    early_parts = _exchange_wait("wait_small_grads_early", sent_cd, res["ab_w_in"][0], list(range(4, 11)))
    late_parts = _exchange_wait("wait_small_grads_late", late_sent, res["ab_w_in"][0])
    small_names = early_names + late_names
    small_parts = list(early_parts) + list(late_parts[:7])
    loss = jnp.sum(late_parts[7][:, 0, 0])
    dmod_all = late_parts[0]
    dmod_cols = lax.dynamic_slice_in_dim(dmod_all, me * 768, 768, axis=2).transpose(1, 0, 2)
    g_ada_w = _ada_bwd(c16, jnp.pad(dmod_cols, ((0, 0), (0, 16 - N_DEV), (0, 0))))
    update("ada_w", ada_w, m_ada_w, v_ada_w, [g_ada_w[None]], (1, 2 * D, 768))

    small_w = dict(ada_b=(ada_b, m_ada_b, v_ada_b), norm1_g=(norm1_g, m_norm1_g, v_norm1_g), norm2_g=(norm2_g, m_norm2_g, v_norm2_g),
                   b_mix_w=(b_mix_w, m_b_mix_w, v_b_mix_w), b_scale=(b_scale, m_b_scale, v_b_scale),
                   c_kv_norm_g=(c_kv_norm_g, m_c_kv_norm_g, v_c_kv_norm_g), d_w_s=(d_w_s, m_d_w_s, v_d_w_s),
                   d_b_s=(d_b_s, m_d_b_s, v_d_b_s), final_norm_g=(final_norm_g, m_final_norm_g, v_final_norm_g),
                   a_conv_w=(a_conv_w, m_a_conv_w, v_a_conv_w), c_q_norm_g=(c_q_norm_g, m_c_q_norm_g, v_c_q_norm_g),
                   d_ln_g=(d_ln_g, m_d_ln_g, v_d_ln_g), d_ln_b=(d_ln_b, m_d_ln_b, v_d_ln_b),
                   ffn_conv_w=(ffn_conv_w, m_ffn_conv_w, v_ffn_conv_w))
    small_out = _adamw_small("adamw_small", [tuple(a.reshape(small_view[n]) for a in small_w[n]) for n in small_names],
                             list(small_parts))
    for n, outs in zip(small_names, small_out):
        res[n] = [o_.reshape(small_w[n][0].shape) for o_ in outs]

    order = ["ada_w", "ada_b", "norm1_g", "norm2_g", "ab_w_in", "a_conv_w", "b_mix_w", "b_scale", "ab_w_out", "cd_w_in", "c_q_norm_g",
             "c_w_uq", "c_kv_norm_g", "c_w_ukv", "d_ln_g", "d_ln_b", "d_w_s", "d_b_s", "cd_w_out", "ffn_w_up", "ffn_conv_w",
             "ffn_w_down", "final_norm_g"]
    return (loss, dx0[None], *[res[n][0] for n in order], *[res[n][1] for n in order], *[res[n][2] for n in order],
            *[res[n][3] for n in order])
```

```python
import functools
import math

import jax
import jax.numpy as jnp
from jax import lax
from jax.experimental import pallas as pl
from jax.experimental.pallas import tpu as pltpu

F32 = jnp.float32
BF16 = jnp.bfloat16
_MXU_DTYPE = BF16
WIRE_DTYPE = BF16
ACT_DTYPE = BF16
_VMEM_LIMIT = 56 * 2 ** 20
N_DEV = 8
EPS = 1e-6
POOL_WINDOWS = (2, 4, 8, 16)
ATTN_SCALE = (64 + 32) ** -0.5
ADAM_LR, ADAM_B1, ADAM_B2, ADAM_EPS, ADAM_WD, ADAM_STEP = 0.001, 0.9, 0.999, 1e-08, 0.01, 10
MESH = pl.DeviceIdType.MESH
ANY = pl.BlockSpec(memory_space=pl.ANY)


def _cp(*sem):
    return pltpu.CompilerParams(dimension_semantics=sem, vmem_limit_bytes=_VMEM_LIMIT)


def _dot(a, b, contract):
    dn = {"nn": (((1,), (0,)), ((), ())), "nt": (((1,), (1,)), ((), ())), "tn": (((0,), (0,)), ((), ()))}[contract]
    return lax.dot_general(a.astype(_MXU_DTYPE), b.astype(_MXU_DTYPE), dn, preferred_element_type=F32)


def _my_position():
    x, y, c = lax.axis_index("x"), lax.axis_index("y"), lax.axis_index("c")
    return x, y, c, 4 * x + 2 * y + c


def _exchange(name, groups, scatter):
    flat = [a for g in groups for a in g]
    n_in, n_grp = len(flat), len(groups)
    out_shapes = []
    for g in groups:
        slab = g[0].shape[1:] if scatter else g[0].shape
        out_shapes.append(jax.ShapeDtypeStruct((N_DEV, len(g)) + tuple(slab), g[0].dtype))

    def body(*refs):
        ins, outs = refs[:n_in], refs[n_in:n_in + n_grp]
        send_sems, recv_sems, local_sems = refs[n_in + n_grp:]
        x, y, c, me = _my_position()
        i = 0
        for gi, g in enumerate(groups):
            for l in range(len(g)):
                src = ins[i]
                i += 1
                pltpu.make_async_copy(src.at[me] if scatter else src, outs[gi].at[me, l], local_sems.at[gi]).start()
                for k in range(1, N_DEV):
                    px = 1 - x if k & 4 else x
                    py = 1 - y if k & 2 else y
                    pc = 1 - c if k & 1 else c
                    peer = 4 * px + 2 * py + pc
                    pltpu.make_async_remote_copy(
                        src_ref=src.at[peer] if scatter else src, dst_ref=outs[gi].at[me, l],
                        send_sem=send_sems.at[gi], recv_sem=recv_sems.at[gi],
                        device_id=(px, py, pc), device_id_type=MESH).start()
        for gi in range(n_grp):
            mine = outs[gi].at[me]
            pltpu.make_async_copy(mine, mine, local_sems.at[gi]).wait()
            seven = outs[gi].at[pl.ds(0, N_DEV - 1)]
            w = pltpu.make_async_remote_copy(src_ref=seven, dst_ref=seven, send_sem=send_sems.at[gi],
                                             recv_sem=recv_sems.at[gi], device_id=(x, y, c), device_id_type=MESH)
            w.wait_send()
            w.wait_recv()

    return pl.pallas_call(
        body, name=name, out_shape=tuple(out_shapes),
        in_specs=[ANY] * n_in, out_specs=tuple([ANY] * n_grp),
        scratch_shapes=[pltpu.SemaphoreType.DMA((n_grp,)), pltpu.SemaphoreType.DMA((n_grp,)),
                        pltpu.SemaphoreType.DMA((n_grp,))],
        compiler_params=pltpu.CompilerParams(has_side_effects=True),
    )(*flat)


HBM_SPEC = pl.BlockSpec(memory_space=pltpu.HBM)
SEM_SPEC = pl.BlockSpec(memory_space=pltpu.SEMAPHORE)
EFFECT = pltpu.SideEffectType.DATAFLOW_SIDE_EFFECTING


def _put_mine(name, srcs, scatter, me):
    n = len(srcs)
    slabs = [tuple(s.shape[1:] if sc else s.shape) for s, sc in zip(srcs, scatter)]

    def body(me_ref, *refs):
        for i in range(n):
            refs[n + i][...] = refs[i][...]

    def at_me(slab):
        return pl.BlockSpec((None,) + slab, lambda g, me_ref, nd=len(slab): (me_ref[0],) + (0,) * nd)

    def whole(slab):
        return pl.BlockSpec(slab, lambda g, me_ref, nd=len(slab): (0,) * nd)

    return pl.pallas_call(
        body, name=name,
        grid_spec=pltpu.PrefetchScalarGridSpec(
            num_scalar_prefetch=1, grid=(1,),
            in_specs=[at_me(slab) if sc else whole(slab) for slab, sc in zip(slabs, scatter)],
            out_specs=[at_me(slab) for slab in slabs]),
        out_shape=[jax.ShapeDtypeStruct((N_DEV,) + slab, s.dtype) for slab, s in zip(slabs, srcs)],
        compiler_params=_cp("arbitrary"))(me.reshape(1), *srcs)


def _exchange_start(name, srcs, scatter, after, me):
    n = len(srcs)
    scatter = list(scatter) if isinstance(scatter, (list, tuple)) else [scatter] * n
    lands = _put_mine(name + "_mine", srcs, scatter, me)
    srcs = [pltpu.with_memory_space_constraint(a, pltpu.HBM) for a in srcs]
    lands = [pltpu.with_memory_space_constraint(a, pltpu.HBM) for a in lands]

    def body(*refs):
        ins, land = refs[:n], refs[n:2 * n]
        send_sems, recv_sems, token = refs[2 * n + 1], refs[2 * n + 2], refs[-1]
        x, y, c, me_in = _my_position()
        for i in range(n):
            for k in range(1, N_DEV):
                px = 1 - x if k & 4 else x
                py = 1 - y if k & 2 else y
                pc = 1 - c if k & 1 else c
                pltpu.make_async_remote_copy(
                    src_ref=ins[i].at[4 * px + 2 * py + pc] if scatter[i] else ins[i], dst_ref=land[i].at[me_in],
                    send_sem=send_sems.at[i], recv_sem=recv_sems.at[i],
                    device_id=(px, py, pc), device_id_type=MESH).start()
        token[...] = jnp.zeros_like(token)

    outs = pl.pallas_call(
        body, name=name,
        out_shape=(pltpu.SemaphoreType.DMA((n,)), pltpu.SemaphoreType.DMA((n,)),
                   *[pltpu.HBM(a.shape, a.dtype) for a in srcs], *[pltpu.HBM(a.shape, a.dtype) for a in lands],
                   jax.ShapeDtypeStruct((8, 128), F32)),
        in_specs=[HBM_SPEC] * (2 * n) + [ANY],
        out_specs=(SEM_SPEC, SEM_SPEC, *[HBM_SPEC] * (2 * n), pl.BlockSpec(memory_space=pltpu.VMEM)),
        input_output_aliases={i: 2 + i for i in range(2 * n)},
        compiler_params=pltpu.CompilerParams(has_side_effects=EFFECT),
    )(*srcs, *lands, after)
    return (outs[0], outs[1], outs[2:2 + n], outs[2 + n:2 + 2 * n]), outs[-1]


def _exchange_wait(name, handle, after, which=None):
    send_sems, recv_sems, srcs, lands = handle
    which = list(range(len(srcs))) if which is None else list(which)
    srcs, lands = [srcs[i] for i in which], [lands[i] for i in which]
    n = len(srcs)

    def body(*refs):
        land, send_ref, recv_ref = refs[n:2 * n], refs[2 * n], refs[2 * n + 1]
        x, y, c, _ = _my_position()
        for k, i in enumerate(which):
            seven = land[k].at[pl.ds(0, N_DEV - 1)]
            w = pltpu.make_async_remote_copy(src_ref=seven, dst_ref=seven, send_sem=send_ref.at[i], recv_sem=recv_ref.at[i],
                                             device_id=(x, y, c), device_id_type=MESH)
            w.wait_send()
            w.wait_recv()

    outs = pl.pallas_call(
        body, name=name,
        out_shape=(*[pltpu.HBM(a.shape, a.dtype) for a in srcs], *[pltpu.HBM(a.shape, a.dtype) for a in lands]),
        in_specs=[HBM_SPEC] * (2 * n) + [SEM_SPEC, SEM_SPEC, ANY],
        out_specs=tuple([HBM_SPEC] * (2 * n)),
        input_output_aliases={i: i for i in range(2 * n)},
        compiler_params=pltpu.CompilerParams(has_side_effects=EFFECT),
    )(*srcs, *lands, send_sems, recv_sems, after)
    return outs[n:]


def _other_chips(x, y):
    return [(1 - x, y), (x, 1 - y), (1 - x, 1 - y)]


def _hier_gather_start(name, srcs, after, me):
    n = len(srcs)
    lands = _put_mine(name + "_mine", srcs, [False] * n, me)
    srcs = [pltpu.with_memory_space_constraint(a, pltpu.HBM) for a in srcs]
    lands = [pltpu.with_memory_space_constraint(a, pltpu.HBM) for a in lands]

    def body(*refs):
        ins, land = refs[:n], refs[n:2 * n]
        ici_send, ici_recv, d2d_send, d2d_recv = refs[2 * n + 1:2 * n + 5]
        token = refs[-1]
        x, y, c, me_in = _my_position()
        for i in range(n):
            pltpu.make_async_remote_copy(src_ref=ins[i], dst_ref=land[i].at[me_in], send_sem=d2d_send.at[i], recv_sem=d2d_recv.at[i],
                                         device_id=(x, y, 1 - c), device_id_type=MESH).start()
            for px, py in _other_chips(x, y):
                pltpu.make_async_remote_copy(src_ref=ins[i], dst_ref=land[i].at[me_in], send_sem=ici_send.at[i],
                                             recv_sem=ici_recv.at[i], device_id=(px, py, c), device_id_type=MESH).start()
        token[...] = jnp.zeros_like(token)

    sem = pltpu.SemaphoreType.DMA((n,))
    outs = pl.pallas_call(
        body, name=name,
        out_shape=(sem, sem, sem, sem, *[pltpu.HBM(a.shape, a.dtype) for a in srcs], *[pltpu.HBM(a.shape, a.dtype) for a in lands],
                   jax.ShapeDtypeStruct((8, 128), F32)),
        in_specs=[HBM_SPEC] * (2 * n) + [ANY],
        out_specs=(SEM_SPEC,) * 4 + (HBM_SPEC,) * (2 * n) + (pl.BlockSpec(memory_space=pltpu.VMEM),),
        input_output_aliases={i: 4 + i for i in range(2 * n)},
        compiler_params=pltpu.CompilerParams(has_side_effects=EFFECT),
    )(*srcs, *lands, after)
    return (outs[:4], outs[4:4 + n], outs[4 + n:4 + 2 * n]), outs[-1]


def _hier_gather_forward(name, handle, after):
    sems, srcs, lands = handle
    n = len(srcs)

    def body(*refs):
        land = refs[n:2 * n]
        ici_send, ici_recv, d2d_send, d2d_recv = refs[2 * n:2 * n + 4]
        x, y, c, _ = _my_position()
        for i in range(n):
            three = land[i].at[pl.ds(0, 3)]
            pltpu.make_async_remote_copy(src_ref=three, dst_ref=three, send_sem=ici_send.at[i], recv_sem=ici_recv.at[i],
                                         device_id=(x, y, c), device_id_type=MESH).wait_recv()
            for px, py in _other_chips(x, y):
                slab = land[i].at[4 * px + 2 * py + c]
                pltpu.make_async_remote_copy(src_ref=slab, dst_ref=slab, send_sem=d2d_send.at[i], recv_sem=d2d_recv.at[i],
                                             device_id=(x, y, 1 - c), device_id_type=MESH).start()

    outs = pl.pallas_call(
        body, name=name,
        out_shape=(*[pltpu.HBM(a.shape, a.dtype) for a in srcs], *[pltpu.HBM(a.shape, a.dtype) for a in lands]),
        in_specs=[HBM_SPEC] * (2 * n) + [SEM_SPEC] * 4 + [ANY],
        out_specs=tuple([HBM_SPEC] * (2 * n)),
        input_output_aliases={i: i for i in range(2 * n)},
        compiler_params=pltpu.CompilerParams(has_side_effects=EFFECT),
    )(*srcs, *lands, *sems, after)
    return (sems, outs[:n], outs[n:])


def _hier_gather_wait(name, handle, after):
    sems, srcs, lands = handle
    n = len(srcs)

    def body(*refs):
        land = refs[n:2 * n]
        ici_send, ici_recv, d2d_send, d2d_recv = refs[2 * n:2 * n + 4]
        x, y, c, _ = _my_position()
        for i in range(n):
            three, four = land[i].at[pl.ds(0, 3)], land[i].at[pl.ds(0, 4)]
            pltpu.make_async_remote_copy(src_ref=three, dst_ref=three, send_sem=ici_send.at[i], recv_sem=ici_recv.at[i],
                                         device_id=(x, y, c), device_id_type=MESH).wait_send()
            w = pltpu.make_async_remote_copy(src_ref=four, dst_ref=four, send_sem=d2d_send.at[i], recv_sem=d2d_recv.at[i],
                                             device_id=(x, y, c), device_id_type=MESH)
            w.wait_send()
            w.wait_recv()

    outs = pl.pallas_call(
        body, name=name,
        out_shape=(*[pltpu.HBM(a.shape, a.dtype) for a in srcs], *[pltpu.HBM(a.shape, a.dtype) for a in lands]),
        in_specs=[HBM_SPEC] * (2 * n) + [SEM_SPEC] * 4 + [ANY],
        out_specs=tuple([HBM_SPEC] * (2 * n)),
        input_output_aliases={i: i for i in range(2 * n)},
        compiler_params=pltpu.CompilerParams(has_side_effects=EFFECT),
    )(*srcs, *lands, *sems, after)
    return outs[n:]


def _pack(arrs):
    flat = jnp.concatenate([a.reshape(-1).astype(F32) for a in arrs])
    n = flat.shape[0]
    rows = -(-n // 1024) * 8
    return jnp.pad(flat, (0, rows * 128 - n)).reshape(rows, 128)


def _unpack(buf, shapes, lead=()):
    flat = buf.reshape(lead + (-1,))
    out, off = [], 0
    for s in shapes:
        n = math.prod(s)
        out.append(flat[..., off:off + n].reshape(lead + tuple(s)))
        off += n
    return out


def _mm(name, a, a_spec, b, b_spec, out_sds, o_spec, grid, contract, nk=1, stacked=0):
    o_blk = tuple(d for d in o_spec.block_shape if d is not None)

    def body(a_ref, b_ref, o_ref, *acc):
        if stacked:
            r = _dot(a_ref[0], b_ref[0], contract)
            for q in range(1, stacked):
                r = r + _dot(a_ref[q], b_ref[q], contract)
        else:
            r = _dot(a_ref[...], b_ref[...], contract)
        if nk == 1:
            o_ref[...] = r.astype(o_ref.dtype)
        else:
            k = pl.program_id(len(grid) - 1)

            @pl.when(k == 0)
            def _():
                acc[0][...] = r

            @pl.when(k > 0)
            def _():
                acc[0][...] += r

            @pl.when(k == nk - 1)
            def _():
                o_ref[...] = acc[0][...].astype(o_ref.dtype)

    sem = ("parallel",) * (len(grid) - 1) + (("arbitrary",) if nk > 1 else ("parallel",))
    return pl.pallas_call(
        body, name=name, out_shape=out_sds, grid=grid, in_specs=[a_spec, b_spec], out_specs=o_spec,
        scratch_shapes=[pltpu.VMEM(o_blk, F32)] if nk > 1 else [],
        compiler_params=pltpu.CompilerParams(dimension_semantics=sem, vmem_limit_bytes=_VMEM_LIMIT,
                                             allow_input_fusion=[True, True]))(a, b)


def _tile(n, want):
    t = min(n, want)
    assert n % t == 0, (n, t)
    return t


def _mm_nn(name, a, b, out_dtype=F32, tm=512, tn=512):
    (M, K), N = a.shape, b.shape[1]
    tm, tn = _tile(M, tm), _tile(N, tn)
    return _mm(name, a, pl.BlockSpec((tm, K), lambda i, j: (i, 0)), b, pl.BlockSpec((K, tn), lambda i, j: (0, j)),
               jax.ShapeDtypeStruct((M, N), out_dtype), pl.BlockSpec((tm, tn), lambda i, j: (i, j)),
               (M // tm, N // tn), "nn")


def _mm_nt(name, a, b, out_dtype=F32, tm=512, tn=512):
    (M, K), N = a.shape, b.shape[0]
    tm, tn = _tile(M, tm), _tile(N, tn)
    return _mm(name, a, pl.BlockSpec((tm, K), lambda i, j: (i, 0)), b, pl.BlockSpec((tn, K), lambda i, j: (j, 0)),
               jax.ShapeDtypeStruct((M, N), out_dtype), pl.BlockSpec((tm, tn), lambda i, j: (i, j)),
               (M // tm, N // tn), "nt")


def _mm_tn(name, a, b, out_dtype=F32, tm=512, tn=512):
    (K, M), N = a.shape, b.shape[1]
    tm, tn = _tile(M, tm), _tile(N, tn)
    return _mm(name, a, pl.BlockSpec((K, tm), lambda i, j: (0, i)), b, pl.BlockSpec((K, tn), lambda i, j: (0, j)),
               jax.ShapeDtypeStruct((M, N), out_dtype), pl.BlockSpec((tm, tn), lambda i, j: (i, j)),
               (M // tm, N // tn), "tn")


def _mm_cols(name, a, w, out_dtype=F32, tm=512):
    (M, K), (J, _, n) = a.shape, w.shape
    tm = _tile(M, tm)
    return _mm(name, a, pl.BlockSpec((tm, K), lambda j, i: (i, 0)), w, pl.BlockSpec((None, K, n), lambda j, i: (j, 0, 0)),
               jax.ShapeDtypeStruct((J, M, n), out_dtype), pl.BlockSpec((None, tm, n), lambda j, i: (j, i, 0)),
               (J, M // tm), "nn")


def _mm_cols_dx(name, d, w, out_dtype=F32, tm=512, jb=None):
    (J, M, n), K = d.shape, w.shape[1]
    tm, jb = _tile(M, tm), J if jb is None else jb
    return _mm(name, d, pl.BlockSpec((jb, tm, n), lambda i, j: (j, i, 0)), w, pl.BlockSpec((jb, K, n), lambda i, j: (j, 0, 0)),
               jax.ShapeDtypeStruct((M, K), out_dtype), pl.BlockSpec((tm, K), lambda i, j: (i, 0)),
               (M // tm, J // jb), "nt", nk=J // jb, stacked=jb)


def _mm_cols_dw(name, a, d, out_dtype=F32, tk=512):
    (M, K), (J, _, n) = a.shape, d.shape
    tk = _tile(K, tk)
    return _mm(name, a, pl.BlockSpec((M, tk), lambda j, i: (0, i)), d, pl.BlockSpec((None, M, n), lambda j, i: (j, 0, 0)),
               jax.ShapeDtypeStruct((J, K, n), out_dtype), pl.BlockSpec((None, tk, n), lambda j, i: (j, i, 0)),
               (J, K // tk), "tn")


def _mm_cols_dwt(name, a, d, out_dtype=F32, tk=512):
    (M, K), (J, _, n) = a.shape, d.shape
    tk = _tile(K, tk)
    return _mm(name, d, pl.BlockSpec((None, M, n), lambda j, i: (j, 0, 0)), a, pl.BlockSpec((M, tk), lambda j, i: (0, i)),
               jax.ShapeDtypeStruct((J, n, K), out_dtype), pl.BlockSpec((None, n, tk), lambda j, i: (j, 0, i)),
               (J, K // tk), "tn")


def _mm_rows_resid(name, a, w, resid, gate, tm=512):
    (Q, M, k), N = a.shape, w.shape[2]
    tm = _tile(M, tm)

    def body(a_ref, w_ref, r_ref, g_ref, y_ref, x_ref):
        y = _dot(a_ref[0], w_ref[0], "nn")
        for q in range(1, Q):
            y = y + _dot(a_ref[q], w_ref[q], "nn")
        y_ref[...] = y.astype(y_ref.dtype)
        x_ref[...] = r_ref[...] + g_ref[...] * y

    return pl.pallas_call(
        body, name=name, grid=(M // tm,),
        out_shape=(jax.ShapeDtypeStruct((M, N), ACT_DTYPE), jax.ShapeDtypeStruct((M, N), F32)),
        in_specs=[pl.BlockSpec((Q, tm, k), lambda i: (0, i, 0)), pl.BlockSpec((Q, k, N), lambda i: (0, 0, 0)),
                  pl.BlockSpec((tm, N), lambda i: (i, 0)), pl.BlockSpec((1, N), lambda i: (0, 0))],
        out_specs=(pl.BlockSpec((tm, N), lambda i: (i, 0)), pl.BlockSpec((tm, N), lambda i: (i, 0))),
        compiler_params=_cp("parallel"))(a, w, resid, gate)


def _mm_rows_dx(name, d, w, out_dtype=F32, tm=512):
    (M, N), (Q, k, _) = d.shape, w.shape
    tm = _tile(M, tm)
    return _mm(name, d, pl.BlockSpec((tm, N), lambda q, i: (i, 0)), w, pl.BlockSpec((None, k, N), lambda q, i: (q, 0, 0)),
               jax.ShapeDtypeStruct((Q, M, k), out_dtype), pl.BlockSpec((None, tm, k), lambda q, i: (q, i, 0)),
               (Q, M // tm), "nt")


def _mm_rows_dw(name, a, d, out_dtype=F32, tn=512):
    (Q, M, k), N = a.shape, d.shape[1]
    tn = _tile(N, tn)
    return _mm(name, a, pl.BlockSpec((None, M, k), lambda q, j: (q, 0, 0)), d, pl.BlockSpec((M, tn), lambda q, j: (0, j)),
               jax.ShapeDtypeStruct((Q, k, N), out_dtype), pl.BlockSpec((None, k, tn), lambda q, j: (q, 0, j)),
               (Q, N // tn), "tn")


def _silu(v):
    return v * jax.nn.sigmoid(v)


def _ada_fwd(c16, ada_w):
    L, D, n = ada_w.shape

    def body(c_ref, w_ref, o_ref):
        o_ref[...] = _dot(_silu(c_ref[...]), w_ref[...], "nn")

    return pl.pallas_call(
        body, name="ada_fwd", grid=(L,), out_shape=jax.ShapeDtypeStruct((L, 16, n), F32),
        in_specs=[pl.BlockSpec((16, D), lambda l: (0, 0)), pl.BlockSpec((None, D, n), lambda l: (l, 0, 0))],
        out_specs=pl.BlockSpec((None, 16, n), lambda l: (l, 0, 0)), compiler_params=_cp("parallel"))(c16, ada_w)


def _ada_bwd(c16, dmod16):
    L, _, n = dmod16.shape
    D = c16.shape[1]

    def body(c_ref, d_ref, o_ref):
        o_ref[...] = _dot(_silu(c_ref[...]), d_ref[...], "tn")

    return pl.pallas_call(
        body, name="ada_bwd", grid=(L,), out_shape=jax.ShapeDtypeStruct((L, D, n), F32),
        in_specs=[pl.BlockSpec((16, D), lambda l: (0, 0)), pl.BlockSpec((None, 16, n), lambda l: (l, 0, 0))],
        out_specs=pl.BlockSpec((None, D, n), lambda l: (l, 0, 0)), compiler_params=_cp("parallel"))(c16, dmod16)


def _row_spec(tr, n):
    return pl.BlockSpec((tr, n), lambda i: (i, 0))


def _vec_spec(n):
    return pl.BlockSpec((1, n), lambda i: (0, 0))


def _rmsmod_fwd(name, x, g, sc, sh, after, tr=512):
    S, D = x.shape

    def body(x_ref, g_ref, sc_ref, sh_ref, after_ref, h_ref):
        xv = x_ref[...]
        rstd = lax.rsqrt(jnp.mean(xv * xv, axis=-1, keepdims=True) + EPS)
        y = xv * rstd * g_ref[...]
        h_ref[...] = (y * (1.0 + sc_ref[...]) + sh_ref[...]).astype(h_ref.dtype)

    return pl.pallas_call(
        body, name=name, grid=(S // tr,), out_shape=jax.ShapeDtypeStruct((S, D), _MXU_DTYPE),
        in_specs=[_row_spec(tr, D), _vec_spec(D), _vec_spec(D), _vec_spec(D), ANY], out_specs=_row_spec(tr, D),
        compiler_params=_cp("parallel"))(x, g, sc, sh, after)


def _acc_rows(ref, val, first):
    s = jnp.sum(val, axis=0, keepdims=True)

    @pl.when(first)
    def _():
        ref[...] = s

    @pl.when(jnp.logical_not(first))
    def _():
        ref[...] += s


def _gate_bwd_tail(dx, y_ref, gate_ref, dy_ref, dgate_ref, first):
    dy_ref[...] = (gate_ref[...] * dx).astype(dy_ref.dtype)
    _acc_rows(dgate_ref, dx * y_ref[...].astype(F32), first)


def _rmsmod_bwd(name, x, g, sc, dh, dres, after, y=None, gate=None, tr=512):
    S, D = x.shape
    tail = y is not None

    def body(x_ref, g_ref, sc_ref, dh_ref, dres_ref, after_ref, *rest):
        (y_ref, gate_ref), rest = (rest[:2], rest[2:]) if tail else ((None, None), rest)
        dx_ref, dg_ref, dsc_ref, dsh_ref = rest[:4]
        first = pl.program_id(0) == 0
        xv, dh_v, gv = x_ref[...], dh_ref[...].astype(F32), g_ref[...]
        rstd = lax.rsqrt(jnp.mean(xv * xv, axis=-1, keepdims=True) + EPS)
        xhat = xv * rstd
        _acc_rows(dsh_ref, dh_v, first)
        _acc_rows(dsc_ref, dh_v * (xhat * gv), first)
        dyg = dh_v * (1.0 + sc_ref[...])
        _acc_rows(dg_ref, dyg * xhat, first)
        dxhat = dyg * gv
        dx = dres_ref[...] + rstd * (dxhat - xhat * jnp.mean(dxhat * xhat, axis=-1, keepdims=True))
        dx_ref[...] = dx
        if tail:
            _gate_bwd_tail(dx, y_ref, gate_ref, rest[4], rest[5], first)

    vec = jax.ShapeDtypeStruct((1, D), F32)
    return pl.pallas_call(
        body, name=name, grid=(S // tr,),
        out_shape=(jax.ShapeDtypeStruct((S, D), F32), vec, vec, vec) + ((jax.ShapeDtypeStruct((S, D), _MXU_DTYPE), vec) if tail else ()),
        in_specs=[_row_spec(tr, D), _vec_spec(D), _vec_spec(D), _row_spec(tr, D), _row_spec(tr, D), ANY]
        + ([_row_spec(tr, D), _vec_spec(D)] if tail else []),
        out_specs=(_row_spec(tr, D), _vec_spec(D), _vec_spec(D), _vec_spec(D)) + ((_row_spec(tr, D), _vec_spec(D)) if tail else ()),
        compiler_params=_cp("arbitrary"))(x, g, sc, dh, dres, after, *((y, gate) if tail else ()))


def _loss_head(x, g, target, y, gate, tr=512):
    S, D = x.shape

    def body(x_ref, g_ref, t_ref, y_ref, gate_ref, loss_ref, dx_ref, dg_ref, dy_ref, dgate_ref):
        first = pl.program_id(0) == 0
        xv, gv = x_ref[...], g_ref[...]
        rstd = lax.rsqrt(jnp.mean(xv * xv, axis=-1, keepdims=True) + EPS)
        xhat = xv * rstd
        err = xhat * gv - t_ref[...]
        part = 0.5 * jnp.sum(jnp.mean(err * err, axis=-1, keepdims=True), axis=0, keepdims=True)

        @pl.when(first)
        def _():
            loss_ref[...] = part

        @pl.when(jnp.logical_not(first))
        def _():
            loss_ref[...] += part

        dout = err * (1.0 / D)
        _acc_rows(dg_ref, dout * xhat, first)
        dxhat = dout * gv
        dx = rstd * (dxhat - xhat * jnp.mean(dxhat * xhat, axis=-1, keepdims=True))
        dx_ref[...] = dx
        _gate_bwd_tail(dx, y_ref, gate_ref, dy_ref, dgate_ref, first)

    vec = jax.ShapeDtypeStruct((1, D), F32)
    return pl.pallas_call(
        body, name="loss_head", grid=(S // tr,),
        out_shape=(jax.ShapeDtypeStruct((1, 1), F32), jax.ShapeDtypeStruct((S, D), F32), vec,
                   jax.ShapeDtypeStruct((S, D), _MXU_DTYPE), vec),
        in_specs=[_row_spec(tr, D), _vec_spec(D), _row_spec(tr, D), _row_spec(tr, D), _vec_spec(D)],
        out_specs=(pl.BlockSpec((1, 1), lambda i: (0, 0)), _row_spec(tr, D), _vec_spec(D), _row_spec(tr, D), _vec_spec(D)),
        compiler_params=_cp("arbitrary"))(x, g, target, y, gate)


def _shift_down(v, k):
    t = lax.broadcasted_iota(jnp.int32, v.shape, 0)
    return jnp.where(t >= k, pltpu.roll(v, k, axis=0), 0.0)


def _shift_up(v, k):
    n = v.shape[0]
    t = lax.broadcasted_iota(jnp.int32, v.shape, 0)
    return jnp.where(t < n - k, pltpu.roll(v, n - k, axis=0), 0.0)


def _window_sum(p, w, shift):
    s, k = p, 1
    while k < w:
        s = s + shift(s, k)
        k *= 2
    return s


def _pool_count(shape, w):
    t = lax.broadcasted_iota(jnp.int32, shape, 0)
    return jnp.minimum(t + 1, w).astype(F32)


def _ab_specs(S):
    zs = [pl.BlockSpec((None, S, 128), functools.partial(lambda g, q: (2 * q + g // 2, 0, g % 2), q=q)) for q in range(4)]
    return zs


def _ab_mix_fwd(z8, conv_w, mix_w, scale):
    S = z8.shape[1]

    def body(b_ref, c_ref, a_ref, p_ref, w_ref, mix_ref, sc_ref, y_ref):
        g = pl.program_id(0)
        cg = c_ref[...].astype(F32) * a_ref[...].astype(F32)
        w = w_ref[...]
        conv = w[0:1] * _shift_down(cg, 2) + w[1:2] * _shift_down(cg, 1) + w[2:3] * cg
        y_ref[0] = (b_ref[...].astype(F32) * conv).astype(y_ref.dtype)
        for gg, win in enumerate(POOL_WINDOWS):
            @pl.when(g == gg)
            def _(win=win):
                p = p_ref[...].astype(F32)
                pooled = _window_sum(p, win, _shift_down) / _pool_count(p.shape, win) - p
                y_ref[1] = (_dot(pooled, mix_ref[...], "nn") * sc_ref[...]).astype(y_ref.dtype)

    return pl.pallas_call(
        body, name="ab_mix_fwd", grid=(4,), out_shape=jax.ShapeDtypeStruct((2, S, 512), _MXU_DTYPE),
        in_specs=_ab_specs(S) + [pl.BlockSpec((3, 128), lambda g: (0, g)), pl.BlockSpec((None, 128, 128), lambda g: (g, 0, 0)),
                                 pl.BlockSpec((1, 128), lambda g: (0, g))],
        out_specs=pl.BlockSpec((2, S, 128), lambda g: (0, 0, g)), compiler_params=_cp("parallel"))(z8, z8, z8, z8, conv_w, mix_w, scale)


def _ab_mix_bwd(z8, dycat2, conv_w, mix_w, scale, after):
    S = z8.shape[1]

    def body(b_ref, c_ref, a_ref, p_ref, dy_ref, w_ref, mix_ref, sc_ref, after_ref, dz_ref, dw_ref, dmix_ref, dsc_ref):
        g = pl.program_id(0)
        bv, cv, av, w = b_ref[...].astype(F32), c_ref[...].astype(F32), a_ref[...].astype(F32), w_ref[...]
        dya = dy_ref[0]
        cg = cv * av
        cg1, cg2 = _shift_down(cg, 1), _shift_down(cg, 2)
        conv = w[0:1] * cg2 + w[1:2] * cg1 + w[2:3] * cg
        dz_ref[0] = (dya * conv).astype(dz_ref.dtype)
        dconv = dya * bv
        dcg = w[2:3] * dconv + w[1:2] * _shift_up(dconv, 1) + w[0:1] * _shift_up(dconv, 2)
        dz_ref[1] = (dcg * av).astype(dz_ref.dtype)
        dz_ref[2] = (dcg * cv).astype(dz_ref.dtype)
        dw_ref[0:1, :] = jnp.sum(dconv * cg2, axis=0, keepdims=True)
        dw_ref[1:2, :] = jnp.sum(dconv * cg1, axis=0, keepdims=True)
        dw_ref[2:3, :] = jnp.sum(dconv * cg, axis=0, keepdims=True)
        for gg, win in enumerate(POOL_WINDOWS):
            @pl.when(g == gg)
            def _(win=win):
                p, dyb, mix = p_ref[...].astype(F32), dy_ref[1], mix_ref[...]
                cnt = _pool_count(p.shape, win)
                pooled = _window_sum(p, win, _shift_down) / cnt - p
                dsc_ref[...] = jnp.sum(dyb * _dot(pooled, mix, "nn"), axis=0, keepdims=True)
                dmixed = dyb * sc_ref[...]
                dmix_ref[...] = _dot(pooled, dmixed, "tn")
                dpooled = _dot(dmixed, mix, "nt")
                dz_ref[3] = (_window_sum(dpooled / cnt, win, _shift_up) - dpooled).astype(dz_ref.dtype)

    return pl.pallas_call(
        body, name="ab_mix_bwd", grid=(4,),
        out_shape=(jax.ShapeDtypeStruct((4, 2, S, 256), _MXU_DTYPE), jax.ShapeDtypeStruct((3, 512), F32),
                   jax.ShapeDtypeStruct((4, 128, 128), F32), jax.ShapeDtypeStruct((1, 512), F32)),
        in_specs=_ab_specs(S) + [pl.BlockSpec((2, S, 128), lambda g: (0, 0, g)), pl.BlockSpec((3, 128), lambda g: (0, g)),
                                 pl.BlockSpec((None, 128, 128), lambda g: (g, 0, 0)), pl.BlockSpec((1, 128), lambda g: (0, g)), ANY],
        out_specs=(pl.BlockSpec((4, None, S, 128), lambda g: (0, g // 2, 0, g % 2)), pl.BlockSpec((3, 128), lambda g: (0, g)),
                   pl.BlockSpec((None, 128, 128), lambda g: (g, 0, 0)), pl.BlockSpec((1, 128), lambda g: (0, g))),
        compiler_params=_cp("parallel"))(z8, z8, z8, z8, dycat2, conv_w, mix_w, scale, after)


HALO = 16


def _ffn_specs(S, n, tr):
    nb = S // HALO
    tile = pl.BlockSpec((2, None, tr, n), lambda j, i: (0, j, i, 0))
    prev = pl.BlockSpec((2, None, HALO, n), lambda j, i: (0, j, jnp.maximum(i * (tr // HALO) - 1, 0), 0))
    nxt = pl.BlockSpec((2, None, HALO, n), lambda j, i: (0, j, jnp.minimum((i + 1) * (tr // HALO), nb - 1), 0))
    cw = pl.BlockSpec((2, None, 3, n), lambda j, i: (0, j, 0, 0))
    return tile, prev, nxt, cw


def _shifted_rows(ext, lo, rows):
    ext = ext.astype(F32)
    return pltpu.roll(ext, 1, axis=0)[lo:lo + rows], pltpu.roll(ext, 2, axis=0)[lo:lo + rows]


def _ffn_gate_fwd(name, u24, cw24, tr=256):
    _, J, S, n = u24.shape
    tile, prev, _, cw = _ffn_specs(S, n, tr)

    def body(u_ref, up_ref, w_ref, a_ref, z_ref):
        keep = (pl.program_id(1) > 0).astype(u_ref.dtype)
        z = []
        for h in range(2):
            ext = jnp.concatenate([up_ref[h] * keep, u_ref[h]], axis=0)
            x1, x2 = _shifted_rows(ext, HALO, tr)
            w = w_ref[h]
            z.append(w[0:1] * x2 + w[1:2] * x1 + w[2:3] * u_ref[h].astype(F32))
        zg, zu = z
        sg = jax.nn.sigmoid(zg)
        silu = zg * sg
        a_ref[...] = (silu * zu).astype(a_ref.dtype)
        z_ref[0] = (zu * (sg * (1.0 + zg * (1.0 - sg)))).astype(z_ref.dtype)
        z_ref[1] = silu.astype(z_ref.dtype)

    return pl.pallas_call(
        body, name=name, grid=(J, S // tr),
        out_shape=(jax.ShapeDtypeStruct((J, S, n), _MXU_DTYPE), jax.ShapeDtypeStruct((2, J, S, n), ACT_DTYPE)),
        in_specs=[tile, prev, cw], out_specs=(pl.BlockSpec((None, tr, n), lambda j, i: (j, i, 0)), tile),
        compiler_params=_cp("parallel", "parallel"))(u24, u24, cw24)


def _ffn_gate_bwd(name, u24, z24, cw24, da4, w_up24, after, tr=256):
    _, J, S, n = u24.shape
    K = w_up24.shape[2]
    nb = S // HALO
    tile = pl.BlockSpec((2, None, tr, n), lambda i, j: (0, j, i, 0))
    nxt = pl.BlockSpec((2, None, HALO, n), lambda i, j: (0, j, jnp.minimum((i + 1) * (tr // HALO), nb - 1), 0))
    whole = lambda shape: pl.BlockSpec(shape, lambda i, j: (0,) * len(shape))

    def body(u_ref, z_ref, zn_ref, cw_ref, da_ref, dan_ref, wup_ref, after_ref, du_ref, dcw_ref, dh_ref, acc_ref):
        i, j = pl.program_id(0), pl.program_id(1)
        first = i == 0
        keep_next = (i < S // tr - 1).astype(F32)
        w = [cw_ref[h, j] for h in range(2)]
        m = tr + HALO
        da = jnp.concatenate([da_ref[...].astype(F32), dan_ref[...].astype(F32) * keep_next], axis=0)
        dz = [da * jnp.concatenate([z_ref[h], zn_ref[h]], axis=0).astype(F32) for h in range(2)]
        dh = None
        for h in range(2):
            d = dz[h]
            d0, d1, d2 = d[:tr], pltpu.roll(d, m - 1, axis=0)[:tr], pltpu.roll(d, m - 2, axis=0)[:tr]
            du = (w[h][2:3] * d0 + w[h][1:2] * d1 + w[h][0:1] * d2).astype(du_ref.dtype)
            du_ref[h] = du
            part = _dot(du, wup_ref[h, j], "nt")
            dh = part if dh is None else dh + part
            x0 = u_ref[h].astype(F32)
            parts = [jnp.sum(x0 * dk, axis=0, keepdims=True) for dk in (d2, d1, d0)]
            for k in range(3):
                @pl.when(first)
                def _(k=k, h=h):
                    dcw_ref[h, j, k:k + 1, :] = parts[k]

                @pl.when(jnp.logical_not(first))
                def _(k=k, h=h):
                    dcw_ref[h, j, k:k + 1, :] += parts[k]

        @pl.when(j == 0)
        def _():
            acc_ref[...] = dh

        @pl.when(j > 0)
        def _():
            acc_ref[...] += dh

        @pl.when(j == J - 1)
        def _():
            dh_ref[...] = acc_ref[...].astype(dh_ref.dtype)

    da_tile = pl.BlockSpec((None, tr, n), lambda i, j: (j, i, 0))
    da_next = pl.BlockSpec((None, HALO, n), lambda i, j: (j, jnp.minimum((i + 1) * (tr // HALO), nb - 1), 0))
    return pl.pallas_call(
        body, name=name, grid=(S // tr, J),
        out_shape=(jax.ShapeDtypeStruct((2, J, S, n), _MXU_DTYPE), jax.ShapeDtypeStruct((2, J, 3, n), F32),
                   jax.ShapeDtypeStruct((S, K), ACT_DTYPE)),
        in_specs=[tile, tile, nxt, whole((2, J, 3, n)), da_tile, da_next, whole((2, J, K, n)), ANY],
        out_specs=(tile, whole((2, J, 3, n)), pl.BlockSpec((tr, K), lambda i, j: (i, 0))),
        scratch_shapes=[pltpu.VMEM((tr, K), F32)],
        compiler_params=_cp("arbitrary", "arbitrary"))(u24, z24, z24, cw24, da4, da4, w_up24, after)


def _rms_rows(v, g):
    rstd = lax.rsqrt(jnp.mean(v * v, axis=-1, keepdims=True) + EPS)
    return v * rstd * g


def _rms_rows_bwd(v, g, dy):
    rstd = lax.rsqrt(jnp.mean(v * v, axis=-1, keepdims=True) + EPS)
    vhat = v * rstd
    dvhat = dy * g
    return rstd * (dvhat - vhat * jnp.mean(dvhat * vhat, axis=-1, keepdims=True)), dy * vhat


def _rope(v, cos, sa, sb):
    return v * cos + pltpu.roll(v, 112, axis=1) * sa + pltpu.roll(v, 16, axis=1) * sb


def _rope_t(d, cos, sa, sb):
    return d * cos + pltpu.roll(d * sa, 16, axis=1) + pltpu.roll(d * sb, 112, axis=1)


def _qkv_rope_fwd(z, qg, kvg, w_uq_t, w_kv, cosq, cosk, sa, sb, tr=512):
    S = z.shape[0]

    def body(ql_ref, kvl_ref, kpe_ref, qg_ref, kvg_ref, wq_ref, wkv_ref, cq_ref, ck_ref, sa_ref, sb_ref,
             qn_ref, kvn_ref, qo_ref, ko_ref, vo_ref):
        cq, ck, sa_v, sb_v = cq_ref[...], ck_ref[...], sa_ref[...], sb_ref[...]
        qn = _rms_rows(ql_ref[...], qg_ref[...]).astype(qn_ref.dtype)
        kvn = _rms_rows(kvl_ref[...], kvg_ref[...]).astype(kvn_ref.dtype)
        qn_ref[...] = qn
        kvn_ref[...] = kvn
        q = _dot(qn, wq_ref[...], "nt")
        kv = _dot(kvn, wkv_ref[...], "nn")
        kpe = _rope(kpe_ref[...], ck, sa_v, sb_v)
        for h in range(8):
            cols = slice(128 * h, 128 * h + 128)
            qo_ref[:, cols] = _rope(q[:, cols], cq, sa_v, sb_v).astype(qo_ref.dtype)
            ko_ref[:, cols] = (kv[:, cols] + kpe).astype(ko_ref.dtype)
        vo_ref[...] = kv[:, 1024:1536].astype(vo_ref.dtype)

    tab = _row_spec(tr, 128)
    whole = lambda a: pl.BlockSpec(a.shape, lambda i: (0, 0))
    return pl.pallas_call(
        body, name="qkv_rope_fwd", grid=(S // tr,),
        out_shape=(jax.ShapeDtypeStruct((S, 256), _MXU_DTYPE), jax.ShapeDtypeStruct((S, 128), _MXU_DTYPE),
                   jax.ShapeDtypeStruct((S, 1024), _MXU_DTYPE), jax.ShapeDtypeStruct((S, 1024), _MXU_DTYPE),
                   jax.ShapeDtypeStruct((S, 512), _MXU_DTYPE)),
        in_specs=[pl.BlockSpec((tr, 256), lambda i: (i, 0)), pl.BlockSpec((tr, 128), lambda i: (i, 2)),
                  pl.BlockSpec((tr, 128), lambda i: (i, 3)), _vec_spec(256), _vec_spec(128), whole(w_uq_t), whole(w_kv),
                  tab, tab, tab, tab],
        out_specs=(_row_spec(tr, 256), _row_spec(tr, 128), _row_spec(tr, 1024), _row_spec(tr, 1024), _row_spec(tr, 512)),
        compiler_params=_cp("parallel"))(z, z, z, qg, kvg, w_uq_t, w_kv, cosq, cosk, sa, sb)


def _attn_bwd_prep(o, dycat2, tr=512):
    S = o.shape[0]

    def body(o_ref, do_ref, delta_ref, doa_ref, dob_ref):
        do = do_ref[...]
        prod = do * o_ref[...]
        lane = lax.broadcasted_iota(jnp.int32, do.shape, 1)
        for p in range(4):
            cols = slice(128 * p, 128 * p + 128)
            first = lane[:, cols] < 128 * p + 64
            da = jnp.sum(jnp.where(first, prod[:, cols], 0.0), axis=-1, keepdims=True)
            db = jnp.sum(jnp.where(first, 0.0, prod[:, cols]), axis=-1, keepdims=True)
            delta_ref[p] = jnp.where(first, da, db)
            doa_ref[p] = jnp.where(first, do[:, cols], 0.0).astype(doa_ref.dtype)
            dob_ref[p] = jnp.where(first, 0.0, do[:, cols]).astype(dob_ref.dtype)

    pair = pl.BlockSpec((4, tr, 128), lambda i: (0, i, 0))
    return pl.pallas_call(
        body, name="attn_bwd_prep", grid=(S // tr,),
        out_shape=(jax.ShapeDtypeStruct((4, S, 128), F32), jax.ShapeDtypeStruct((4, S, 128), _MXU_DTYPE),
                   jax.ShapeDtypeStruct((4, S, 128), _MXU_DTYPE)),
        in_specs=[_row_spec(tr, 512), pl.BlockSpec((None, tr, 512), lambda i: (0, i, 0))],
        out_specs=(pair, pair, pair), compiler_params=_cp("parallel"))(o, dycat2)


def _qkv_rope_bwd(z, qg, kvg, dq, dk, dv, duv, w_uq_t, w_kv, cosq, cosk, sa, sb, tr=512):
    S = z.shape[0]

    def body(ql_ref, kvl_ref, qg_ref, kvg_ref, dq_ref, dk_ref, dv_ref, duv_ref, wq_ref, wkv_ref, cq_ref, ck_ref, sa_ref, sb_ref,
             dqo_ref, dkv_ref, dz_ref, dqg_ref, dkvg_ref):
        first = pl.program_id(0) == 0
        cq, ck, sa_v, sb_v = cq_ref[...], ck_ref[...], sa_ref[...], sb_ref[...]
        tot = jnp.zeros((tr, 128), F32)
        for h in range(8):
            cols = slice(128 * h, 128 * h + 128)
            dqo_ref[:, cols] = _rope_t(dq_ref[:, cols], cq, sa_v, sb_v).astype(dqo_ref.dtype)
            dkh = dk_ref[:, cols]
            tot = tot + dkh
            dkv_ref[:, cols] = dkh.astype(dkv_ref.dtype)
        dkv_ref[:, 1024:1536] = dv_ref[...].astype(dkv_ref.dtype)
        dqn = _dot(dqo_ref[...], wq_ref[...], "nn")
        dkvn = _dot(dkv_ref[...], wkv_ref[...], "nt")
        dql, dqg = _rms_rows_bwd(ql_ref[...], qg_ref[...], dqn)
        dkvl, dkvg = _rms_rows_bwd(kvl_ref[...], kvg_ref[...], dkvn)
        _acc_rows(dqg_ref, dqg, first)
        _acc_rows(dkvg_ref, dkvg, first)
        dz_ref[:, 0:256] = dql.astype(dz_ref.dtype)
        dz_ref[:, 256:384] = dkvl.astype(dz_ref.dtype)
        dz_ref[:, 384:512] = _rope_t(tot, ck, sa_v, sb_v).astype(dz_ref.dtype)
        dz_ref[:, 512:1536] = duv_ref[...].astype(dz_ref.dtype)

    tab = _row_spec(tr, 128)
    whole = lambda a: pl.BlockSpec(a.shape, lambda i: (0, 0))
    return pl.pallas_call(
        body, name="qkv_rope_bwd", grid=(S // tr,),
        out_shape=(jax.ShapeDtypeStruct((S, 1024), _MXU_DTYPE), jax.ShapeDtypeStruct((S, 1536), _MXU_DTYPE),
                   jax.ShapeDtypeStruct((S, 1536), _MXU_DTYPE), jax.ShapeDtypeStruct((1, 256), F32), jax.ShapeDtypeStruct((1, 128), F32)),
        in_specs=[pl.BlockSpec((tr, 256), lambda i: (i, 0)), pl.BlockSpec((tr, 128), lambda i: (i, 2)), _vec_spec(256), _vec_spec(128),
                  _row_spec(tr, 1024), _row_spec(tr, 1024), _row_spec(tr, 512), _row_spec(tr, 1024), whole(w_uq_t), whole(w_kv),
                  tab, tab, tab, tab],
        out_specs=(_row_spec(tr, 1024), _row_spec(tr, 1536), _row_spec(tr, 1536), _vec_spec(256), _vec_spec(128)),
        compiler_params=_cp("arbitrary"))(z, z, qg, kvg, dq, dk, dv, duv, w_uq_t, w_kv, cosq, cosk, sa, sb)


NEG = -1e30


def _attn_fwd(q, k, v, tq=1024, tk=1024):
    S = q.shape[0]
    assert tq == tk

    def body(q_ref, k_ref, v_ref, o_ref, lse_ref):
        i = pl.program_id(1)
        qs = [q_ref[:, 0:128], q_ref[:, 128:256]]

        def step(kb, carry, diagonal=False):
            start = pl.multiple_of(kb * tk, tk)
            vv = v_ref[pl.ds(start, tk), :]
            out = []
            for h in range(2):
                m, l, acc = carry[3 * h:3 * h + 3]
                s = _dot(qs[h], k_ref[pl.ds(start, tk), 128 * h:128 * h + 128], "nt") * ATTN_SCALE
                if diagonal:
                    s = jnp.where(below, s, NEG)
                m_new = jnp.maximum(m, jnp.max(s, axis=-1, keepdims=True))
                alpha = jnp.exp(m - m_new)
                p = jnp.exp(s - m_new)
                out += [m_new, alpha * l + jnp.sum(p, axis=-1, keepdims=True), alpha * acc + _dot(p, vv, "nn")]
            return tuple(out)

        below = lax.broadcasted_iota(jnp.int32, (tq, tk), 1) <= lax.broadcasted_iota(jnp.int32, (tq, tk), 0)
        init = (jnp.full((tq, 1), NEG, F32), jnp.zeros((tq, 1), F32), jnp.zeros((tq, 128), F32)) * 2
        ma, la, acca, mb, lb, accb = step(i, lax.fori_loop(0, i, step, init), diagonal=True)
        lane = lax.broadcasted_iota(jnp.int32, (tq, 128), 1)
        o_ref[...] = jnp.where(lane < 64, acca / la, accb / lb)
        lse_ref[...] = jnp.where(lane < 64, ma + jnp.log(la), mb + jnp.log(lb))

    return pl.pallas_call(
        body, name="attn_fwd", grid=(4, S // tq),
        out_shape=(jax.ShapeDtypeStruct((S, 512), F32), jax.ShapeDtypeStruct((4, S, 128), F32)),
        in_specs=[pl.BlockSpec((tq, 256), lambda p, i: (i, p)), pl.BlockSpec((S, 256), lambda p, i: (0, p)),
                  pl.BlockSpec((S, 128), lambda p, i: (0, p))],
        out_specs=(pl.BlockSpec((tq, 128), lambda p, i: (i, p)), pl.BlockSpec((None, tq, 128), lambda p, i: (p, i, 0))),
        compiler_params=_cp("parallel", "parallel"))(q, k, v)


def _attn_bwd(q, k, v, lse, delta, doa, dob, tq=512, tk=512):
    S = q.shape[0]
    assert tq == tk

    def body(q_ref, k_ref, v_ref, lse_ref, delta_ref, doa_ref, dob_ref, dq_ref, dk_ref, dv_ref):
        j = pl.program_id(1)

        @pl.when(j == 0)
        def _():
            dq_ref[...] = jnp.zeros_like(dq_ref)

        below = lax.broadcasted_iota(jnp.int32, (tq, tk), 1) <= lax.broadcasted_iota(jnp.int32, (tq, tk), 0)
        ks = [k_ref[:, 0:128], k_ref[:, 128:256]]
        vv = v_ref[...]

        def step(qb, carry, diagonal=False):
            dka, dkb, dvp = carry
            start = pl.multiple_of(qb * tq, tq)
            rows = pl.ds(start, tq)
            lse_v, delta_v = lse_ref[rows, :], delta_ref[rows, :]
            dos = [doa_ref[rows, :], dob_ref[rows, :]]
            dks = [dka, dkb]
            for h in range(2):
                delta = delta_v[:, 64 * h:64 * h + 1]
                do_h = dos[h]
                qh = q_ref[rows, 128 * h:128 * h + 128]
                s = _dot(qh, ks[h], "nt") * ATTN_SCALE
                p = jnp.exp(s - lse_v[:, 64 * h:64 * h + 1])
                if diagonal:
                    p = jnp.where(below, p, 0.0)
                dvp = dvp + _dot(p, do_h, "tn")
                ds = p * (_dot(do_h, vv, "nt") - delta) * ATTN_SCALE
                dq_ref[rows, 128 * h:128 * h + 128] += _dot(ds, ks[h], "nn")
                dks[h] = dks[h] + _dot(ds, qh, "tn")
            return dks[0], dks[1], dvp

        zero = jnp.zeros((tk, 128), F32)
        dka, dkb, dvp = lax.fori_loop(j + 1, S // tq, step, step(j, (zero, zero, zero), diagonal=True))
        dk_ref[:, 0:128] = dka
        dk_ref[:, 128:256] = dkb
        dv_ref[...] = dvp

    return pl.pallas_call(
        body, name="attn_bwd", grid=(4, S // tk),
        out_shape=(jax.ShapeDtypeStruct((S, 1024), F32), jax.ShapeDtypeStruct((S, 1024), F32), jax.ShapeDtypeStruct((S, 512), F32)),
        in_specs=[pl.BlockSpec((S, 256), lambda p, j: (0, p)), pl.BlockSpec((tk, 256), lambda p, j: (j, p)),
                  pl.BlockSpec((tk, 128), lambda p, j: (j, p))] + [pl.BlockSpec((None, S, 128), lambda p, j: (p, 0, 0))] * 4,
        out_specs=(pl.BlockSpec((S, 256), lambda p, j: (0, p)), pl.BlockSpec((tk, 256), lambda p, j: (j, p)),
                   pl.BlockSpec((tk, 128), lambda p, j: (j, p))),
        compiler_params=_cp("parallel", "arbitrary"))(q, k, v, lse, delta, doa, dob)


CHUNK = 128
GELU_C = math.sqrt(2.0 / math.pi)


def _gelu(v):
    t = jnp.tanh(GELU_C * (v + 0.044715 * (v * v * v)))
    return v * (0.5 * (1.0 + t)), t


def _gelu_grad(v, t):
    return 0.5 * (1.0 + t) + v * (0.5 * (1.0 - t * t) * GELU_C * (1.0 + 3.0 * 0.044715 * v * v))


def _tril(w):
    r = lax.broadcasted_iota(jnp.int32, w.shape, 0)
    c = lax.broadcasted_iota(jnp.int32, w.shape, 1)
    return jnp.where(c <= r, w, 0.0)


def _layer_norm(v, g, b):
    xc = v - jnp.mean(v, axis=-1, keepdims=True)
    rstd = lax.rsqrt(jnp.mean(xc * xc, axis=-1, keepdims=True) + EPS)
    xhat = xc * rstd
    return xhat * g + b, xhat, rstd


def _sgu_fwd(z, o, ln_g, ln_b, w_s, b_st, tr=512):
    S = z.shape[0]

    def body(u_ref, v_ref, o_ref, g_ref, b_ref, ws_ref, bs_ref, y_ref):
        gu, _ = _gelu(u_ref[...])
        gv, _ = _gelu(v_ref[...])
        vln, _, _ = _layer_norm(gv, g_ref[...], b_ref[...])
        y_ref[0] = o_ref[...].astype(y_ref.dtype)
        for g in range(4):
            wt = _tril(ws_ref[g])
            cols = slice(128 * g, 128 * g + 128)
            for ch in range(tr // CHUNK):
                rows = slice(CHUNK * ch, CHUNK * ch + CHUNK)
                mixed = _dot(wt, vln[rows, cols], "nn") + bs_ref[:, g:g + 1]
                y_ref[1, rows, cols] = (gu[rows, cols] * mixed).astype(y_ref.dtype)

    return pl.pallas_call(
        body, name="sgu_fwd", grid=(S // tr,), out_shape=jax.ShapeDtypeStruct((2, S, 512), _MXU_DTYPE),
        in_specs=[pl.BlockSpec((tr, 512), lambda i: (i, 1)), pl.BlockSpec((tr, 512), lambda i: (i, 2)), _row_spec(tr, 512),
                  _vec_spec(512), _vec_spec(512), pl.BlockSpec((4, 128, 128), lambda i: (0, 0, 0)), pl.BlockSpec((128, 4), lambda i: (0, 0))],
        out_specs=pl.BlockSpec((2, tr, 512), lambda i: (0, i, 0)), compiler_params=_cp("parallel"))(z, z, o, ln_g, ln_b, w_s, b_st)


def _sgu_bwd(z, dycat2, ln_g, ln_b, w_s, b_st, tr=512):
    S = z.shape[0]

    def body(u_ref, v_ref, dy_ref, g_ref, b_ref, ws_ref, bs_ref, duv_ref, dg_ref, db_ref, dws_ref, dbs_ref):
        first = pl.program_id(0) == 0
        u_pre, v_pre = u_ref[...], v_ref[...]
        gu, tu = _gelu(u_pre)
        gv, tv = _gelu(v_pre)
        gain = g_ref[...]
        vln, xhat, rstd = _layer_norm(gv, gain, b_ref[...])

        @pl.when(first)
        def _():
            dws_ref[...] = jnp.zeros_like(dws_ref)
            dbs_ref[...] = jnp.zeros_like(dbs_ref)

        dvln_cols = []
        for g in range(4):
            wt = _tril(ws_ref[g])
            cols = slice(128 * g, 128 * g + 128)
            dmixed_sum = jnp.zeros((CHUNK, 128), F32)
            dw = jnp.zeros((CHUNK, CHUNK), F32)
            dvln_rows = []
            for ch in range(tr // CHUNK):
                rows = slice(CHUNK * ch, CHUNK * ch + CHUNK)
                vt = vln[rows, cols]
                mixed = _dot(wt, vt, "nn") + bs_ref[:, g:g + 1]
                dyd = dy_ref[rows, cols]
                duv_ref[rows, cols] = (dyd * mixed * _gelu_grad(u_pre[rows, cols], tu[rows, cols])).astype(duv_ref.dtype)
                dmixed = dyd * gu[rows, cols]
                dmixed_sum = dmixed_sum + dmixed
                dw = dw + _dot(dmixed, vt, "nt")
                dvln_rows.append(_dot(wt, dmixed, "tn"))
            dws_ref[g] += _tril(dw)
            dbs_ref[g:g + 1, :] += jnp.sum(dmixed_sum.T, axis=0, keepdims=True)
            dvln_cols.append(jnp.concatenate(dvln_rows, axis=0))
        dvln = jnp.concatenate(dvln_cols, axis=1)
        _acc_rows(dg_ref, dvln * xhat, first)
        _acc_rows(db_ref, dvln, first)
        dxhat = dvln * gain
        dgv = rstd * (dxhat - jnp.mean(dxhat, axis=-1, keepdims=True) - xhat * jnp.mean(dxhat * xhat, axis=-1, keepdims=True))
        duv_ref[:, 512:1024] = (dgv * _gelu_grad(v_pre, tv)).astype(duv_ref.dtype)

    return pl.pallas_call(
        body, name="sgu_bwd", grid=(S // tr,),
        out_shape=(jax.ShapeDtypeStruct((S, 1024), _MXU_DTYPE), jax.ShapeDtypeStruct((1, 512), F32), jax.ShapeDtypeStruct((1, 512), F32),
                   jax.ShapeDtypeStruct((4, 128, 128), F32), jax.ShapeDtypeStruct((4, 128), F32)),
        in_specs=[pl.BlockSpec((tr, 512), lambda i: (i, 1)), pl.BlockSpec((tr, 512), lambda i: (i, 2)),
                  pl.BlockSpec((None, tr, 512), lambda i: (1, i, 0)), _vec_spec(512), _vec_spec(512),
                  pl.BlockSpec((4, 128, 128), lambda i: (0, 0, 0)), pl.BlockSpec((128, 4), lambda i: (0, 0))],
        out_specs=(_row_spec(tr, 1024), _vec_spec(512), _vec_spec(512), pl.BlockSpec((4, 128, 128), lambda i: (0, 0, 0)),
                   pl.BlockSpec((4, 128), lambda i: (0, 0))),
        compiler_params=_cp("arbitrary"))(z, z, dycat2, ln_g, ln_b, w_s, b_st)


def _adamw_math(w, m, v, g):
    c1 = 1.0 / (1.0 - ADAM_B1 ** ADAM_STEP)
    c2 = 1.0 / (1.0 - ADAM_B2 ** ADAM_STEP)
    m2 = ADAM_B1 * m + (1.0 - ADAM_B1) * g
    v2 = ADAM_B2 * v + (1.0 - ADAM_B2) * (g * g)
    return -ADAM_LR * ((m2 * c1) / (jnp.sqrt(v2 * c2) + ADAM_EPS) + ADAM_WD * w), m2, v2


def _adamw_small(name, params, parts):
    n = len(params)

    def body(*refs):
        ins, outs = refs[:4 * n], refs[4 * n:]
        for i in range(n):
            w_ref, m_ref, v_ref, p_ref = ins[4 * i:4 * i + 4]
            g = p_ref[0].astype(F32)
            for k in range(1, N_DEV):
                g = g + p_ref[k].astype(F32)
            delta, m2, v2 = _adamw_math(w_ref[...], m_ref[...], v_ref[...], g)
            outs[4 * i][...] = g
            outs[4 * i + 1][...] = delta
            outs[4 * i + 2][...] = m2
            outs[4 * i + 3][...] = v2

    flat = [a for (w, m, v), p in zip(params, parts) for a in (w, m, v, p)]
    out = pl.pallas_call(
        body, name=name, out_shape=[jax.ShapeDtypeStruct(w.shape, F32) for (w, _, _) in params for _ in range(4)],
        compiler_params=pltpu.CompilerParams(vmem_limit_bytes=_VMEM_LIMIT))(*flat)
    return [out[4 * i:4 * i + 4] for i in range(n)]


ADAMW_BLOCK_BYTES = 36 * 2 ** 20


def _adamw(name, w, m, v, parts):
    L, R, C = w.shape
    P = parts[0].shape[0]
    row_bytes = 2 * C * (7 * 4 + P * parts[0].dtype.itemsize)
    tr = R
    if R * row_bytes > ADAMW_BLOCK_BYTES:
        tr = next(t for t in (1024, 512, 256, 128, 64, 32, 16) if R % t == 0 and t * row_bytes <= ADAMW_BLOCK_BYTES)
    nr = R // tr
    c1 = 1.0 / (1.0 - ADAM_B1 ** ADAM_STEP)
    c2 = 1.0 / (1.0 - ADAM_B2 ** ADAM_STEP)

    def body(w_ref, m_ref, v_ref, *rest):
        p_refs, (g_ref, d_ref, mo_ref, vo_ref) = rest[:L], rest[L:]
        for ll in range(L):
            @pl.when(pl.program_id(0) == ll)
            def _(p_ref=p_refs[ll]):
                g = p_ref[0].astype(F32)
                for k in range(1, P):
                    g = g + p_ref[k].astype(F32)
                m2 = ADAM_B1 * m_ref[...] + (1.0 - ADAM_B1) * g
                v2 = ADAM_B2 * v_ref[...] + (1.0 - ADAM_B2) * (g * g)
                g_ref[...] = g
                mo_ref[...] = m2
                vo_ref[...] = v2
                d_ref[...] = -ADAM_LR * ((m2 * c1) / (jnp.sqrt(v2 * c2) + ADAM_EPS) + ADAM_WD * w_ref[...])

    def part_spec(ll):
        return pl.BlockSpec((P, tr, C), lambda l, i: (0, jnp.where(l == ll, i, jnp.where(l < ll, 0, nr - 1)), 0))

    full = pl.BlockSpec((None, tr, C), lambda l, i: (l, i, 0))
    sds = jax.ShapeDtypeStruct((L, R, C), F32)
    return pl.pallas_call(
        body, name=name, grid=(L, nr), out_shape=(sds, sds, sds, sds),
        in_specs=[full] * 3 + [part_spec(ll) for ll in range(L)],
        out_specs=(full,) * 4, compiler_params=_cp("arbitrary", "arbitrary"))(w, m, v, *parts)


def _rope_tables(positions):
    half = 16
    inv_freq = 10000.0 ** (-jnp.arange(half, dtype=F32) / half)
    ang = positions.astype(F32)[:, None] * inv_freq
    cos, sin = jnp.cos(ang), jnp.sin(ang)
    S = positions.shape[0]
    z16, z32, z64 = jnp.zeros((S, 16), F32), jnp.zeros((S, 32), F32), jnp.zeros((S, 64), F32)
    cosk = jnp.concatenate([z64, cos, cos, z32], axis=1)
    cosq = jnp.concatenate([jnp.ones((S, 64), F32), cos, cos, z32], axis=1)
    sa = jnp.concatenate([z64, -sin, z16, z32], axis=1)
    sb = jnp.concatenate([z64, z16, sin, z32], axis=1)
    return cosq, cosk, sa, sb


def _ffn_fwd(l, x, mod, n2g, get_w_up8, cw24, get_w_down4):
    sh, sc, gate = mod
    h = _rmsmod_fwd(f"ffn{l}_norm", x, n2g, sc, sh, n2g)
    w_up8 = get_w_up8(h)
    u8 = _mm_cols(f"ffn{l}_up", h, w_up8, out_dtype=ACT_DTYPE, tm=2048)
    S, n = u8.shape[1], u8.shape[2]
    u24 = u8.reshape(2, 4, S, n)
    a4, z24 = _ffn_gate_fwd(f"ffn{l}_gate", u24, cw24)
    w_down4 = get_w_down4(a4)
    f, x_new = _mm_rows_resid(f"ffn{l}_down", a4, w_down4, x, gate)
    return x_new, (x, h, u24, a4, f, z24), w_up8, w_down4


def _ffn_bwd(l, dx, df, dgate, saved, mod, n2g, w_up8, cw24, w_down4, me, y_prev, gate_prev):
    sh, sc, gate = mod
    x, h, u24, a4, f, z24 = saved
    da4 = _mm_rows_dx(f"ffn{l}_down_dx", df, w_down4, out_dtype=ACT_DTYPE, tm=2048)
    dw_down4 = _mm_rows_dw(f"ffn{l}_down_dw", a4, df, out_dtype=WIRE_DTYPE, tn=1024)
    sent_down, token = _exchange_start(f"scatter_ffn{l}_down", [dw_down4.reshape(8, 352, dw_down4.shape[2])], True, dgate, me)
    du24, dcw24, dh = _ffn_gate_bwd(f"ffn{l}_act_bwd", u24, z24, cw24, da4, w_up8.reshape((2, 4) + w_up8.shape[1:]), token)
    du8 = du24.reshape((8,) + du24.shape[2:])
    dw_up8t = _mm_cols_dwt(f"ffn{l}_up_dw", h, du8, out_dtype=WIRE_DTYPE, tk=1024)
    sent_up, token = _exchange_start(f"scatter_ffn{l}_up", [dw_up8t], True, dcw24, me)
    dx_new, dn2g, dsc, dsh, dy_prev, dgate_prev = _rmsmod_bwd(f"ffn{l}_norm_bwd", x, n2g, sc, dh, dx, token, y_prev, gate_prev)
    return dx_new, dict(sent_up=sent_up, sent_down=sent_down, cw24=dcw24, n2g=dn2g, mod=(dsh, dsc, dgate)), dy_prev, dgate_prev


def kernel(x, c, positions, ada_w, ada_b, norm1_g, norm2_g, ab_w_in, a_conv_w, b_mix_w, b_scale, ab_w_out, cd_w_in, c_q_norm_g, c_w_uq, c_kv_norm_g, c_w_ukv, d_ln_g, d_ln_b, d_w_s, d_b_s, cd_w_out, ffn_w_up, ffn_conv_w, ffn_w_down, final_norm_g, loss_target, m_ada_w, m_ada_b, m_norm1_g, m_norm2_g, m_ab_w_in, m_a_conv_w, m_b_mix_w, m_b_scale, m_ab_w_out, m_cd_w_in, m_c_q_norm_g, m_c_w_uq, m_c_kv_norm_g, m_c_w_ukv, m_d_ln_g, m_d_ln_b, m_d_w_s, m_d_b_s, m_cd_w_out, m_ffn_w_up, m_ffn_conv_w, m_ffn_w_down, m_final_norm_g, v_ada_w, v_ada_b, v_norm1_g, v_norm2_g, v_ab_w_in, v_a_conv_w, v_b_mix_w, v_b_scale, v_ab_w_out, v_cd_w_in, v_c_q_norm_g, v_c_w_uq, v_c_kv_norm_g, v_c_w_ukv, v_d_ln_g, v_d_ln_b, v_d_w_s, v_d_b_s, v_cd_w_out, v_ffn_w_up, v_ffn_conv_w, v_ffn_w_down, v_final_norm_g):
    S, D = x.shape[1], x.shape[2]
    me = 4 * lax.axis_index("x") + 2 * lax.axis_index("y") + lax.axis_index("c")
    x0, target = x[0], loss_target[0]
    W = _MXU_DTYPE

    small_shapes = [(1024,), (3, 64), (32,), (64,), (64,), (2, 3, 704)]
    (g0,) = _exchange("gather_small", [[_pack([c, a_conv_w, c_q_norm_g, d_ln_g, d_ln_b, ffn_conv_w])]], scatter=False)
    c_all, aconv_s, qg_s, lng_s, lnb_s, fcw_s = _unpack(g0[:, 0], small_shapes, lead=(N_DEV,))
    conv_w = aconv_s.transpose(1, 0, 2).reshape(3, 512)
    qg, ln_g, ln_b = qg_s.reshape(1, 256), lng_s.reshape(1, 512), lnb_s.reshape(1, 512)
    cw24 = [fcw_s[:, l].reshape(2, 4, 3, 704) for l in range(2)]
    c16 = jnp.pad(c_all, ((0, 16 - N_DEV), (0, 0)))

    mod_cols = _ada_fwd(c16, ada_w)
    (g1,) = _exchange("scatter_mod", [[mod_cols[:, :N_DEV].transpose(1, 0, 2)]], scatter=True)
    mod_mine = g1[:, 0]
    mod = mod_mine.transpose(1, 0, 2).reshape(2, 6 * D) + ada_b
    mods = [[mod[l, k * D:(k + 1) * D].reshape(1, D) for k in range(6)] for l in range(2)]

    gw_ab, token = _hier_gather_start("gather_w_ab", [ab_w_in[0].astype(W), ab_w_out[0].astype(W)], mod, me)
    gw_up0, token = _hier_gather_start("gather_w_ffn0_up", [ffn_w_up[0].astype(W)], token, me)
    gw_rest, started = _exchange_start("gather_w_rest", [
        ffn_w_down[0].astype(W), cd_w_in[0].T.astype(W), c_w_uq[0].T.astype(W), c_w_ukv[0].astype(W), cd_w_out[0].astype(W),
        ffn_w_up[1].astype(W), ffn_w_down[1].astype(W)], False, token, me)

    cosq, cosk, sa, sb = _rope_tables(positions[0])
    n1g = [norm1_g[l].reshape(1, D) for l in range(2)]
    n2g = [norm2_g[l].reshape(1, D) for l in range(2)]
    mix_w, scale = b_mix_w[0], b_scale
    kvg = c_kv_norm_g
    w_s, b_st = d_w_s[0], d_b_s[0].T

    sh1, sc1, g1m = mods[0][:3]
    h_ab = _rmsmod_fwd("ab_norm", x0, n1g[0], sc1, sh1, started)
    w_abin8, w_about = _hier_gather_wait("wait_w_ab", _hier_gather_forward("forward_w_ab", gw_ab, h_ab), h_ab)
    w_about2 = w_about.reshape(2, 512, D)
    z8 = _mm_cols("ab_in", h_ab, w_abin8, out_dtype=ACT_DTYPE, tm=2048)
    ycat_ab = _ab_mix_fwd(z8, conv_w, mix_w, scale)
    y_ab, x1 = _mm_rows_resid("ab_out", ycat_ab, w_about2, x0, g1m, tm=1024)
    w_up8, w_down4 = [None, None], [None, None]
    gw_up0 = _hier_gather_forward("forward_w_ffn0_up", gw_up0, x1)
    x2, ffn0_saved, w_up8[0], w_down4[0] = _ffn_fwd(
        0, x1, mods[0][3:], n2g[0], lambda after: _hier_gather_wait("wait_w_ffn0_up", gw_up0, after)[0], cw24[0],
        lambda after: _exchange_wait("wait_w_ffn0_down", gw_rest, after, [0])[0].reshape(4, 704, D))

    w_cdin, w_uq, w_ukv, w_cdout = _exchange_wait("wait_w_cd", gw_rest, x2, [1, 2, 3, 4])
    w_cdout2 = w_cdout.reshape(2, 512, D)
    w_cd_t = w_cdin.reshape(1440, D)
    zr = lambda n: jnp.zeros((n, D), W)
    w_cd_pad = jnp.concatenate([w_cd_t[:384], zr(64), w_cd_t[384:416], zr(32), w_cd_t[416:]], axis=0)
    w_uq_pad = jnp.pad(w_uq, ((0, 0), (0, 32), (0, 0))).reshape(1024, 256)
    w_ukv_h = w_ukv.transpose(1, 0, 2)
    w_k_pad = jnp.pad(w_ukv_h[:, :, :64], ((0, 0), (0, 0), (0, 64))).reshape(128, 1024)
    w_kv_pad = jnp.concatenate([w_k_pad, w_ukv_h[:, :, 64:].reshape(128, 512)], axis=1)

    sh1, sc1, g1c = mods[1][:3]
    h_cd = _rmsmod_fwd("cd_norm", x2, n1g[1], sc1, sh1, n1g[1])
    z_cd = _mm_nt("cd_in", h_cd, w_cd_pad, tm=1024, tn=1536)
    qn, kvn, q_r, k_r, v_r = _qkv_rope_fwd(z_cd, qg, kvg, w_uq_pad, w_kv_pad, cosq, cosk, sa, sb)
    o, lse = _attn_fwd(q_r, k_r, v_r)
    ycat_cd = _sgu_fwd(z_cd, o, ln_g, ln_b, w_s, b_st)
    y_cd, x3 = _mm_rows_resid("cd_out", ycat_cd, w_cdout2, x2, g1c, tm=1024)
    x4, ffn1_saved, w_up8[1], w_down4[1] = _ffn_fwd(
        1, x3, mods[1][3:], n2g[1], lambda after: _exchange_wait("wait_w_ffn1_up", gw_rest, after, [5])[0], cw24[1],
        lambda after: _exchange_wait("wait_w_ffn1_down", gw_rest, after, [6])[0].reshape(4, 704, D))

    loss_local, dx4, dfg, df1, dgate1 = _loss_head(x4, final_norm_g.reshape(1, D), target, ffn1_saved[4], mods[1][5])

    dx3, gf1, dy, dg1c = _ffn_bwd(1, dx4, df1, dgate1, ffn1_saved, mods[1][3:], n2g[1], w_up8[1], cw24[1], w_down4[1], me, y_cd, g1c)

    dycat = _mm_rows_dx("cd_out_dx", dy, w_cdout2, tm=2048)
    dw_cdout = _mm_rows_dw("cd_out_dw", ycat_cd, dy, out_dtype=WIRE_DTYPE, tn=1024)
    duv, dln_g, dln_b, dws, dbs = _sgu_bwd(z_cd, dycat, ln_g, ln_b, w_s, b_st)
    dq_r, dk_r, dv_r = _attn_bwd(q_r, k_r, v_r, lse, *_attn_bwd_prep(o, dycat))
    dqraw, dkvall, dz_cd, dqg, dkvg = _qkv_rope_bwd(z_cd, qg, kvg, dq_r, dk_r, dv_r, duv, w_uq_pad, w_kv_pad, cosq, cosk, sa, sb)
    dw_uq_pad = _mm_tn("cd_uq_dw", dqraw, qn, tn=256)
    dw_kv_pad = _mm_tn("cd_ukv_dw", kvn, dkvall, tm=128)
    dh_cd = _mm_nn("cd_in_dx", dz_cd, w_cd_pad, out_dtype=ACT_DTYPE, tm=1024, tn=1024)
    dw_cd_pad = _mm_tn("cd_in_dw", dz_cd, h_cd, tm=768, tn=1024)
    dw_cd8 = jnp.concatenate([dw_cd_pad[:384], dw_cd_pad[448:480], dw_cd_pad[512:]], axis=0).astype(WIRE_DTYPE).reshape(8, 180, D)
    dw_uq8 = dw_uq_pad.reshape(8, 128, 256)[:, :96].astype(WIRE_DTYPE)
    dw_ukv8 = jnp.concatenate([dw_kv_pad[:, :1024].reshape(128, 8, 128)[:, :, :64], dw_kv_pad[:, 1024:].reshape(128, 8, 64)],
                              axis=2).transpose(1, 0, 2).astype(WIRE_DTYPE)
    early_names = ["c_kv_norm_g", "d_w_s", "d_b_s", "final_norm_g", "c_q_norm_g", "d_ln_g", "d_ln_b"]
    early_grads = [dkvg, dws.reshape(512, 128).astype(WIRE_DTYPE), dbs, dfg, dqg.reshape(8, 1, 32), dln_g.reshape(8, 1, 64),
                   dln_b.reshape(8, 1, 64)]
    sent_cd, token = _exchange_start("scatter_cd", [dw_cd8, dw_uq8, dw_ukv8, dw_cdout.reshape(8, 128, D)] + early_grads,
                                     [True] * 4 + [False] * 4 + [True] * 3, dqg, me)
    dx2, dn1g_cd, dsc1_cd, dsh1_cd, df0, dgate0 = _rmsmod_bwd("cd_norm_bwd", x2, n1g[1], sc1, dh_cd, dx3, token,
                                                              ffn0_saved[4], mods[0][5])

    dx1, gf0, dy, dg1m = _ffn_bwd(0, dx2, df0, dgate0, ffn0_saved, mods[0][3:], n2g[0], w_up8[0], cw24[0], w_down4[0], me, y_ab, g1m)

    dw_about = _mm_rows_dw("ab_out_dw", ycat_ab, dy, out_dtype=WIRE_DTYPE, tn=1024)
    sent_about, token = _exchange_start("scatter_ab_out", [dw_about.reshape(8, 128, D)], True, dg1m, me)
    dycat = _mm_rows_dx("ab_out_dx", dy, w_about2, tm=2048)
    dz8, dconv_w, dmix_w, dscale = _ab_mix_bwd(z8, dycat, conv_w, mix_w, scale, token)
    dz8 = dz8.reshape(8, S, 256)
    dw_abin8 = _mm_cols_dw("ab_in_dw", h_ab, dz8, out_dtype=WIRE_DTYPE, tk=1024)
    sent_abin, token = _exchange_start("scatter_ab_in", [dw_abin8], True, dscale, me)
    dh_ab = _mm_cols_dx("ab_in_dx", dz8, w_abin8, out_dtype=ACT_DTYPE, tm=1024)
    dx0, dn1g_ab, dsc1_ab, dsh1_ab = _rmsmod_bwd("ab_norm_bwd", x0, n1g[0], mods[0][1], dh_ab, dx1, token)

    dmod = jnp.stack([jnp.concatenate([dsh1_ab, dsc1_ab, dg1m, *gf0["mod"]], axis=1)[0],
                      jnp.concatenate([dsh1_cd, dsc1_cd, dg1c, *gf1["mod"]], axis=1)[0]])
    late_names = ["ada_b", "norm1_g", "norm2_g", "b_mix_w", "b_scale", "a_conv_w", "ffn_conv_w"]
    late_grads = [dmod, jnp.concatenate([dn1g_ab, dn1g_cd]), jnp.concatenate([gf0["n2g"], gf1["n2g"]]),
                  dmix_w.reshape(512, 128).astype(WIRE_DTYPE), dscale, dconv_w.reshape(3, 8, 64).transpose(1, 0, 2),
                  jnp.stack([gf0["cw24"].reshape(8, 3, 704), gf1["cw24"].reshape(8, 3, 704)], axis=1),
                  jnp.pad(loss_local, ((0, 0), (0, 127)))]
    small_view = dict(ada_b=(2, 6 * D), norm1_g=(2, D), norm2_g=(2, D), b_mix_w=(512, 128), b_scale=(1, 512), c_kv_norm_g=(1, 128),
                      d_w_s=(512, 128), d_b_s=(4, 128), final_norm_g=(1, D),
                      a_conv_w=(3, 64), c_q_norm_g=(1, 32), d_ln_g=(1, 64), d_ln_b=(1, 64), ffn_conv_w=(2, 3, 704))
    late_sent, token = _exchange_start("gather_small_grads_late", late_grads, [False] * 5 + [True] * 2 + [False], dx0, me)

    res = {}

    def update(name, w, m, v, parts, shape3d):
        outs = _adamw("adamw_" + name, w.reshape(shape3d), m.reshape(shape3d), v.reshape(shape3d),
                      [p.reshape((p.shape[0],) + shape3d[1:]) for p in parts])
        res[name] = [o_.reshape(w.shape) for o_ in outs]

    p_cdin, p_uq, p_ukv, p_cdout = _exchange_wait("wait_scatter_cd", sent_cd, token, [0, 1, 2, 3])
    swap = lambda a: jnp.swapaxes(a, 1, 2)
    update("cd_w_in", swap(cd_w_in), swap(m_cd_w_in), swap(v_cd_w_in), [p_cdin], (1, 180, D))
    update("c_w_uq", swap(c_w_uq), swap(m_c_w_uq), swap(v_c_w_uq), [p_uq], (1, 96, 256))
    for name in ("cd_w_in", "c_w_uq"):
        res[name] = [swap(o_) for o_ in res[name]]
    update("c_w_ukv", c_w_ukv, m_c_w_ukv, v_c_w_ukv, [p_ukv], (1, 128, 128))
    update("cd_w_out", cd_w_out, m_cd_w_out, v_cd_w_out, [p_cdout], (1, 128, D))
    (p_dn1,) = _exchange_wait("wait_scatter_ffn1_down", gf1["sent_down"], token)
    (p_dn0,) = _exchange_wait("wait_scatter_ffn0_down", gf0["sent_down"], res["cd_w_out"][0])
    update("ffn_w_down", ffn_w_down, m_ffn_w_down, v_ffn_w_down, [p_dn0, p_dn1], (2, 352, D))
    (p_up1,) = _exchange_wait("wait_scatter_ffn1_up", gf1["sent_up"], token)
    (p_up0,) = _exchange_wait("wait_scatter_ffn0_up", gf0["sent_up"], res["ffn_w_down"][0])
    swap = lambda a: jnp.swapaxes(a, 1, 2)
    update("ffn_w_up", swap(ffn_w_up), swap(m_ffn_w_up), swap(v_ffn_w_up), [p_up0, p_up1], (2, 704, D))
    up_done = res["ffn_w_up"][0]
    res["ffn_w_up"] = [swap(o_) for o_ in res["ffn_w_up"]]
    (p_about,) = _exchange_wait("wait_scatter_ab_out", sent_about, up_done)
    update("ab_w_out", ab_w_out, m_ab_w_out, v_ab_w_out, [p_about], (1, 128, D))
    (p_abin,) = _exchange_wait("wait_scatter_ab_in", sent_abin, res["ab_w_out"][0])
    update("ab_w_in", ab_w_in, m_ab_w_in, v_ab_w_in, [p_abin], (1, D, 256))

    early_parts = _exchange_wait("wait_small_grads_early", sent_cd, res["ab_w_in"][0], list(range(4, 11)))
    late_parts = _exchange_wait("wait_small_grads_late", late_sent, res["ab_w_in"][0])
    small_names = early_names + late_names
    small_parts = list(early_parts) + list(late_parts[:7])
    loss = jnp.sum(late_parts[7][:, 0, 0])
    dmod_all = late_parts[0]
    dmod_cols = lax.dynamic_slice_in_dim(dmod_all, me * 768, 768, axis=2).transpose(1, 0, 2)
    g_ada_w = _ada_bwd(c16, jnp.pad(dmod_cols, ((0, 0), (0, 16 - N_DEV), (0, 0))))
    update("ada_w", ada_w, m_ada_w, v_ada_w, [g_ada_w[None]], (1, 2 * D, 768))

    small_w = dict(ada_b=(ada_b, m_ada_b, v_ada_b), norm1_g=(norm1_g, m_norm1_g, v_norm1_g), norm2_g=(norm2_g, m_norm2_g, v_norm2_g),
                   b_mix_w=(b_mix_w, m_b_mix_w, v_b_mix_w), b_scale=(b_scale, m_b_scale, v_b_scale),
                   c_kv_norm_g=(c_kv_norm_g, m_c_kv_norm_g, v_c_kv_norm_g), d_w_s=(d_w_s, m_d_w_s, v_d_w_s),
                   d_b_s=(d_b_s, m_d_b_s, v_d_b_s), final_norm_g=(final_norm_g, m_final_norm_g, v_final_norm_g),
                   a_conv_w=(a_conv_w, m_a_conv_w, v_a_conv_w), c_q_norm_g=(c_q_norm_g, m_c_q_norm_g, v_c_q_norm_g),
                   d_ln_g=(d_ln_g, m_d_ln_g, v_d_ln_g), d_ln_b=(d_ln_b, m_d_ln_b, v_d_ln_b),
                   ffn_conv_w=(ffn_conv_w, m_ffn_conv_w, v_ffn_conv_w))
    small_out = _adamw_small("adamw_small", [tuple(a.reshape(small_view[n]) for a in small_w[n]) for n in small_names],
                             list(small_parts))
    for n, outs in zip(small_names, small_out):
        res[n] = [o_.reshape(small_w[n][0].shape) for o_ in outs]

    order = ["ada_w", "ada_b", "norm1_g", "norm2_g", "ab_w_in", "a_conv_w", "b_mix_w", "b_scale", "ab_w_out", "cd_w_in", "c_q_norm_g",
             "c_w_uq", "c_kv_norm_g", "c_w_ukv", "d_ln_g", "d_ln_b", "d_w_s", "d_b_s", "cd_w_out", "ffn_w_up", "ffn_conv_w",
             "ffn_w_down", "final_norm_g"]
    return (loss, dx0[None], *[res[n][0] for n in order], *[res[n][1] for n in order], *[res[n][2] for n in order],
            *[res[n][3] for n in order])
```
